```python
import jax, jax.numpy as jnp
from jax import lax
import numpy as np

D_MODEL = 2048
BATCH = 8
SEQ = 2048
DEPTH = 1

CHUNK = 64
D_CONV = D_MODEL // 2
CONV_K = 3
D_POOL = D_MODEL // 2
POOL_WINDOWS = (2, 4, 8, 16)
N_POOL_GROUPS = len(POOL_WINDOWS)
POOL_GROUP_W = D_POOL // N_POOL_GROUPS
POOL_GROUP_OUT = D_MODEL // N_POOL_GROUPS
D_FF = ((8 * D_MODEL // 3 + 255) // 256) * 256
D_IN = 3 * D_CONV + D_POOL + 2 * D_MODEL
EPS = 1e-6

kernel_name = "hybrid_shortconv_multipool_gated_block"


def rmsnorm(x, g):
    xf = x.astype(jnp.float32)
    y = xf * lax.rsqrt(jnp.mean(xf * xf, axis=-1, keepdims=True) + EPS)
    return (y * g.astype(jnp.float32)).astype(x.dtype)


def causal_depthwise_conv(u, w, b):
    S = u.shape[1]
    up = jnp.pad(u, ((0, 0), (CONV_K - 1, 0), (0, 0)))
    out = b
    for k in range(CONV_K):
        out = out + w[k] * up[:, k:k + S]
    return out


def multiscale_pool(v):
    B, S, _ = v.shape
    vg = v.reshape(B, S, N_POOL_GROUPS, POOL_GROUP_W)
    vf = vg.astype(jnp.float32)
    cs = jnp.cumsum(vf, axis=1)
    cs = jnp.concatenate([jnp.zeros_like(cs[:, :1]), cs], axis=1)
    t = jnp.arange(S, dtype=jnp.int32)[:, None]
    win = jnp.asarray(POOL_WINDOWS, dtype=jnp.int32)[None, :]
    lo = jnp.maximum(t + 1 - win, 0)
    cnt = (t + 1 - lo).astype(jnp.float32)
    g_idx = jnp.arange(N_POOL_GROUPS, dtype=jnp.int32)[None, :]
    window_sum = cs[:, 1:] - cs[:, lo, g_idx, :]
    mean = window_sum / cnt[None, :, :, None]
    return (mean - vf).astype(v.dtype)


def swiglu(h, w_gate, w_up, w_down):
    return (jax.nn.silu(h @ w_gate) * (h @ w_up)) @ w_down


def _fwd_setup_inputs(seed: int = 0) -> dict:
    key = jax.random.key(seed)
    ks = jax.random.split(key, 16)
    f32 = jnp.float32
    L = DEPTH
    nrm = lambda k, shape, fan_in: jax.random.normal(k, shape, f32) * (fan_in ** -0.5)
    return {
        "x": jax.random.normal(ks[0], (BATCH, SEQ, D_MODEL), f32),
        "norm1_g": 1.0 + 0.02 * jax.random.normal(ks[1], (L, D_MODEL), f32),
        "w_in": nrm(ks[2], (L, D_MODEL, D_IN), D_MODEL),
        "b_gate": 0.01 * jax.random.normal(ks[3], (L, 2 * D_MODEL), f32),
        "conv_w": nrm(ks[4], (L, CONV_K, D_CONV), CONV_K),
        "conv_b": 0.01 * jax.random.normal(ks[5], (L, D_CONV), f32),
        "w_a_out": nrm(ks[6], (L, D_CONV, D_MODEL), D_CONV),
        "w_pool": nrm(ks[7], (L, N_POOL_GROUPS, POOL_GROUP_W, POOL_GROUP_OUT), POOL_GROUP_W),
        "pool_scale": 1.0 + 0.02 * jax.random.normal(ks[8], (L, D_MODEL), f32),
        "w_o": nrm(ks[9], (L, D_MODEL, D_MODEL), D_MODEL),
        "norm2_g": 1.0 + 0.02 * jax.random.normal(ks[10], (L, D_MODEL), f32),
        "w_ffn_gate": nrm(ks[11], (L, D_MODEL, D_FF), D_MODEL),
        "w_ffn_up": nrm(ks[12], (L, D_MODEL, D_FF), D_MODEL),
        "w_ffn_down": nrm(ks[13], (L, D_FF, D_MODEL), D_FF),
        "final_g": 1.0 + 0.02 * jax.random.normal(ks[14], (D_MODEL,), f32),
    }


def _fwd_reference(x, norm1_g, w_in, b_gate, conv_w, conv_b, w_a_out, w_pool, pool_scale,
              w_o, norm2_g, w_ffn_gate, w_ffn_up, w_ffn_down, final_g):
    B, S, _ = x.shape
    splits = np.cumsum([D_CONV, D_CONV, D_CONV, D_POOL, D_MODEL]).tolist()
    for l in range(DEPTH):
        h = rmsnorm(x, norm1_g[l])
        proj = h @ w_in[l]
        b_a, c_a, v_a, v_b, g_a, g_b = jnp.split(proj, splits, axis=-1)

        u = causal_depthwise_conv(c_a * v_a, conv_w[l], conv_b[l])
        y_a = (b_a * u) @ w_a_out[l]

        p = multiscale_pool(v_b)
        y_b = jnp.einsum("bsgc,gcd->bsgd", p, w_pool[l]).reshape(B, S, D_MODEL)
        y_b = y_b * pool_scale[l]

        gb = b_gate[l]
        merged = jax.nn.sigmoid(g_a + gb[:D_MODEL]) * y_a + jax.nn.sigmoid(g_b + gb[D_MODEL:]) * y_b
        x = x + merged @ w_o[l]

        h2 = rmsnorm(x, norm2_g[l])
        x = x + swiglu(h2, w_ffn_gate[l], w_ffn_up[l], w_ffn_down[l])
    return rmsnorm(x, final_g)


import jax as _jax
import jax.numpy as _jnp

TWIN_FORMAT = 'train_step'
FWD_PARAMS = ['x', 'norm1_g', 'w_in', 'b_gate', 'conv_w', 'conv_b', 'w_a_out', 'w_pool', 'pool_scale', 'w_o', 'norm2_g', 'w_ffn_gate', 'w_ffn_up', 'w_ffn_down', 'final_g']
TWIN_WEIGHTS = ['norm1_g', 'w_in', 'b_gate', 'conv_w', 'conv_b', 'w_a_out', 'w_pool', 'pool_scale', 'w_o', 'norm2_g', 'w_ffn_gate', 'w_ffn_up', 'w_ffn_down', 'final_g']
TWIN_DIFF_INPUT = 'x'
TWIN_INPUTS = ['x', 'norm1_g', 'w_in', 'b_gate', 'conv_w', 'conv_b', 'w_a_out', 'w_pool', 'pool_scale', 'w_o', 'norm2_g', 'w_ffn_gate', 'w_ffn_up', 'w_ffn_down', 'final_g', 'loss_target', 'm_norm1_g', 'm_w_in', 'm_b_gate', 'm_conv_w', 'm_conv_b', 'm_w_a_out', 'm_w_pool', 'm_pool_scale', 'm_w_o', 'm_norm2_g', 'm_w_ffn_gate', 'm_w_ffn_up', 'm_w_ffn_down', 'm_final_g', 'v_norm1_g', 'v_w_in', 'v_b_gate', 'v_conv_w', 'v_conv_b', 'v_w_a_out', 'v_w_pool', 'v_pool_scale', 'v_w_o', 'v_norm2_g', 'v_w_ffn_gate', 'v_w_ffn_up', 'v_w_ffn_down', 'v_final_g']
TWIN_OUTPUTS = ['loss', 'grad_x', 'grad_norm1_g', 'grad_w_in', 'grad_b_gate', 'grad_conv_w', 'grad_conv_b', 'grad_w_a_out', 'grad_w_pool', 'grad_pool_scale', 'grad_w_o', 'grad_norm2_g', 'grad_w_ffn_gate', 'grad_w_ffn_up', 'grad_w_ffn_down', 'grad_final_g', 'delta_norm1_g', 'delta_w_in', 'delta_b_gate', 'delta_conv_w', 'delta_conv_b', 'delta_w_a_out', 'delta_w_pool', 'delta_pool_scale', 'delta_w_o', 'delta_norm2_g', 'delta_w_ffn_gate', 'delta_w_ffn_up', 'delta_w_ffn_down', 'delta_final_g', 'new_m_norm1_g', 'new_m_w_in', 'new_m_b_gate', 'new_m_conv_w', 'new_m_conv_b', 'new_m_w_a_out', 'new_m_w_pool', 'new_m_pool_scale', 'new_m_w_o', 'new_m_norm2_g', 'new_m_w_ffn_gate', 'new_m_w_ffn_up', 'new_m_w_ffn_down', 'new_m_final_g', 'new_v_norm1_g', 'new_v_w_in', 'new_v_b_gate', 'new_v_conv_w', 'new_v_conv_b', 'new_v_w_a_out', 'new_v_w_pool', 'new_v_pool_scale', 'new_v_w_o', 'new_v_norm2_g', 'new_v_w_ffn_gate', 'new_v_w_ffn_up', 'new_v_w_ffn_down', 'new_v_final_g']
TWIN_LEAF_KINDS = {'loss': 'loss', 'grad_x': 'grad_x', 'grad_norm1_g': 'grad_w', 'grad_w_in': 'grad_w', 'grad_b_gate': 'grad_w', 'grad_conv_w': 'grad_w', 'grad_conv_b': 'grad_w', 'grad_w_a_out': 'grad_w', 'grad_w_pool': 'grad_w', 'grad_pool_scale': 'grad_w', 'grad_w_o': 'grad_w', 'grad_norm2_g': 'grad_w', 'grad_w_ffn_gate': 'grad_w', 'grad_w_ffn_up': 'grad_w', 'grad_w_ffn_down': 'grad_w', 'grad_final_g': 'grad_w', 'delta_norm1_g': 'delta_w', 'delta_w_in': 'delta_w', 'delta_b_gate': 'delta_w', 'delta_conv_w': 'delta_w', 'delta_conv_b': 'delta_w', 'delta_w_a_out': 'delta_w', 'delta_w_pool': 'delta_w', 'delta_pool_scale': 'delta_w', 'delta_w_o': 'delta_w', 'delta_norm2_g': 'delta_w', 'delta_w_ffn_gate': 'delta_w', 'delta_w_ffn_up': 'delta_w', 'delta_w_ffn_down': 'delta_w', 'delta_final_g': 'delta_w', 'new_m_norm1_g': 'new_m', 'new_m_w_in': 'new_m', 'new_m_b_gate': 'new_m', 'new_m_conv_w': 'new_m', 'new_m_conv_b': 'new_m', 'new_m_w_a_out': 'new_m', 'new_m_w_pool': 'new_m', 'new_m_pool_scale': 'new_m', 'new_m_w_o': 'new_m', 'new_m_norm2_g': 'new_m', 'new_m_w_ffn_gate': 'new_m', 'new_m_w_ffn_up': 'new_m', 'new_m_w_ffn_down': 'new_m', 'new_m_final_g': 'new_m', 'new_v_norm1_g': 'new_v', 'new_v_w_in': 'new_v', 'new_v_b_gate': 'new_v', 'new_v_conv_w': 'new_v', 'new_v_conv_b': 'new_v', 'new_v_w_a_out': 'new_v', 'new_v_w_pool': 'new_v', 'new_v_pool_scale': 'new_v', 'new_v_w_o': 'new_v', 'new_v_norm2_g': 'new_v', 'new_v_w_ffn_gate': 'new_v', 'new_v_w_ffn_up': 'new_v', 'new_v_w_ffn_down': 'new_v', 'new_v_final_g': 'new_v'}


def _forward(args):
    return _fwd_reference(*[args[k] for k in FWD_PARAMS])


def _output_shape():
    out = _jax.eval_shape(lambda: _forward(_fwd_setup_inputs(0)))
    return out.shape, out.dtype

N_MICROBATCH = 1
ADAM_LR = 0.001
ADAM_B1 = 0.9
ADAM_B2 = 0.999
ADAM_EPS = 1e-08
ADAM_WD = 0.01
ADAM_STEP = 10
PER_EXAMPLE_BATCH_AXIS = {'x': 0, 'loss_target': 0}
SHARED_INPUTS = []
_WEIGHT_DTYPES = {'norm1_g': _jnp.float32, 'w_in': _jnp.float32, 'b_gate': _jnp.float32, 'conv_w': _jnp.float32, 'conv_b': _jnp.float32, 'w_a_out': _jnp.float32, 'w_pool': _jnp.float32, 'pool_scale': _jnp.float32, 'w_o': _jnp.float32, 'norm2_g': _jnp.float32, 'w_ffn_gate': _jnp.float32, 'w_ffn_up': _jnp.float32, 'w_ffn_down': _jnp.float32, 'final_g': _jnp.float32}
MOMENT_SCALE = {'norm1_g': 6.122432e-02, 'w_in': 3.032267e-02, 'b_gate': 1.119372e-02, 'conv_w': 4.354813e-02, 'conv_b': 4.293404e-02, 'w_a_out': 3.036398e-02, 'w_pool': 2.642535e-02, 'pool_scale': 2.657408e-02, 'w_o': 4.032277e-02, 'norm2_g': 3.971730e-02, 'w_ffn_gate': 1.699461e-02, 'w_ffn_up': 1.644981e-02, 'w_ffn_down': 2.724476e-02, 'final_g': 7.991512e+00}


def _to_microbatches(a, axis):
    t = _jnp.moveaxis(a, axis, 0)
    t = t.reshape((N_MICROBATCH, t.shape[0] // N_MICROBATCH) + t.shape[1:])
    return _jnp.moveaxis(t, 1, axis + 1)


def setup_inputs(seed: int = 0) -> dict:
    inp = _fwd_setup_inputs(seed)
    key = _jax.random.fold_in(_jax.random.key(seed), 7919)
    shape, _ = _output_shape()
    out = dict(inp)
    out["loss_target"] = _jax.random.normal(_jax.random.fold_in(key, 0), shape, _jnp.float32)
    for i, name in enumerate(TWIN_WEIGHTS):
        w = inp[name].astype(_jnp.float32)
        if MOMENT_SCALE is None:
            s = _jnp.sqrt(_jnp.mean(_jnp.square(w)) + 1e-30)
        else:
            s = MOMENT_SCALE[name]
        km, kv = _jax.random.split(_jax.random.fold_in(key, i + 1))
        out[name] = w
        out["m_" + name] = s * _jax.random.normal(km, w.shape, _jnp.float32)
        out["v_" + name] = (s * s) * _jax.random.uniform(kv, w.shape, _jnp.float32, 0.5, 1.5)
    if N_MICROBATCH > 1:
        for name, axis in PER_EXAMPLE_BATCH_AXIS.items():
            out[name] = _to_microbatches(out[name], axis)
    return {'x': out['x'], 'norm1_g': out['norm1_g'], 'w_in': out['w_in'], 'b_gate': out['b_gate'], 'conv_w': out['conv_w'], 'conv_b': out['conv_b'], 'w_a_out': out['w_a_out'], 'w_pool': out['w_pool'], 'pool_scale': out['pool_scale'], 'w_o': out['w_o'], 'norm2_g': out['norm2_g'], 'w_ffn_gate': out['w_ffn_gate'], 'w_ffn_up': out['w_ffn_up'], 'w_ffn_down': out['w_ffn_down'], 'final_g': out['final_g'], 'loss_target': out['loss_target'], 'm_norm1_g': out['m_norm1_g'], 'm_w_in': out['m_w_in'], 'm_b_gate': out['m_b_gate'], 'm_conv_w': out['m_conv_w'], 'm_conv_b': out['m_conv_b'], 'm_w_a_out': out['m_w_a_out'], 'm_w_pool': out['m_w_pool'], 'm_pool_scale': out['m_pool_scale'], 'm_w_o': out['m_w_o'], 'm_norm2_g': out['m_norm2_g'], 'm_w_ffn_gate': out['m_w_ffn_gate'], 'm_w_ffn_up': out['m_w_ffn_up'], 'm_w_ffn_down': out['m_w_ffn_down'], 'm_final_g': out['m_final_g'], 'v_norm1_g': out['v_norm1_g'], 'v_w_in': out['v_w_in'], 'v_b_gate': out['v_b_gate'], 'v_conv_w': out['v_conv_w'], 'v_conv_b': out['v_conv_b'], 'v_w_a_out': out['v_w_a_out'], 'v_w_pool': out['v_w_pool'], 'v_pool_scale': out['v_pool_scale'], 'v_w_o': out['v_w_o'], 'v_norm2_g': out['v_norm2_g'], 'v_w_ffn_gate': out['v_w_ffn_gate'], 'v_w_ffn_up': out['v_w_ffn_up'], 'v_w_ffn_down': out['v_w_ffn_down'], 'v_final_g': out['v_final_g']}


def _loss(weights, diff, rest, loss_target):
    with _jax.named_scope("forward"):
        args = {**rest, TWIN_DIFF_INPUT: diff, **{k: w.astype(_WEIGHT_DTYPES[k]) for k, w in weights.items()}}
        y = _forward(args)
    with _jax.named_scope("loss_head"):
        err = _jnp.square(y.astype(_jnp.float32) - loss_target)
        return 0.5 * _jnp.sum(_jnp.mean(err, axis=-1)) if err.ndim else 0.5 * err


def _adamw(w, g, m, v):
    m = ADAM_B1 * m + (1.0 - ADAM_B1) * g
    v = ADAM_B2 * v + (1.0 - ADAM_B2) * _jnp.square(g)
    m_hat = m / (1.0 - ADAM_B1 ** ADAM_STEP)
    v_hat = v / (1.0 - ADAM_B2 ** ADAM_STEP)
    delta = -ADAM_LR * (m_hat / (_jnp.sqrt(v_hat) + ADAM_EPS) + ADAM_WD * w)
    return delta, m, v


def reference(x, norm1_g, w_in, b_gate, conv_w, conv_b, w_a_out, w_pool, pool_scale, w_o, norm2_g, w_ffn_gate, w_ffn_up, w_ffn_down, final_g, loss_target, m_norm1_g, m_w_in, m_b_gate, m_conv_w, m_conv_b, m_w_a_out, m_w_pool, m_pool_scale, m_w_o, m_norm2_g, m_w_ffn_gate, m_w_ffn_up, m_w_ffn_down, m_final_g, v_norm1_g, v_w_in, v_b_gate, v_conv_w, v_conv_b, v_w_a_out, v_w_pool, v_pool_scale, v_w_o, v_norm2_g, v_w_ffn_gate, v_w_ffn_up, v_w_ffn_down, v_final_g):
    given = dict(x=x, norm1_g=norm1_g, w_in=w_in, b_gate=b_gate, conv_w=conv_w, conv_b=conv_b, w_a_out=w_a_out, w_pool=w_pool, pool_scale=pool_scale, w_o=w_o, norm2_g=norm2_g, w_ffn_gate=w_ffn_gate, w_ffn_up=w_ffn_up, w_ffn_down=w_ffn_down, final_g=final_g, loss_target=loss_target, m_norm1_g=m_norm1_g, m_w_in=m_w_in, m_b_gate=m_b_gate, m_conv_w=m_conv_w, m_conv_b=m_conv_b, m_w_a_out=m_w_a_out, m_w_pool=m_w_pool, m_pool_scale=m_pool_scale, m_w_o=m_w_o, m_norm2_g=m_norm2_g, m_w_ffn_gate=m_w_ffn_gate, m_w_ffn_up=m_w_ffn_up, m_w_ffn_down=m_w_ffn_down, m_final_g=m_final_g, v_norm1_g=v_norm1_g, v_w_in=v_w_in, v_b_gate=v_b_gate, v_conv_w=v_conv_w, v_conv_b=v_conv_b, v_w_a_out=v_w_a_out, v_w_pool=v_w_pool, v_pool_scale=v_pool_scale, v_w_o=v_w_o, v_norm2_g=v_norm2_g, v_w_ffn_gate=v_w_ffn_gate, v_w_ffn_up=v_w_ffn_up, v_w_ffn_down=v_w_ffn_down, v_final_g=v_final_g)
    weights = {n: given[n] for n in TWIN_WEIGHTS}
    shared = {n: given[n] for n in SHARED_INPUTS}
    per_example = {n: given[n] for n in ['x']}
    grad_fn = _jax.value_and_grad(_loss, argnums=(0, 1))

    def one_microbatch(ex, loss_target):
        ex = dict(ex)
        diff = ex.pop(TWIN_DIFF_INPUT)
        return grad_fn(weights, diff, {**shared, **ex}, loss_target)

    if N_MICROBATCH == 1:
        loss, (grad_w, grad_x) = one_microbatch(per_example, given["loss_target"])
    else:
        def body(carry, xs):
            loss_sum, grad_sum = carry
            l_k, (gw_k, gx_k) = one_microbatch(xs[0], xs[1])
            with _jax.named_scope("update"):
                return (loss_sum + l_k, _jax.tree.map(_jnp.add, grad_sum, gw_k)), gx_k

        init = (_jnp.zeros((), _jnp.float32), _jax.tree.map(_jnp.zeros_like, weights))
        (loss, grad_w), grad_x = _jax.lax.scan(body, init, (per_example, given["loss_target"]))
    with _jax.named_scope("update"):
        delta_w, new_m, new_v = {}, {}, {}
        for n in TWIN_WEIGHTS:
            delta_w[n], new_m[n], new_v[n] = _adamw(weights[n], grad_w[n], given["m_" + n], given["v_" + n])
    return (loss, grad_x, *[grad_w[n] for n in TWIN_WEIGHTS], *[delta_w[n] for n in TWIN_WEIGHTS],
            *[new_m[n] for n in TWIN_WEIGHTS], *[new_v[n] for n in TWIN_WEIGHTS])
```

```python
import functools

import jax
import jax.numpy as jnp
from jax import lax
from jax.experimental import pallas as pl
from jax.experimental.pallas import tpu as pltpu

F32 = jnp.float32
BF16 = jnp.bfloat16
MESH = pl.DeviceIdType.MESH

N_DEV = 8
EPS = 1e-6
CONV_K = 3
POOL_WINDOWS = (2, 4, 8, 16)
ADAM_LR = 0.001
ADAM_B1 = 0.9
ADAM_B2 = 0.999
ADAM_EPS = 1e-08
ADAM_WD = 0.01
ADAM_STEP = 10

V7X_VMEM_LIMIT_BYTES = 56 * 1024 * 1024
LANES = 128

NN = ((1,), (0,))
NT = ((1,), (1,))
TN = ((0,), (0,))


def _dot(a, b, dims):
    return lax.dot_general(a, b, (dims, ((), ())), preferred_element_type=F32)


def _cp(n_axes):
    return pltpu.CompilerParams(dimension_semantics=("arbitrary",) * n_axes,
                                vmem_limit_bytes=V7X_VMEM_LIMIT_BYTES)


def _row_tile(rows, bytes_per_row, cap_bytes):
    best = None
    for t in range(16, rows + 1, 16):
        if rows % t == 0 and t * bytes_per_row <= cap_bytes:
            best = t
    return best if best is not None else rows


def _chunks(total, size):
    size = min(size, total)
    assert total % size == 0
    return [slice(r, r + size) for r in range(0, total, size)]


def _shift_down(v, k):
    row = lax.broadcasted_iota(jnp.int32, v.shape, 0)
    return jnp.where(row >= k, pltpu.roll(v, k, 0), 0.0)


def _shift_up(v, k):
    n = v.shape[0]
    row = lax.broadcasted_iota(jnp.int32, v.shape, 0)
    return jnp.where(row < n - k, pltpu.roll(v, n - k, 0), 0.0)


def _sigmoid(v):
    return jax.nn.sigmoid(v)


def _cast_bf16(w2d, name):
    rows, cols = w2d.shape
    tr = _row_tile(rows, cols * 4, 2 << 20)

    def body(i_ref, o_ref):
        o_ref[...] = i_ref[...].astype(BF16)

    return pl.pallas_call(
        body, name=name, grid=(rows // tr,),
        in_specs=[pl.BlockSpec((tr, cols), lambda i: (i, 0))],
        out_specs=pl.BlockSpec((tr, cols), lambda i: (i, 0)),
        out_shape=jax.ShapeDtypeStruct((rows, cols), BF16),
        compiler_params=_cp(1),
    )(w2d)


def _rms_fwd(x2d, g):
    s, d = x2d.shape
    tm = min(256, s)

    def body(x_ref, g_ref, h_ref):
        xv = x_ref[...]
        r = lax.rsqrt(jnp.mean(xv * xv, axis=-1, keepdims=True) + EPS)
        h_ref[...] = (xv * r * g_ref[...]).astype(BF16)

    return pl.pallas_call(
        body, name="rms1_fwd", grid=(s // tm,),
        in_specs=[pl.BlockSpec((tm, d), lambda i: (i, 0)), pl.BlockSpec((1, d), lambda i: (0, 0))],
        out_specs=pl.BlockSpec((tm, d), lambda i: (i, 0)),
        out_shape=jax.ShapeDtypeStruct((s, d), BF16),
        compiler_params=_cp(1),
    )(x2d, g)


def _coords():
    return lax.axis_index("x"), lax.axis_index("y"), lax.axis_index("c")


def _slot(p):
    return 4 * p[0] + 2 * p[1] + p[2]


def _allgather_big(shards):
    n = len(shards)

    def body(*refs):
        ins, outs = refs[:n], refs[n:2 * n]
        send_sems, recv_sems, local_sems = refs[2 * n:]
        x, y, c = _coords()
        me, sibling = (x, y, c), (x, y, 1 - c)
        chips = [(1 - x, y), (x, 1 - y), (1 - x, 1 - y)]

        def copy(a, k, block, to, src=None):
            dst = outs[a].at[_slot(block)]
            return pltpu.make_async_remote_copy(
                src_ref=dst if src is None else src, dst_ref=dst,
                send_sem=send_sems.at[a, k], recv_sem=recv_sems.at[a, k],
                device_id=to, device_id_type=MESH)

        mine, first, passed = [], [], []
        for a in range(n):
            cp = pltpu.make_async_copy(ins[a], outs[a].at[_slot(me)], local_sems.at[a])
            cp.start()
            mine.append(cp)
            f = [copy(a, 0, me, sibling, src=ins[a])]
            f += [copy(a, 1 + j, me, (*chip, c), src=ins[a]) for j, chip in enumerate(chips)]
            for cp in f:
                cp.start()
            first += f
        for a in range(n):
            for j, chip in enumerate(chips):
                copy(a, 1 + j, (*chip, c), me).wait_recv()
                cp = copy(a, 4 + j, (*chip, c), sibling)
                cp.start()
                passed.append(cp)
        for a in range(n):
            copy(a, 0, sibling, me).wait_recv()
            for j, chip in enumerate(chips):
                copy(a, 4 + j, (*chip, 1 - c), me).wait_recv()
        for cp in first + passed:
            cp.wait_send()
        for cp in mine:
            cp.wait()

    any_spec = pl.BlockSpec(memory_space=pl.ANY)
    return pl.pallas_call(
        body, name="allgather_weights",
        in_specs=[any_spec] * n, out_specs=[any_spec] * n,
        out_shape=[jax.ShapeDtypeStruct((N_DEV,) + s.shape, s.dtype) for s in shards],
        scratch_shapes=[pltpu.SemaphoreType.DMA((n, 7)), pltpu.SemaphoreType.DMA((n, 7)),
                        pltpu.SemaphoreType.DMA((n,))],
    )(*shards)


def _exchange_sibling(grads):
    n = len(grads)

    def body(*refs):
        ins, outs = refs[:n], refs[n:2 * n]
        send_sems, recv_sems = refs[2 * n:]
        x, y, c = _coords()
        sibling = (x, y, 1 - c)
        copies = []
        for a in range(n):
            for q in range(4):
                cp = pltpu.make_async_remote_copy(
                    src_ref=ins[a].at[2 * q + (1 - c)], dst_ref=outs[a].at[q],
                    send_sem=send_sems.at[a, q], recv_sem=recv_sems.at[a, q],
                    device_id=sibling, device_id_type=MESH)
                cp.start()
                copies.append(cp)
        for cp in copies:
            cp.wait_recv()
        for cp in copies:
            cp.wait_send()

    any_spec = pl.BlockSpec(memory_space=pl.ANY)
    return pl.pallas_call(
        body, name="grad_exchange_sibling",
        in_specs=[any_spec] * n, out_specs=[any_spec] * n,
        out_shape=[jax.ShapeDtypeStruct((4,) + g.shape[1:], g.dtype) for g in grads],
        scratch_shapes=[pltpu.SemaphoreType.DMA((n, 4)), pltpu.SemaphoreType.DMA((n, 4))],
    )(*grads)


def _exchange_chips(psums):
    n = len(psums)

    def body(*refs):
        ins, outs = refs[:n], refs[n:2 * n]
        send_sems, recv_sems = refs[2 * n:]
        x, y, c = _coords()
        chips = [(1 - x, y), (x, 1 - y), (1 - x, 1 - y)]
        copies = []
        for a in range(n):
            for j, chip in enumerate(chips):
                cp = pltpu.make_async_remote_copy(
                    src_ref=ins[a].at[2 * chip[0] + chip[1]], dst_ref=outs[a].at[j],
                    send_sem=send_sems.at[a, j], recv_sem=recv_sems.at[a, j],
                    device_id=(*chip, c), device_id_type=MESH)
                cp.start()
                copies.append(cp)
        for cp in copies:
            cp.wait_recv()
        for cp in copies:
            cp.wait_send()

    any_spec = pl.BlockSpec(memory_space=pl.ANY)
    return pl.pallas_call(
        body, name="grad_exchange_chips",
        in_specs=[any_spec] * n, out_specs=[any_spec] * n,
        out_shape=[jax.ShapeDtypeStruct((3,) + p.shape[1:], p.dtype) for p in psums],
        scratch_shapes=[pltpu.SemaphoreType.DMA((n, 3)), pltpu.SemaphoreType.DMA((n, 3))],
    )(*psums)


def _allgather_small(v2d, name):
    rows, cols = v2d.shape

    def body(v_ref, out_ref, send_sems, recv_sems):
        x, y, c = _coords()
        me = (x, y, c)
        out_ref[_slot(me)] = v_ref[...]
        peers = []
        for k in range(1, N_DEV):
            fx, fy, fc = (k >> 2) & 1, (k >> 1) & 1, k & 1
            peers.append(((1 - x) if fx else x, (1 - y) if fy else y, (1 - c) if fc else c))
        sends = []
        for k, peer in enumerate(peers):
            cp = pltpu.make_async_remote_copy(
                src_ref=v_ref, dst_ref=out_ref.at[_slot(me)],
                send_sem=send_sems.at[k], recv_sem=recv_sems.at[k],
                device_id=peer, device_id_type=MESH)
            cp.start()
            sends.append(cp)
        for k, peer in enumerate(peers):
            pltpu.make_async_remote_copy(
                src_ref=v_ref, dst_ref=out_ref.at[_slot(peer)],
                send_sem=send_sems.at[k], recv_sem=recv_sems.at[k],
                device_id=peer, device_id_type=MESH).wait_recv()
        for cp in sends:
            cp.wait_send()

    vmem = pl.BlockSpec(memory_space=pltpu.VMEM)
    return pl.pallas_call(
        body, name=name, in_specs=[vmem], out_specs=vmem,
        out_shape=jax.ShapeDtypeStruct((N_DEV, rows, cols), v2d.dtype),
        scratch_shapes=[pltpu.SemaphoreType.DMA((N_DEV - 1,)), pltpu.SemaphoreType.DMA((N_DEV - 1,))],
    )(v2d)


def _chip_partial(slots, g3, recv, name):
    _, rows, cols = g3.shape
    tr = _row_tile(rows, cols * 4, 1 << 20)

    def body(slots_ref, g_ref, r_ref, o_ref):
        o_ref[...] = (g_ref[...].astype(F32) + r_ref[...].astype(F32)).astype(BF16)

    return pl.pallas_call(
        body, name=name,
        grid_spec=pltpu.PrefetchScalarGridSpec(
            num_scalar_prefetch=1, grid=(4, rows // tr),
            in_specs=[pl.BlockSpec((None, tr, cols), lambda q, i, sl: (sl[q], i, 0)),
                      pl.BlockSpec((None, tr, cols), lambda q, i, sl: (q, i, 0))],
            out_specs=pl.BlockSpec((None, tr, cols), lambda q, i, sl: (q, i, 0))),
        out_shape=jax.ShapeDtypeStruct((4, rows, cols), BF16),
        compiler_params=_cp(2),
    )(slots, g3, recv)


def _adam_math(w, g, m, v):
    m = ADAM_B1 * m + (1.0 - ADAM_B1) * g
    v = ADAM_B2 * v + (1.0 - ADAM_B2) * (g * g)
    m_hat = m / (1.0 - ADAM_B1 ** ADAM_STEP)
    v_hat = v / (1.0 - ADAM_B2 ** ADAM_STEP)
    delta = -ADAM_LR * (m_hat / (jnp.sqrt(v_hat) + ADAM_EPS) + ADAM_WD * w)
    return delta, m, v


def _adam_big(own, w, m, v, g3, recv_sib, recv_chips, name):
    rows, cols = w.shape
    tr = _row_tile(rows, cols * 4, 1 << 20)

    def body(own_ref, w_ref, m_ref, v_ref, g_ref, rs_ref, rc_ref, go_ref, do_ref, mo_ref, vo_ref):
        g = g_ref[...].astype(F32) + rs_ref[...].astype(F32)
        g = g + rc_ref[0].astype(F32)
        g = g + rc_ref[1].astype(F32)
        g = g + rc_ref[2].astype(F32)
        delta, m_new, v_new = _adam_math(w_ref[...], g, m_ref[...], v_ref[...])
        go_ref[...] = g
        do_ref[...] = delta
        mo_ref[...] = m_new
        vo_ref[...] = v_new

    blk = pl.BlockSpec((tr, cols), lambda i, o: (i, 0))
    out = jax.ShapeDtypeStruct((rows, cols), F32)
    return pl.pallas_call(
        body, name=name,
        grid_spec=pltpu.PrefetchScalarGridSpec(
            num_scalar_prefetch=1, grid=(rows // tr,),
            in_specs=[blk, blk, blk,
                      pl.BlockSpec((None, tr, cols), lambda i, o: (o[0], i, 0)),
                      pl.BlockSpec((None, tr, cols), lambda i, o: (o[1], i, 0)),
                      pl.BlockSpec((3, tr, cols), lambda i, o: (0, i, 0))],
            out_specs=[blk, blk, blk, blk]),
        out_shape=[out, out, out, out],
        compiler_params=_cp(1),
    )(own, w, m, v, g3, recv_sib, recv_chips)


def _sum_small(gathered):
    _, rows, cols = gathered.shape

    def body(g_ref, o_ref):
        acc = g_ref[0]
        for k in range(1, N_DEV):
            acc = acc + g_ref[k]
        o_ref[...] = acc

    vmem = pl.BlockSpec(memory_space=pltpu.VMEM)
    return pl.pallas_call(body, name="small_grad_sum", in_specs=[vmem], out_specs=vmem,
                          out_shape=jax.ShapeDtypeStruct((rows, cols), F32))(gathered)


def _adam_small(w, g, m, v):
    def body(w_ref, g_ref, m_ref, v_ref, do_ref, mo_ref, vo_ref):
        delta, m_new, v_new = _adam_math(w_ref[...], g_ref[...], m_ref[...], v_ref[...])
        do_ref[...] = delta
        mo_ref[...] = m_new
        vo_ref[...] = v_new

    vmem = pl.BlockSpec(memory_space=pltpu.VMEM)
    out = jax.ShapeDtypeStruct(w.shape, F32)
    return pl.pallas_call(body, name="adam_small", in_specs=[vmem] * 4, out_specs=[vmem] * 3,
                          out_shape=[out, out, out])(w, g, m, v)


def _proj_fwd(h, win_g):
    s, d = h.shape
    sw = win_g.shape[2]
    tn = min(512, sw)
    nh = sw // tn

    def body(h_ref, w_ref, o_ref):
        for rs in _chunks(s, 512):
            o_ref[rs, :] = _dot(h_ref[rs, :], w_ref[...], NN)

    return pl.pallas_call(
        body, name="proj_fwd", grid=(N_DEV * nh,),
        in_specs=[pl.BlockSpec((s, d), lambda j: (0, 0)),
                  pl.BlockSpec((None, d, tn), lambda j: (j // nh, 0, j % nh))],
        out_specs=pl.BlockSpec((None, s, tn), lambda j: (j // nh, 0, j % nh)),
        out_shape=jax.ShapeDtypeStruct((N_DEV, s, sw), F32),
        compiler_params=_cp(1),
    )(h, win_g)


def _conv_fwd(proj, conv_w, conv_b):
    _, s, sw = proj.shape
    tc = min(LANES, sw)

    def body(ba_ref, ca_ref, va_ref, cw_ref, cb_ref, z_ref):
        cv = ca_ref[...] * va_ref[...]
        u = (cb_ref[...] + cw_ref[0:1, :] * _shift_down(cv, 2) + cw_ref[1:2, :] * _shift_down(cv, 1)
             + cw_ref[2:3, :] * cv)
        z_ref[...] = (ba_ref[...] * u).astype(BF16)

    def part(k):
        return pl.BlockSpec((None, s, tc), lambda i: (k, 0, i))

    return pl.pallas_call(
        body, name="conv_fwd", grid=(sw // tc,),
        in_specs=[part(0), part(1), part(2),
                  pl.BlockSpec((CONV_K, tc), lambda i: (0, i)), pl.BlockSpec((1, tc), lambda i: (0, i))],
        out_specs=pl.BlockSpec((s, tc), lambda i: (0, i)),
        out_shape=jax.ShapeDtypeStruct((s, sw), BF16),
        compiler_params=_cp(1),
    )(proj, proj, proj, conv_w, conv_b)


def _pool_counts(shape, window):
    t = lax.broadcasted_iota(jnp.int32, shape, 0)
    return jnp.minimum(t + 1, window).astype(F32)


def _pool_fwd(proj):
    _, s, sw = proj.shape
    gw = sw // len(POOL_WINDOWS)

    def body(v_ref, p_ref):
        for gi, window in enumerate(POOL_WINDOWS):
            @pl.when(pl.program_id(0) == gi)
            def _():
                v = v_ref[...]
                acc, k = v, 1
                while k < window:
                    acc = acc + _shift_down(acc, k)
                    k *= 2
                p_ref[...] = (acc / _pool_counts(v.shape, window) - v).astype(BF16)

    return pl.pallas_call(
        body, name="pool_fwd", grid=(len(POOL_WINDOWS),),
        in_specs=[pl.BlockSpec((None, s, gw), lambda g: (3, 0, g))],
        out_specs=pl.BlockSpec((s, gw), lambda g: (0, g)),
        out_shape=jax.ShapeDtypeStruct((s, sw), BF16),
        compiler_params=_cp(1),
    )(proj)


def _merge_fwd(z, wa, p, wpool, proj, b_gate2, pool_scale):
    s, sw = z.shape
    d = wa.shape[1]
    tn = d // N_DEV
    gw = sw // len(POOL_WINDOWS)
    nq = sw // tn

    def body(z_ref, wa_ref, p_ref, wp_ref, ga_ref, gb_ref, bg_ref, sc_ref, ya_ref, yb_ref, m_ref):
        for rs in _chunks(s, 512):
            ya = _dot(z_ref[rs, :], wa_ref[...], NN)
            yb = _dot(p_ref[rs, :], wp_ref[...], NN)
            sa = _sigmoid(ga_ref[rs, :] + bg_ref[0:1, :])
            sb = _sigmoid(gb_ref[rs, :] + bg_ref[1:2, :])
            ya_ref[rs, :] = ya.astype(BF16)
            yb_ref[rs, :] = yb.astype(BF16)
            m_ref[rs, :] = (sa * ya + sb * (yb * sc_ref[...])).astype(BF16)

    col = pl.BlockSpec((s, tn), lambda j: (0, j))
    out = jax.ShapeDtypeStruct((s, d), BF16)
    return pl.pallas_call(
        body, name="merge_fwd", grid=(N_DEV,),
        in_specs=[pl.BlockSpec((s, sw), lambda j: (0, 0)),
                  pl.BlockSpec((sw, tn), lambda j: (0, j)),
                  pl.BlockSpec((s, gw), lambda j: (0, j // 2)),
                  pl.BlockSpec((None, gw, tn), lambda j: (j // 2, 0, j % 2)),
                  pl.BlockSpec((None, s, tn), lambda j: (4 + j // nq, 0, j % nq)),
                  pl.BlockSpec((None, s, tn), lambda j: (6 + j // nq, 0, j % nq)),
                  pl.BlockSpec((2, tn), lambda j: (0, j)),
                  pl.BlockSpec((1, tn), lambda j: (0, j))],
        out_specs=[col, col, col],
        out_shape=[out, out, out],
        compiler_params=_cp(1),
    )(z, wa, p, wpool, proj, proj, b_gate2, pool_scale)


def _wo_fwd(merged, wo, x2d, g2):
    s, d = x2d.shape
    tm = min(256, s)

    def body(m_ref, wo_ref, x_ref, g_ref, x1_ref, h2_ref):
        x1 = x_ref[...] + _dot(m_ref[...], wo_ref[...], NN)
        x1_ref[...] = x1
        r = lax.rsqrt(jnp.mean(x1 * x1, axis=-1, keepdims=True) + EPS)
        h2_ref[...] = (x1 * r * g_ref[...]).astype(BF16)

    row = pl.BlockSpec((tm, d), lambda i: (i, 0))
    return pl.pallas_call(
        body, name="wo_fwd", grid=(s // tm,),
        in_specs=[row, pl.BlockSpec((d, d), lambda i: (0, 0)), row, pl.BlockSpec((1, d), lambda i: (0, 0))],
        out_specs=[row, row],
        out_shape=[jax.ShapeDtypeStruct((s, d), F32), jax.ShapeDtypeStruct((s, d), BF16)],
        compiler_params=_cp(1),
    )(merged, wo, x2d, g2)


def _ffn_up_fwd(h2, wg_g, wu_g):
    s, d = h2.shape
    f8 = wg_g.shape[2]

    def body(h_ref, wg_ref, wu_ref, g_ref, u_ref):
        for rs in _chunks(s, 512):
            a = h_ref[rs, :]
            g_ref[rs, :] = _dot(a, wg_ref[...], NN).astype(BF16)
            u_ref[rs, :] = _dot(a, wu_ref[...], NN).astype(BF16)

    wspec = pl.BlockSpec((None, d, f8), lambda j: (j, 0, 0))
    ospec = pl.BlockSpec((None, s, f8), lambda j: (j, 0, 0))
    out = jax.ShapeDtypeStruct((N_DEV, s, f8), BF16)
    return pl.pallas_call(
        body, name="ffn_up_fwd", grid=(N_DEV,),
        in_specs=[pl.BlockSpec((s, d), lambda j: (0, 0)), wspec, wspec],
        out_specs=[ospec, ospec], out_shape=[out, out],
        compiler_params=_cp(1),
    )(h2, wg_g, wu_g)


def _ffn_down_loss(gact, uact, wd_g, x1, target, final_g):
    _, s, f8 = gact.shape
    d = x1.shape[1]
    tm = min(256, s)
    last = N_DEV - 1

    def body(g_ref, u_ref, wd_ref, x1_ref, t_ref, gf_ref, dx_ref, dxb_ref, dgf_ref, loss_ref, acc_ref):
        i, j = pl.program_id(0), pl.program_id(1)

        @pl.when(j == 0)
        def _():
            acc_ref[...] = jnp.zeros_like(acc_ref)

        @pl.when((i == 0) & (j == 0))
        def _():
            dgf_ref[...] = jnp.zeros_like(dgf_ref)
            loss_ref[...] = jnp.zeros_like(loss_ref)

        for rs in _chunks(tm, 256):
            g = g_ref[rs, :].astype(F32)
            act = (g * _sigmoid(g) * u_ref[rs, :].astype(F32)).astype(BF16)
            acc_ref[rs, :] += _dot(act, wd_ref[...], NN)

        @pl.when(j == last)
        def _():
            for rs in _chunks(tm, 256):
                x2 = x1_ref[rs, :] + acc_ref[rs, :]
                r = lax.rsqrt(jnp.mean(x2 * x2, axis=-1, keepdims=True) + EPS)
                nrm = x2 * r
                gf = gf_ref[...]
                err = nrm * gf - t_ref[rs, :]
                loss_ref[...] += jnp.sum(err * err) * (0.5 / d)
                dy = err * (1.0 / d)
                dgf_ref[...] += jnp.sum(dy * nrm, axis=0, keepdims=True)
                dn = dy * gf
                dx = r * (dn - nrm * jnp.mean(dn * nrm, axis=-1, keepdims=True))
                dx_ref[rs, :] = dx
                dxb_ref[rs, :] = dx.astype(BF16)

    aspec = pl.BlockSpec((None, tm, f8), lambda i, j: (j, i, 0))
    row = pl.BlockSpec((tm, d), lambda i, j: (i, 0))
    return pl.pallas_call(
        body, name="ffn_down_loss", grid=(s // tm, N_DEV),
        in_specs=[aspec, aspec, pl.BlockSpec((None, f8, d), lambda i, j: (j, 0, 0)), row, row,
                  pl.BlockSpec((1, d), lambda i, j: (0, 0))],
        out_specs=[row, row, pl.BlockSpec((1, d), lambda i, j: (0, 0)),
                   pl.BlockSpec((8, LANES), lambda i, j: (0, 0))],
        out_shape=[jax.ShapeDtypeStruct((s, d), F32), jax.ShapeDtypeStruct((s, d), BF16),
                   jax.ShapeDtypeStruct((1, d), F32), jax.ShapeDtypeStruct((8, LANES), F32)],
        scratch_shapes=[pltpu.VMEM((tm, d), F32)],
        compiler_params=_cp(2),
    )(gact, uact, wd_g, x1, target, final_g)


def _ffn_act_bwd(dx2b, wd_g, gact, uact):
    s, d = dx2b.shape
    f8 = gact.shape[2]
    tm = min(1024, s)

    def body(dx_ref, wd_ref, g_ref, u_ref, dg_ref, du_ref, act_ref):
        for rs in _chunks(tm, 256):
            da = _dot(dx_ref[rs, :], wd_ref[...], NT)
            g = g_ref[rs, :].astype(F32)
            u = u_ref[rs, :].astype(F32)
            sg = _sigmoid(g)
            silu = g * sg
            act_ref[rs, :] = (silu * u).astype(BF16)
            du_ref[rs, :] = (da * silu).astype(BF16)
            dg_ref[rs, :] = (da * u * (sg * (1.0 + g * (1.0 - sg)))).astype(BF16)

    aspec = pl.BlockSpec((None, tm, f8), lambda j, i: (j, i, 0))
    out = jax.ShapeDtypeStruct((N_DEV, s, f8), BF16)
    return pl.pallas_call(
        body, name="ffn_act_bwd", grid=(N_DEV, s // tm),
        in_specs=[pl.BlockSpec((tm, d), lambda j, i: (i, 0)),
                  pl.BlockSpec((None, f8, d), lambda j, i: (j, 0, 0)), aspec, aspec],
        out_specs=[aspec, aspec, aspec], out_shape=[out, out, out],
        compiler_params=_cp(2),
    )(dx2b, wd_g, gact, uact)


def _wgrad_shard_a(a3, b, name):
    _, s, k = a3.shape
    n = b.shape[1]
    ts = min(512, s)
    ns = s // ts

    def body(a_ref, b_ref, o_ref, acc_ref):
        i = pl.program_id(1)

        @pl.when(i == 0)
        def _():
            acc_ref[...] = jnp.zeros_like(acc_ref)

        acc_ref[...] += _dot(a_ref[...], b_ref[...], TN)

        @pl.when(i == ns - 1)
        def _():
            o_ref[...] = acc_ref[...].astype(BF16)

    return pl.pallas_call(
        body, name=name, grid=(N_DEV, ns),
        in_specs=[pl.BlockSpec((None, ts, k), lambda j, i: (j, i, 0)),
                  pl.BlockSpec((ts, n), lambda j, i: (i, 0))],
        out_specs=pl.BlockSpec((None, k, n), lambda j, i: (j, 0, 0)),
        out_shape=jax.ShapeDtypeStruct((N_DEV, k, n), BF16),
        scratch_shapes=[pltpu.VMEM((k, n), F32)],
        compiler_params=_cp(2),
    )(a3, b)


def _wgrad_shard_b(a, b3, name):
    s, k = a.shape
    n = b3.shape[2]
    ts = min(512, s)
    ns = s // ts

    def body(a_ref, b_ref, o_ref, acc_ref):
        i = pl.program_id(1)

        @pl.when(i == 0)
        def _():
            acc_ref[...] = jnp.zeros_like(acc_ref)

        acc_ref[...] += _dot(a_ref[...], b_ref[...], TN)

        @pl.when(i == ns - 1)
        def _():
            o_ref[...] = acc_ref[...].astype(BF16)

    return pl.pallas_call(
        body, name=name, grid=(N_DEV, ns),
        in_specs=[pl.BlockSpec((ts, k), lambda j, i: (i, 0)),
                  pl.BlockSpec((None, ts, n), lambda j, i: (j, i, 0))],
        out_specs=pl.BlockSpec((None, k, n), lambda j, i: (j, 0, 0)),
        out_shape=jax.ShapeDtypeStruct((N_DEV, k, n), BF16),
        scratch_shapes=[pltpu.VMEM((k, n), F32)],
        compiler_params=_cp(2),
    )(a, b3)


def _wgrad_full(a, b, name):
    s, k = a.shape
    n = b.shape[1]
    tk = min(512, k)
    ts = min(512, s)
    ns = s // ts

    def body(a_ref, b_ref, o_ref, acc_ref):
        i = pl.program_id(1)

        @pl.when(i == 0)
        def _():
            acc_ref[...] = jnp.zeros_like(acc_ref)

        acc_ref[...] += _dot(a_ref[...], b_ref[...], TN)

        @pl.when(i == ns - 1)
        def _():
            o_ref[...] = acc_ref[...].astype(BF16)

    return pl.pallas_call(
        body, name=name, grid=(k // tk, ns),
        in_specs=[pl.BlockSpec((ts, tk), lambda j, i: (i, j)),
                  pl.BlockSpec((ts, n), lambda j, i: (i, 0))],
        out_specs=pl.BlockSpec((tk, n), lambda j, i: (j, 0)),
        out_shape=jax.ShapeDtypeStruct((k, n), BF16),
        scratch_shapes=[pltpu.VMEM((tk, n), F32)],
        compiler_params=_cp(2),
    )(a, b)


def _wgrad_pool(p, dyb, n_groups):
    s, sw = p.shape
    d = dyb.shape[1]
    gw, go = sw // n_groups, d // n_groups
    ts = min(512, s)
    ns = s // ts

    def body(a_ref, b_ref, o_ref, acc_ref):
        i = pl.program_id(1)

        @pl.when(i == 0)
        def _():
            acc_ref[...] = jnp.zeros_like(acc_ref)

        acc_ref[...] += _dot(a_ref[...], b_ref[...], TN)

        @pl.when(i == ns - 1)
        def _():
            o_ref[...] = acc_ref[...].astype(BF16)

    return pl.pallas_call(
        body, name="wgrad_pool", grid=(n_groups, ns),
        in_specs=[pl.BlockSpec((ts, gw), lambda g, i: (i, g)),
                  pl.BlockSpec((ts, go), lambda g, i: (i, g))],
        out_specs=pl.BlockSpec((None, gw, go), lambda g, i: (g, 0, 0)),
        out_shape=jax.ShapeDtypeStruct((n_groups, gw, go), BF16),
        scratch_shapes=[pltpu.VMEM((gw, go), F32)],
        compiler_params=_cp(2),
    )(p, dyb)


def _input_grad_rms(pairs, xres, g, dres, name):
    s, d = xres.shape
    tm = min(256, s)
    last = N_DEV - 1
    npair = len(pairs)

    def body(*refs):
        ops = refs[:2 * npair]
        x_ref, g_ref, dres_ref, dx_ref, dxb_ref, dg_ref, acc_ref = refs[2 * npair:]
        i, j = pl.program_id(0), pl.program_id(1)

        @pl.when(j == 0)
        def _():
            acc_ref[...] = jnp.zeros_like(acc_ref)

        @pl.when((i == 0) & (j == 0))
        def _():
            dg_ref[...] = jnp.zeros_like(dg_ref)

        for rs in _chunks(tm, 256):
            part = _dot(ops[0][rs, :], ops[1][...], NT)
            for q in range(1, npair):
                part = part + _dot(ops[2 * q][rs, :], ops[2 * q + 1][...], NT)
            acc_ref[rs, :] += part

        @pl.when(j == last)
        def _():
            for rs in _chunks(tm, 256):
                xv = x_ref[rs, :]
                dh = acc_ref[rs, :]
                r = lax.rsqrt(jnp.mean(xv * xv, axis=-1, keepdims=True) + EPS)
                nrm = xv * r
                dg_ref[...] += jnp.sum(dh * nrm, axis=0, keepdims=True)
                dn = dh * g_ref[...]
                dx = dres_ref[rs, :] + r * (dn - nrm * jnp.mean(dn * nrm, axis=-1, keepdims=True))
                dx_ref[rs, :] = dx
                dxb_ref[rs, :] = dx.astype(BF16)

    in_specs, args = [], []
    for a3, w3 in pairs:
        k = a3.shape[2]
        in_specs += [pl.BlockSpec((None, tm, k), lambda i, j: (j, i, 0)),
                     pl.BlockSpec((None, d, k), lambda i, j: (j, 0, 0))]
        args += [a3, w3]
    row = pl.BlockSpec((tm, d), lambda i, j: (i, 0))
    vec = pl.BlockSpec((1, d), lambda i, j: (0, 0))
    return pl.pallas_call(
        body, name=name, grid=(s // tm, N_DEV),
        in_specs=in_specs + [row, vec, row],
        out_specs=[row, row, vec],
        out_shape=[jax.ShapeDtypeStruct((s, d), F32), jax.ShapeDtypeStruct((s, d), BF16),
                   jax.ShapeDtypeStruct((1, d), F32)],
        scratch_shapes=[pltpu.VMEM((tm, d), F32)],
        compiler_params=_cp(2),
    )(*args, xres, g, dres)


def _wo_bwd(dx1b, wo, ya, yb, proj, b_gate2, pool_scale):
    s, d = dx1b.shape
    sw = proj.shape[2]
    tn = d // N_DEV
    nq = sw // tn

    def body(dx_ref, wo_ref, ya_ref, yb_ref, ga_ref, gb_ref, bg_ref, sc_ref,
             dya_ref, dyb_ref, dp_ref, dbg_ref, dsc_ref):
        dbg_ref[...] = jnp.zeros_like(dbg_ref)
        dsc_ref[...] = jnp.zeros_like(dsc_ref)
        for rs in _chunks(s, 256):
            dm = _dot(dx_ref[rs, :], wo_ref[...], NT)
            ya_v = ya_ref[rs, :].astype(F32)
            yb_v = yb_ref[rs, :].astype(F32)
            sa = _sigmoid(ga_ref[rs, :] + bg_ref[0:1, :])
            sb = _sigmoid(gb_ref[rs, :] + bg_ref[1:2, :])
            sc = sc_ref[...]
            dya_ref[rs, :] = (dm * sa).astype(BF16)
            dsb = dm * sb
            dyb_ref[rs, :] = (dsb * sc).astype(BF16)
            dsc_ref[...] += jnp.sum(dsb * yb_v, axis=0, keepdims=True)
            dga = dm * ya_v * (sa * (1.0 - sa))
            dgb = dm * (yb_v * sc) * (sb * (1.0 - sb))
            dp_ref[0, rs, :] = dga.astype(BF16)
            dp_ref[1, rs, :] = dgb.astype(BF16)
            dbg_ref[0:1, :] += jnp.sum(dga, axis=0, keepdims=True)
            dbg_ref[1:2, :] += jnp.sum(dgb, axis=0, keepdims=True)

    col = pl.BlockSpec((s, tn), lambda j: (0, j))
    out = jax.ShapeDtypeStruct((s, d), BF16)
    return pl.pallas_call(
        body, name="wo_bwd", grid=(N_DEV,),
        in_specs=[pl.BlockSpec((s, d), lambda j: (0, 0)),
                  pl.BlockSpec((tn, d), lambda j: (j, 0)), col, col,
                  pl.BlockSpec((None, s, tn), lambda j: (4 + j // nq, 0, j % nq)),
                  pl.BlockSpec((None, s, tn), lambda j: (6 + j // nq, 0, j % nq)),
                  pl.BlockSpec((2, tn), lambda j: (0, j)),
                  pl.BlockSpec((1, tn), lambda j: (0, j))],
        out_specs=[col, col,
                   pl.BlockSpec((2, None, s, tn), lambda j: (1, j // nq, 0, j % nq)),
                   pl.BlockSpec((2, tn), lambda j: (0, j)),
                   pl.BlockSpec((1, tn), lambda j: (0, j))],
        out_shape=[out, out, jax.ShapeDtypeStruct((4, 2, s, sw), BF16),
                   jax.ShapeDtypeStruct((2, d), F32), jax.ShapeDtypeStruct((1, d), F32)],
        compiler_params=_cp(1),
    )(dx1b, wo, ya, yb, proj, proj, b_gate2, pool_scale)


def _conv_bwd(dproj, dya, wa, proj, conv_w, conv_b):
    s, d = dya.shape
    sw = wa.shape[0]
    tc = min(LANES, sw)

    def body(dproj_hbm, dya_ref, wa_ref, ba_ref, ca_ref, va_ref, cw_ref, cb_ref,
             dp_ref, dcw_ref, dcb_ref, dz_ref):
        del dproj_hbm
        for rs in _chunks(s, 512):
            dz_ref[rs, :] = _dot(dya_ref[rs, :], wa_ref[...], NT)
        dz = dz_ref[...]
        ba, ca, va = ba_ref[...], ca_ref[...], va_ref[...]
        cv = ca * va
        cv1, cv2 = _shift_down(cv, 1), _shift_down(cv, 2)
        w0, w1, w2 = cw_ref[0:1, :], cw_ref[1:2, :], cw_ref[2:3, :]
        u = cb_ref[...] + w0 * cv2 + w1 * cv1 + w2 * cv
        du = dz * ba
        dp_ref[0] = (dz * u).astype(BF16)
        dcv = w2 * du + w1 * _shift_up(du, 1) + w0 * _shift_up(du, 2)
        dp_ref[1] = (dcv * va).astype(BF16)
        dp_ref[2] = (dcv * ca).astype(BF16)
        dcw_ref[0:1, :] = jnp.sum(du * cv2, axis=0, keepdims=True)
        dcw_ref[1:2, :] = jnp.sum(du * cv1, axis=0, keepdims=True)
        dcw_ref[2:3, :] = jnp.sum(du * cv, axis=0, keepdims=True)
        dcb_ref[...] = jnp.sum(du, axis=0, keepdims=True)

    def part(k):
        return pl.BlockSpec((None, s, tc), lambda i: (k, 0, i))

    return pl.pallas_call(
        body, name="conv_bwd", grid=(sw // tc,),
        in_specs=[pl.BlockSpec(memory_space=pl.ANY),
                  pl.BlockSpec((s, d), lambda i: (0, 0)),
                  pl.BlockSpec((tc, d), lambda i: (i, 0)),
                  part(0), part(1), part(2),
                  pl.BlockSpec((CONV_K, tc), lambda i: (0, i)), pl.BlockSpec((1, tc), lambda i: (0, i))],
        out_specs=[pl.BlockSpec((3, s, tc), lambda i: (0, 0, i)),
                   pl.BlockSpec((CONV_K, tc), lambda i: (0, i)), pl.BlockSpec((1, tc), lambda i: (0, i))],
        out_shape=[jax.ShapeDtypeStruct(dproj.shape, BF16),
                   jax.ShapeDtypeStruct((CONV_K, sw), F32), jax.ShapeDtypeStruct((1, sw), F32)],
        scratch_shapes=[pltpu.VMEM((s, tc), F32)],
        input_output_aliases={0: 0},
        compiler_params=_cp(1),
    )(dproj, dya, wa, proj, proj, proj, conv_w, conv_b)


def _pool_bwd(dproj, dyb, wpool):
    s, d = dyb.shape
    n_groups, gw, go = wpool.shape

    def body(dproj_hbm, dyb_ref, wp_ref, dp_ref):
        del dproj_hbm
        for gi, window in enumerate(POOL_WINDOWS):
            @pl.when(pl.program_id(0) == gi)
            def _():
                dpool = _dot(dyb_ref[...], wp_ref[...], NT)
                acc, k = dpool / _pool_counts(dpool.shape, window), 1
                while k < window:
                    acc = acc + _shift_up(acc, k)
                    k *= 2
                dp_ref[...] = (acc - dpool).astype(BF16)

    return pl.pallas_call(
        body, name="pool_bwd", grid=(n_groups,),
        in_specs=[pl.BlockSpec(memory_space=pl.ANY),
                  pl.BlockSpec((s, go), lambda g: (0, g)),
                  pl.BlockSpec((None, gw, go), lambda g: (g, 0, 0))],
        out_specs=pl.BlockSpec((None, s, gw), lambda g: (3, 0, g)),
        out_shape=jax.ShapeDtypeStruct(dproj.shape, BF16),
        input_output_aliases={0: 0},
        compiler_params=_cp(1),
    )(dproj, dyb, wpool)


def _rows128(v):
    return v.reshape(-1, LANES)


def kernel(x, norm1_g, w_in, b_gate, conv_w, conv_b, w_a_out, w_pool, pool_scale, w_o, norm2_g, w_ffn_gate, w_ffn_up, w_ffn_down, final_g, loss_target, m_norm1_g, m_w_in, m_b_gate, m_conv_w, m_conv_b, m_w_a_out, m_w_pool, m_pool_scale, m_w_o, m_norm2_g, m_w_ffn_gate, m_w_ffn_up, m_w_ffn_down, m_final_g, v_norm1_g, v_w_in, v_b_gate, v_conv_w, v_conv_b, v_w_a_out, v_w_pool, v_pool_scale, v_w_o, v_norm2_g, v_w_ffn_gate, v_w_ffn_up, v_w_ffn_down, v_final_g):
    s, d = x.shape[1], x.shape[2]
    sw = w_in.shape[2]
    n_groups = w_pool.shape[1]
    gw = w_pool.shape[2]
    go = w_pool.shape[3] * N_DEV
    f8 = w_ffn_gate.shape[2]
    cws = conv_w.shape[2]
    assert sw == conv_w.shape[2] * N_DEV == gw * n_groups and go * n_groups == d and n_groups == len(POOL_WINDOWS)

    xi, yi, ci = _coords()
    me = 4 * xi + 2 * yi + ci
    my_chip = 2 * xi + yi

    x2d = x.reshape(s, d)
    target = loss_target.reshape(s, d)
    final_g2 = final_g.reshape(1, d)
    b_gate2 = b_gate.reshape(2, d)

    big_names = ["w_in", "w_a_out", "w_pool", "w_o", "w_ffn_gate", "w_ffn_up", "w_ffn_down"]
    big_w = [w_in, w_a_out, w_pool, w_o, w_ffn_gate, w_ffn_up, w_ffn_down]
    big_m = [m_w_in, m_w_a_out, m_w_pool, m_w_o, m_w_ffn_gate, m_w_ffn_up, m_w_ffn_down]
    big_v = [v_w_in, v_w_a_out, v_w_pool, v_w_o, v_w_ffn_gate, v_w_ffn_up, v_w_ffn_down]
    shapes2d = [(w.size // w.shape[-1], w.shape[-1]) for w in big_w]
    big_w2 = [w.reshape(sh) for w, sh in zip(big_w, shapes2d)]

    shards_bf16 = [_cast_bf16(w, "cast_" + nm) for w, nm in zip(big_w2, big_names)]
    win_g, wa_g, wpool_g, wo_g, wg_g, wu_g, wd_g = _allgather_big(shards_bf16)
    convw_g = _allgather_small(jnp.pad(conv_w.reshape(CONV_K, cws), ((0, 8 - CONV_K), (0, 0))), "allgather_conv_w")
    conv_w_full = convw_g[:, :CONV_K, :].transpose(1, 0, 2).reshape(CONV_K, sw)
    wa = wa_g.transpose(1, 0, 2).reshape(sw, d)
    wpool = wpool_g.reshape(N_DEV, n_groups, gw, go // N_DEV).transpose(1, 2, 0, 3).reshape(n_groups, gw, go)
    wo = wo_g.reshape(d, d)

    h = _rms_fwd(x2d, norm1_g)
    proj = _proj_fwd(h, win_g)
    z = _conv_fwd(proj, conv_w_full, conv_b)
    p = _pool_fwd(proj)
    ya, yb, merged = _merge_fwd(z, wa, p, wpool, proj, b_gate2, pool_scale)
    x1, h2 = _wo_fwd(merged, wo, x2d, norm2_g)
    gact, uact = _ffn_up_fwd(h2, wg_g, wu_g)
    dx2, dx2b, d_final_g, loss_blk = _ffn_down_loss(gact, uact, wd_g, x1, target, final_g2)

    dg_act, du_act, act = _ffn_act_bwd(dx2b, wd_g, gact, uact)
    gw_down = _wgrad_shard_a(act, dx2b, "wgrad_ffn_down")
    gw_gate = _wgrad_shard_b(h2, dg_act, "wgrad_ffn_gate")
    gw_up = _wgrad_shard_b(h2, du_act, "wgrad_ffn_up")
    dx1, dx1b, d_norm2_g = _input_grad_rms([(dg_act, wg_g), (du_act, wu_g)], x1, norm2_g, dx2, "ffn_in_bwd")
    dya, dyb, dproj42, d_b_gate, d_pool_scale = _wo_bwd(dx1b, wo, ya, yb, proj, b_gate2, pool_scale)
    gw_o = _wgrad_full(merged, dx1b, "wgrad_o")
    dproj = dproj42.reshape(N_DEV, s, sw)
    dproj, d_conv_w, d_conv_b = _conv_bwd(dproj, dya, wa, proj, conv_w_full, conv_b)
    dproj = _pool_bwd(dproj, dyb, wpool)
    gw_a = _wgrad_full(z, dya, "wgrad_a_out")
    gw_pool = _wgrad_pool(p, dyb, n_groups)
    gw_in = _wgrad_shard_b(h, dproj, "wgrad_in")
    grad_x, _, d_norm1_g = _input_grad_rms([(dproj, win_g)], x2d, norm1_g, dx1, "proj_in_bwd")

    grads3 = [gw_in,
              gw_a.reshape(sw, N_DEV, d // N_DEV).transpose(1, 0, 2),
              gw_pool.reshape(n_groups, gw, N_DEV, go // N_DEV).transpose(2, 0, 1, 3).reshape(N_DEV, n_groups * gw, go // N_DEV),
              gw_o.reshape(N_DEV, d // N_DEV, d),
              gw_gate, gw_up, gw_down]
    recv_sib = _exchange_sibling(grads3)
    chip_slots = (2 * jnp.arange(4, dtype=jnp.int32) + ci).astype(jnp.int32)
    psums = [_chip_partial(chip_slots, g3, r, "chip_partial_" + nm) for g3, r, nm in zip(grads3, recv_sib, big_names)]
    recv_chips = _exchange_chips(psums)
    own = jnp.stack([me, my_chip]).astype(jnp.int32)
    big_out = []
    for a, nm in enumerate(big_names):
        outs = _adam_big(own, big_w2[a], big_m[a].reshape(shapes2d[a]), big_v[a].reshape(shapes2d[a]),
                         grads3[a], recv_sib[a], recv_chips[a], "adam_" + nm)
        big_out.append([o.reshape(big_w[a].shape) for o in outs])

    small_parts = [d_norm1_g, d_b_gate, d_conv_w, d_conv_b, d_pool_scale, d_norm2_g, d_final_g, loss_blk]
    sizes = [v.size for v in small_parts]
    packed = jnp.concatenate([_rows128(v) for v in small_parts], axis=0)
    summed = _sum_small(_allgather_small(packed, "allgather_small_grads")).reshape(-1)
    offs = [0]
    for n in sizes:
        offs.append(offs[-1] + n)
    g_norm1, g_bgate, g_convw_full, g_convb, g_pscale, g_norm2, g_final, loss_sum = [
        summed[offs[k]:offs[k + 1]] for k in range(len(sizes))]
    loss = loss_sum[0]
    g_convw = lax.dynamic_slice(g_convw_full.reshape(CONV_K, sw), (0, me * cws), (CONV_K, cws))
    small_w = [norm1_g, b_gate, conv_w, conv_b, pool_scale, norm2_g, final_g]
    small_m = [m_norm1_g, m_b_gate, m_conv_w, m_conv_b, m_pool_scale, m_norm2_g, m_final_g]
    small_v = [v_norm1_g, v_b_gate, v_conv_w, v_conv_b, v_pool_scale, v_norm2_g, v_final_g]
    small_g = [g_norm1, g_bgate, g_convw, g_convb, g_pscale, g_norm2, g_final]

    def pack(parts):
        flat = jnp.concatenate([v.reshape(-1) for v in parts])
        pad = (-flat.size) % (8 * LANES)
        return jnp.pad(flat, (0, pad)).reshape(-1, LANES)

    s_delta, s_m, s_v = _adam_small(pack(small_w), pack(small_g), pack(small_m), pack(small_v))
    soffs = [0]
    for w in small_w:
        soffs.append(soffs[-1] + w.size)

    def unpack(buf):
        flat = buf.reshape(-1)
        return [flat[soffs[k]:soffs[k + 1]].reshape(small_w[k].shape) for k in range(len(small_w))]

    small_grads = [g.reshape(w.shape) for g, w in zip(small_g, small_w)]
    small_delta, small_new_m, small_new_v = unpack(s_delta), unpack(s_m), unpack(s_v)

    order = ["norm1_g", "w_in", "b_gate", "conv_w", "conv_b", "w_a_out", "w_pool", "pool_scale", "w_o", "norm2_g",
             "w_ffn_gate", "w_ffn_up", "w_ffn_down", "final_g"]
    small_names = ["norm1_g", "b_gate", "conv_w", "conv_b", "pool_scale", "norm2_g", "final_g"]
    per_kind = [{}, {}, {}, {}]
    for a, nm in enumerate(big_names):
        for kind in range(4):
            per_kind[kind][nm] = big_out[a][kind]
    for k, nm in enumerate(small_names):
        per_kind[0][nm] = small_grads[k]
        per_kind[1][nm] = small_delta[k]
        per_kind[2][nm] = small_new_m[k]
        per_kind[3][nm] = small_new_v[k]
    result = [loss, grad_x.reshape(x.shape)]
    for kind in range(4):
        result += [per_kind[kind][nm] for nm in order]
    return tuple(result)
```

```python
import functools

import jax
import jax.numpy as jnp
from jax import lax
from jax.experimental import pallas as pl
from jax.experimental.pallas import tpu as pltpu
from jax.experimental.pallas import tpu_sc as plsc

F32 = jnp.float32
BF16 = jnp.bfloat16
MESH = pl.DeviceIdType.MESH

N_DEV = 8
EPS = 1e-6
CONV_K = 3
POOL_WINDOWS = (2, 4, 8, 16)
ADAM_LR = 0.001
ADAM_B1 = 0.9
ADAM_B2 = 0.999
ADAM_EPS = 1e-08
ADAM_WD = 0.01
ADAM_STEP = 10

V7X_VMEM_LIMIT_BYTES = 56 * 1024 * 1024
LANES = 128

NN = ((1,), (0,))
NT = ((1,), (1,))
TN = ((0,), (0,))


def _dot(a, b, dims):
    return lax.dot_general(a, b, (dims, ((), ())), preferred_element_type=F32)


def _cp(n_axes):
    return pltpu.CompilerParams(dimension_semantics=("arbitrary",) * n_axes,
                                vmem_limit_bytes=V7X_VMEM_LIMIT_BYTES)


def _row_tile(rows, bytes_per_row, cap_bytes):
    best = None
    for t in range(16, rows + 1, 16):
        if rows % t == 0 and t * bytes_per_row <= cap_bytes:
            best = t
    return best if best is not None else rows


def _chunks(total, size):
    size = min(size, total)
    assert total % size == 0
    return [slice(r, r + size) for r in range(0, total, size)]


def _shift_down(v, k):
    row = lax.broadcasted_iota(jnp.int32, v.shape, 0)
    return jnp.where(row >= k, pltpu.roll(v, k, 0), 0.0)


def _shift_up(v, k):
    n = v.shape[0]
    row = lax.broadcasted_iota(jnp.int32, v.shape, 0)
    return jnp.where(row < n - k, pltpu.roll(v, n - k, 0), 0.0)


def _sigmoid(v):
    return jax.nn.sigmoid(v)


def _cast_bf16(w2d, name):
    rows, cols = w2d.shape
    tr = _row_tile(rows, cols * 4, 2 << 20)

    def body(i_ref, o_ref):
        o_ref[...] = i_ref[...].astype(BF16)

    return pl.pallas_call(
        body, name=name, grid=(rows // tr,),
        in_specs=[pl.BlockSpec((tr, cols), lambda i: (i, 0))],
        out_specs=pl.BlockSpec((tr, cols), lambda i: (i, 0)),
        out_shape=jax.ShapeDtypeStruct((rows, cols), BF16),
        compiler_params=_cp(1),
    )(w2d)


def _rms_fwd(x2d, g):
    s, d = x2d.shape
    tm = min(256, s)

    def body(x_ref, g_ref, h_ref):
        xv = x_ref[...]
        r = lax.rsqrt(jnp.mean(xv * xv, axis=-1, keepdims=True) + EPS)
        h_ref[...] = (xv * r * g_ref[...]).astype(BF16)

    return pl.pallas_call(
        body, name="rms1_fwd", grid=(s // tm,),
        in_specs=[pl.BlockSpec((tm, d), lambda i: (i, 0)), pl.BlockSpec((1, d), lambda i: (0, 0))],
        out_specs=pl.BlockSpec((tm, d), lambda i: (i, 0)),
        out_shape=jax.ShapeDtypeStruct((s, d), BF16),
        compiler_params=_cp(1),
    )(x2d, g)


def _coords():
    return lax.axis_index("x"), lax.axis_index("y"), lax.axis_index("c")


def _slot(p):
    return 4 * p[0] + 2 * p[1] + p[2]


def _handshake(peers):
    barrier = pltpu.get_barrier_semaphore()
    for peer in peers:
        pl.semaphore_signal(barrier, inc=1, device_id=peer, device_id_type=MESH)
    pl.semaphore_wait(barrier, len(peers))


def _sequencer_call(body, out_type, scratch_types, name, collective_id):
    return pl.kernel(
        body, out_type=out_type, name=name,
        mesh=plsc.ScalarSubcoreMesh(axis_name="seq", num_cores=1),
        scratch_types=scratch_types,
        compiler_params=pltpu.CompilerParams(collective_id=collective_id))


def _allgather_big(shards, name, collective_id):
    n = len(shards)

    def body(*refs):
        ins, outs = refs[:n], refs[n:2 * n]
        send_sems, recv_sems, local_sems = refs[2 * n:]
        x, y, c = _coords()
        me, sibling = (x, y, c), (x, y, 1 - c)
        chips = [(1 - x, y), (x, 1 - y), (1 - x, 1 - y)]
        _handshake([sibling] + [(*chip, c) for chip in chips])

        def copy(a, k, block, to, src=None):
            dst = outs[a].at[_slot(block)]
            return pltpu.make_async_remote_copy(
                src_ref=dst if src is None else src, dst_ref=dst,
                send_sem=send_sems.at[a, k], recv_sem=recv_sems.at[a, k],
                device_id=to, device_id_type=MESH)

        mine, first, passed = [], [], []
        for a in range(n):
            cp = pltpu.make_async_copy(ins[a], outs[a].at[_slot(me)], local_sems.at[a])
            cp.start()
            mine.append(cp)
            f = [copy(a, 0, me, sibling, src=ins[a])]
            f += [copy(a, 1 + j, me, (*chip, c), src=ins[a]) for j, chip in enumerate(chips)]
            for cp in f:
                cp.start()
            first += f
        for a in range(n):
            for j, chip in enumerate(chips):
                copy(a, 1 + j, (*chip, c), me).wait_recv()
                cp = copy(a, 4 + j, (*chip, c), sibling)
                cp.start()
                passed.append(cp)
        for a in range(n):
            copy(a, 0, sibling, me).wait_recv()
            for j, chip in enumerate(chips):
                copy(a, 4 + j, (*chip, 1 - c), me).wait_recv()
        for cp in first + passed:
            cp.wait_send()
        for cp in mine:
            cp.wait()

    return _sequencer_call(
        body, [jax.ShapeDtypeStruct((N_DEV,) + s.shape, s.dtype) for s in shards],
        [pltpu.SemaphoreType.DMA((n, 7)), pltpu.SemaphoreType.DMA((n, 7)), pltpu.SemaphoreType.DMA((n,))],
        name, collective_id)(*shards)


def _exchange_sibling(grads):
    n = len(grads)

    def body(*refs):
        ins, outs = refs[:n], refs[n:2 * n]
        send_sems, recv_sems = refs[2 * n:]
        x, y, c = _coords()
        sibling = (x, y, 1 - c)
        copies = []
        for a in range(n):
            for q in range(4):
                cp = pltpu.make_async_remote_copy(
                    src_ref=ins[a].at[2 * q + (1 - c)], dst_ref=outs[a].at[q],
                    send_sem=send_sems.at[a, q], recv_sem=recv_sems.at[a, q],
                    device_id=sibling, device_id_type=MESH)
                cp.start()
                copies.append(cp)
        for cp in copies:
            cp.wait_recv()
        for cp in copies:
            cp.wait_send()

    any_spec = pl.BlockSpec(memory_space=pl.ANY)
    return pl.pallas_call(
        body, name="grad_exchange_sibling",
        in_specs=[any_spec] * n, out_specs=[any_spec] * n,
        out_shape=[jax.ShapeDtypeStruct((4,) + g.shape[1:], g.dtype) for g in grads],
        scratch_shapes=[pltpu.SemaphoreType.DMA((n, 4)), pltpu.SemaphoreType.DMA((n, 4))],
    )(*grads)


def _exchange_chips(psums):
    n = len(psums)

    def body(*refs):
        ins, outs = refs[:n], refs[n:2 * n]
        send_sems, recv_sems = refs[2 * n:]
        x, y, c = _coords()
        chips = [(1 - x, y), (x, 1 - y), (1 - x, 1 - y)]
        copies = []
        for a in range(n):
            for j, chip in enumerate(chips):
                cp = pltpu.make_async_remote_copy(
                    src_ref=ins[a].at[2 * chip[0] + chip[1]], dst_ref=outs[a].at[j],
                    send_sem=send_sems.at[a, j], recv_sem=recv_sems.at[a, j],
                    device_id=(*chip, c), device_id_type=MESH)
                cp.start()
                copies.append(cp)
        for cp in copies:
            cp.wait_recv()
        for cp in copies:
            cp.wait_send()

    any_spec = pl.BlockSpec(memory_space=pl.ANY)
    return pl.pallas_call(
        body, name="grad_exchange_chips",
        in_specs=[any_spec] * n, out_specs=[any_spec] * n,
        out_shape=[jax.ShapeDtypeStruct((3,) + p.shape[1:], p.dtype) for p in psums],
        scratch_shapes=[pltpu.SemaphoreType.DMA((n, 3)), pltpu.SemaphoreType.DMA((n, 3))],
    )(*psums)


def _allgather_small(v2d, name):
    rows, cols = v2d.shape

    def body(v_ref, out_ref, send_sems, recv_sems):
        x, y, c = _coords()
        me = (x, y, c)
        out_ref[_slot(me)] = v_ref[...]
        peers = []
        for k in range(1, N_DEV):
            fx, fy, fc = (k >> 2) & 1, (k >> 1) & 1, k & 1
            peers.append(((1 - x) if fx else x, (1 - y) if fy else y, (1 - c) if fc else c))
        sends = []
        for k, peer in enumerate(peers):
            cp = pltpu.make_async_remote_copy(
                src_ref=v_ref, dst_ref=out_ref.at[_slot(me)],
                send_sem=send_sems.at[k], recv_sem=recv_sems.at[k],
                device_id=peer, device_id_type=MESH)
            cp.start()
            sends.append(cp)
        for k, peer in enumerate(peers):
            pltpu.make_async_remote_copy(
                src_ref=v_ref, dst_ref=out_ref.at[_slot(peer)],
                send_sem=send_sems.at[k], recv_sem=recv_sems.at[k],
                device_id=peer, device_id_type=MESH).wait_recv()
        for cp in sends:
            cp.wait_send()

    vmem = pl.BlockSpec(memory_space=pltpu.VMEM)
    return pl.pallas_call(
        body, name=name, in_specs=[vmem], out_specs=vmem,
        out_shape=jax.ShapeDtypeStruct((N_DEV, rows, cols), v2d.dtype),
        scratch_shapes=[pltpu.SemaphoreType.DMA((N_DEV - 1,)), pltpu.SemaphoreType.DMA((N_DEV - 1,))],
    )(v2d)


def _chip_partial(slots, g3, recv, name):
    _, rows, cols = g3.shape
    tr = _row_tile(rows, cols * 4, 1 << 20)

    def body(slots_ref, g_ref, r_ref, o_ref):
        o_ref[...] = (g_ref[...].astype(F32) + r_ref[...].astype(F32)).astype(BF16)

    return pl.pallas_call(
        body, name=name,
        grid_spec=pltpu.PrefetchScalarGridSpec(
            num_scalar_prefetch=1, grid=(4, rows // tr),
            in_specs=[pl.BlockSpec((None, tr, cols), lambda q, i, sl: (sl[q], i, 0)),
                      pl.BlockSpec((None, tr, cols), lambda q, i, sl: (q, i, 0))],
            out_specs=pl.BlockSpec((None, tr, cols), lambda q, i, sl: (q, i, 0))),
        out_shape=jax.ShapeDtypeStruct((4, rows, cols), BF16),
        compiler_params=_cp(2),
    )(slots, g3, recv)


def _adam_math(w, g, m, v):
    m = ADAM_B1 * m + (1.0 - ADAM_B1) * g
    v = ADAM_B2 * v + (1.0 - ADAM_B2) * (g * g)
    m_hat = m / (1.0 - ADAM_B1 ** ADAM_STEP)
    v_hat = v / (1.0 - ADAM_B2 ** ADAM_STEP)
    delta = -ADAM_LR * (m_hat / (jnp.sqrt(v_hat) + ADAM_EPS) + ADAM_WD * w)
    return delta, m, v


def _adam_big(own, w, m, v, g3, recv_sib, recv_chips, name):
    rows, cols = w.shape
    tr = _row_tile(rows, cols * 4, 1 << 20)

    def body(own_ref, w_ref, m_ref, v_ref, g_ref, rs_ref, rc_ref, go_ref, do_ref, mo_ref, vo_ref):
        g = g_ref[...].astype(F32) + rs_ref[...].astype(F32)
        g = g + rc_ref[0].astype(F32)
        g = g + rc_ref[1].astype(F32)
        g = g + rc_ref[2].astype(F32)
        delta, m_new, v_new = _adam_math(w_ref[...], g, m_ref[...], v_ref[...])
        go_ref[...] = g
        do_ref[...] = delta
        mo_ref[...] = m_new
        vo_ref[...] = v_new

    blk = pl.BlockSpec((tr, cols), lambda i, o: (i, 0))
    out = jax.ShapeDtypeStruct((rows, cols), F32)
    return pl.pallas_call(
        body, name=name,
        grid_spec=pltpu.PrefetchScalarGridSpec(
            num_scalar_prefetch=1, grid=(rows // tr,),
            in_specs=[blk, blk, blk,
                      pl.BlockSpec((None, tr, cols), lambda i, o: (o[0], i, 0)),
                      pl.BlockSpec((None, tr, cols), lambda i, o: (o[1], i, 0)),
                      pl.BlockSpec((3, tr, cols), lambda i, o: (0, i, 0))],
            out_specs=[blk, blk, blk, blk]),
        out_shape=[out, out, out, out],
        compiler_params=_cp(1),
    )(own, w, m, v, g3, recv_sib, recv_chips)


def _sum_small(gathered):
    _, rows, cols = gathered.shape

    def body(g_ref, o_ref):
        acc = g_ref[0]
        for k in range(1, N_DEV):
            acc = acc + g_ref[k]
        o_ref[...] = acc

    vmem = pl.BlockSpec(memory_space=pltpu.VMEM)
    return pl.pallas_call(body, name="small_grad_sum", in_specs=[vmem], out_specs=vmem,
                          out_shape=jax.ShapeDtypeStruct((rows, cols), F32))(gathered)


def _adam_small(w, g, m, v):
    def body(w_ref, g_ref, m_ref, v_ref, do_ref, mo_ref, vo_ref):
        delta, m_new, v_new = _adam_math(w_ref[...], g_ref[...], m_ref[...], v_ref[...])
        do_ref[...] = delta
        mo_ref[...] = m_new
        vo_ref[...] = v_new

    vmem = pl.BlockSpec(memory_space=pltpu.VMEM)
    out = jax.ShapeDtypeStruct(w.shape, F32)
    return pl.pallas_call(body, name="adam_small", in_specs=[vmem] * 4, out_specs=[vmem] * 3,
                          out_shape=[out, out, out])(w, g, m, v)


def _proj_fwd(h, win_g):
    s, d = h.shape
    sw = win_g.shape[2]
    tn = min(512, sw)
    nh = sw // tn

    def body(h_ref, w_ref, o_ref):
        for rs in _chunks(s, 512):
            o_ref[rs, :] = _dot(h_ref[rs, :], w_ref[...], NN)

    return pl.pallas_call(
        body, name="proj_fwd", grid=(N_DEV * nh,),
        in_specs=[pl.BlockSpec((s, d), lambda j: (0, 0)),
                  pl.BlockSpec((None, d, tn), lambda j: (j // nh, 0, j % nh))],
        out_specs=pl.BlockSpec((None, s, tn), lambda j: (j // nh, 0, j % nh)),
        out_shape=jax.ShapeDtypeStruct((N_DEV, s, sw), F32),
        compiler_params=_cp(1),
    )(h, win_g)


def _conv_fwd(proj, conv_w, conv_b):
    _, s, sw = proj.shape
    tc = min(LANES, sw)

    def body(ba_ref, ca_ref, va_ref, cw_ref, cb_ref, z_ref):
        cv = ca_ref[...] * va_ref[...]
        u = (cb_ref[...] + cw_ref[0:1, :] * _shift_down(cv, 2) + cw_ref[1:2, :] * _shift_down(cv, 1)
             + cw_ref[2:3, :] * cv)
        z_ref[...] = (ba_ref[...] * u).astype(BF16)

    def part(k):
        return pl.BlockSpec((None, s, tc), lambda i: (k, 0, i))

    return pl.pallas_call(
        body, name="conv_fwd", grid=(sw // tc,),
        in_specs=[part(0), part(1), part(2),
                  pl.BlockSpec((CONV_K, tc), lambda i: (0, i)), pl.BlockSpec((1, tc), lambda i: (0, i))],
        out_specs=pl.BlockSpec((s, tc), lambda i: (0, i)),
        out_shape=jax.ShapeDtypeStruct((s, sw), BF16),
        compiler_params=_cp(1),
    )(proj, proj, proj, conv_w, conv_b)


def _pool_counts(shape, window):
    t = lax.broadcasted_iota(jnp.int32, shape, 0)
    return jnp.minimum(t + 1, window).astype(F32)


def _pool_fwd(proj):
    _, s, sw = proj.shape
    gw = sw // len(POOL_WINDOWS)

    def body(v_ref, p_ref):
        for gi, window in enumerate(POOL_WINDOWS):
            @pl.when(pl.program_id(0) == gi)
            def _():
                v = v_ref[...]
                acc, k = v, 1
                while k < window:
                    acc = acc + _shift_down(acc, k)
                    k *= 2
                p_ref[...] = (acc / _pool_counts(v.shape, window) - v).astype(BF16)

    return pl.pallas_call(
        body, name="pool_fwd", grid=(len(POOL_WINDOWS),),
        in_specs=[pl.BlockSpec((None, s, gw), lambda g: (3, 0, g))],
        out_specs=pl.BlockSpec((s, gw), lambda g: (0, g)),
        out_shape=jax.ShapeDtypeStruct((s, sw), BF16),
        compiler_params=_cp(1),
    )(proj)


def _merge_fwd(z, wa, p, wpool, proj, b_gate2, pool_scale):
    s, sw = z.shape
    d = wa.shape[1]
    tn = d // N_DEV
    gw = sw // len(POOL_WINDOWS)
    nq = sw // tn

    def body(z_ref, wa_ref, p_ref, wp_ref, ga_ref, gb_ref, bg_ref, sc_ref, ya_ref, yb_ref, m_ref):
        for rs in _chunks(s, 512):
            ya = _dot(z_ref[rs, :], wa_ref[...], NN)
            yb = _dot(p_ref[rs, :], wp_ref[...], NN)
            sa = _sigmoid(ga_ref[rs, :] + bg_ref[0:1, :])
            sb = _sigmoid(gb_ref[rs, :] + bg_ref[1:2, :])
            ya_ref[rs, :] = ya.astype(BF16)
            yb_ref[rs, :] = yb.astype(BF16)
            m_ref[rs, :] = (sa * ya + sb * (yb * sc_ref[...])).astype(BF16)

    col = pl.BlockSpec((s, tn), lambda j: (0, j))
    out = jax.ShapeDtypeStruct((s, d), BF16)
    return pl.pallas_call(
        body, name="merge_fwd", grid=(N_DEV,),
        in_specs=[pl.BlockSpec((s, sw), lambda j: (0, 0)),
                  pl.BlockSpec((sw, tn), lambda j: (0, j)),
                  pl.BlockSpec((s, gw), lambda j: (0, j // 2)),
                  pl.BlockSpec((None, gw, tn), lambda j: (j // 2, 0, j % 2)),
                  pl.BlockSpec((None, s, tn), lambda j: (4 + j // nq, 0, j % nq)),
                  pl.BlockSpec((None, s, tn), lambda j: (6 + j // nq, 0, j % nq)),
                  pl.BlockSpec((2, tn), lambda j: (0, j)),
                  pl.BlockSpec((1, tn), lambda j: (0, j))],
        out_specs=[col, col, col],
        out_shape=[out, out, out],
        compiler_params=_cp(1),
    )(z, wa, p, wpool, proj, proj, b_gate2, pool_scale)


def _wo_fwd(merged, wo, x2d, g2):
    s, d = x2d.shape
    tm = min(256, s)

    def body(m_ref, wo_ref, x_ref, g_ref, x1_ref, h2_ref):
        x1 = x_ref[...] + _dot(m_ref[...], wo_ref[...], NN)
        x1_ref[...] = x1
        r = lax.rsqrt(jnp.mean(x1 * x1, axis=-1, keepdims=True) + EPS)
        h2_ref[...] = (x1 * r * g_ref[...]).astype(BF16)

    row = pl.BlockSpec((tm, d), lambda i: (i, 0))
    return pl.pallas_call(
        body, name="wo_fwd", grid=(s // tm,),
        in_specs=[row, pl.BlockSpec((d, d), lambda i: (0, 0)), row, pl.BlockSpec((1, d), lambda i: (0, 0))],
        out_specs=[row, row],
        out_shape=[jax.ShapeDtypeStruct((s, d), F32), jax.ShapeDtypeStruct((s, d), BF16)],
        compiler_params=_cp(1),
    )(merged, wo, x2d, g2)


def _ffn_up_fwd(h2, wg_g, wu_g):
    s, d = h2.shape
    f8 = wg_g.shape[2]

    def body(h_ref, wg_ref, wu_ref, g_ref, u_ref):
        for rs in _chunks(s, 512):
            a = h_ref[rs, :]
            g_ref[rs, :] = _dot(a, wg_ref[...], NN).astype(BF16)
            u_ref[rs, :] = _dot(a, wu_ref[...], NN).astype(BF16)

    wspec = pl.BlockSpec((None, d, f8), lambda j: (j, 0, 0))
    ospec = pl.BlockSpec((None, s, f8), lambda j: (j, 0, 0))
    out = jax.ShapeDtypeStruct((N_DEV, s, f8), BF16)
    return pl.pallas_call(
        body, name="ffn_up_fwd", grid=(N_DEV,),
        in_specs=[pl.BlockSpec((s, d), lambda j: (0, 0)), wspec, wspec],
        out_specs=[ospec, ospec], out_shape=[out, out],
        compiler_params=_cp(1),
    )(h2, wg_g, wu_g)


def _ffn_down_loss(gact, uact, wd_g, x1, target, final_g):
    _, s, f8 = gact.shape
    d = x1.shape[1]
    tm = min(256, s)
    last = N_DEV - 1

    def body(g_ref, u_ref, wd_ref, x1_ref, t_ref, gf_ref, dx_ref, dxb_ref, dgf_ref, loss_ref, acc_ref):
        i, j = pl.program_id(0), pl.program_id(1)

        @pl.when(j == 0)
        def _():
            acc_ref[...] = jnp.zeros_like(acc_ref)

        @pl.when((i == 0) & (j == 0))
        def _():
            dgf_ref[...] = jnp.zeros_like(dgf_ref)
            loss_ref[...] = jnp.zeros_like(loss_ref)

        for rs in _chunks(tm, 256):
            g = g_ref[rs, :].astype(F32)
            act = (g * _sigmoid(g) * u_ref[rs, :].astype(F32)).astype(BF16)
            acc_ref[rs, :] += _dot(act, wd_ref[...], NN)

        @pl.when(j == last)
        def _():
            for rs in _chunks(tm, 256):
                x2 = x1_ref[rs, :] + acc_ref[rs, :]
                r = lax.rsqrt(jnp.mean(x2 * x2, axis=-1, keepdims=True) + EPS)
                nrm = x2 * r
                gf = gf_ref[...]
                err = nrm * gf - t_ref[rs, :]
                loss_ref[...] += jnp.sum(err * err) * (0.5 / d)
                dy = err * (1.0 / d)
                dgf_ref[...] += jnp.sum(dy * nrm, axis=0, keepdims=True)
                dn = dy * gf
                dx = r * (dn - nrm * jnp.mean(dn * nrm, axis=-1, keepdims=True))
                dx_ref[rs, :] = dx
                dxb_ref[rs, :] = dx.astype(BF16)

    aspec = pl.BlockSpec((None, tm, f8), lambda i, j: (j, i, 0))
    row = pl.BlockSpec((tm, d), lambda i, j: (i, 0))
    return pl.pallas_call(
        body, name="ffn_down_loss", grid=(s // tm, N_DEV),
        in_specs=[aspec, aspec, pl.BlockSpec((None, f8, d), lambda i, j: (j, 0, 0)), row, row,
                  pl.BlockSpec((1, d), lambda i, j: (0, 0))],
        out_specs=[row, row, pl.BlockSpec((1, d), lambda i, j: (0, 0)),
                   pl.BlockSpec((8, LANES), lambda i, j: (0, 0))],
        out_shape=[jax.ShapeDtypeStruct((s, d), F32), jax.ShapeDtypeStruct((s, d), BF16),
                   jax.ShapeDtypeStruct((1, d), F32), jax.ShapeDtypeStruct((8, LANES), F32)],
        scratch_shapes=[pltpu.VMEM((tm, d), F32)],
        compiler_params=_cp(2),
    )(gact, uact, wd_g, x1, target, final_g)


def _ffn_act_bwd(dx2b, wd_g, gact, uact):
    s, d = dx2b.shape
    f8 = gact.shape[2]
    tm = min(1024, s)

    def body(dx_ref, wd_ref, g_ref, u_ref, dg_ref, du_ref, act_ref):
        for rs in _chunks(tm, 256):
            da = _dot(dx_ref[rs, :], wd_ref[...], NT)
            g = g_ref[rs, :].astype(F32)
            u = u_ref[rs, :].astype(F32)
            sg = _sigmoid(g)
            silu = g * sg
            act_ref[rs, :] = (silu * u).astype(BF16)
            du_ref[rs, :] = (da * silu).astype(BF16)
            dg_ref[rs, :] = (da * u * (sg * (1.0 + g * (1.0 - sg)))).astype(BF16)

    aspec = pl.BlockSpec((None, tm, f8), lambda j, i: (j, i, 0))
    out = jax.ShapeDtypeStruct((N_DEV, s, f8), BF16)
    return pl.pallas_call(
        body, name="ffn_act_bwd", grid=(N_DEV, s // tm),
        in_specs=[pl.BlockSpec((tm, d), lambda j, i: (i, 0)),
                  pl.BlockSpec((None, f8, d), lambda j, i: (j, 0, 0)), aspec, aspec],
        out_specs=[aspec, aspec, aspec], out_shape=[out, out, out],
        compiler_params=_cp(2),
    )(dx2b, wd_g, gact, uact)


def _wgrad_shard_a(a3, b, name):
    _, s, k = a3.shape
    n = b.shape[1]
    ts = min(512, s)
    ns = s // ts

    def body(a_ref, b_ref, o_ref, acc_ref):
        i = pl.program_id(1)

        @pl.when(i == 0)
        def _():
            acc_ref[...] = jnp.zeros_like(acc_ref)

        acc_ref[...] += _dot(a_ref[...], b_ref[...], TN)

        @pl.when(i == ns - 1)
        def _():
            o_ref[...] = acc_ref[...].astype(BF16)

    return pl.pallas_call(
        body, name=name, grid=(N_DEV, ns),
        in_specs=[pl.BlockSpec((None, ts, k), lambda j, i: (j, i, 0)),
                  pl.BlockSpec((ts, n), lambda j, i: (i, 0))],
        out_specs=pl.BlockSpec((None, k, n), lambda j, i: (j, 0, 0)),
        out_shape=jax.ShapeDtypeStruct((N_DEV, k, n), BF16),
        scratch_shapes=[pltpu.VMEM((k, n), F32)],
        compiler_params=_cp(2),
    )(a3, b)


def _wgrad_shard_b(a, b3, name):
    s, k = a.shape
    n = b3.shape[2]
    ts = min(512, s)
    ns = s // ts

    def body(a_ref, b_ref, o_ref, acc_ref):
        i = pl.program_id(1)

        @pl.when(i == 0)
        def _():
            acc_ref[...] = jnp.zeros_like(acc_ref)

        acc_ref[...] += _dot(a_ref[...], b_ref[...], TN)

        @pl.when(i == ns - 1)
        def _():
            o_ref[...] = acc_ref[...].astype(BF16)

    return pl.pallas_call(
        body, name=name, grid=(N_DEV, ns),
        in_specs=[pl.BlockSpec((ts, k), lambda j, i: (i, 0)),
                  pl.BlockSpec((None, ts, n), lambda j, i: (j, i, 0))],
        out_specs=pl.BlockSpec((None, k, n), lambda j, i: (j, 0, 0)),
        out_shape=jax.ShapeDtypeStruct((N_DEV, k, n), BF16),
        scratch_shapes=[pltpu.VMEM((k, n), F32)],
        compiler_params=_cp(2),
    )(a, b3)


def _wgrad_full(a, b, name):
    s, k = a.shape
    n = b.shape[1]
    tk = min(512, k)
    ts = min(512, s)
    ns = s // ts

    def body(a_ref, b_ref, o_ref, acc_ref):
        i = pl.program_id(1)

        @pl.when(i == 0)
        def _():
            acc_ref[...] = jnp.zeros_like(acc_ref)

        acc_ref[...] += _dot(a_ref[...], b_ref[...], TN)

        @pl.when(i == ns - 1)
        def _():
            o_ref[...] = acc_ref[...].astype(BF16)

    return pl.pallas_call(
        body, name=name, grid=(k // tk, ns),
        in_specs=[pl.BlockSpec((ts, tk), lambda j, i: (i, j)),
                  pl.BlockSpec((ts, n), lambda j, i: (i, 0))],
        out_specs=pl.BlockSpec((tk, n), lambda j, i: (j, 0)),
        out_shape=jax.ShapeDtypeStruct((k, n), BF16),
        scratch_shapes=[pltpu.VMEM((tk, n), F32)],
        compiler_params=_cp(2),
    )(a, b)


def _wgrad_pool(p, dyb, n_groups):
    s, sw = p.shape
    d = dyb.shape[1]
    gw, go = sw // n_groups, d // n_groups
    ts = min(512, s)
    ns = s // ts

    def body(a_ref, b_ref, o_ref, acc_ref):
        i = pl.program_id(1)

        @pl.when(i == 0)
        def _():
            acc_ref[...] = jnp.zeros_like(acc_ref)

        acc_ref[...] += _dot(a_ref[...], b_ref[...], TN)

        @pl.when(i == ns - 1)
        def _():
            o_ref[...] = acc_ref[...].astype(BF16)

    return pl.pallas_call(
        body, name="wgrad_pool", grid=(n_groups, ns),
        in_specs=[pl.BlockSpec((ts, gw), lambda g, i: (i, g)),
                  pl.BlockSpec((ts, go), lambda g, i: (i, g))],
        out_specs=pl.BlockSpec((None, gw, go), lambda g, i: (g, 0, 0)),
        out_shape=jax.ShapeDtypeStruct((n_groups, gw, go), BF16),
        scratch_shapes=[pltpu.VMEM((gw, go), F32)],
        compiler_params=_cp(2),
    )(p, dyb)


def _input_grad_rms(pairs, xres, g, dres, name):
    s, d = xres.shape
    tm = min(256, s)
    last = N_DEV - 1
    npair = len(pairs)

    def body(*refs):
        ops = refs[:2 * npair]
        x_ref, g_ref, dres_ref, dx_ref, dxb_ref, dg_ref, acc_ref = refs[2 * npair:]
        i, j = pl.program_id(0), pl.program_id(1)

        @pl.when(j == 0)
        def _():
            acc_ref[...] = jnp.zeros_like(acc_ref)

        @pl.when((i == 0) & (j == 0))
        def _():
            dg_ref[...] = jnp.zeros_like(dg_ref)

        for rs in _chunks(tm, 256):
            part = _dot(ops[0][rs, :], ops[1][...], NT)
            for q in range(1, npair):
                part = part + _dot(ops[2 * q][rs, :], ops[2 * q + 1][...], NT)
            acc_ref[rs, :] += part

        @pl.when(j == last)
        def _():
            for rs in _chunks(tm, 256):
                xv = x_ref[rs, :]
                dh = acc_ref[rs, :]
                r = lax.rsqrt(jnp.mean(xv * xv, axis=-1, keepdims=True) + EPS)
                nrm = xv * r
                dg_ref[...] += jnp.sum(dh * nrm, axis=0, keepdims=True)
                dn = dh * g_ref[...]
                dx = dres_ref[rs, :] + r * (dn - nrm * jnp.mean(dn * nrm, axis=-1, keepdims=True))
                dx_ref[rs, :] = dx
                dxb_ref[rs, :] = dx.astype(BF16)

    in_specs, args = [], []
    for a3, w3 in pairs:
        k = a3.shape[2]
        in_specs += [pl.BlockSpec((None, tm, k), lambda i, j: (j, i, 0)),
                     pl.BlockSpec((None, d, k), lambda i, j: (j, 0, 0))]
        args += [a3, w3]
    row = pl.BlockSpec((tm, d), lambda i, j: (i, 0))
    vec = pl.BlockSpec((1, d), lambda i, j: (0, 0))
    return pl.pallas_call(
        body, name=name, grid=(s // tm, N_DEV),
        in_specs=in_specs + [row, vec, row],
        out_specs=[row, row, vec],
        out_shape=[jax.ShapeDtypeStruct((s, d), F32), jax.ShapeDtypeStruct((s, d), BF16),
                   jax.ShapeDtypeStruct((1, d), F32)],
        scratch_shapes=[pltpu.VMEM((tm, d), F32)],
        compiler_params=_cp(2),
    )(*args, xres, g, dres)


def _wo_bwd(dx1b, wo, ya, yb, proj, b_gate2, pool_scale):
    s, d = dx1b.shape
    sw = proj.shape[2]
    tn = d // N_DEV
    nq = sw // tn

    def body(dx_ref, wo_ref, ya_ref, yb_ref, ga_ref, gb_ref, bg_ref, sc_ref,
             dya_ref, dyb_ref, dp_ref, dbg_ref, dsc_ref):
        dbg_ref[...] = jnp.zeros_like(dbg_ref)
        dsc_ref[...] = jnp.zeros_like(dsc_ref)
        for rs in _chunks(s, 256):
            dm = _dot(dx_ref[rs, :], wo_ref[...], NT)
            ya_v = ya_ref[rs, :].astype(F32)
            yb_v = yb_ref[rs, :].astype(F32)
            sa = _sigmoid(ga_ref[rs, :] + bg_ref[0:1, :])
            sb = _sigmoid(gb_ref[rs, :] + bg_ref[1:2, :])
            sc = sc_ref[...]
            dya_ref[rs, :] = (dm * sa).astype(BF16)
            dsb = dm * sb
            dyb_ref[rs, :] = (dsb * sc).astype(BF16)
            dsc_ref[...] += jnp.sum(dsb * yb_v, axis=0, keepdims=True)
            dga = dm * ya_v * (sa * (1.0 - sa))
            dgb = dm * (yb_v * sc) * (sb * (1.0 - sb))
            dp_ref[0, rs, :] = dga.astype(BF16)
            dp_ref[1, rs, :] = dgb.astype(BF16)
            dbg_ref[0:1, :] += jnp.sum(dga, axis=0, keepdims=True)
            dbg_ref[1:2, :] += jnp.sum(dgb, axis=0, keepdims=True)

    col = pl.BlockSpec((s, tn), lambda j: (0, j))
    out = jax.ShapeDtypeStruct((s, d), BF16)
    return pl.pallas_call(
        body, name="wo_bwd", grid=(N_DEV,),
        in_specs=[pl.BlockSpec((s, d), lambda j: (0, 0)),
                  pl.BlockSpec((tn, d), lambda j: (j, 0)), col, col,
                  pl.BlockSpec((None, s, tn), lambda j: (4 + j // nq, 0, j % nq)),
                  pl.BlockSpec((None, s, tn), lambda j: (6 + j // nq, 0, j % nq)),
                  pl.BlockSpec((2, tn), lambda j: (0, j)),
                  pl.BlockSpec((1, tn), lambda j: (0, j))],
        out_specs=[col, col,
                   pl.BlockSpec((2, None, s, tn), lambda j: (1, j // nq, 0, j % nq)),
                   pl.BlockSpec((2, tn), lambda j: (0, j)),
                   pl.BlockSpec((1, tn), lambda j: (0, j))],
        out_shape=[out, out, jax.ShapeDtypeStruct((4, 2, s, sw), BF16),
                   jax.ShapeDtypeStruct((2, d), F32), jax.ShapeDtypeStruct((1, d), F32)],
        compiler_params=_cp(1),
    )(dx1b, wo, ya, yb, proj, proj, b_gate2, pool_scale)


def _conv_bwd(dproj, dya, wa, proj, conv_w, conv_b):
    s, d = dya.shape
    sw = wa.shape[0]
    tc = min(LANES, sw)

    def body(dproj_hbm, dya_ref, wa_ref, ba_ref, ca_ref, va_ref, cw_ref, cb_ref,
             dp_ref, dcw_ref, dcb_ref, dz_ref):
        del dproj_hbm
        for rs in _chunks(s, 512):
            dz_ref[rs, :] = _dot(dya_ref[rs, :], wa_ref[...], NT)
        dz = dz_ref[...]
        ba, ca, va = ba_ref[...], ca_ref[...], va_ref[...]
        cv = ca * va
        cv1, cv2 = _shift_down(cv, 1), _shift_down(cv, 2)
        w0, w1, w2 = cw_ref[0:1, :], cw_ref[1:2, :], cw_ref[2:3, :]
        u = cb_ref[...] + w0 * cv2 + w1 * cv1 + w2 * cv
        du = dz * ba
        dp_ref[0] = (dz * u).astype(BF16)
        dcv = w2 * du + w1 * _shift_up(du, 1) + w0 * _shift_up(du, 2)
        dp_ref[1] = (dcv * va).astype(BF16)
        dp_ref[2] = (dcv * ca).astype(BF16)
        dcw_ref[0:1, :] = jnp.sum(du * cv2, axis=0, keepdims=True)
        dcw_ref[1:2, :] = jnp.sum(du * cv1, axis=0, keepdims=True)
        dcw_ref[2:3, :] = jnp.sum(du * cv, axis=0, keepdims=True)
        dcb_ref[...] = jnp.sum(du, axis=0, keepdims=True)

    def part(k):
        return pl.BlockSpec((None, s, tc), lambda i: (k, 0, i))

    return pl.pallas_call(
        body, name="conv_bwd", grid=(sw // tc,),
        in_specs=[pl.BlockSpec(memory_space=pl.ANY),
                  pl.BlockSpec((s, d), lambda i: (0, 0)),
                  pl.BlockSpec((tc, d), lambda i: (i, 0)),
                  part(0), part(1), part(2),
                  pl.BlockSpec((CONV_K, tc), lambda i: (0, i)), pl.BlockSpec((1, tc), lambda i: (0, i))],
        out_specs=[pl.BlockSpec((3, s, tc), lambda i: (0, 0, i)),
                   pl.BlockSpec((CONV_K, tc), lambda i: (0, i)), pl.BlockSpec((1, tc), lambda i: (0, i))],
        out_shape=[jax.ShapeDtypeStruct(dproj.shape, BF16),
                   jax.ShapeDtypeStruct((CONV_K, sw), F32), jax.ShapeDtypeStruct((1, sw), F32)],
        scratch_shapes=[pltpu.VMEM((s, tc), F32)],
        input_output_aliases={0: 0},
        compiler_params=_cp(1),
    )(dproj, dya, wa, proj, proj, proj, conv_w, conv_b)


def _pool_bwd(dproj, dyb, wpool):
    s, d = dyb.shape
    n_groups, gw, go = wpool.shape

    def body(dproj_hbm, dyb_ref, wp_ref, dp_ref):
        del dproj_hbm
        for gi, window in enumerate(POOL_WINDOWS):
            @pl.when(pl.program_id(0) == gi)
            def _():
                dpool = _dot(dyb_ref[...], wp_ref[...], NT)
                acc, k = dpool / _pool_counts(dpool.shape, window), 1
                while k < window:
                    acc = acc + _shift_up(acc, k)
                    k *= 2
                dp_ref[...] = (acc - dpool).astype(BF16)

    return pl.pallas_call(
        body, name="pool_bwd", grid=(n_groups,),
        in_specs=[pl.BlockSpec(memory_space=pl.ANY),
                  pl.BlockSpec((s, go), lambda g: (0, g)),
                  pl.BlockSpec((None, gw, go), lambda g: (g, 0, 0))],
        out_specs=pl.BlockSpec((None, s, gw), lambda g: (3, 0, g)),
        out_shape=jax.ShapeDtypeStruct(dproj.shape, BF16),
        input_output_aliases={0: 0},
        compiler_params=_cp(1),
    )(dproj, dyb, wpool)


def _rows128(v):
    return v.reshape(-1, LANES)


def kernel(x, norm1_g, w_in, b_gate, conv_w, conv_b, w_a_out, w_pool, pool_scale, w_o, norm2_g, w_ffn_gate, w_ffn_up, w_ffn_down, final_g, loss_target, m_norm1_g, m_w_in, m_b_gate, m_conv_w, m_conv_b, m_w_a_out, m_w_pool, m_pool_scale, m_w_o, m_norm2_g, m_w_ffn_gate, m_w_ffn_up, m_w_ffn_down, m_final_g, v_norm1_g, v_w_in, v_b_gate, v_conv_w, v_conv_b, v_w_a_out, v_w_pool, v_pool_scale, v_w_o, v_norm2_g, v_w_ffn_gate, v_w_ffn_up, v_w_ffn_down, v_final_g):
    s, d = x.shape[1], x.shape[2]
    sw = w_in.shape[2]
    n_groups = w_pool.shape[1]
    gw = w_pool.shape[2]
    go = w_pool.shape[3] * N_DEV
    f8 = w_ffn_gate.shape[2]
    cws = conv_w.shape[2]
    assert sw == conv_w.shape[2] * N_DEV == gw * n_groups and go * n_groups == d and n_groups == len(POOL_WINDOWS)

    xi, yi, ci = _coords()
    me = 4 * xi + 2 * yi + ci
    my_chip = 2 * xi + yi

    x2d = x.reshape(s, d)
    target = loss_target.reshape(s, d)
    final_g2 = final_g.reshape(1, d)
    b_gate2 = b_gate.reshape(2, d)

    big_names = ["w_in", "w_a_out", "w_pool", "w_o", "w_ffn_gate", "w_ffn_up", "w_ffn_down"]
    big_w = [w_in, w_a_out, w_pool, w_o, w_ffn_gate, w_ffn_up, w_ffn_down]
    big_m = [m_w_in, m_w_a_out, m_w_pool, m_w_o, m_w_ffn_gate, m_w_ffn_up, m_w_ffn_down]
    big_v = [v_w_in, v_w_a_out, v_w_pool, v_w_o, v_w_ffn_gate, v_w_ffn_up, v_w_ffn_down]
    shapes2d = [(w.size // w.shape[-1], w.shape[-1]) for w in big_w]
    big_w2 = [w.reshape(sh) for w, sh in zip(big_w, shapes2d)]

    sb = [_cast_bf16(w, "cast_" + nm) for w, nm in zip(big_w2, big_names)]
    (win_g,) = _allgather_big(sb[0:1], "allgather_w_in", 1)
    wa_g, wpool_g, wo_g = _allgather_big(sb[1:4], "allgather_mixer", 2)
    wg_g, wu_g = _allgather_big(sb[4:6], "allgather_ffn_up", 3)
    (wd_g,) = _allgather_big(sb[6:7], "allgather_ffn_down", 4)
    convw_g = _allgather_small(jnp.pad(conv_w.reshape(CONV_K, cws), ((0, 8 - CONV_K), (0, 0))), "allgather_conv_w")
    conv_w_full = convw_g[:, :CONV_K, :].transpose(1, 0, 2).reshape(CONV_K, sw)
    wa = wa_g.transpose(1, 0, 2).reshape(sw, d)
    wpool = wpool_g.reshape(N_DEV, n_groups, gw, go // N_DEV).transpose(1, 2, 0, 3).reshape(n_groups, gw, go)
    wo = wo_g.reshape(d, d)

    h = _rms_fwd(x2d, norm1_g)
    proj = _proj_fwd(h, win_g)
    z = _conv_fwd(proj, conv_w_full, conv_b)
    p = _pool_fwd(proj)
    ya, yb, merged = _merge_fwd(z, wa, p, wpool, proj, b_gate2, pool_scale)
    x1, h2 = _wo_fwd(merged, wo, x2d, norm2_g)
    gact, uact = _ffn_up_fwd(h2, wg_g, wu_g)
    dx2, dx2b, d_final_g, loss_blk = _ffn_down_loss(gact, uact, wd_g, x1, target, final_g2)

    dg_act, du_act, act = _ffn_act_bwd(dx2b, wd_g, gact, uact)
    gw_down = _wgrad_shard_a(act, dx2b, "wgrad_ffn_down")
    gw_gate = _wgrad_shard_b(h2, dg_act, "wgrad_ffn_gate")
    gw_up = _wgrad_shard_b(h2, du_act, "wgrad_ffn_up")
    dx1, dx1b, d_norm2_g = _input_grad_rms([(dg_act, wg_g), (du_act, wu_g)], x1, norm2_g, dx2, "ffn_in_bwd")
    dya, dyb, dproj42, d_b_gate, d_pool_scale = _wo_bwd(dx1b, wo, ya, yb, proj, b_gate2, pool_scale)
    gw_o = _wgrad_full(merged, dx1b, "wgrad_o")
    dproj = dproj42.reshape(N_DEV, s, sw)
    dproj, d_conv_w, d_conv_b = _conv_bwd(dproj, dya, wa, proj, conv_w_full, conv_b)
    dproj = _pool_bwd(dproj, dyb, wpool)
    gw_a = _wgrad_full(z, dya, "wgrad_a_out")
    gw_pool = _wgrad_pool(p, dyb, n_groups)
    gw_in = _wgrad_shard_b(h, dproj, "wgrad_in")
    grad_x, _, d_norm1_g = _input_grad_rms([(dproj, win_g)], x2d, norm1_g, dx1, "proj_in_bwd")

    grads3 = [gw_in,
              gw_a.reshape(sw, N_DEV, d // N_DEV).transpose(1, 0, 2),
              gw_pool.reshape(n_groups, gw, N_DEV, go // N_DEV).transpose(2, 0, 1, 3).reshape(N_DEV, n_groups * gw, go // N_DEV),
              gw_o.reshape(N_DEV, d // N_DEV, d),
              gw_gate, gw_up, gw_down]
    recv_sib = _exchange_sibling(grads3)
    chip_slots = (2 * jnp.arange(4, dtype=jnp.int32) + ci).astype(jnp.int32)
    psums = [_chip_partial(chip_slots, g3, r, "chip_partial_" + nm) for g3, r, nm in zip(grads3, recv_sib, big_names)]
    recv_chips = _exchange_chips(psums)
    own = jnp.stack([me, my_chip]).astype(jnp.int32)
    big_out = []
    for a, nm in enumerate(big_names):
        outs = _adam_big(own, big_w2[a], big_m[a].reshape(shapes2d[a]), big_v[a].reshape(shapes2d[a]),
                         grads3[a], recv_sib[a], recv_chips[a], "adam_" + nm)
        big_out.append([o.reshape(big_w[a].shape) for o in outs])

    small_parts = [d_norm1_g, d_b_gate, d_conv_w, d_conv_b, d_pool_scale, d_norm2_g, d_final_g, loss_blk]
    sizes = [v.size for v in small_parts]
    packed = jnp.concatenate([_rows128(v) for v in small_parts], axis=0)
    summed = _sum_small(_allgather_small(packed, "allgather_small_grads")).reshape(-1)
    offs = [0]
    for n in sizes:
        offs.append(offs[-1] + n)
    g_norm1, g_bgate, g_convw_full, g_convb, g_pscale, g_norm2, g_final, loss_sum = [
        summed[offs[k]:offs[k + 1]] for k in range(len(sizes))]
    loss = loss_sum[0]
    g_convw = lax.dynamic_slice(g_convw_full.reshape(CONV_K, sw), (0, me * cws), (CONV_K, cws))
    small_w = [norm1_g, b_gate, conv_w, conv_b, pool_scale, norm2_g, final_g]
    small_m = [m_norm1_g, m_b_gate, m_conv_w, m_conv_b, m_pool_scale, m_norm2_g, m_final_g]
    small_v = [v_norm1_g, v_b_gate, v_conv_w, v_conv_b, v_pool_scale, v_norm2_g, v_final_g]
    small_g = [g_norm1, g_bgate, g_convw, g_convb, g_pscale, g_norm2, g_final]

    def pack(parts):
        flat = jnp.concatenate([v.reshape(-1) for v in parts])
        pad = (-flat.size) % (8 * LANES)
        return jnp.pad(flat, (0, pad)).reshape(-1, LANES)

    s_delta, s_m, s_v = _adam_small(pack(small_w), pack(small_g), pack(small_m), pack(small_v))
    soffs = [0]
    for w in small_w:
        soffs.append(soffs[-1] + w.size)

    def unpack(buf):
        flat = buf.reshape(-1)
        return [flat[soffs[k]:soffs[k + 1]].reshape(small_w[k].shape) for k in range(len(small_w))]

    small_grads = [g.reshape(w.shape) for g, w in zip(small_g, small_w)]
    small_delta, small_new_m, small_new_v = unpack(s_delta), unpack(s_m), unpack(s_v)

    order = ["norm1_g", "w_in", "b_gate", "conv_w", "conv_b", "w_a_out", "w_pool", "pool_scale", "w_o", "norm2_g",
             "w_ffn_gate", "w_ffn_up", "w_ffn_down", "final_g"]
    small_names = ["norm1_g", "b_gate", "conv_w", "conv_b", "pool_scale", "norm2_g", "final_g"]
    per_kind = [{}, {}, {}, {}]
    for a, nm in enumerate(big_names):
        for kind in range(4):
            per_kind[kind][nm] = big_out[a][kind]
    for k, nm in enumerate(small_names):
        per_kind[0][nm] = small_grads[k]
        per_kind[1][nm] = small_delta[k]
        per_kind[2][nm] = small_new_m[k]
        per_kind[3][nm] = small_new_v[k]
    result = [loss, grad_x.reshape(x.shape)]
    for kind in range(4):
        result += [per_kind[kind][nm] for nm in order]
    return tuple(result)
```

```python
import functools

import jax
import jax.numpy as jnp
from jax import lax
from jax.experimental import pallas as pl
from jax.experimental.pallas import tpu as pltpu
from jax.experimental.pallas import tpu_sc as plsc

F32 = jnp.float32
BF16 = jnp.bfloat16
MESH = pl.DeviceIdType.MESH

N_DEV = 8
EPS = 1e-6
CONV_K = 3
POOL_WINDOWS = (2, 4, 8, 16)
ADAM_LR = 0.001
ADAM_B1 = 0.9
ADAM_B2 = 0.999
ADAM_EPS = 1e-08
ADAM_WD = 0.01
ADAM_STEP = 10

V7X_VMEM_LIMIT_BYTES = 56 * 1024 * 1024
LANES = 128

COLLECTIVE_GATHER = 1
COLLECTIVE_SIBLING = 2
COLLECTIVE_CHIPS = 3
SEQUENCER_COST_BYTES = 4 * 10**9

NN = ((1,), (0,))
NT = ((1,), (1,))
TN = ((0,), (0,))


def _dot(a, b, dims):
    return lax.dot_general(a, b, (dims, ((), ())), preferred_element_type=F32)


def _cp(n_axes):
    return pltpu.CompilerParams(dimension_semantics=("arbitrary",) * n_axes,
                                vmem_limit_bytes=V7X_VMEM_LIMIT_BYTES)


def _row_tile(rows, bytes_per_row, cap_bytes):
    best = None
    for t in range(16, rows + 1, 16):
        if rows % t == 0 and t * bytes_per_row <= cap_bytes:
            best = t
    return best if best is not None else rows


def _chunks(total, size):
    size = min(size, total)
    assert total % size == 0
    return [slice(r, r + size) for r in range(0, total, size)]


def _after_specs(after):
    return [pl.BlockSpec(memory_space=pl.ANY)] * len(after)


def _shift_down(v, k):
    row = lax.broadcasted_iota(jnp.int32, v.shape, 0)
    return jnp.where(row >= k, pltpu.roll(v, k, 0), 0.0)


def _shift_up(v, k):
    n = v.shape[0]
    row = lax.broadcasted_iota(jnp.int32, v.shape, 0)
    return jnp.where(row < n - k, pltpu.roll(v, n - k, 0), 0.0)


def _sigmoid(v):
    return jax.nn.sigmoid(v)


def _cast_bf16(w2d, name):
    rows, cols = w2d.shape
    tr = _row_tile(rows, cols * 4, 2 << 20)

    def body(i_ref, o_ref):
        o_ref[...] = i_ref[...].astype(BF16)

    return pl.pallas_call(
        body, name=name, grid=(rows // tr,),
        in_specs=[pl.BlockSpec((tr, cols), lambda i: (i, 0))],
        out_specs=pl.BlockSpec((tr, cols), lambda i: (i, 0)),
        out_shape=jax.ShapeDtypeStruct((rows, cols), BF16),
        compiler_params=_cp(1),
    )(w2d)


def _rms_fwd(x2d, g):
    s, d = x2d.shape
    tm = min(256, s)

    def body(x_ref, g_ref, h_ref):
        xv = x_ref[...]
        r = lax.rsqrt(jnp.mean(xv * xv, axis=-1, keepdims=True) + EPS)
        h_ref[...] = (xv * r * g_ref[...]).astype(BF16)

    return pl.pallas_call(
        body, name="rms1_fwd", grid=(s // tm,),
        in_specs=[pl.BlockSpec((tm, d), lambda i: (i, 0)), pl.BlockSpec((1, d), lambda i: (0, 0))],
        out_specs=pl.BlockSpec((tm, d), lambda i: (i, 0)),
        out_shape=jax.ShapeDtypeStruct((s, d), BF16),
        compiler_params=_cp(1),
    )(x2d, g)


def _coords():
    return lax.axis_index("x"), lax.axis_index("y"), lax.axis_index("c")


def _slot(p):
    return 4 * p[0] + 2 * p[1] + p[2]


def _handshake(peers):
    barrier = pltpu.get_barrier_semaphore()
    for peer in peers:
        pl.semaphore_signal(barrier, inc=1, device_id=peer, device_id_type=MESH)
    pl.semaphore_wait(barrier, len(peers))


def _sequencer_call(body, out_type, scratch_types, name, collective_id):
    return pl.kernel(
        body, out_type=out_type, name=name,
        mesh=plsc.ScalarSubcoreMesh(axis_name="seq", num_cores=1),
        scratch_types=scratch_types,
        cost_estimate=pl.CostEstimate(flops=0, transcendentals=0, bytes_accessed=SEQUENCER_COST_BYTES),
        compiler_params=pltpu.CompilerParams(collective_id=collective_id))


def _allgather_big(shards, name, collective_id, after=()):
    n = len(shards)

    def body(*refs):
        ins, outs = refs[:n], refs[n + len(after):2 * n + len(after)]
        send_sems, recv_sems, local_sems = refs[2 * n + len(after):]
        x, y, c = _coords()
        me, sibling = (x, y, c), (x, y, 1 - c)
        chips = [(1 - x, y), (x, 1 - y), (1 - x, 1 - y)]
        _handshake([sibling] + [(*chip, c) for chip in chips])

        def copy(a, k, block, to, src=None):
            dst = outs[a].at[_slot(block)]
            return pltpu.make_async_remote_copy(
                src_ref=dst if src is None else src, dst_ref=dst,
                send_sem=send_sems.at[a, k], recv_sem=recv_sems.at[a, k],
                device_id=to, device_id_type=MESH)

        mine, first, passed = [], [], []
        for a in range(n):
            cp = pltpu.make_async_copy(ins[a], outs[a].at[_slot(me)], local_sems.at[a])
            cp.start()
            mine.append(cp)
            f = [copy(a, 0, me, sibling, src=ins[a])]
            f += [copy(a, 1 + j, me, (*chip, c), src=ins[a]) for j, chip in enumerate(chips)]
            for cp in f:
                cp.start()
            first += f
        for a in range(n):
            for j, chip in enumerate(chips):
                copy(a, 1 + j, (*chip, c), me).wait_recv()
                cp = copy(a, 4 + j, (*chip, c), sibling)
                cp.start()
                passed.append(cp)
        for a in range(n):
            copy(a, 0, sibling, me).wait_recv()
            for j, chip in enumerate(chips):
                copy(a, 4 + j, (*chip, 1 - c), me).wait_recv()
        for cp in first + passed:
            cp.wait_send()
        for cp in mine:
            cp.wait()

    return _sequencer_call(
        body, [jax.ShapeDtypeStruct((N_DEV,) + s.shape, s.dtype) for s in shards],
        [pltpu.SemaphoreType.DMA((n, 7)), pltpu.SemaphoreType.DMA((n, 7)), pltpu.SemaphoreType.DMA((n,))],
        name, collective_id)(*shards, *after)


def _exchange_sibling(grads, name, collective_id):
    n = len(grads)

    def body(*refs):
        ins, outs = refs[:n], refs[n:2 * n]
        send_sems, recv_sems = refs[2 * n:]
        x, y, c = _coords()
        sibling = (x, y, 1 - c)
        _handshake([sibling])
        copies = []
        for a in range(n):
            for q in range(4):
                cp = pltpu.make_async_remote_copy(
                    src_ref=ins[a].at[2 * q + (1 - c)], dst_ref=outs[a].at[q],
                    send_sem=send_sems.at[a, q], recv_sem=recv_sems.at[a, q],
                    device_id=sibling, device_id_type=MESH)
                cp.start()
                copies.append(cp)
        for cp in copies:
            cp.wait_recv()
        for cp in copies:
            cp.wait_send()

    any_spec = pl.BlockSpec(memory_space=pl.ANY)
    return pl.pallas_call(
        body, name=name,
        in_specs=[any_spec] * n, out_specs=[any_spec] * n,
        out_shape=[jax.ShapeDtypeStruct((4,) + g.shape[1:], g.dtype) for g in grads],
        scratch_shapes=[pltpu.SemaphoreType.DMA((n, 4)), pltpu.SemaphoreType.DMA((n, 4))],
        compiler_params=pltpu.CompilerParams(collective_id=collective_id),
    )(*grads)


def _exchange_chips(psums, name, collective_id):
    n = len(psums)

    def body(*refs):
        ins, outs = refs[:n], refs[n:2 * n]
        send_sems, recv_sems = refs[2 * n:]
        x, y, c = _coords()
        chips = [(1 - x, y), (x, 1 - y), (1 - x, 1 - y)]
        _handshake([(*chip, c) for chip in chips])
        copies = []
        for a in range(n):
            for j, chip in enumerate(chips):
                cp = pltpu.make_async_remote_copy(
                    src_ref=ins[a].at[2 * chip[0] + chip[1]], dst_ref=outs[a].at[j],
                    send_sem=send_sems.at[a, j], recv_sem=recv_sems.at[a, j],
                    device_id=(*chip, c), device_id_type=MESH)
                cp.start()
                copies.append(cp)
        for cp in copies:
            cp.wait_recv()
        for cp in copies:
            cp.wait_send()

    return _sequencer_call(
        body, [jax.ShapeDtypeStruct((3,) + p.shape[1:], p.dtype) for p in psums],
        [pltpu.SemaphoreType.DMA((n, 3)), pltpu.SemaphoreType.DMA((n, 3))],
        name, collective_id)(*psums)


def _allgather_small(v2d, name):
    rows, cols = v2d.shape

    def body(v_ref, out_ref, send_sems, recv_sems):
        x, y, c = _coords()
        me = (x, y, c)
        out_ref[_slot(me)] = v_ref[...]
        peers = []
        for k in range(1, N_DEV):
            fx, fy, fc = (k >> 2) & 1, (k >> 1) & 1, k & 1
            peers.append(((1 - x) if fx else x, (1 - y) if fy else y, (1 - c) if fc else c))
        sends = []
        for k, peer in enumerate(peers):
            cp = pltpu.make_async_remote_copy(
                src_ref=v_ref, dst_ref=out_ref.at[_slot(me)],
                send_sem=send_sems.at[k], recv_sem=recv_sems.at[k],
                device_id=peer, device_id_type=MESH)
            cp.start()
            sends.append(cp)
        for k, peer in enumerate(peers):
            pltpu.make_async_remote_copy(
                src_ref=v_ref, dst_ref=out_ref.at[_slot(peer)],
                send_sem=send_sems.at[k], recv_sem=recv_sems.at[k],
                device_id=peer, device_id_type=MESH).wait_recv()
        for cp in sends:
            cp.wait_send()

    vmem = pl.BlockSpec(memory_space=pltpu.VMEM)
    return pl.pallas_call(
        body, name=name, in_specs=[vmem], out_specs=vmem,
        out_shape=jax.ShapeDtypeStruct((N_DEV, rows, cols), v2d.dtype),
        scratch_shapes=[pltpu.SemaphoreType.DMA((N_DEV - 1,)), pltpu.SemaphoreType.DMA((N_DEV - 1,))],
    )(v2d)


def _chip_partial(slots, g3, recv, name):
    _, rows, cols = g3.shape
    tr = _row_tile(rows, cols * 2, 2 << 20)

    def body(slots_ref, g_ref, r_ref, o_ref):
        o_ref[...] = (g_ref[...].astype(F32) + r_ref[...].astype(F32)).astype(BF16)

    return pl.pallas_call(
        body, name=name,
        grid_spec=pltpu.PrefetchScalarGridSpec(
            num_scalar_prefetch=1, grid=(4, rows // tr),
            in_specs=[pl.BlockSpec((None, tr, cols), lambda q, i, sl: (sl[q], i, 0)),
                      pl.BlockSpec((None, tr, cols), lambda q, i, sl: (q, i, 0))],
            out_specs=pl.BlockSpec((None, tr, cols), lambda q, i, sl: (q, i, 0))),
        out_shape=jax.ShapeDtypeStruct((4, rows, cols), BF16),
        compiler_params=_cp(2),
    )(slots, g3, recv)


def _adam_math(w, g, m, v):
    m = ADAM_B1 * m + (1.0 - ADAM_B1) * g
    v = ADAM_B2 * v + (1.0 - ADAM_B2) * (g * g)
    m_hat = m / (1.0 - ADAM_B1 ** ADAM_STEP)
    v_hat = v / (1.0 - ADAM_B2 ** ADAM_STEP)
    delta = -ADAM_LR * (m_hat / (jnp.sqrt(v_hat) + ADAM_EPS) + ADAM_WD * w)
    return delta, m, v


def _adam_big(own, w, m, v, g3, recv_sib, recv_chips, name):
    rows, cols = w.shape
    tr = _row_tile(rows, cols * 4, 2 << 20)

    def body(own_ref, w_ref, m_ref, v_ref, g_ref, rs_ref, rc_ref, go_ref, do_ref, mo_ref, vo_ref):
        g = g_ref[...].astype(F32) + rs_ref[...].astype(F32)
        g = g + rc_ref[0].astype(F32)
        g = g + rc_ref[1].astype(F32)
        g = g + rc_ref[2].astype(F32)
        delta, m_new, v_new = _adam_math(w_ref[...], g, m_ref[...], v_ref[...])
        go_ref[...] = g
        do_ref[...] = delta
        mo_ref[...] = m_new
        vo_ref[...] = v_new

    blk = pl.BlockSpec((tr, cols), lambda i, o: (i, 0))
    out = jax.ShapeDtypeStruct((rows, cols), F32)
    return pl.pallas_call(
        body, name=name,
        grid_spec=pltpu.PrefetchScalarGridSpec(
            num_scalar_prefetch=1, grid=(rows // tr,),
            in_specs=[blk, blk, blk,
                      pl.BlockSpec((None, tr, cols), lambda i, o: (o[0], i, 0)),
                      pl.BlockSpec((None, tr, cols), lambda i, o: (o[1], i, 0)),
                      pl.BlockSpec((3, tr, cols), lambda i, o: (0, i, 0))],
            out_specs=[blk, blk, blk, blk]),
        out_shape=[out, out, out, out],
        compiler_params=_cp(1),
    )(own, w, m, v, g3, recv_sib, recv_chips)


def _sum_small(gathered):
    _, rows, cols = gathered.shape

    def body(g_ref, o_ref):
        acc = g_ref[0]
        for k in range(1, N_DEV):
            acc = acc + g_ref[k]
        o_ref[...] = acc

    vmem = pl.BlockSpec(memory_space=pltpu.VMEM)
    return pl.pallas_call(body, name="small_grad_sum", in_specs=[vmem], out_specs=vmem,
                          out_shape=jax.ShapeDtypeStruct((rows, cols), F32))(gathered)


def _adam_small(w, g, m, v):
    def body(w_ref, g_ref, m_ref, v_ref, do_ref, mo_ref, vo_ref):
        delta, m_new, v_new = _adam_math(w_ref[...], g_ref[...], m_ref[...], v_ref[...])
        do_ref[...] = delta
        mo_ref[...] = m_new
        vo_ref[...] = v_new

    vmem = pl.BlockSpec(memory_space=pltpu.VMEM)
    out = jax.ShapeDtypeStruct(w.shape, F32)
    return pl.pallas_call(body, name="adam_small", in_specs=[vmem] * 4, out_specs=[vmem] * 3,
                          out_shape=[out, out, out])(w, g, m, v)


def _proj_fwd(h, win_g):
    s, d = h.shape
    sw = win_g.shape[2]
    tn = min(512, sw)
    nh = sw // tn

    def body(h_ref, w_ref, o_ref):
        for rs in _chunks(s, 512):
            o_ref[rs, :] = _dot(h_ref[rs, :], w_ref[...], NN)

    return pl.pallas_call(
        body, name="proj_fwd", grid=(N_DEV * nh,),
        in_specs=[pl.BlockSpec((s, d), lambda j: (0, 0)),
                  pl.BlockSpec((None, d, tn), lambda j: (j // nh, 0, j % nh))],
        out_specs=pl.BlockSpec((None, s, tn), lambda j: (j // nh, 0, j % nh)),
        out_shape=jax.ShapeDtypeStruct((N_DEV, s, sw), F32),
        compiler_params=_cp(1),
    )(h, win_g)


def _conv_fwd(proj, conv_w, conv_b):
    _, s, sw = proj.shape
    tc = min(LANES, sw)

    def body(ba_ref, ca_ref, va_ref, cw_ref, cb_ref, z_ref):
        cv = ca_ref[...] * va_ref[...]
        u = (cb_ref[...] + cw_ref[0:1, :] * _shift_down(cv, 2) + cw_ref[1:2, :] * _shift_down(cv, 1)
             + cw_ref[2:3, :] * cv)
        z_ref[...] = (ba_ref[...] * u).astype(BF16)

    def part(k):
        return pl.BlockSpec((None, s, tc), lambda i: (k, 0, i))

    return pl.pallas_call(
        body, name="conv_fwd", grid=(sw // tc,),
        in_specs=[part(0), part(1), part(2),
                  pl.BlockSpec((CONV_K, tc), lambda i: (0, i)), pl.BlockSpec((1, tc), lambda i: (0, i))],
        out_specs=pl.BlockSpec((s, tc), lambda i: (0, i)),
        out_shape=jax.ShapeDtypeStruct((s, sw), BF16),
        compiler_params=_cp(1),
    )(proj, proj, proj, conv_w, conv_b)


def _pool_counts(shape, window):
    t = lax.broadcasted_iota(jnp.int32, shape, 0)
    return jnp.minimum(t + 1, window).astype(F32)


def _pool_fwd(proj):
    _, s, sw = proj.shape
    gw = sw // len(POOL_WINDOWS)

    def body(v_ref, p_ref):
        for gi, window in enumerate(POOL_WINDOWS):
            @pl.when(pl.program_id(0) == gi)
            def _():
                v = v_ref[...]
                acc, k = v, 1
                while k < window:
                    acc = acc + _shift_down(acc, k)
                    k *= 2
                p_ref[...] = (acc / _pool_counts(v.shape, window) - v).astype(BF16)

    return pl.pallas_call(
        body, name="pool_fwd", grid=(len(POOL_WINDOWS),),
        in_specs=[pl.BlockSpec((None, s, gw), lambda g: (3, 0, g))],
        out_specs=pl.BlockSpec((s, gw), lambda g: (0, g)),
        out_shape=jax.ShapeDtypeStruct((s, sw), BF16),
        compiler_params=_cp(1),
    )(proj)


def _merge_fwd(z, wa, p, wpool, proj, b_gate2, pool_scale):
    s, sw = z.shape
    d = wa.shape[1]
    tn = d // N_DEV
    gw = sw // len(POOL_WINDOWS)
    nq = sw // tn

    def body(z_ref, wa_ref, p_ref, wp_ref, ga_ref, gb_ref, bg_ref, sc_ref, ya_ref, yb_ref, m_ref):
        for rs in _chunks(s, 512):
            ya = _dot(z_ref[rs, :], wa_ref[...], NN)
            yb = _dot(p_ref[rs, :], wp_ref[...], NN)
            sa = _sigmoid(ga_ref[rs, :] + bg_ref[0:1, :])
            sb = _sigmoid(gb_ref[rs, :] + bg_ref[1:2, :])
            ya_ref[rs, :] = ya.astype(BF16)
            yb_ref[rs, :] = yb.astype(BF16)
            m_ref[rs, :] = (sa * ya + sb * (yb * sc_ref[...])).astype(BF16)

    col = pl.BlockSpec((s, tn), lambda j: (0, j))
    out = jax.ShapeDtypeStruct((s, d), BF16)
    return pl.pallas_call(
        body, name="merge_fwd", grid=(N_DEV,),
        in_specs=[pl.BlockSpec((s, sw), lambda j: (0, 0)),
                  pl.BlockSpec((sw, tn), lambda j: (0, j)),
                  pl.BlockSpec((s, gw), lambda j: (0, j // 2)),
                  pl.BlockSpec((None, gw, tn), lambda j: (j // 2, 0, j % 2)),
                  pl.BlockSpec((None, s, tn), lambda j: (4 + j // nq, 0, j % nq)),
                  pl.BlockSpec((None, s, tn), lambda j: (6 + j // nq, 0, j % nq)),
                  pl.BlockSpec((2, tn), lambda j: (0, j)),
                  pl.BlockSpec((1, tn), lambda j: (0, j))],
        out_specs=[col, col, col],
        out_shape=[out, out, out],
        compiler_params=_cp(1),
    )(z, wa, p, wpool, proj, proj, b_gate2, pool_scale)


def _wo_fwd(merged, wo, x2d, g2):
    s, d = x2d.shape
    tm = min(256, s)

    def body(m_ref, wo_ref, x_ref, g_ref, x1_ref, h2_ref):
        x1 = x_ref[...] + _dot(m_ref[...], wo_ref[...], NN)
        x1_ref[...] = x1
        r = lax.rsqrt(jnp.mean(x1 * x1, axis=-1, keepdims=True) + EPS)
        h2_ref[...] = (x1 * r * g_ref[...]).astype(BF16)

    row = pl.BlockSpec((tm, d), lambda i: (i, 0))
    return pl.pallas_call(
        body, name="wo_fwd", grid=(s // tm,),
        in_specs=[row, pl.BlockSpec((d, d), lambda i: (0, 0)), row, pl.BlockSpec((1, d), lambda i: (0, 0))],
        out_specs=[row, row],
        out_shape=[jax.ShapeDtypeStruct((s, d), F32), jax.ShapeDtypeStruct((s, d), BF16)],
        compiler_params=_cp(1),
    )(merged, wo, x2d, g2)


def _ffn_up_fwd(h2, wg_g, wu_g):
    s, d = h2.shape
    f8 = wg_g.shape[2]

    def body(h_ref, wg_ref, wu_ref, g_ref, u_ref):
        for rs in _chunks(s, 512):
            a = h_ref[rs, :]
            g_ref[rs, :] = _dot(a, wg_ref[...], NN).astype(BF16)
            u_ref[rs, :] = _dot(a, wu_ref[...], NN).astype(BF16)

    wspec = pl.BlockSpec((None, d, f8), lambda j: (j, 0, 0))
    ospec = pl.BlockSpec((None, s, f8), lambda j: (j, 0, 0))
    out = jax.ShapeDtypeStruct((N_DEV, s, f8), BF16)
    return pl.pallas_call(
        body, name="ffn_up_fwd", grid=(N_DEV,),
        in_specs=[pl.BlockSpec((s, d), lambda j: (0, 0)), wspec, wspec],
        out_specs=[ospec, ospec], out_shape=[out, out],
        compiler_params=_cp(1),
    )(h2, wg_g, wu_g)


def _ffn_down_loss(gact, uact, wd_g, x1, target, final_g):
    _, s, f8 = gact.shape
    d = x1.shape[1]
    tm = min(256, s)
    last = N_DEV - 1

    def body(g_ref, u_ref, wd_ref, x1_ref, t_ref, gf_ref, dx_ref, dxb_ref, dgf_ref, loss_ref, acc_ref):
        i, j = pl.program_id(0), pl.program_id(1)

        @pl.when(j == 0)
        def _():
            acc_ref[...] = jnp.zeros_like(acc_ref)

        @pl.when((i == 0) & (j == 0))
        def _():
            dgf_ref[...] = jnp.zeros_like(dgf_ref)
            loss_ref[...] = jnp.zeros_like(loss_ref)

        for rs in _chunks(tm, 256):
            g = g_ref[rs, :].astype(F32)
            act = (g * _sigmoid(g) * u_ref[rs, :].astype(F32)).astype(BF16)
            acc_ref[rs, :] += _dot(act, wd_ref[...], NN)

        @pl.when(j == last)
        def _():
            for rs in _chunks(tm, 256):
                x2 = x1_ref[rs, :] + acc_ref[rs, :]
                r = lax.rsqrt(jnp.mean(x2 * x2, axis=-1, keepdims=True) + EPS)
                nrm = x2 * r
                gf = gf_ref[...]
                err = nrm * gf - t_ref[rs, :]
                loss_ref[...] += jnp.sum(err * err) * (0.5 / d)
                dy = err * (1.0 / d)
                dgf_ref[...] += jnp.sum(dy * nrm, axis=0, keepdims=True)
                dn = dy * gf
                dx = r * (dn - nrm * jnp.mean(dn * nrm, axis=-1, keepdims=True))
                dx_ref[rs, :] = dx
                dxb_ref[rs, :] = dx.astype(BF16)

    aspec = pl.BlockSpec((None, tm, f8), lambda i, j: (j, i, 0))
    row = pl.BlockSpec((tm, d), lambda i, j: (i, 0))
    return pl.pallas_call(
        body, name="ffn_down_loss", grid=(s // tm, N_DEV),
        in_specs=[aspec, aspec, pl.BlockSpec((None, f8, d), lambda i, j: (j, 0, 0)), row, row,
                  pl.BlockSpec((1, d), lambda i, j: (0, 0))],
        out_specs=[row, row, pl.BlockSpec((1, d), lambda i, j: (0, 0)),
                   pl.BlockSpec((8, LANES), lambda i, j: (0, 0))],
        out_shape=[jax.ShapeDtypeStruct((s, d), F32), jax.ShapeDtypeStruct((s, d), BF16),
                   jax.ShapeDtypeStruct((1, d), F32), jax.ShapeDtypeStruct((8, LANES), F32)],
        scratch_shapes=[pltpu.VMEM((tm, d), F32)],
        compiler_params=_cp(2),
    )(gact, uact, wd_g, x1, target, final_g)


def _ffn_act_bwd(dx2b, wd_g, gact, uact):
    s, d = dx2b.shape
    f8 = gact.shape[2]
    tm = min(1024, s)

    def body(dx_ref, wd_ref, g_ref, u_ref, dg_ref, du_ref, act_ref):
        for rs in _chunks(tm, 256):
            da = _dot(dx_ref[rs, :], wd_ref[...], NT)
            g = g_ref[rs, :].astype(F32)
            u = u_ref[rs, :].astype(F32)
            sg = _sigmoid(g)
            silu = g * sg
            act_ref[rs, :] = (silu * u).astype(BF16)
            du_ref[rs, :] = (da * silu).astype(BF16)
            dg_ref[rs, :] = (da * u * (sg * (1.0 + g * (1.0 - sg)))).astype(BF16)

    aspec = pl.BlockSpec((None, tm, f8), lambda j, i: (j, i, 0))
    out = jax.ShapeDtypeStruct((N_DEV, s, f8), BF16)
    return pl.pallas_call(
        body, name="ffn_act_bwd", grid=(N_DEV, s // tm),
        in_specs=[pl.BlockSpec((tm, d), lambda j, i: (i, 0)),
                  pl.BlockSpec((None, f8, d), lambda j, i: (j, 0, 0)), aspec, aspec],
        out_specs=[aspec, aspec, aspec], out_shape=[out, out, out],
        compiler_params=_cp(2),
    )(dx2b, wd_g, gact, uact)


def _wgrad_shard_a(a3, b, name, after=()):
    _, s, k = a3.shape
    n = b.shape[1]
    ts = min(512, s)
    ns = s // ts

    def body(a_ref, b_ref, *rest):
        o_ref, acc_ref = rest[len(after):]
        i = pl.program_id(1)

        @pl.when(i == 0)
        def _():
            acc_ref[...] = jnp.zeros_like(acc_ref)

        acc_ref[...] += _dot(a_ref[...], b_ref[...], TN)

        @pl.when(i == ns - 1)
        def _():
            o_ref[...] = acc_ref[...].astype(BF16)

    return pl.pallas_call(
        body, name=name, grid=(N_DEV, ns),
        in_specs=[pl.BlockSpec((None, ts, k), lambda j, i: (j, i, 0)),
                  pl.BlockSpec((ts, n), lambda j, i: (i, 0))] + _after_specs(after),
        out_specs=pl.BlockSpec((None, k, n), lambda j, i: (j, 0, 0)),
        out_shape=jax.ShapeDtypeStruct((N_DEV, k, n), BF16),
        scratch_shapes=[pltpu.VMEM((k, n), F32)],
        compiler_params=_cp(2),
    )(a3, b, *after)


def _wgrad_shard_b(a, b3, name, after=()):
    s, k = a.shape
    n = b3.shape[2]
    ts = min(512, s)
    ns = s // ts

    def body(a_ref, b_ref, *rest):
        o_ref, acc_ref = rest[len(after):]
        i = pl.program_id(1)

        @pl.when(i == 0)
        def _():
            acc_ref[...] = jnp.zeros_like(acc_ref)

        acc_ref[...] += _dot(a_ref[...], b_ref[...], TN)

        @pl.when(i == ns - 1)
        def _():
            o_ref[...] = acc_ref[...].astype(BF16)

    return pl.pallas_call(
        body, name=name, grid=(N_DEV, ns),
        in_specs=[pl.BlockSpec((ts, k), lambda j, i: (i, 0)),
                  pl.BlockSpec((None, ts, n), lambda j, i: (j, i, 0))] + _after_specs(after),
        out_specs=pl.BlockSpec((None, k, n), lambda j, i: (j, 0, 0)),
        out_shape=jax.ShapeDtypeStruct((N_DEV, k, n), BF16),
        scratch_shapes=[pltpu.VMEM((k, n), F32)],
        compiler_params=_cp(2),
    )(a, b3, *after)


def _wgrad_full(a, b, name):
    s, k = a.shape
    n = b.shape[1]
    tk = min(512, k)
    ts = min(512, s)
    ns = s // ts

    def body(a_ref, b_ref, o_ref, acc_ref):
        i = pl.program_id(1)

        @pl.when(i == 0)
        def _():
            acc_ref[...] = jnp.zeros_like(acc_ref)

        acc_ref[...] += _dot(a_ref[...], b_ref[...], TN)

        @pl.when(i == ns - 1)
        def _():
            o_ref[...] = acc_ref[...].astype(BF16)

    return pl.pallas_call(
        body, name=name, grid=(k // tk, ns),
        in_specs=[pl.BlockSpec((ts, tk), lambda j, i: (i, j)),
                  pl.BlockSpec((ts, n), lambda j, i: (i, 0))],
        out_specs=pl.BlockSpec((tk, n), lambda j, i: (j, 0)),
        out_shape=jax.ShapeDtypeStruct((k, n), BF16),
        scratch_shapes=[pltpu.VMEM((tk, n), F32)],
        compiler_params=_cp(2),
    )(a, b)


def _wgrad_pool(p, dyb, n_groups):
    s, sw = p.shape
    d = dyb.shape[1]
    gw, go = sw // n_groups, d // n_groups
    ts = min(512, s)
    ns = s // ts

    def body(a_ref, b_ref, o_ref, acc_ref):
        i = pl.program_id(1)

        @pl.when(i == 0)
        def _():
            acc_ref[...] = jnp.zeros_like(acc_ref)

        acc_ref[...] += _dot(a_ref[...], b_ref[...], TN)

        @pl.when(i == ns - 1)
        def _():
            o_ref[...] = acc_ref[...].astype(BF16)

    return pl.pallas_call(
        body, name="wgrad_pool", grid=(n_groups, ns),
        in_specs=[pl.BlockSpec((ts, gw), lambda g, i: (i, g)),
                  pl.BlockSpec((ts, go), lambda g, i: (i, g))],
        out_specs=pl.BlockSpec((None, gw, go), lambda g, i: (g, 0, 0)),
        out_shape=jax.ShapeDtypeStruct((n_groups, gw, go), BF16),
        scratch_shapes=[pltpu.VMEM((gw, go), F32)],
        compiler_params=_cp(2),
    )(p, dyb)


def _input_grad_rms(pairs, xres, g, dres, name, after=()):
    s, d = xres.shape
    tm = min(256, s)
    last = N_DEV - 1
    npair = len(pairs)

    def body(*refs):
        ops = refs[:2 * npair]
        x_ref, g_ref, dres_ref = refs[2 * npair:2 * npair + 3]
        dx_ref, dxb_ref, dg_ref, acc_ref = refs[2 * npair + 3 + len(after):]
        i, j = pl.program_id(0), pl.program_id(1)

        @pl.when(j == 0)
        def _():
            acc_ref[...] = jnp.zeros_like(acc_ref)

        @pl.when((i == 0) & (j == 0))
        def _():
            dg_ref[...] = jnp.zeros_like(dg_ref)

        for rs in _chunks(tm, 256):
            part = _dot(ops[0][rs, :], ops[1][...], NT)
            for q in range(1, npair):
                part = part + _dot(ops[2 * q][rs, :], ops[2 * q + 1][...], NT)
            acc_ref[rs, :] += part

        @pl.when(j == last)
        def _():
            for rs in _chunks(tm, 256):
                xv = x_ref[rs, :]
                dh = acc_ref[rs, :]
                r = lax.rsqrt(jnp.mean(xv * xv, axis=-1, keepdims=True) + EPS)
                nrm = xv * r
                dg_ref[...] += jnp.sum(dh * nrm, axis=0, keepdims=True)
                dn = dh * g_ref[...]
                dx = dres_ref[rs, :] + r * (dn - nrm * jnp.mean(dn * nrm, axis=-1, keepdims=True))
                dx_ref[rs, :] = dx
                dxb_ref[rs, :] = dx.astype(BF16)

    in_specs, args = [], []
    for a3, w3 in pairs:
        k = a3.shape[2]
        in_specs += [pl.BlockSpec((None, tm, k), lambda i, j: (j, i, 0)),
                     pl.BlockSpec((None, d, k), lambda i, j: (j, 0, 0))]
        args += [a3, w3]
    row = pl.BlockSpec((tm, d), lambda i, j: (i, 0))
    vec = pl.BlockSpec((1, d), lambda i, j: (0, 0))
    return pl.pallas_call(
        body, name=name, grid=(s // tm, N_DEV),
        in_specs=in_specs + [row, vec, row] + _after_specs(after),
        out_specs=[row, row, vec],
        out_shape=[jax.ShapeDtypeStruct((s, d), F32), jax.ShapeDtypeStruct((s, d), BF16),
                   jax.ShapeDtypeStruct((1, d), F32)],
        scratch_shapes=[pltpu.VMEM((tm, d), F32)],
        compiler_params=_cp(2),
    )(*args, xres, g, dres, *after)


def _wo_bwd(dx1b, wo, ya, yb, proj, b_gate2, pool_scale, after=()):
    s, d = dx1b.shape
    sw = proj.shape[2]
    tn = d // N_DEV
    nq = sw // tn

    def body(dx_ref, wo_ref, ya_ref, yb_ref, ga_ref, gb_ref, bg_ref, sc_ref, *rest):
        dya_ref, dyb_ref, dp_ref, dbg_ref, dsc_ref = rest[len(after):]
        dbg_ref[...] = jnp.zeros_like(dbg_ref)
        dsc_ref[...] = jnp.zeros_like(dsc_ref)
        for rs in _chunks(s, 256):
            dm = _dot(dx_ref[rs, :], wo_ref[...], NT)
            ya_v = ya_ref[rs, :].astype(F32)
            yb_v = yb_ref[rs, :].astype(F32)
            sa = _sigmoid(ga_ref[rs, :] + bg_ref[0:1, :])
            sb = _sigmoid(gb_ref[rs, :] + bg_ref[1:2, :])
            sc = sc_ref[...]
            dya_ref[rs, :] = (dm * sa).astype(BF16)
            dsb = dm * sb
            dyb_ref[rs, :] = (dsb * sc).astype(BF16)
            dsc_ref[...] += jnp.sum(dsb * yb_v, axis=0, keepdims=True)
            dga = dm * ya_v * (sa * (1.0 - sa))
            dgb = dm * (yb_v * sc) * (sb * (1.0 - sb))
            dp_ref[0, rs, :] = dga.astype(BF16)
            dp_ref[1, rs, :] = dgb.astype(BF16)
            dbg_ref[0:1, :] += jnp.sum(dga, axis=0, keepdims=True)
            dbg_ref[1:2, :] += jnp.sum(dgb, axis=0, keepdims=True)

    col = pl.BlockSpec((s, tn), lambda j: (0, j))
    out = jax.ShapeDtypeStruct((s, d), BF16)
    return pl.pallas_call(
        body, name="wo_bwd", grid=(N_DEV,),
        in_specs=[pl.BlockSpec((s, d), lambda j: (0, 0)),
                  pl.BlockSpec((tn, d), lambda j: (j, 0)), col, col,
                  pl.BlockSpec((None, s, tn), lambda j: (4 + j // nq, 0, j % nq)),
                  pl.BlockSpec((None, s, tn), lambda j: (6 + j // nq, 0, j % nq)),
                  pl.BlockSpec((2, tn), lambda j: (0, j)),
                  pl.BlockSpec((1, tn), lambda j: (0, j))] + _after_specs(after),
        out_specs=[col, col,
                   pl.BlockSpec((2, None, s, tn), lambda j: (1, j // nq, 0, j % nq)),
                   pl.BlockSpec((2, tn), lambda j: (0, j)),
                   pl.BlockSpec((1, tn), lambda j: (0, j))],
        out_shape=[out, out, jax.ShapeDtypeStruct((4, 2, s, sw), BF16),
                   jax.ShapeDtypeStruct((2, d), F32), jax.ShapeDtypeStruct((1, d), F32)],
        compiler_params=_cp(1),
    )(dx1b, wo, ya, yb, proj, proj, b_gate2, pool_scale, *after)


def _conv_bwd(dproj, dya, wa, proj, conv_w, conv_b):
    s, d = dya.shape
    sw = wa.shape[0]
    tc = min(LANES, sw)

    def body(dproj_hbm, dya_ref, wa_ref, ba_ref, ca_ref, va_ref, cw_ref, cb_ref,
             dp_ref, dcw_ref, dcb_ref, dz_ref):
        del dproj_hbm
        for rs in _chunks(s, 512):
            dz_ref[rs, :] = _dot(dya_ref[rs, :], wa_ref[...], NT)
        dz = dz_ref[...]
        ba, ca, va = ba_ref[...], ca_ref[...], va_ref[...]
        cv = ca * va
        cv1, cv2 = _shift_down(cv, 1), _shift_down(cv, 2)
        w0, w1, w2 = cw_ref[0:1, :], cw_ref[1:2, :], cw_ref[2:3, :]
        u = cb_ref[...] + w0 * cv2 + w1 * cv1 + w2 * cv
        du = dz * ba
        dp_ref[0] = (dz * u).astype(BF16)
        dcv = w2 * du + w1 * _shift_up(du, 1) + w0 * _shift_up(du, 2)
        dp_ref[1] = (dcv * va).astype(BF16)
        dp_ref[2] = (dcv * ca).astype(BF16)
        dcw_ref[0:1, :] = jnp.sum(du * cv2, axis=0, keepdims=True)
        dcw_ref[1:2, :] = jnp.sum(du * cv1, axis=0, keepdims=True)
        dcw_ref[2:3, :] = jnp.sum(du * cv, axis=0, keepdims=True)
        dcb_ref[...] = jnp.sum(du, axis=0, keepdims=True)

    def part(k):
        return pl.BlockSpec((None, s, tc), lambda i: (k, 0, i))

    return pl.pallas_call(
        body, name="conv_bwd", grid=(sw // tc,),
        in_specs=[pl.BlockSpec(memory_space=pl.ANY),
                  pl.BlockSpec((s, d), lambda i: (0, 0)),
                  pl.BlockSpec((tc, d), lambda i: (i, 0)),
                  part(0), part(1), part(2),
                  pl.BlockSpec((CONV_K, tc), lambda i: (0, i)), pl.BlockSpec((1, tc), lambda i: (0, i))],
        out_specs=[pl.BlockSpec((3, s, tc), lambda i: (0, 0, i)),
                   pl.BlockSpec((CONV_K, tc), lambda i: (0, i)), pl.BlockSpec((1, tc), lambda i: (0, i))],
        out_shape=[jax.ShapeDtypeStruct(dproj.shape, BF16),
                   jax.ShapeDtypeStruct((CONV_K, sw), F32), jax.ShapeDtypeStruct((1, sw), F32)],
        scratch_shapes=[pltpu.VMEM((s, tc), F32)],
        input_output_aliases={0: 0},
        compiler_params=_cp(1),
    )(dproj, dya, wa, proj, proj, proj, conv_w, conv_b)


def _pool_bwd(dproj, dyb, wpool):
    s, d = dyb.shape
    n_groups, gw, go = wpool.shape

    def body(dproj_hbm, dyb_ref, wp_ref, dp_ref):
        del dproj_hbm
        for gi, window in enumerate(POOL_WINDOWS):
            @pl.when(pl.program_id(0) == gi)
            def _():
                dpool = _dot(dyb_ref[...], wp_ref[...], NT)
                acc, k = dpool / _pool_counts(dpool.shape, window), 1
                while k < window:
                    acc = acc + _shift_up(acc, k)
                    k *= 2
                dp_ref[...] = (acc - dpool).astype(BF16)

    return pl.pallas_call(
        body, name="pool_bwd", grid=(n_groups,),
        in_specs=[pl.BlockSpec(memory_space=pl.ANY),
                  pl.BlockSpec((s, go), lambda g: (0, g)),
                  pl.BlockSpec((None, gw, go), lambda g: (g, 0, 0))],
        out_specs=pl.BlockSpec((None, s, gw), lambda g: (3, 0, g)),
        out_shape=jax.ShapeDtypeStruct(dproj.shape, BF16),
        input_output_aliases={0: 0},
        compiler_params=_cp(1),
    )(dproj, dyb, wpool)


def _rows128(v):
    return v.reshape(-1, LANES)


def kernel(x, norm1_g, w_in, b_gate, conv_w, conv_b, w_a_out, w_pool, pool_scale, w_o, norm2_g, w_ffn_gate, w_ffn_up, w_ffn_down, final_g, loss_target, m_norm1_g, m_w_in, m_b_gate, m_conv_w, m_conv_b, m_w_a_out, m_w_pool, m_pool_scale, m_w_o, m_norm2_g, m_w_ffn_gate, m_w_ffn_up, m_w_ffn_down, m_final_g, v_norm1_g, v_w_in, v_b_gate, v_conv_w, v_conv_b, v_w_a_out, v_w_pool, v_pool_scale, v_w_o, v_norm2_g, v_w_ffn_gate, v_w_ffn_up, v_w_ffn_down, v_final_g):
    s, d = x.shape[1], x.shape[2]
    sw = w_in.shape[2]
    n_groups = w_pool.shape[1]
    gw = w_pool.shape[2]
    go = w_pool.shape[3] * N_DEV
    f8 = w_ffn_gate.shape[2]
    cws = conv_w.shape[2]
    assert sw == conv_w.shape[2] * N_DEV == gw * n_groups and go * n_groups == d and n_groups == len(POOL_WINDOWS)

    xi, yi, ci = _coords()
    me = 4 * xi + 2 * yi + ci
    my_chip = 2 * xi + yi

    x2d = x.reshape(s, d)
    target = loss_target.reshape(s, d)
    final_g2 = final_g.reshape(1, d)
    b_gate2 = b_gate.reshape(2, d)

    big_names = ["w_in", "w_a_out", "w_pool", "w_o", "w_ffn_gate", "w_ffn_up", "w_ffn_down"]
    big_w = [w_in, w_a_out, w_pool, w_o, w_ffn_gate, w_ffn_up, w_ffn_down]
    big_m = [m_w_in, m_w_a_out, m_w_pool, m_w_o, m_w_ffn_gate, m_w_ffn_up, m_w_ffn_down]
    big_v = [v_w_in, v_w_a_out, v_w_pool, v_w_o, v_w_ffn_gate, v_w_ffn_up, v_w_ffn_down]
    shapes2d = [(w.size // w.shape[-1], w.shape[-1]) for w in big_w]
    big_w2 = [w.reshape(sh) for w, sh in zip(big_w, shapes2d)]

    sb = [_cast_bf16(w, "cast_" + nm) for w, nm in zip(big_w2, big_names)]
    win_g, wa_g, wpool_g, wo_g = _allgather_big(sb[0:4], "allgather_mixer", COLLECTIVE_GATHER)
    wg_g, wu_g = _allgather_big(sb[4:6], "allgather_ffn_up", COLLECTIVE_GATHER)
    (wd_g,) = _allgather_big(sb[6:7], "allgather_ffn_down", COLLECTIVE_GATHER)
    convw_g = _allgather_small(jnp.pad(conv_w.reshape(CONV_K, cws), ((0, 8 - CONV_K), (0, 0))), "allgather_conv_w")
    conv_w_full = convw_g[:, :CONV_K, :].transpose(1, 0, 2).reshape(CONV_K, sw)
    wa = wa_g.transpose(1, 0, 2).reshape(sw, d)
    wpool = wpool_g.reshape(N_DEV, n_groups, gw, go // N_DEV).transpose(1, 2, 0, 3).reshape(n_groups, gw, go)
    wo = wo_g.reshape(d, d)

    h = _rms_fwd(x2d, norm1_g)
    proj = _proj_fwd(h, win_g)
    z = _conv_fwd(proj, conv_w_full, conv_b)
    p = _pool_fwd(proj)
    ya, yb, merged = _merge_fwd(z, wa, p, wpool, proj, b_gate2, pool_scale)
    x1, h2 = _wo_fwd(merged, wo, x2d, norm2_g)
    gact, uact = _ffn_up_fwd(h2, wg_g, wu_g)
    dx2, dx2b, d_final_g, loss_blk = _ffn_down_loss(gact, uact, wd_g, x1, target, final_g2)

    chip_slots = (2 * jnp.arange(4, dtype=jnp.int32) + ci).astype(jnp.int32)

    def partials(grads, recvs, names):
        return [_chip_partial(chip_slots, g3, r, "chip_partial_" + nm) for g3, r, nm in zip(grads, recvs, names)]

    own = jnp.stack([me, my_chip]).astype(jnp.int32)

    def adam(a, g3, sib, chips):
        outs = _adam_big(own, big_w2[a], big_m[a].reshape(shapes2d[a]), big_v[a].reshape(shapes2d[a]),
                         g3, sib, chips, "adam_" + big_names[a])
        return [o.reshape(big_w[a].shape) for o in outs]

    big_out = [None] * len(big_names)
    dg_act, du_act, act = _ffn_act_bwd(dx2b, wd_g, gact, uact)
    gw_gate = _wgrad_shard_b(h2, dg_act, "wgrad_ffn_gate")
    gw_up = _wgrad_shard_b(h2, du_act, "wgrad_ffn_up")
    sib_gu = _exchange_sibling([gw_gate, gw_up], "rs_sibling_ffn_up", COLLECTIVE_SIBLING)
    ps_gu = partials([gw_gate, gw_up], sib_gu, ["w_ffn_gate", "w_ffn_up"])
    chips_gu = _exchange_chips(ps_gu, "rs_chips_ffn_up", COLLECTIVE_CHIPS)
    gw_down = _wgrad_shard_a(act, dx2b, "wgrad_ffn_down", after=ps_gu)
    sib_down = _exchange_sibling([gw_down], "rs_sibling_ffn_down", COLLECTIVE_SIBLING)
    ps_down = partials([gw_down], sib_down, ["w_ffn_down"])
    chips_down = _exchange_chips(ps_down, "rs_chips_ffn_down", COLLECTIVE_CHIPS)
    dx1, dx1b, d_norm2_g = _input_grad_rms([(dg_act, wg_g), (du_act, wu_g)], x1, norm2_g, dx2, "ffn_in_bwd",
                                           after=ps_down)
    dya, dyb, dproj42, d_b_gate, d_pool_scale = _wo_bwd(dx1b, wo, ya, yb, proj, b_gate2, pool_scale)
    gw_o = _wgrad_full(merged, dx1b, "wgrad_o")
    dproj = dproj42.reshape(N_DEV, s, sw)
    dproj, d_conv_w, d_conv_b = _conv_bwd(dproj, dya, wa, proj, conv_w_full, conv_b)
    dproj = _pool_bwd(dproj, dyb, wpool)
    gw_a = _wgrad_full(z, dya, "wgrad_a_out")
    gw_pool = _wgrad_pool(p, dyb, n_groups)
    mix3 = [gw_a.reshape(sw, N_DEV, d // N_DEV).transpose(1, 0, 2),
            gw_pool.reshape(n_groups, gw, N_DEV, go // N_DEV).transpose(2, 0, 1, 3).reshape(N_DEV, n_groups * gw, go // N_DEV),
            gw_o.reshape(N_DEV, d // N_DEV, d)]
    sib_mix = _exchange_sibling(mix3, "rs_sibling_mixer", COLLECTIVE_SIBLING)
    ps_mix = partials(mix3, sib_mix, ["w_a_out", "w_pool", "w_o"])
    chips_mix = _exchange_chips(ps_mix, "rs_chips_mixer", COLLECTIVE_CHIPS)
    big_out[4] = adam(4, gw_gate, sib_gu[0], chips_gu[0])
    big_out[5] = adam(5, gw_up, sib_gu[1], chips_gu[1])
    gw_in = _wgrad_shard_b(h, dproj, "wgrad_in", after=ps_mix + [big_out[4][0], big_out[5][0]])
    sib_in = _exchange_sibling([gw_in], "rs_sibling_w_in", COLLECTIVE_SIBLING)
    big_out[6] = adam(6, gw_down, sib_down[0], chips_down[0])
    ps_in = partials([gw_in], sib_in, ["w_in"])
    chips_in = _exchange_chips(ps_in, "rs_chips_w_in", COLLECTIVE_CHIPS)
    grad_x, _, d_norm1_g = _input_grad_rms([(dproj, win_g)], x2d, norm1_g, dx1, "proj_in_bwd",
                                           after=ps_mix + ps_in + [big_out[6][0]])
    for k in range(3):
        big_out[1 + k] = adam(1 + k, mix3[k], sib_mix[k], chips_mix[k])
    big_out[0] = adam(0, gw_in, sib_in[0], chips_in[0])

    small_parts = [d_norm1_g, d_b_gate, d_conv_w, d_conv_b, d_pool_scale, d_norm2_g, d_final_g, loss_blk]
    sizes = [v.size for v in small_parts]
    packed = jnp.concatenate([_rows128(v) for v in small_parts], axis=0)
    summed = _sum_small(_allgather_small(packed, "allgather_small_grads")).reshape(-1)
    offs = [0]
    for n in sizes:
        offs.append(offs[-1] + n)
    g_norm1, g_bgate, g_convw_full, g_convb, g_pscale, g_norm2, g_final, loss_sum = [
        summed[offs[k]:offs[k + 1]] for k in range(len(sizes))]
    loss = loss_sum[0]
    g_convw = lax.dynamic_slice(g_convw_full.reshape(CONV_K, sw), (0, me * cws), (CONV_K, cws))
    small_w = [norm1_g, b_gate, conv_w, conv_b, pool_scale, norm2_g, final_g]
    small_m = [m_norm1_g, m_b_gate, m_conv_w, m_conv_b, m_pool_scale, m_norm2_g, m_final_g]
    small_v = [v_norm1_g, v_b_gate, v_conv_w, v_conv_b, v_pool_scale, v_norm2_g, v_final_g]
    small_g = [g_norm1, g_bgate, g_convw, g_convb, g_pscale, g_norm2, g_final]

    def pack(parts):
        flat = jnp.concatenate([v.reshape(-1) for v in parts])
        pad = (-flat.size) % (8 * LANES)
        return jnp.pad(flat, (0, pad)).reshape(-1, LANES)

    s_delta, s_m, s_v = _adam_small(pack(small_w), pack(small_g), pack(small_m), pack(small_v))
    soffs = [0]
    for w in small_w:
        soffs.append(soffs[-1] + w.size)

    def unpack(buf):
        flat = buf.reshape(-1)
        return [flat[soffs[k]:soffs[k + 1]].reshape(small_w[k].shape) for k in range(len(small_w))]

    small_grads = [g.reshape(w.shape) for g, w in zip(small_g, small_w)]
    small_delta, small_new_m, small_new_v = unpack(s_delta), unpack(s_m), unpack(s_v)

    order = ["norm1_g", "w_in", "b_gate", "conv_w", "conv_b", "w_a_out", "w_pool", "pool_scale", "w_o", "norm2_g",
             "w_ffn_gate", "w_ffn_up", "w_ffn_down", "final_g"]
    small_names = ["norm1_g", "b_gate", "conv_w", "conv_b", "pool_scale", "norm2_g", "final_g"]
    per_kind = [{}, {}, {}, {}]
    for a, nm in enumerate(big_names):
        for kind in range(4):
            per_kind[kind][nm] = big_out[a][kind]
    for k, nm in enumerate(small_names):
        per_kind[0][nm] = small_grads[k]
        per_kind[1][nm] = small_delta[k]
        per_kind[2][nm] = small_new_m[k]
        per_kind[3][nm] = small_new_v[k]
    result = [loss, grad_x.reshape(x.shape)]
    for kind in range(4):
        result += [per_kind[kind][nm] for nm in order]
    return tuple(result)
```

```python
import functools

import jax
import jax.numpy as jnp
from jax import lax
from jax.experimental import pallas as pl
from jax.experimental.pallas import tpu as pltpu
from jax.experimental.pallas import tpu_sc as plsc

F32 = jnp.float32
BF16 = jnp.bfloat16
MESH = pl.DeviceIdType.MESH

N_DEV = 8
EPS = 1e-6
CONV_K = 3
POOL_WINDOWS = (2, 4, 8, 16)
ADAM_LR = 0.001
ADAM_B1 = 0.9
ADAM_B2 = 0.999
ADAM_EPS = 1e-08
ADAM_WD = 0.01
ADAM_STEP = 10

V7X_VMEM_LIMIT_BYTES = 56 * 1024 * 1024
LANES = 128

COLLECTIVE_GATHER = 1
COLLECTIVE_SIBLING = 2
COLLECTIVE_CHIPS = 3
SEQUENCER_COST_BYTES = 4 * 10**9

NN = ((1,), (0,))
NT = ((1,), (1,))
TN = ((0,), (0,))


def _dot(a, b, dims):
    return lax.dot_general(a, b, (dims, ((), ())), preferred_element_type=F32)


def _cp(n_axes):
    return pltpu.CompilerParams(dimension_semantics=("arbitrary",) * n_axes,
                                vmem_limit_bytes=V7X_VMEM_LIMIT_BYTES)


def _row_tile(rows, bytes_per_row, cap_bytes):
    best = None
    for t in range(16, rows + 1, 16):
        if rows % t == 0 and t * bytes_per_row <= cap_bytes:
            best = t
    return best if best is not None else rows


def _chunks(total, size):
    size = min(size, total)
    assert total % size == 0
    return [slice(r, r + size) for r in range(0, total, size)]


def _after_specs(after):
    return [pl.BlockSpec(memory_space=pl.ANY)] * len(after)


def _shift_down(v, k):
    row = lax.broadcasted_iota(jnp.int32, v.shape, 0)
    return jnp.where(row >= k, pltpu.roll(v, k, 0), 0.0)


def _shift_up(v, k):
    n = v.shape[0]
    row = lax.broadcasted_iota(jnp.int32, v.shape, 0)
    return jnp.where(row < n - k, pltpu.roll(v, n - k, 0), 0.0)


def _sigmoid(v):
    return jax.nn.sigmoid(v)


def _cast_bf16(w2d, name):
    rows, cols = w2d.shape
    tr = _row_tile(rows, cols * 4, 2 << 20)

    def body(i_ref, o_ref):
        o_ref[...] = i_ref[...].astype(BF16)

    return pl.pallas_call(
        body, name=name, grid=(rows // tr,),
        in_specs=[pl.BlockSpec((tr, cols), lambda i: (i, 0))],
        out_specs=pl.BlockSpec((tr, cols), lambda i: (i, 0)),
        out_shape=jax.ShapeDtypeStruct((rows, cols), BF16),
        compiler_params=_cp(1),
    )(w2d)


def _rms_fwd(x2d, g):
    s, d = x2d.shape
    tm = min(256, s)

    def body(x_ref, g_ref, h_ref):
        xv = x_ref[...]
        r = lax.rsqrt(jnp.mean(xv * xv, axis=-1, keepdims=True) + EPS)
        h_ref[...] = (xv * r * g_ref[...]).astype(BF16)

    return pl.pallas_call(
        body, name="rms1_fwd", grid=(s // tm,),
        in_specs=[pl.BlockSpec((tm, d), lambda i: (i, 0)), pl.BlockSpec((1, d), lambda i: (0, 0))],
        out_specs=pl.BlockSpec((tm, d), lambda i: (i, 0)),
        out_shape=jax.ShapeDtypeStruct((s, d), BF16),
        compiler_params=_cp(1),
    )(x2d, g)


def _coords():
    return lax.axis_index("x"), lax.axis_index("y"), lax.axis_index("c")


def _slot(p):
    return 4 * p[0] + 2 * p[1] + p[2]


def _handshake(peers):
    barrier = pltpu.get_barrier_semaphore()
    for peer in peers:
        pl.semaphore_signal(barrier, inc=1, device_id=peer, device_id_type=MESH)
    pl.semaphore_wait(barrier, len(peers))


def _sequencer_call(body, out_type, scratch_types, name, collective_id):
    return pl.kernel(
        body, out_type=out_type, name=name,
        mesh=plsc.ScalarSubcoreMesh(axis_name="seq", num_cores=1),
        scratch_types=scratch_types,
        cost_estimate=pl.CostEstimate(flops=0, transcendentals=0, bytes_accessed=SEQUENCER_COST_BYTES),
        compiler_params=pltpu.CompilerParams(collective_id=collective_id))


def _allgather_big(shards, name, collective_id, after=()):
    n = len(shards)

    def body(*refs):
        ins, outs = refs[:n], refs[n + len(after):2 * n + len(after)]
        send_sems, recv_sems, local_sems = refs[2 * n + len(after):]
        x, y, c = _coords()
        me, sibling = (x, y, c), (x, y, 1 - c)
        chips = [(1 - x, y), (x, 1 - y), (1 - x, 1 - y)]
        _handshake([sibling] + [(*chip, c) for chip in chips])

        def copy(a, k, block, to, src=None):
            dst = outs[a].at[_slot(block)]
            return pltpu.make_async_remote_copy(
                src_ref=dst if src is None else src, dst_ref=dst,
                send_sem=send_sems.at[a, k], recv_sem=recv_sems.at[a, k],
                device_id=to, device_id_type=MESH)

        mine, first, passed = [], [], []
        for a in range(n):
            cp = pltpu.make_async_copy(ins[a], outs[a].at[_slot(me)], local_sems.at[a])
            cp.start()
            mine.append(cp)
            f = [copy(a, 0, me, sibling, src=ins[a])]
            f += [copy(a, 1 + j, me, (*chip, c), src=ins[a]) for j, chip in enumerate(chips)]
            for cp in f:
                cp.start()
            first += f
        for a in range(n):
            for j, chip in enumerate(chips):
                copy(a, 1 + j, (*chip, c), me).wait_recv()
                cp = copy(a, 4 + j, (*chip, c), sibling)
                cp.start()
                passed.append(cp)
        for a in range(n):
            copy(a, 0, sibling, me).wait_recv()
            for j, chip in enumerate(chips):
                copy(a, 4 + j, (*chip, 1 - c), me).wait_recv()
        for cp in first + passed:
            cp.wait_send()
        for cp in mine:
            cp.wait()

    return _sequencer_call(
        body, [jax.ShapeDtypeStruct((N_DEV,) + s.shape, s.dtype) for s in shards],
        [pltpu.SemaphoreType.DMA((n, 7)), pltpu.SemaphoreType.DMA((n, 7)), pltpu.SemaphoreType.DMA((n,))],
        name, collective_id)(*shards, *after)


def _exchange_sibling(grads, name, collective_id):
    n = len(grads)

    def body(*refs):
        ins, outs = refs[:n], refs[n:2 * n]
        send_sems, recv_sems = refs[2 * n:]
        x, y, c = _coords()
        sibling = (x, y, 1 - c)
        _handshake([sibling])
        copies = []
        for a in range(n):
            for q in range(4):
                cp = pltpu.make_async_remote_copy(
                    src_ref=ins[a].at[2 * q + (1 - c)], dst_ref=outs[a].at[q],
                    send_sem=send_sems.at[a, q], recv_sem=recv_sems.at[a, q],
                    device_id=sibling, device_id_type=MESH)
                cp.start()
                copies.append(cp)
        for cp in copies:
            cp.wait_recv()
        for cp in copies:
            cp.wait_send()

    any_spec = pl.BlockSpec(memory_space=pl.ANY)
    return pl.pallas_call(
        body, name=name,
        in_specs=[any_spec] * n, out_specs=[any_spec] * n,
        out_shape=[jax.ShapeDtypeStruct((4,) + g.shape[1:], g.dtype) for g in grads],
        scratch_shapes=[pltpu.SemaphoreType.DMA((n, 4)), pltpu.SemaphoreType.DMA((n, 4))],
        compiler_params=pltpu.CompilerParams(collective_id=collective_id),
    )(*grads)


def _exchange_chips(psums, name, collective_id):
    n = len(psums)

    def body(*refs):
        ins, outs = refs[:n], refs[n:2 * n]
        send_sems, recv_sems = refs[2 * n:]
        x, y, c = _coords()
        chips = [(1 - x, y), (x, 1 - y), (1 - x, 1 - y)]
        _handshake([(*chip, c) for chip in chips])
        copies = []
        for a in range(n):
            for j, chip in enumerate(chips):
                cp = pltpu.make_async_remote_copy(
                    src_ref=ins[a].at[2 * chip[0] + chip[1]], dst_ref=outs[a].at[j],
                    send_sem=send_sems.at[a, j], recv_sem=recv_sems.at[a, j],
                    device_id=(*chip, c), device_id_type=MESH)
                cp.start()
                copies.append(cp)
        for cp in copies:
            cp.wait_recv()
        for cp in copies:
            cp.wait_send()

    return _sequencer_call(
        body, [jax.ShapeDtypeStruct((3,) + p.shape[1:], p.dtype) for p in psums],
        [pltpu.SemaphoreType.DMA((n, 3)), pltpu.SemaphoreType.DMA((n, 3))],
        name, collective_id)(*psums)


def _allgather_small(v2d, name):
    rows, cols = v2d.shape

    def body(v_ref, out_ref, send_sems, recv_sems):
        x, y, c = _coords()
        me = (x, y, c)
        out_ref[_slot(me)] = v_ref[...]
        peers = []
        for k in range(1, N_DEV):
            fx, fy, fc = (k >> 2) & 1, (k >> 1) & 1, k & 1
            peers.append(((1 - x) if fx else x, (1 - y) if fy else y, (1 - c) if fc else c))
        sends = []
        for k, peer in enumerate(peers):
            cp = pltpu.make_async_remote_copy(
                src_ref=v_ref, dst_ref=out_ref.at[_slot(me)],
                send_sem=send_sems.at[k], recv_sem=recv_sems.at[k],
                device_id=peer, device_id_type=MESH)
            cp.start()
            sends.append(cp)
        for k, peer in enumerate(peers):
            pltpu.make_async_remote_copy(
                src_ref=v_ref, dst_ref=out_ref.at[_slot(peer)],
                send_sem=send_sems.at[k], recv_sem=recv_sems.at[k],
                device_id=peer, device_id_type=MESH).wait_recv()
        for cp in sends:
            cp.wait_send()

    vmem = pl.BlockSpec(memory_space=pltpu.VMEM)
    return pl.pallas_call(
        body, name=name, in_specs=[vmem], out_specs=vmem,
        out_shape=jax.ShapeDtypeStruct((N_DEV, rows, cols), v2d.dtype),
        scratch_shapes=[pltpu.SemaphoreType.DMA((N_DEV - 1,)), pltpu.SemaphoreType.DMA((N_DEV - 1,))],
    )(v2d)


def _chip_partial(slots, g3, recv, name):
    _, rows, cols = g3.shape
    tr = _row_tile(rows, cols * 2, 2 << 20)

    def body(slots_ref, g_ref, r_ref, o_ref):
        o_ref[...] = (g_ref[...].astype(F32) + r_ref[...].astype(F32)).astype(BF16)

    return pl.pallas_call(
        body, name=name,
        grid_spec=pltpu.PrefetchScalarGridSpec(
            num_scalar_prefetch=1, grid=(4, rows // tr),
            in_specs=[pl.BlockSpec((None, tr, cols), lambda q, i, sl: (sl[q], i, 0)),
                      pl.BlockSpec((None, tr, cols), lambda q, i, sl: (q, i, 0))],
            out_specs=pl.BlockSpec((None, tr, cols), lambda q, i, sl: (q, i, 0))),
        out_shape=jax.ShapeDtypeStruct((4, rows, cols), BF16),
        compiler_params=_cp(2),
    )(slots, g3, recv)


def _adam_math(w, g, m, v):
    m = ADAM_B1 * m + (1.0 - ADAM_B1) * g
    v = ADAM_B2 * v + (1.0 - ADAM_B2) * (g * g)
    m_hat = m / (1.0 - ADAM_B1 ** ADAM_STEP)
    v_hat = v / (1.0 - ADAM_B2 ** ADAM_STEP)
    delta = -ADAM_LR * (m_hat / (jnp.sqrt(v_hat) + ADAM_EPS) + ADAM_WD * w)
    return delta, m, v


def _adam_big(own, w, m, v, g3, recv_sib, recv_chips, name):
    rows, cols = w.shape
    tr = _row_tile(rows, cols * 4, 2 << 20)

    def body(own_ref, w_ref, m_ref, v_ref, g_ref, rs_ref, rc_ref, go_ref, do_ref, mo_ref, vo_ref):
        g = g_ref[...].astype(F32) + rs_ref[...].astype(F32)
        g = g + rc_ref[0].astype(F32)
        g = g + rc_ref[1].astype(F32)
        g = g + rc_ref[2].astype(F32)
        delta, m_new, v_new = _adam_math(w_ref[...], g, m_ref[...], v_ref[...])
        go_ref[...] = g
        do_ref[...] = delta
        mo_ref[...] = m_new
        vo_ref[...] = v_new

    blk = pl.BlockSpec((tr, cols), lambda i, o: (i, 0))
    out = jax.ShapeDtypeStruct((rows, cols), F32)
    return pl.pallas_call(
        body, name=name,
        grid_spec=pltpu.PrefetchScalarGridSpec(
            num_scalar_prefetch=1, grid=(rows // tr,),
            in_specs=[blk, blk, blk,
                      pl.BlockSpec((None, tr, cols), lambda i, o: (o[0], i, 0)),
                      pl.BlockSpec((None, tr, cols), lambda i, o: (o[1], i, 0)),
                      pl.BlockSpec((3, tr, cols), lambda i, o: (0, i, 0))],
            out_specs=[blk, blk, blk, blk]),
        out_shape=[out, out, out, out],
        compiler_params=_cp(1),
    )(own, w, m, v, g3, recv_sib, recv_chips)


def _sum_small(gathered):
    _, rows, cols = gathered.shape

    def body(g_ref, o_ref):
        acc = g_ref[0]
        for k in range(1, N_DEV):
            acc = acc + g_ref[k]
        o_ref[...] = acc

    vmem = pl.BlockSpec(memory_space=pltpu.VMEM)
    return pl.pallas_call(body, name="small_grad_sum", in_specs=[vmem], out_specs=vmem,
                          out_shape=jax.ShapeDtypeStruct((rows, cols), F32))(gathered)


def _adam_small(w, g, m, v):
    def body(w_ref, g_ref, m_ref, v_ref, do_ref, mo_ref, vo_ref):
        delta, m_new, v_new = _adam_math(w_ref[...], g_ref[...], m_ref[...], v_ref[...])
        do_ref[...] = delta
        mo_ref[...] = m_new
        vo_ref[...] = v_new

    vmem = pl.BlockSpec(memory_space=pltpu.VMEM)
    out = jax.ShapeDtypeStruct(w.shape, F32)
    return pl.pallas_call(body, name="adam_small", in_specs=[vmem] * 4, out_specs=[vmem] * 3,
                          out_shape=[out, out, out])(w, g, m, v)


def _proj_fwd(h, win_g):
    s, d = h.shape
    sw = win_g.shape[2]
    tn = min(512, sw)
    nh = sw // tn

    def body(h_ref, w_ref, o_ref):
        for rs in _chunks(s, 512):
            o_ref[rs, :] = _dot(h_ref[rs, :], w_ref[...], NN)

    return pl.pallas_call(
        body, name="proj_fwd", grid=(N_DEV * nh,),
        in_specs=[pl.BlockSpec((s, d), lambda j: (0, 0)),
                  pl.BlockSpec((None, d, tn), lambda j: (j // nh, 0, j % nh))],
        out_specs=pl.BlockSpec((None, s, tn), lambda j: (j // nh, 0, j % nh)),
        out_shape=jax.ShapeDtypeStruct((N_DEV, s, sw), F32),
        compiler_params=_cp(1),
    )(h, win_g)


def _conv_fwd(proj, conv_w, conv_b):
    _, s, sw = proj.shape
    tc = min(LANES, sw)

    def body(ba_ref, ca_ref, va_ref, cw_ref, cb_ref, z_ref):
        cv = ca_ref[...] * va_ref[...]
        u = (cb_ref[...] + cw_ref[0:1, :] * _shift_down(cv, 2) + cw_ref[1:2, :] * _shift_down(cv, 1)
             + cw_ref[2:3, :] * cv)
        z_ref[...] = (ba_ref[...] * u).astype(BF16)

    def part(k):
        return pl.BlockSpec((None, s, tc), lambda i: (k, 0, i))

    return pl.pallas_call(
        body, name="conv_fwd", grid=(sw // tc,),
        in_specs=[part(0), part(1), part(2),
                  pl.BlockSpec((CONV_K, tc), lambda i: (0, i)), pl.BlockSpec((1, tc), lambda i: (0, i))],
        out_specs=pl.BlockSpec((s, tc), lambda i: (0, i)),
        out_shape=jax.ShapeDtypeStruct((s, sw), BF16),
        compiler_params=_cp(1),
    )(proj, proj, proj, conv_w, conv_b)


def _pool_counts(shape, window):
    t = lax.broadcasted_iota(jnp.int32, shape, 0)
    return jnp.minimum(t + 1, window).astype(F32)


def _pool_fwd(proj):
    _, s, sw = proj.shape
    gw = sw // len(POOL_WINDOWS)

    def body(v_ref, p_ref):
        for gi, window in enumerate(POOL_WINDOWS):
            @pl.when(pl.program_id(0) == gi)
            def _():
                v = v_ref[...]
                acc, k = v, 1
                while k < window:
                    acc = acc + _shift_down(acc, k)
                    k *= 2
                p_ref[...] = (acc / _pool_counts(v.shape, window) - v).astype(BF16)

    return pl.pallas_call(
        body, name="pool_fwd", grid=(len(POOL_WINDOWS),),
        in_specs=[pl.BlockSpec((None, s, gw), lambda g: (3, 0, g))],
        out_specs=pl.BlockSpec((s, gw), lambda g: (0, g)),
        out_shape=jax.ShapeDtypeStruct((s, sw), BF16),
        compiler_params=_cp(1),
    )(proj)


def _merge_fwd(z, wa, p, wpool, proj, b_gate2, pool_scale):
    s, sw = z.shape
    d = wa.shape[1]
    tn = d // N_DEV
    gw = sw // len(POOL_WINDOWS)
    nq = sw // tn

    def body(z_ref, wa_ref, p_ref, wp_ref, ga_ref, gb_ref, bg_ref, sc_ref, ya_ref, yb_ref, m_ref):
        for rs in _chunks(s, 512):
            ya = _dot(z_ref[rs, :], wa_ref[...], NN)
            yb = _dot(p_ref[rs, :], wp_ref[...], NN)
            sa = _sigmoid(ga_ref[rs, :] + bg_ref[0:1, :])
            sb = _sigmoid(gb_ref[rs, :] + bg_ref[1:2, :])
            ya_ref[rs, :] = ya.astype(BF16)
            yb_ref[rs, :] = yb.astype(BF16)
            m_ref[rs, :] = (sa * ya + sb * (yb * sc_ref[...])).astype(BF16)

    col = pl.BlockSpec((s, tn), lambda j: (0, j))
    out = jax.ShapeDtypeStruct((s, d), BF16)
    return pl.pallas_call(
        body, name="merge_fwd", grid=(N_DEV,),
        in_specs=[pl.BlockSpec((s, sw), lambda j: (0, 0)),
                  pl.BlockSpec((sw, tn), lambda j: (0, j)),
                  pl.BlockSpec((s, gw), lambda j: (0, j // 2)),
                  pl.BlockSpec((None, gw, tn), lambda j: (j // 2, 0, j % 2)),
                  pl.BlockSpec((None, s, tn), lambda j: (4 + j // nq, 0, j % nq)),
                  pl.BlockSpec((None, s, tn), lambda j: (6 + j // nq, 0, j % nq)),
                  pl.BlockSpec((2, tn), lambda j: (0, j)),
                  pl.BlockSpec((1, tn), lambda j: (0, j))],
        out_specs=[col, col, col],
        out_shape=[out, out, out],
        compiler_params=_cp(1),
    )(z, wa, p, wpool, proj, proj, b_gate2, pool_scale)


def _wo_fwd(merged, wo, x2d, g2):
    s, d = x2d.shape
    tm = min(256, s)

    def body(m_ref, wo_ref, x_ref, g_ref, x1_ref, h2_ref):
        x1 = x_ref[...] + _dot(m_ref[...], wo_ref[...], NN)
        x1_ref[...] = x1
        r = lax.rsqrt(jnp.mean(x1 * x1, axis=-1, keepdims=True) + EPS)
        h2_ref[...] = (x1 * r * g_ref[...]).astype(BF16)

    row = pl.BlockSpec((tm, d), lambda i: (i, 0))
    return pl.pallas_call(
        body, name="wo_fwd", grid=(s // tm,),
        in_specs=[row, pl.BlockSpec((d, d), lambda i: (0, 0)), row, pl.BlockSpec((1, d), lambda i: (0, 0))],
        out_specs=[row, row],
        out_shape=[jax.ShapeDtypeStruct((s, d), F32), jax.ShapeDtypeStruct((s, d), BF16)],
        compiler_params=_cp(1),
    )(merged, wo, x2d, g2)


def _ffn_up_fwd(h2, wg_g, wu_g):
    s, d = h2.shape
    f8 = wg_g.shape[2]

    def body(h_ref, wg_ref, wu_ref, g_ref, u_ref):
        for rs in _chunks(s, 512):
            a = h_ref[rs, :]
            g_ref[rs, :] = _dot(a, wg_ref[...], NN).astype(BF16)
            u_ref[rs, :] = _dot(a, wu_ref[...], NN).astype(BF16)

    wspec = pl.BlockSpec((None, d, f8), lambda j: (j, 0, 0))
    ospec = pl.BlockSpec((None, s, f8), lambda j: (j, 0, 0))
    out = jax.ShapeDtypeStruct((N_DEV, s, f8), BF16)
    return pl.pallas_call(
        body, name="ffn_up_fwd", grid=(N_DEV,),
        in_specs=[pl.BlockSpec((s, d), lambda j: (0, 0)), wspec, wspec],
        out_specs=[ospec, ospec], out_shape=[out, out],
        compiler_params=_cp(1),
    )(h2, wg_g, wu_g)


def _ffn_down_loss(gact, uact, wd_g, x1, target, final_g):
    _, s, f8 = gact.shape
    d = x1.shape[1]
    tm = min(256, s)
    last = N_DEV - 1

    def body(g_ref, u_ref, wd_ref, x1_ref, t_ref, gf_ref, dx_ref, dxb_ref, dgf_ref, loss_ref, acc_ref):
        i, j = pl.program_id(0), pl.program_id(1)

        @pl.when(j == 0)
        def _():
            acc_ref[...] = jnp.zeros_like(acc_ref)

        @pl.when((i == 0) & (j == 0))
        def _():
            dgf_ref[...] = jnp.zeros_like(dgf_ref)
            loss_ref[...] = jnp.zeros_like(loss_ref)

        for rs in _chunks(tm, 256):
            g = g_ref[rs, :].astype(F32)
            act = (g * _sigmoid(g) * u_ref[rs, :].astype(F32)).astype(BF16)
            acc_ref[rs, :] += _dot(act, wd_ref[...], NN)

        @pl.when(j == last)
        def _():
            for rs in _chunks(tm, 256):
                x2 = x1_ref[rs, :] + acc_ref[rs, :]
                r = lax.rsqrt(jnp.mean(x2 * x2, axis=-1, keepdims=True) + EPS)
                nrm = x2 * r
                gf = gf_ref[...]
                err = nrm * gf - t_ref[rs, :]
                loss_ref[...] += jnp.sum(err * err) * (0.5 / d)
                dy = err * (1.0 / d)
                dgf_ref[...] += jnp.sum(dy * nrm, axis=0, keepdims=True)
                dn = dy * gf
                dx = r * (dn - nrm * jnp.mean(dn * nrm, axis=-1, keepdims=True))
                dx_ref[rs, :] = dx
                dxb_ref[rs, :] = dx.astype(BF16)

    aspec = pl.BlockSpec((None, tm, f8), lambda i, j: (j, i, 0))
    row = pl.BlockSpec((tm, d), lambda i, j: (i, 0))
    return pl.pallas_call(
        body, name="ffn_down_loss", grid=(s // tm, N_DEV),
        in_specs=[aspec, aspec, pl.BlockSpec((None, f8, d), lambda i, j: (j, 0, 0)), row, row,
                  pl.BlockSpec((1, d), lambda i, j: (0, 0))],
        out_specs=[row, row, pl.BlockSpec((1, d), lambda i, j: (0, 0)),
                   pl.BlockSpec((8, LANES), lambda i, j: (0, 0))],
        out_shape=[jax.ShapeDtypeStruct((s, d), F32), jax.ShapeDtypeStruct((s, d), BF16),
                   jax.ShapeDtypeStruct((1, d), F32), jax.ShapeDtypeStruct((8, LANES), F32)],
        scratch_shapes=[pltpu.VMEM((tm, d), F32)],
        compiler_params=_cp(2),
    )(gact, uact, wd_g, x1, target, final_g)


def _ffn_act_bwd(dx2b, wd_g, gact, uact):
    s, d = dx2b.shape
    f8 = gact.shape[2]
    tm = min(1024, s)

    def body(dx_ref, wd_ref, g_ref, u_ref, dg_ref, du_ref, act_ref):
        for rs in _chunks(tm, 256):
            da = _dot(dx_ref[rs, :], wd_ref[...], NT)
            g = g_ref[rs, :].astype(F32)
            u = u_ref[rs, :].astype(F32)
            sg = _sigmoid(g)
            silu = g * sg
            act_ref[rs, :] = (silu * u).astype(BF16)
            du_ref[rs, :] = (da * silu).astype(BF16)
            dg_ref[rs, :] = (da * u * (sg * (1.0 + g * (1.0 - sg)))).astype(BF16)

    aspec = pl.BlockSpec((None, tm, f8), lambda j, i: (j, i, 0))
    out = jax.ShapeDtypeStruct((N_DEV, s, f8), BF16)
    return pl.pallas_call(
        body, name="ffn_act_bwd", grid=(N_DEV, s // tm),
        in_specs=[pl.BlockSpec((tm, d), lambda j, i: (i, 0)),
                  pl.BlockSpec((None, f8, d), lambda j, i: (j, 0, 0)), aspec, aspec],
        out_specs=[aspec, aspec, aspec], out_shape=[out, out, out],
        compiler_params=_cp(2),
    )(dx2b, wd_g, gact, uact)


def _wgrad_shard_a(a3, b, name, after=()):
    _, s, k = a3.shape
    n = b.shape[1]
    ts = min(512, s)
    ns = s // ts

    def body(a_ref, b_ref, *rest):
        o_ref, acc_ref = rest[len(after):]
        i = pl.program_id(1)

        @pl.when(i == 0)
        def _():
            acc_ref[...] = jnp.zeros_like(acc_ref)

        acc_ref[...] += _dot(a_ref[...], b_ref[...], TN)

        @pl.when(i == ns - 1)
        def _():
            o_ref[...] = acc_ref[...].astype(BF16)

    return pl.pallas_call(
        body, name=name, grid=(N_DEV, ns),
        in_specs=[pl.BlockSpec((None, ts, k), lambda j, i: (j, i, 0)),
                  pl.BlockSpec((ts, n), lambda j, i: (i, 0))] + _after_specs(after),
        out_specs=pl.BlockSpec((None, k, n), lambda j, i: (j, 0, 0)),
        out_shape=jax.ShapeDtypeStruct((N_DEV, k, n), BF16),
        scratch_shapes=[pltpu.VMEM((k, n), F32)],
        compiler_params=_cp(2),
    )(a3, b, *after)


def _wgrad_shard_b(a, b3, name, after=()):
    s, k = a.shape
    n = b3.shape[2]
    ts = min(512, s)
    ns = s // ts

    def body(a_ref, b_ref, *rest):
        o_ref, acc_ref = rest[len(after):]
        i = pl.program_id(1)

        @pl.when(i == 0)
        def _():
            acc_ref[...] = jnp.zeros_like(acc_ref)

        acc_ref[...] += _dot(a_ref[...], b_ref[...], TN)

        @pl.when(i == ns - 1)
        def _():
            o_ref[...] = acc_ref[...].astype(BF16)

    return pl.pallas_call(
        body, name=name, grid=(N_DEV, ns),
        in_specs=[pl.BlockSpec((ts, k), lambda j, i: (i, 0)),
                  pl.BlockSpec((None, ts, n), lambda j, i: (j, i, 0))] + _after_specs(after),
        out_specs=pl.BlockSpec((None, k, n), lambda j, i: (j, 0, 0)),
        out_shape=jax.ShapeDtypeStruct((N_DEV, k, n), BF16),
        scratch_shapes=[pltpu.VMEM((k, n), F32)],
        compiler_params=_cp(2),
    )(a, b3, *after)


def _ffn_up_act_fwd(h2, wg_g, wu_g):
    s, d = h2.shape
    f8 = wg_g.shape[2]
    th = min(1024, s)

    def body(h_ref, wg_ref, wu_ref, g_ref, u_ref, a_ref):
        i = pl.program_id(1)
        for rs in _chunks(th, 512):
            rows = pl.ds(pl.multiple_of(i * th + rs.start, rs.stop - rs.start), rs.stop - rs.start)
            a = h_ref[rows, :]
            g = _dot(a, wg_ref[...], NN)
            u = _dot(a, wu_ref[...], NN)
            g_ref[rs, :] = g.astype(BF16)
            u_ref[rs, :] = u.astype(BF16)
            a_ref[rs, :] = (g * _sigmoid(g) * u).astype(BF16)

    wspec = pl.BlockSpec((None, d, f8), lambda j, i: (j, 0, 0))
    ospec = pl.BlockSpec((None, th, f8), lambda j, i: (j, i, 0))
    out = jax.ShapeDtypeStruct((N_DEV, s, f8), BF16)
    return pl.pallas_call(
        body, name="ffn_up_fwd", grid=(N_DEV, s // th),
        in_specs=[pl.BlockSpec((s, d), lambda j, i: (0, 0)), wspec, wspec],
        out_specs=[ospec, ospec, ospec], out_shape=[out, out, out],
        compiler_params=_cp(2),
    )(h2, wg_g, wu_g)


def _ffn_down_fwd(act, wd_g):
    _, s, f8 = act.shape
    d = wd_g.shape[2]
    tn = min(1024, d)

    def body(a_ref, wd_ref, o_ref):
        j = pl.program_id(1)

        @pl.when(j == 0)
        def _():
            o_ref[...] = jnp.zeros_like(o_ref)

        for rs in _chunks(s, 1024):
            o_ref[rs, :] += _dot(a_ref[rs, :], wd_ref[...], NN)

    return pl.pallas_call(
        body, name="ffn_down_fwd", grid=(d // tn, N_DEV),
        in_specs=[pl.BlockSpec((None, s, f8), lambda n, j: (j, 0, 0)),
                  pl.BlockSpec((None, f8, tn), lambda n, j: (j, 0, n))],
        out_specs=pl.BlockSpec((s, tn), lambda n, j: (0, n)),
        out_shape=jax.ShapeDtypeStruct((s, d), F32),
        compiler_params=_cp(2),
    )(act, wd_g)


def _loss_bwd(ffn_out, x1, target, final_g):
    s, d = x1.shape
    tm = min(256, s)

    def body(f_ref, x1_ref, t_ref, gf_ref, dx_ref, dxb_ref, dgf_ref, loss_ref):
        @pl.when(pl.program_id(0) == 0)
        def _():
            dgf_ref[...] = jnp.zeros_like(dgf_ref)
            loss_ref[...] = jnp.zeros_like(loss_ref)

        x2 = x1_ref[...] + f_ref[...]
        r = lax.rsqrt(jnp.mean(x2 * x2, axis=-1, keepdims=True) + EPS)
        nrm = x2 * r
        gf = gf_ref[...]
        err = nrm * gf - t_ref[...]
        loss_ref[...] += jnp.sum(err * err) * (0.5 / d)
        dy = err * (1.0 / d)
        dgf_ref[...] += jnp.sum(dy * nrm, axis=0, keepdims=True)
        dn = dy * gf
        dx = r * (dn - nrm * jnp.mean(dn * nrm, axis=-1, keepdims=True))
        dx_ref[...] = dx
        dxb_ref[...] = dx.astype(BF16)

    row = pl.BlockSpec((tm, d), lambda i: (i, 0))
    vec = pl.BlockSpec((1, d), lambda i: (0, 0))
    return pl.pallas_call(
        body, name="loss_bwd", grid=(s // tm,),
        in_specs=[row, row, row, vec],
        out_specs=[row, row, vec, pl.BlockSpec((8, LANES), lambda i: (0, 0))],
        out_shape=[jax.ShapeDtypeStruct((s, d), F32), jax.ShapeDtypeStruct((s, d), BF16),
                   jax.ShapeDtypeStruct((1, d), F32), jax.ShapeDtypeStruct((8, LANES), F32)],
        compiler_params=_cp(1),
    )(ffn_out, x1, target, final_g)


def _ffn_gate_bwd(dx2b, wd_g, gact, uact):
    s, d = dx2b.shape
    f8 = gact.shape[2]
    th = min(1024, s)

    def body(dx_ref, wd_ref, g_ref, u_ref, dg_ref, du_ref):
        i = pl.program_id(1)
        for rs in _chunks(th, 512):
            rows = pl.ds(pl.multiple_of(i * th + rs.start, rs.stop - rs.start), rs.stop - rs.start)
            da = _dot(dx_ref[rows, :], wd_ref[...], NT)
            g = g_ref[rs, :].astype(F32)
            u = u_ref[rs, :].astype(F32)
            sg = _sigmoid(g)
            du_ref[rs, :] = (da * (g * sg)).astype(BF16)
            dg_ref[rs, :] = (da * u * (sg * (1.0 + g * (1.0 - sg)))).astype(BF16)

    aspec = pl.BlockSpec((None, th, f8), lambda j, i: (j, i, 0))
    out = jax.ShapeDtypeStruct((N_DEV, s, f8), BF16)
    return pl.pallas_call(
        body, name="ffn_act_bwd", grid=(N_DEV, s // th),
        in_specs=[pl.BlockSpec((s, d), lambda j, i: (0, 0)),
                  pl.BlockSpec((None, f8, d), lambda j, i: (j, 0, 0)), aspec, aspec],
        out_specs=[aspec, aspec], out_shape=[out, out],
        compiler_params=_cp(2),
    )(dx2b, wd_g, gact, uact)


def _wgrad_rows(a3, b, name, after=()):
    _, s, k = a3.shape
    n = b.shape[1]

    def body(a_ref, b_ref, *rest):
        o_ref = rest[len(after)]
        o_ref[...] = _dot(a_ref[...], b_ref[...], TN).astype(BF16)

    return pl.pallas_call(
        body, name=name, grid=(N_DEV,),
        in_specs=[pl.BlockSpec((None, s, k), lambda j: (j, 0, 0)),
                  pl.BlockSpec((s, n), lambda j: (0, 0))] + _after_specs(after),
        out_specs=pl.BlockSpec((None, k, n), lambda j: (j, 0, 0)),
        out_shape=jax.ShapeDtypeStruct((N_DEV, k, n), BF16),
        compiler_params=_cp(1),
    )(a3, b, *after)


def _wgrad_cols(a, b3, name, after=()):
    s, k = a.shape
    n = b3.shape[2]

    def body(a_ref, b_ref, *rest):
        o_ref = rest[len(after)]
        o_ref[...] = _dot(a_ref[...], b_ref[...], TN).astype(BF16)

    return pl.pallas_call(
        body, name=name, grid=(N_DEV,),
        in_specs=[pl.BlockSpec((s, k), lambda j: (0, 0)),
                  pl.BlockSpec((None, s, n), lambda j: (j, 0, 0))] + _after_specs(after),
        out_specs=pl.BlockSpec((None, k, n), lambda j: (j, 0, 0)),
        out_shape=jax.ShapeDtypeStruct((N_DEV, k, n), BF16),
        compiler_params=_cp(1),
    )(a, b3, *after)


def _input_grad(pairs, name, after=()):
    s = pairs[0][0].shape[1]
    d = pairs[0][1].shape[1]
    tn = min(1024, d)
    npair = len(pairs)

    def body(*refs):
        ops = refs[:2 * npair]
        o_ref = refs[2 * npair + len(after)]
        j = pl.program_id(1)

        @pl.when(j == 0)
        def _():
            o_ref[...] = jnp.zeros_like(o_ref)

        for rs in _chunks(s, 1024):
            part = _dot(ops[0][rs, :], ops[1][...], NT)
            for q in range(1, npair):
                part = part + _dot(ops[2 * q][rs, :], ops[2 * q + 1][...], NT)
            o_ref[rs, :] += part

    in_specs, args = [], []
    for a3, w3 in pairs:
        k = a3.shape[2]
        in_specs += [pl.BlockSpec((None, s, k), lambda n, j: (j, 0, 0)),
                     pl.BlockSpec((None, tn, k), lambda n, j: (j, n, 0))]
        args += [a3, w3]
    return pl.pallas_call(
        body, name=name, grid=(d // tn, N_DEV),
        in_specs=in_specs + _after_specs(after),
        out_specs=pl.BlockSpec((s, tn), lambda n, j: (0, n)),
        out_shape=jax.ShapeDtypeStruct((s, d), F32),
        compiler_params=_cp(2),
    )(*args, *after)


def _rms_bwd(dh, xres, g, dres, name):
    s, d = xres.shape
    tm = min(256, s)

    def body(dh_ref, x_ref, g_ref, dres_ref, dx_ref, dxb_ref, dg_ref):
        @pl.when(pl.program_id(0) == 0)
        def _():
            dg_ref[...] = jnp.zeros_like(dg_ref)

        xv = x_ref[...]
        dh_v = dh_ref[...]
        r = lax.rsqrt(jnp.mean(xv * xv, axis=-1, keepdims=True) + EPS)
        nrm = xv * r
        dg_ref[...] += jnp.sum(dh_v * nrm, axis=0, keepdims=True)
        dn = dh_v * g_ref[...]
        dx = dres_ref[...] + r * (dn - nrm * jnp.mean(dn * nrm, axis=-1, keepdims=True))
        dx_ref[...] = dx
        dxb_ref[...] = dx.astype(BF16)

    row = pl.BlockSpec((tm, d), lambda i: (i, 0))
    vec = pl.BlockSpec((1, d), lambda i: (0, 0))
    return pl.pallas_call(
        body, name=name, grid=(s // tm,),
        in_specs=[row, row, vec, row],
        out_specs=[row, row, vec],
        out_shape=[jax.ShapeDtypeStruct((s, d), F32), jax.ShapeDtypeStruct((s, d), BF16),
                   jax.ShapeDtypeStruct((1, d), F32)],
        compiler_params=_cp(1),
    )(dh, xres, g, dres)


def _wgrad_full(a, b, name):
    s, k = a.shape
    n = b.shape[1]
    tk = min(512, k)
    ts = min(512, s)
    ns = s // ts

    def body(a_ref, b_ref, o_ref, acc_ref):
        i = pl.program_id(1)

        @pl.when(i == 0)
        def _():
            acc_ref[...] = jnp.zeros_like(acc_ref)

        acc_ref[...] += _dot(a_ref[...], b_ref[...], TN)

        @pl.when(i == ns - 1)
        def _():
            o_ref[...] = acc_ref[...].astype(BF16)

    return pl.pallas_call(
        body, name=name, grid=(k // tk, ns),
        in_specs=[pl.BlockSpec((ts, tk), lambda j, i: (i, j)),
                  pl.BlockSpec((ts, n), lambda j, i: (i, 0))],
        out_specs=pl.BlockSpec((tk, n), lambda j, i: (j, 0)),
        out_shape=jax.ShapeDtypeStruct((k, n), BF16),
        scratch_shapes=[pltpu.VMEM((tk, n), F32)],
        compiler_params=_cp(2),
    )(a, b)


def _wgrad_pool(p, dyb, n_groups):
    s, sw = p.shape
    d = dyb.shape[1]
    gw, go = sw // n_groups, d // n_groups
    ts = min(512, s)
    ns = s // ts

    def body(a_ref, b_ref, o_ref, acc_ref):
        i = pl.program_id(1)

        @pl.when(i == 0)
        def _():
            acc_ref[...] = jnp.zeros_like(acc_ref)

        acc_ref[...] += _dot(a_ref[...], b_ref[...], TN)

        @pl.when(i == ns - 1)
        def _():
            o_ref[...] = acc_ref[...].astype(BF16)

    return pl.pallas_call(
        body, name="wgrad_pool", grid=(n_groups, ns),
        in_specs=[pl.BlockSpec((ts, gw), lambda g, i: (i, g)),
                  pl.BlockSpec((ts, go), lambda g, i: (i, g))],
        out_specs=pl.BlockSpec((None, gw, go), lambda g, i: (g, 0, 0)),
        out_shape=jax.ShapeDtypeStruct((n_groups, gw, go), BF16),
        scratch_shapes=[pltpu.VMEM((gw, go), F32)],
        compiler_params=_cp(2),
    )(p, dyb)


def _input_grad_rms(pairs, xres, g, dres, name, after=()):
    s, d = xres.shape
    tm = min(256, s)
    last = N_DEV - 1
    npair = len(pairs)

    def body(*refs):
        ops = refs[:2 * npair]
        x_ref, g_ref, dres_ref = refs[2 * npair:2 * npair + 3]
        dx_ref, dxb_ref, dg_ref, acc_ref = refs[2 * npair + 3 + len(after):]
        i, j = pl.program_id(0), pl.program_id(1)

        @pl.when(j == 0)
        def _():
            acc_ref[...] = jnp.zeros_like(acc_ref)

        @pl.when((i == 0) & (j == 0))
        def _():
            dg_ref[...] = jnp.zeros_like(dg_ref)

        for rs in _chunks(tm, 256):
            part = _dot(ops[0][rs, :], ops[1][...], NT)
            for q in range(1, npair):
                part = part + _dot(ops[2 * q][rs, :], ops[2 * q + 1][...], NT)
            acc_ref[rs, :] += part

        @pl.when(j == last)
        def _():
            for rs in _chunks(tm, 256):
                xv = x_ref[rs, :]
                dh = acc_ref[rs, :]
                r = lax.rsqrt(jnp.mean(xv * xv, axis=-1, keepdims=True) + EPS)
                nrm = xv * r
                dg_ref[...] += jnp.sum(dh * nrm, axis=0, keepdims=True)
                dn = dh * g_ref[...]
                dx = dres_ref[rs, :] + r * (dn - nrm * jnp.mean(dn * nrm, axis=-1, keepdims=True))
                dx_ref[rs, :] = dx
                dxb_ref[rs, :] = dx.astype(BF16)

    in_specs, args = [], []
    for a3, w3 in pairs:
        k = a3.shape[2]
        in_specs += [pl.BlockSpec((None, tm, k), lambda i, j: (j, i, 0)),
                     pl.BlockSpec((None, d, k), lambda i, j: (j, 0, 0))]
        args += [a3, w3]
    row = pl.BlockSpec((tm, d), lambda i, j: (i, 0))
    vec = pl.BlockSpec((1, d), lambda i, j: (0, 0))
    return pl.pallas_call(
        body, name=name, grid=(s // tm, N_DEV),
        in_specs=in_specs + [row, vec, row] + _after_specs(after),
        out_specs=[row, row, vec],
        out_shape=[jax.ShapeDtypeStruct((s, d), F32), jax.ShapeDtypeStruct((s, d), BF16),
                   jax.ShapeDtypeStruct((1, d), F32)],
        scratch_shapes=[pltpu.VMEM((tm, d), F32)],
        compiler_params=_cp(2),
    )(*args, xres, g, dres, *after)


def _wo_bwd(dx1b, wo, ya, yb, proj, b_gate2, pool_scale, after=()):
    s, d = dx1b.shape
    sw = proj.shape[2]
    tn = d // N_DEV
    nq = sw // tn

    def body(dx_ref, wo_ref, ya_ref, yb_ref, ga_ref, gb_ref, bg_ref, sc_ref, *rest):
        dya_ref, dyb_ref, dp_ref, dbg_ref, dsc_ref = rest[len(after):]
        dbg_ref[...] = jnp.zeros_like(dbg_ref)
        dsc_ref[...] = jnp.zeros_like(dsc_ref)
        for rs in _chunks(s, 256):
            dm = _dot(dx_ref[rs, :], wo_ref[...], NT)
            ya_v = ya_ref[rs, :].astype(F32)
            yb_v = yb_ref[rs, :].astype(F32)
            sa = _sigmoid(ga_ref[rs, :] + bg_ref[0:1, :])
            sb = _sigmoid(gb_ref[rs, :] + bg_ref[1:2, :])
            sc = sc_ref[...]
            dya_ref[rs, :] = (dm * sa).astype(BF16)
            dsb = dm * sb
            dyb_ref[rs, :] = (dsb * sc).astype(BF16)
            dsc_ref[...] += jnp.sum(dsb * yb_v, axis=0, keepdims=True)
            dga = dm * ya_v * (sa * (1.0 - sa))
            dgb = dm * (yb_v * sc) * (sb * (1.0 - sb))
            dp_ref[0, rs, :] = dga.astype(BF16)
            dp_ref[1, rs, :] = dgb.astype(BF16)
            dbg_ref[0:1, :] += jnp.sum(dga, axis=0, keepdims=True)
            dbg_ref[1:2, :] += jnp.sum(dgb, axis=0, keepdims=True)

    col = pl.BlockSpec((s, tn), lambda j: (0, j))
    out = jax.ShapeDtypeStruct((s, d), BF16)
    return pl.pallas_call(
        body, name="wo_bwd", grid=(N_DEV,),
        in_specs=[pl.BlockSpec((s, d), lambda j: (0, 0)),
                  pl.BlockSpec((tn, d), lambda j: (j, 0)), col, col,
                  pl.BlockSpec((None, s, tn), lambda j: (4 + j // nq, 0, j % nq)),
                  pl.BlockSpec((None, s, tn), lambda j: (6 + j // nq, 0, j % nq)),
                  pl.BlockSpec((2, tn), lambda j: (0, j)),
                  pl.BlockSpec((1, tn), lambda j: (0, j))] + _after_specs(after),
        out_specs=[col, col,
                   pl.BlockSpec((2, None, s, tn), lambda j: (1, j // nq, 0, j % nq)),
                   pl.BlockSpec((2, tn), lambda j: (0, j)),
                   pl.BlockSpec((1, tn), lambda j: (0, j))],
        out_shape=[out, out, jax.ShapeDtypeStruct((4, 2, s, sw), BF16),
                   jax.ShapeDtypeStruct((2, d), F32), jax.ShapeDtypeStruct((1, d), F32)],
        compiler_params=_cp(1),
    )(dx1b, wo, ya, yb, proj, proj, b_gate2, pool_scale, *after)


def _conv_bwd(dproj, dya, wa, proj, conv_w, conv_b):
    s, d = dya.shape
    sw = wa.shape[0]
    tc = min(LANES, sw)

    def body(dproj_hbm, dya_ref, wa_ref, ba_ref, ca_ref, va_ref, cw_ref, cb_ref,
             dp_ref, dcw_ref, dcb_ref, dz_ref):
        del dproj_hbm
        for rs in _chunks(s, 512):
            dz_ref[rs, :] = _dot(dya_ref[rs, :], wa_ref[...], NT)
        dz = dz_ref[...]
        ba, ca, va = ba_ref[...], ca_ref[...], va_ref[...]
        cv = ca * va
        cv1, cv2 = _shift_down(cv, 1), _shift_down(cv, 2)
        w0, w1, w2 = cw_ref[0:1, :], cw_ref[1:2, :], cw_ref[2:3, :]
        u = cb_ref[...] + w0 * cv2 + w1 * cv1 + w2 * cv
        du = dz * ba
        dp_ref[0] = (dz * u).astype(BF16)
        dcv = w2 * du + w1 * _shift_up(du, 1) + w0 * _shift_up(du, 2)
        dp_ref[1] = (dcv * va).astype(BF16)
        dp_ref[2] = (dcv * ca).astype(BF16)
        dcw_ref[0:1, :] = jnp.sum(du * cv2, axis=0, keepdims=True)
        dcw_ref[1:2, :] = jnp.sum(du * cv1, axis=0, keepdims=True)
        dcw_ref[2:3, :] = jnp.sum(du * cv, axis=0, keepdims=True)
        dcb_ref[...] = jnp.sum(du, axis=0, keepdims=True)

    def part(k):
        return pl.BlockSpec((None, s, tc), lambda i: (k, 0, i))

    return pl.pallas_call(
        body, name="conv_bwd", grid=(sw // tc,),
        in_specs=[pl.BlockSpec(memory_space=pl.ANY),
                  pl.BlockSpec((s, d), lambda i: (0, 0)),
                  pl.BlockSpec((tc, d), lambda i: (i, 0)),
                  part(0), part(1), part(2),
                  pl.BlockSpec((CONV_K, tc), lambda i: (0, i)), pl.BlockSpec((1, tc), lambda i: (0, i))],
        out_specs=[pl.BlockSpec((3, s, tc), lambda i: (0, 0, i)),
                   pl.BlockSpec((CONV_K, tc), lambda i: (0, i)), pl.BlockSpec((1, tc), lambda i: (0, i))],
        out_shape=[jax.ShapeDtypeStruct(dproj.shape, BF16),
                   jax.ShapeDtypeStruct((CONV_K, sw), F32), jax.ShapeDtypeStruct((1, sw), F32)],
        scratch_shapes=[pltpu.VMEM((s, tc), F32)],
        input_output_aliases={0: 0},
        compiler_params=_cp(1),
    )(dproj, dya, wa, proj, proj, proj, conv_w, conv_b)


def _pool_bwd(dproj, dyb, wpool):
    s, d = dyb.shape
    n_groups, gw, go = wpool.shape

    def body(dproj_hbm, dyb_ref, wp_ref, dp_ref):
        del dproj_hbm
        for gi, window in enumerate(POOL_WINDOWS):
            @pl.when(pl.program_id(0) == gi)
            def _():
                dpool = _dot(dyb_ref[...], wp_ref[...], NT)
                acc, k = dpool / _pool_counts(dpool.shape, window), 1
                while k < window:
                    acc = acc + _shift_up(acc, k)
                    k *= 2
                dp_ref[...] = (acc - dpool).astype(BF16)

    return pl.pallas_call(
        body, name="pool_bwd", grid=(n_groups,),
        in_specs=[pl.BlockSpec(memory_space=pl.ANY),
                  pl.BlockSpec((s, go), lambda g: (0, g)),
                  pl.BlockSpec((None, gw, go), lambda g: (g, 0, 0))],
        out_specs=pl.BlockSpec((None, s, gw), lambda g: (3, 0, g)),
        out_shape=jax.ShapeDtypeStruct(dproj.shape, BF16),
        input_output_aliases={0: 0},
        compiler_params=_cp(1),
    )(dproj, dyb, wpool)


def _rows128(v):
    return v.reshape(-1, LANES)


def kernel(x, norm1_g, w_in, b_gate, conv_w, conv_b, w_a_out, w_pool, pool_scale, w_o, norm2_g, w_ffn_gate, w_ffn_up, w_ffn_down, final_g, loss_target, m_norm1_g, m_w_in, m_b_gate, m_conv_w, m_conv_b, m_w_a_out, m_w_pool, m_pool_scale, m_w_o, m_norm2_g, m_w_ffn_gate, m_w_ffn_up, m_w_ffn_down, m_final_g, v_norm1_g, v_w_in, v_b_gate, v_conv_w, v_conv_b, v_w_a_out, v_w_pool, v_pool_scale, v_w_o, v_norm2_g, v_w_ffn_gate, v_w_ffn_up, v_w_ffn_down, v_final_g):
    s, d = x.shape[1], x.shape[2]
    sw = w_in.shape[2]
    n_groups = w_pool.shape[1]
    gw = w_pool.shape[2]
    go = w_pool.shape[3] * N_DEV
    f8 = w_ffn_gate.shape[2]
    cws = conv_w.shape[2]
    assert sw == conv_w.shape[2] * N_DEV == gw * n_groups and go * n_groups == d and n_groups == len(POOL_WINDOWS)

    xi, yi, ci = _coords()
    me = 4 * xi + 2 * yi + ci
    my_chip = 2 * xi + yi

    x2d = x.reshape(s, d)
    target = loss_target.reshape(s, d)
    final_g2 = final_g.reshape(1, d)
    b_gate2 = b_gate.reshape(2, d)

    big_names = ["w_in", "w_a_out", "w_pool", "w_o", "w_ffn_gate", "w_ffn_up", "w_ffn_down"]
    big_w = [w_in, w_a_out, w_pool, w_o, w_ffn_gate, w_ffn_up, w_ffn_down]
    big_m = [m_w_in, m_w_a_out, m_w_pool, m_w_o, m_w_ffn_gate, m_w_ffn_up, m_w_ffn_down]
    big_v = [v_w_in, v_w_a_out, v_w_pool, v_w_o, v_w_ffn_gate, v_w_ffn_up, v_w_ffn_down]
    shapes2d = [(w.size // w.shape[-1], w.shape[-1]) for w in big_w]
    big_w2 = [w.reshape(sh) for w, sh in zip(big_w, shapes2d)]

    sb = [_cast_bf16(w, "cast_" + nm) for w, nm in zip(big_w2, big_names)]
    win_g, wa_g, wpool_g, wo_g = _allgather_big(sb[0:4], "allgather_mixer", COLLECTIVE_GATHER)
    wg_g, wu_g = _allgather_big(sb[4:6], "allgather_ffn_up", COLLECTIVE_GATHER)
    (wd_g,) = _allgather_big(sb[6:7], "allgather_ffn_down", COLLECTIVE_GATHER)
    convw_g = _allgather_small(jnp.pad(conv_w.reshape(CONV_K, cws), ((0, 8 - CONV_K), (0, 0))), "allgather_conv_w")
    conv_w_full = convw_g[:, :CONV_K, :].transpose(1, 0, 2).reshape(CONV_K, sw)
    wa = wa_g.transpose(1, 0, 2).reshape(sw, d)
    wpool = wpool_g.reshape(N_DEV, n_groups, gw, go // N_DEV).transpose(1, 2, 0, 3).reshape(n_groups, gw, go)
    wo = wo_g.reshape(d, d)

    h = _rms_fwd(x2d, norm1_g)
    proj = _proj_fwd(h, win_g)
    z = _conv_fwd(proj, conv_w_full, conv_b)
    p = _pool_fwd(proj)
    ya, yb, merged = _merge_fwd(z, wa, p, wpool, proj, b_gate2, pool_scale)
    x1, h2 = _wo_fwd(merged, wo, x2d, norm2_g)
    gact, uact, act = _ffn_up_act_fwd(h2, wg_g, wu_g)
    ffn_out = _ffn_down_fwd(act, wd_g)
    dx2, dx2b, d_final_g, loss_blk = _loss_bwd(ffn_out, x1, target, final_g2)

    chip_slots = (2 * jnp.arange(4, dtype=jnp.int32) + ci).astype(jnp.int32)

    def partials(grads, recvs, names):
        return [_chip_partial(chip_slots, g3, r, "chip_partial_" + nm) for g3, r, nm in zip(grads, recvs, names)]

    own = jnp.stack([me, my_chip]).astype(jnp.int32)

    def adam(a, g3, sib, chips):
        outs = _adam_big(own, big_w2[a], big_m[a].reshape(shapes2d[a]), big_v[a].reshape(shapes2d[a]),
                         g3, sib, chips, "adam_" + big_names[a])
        return [o.reshape(big_w[a].shape) for o in outs]

    big_out = [None] * len(big_names)
    dg_act, du_act = _ffn_gate_bwd(dx2b, wd_g, gact, uact)
    gw_gate = _wgrad_cols(h2, dg_act, "wgrad_ffn_gate")
    gw_up = _wgrad_cols(h2, du_act, "wgrad_ffn_up")
    sib_gu = _exchange_sibling([gw_gate, gw_up], "rs_sibling_ffn_up", COLLECTIVE_SIBLING)
    ps_gu = partials([gw_gate, gw_up], sib_gu, ["w_ffn_gate", "w_ffn_up"])
    chips_gu = _exchange_chips(ps_gu, "rs_chips_ffn_up", COLLECTIVE_CHIPS)
    gw_down = _wgrad_rows(act, dx2b, "wgrad_ffn_down", after=ps_gu)
    sib_down = _exchange_sibling([gw_down], "rs_sibling_ffn_down", COLLECTIVE_SIBLING)
    ps_down = partials([gw_down], sib_down, ["w_ffn_down"])
    chips_down = _exchange_chips(ps_down, "rs_chips_ffn_down", COLLECTIVE_CHIPS)
    dh2 = _input_grad([(dg_act, wg_g), (du_act, wu_g)], "ffn_in_bwd", after=ps_down)
    dx1, dx1b, d_norm2_g = _rms_bwd(dh2, x1, norm2_g, dx2, "rms2_bwd")
    dya, dyb, dproj42, d_b_gate, d_pool_scale = _wo_bwd(dx1b, wo, ya, yb, proj, b_gate2, pool_scale)
    gw_o = _wgrad_full(merged, dx1b, "wgrad_o")
    dproj = dproj42.reshape(N_DEV, s, sw)
    dproj, d_conv_w, d_conv_b = _conv_bwd(dproj, dya, wa, proj, conv_w_full, conv_b)
    dproj = _pool_bwd(dproj, dyb, wpool)
    gw_a = _wgrad_full(z, dya, "wgrad_a_out")
    gw_pool = _wgrad_pool(p, dyb, n_groups)
    mix3 = [gw_a.reshape(sw, N_DEV, d // N_DEV).transpose(1, 0, 2),
            gw_pool.reshape(n_groups, gw, N_DEV, go // N_DEV).transpose(2, 0, 1, 3).reshape(N_DEV, n_groups * gw, go // N_DEV),
            gw_o.reshape(N_DEV, d // N_DEV, d)]
    sib_mix = _exchange_sibling(mix3, "rs_sibling_mixer", COLLECTIVE_SIBLING)
    ps_mix = partials(mix3, sib_mix, ["w_a_out", "w_pool", "w_o"])
    chips_mix = _exchange_chips(ps_mix, "rs_chips_mixer", COLLECTIVE_CHIPS)
    big_out[4] = adam(4, gw_gate, sib_gu[0], chips_gu[0])
    big_out[5] = adam(5, gw_up, sib_gu[1], chips_gu[1])
    gw_in = _wgrad_cols(h, dproj, "wgrad_in", after=ps_mix + [big_out[4][0], big_out[5][0]])
    sib_in = _exchange_sibling([gw_in], "rs_sibling_w_in", COLLECTIVE_SIBLING)
    big_out[6] = adam(6, gw_down, sib_down[0], chips_down[0])
    ps_in = partials([gw_in], sib_in, ["w_in"])
    chips_in = _exchange_chips(ps_in, "rs_chips_w_in", COLLECTIVE_CHIPS)
    dh = _input_grad([(dproj, win_g)], "proj_in_bwd", after=ps_mix + ps_in + [big_out[6][0]])
    grad_x, _, d_norm1_g = _rms_bwd(dh, x2d, norm1_g, dx1, "rms1_bwd")
    for k in range(3):
        big_out[1 + k] = adam(1 + k, mix3[k], sib_mix[k], chips_mix[k])
    big_out[0] = adam(0, gw_in, sib_in[0], chips_in[0])

    small_parts = [d_norm1_g, d_b_gate, d_conv_w, d_conv_b, d_pool_scale, d_norm2_g, d_final_g, loss_blk]
    sizes = [v.size for v in small_parts]
    packed = jnp.concatenate([_rows128(v) for v in small_parts], axis=0)
    summed = _sum_small(_allgather_small(packed, "allgather_small_grads")).reshape(-1)
    offs = [0]
    for n in sizes:
        offs.append(offs[-1] + n)
    g_norm1, g_bgate, g_convw_full, g_convb, g_pscale, g_norm2, g_final, loss_sum = [
        summed[offs[k]:offs[k + 1]] for k in range(len(sizes))]
    loss = loss_sum[0]
    g_convw = lax.dynamic_slice(g_convw_full.reshape(CONV_K, sw), (0, me * cws), (CONV_K, cws))
    small_w = [norm1_g, b_gate, conv_w, conv_b, pool_scale, norm2_g, final_g]
    small_m = [m_norm1_g, m_b_gate, m_conv_w, m_conv_b, m_pool_scale, m_norm2_g, m_final_g]
    small_v = [v_norm1_g, v_b_gate, v_conv_w, v_conv_b, v_pool_scale, v_norm2_g, v_final_g]
    small_g = [g_norm1, g_bgate, g_convw, g_convb, g_pscale, g_norm2, g_final]

    def pack(parts):
        flat = jnp.concatenate([v.reshape(-1) for v in parts])
        pad = (-flat.size) % (8 * LANES)
        return jnp.pad(flat, (0, pad)).reshape(-1, LANES)

    s_delta, s_m, s_v = _adam_small(pack(small_w), pack(small_g), pack(small_m), pack(small_v))
    soffs = [0]
    for w in small_w:
        soffs.append(soffs[-1] + w.size)

    def unpack(buf):
        flat = buf.reshape(-1)
        return [flat[soffs[k]:soffs[k + 1]].reshape(small_w[k].shape) for k in range(len(small_w))]

    small_grads = [g.reshape(w.shape) for g, w in zip(small_g, small_w)]
    small_delta, small_new_m, small_new_v = unpack(s_delta), unpack(s_m), unpack(s_v)

    order = ["norm1_g", "w_in", "b_gate", "conv_w", "conv_b", "w_a_out", "w_pool", "pool_scale", "w_o", "norm2_g",
             "w_ffn_gate", "w_ffn_up", "w_ffn_down", "final_g"]
    small_names = ["norm1_g", "b_gate", "conv_w", "conv_b", "pool_scale", "norm2_g", "final_g"]
    per_kind = [{}, {}, {}, {}]
    for a, nm in enumerate(big_names):
        for kind in range(4):
            per_kind[kind][nm] = big_out[a][kind]
    for k, nm in enumerate(small_names):
        per_kind[0][nm] = small_grads[k]
        per_kind[1][nm] = small_delta[k]
        per_kind[2][nm] = small_new_m[k]
        per_kind[3][nm] = small_new_v[k]
    result = [loss, grad_x.reshape(x.shape)]
    for kind in range(4):
        result += [per_kind[kind][nm] for nm in order]
    return tuple(result)
```

```python
import functools

import jax
import jax.numpy as jnp
from jax import lax
from jax.experimental import pallas as pl
from jax.experimental.pallas import tpu as pltpu
from jax.experimental.pallas import tpu_sc as plsc

F32 = jnp.float32
BF16 = jnp.bfloat16
MESH = pl.DeviceIdType.MESH

N_DEV = 8
EPS = 1e-6
CONV_K = 3
POOL_WINDOWS = (2, 4, 8, 16)
ADAM_LR = 0.001
ADAM_B1 = 0.9
ADAM_B2 = 0.999
ADAM_EPS = 1e-08
ADAM_WD = 0.01
ADAM_STEP = 10

V7X_VMEM_LIMIT_BYTES = 56 * 1024 * 1024
LANES = 128

COLLECTIVE_GATHER = 1
COLLECTIVE_SIBLING = 2
COLLECTIVE_CHIPS = 3
SEQUENCER_COST_BYTES = 4 * 10**9

NN = ((1,), (0,))
NT = ((1,), (1,))
TN = ((0,), (0,))


def _dot(a, b, dims):
    return lax.dot_general(a, b, (dims, ((), ())), preferred_element_type=F32)


def _cp(n_axes):
    return pltpu.CompilerParams(dimension_semantics=("arbitrary",) * n_axes,
                                vmem_limit_bytes=V7X_VMEM_LIMIT_BYTES)


def _row_tile(rows, bytes_per_row, cap_bytes):
    best = None
    for t in range(16, rows + 1, 16):
        if rows % t == 0 and t * bytes_per_row <= cap_bytes:
            best = t
    return best if best is not None else rows


def _chunks(total, size):
    size = min(size, total)
    assert total % size == 0
    return [slice(r, r + size) for r in range(0, total, size)]


def _after_specs(after):
    return [pl.BlockSpec(memory_space=pl.ANY)] * len(after)


def _shift_down(v, k):
    row = lax.broadcasted_iota(jnp.int32, v.shape, 0)
    return jnp.where(row >= k, pltpu.roll(v, k, 0), 0.0)


def _shift_up(v, k):
    n = v.shape[0]
    row = lax.broadcasted_iota(jnp.int32, v.shape, 0)
    return jnp.where(row < n - k, pltpu.roll(v, n - k, 0), 0.0)


def _sigmoid(v):
    return jax.nn.sigmoid(v)


def _cast_bf16(w2d, name):
    rows, cols = w2d.shape
    tr = _row_tile(rows, cols * 4, 2 << 20)

    def body(i_ref, o_ref):
        o_ref[...] = i_ref[...].astype(BF16)

    return pl.pallas_call(
        body, name=name, grid=(rows // tr,),
        in_specs=[pl.BlockSpec((tr, cols), lambda i: (i, 0))],
        out_specs=pl.BlockSpec((tr, cols), lambda i: (i, 0)),
        out_shape=jax.ShapeDtypeStruct((rows, cols), BF16),
        compiler_params=_cp(1),
    )(w2d)


def _rms_fwd(x2d, g):
    s, d = x2d.shape
    tm = min(256, s)

    def body(x_ref, g_ref, h_ref):
        xv = x_ref[...]
        r = lax.rsqrt(jnp.mean(xv * xv, axis=-1, keepdims=True) + EPS)
        h_ref[...] = (xv * r * g_ref[...]).astype(BF16)

    return pl.pallas_call(
        body, name="rms1_fwd", grid=(s // tm,),
        in_specs=[pl.BlockSpec((tm, d), lambda i: (i, 0)), pl.BlockSpec((1, d), lambda i: (0, 0))],
        out_specs=pl.BlockSpec((tm, d), lambda i: (i, 0)),
        out_shape=jax.ShapeDtypeStruct((s, d), BF16),
        compiler_params=_cp(1),
    )(x2d, g)


def _coords():
    return lax.axis_index("x"), lax.axis_index("y"), lax.axis_index("c")


def _slot(p):
    return 4 * p[0] + 2 * p[1] + p[2]


def _handshake(peers):
    barrier = pltpu.get_barrier_semaphore()
    for peer in peers:
        pl.semaphore_signal(barrier, inc=1, device_id=peer, device_id_type=MESH)
    pl.semaphore_wait(barrier, len(peers))


def _sequencer_call(body, out_type, scratch_types, name, collective_id):
    return pl.kernel(
        body, out_type=out_type, name=name,
        mesh=plsc.ScalarSubcoreMesh(axis_name="seq", num_cores=1),
        scratch_types=scratch_types,
        cost_estimate=pl.CostEstimate(flops=0, transcendentals=0, bytes_accessed=SEQUENCER_COST_BYTES),
        compiler_params=pltpu.CompilerParams(collective_id=collective_id))


def _allgather_big(shards, name, collective_id, after=()):
    n = len(shards)

    def body(*refs):
        ins, outs = refs[:n], refs[n + len(after):2 * n + len(after)]
        send_sems, recv_sems, local_sems = refs[2 * n + len(after):]
        x, y, c = _coords()
        me, sibling = (x, y, c), (x, y, 1 - c)
        x_nbr, y_nbr, diag = (1 - x, y), (x, 1 - y), (1 - x, 1 - y)
        relay_from = (x + (1 - c) * (1 - 2 * x), y + c * (1 - 2 * y))
        relay_to = (x + c * (1 - 2 * x), y + (1 - c) * (1 - 2 * y))
        _handshake([sibling, (*x_nbr, c), (*y_nbr, c)])

        def copy(a, k, block, to, src=None):
            dst = outs[a].at[_slot(block)]
            return pltpu.make_async_remote_copy(
                src_ref=dst if src is None else src, dst_ref=dst,
                send_sem=send_sems.at[a, k], recv_sem=recv_sems.at[a, k],
                device_id=to, device_id_type=MESH)

        mine, sends = [], []
        for a in range(n):
            cp = pltpu.make_async_copy(ins[a], outs[a].at[_slot(me)], local_sems.at[a])
            cp.start()
            mine.append(cp)
            first = [copy(a, 0, me, sibling, src=ins[a]),
                     copy(a, 1, me, (*x_nbr, c), src=ins[a]),
                     copy(a, 2, me, (*y_nbr, c), src=ins[a])]
            for cp in first:
                cp.start()
            sends += first
        for a in range(n):
            copy(a, 1 + c, (*relay_from, c), me).wait_recv()
            passed = [copy(a, 3, (*relay_from, c), (*relay_to, c)), copy(a, 4 + c, (*relay_from, c), sibling)]
            for cp in passed:
                cp.start()
            copy(a, 2 - c, (*relay_to, c), me).wait_recv()
            cp = copy(a, 5 - c, (*relay_to, c), sibling)
            cp.start()
            passed.append(cp)
            copy(a, 3, (*diag, c), me).wait_recv()
            cp = copy(a, 6, (*diag, c), sibling)
            cp.start()
            sends += passed + [cp]
        for a in range(n):
            copy(a, 0, sibling, me).wait_recv()
            copy(a, 4, (*x_nbr, 1 - c), me).wait_recv()
            copy(a, 5, (*y_nbr, 1 - c), me).wait_recv()
            copy(a, 6, (*diag, 1 - c), me).wait_recv()
        for cp in sends:
            cp.wait_send()
        for cp in mine:
            cp.wait()

    return _sequencer_call(
        body, [jax.ShapeDtypeStruct((N_DEV,) + s.shape, s.dtype) for s in shards],
        [pltpu.SemaphoreType.DMA((n, 7)), pltpu.SemaphoreType.DMA((n, 7)), pltpu.SemaphoreType.DMA((n,))],
        name, collective_id)(*shards, *after)


def _exchange_sibling(grads, name, collective_id):
    n = len(grads)

    def body(*refs):
        ins, outs = refs[:n], refs[n:2 * n]
        send_sems, recv_sems = refs[2 * n:]
        x, y, c = _coords()
        sibling = (x, y, 1 - c)
        _handshake([sibling])
        copies = []
        for a in range(n):
            for q in range(4):
                cp = pltpu.make_async_remote_copy(
                    src_ref=ins[a].at[2 * q + (1 - c)], dst_ref=outs[a].at[q],
                    send_sem=send_sems.at[a, q], recv_sem=recv_sems.at[a, q],
                    device_id=sibling, device_id_type=MESH)
                cp.start()
                copies.append(cp)
        for cp in copies:
            cp.wait_recv()
        for cp in copies:
            cp.wait_send()

    any_spec = pl.BlockSpec(memory_space=pl.ANY)
    return pl.pallas_call(
        body, name=name,
        in_specs=[any_spec] * n, out_specs=[any_spec] * n,
        out_shape=[jax.ShapeDtypeStruct((4,) + g.shape[1:], g.dtype) for g in grads],
        scratch_shapes=[pltpu.SemaphoreType.DMA((n, 4)), pltpu.SemaphoreType.DMA((n, 4))],
        compiler_params=pltpu.CompilerParams(collective_id=collective_id),
    )(*grads)


def _exchange_chips(psums, name, collective_id):
    n = len(psums)

    def body(*refs):
        ins, outs = refs[:n], refs[n:2 * n]
        send_sems, recv_sems = refs[2 * n:]
        x, y, c = _coords()
        chips = [(1 - x, y), (x, 1 - y), (1 - x, 1 - y)]
        _handshake([(*chip, c) for chip in chips])
        copies = []
        for a in range(n):
            for j, chip in enumerate(chips):
                cp = pltpu.make_async_remote_copy(
                    src_ref=ins[a].at[2 * chip[0] + chip[1]], dst_ref=outs[a].at[j],
                    send_sem=send_sems.at[a, j], recv_sem=recv_sems.at[a, j],
                    device_id=(*chip, c), device_id_type=MESH)
                cp.start()
                copies.append(cp)
        for cp in copies:
            cp.wait_recv()
        for cp in copies:
            cp.wait_send()

    return _sequencer_call(
        body, [jax.ShapeDtypeStruct((3,) + p.shape[1:], p.dtype) for p in psums],
        [pltpu.SemaphoreType.DMA((n, 3)), pltpu.SemaphoreType.DMA((n, 3))],
        name, collective_id)(*psums)


def _allgather_small(v2d, name):
    rows, cols = v2d.shape

    def body(v_ref, out_ref, send_sems, recv_sems):
        x, y, c = _coords()
        me = (x, y, c)
        out_ref[_slot(me)] = v_ref[...]
        peers = []
        for k in range(1, N_DEV):
            fx, fy, fc = (k >> 2) & 1, (k >> 1) & 1, k & 1
            peers.append(((1 - x) if fx else x, (1 - y) if fy else y, (1 - c) if fc else c))
        sends = []
        for k, peer in enumerate(peers):
            cp = pltpu.make_async_remote_copy(
                src_ref=v_ref, dst_ref=out_ref.at[_slot(me)],
                send_sem=send_sems.at[k], recv_sem=recv_sems.at[k],
                device_id=peer, device_id_type=MESH)
            cp.start()
            sends.append(cp)
        for k, peer in enumerate(peers):
            pltpu.make_async_remote_copy(
                src_ref=v_ref, dst_ref=out_ref.at[_slot(peer)],
                send_sem=send_sems.at[k], recv_sem=recv_sems.at[k],
                device_id=peer, device_id_type=MESH).wait_recv()
        for cp in sends:
            cp.wait_send()

    vmem = pl.BlockSpec(memory_space=pltpu.VMEM)
    return pl.pallas_call(
        body, name=name, in_specs=[vmem], out_specs=vmem,
        out_shape=jax.ShapeDtypeStruct((N_DEV, rows, cols), v2d.dtype),
        scratch_shapes=[pltpu.SemaphoreType.DMA((N_DEV - 1,)), pltpu.SemaphoreType.DMA((N_DEV - 1,))],
    )(v2d)


def _chip_partial(slots, g3, recv, name):
    _, rows, cols = g3.shape
    tr = _row_tile(rows, cols * 2, 2 << 20)

    def body(slots_ref, g_ref, r_ref, o_ref):
        o_ref[...] = (g_ref[...].astype(F32) + r_ref[...].astype(F32)).astype(BF16)

    return pl.pallas_call(
        body, name=name,
        grid_spec=pltpu.PrefetchScalarGridSpec(
            num_scalar_prefetch=1, grid=(4, rows // tr),
            in_specs=[pl.BlockSpec((None, tr, cols), lambda q, i, sl: (sl[q], i, 0)),
                      pl.BlockSpec((None, tr, cols), lambda q, i, sl: (q, i, 0))],
            out_specs=pl.BlockSpec((None, tr, cols), lambda q, i, sl: (q, i, 0))),
        out_shape=jax.ShapeDtypeStruct((4, rows, cols), BF16),
        compiler_params=_cp(2),
    )(slots, g3, recv)


def _adam_math(w, g, m, v):
    m = ADAM_B1 * m + (1.0 - ADAM_B1) * g
    v = ADAM_B2 * v + (1.0 - ADAM_B2) * (g * g)
    m_hat = m / (1.0 - ADAM_B1 ** ADAM_STEP)
    v_hat = v / (1.0 - ADAM_B2 ** ADAM_STEP)
    delta = -ADAM_LR * (m_hat / (jnp.sqrt(v_hat) + ADAM_EPS) + ADAM_WD * w)
    return delta, m, v


def _adam_big(own, w, m, v, g3, recv_sib, recv_chips, name):
    rows, cols = w.shape
    tr = _row_tile(rows, cols * 4, 2 << 20)

    def body(own_ref, w_ref, m_ref, v_ref, g_ref, rs_ref, rc_ref, go_ref, do_ref, mo_ref, vo_ref):
        g = g_ref[...].astype(F32) + rs_ref[...].astype(F32)
        g = g + rc_ref[0].astype(F32)
        g = g + rc_ref[1].astype(F32)
        g = g + rc_ref[2].astype(F32)
        delta, m_new, v_new = _adam_math(w_ref[...], g, m_ref[...], v_ref[...])
        go_ref[...] = g
        do_ref[...] = delta
        mo_ref[...] = m_new
        vo_ref[...] = v_new

    blk = pl.BlockSpec((tr, cols), lambda i, o: (i, 0))
    out = jax.ShapeDtypeStruct((rows, cols), F32)
    return pl.pallas_call(
        body, name=name,
        grid_spec=pltpu.PrefetchScalarGridSpec(
            num_scalar_prefetch=1, grid=(rows // tr,),
            in_specs=[blk, blk, blk,
                      pl.BlockSpec((None, tr, cols), lambda i, o: (o[0], i, 0)),
                      pl.BlockSpec((None, tr, cols), lambda i, o: (o[1], i, 0)),
                      pl.BlockSpec((3, tr, cols), lambda i, o: (0, i, 0))],
            out_specs=[blk, blk, blk, blk]),
        out_shape=[out, out, out, out],
        compiler_params=_cp(1),
    )(own, w, m, v, g3, recv_sib, recv_chips)


def _sum_small(gathered):
    _, rows, cols = gathered.shape

    def body(g_ref, o_ref):
        acc = g_ref[0]
        for k in range(1, N_DEV):
            acc = acc + g_ref[k]
        o_ref[...] = acc

    vmem = pl.BlockSpec(memory_space=pltpu.VMEM)
    return pl.pallas_call(body, name="small_grad_sum", in_specs=[vmem], out_specs=vmem,
                          out_shape=jax.ShapeDtypeStruct((rows, cols), F32))(gathered)


def _adam_small(w, g, m, v):
    def body(w_ref, g_ref, m_ref, v_ref, do_ref, mo_ref, vo_ref):
        delta, m_new, v_new = _adam_math(w_ref[...], g_ref[...], m_ref[...], v_ref[...])
        do_ref[...] = delta
        mo_ref[...] = m_new
        vo_ref[...] = v_new

    vmem = pl.BlockSpec(memory_space=pltpu.VMEM)
    out = jax.ShapeDtypeStruct(w.shape, F32)
    return pl.pallas_call(body, name="adam_small", in_specs=[vmem] * 4, out_specs=[vmem] * 3,
                          out_shape=[out, out, out])(w, g, m, v)


def _proj_fwd(h, win_g):
    s, d = h.shape
    sw = win_g.shape[2]
    tn = min(512, sw)
    nh = sw // tn

    def body(h_ref, w_ref, o_ref):
        for rs in _chunks(s, 512):
            o_ref[rs, :] = _dot(h_ref[rs, :], w_ref[...], NN)

    return pl.pallas_call(
        body, name="proj_fwd", grid=(N_DEV * nh,),
        in_specs=[pl.BlockSpec((s, d), lambda j: (0, 0)),
                  pl.BlockSpec((None, d, tn), lambda j: (j // nh, 0, j % nh))],
        out_specs=pl.BlockSpec((None, s, tn), lambda j: (j // nh, 0, j % nh)),
        out_shape=jax.ShapeDtypeStruct((N_DEV, s, sw), F32),
        compiler_params=_cp(1),
    )(h, win_g)


def _conv_fwd(proj, conv_w, conv_b):
    _, s, sw = proj.shape
    tc = min(LANES, sw)

    def body(ba_ref, ca_ref, va_ref, cw_ref, cb_ref, z_ref):
        cv = ca_ref[...] * va_ref[...]
        u = (cb_ref[...] + cw_ref[0:1, :] * _shift_down(cv, 2) + cw_ref[1:2, :] * _shift_down(cv, 1)
             + cw_ref[2:3, :] * cv)
        z_ref[...] = (ba_ref[...] * u).astype(BF16)

    def part(k):
        return pl.BlockSpec((None, s, tc), lambda i: (k, 0, i))

    return pl.pallas_call(
        body, name="conv_fwd", grid=(sw // tc,),
        in_specs=[part(0), part(1), part(2),
                  pl.BlockSpec((CONV_K, tc), lambda i: (0, i)), pl.BlockSpec((1, tc), lambda i: (0, i))],
        out_specs=pl.BlockSpec((s, tc), lambda i: (0, i)),
        out_shape=jax.ShapeDtypeStruct((s, sw), BF16),
        compiler_params=_cp(1),
    )(proj, proj, proj, conv_w, conv_b)


def _pool_counts(shape, window):
    t = lax.broadcasted_iota(jnp.int32, shape, 0)
    return jnp.minimum(t + 1, window).astype(F32)


def _pool_fwd(proj):
    _, s, sw = proj.shape
    gw = sw // len(POOL_WINDOWS)

    def body(v_ref, p_ref):
        for gi, window in enumerate(POOL_WINDOWS):
            @pl.when(pl.program_id(0) == gi)
            def _():
                v = v_ref[...]
                acc, k = v, 1
                while k < window:
                    acc = acc + _shift_down(acc, k)
                    k *= 2
                p_ref[...] = (acc / _pool_counts(v.shape, window) - v).astype(BF16)

    return pl.pallas_call(
        body, name="pool_fwd", grid=(len(POOL_WINDOWS),),
        in_specs=[pl.BlockSpec((None, s, gw), lambda g: (3, 0, g))],
        out_specs=pl.BlockSpec((s, gw), lambda g: (0, g)),
        out_shape=jax.ShapeDtypeStruct((s, sw), BF16),
        compiler_params=_cp(1),
    )(proj)


def _merge_fwd(z, wa, p, wpool, proj, b_gate2, pool_scale):
    s, sw = z.shape
    d = wa.shape[1]
    tn = d // N_DEV
    gw = sw // len(POOL_WINDOWS)
    nq = sw // tn

    def body(z_ref, wa_ref, p_ref, wp_ref, ga_ref, gb_ref, bg_ref, sc_ref, ya_ref, yb_ref, m_ref):
        for rs in _chunks(s, 512):
            ya = _dot(z_ref[rs, :], wa_ref[...], NN)
            yb = _dot(p_ref[rs, :], wp_ref[...], NN)
            sa = _sigmoid(ga_ref[rs, :] + bg_ref[0:1, :])
            sb = _sigmoid(gb_ref[rs, :] + bg_ref[1:2, :])
            ya_ref[rs, :] = ya.astype(BF16)
            yb_ref[rs, :] = yb.astype(BF16)
            m_ref[rs, :] = (sa * ya + sb * (yb * sc_ref[...])).astype(BF16)

    col = pl.BlockSpec((s, tn), lambda j: (0, j))
    out = jax.ShapeDtypeStruct((s, d), BF16)
    return pl.pallas_call(
        body, name="merge_fwd", grid=(N_DEV,),
        in_specs=[pl.BlockSpec((s, sw), lambda j: (0, 0)),
                  pl.BlockSpec((sw, tn), lambda j: (0, j)),
                  pl.BlockSpec((s, gw), lambda j: (0, j // 2)),
                  pl.BlockSpec((None, gw, tn), lambda j: (j // 2, 0, j % 2)),
                  pl.BlockSpec((None, s, tn), lambda j: (4 + j // nq, 0, j % nq)),
                  pl.BlockSpec((None, s, tn), lambda j: (6 + j // nq, 0, j % nq)),
                  pl.BlockSpec((2, tn), lambda j: (0, j)),
                  pl.BlockSpec((1, tn), lambda j: (0, j))],
        out_specs=[col, col, col],
        out_shape=[out, out, out],
        compiler_params=_cp(1),
    )(z, wa, p, wpool, proj, proj, b_gate2, pool_scale)


def _wo_fwd(merged, wo, x2d, g2):
    s, d = x2d.shape
    tm = min(256, s)

    def body(m_ref, wo_ref, x_ref, g_ref, x1_ref, h2_ref):
        x1 = x_ref[...] + _dot(m_ref[...], wo_ref[...], NN)
        x1_ref[...] = x1
        r = lax.rsqrt(jnp.mean(x1 * x1, axis=-1, keepdims=True) + EPS)
        h2_ref[...] = (x1 * r * g_ref[...]).astype(BF16)

    row = pl.BlockSpec((tm, d), lambda i: (i, 0))
    return pl.pallas_call(
        body, name="wo_fwd", grid=(s // tm,),
        in_specs=[row, pl.BlockSpec((d, d), lambda i: (0, 0)), row, pl.BlockSpec((1, d), lambda i: (0, 0))],
        out_specs=[row, row],
        out_shape=[jax.ShapeDtypeStruct((s, d), F32), jax.ShapeDtypeStruct((s, d), BF16)],
        compiler_params=_cp(1),
    )(merged, wo, x2d, g2)


def _ffn_up_fwd(h2, wg_g, wu_g):
    s, d = h2.shape
    f8 = wg_g.shape[2]

    def body(h_ref, wg_ref, wu_ref, g_ref, u_ref):
        for rs in _chunks(s, 512):
            a = h_ref[rs, :]
            g_ref[rs, :] = _dot(a, wg_ref[...], NN).astype(BF16)
            u_ref[rs, :] = _dot(a, wu_ref[...], NN).astype(BF16)

    wspec = pl.BlockSpec((None, d, f8), lambda j: (j, 0, 0))
    ospec = pl.BlockSpec((None, s, f8), lambda j: (j, 0, 0))
    out = jax.ShapeDtypeStruct((N_DEV, s, f8), BF16)
    return pl.pallas_call(
        body, name="ffn_up_fwd", grid=(N_DEV,),
        in_specs=[pl.BlockSpec((s, d), lambda j: (0, 0)), wspec, wspec],
        out_specs=[ospec, ospec], out_shape=[out, out],
        compiler_params=_cp(1),
    )(h2, wg_g, wu_g)


def _ffn_down_loss(gact, uact, wd_g, x1, target, final_g):
    _, s, f8 = gact.shape
    d = x1.shape[1]
    tm = min(256, s)
    last = N_DEV - 1

    def body(g_ref, u_ref, wd_ref, x1_ref, t_ref, gf_ref, dx_ref, dxb_ref, dgf_ref, loss_ref, acc_ref):
        i, j = pl.program_id(0), pl.program_id(1)

        @pl.when(j == 0)
        def _():
            acc_ref[...] = jnp.zeros_like(acc_ref)

        @pl.when((i == 0) & (j == 0))
        def _():
            dgf_ref[...] = jnp.zeros_like(dgf_ref)
            loss_ref[...] = jnp.zeros_like(loss_ref)

        for rs in _chunks(tm, 256):
            g = g_ref[rs, :].astype(F32)
            act = (g * _sigmoid(g) * u_ref[rs, :].astype(F32)).astype(BF16)
            acc_ref[rs, :] += _dot(act, wd_ref[...], NN)

        @pl.when(j == last)
        def _():
            for rs in _chunks(tm, 256):
                x2 = x1_ref[rs, :] + acc_ref[rs, :]
                r = lax.rsqrt(jnp.mean(x2 * x2, axis=-1, keepdims=True) + EPS)
                nrm = x2 * r
                gf = gf_ref[...]
                err = nrm * gf - t_ref[rs, :]
                loss_ref[...] += jnp.sum(err * err) * (0.5 / d)
                dy = err * (1.0 / d)
                dgf_ref[...] += jnp.sum(dy * nrm, axis=0, keepdims=True)
                dn = dy * gf
                dx = r * (dn - nrm * jnp.mean(dn * nrm, axis=-1, keepdims=True))
                dx_ref[rs, :] = dx
                dxb_ref[rs, :] = dx.astype(BF16)

    aspec = pl.BlockSpec((None, tm, f8), lambda i, j: (j, i, 0))
    row = pl.BlockSpec((tm, d), lambda i, j: (i, 0))
    return pl.pallas_call(
        body, name="ffn_down_loss", grid=(s // tm, N_DEV),
        in_specs=[aspec, aspec, pl.BlockSpec((None, f8, d), lambda i, j: (j, 0, 0)), row, row,
                  pl.BlockSpec((1, d), lambda i, j: (0, 0))],
        out_specs=[row, row, pl.BlockSpec((1, d), lambda i, j: (0, 0)),
                   pl.BlockSpec((8, LANES), lambda i, j: (0, 0))],
        out_shape=[jax.ShapeDtypeStruct((s, d), F32), jax.ShapeDtypeStruct((s, d), BF16),
                   jax.ShapeDtypeStruct((1, d), F32), jax.ShapeDtypeStruct((8, LANES), F32)],
        scratch_shapes=[pltpu.VMEM((tm, d), F32)],
        compiler_params=_cp(2),
    )(gact, uact, wd_g, x1, target, final_g)


def _ffn_act_bwd(dx2b, wd_g, gact, uact):
    s, d = dx2b.shape
    f8 = gact.shape[2]
    tm = min(1024, s)

    def body(dx_ref, wd_ref, g_ref, u_ref, dg_ref, du_ref, act_ref):
        for rs in _chunks(tm, 256):
            da = _dot(dx_ref[rs, :], wd_ref[...], NT)
            g = g_ref[rs, :].astype(F32)
            u = u_ref[rs, :].astype(F32)
            sg = _sigmoid(g)
            silu = g * sg
            act_ref[rs, :] = (silu * u).astype(BF16)
            du_ref[rs, :] = (da * silu).astype(BF16)
            dg_ref[rs, :] = (da * u * (sg * (1.0 + g * (1.0 - sg)))).astype(BF16)

    aspec = pl.BlockSpec((None, tm, f8), lambda j, i: (j, i, 0))
    out = jax.ShapeDtypeStruct((N_DEV, s, f8), BF16)
    return pl.pallas_call(
        body, name="ffn_act_bwd", grid=(N_DEV, s // tm),
        in_specs=[pl.BlockSpec((tm, d), lambda j, i: (i, 0)),
                  pl.BlockSpec((None, f8, d), lambda j, i: (j, 0, 0)), aspec, aspec],
        out_specs=[aspec, aspec, aspec], out_shape=[out, out, out],
        compiler_params=_cp(2),
    )(dx2b, wd_g, gact, uact)


def _wgrad_shard_a(a3, b, name, after=()):
    _, s, k = a3.shape
    n = b.shape[1]
    ts = min(512, s)
    ns = s // ts

    def body(a_ref, b_ref, *rest):
        o_ref, acc_ref = rest[len(after):]
        i = pl.program_id(1)

        @pl.when(i == 0)
        def _():
            acc_ref[...] = jnp.zeros_like(acc_ref)

        acc_ref[...] += _dot(a_ref[...], b_ref[...], TN)

        @pl.when(i == ns - 1)
        def _():
            o_ref[...] = acc_ref[...].astype(BF16)

    return pl.pallas_call(
        body, name=name, grid=(N_DEV, ns),
        in_specs=[pl.BlockSpec((None, ts, k), lambda j, i: (j, i, 0)),
                  pl.BlockSpec((ts, n), lambda j, i: (i, 0))] + _after_specs(after),
        out_specs=pl.BlockSpec((None, k, n), lambda j, i: (j, 0, 0)),
        out_shape=jax.ShapeDtypeStruct((N_DEV, k, n), BF16),
        scratch_shapes=[pltpu.VMEM((k, n), F32)],
        compiler_params=_cp(2),
    )(a3, b, *after)


def _wgrad_shard_b(a, b3, name, after=()):
    s, k = a.shape
    n = b3.shape[2]
    ts = min(512, s)
    ns = s // ts

    def body(a_ref, b_ref, *rest):
        o_ref, acc_ref = rest[len(after):]
        i = pl.program_id(1)

        @pl.when(i == 0)
        def _():
            acc_ref[...] = jnp.zeros_like(acc_ref)

        acc_ref[...] += _dot(a_ref[...], b_ref[...], TN)

        @pl.when(i == ns - 1)
        def _():
            o_ref[...] = acc_ref[...].astype(BF16)

    return pl.pallas_call(
        body, name=name, grid=(N_DEV, ns),
        in_specs=[pl.BlockSpec((ts, k), lambda j, i: (i, 0)),
                  pl.BlockSpec((None, ts, n), lambda j, i: (j, i, 0))] + _after_specs(after),
        out_specs=pl.BlockSpec((None, k, n), lambda j, i: (j, 0, 0)),
        out_shape=jax.ShapeDtypeStruct((N_DEV, k, n), BF16),
        scratch_shapes=[pltpu.VMEM((k, n), F32)],
        compiler_params=_cp(2),
    )(a, b3, *after)


def _ffn_up_act_fwd(h2, wg_g, wu_g):
    s, d = h2.shape
    f8 = wg_g.shape[2]
    th = min(1024, s)

    def body(h_ref, wg_ref, wu_ref, g_ref, u_ref, a_ref):
        i = pl.program_id(1)
        for rs in _chunks(th, 512):
            rows = pl.ds(pl.multiple_of(i * th + rs.start, rs.stop - rs.start), rs.stop - rs.start)
            a = h_ref[rows, :]
            g = _dot(a, wg_ref[...], NN)
            u = _dot(a, wu_ref[...], NN)
            g_ref[rs, :] = g.astype(BF16)
            u_ref[rs, :] = u.astype(BF16)
            a_ref[rs, :] = (g * _sigmoid(g) * u).astype(BF16)

    wspec = pl.BlockSpec((None, d, f8), lambda j, i: (j, 0, 0))
    ospec = pl.BlockSpec((None, th, f8), lambda j, i: (j, i, 0))
    out = jax.ShapeDtypeStruct((N_DEV, s, f8), BF16)
    return pl.pallas_call(
        body, name="ffn_up_fwd", grid=(N_DEV, s // th),
        in_specs=[pl.BlockSpec((s, d), lambda j, i: (0, 0)), wspec, wspec],
        out_specs=[ospec, ospec, ospec], out_shape=[out, out, out],
        compiler_params=_cp(2),
    )(h2, wg_g, wu_g)


def _ffn_down_fwd(act, wd_g):
    _, s, f8 = act.shape
    d = wd_g.shape[2]
    tn = min(1024, d)

    def body(a_ref, wd_ref, o_ref):
        j = pl.program_id(1)

        @pl.when(j == 0)
        def _():
            o_ref[...] = jnp.zeros_like(o_ref)

        for rs in _chunks(s, 1024):
            o_ref[rs, :] += _dot(a_ref[rs, :], wd_ref[...], NN)

    return pl.pallas_call(
        body, name="ffn_down_fwd", grid=(d // tn, N_DEV),
        in_specs=[pl.BlockSpec((None, s, f8), lambda n, j: (j, 0, 0)),
                  pl.BlockSpec((None, f8, tn), lambda n, j: (j, 0, n))],
        out_specs=pl.BlockSpec((s, tn), lambda n, j: (0, n)),
        out_shape=jax.ShapeDtypeStruct((s, d), F32),
        compiler_params=_cp(2),
    )(act, wd_g)


def _loss_bwd(ffn_out, x1, target, final_g):
    s, d = x1.shape
    tm = min(256, s)

    def body(f_ref, x1_ref, t_ref, gf_ref, dx_ref, dxb_ref, dgf_ref, loss_ref):
        @pl.when(pl.program_id(0) == 0)
        def _():
            dgf_ref[...] = jnp.zeros_like(dgf_ref)
            loss_ref[...] = jnp.zeros_like(loss_ref)

        x2 = x1_ref[...] + f_ref[...]
        r = lax.rsqrt(jnp.mean(x2 * x2, axis=-1, keepdims=True) + EPS)
        nrm = x2 * r
        gf = gf_ref[...]
        err = nrm * gf - t_ref[...]
        loss_ref[...] += jnp.sum(err * err) * (0.5 / d)
        dy = err * (1.0 / d)
        dgf_ref[...] += jnp.sum(dy * nrm, axis=0, keepdims=True)
        dn = dy * gf
        dx = r * (dn - nrm * jnp.mean(dn * nrm, axis=-1, keepdims=True))
        dx_ref[...] = dx
        dxb_ref[...] = dx.astype(BF16)

    row = pl.BlockSpec((tm, d), lambda i: (i, 0))
    vec = pl.BlockSpec((1, d), lambda i: (0, 0))
    return pl.pallas_call(
        body, name="loss_bwd", grid=(s // tm,),
        in_specs=[row, row, row, vec],
        out_specs=[row, row, vec, pl.BlockSpec((8, LANES), lambda i: (0, 0))],
        out_shape=[jax.ShapeDtypeStruct((s, d), F32), jax.ShapeDtypeStruct((s, d), BF16),
                   jax.ShapeDtypeStruct((1, d), F32), jax.ShapeDtypeStruct((8, LANES), F32)],
        compiler_params=_cp(1),
    )(ffn_out, x1, target, final_g)


def _ffn_gate_bwd(dx2b, wd_g, gact, uact):
    s, d = dx2b.shape
    f8 = gact.shape[2]
    th = min(1024, s)

    def body(dx_ref, wd_ref, g_ref, u_ref, dg_ref, du_ref):
        i = pl.program_id(1)
        for rs in _chunks(th, 512):
            rows = pl.ds(pl.multiple_of(i * th + rs.start, rs.stop - rs.start), rs.stop - rs.start)
            da = _dot(dx_ref[rows, :], wd_ref[...], NT)
            g = g_ref[rs, :].astype(F32)
            u = u_ref[rs, :].astype(F32)
            sg = _sigmoid(g)
            du_ref[rs, :] = (da * (g * sg)).astype(BF16)
            dg_ref[rs, :] = (da * u * (sg * (1.0 + g * (1.0 - sg)))).astype(BF16)

    aspec = pl.BlockSpec((None, th, f8), lambda j, i: (j, i, 0))
    out = jax.ShapeDtypeStruct((N_DEV, s, f8), BF16)
    return pl.pallas_call(
        body, name="ffn_act_bwd", grid=(N_DEV, s // th),
        in_specs=[pl.BlockSpec((s, d), lambda j, i: (0, 0)),
                  pl.BlockSpec((None, f8, d), lambda j, i: (j, 0, 0)), aspec, aspec],
        out_specs=[aspec, aspec], out_shape=[out, out],
        compiler_params=_cp(2),
    )(dx2b, wd_g, gact, uact)


def _wgrad_rows(a3, b, name, after=()):
    _, s, k = a3.shape
    n = b.shape[1]

    def body(a_ref, b_ref, *rest):
        o_ref = rest[len(after)]
        o_ref[...] = _dot(a_ref[...], b_ref[...], TN).astype(BF16)

    return pl.pallas_call(
        body, name=name, grid=(N_DEV,),
        in_specs=[pl.BlockSpec((None, s, k), lambda j: (j, 0, 0)),
                  pl.BlockSpec((s, n), lambda j: (0, 0))] + _after_specs(after),
        out_specs=pl.BlockSpec((None, k, n), lambda j: (j, 0, 0)),
        out_shape=jax.ShapeDtypeStruct((N_DEV, k, n), BF16),
        compiler_params=_cp(1),
    )(a3, b, *after)


def _wgrad_cols(a, b3, name, after=()):
    s, k = a.shape
    n = b3.shape[2]

    def body(a_ref, b_ref, *rest):
        o_ref = rest[len(after)]
        o_ref[...] = _dot(a_ref[...], b_ref[...], TN).astype(BF16)

    return pl.pallas_call(
        body, name=name, grid=(N_DEV,),
        in_specs=[pl.BlockSpec((s, k), lambda j: (0, 0)),
                  pl.BlockSpec((None, s, n), lambda j: (j, 0, 0))] + _after_specs(after),
        out_specs=pl.BlockSpec((None, k, n), lambda j: (j, 0, 0)),
        out_shape=jax.ShapeDtypeStruct((N_DEV, k, n), BF16),
        compiler_params=_cp(1),
    )(a, b3, *after)


def _input_grad(pairs, name, after=()):
    s = pairs[0][0].shape[1]
    d = pairs[0][1].shape[1]
    tn = min(1024, d)
    npair = len(pairs)

    def body(*refs):
        ops = refs[:2 * npair]
        o_ref = refs[2 * npair + len(after)]
        j = pl.program_id(1)

        @pl.when(j == 0)
        def _():
            o_ref[...] = jnp.zeros_like(o_ref)

        for rs in _chunks(s, 1024):
            part = _dot(ops[0][rs, :], ops[1][...], NT)
            for q in range(1, npair):
                part = part + _dot(ops[2 * q][rs, :], ops[2 * q + 1][...], NT)
            o_ref[rs, :] += part

    in_specs, args = [], []
    for a3, w3 in pairs:
        k = a3.shape[2]
        in_specs += [pl.BlockSpec((None, s, k), lambda n, j: (j, 0, 0)),
                     pl.BlockSpec((None, tn, k), lambda n, j: (j, n, 0))]
        args += [a3, w3]
    return pl.pallas_call(
        body, name=name, grid=(d // tn, N_DEV),
        in_specs=in_specs + _after_specs(after),
        out_specs=pl.BlockSpec((s, tn), lambda n, j: (0, n)),
        out_shape=jax.ShapeDtypeStruct((s, d), F32),
        compiler_params=_cp(2),
    )(*args, *after)


def _rms_bwd(dh, xres, g, dres, name):
    s, d = xres.shape
    tm = min(256, s)

    def body(dh_ref, x_ref, g_ref, dres_ref, dx_ref, dxb_ref, dg_ref):
        @pl.when(pl.program_id(0) == 0)
        def _():
            dg_ref[...] = jnp.zeros_like(dg_ref)

        xv = x_ref[...]
        dh_v = dh_ref[...]
        r = lax.rsqrt(jnp.mean(xv * xv, axis=-1, keepdims=True) + EPS)
        nrm = xv * r
        dg_ref[...] += jnp.sum(dh_v * nrm, axis=0, keepdims=True)
        dn = dh_v * g_ref[...]
        dx = dres_ref[...] + r * (dn - nrm * jnp.mean(dn * nrm, axis=-1, keepdims=True))
        dx_ref[...] = dx
        dxb_ref[...] = dx.astype(BF16)

    row = pl.BlockSpec((tm, d), lambda i: (i, 0))
    vec = pl.BlockSpec((1, d), lambda i: (0, 0))
    return pl.pallas_call(
        body, name=name, grid=(s // tm,),
        in_specs=[row, row, vec, row],
        out_specs=[row, row, vec],
        out_shape=[jax.ShapeDtypeStruct((s, d), F32), jax.ShapeDtypeStruct((s, d), BF16),
                   jax.ShapeDtypeStruct((1, d), F32)],
        compiler_params=_cp(1),
    )(dh, xres, g, dres)


def _wgrad_full(a, b, name):
    s, k = a.shape
    n = b.shape[1]
    tk = min(512, k)
    ts = min(512, s)
    ns = s // ts

    def body(a_ref, b_ref, o_ref, acc_ref):
        i = pl.program_id(1)

        @pl.when(i == 0)
        def _():
            acc_ref[...] = jnp.zeros_like(acc_ref)

        acc_ref[...] += _dot(a_ref[...], b_ref[...], TN)

        @pl.when(i == ns - 1)
        def _():
            o_ref[...] = acc_ref[...].astype(BF16)

    return pl.pallas_call(
        body, name=name, grid=(k // tk, ns),
        in_specs=[pl.BlockSpec((ts, tk), lambda j, i: (i, j)),
                  pl.BlockSpec((ts, n), lambda j, i: (i, 0))],
        out_specs=pl.BlockSpec((tk, n), lambda j, i: (j, 0)),
        out_shape=jax.ShapeDtypeStruct((k, n), BF16),
        scratch_shapes=[pltpu.VMEM((tk, n), F32)],
        compiler_params=_cp(2),
    )(a, b)


def _wgrad_pool(p, dyb, n_groups):
    s, sw = p.shape
    d = dyb.shape[1]
    gw, go = sw // n_groups, d // n_groups
    ts = min(512, s)
    ns = s // ts

    def body(a_ref, b_ref, o_ref, acc_ref):
        i = pl.program_id(1)

        @pl.when(i == 0)
        def _():
            acc_ref[...] = jnp.zeros_like(acc_ref)

        acc_ref[...] += _dot(a_ref[...], b_ref[...], TN)

        @pl.when(i == ns - 1)
        def _():
            o_ref[...] = acc_ref[...].astype(BF16)

    return pl.pallas_call(
        body, name="wgrad_pool", grid=(n_groups, ns),
        in_specs=[pl.BlockSpec((ts, gw), lambda g, i: (i, g)),
                  pl.BlockSpec((ts, go), lambda g, i: (i, g))],
        out_specs=pl.BlockSpec((None, gw, go), lambda g, i: (g, 0, 0)),
        out_shape=jax.ShapeDtypeStruct((n_groups, gw, go), BF16),
        scratch_shapes=[pltpu.VMEM((gw, go), F32)],
        compiler_params=_cp(2),
    )(p, dyb)


def _input_grad_rms(pairs, xres, g, dres, name, after=()):
    s, d = xres.shape
    tm = min(256, s)
    last = N_DEV - 1
    npair = len(pairs)

    def body(*refs):
        ops = refs[:2 * npair]
        x_ref, g_ref, dres_ref = refs[2 * npair:2 * npair + 3]
        dx_ref, dxb_ref, dg_ref, acc_ref = refs[2 * npair + 3 + len(after):]
        i, j = pl.program_id(0), pl.program_id(1)

        @pl.when(j == 0)
        def _():
            acc_ref[...] = jnp.zeros_like(acc_ref)

        @pl.when((i == 0) & (j == 0))
        def _():
            dg_ref[...] = jnp.zeros_like(dg_ref)

        for rs in _chunks(tm, 256):
            part = _dot(ops[0][rs, :], ops[1][...], NT)
            for q in range(1, npair):
                part = part + _dot(ops[2 * q][rs, :], ops[2 * q + 1][...], NT)
            acc_ref[rs, :] += part

        @pl.when(j == last)
        def _():
            for rs in _chunks(tm, 256):
                xv = x_ref[rs, :]
                dh = acc_ref[rs, :]
                r = lax.rsqrt(jnp.mean(xv * xv, axis=-1, keepdims=True) + EPS)
                nrm = xv * r
                dg_ref[...] += jnp.sum(dh * nrm, axis=0, keepdims=True)
                dn = dh * g_ref[...]
                dx = dres_ref[rs, :] + r * (dn - nrm * jnp.mean(dn * nrm, axis=-1, keepdims=True))
                dx_ref[rs, :] = dx
                dxb_ref[rs, :] = dx.astype(BF16)

    in_specs, args = [], []
    for a3, w3 in pairs:
        k = a3.shape[2]
        in_specs += [pl.BlockSpec((None, tm, k), lambda i, j: (j, i, 0)),
                     pl.BlockSpec((None, d, k), lambda i, j: (j, 0, 0))]
        args += [a3, w3]
    row = pl.BlockSpec((tm, d), lambda i, j: (i, 0))
    vec = pl.BlockSpec((1, d), lambda i, j: (0, 0))
    return pl.pallas_call(
        body, name=name, grid=(s // tm, N_DEV),
        in_specs=in_specs + [row, vec, row] + _after_specs(after),
        out_specs=[row, row, vec],
        out_shape=[jax.ShapeDtypeStruct((s, d), F32), jax.ShapeDtypeStruct((s, d), BF16),
                   jax.ShapeDtypeStruct((1, d), F32)],
        scratch_shapes=[pltpu.VMEM((tm, d), F32)],
        compiler_params=_cp(2),
    )(*args, xres, g, dres, *after)


def _wo_bwd(dx1b, wo, ya, yb, proj, b_gate2, pool_scale, after=()):
    s, d = dx1b.shape
    sw = proj.shape[2]
    tn = d // N_DEV
    nq = sw // tn

    def body(dx_ref, wo_ref, ya_ref, yb_ref, ga_ref, gb_ref, bg_ref, sc_ref, *rest):
        dya_ref, dyb_ref, dp_ref, dbg_ref, dsc_ref = rest[len(after):]
        dbg_ref[...] = jnp.zeros_like(dbg_ref)
        dsc_ref[...] = jnp.zeros_like(dsc_ref)
        for rs in _chunks(s, 256):
            dm = _dot(dx_ref[rs, :], wo_ref[...], NT)
            ya_v = ya_ref[rs, :].astype(F32)
            yb_v = yb_ref[rs, :].astype(F32)
            sa = _sigmoid(ga_ref[rs, :] + bg_ref[0:1, :])
            sb = _sigmoid(gb_ref[rs, :] + bg_ref[1:2, :])
            sc = sc_ref[...]
            dya_ref[rs, :] = (dm * sa).astype(BF16)
            dsb = dm * sb
            dyb_ref[rs, :] = (dsb * sc).astype(BF16)
            dsc_ref[...] += jnp.sum(dsb * yb_v, axis=0, keepdims=True)
            dga = dm * ya_v * (sa * (1.0 - sa))
            dgb = dm * (yb_v * sc) * (sb * (1.0 - sb))
            dp_ref[0, rs, :] = dga.astype(BF16)
            dp_ref[1, rs, :] = dgb.astype(BF16)
            dbg_ref[0:1, :] += jnp.sum(dga, axis=0, keepdims=True)
            dbg_ref[1:2, :] += jnp.sum(dgb, axis=0, keepdims=True)

    col = pl.BlockSpec((s, tn), lambda j: (0, j))
    out = jax.ShapeDtypeStruct((s, d), BF16)
    return pl.pallas_call(
        body, name="wo_bwd", grid=(N_DEV,),
        in_specs=[pl.BlockSpec((s, d), lambda j: (0, 0)),
                  pl.BlockSpec((tn, d), lambda j: (j, 0)), col, col,
                  pl.BlockSpec((None, s, tn), lambda j: (4 + j // nq, 0, j % nq)),
                  pl.BlockSpec((None, s, tn), lambda j: (6 + j // nq, 0, j % nq)),
                  pl.BlockSpec((2, tn), lambda j: (0, j)),
                  pl.BlockSpec((1, tn), lambda j: (0, j))] + _after_specs(after),
        out_specs=[col, col,
                   pl.BlockSpec((2, None, s, tn), lambda j: (1, j // nq, 0, j % nq)),
                   pl.BlockSpec((2, tn), lambda j: (0, j)),
                   pl.BlockSpec((1, tn), lambda j: (0, j))],
        out_shape=[out, out, jax.ShapeDtypeStruct((4, 2, s, sw), BF16),
                   jax.ShapeDtypeStruct((2, d), F32), jax.ShapeDtypeStruct((1, d), F32)],
        compiler_params=_cp(1),
    )(dx1b, wo, ya, yb, proj, proj, b_gate2, pool_scale, *after)


def _conv_bwd(dproj, dya, wa, proj, conv_w, conv_b):
    s, d = dya.shape
    sw = wa.shape[0]
    tc = min(LANES, sw)

    def body(dproj_hbm, dya_ref, wa_ref, ba_ref, ca_ref, va_ref, cw_ref, cb_ref,
             dp_ref, dcw_ref, dcb_ref, dz_ref):
        del dproj_hbm
        for rs in _chunks(s, 512):
            dz_ref[rs, :] = _dot(dya_ref[rs, :], wa_ref[...], NT)
        dz = dz_ref[...]
        ba, ca, va = ba_ref[...], ca_ref[...], va_ref[...]
        cv = ca * va
        cv1, cv2 = _shift_down(cv, 1), _shift_down(cv, 2)
        w0, w1, w2 = cw_ref[0:1, :], cw_ref[1:2, :], cw_ref[2:3, :]
        u = cb_ref[...] + w0 * cv2 + w1 * cv1 + w2 * cv
        du = dz * ba
        dp_ref[0] = (dz * u).astype(BF16)
        dcv = w2 * du + w1 * _shift_up(du, 1) + w0 * _shift_up(du, 2)
        dp_ref[1] = (dcv * va).astype(BF16)
        dp_ref[2] = (dcv * ca).astype(BF16)
        dcw_ref[0:1, :] = jnp.sum(du * cv2, axis=0, keepdims=True)
        dcw_ref[1:2, :] = jnp.sum(du * cv1, axis=0, keepdims=True)
        dcw_ref[2:3, :] = jnp.sum(du * cv, axis=0, keepdims=True)
        dcb_ref[...] = jnp.sum(du, axis=0, keepdims=True)

    def part(k):
        return pl.BlockSpec((None, s, tc), lambda i: (k, 0, i))

    return pl.pallas_call(
        body, name="conv_bwd", grid=(sw // tc,),
        in_specs=[pl.BlockSpec(memory_space=pl.ANY),
                  pl.BlockSpec((s, d), lambda i: (0, 0)),
                  pl.BlockSpec((tc, d), lambda i: (i, 0)),
                  part(0), part(1), part(2),
                  pl.BlockSpec((CONV_K, tc), lambda i: (0, i)), pl.BlockSpec((1, tc), lambda i: (0, i))],
        out_specs=[pl.BlockSpec((3, s, tc), lambda i: (0, 0, i)),
                   pl.BlockSpec((CONV_K, tc), lambda i: (0, i)), pl.BlockSpec((1, tc), lambda i: (0, i))],
        out_shape=[jax.ShapeDtypeStruct(dproj.shape, BF16),
                   jax.ShapeDtypeStruct((CONV_K, sw), F32), jax.ShapeDtypeStruct((1, sw), F32)],
        scratch_shapes=[pltpu.VMEM((s, tc), F32)],
        input_output_aliases={0: 0},
        compiler_params=_cp(1),
    )(dproj, dya, wa, proj, proj, proj, conv_w, conv_b)


def _pool_bwd(dproj, dyb, wpool):
    s, d = dyb.shape
    n_groups, gw, go = wpool.shape

    def body(dproj_hbm, dyb_ref, wp_ref, dp_ref):
        del dproj_hbm
        for gi, window in enumerate(POOL_WINDOWS):
            @pl.when(pl.program_id(0) == gi)
            def _():
                dpool = _dot(dyb_ref[...], wp_ref[...], NT)
                acc, k = dpool / _pool_counts(dpool.shape, window), 1
                while k < window:
                    acc = acc + _shift_up(acc, k)
                    k *= 2
                dp_ref[...] = (acc - dpool).astype(BF16)

    return pl.pallas_call(
        body, name="pool_bwd", grid=(n_groups,),
        in_specs=[pl.BlockSpec(memory_space=pl.ANY),
                  pl.BlockSpec((s, go), lambda g: (0, g)),
                  pl.BlockSpec((None, gw, go), lambda g: (g, 0, 0))],
        out_specs=pl.BlockSpec((None, s, gw), lambda g: (3, 0, g)),
        out_shape=jax.ShapeDtypeStruct(dproj.shape, BF16),
        input_output_aliases={0: 0},
        compiler_params=_cp(1),
    )(dproj, dyb, wpool)


def _rows128(v):
    return v.reshape(-1, LANES)


def kernel(x, norm1_g, w_in, b_gate, conv_w, conv_b, w_a_out, w_pool, pool_scale, w_o, norm2_g, w_ffn_gate, w_ffn_up, w_ffn_down, final_g, loss_target, m_norm1_g, m_w_in, m_b_gate, m_conv_w, m_conv_b, m_w_a_out, m_w_pool, m_pool_scale, m_w_o, m_norm2_g, m_w_ffn_gate, m_w_ffn_up, m_w_ffn_down, m_final_g, v_norm1_g, v_w_in, v_b_gate, v_conv_w, v_conv_b, v_w_a_out, v_w_pool, v_pool_scale, v_w_o, v_norm2_g, v_w_ffn_gate, v_w_ffn_up, v_w_ffn_down, v_final_g):
    s, d = x.shape[1], x.shape[2]
    sw = w_in.shape[2]
    n_groups = w_pool.shape[1]
    gw = w_pool.shape[2]
    go = w_pool.shape[3] * N_DEV
    f8 = w_ffn_gate.shape[2]
    cws = conv_w.shape[2]
    assert sw == conv_w.shape[2] * N_DEV == gw * n_groups and go * n_groups == d and n_groups == len(POOL_WINDOWS)

    xi, yi, ci = _coords()
    me = 4 * xi + 2 * yi + ci
    my_chip = 2 * xi + yi

    x2d = x.reshape(s, d)
    target = loss_target.reshape(s, d)
    final_g2 = final_g.reshape(1, d)
    b_gate2 = b_gate.reshape(2, d)

    big_names = ["w_in", "w_a_out", "w_pool", "w_o", "w_ffn_gate", "w_ffn_up", "w_ffn_down"]
    big_w = [w_in, w_a_out, w_pool, w_o, w_ffn_gate, w_ffn_up, w_ffn_down]
    big_m = [m_w_in, m_w_a_out, m_w_pool, m_w_o, m_w_ffn_gate, m_w_ffn_up, m_w_ffn_down]
    big_v = [v_w_in, v_w_a_out, v_w_pool, v_w_o, v_w_ffn_gate, v_w_ffn_up, v_w_ffn_down]
    shapes2d = [(w.size // w.shape[-1], w.shape[-1]) for w in big_w]
    big_w2 = [w.reshape(sh) for w, sh in zip(big_w, shapes2d)]

    sb = [_cast_bf16(w, "cast_" + nm) for w, nm in zip(big_w2, big_names)]
    win_g, wa_g, wpool_g, wo_g = _allgather_big(sb[0:4], "allgather_mixer", COLLECTIVE_GATHER)
    wg_g, wu_g = _allgather_big(sb[4:6], "allgather_ffn_up", COLLECTIVE_GATHER)
    (wd_g,) = _allgather_big(sb[6:7], "allgather_ffn_down", COLLECTIVE_GATHER)
    convw_g = _allgather_small(jnp.pad(conv_w.reshape(CONV_K, cws), ((0, 8 - CONV_K), (0, 0))), "allgather_conv_w")
    conv_w_full = convw_g[:, :CONV_K, :].transpose(1, 0, 2).reshape(CONV_K, sw)
    wa = wa_g.transpose(1, 0, 2).reshape(sw, d)
    wpool = wpool_g.reshape(N_DEV, n_groups, gw, go // N_DEV).transpose(1, 2, 0, 3).reshape(n_groups, gw, go)
    wo = wo_g.reshape(d, d)

    h = _rms_fwd(x2d, norm1_g)
    proj = _proj_fwd(h, win_g)
    z = _conv_fwd(proj, conv_w_full, conv_b)
    p = _pool_fwd(proj)
    ya, yb, merged = _merge_fwd(z, wa, p, wpool, proj, b_gate2, pool_scale)
    x1, h2 = _wo_fwd(merged, wo, x2d, norm2_g)
    gact, uact, act = _ffn_up_act_fwd(h2, wg_g, wu_g)
    ffn_out = _ffn_down_fwd(act, wd_g)
    dx2, dx2b, d_final_g, loss_blk = _loss_bwd(ffn_out, x1, target, final_g2)

    chip_slots = (2 * jnp.arange(4, dtype=jnp.int32) + ci).astype(jnp.int32)

    def partials(grads, recvs, names):
        return [_chip_partial(chip_slots, g3, r, "chip_partial_" + nm) for g3, r, nm in zip(grads, recvs, names)]

    own = jnp.stack([me, my_chip]).astype(jnp.int32)

    def adam(a, g3, sib, chips):
        outs = _adam_big(own, big_w2[a], big_m[a].reshape(shapes2d[a]), big_v[a].reshape(shapes2d[a]),
                         g3, sib, chips, "adam_" + big_names[a])
        return [o.reshape(big_w[a].shape) for o in outs]

    big_out = [None] * len(big_names)
    dg_act, du_act = _ffn_gate_bwd(dx2b, wd_g, gact, uact)
    gw_gate = _wgrad_cols(h2, dg_act, "wgrad_ffn_gate")
    gw_up = _wgrad_cols(h2, du_act, "wgrad_ffn_up")
    sib_gu = _exchange_sibling([gw_gate, gw_up], "rs_sibling_ffn_up", COLLECTIVE_SIBLING)
    ps_gu = partials([gw_gate, gw_up], sib_gu, ["w_ffn_gate", "w_ffn_up"])
    chips_gu = _exchange_chips(ps_gu, "rs_chips_ffn_up", COLLECTIVE_CHIPS)
    gw_down = _wgrad_rows(act, dx2b, "wgrad_ffn_down", after=ps_gu)
    sib_down = _exchange_sibling([gw_down], "rs_sibling_ffn_down", COLLECTIVE_SIBLING)
    ps_down = partials([gw_down], sib_down, ["w_ffn_down"])
    chips_down = _exchange_chips(ps_down, "rs_chips_ffn_down", COLLECTIVE_CHIPS)
    dh2 = _input_grad([(dg_act, wg_g), (du_act, wu_g)], "ffn_in_bwd", after=ps_down)
    dx1, dx1b, d_norm2_g = _rms_bwd(dh2, x1, norm2_g, dx2, "rms2_bwd")
    dya, dyb, dproj42, d_b_gate, d_pool_scale = _wo_bwd(dx1b, wo, ya, yb, proj, b_gate2, pool_scale)
    gw_o = _wgrad_full(merged, dx1b, "wgrad_o")
    dproj = dproj42.reshape(N_DEV, s, sw)
    dproj, d_conv_w, d_conv_b = _conv_bwd(dproj, dya, wa, proj, conv_w_full, conv_b)
    dproj = _pool_bwd(dproj, dyb, wpool)
    gw_a = _wgrad_full(z, dya, "wgrad_a_out")
    gw_pool = _wgrad_pool(p, dyb, n_groups)
    mix3 = [gw_a.reshape(sw, N_DEV, d // N_DEV).transpose(1, 0, 2),
            gw_pool.reshape(n_groups, gw, N_DEV, go // N_DEV).transpose(2, 0, 1, 3).reshape(N_DEV, n_groups * gw, go // N_DEV),
            gw_o.reshape(N_DEV, d // N_DEV, d)]
    sib_mix = _exchange_sibling(mix3, "rs_sibling_mixer", COLLECTIVE_SIBLING)
    ps_mix = partials(mix3, sib_mix, ["w_a_out", "w_pool", "w_o"])
    chips_mix = _exchange_chips(ps_mix, "rs_chips_mixer", COLLECTIVE_CHIPS)
    big_out[4] = adam(4, gw_gate, sib_gu[0], chips_gu[0])
    big_out[5] = adam(5, gw_up, sib_gu[1], chips_gu[1])
    gw_in = _wgrad_cols(h, dproj, "wgrad_in", after=ps_mix + [big_out[4][0], big_out[5][0]])
    sib_in = _exchange_sibling([gw_in], "rs_sibling_w_in", COLLECTIVE_SIBLING)
    big_out[6] = adam(6, gw_down, sib_down[0], chips_down[0])
    ps_in = partials([gw_in], sib_in, ["w_in"])
    chips_in = _exchange_chips(ps_in, "rs_chips_w_in", COLLECTIVE_CHIPS)
    dh = _input_grad([(dproj, win_g)], "proj_in_bwd", after=ps_mix + ps_in + [big_out[6][0]])
    grad_x, _, d_norm1_g = _rms_bwd(dh, x2d, norm1_g, dx1, "rms1_bwd")
    for k in range(3):
        big_out[1 + k] = adam(1 + k, mix3[k], sib_mix[k], chips_mix[k])
    big_out[0] = adam(0, gw_in, sib_in[0], chips_in[0])

    small_parts = [d_norm1_g, d_b_gate, d_conv_w, d_conv_b, d_pool_scale, d_norm2_g, d_final_g, loss_blk]
    sizes = [v.size for v in small_parts]
    packed = jnp.concatenate([_rows128(v) for v in small_parts], axis=0)
    summed = _sum_small(_allgather_small(packed, "allgather_small_grads")).reshape(-1)
    offs = [0]
    for n in sizes:
        offs.append(offs[-1] + n)
    g_norm1, g_bgate, g_convw_full, g_convb, g_pscale, g_norm2, g_final, loss_sum = [
        summed[offs[k]:offs[k + 1]] for k in range(len(sizes))]
    loss = loss_sum[0]
    g_convw = lax.dynamic_slice(g_convw_full.reshape(CONV_K, sw), (0, me * cws), (CONV_K, cws))
    small_w = [norm1_g, b_gate, conv_w, conv_b, pool_scale, norm2_g, final_g]
    small_m = [m_norm1_g, m_b_gate, m_conv_w, m_conv_b, m_pool_scale, m_norm2_g, m_final_g]
    small_v = [v_norm1_g, v_b_gate, v_conv_w, v_conv_b, v_pool_scale, v_norm2_g, v_final_g]
    small_g = [g_norm1, g_bgate, g_convw, g_convb, g_pscale, g_norm2, g_final]

    def pack(parts):
        flat = jnp.concatenate([v.reshape(-1) for v in parts])
        pad = (-flat.size) % (8 * LANES)
        return jnp.pad(flat, (0, pad)).reshape(-1, LANES)

    s_delta, s_m, s_v = _adam_small(pack(small_w), pack(small_g), pack(small_m), pack(small_v))
    soffs = [0]
    for w in small_w:
        soffs.append(soffs[-1] + w.size)

    def unpack(buf):
        flat = buf.reshape(-1)
        return [flat[soffs[k]:soffs[k + 1]].reshape(small_w[k].shape) for k in range(len(small_w))]

    small_grads = [g.reshape(w.shape) for g, w in zip(small_g, small_w)]
    small_delta, small_new_m, small_new_v = unpack(s_delta), unpack(s_m), unpack(s_v)

    order = ["norm1_g", "w_in", "b_gate", "conv_w", "conv_b", "w_a_out", "w_pool", "pool_scale", "w_o", "norm2_g",
             "w_ffn_gate", "w_ffn_up", "w_ffn_down", "final_g"]
    small_names = ["norm1_g", "b_gate", "conv_w", "conv_b", "pool_scale", "norm2_g", "final_g"]
    per_kind = [{}, {}, {}, {}]
    for a, nm in enumerate(big_names):
        for kind in range(4):
            per_kind[kind][nm] = big_out[a][kind]
    for k, nm in enumerate(small_names):
        per_kind[0][nm] = small_grads[k]
        per_kind[1][nm] = small_delta[k]
        per_kind[2][nm] = small_new_m[k]
        per_kind[3][nm] = small_new_v[k]
    result = [loss, grad_x.reshape(x.shape)]
    for kind in range(4):
        result += [per_kind[kind][nm] for nm in order]
    return tuple(result)
```

```python
import functools

import jax
import jax.numpy as jnp
from jax import lax
from jax.experimental import pallas as pl
from jax.experimental.pallas import tpu as pltpu
from jax.experimental.pallas import tpu_sc as plsc

F32 = jnp.float32
BF16 = jnp.bfloat16
MESH = pl.DeviceIdType.MESH

N_DEV = 8
EPS = 1e-6
CONV_K = 3
POOL_WINDOWS = (2, 4, 8, 16)
ADAM_LR = 0.001
ADAM_B1 = 0.9
ADAM_B2 = 0.999
ADAM_EPS = 1e-08
ADAM_WD = 0.01
ADAM_STEP = 10

V7X_VMEM_LIMIT_BYTES = 56 * 1024 * 1024
LANES = 128

COLLECTIVE_GATHER = 1
COLLECTIVE_SIBLING = 2
COLLECTIVE_CHIPS = 3
SEQUENCER_COST_BYTES = 4 * 10**9

NN = ((1,), (0,))
NT = ((1,), (1,))
TN = ((0,), (0,))


def _dot(a, b, dims):
    return lax.dot_general(a, b, (dims, ((), ())), preferred_element_type=F32)


def _cp(n_axes):
    return pltpu.CompilerParams(dimension_semantics=("arbitrary",) * n_axes,
                                vmem_limit_bytes=V7X_VMEM_LIMIT_BYTES)


def _row_tile(rows, bytes_per_row, cap_bytes):
    best = None
    for t in range(16, rows + 1, 16):
        if rows % t == 0 and t * bytes_per_row <= cap_bytes:
            best = t
    return best if best is not None else rows


def _chunks(total, size):
    size = min(size, total)
    assert total % size == 0
    return [slice(r, r + size) for r in range(0, total, size)]


def _after_specs(after):
    return [pl.BlockSpec(memory_space=pl.ANY)] * len(after)


def _shift_down(v, k):
    row = lax.broadcasted_iota(jnp.int32, v.shape, 0)
    return jnp.where(row >= k, pltpu.roll(v, k, 0), 0.0)


def _shift_up(v, k):
    n = v.shape[0]
    row = lax.broadcasted_iota(jnp.int32, v.shape, 0)
    return jnp.where(row < n - k, pltpu.roll(v, n - k, 0), 0.0)


def _sigmoid(v):
    return jax.nn.sigmoid(v)


def _cast_bf16(w2d, name):
    rows, cols = w2d.shape
    tr = _row_tile(rows, cols * 4, 2 << 20)

    def body(i_ref, o_ref):
        o_ref[...] = i_ref[...].astype(BF16)

    return pl.pallas_call(
        body, name=name, grid=(rows // tr,),
        in_specs=[pl.BlockSpec((tr, cols), lambda i: (i, 0))],
        out_specs=pl.BlockSpec((tr, cols), lambda i: (i, 0)),
        out_shape=jax.ShapeDtypeStruct((rows, cols), BF16),
        compiler_params=_cp(1),
    )(w2d)


def _rms_fwd(x2d, g):
    s, d = x2d.shape
    tm = min(256, s)

    def body(x_ref, g_ref, h_ref):
        xv = x_ref[...]
        r = lax.rsqrt(jnp.mean(xv * xv, axis=-1, keepdims=True) + EPS)
        h_ref[...] = (xv * r * g_ref[...]).astype(BF16)

    return pl.pallas_call(
        body, name="rms1_fwd", grid=(s // tm,),
        in_specs=[pl.BlockSpec((tm, d), lambda i: (i, 0)), pl.BlockSpec((1, d), lambda i: (0, 0))],
        out_specs=pl.BlockSpec((tm, d), lambda i: (i, 0)),
        out_shape=jax.ShapeDtypeStruct((s, d), BF16),
        compiler_params=_cp(1),
    )(x2d, g)


def _coords():
    return lax.axis_index("x"), lax.axis_index("y"), lax.axis_index("c")


def _slot(p):
    return 4 * p[0] + 2 * p[1] + p[2]


def _handshake(peers):
    barrier = pltpu.get_barrier_semaphore()
    for peer in peers:
        pl.semaphore_signal(barrier, inc=1, device_id=peer, device_id_type=MESH)
    pl.semaphore_wait(barrier, len(peers))


def _sequencer_call(body, out_type, scratch_types, name, collective_id):
    return pl.kernel(
        body, out_type=out_type, name=name,
        mesh=plsc.ScalarSubcoreMesh(axis_name="seq", num_cores=1),
        scratch_types=scratch_types,
        cost_estimate=pl.CostEstimate(flops=0, transcendentals=0, bytes_accessed=SEQUENCER_COST_BYTES),
        compiler_params=pltpu.CompilerParams(collective_id=collective_id))


def _allgather_big(shards, name, collective_id, after=()):
    n = len(shards)

    def body(*refs):
        ins, outs = refs[:n], refs[n + len(after):2 * n + len(after)]
        send_sems, recv_sems, local_sems = refs[2 * n + len(after):]
        x, y, c = _coords()
        me, sibling = (x, y, c), (x, y, 1 - c)
        x_nbr, y_nbr, diag = (1 - x, y), (x, 1 - y), (1 - x, 1 - y)
        relay_from = (x + (1 - c) * (1 - 2 * x), y + c * (1 - 2 * y))
        relay_to = (x + c * (1 - 2 * x), y + (1 - c) * (1 - 2 * y))
        _handshake([sibling, (*x_nbr, c), (*y_nbr, c)])

        def copy(a, k, block, to, src=None):
            dst = outs[a].at[_slot(block)]
            return pltpu.make_async_remote_copy(
                src_ref=dst if src is None else src, dst_ref=dst,
                send_sem=send_sems.at[a, k], recv_sem=recv_sems.at[a, k],
                device_id=to, device_id_type=MESH)

        mine, sends = [], []
        for a in range(n):
            cp = pltpu.make_async_copy(ins[a], outs[a].at[_slot(me)], local_sems.at[a])
            cp.start()
            mine.append(cp)
            first = [copy(a, 0, me, sibling, src=ins[a]),
                     copy(a, 1, me, (*x_nbr, c), src=ins[a]),
                     copy(a, 2, me, (*y_nbr, c), src=ins[a])]
            for cp in first:
                cp.start()
            sends += first
        for a in range(n):
            copy(a, 1 + c, (*relay_from, c), me).wait_recv()
            passed = [copy(a, 3, (*relay_from, c), (*relay_to, c)), copy(a, 4 + c, (*relay_from, c), sibling)]
            for cp in passed:
                cp.start()
            copy(a, 2 - c, (*relay_to, c), me).wait_recv()
            cp = copy(a, 5 - c, (*relay_to, c), sibling)
            cp.start()
            passed.append(cp)
            copy(a, 3, (*diag, c), me).wait_recv()
            cp = copy(a, 6, (*diag, c), sibling)
            cp.start()
            sends += passed + [cp]
        for a in range(n):
            copy(a, 0, sibling, me).wait_recv()
            copy(a, 4, (*x_nbr, 1 - c), me).wait_recv()
            copy(a, 5, (*y_nbr, 1 - c), me).wait_recv()
            copy(a, 6, (*diag, 1 - c), me).wait_recv()
        for cp in sends:
            cp.wait_send()
        for cp in mine:
            cp.wait()

    return _sequencer_call(
        body, [jax.ShapeDtypeStruct((N_DEV,) + s.shape, s.dtype) for s in shards],
        [pltpu.SemaphoreType.DMA((n, 7)), pltpu.SemaphoreType.DMA((n, 7)), pltpu.SemaphoreType.DMA((n,))],
        name, collective_id)(*shards, *after)


def _exchange_sibling(grads, name, collective_id):
    n = len(grads)

    def body(*refs):
        ins, outs = refs[:n], refs[n:2 * n]
        send_sems, recv_sems = refs[2 * n:]
        x, y, c = _coords()
        sibling = (x, y, 1 - c)
        _handshake([sibling])
        copies = []
        for a in range(n):
            for q in range(4):
                cp = pltpu.make_async_remote_copy(
                    src_ref=ins[a].at[2 * q + (1 - c)], dst_ref=outs[a].at[q],
                    send_sem=send_sems.at[a, q], recv_sem=recv_sems.at[a, q],
                    device_id=sibling, device_id_type=MESH)
                cp.start()
                copies.append(cp)
        for cp in copies:
            cp.wait_recv()
        for cp in copies:
            cp.wait_send()

    any_spec = pl.BlockSpec(memory_space=pl.ANY)
    return pl.pallas_call(
        body, name=name,
        in_specs=[any_spec] * n, out_specs=[any_spec] * n,
        out_shape=[jax.ShapeDtypeStruct((4,) + g.shape[1:], g.dtype) for g in grads],
        scratch_shapes=[pltpu.SemaphoreType.DMA((n, 4)), pltpu.SemaphoreType.DMA((n, 4))],
        compiler_params=pltpu.CompilerParams(collective_id=collective_id),
    )(*grads)


def _sibling_partials(grads, name, collective_id):
    n = len(grads)

    def body(*refs):
        ins, recvs, psums = refs[:n], refs[n:2 * n], refs[2 * n:3 * n]
        send_sems, recv_sems = refs[3 * n:]
        x, y, c = _coords()
        sibling = (x, y, 1 - c)
        my_chip = 2 * x + y
        _handshake([sibling])
        copies = []
        for a in range(n):
            for q in range(4):
                cp = pltpu.make_async_remote_copy(
                    src_ref=ins[a].at[2 * q + (1 - c)], dst_ref=recvs[a].at[q],
                    send_sem=send_sems.at[a, q], recv_sem=recv_sems.at[a, q],
                    device_id=sibling, device_id_type=MESH)
                cp.start()
                copies.append(cp)

        def add(g_ref, r_ref, o_ref):
            o_ref[...] = (g_ref[...].astype(F32) + r_ref[...].astype(F32)).astype(BF16)

        for a in range(n):
            _, rows, cols = grads[a].shape
            tr = _row_tile(rows, cols * 2, 1 << 20)
            blk = pl.BlockSpec((tr, cols), lambda i: (i, 0))
            for q in range(4):
                copies[4 * a + q].wait_recv()

                @pl.when(q != my_chip)
                def _():
                    pltpu.emit_pipeline(add, grid=(rows // tr,), in_specs=[blk, blk], out_specs=[blk])(
                        ins[a].at[2 * q + c], recvs[a].at[q], psums[a].at[q])
        for cp in copies:
            cp.wait_send()

    any_spec = pl.BlockSpec(memory_space=pl.ANY)
    shapes = [jax.ShapeDtypeStruct((4,) + g.shape[1:], g.dtype) for g in grads]
    outs = pl.pallas_call(
        body, name=name,
        in_specs=[any_spec] * n, out_specs=[any_spec] * (2 * n),
        out_shape=shapes + shapes,
        scratch_shapes=[pltpu.SemaphoreType.DMA((n, 4)), pltpu.SemaphoreType.DMA((n, 4))],
        compiler_params=pltpu.CompilerParams(collective_id=collective_id,
                                             vmem_limit_bytes=V7X_VMEM_LIMIT_BYTES),
    )(*grads)
    return list(outs[:n]), list(outs[n:])


def _exchange_chips(psums, name, collective_id):
    n = len(psums)

    def body(*refs):
        ins, outs = refs[:n], refs[n:2 * n]
        send_sems, recv_sems = refs[2 * n:]
        x, y, c = _coords()
        chips = [(1 - x, y), (x, 1 - y), (1 - x, 1 - y)]
        _handshake([(*chip, c) for chip in chips])
        copies = []
        for a in range(n):
            for j, chip in enumerate(chips):
                cp = pltpu.make_async_remote_copy(
                    src_ref=ins[a].at[2 * chip[0] + chip[1]], dst_ref=outs[a].at[j],
                    send_sem=send_sems.at[a, j], recv_sem=recv_sems.at[a, j],
                    device_id=(*chip, c), device_id_type=MESH)
                cp.start()
                copies.append(cp)
        for cp in copies:
            cp.wait_recv()
        for cp in copies:
            cp.wait_send()

    return _sequencer_call(
        body, [jax.ShapeDtypeStruct((3,) + p.shape[1:], p.dtype) for p in psums],
        [pltpu.SemaphoreType.DMA((n, 3)), pltpu.SemaphoreType.DMA((n, 3))],
        name, collective_id)(*psums)


def _allgather_small(v2d, name):
    rows, cols = v2d.shape

    def body(v_ref, out_ref, send_sems, recv_sems):
        x, y, c = _coords()
        me = (x, y, c)
        out_ref[_slot(me)] = v_ref[...]
        peers = []
        for k in range(1, N_DEV):
            fx, fy, fc = (k >> 2) & 1, (k >> 1) & 1, k & 1
            peers.append(((1 - x) if fx else x, (1 - y) if fy else y, (1 - c) if fc else c))
        sends = []
        for k, peer in enumerate(peers):
            cp = pltpu.make_async_remote_copy(
                src_ref=v_ref, dst_ref=out_ref.at[_slot(me)],
                send_sem=send_sems.at[k], recv_sem=recv_sems.at[k],
                device_id=peer, device_id_type=MESH)
            cp.start()
            sends.append(cp)
        for k, peer in enumerate(peers):
            pltpu.make_async_remote_copy(
                src_ref=v_ref, dst_ref=out_ref.at[_slot(peer)],
                send_sem=send_sems.at[k], recv_sem=recv_sems.at[k],
                device_id=peer, device_id_type=MESH).wait_recv()
        for cp in sends:
            cp.wait_send()

    vmem = pl.BlockSpec(memory_space=pltpu.VMEM)
    return pl.pallas_call(
        body, name=name, in_specs=[vmem], out_specs=vmem,
        out_shape=jax.ShapeDtypeStruct((N_DEV, rows, cols), v2d.dtype),
        scratch_shapes=[pltpu.SemaphoreType.DMA((N_DEV - 1,)), pltpu.SemaphoreType.DMA((N_DEV - 1,))],
    )(v2d)


def _chip_partial(slots, g3, recv, name):
    _, rows, cols = g3.shape
    tr = _row_tile(rows, cols * 2, 2 << 20)

    def body(slots_ref, g_ref, r_ref, o_ref):
        o_ref[...] = (g_ref[...].astype(F32) + r_ref[...].astype(F32)).astype(BF16)

    return pl.pallas_call(
        body, name=name,
        grid_spec=pltpu.PrefetchScalarGridSpec(
            num_scalar_prefetch=1, grid=(4, rows // tr),
            in_specs=[pl.BlockSpec((None, tr, cols), lambda q, i, sl: (sl[q], i, 0)),
                      pl.BlockSpec((None, tr, cols), lambda q, i, sl: (q, i, 0))],
            out_specs=pl.BlockSpec((None, tr, cols), lambda q, i, sl: (q, i, 0))),
        out_shape=jax.ShapeDtypeStruct((4, rows, cols), BF16),
        compiler_params=_cp(2),
    )(slots, g3, recv)


def _adam_math(w, g, m, v):
    m = ADAM_B1 * m + (1.0 - ADAM_B1) * g
    v = ADAM_B2 * v + (1.0 - ADAM_B2) * (g * g)
    m_hat = m / (1.0 - ADAM_B1 ** ADAM_STEP)
    v_hat = v / (1.0 - ADAM_B2 ** ADAM_STEP)
    delta = -ADAM_LR * (m_hat / (jnp.sqrt(v_hat) + ADAM_EPS) + ADAM_WD * w)
    return delta, m, v


def _adam_big(own, w, m, v, g3, recv_sib, recv_chips, name):
    rows, cols = w.shape
    tr = _row_tile(rows, cols * 4, 2 << 20)

    def body(own_ref, w_ref, m_ref, v_ref, g_ref, rs_ref, rc_ref, go_ref, do_ref, mo_ref, vo_ref):
        g = g_ref[...].astype(F32) + rs_ref[...].astype(F32)
        g = g + rc_ref[0].astype(F32)
        g = g + rc_ref[1].astype(F32)
        g = g + rc_ref[2].astype(F32)
        delta, m_new, v_new = _adam_math(w_ref[...], g, m_ref[...], v_ref[...])
        go_ref[...] = g
        do_ref[...] = delta
        mo_ref[...] = m_new
        vo_ref[...] = v_new

    blk = pl.BlockSpec((tr, cols), lambda i, o: (i, 0))
    out = jax.ShapeDtypeStruct((rows, cols), F32)
    return pl.pallas_call(
        body, name=name,
        grid_spec=pltpu.PrefetchScalarGridSpec(
            num_scalar_prefetch=1, grid=(rows // tr,),
            in_specs=[blk, blk, blk,
                      pl.BlockSpec((None, tr, cols), lambda i, o: (o[0], i, 0)),
                      pl.BlockSpec((None, tr, cols), lambda i, o: (o[1], i, 0)),
                      pl.BlockSpec((3, tr, cols), lambda i, o: (0, i, 0))],
            out_specs=[blk, blk, blk, blk]),
        out_shape=[out, out, out, out],
        compiler_params=_cp(1),
    )(own, w, m, v, g3, recv_sib, recv_chips)


def _sum_small(gathered):
    _, rows, cols = gathered.shape

    def body(g_ref, o_ref):
        acc = g_ref[0]
        for k in range(1, N_DEV):
            acc = acc + g_ref[k]
        o_ref[...] = acc

    vmem = pl.BlockSpec(memory_space=pltpu.VMEM)
    return pl.pallas_call(body, name="small_grad_sum", in_specs=[vmem], out_specs=vmem,
                          out_shape=jax.ShapeDtypeStruct((rows, cols), F32))(gathered)


def _adam_small(w, g, m, v):
    def body(w_ref, g_ref, m_ref, v_ref, do_ref, mo_ref, vo_ref):
        delta, m_new, v_new = _adam_math(w_ref[...], g_ref[...], m_ref[...], v_ref[...])
        do_ref[...] = delta
        mo_ref[...] = m_new
        vo_ref[...] = v_new

    vmem = pl.BlockSpec(memory_space=pltpu.VMEM)
    out = jax.ShapeDtypeStruct(w.shape, F32)
    return pl.pallas_call(body, name="adam_small", in_specs=[vmem] * 4, out_specs=[vmem] * 3,
                          out_shape=[out, out, out])(w, g, m, v)


def _proj_fwd(h, win_g):
    s, d = h.shape
    sw = win_g.shape[2]
    tn = min(512, sw)
    nh = sw // tn

    def body(h_ref, w_ref, o_ref):
        for rs in _chunks(s, 512):
            o_ref[rs, :] = _dot(h_ref[rs, :], w_ref[...], NN)

    return pl.pallas_call(
        body, name="proj_fwd", grid=(N_DEV * nh,),
        in_specs=[pl.BlockSpec((s, d), lambda j: (0, 0)),
                  pl.BlockSpec((None, d, tn), lambda j: (j // nh, 0, j % nh))],
        out_specs=pl.BlockSpec((None, s, tn), lambda j: (j // nh, 0, j % nh)),
        out_shape=jax.ShapeDtypeStruct((N_DEV, s, sw), F32),
        compiler_params=_cp(1),
    )(h, win_g)


def _conv_fwd(proj, conv_w, conv_b):
    _, s, sw = proj.shape
    tc = min(LANES, sw)

    def body(ba_ref, ca_ref, va_ref, cw_ref, cb_ref, z_ref):
        cv = ca_ref[...] * va_ref[...]
        u = (cb_ref[...] + cw_ref[0:1, :] * _shift_down(cv, 2) + cw_ref[1:2, :] * _shift_down(cv, 1)
             + cw_ref[2:3, :] * cv)
        z_ref[...] = (ba_ref[...] * u).astype(BF16)

    def part(k):
        return pl.BlockSpec((None, s, tc), lambda i: (k, 0, i))

    return pl.pallas_call(
        body, name="conv_fwd", grid=(sw // tc,),
        in_specs=[part(0), part(1), part(2),
                  pl.BlockSpec((CONV_K, tc), lambda i: (0, i)), pl.BlockSpec((1, tc), lambda i: (0, i))],
        out_specs=pl.BlockSpec((s, tc), lambda i: (0, i)),
        out_shape=jax.ShapeDtypeStruct((s, sw), BF16),
        compiler_params=_cp(1),
    )(proj, proj, proj, conv_w, conv_b)


def _pool_counts(shape, window):
    t = lax.broadcasted_iota(jnp.int32, shape, 0)
    return jnp.minimum(t + 1, window).astype(F32)


def _pool_fwd(proj):
    _, s, sw = proj.shape
    gw = sw // len(POOL_WINDOWS)

    def body(v_ref, p_ref):
        for gi, window in enumerate(POOL_WINDOWS):
            @pl.when(pl.program_id(0) == gi)
            def _():
                v = v_ref[...]
                acc, k = v, 1
                while k < window:
                    acc = acc + _shift_down(acc, k)
                    k *= 2
                p_ref[...] = (acc / _pool_counts(v.shape, window) - v).astype(BF16)

    return pl.pallas_call(
        body, name="pool_fwd", grid=(len(POOL_WINDOWS),),
        in_specs=[pl.BlockSpec((None, s, gw), lambda g: (3, 0, g))],
        out_specs=pl.BlockSpec((s, gw), lambda g: (0, g)),
        out_shape=jax.ShapeDtypeStruct((s, sw), BF16),
        compiler_params=_cp(1),
    )(proj)


def _merge_fwd(z, wa, p, wpool, proj, b_gate2, pool_scale):
    s, sw = z.shape
    d = wa.shape[1]
    tn = d // N_DEV
    gw = sw // len(POOL_WINDOWS)
    nq = sw // tn

    def body(z_ref, wa_ref, p_ref, wp_ref, ga_ref, gb_ref, bg_ref, sc_ref, ya_ref, yb_ref, m_ref):
        for rs in _chunks(s, 512):
            ya = _dot(z_ref[rs, :], wa_ref[...], NN)
            yb = _dot(p_ref[rs, :], wp_ref[...], NN)
            sa = _sigmoid(ga_ref[rs, :] + bg_ref[0:1, :])
            sb = _sigmoid(gb_ref[rs, :] + bg_ref[1:2, :])
            ya_ref[rs, :] = ya.astype(BF16)
            yb_ref[rs, :] = yb.astype(BF16)
            m_ref[rs, :] = (sa * ya + sb * (yb * sc_ref[...])).astype(BF16)

    col = pl.BlockSpec((s, tn), lambda j: (0, j))
    out = jax.ShapeDtypeStruct((s, d), BF16)
    return pl.pallas_call(
        body, name="merge_fwd", grid=(N_DEV,),
        in_specs=[pl.BlockSpec((s, sw), lambda j: (0, 0)),
                  pl.BlockSpec((sw, tn), lambda j: (0, j)),
                  pl.BlockSpec((s, gw), lambda j: (0, j // 2)),
                  pl.BlockSpec((None, gw, tn), lambda j: (j // 2, 0, j % 2)),
                  pl.BlockSpec((None, s, tn), lambda j: (4 + j // nq, 0, j % nq)),
                  pl.BlockSpec((None, s, tn), lambda j: (6 + j // nq, 0, j % nq)),
                  pl.BlockSpec((2, tn), lambda j: (0, j)),
                  pl.BlockSpec((1, tn), lambda j: (0, j))],
        out_specs=[col, col, col],
        out_shape=[out, out, out],
        compiler_params=_cp(1),
    )(z, wa, p, wpool, proj, proj, b_gate2, pool_scale)


def _wo_fwd(merged, wo, x2d, g2):
    s, d = x2d.shape
    tm = min(256, s)

    def body(m_ref, wo_ref, x_ref, g_ref, x1_ref, h2_ref):
        x1 = x_ref[...] + _dot(m_ref[...], wo_ref[...], NN)
        x1_ref[...] = x1
        r = lax.rsqrt(jnp.mean(x1 * x1, axis=-1, keepdims=True) + EPS)
        h2_ref[...] = (x1 * r * g_ref[...]).astype(BF16)

    row = pl.BlockSpec((tm, d), lambda i: (i, 0))
    return pl.pallas_call(
        body, name="wo_fwd", grid=(s // tm,),
        in_specs=[row, pl.BlockSpec((d, d), lambda i: (0, 0)), row, pl.BlockSpec((1, d), lambda i: (0, 0))],
        out_specs=[row, row],
        out_shape=[jax.ShapeDtypeStruct((s, d), F32), jax.ShapeDtypeStruct((s, d), BF16)],
        compiler_params=_cp(1),
    )(merged, wo, x2d, g2)


def _ffn_up_fwd(h2, wg_g, wu_g):
    s, d = h2.shape
    f8 = wg_g.shape[2]

    def body(h_ref, wg_ref, wu_ref, g_ref, u_ref):
        for rs in _chunks(s, 512):
            a = h_ref[rs, :]
            g_ref[rs, :] = _dot(a, wg_ref[...], NN).astype(BF16)
            u_ref[rs, :] = _dot(a, wu_ref[...], NN).astype(BF16)

    wspec = pl.BlockSpec((None, d, f8), lambda j: (j, 0, 0))
    ospec = pl.BlockSpec((None, s, f8), lambda j: (j, 0, 0))
    out = jax.ShapeDtypeStruct((N_DEV, s, f8), BF16)
    return pl.pallas_call(
        body, name="ffn_up_fwd", grid=(N_DEV,),
        in_specs=[pl.BlockSpec((s, d), lambda j: (0, 0)), wspec, wspec],
        out_specs=[ospec, ospec], out_shape=[out, out],
        compiler_params=_cp(1),
    )(h2, wg_g, wu_g)


def _ffn_down_loss(gact, uact, wd_g, x1, target, final_g):
    _, s, f8 = gact.shape
    d = x1.shape[1]
    tm = min(256, s)
    last = N_DEV - 1

    def body(g_ref, u_ref, wd_ref, x1_ref, t_ref, gf_ref, dx_ref, dxb_ref, dgf_ref, loss_ref, acc_ref):
        i, j = pl.program_id(0), pl.program_id(1)

        @pl.when(j == 0)
        def _():
            acc_ref[...] = jnp.zeros_like(acc_ref)

        @pl.when((i == 0) & (j == 0))
        def _():
            dgf_ref[...] = jnp.zeros_like(dgf_ref)
            loss_ref[...] = jnp.zeros_like(loss_ref)

        for rs in _chunks(tm, 256):
            g = g_ref[rs, :].astype(F32)
            act = (g * _sigmoid(g) * u_ref[rs, :].astype(F32)).astype(BF16)
            acc_ref[rs, :] += _dot(act, wd_ref[...], NN)

        @pl.when(j == last)
        def _():
            for rs in _chunks(tm, 256):
                x2 = x1_ref[rs, :] + acc_ref[rs, :]
                r = lax.rsqrt(jnp.mean(x2 * x2, axis=-1, keepdims=True) + EPS)
                nrm = x2 * r
                gf = gf_ref[...]
                err = nrm * gf - t_ref[rs, :]
                loss_ref[...] += jnp.sum(err * err) * (0.5 / d)
                dy = err * (1.0 / d)
                dgf_ref[...] += jnp.sum(dy * nrm, axis=0, keepdims=True)
                dn = dy * gf
                dx = r * (dn - nrm * jnp.mean(dn * nrm, axis=-1, keepdims=True))
                dx_ref[rs, :] = dx
                dxb_ref[rs, :] = dx.astype(BF16)

    aspec = pl.BlockSpec((None, tm, f8), lambda i, j: (j, i, 0))
    row = pl.BlockSpec((tm, d), lambda i, j: (i, 0))
    return pl.pallas_call(
        body, name="ffn_down_loss", grid=(s // tm, N_DEV),
        in_specs=[aspec, aspec, pl.BlockSpec((None, f8, d), lambda i, j: (j, 0, 0)), row, row,
                  pl.BlockSpec((1, d), lambda i, j: (0, 0))],
        out_specs=[row, row, pl.BlockSpec((1, d), lambda i, j: (0, 0)),
                   pl.BlockSpec((8, LANES), lambda i, j: (0, 0))],
        out_shape=[jax.ShapeDtypeStruct((s, d), F32), jax.ShapeDtypeStruct((s, d), BF16),
                   jax.ShapeDtypeStruct((1, d), F32), jax.ShapeDtypeStruct((8, LANES), F32)],
        scratch_shapes=[pltpu.VMEM((tm, d), F32)],
        compiler_params=_cp(2),
    )(gact, uact, wd_g, x1, target, final_g)


def _ffn_act_bwd(dx2b, wd_g, gact, uact):
    s, d = dx2b.shape
    f8 = gact.shape[2]
    tm = min(1024, s)

    def body(dx_ref, wd_ref, g_ref, u_ref, dg_ref, du_ref, act_ref):
        for rs in _chunks(tm, 256):
            da = _dot(dx_ref[rs, :], wd_ref[...], NT)
            g = g_ref[rs, :].astype(F32)
            u = u_ref[rs, :].astype(F32)
            sg = _sigmoid(g)
            silu = g * sg
            act_ref[rs, :] = (silu * u).astype(BF16)
            du_ref[rs, :] = (da * silu).astype(BF16)
            dg_ref[rs, :] = (da * u * (sg * (1.0 + g * (1.0 - sg)))).astype(BF16)

    aspec = pl.BlockSpec((None, tm, f8), lambda j, i: (j, i, 0))
    out = jax.ShapeDtypeStruct((N_DEV, s, f8), BF16)
    return pl.pallas_call(
        body, name="ffn_act_bwd", grid=(N_DEV, s // tm),
        in_specs=[pl.BlockSpec((tm, d), lambda j, i: (i, 0)),
                  pl.BlockSpec((None, f8, d), lambda j, i: (j, 0, 0)), aspec, aspec],
        out_specs=[aspec, aspec, aspec], out_shape=[out, out, out],
        compiler_params=_cp(2),
    )(dx2b, wd_g, gact, uact)


def _wgrad_shard_a(a3, b, name, after=()):
    _, s, k = a3.shape
    n = b.shape[1]
    ts = min(512, s)
    ns = s // ts

    def body(a_ref, b_ref, *rest):
        o_ref, acc_ref = rest[len(after):]
        i = pl.program_id(1)

        @pl.when(i == 0)
        def _():
            acc_ref[...] = jnp.zeros_like(acc_ref)

        acc_ref[...] += _dot(a_ref[...], b_ref[...], TN)

        @pl.when(i == ns - 1)
        def _():
            o_ref[...] = acc_ref[...].astype(BF16)

    return pl.pallas_call(
        body, name=name, grid=(N_DEV, ns),
        in_specs=[pl.BlockSpec((None, ts, k), lambda j, i: (j, i, 0)),
                  pl.BlockSpec((ts, n), lambda j, i: (i, 0))] + _after_specs(after),
        out_specs=pl.BlockSpec((None, k, n), lambda j, i: (j, 0, 0)),
        out_shape=jax.ShapeDtypeStruct((N_DEV, k, n), BF16),
        scratch_shapes=[pltpu.VMEM((k, n), F32)],
        compiler_params=_cp(2),
    )(a3, b, *after)


def _wgrad_shard_b(a, b3, name, after=()):
    s, k = a.shape
    n = b3.shape[2]
    ts = min(512, s)
    ns = s // ts

    def body(a_ref, b_ref, *rest):
        o_ref, acc_ref = rest[len(after):]
        i = pl.program_id(1)

        @pl.when(i == 0)
        def _():
            acc_ref[...] = jnp.zeros_like(acc_ref)

        acc_ref[...] += _dot(a_ref[...], b_ref[...], TN)

        @pl.when(i == ns - 1)
        def _():
            o_ref[...] = acc_ref[...].astype(BF16)

    return pl.pallas_call(
        body, name=name, grid=(N_DEV, ns),
        in_specs=[pl.BlockSpec((ts, k), lambda j, i: (i, 0)),
                  pl.BlockSpec((None, ts, n), lambda j, i: (j, i, 0))] + _after_specs(after),
        out_specs=pl.BlockSpec((None, k, n), lambda j, i: (j, 0, 0)),
        out_shape=jax.ShapeDtypeStruct((N_DEV, k, n), BF16),
        scratch_shapes=[pltpu.VMEM((k, n), F32)],
        compiler_params=_cp(2),
    )(a, b3, *after)


def _ffn_up_act_fwd(h2, wg_g, wu_g):
    s, d = h2.shape
    f8 = wg_g.shape[2]
    th = min(1024, s)

    def body(h_ref, wg_ref, wu_ref, g_ref, u_ref, a_ref):
        i = pl.program_id(1)
        for rs in _chunks(th, 512):
            rows = pl.ds(pl.multiple_of(i * th + rs.start, rs.stop - rs.start), rs.stop - rs.start)
            a = h_ref[rows, :]
            g = _dot(a, wg_ref[...], NN)
            u = _dot(a, wu_ref[...], NN)
            g_ref[rs, :] = g.astype(BF16)
            u_ref[rs, :] = u.astype(BF16)
            a_ref[rs, :] = (g * _sigmoid(g) * u).astype(BF16)

    wspec = pl.BlockSpec((None, d, f8), lambda j, i: (j, 0, 0))
    ospec = pl.BlockSpec((None, th, f8), lambda j, i: (j, i, 0))
    out = jax.ShapeDtypeStruct((N_DEV, s, f8), BF16)
    return pl.pallas_call(
        body, name="ffn_up_fwd", grid=(N_DEV, s // th),
        in_specs=[pl.BlockSpec((s, d), lambda j, i: (0, 0)), wspec, wspec],
        out_specs=[ospec, ospec, ospec], out_shape=[out, out, out],
        compiler_params=_cp(2),
    )(h2, wg_g, wu_g)


def _ffn_down_fwd(act, wd_g):
    _, s, f8 = act.shape
    d = wd_g.shape[2]
    tn = min(1024, d)

    def body(a_ref, wd_ref, o_ref):
        j = pl.program_id(1)

        @pl.when(j == 0)
        def _():
            o_ref[...] = jnp.zeros_like(o_ref)

        for rs in _chunks(s, 1024):
            o_ref[rs, :] += _dot(a_ref[rs, :], wd_ref[...], NN)

    return pl.pallas_call(
        body, name="ffn_down_fwd", grid=(d // tn, N_DEV),
        in_specs=[pl.BlockSpec((None, s, f8), lambda n, j: (j, 0, 0)),
                  pl.BlockSpec((None, f8, tn), lambda n, j: (j, 0, n))],
        out_specs=pl.BlockSpec((s, tn), lambda n, j: (0, n)),
        out_shape=jax.ShapeDtypeStruct((s, d), F32),
        compiler_params=_cp(2),
    )(act, wd_g)


def _loss_bwd(ffn_out, x1, target, final_g):
    s, d = x1.shape
    tm = min(256, s)

    def body(f_ref, x1_ref, t_ref, gf_ref, dx_ref, dxb_ref, dgf_ref, loss_ref):
        @pl.when(pl.program_id(0) == 0)
        def _():
            dgf_ref[...] = jnp.zeros_like(dgf_ref)
            loss_ref[...] = jnp.zeros_like(loss_ref)

        x2 = x1_ref[...] + f_ref[...]
        r = lax.rsqrt(jnp.mean(x2 * x2, axis=-1, keepdims=True) + EPS)
        nrm = x2 * r
        gf = gf_ref[...]
        err = nrm * gf - t_ref[...]
        loss_ref[...] += jnp.sum(err * err) * (0.5 / d)
        dy = err * (1.0 / d)
        dgf_ref[...] += jnp.sum(dy * nrm, axis=0, keepdims=True)
        dn = dy * gf
        dx = r * (dn - nrm * jnp.mean(dn * nrm, axis=-1, keepdims=True))
        dx_ref[...] = dx
        dxb_ref[...] = dx.astype(BF16)

    row = pl.BlockSpec((tm, d), lambda i: (i, 0))
    vec = pl.BlockSpec((1, d), lambda i: (0, 0))
    return pl.pallas_call(
        body, name="loss_bwd", grid=(s // tm,),
        in_specs=[row, row, row, vec],
        out_specs=[row, row, vec, pl.BlockSpec((8, LANES), lambda i: (0, 0))],
        out_shape=[jax.ShapeDtypeStruct((s, d), F32), jax.ShapeDtypeStruct((s, d), BF16),
                   jax.ShapeDtypeStruct((1, d), F32), jax.ShapeDtypeStruct((8, LANES), F32)],
        compiler_params=_cp(1),
    )(ffn_out, x1, target, final_g)


def _ffn_gate_bwd(dx2b, wd_g, gact, uact):
    s, d = dx2b.shape
    f8 = gact.shape[2]
    th = min(1024, s)

    def body(dx_ref, wd_ref, g_ref, u_ref, dg_ref, du_ref):
        i = pl.program_id(1)
        for rs in _chunks(th, 256):
            rows = pl.ds(pl.multiple_of(i * th + rs.start, rs.stop - rs.start), rs.stop - rs.start)
            da = _dot(dx_ref[rows, :], wd_ref[...], NT)
            g = g_ref[rs, :].astype(F32)
            u = u_ref[rs, :].astype(F32)
            sg = _sigmoid(g)
            du_ref[rs, :] = (da * (g * sg)).astype(BF16)
            dg_ref[rs, :] = (da * u * (sg * (1.0 + g * (1.0 - sg)))).astype(BF16)

    aspec = pl.BlockSpec((None, th, f8), lambda j, i: (j, i, 0))
    out = jax.ShapeDtypeStruct((N_DEV, s, f8), BF16)
    return pl.pallas_call(
        body, name="ffn_act_bwd", grid=(N_DEV, s // th),
        in_specs=[pl.BlockSpec((s, d), lambda j, i: (0, 0)),
                  pl.BlockSpec((None, f8, d), lambda j, i: (j, 0, 0)), aspec, aspec],
        out_specs=[aspec, aspec], out_shape=[out, out],
        compiler_params=_cp(2),
    )(dx2b, wd_g, gact, uact)


def _wgrad_rows(a3, b, name, after=()):
    _, s, k = a3.shape
    n = b.shape[1]

    def body(a_ref, b_ref, *rest):
        o_ref = rest[len(after)]
        o_ref[...] = _dot(a_ref[...], b_ref[...], TN).astype(BF16)

    return pl.pallas_call(
        body, name=name, grid=(N_DEV,),
        in_specs=[pl.BlockSpec((None, s, k), lambda j: (j, 0, 0)),
                  pl.BlockSpec((s, n), lambda j: (0, 0))] + _after_specs(after),
        out_specs=pl.BlockSpec((None, k, n), lambda j: (j, 0, 0)),
        out_shape=jax.ShapeDtypeStruct((N_DEV, k, n), BF16),
        compiler_params=_cp(1),
    )(a3, b, *after)


def _wgrad_cols(a, b3, name, after=()):
    s, k = a.shape
    n = b3.shape[2]

    def body(a_ref, b_ref, *rest):
        o_ref = rest[len(after)]
        o_ref[...] = _dot(a_ref[...], b_ref[...], TN).astype(BF16)

    return pl.pallas_call(
        body, name=name, grid=(N_DEV,),
        in_specs=[pl.BlockSpec((s, k), lambda j: (0, 0)),
                  pl.BlockSpec((None, s, n), lambda j: (j, 0, 0))] + _after_specs(after),
        out_specs=pl.BlockSpec((None, k, n), lambda j: (j, 0, 0)),
        out_shape=jax.ShapeDtypeStruct((N_DEV, k, n), BF16),
        compiler_params=_cp(1),
    )(a, b3, *after)


def _input_grad(pairs, name, after=()):
    s = pairs[0][0].shape[1]
    d = pairs[0][1].shape[1]
    tn = min(1024, d)
    npair = len(pairs)

    def body(*refs):
        ops = refs[:2 * npair]
        o_ref = refs[2 * npair + len(after)]
        j = pl.program_id(1)

        @pl.when(j == 0)
        def _():
            o_ref[...] = jnp.zeros_like(o_ref)

        for rs in _chunks(s, 1024):
            part = _dot(ops[0][rs, :], ops[1][...], NT)
            for q in range(1, npair):
                part = part + _dot(ops[2 * q][rs, :], ops[2 * q + 1][...], NT)
            o_ref[rs, :] += part

    in_specs, args = [], []
    for a3, w3 in pairs:
        k = a3.shape[2]
        in_specs += [pl.BlockSpec((None, s, k), lambda n, j: (j, 0, 0)),
                     pl.BlockSpec((None, tn, k), lambda n, j: (j, n, 0))]
        args += [a3, w3]
    return pl.pallas_call(
        body, name=name, grid=(d // tn, N_DEV),
        in_specs=in_specs + _after_specs(after),
        out_specs=pl.BlockSpec((s, tn), lambda n, j: (0, n)),
        out_shape=jax.ShapeDtypeStruct((s, d), F32),
        compiler_params=_cp(2),
    )(*args, *after)


def _rms_bwd(dh, xres, g, dres, name):
    s, d = xres.shape
    tm = min(256, s)

    def body(dh_ref, x_ref, g_ref, dres_ref, dx_ref, dxb_ref, dg_ref):
        @pl.when(pl.program_id(0) == 0)
        def _():
            dg_ref[...] = jnp.zeros_like(dg_ref)

        xv = x_ref[...]
        dh_v = dh_ref[...]
        r = lax.rsqrt(jnp.mean(xv * xv, axis=-1, keepdims=True) + EPS)
        nrm = xv * r
        dg_ref[...] += jnp.sum(dh_v * nrm, axis=0, keepdims=True)
        dn = dh_v * g_ref[...]
        dx = dres_ref[...] + r * (dn - nrm * jnp.mean(dn * nrm, axis=-1, keepdims=True))
        dx_ref[...] = dx
        dxb_ref[...] = dx.astype(BF16)

    row = pl.BlockSpec((tm, d), lambda i: (i, 0))
    vec = pl.BlockSpec((1, d), lambda i: (0, 0))
    return pl.pallas_call(
        body, name=name, grid=(s // tm,),
        in_specs=[row, row, vec, row],
        out_specs=[row, row, vec],
        out_shape=[jax.ShapeDtypeStruct((s, d), F32), jax.ShapeDtypeStruct((s, d), BF16),
                   jax.ShapeDtypeStruct((1, d), F32)],
        compiler_params=_cp(1),
    )(dh, xres, g, dres)


def _wgrad_full(a, b, name):
    s, k = a.shape
    n = b.shape[1]
    tk = min(512, k)
    ts = min(512, s)
    ns = s // ts

    def body(a_ref, b_ref, o_ref, acc_ref):
        i = pl.program_id(1)

        @pl.when(i == 0)
        def _():
            acc_ref[...] = jnp.zeros_like(acc_ref)

        acc_ref[...] += _dot(a_ref[...], b_ref[...], TN)

        @pl.when(i == ns - 1)
        def _():
            o_ref[...] = acc_ref[...].astype(BF16)

    return pl.pallas_call(
        body, name=name, grid=(k // tk, ns),
        in_specs=[pl.BlockSpec((ts, tk), lambda j, i: (i, j)),
                  pl.BlockSpec((ts, n), lambda j, i: (i, 0))],
        out_specs=pl.BlockSpec((tk, n), lambda j, i: (j, 0)),
        out_shape=jax.ShapeDtypeStruct((k, n), BF16),
        scratch_shapes=[pltpu.VMEM((tk, n), F32)],
        compiler_params=_cp(2),
    )(a, b)


def _wgrad_pool(p, dyb, n_groups):
    s, sw = p.shape
    d = dyb.shape[1]
    gw, go = sw // n_groups, d // n_groups
    ts = min(512, s)
    ns = s // ts

    def body(a_ref, b_ref, o_ref, acc_ref):
        i = pl.program_id(1)

        @pl.when(i == 0)
        def _():
            acc_ref[...] = jnp.zeros_like(acc_ref)

        acc_ref[...] += _dot(a_ref[...], b_ref[...], TN)

        @pl.when(i == ns - 1)
        def _():
            o_ref[...] = acc_ref[...].astype(BF16)

    return pl.pallas_call(
        body, name="wgrad_pool", grid=(n_groups, ns),
        in_specs=[pl.BlockSpec((ts, gw), lambda g, i: (i, g)),
                  pl.BlockSpec((ts, go), lambda g, i: (i, g))],
        out_specs=pl.BlockSpec((None, gw, go), lambda g, i: (g, 0, 0)),
        out_shape=jax.ShapeDtypeStruct((n_groups, gw, go), BF16),
        scratch_shapes=[pltpu.VMEM((gw, go), F32)],
        compiler_params=_cp(2),
    )(p, dyb)


def _input_grad_rms(pairs, xres, g, dres, name, after=()):
    s, d = xres.shape
    tm = min(256, s)
    last = N_DEV - 1
    npair = len(pairs)

    def body(*refs):
        ops = refs[:2 * npair]
        x_ref, g_ref, dres_ref = refs[2 * npair:2 * npair + 3]
        dx_ref, dxb_ref, dg_ref, acc_ref = refs[2 * npair + 3 + len(after):]
        i, j = pl.program_id(0), pl.program_id(1)

        @pl.when(j == 0)
        def _():
            acc_ref[...] = jnp.zeros_like(acc_ref)

        @pl.when((i == 0) & (j == 0))
        def _():
            dg_ref[...] = jnp.zeros_like(dg_ref)

        for rs in _chunks(tm, 256):
            part = _dot(ops[0][rs, :], ops[1][...], NT)
            for q in range(1, npair):
                part = part + _dot(ops[2 * q][rs, :], ops[2 * q + 1][...], NT)
            acc_ref[rs, :] += part

        @pl.when(j == last)
        def _():
            for rs in _chunks(tm, 256):
                xv = x_ref[rs, :]
                dh = acc_ref[rs, :]
                r = lax.rsqrt(jnp.mean(xv * xv, axis=-1, keepdims=True) + EPS)
                nrm = xv * r
                dg_ref[...] += jnp.sum(dh * nrm, axis=0, keepdims=True)
                dn = dh * g_ref[...]
                dx = dres_ref[rs, :] + r * (dn - nrm * jnp.mean(dn * nrm, axis=-1, keepdims=True))
                dx_ref[rs, :] = dx
                dxb_ref[rs, :] = dx.astype(BF16)

    in_specs, args = [], []
    for a3, w3 in pairs:
        k = a3.shape[2]
        in_specs += [pl.BlockSpec((None, tm, k), lambda i, j: (j, i, 0)),
                     pl.BlockSpec((None, d, k), lambda i, j: (j, 0, 0))]
        args += [a3, w3]
    row = pl.BlockSpec((tm, d), lambda i, j: (i, 0))
    vec = pl.BlockSpec((1, d), lambda i, j: (0, 0))
    return pl.pallas_call(
        body, name=name, grid=(s // tm, N_DEV),
        in_specs=in_specs + [row, vec, row] + _after_specs(after),
        out_specs=[row, row, vec],
        out_shape=[jax.ShapeDtypeStruct((s, d), F32), jax.ShapeDtypeStruct((s, d), BF16),
                   jax.ShapeDtypeStruct((1, d), F32)],
        scratch_shapes=[pltpu.VMEM((tm, d), F32)],
        compiler_params=_cp(2),
    )(*args, xres, g, dres, *after)


def _wo_bwd(dx1b, wo, ya, yb, proj, b_gate2, pool_scale, after=()):
    s, d = dx1b.shape
    sw = proj.shape[2]
    tn = d // N_DEV
    nq = sw // tn

    def body(dx_ref, wo_ref, ya_ref, yb_ref, ga_ref, gb_ref, bg_ref, sc_ref, *rest):
        dya_ref, dyb_ref, dp_ref, dbg_ref, dsc_ref = rest[len(after):]
        dbg_ref[...] = jnp.zeros_like(dbg_ref)
        dsc_ref[...] = jnp.zeros_like(dsc_ref)
        for rs in _chunks(s, 256):
            dm = _dot(dx_ref[rs, :], wo_ref[...], NT)
            ya_v = ya_ref[rs, :].astype(F32)
            yb_v = yb_ref[rs, :].astype(F32)
            sa = _sigmoid(ga_ref[rs, :] + bg_ref[0:1, :])
            sb = _sigmoid(gb_ref[rs, :] + bg_ref[1:2, :])
            sc = sc_ref[...]
            dya_ref[rs, :] = (dm * sa).astype(BF16)
            dsb = dm * sb
            dyb_ref[rs, :] = (dsb * sc).astype(BF16)
            dsc_ref[...] += jnp.sum(dsb * yb_v, axis=0, keepdims=True)
            dga = dm * ya_v * (sa * (1.0 - sa))
            dgb = dm * (yb_v * sc) * (sb * (1.0 - sb))
            dp_ref[0, rs, :] = dga.astype(BF16)
            dp_ref[1, rs, :] = dgb.astype(BF16)
            dbg_ref[0:1, :] += jnp.sum(dga, axis=0, keepdims=True)
            dbg_ref[1:2, :] += jnp.sum(dgb, axis=0, keepdims=True)

    col = pl.BlockSpec((s, tn), lambda j: (0, j))
    out = jax.ShapeDtypeStruct((s, d), BF16)
    return pl.pallas_call(
        body, name="wo_bwd", grid=(N_DEV,),
        in_specs=[pl.BlockSpec((s, d), lambda j: (0, 0)),
                  pl.BlockSpec((tn, d), lambda j: (j, 0)), col, col,
                  pl.BlockSpec((None, s, tn), lambda j: (4 + j // nq, 0, j % nq)),
                  pl.BlockSpec((None, s, tn), lambda j: (6 + j // nq, 0, j % nq)),
                  pl.BlockSpec((2, tn), lambda j: (0, j)),
                  pl.BlockSpec((1, tn), lambda j: (0, j))] + _after_specs(after),
        out_specs=[col, col,
                   pl.BlockSpec((2, None, s, tn), lambda j: (1, j // nq, 0, j % nq)),
                   pl.BlockSpec((2, tn), lambda j: (0, j)),
                   pl.BlockSpec((1, tn), lambda j: (0, j))],
        out_shape=[out, out, jax.ShapeDtypeStruct((4, 2, s, sw), BF16),
                   jax.ShapeDtypeStruct((2, d), F32), jax.ShapeDtypeStruct((1, d), F32)],
        compiler_params=_cp(1),
    )(dx1b, wo, ya, yb, proj, proj, b_gate2, pool_scale, *after)


def _conv_bwd(dproj, dya, wa, proj, conv_w, conv_b):
    s, d = dya.shape
    sw = wa.shape[0]
    tc = min(LANES, sw)

    def body(dproj_hbm, dya_ref, wa_ref, ba_ref, ca_ref, va_ref, cw_ref, cb_ref,
             dp_ref, dcw_ref, dcb_ref, dz_ref):
        del dproj_hbm
        for rs in _chunks(s, 512):
            dz_ref[rs, :] = _dot(dya_ref[rs, :], wa_ref[...], NT)
        dz = dz_ref[...]
        ba, ca, va = ba_ref[...], ca_ref[...], va_ref[...]
        cv = ca * va
        cv1, cv2 = _shift_down(cv, 1), _shift_down(cv, 2)
        w0, w1, w2 = cw_ref[0:1, :], cw_ref[1:2, :], cw_ref[2:3, :]
        u = cb_ref[...] + w0 * cv2 + w1 * cv1 + w2 * cv
        du = dz * ba
        dp_ref[0] = (dz * u).astype(BF16)
        dcv = w2 * du + w1 * _shift_up(du, 1) + w0 * _shift_up(du, 2)
        dp_ref[1] = (dcv * va).astype(BF16)
        dp_ref[2] = (dcv * ca).astype(BF16)
        dcw_ref[0:1, :] = jnp.sum(du * cv2, axis=0, keepdims=True)
        dcw_ref[1:2, :] = jnp.sum(du * cv1, axis=0, keepdims=True)
        dcw_ref[2:3, :] = jnp.sum(du * cv, axis=0, keepdims=True)
        dcb_ref[...] = jnp.sum(du, axis=0, keepdims=True)

    def part(k):
        return pl.BlockSpec((None, s, tc), lambda i: (k, 0, i))

    return pl.pallas_call(
        body, name="conv_bwd", grid=(sw // tc,),
        in_specs=[pl.BlockSpec(memory_space=pl.ANY),
                  pl.BlockSpec((s, d), lambda i: (0, 0)),
                  pl.BlockSpec((tc, d), lambda i: (i, 0)),
                  part(0), part(1), part(2),
                  pl.BlockSpec((CONV_K, tc), lambda i: (0, i)), pl.BlockSpec((1, tc), lambda i: (0, i))],
        out_specs=[pl.BlockSpec((3, s, tc), lambda i: (0, 0, i)),
                   pl.BlockSpec((CONV_K, tc), lambda i: (0, i)), pl.BlockSpec((1, tc), lambda i: (0, i))],
        out_shape=[jax.ShapeDtypeStruct(dproj.shape, BF16),
                   jax.ShapeDtypeStruct((CONV_K, sw), F32), jax.ShapeDtypeStruct((1, sw), F32)],
        scratch_shapes=[pltpu.VMEM((s, tc), F32)],
        input_output_aliases={0: 0},
        compiler_params=_cp(1),
    )(dproj, dya, wa, proj, proj, proj, conv_w, conv_b)


def _pool_bwd(dproj, dyb, wpool):
    s, d = dyb.shape
    n_groups, gw, go = wpool.shape

    def body(dproj_hbm, dyb_ref, wp_ref, dp_ref):
        del dproj_hbm
        for gi, window in enumerate(POOL_WINDOWS):
            @pl.when(pl.program_id(0) == gi)
            def _():
                dpool = _dot(dyb_ref[...], wp_ref[...], NT)
                acc, k = dpool / _pool_counts(dpool.shape, window), 1
                while k < window:
                    acc = acc + _shift_up(acc, k)
                    k *= 2
                dp_ref[...] = (acc - dpool).astype(BF16)

    return pl.pallas_call(
        body, name="pool_bwd", grid=(n_groups,),
        in_specs=[pl.BlockSpec(memory_space=pl.ANY),
                  pl.BlockSpec((s, go), lambda g: (0, g)),
                  pl.BlockSpec((None, gw, go), lambda g: (g, 0, 0))],
        out_specs=pl.BlockSpec((None, s, gw), lambda g: (3, 0, g)),
        out_shape=jax.ShapeDtypeStruct(dproj.shape, BF16),
        input_output_aliases={0: 0},
        compiler_params=_cp(1),
    )(dproj, dyb, wpool)


def _rows128(v):
    return v.reshape(-1, LANES)


def kernel(x, norm1_g, w_in, b_gate, conv_w, conv_b, w_a_out, w_pool, pool_scale, w_o, norm2_g, w_ffn_gate, w_ffn_up, w_ffn_down, final_g, loss_target, m_norm1_g, m_w_in, m_b_gate, m_conv_w, m_conv_b, m_w_a_out, m_w_pool, m_pool_scale, m_w_o, m_norm2_g, m_w_ffn_gate, m_w_ffn_up, m_w_ffn_down, m_final_g, v_norm1_g, v_w_in, v_b_gate, v_conv_w, v_conv_b, v_w_a_out, v_w_pool, v_pool_scale, v_w_o, v_norm2_g, v_w_ffn_gate, v_w_ffn_up, v_w_ffn_down, v_final_g):
    s, d = x.shape[1], x.shape[2]
    sw = w_in.shape[2]
    n_groups = w_pool.shape[1]
    gw = w_pool.shape[2]
    go = w_pool.shape[3] * N_DEV
    f8 = w_ffn_gate.shape[2]
    cws = conv_w.shape[2]
    assert sw == conv_w.shape[2] * N_DEV == gw * n_groups and go * n_groups == d and n_groups == len(POOL_WINDOWS)

    xi, yi, ci = _coords()
    me = 4 * xi + 2 * yi + ci
    my_chip = 2 * xi + yi

    x2d = x.reshape(s, d)
    target = loss_target.reshape(s, d)
    final_g2 = final_g.reshape(1, d)
    b_gate2 = b_gate.reshape(2, d)

    big_names = ["w_in", "w_a_out", "w_pool", "w_o", "w_ffn_gate", "w_ffn_up", "w_ffn_down"]
    big_w = [w_in, w_a_out, w_pool, w_o, w_ffn_gate, w_ffn_up, w_ffn_down]
    big_m = [m_w_in, m_w_a_out, m_w_pool, m_w_o, m_w_ffn_gate, m_w_ffn_up, m_w_ffn_down]
    big_v = [v_w_in, v_w_a_out, v_w_pool, v_w_o, v_w_ffn_gate, v_w_ffn_up, v_w_ffn_down]
    shapes2d = [(w.size // w.shape[-1], w.shape[-1]) for w in big_w]
    big_w2 = [w.reshape(sh) for w, sh in zip(big_w, shapes2d)]

    sb = [_cast_bf16(w, "cast_" + nm) for w, nm in zip(big_w2, big_names)]
    win_g, wa_g, wpool_g, wo_g = _allgather_big(sb[0:4], "allgather_mixer", COLLECTIVE_GATHER)
    wg_g, wu_g = _allgather_big(sb[4:6], "allgather_ffn_up", COLLECTIVE_GATHER)
    (wd_g,) = _allgather_big(sb[6:7], "allgather_ffn_down", COLLECTIVE_GATHER)
    convw_g = _allgather_small(jnp.pad(conv_w.reshape(CONV_K, cws), ((0, 8 - CONV_K), (0, 0))), "allgather_conv_w")
    conv_w_full = convw_g[:, :CONV_K, :].transpose(1, 0, 2).reshape(CONV_K, sw)
    wa = wa_g.transpose(1, 0, 2).reshape(sw, d)
    wpool = wpool_g.reshape(N_DEV, n_groups, gw, go // N_DEV).transpose(1, 2, 0, 3).reshape(n_groups, gw, go)
    wo = wo_g.reshape(d, d)

    h = _rms_fwd(x2d, norm1_g)
    proj = _proj_fwd(h, win_g)
    z = _conv_fwd(proj, conv_w_full, conv_b)
    p = _pool_fwd(proj)
    ya, yb, merged = _merge_fwd(z, wa, p, wpool, proj, b_gate2, pool_scale)
    x1, h2 = _wo_fwd(merged, wo, x2d, norm2_g)
    gact, uact, act = _ffn_up_act_fwd(h2, wg_g, wu_g)
    ffn_out = _ffn_down_fwd(act, wd_g)
    dx2, dx2b, d_final_g, loss_blk = _loss_bwd(ffn_out, x1, target, final_g2)

    chip_slots = (2 * jnp.arange(4, dtype=jnp.int32) + ci).astype(jnp.int32)

    def partials(grads, recvs, names):
        return [_chip_partial(chip_slots, g3, r, "chip_partial_" + nm) for g3, r, nm in zip(grads, recvs, names)]

    own = jnp.stack([me, my_chip]).astype(jnp.int32)

    def adam(a, g3, sib, chips):
        outs = _adam_big(own, big_w2[a], big_m[a].reshape(shapes2d[a]), big_v[a].reshape(shapes2d[a]),
                         g3, sib, chips, "adam_" + big_names[a])
        return [o.reshape(big_w[a].shape) for o in outs]

    big_out = [None] * len(big_names)
    dg_act, du_act = _ffn_gate_bwd(dx2b, wd_g, gact, uact)
    gw_gate = _wgrad_cols(h2, dg_act, "wgrad_ffn_gate")
    gw_up = _wgrad_cols(h2, du_act, "wgrad_ffn_up")
    sib_gu, ps_gu = _sibling_partials([gw_gate, gw_up], "rs_sibling_ffn_up", COLLECTIVE_SIBLING)
    chips_gu = _exchange_chips(ps_gu, "rs_chips_ffn_up", COLLECTIVE_CHIPS)
    gw_down = _wgrad_rows(act, dx2b, "wgrad_ffn_down", after=ps_gu)
    sib_down, ps_down = _sibling_partials([gw_down], "rs_sibling_ffn_down", COLLECTIVE_SIBLING)
    chips_down = _exchange_chips(ps_down, "rs_chips_ffn_down", COLLECTIVE_CHIPS)
    dh2 = _input_grad([(dg_act, wg_g), (du_act, wu_g)], "ffn_in_bwd", after=ps_down)
    dx1, dx1b, d_norm2_g = _rms_bwd(dh2, x1, norm2_g, dx2, "rms2_bwd")
    dya, dyb, dproj42, d_b_gate, d_pool_scale = _wo_bwd(dx1b, wo, ya, yb, proj, b_gate2, pool_scale)
    gw_o = _wgrad_full(merged, dx1b, "wgrad_o")
    dproj = dproj42.reshape(N_DEV, s, sw)
    dproj, d_conv_w, d_conv_b = _conv_bwd(dproj, dya, wa, proj, conv_w_full, conv_b)
    dproj = _pool_bwd(dproj, dyb, wpool)
    gw_a = _wgrad_full(z, dya, "wgrad_a_out")
    gw_pool = _wgrad_pool(p, dyb, n_groups)
    mix3 = [gw_a.reshape(sw, N_DEV, d // N_DEV).transpose(1, 0, 2),
            gw_pool.reshape(n_groups, gw, N_DEV, go // N_DEV).transpose(2, 0, 1, 3).reshape(N_DEV, n_groups * gw, go // N_DEV),
            gw_o.reshape(N_DEV, d // N_DEV, d)]
    sib_mix, ps_mix = _sibling_partials(mix3, "rs_sibling_mixer", COLLECTIVE_SIBLING)
    chips_mix = _exchange_chips(ps_mix, "rs_chips_mixer", COLLECTIVE_CHIPS)
    big_out[4] = adam(4, gw_gate, sib_gu[0], chips_gu[0])
    big_out[5] = adam(5, gw_up, sib_gu[1], chips_gu[1])
    gw_in = _wgrad_cols(h, dproj, "wgrad_in", after=ps_mix + [big_out[4][0], big_out[5][0]])
    sib_in, ps_in = _sibling_partials([gw_in], "rs_sibling_w_in", COLLECTIVE_SIBLING)
    big_out[6] = adam(6, gw_down, sib_down[0], chips_down[0])
    chips_in = _exchange_chips(ps_in, "rs_chips_w_in", COLLECTIVE_CHIPS)
    dh = _input_grad([(dproj, win_g)], "proj_in_bwd", after=ps_mix + ps_in + [big_out[6][0]])
    grad_x, _, d_norm1_g = _rms_bwd(dh, x2d, norm1_g, dx1, "rms1_bwd")
    for k in range(3):
        big_out[1 + k] = adam(1 + k, mix3[k], sib_mix[k], chips_mix[k])
    big_out[0] = adam(0, gw_in, sib_in[0], chips_in[0])

    small_parts = [d_norm1_g, d_b_gate, d_conv_w, d_conv_b, d_pool_scale, d_norm2_g, d_final_g, loss_blk]
    sizes = [v.size for v in small_parts]
    packed = jnp.concatenate([_rows128(v) for v in small_parts], axis=0)
    summed = _sum_small(_allgather_small(packed, "allgather_small_grads")).reshape(-1)
    offs = [0]
    for n in sizes:
        offs.append(offs[-1] + n)
    g_norm1, g_bgate, g_convw_full, g_convb, g_pscale, g_norm2, g_final, loss_sum = [
        summed[offs[k]:offs[k + 1]] for k in range(len(sizes))]
    loss = loss_sum[0]
    g_convw = lax.dynamic_slice(g_convw_full.reshape(CONV_K, sw), (0, me * cws), (CONV_K, cws))
    small_w = [norm1_g, b_gate, conv_w, conv_b, pool_scale, norm2_g, final_g]
    small_m = [m_norm1_g, m_b_gate, m_conv_w, m_conv_b, m_pool_scale, m_norm2_g, m_final_g]
    small_v = [v_norm1_g, v_b_gate, v_conv_w, v_conv_b, v_pool_scale, v_norm2_g, v_final_g]
    small_g = [g_norm1, g_bgate, g_convw, g_convb, g_pscale, g_norm2, g_final]

    def pack(parts):
        flat = jnp.concatenate([v.reshape(-1) for v in parts])
        pad = (-flat.size) % (8 * LANES)
        return jnp.pad(flat, (0, pad)).reshape(-1, LANES)

    s_delta, s_m, s_v = _adam_small(pack(small_w), pack(small_g), pack(small_m), pack(small_v))
    soffs = [0]
    for w in small_w:
        soffs.append(soffs[-1] + w.size)

    def unpack(buf):
        flat = buf.reshape(-1)
        return [flat[soffs[k]:soffs[k + 1]].reshape(small_w[k].shape) for k in range(len(small_w))]

    small_grads = [g.reshape(w.shape) for g, w in zip(small_g, small_w)]
    small_delta, small_new_m, small_new_v = unpack(s_delta), unpack(s_m), unpack(s_v)

    order = ["norm1_g", "w_in", "b_gate", "conv_w", "conv_b", "w_a_out", "w_pool", "pool_scale", "w_o", "norm2_g",
             "w_ffn_gate", "w_ffn_up", "w_ffn_down", "final_g"]
    small_names = ["norm1_g", "b_gate", "conv_w", "conv_b", "pool_scale", "norm2_g", "final_g"]
    per_kind = [{}, {}, {}, {}]
    for a, nm in enumerate(big_names):
        for kind in range(4):
            per_kind[kind][nm] = big_out[a][kind]
    for k, nm in enumerate(small_names):
        per_kind[0][nm] = small_grads[k]
        per_kind[1][nm] = small_delta[k]
        per_kind[2][nm] = small_new_m[k]
        per_kind[3][nm] = small_new_v[k]
    result = [loss, grad_x.reshape(x.shape)]
    for kind in range(4):
        result += [per_kind[kind][nm] for nm in order]
    return tuple(result)
```

```python
import functools

import jax
import jax.numpy as jnp
from jax import lax
from jax.experimental import pallas as pl
from jax.experimental.pallas import tpu as pltpu
from jax.experimental.pallas import tpu_sc as plsc

F32 = jnp.float32
BF16 = jnp.bfloat16
MESH = pl.DeviceIdType.MESH

N_DEV = 8
EPS = 1e-6
CONV_K = 3
POOL_WINDOWS = (2, 4, 8, 16)
ADAM_LR = 0.001
ADAM_B1 = 0.9
ADAM_B2 = 0.999
ADAM_EPS = 1e-08
ADAM_WD = 0.01
ADAM_STEP = 10

V7X_VMEM_LIMIT_BYTES = 56 * 1024 * 1024
LANES = 128

COLLECTIVE_GATHER = 1
COLLECTIVE_SIBLING = 2
COLLECTIVE_CHIPS = 3
SEQUENCER_COST_BYTES = 4 * 10**9

NN = ((1,), (0,))
NT = ((1,), (1,))
TN = ((0,), (0,))


def _dot(a, b, dims):
    return lax.dot_general(a, b, (dims, ((), ())), preferred_element_type=F32)


def _cp(n_axes):
    return pltpu.CompilerParams(dimension_semantics=("arbitrary",) * n_axes,
                                vmem_limit_bytes=V7X_VMEM_LIMIT_BYTES)


def _row_tile(rows, bytes_per_row, cap_bytes):
    best = None
    for t in range(16, rows + 1, 16):
        if rows % t == 0 and t * bytes_per_row <= cap_bytes:
            best = t
    return best if best is not None else rows


def _chunks(total, size):
    size = min(size, total)
    assert total % size == 0
    return [slice(r, r + size) for r in range(0, total, size)]


def _after_specs(after):
    return [pl.BlockSpec(memory_space=pl.ANY)] * len(after)


def _shift_down(v, k):
    row = lax.broadcasted_iota(jnp.int32, v.shape, 0)
    return jnp.where(row >= k, pltpu.roll(v, k, 0), 0.0)


def _shift_up(v, k):
    n = v.shape[0]
    row = lax.broadcasted_iota(jnp.int32, v.shape, 0)
    return jnp.where(row < n - k, pltpu.roll(v, n - k, 0), 0.0)


def _sigmoid(v):
    return jax.nn.sigmoid(v)


def _cast_bf16(w2d, name):
    rows, cols = w2d.shape
    tr = _row_tile(rows, cols * 4, 2 << 20)

    def body(i_ref, o_ref):
        o_ref[...] = i_ref[...].astype(BF16)

    return pl.pallas_call(
        body, name=name, grid=(rows // tr,),
        in_specs=[pl.BlockSpec((tr, cols), lambda i: (i, 0))],
        out_specs=pl.BlockSpec((tr, cols), lambda i: (i, 0)),
        out_shape=jax.ShapeDtypeStruct((rows, cols), BF16),
        compiler_params=_cp(1),
    )(w2d)


def _rms_fwd(x2d, g):
    s, d = x2d.shape
    tm = min(256, s)

    def body(x_ref, g_ref, h_ref):
        xv = x_ref[...]
        r = lax.rsqrt(jnp.mean(xv * xv, axis=-1, keepdims=True) + EPS)
        h_ref[...] = (xv * r * g_ref[...]).astype(BF16)

    return pl.pallas_call(
        body, name="rms1_fwd", grid=(s // tm,),
        in_specs=[pl.BlockSpec((tm, d), lambda i: (i, 0)), pl.BlockSpec((1, d), lambda i: (0, 0))],
        out_specs=pl.BlockSpec((tm, d), lambda i: (i, 0)),
        out_shape=jax.ShapeDtypeStruct((s, d), BF16),
        compiler_params=_cp(1),
    )(x2d, g)


def _coords():
    return lax.axis_index("x"), lax.axis_index("y"), lax.axis_index("c")


def _slot(p):
    return 4 * p[0] + 2 * p[1] + p[2]


def _handshake(peers):
    barrier = pltpu.get_barrier_semaphore()
    for peer in peers:
        pl.semaphore_signal(barrier, inc=1, device_id=peer, device_id_type=MESH)
    pl.semaphore_wait(barrier, len(peers))


def _sequencer_call(body, out_type, scratch_types, name, collective_id):
    return pl.kernel(
        body, out_type=out_type, name=name,
        mesh=plsc.ScalarSubcoreMesh(axis_name="seq", num_cores=1),
        scratch_types=scratch_types,
        cost_estimate=pl.CostEstimate(flops=0, transcendentals=0, bytes_accessed=SEQUENCER_COST_BYTES),
        compiler_params=pltpu.CompilerParams(collective_id=collective_id))


def _allgather_big(shards, name, collective_id, after=()):
    n = len(shards)

    def body(*refs):
        ins, outs = refs[:n], refs[n + len(after):2 * n + len(after)]
        send_sems, recv_sems, local_sems = refs[2 * n + len(after):]
        x, y, c = _coords()
        me, sibling = (x, y, c), (x, y, 1 - c)
        x_nbr, y_nbr, diag = (1 - x, y), (x, 1 - y), (1 - x, 1 - y)
        relay_from = (x + (1 - c) * (1 - 2 * x), y + c * (1 - 2 * y))
        relay_to = (x + c * (1 - 2 * x), y + (1 - c) * (1 - 2 * y))
        _handshake([sibling, (*x_nbr, c), (*y_nbr, c)])

        def copy(a, k, block, to, src=None):
            dst = outs[a].at[_slot(block)]
            return pltpu.make_async_remote_copy(
                src_ref=dst if src is None else src, dst_ref=dst,
                send_sem=send_sems.at[a, k], recv_sem=recv_sems.at[a, k],
                device_id=to, device_id_type=MESH)

        mine, sends = [], []
        for a in range(n):
            cp = pltpu.make_async_copy(ins[a], outs[a].at[_slot(me)], local_sems.at[a])
            cp.start()
            mine.append(cp)
            first = [copy(a, 0, me, sibling, src=ins[a]),
                     copy(a, 1, me, (*x_nbr, c), src=ins[a]),
                     copy(a, 2, me, (*y_nbr, c), src=ins[a])]
            for cp in first:
                cp.start()
            sends += first
        for a in range(n):
            copy(a, 1 + c, (*relay_from, c), me).wait_recv()
            passed = [copy(a, 3, (*relay_from, c), (*relay_to, c)), copy(a, 4 + c, (*relay_from, c), sibling)]
            for cp in passed:
                cp.start()
            copy(a, 2 - c, (*relay_to, c), me).wait_recv()
            cp = copy(a, 5 - c, (*relay_to, c), sibling)
            cp.start()
            passed.append(cp)
            copy(a, 3, (*diag, c), me).wait_recv()
            cp = copy(a, 6, (*diag, c), sibling)
            cp.start()
            sends += passed + [cp]
        for a in range(n):
            copy(a, 0, sibling, me).wait_recv()
            copy(a, 4, (*x_nbr, 1 - c), me).wait_recv()
            copy(a, 5, (*y_nbr, 1 - c), me).wait_recv()
            copy(a, 6, (*diag, 1 - c), me).wait_recv()
        for cp in sends:
            cp.wait_send()
        for cp in mine:
            cp.wait()

    return _sequencer_call(
        body, [jax.ShapeDtypeStruct((N_DEV,) + s.shape, s.dtype) for s in shards],
        [pltpu.SemaphoreType.DMA((n, 7)), pltpu.SemaphoreType.DMA((n, 7)), pltpu.SemaphoreType.DMA((n,))],
        name, collective_id)(*shards, *after)


def _exchange_sibling(grads, name, collective_id):
    n = len(grads)

    def body(*refs):
        ins, outs = refs[:n], refs[n:2 * n]
        send_sems, recv_sems = refs[2 * n:]
        x, y, c = _coords()
        sibling = (x, y, 1 - c)
        _handshake([sibling])
        copies = []
        for a in range(n):
            for q in range(4):
                cp = pltpu.make_async_remote_copy(
                    src_ref=ins[a].at[2 * q + (1 - c)], dst_ref=outs[a].at[q],
                    send_sem=send_sems.at[a, q], recv_sem=recv_sems.at[a, q],
                    device_id=sibling, device_id_type=MESH)
                cp.start()
                copies.append(cp)
        for cp in copies:
            cp.wait_recv()
        for cp in copies:
            cp.wait_send()

    any_spec = pl.BlockSpec(memory_space=pl.ANY)
    return pl.pallas_call(
        body, name=name,
        in_specs=[any_spec] * n, out_specs=[any_spec] * n,
        out_shape=[jax.ShapeDtypeStruct((4,) + g.shape[1:], g.dtype) for g in grads],
        scratch_shapes=[pltpu.SemaphoreType.DMA((n, 4)), pltpu.SemaphoreType.DMA((n, 4))],
        compiler_params=pltpu.CompilerParams(collective_id=collective_id),
    )(*grads)


def _sibling_partials(grads, name, collective_id):
    n = len(grads)

    def body(*refs):
        ins, recvs, psums = refs[:n], refs[n:2 * n], refs[2 * n:3 * n]
        send_sems, recv_sems = refs[3 * n:]
        x, y, c = _coords()
        sibling = (x, y, 1 - c)
        my_chip = 2 * x + y
        _handshake([sibling])
        copies = []
        for a in range(n):
            for q in range(4):
                cp = pltpu.make_async_remote_copy(
                    src_ref=ins[a].at[2 * q + (1 - c)], dst_ref=recvs[a].at[q],
                    send_sem=send_sems.at[a, q], recv_sem=recv_sems.at[a, q],
                    device_id=sibling, device_id_type=MESH)
                cp.start()
                copies.append(cp)

        def add(g_ref, r_ref, o_ref):
            o_ref[...] = (g_ref[...].astype(F32) + r_ref[...].astype(F32)).astype(BF16)

        for a in range(n):
            _, rows, cols = grads[a].shape
            tr = _row_tile(rows, cols * 2, 1 << 20)
            blk = pl.BlockSpec((tr, cols), lambda i: (i, 0))
            for q in range(4):
                copies[4 * a + q].wait_recv()

                @pl.when(q != my_chip)
                def _():
                    pltpu.emit_pipeline(add, grid=(rows // tr,), in_specs=[blk, blk], out_specs=[blk])(
                        ins[a].at[2 * q + c], recvs[a].at[q], psums[a].at[q])
        for cp in copies:
            cp.wait_send()

    any_spec = pl.BlockSpec(memory_space=pl.ANY)
    shapes = [jax.ShapeDtypeStruct((4,) + g.shape[1:], g.dtype) for g in grads]
    outs = pl.pallas_call(
        body, name=name,
        in_specs=[any_spec] * n, out_specs=[any_spec] * (2 * n),
        out_shape=shapes + shapes,
        scratch_shapes=[pltpu.SemaphoreType.DMA((n, 4)), pltpu.SemaphoreType.DMA((n, 4))],
        compiler_params=pltpu.CompilerParams(collective_id=collective_id,
                                             vmem_limit_bytes=V7X_VMEM_LIMIT_BYTES),
    )(*grads)
    return list(outs[:n]), list(outs[n:])


def _exchange_chips(psums, name, collective_id):
    n = len(psums)

    def body(*refs):
        ins, outs = refs[:n], refs[n:2 * n]
        send_sems, recv_sems = refs[2 * n:]
        x, y, c = _coords()
        chips = [(1 - x, y), (x, 1 - y), (1 - x, 1 - y)]
        _handshake([(*chip, c) for chip in chips])
        copies = []
        for a in range(n):
            for j, chip in enumerate(chips):
                cp = pltpu.make_async_remote_copy(
                    src_ref=ins[a].at[2 * chip[0] + chip[1]], dst_ref=outs[a].at[j],
                    send_sem=send_sems.at[a, j], recv_sem=recv_sems.at[a, j],
                    device_id=(*chip, c), device_id_type=MESH)
                cp.start()
                copies.append(cp)
        for cp in copies:
            cp.wait_recv()
        for cp in copies:
            cp.wait_send()

    return _sequencer_call(
        body, [jax.ShapeDtypeStruct((3,) + p.shape[1:], p.dtype) for p in psums],
        [pltpu.SemaphoreType.DMA((n, 3)), pltpu.SemaphoreType.DMA((n, 3))],
        name, collective_id)(*psums)


def _allgather_small(v2d, name):
    rows, cols = v2d.shape

    def body(v_ref, out_ref, send_sems, recv_sems):
        x, y, c = _coords()
        me = (x, y, c)
        out_ref[_slot(me)] = v_ref[...]
        peers = []
        for k in range(1, N_DEV):
            fx, fy, fc = (k >> 2) & 1, (k >> 1) & 1, k & 1
            peers.append(((1 - x) if fx else x, (1 - y) if fy else y, (1 - c) if fc else c))
        sends = []
        for k, peer in enumerate(peers):
            cp = pltpu.make_async_remote_copy(
                src_ref=v_ref, dst_ref=out_ref.at[_slot(me)],
                send_sem=send_sems.at[k], recv_sem=recv_sems.at[k],
                device_id=peer, device_id_type=MESH)
            cp.start()
            sends.append(cp)
        for k, peer in enumerate(peers):
            pltpu.make_async_remote_copy(
                src_ref=v_ref, dst_ref=out_ref.at[_slot(peer)],
                send_sem=send_sems.at[k], recv_sem=recv_sems.at[k],
                device_id=peer, device_id_type=MESH).wait_recv()
        for cp in sends:
            cp.wait_send()

    vmem = pl.BlockSpec(memory_space=pltpu.VMEM)
    return pl.pallas_call(
        body, name=name, in_specs=[vmem], out_specs=vmem,
        out_shape=jax.ShapeDtypeStruct((N_DEV, rows, cols), v2d.dtype),
        scratch_shapes=[pltpu.SemaphoreType.DMA((N_DEV - 1,)), pltpu.SemaphoreType.DMA((N_DEV - 1,))],
    )(v2d)


def _chip_partial(slots, g3, recv, name):
    _, rows, cols = g3.shape
    tr = _row_tile(rows, cols * 2, 2 << 20)

    def body(slots_ref, g_ref, r_ref, o_ref):
        o_ref[...] = (g_ref[...].astype(F32) + r_ref[...].astype(F32)).astype(BF16)

    return pl.pallas_call(
        body, name=name,
        grid_spec=pltpu.PrefetchScalarGridSpec(
            num_scalar_prefetch=1, grid=(4, rows // tr),
            in_specs=[pl.BlockSpec((None, tr, cols), lambda q, i, sl: (sl[q], i, 0)),
                      pl.BlockSpec((None, tr, cols), lambda q, i, sl: (q, i, 0))],
            out_specs=pl.BlockSpec((None, tr, cols), lambda q, i, sl: (q, i, 0))),
        out_shape=jax.ShapeDtypeStruct((4, rows, cols), BF16),
        compiler_params=_cp(2),
    )(slots, g3, recv)


def _adam_math(w, g, m, v):
    m = ADAM_B1 * m + (1.0 - ADAM_B1) * g
    v = ADAM_B2 * v + (1.0 - ADAM_B2) * (g * g)
    m_hat = m / (1.0 - ADAM_B1 ** ADAM_STEP)
    v_hat = v / (1.0 - ADAM_B2 ** ADAM_STEP)
    delta = -ADAM_LR * (m_hat / (jnp.sqrt(v_hat) + ADAM_EPS) + ADAM_WD * w)
    return delta, m, v


def _adam_big(own, w, m, v, g3, recv_sib, recv_chips, name):
    rows, cols = w.shape
    tr = _row_tile(rows, cols * 4, 2 << 20)

    def body(own_ref, w_ref, m_ref, v_ref, g_ref, rs_ref, rc_ref, go_ref, do_ref, mo_ref, vo_ref):
        g = g_ref[...].astype(F32) + rs_ref[...].astype(F32)
        g = g + rc_ref[0].astype(F32)
        g = g + rc_ref[1].astype(F32)
        g = g + rc_ref[2].astype(F32)
        delta, m_new, v_new = _adam_math(w_ref[...], g, m_ref[...], v_ref[...])
        go_ref[...] = g
        do_ref[...] = delta
        mo_ref[...] = m_new
        vo_ref[...] = v_new

    blk = pl.BlockSpec((tr, cols), lambda i, o: (i, 0))
    out = jax.ShapeDtypeStruct((rows, cols), F32)
    return pl.pallas_call(
        body, name=name,
        grid_spec=pltpu.PrefetchScalarGridSpec(
            num_scalar_prefetch=1, grid=(rows // tr,),
            in_specs=[blk, blk, blk,
                      pl.BlockSpec((None, tr, cols), lambda i, o: (o[0], i, 0)),
                      pl.BlockSpec((None, tr, cols), lambda i, o: (o[1], i, 0)),
                      pl.BlockSpec((3, tr, cols), lambda i, o: (0, i, 0))],
            out_specs=[blk, blk, blk, blk]),
        out_shape=[out, out, out, out],
        compiler_params=_cp(1),
    )(own, w, m, v, g3, recv_sib, recv_chips)


def _sum_small(gathered):
    _, rows, cols = gathered.shape

    def body(g_ref, o_ref):
        acc = g_ref[0]
        for k in range(1, N_DEV):
            acc = acc + g_ref[k]
        o_ref[...] = acc

    vmem = pl.BlockSpec(memory_space=pltpu.VMEM)
    return pl.pallas_call(body, name="small_grad_sum", in_specs=[vmem], out_specs=vmem,
                          out_shape=jax.ShapeDtypeStruct((rows, cols), F32))(gathered)


def _adam_small(w, g, m, v):
    def body(w_ref, g_ref, m_ref, v_ref, do_ref, mo_ref, vo_ref):
        delta, m_new, v_new = _adam_math(w_ref[...], g_ref[...], m_ref[...], v_ref[...])
        do_ref[...] = delta
        mo_ref[...] = m_new
        vo_ref[...] = v_new

    vmem = pl.BlockSpec(memory_space=pltpu.VMEM)
    out = jax.ShapeDtypeStruct(w.shape, F32)
    return pl.pallas_call(body, name="adam_small", in_specs=[vmem] * 4, out_specs=[vmem] * 3,
                          out_shape=[out, out, out])(w, g, m, v)


def _proj_fwd(h, win_g):
    s, d = h.shape
    sw = win_g.shape[2]
    tn = min(512, sw)
    nh = sw // tn

    def body(h_ref, w_ref, o_ref):
        for rs in _chunks(s, 512):
            o_ref[rs, :] = _dot(h_ref[rs, :], w_ref[...], NN)

    return pl.pallas_call(
        body, name="proj_fwd", grid=(N_DEV * nh,),
        in_specs=[pl.BlockSpec((s, d), lambda j: (0, 0)),
                  pl.BlockSpec((None, d, tn), lambda j: (j // nh, 0, j % nh))],
        out_specs=pl.BlockSpec((None, s, tn), lambda j: (j // nh, 0, j % nh)),
        out_shape=jax.ShapeDtypeStruct((N_DEV, s, sw), F32),
        compiler_params=_cp(1),
    )(h, win_g)


def _conv_fwd(proj, conv_w, conv_b):
    _, s, sw = proj.shape
    tc = min(LANES, sw)

    def body(ba_ref, ca_ref, va_ref, cw_ref, cb_ref, z_ref):
        cv = ca_ref[...] * va_ref[...]
        u = (cb_ref[...] + cw_ref[0:1, :] * _shift_down(cv, 2) + cw_ref[1:2, :] * _shift_down(cv, 1)
             + cw_ref[2:3, :] * cv)
        z_ref[...] = (ba_ref[...] * u).astype(BF16)

    def part(k):
        return pl.BlockSpec((None, s, tc), lambda i: (k, 0, i))

    return pl.pallas_call(
        body, name="conv_fwd", grid=(sw // tc,),
        in_specs=[part(0), part(1), part(2),
                  pl.BlockSpec((CONV_K, tc), lambda i: (0, i)), pl.BlockSpec((1, tc), lambda i: (0, i))],
        out_specs=pl.BlockSpec((s, tc), lambda i: (0, i)),
        out_shape=jax.ShapeDtypeStruct((s, sw), BF16),
        compiler_params=_cp(1),
    )(proj, proj, proj, conv_w, conv_b)


def _pool_counts(shape, window):
    t = lax.broadcasted_iota(jnp.int32, shape, 0)
    return jnp.minimum(t + 1, window).astype(F32)


def _pool_fwd(proj):
    _, s, sw = proj.shape
    gw = sw // len(POOL_WINDOWS)

    def body(v_ref, p_ref):
        for gi, window in enumerate(POOL_WINDOWS):
            @pl.when(pl.program_id(0) == gi)
            def _():
                v = v_ref[...]
                acc, k = v, 1
                while k < window:
                    acc = acc + _shift_down(acc, k)
                    k *= 2
                p_ref[...] = (acc / _pool_counts(v.shape, window) - v).astype(BF16)

    return pl.pallas_call(
        body, name="pool_fwd", grid=(len(POOL_WINDOWS),),
        in_specs=[pl.BlockSpec((None, s, gw), lambda g: (3, 0, g))],
        out_specs=pl.BlockSpec((s, gw), lambda g: (0, g)),
        out_shape=jax.ShapeDtypeStruct((s, sw), BF16),
        compiler_params=_cp(1),
    )(proj)


def _merge_fwd(z, wa, p, wpool, proj, b_gate2, pool_scale):
    s, sw = z.shape
    tn = wa.shape[2]
    d = tn * N_DEV
    gw = sw // len(POOL_WINDOWS)
    nq = sw // tn

    def body(z_ref, wa_ref, p_ref, wp_ref, ga_ref, gb_ref, bg_ref, sc_ref, ya_ref, yb_ref, m_ref):
        for rs in _chunks(s, 512):
            ya = _dot(z_ref[rs, :], wa_ref[...], NN)
            yb = _dot(p_ref[rs, :], wp_ref[...], NN)
            sa = _sigmoid(ga_ref[rs, :] + bg_ref[0:1, :])
            sb = _sigmoid(gb_ref[rs, :] + bg_ref[1:2, :])
            ya_ref[rs, :] = ya.astype(BF16)
            yb_ref[rs, :] = yb.astype(BF16)
            m_ref[rs, :] = (sa * ya + sb * (yb * sc_ref[...])).astype(BF16)

    col = pl.BlockSpec((s, tn), lambda j: (0, j))
    out = jax.ShapeDtypeStruct((s, d), BF16)
    return pl.pallas_call(
        body, name="merge_fwd", grid=(N_DEV,),
        in_specs=[pl.BlockSpec((s, sw), lambda j: (0, 0)),
                  pl.BlockSpec((None, sw, tn), lambda j: (j, 0, 0)),
                  pl.BlockSpec((s, gw), lambda j: (0, j // 2)),
                  pl.BlockSpec((None, gw, tn), lambda j: (j // 2, 0, j % 2)),
                  pl.BlockSpec((None, s, tn), lambda j: (4 + j // nq, 0, j % nq)),
                  pl.BlockSpec((None, s, tn), lambda j: (6 + j // nq, 0, j % nq)),
                  pl.BlockSpec((2, tn), lambda j: (0, j)),
                  pl.BlockSpec((1, tn), lambda j: (0, j))],
        out_specs=[col, col, col],
        out_shape=[out, out, out],
        compiler_params=_cp(1),
    )(z, wa, p, wpool, proj, proj, b_gate2, pool_scale)


def _wo_fwd(merged, wo, x2d, g2):
    s, d = x2d.shape
    tm = min(256, s)

    def body(m_ref, wo_ref, x_ref, g_ref, x1_ref, h2_ref):
        x1 = x_ref[...] + _dot(m_ref[...], wo_ref[...], NN)
        x1_ref[...] = x1
        r = lax.rsqrt(jnp.mean(x1 * x1, axis=-1, keepdims=True) + EPS)
        h2_ref[...] = (x1 * r * g_ref[...]).astype(BF16)

    row = pl.BlockSpec((tm, d), lambda i: (i, 0))
    return pl.pallas_call(
        body, name="wo_fwd", grid=(s // tm,),
        in_specs=[row, pl.BlockSpec((d, d), lambda i: (0, 0)), row, pl.BlockSpec((1, d), lambda i: (0, 0))],
        out_specs=[row, row],
        out_shape=[jax.ShapeDtypeStruct((s, d), F32), jax.ShapeDtypeStruct((s, d), BF16)],
        compiler_params=_cp(1),
    )(merged, wo, x2d, g2)


def _ffn_up_fwd(h2, wg_g, wu_g):
    s, d = h2.shape
    f8 = wg_g.shape[2]

    def body(h_ref, wg_ref, wu_ref, g_ref, u_ref):
        for rs in _chunks(s, 512):
            a = h_ref[rs, :]
            g_ref[rs, :] = _dot(a, wg_ref[...], NN).astype(BF16)
            u_ref[rs, :] = _dot(a, wu_ref[...], NN).astype(BF16)

    wspec = pl.BlockSpec((None, d, f8), lambda j: (j, 0, 0))
    ospec = pl.BlockSpec((None, s, f8), lambda j: (j, 0, 0))
    out = jax.ShapeDtypeStruct((N_DEV, s, f8), BF16)
    return pl.pallas_call(
        body, name="ffn_up_fwd", grid=(N_DEV,),
        in_specs=[pl.BlockSpec((s, d), lambda j: (0, 0)), wspec, wspec],
        out_specs=[ospec, ospec], out_shape=[out, out],
        compiler_params=_cp(1),
    )(h2, wg_g, wu_g)


def _ffn_down_loss(gact, uact, wd_g, x1, target, final_g):
    _, s, f8 = gact.shape
    d = x1.shape[1]
    tm = min(256, s)
    last = N_DEV - 1

    def body(g_ref, u_ref, wd_ref, x1_ref, t_ref, gf_ref, dx_ref, dxb_ref, dgf_ref, loss_ref, acc_ref):
        i, j = pl.program_id(0), pl.program_id(1)

        @pl.when(j == 0)
        def _():
            acc_ref[...] = jnp.zeros_like(acc_ref)

        @pl.when((i == 0) & (j == 0))
        def _():
            dgf_ref[...] = jnp.zeros_like(dgf_ref)
            loss_ref[...] = jnp.zeros_like(loss_ref)

        for rs in _chunks(tm, 256):
            g = g_ref[rs, :].astype(F32)
            act = (g * _sigmoid(g) * u_ref[rs, :].astype(F32)).astype(BF16)
            acc_ref[rs, :] += _dot(act, wd_ref[...], NN)

        @pl.when(j == last)
        def _():
            for rs in _chunks(tm, 256):
                x2 = x1_ref[rs, :] + acc_ref[rs, :]
                r = lax.rsqrt(jnp.mean(x2 * x2, axis=-1, keepdims=True) + EPS)
                nrm = x2 * r
                gf = gf_ref[...]
                err = nrm * gf - t_ref[rs, :]
                loss_ref[...] += jnp.sum(err * err) * (0.5 / d)
                dy = err * (1.0 / d)
                dgf_ref[...] += jnp.sum(dy * nrm, axis=0, keepdims=True)
                dn = dy * gf
                dx = r * (dn - nrm * jnp.mean(dn * nrm, axis=-1, keepdims=True))
                dx_ref[rs, :] = dx
                dxb_ref[rs, :] = dx.astype(BF16)

    aspec = pl.BlockSpec((None, tm, f8), lambda i, j: (j, i, 0))
    row = pl.BlockSpec((tm, d), lambda i, j: (i, 0))
    return pl.pallas_call(
        body, name="ffn_down_loss", grid=(s // tm, N_DEV),
        in_specs=[aspec, aspec, pl.BlockSpec((None, f8, d), lambda i, j: (j, 0, 0)), row, row,
                  pl.BlockSpec((1, d), lambda i, j: (0, 0))],
        out_specs=[row, row, pl.BlockSpec((1, d), lambda i, j: (0, 0)),
                   pl.BlockSpec((8, LANES), lambda i, j: (0, 0))],
        out_shape=[jax.ShapeDtypeStruct((s, d), F32), jax.ShapeDtypeStruct((s, d), BF16),
                   jax.ShapeDtypeStruct((1, d), F32), jax.ShapeDtypeStruct((8, LANES), F32)],
        scratch_shapes=[pltpu.VMEM((tm, d), F32)],
        compiler_params=_cp(2),
    )(gact, uact, wd_g, x1, target, final_g)


def _ffn_act_bwd(dx2b, wd_g, gact, uact):
    s, d = dx2b.shape
    f8 = gact.shape[2]
    tm = min(1024, s)

    def body(dx_ref, wd_ref, g_ref, u_ref, dg_ref, du_ref, act_ref):
        for rs in _chunks(tm, 256):
            da = _dot(dx_ref[rs, :], wd_ref[...], NT)
            g = g_ref[rs, :].astype(F32)
            u = u_ref[rs, :].astype(F32)
            sg = _sigmoid(g)
            silu = g * sg
            act_ref[rs, :] = (silu * u).astype(BF16)
            du_ref[rs, :] = (da * silu).astype(BF16)
            dg_ref[rs, :] = (da * u * (sg * (1.0 + g * (1.0 - sg)))).astype(BF16)

    aspec = pl.BlockSpec((None, tm, f8), lambda j, i: (j, i, 0))
    out = jax.ShapeDtypeStruct((N_DEV, s, f8), BF16)
    return pl.pallas_call(
        body, name="ffn_act_bwd", grid=(N_DEV, s // tm),
        in_specs=[pl.BlockSpec((tm, d), lambda j, i: (i, 0)),
                  pl.BlockSpec((None, f8, d), lambda j, i: (j, 0, 0)), aspec, aspec],
        out_specs=[aspec, aspec, aspec], out_shape=[out, out, out],
        compiler_params=_cp(2),
    )(dx2b, wd_g, gact, uact)


def _wgrad_shard_a(a3, b, name, after=()):
    _, s, k = a3.shape
    n = b.shape[1]
    ts = min(512, s)
    ns = s // ts

    def body(a_ref, b_ref, *rest):
        o_ref, acc_ref = rest[len(after):]
        i = pl.program_id(1)

        @pl.when(i == 0)
        def _():
            acc_ref[...] = jnp.zeros_like(acc_ref)

        acc_ref[...] += _dot(a_ref[...], b_ref[...], TN)

        @pl.when(i == ns - 1)
        def _():
            o_ref[...] = acc_ref[...].astype(BF16)

    return pl.pallas_call(
        body, name=name, grid=(N_DEV, ns),
        in_specs=[pl.BlockSpec((None, ts, k), lambda j, i: (j, i, 0)),
                  pl.BlockSpec((ts, n), lambda j, i: (i, 0))] + _after_specs(after),
        out_specs=pl.BlockSpec((None, k, n), lambda j, i: (j, 0, 0)),
        out_shape=jax.ShapeDtypeStruct((N_DEV, k, n), BF16),
        scratch_shapes=[pltpu.VMEM((k, n), F32)],
        compiler_params=_cp(2),
    )(a3, b, *after)


def _wgrad_shard_b(a, b3, name, after=()):
    s, k = a.shape
    n = b3.shape[2]
    ts = min(512, s)
    ns = s // ts

    def body(a_ref, b_ref, *rest):
        o_ref, acc_ref = rest[len(after):]
        i = pl.program_id(1)

        @pl.when(i == 0)
        def _():
            acc_ref[...] = jnp.zeros_like(acc_ref)

        acc_ref[...] += _dot(a_ref[...], b_ref[...], TN)

        @pl.when(i == ns - 1)
        def _():
            o_ref[...] = acc_ref[...].astype(BF16)

    return pl.pallas_call(
        body, name=name, grid=(N_DEV, ns),
        in_specs=[pl.BlockSpec((ts, k), lambda j, i: (i, 0)),
                  pl.BlockSpec((None, ts, n), lambda j, i: (j, i, 0))] + _after_specs(after),
        out_specs=pl.BlockSpec((None, k, n), lambda j, i: (j, 0, 0)),
        out_shape=jax.ShapeDtypeStruct((N_DEV, k, n), BF16),
        scratch_shapes=[pltpu.VMEM((k, n), F32)],
        compiler_params=_cp(2),
    )(a, b3, *after)


def _ffn_up_act_fwd(h2, wg_g, wu_g):
    s, d = h2.shape
    f8 = wg_g.shape[2]
    th = min(1024, s)

    def body(h_ref, wg_ref, wu_ref, g_ref, u_ref, a_ref):
        i = pl.program_id(1)
        for rs in _chunks(th, 512):
            rows = pl.ds(pl.multiple_of(i * th + rs.start, rs.stop - rs.start), rs.stop - rs.start)
            a = h_ref[rows, :]
            g = _dot(a, wg_ref[...], NN)
            u = _dot(a, wu_ref[...], NN)
            g_ref[rs, :] = g.astype(BF16)
            u_ref[rs, :] = u.astype(BF16)
            a_ref[rs, :] = (g * _sigmoid(g) * u).astype(BF16)

    wspec = pl.BlockSpec((None, d, f8), lambda j, i: (j, 0, 0))
    ospec = pl.BlockSpec((None, th, f8), lambda j, i: (j, i, 0))
    out = jax.ShapeDtypeStruct((N_DEV, s, f8), BF16)
    return pl.pallas_call(
        body, name="ffn_up_fwd", grid=(N_DEV, s // th),
        in_specs=[pl.BlockSpec((s, d), lambda j, i: (0, 0)), wspec, wspec],
        out_specs=[ospec, ospec, ospec], out_shape=[out, out, out],
        compiler_params=_cp(2),
    )(h2, wg_g, wu_g)


def _ffn_down_fwd(act, wd_g):
    _, s, f8 = act.shape
    d = wd_g.shape[2]
    tn = min(1024, d)

    def body(a_ref, wd_ref, o_ref):
        j = pl.program_id(1)

        @pl.when(j == 0)
        def _():
            o_ref[...] = jnp.zeros_like(o_ref)

        for rs in _chunks(s, 1024):
            o_ref[rs, :] += _dot(a_ref[rs, :], wd_ref[...], NN)

    return pl.pallas_call(
        body, name="ffn_down_fwd", grid=(d // tn, N_DEV),
        in_specs=[pl.BlockSpec((None, s, f8), lambda n, j: (j, 0, 0)),
                  pl.BlockSpec((None, f8, tn), lambda n, j: (j, 0, n))],
        out_specs=pl.BlockSpec((s, tn), lambda n, j: (0, n)),
        out_shape=jax.ShapeDtypeStruct((s, d), F32),
        compiler_params=_cp(2),
    )(act, wd_g)


def _loss_bwd(ffn_out, x1, target, final_g):
    s, d = x1.shape
    tm = min(256, s)

    def body(f_ref, x1_ref, t_ref, gf_ref, dx_ref, dxb_ref, dgf_ref, loss_ref):
        @pl.when(pl.program_id(0) == 0)
        def _():
            dgf_ref[...] = jnp.zeros_like(dgf_ref)
            loss_ref[...] = jnp.zeros_like(loss_ref)

        x2 = x1_ref[...] + f_ref[...]
        r = lax.rsqrt(jnp.mean(x2 * x2, axis=-1, keepdims=True) + EPS)
        nrm = x2 * r
        gf = gf_ref[...]
        err = nrm * gf - t_ref[...]
        loss_ref[...] += jnp.sum(err * err) * (0.5 / d)
        dy = err * (1.0 / d)
        dgf_ref[...] += jnp.sum(dy * nrm, axis=0, keepdims=True)
        dn = dy * gf
        dx = r * (dn - nrm * jnp.mean(dn * nrm, axis=-1, keepdims=True))
        dx_ref[...] = dx
        dxb_ref[...] = dx.astype(BF16)

    row = pl.BlockSpec((tm, d), lambda i: (i, 0))
    vec = pl.BlockSpec((1, d), lambda i: (0, 0))
    return pl.pallas_call(
        body, name="loss_bwd", grid=(s // tm,),
        in_specs=[row, row, row, vec],
        out_specs=[row, row, vec, pl.BlockSpec((8, LANES), lambda i: (0, 0))],
        out_shape=[jax.ShapeDtypeStruct((s, d), F32), jax.ShapeDtypeStruct((s, d), BF16),
                   jax.ShapeDtypeStruct((1, d), F32), jax.ShapeDtypeStruct((8, LANES), F32)],
        compiler_params=_cp(1),
    )(ffn_out, x1, target, final_g)


def _ffn_gate_bwd(dx2b, wd_g, gact, uact):
    s, d = dx2b.shape
    f8 = gact.shape[2]
    th = min(1024, s)

    def body(dx_ref, wd_ref, g_ref, u_ref, dg_ref, du_ref):
        i = pl.program_id(1)
        for rs in _chunks(th, 256):
            rows = pl.ds(pl.multiple_of(i * th + rs.start, rs.stop - rs.start), rs.stop - rs.start)
            da = _dot(dx_ref[rows, :], wd_ref[...], NT)
            g = g_ref[rs, :].astype(F32)
            u = u_ref[rs, :].astype(F32)
            sg = _sigmoid(g)
            du_ref[rs, :] = (da * (g * sg)).astype(BF16)
            dg_ref[rs, :] = (da * u * (sg * (1.0 + g * (1.0 - sg)))).astype(BF16)

    aspec = pl.BlockSpec((None, th, f8), lambda j, i: (j, i, 0))
    out = jax.ShapeDtypeStruct((N_DEV, s, f8), BF16)
    return pl.pallas_call(
        body, name="ffn_act_bwd", grid=(N_DEV, s // th),
        in_specs=[pl.BlockSpec((s, d), lambda j, i: (0, 0)),
                  pl.BlockSpec((None, f8, d), lambda j, i: (j, 0, 0)), aspec, aspec],
        out_specs=[aspec, aspec], out_shape=[out, out],
        compiler_params=_cp(2),
    )(dx2b, wd_g, gact, uact)


def _wgrad_rows(a3, b, name, after=()):
    _, s, k = a3.shape
    n = b.shape[1]

    def body(a_ref, b_ref, *rest):
        o_ref = rest[len(after)]
        o_ref[...] = _dot(a_ref[...], b_ref[...], TN).astype(BF16)

    return pl.pallas_call(
        body, name=name, grid=(N_DEV,),
        in_specs=[pl.BlockSpec((None, s, k), lambda j: (j, 0, 0)),
                  pl.BlockSpec((s, n), lambda j: (0, 0))] + _after_specs(after),
        out_specs=pl.BlockSpec((None, k, n), lambda j: (j, 0, 0)),
        out_shape=jax.ShapeDtypeStruct((N_DEV, k, n), BF16),
        compiler_params=_cp(1),
    )(a3, b, *after)


def _wgrad_cols(a, b3, name, after=()):
    s, k = a.shape
    if b3.ndim == 2:
        n = b3.shape[1] // N_DEV
        b_spec = pl.BlockSpec((s, n), lambda j: (0, j))
    else:
        n = b3.shape[2]
        b_spec = pl.BlockSpec((None, s, n), lambda j: (j, 0, 0))

    def body(a_ref, b_ref, *rest):
        o_ref = rest[len(after)]
        o_ref[...] = _dot(a_ref[...], b_ref[...], TN).astype(BF16)

    return pl.pallas_call(
        body, name=name, grid=(N_DEV,),
        in_specs=[pl.BlockSpec((s, k), lambda j: (0, 0)), b_spec] + _after_specs(after),
        out_specs=pl.BlockSpec((None, k, n), lambda j: (j, 0, 0)),
        out_shape=jax.ShapeDtypeStruct((N_DEV, k, n), BF16),
        compiler_params=_cp(1),
    )(a, b3, *after)


def _input_grad(pairs, name, after=()):
    s = pairs[0][0].shape[1]
    d = pairs[0][1].shape[1]
    tn = min(1024, d)
    npair = len(pairs)

    def body(*refs):
        ops = refs[:2 * npair]
        o_ref = refs[2 * npair + len(after)]
        j = pl.program_id(1)

        @pl.when(j == 0)
        def _():
            o_ref[...] = jnp.zeros_like(o_ref)

        for rs in _chunks(s, 1024):
            part = _dot(ops[0][rs, :], ops[1][...], NT)
            for q in range(1, npair):
                part = part + _dot(ops[2 * q][rs, :], ops[2 * q + 1][...], NT)
            o_ref[rs, :] += part

    in_specs, args = [], []
    for a3, w3 in pairs:
        k = a3.shape[2]
        in_specs += [pl.BlockSpec((None, s, k), lambda n, j: (j, 0, 0)),
                     pl.BlockSpec((None, tn, k), lambda n, j: (j, n, 0))]
        args += [a3, w3]
    return pl.pallas_call(
        body, name=name, grid=(d // tn, N_DEV),
        in_specs=in_specs + _after_specs(after),
        out_specs=pl.BlockSpec((s, tn), lambda n, j: (0, n)),
        out_shape=jax.ShapeDtypeStruct((s, d), F32),
        compiler_params=_cp(2),
    )(*args, *after)


def _rms_bwd(dh, xres, g, dres, name):
    s, d = xres.shape
    tm = min(256, s)

    def body(dh_ref, x_ref, g_ref, dres_ref, dx_ref, dxb_ref, dg_ref):
        @pl.when(pl.program_id(0) == 0)
        def _():
            dg_ref[...] = jnp.zeros_like(dg_ref)

        xv = x_ref[...]
        dh_v = dh_ref[...]
        r = lax.rsqrt(jnp.mean(xv * xv, axis=-1, keepdims=True) + EPS)
        nrm = xv * r
        dg_ref[...] += jnp.sum(dh_v * nrm, axis=0, keepdims=True)
        dn = dh_v * g_ref[...]
        dx = dres_ref[...] + r * (dn - nrm * jnp.mean(dn * nrm, axis=-1, keepdims=True))
        dx_ref[...] = dx
        dxb_ref[...] = dx.astype(BF16)

    row = pl.BlockSpec((tm, d), lambda i: (i, 0))
    vec = pl.BlockSpec((1, d), lambda i: (0, 0))
    return pl.pallas_call(
        body, name=name, grid=(s // tm,),
        in_specs=[row, row, vec, row],
        out_specs=[row, row, vec],
        out_shape=[jax.ShapeDtypeStruct((s, d), F32), jax.ShapeDtypeStruct((s, d), BF16),
                   jax.ShapeDtypeStruct((1, d), F32)],
        compiler_params=_cp(1),
    )(dh, xres, g, dres)


def _wgrad_full(a, b, name):
    s, k = a.shape
    n = b.shape[1]
    tk = min(512, k)
    ts = min(512, s)
    ns = s // ts

    def body(a_ref, b_ref, o_ref, acc_ref):
        i = pl.program_id(1)

        @pl.when(i == 0)
        def _():
            acc_ref[...] = jnp.zeros_like(acc_ref)

        acc_ref[...] += _dot(a_ref[...], b_ref[...], TN)

        @pl.when(i == ns - 1)
        def _():
            o_ref[...] = acc_ref[...].astype(BF16)

    return pl.pallas_call(
        body, name=name, grid=(k // tk, ns),
        in_specs=[pl.BlockSpec((ts, tk), lambda j, i: (i, j)),
                  pl.BlockSpec((ts, n), lambda j, i: (i, 0))],
        out_specs=pl.BlockSpec((tk, n), lambda j, i: (j, 0)),
        out_shape=jax.ShapeDtypeStruct((k, n), BF16),
        scratch_shapes=[pltpu.VMEM((tk, n), F32)],
        compiler_params=_cp(2),
    )(a, b)


def _wgrad_pool(p, dyb, n_groups):
    s, sw = p.shape
    d = dyb.shape[1]
    gw, go = sw // n_groups, d // n_groups
    ts = min(512, s)
    ns = s // ts

    def body(a_ref, b_ref, o_ref, acc_ref):
        i = pl.program_id(1)

        @pl.when(i == 0)
        def _():
            acc_ref[...] = jnp.zeros_like(acc_ref)

        acc_ref[...] += _dot(a_ref[...], b_ref[...], TN)

        @pl.when(i == ns - 1)
        def _():
            o_ref[...] = acc_ref[...].astype(BF16)

    return pl.pallas_call(
        body, name="wgrad_pool", grid=(n_groups, ns),
        in_specs=[pl.BlockSpec((ts, gw), lambda g, i: (i, g)),
                  pl.BlockSpec((ts, go), lambda g, i: (i, g))],
        out_specs=pl.BlockSpec((None, gw, go), lambda g, i: (g, 0, 0)),
        out_shape=jax.ShapeDtypeStruct((n_groups, gw, go), BF16),
        scratch_shapes=[pltpu.VMEM((gw, go), F32)],
        compiler_params=_cp(2),
    )(p, dyb)


def _input_grad_rms(pairs, xres, g, dres, name, after=()):
    s, d = xres.shape
    tm = min(256, s)
    last = N_DEV - 1
    npair = len(pairs)

    def body(*refs):
        ops = refs[:2 * npair]
        x_ref, g_ref, dres_ref = refs[2 * npair:2 * npair + 3]
        dx_ref, dxb_ref, dg_ref, acc_ref = refs[2 * npair + 3 + len(after):]
        i, j = pl.program_id(0), pl.program_id(1)

        @pl.when(j == 0)
        def _():
            acc_ref[...] = jnp.zeros_like(acc_ref)

        @pl.when((i == 0) & (j == 0))
        def _():
            dg_ref[...] = jnp.zeros_like(dg_ref)

        for rs in _chunks(tm, 256):
            part = _dot(ops[0][rs, :], ops[1][...], NT)
            for q in range(1, npair):
                part = part + _dot(ops[2 * q][rs, :], ops[2 * q + 1][...], NT)
            acc_ref[rs, :] += part

        @pl.when(j == last)
        def _():
            for rs in _chunks(tm, 256):
                xv = x_ref[rs, :]
                dh = acc_ref[rs, :]
                r = lax.rsqrt(jnp.mean(xv * xv, axis=-1, keepdims=True) + EPS)
                nrm = xv * r
                dg_ref[...] += jnp.sum(dh * nrm, axis=0, keepdims=True)
                dn = dh * g_ref[...]
                dx = dres_ref[rs, :] + r * (dn - nrm * jnp.mean(dn * nrm, axis=-1, keepdims=True))
                dx_ref[rs, :] = dx
                dxb_ref[rs, :] = dx.astype(BF16)

    in_specs, args = [], []
    for a3, w3 in pairs:
        k = a3.shape[2]
        in_specs += [pl.BlockSpec((None, tm, k), lambda i, j: (j, i, 0)),
                     pl.BlockSpec((None, d, k), lambda i, j: (j, 0, 0))]
        args += [a3, w3]
    row = pl.BlockSpec((tm, d), lambda i, j: (i, 0))
    vec = pl.BlockSpec((1, d), lambda i, j: (0, 0))
    return pl.pallas_call(
        body, name=name, grid=(s // tm, N_DEV),
        in_specs=in_specs + [row, vec, row] + _after_specs(after),
        out_specs=[row, row, vec],
        out_shape=[jax.ShapeDtypeStruct((s, d), F32), jax.ShapeDtypeStruct((s, d), BF16),
                   jax.ShapeDtypeStruct((1, d), F32)],
        scratch_shapes=[pltpu.VMEM((tm, d), F32)],
        compiler_params=_cp(2),
    )(*args, xres, g, dres, *after)


def _wo_bwd(dx1b, wo, ya, yb, proj, b_gate2, pool_scale, after=()):
    s, d = dx1b.shape
    sw = proj.shape[2]
    tn = d // N_DEV
    nq = sw // tn

    def body(dx_ref, wo_ref, ya_ref, yb_ref, ga_ref, gb_ref, bg_ref, sc_ref, *rest):
        dya_ref, dyb_ref, dp_ref, dbg_ref, dsc_ref = rest[len(after):]
        dbg_ref[...] = jnp.zeros_like(dbg_ref)
        dsc_ref[...] = jnp.zeros_like(dsc_ref)
        for rs in _chunks(s, 256):
            dm = _dot(dx_ref[rs, :], wo_ref[...], NT)
            ya_v = ya_ref[rs, :].astype(F32)
            yb_v = yb_ref[rs, :].astype(F32)
            sa = _sigmoid(ga_ref[rs, :] + bg_ref[0:1, :])
            sb = _sigmoid(gb_ref[rs, :] + bg_ref[1:2, :])
            sc = sc_ref[...]
            dya_ref[rs, :] = (dm * sa).astype(BF16)
            dsb = dm * sb
            dyb_ref[rs, :] = (dsb * sc).astype(BF16)
            dsc_ref[...] += jnp.sum(dsb * yb_v, axis=0, keepdims=True)
            dga = dm * ya_v * (sa * (1.0 - sa))
            dgb = dm * (yb_v * sc) * (sb * (1.0 - sb))
            dp_ref[0, rs, :] = dga.astype(BF16)
            dp_ref[1, rs, :] = dgb.astype(BF16)
            dbg_ref[0:1, :] += jnp.sum(dga, axis=0, keepdims=True)
            dbg_ref[1:2, :] += jnp.sum(dgb, axis=0, keepdims=True)

    col = pl.BlockSpec((s, tn), lambda j: (0, j))
    out = jax.ShapeDtypeStruct((s, d), BF16)
    return pl.pallas_call(
        body, name="wo_bwd", grid=(N_DEV,),
        in_specs=[pl.BlockSpec((s, d), lambda j: (0, 0)),
                  pl.BlockSpec((tn, d), lambda j: (j, 0)), col, col,
                  pl.BlockSpec((None, s, tn), lambda j: (4 + j // nq, 0, j % nq)),
                  pl.BlockSpec((None, s, tn), lambda j: (6 + j // nq, 0, j % nq)),
                  pl.BlockSpec((2, tn), lambda j: (0, j)),
                  pl.BlockSpec((1, tn), lambda j: (0, j))] + _after_specs(after),
        out_specs=[col, col,
                   pl.BlockSpec((2, None, s, tn), lambda j: (1, j // nq, 0, j % nq)),
                   pl.BlockSpec((2, tn), lambda j: (0, j)),
                   pl.BlockSpec((1, tn), lambda j: (0, j))],
        out_shape=[out, out, jax.ShapeDtypeStruct((4, 2, s, sw), BF16),
                   jax.ShapeDtypeStruct((2, d), F32), jax.ShapeDtypeStruct((1, d), F32)],
        compiler_params=_cp(1),
    )(dx1b, wo, ya, yb, proj, proj, b_gate2, pool_scale, *after)


def _conv_bwd(dproj, dya, wa, proj, conv_w, conv_b):
    s, d = dya.shape
    sw, tn = wa.shape[1], wa.shape[2]
    tc = min(LANES, sw)

    def body(dproj_hbm, dya_ref, wa_ref, ba_ref, ca_ref, va_ref, cw_ref, cb_ref,
             dp_ref, dcw_ref, dcb_ref, dz_ref):
        del dproj_hbm
        for rs in _chunks(s, 512):
            part = _dot(dya_ref[rs, 0:tn], wa_ref[0], NT)
            for j in range(1, N_DEV):
                part = part + _dot(dya_ref[rs, j * tn:(j + 1) * tn], wa_ref[j], NT)
            dz_ref[rs, :] = part
        dz = dz_ref[...]
        ba, ca, va = ba_ref[...], ca_ref[...], va_ref[...]
        cv = ca * va
        cv1, cv2 = _shift_down(cv, 1), _shift_down(cv, 2)
        w0, w1, w2 = cw_ref[0:1, :], cw_ref[1:2, :], cw_ref[2:3, :]
        u = cb_ref[...] + w0 * cv2 + w1 * cv1 + w2 * cv
        du = dz * ba
        dp_ref[0] = (dz * u).astype(BF16)
        dcv = w2 * du + w1 * _shift_up(du, 1) + w0 * _shift_up(du, 2)
        dp_ref[1] = (dcv * va).astype(BF16)
        dp_ref[2] = (dcv * ca).astype(BF16)
        dcw_ref[0:1, :] = jnp.sum(du * cv2, axis=0, keepdims=True)
        dcw_ref[1:2, :] = jnp.sum(du * cv1, axis=0, keepdims=True)
        dcw_ref[2:3, :] = jnp.sum(du * cv, axis=0, keepdims=True)
        dcb_ref[...] = jnp.sum(du, axis=0, keepdims=True)

    def part(k):
        return pl.BlockSpec((None, s, tc), lambda i: (k, 0, i))

    return pl.pallas_call(
        body, name="conv_bwd", grid=(sw // tc,),
        in_specs=[pl.BlockSpec(memory_space=pl.ANY),
                  pl.BlockSpec((s, d), lambda i: (0, 0)),
                  pl.BlockSpec((N_DEV, tc, tn), lambda i: (0, i, 0)),
                  part(0), part(1), part(2),
                  pl.BlockSpec((CONV_K, tc), lambda i: (0, i)), pl.BlockSpec((1, tc), lambda i: (0, i))],
        out_specs=[pl.BlockSpec((3, s, tc), lambda i: (0, 0, i)),
                   pl.BlockSpec((CONV_K, tc), lambda i: (0, i)), pl.BlockSpec((1, tc), lambda i: (0, i))],
        out_shape=[jax.ShapeDtypeStruct(dproj.shape, BF16),
                   jax.ShapeDtypeStruct((CONV_K, sw), F32), jax.ShapeDtypeStruct((1, sw), F32)],
        scratch_shapes=[pltpu.VMEM((s, tc), F32)],
        input_output_aliases={0: 0},
        compiler_params=_cp(1),
    )(dproj, dya, wa, proj, proj, proj, conv_w, conv_b)


def _pool_bwd(dproj, dyb, wpool):
    s, d = dyb.shape
    n_groups, gw, go = wpool.shape

    def body(dproj_hbm, dyb_ref, wp_ref, dp_ref):
        del dproj_hbm
        for gi, window in enumerate(POOL_WINDOWS):
            @pl.when(pl.program_id(0) == gi)
            def _():
                dpool = _dot(dyb_ref[...], wp_ref[...], NT)
                acc, k = dpool / _pool_counts(dpool.shape, window), 1
                while k < window:
                    acc = acc + _shift_up(acc, k)
                    k *= 2
                dp_ref[...] = (acc - dpool).astype(BF16)

    return pl.pallas_call(
        body, name="pool_bwd", grid=(n_groups,),
        in_specs=[pl.BlockSpec(memory_space=pl.ANY),
                  pl.BlockSpec((s, go), lambda g: (0, g)),
                  pl.BlockSpec((None, gw, go), lambda g: (g, 0, 0))],
        out_specs=pl.BlockSpec((None, s, gw), lambda g: (3, 0, g)),
        out_shape=jax.ShapeDtypeStruct(dproj.shape, BF16),
        input_output_aliases={0: 0},
        compiler_params=_cp(1),
    )(dproj, dyb, wpool)


def _rows128(v):
    return v.reshape(-1, LANES)


def kernel(x, norm1_g, w_in, b_gate, conv_w, conv_b, w_a_out, w_pool, pool_scale, w_o, norm2_g, w_ffn_gate, w_ffn_up, w_ffn_down, final_g, loss_target, m_norm1_g, m_w_in, m_b_gate, m_conv_w, m_conv_b, m_w_a_out, m_w_pool, m_pool_scale, m_w_o, m_norm2_g, m_w_ffn_gate, m_w_ffn_up, m_w_ffn_down, m_final_g, v_norm1_g, v_w_in, v_b_gate, v_conv_w, v_conv_b, v_w_a_out, v_w_pool, v_pool_scale, v_w_o, v_norm2_g, v_w_ffn_gate, v_w_ffn_up, v_w_ffn_down, v_final_g):
    s, d = x.shape[1], x.shape[2]
    sw = w_in.shape[2]
    n_groups = w_pool.shape[1]
    gw = w_pool.shape[2]
    go = w_pool.shape[3] * N_DEV
    f8 = w_ffn_gate.shape[2]
    cws = conv_w.shape[2]
    assert sw == conv_w.shape[2] * N_DEV == gw * n_groups and go * n_groups == d and n_groups == len(POOL_WINDOWS)

    xi, yi, ci = _coords()
    me = 4 * xi + 2 * yi + ci
    my_chip = 2 * xi + yi

    x2d = x.reshape(s, d)
    target = loss_target.reshape(s, d)
    final_g2 = final_g.reshape(1, d)
    b_gate2 = b_gate.reshape(2, d)

    big_names = ["w_in", "w_a_out", "w_pool", "w_o", "w_ffn_gate", "w_ffn_up", "w_ffn_down"]
    big_w = [w_in, w_a_out, w_pool, w_o, w_ffn_gate, w_ffn_up, w_ffn_down]
    big_m = [m_w_in, m_w_a_out, m_w_pool, m_w_o, m_w_ffn_gate, m_w_ffn_up, m_w_ffn_down]
    big_v = [v_w_in, v_w_a_out, v_w_pool, v_w_o, v_w_ffn_gate, v_w_ffn_up, v_w_ffn_down]
    shapes2d = [(w.size // w.shape[-1], w.shape[-1]) for w in big_w]
    big_w2 = [w.reshape(sh) for w, sh in zip(big_w, shapes2d)]
    transposed = (4, 5)

    def view2d(t, a):
        t2 = t.reshape(shapes2d[a])
        return t2.T if a in transposed else t2

    def unview(o, a):
        return (o.T if a in transposed else o).reshape(big_w[a].shape)

    sb = [_cast_bf16(w, "cast_" + nm) for w, nm in zip(big_w2, big_names)]
    win_g, wa_g, wpool_g, wo_g = _allgather_big(sb[0:4], "allgather_mixer", COLLECTIVE_GATHER)
    wg_g, wu_g = _allgather_big(sb[4:6], "allgather_ffn_up", COLLECTIVE_GATHER)
    (wd_g,) = _allgather_big(sb[6:7], "allgather_ffn_down", COLLECTIVE_GATHER)
    convw_g = _allgather_small(jnp.pad(conv_w.reshape(CONV_K, cws), ((0, 8 - CONV_K), (0, 0))), "allgather_conv_w")
    conv_w_full = convw_g[:, :CONV_K, :].transpose(1, 0, 2).reshape(CONV_K, sw)
    wpool = wpool_g.reshape(N_DEV, n_groups, gw, go // N_DEV).transpose(1, 2, 0, 3).reshape(n_groups, gw, go)
    wo = wo_g.reshape(d, d)

    h = _rms_fwd(x2d, norm1_g)
    proj = _proj_fwd(h, win_g)
    z = _conv_fwd(proj, conv_w_full, conv_b)
    p = _pool_fwd(proj)
    ya, yb, merged = _merge_fwd(z, wa_g, p, wpool, proj, b_gate2, pool_scale)
    x1, h2 = _wo_fwd(merged, wo, x2d, norm2_g)
    gact, uact, act = _ffn_up_act_fwd(h2, wg_g, wu_g)
    ffn_out = _ffn_down_fwd(act, wd_g)
    dx2, dx2b, d_final_g, loss_blk = _loss_bwd(ffn_out, x1, target, final_g2)

    chip_slots = (2 * jnp.arange(4, dtype=jnp.int32) + ci).astype(jnp.int32)

    def partials(grads, recvs, names):
        return [_chip_partial(chip_slots, g3, r, "chip_partial_" + nm) for g3, r, nm in zip(grads, recvs, names)]

    own = jnp.stack([me, my_chip]).astype(jnp.int32)

    def adam(a, g3, sib, chips):
        outs = _adam_big(own, view2d(big_w[a], a), view2d(big_m[a], a), view2d(big_v[a], a),
                         g3, sib, chips, "adam_" + big_names[a])
        return [unview(o, a) for o in outs]

    big_out = [None] * len(big_names)
    dg_act, du_act = _ffn_gate_bwd(dx2b, wd_g, gact, uact)
    gw_gate = _wgrad_rows(dg_act, h2, "wgrad_ffn_gate")
    gw_up = _wgrad_rows(du_act, h2, "wgrad_ffn_up")
    sib_gu, ps_gu = _sibling_partials([gw_gate, gw_up], "rs_sibling_ffn_up", COLLECTIVE_SIBLING)
    chips_gu = _exchange_chips(ps_gu, "rs_chips_ffn_up", COLLECTIVE_CHIPS)
    gw_down = _wgrad_rows(act, dx2b, "wgrad_ffn_down", after=ps_gu)
    sib_down, ps_down = _sibling_partials([gw_down], "rs_sibling_ffn_down", COLLECTIVE_SIBLING)
    chips_down = _exchange_chips(ps_down, "rs_chips_ffn_down", COLLECTIVE_CHIPS)
    dh2 = _input_grad([(dg_act, wg_g), (du_act, wu_g)], "ffn_in_bwd", after=ps_down)
    dx1, dx1b, d_norm2_g = _rms_bwd(dh2, x1, norm2_g, dx2, "rms2_bwd")
    dya, dyb, dproj42, d_b_gate, d_pool_scale = _wo_bwd(dx1b, wo, ya, yb, proj, b_gate2, pool_scale)
    gw_o = _wgrad_full(merged, dx1b, "wgrad_o")
    dproj = dproj42.reshape(N_DEV, s, sw)
    dproj, d_conv_w, d_conv_b = _conv_bwd(dproj, dya, wa_g, proj, conv_w_full, conv_b)
    dproj = _pool_bwd(dproj, dyb, wpool)
    gw_a = _wgrad_cols(z, dya, "wgrad_a_out")
    gw_pool = _wgrad_pool(p, dyb, n_groups)
    mix3 = [gw_a,
            gw_pool.reshape(n_groups, gw, N_DEV, go // N_DEV).transpose(2, 0, 1, 3).reshape(N_DEV, n_groups * gw, go // N_DEV),
            gw_o.reshape(N_DEV, d // N_DEV, d)]
    sib_mix, ps_mix = _sibling_partials(mix3, "rs_sibling_mixer", COLLECTIVE_SIBLING)
    chips_mix = _exchange_chips(ps_mix, "rs_chips_mixer", COLLECTIVE_CHIPS)
    big_out[4] = adam(4, gw_gate, sib_gu[0], chips_gu[0])
    big_out[5] = adam(5, gw_up, sib_gu[1], chips_gu[1])
    gw_in = _wgrad_cols(h, dproj, "wgrad_in", after=ps_mix + [big_out[4][0], big_out[5][0]])
    sib_in, ps_in = _sibling_partials([gw_in], "rs_sibling_w_in", COLLECTIVE_SIBLING)
    big_out[6] = adam(6, gw_down, sib_down[0], chips_down[0])
    chips_in = _exchange_chips(ps_in, "rs_chips_w_in", COLLECTIVE_CHIPS)
    dh = _input_grad([(dproj, win_g)], "proj_in_bwd", after=ps_mix + ps_in + [big_out[6][0]])
    grad_x, _, d_norm1_g = _rms_bwd(dh, x2d, norm1_g, dx1, "rms1_bwd")
    for k in range(3):
        big_out[1 + k] = adam(1 + k, mix3[k], sib_mix[k], chips_mix[k])
    big_out[0] = adam(0, gw_in, sib_in[0], chips_in[0])

    small_parts = [d_norm1_g, d_b_gate, d_conv_w, d_conv_b, d_pool_scale, d_norm2_g, d_final_g, loss_blk]
    sizes = [v.size for v in small_parts]
    packed = jnp.concatenate([_rows128(v) for v in small_parts], axis=0)
    summed = _sum_small(_allgather_small(packed, "allgather_small_grads")).reshape(-1)
    offs = [0]
    for n in sizes:
        offs.append(offs[-1] + n)
    g_norm1, g_bgate, g_convw_full, g_convb, g_pscale, g_norm2, g_final, loss_sum = [
        summed[offs[k]:offs[k + 1]] for k in range(len(sizes))]
    loss = loss_sum[0]
    g_convw = lax.dynamic_slice(g_convw_full.reshape(CONV_K, sw), (0, me * cws), (CONV_K, cws))
    small_w = [norm1_g, b_gate, conv_w, conv_b, pool_scale, norm2_g, final_g]
    small_m = [m_norm1_g, m_b_gate, m_conv_w, m_conv_b, m_pool_scale, m_norm2_g, m_final_g]
    small_v = [v_norm1_g, v_b_gate, v_conv_w, v_conv_b, v_pool_scale, v_norm2_g, v_final_g]
    small_g = [g_norm1, g_bgate, g_convw, g_convb, g_pscale, g_norm2, g_final]

    def pack(parts):
        flat = jnp.concatenate([v.reshape(-1) for v in parts])
        pad = (-flat.size) % (8 * LANES)
        return jnp.pad(flat, (0, pad)).reshape(-1, LANES)

    s_delta, s_m, s_v = _adam_small(pack(small_w), pack(small_g), pack(small_m), pack(small_v))
    soffs = [0]
    for w in small_w:
        soffs.append(soffs[-1] + w.size)

    def unpack(buf):
        flat = buf.reshape(-1)
        return [flat[soffs[k]:soffs[k + 1]].reshape(small_w[k].shape) for k in range(len(small_w))]

    small_grads = [g.reshape(w.shape) for g, w in zip(small_g, small_w)]
    small_delta, small_new_m, small_new_v = unpack(s_delta), unpack(s_m), unpack(s_v)

    order = ["norm1_g", "w_in", "b_gate", "conv_w", "conv_b", "w_a_out", "w_pool", "pool_scale", "w_o", "norm2_g",
             "w_ffn_gate", "w_ffn_up", "w_ffn_down", "final_g"]
    small_names = ["norm1_g", "b_gate", "conv_w", "conv_b", "pool_scale", "norm2_g", "final_g"]
    per_kind = [{}, {}, {}, {}]
    for a, nm in enumerate(big_names):
        for kind in range(4):
            per_kind[kind][nm] = big_out[a][kind]
    for k, nm in enumerate(small_names):
        per_kind[0][nm] = small_grads[k]
        per_kind[1][nm] = small_delta[k]
        per_kind[2][nm] = small_new_m[k]
        per_kind[3][nm] = small_new_v[k]
    result = [loss, grad_x.reshape(x.shape)]
    for kind in range(4):
        result += [per_kind[kind][nm] for nm in order]
    return tuple(result)
```

```python
import jax
import jax.numpy as jnp
from jax import lax
from jax.experimental import pallas as pl
from jax.experimental.pallas import tpu as pltpu
from jax.experimental.pallas import tpu_sc as plsc

F32 = jnp.float32
BF16 = jnp.bfloat16
MESH = pl.DeviceIdType.MESH

N_DEV = 8
EPS = 1e-6
CONV_K = 3
POOL_WINDOWS = (2, 4, 8, 16)
ADAM_LR = 0.001
ADAM_B1 = 0.9
ADAM_B2 = 0.999
ADAM_EPS = 1e-08
ADAM_WD = 0.01
ADAM_STEP = 10

V7X_VMEM_LIMIT_BYTES = 56 * 1024 * 1024
LANES = 128

COLLECTIVE_GATHER = 1
COLLECTIVE_SIBLING = 2
COLLECTIVE_CHIPS = 3
SEQUENCER_COST_BYTES = 4 * 10**9

NN = ((1,), (0,))
NT = ((1,), (1,))
TN = ((0,), (0,))


def _dot(a, b, dims):
    return lax.dot_general(a, b, (dims, ((), ())), preferred_element_type=F32)


def _cp(n_axes):
    return pltpu.CompilerParams(dimension_semantics=("arbitrary",) * n_axes,
                                vmem_limit_bytes=V7X_VMEM_LIMIT_BYTES)


def _row_tile(rows, bytes_per_row, cap_bytes):
    best = None
    for t in range(16, rows + 1, 16):
        if rows % t == 0 and t * bytes_per_row <= cap_bytes:
            best = t
    return best if best is not None else rows


def _chunks(total, size):
    size = min(size, total)
    assert total % size == 0
    return [slice(r, r + size) for r in range(0, total, size)]


def _after_specs(after):
    return [pl.BlockSpec(memory_space=pl.ANY)] * len(after)


def _shift_down(v, k):
    row = lax.broadcasted_iota(jnp.int32, v.shape, 0)
    return jnp.where(row >= k, pltpu.roll(v, k, 0), 0.0)


def _shift_up(v, k):
    n = v.shape[0]
    row = lax.broadcasted_iota(jnp.int32, v.shape, 0)
    return jnp.where(row < n - k, pltpu.roll(v, n - k, 0), 0.0)


def _sigmoid(v):
    return jax.nn.sigmoid(v)


def _cast_bf16(w2d, name):
    rows, cols = w2d.shape
    tr = _row_tile(rows, cols * 4, 2 << 20)

    def body(i_ref, o_ref):
        o_ref[...] = i_ref[...].astype(BF16)

    return pl.pallas_call(
        body, name=name, grid=(rows // tr,),
        in_specs=[pl.BlockSpec((tr, cols), lambda i: (i, 0))],
        out_specs=pl.BlockSpec((tr, cols), lambda i: (i, 0)),
        out_shape=jax.ShapeDtypeStruct((rows, cols), BF16),
        compiler_params=_cp(1),
    )(w2d)


def _rms_fwd(x2d, g):
    s, d = x2d.shape
    tm = min(256, s)

    def body(x_ref, g_ref, h_ref):
        xv = x_ref[...]
        r = lax.rsqrt(jnp.mean(xv * xv, axis=-1, keepdims=True) + EPS)
        h_ref[...] = (xv * r * g_ref[...]).astype(BF16)

    return pl.pallas_call(
        body, name="rms1_fwd", grid=(s // tm,),
        in_specs=[pl.BlockSpec((tm, d), lambda i: (i, 0)), pl.BlockSpec((1, d), lambda i: (0, 0))],
        out_specs=pl.BlockSpec((tm, d), lambda i: (i, 0)),
        out_shape=jax.ShapeDtypeStruct((s, d), BF16),
        compiler_params=_cp(1),
    )(x2d, g)


def _coords():
    return lax.axis_index("x"), lax.axis_index("y"), lax.axis_index("c")


def _slot(p):
    return 4 * p[0] + 2 * p[1] + p[2]


def _handshake(peers):
    barrier = pltpu.get_barrier_semaphore()
    for peer in peers:
        pl.semaphore_signal(barrier, inc=1, device_id=peer, device_id_type=MESH)
    pl.semaphore_wait(barrier, len(peers))


def _sequencer_call(body, out_type, scratch_types, name, collective_id):
    return pl.kernel(
        body, out_type=out_type, name=name,
        mesh=plsc.ScalarSubcoreMesh(axis_name="seq", num_cores=1),
        scratch_types=scratch_types,
        cost_estimate=pl.CostEstimate(flops=0, transcendentals=0, bytes_accessed=SEQUENCER_COST_BYTES),
        compiler_params=pltpu.CompilerParams(collective_id=collective_id))


def _allgather_big(shards, name, collective_id, after=()):
    n = len(shards)

    def body(*refs):
        ins, outs = refs[:n], refs[n + len(after):2 * n + len(after)]
        send_sems, recv_sems, local_sems = refs[2 * n + len(after):]
        x, y, c = _coords()
        me, sibling = (x, y, c), (x, y, 1 - c)
        x_nbr, y_nbr, diag = (1 - x, y), (x, 1 - y), (1 - x, 1 - y)
        relay_from = (x + (1 - c) * (1 - 2 * x), y + c * (1 - 2 * y))
        relay_to = (x + c * (1 - 2 * x), y + (1 - c) * (1 - 2 * y))
        _handshake([sibling, (*x_nbr, c), (*y_nbr, c)])

        def copy(a, k, block, to, src=None):
            dst = outs[a].at[_slot(block)]
            return pltpu.make_async_remote_copy(
                src_ref=dst if src is None else src, dst_ref=dst,
                send_sem=send_sems.at[a, k], recv_sem=recv_sems.at[a, k],
                device_id=to, device_id_type=MESH)

        mine, sends = [], []
        for a in range(n):
            cp = pltpu.make_async_copy(ins[a], outs[a].at[_slot(me)], local_sems.at[a])
            cp.start()
            mine.append(cp)
            first = [copy(a, 0, me, sibling, src=ins[a]),
                     copy(a, 1, me, (*x_nbr, c), src=ins[a]),
                     copy(a, 2, me, (*y_nbr, c), src=ins[a])]
            for cp in first:
                cp.start()
            sends += first
        for a in range(n):
            copy(a, 1 + c, (*relay_from, c), me).wait_recv()
            passed = [copy(a, 3, (*relay_from, c), (*relay_to, c)), copy(a, 4 + c, (*relay_from, c), sibling)]
            for cp in passed:
                cp.start()
            copy(a, 2 - c, (*relay_to, c), me).wait_recv()
            cp = copy(a, 5 - c, (*relay_to, c), sibling)
            cp.start()
            passed.append(cp)
            copy(a, 3, (*diag, c), me).wait_recv()
            cp = copy(a, 6, (*diag, c), sibling)
            cp.start()
            sends += passed + [cp]
        for a in range(n):
            copy(a, 0, sibling, me).wait_recv()
            copy(a, 4, (*x_nbr, 1 - c), me).wait_recv()
            copy(a, 5, (*y_nbr, 1 - c), me).wait_recv()
            copy(a, 6, (*diag, 1 - c), me).wait_recv()
        for cp in sends:
            cp.wait_send()
        for cp in mine:
            cp.wait()

    return _sequencer_call(
        body, [jax.ShapeDtypeStruct((N_DEV,) + s.shape, s.dtype) for s in shards],
        [pltpu.SemaphoreType.DMA((n, 7)), pltpu.SemaphoreType.DMA((n, 7)), pltpu.SemaphoreType.DMA((n,))],
        name, collective_id)(*shards, *after)


def _sibling_copies(ins, recvs, send_sems, recv_sems):
    x, y, c = _coords()
    return [pltpu.make_async_remote_copy(
        src_ref=ins[a].at[2 * q + (1 - c)], dst_ref=recvs[a].at[q],
        send_sem=send_sems.at[a, q], recv_sem=recv_sems.at[a, q],
        device_id=(x, y, 1 - c), device_id_type=MESH) for a in range(len(ins)) for q in range(4)]


def _carry_specs(carry):
    any_spec = pl.BlockSpec(memory_space=pl.ANY)
    n = len(carry)
    sems = [pltpu.SemaphoreType.DMA((n, 4)), pltpu.SemaphoreType.DMA((n, 4))] if n else []
    return ([any_spec] * n, [any_spec] * n,
            [jax.ShapeDtypeStruct((4,) + g.shape[1:], g.dtype) for g in carry], sems)


def _carry_run(first, last, ins, recvs, sems):
    if not ins:
        return

    @pl.when(first)
    def _():
        x, y, c = _coords()
        _handshake([(x, y, 1 - c)])
        for cp in _sibling_copies(ins, recvs, *sems):
            cp.start()

    @pl.when(last)
    def _():
        copies = _sibling_copies(ins, recvs, *sems)
        for cp in copies:
            cp.wait_recv()
        for cp in copies:
            cp.wait_send()


def _cp_carry(n_axes, carry):
    if not carry:
        return _cp(n_axes)
    return pltpu.CompilerParams(dimension_semantics=("arbitrary",) * n_axes, vmem_limit_bytes=V7X_VMEM_LIMIT_BYTES,
                                collective_id=COLLECTIVE_SIBLING)


def _sibling_partials(grads, name, collective_id):
    n = len(grads)

    def body(*refs):
        ins, recvs, psums = refs[:n], refs[n:2 * n], refs[2 * n:3 * n]
        send_sems, recv_sems = refs[3 * n:]
        x, y, c = _coords()
        my_chip = 2 * x + y
        _handshake([(x, y, 1 - c)])
        copies = _sibling_copies(ins, recvs, send_sems, recv_sems)
        for cp in copies:
            cp.start()

        def add(g_ref, r_ref, o_ref):
            o_ref[...] = (g_ref[...].astype(F32) + r_ref[...].astype(F32)).astype(BF16)

        for a in range(n):
            _, rows, cols = grads[a].shape
            tr = _row_tile(rows, cols * 2, 1 << 20)
            blk = pl.BlockSpec((tr, cols), lambda i: (i, 0))
            for q in range(4):
                copies[4 * a + q].wait_recv()

                @pl.when(q != my_chip)
                def _():
                    pltpu.emit_pipeline(add, grid=(rows // tr,), in_specs=[blk, blk], out_specs=[blk])(
                        ins[a].at[2 * q + c], recvs[a].at[q], psums[a].at[q])
        for cp in copies:
            cp.wait_send()

    any_spec = pl.BlockSpec(memory_space=pl.ANY)
    shapes = [jax.ShapeDtypeStruct((4,) + g.shape[1:], g.dtype) for g in grads]
    outs = pl.pallas_call(
        body, name=name,
        in_specs=[any_spec] * n, out_specs=[any_spec] * (2 * n),
        out_shape=shapes + shapes,
        scratch_shapes=[pltpu.SemaphoreType.DMA((n, 4)), pltpu.SemaphoreType.DMA((n, 4))],
        compiler_params=pltpu.CompilerParams(collective_id=collective_id,
                                             vmem_limit_bytes=V7X_VMEM_LIMIT_BYTES),
    )(*grads)
    return list(outs[:n]), list(outs[n:])


def _exchange_chips(psums, name, collective_id):
    n = len(psums)

    def body(*refs):
        ins, outs = refs[:n], refs[n:2 * n]
        send_sems, recv_sems = refs[2 * n:]
        x, y, c = _coords()
        chips = [(1 - x, y), (x, 1 - y), (1 - x, 1 - y)]
        _handshake([(*chip, c) for chip in chips])
        copies = []
        for a in range(n):
            for j, chip in enumerate(chips):
                cp = pltpu.make_async_remote_copy(
                    src_ref=ins[a].at[2 * chip[0] + chip[1]], dst_ref=outs[a].at[j],
                    send_sem=send_sems.at[a, j], recv_sem=recv_sems.at[a, j],
                    device_id=(*chip, c), device_id_type=MESH)
                cp.start()
                copies.append(cp)
        for cp in copies:
            cp.wait_recv()
        for cp in copies:
            cp.wait_send()

    return _sequencer_call(
        body, [jax.ShapeDtypeStruct((3,) + p.shape[1:], p.dtype) for p in psums],
        [pltpu.SemaphoreType.DMA((n, 3)), pltpu.SemaphoreType.DMA((n, 3))],
        name, collective_id)(*psums)


def _allgather_small(v2d, name):
    rows, cols = v2d.shape

    def body(v_ref, out_ref, send_sems, recv_sems):
        x, y, c = _coords()
        me = (x, y, c)
        out_ref[_slot(me)] = v_ref[...]
        peers = []
        for k in range(1, N_DEV):
            fx, fy, fc = (k >> 2) & 1, (k >> 1) & 1, k & 1
            peers.append(((1 - x) if fx else x, (1 - y) if fy else y, (1 - c) if fc else c))
        sends = []
        for k, peer in enumerate(peers):
            cp = pltpu.make_async_remote_copy(
                src_ref=v_ref, dst_ref=out_ref.at[_slot(me)],
                send_sem=send_sems.at[k], recv_sem=recv_sems.at[k],
                device_id=peer, device_id_type=MESH)
            cp.start()
            sends.append(cp)
        for k, peer in enumerate(peers):
            pltpu.make_async_remote_copy(
                src_ref=v_ref, dst_ref=out_ref.at[_slot(peer)],
                send_sem=send_sems.at[k], recv_sem=recv_sems.at[k],
                device_id=peer, device_id_type=MESH).wait_recv()
        for cp in sends:
            cp.wait_send()

    vmem = pl.BlockSpec(memory_space=pltpu.VMEM)
    return pl.pallas_call(
        body, name=name, in_specs=[vmem], out_specs=vmem,
        out_shape=jax.ShapeDtypeStruct((N_DEV, rows, cols), v2d.dtype),
        scratch_shapes=[pltpu.SemaphoreType.DMA((N_DEV - 1,)), pltpu.SemaphoreType.DMA((N_DEV - 1,))],
    )(v2d)


def _chip_partial(others, g3, recv, name):
    _, rows, cols = g3.shape
    tr = _row_tile(rows, cols * 2, 2 << 20)

    def body(others_ref, g_ref, r_ref, o_ref):
        o_ref[...] = (g_ref[...].astype(F32) + r_ref[...].astype(F32)).astype(BF16)

    return pl.pallas_call(
        body, name=name,
        grid_spec=pltpu.PrefetchScalarGridSpec(
            num_scalar_prefetch=1, grid=(3, rows // tr),
            in_specs=[pl.BlockSpec((None, tr, cols), lambda k, i, o: (o[3 + k], i, 0)),
                      pl.BlockSpec((None, tr, cols), lambda k, i, o: (o[k], i, 0))],
            out_specs=pl.BlockSpec((None, tr, cols), lambda k, i, o: (o[k], i, 0))),
        out_shape=jax.ShapeDtypeStruct((4, rows, cols), BF16),
        compiler_params=_cp(2),
    )(others, g3, recv)


def _adam_math(w, g, m, v):
    m = ADAM_B1 * m + (1.0 - ADAM_B1) * g
    v = ADAM_B2 * v + (1.0 - ADAM_B2) * (g * g)
    m_hat = m / (1.0 - ADAM_B1 ** ADAM_STEP)
    v_hat = v / (1.0 - ADAM_B2 ** ADAM_STEP)
    delta = -ADAM_LR * (m_hat / (jnp.sqrt(v_hat) + ADAM_EPS) + ADAM_WD * w)
    return delta, m, v


def _adam_big(own, w, m, v, g3, recv_sib, recv_chips, name):
    rows, cols = w.shape
    tr = _row_tile(rows, cols * 4, 2 << 20)

    def body(own_ref, w_ref, m_ref, v_ref, g_ref, rs_ref, rc_ref, go_ref, do_ref, mo_ref, vo_ref):
        g = g_ref[...].astype(F32) + rs_ref[...].astype(F32)
        g = g + rc_ref[0].astype(F32)
        g = g + rc_ref[1].astype(F32)
        g = g + rc_ref[2].astype(F32)
        delta, m_new, v_new = _adam_math(w_ref[...], g, m_ref[...], v_ref[...])
        go_ref[...] = g
        do_ref[...] = delta
        mo_ref[...] = m_new
        vo_ref[...] = v_new

    blk = pl.BlockSpec((tr, cols), lambda i, o: (i, 0))
    out = jax.ShapeDtypeStruct((rows, cols), F32)
    return pl.pallas_call(
        body, name=name,
        grid_spec=pltpu.PrefetchScalarGridSpec(
            num_scalar_prefetch=1, grid=(rows // tr,),
            in_specs=[blk, blk, blk,
                      pl.BlockSpec((None, tr, cols), lambda i, o: (o[0], i, 0)),
                      pl.BlockSpec((None, tr, cols), lambda i, o: (o[1], i, 0)),
                      pl.BlockSpec((3, tr, cols), lambda i, o: (0, i, 0))],
            out_specs=[blk, blk, blk, blk]),
        out_shape=[out, out, out, out],
        compiler_params=_cp(1),
    )(own, w, m, v, g3, recv_sib, recv_chips)


def _sum_small(gathered):
    _, rows, cols = gathered.shape

    def body(g_ref, o_ref):
        acc = g_ref[0]
        for k in range(1, N_DEV):
            acc = acc + g_ref[k]
        o_ref[...] = acc

    vmem = pl.BlockSpec(memory_space=pltpu.VMEM)
    return pl.pallas_call(body, name="small_grad_sum", in_specs=[vmem], out_specs=vmem,
                          out_shape=jax.ShapeDtypeStruct((rows, cols), F32))(gathered)


def _adam_small(w, g, m, v):
    def body(w_ref, g_ref, m_ref, v_ref, do_ref, mo_ref, vo_ref):
        delta, m_new, v_new = _adam_math(w_ref[...], g_ref[...], m_ref[...], v_ref[...])
        do_ref[...] = delta
        mo_ref[...] = m_new
        vo_ref[...] = v_new

    vmem = pl.BlockSpec(memory_space=pltpu.VMEM)
    out = jax.ShapeDtypeStruct(w.shape, F32)
    return pl.pallas_call(body, name="adam_small", in_specs=[vmem] * 4, out_specs=[vmem] * 3,
                          out_shape=[out, out, out])(w, g, m, v)


def _proj_fwd(h, win_g):
    s, d = h.shape
    sw = win_g.shape[2]
    tn = min(512, sw)
    nh = sw // tn

    def body(h_ref, w_ref, o_ref):
        for rs in _chunks(s, 512):
            o_ref[rs, :] = _dot(h_ref[rs, :], w_ref[...], NN)

    return pl.pallas_call(
        body, name="proj_fwd", grid=(N_DEV * nh,),
        in_specs=[pl.BlockSpec((s, d), lambda j: (0, 0)),
                  pl.BlockSpec((None, d, tn), lambda j: (j // nh, 0, j % nh))],
        out_specs=pl.BlockSpec((None, s, tn), lambda j: (j // nh, 0, j % nh)),
        out_shape=jax.ShapeDtypeStruct((N_DEV, s, sw), F32),
        compiler_params=_cp(1),
    )(h, win_g)


def _conv_fwd(proj, conv_w, conv_b):
    _, s, sw = proj.shape
    tc = min(LANES, sw)

    def body(ba_ref, ca_ref, va_ref, cw_ref, cb_ref, z_ref):
        cv = ca_ref[...] * va_ref[...]
        u = (cb_ref[...] + cw_ref[0:1, :] * _shift_down(cv, 2) + cw_ref[1:2, :] * _shift_down(cv, 1)
             + cw_ref[2:3, :] * cv)
        z_ref[...] = (ba_ref[...] * u).astype(BF16)

    def part(k):
        return pl.BlockSpec((None, s, tc), lambda i: (k, 0, i))

    return pl.pallas_call(
        body, name="conv_fwd", grid=(sw // tc,),
        in_specs=[part(0), part(1), part(2),
                  pl.BlockSpec((CONV_K, tc), lambda i: (0, i)), pl.BlockSpec((1, tc), lambda i: (0, i))],
        out_specs=pl.BlockSpec((s, tc), lambda i: (0, i)),
        out_shape=jax.ShapeDtypeStruct((s, sw), BF16),
        compiler_params=_cp(1),
    )(proj, proj, proj, conv_w, conv_b)


def _pool_counts(shape, window):
    t = lax.broadcasted_iota(jnp.int32, shape, 0)
    return jnp.minimum(t + 1, window).astype(F32)


def _pool_fwd(proj):
    _, s, sw = proj.shape
    gw = sw // len(POOL_WINDOWS)

    def body(v_ref, p_ref):
        for gi, window in enumerate(POOL_WINDOWS):
            @pl.when(pl.program_id(0) == gi)
            def _():
                v = v_ref[...]
                acc, k = v, 1
                while k < window:
                    acc = acc + _shift_down(acc, k)
                    k *= 2
                p_ref[...] = (acc / _pool_counts(v.shape, window) - v).astype(BF16)

    return pl.pallas_call(
        body, name="pool_fwd", grid=(len(POOL_WINDOWS),),
        in_specs=[pl.BlockSpec((None, s, gw), lambda g: (3, 0, g))],
        out_specs=pl.BlockSpec((s, gw), lambda g: (0, g)),
        out_shape=jax.ShapeDtypeStruct((s, sw), BF16),
        compiler_params=_cp(1),
    )(proj)


def _merge_fwd(z, wa, p, wpool, proj, b_gate2, pool_scale):
    s, sw = z.shape
    tn = wa.shape[2]
    d = tn * N_DEV
    gw = sw // len(POOL_WINDOWS)
    nq = sw // tn

    def body(z_ref, wa_ref, p_ref, wp_ref, ga_ref, gb_ref, bg_ref, sc_ref, ya_ref, yb_ref, m_ref):
        for rs in _chunks(s, 512):
            ya = _dot(z_ref[rs, :], wa_ref[...], NN)
            yb = _dot(p_ref[rs, :], wp_ref[...], NN)
            sa = _sigmoid(ga_ref[rs, :] + bg_ref[0:1, :])
            sb = _sigmoid(gb_ref[rs, :] + bg_ref[1:2, :])
            ya_ref[rs, :] = ya.astype(BF16)
            yb_ref[rs, :] = yb.astype(BF16)
            m_ref[rs, :] = (sa * ya + sb * (yb * sc_ref[...])).astype(BF16)

    col = pl.BlockSpec((s, tn), lambda j: (0, j))
    out = jax.ShapeDtypeStruct((s, d), BF16)
    return pl.pallas_call(
        body, name="merge_fwd", grid=(N_DEV,),
        in_specs=[pl.BlockSpec((s, sw), lambda j: (0, 0)),
                  pl.BlockSpec((None, sw, tn), lambda j: (j, 0, 0)),
                  pl.BlockSpec((s, gw), lambda j: (0, j // 2)),
                  pl.BlockSpec((None, gw, tn), lambda j: (j // 2, 0, j % 2)),
                  pl.BlockSpec((None, s, tn), lambda j: (4 + j // nq, 0, j % nq)),
                  pl.BlockSpec((None, s, tn), lambda j: (6 + j // nq, 0, j % nq)),
                  pl.BlockSpec((2, tn), lambda j: (0, j)),
                  pl.BlockSpec((1, tn), lambda j: (0, j))],
        out_specs=[col, col, col],
        out_shape=[out, out, out],
        compiler_params=_cp(1),
    )(z, wa, p, wpool, proj, proj, b_gate2, pool_scale)


def _wo_fwd(merged, wo, x2d, g2):
    s, d = x2d.shape
    tm = min(256, s)

    def body(m_ref, wo_ref, x_ref, g_ref, x1_ref, h2_ref):
        x1 = x_ref[...] + _dot(m_ref[...], wo_ref[...], NN)
        x1_ref[...] = x1
        r = lax.rsqrt(jnp.mean(x1 * x1, axis=-1, keepdims=True) + EPS)
        h2_ref[...] = (x1 * r * g_ref[...]).astype(BF16)

    row = pl.BlockSpec((tm, d), lambda i: (i, 0))
    return pl.pallas_call(
        body, name="wo_fwd", grid=(s // tm,),
        in_specs=[row, pl.BlockSpec((d, d), lambda i: (0, 0)), row, pl.BlockSpec((1, d), lambda i: (0, 0))],
        out_specs=[row, row],
        out_shape=[jax.ShapeDtypeStruct((s, d), F32), jax.ShapeDtypeStruct((s, d), BF16)],
        compiler_params=_cp(1),
    )(merged, wo, x2d, g2)


def _ffn_up_act_fwd(h2, wg_g, wu_g):
    s, d = h2.shape
    f8 = wg_g.shape[2]
    th = min(1024, s)

    def body(h_ref, wg_ref, wu_ref, g_ref, u_ref, a_ref):
        i = pl.program_id(1)
        for rs in _chunks(th, 512):
            rows = pl.ds(pl.multiple_of(i * th + rs.start, rs.stop - rs.start), rs.stop - rs.start)
            a = h_ref[rows, :]
            g = _dot(a, wg_ref[...], NN)
            u = _dot(a, wu_ref[...], NN)
            g_ref[rs, :] = g.astype(BF16)
            u_ref[rs, :] = u.astype(BF16)
            a_ref[rs, :] = (g * _sigmoid(g) * u).astype(BF16)

    wspec = pl.BlockSpec((None, d, f8), lambda j, i: (j, 0, 0))
    ospec = pl.BlockSpec((None, th, f8), lambda j, i: (j, i, 0))
    out = jax.ShapeDtypeStruct((N_DEV, s, f8), BF16)
    return pl.pallas_call(
        body, name="ffn_up_fwd", grid=(N_DEV, s // th),
        in_specs=[pl.BlockSpec((s, d), lambda j, i: (0, 0)), wspec, wspec],
        out_specs=[ospec, ospec, ospec], out_shape=[out, out, out],
        compiler_params=_cp(2),
    )(h2, wg_g, wu_g)


def _ffn_down_fwd(act, wd_g):
    _, s, f8 = act.shape
    d = wd_g.shape[2]
    tn = min(1024, d)

    def body(a_ref, wd_ref, o_ref):
        j = pl.program_id(1)

        @pl.when(j == 0)
        def _():
            o_ref[...] = jnp.zeros_like(o_ref)

        for rs in _chunks(s, 1024):
            o_ref[rs, :] += _dot(a_ref[rs, :], wd_ref[...], NN)

    return pl.pallas_call(
        body, name="ffn_down_fwd", grid=(d // tn, N_DEV),
        in_specs=[pl.BlockSpec((None, s, f8), lambda n, j: (j, 0, 0)),
                  pl.BlockSpec((None, f8, tn), lambda n, j: (j, 0, n))],
        out_specs=pl.BlockSpec((s, tn), lambda n, j: (0, n)),
        out_shape=jax.ShapeDtypeStruct((s, d), F32),
        compiler_params=_cp(2),
    )(act, wd_g)


def _loss_bwd(ffn_out, x1, target, final_g):
    s, d = x1.shape
    tm = min(256, s)

    def body(f_ref, x1_ref, t_ref, gf_ref, dx_ref, dxb_ref, dgf_ref, loss_ref):
        @pl.when(pl.program_id(0) == 0)
        def _():
            dgf_ref[...] = jnp.zeros_like(dgf_ref)
            loss_ref[...] = jnp.zeros_like(loss_ref)

        x2 = x1_ref[...] + f_ref[...]
        r = lax.rsqrt(jnp.mean(x2 * x2, axis=-1, keepdims=True) + EPS)
        nrm = x2 * r
        gf = gf_ref[...]
        err = nrm * gf - t_ref[...]
        loss_ref[...] += jnp.sum(err * err) * (0.5 / d)
        dy = err * (1.0 / d)
        dgf_ref[...] += jnp.sum(dy * nrm, axis=0, keepdims=True)
        dn = dy * gf
        dx = r * (dn - nrm * jnp.mean(dn * nrm, axis=-1, keepdims=True))
        dx_ref[...] = dx
        dxb_ref[...] = dx.astype(BF16)

    row = pl.BlockSpec((tm, d), lambda i: (i, 0))
    vec = pl.BlockSpec((1, d), lambda i: (0, 0))
    return pl.pallas_call(
        body, name="loss_bwd", grid=(s // tm,),
        in_specs=[row, row, row, vec],
        out_specs=[row, row, vec, pl.BlockSpec((8, LANES), lambda i: (0, 0))],
        out_shape=[jax.ShapeDtypeStruct((s, d), F32), jax.ShapeDtypeStruct((s, d), BF16),
                   jax.ShapeDtypeStruct((1, d), F32), jax.ShapeDtypeStruct((8, LANES), F32)],
        compiler_params=_cp(1),
    )(ffn_out, x1, target, final_g)


def _ffn_gate_bwd(dx2b, wd_g, gact, uact):
    s, d = dx2b.shape
    f8 = gact.shape[2]
    th = min(1024, s)

    def body(dx_ref, wd_ref, g_ref, u_ref, dg_ref, du_ref):
        i = pl.program_id(1)
        for rs in _chunks(th, 256):
            rows = pl.ds(pl.multiple_of(i * th + rs.start, rs.stop - rs.start), rs.stop - rs.start)
            da = _dot(dx_ref[rows, :], wd_ref[...], NT)
            g = g_ref[rs, :].astype(F32)
            u = u_ref[rs, :].astype(F32)
            sg = _sigmoid(g)
            du_ref[rs, :] = (da * (g * sg)).astype(BF16)
            dg_ref[rs, :] = (da * u * (sg * (1.0 + g * (1.0 - sg)))).astype(BF16)

    aspec = pl.BlockSpec((None, th, f8), lambda j, i: (j, i, 0))
    out = jax.ShapeDtypeStruct((N_DEV, s, f8), BF16)
    return pl.pallas_call(
        body, name="ffn_act_bwd", grid=(N_DEV, s // th),
        in_specs=[pl.BlockSpec((s, d), lambda j, i: (0, 0)),
                  pl.BlockSpec((None, f8, d), lambda j, i: (j, 0, 0)), aspec, aspec],
        out_specs=[aspec, aspec], out_shape=[out, out],
        compiler_params=_cp(2),
    )(dx2b, wd_g, gact, uact)


def _wgrad_rows(a3, b, name, after=(), carry=()):
    _, s, k = a3.shape
    n = b.shape[1]
    nc = len(carry)
    c_in, c_out, c_shape, c_sems = _carry_specs(carry)

    def body(a_ref, b_ref, *rest):
        rest = rest[len(after):]
        o_ref = rest[nc]
        j = pl.program_id(0)
        _carry_run(j == 0, j == N_DEV - 1, rest[:nc], rest[nc + 1:2 * nc + 1], rest[2 * nc + 1:])
        o_ref[...] = _dot(a_ref[...], b_ref[...], TN).astype(BF16)

    outs = pl.pallas_call(
        body, name=name, grid=(N_DEV,),
        in_specs=[pl.BlockSpec((None, s, k), lambda j: (j, 0, 0)),
                  pl.BlockSpec((s, n), lambda j: (0, 0))] + _after_specs(after) + c_in,
        out_specs=[pl.BlockSpec((None, k, n), lambda j: (j, 0, 0))] + c_out,
        out_shape=[jax.ShapeDtypeStruct((N_DEV, k, n), BF16)] + c_shape,
        scratch_shapes=c_sems,
        compiler_params=_cp_carry(1, carry),
    )(a3, b, *after, *carry)
    return (outs[0], list(outs[1:])) if nc else outs[0]


def _wgrad_cols(a, b3, name, after=()):
    s, k = a.shape
    if b3.ndim == 2:
        n = b3.shape[1] // N_DEV
        b_spec = pl.BlockSpec((s, n), lambda j: (0, j))
    else:
        n = b3.shape[2]
        b_spec = pl.BlockSpec((None, s, n), lambda j: (j, 0, 0))

    def body(a_ref, b_ref, *rest):
        o_ref = rest[len(after)]
        o_ref[...] = _dot(a_ref[...], b_ref[...], TN).astype(BF16)

    return pl.pallas_call(
        body, name=name, grid=(N_DEV,),
        in_specs=[pl.BlockSpec((s, k), lambda j: (0, 0)), b_spec] + _after_specs(after),
        out_specs=pl.BlockSpec((None, k, n), lambda j: (j, 0, 0)),
        out_shape=jax.ShapeDtypeStruct((N_DEV, k, n), BF16),
        compiler_params=_cp(1),
    )(a, b3, *after)


def _input_grad(pairs, name, after=(), carry=()):
    s = pairs[0][0].shape[1]
    d = pairs[0][1].shape[1]
    tn = min(1024, d)
    npair = len(pairs)
    nc = len(carry)
    c_in, c_out, c_shape, c_sems = _carry_specs(carry)

    def body(*refs):
        ops = refs[:2 * npair]
        rest = refs[2 * npair + len(after):]
        o_ref = rest[nc]
        nh, j = pl.program_id(0), pl.program_id(1)
        _carry_run((nh == 0) & (j == 0), (nh == d // tn - 1) & (j == N_DEV - 1),
                   rest[:nc], rest[nc + 1:2 * nc + 1], rest[2 * nc + 1:])

        @pl.when(j == 0)
        def _():
            o_ref[...] = jnp.zeros_like(o_ref)

        for rs in _chunks(s, 1024):
            part = _dot(ops[0][rs, :], ops[1][...], NT)
            for q in range(1, npair):
                part = part + _dot(ops[2 * q][rs, :], ops[2 * q + 1][...], NT)
            o_ref[rs, :] += part

    in_specs, args = [], []
    for a3, w3 in pairs:
        k = a3.shape[2]
        in_specs += [pl.BlockSpec((None, s, k), lambda n, j: (j, 0, 0)),
                     pl.BlockSpec((None, tn, k), lambda n, j: (j, n, 0))]
        args += [a3, w3]
    outs = pl.pallas_call(
        body, name=name, grid=(d // tn, N_DEV),
        in_specs=in_specs + _after_specs(after) + c_in,
        out_specs=[pl.BlockSpec((s, tn), lambda n, j: (0, n))] + c_out,
        out_shape=[jax.ShapeDtypeStruct((s, d), F32)] + c_shape,
        scratch_shapes=c_sems,
        compiler_params=_cp_carry(2, carry),
    )(*args, *after, *carry)
    return (outs[0], list(outs[1:])) if nc else outs[0]


def _rms_bwd(dh, xres, g, dres, name):
    s, d = xres.shape
    tm = min(256, s)

    def body(dh_ref, x_ref, g_ref, dres_ref, dx_ref, dxb_ref, dg_ref):
        @pl.when(pl.program_id(0) == 0)
        def _():
            dg_ref[...] = jnp.zeros_like(dg_ref)

        xv = x_ref[...]
        dh_v = dh_ref[...]
        r = lax.rsqrt(jnp.mean(xv * xv, axis=-1, keepdims=True) + EPS)
        nrm = xv * r
        dg_ref[...] += jnp.sum(dh_v * nrm, axis=0, keepdims=True)
        dn = dh_v * g_ref[...]
        dx = dres_ref[...] + r * (dn - nrm * jnp.mean(dn * nrm, axis=-1, keepdims=True))
        dx_ref[...] = dx
        dxb_ref[...] = dx.astype(BF16)

    row = pl.BlockSpec((tm, d), lambda i: (i, 0))
    vec = pl.BlockSpec((1, d), lambda i: (0, 0))
    return pl.pallas_call(
        body, name=name, grid=(s // tm,),
        in_specs=[row, row, vec, row],
        out_specs=[row, row, vec],
        out_shape=[jax.ShapeDtypeStruct((s, d), F32), jax.ShapeDtypeStruct((s, d), BF16),
                   jax.ShapeDtypeStruct((1, d), F32)],
        compiler_params=_cp(1),
    )(dh, xres, g, dres)


def _wgrad_full(a, b, name, after=()):
    s, k = a.shape
    n = b.shape[1]
    tk = min(512, k)

    def body(a_ref, b_ref, *rest):
        o_ref = rest[len(after)]
        o_ref[...] = _dot(a_ref[...], b_ref[...], TN).astype(BF16)

    return pl.pallas_call(
        body, name=name, grid=(k // tk,),
        in_specs=[pl.BlockSpec((s, tk), lambda j: (0, j)),
                  pl.BlockSpec((s, n), lambda j: (0, 0))] + _after_specs(after),
        out_specs=pl.BlockSpec((tk, n), lambda j: (j, 0)),
        out_shape=jax.ShapeDtypeStruct((k, n), BF16),
        compiler_params=_cp(1),
    )(a, b, *after)


def _wgrad_pool(p, dyb, n_groups):
    s, sw = p.shape
    d = dyb.shape[1]
    gw, go = sw // n_groups, d // n_groups
    ts = min(512, s)
    ns = s // ts

    def body(a_ref, b_ref, o_ref, acc_ref):
        i = pl.program_id(1)

        @pl.when(i == 0)
        def _():
            acc_ref[...] = jnp.zeros_like(acc_ref)

        acc_ref[...] += _dot(a_ref[...], b_ref[...], TN)

        @pl.when(i == ns - 1)
        def _():
            o_ref[...] = acc_ref[...].astype(BF16)

    return pl.pallas_call(
        body, name="wgrad_pool", grid=(n_groups, ns),
        in_specs=[pl.BlockSpec((ts, gw), lambda g, i: (i, g)),
                  pl.BlockSpec((ts, go), lambda g, i: (i, g))],
        out_specs=pl.BlockSpec((None, gw, go), lambda g, i: (g, 0, 0)),
        out_shape=jax.ShapeDtypeStruct((n_groups, gw, go), BF16),
        scratch_shapes=[pltpu.VMEM((gw, go), F32)],
        compiler_params=_cp(2),
    )(p, dyb)


def _wo_bwd(dx1b, wo, ya, yb, proj, b_gate2, pool_scale, after=()):
    s, d = dx1b.shape
    sw = proj.shape[2]
    tn = d // N_DEV
    nq = sw // tn

    def body(dx_ref, wo_ref, ya_ref, yb_ref, ga_ref, gb_ref, bg_ref, sc_ref, *rest):
        dya_ref, dyb_ref, dp_ref, dbg_ref, dsc_ref = rest[len(after):]
        dbg_ref[...] = jnp.zeros_like(dbg_ref)
        dsc_ref[...] = jnp.zeros_like(dsc_ref)
        for rs in _chunks(s, 256):
            dm = _dot(dx_ref[rs, :], wo_ref[...], NT)
            ya_v = ya_ref[rs, :].astype(F32)
            yb_v = yb_ref[rs, :].astype(F32)
            sa = _sigmoid(ga_ref[rs, :] + bg_ref[0:1, :])
            sb = _sigmoid(gb_ref[rs, :] + bg_ref[1:2, :])
            sc = sc_ref[...]
            dya_ref[rs, :] = (dm * sa).astype(BF16)
            dsb = dm * sb
            dyb_ref[rs, :] = (dsb * sc).astype(BF16)
            dsc_ref[...] += jnp.sum(dsb * yb_v, axis=0, keepdims=True)
            dga = dm * ya_v * (sa * (1.0 - sa))
            dgb = dm * (yb_v * sc) * (sb * (1.0 - sb))
            dp_ref[0, rs, :] = dga.astype(BF16)
            dp_ref[1, rs, :] = dgb.astype(BF16)
            dbg_ref[0:1, :] += jnp.sum(dga, axis=0, keepdims=True)
            dbg_ref[1:2, :] += jnp.sum(dgb, axis=0, keepdims=True)

    col = pl.BlockSpec((s, tn), lambda j: (0, j))
    out = jax.ShapeDtypeStruct((s, d), BF16)
    return pl.pallas_call(
        body, name="wo_bwd", grid=(N_DEV,),
        in_specs=[pl.BlockSpec((s, d), lambda j: (0, 0)),
                  pl.BlockSpec((tn, d), lambda j: (j, 0)), col, col,
                  pl.BlockSpec((None, s, tn), lambda j: (4 + j // nq, 0, j % nq)),
                  pl.BlockSpec((None, s, tn), lambda j: (6 + j // nq, 0, j % nq)),
                  pl.BlockSpec((2, tn), lambda j: (0, j)),
                  pl.BlockSpec((1, tn), lambda j: (0, j))] + _after_specs(after),
        out_specs=[col, col,
                   pl.BlockSpec((2, None, s, tn), lambda j: (1, j // nq, 0, j % nq)),
                   pl.BlockSpec((2, tn), lambda j: (0, j)),
                   pl.BlockSpec((1, tn), lambda j: (0, j))],
        out_shape=[out, out, jax.ShapeDtypeStruct((4, 2, s, sw), BF16),
                   jax.ShapeDtypeStruct((2, d), F32), jax.ShapeDtypeStruct((1, d), F32)],
        compiler_params=_cp(1),
    )(dx1b, wo, ya, yb, proj, proj, b_gate2, pool_scale, *after)


def _conv_bwd(dproj, dya, wa, proj, conv_w, conv_b):
    s, d = dya.shape
    sw, tn = wa.shape[1], wa.shape[2]
    tc = min(LANES, sw)

    def body(dproj_hbm, dya_ref, wa_ref, ba_ref, ca_ref, va_ref, cw_ref, cb_ref,
             dp_ref, dcw_ref, dcb_ref, dz_ref):
        del dproj_hbm
        for rs in _chunks(s, 512):
            part = _dot(dya_ref[rs, 0:tn], wa_ref[0], NT)
            for j in range(1, N_DEV):
                part = part + _dot(dya_ref[rs, j * tn:(j + 1) * tn], wa_ref[j], NT)
            dz_ref[rs, :] = part
        dz = dz_ref[...]
        ba, ca, va = ba_ref[...], ca_ref[...], va_ref[...]
        cv = ca * va
        cv1, cv2 = _shift_down(cv, 1), _shift_down(cv, 2)
        w0, w1, w2 = cw_ref[0:1, :], cw_ref[1:2, :], cw_ref[2:3, :]
        u = cb_ref[...] + w0 * cv2 + w1 * cv1 + w2 * cv
        du = dz * ba
        dp_ref[0] = (dz * u).astype(BF16)
        dcv = w2 * du + w1 * _shift_up(du, 1) + w0 * _shift_up(du, 2)
        dp_ref[1] = (dcv * va).astype(BF16)
        dp_ref[2] = (dcv * ca).astype(BF16)
        dcw_ref[0:1, :] = jnp.sum(du * cv2, axis=0, keepdims=True)
        dcw_ref[1:2, :] = jnp.sum(du * cv1, axis=0, keepdims=True)
        dcw_ref[2:3, :] = jnp.sum(du * cv, axis=0, keepdims=True)
        dcb_ref[...] = jnp.sum(du, axis=0, keepdims=True)

    def part(k):
        return pl.BlockSpec((None, s, tc), lambda i: (k, 0, i))

    return pl.pallas_call(
        body, name="conv_bwd", grid=(sw // tc,),
        in_specs=[pl.BlockSpec(memory_space=pl.ANY),
                  pl.BlockSpec((s, d), lambda i: (0, 0)),
                  pl.BlockSpec((N_DEV, tc, tn), lambda i: (0, i, 0)),
                  part(0), part(1), part(2),
                  pl.BlockSpec((CONV_K, tc), lambda i: (0, i)), pl.BlockSpec((1, tc), lambda i: (0, i))],
        out_specs=[pl.BlockSpec((3, s, tc), lambda i: (0, 0, i)),
                   pl.BlockSpec((CONV_K, tc), lambda i: (0, i)), pl.BlockSpec((1, tc), lambda i: (0, i))],
        out_shape=[jax.ShapeDtypeStruct(dproj.shape, BF16),
                   jax.ShapeDtypeStruct((CONV_K, sw), F32), jax.ShapeDtypeStruct((1, sw), F32)],
        scratch_shapes=[pltpu.VMEM((s, tc), F32)],
        input_output_aliases={0: 0},
        compiler_params=_cp(1),
    )(dproj, dya, wa, proj, proj, proj, conv_w, conv_b)


def _pool_bwd(dproj, dyb, wpool):
    s, d = dyb.shape
    n_groups, gw, go = wpool.shape

    def body(dproj_hbm, dyb_ref, wp_ref, dp_ref):
        del dproj_hbm
        for gi, window in enumerate(POOL_WINDOWS):
            @pl.when(pl.program_id(0) == gi)
            def _():
                dpool = _dot(dyb_ref[...], wp_ref[...], NT)
                acc, k = dpool / _pool_counts(dpool.shape, window), 1
                while k < window:
                    acc = acc + _shift_up(acc, k)
                    k *= 2
                dp_ref[...] = (acc - dpool).astype(BF16)

    return pl.pallas_call(
        body, name="pool_bwd", grid=(n_groups,),
        in_specs=[pl.BlockSpec(memory_space=pl.ANY),
                  pl.BlockSpec((s, go), lambda g: (0, g)),
                  pl.BlockSpec((None, gw, go), lambda g: (g, 0, 0))],
        out_specs=pl.BlockSpec((None, s, gw), lambda g: (3, 0, g)),
        out_shape=jax.ShapeDtypeStruct(dproj.shape, BF16),
        input_output_aliases={0: 0},
        compiler_params=_cp(1),
    )(dproj, dyb, wpool)


def _rows128(v):
    return v.reshape(-1, LANES)


def kernel(x, norm1_g, w_in, b_gate, conv_w, conv_b, w_a_out, w_pool, pool_scale, w_o, norm2_g, w_ffn_gate, w_ffn_up, w_ffn_down, final_g, loss_target, m_norm1_g, m_w_in, m_b_gate, m_conv_w, m_conv_b, m_w_a_out, m_w_pool, m_pool_scale, m_w_o, m_norm2_g, m_w_ffn_gate, m_w_ffn_up, m_w_ffn_down, m_final_g, v_norm1_g, v_w_in, v_b_gate, v_conv_w, v_conv_b, v_w_a_out, v_w_pool, v_pool_scale, v_w_o, v_norm2_g, v_w_ffn_gate, v_w_ffn_up, v_w_ffn_down, v_final_g):
    s, d = x.shape[1], x.shape[2]
    sw = w_in.shape[2]
    n_groups = w_pool.shape[1]
    gw = w_pool.shape[2]
    go = w_pool.shape[3] * N_DEV
    f8 = w_ffn_gate.shape[2]
    cws = conv_w.shape[2]
    assert sw == conv_w.shape[2] * N_DEV == gw * n_groups and go * n_groups == d and n_groups == len(POOL_WINDOWS)

    xi, yi, ci = _coords()
    me = 4 * xi + 2 * yi + ci
    my_chip = 2 * xi + yi

    x2d = x.reshape(s, d)
    target = loss_target.reshape(s, d)
    final_g2 = final_g.reshape(1, d)
    b_gate2 = b_gate.reshape(2, d)

    big_names = ["w_in", "w_a_out", "w_pool", "w_o", "w_ffn_gate", "w_ffn_up", "w_ffn_down"]
    big_w = [w_in, w_a_out, w_pool, w_o, w_ffn_gate, w_ffn_up, w_ffn_down]
    big_m = [m_w_in, m_w_a_out, m_w_pool, m_w_o, m_w_ffn_gate, m_w_ffn_up, m_w_ffn_down]
    big_v = [v_w_in, v_w_a_out, v_w_pool, v_w_o, v_w_ffn_gate, v_w_ffn_up, v_w_ffn_down]
    shapes2d = [(w.size // w.shape[-1], w.shape[-1]) for w in big_w]
    big_w2 = [w.reshape(sh) for w, sh in zip(big_w, shapes2d)]
    transposed = (4, 5)

    def view2d(t, a):
        t2 = t.reshape(shapes2d[a])
        return t2.T if a in transposed else t2

    def unview(o, a):
        return (o.T if a in transposed else o).reshape(big_w[a].shape)

    sb = [_cast_bf16(w, "cast_" + nm) for w, nm in zip(big_w2, big_names)]
    win_g, wa_g, wpool_g, wo_g = _allgather_big(sb[0:4], "allgather_mixer", COLLECTIVE_GATHER)
    wg_g, wu_g = _allgather_big(sb[4:6], "allgather_ffn_up", COLLECTIVE_GATHER)
    (wd_g,) = _allgather_big(sb[6:7], "allgather_ffn_down", COLLECTIVE_GATHER)
    convw_g = _allgather_small(jnp.pad(conv_w.reshape(CONV_K, cws), ((0, 8 - CONV_K), (0, 0))), "allgather_conv_w")
    conv_w_full = convw_g[:, :CONV_K, :].transpose(1, 0, 2).reshape(CONV_K, sw)
    wpool = wpool_g.reshape(N_DEV, n_groups, gw, go // N_DEV).transpose(1, 2, 0, 3).reshape(n_groups, gw, go)
    wo = wo_g.reshape(d, d)

    h = _rms_fwd(x2d, norm1_g)
    proj = _proj_fwd(h, win_g)
    z = _conv_fwd(proj, conv_w_full, conv_b)
    p = _pool_fwd(proj)
    ya, yb, merged = _merge_fwd(z, wa_g, p, wpool, proj, b_gate2, pool_scale)
    x1, h2 = _wo_fwd(merged, wo, x2d, norm2_g)
    gact, uact, act = _ffn_up_act_fwd(h2, wg_g, wu_g)
    ffn_out = _ffn_down_fwd(act, wd_g)
    dx2, dx2b, d_final_g, loss_blk = _loss_bwd(ffn_out, x1, target, final_g2)

    other_chips = jnp.stack([2 * (1 - xi) + yi, 2 * xi + (1 - yi), 2 * (1 - xi) + (1 - yi)])
    others = jnp.concatenate([other_chips, 2 * other_chips + ci]).astype(jnp.int32)

    def partials(grads, recvs, names):
        return [_chip_partial(others, g3, r, "chip_partial_" + nm) for g3, r, nm in zip(grads, recvs, names)]

    own = jnp.stack([me, my_chip]).astype(jnp.int32)

    def adam(a, g3, sib, chips):
        outs = _adam_big(own, view2d(big_w[a], a), view2d(big_m[a], a), view2d(big_v[a], a),
                         g3, sib, chips, "adam_" + big_names[a])
        return [unview(o, a) for o in outs]

    big_out = [None] * len(big_names)
    dg_act, du_act = _ffn_gate_bwd(dx2b, wd_g, gact, uact)
    gw_gate = _wgrad_rows(dg_act, h2, "wgrad_ffn_gate")
    gw_up = _wgrad_rows(du_act, h2, "wgrad_ffn_up")
    gw_down, sib_gu = _wgrad_rows(act, dx2b, "wgrad_ffn_down", carry=[gw_gate, gw_up])
    ps_gu = partials([gw_gate, gw_up], sib_gu, ["w_ffn_gate", "w_ffn_up"])
    chips_gu = _exchange_chips(ps_gu, "rs_chips_ffn_up", COLLECTIVE_CHIPS)
    dh2, sib_down = _input_grad([(dg_act, wg_g), (du_act, wu_g)], "ffn_in_bwd", after=ps_gu, carry=[gw_down])
    ps_down = partials([gw_down], sib_down, ["w_ffn_down"])
    chips_down = _exchange_chips(ps_down, "rs_chips_ffn_down", COLLECTIVE_CHIPS)
    dx1, dx1b, d_norm2_g = _rms_bwd(dh2, x1, norm2_g, dx2, "rms2_bwd")
    dya, dyb, dproj42, d_b_gate, d_pool_scale = _wo_bwd(dx1b, wo, ya, yb, proj, b_gate2, pool_scale, after=ps_down)
    dproj = dproj42.reshape(N_DEV, s, sw)
    dproj, d_conv_w, d_conv_b = _conv_bwd(dproj, dya, wa_g, proj, conv_w_full, conv_b)
    dproj = _pool_bwd(dproj, dyb, wpool)
    gw_in = _wgrad_cols(h, dproj, "wgrad_in")
    sib_in, ps_in = _sibling_partials([gw_in], "rs_sibling_w_in", COLLECTIVE_SIBLING)
    chips_in = _exchange_chips(ps_in, "rs_chips_w_in", COLLECTIVE_CHIPS)
    gw_o = _wgrad_full(merged, dx1b, "wgrad_o", after=ps_in)
    gw_a = _wgrad_cols(z, dya, "wgrad_a_out", after=ps_in)
    gw_pool = _wgrad_pool(p, dyb, n_groups)
    mix3 = [gw_a,
            gw_pool.reshape(n_groups, gw, N_DEV, go // N_DEV).transpose(2, 0, 1, 3).reshape(N_DEV, n_groups * gw, go // N_DEV),
            gw_o.reshape(N_DEV, d // N_DEV, d)]
    sib_mix, ps_mix = _sibling_partials(mix3, "rs_sibling_mixer", COLLECTIVE_SIBLING)
    chips_mix = _exchange_chips(ps_mix, "rs_chips_mixer", COLLECTIVE_CHIPS)
    big_out[4] = adam(4, gw_gate, sib_gu[0], chips_gu[0])
    big_out[5] = adam(5, gw_up, sib_gu[1], chips_gu[1])
    big_out[6] = adam(6, gw_down, sib_down[0], chips_down[0])
    dh = _input_grad([(dproj, win_g)], "proj_in_bwd", after=ps_mix + [big_out[4][0], big_out[5][0], big_out[6][0]])
    grad_x, _, d_norm1_g = _rms_bwd(dh, x2d, norm1_g, dx1, "rms1_bwd")
    big_out[0] = adam(0, gw_in, sib_in[0], chips_in[0])
    for k in range(3):
        big_out[1 + k] = adam(1 + k, mix3[k], sib_mix[k], chips_mix[k])

    small_parts = [d_norm1_g, d_b_gate, d_conv_w, d_conv_b, d_pool_scale, d_norm2_g, d_final_g, loss_blk]
    sizes = [v.size for v in small_parts]
    packed = jnp.concatenate([_rows128(v) for v in small_parts], axis=0)
    summed = _sum_small(_allgather_small(packed, "allgather_small_grads")).reshape(-1)
    offs = [0]
    for n in sizes:
        offs.append(offs[-1] + n)
    g_norm1, g_bgate, g_convw_full, g_convb, g_pscale, g_norm2, g_final, loss_sum = [
        summed[offs[k]:offs[k + 1]] for k in range(len(sizes))]
    loss = loss_sum[0]
    g_convw = lax.dynamic_slice(g_convw_full.reshape(CONV_K, sw), (0, me * cws), (CONV_K, cws))
    small_w = [norm1_g, b_gate, conv_w, conv_b, pool_scale, norm2_g, final_g]
    small_m = [m_norm1_g, m_b_gate, m_conv_w, m_conv_b, m_pool_scale, m_norm2_g, m_final_g]
    small_v = [v_norm1_g, v_b_gate, v_conv_w, v_conv_b, v_pool_scale, v_norm2_g, v_final_g]
    small_g = [g_norm1, g_bgate, g_convw, g_convb, g_pscale, g_norm2, g_final]

    def pack(parts):
        flat = jnp.concatenate([v.reshape(-1) for v in parts])
        pad = (-flat.size) % (8 * LANES)
        return jnp.pad(flat, (0, pad)).reshape(-1, LANES)

    s_delta, s_m, s_v = _adam_small(pack(small_w), pack(small_g), pack(small_m), pack(small_v))
    soffs = [0]
    for w in small_w:
        soffs.append(soffs[-1] + w.size)

    def unpack(buf):
        flat = buf.reshape(-1)
        return [flat[soffs[k]:soffs[k + 1]].reshape(small_w[k].shape) for k in range(len(small_w))]

    small_grads = [g.reshape(w.shape) for g, w in zip(small_g, small_w)]
    small_delta, small_new_m, small_new_v = unpack(s_delta), unpack(s_m), unpack(s_v)

    order = ["norm1_g", "w_in", "b_gate", "conv_w", "conv_b", "w_a_out", "w_pool", "pool_scale", "w_o", "norm2_g",
             "w_ffn_gate", "w_ffn_up", "w_ffn_down", "final_g"]
    small_names = ["norm1_g", "b_gate", "conv_w", "conv_b", "pool_scale", "norm2_g", "final_g"]
    per_kind = [{}, {}, {}, {}]
    for a, nm in enumerate(big_names):
        for kind in range(4):
            per_kind[kind][nm] = big_out[a][kind]
    for k, nm in enumerate(small_names):
        per_kind[0][nm] = small_grads[k]
        per_kind[1][nm] = small_delta[k]
        per_kind[2][nm] = small_new_m[k]
        per_kind[3][nm] = small_new_v[k]
    result = [loss, grad_x.reshape(x.shape)]
    for kind in range(4):
        result += [per_kind[kind][nm] for nm in order]
    return tuple(result)
```

```python
import jax
import jax.numpy as jnp
from jax import lax
from jax.experimental import pallas as pl
from jax.experimental.pallas import tpu as pltpu
from jax.experimental.pallas import tpu_sc as plsc

F32 = jnp.float32
BF16 = jnp.bfloat16
MESH = pl.DeviceIdType.MESH

N_DEV = 8
EPS = 1e-6
CONV_K = 3
POOL_WINDOWS = (2, 4, 8, 16)
ADAM_LR = 0.001
ADAM_B1 = 0.9
ADAM_B2 = 0.999
ADAM_EPS = 1e-08
ADAM_WD = 0.01
ADAM_STEP = 10

V7X_VMEM_LIMIT_BYTES = 56 * 1024 * 1024
LANES = 128

COLLECTIVE_GATHER = 1
COLLECTIVE_SIBLING = 2
COLLECTIVE_CHIPS = 3
SEQUENCER_COST_BYTES = 4 * 10**9

NN = ((1,), (0,))
NT = ((1,), (1,))
TN = ((0,), (0,))


def _dot(a, b, dims):
    return lax.dot_general(a, b, (dims, ((), ())), preferred_element_type=F32)


def _cp(n_axes):
    return pltpu.CompilerParams(dimension_semantics=("arbitrary",) * n_axes,
                                vmem_limit_bytes=V7X_VMEM_LIMIT_BYTES)


def _row_tile(rows, bytes_per_row, cap_bytes):
    best = None
    for t in range(16, rows + 1, 16):
        if rows % t == 0 and t * bytes_per_row <= cap_bytes:
            best = t
    return best if best is not None else rows


def _chunks(total, size):
    size = min(size, total)
    assert total % size == 0
    return [slice(r, r + size) for r in range(0, total, size)]


def _after_specs(after):
    return [pl.BlockSpec(memory_space=pl.ANY)] * len(after)


def _shift_down(v, k):
    row = lax.broadcasted_iota(jnp.int32, v.shape, 0)
    return jnp.where(row >= k, pltpu.roll(v, k, 0), 0.0)


def _shift_up(v, k):
    n = v.shape[0]
    row = lax.broadcasted_iota(jnp.int32, v.shape, 0)
    return jnp.where(row < n - k, pltpu.roll(v, n - k, 0), 0.0)


def _sigmoid(v):
    return jax.nn.sigmoid(v)


def _cast_bf16(w2d, name):
    rows, cols = w2d.shape
    tr = _row_tile(rows, cols * 4, 2 << 20)

    def body(i_ref, o_ref):
        o_ref[...] = i_ref[...].astype(BF16)

    return pl.pallas_call(
        body, name=name, grid=(rows // tr,),
        in_specs=[pl.BlockSpec((tr, cols), lambda i: (i, 0))],
        out_specs=pl.BlockSpec((tr, cols), lambda i: (i, 0)),
        out_shape=jax.ShapeDtypeStruct((rows, cols), BF16),
        compiler_params=_cp(1),
    )(w2d)


def _rms_fwd(x2d, g):
    s, d = x2d.shape
    tm = min(256, s)

    def body(x_ref, g_ref, h_ref):
        xv = x_ref[...]
        r = lax.rsqrt(jnp.mean(xv * xv, axis=-1, keepdims=True) + EPS)
        h_ref[...] = (xv * r * g_ref[...]).astype(BF16)

    return pl.pallas_call(
        body, name="rms1_fwd", grid=(s // tm,),
        in_specs=[pl.BlockSpec((tm, d), lambda i: (i, 0)), pl.BlockSpec((1, d), lambda i: (0, 0))],
        out_specs=pl.BlockSpec((tm, d), lambda i: (i, 0)),
        out_shape=jax.ShapeDtypeStruct((s, d), BF16),
        compiler_params=_cp(1),
    )(x2d, g)


def _coords():
    return lax.axis_index("x"), lax.axis_index("y"), lax.axis_index("c")


def _slot(p):
    return 4 * p[0] + 2 * p[1] + p[2]


def _handshake(peers):
    barrier = pltpu.get_barrier_semaphore()
    for peer in peers:
        pl.semaphore_signal(barrier, inc=1, device_id=peer, device_id_type=MESH)
    pl.semaphore_wait(barrier, len(peers))


def _sequencer_call(body, out_type, scratch_types, name, collective_id):
    return pl.kernel(
        body, out_type=out_type, name=name,
        mesh=plsc.ScalarSubcoreMesh(axis_name="seq", num_cores=1),
        scratch_types=scratch_types,
        cost_estimate=pl.CostEstimate(flops=0, transcendentals=0, bytes_accessed=SEQUENCER_COST_BYTES),
        compiler_params=pltpu.CompilerParams(collective_id=collective_id))


def _allgather_big(shards, name, collective_id, after=()):
    n = len(shards)

    def body(*refs):
        ins, outs = refs[:n], refs[n + len(after):2 * n + len(after)]
        send_sems, recv_sems, local_sems = refs[2 * n + len(after):]
        x, y, c = _coords()
        me, sibling = (x, y, c), (x, y, 1 - c)
        x_nbr, y_nbr, diag = (1 - x, y), (x, 1 - y), (1 - x, 1 - y)
        relay_from = (x + (1 - c) * (1 - 2 * x), y + c * (1 - 2 * y))
        relay_to = (x + c * (1 - 2 * x), y + (1 - c) * (1 - 2 * y))
        _handshake([sibling, (*x_nbr, c), (*y_nbr, c)])

        def copy(a, k, block, to, src=None):
            dst = outs[a].at[_slot(block)]
            return pltpu.make_async_remote_copy(
                src_ref=dst if src is None else src, dst_ref=dst,
                send_sem=send_sems.at[a, k], recv_sem=recv_sems.at[a, k],
                device_id=to, device_id_type=MESH)

        mine, sends = [], []
        for a in range(n):
            cp = pltpu.make_async_copy(ins[a], outs[a].at[_slot(me)], local_sems.at[a])
            cp.start()
            mine.append(cp)
            first = [copy(a, 0, me, sibling, src=ins[a]),
                     copy(a, 1, me, (*x_nbr, c), src=ins[a]),
                     copy(a, 2, me, (*y_nbr, c), src=ins[a])]
            for cp in first:
                cp.start()
            sends += first
        for a in range(n):
            copy(a, 1 + c, (*relay_from, c), me).wait_recv()
            passed = [copy(a, 3, (*relay_from, c), (*relay_to, c)), copy(a, 4 + c, (*relay_from, c), sibling)]
            for cp in passed:
                cp.start()
            copy(a, 2 - c, (*relay_to, c), me).wait_recv()
            cp = copy(a, 5 - c, (*relay_to, c), sibling)
            cp.start()
            passed.append(cp)
            copy(a, 3, (*diag, c), me).wait_recv()
            cp = copy(a, 6, (*diag, c), sibling)
            cp.start()
            sends += passed + [cp]
        for a in range(n):
            copy(a, 0, sibling, me).wait_recv()
            copy(a, 4, (*x_nbr, 1 - c), me).wait_recv()
            copy(a, 5, (*y_nbr, 1 - c), me).wait_recv()
            copy(a, 6, (*diag, 1 - c), me).wait_recv()
        for cp in sends:
            cp.wait_send()
        for cp in mine:
            cp.wait()

    return _sequencer_call(
        body, [jax.ShapeDtypeStruct((N_DEV,) + s.shape, s.dtype) for s in shards],
        [pltpu.SemaphoreType.DMA((n, 7)), pltpu.SemaphoreType.DMA((n, 7)), pltpu.SemaphoreType.DMA((n,))],
        name, collective_id)(*shards, *after)


def _sibling_copies(ins, recvs, send_sems, recv_sems):
    x, y, c = _coords()
    return [pltpu.make_async_remote_copy(
        src_ref=ins[a].at[2 * q + (1 - c)], dst_ref=recvs[a].at[q],
        send_sem=send_sems.at[a, q], recv_sem=recv_sems.at[a, q],
        device_id=(x, y, 1 - c), device_id_type=MESH) for a in range(len(ins)) for q in range(4)]


def _carry_specs(carry):
    any_spec = pl.BlockSpec(memory_space=pl.ANY)
    n = len(carry)
    sems = [pltpu.SemaphoreType.DMA((n, 4)), pltpu.SemaphoreType.DMA((n, 4))] if n else []
    return ([any_spec] * n, [any_spec] * n,
            [jax.ShapeDtypeStruct((4,) + g.shape[1:], g.dtype) for g in carry], sems)


def _carry_run(first, last, ins, recvs, sems):
    if not ins:
        return

    @pl.when(first)
    def _():
        x, y, c = _coords()
        _handshake([(x, y, 1 - c)])
        for cp in _sibling_copies(ins, recvs, *sems):
            cp.start()

    @pl.when(last)
    def _():
        copies = _sibling_copies(ins, recvs, *sems)
        for cp in copies:
            cp.wait_recv()
        for cp in copies:
            cp.wait_send()


def _cp_carry(n_axes, carry):
    if not carry:
        return _cp(n_axes)
    return pltpu.CompilerParams(dimension_semantics=("arbitrary",) * n_axes, vmem_limit_bytes=V7X_VMEM_LIMIT_BYTES,
                                collective_id=COLLECTIVE_SIBLING)


def _exchange_chips(psums, name, collective_id):
    n = len(psums)

    def body(*refs):
        ins, outs = refs[:n], refs[n:2 * n]
        send_sems, recv_sems = refs[2 * n:]
        x, y, c = _coords()
        chips = [(1 - x, y), (x, 1 - y), (1 - x, 1 - y)]
        _handshake([(*chip, c) for chip in chips])
        copies = []
        for a in range(n):
            for j, chip in enumerate(chips):
                cp = pltpu.make_async_remote_copy(
                    src_ref=ins[a].at[2 * chip[0] + chip[1]], dst_ref=outs[a].at[j],
                    send_sem=send_sems.at[a, j], recv_sem=recv_sems.at[a, j],
                    device_id=(*chip, c), device_id_type=MESH)
                cp.start()
                copies.append(cp)
        for cp in copies:
            cp.wait_recv()
        for cp in copies:
            cp.wait_send()

    return _sequencer_call(
        body, [jax.ShapeDtypeStruct((3,) + p.shape[1:], p.dtype) for p in psums],
        [pltpu.SemaphoreType.DMA((n, 3)), pltpu.SemaphoreType.DMA((n, 3))],
        name, collective_id)(*psums)


def _allgather_small(v2d, name):
    rows, cols = v2d.shape

    def body(v_ref, out_ref, send_sems, recv_sems):
        x, y, c = _coords()
        me = (x, y, c)
        out_ref[_slot(me)] = v_ref[...]
        peers = []
        for k in range(1, N_DEV):
            fx, fy, fc = (k >> 2) & 1, (k >> 1) & 1, k & 1
            peers.append(((1 - x) if fx else x, (1 - y) if fy else y, (1 - c) if fc else c))
        sends = []
        for k, peer in enumerate(peers):
            cp = pltpu.make_async_remote_copy(
                src_ref=v_ref, dst_ref=out_ref.at[_slot(me)],
                send_sem=send_sems.at[k], recv_sem=recv_sems.at[k],
                device_id=peer, device_id_type=MESH)
            cp.start()
            sends.append(cp)
        for k, peer in enumerate(peers):
            pltpu.make_async_remote_copy(
                src_ref=v_ref, dst_ref=out_ref.at[_slot(peer)],
                send_sem=send_sems.at[k], recv_sem=recv_sems.at[k],
                device_id=peer, device_id_type=MESH).wait_recv()
        for cp in sends:
            cp.wait_send()

    vmem = pl.BlockSpec(memory_space=pltpu.VMEM)
    return pl.pallas_call(
        body, name=name, in_specs=[vmem], out_specs=vmem,
        out_shape=jax.ShapeDtypeStruct((N_DEV, rows, cols), v2d.dtype),
        scratch_shapes=[pltpu.SemaphoreType.DMA((N_DEV - 1,)), pltpu.SemaphoreType.DMA((N_DEV - 1,))],
    )(v2d)


def _chip_partial(others, g3, recv, name):
    _, rows, cols = g3.shape
    tr = _row_tile(rows, cols * 2, 2 << 20)

    def body(others_ref, g_ref, r_ref, o_ref):
        o_ref[...] = (g_ref[...].astype(F32) + r_ref[...].astype(F32)).astype(BF16)

    return pl.pallas_call(
        body, name=name,
        grid_spec=pltpu.PrefetchScalarGridSpec(
            num_scalar_prefetch=1, grid=(3, rows // tr),
            in_specs=[pl.BlockSpec((None, tr, cols), lambda k, i, o: (o[3 + k], i, 0)),
                      pl.BlockSpec((None, tr, cols), lambda k, i, o: (o[k], i, 0))],
            out_specs=pl.BlockSpec((None, tr, cols), lambda k, i, o: (o[k], i, 0))),
        out_shape=jax.ShapeDtypeStruct((4, rows, cols), BF16),
        compiler_params=_cp(2),
    )(others, g3, recv)


def _adam_math(w, g, m, v):
    m = ADAM_B1 * m + (1.0 - ADAM_B1) * g
    v = ADAM_B2 * v + (1.0 - ADAM_B2) * (g * g)
    m_hat = m / (1.0 - ADAM_B1 ** ADAM_STEP)
    v_hat = v / (1.0 - ADAM_B2 ** ADAM_STEP)
    delta = -ADAM_LR * (m_hat / (jnp.sqrt(v_hat) + ADAM_EPS) + ADAM_WD * w)
    return delta, m, v


def _adam_big(own, w, m, v, g3, recv_sib, recv_chips, name):
    rows, cols = w.shape
    tr = _row_tile(rows, cols * 4, 2 << 20)

    def body(own_ref, w_ref, m_ref, v_ref, g_ref, rs_ref, rc_ref, go_ref, do_ref, mo_ref, vo_ref):
        g = g_ref[...].astype(F32) + rs_ref[...].astype(F32)
        g = g + rc_ref[0].astype(F32)
        g = g + rc_ref[1].astype(F32)
        g = g + rc_ref[2].astype(F32)
        delta, m_new, v_new = _adam_math(w_ref[...], g, m_ref[...], v_ref[...])
        go_ref[...] = g
        do_ref[...] = delta
        mo_ref[...] = m_new
        vo_ref[...] = v_new

    blk = pl.BlockSpec((tr, cols), lambda i, o: (i, 0))
    out = jax.ShapeDtypeStruct((rows, cols), F32)
    return pl.pallas_call(
        body, name=name,
        grid_spec=pltpu.PrefetchScalarGridSpec(
            num_scalar_prefetch=1, grid=(rows // tr,),
            in_specs=[blk, blk, blk,
                      pl.BlockSpec((None, tr, cols), lambda i, o: (o[0], i, 0)),
                      pl.BlockSpec((None, tr, cols), lambda i, o: (o[1], i, 0)),
                      pl.BlockSpec((3, tr, cols), lambda i, o: (0, i, 0))],
            out_specs=[blk, blk, blk, blk]),
        out_shape=[out, out, out, out],
        compiler_params=_cp(1),
    )(own, w, m, v, g3, recv_sib, recv_chips)


def _sum_small(gathered):
    _, rows, cols = gathered.shape

    def body(g_ref, o_ref):
        acc = g_ref[0]
        for k in range(1, N_DEV):
            acc = acc + g_ref[k]
        o_ref[...] = acc

    vmem = pl.BlockSpec(memory_space=pltpu.VMEM)
    return pl.pallas_call(body, name="small_grad_sum", in_specs=[vmem], out_specs=vmem,
                          out_shape=jax.ShapeDtypeStruct((rows, cols), F32))(gathered)


def _adam_small(w, g, m, v):
    def body(w_ref, g_ref, m_ref, v_ref, do_ref, mo_ref, vo_ref):
        delta, m_new, v_new = _adam_math(w_ref[...], g_ref[...], m_ref[...], v_ref[...])
        do_ref[...] = delta
        mo_ref[...] = m_new
        vo_ref[...] = v_new

    vmem = pl.BlockSpec(memory_space=pltpu.VMEM)
    out = jax.ShapeDtypeStruct(w.shape, F32)
    return pl.pallas_call(body, name="adam_small", in_specs=[vmem] * 4, out_specs=[vmem] * 3,
                          out_shape=[out, out, out])(w, g, m, v)


def _proj_fwd(h, win_g):
    s, d = h.shape
    sw = win_g.shape[2]
    tn = min(512, sw)
    nh = sw // tn

    def body(h_ref, w_ref, o_ref):
        for rs in _chunks(s, 512):
            o_ref[rs, :] = _dot(h_ref[rs, :], w_ref[...], NN)

    return pl.pallas_call(
        body, name="proj_fwd", grid=(N_DEV * nh,),
        in_specs=[pl.BlockSpec((s, d), lambda j: (0, 0)),
                  pl.BlockSpec((None, d, tn), lambda j: (j // nh, 0, j % nh))],
        out_specs=pl.BlockSpec((None, s, tn), lambda j: (j // nh, 0, j % nh)),
        out_shape=jax.ShapeDtypeStruct((N_DEV, s, sw), F32),
        compiler_params=_cp(1),
    )(h, win_g)


def _conv_fwd(proj, conv_w, conv_b):
    _, s, sw = proj.shape
    tc = min(LANES, sw)

    def body(ba_ref, ca_ref, va_ref, cw_ref, cb_ref, z_ref):
        cv = ca_ref[...] * va_ref[...]
        u = (cb_ref[...] + cw_ref[0:1, :] * _shift_down(cv, 2) + cw_ref[1:2, :] * _shift_down(cv, 1)
             + cw_ref[2:3, :] * cv)
        z_ref[...] = (ba_ref[...] * u).astype(BF16)

    def part(k):
        return pl.BlockSpec((None, s, tc), lambda i: (k, 0, i))

    return pl.pallas_call(
        body, name="conv_fwd", grid=(sw // tc,),
        in_specs=[part(0), part(1), part(2),
                  pl.BlockSpec((CONV_K, tc), lambda i: (0, i)), pl.BlockSpec((1, tc), lambda i: (0, i))],
        out_specs=pl.BlockSpec((s, tc), lambda i: (0, i)),
        out_shape=jax.ShapeDtypeStruct((s, sw), BF16),
        compiler_params=_cp(1),
    )(proj, proj, proj, conv_w, conv_b)


def _pool_counts(shape, window):
    t = lax.broadcasted_iota(jnp.int32, shape, 0)
    return jnp.minimum(t + 1, window).astype(F32)


def _pool_fwd(proj):
    _, s, sw = proj.shape
    gw = sw // len(POOL_WINDOWS)

    def body(v_ref, p_ref):
        for gi, window in enumerate(POOL_WINDOWS):
            @pl.when(pl.program_id(0) == gi)
            def _():
                v = v_ref[...]
                acc, k = v, 1
                while k < window:
                    acc = acc + _shift_down(acc, k)
                    k *= 2
                p_ref[...] = (acc / _pool_counts(v.shape, window) - v).astype(BF16)

    return pl.pallas_call(
        body, name="pool_fwd", grid=(len(POOL_WINDOWS),),
        in_specs=[pl.BlockSpec((None, s, gw), lambda g: (3, 0, g))],
        out_specs=pl.BlockSpec((s, gw), lambda g: (0, g)),
        out_shape=jax.ShapeDtypeStruct((s, sw), BF16),
        compiler_params=_cp(1),
    )(proj)


def _merge_fwd(z, wa, p, wpool, proj, b_gate2, pool_scale):
    s, sw = z.shape
    tn = wa.shape[2]
    d = tn * N_DEV
    gw = sw // len(POOL_WINDOWS)
    nq = sw // tn

    def body(z_ref, wa_ref, p_ref, wp_ref, ga_ref, gb_ref, bg_ref, sc_ref, ya_ref, yb_ref, m_ref):
        for rs in _chunks(s, 512):
            ya = _dot(z_ref[rs, :], wa_ref[...], NN)
            yb = _dot(p_ref[rs, :], wp_ref[...], NN)
            sa = _sigmoid(ga_ref[rs, :] + bg_ref[0:1, :])
            sb = _sigmoid(gb_ref[rs, :] + bg_ref[1:2, :])
            ya_ref[rs, :] = ya.astype(BF16)
            yb_ref[rs, :] = yb.astype(BF16)
            m_ref[rs, :] = (sa * ya + sb * (yb * sc_ref[...])).astype(BF16)

    col = pl.BlockSpec((s, tn), lambda j: (0, j))
    out = jax.ShapeDtypeStruct((s, d), BF16)
    return pl.pallas_call(
        body, name="merge_fwd", grid=(N_DEV,),
        in_specs=[pl.BlockSpec((s, sw), lambda j: (0, 0)),
                  pl.BlockSpec((None, sw, tn), lambda j: (j, 0, 0)),
                  pl.BlockSpec((s, gw), lambda j: (0, j // 2)),
                  pl.BlockSpec((None, gw, tn), lambda j: (j // 2, 0, j % 2)),
                  pl.BlockSpec((None, s, tn), lambda j: (4 + j // nq, 0, j % nq)),
                  pl.BlockSpec((None, s, tn), lambda j: (6 + j // nq, 0, j % nq)),
                  pl.BlockSpec((2, tn), lambda j: (0, j)),
                  pl.BlockSpec((1, tn), lambda j: (0, j))],
        out_specs=[col, col, col],
        out_shape=[out, out, out],
        compiler_params=_cp(1),
    )(z, wa, p, wpool, proj, proj, b_gate2, pool_scale)


def _wo_fwd(merged, wo, x2d, g2):
    s, d = x2d.shape
    tm = min(256, s)

    def body(m_ref, wo_ref, x_ref, g_ref, x1_ref, h2_ref):
        x1 = x_ref[...] + _dot(m_ref[...], wo_ref[...], NN)
        x1_ref[...] = x1
        r = lax.rsqrt(jnp.mean(x1 * x1, axis=-1, keepdims=True) + EPS)
        h2_ref[...] = (x1 * r * g_ref[...]).astype(BF16)

    row = pl.BlockSpec((tm, d), lambda i: (i, 0))
    return pl.pallas_call(
        body, name="wo_fwd", grid=(s // tm,),
        in_specs=[row, pl.BlockSpec((d, d), lambda i: (0, 0)), row, pl.BlockSpec((1, d), lambda i: (0, 0))],
        out_specs=[row, row],
        out_shape=[jax.ShapeDtypeStruct((s, d), F32), jax.ShapeDtypeStruct((s, d), BF16)],
        compiler_params=_cp(1),
    )(merged, wo, x2d, g2)


def _ffn_up_act_fwd(h2, wg_g, wu_g):
    s, d = h2.shape
    f8 = wg_g.shape[2]
    th = min(1024, s)

    def body(h_ref, wg_ref, wu_ref, g_ref, u_ref, a_ref):
        i = pl.program_id(1)
        for rs in _chunks(th, 512):
            rows = pl.ds(pl.multiple_of(i * th + rs.start, rs.stop - rs.start), rs.stop - rs.start)
            a = h_ref[rows, :]
            g = _dot(a, wg_ref[...], NN)
            u = _dot(a, wu_ref[...], NN)
            g_ref[rs, :] = g.astype(BF16)
            u_ref[rs, :] = u.astype(BF16)
            a_ref[rs, :] = (g * _sigmoid(g) * u).astype(BF16)

    wspec = pl.BlockSpec((None, d, f8), lambda j, i: (j, 0, 0))
    ospec = pl.BlockSpec((None, th, f8), lambda j, i: (j, i, 0))
    out = jax.ShapeDtypeStruct((N_DEV, s, f8), BF16)
    return pl.pallas_call(
        body, name="ffn_up_fwd", grid=(N_DEV, s // th),
        in_specs=[pl.BlockSpec((s, d), lambda j, i: (0, 0)), wspec, wspec],
        out_specs=[ospec, ospec, ospec], out_shape=[out, out, out],
        compiler_params=_cp(2),
    )(h2, wg_g, wu_g)


def _ffn_down_fwd(act, wd_g):
    _, s, f8 = act.shape
    d = wd_g.shape[2]
    tn = min(1024, d)

    def body(a_ref, wd_ref, o_ref):
        j = pl.program_id(1)

        @pl.when(j == 0)
        def _():
            o_ref[...] = jnp.zeros_like(o_ref)

        for rs in _chunks(s, 1024):
            o_ref[rs, :] += _dot(a_ref[rs, :], wd_ref[...], NN)

    return pl.pallas_call(
        body, name="ffn_down_fwd", grid=(d // tn, N_DEV),
        in_specs=[pl.BlockSpec((None, s, f8), lambda n, j: (j, 0, 0)),
                  pl.BlockSpec((None, f8, tn), lambda n, j: (j, 0, n))],
        out_specs=pl.BlockSpec((s, tn), lambda n, j: (0, n)),
        out_shape=jax.ShapeDtypeStruct((s, d), F32),
        compiler_params=_cp(2),
    )(act, wd_g)


def _loss_bwd(ffn_out, x1, target, final_g):
    s, d = x1.shape
    tm = min(256, s)

    def body(f_ref, x1_ref, t_ref, gf_ref, dx_ref, dxb_ref, dgf_ref, loss_ref):
        @pl.when(pl.program_id(0) == 0)
        def _():
            dgf_ref[...] = jnp.zeros_like(dgf_ref)
            loss_ref[...] = jnp.zeros_like(loss_ref)

        x2 = x1_ref[...] + f_ref[...]
        r = lax.rsqrt(jnp.mean(x2 * x2, axis=-1, keepdims=True) + EPS)
        nrm = x2 * r
        gf = gf_ref[...]
        err = nrm * gf - t_ref[...]
        loss_ref[...] += jnp.sum(err * err) * (0.5 / d)
        dy = err * (1.0 / d)
        dgf_ref[...] += jnp.sum(dy * nrm, axis=0, keepdims=True)
        dn = dy * gf
        dx = r * (dn - nrm * jnp.mean(dn * nrm, axis=-1, keepdims=True))
        dx_ref[...] = dx
        dxb_ref[...] = dx.astype(BF16)

    row = pl.BlockSpec((tm, d), lambda i: (i, 0))
    vec = pl.BlockSpec((1, d), lambda i: (0, 0))
    return pl.pallas_call(
        body, name="loss_bwd", grid=(s // tm,),
        in_specs=[row, row, row, vec],
        out_specs=[row, row, vec, pl.BlockSpec((8, LANES), lambda i: (0, 0))],
        out_shape=[jax.ShapeDtypeStruct((s, d), F32), jax.ShapeDtypeStruct((s, d), BF16),
                   jax.ShapeDtypeStruct((1, d), F32), jax.ShapeDtypeStruct((8, LANES), F32)],
        compiler_params=_cp(1),
    )(ffn_out, x1, target, final_g)


def _ffn_gate_bwd(dx2b, wd_g, gact, uact):
    s, d = dx2b.shape
    f8 = gact.shape[2]
    th = min(1024, s)

    def body(dx_ref, wd_ref, g_ref, u_ref, dg_ref, du_ref, da_ref):
        i = pl.program_id(1)
        chunks = _chunks(th, 256)

        def matmul(rs):
            rows = pl.ds(pl.multiple_of(i * th + rs.start, rs.stop - rs.start), rs.stop - rs.start)
            da_ref[rs, :] = _dot(dx_ref[rows, :], wd_ref[...], NT)

        matmul(chunks[0])
        for k, rs in enumerate(chunks):
            if k + 1 < len(chunks):
                matmul(chunks[k + 1])
            da = da_ref[rs, :]
            g = g_ref[rs, :].astype(F32)
            u = u_ref[rs, :].astype(F32)
            sg = _sigmoid(g)
            du_ref[rs, :] = (da * (g * sg)).astype(BF16)
            dg_ref[rs, :] = (da * u * (sg * (1.0 + g * (1.0 - sg)))).astype(BF16)

    aspec = pl.BlockSpec((None, th, f8), lambda j, i: (j, i, 0))
    out = jax.ShapeDtypeStruct((N_DEV, s, f8), BF16)
    return pl.pallas_call(
        body, name="ffn_act_bwd", grid=(N_DEV, s // th),
        in_specs=[pl.BlockSpec((s, d), lambda j, i: (0, 0)),
                  pl.BlockSpec((None, f8, d), lambda j, i: (j, 0, 0)), aspec, aspec],
        out_specs=[aspec, aspec], out_shape=[out, out],
        scratch_shapes=[pltpu.VMEM((th, f8), F32)],
        compiler_params=_cp(2),
    )(dx2b, wd_g, gact, uact)


def _wgrad_rows(a3, b, name, after=(), carry=()):
    _, s, k = a3.shape
    n = b.shape[1]
    nc = len(carry)
    c_in, c_out, c_shape, c_sems = _carry_specs(carry)

    def body(a_ref, b_ref, *rest):
        rest = rest[len(after):]
        o_ref = rest[nc]
        j = pl.program_id(0)
        _carry_run(j == 0, j == N_DEV - 1, rest[:nc], rest[nc + 1:2 * nc + 1], rest[2 * nc + 1:])
        o_ref[...] = _dot(a_ref[...], b_ref[...], TN).astype(BF16)

    outs = pl.pallas_call(
        body, name=name, grid=(N_DEV,),
        in_specs=[pl.BlockSpec((None, s, k), lambda j: (j, 0, 0)),
                  pl.BlockSpec((s, n), lambda j: (0, 0))] + _after_specs(after) + c_in,
        out_specs=[pl.BlockSpec((None, k, n), lambda j: (j, 0, 0))] + c_out,
        out_shape=[jax.ShapeDtypeStruct((N_DEV, k, n), BF16)] + c_shape,
        scratch_shapes=c_sems,
        compiler_params=_cp_carry(1, carry),
    )(a3, b, *after, *carry)
    return (outs[0], list(outs[1:])) if nc else outs[0]


def _wgrad_cols(a, b3, name, after=()):
    s, k = a.shape
    if b3.ndim == 2:
        n = b3.shape[1] // N_DEV
        b_spec = pl.BlockSpec((s, n), lambda j: (0, j))
    else:
        n = b3.shape[2]
        b_spec = pl.BlockSpec((None, s, n), lambda j: (j, 0, 0))

    def body(a_ref, b_ref, *rest):
        o_ref = rest[len(after)]
        o_ref[...] = _dot(a_ref[...], b_ref[...], TN).astype(BF16)

    return pl.pallas_call(
        body, name=name, grid=(N_DEV,),
        in_specs=[pl.BlockSpec((s, k), lambda j: (0, 0)), b_spec] + _after_specs(after),
        out_specs=pl.BlockSpec((None, k, n), lambda j: (j, 0, 0)),
        out_shape=jax.ShapeDtypeStruct((N_DEV, k, n), BF16),
        compiler_params=_cp(1),
    )(a, b3, *after)


def _input_grad(pairs, name, after=(), carry=()):
    s = pairs[0][0].shape[1]
    d = pairs[0][1].shape[1]
    tn = min(1024, d)
    npair = len(pairs)
    nc = len(carry)
    c_in, c_out, c_shape, c_sems = _carry_specs(carry)

    def body(*refs):
        ops = refs[:2 * npair]
        rest = refs[2 * npair + len(after):]
        o_ref = rest[nc]
        nh, j = pl.program_id(0), pl.program_id(1)
        _carry_run((nh == 0) & (j == 0), (nh == d // tn - 1) & (j == N_DEV - 1),
                   rest[:nc], rest[nc + 1:2 * nc + 1], rest[2 * nc + 1:])

        @pl.when(j == 0)
        def _():
            o_ref[...] = jnp.zeros_like(o_ref)

        for rs in _chunks(s, 1024):
            part = _dot(ops[0][rs, :], ops[1][...], NT)
            for q in range(1, npair):
                part = part + _dot(ops[2 * q][rs, :], ops[2 * q + 1][...], NT)
            o_ref[rs, :] += part

    in_specs, args = [], []
    for a3, w3 in pairs:
        k = a3.shape[2]
        in_specs += [pl.BlockSpec((None, s, k), lambda n, j: (j, 0, 0)),
                     pl.BlockSpec((None, tn, k), lambda n, j: (j, n, 0))]
        args += [a3, w3]
    outs = pl.pallas_call(
        body, name=name, grid=(d // tn, N_DEV),
        in_specs=in_specs + _after_specs(after) + c_in,
        out_specs=[pl.BlockSpec((s, tn), lambda n, j: (0, n))] + c_out,
        out_shape=[jax.ShapeDtypeStruct((s, d), F32)] + c_shape,
        scratch_shapes=c_sems,
        compiler_params=_cp_carry(2, carry),
    )(*args, *after, *carry)
    return (outs[0], list(outs[1:])) if nc else outs[0]


def _rms_bwd(dh, xres, g, dres, name):
    s, d = xres.shape
    tm = min(256, s)

    def body(dh_ref, x_ref, g_ref, dres_ref, dx_ref, dxb_ref, dg_ref):
        @pl.when(pl.program_id(0) == 0)
        def _():
            dg_ref[...] = jnp.zeros_like(dg_ref)

        xv = x_ref[...]
        dh_v = dh_ref[...]
        r = lax.rsqrt(jnp.mean(xv * xv, axis=-1, keepdims=True) + EPS)
        nrm = xv * r
        dg_ref[...] += jnp.sum(dh_v * nrm, axis=0, keepdims=True)
        dn = dh_v * g_ref[...]
        dx = dres_ref[...] + r * (dn - nrm * jnp.mean(dn * nrm, axis=-1, keepdims=True))
        dx_ref[...] = dx
        dxb_ref[...] = dx.astype(BF16)

    row = pl.BlockSpec((tm, d), lambda i: (i, 0))
    vec = pl.BlockSpec((1, d), lambda i: (0, 0))
    return pl.pallas_call(
        body, name=name, grid=(s // tm,),
        in_specs=[row, row, vec, row],
        out_specs=[row, row, vec],
        out_shape=[jax.ShapeDtypeStruct((s, d), F32), jax.ShapeDtypeStruct((s, d), BF16),
                   jax.ShapeDtypeStruct((1, d), F32)],
        compiler_params=_cp(1),
    )(dh, xres, g, dres)


def _wgrad_full(a, b, name, after=(), carry=()):
    s, k = a.shape
    n = b.shape[1]
    tk = min(512, k)
    nc = len(carry)
    c_in, c_out, c_shape, c_sems = _carry_specs(carry)

    def body(a_ref, b_ref, *rest):
        rest = rest[len(after):]
        o_ref = rest[nc]
        j = pl.program_id(0)
        _carry_run(j == 0, j == k // tk - 1, rest[:nc], rest[nc + 1:2 * nc + 1], rest[2 * nc + 1:])
        o_ref[...] = _dot(a_ref[...], b_ref[...], TN).astype(BF16)

    outs = pl.pallas_call(
        body, name=name, grid=(k // tk,),
        in_specs=[pl.BlockSpec((s, tk), lambda j: (0, j)),
                  pl.BlockSpec((s, n), lambda j: (0, 0))] + _after_specs(after) + c_in,
        out_specs=[pl.BlockSpec((tk, n), lambda j: (j, 0))] + c_out,
        out_shape=[jax.ShapeDtypeStruct((k, n), BF16)] + c_shape,
        scratch_shapes=c_sems,
        compiler_params=_cp_carry(1, carry),
    )(a, b, *after, *carry)
    return (outs[0], list(outs[1:])) if nc else outs[0]


def _wgrad_pool(p, dyb, n_groups):
    s, sw = p.shape
    d = dyb.shape[1]
    gw, go = sw // n_groups, d // n_groups
    ts = min(512, s)
    ns = s // ts

    def body(a_ref, b_ref, o_ref, acc_ref):
        i = pl.program_id(1)

        @pl.when(i == 0)
        def _():
            acc_ref[...] = jnp.zeros_like(acc_ref)

        acc_ref[...] += _dot(a_ref[...], b_ref[...], TN)

        @pl.when(i == ns - 1)
        def _():
            o_ref[...] = acc_ref[...].astype(BF16)

    return pl.pallas_call(
        body, name="wgrad_pool", grid=(n_groups, ns),
        in_specs=[pl.BlockSpec((ts, gw), lambda g, i: (i, g)),
                  pl.BlockSpec((ts, go), lambda g, i: (i, g))],
        out_specs=pl.BlockSpec((None, gw, go), lambda g, i: (g, 0, 0)),
        out_shape=jax.ShapeDtypeStruct((n_groups, gw, go), BF16),
        scratch_shapes=[pltpu.VMEM((gw, go), F32)],
        compiler_params=_cp(2),
    )(p, dyb)


def _wo_bwd(dx1b, wo, ya, yb, proj, b_gate2, pool_scale, after=()):
    s, d = dx1b.shape
    sw = proj.shape[2]
    tn = d // N_DEV
    nq = sw // tn

    def body(dx_ref, wo_ref, ya_ref, yb_ref, ga_ref, gb_ref, bg_ref, sc_ref, *rest):
        dya_ref, dyb_ref, dp_ref, dbg_ref, dsc_ref, dm_ref = rest[len(after):]
        dbg_ref[...] = jnp.zeros_like(dbg_ref)
        dsc_ref[...] = jnp.zeros_like(dsc_ref)
        for rs in _chunks(s, 1024):
            dm_ref[rs, :] = _dot(dx_ref[rs, :], wo_ref[...], NT)
        for rs in _chunks(s, 256):
            dm = dm_ref[rs, :]
            ya_v = ya_ref[rs, :].astype(F32)
            yb_v = yb_ref[rs, :].astype(F32)
            sa = _sigmoid(ga_ref[rs, :] + bg_ref[0:1, :])
            sb = _sigmoid(gb_ref[rs, :] + bg_ref[1:2, :])
            sc = sc_ref[...]
            dya_ref[rs, :] = (dm * sa).astype(BF16)
            dsb = dm * sb
            dyb_ref[rs, :] = (dsb * sc).astype(BF16)
            dsc_ref[...] += jnp.sum(dsb * yb_v, axis=0, keepdims=True)
            dga = dm * ya_v * (sa * (1.0 - sa))
            dgb = dm * (yb_v * sc) * (sb * (1.0 - sb))
            dp_ref[0, rs, :] = dga.astype(BF16)
            dp_ref[1, rs, :] = dgb.astype(BF16)
            dbg_ref[0:1, :] += jnp.sum(dga, axis=0, keepdims=True)
            dbg_ref[1:2, :] += jnp.sum(dgb, axis=0, keepdims=True)

    col = pl.BlockSpec((s, tn), lambda j: (0, j))
    out = jax.ShapeDtypeStruct((s, d), BF16)
    return pl.pallas_call(
        body, name="wo_bwd", grid=(N_DEV,),
        in_specs=[pl.BlockSpec((s, d), lambda j: (0, 0)),
                  pl.BlockSpec((tn, d), lambda j: (j, 0)), col, col,
                  pl.BlockSpec((None, s, tn), lambda j: (4 + j // nq, 0, j % nq)),
                  pl.BlockSpec((None, s, tn), lambda j: (6 + j // nq, 0, j % nq)),
                  pl.BlockSpec((2, tn), lambda j: (0, j)),
                  pl.BlockSpec((1, tn), lambda j: (0, j))] + _after_specs(after),
        out_specs=[col, col,
                   pl.BlockSpec((2, None, s, tn), lambda j: (1, j // nq, 0, j % nq)),
                   pl.BlockSpec((2, tn), lambda j: (0, j)),
                   pl.BlockSpec((1, tn), lambda j: (0, j))],
        out_shape=[out, out, jax.ShapeDtypeStruct((4, 2, s, sw), BF16),
                   jax.ShapeDtypeStruct((2, d), F32), jax.ShapeDtypeStruct((1, d), F32)],
        scratch_shapes=[pltpu.VMEM((s, tn), F32)],
        compiler_params=_cp(1),
    )(dx1b, wo, ya, yb, proj, proj, b_gate2, pool_scale, *after)


def _conv_bwd(dproj, dya, wa, proj, conv_w, conv_b):
    s, d = dya.shape
    sw, tn = wa.shape[1], wa.shape[2]
    tc = min(LANES, sw)

    def body(dproj_hbm, dya_ref, wa_ref, ba_ref, ca_ref, va_ref, cw_ref, cb_ref,
             dp_ref, dcw_ref, dcb_ref, dz_ref):
        del dproj_hbm
        for rs in _chunks(s, 512):
            part = _dot(dya_ref[rs, 0:tn], wa_ref[0], NT)
            for j in range(1, N_DEV):
                part = part + _dot(dya_ref[rs, j * tn:(j + 1) * tn], wa_ref[j], NT)
            dz_ref[rs, :] = part
        dz = dz_ref[...]
        ba, ca, va = ba_ref[...], ca_ref[...], va_ref[...]
        cv = ca * va
        cv1, cv2 = _shift_down(cv, 1), _shift_down(cv, 2)
        w0, w1, w2 = cw_ref[0:1, :], cw_ref[1:2, :], cw_ref[2:3, :]
        u = cb_ref[...] + w0 * cv2 + w1 * cv1 + w2 * cv
        du = dz * ba
        dp_ref[0] = (dz * u).astype(BF16)
        dcv = w2 * du + w1 * _shift_up(du, 1) + w0 * _shift_up(du, 2)
        dp_ref[1] = (dcv * va).astype(BF16)
        dp_ref[2] = (dcv * ca).astype(BF16)
        dcw_ref[0:1, :] = jnp.sum(du * cv2, axis=0, keepdims=True)
        dcw_ref[1:2, :] = jnp.sum(du * cv1, axis=0, keepdims=True)
        dcw_ref[2:3, :] = jnp.sum(du * cv, axis=0, keepdims=True)
        dcb_ref[...] = jnp.sum(du, axis=0, keepdims=True)

    def part(k):
        return pl.BlockSpec((None, s, tc), lambda i: (k, 0, i))

    return pl.pallas_call(
        body, name="conv_bwd", grid=(sw // tc,),
        in_specs=[pl.BlockSpec(memory_space=pl.ANY),
                  pl.BlockSpec((s, d), lambda i: (0, 0)),
                  pl.BlockSpec((N_DEV, tc, tn), lambda i: (0, i, 0)),
                  part(0), part(1), part(2),
                  pl.BlockSpec((CONV_K, tc), lambda i: (0, i)), pl.BlockSpec((1, tc), lambda i: (0, i))],
        out_specs=[pl.BlockSpec((3, s, tc), lambda i: (0, 0, i)),
                   pl.BlockSpec((CONV_K, tc), lambda i: (0, i)), pl.BlockSpec((1, tc), lambda i: (0, i))],
        out_shape=[jax.ShapeDtypeStruct(dproj.shape, BF16),
                   jax.ShapeDtypeStruct((CONV_K, sw), F32), jax.ShapeDtypeStruct((1, sw), F32)],
        scratch_shapes=[pltpu.VMEM((s, tc), F32)],
        input_output_aliases={0: 0},
        compiler_params=_cp(1),
    )(dproj, dya, wa, proj, proj, proj, conv_w, conv_b)


def _pool_bwd(dproj, dyb, wpool):
    s, d = dyb.shape
    n_groups, gw, go = wpool.shape

    def body(dproj_hbm, dyb_ref, wp_ref, dp_ref):
        del dproj_hbm
        for gi, window in enumerate(POOL_WINDOWS):
            @pl.when(pl.program_id(0) == gi)
            def _():
                dpool = _dot(dyb_ref[...], wp_ref[...], NT)
                acc, k = dpool / _pool_counts(dpool.shape, window), 1
                while k < window:
                    acc = acc + _shift_up(acc, k)
                    k *= 2
                dp_ref[...] = (acc - dpool).astype(BF16)

    return pl.pallas_call(
        body, name="pool_bwd", grid=(n_groups,),
        in_specs=[pl.BlockSpec(memory_space=pl.ANY),
                  pl.BlockSpec((s, go), lambda g: (0, g)),
                  pl.BlockSpec((None, gw, go), lambda g: (g, 0, 0))],
        out_specs=pl.BlockSpec((None, s, gw), lambda g: (3, 0, g)),
        out_shape=jax.ShapeDtypeStruct(dproj.shape, BF16),
        input_output_aliases={0: 0},
        compiler_params=_cp(1),
    )(dproj, dyb, wpool)


def _rows128(v):
    return v.reshape(-1, LANES)


def kernel(x, norm1_g, w_in, b_gate, conv_w, conv_b, w_a_out, w_pool, pool_scale, w_o, norm2_g, w_ffn_gate, w_ffn_up, w_ffn_down, final_g, loss_target, m_norm1_g, m_w_in, m_b_gate, m_conv_w, m_conv_b, m_w_a_out, m_w_pool, m_pool_scale, m_w_o, m_norm2_g, m_w_ffn_gate, m_w_ffn_up, m_w_ffn_down, m_final_g, v_norm1_g, v_w_in, v_b_gate, v_conv_w, v_conv_b, v_w_a_out, v_w_pool, v_pool_scale, v_w_o, v_norm2_g, v_w_ffn_gate, v_w_ffn_up, v_w_ffn_down, v_final_g):
    s, d = x.shape[1], x.shape[2]
    sw = w_in.shape[2]
    n_groups = w_pool.shape[1]
    gw = w_pool.shape[2]
    go = w_pool.shape[3] * N_DEV
    f8 = w_ffn_gate.shape[2]
    cws = conv_w.shape[2]
    assert sw == conv_w.shape[2] * N_DEV == gw * n_groups and go * n_groups == d and n_groups == len(POOL_WINDOWS)

    xi, yi, ci = _coords()
    me = 4 * xi + 2 * yi + ci
    my_chip = 2 * xi + yi

    x2d = x.reshape(s, d)
    target = loss_target.reshape(s, d)
    final_g2 = final_g.reshape(1, d)
    b_gate2 = b_gate.reshape(2, d)

    big_names = ["w_in", "w_a_out", "w_pool", "w_o", "w_ffn_gate", "w_ffn_up", "w_ffn_down"]
    big_w = [w_in, w_a_out, w_pool, w_o, w_ffn_gate, w_ffn_up, w_ffn_down]
    big_m = [m_w_in, m_w_a_out, m_w_pool, m_w_o, m_w_ffn_gate, m_w_ffn_up, m_w_ffn_down]
    big_v = [v_w_in, v_w_a_out, v_w_pool, v_w_o, v_w_ffn_gate, v_w_ffn_up, v_w_ffn_down]
    shapes2d = [(w.size // w.shape[-1], w.shape[-1]) for w in big_w]
    big_w2 = [w.reshape(sh) for w, sh in zip(big_w, shapes2d)]
    transposed = (4, 5)

    def view2d(t, a):
        t2 = t.reshape(shapes2d[a])
        return t2.T if a in transposed else t2

    def unview(o, a):
        return (o.T if a in transposed else o).reshape(big_w[a].shape)

    sb = [_cast_bf16(w, "cast_" + nm) for w, nm in zip(big_w2, big_names)]
    win_g, wa_g, wpool_g, wo_g = _allgather_big(sb[0:4], "allgather_mixer", COLLECTIVE_GATHER)
    wg_g, wu_g = _allgather_big(sb[4:6], "allgather_ffn_up", COLLECTIVE_GATHER)
    (wd_g,) = _allgather_big(sb[6:7], "allgather_ffn_down", COLLECTIVE_GATHER)
    convw_g = _allgather_small(jnp.pad(conv_w.reshape(CONV_K, cws), ((0, 8 - CONV_K), (0, 0))), "allgather_conv_w")
    conv_w_full = convw_g[:, :CONV_K, :].transpose(1, 0, 2).reshape(CONV_K, sw)
    wpool = wpool_g.reshape(N_DEV, n_groups, gw, go // N_DEV).transpose(1, 2, 0, 3).reshape(n_groups, gw, go)
    wo = wo_g.reshape(d, d)

    h = _rms_fwd(x2d, norm1_g)
    proj = _proj_fwd(h, win_g)
    z = _conv_fwd(proj, conv_w_full, conv_b)
    p = _pool_fwd(proj)
    ya, yb, merged = _merge_fwd(z, wa_g, p, wpool, proj, b_gate2, pool_scale)
    x1, h2 = _wo_fwd(merged, wo, x2d, norm2_g)
    gact, uact, act = _ffn_up_act_fwd(h2, wg_g, wu_g)
    ffn_out = _ffn_down_fwd(act, wd_g)
    dx2, dx2b, d_final_g, loss_blk = _loss_bwd(ffn_out, x1, target, final_g2)

    other_chips = jnp.stack([2 * (1 - xi) + yi, 2 * xi + (1 - yi), 2 * (1 - xi) + (1 - yi)])
    others = jnp.concatenate([other_chips, 2 * other_chips + ci]).astype(jnp.int32)

    def partials(grads, recvs, names):
        return [_chip_partial(others, g3, r, "chip_partial_" + nm) for g3, r, nm in zip(grads, recvs, names)]

    own = jnp.stack([me, my_chip]).astype(jnp.int32)

    def adam(a, g3, sib, chips):
        outs = _adam_big(own, view2d(big_w[a], a), view2d(big_m[a], a), view2d(big_v[a], a),
                         g3, sib, chips, "adam_" + big_names[a])
        return [unview(o, a) for o in outs]

    big_out = [None] * len(big_names)
    dg_act, du_act = _ffn_gate_bwd(dx2b, wd_g, gact, uact)
    gw_gate = _wgrad_rows(dg_act, h2, "wgrad_ffn_gate")
    gw_up = _wgrad_rows(du_act, h2, "wgrad_ffn_up")
    gw_down, sib_gu = _wgrad_rows(act, dx2b, "wgrad_ffn_down", carry=[gw_gate, gw_up])
    ps_gu = partials([gw_gate, gw_up], sib_gu, ["w_ffn_gate", "w_ffn_up"])
    chips_gu = _exchange_chips(ps_gu, "rs_chips_ffn_up", COLLECTIVE_CHIPS)
    dh2, sib_down = _input_grad([(dg_act, wg_g), (du_act, wu_g)], "ffn_in_bwd", after=ps_gu, carry=[gw_down])
    ps_down = partials([gw_down], sib_down, ["w_ffn_down"])
    chips_down = _exchange_chips(ps_down, "rs_chips_ffn_down", COLLECTIVE_CHIPS)
    dx1, dx1b, d_norm2_g = _rms_bwd(dh2, x1, norm2_g, dx2, "rms2_bwd")
    dya, dyb, dproj42, d_b_gate, d_pool_scale = _wo_bwd(dx1b, wo, ya, yb, proj, b_gate2, pool_scale, after=ps_down)
    dproj = dproj42.reshape(N_DEV, s, sw)
    dproj, d_conv_w, d_conv_b = _conv_bwd(dproj, dya, wa_g, proj, conv_w_full, conv_b)
    dproj = _pool_bwd(dproj, dyb, wpool)
    gw_in = _wgrad_cols(h, dproj, "wgrad_in")
    gw_o, sib_in = _wgrad_full(merged, dx1b, "wgrad_o", carry=[gw_in])
    ps_in = partials([gw_in], sib_in, ["w_in"])
    chips_in = _exchange_chips(ps_in, "rs_chips_w_in", COLLECTIVE_CHIPS)
    gw_a = _wgrad_cols(z, dya, "wgrad_a_out", after=ps_in)
    gw_pool = _wgrad_pool(p, dyb, n_groups)
    mix3 = [gw_a,
            gw_pool.reshape(n_groups, gw, N_DEV, go // N_DEV).transpose(2, 0, 1, 3).reshape(N_DEV, n_groups * gw, go // N_DEV),
            gw_o.reshape(N_DEV, d // N_DEV, d)]
    big_out[4] = adam(4, gw_gate, sib_gu[0], chips_gu[0])
    big_out[5] = adam(5, gw_up, sib_gu[1], chips_gu[1])
    big_out[6] = adam(6, gw_down, sib_down[0], chips_down[0])
    dh, sib_mix = _input_grad([(dproj, win_g)], "proj_in_bwd",
                              after=[big_out[4][0], big_out[5][0], big_out[6][0]], carry=mix3)
    ps_mix = partials(mix3, sib_mix, ["w_a_out", "w_pool", "w_o"])
    chips_mix = _exchange_chips(ps_mix, "rs_chips_mixer", COLLECTIVE_CHIPS)
    grad_x, _, d_norm1_g = _rms_bwd(dh, x2d, norm1_g, dx1, "rms1_bwd")
    big_out[0] = adam(0, gw_in, sib_in[0], chips_in[0])
    for k in range(3):
        big_out[1 + k] = adam(1 + k, mix3[k], sib_mix[k], chips_mix[k])

    small_parts = [d_norm1_g, d_b_gate, d_conv_w, d_conv_b, d_pool_scale, d_norm2_g, d_final_g, loss_blk]
    sizes = [v.size for v in small_parts]
    packed = jnp.concatenate([_rows128(v) for v in small_parts], axis=0)
    summed = _sum_small(_allgather_small(packed, "allgather_small_grads")).reshape(-1)
    offs = [0]
    for n in sizes:
        offs.append(offs[-1] + n)
    g_norm1, g_bgate, g_convw_full, g_convb, g_pscale, g_norm2, g_final, loss_sum = [
        summed[offs[k]:offs[k + 1]] for k in range(len(sizes))]
    loss = loss_sum[0]
    g_convw = lax.dynamic_slice(g_convw_full.reshape(CONV_K, sw), (0, me * cws), (CONV_K, cws))
    small_w = [norm1_g, b_gate, conv_w, conv_b, pool_scale, norm2_g, final_g]
    small_m = [m_norm1_g, m_b_gate, m_conv_w, m_conv_b, m_pool_scale, m_norm2_g, m_final_g]
    small_v = [v_norm1_g, v_b_gate, v_conv_w, v_conv_b, v_pool_scale, v_norm2_g, v_final_g]
    small_g = [g_norm1, g_bgate, g_convw, g_convb, g_pscale, g_norm2, g_final]

    def pack(parts):
        flat = jnp.concatenate([v.reshape(-1) for v in parts])
        pad = (-flat.size) % (8 * LANES)
        return jnp.pad(flat, (0, pad)).reshape(-1, LANES)

    s_delta, s_m, s_v = _adam_small(pack(small_w), pack(small_g), pack(small_m), pack(small_v))
    soffs = [0]
    for w in small_w:
        soffs.append(soffs[-1] + w.size)

    def unpack(buf):
        flat = buf.reshape(-1)
        return [flat[soffs[k]:soffs[k + 1]].reshape(small_w[k].shape) for k in range(len(small_w))]

    small_grads = [g.reshape(w.shape) for g, w in zip(small_g, small_w)]
    small_delta, small_new_m, small_new_v = unpack(s_delta), unpack(s_m), unpack(s_v)

    order = ["norm1_g", "w_in", "b_gate", "conv_w", "conv_b", "w_a_out", "w_pool", "pool_scale", "w_o", "norm2_g",
             "w_ffn_gate", "w_ffn_up", "w_ffn_down", "final_g"]
    small_names = ["norm1_g", "b_gate", "conv_w", "conv_b", "pool_scale", "norm2_g", "final_g"]
    per_kind = [{}, {}, {}, {}]
    for a, nm in enumerate(big_names):
        for kind in range(4):
            per_kind[kind][nm] = big_out[a][kind]
    for k, nm in enumerate(small_names):
        per_kind[0][nm] = small_grads[k]
        per_kind[1][nm] = small_delta[k]
        per_kind[2][nm] = small_new_m[k]
        per_kind[3][nm] = small_new_v[k]
    result = [loss, grad_x.reshape(x.shape)]
    for kind in range(4):
        result += [per_kind[kind][nm] for nm in order]
    return tuple(result)
```

```python
import jax
import jax.numpy as jnp
from jax import lax
from jax.experimental import pallas as pl
from jax.experimental.pallas import tpu as pltpu
from jax.experimental.pallas import tpu_sc as plsc

F32 = jnp.float32
BF16 = jnp.bfloat16
MESH = pl.DeviceIdType.MESH

N_DEV = 8
EPS = 1e-6
CONV_K = 3
POOL_WINDOWS = (2, 4, 8, 16)
ADAM_LR = 0.001
ADAM_B1 = 0.9
ADAM_B2 = 0.999
ADAM_EPS = 1e-08
ADAM_WD = 0.01
ADAM_STEP = 10

V7X_VMEM_LIMIT_BYTES = 56 * 1024 * 1024
LANES = 128

COLLECTIVE_GATHER = 1
COLLECTIVE_SIBLING = 2
COLLECTIVE_CHIPS = 3
SEQUENCER_COST_BYTES = 4 * 10**9

NN = ((1,), (0,))
NT = ((1,), (1,))
TN = ((0,), (0,))


def _dot(a, b, dims):
    return lax.dot_general(a, b, (dims, ((), ())), preferred_element_type=F32)


def _cp(n_axes):
    return pltpu.CompilerParams(dimension_semantics=("arbitrary",) * n_axes,
                                vmem_limit_bytes=V7X_VMEM_LIMIT_BYTES)


def _row_tile(rows, bytes_per_row, cap_bytes):
    best = None
    for t in range(16, rows + 1, 16):
        if rows % t == 0 and t * bytes_per_row <= cap_bytes:
            best = t
    return best if best is not None else rows


def _chunks(total, size):
    size = min(size, total)
    assert total % size == 0
    return [slice(r, r + size) for r in range(0, total, size)]


def _after_specs(after):
    return [pl.BlockSpec(memory_space=pl.ANY)] * len(after)


def _shift_down(v, k):
    row = lax.broadcasted_iota(jnp.int32, v.shape, 0)
    return jnp.where(row >= k, pltpu.roll(v, k, 0), 0.0)


def _shift_up(v, k):
    n = v.shape[0]
    row = lax.broadcasted_iota(jnp.int32, v.shape, 0)
    return jnp.where(row < n - k, pltpu.roll(v, n - k, 0), 0.0)


def _sigmoid(v):
    return jax.nn.sigmoid(v)


def _cast_bf16(w2d, name):
    rows, cols = w2d.shape
    tr = _row_tile(rows, cols * 4, 2 << 20)

    def body(i_ref, o_ref):
        o_ref[...] = i_ref[...].astype(BF16)

    return pl.pallas_call(
        body, name=name, grid=(rows // tr,),
        in_specs=[pl.BlockSpec((tr, cols), lambda i: (i, 0))],
        out_specs=pl.BlockSpec((tr, cols), lambda i: (i, 0)),
        out_shape=jax.ShapeDtypeStruct((rows, cols), BF16),
        compiler_params=_cp(1),
    )(w2d)


def _rms_fwd(x2d, g):
    s, d = x2d.shape
    tm = min(256, s)

    def body(x_ref, g_ref, h_ref):
        xv = x_ref[...]
        r = lax.rsqrt(jnp.mean(xv * xv, axis=-1, keepdims=True) + EPS)
        h_ref[...] = (xv * r * g_ref[...]).astype(BF16)

    return pl.pallas_call(
        body, name="rms1_fwd", grid=(s // tm,),
        in_specs=[pl.BlockSpec((tm, d), lambda i: (i, 0)), pl.BlockSpec((1, d), lambda i: (0, 0))],
        out_specs=pl.BlockSpec((tm, d), lambda i: (i, 0)),
        out_shape=jax.ShapeDtypeStruct((s, d), BF16),
        compiler_params=_cp(1),
    )(x2d, g)


def _coords():
    return lax.axis_index("x"), lax.axis_index("y"), lax.axis_index("c")


def _slot(p):
    return 4 * p[0] + 2 * p[1] + p[2]


def _handshake(peers):
    barrier = pltpu.get_barrier_semaphore()
    for peer in peers:
        pl.semaphore_signal(barrier, inc=1, device_id=peer, device_id_type=MESH)
    pl.semaphore_wait(barrier, len(peers))


def _sequencer_call(body, out_type, scratch_types, name, collective_id):
    return pl.kernel(
        body, out_type=out_type, name=name,
        mesh=plsc.ScalarSubcoreMesh(axis_name="seq", num_cores=1),
        scratch_types=scratch_types,
        cost_estimate=pl.CostEstimate(flops=0, transcendentals=0, bytes_accessed=SEQUENCER_COST_BYTES),
        compiler_params=pltpu.CompilerParams(collective_id=collective_id))


def _allgather_big(shards, name, collective_id, after=()):
    n = len(shards)

    def body(*refs):
        ins, outs = refs[:n], refs[n + len(after):2 * n + len(after)]
        send_sems, recv_sems, local_sems = refs[2 * n + len(after):]
        x, y, c = _coords()
        me, sibling = (x, y, c), (x, y, 1 - c)
        x_nbr, y_nbr, diag = (1 - x, y), (x, 1 - y), (1 - x, 1 - y)
        relay_from = (x + (1 - c) * (1 - 2 * x), y + c * (1 - 2 * y))
        relay_to = (x + c * (1 - 2 * x), y + (1 - c) * (1 - 2 * y))
        _handshake([sibling, (*x_nbr, c), (*y_nbr, c)])

        def copy(a, k, block, to, src=None):
            dst = outs[a].at[_slot(block)]
            return pltpu.make_async_remote_copy(
                src_ref=dst if src is None else src, dst_ref=dst,
                send_sem=send_sems.at[a, k], recv_sem=recv_sems.at[a, k],
                device_id=to, device_id_type=MESH)

        mine, sends = [], []
        for a in range(n):
            cp = pltpu.make_async_copy(ins[a], outs[a].at[_slot(me)], local_sems.at[a])
            cp.start()
            mine.append(cp)
            first = [copy(a, 0, me, sibling, src=ins[a]),
                     copy(a, 1, me, (*x_nbr, c), src=ins[a]),
                     copy(a, 2, me, (*y_nbr, c), src=ins[a])]
            for cp in first:
                cp.start()
            sends += first
        for a in range(n):
            copy(a, 1 + c, (*relay_from, c), me).wait_recv()
            passed = [copy(a, 3, (*relay_from, c), (*relay_to, c)), copy(a, 4 + c, (*relay_from, c), sibling)]
            for cp in passed:
                cp.start()
            copy(a, 2 - c, (*relay_to, c), me).wait_recv()
            cp = copy(a, 5 - c, (*relay_to, c), sibling)
            cp.start()
            passed.append(cp)
            copy(a, 3, (*diag, c), me).wait_recv()
            cp = copy(a, 6, (*diag, c), sibling)
            cp.start()
            sends += passed + [cp]
        for a in range(n):
            copy(a, 0, sibling, me).wait_recv()
            copy(a, 4, (*x_nbr, 1 - c), me).wait_recv()
            copy(a, 5, (*y_nbr, 1 - c), me).wait_recv()
            copy(a, 6, (*diag, 1 - c), me).wait_recv()
        for cp in sends:
            cp.wait_send()
        for cp in mine:
            cp.wait()

    return _sequencer_call(
        body, [jax.ShapeDtypeStruct((N_DEV,) + s.shape, s.dtype) for s in shards],
        [pltpu.SemaphoreType.DMA((n, 7)), pltpu.SemaphoreType.DMA((n, 7)), pltpu.SemaphoreType.DMA((n,))],
        name, collective_id)(*shards, *after)


def _sibling_copies(ins, recvs, send_sems, recv_sems):
    x, y, c = _coords()
    return [pltpu.make_async_remote_copy(
        src_ref=ins[a].at[2 * q + (1 - c)], dst_ref=recvs[a].at[q],
        send_sem=send_sems.at[a, q], recv_sem=recv_sems.at[a, q],
        device_id=(x, y, 1 - c), device_id_type=MESH) for a in range(len(ins)) for q in range(4)]


def _carry_specs(carry):
    any_spec = pl.BlockSpec(memory_space=pl.ANY)
    n = len(carry)
    sems = [pltpu.SemaphoreType.DMA((n, 4)), pltpu.SemaphoreType.DMA((n, 4))] if n else []
    return ([any_spec] * n, [any_spec] * n,
            [jax.ShapeDtypeStruct((4,) + g.shape[1:], g.dtype) for g in carry], sems)


def _carry_run(first, last, ins, recvs, sems):
    if not ins:
        return

    @pl.when(first)
    def _():
        x, y, c = _coords()
        _handshake([(x, y, 1 - c)])
        for cp in _sibling_copies(ins, recvs, *sems):
            cp.start()

    @pl.when(last)
    def _():
        copies = _sibling_copies(ins, recvs, *sems)
        for cp in copies:
            cp.wait_recv()
        for cp in copies:
            cp.wait_send()


def _cp_carry(n_axes, carry):
    if not carry:
        return _cp(n_axes)
    return pltpu.CompilerParams(dimension_semantics=("arbitrary",) * n_axes, vmem_limit_bytes=V7X_VMEM_LIMIT_BYTES,
                                collective_id=COLLECTIVE_SIBLING)


def _exchange_chips(psums, name, collective_id):
    n = len(psums)

    def body(*refs):
        ins, outs = refs[:n], refs[n:2 * n]
        send_sems, recv_sems = refs[2 * n:]
        x, y, c = _coords()
        chips = [(1 - x, y), (x, 1 - y), (1 - x, 1 - y)]
        _handshake([(*chip, c) for chip in chips])
        copies = []
        for a in range(n):
            for j, chip in enumerate(chips):
                cp = pltpu.make_async_remote_copy(
                    src_ref=ins[a].at[2 * chip[0] + chip[1]], dst_ref=outs[a].at[j],
                    send_sem=send_sems.at[a, j], recv_sem=recv_sems.at[a, j],
                    device_id=(*chip, c), device_id_type=MESH)
                cp.start()
                copies.append(cp)
        for cp in copies:
            cp.wait_recv()
        for cp in copies:
            cp.wait_send()

    return _sequencer_call(
        body, [jax.ShapeDtypeStruct((3,) + p.shape[1:], p.dtype) for p in psums],
        [pltpu.SemaphoreType.DMA((n, 3)), pltpu.SemaphoreType.DMA((n, 3))],
        name, collective_id)(*psums)


def _allgather_small(v2d, name):
    rows, cols = v2d.shape

    def body(v_ref, out_ref, send_sems, recv_sems):
        x, y, c = _coords()
        me = (x, y, c)
        out_ref[_slot(me)] = v_ref[...]
        peers = []
        for k in range(1, N_DEV):
            fx, fy, fc = (k >> 2) & 1, (k >> 1) & 1, k & 1
            peers.append(((1 - x) if fx else x, (1 - y) if fy else y, (1 - c) if fc else c))
        sends = []
        for k, peer in enumerate(peers):
            cp = pltpu.make_async_remote_copy(
                src_ref=v_ref, dst_ref=out_ref.at[_slot(me)],
                send_sem=send_sems.at[k], recv_sem=recv_sems.at[k],
                device_id=peer, device_id_type=MESH)
            cp.start()
            sends.append(cp)
        for k, peer in enumerate(peers):
            pltpu.make_async_remote_copy(
                src_ref=v_ref, dst_ref=out_ref.at[_slot(peer)],
                send_sem=send_sems.at[k], recv_sem=recv_sems.at[k],
                device_id=peer, device_id_type=MESH).wait_recv()
        for cp in sends:
            cp.wait_send()

    vmem = pl.BlockSpec(memory_space=pltpu.VMEM)
    return pl.pallas_call(
        body, name=name, in_specs=[vmem], out_specs=vmem,
        out_shape=jax.ShapeDtypeStruct((N_DEV, rows, cols), v2d.dtype),
        scratch_shapes=[pltpu.SemaphoreType.DMA((N_DEV - 1,)), pltpu.SemaphoreType.DMA((N_DEV - 1,))],
    )(v2d)


def _chip_partial(others, g3, recv, name):
    _, rows, cols = g3.shape
    tr = _row_tile(rows, cols * 2, 2 << 20)

    def body(others_ref, g_ref, r_ref, o_ref):
        o_ref[...] = (g_ref[...].astype(F32) + r_ref[...].astype(F32)).astype(BF16)

    return pl.pallas_call(
        body, name=name,
        grid_spec=pltpu.PrefetchScalarGridSpec(
            num_scalar_prefetch=1, grid=(3, rows // tr),
            in_specs=[pl.BlockSpec((None, tr, cols), lambda k, i, o: (o[3 + k], i, 0)),
                      pl.BlockSpec((None, tr, cols), lambda k, i, o: (o[k], i, 0))],
            out_specs=pl.BlockSpec((None, tr, cols), lambda k, i, o: (o[k], i, 0))),
        out_shape=jax.ShapeDtypeStruct((4, rows, cols), BF16),
        compiler_params=_cp(2),
    )(others, g3, recv)


def _adam_math(w, g, m, v):
    m = ADAM_B1 * m + (1.0 - ADAM_B1) * g
    v = ADAM_B2 * v + (1.0 - ADAM_B2) * (g * g)
    m_hat = m / (1.0 - ADAM_B1 ** ADAM_STEP)
    v_hat = v / (1.0 - ADAM_B2 ** ADAM_STEP)
    delta = -ADAM_LR * (m_hat / (jnp.sqrt(v_hat) + ADAM_EPS) + ADAM_WD * w)
    return delta, m, v


def _adam_big(own, w, m, v, g3, recv_sib, recv_chips, name):
    rows, cols = w.shape
    tr = _row_tile(rows, cols * 4, 2 << 20)

    def body(own_ref, w_ref, m_ref, v_ref, g_ref, rs_ref, rc_ref, go_ref, do_ref, mo_ref, vo_ref):
        g = g_ref[...].astype(F32) + rs_ref[...].astype(F32)
        g = g + rc_ref[0].astype(F32)
        g = g + rc_ref[1].astype(F32)
        g = g + rc_ref[2].astype(F32)
        delta, m_new, v_new = _adam_math(w_ref[...], g, m_ref[...], v_ref[...])
        go_ref[...] = g
        do_ref[...] = delta
        mo_ref[...] = m_new
        vo_ref[...] = v_new

    blk = pl.BlockSpec((tr, cols), lambda i, o: (i, 0))
    out = jax.ShapeDtypeStruct((rows, cols), F32)
    return pl.pallas_call(
        body, name=name,
        grid_spec=pltpu.PrefetchScalarGridSpec(
            num_scalar_prefetch=1, grid=(rows // tr,),
            in_specs=[blk, blk, blk,
                      pl.BlockSpec((None, tr, cols), lambda i, o: (o[0], i, 0)),
                      pl.BlockSpec((None, tr, cols), lambda i, o: (o[1], i, 0)),
                      pl.BlockSpec((3, tr, cols), lambda i, o: (0, i, 0))],
            out_specs=[blk, blk, blk, blk]),
        out_shape=[out, out, out, out],
        compiler_params=_cp(1),
    )(own, w, m, v, g3, recv_sib, recv_chips)


def _sum_small(gathered):
    _, rows, cols = gathered.shape

    def body(g_ref, o_ref):
        acc = g_ref[0]
        for k in range(1, N_DEV):
            acc = acc + g_ref[k]
        o_ref[...] = acc

    vmem = pl.BlockSpec(memory_space=pltpu.VMEM)
    return pl.pallas_call(body, name="small_grad_sum", in_specs=[vmem], out_specs=vmem,
                          out_shape=jax.ShapeDtypeStruct((rows, cols), F32))(gathered)


def _adam_small(w, g, m, v):
    def body(w_ref, g_ref, m_ref, v_ref, do_ref, mo_ref, vo_ref):
        delta, m_new, v_new = _adam_math(w_ref[...], g_ref[...], m_ref[...], v_ref[...])
        do_ref[...] = delta
        mo_ref[...] = m_new
        vo_ref[...] = v_new

    vmem = pl.BlockSpec(memory_space=pltpu.VMEM)
    out = jax.ShapeDtypeStruct(w.shape, F32)
    return pl.pallas_call(body, name="adam_small", in_specs=[vmem] * 4, out_specs=[vmem] * 3,
                          out_shape=[out, out, out])(w, g, m, v)


def _proj_fwd(h, win_g):
    s, d = h.shape
    sw = win_g.shape[2]
    tn = min(512, sw)
    nh = sw // tn

    def body(h_ref, w_ref, o_ref):
        for rs in _chunks(s, 512):
            o_ref[rs, :] = _dot(h_ref[rs, :], w_ref[...], NN)

    return pl.pallas_call(
        body, name="proj_fwd", grid=(N_DEV * nh,),
        in_specs=[pl.BlockSpec((s, d), lambda j: (0, 0)),
                  pl.BlockSpec((None, d, tn), lambda j: (j // nh, 0, j % nh))],
        out_specs=pl.BlockSpec((None, s, tn), lambda j: (j // nh, 0, j % nh)),
        out_shape=jax.ShapeDtypeStruct((N_DEV, s, sw), F32),
        compiler_params=_cp(1),
    )(h, win_g)


def _conv_fwd(proj, conv_w, conv_b):
    _, s, sw = proj.shape
    tc = min(LANES, sw)

    def body(ba_ref, ca_ref, va_ref, cw_ref, cb_ref, z_ref):
        cv = ca_ref[...] * va_ref[...]
        u = (cb_ref[...] + cw_ref[0:1, :] * _shift_down(cv, 2) + cw_ref[1:2, :] * _shift_down(cv, 1)
             + cw_ref[2:3, :] * cv)
        z_ref[...] = (ba_ref[...] * u).astype(BF16)

    def part(k):
        return pl.BlockSpec((None, s, tc), lambda i: (k, 0, i))

    return pl.pallas_call(
        body, name="conv_fwd", grid=(sw // tc,),
        in_specs=[part(0), part(1), part(2),
                  pl.BlockSpec((CONV_K, tc), lambda i: (0, i)), pl.BlockSpec((1, tc), lambda i: (0, i))],
        out_specs=pl.BlockSpec((s, tc), lambda i: (0, i)),
        out_shape=jax.ShapeDtypeStruct((s, sw), BF16),
        compiler_params=_cp(1),
    )(proj, proj, proj, conv_w, conv_b)


def _pool_counts(shape, window):
    t = lax.broadcasted_iota(jnp.int32, shape, 0)
    return jnp.minimum(t + 1, window).astype(F32)


def _pool_fwd(proj):
    _, s, sw = proj.shape
    gw = sw // len(POOL_WINDOWS)

    def body(v_ref, p_ref):
        for gi, window in enumerate(POOL_WINDOWS):
            @pl.when(pl.program_id(0) == gi)
            def _():
                v = v_ref[...]
                acc, k = v, 1
                while k < window:
                    acc = acc + _shift_down(acc, k)
                    k *= 2
                p_ref[...] = (acc / _pool_counts(v.shape, window) - v).astype(BF16)

    return pl.pallas_call(
        body, name="pool_fwd", grid=(len(POOL_WINDOWS),),
        in_specs=[pl.BlockSpec((None, s, gw), lambda g: (3, 0, g))],
        out_specs=pl.BlockSpec((s, gw), lambda g: (0, g)),
        out_shape=jax.ShapeDtypeStruct((s, sw), BF16),
        compiler_params=_cp(1),
    )(proj)


def _merge_fwd(z, wa, p, wpool, proj, b_gate2, pool_scale):
    s, sw = z.shape
    tn = wa.shape[2]
    d = tn * N_DEV
    gw = sw // len(POOL_WINDOWS)
    nq = sw // tn

    def body(z_ref, wa_ref, p_ref, wp_ref, ga_ref, gb_ref, bg_ref, sc_ref,
             m_ref, dya_ref, dyb_ref, dga_ref, dgb_ref, dsc_ref):
        for rs in _chunks(s, 512):
            ya = _dot(z_ref[rs, :], wa_ref[...], NN)
            yb = _dot(p_ref[rs, :], wp_ref[...], NN)
            sa = _sigmoid(ga_ref[rs, :] + bg_ref[0:1, :])
            sb = _sigmoid(gb_ref[rs, :] + bg_ref[1:2, :])
            sc = sc_ref[...]
            sb_yb = sb * yb
            m_ref[rs, :] = (sa * ya + sb_yb * sc).astype(BF16)
            dya_ref[rs, :] = sa.astype(BF16)
            dyb_ref[rs, :] = (sb * sc).astype(BF16)
            dga_ref[rs, :] = (ya * (sa * (1.0 - sa))).astype(BF16)
            dgb_ref[rs, :] = ((yb * sc) * (sb * (1.0 - sb))).astype(BF16)
            dsc_ref[rs, :] = sb_yb.astype(BF16)

    col = pl.BlockSpec((s, tn), lambda j: (0, j))
    out = jax.ShapeDtypeStruct((s, d), BF16)
    return pl.pallas_call(
        body, name="merge_fwd", grid=(N_DEV,),
        in_specs=[pl.BlockSpec((s, sw), lambda j: (0, 0)),
                  pl.BlockSpec((None, sw, tn), lambda j: (j, 0, 0)),
                  pl.BlockSpec((s, gw), lambda j: (0, j // 2)),
                  pl.BlockSpec((None, gw, tn), lambda j: (j // 2, 0, j % 2)),
                  pl.BlockSpec((None, s, tn), lambda j: (4 + j // nq, 0, j % nq)),
                  pl.BlockSpec((None, s, tn), lambda j: (6 + j // nq, 0, j % nq)),
                  pl.BlockSpec((2, tn), lambda j: (0, j)),
                  pl.BlockSpec((1, tn), lambda j: (0, j))],
        out_specs=[col] * 6,
        out_shape=[out] * 6,
        compiler_params=_cp(1),
    )(z, wa, p, wpool, proj, proj, b_gate2, pool_scale)


def _wo_fwd(merged, wo, x2d, g2):
    s, d = x2d.shape
    tm = min(256, s)

    def body(m_ref, wo_ref, x_ref, g_ref, x1_ref, h2_ref):
        x1 = x_ref[...] + _dot(m_ref[...], wo_ref[...], NN)
        x1_ref[...] = x1
        r = lax.rsqrt(jnp.mean(x1 * x1, axis=-1, keepdims=True) + EPS)
        h2_ref[...] = (x1 * r * g_ref[...]).astype(BF16)

    row = pl.BlockSpec((tm, d), lambda i: (i, 0))
    return pl.pallas_call(
        body, name="wo_fwd", grid=(s // tm,),
        in_specs=[row, pl.BlockSpec((d, d), lambda i: (0, 0)), row, pl.BlockSpec((1, d), lambda i: (0, 0))],
        out_specs=[row, row],
        out_shape=[jax.ShapeDtypeStruct((s, d), F32), jax.ShapeDtypeStruct((s, d), BF16)],
        compiler_params=_cp(1),
    )(merged, wo, x2d, g2)


def _ffn_up_act_fwd(h2, wg_g, wu_g):
    s, d = h2.shape
    f8 = wg_g.shape[2]
    th = min(1024, s)

    def body(h_ref, wg_ref, wu_ref, dadu_ref, dadg_ref, a_ref):
        i = pl.program_id(1)
        for rs in _chunks(th, 512):
            rows = pl.ds(pl.multiple_of(i * th + rs.start, rs.stop - rs.start), rs.stop - rs.start)
            a = h_ref[rows, :]
            g = _dot(a, wg_ref[...], NN)
            u = _dot(a, wu_ref[...], NN)
            sg = _sigmoid(g)
            silu = g * sg
            dadu_ref[rs, :] = silu.astype(BF16)
            dadg_ref[rs, :] = (u * (sg * (1.0 + g * (1.0 - sg)))).astype(BF16)
            a_ref[rs, :] = (silu * u).astype(BF16)

    wspec = pl.BlockSpec((None, d, f8), lambda j, i: (j, 0, 0))
    ospec = pl.BlockSpec((None, th, f8), lambda j, i: (j, i, 0))
    out = jax.ShapeDtypeStruct((N_DEV, s, f8), BF16)
    return pl.pallas_call(
        body, name="ffn_up_fwd", grid=(N_DEV, s // th),
        in_specs=[pl.BlockSpec((s, d), lambda j, i: (0, 0)), wspec, wspec],
        out_specs=[ospec, ospec, ospec], out_shape=[out, out, out],
        compiler_params=_cp(2),
    )(h2, wg_g, wu_g)


def _ffn_down_fwd(act, wd_g):
    _, s, f8 = act.shape
    d = wd_g.shape[2]
    tn = min(1024, d)

    def body(a_ref, wd_ref, o_ref):
        j = pl.program_id(1)

        @pl.when(j == 0)
        def _():
            o_ref[...] = jnp.zeros_like(o_ref)

        for rs in _chunks(s, 1024):
            o_ref[rs, :] += _dot(a_ref[rs, :], wd_ref[...], NN)

    return pl.pallas_call(
        body, name="ffn_down_fwd", grid=(d // tn, N_DEV),
        in_specs=[pl.BlockSpec((None, s, f8), lambda n, j: (j, 0, 0)),
                  pl.BlockSpec((None, f8, tn), lambda n, j: (j, 0, n))],
        out_specs=pl.BlockSpec((s, tn), lambda n, j: (0, n)),
        out_shape=jax.ShapeDtypeStruct((s, d), F32),
        compiler_params=_cp(2),
    )(act, wd_g)


def _loss_bwd(ffn_out, x1, target, final_g):
    s, d = x1.shape
    tm = min(256, s)

    def body(f_ref, x1_ref, t_ref, gf_ref, dx_ref, dxb_ref, dgf_ref, loss_ref):
        @pl.when(pl.program_id(0) == 0)
        def _():
            dgf_ref[...] = jnp.zeros_like(dgf_ref)
            loss_ref[...] = jnp.zeros_like(loss_ref)

        x2 = x1_ref[...] + f_ref[...]
        r = lax.rsqrt(jnp.mean(x2 * x2, axis=-1, keepdims=True) + EPS)
        nrm = x2 * r
        gf = gf_ref[...]
        err = nrm * gf - t_ref[...]
        loss_ref[...] += jnp.sum(err * err) * (0.5 / d)
        dy = err * (1.0 / d)
        dgf_ref[...] += jnp.sum(dy * nrm, axis=0, keepdims=True)
        dn = dy * gf
        dx = r * (dn - nrm * jnp.mean(dn * nrm, axis=-1, keepdims=True))
        dx_ref[...] = dx
        dxb_ref[...] = dx.astype(BF16)

    row = pl.BlockSpec((tm, d), lambda i: (i, 0))
    vec = pl.BlockSpec((1, d), lambda i: (0, 0))
    return pl.pallas_call(
        body, name="loss_bwd", grid=(s // tm,),
        in_specs=[row, row, row, vec],
        out_specs=[row, row, vec, pl.BlockSpec((8, LANES), lambda i: (0, 0))],
        out_shape=[jax.ShapeDtypeStruct((s, d), F32), jax.ShapeDtypeStruct((s, d), BF16),
                   jax.ShapeDtypeStruct((1, d), F32), jax.ShapeDtypeStruct((8, LANES), F32)],
        compiler_params=_cp(1),
    )(ffn_out, x1, target, final_g)


def _ffn_gate_bwd(dx2b, wd_g, dadg, dadu):
    s, d = dx2b.shape
    f8 = dadg.shape[2]
    th = min(1024, s)

    def body(dx_ref, wd_ref, g_ref, u_ref, dg_ref, du_ref, da_ref):
        i = pl.program_id(1)
        chunks = _chunks(th, 256)

        def matmul(rs):
            rows = pl.ds(pl.multiple_of(i * th + rs.start, rs.stop - rs.start), rs.stop - rs.start)
            da_ref[rs, :] = _dot(dx_ref[rows, :], wd_ref[...], NT)

        matmul(chunks[0])
        for k, rs in enumerate(chunks):
            if k + 1 < len(chunks):
                matmul(chunks[k + 1])
            da = da_ref[rs, :]
            dg_ref[rs, :] = (da * g_ref[rs, :].astype(F32)).astype(BF16)
            du_ref[rs, :] = (da * u_ref[rs, :].astype(F32)).astype(BF16)

    aspec = pl.BlockSpec((None, th, f8), lambda j, i: (j, i, 0))
    out = jax.ShapeDtypeStruct((N_DEV, s, f8), BF16)
    return pl.pallas_call(
        body, name="ffn_act_bwd", grid=(N_DEV, s // th),
        in_specs=[pl.BlockSpec((s, d), lambda j, i: (0, 0)),
                  pl.BlockSpec((None, f8, d), lambda j, i: (j, 0, 0)), aspec, aspec],
        out_specs=[aspec, aspec], out_shape=[out, out],
        scratch_shapes=[pltpu.VMEM((th, f8), F32)],
        compiler_params=_cp(2),
    )(dx2b, wd_g, dadg, dadu)


def _wgrad_rows(a3, b, name, after=(), carry=()):
    _, s, k = a3.shape
    n = b.shape[1]
    nc = len(carry)
    c_in, c_out, c_shape, c_sems = _carry_specs(carry)

    def body(a_ref, b_ref, *rest):
        rest = rest[len(after):]
        o_ref = rest[nc]
        j = pl.program_id(0)
        _carry_run(j == 0, j == N_DEV - 1, rest[:nc], rest[nc + 1:2 * nc + 1], rest[2 * nc + 1:])
        o_ref[...] = _dot(a_ref[...], b_ref[...], TN).astype(BF16)

    outs = pl.pallas_call(
        body, name=name, grid=(N_DEV,),
        in_specs=[pl.BlockSpec((None, s, k), lambda j: (j, 0, 0)),
                  pl.BlockSpec((s, n), lambda j: (0, 0))] + _after_specs(after) + c_in,
        out_specs=[pl.BlockSpec((None, k, n), lambda j: (j, 0, 0))] + c_out,
        out_shape=[jax.ShapeDtypeStruct((N_DEV, k, n), BF16)] + c_shape,
        scratch_shapes=c_sems,
        compiler_params=_cp_carry(1, carry),
    )(a3, b, *after, *carry)
    return (outs[0], list(outs[1:])) if nc else outs[0]


def _wgrad_cols(a, b3, name, after=()):
    s, k = a.shape
    if b3.ndim == 2:
        n = b3.shape[1] // N_DEV
        b_spec = pl.BlockSpec((s, n), lambda j: (0, j))
    else:
        n = b3.shape[2]
        b_spec = pl.BlockSpec((None, s, n), lambda j: (j, 0, 0))

    def body(a_ref, b_ref, *rest):
        o_ref = rest[len(after)]
        o_ref[...] = _dot(a_ref[...], b_ref[...], TN).astype(BF16)

    return pl.pallas_call(
        body, name=name, grid=(N_DEV,),
        in_specs=[pl.BlockSpec((s, k), lambda j: (0, 0)), b_spec] + _after_specs(after),
        out_specs=pl.BlockSpec((None, k, n), lambda j: (j, 0, 0)),
        out_shape=jax.ShapeDtypeStruct((N_DEV, k, n), BF16),
        compiler_params=_cp(1),
    )(a, b3, *after)


def _input_grad(pairs, name, after=(), carry=()):
    s = pairs[0][0].shape[1]
    d = pairs[0][1].shape[1]
    tn = min(1024, d)
    npair = len(pairs)
    nc = len(carry)
    c_in, c_out, c_shape, c_sems = _carry_specs(carry)

    def body(*refs):
        ops = refs[:2 * npair]
        rest = refs[2 * npair + len(after):]
        o_ref = rest[nc]
        nh, j = pl.program_id(0), pl.program_id(1)
        _carry_run((nh == 0) & (j == 0), (nh == d // tn - 1) & (j == N_DEV - 1),
                   rest[:nc], rest[nc + 1:2 * nc + 1], rest[2 * nc + 1:])

        @pl.when(j == 0)
        def _():
            o_ref[...] = jnp.zeros_like(o_ref)

        for rs in _chunks(s, 1024):
            part = _dot(ops[0][rs, :], ops[1][...], NT)
            for q in range(1, npair):
                part = part + _dot(ops[2 * q][rs, :], ops[2 * q + 1][...], NT)
            o_ref[rs, :] += part

    in_specs, args = [], []
    for a3, w3 in pairs:
        k = a3.shape[2]
        in_specs += [pl.BlockSpec((None, s, k), lambda n, j: (j, 0, 0)),
                     pl.BlockSpec((None, tn, k), lambda n, j: (j, n, 0))]
        args += [a3, w3]
    outs = pl.pallas_call(
        body, name=name, grid=(d // tn, N_DEV),
        in_specs=in_specs + _after_specs(after) + c_in,
        out_specs=[pl.BlockSpec((s, tn), lambda n, j: (0, n))] + c_out,
        out_shape=[jax.ShapeDtypeStruct((s, d), F32)] + c_shape,
        scratch_shapes=c_sems,
        compiler_params=_cp_carry(2, carry),
    )(*args, *after, *carry)
    return (outs[0], list(outs[1:])) if nc else outs[0]


def _rms_bwd(dh, xres, g, dres, name):
    s, d = xres.shape
    tm = min(256, s)

    def body(dh_ref, x_ref, g_ref, dres_ref, dx_ref, dxb_ref, dg_ref):
        @pl.when(pl.program_id(0) == 0)
        def _():
            dg_ref[...] = jnp.zeros_like(dg_ref)

        xv = x_ref[...]
        dh_v = dh_ref[...]
        r = lax.rsqrt(jnp.mean(xv * xv, axis=-1, keepdims=True) + EPS)
        nrm = xv * r
        dg_ref[...] += jnp.sum(dh_v * nrm, axis=0, keepdims=True)
        dn = dh_v * g_ref[...]
        dx = dres_ref[...] + r * (dn - nrm * jnp.mean(dn * nrm, axis=-1, keepdims=True))
        dx_ref[...] = dx
        dxb_ref[...] = dx.astype(BF16)

    row = pl.BlockSpec((tm, d), lambda i: (i, 0))
    vec = pl.BlockSpec((1, d), lambda i: (0, 0))
    return pl.pallas_call(
        body, name=name, grid=(s // tm,),
        in_specs=[row, row, vec, row],
        out_specs=[row, row, vec],
        out_shape=[jax.ShapeDtypeStruct((s, d), F32), jax.ShapeDtypeStruct((s, d), BF16),
                   jax.ShapeDtypeStruct((1, d), F32)],
        compiler_params=_cp(1),
    )(dh, xres, g, dres)


def _wgrad_full(a, b, name, after=(), carry=()):
    s, k = a.shape
    n = b.shape[1]
    tk = min(512, k)
    nc = len(carry)
    c_in, c_out, c_shape, c_sems = _carry_specs(carry)

    def body(a_ref, b_ref, *rest):
        rest = rest[len(after):]
        o_ref = rest[nc]
        j = pl.program_id(0)
        _carry_run(j == 0, j == k // tk - 1, rest[:nc], rest[nc + 1:2 * nc + 1], rest[2 * nc + 1:])
        o_ref[...] = _dot(a_ref[...], b_ref[...], TN).astype(BF16)

    outs = pl.pallas_call(
        body, name=name, grid=(k // tk,),
        in_specs=[pl.BlockSpec((s, tk), lambda j: (0, j)),
                  pl.BlockSpec((s, n), lambda j: (0, 0))] + _after_specs(after) + c_in,
        out_specs=[pl.BlockSpec((tk, n), lambda j: (j, 0))] + c_out,
        out_shape=[jax.ShapeDtypeStruct((k, n), BF16)] + c_shape,
        scratch_shapes=c_sems,
        compiler_params=_cp_carry(1, carry),
    )(a, b, *after, *carry)
    return (outs[0], list(outs[1:])) if nc else outs[0]


def _wgrad_pool(p, dyb, n_groups):
    s, sw = p.shape
    d = dyb.shape[1]
    gw, go = sw // n_groups, d // n_groups
    ts = min(512, s)
    ns = s // ts

    def body(a_ref, b_ref, o_ref, acc_ref):
        i = pl.program_id(1)

        @pl.when(i == 0)
        def _():
            acc_ref[...] = jnp.zeros_like(acc_ref)

        acc_ref[...] += _dot(a_ref[...], b_ref[...], TN)

        @pl.when(i == ns - 1)
        def _():
            o_ref[...] = acc_ref[...].astype(BF16)

    return pl.pallas_call(
        body, name="wgrad_pool", grid=(n_groups, ns),
        in_specs=[pl.BlockSpec((ts, gw), lambda g, i: (i, g)),
                  pl.BlockSpec((ts, go), lambda g, i: (i, g))],
        out_specs=pl.BlockSpec((None, gw, go), lambda g, i: (g, 0, 0)),
        out_shape=jax.ShapeDtypeStruct((n_groups, gw, go), BF16),
        scratch_shapes=[pltpu.VMEM((gw, go), F32)],
        compiler_params=_cp(2),
    )(p, dyb)


def _wo_bwd(dx1b, wo, factors, sw, after=()):
    s, d = dx1b.shape
    tn = d // N_DEV
    nq = sw // tn

    def body(dx_ref, wo_ref, fya_ref, fyb_ref, fga_ref, fgb_ref, fsc_ref, *rest):
        dya_ref, dyb_ref, dp_ref, dbg_ref, dsc_ref, dm_ref = rest[len(after):]
        dbg_ref[...] = jnp.zeros_like(dbg_ref)
        dsc_ref[...] = jnp.zeros_like(dsc_ref)
        for rs in _chunks(s, 1024):
            dm_ref[rs, :] = _dot(dx_ref[rs, :], wo_ref[...], NT)
        for rs in _chunks(s, 256):
            dm = dm_ref[rs, :]
            dya_ref[rs, :] = (dm * fya_ref[rs, :].astype(F32)).astype(BF16)
            dyb_ref[rs, :] = (dm * fyb_ref[rs, :].astype(F32)).astype(BF16)
            dsc_ref[...] += jnp.sum(dm * fsc_ref[rs, :].astype(F32), axis=0, keepdims=True)
            dga = dm * fga_ref[rs, :].astype(F32)
            dgb = dm * fgb_ref[rs, :].astype(F32)
            dp_ref[0, rs, :] = dga.astype(BF16)
            dp_ref[1, rs, :] = dgb.astype(BF16)
            dbg_ref[0:1, :] += jnp.sum(dga, axis=0, keepdims=True)
            dbg_ref[1:2, :] += jnp.sum(dgb, axis=0, keepdims=True)

    col = pl.BlockSpec((s, tn), lambda j: (0, j))
    out = jax.ShapeDtypeStruct((s, d), BF16)
    return pl.pallas_call(
        body, name="wo_bwd", grid=(N_DEV,),
        in_specs=[pl.BlockSpec((s, d), lambda j: (0, 0)),
                  pl.BlockSpec((tn, d), lambda j: (j, 0))] + [col] * 5 + _after_specs(after),
        out_specs=[col, col,
                   pl.BlockSpec((2, None, s, tn), lambda j: (1, j // nq, 0, j % nq)),
                   pl.BlockSpec((2, tn), lambda j: (0, j)),
                   pl.BlockSpec((1, tn), lambda j: (0, j))],
        out_shape=[out, out, jax.ShapeDtypeStruct((4, 2, s, sw), BF16),
                   jax.ShapeDtypeStruct((2, d), F32), jax.ShapeDtypeStruct((1, d), F32)],
        scratch_shapes=[pltpu.VMEM((s, tn), F32)],
        compiler_params=_cp(1),
    )(dx1b, wo, *factors, *after)


def _conv_bwd(dproj, dya, wa, proj, conv_w, conv_b):
    s, d = dya.shape
    sw, tn = wa.shape[1], wa.shape[2]
    tc = min(LANES, sw)

    def body(dproj_hbm, dya_ref, wa_ref, ba_ref, ca_ref, va_ref, cw_ref, cb_ref,
             dp_ref, dcw_ref, dcb_ref, dz_ref):
        del dproj_hbm
        for rs in _chunks(s, 512):
            part = _dot(dya_ref[rs, 0:tn], wa_ref[0], NT)
            for j in range(1, N_DEV):
                part = part + _dot(dya_ref[rs, j * tn:(j + 1) * tn], wa_ref[j], NT)
            dz_ref[rs, :] = part
        dz = dz_ref[...]
        ba, ca, va = ba_ref[...], ca_ref[...], va_ref[...]
        cv = ca * va
        cv1, cv2 = _shift_down(cv, 1), _shift_down(cv, 2)
        w0, w1, w2 = cw_ref[0:1, :], cw_ref[1:2, :], cw_ref[2:3, :]
        u = cb_ref[...] + w0 * cv2 + w1 * cv1 + w2 * cv
        du = dz * ba
        dp_ref[0] = (dz * u).astype(BF16)
        dcv = w2 * du + w1 * _shift_up(du, 1) + w0 * _shift_up(du, 2)
        dp_ref[1] = (dcv * va).astype(BF16)
        dp_ref[2] = (dcv * ca).astype(BF16)
        dcw_ref[0:1, :] = jnp.sum(du * cv2, axis=0, keepdims=True)
        dcw_ref[1:2, :] = jnp.sum(du * cv1, axis=0, keepdims=True)
        dcw_ref[2:3, :] = jnp.sum(du * cv, axis=0, keepdims=True)
        dcb_ref[...] = jnp.sum(du, axis=0, keepdims=True)

    def part(k):
        return pl.BlockSpec((None, s, tc), lambda i: (k, 0, i))

    return pl.pallas_call(
        body, name="conv_bwd", grid=(sw // tc,),
        in_specs=[pl.BlockSpec(memory_space=pl.ANY),
                  pl.BlockSpec((s, d), lambda i: (0, 0)),
                  pl.BlockSpec((N_DEV, tc, tn), lambda i: (0, i, 0)),
                  part(0), part(1), part(2),
                  pl.BlockSpec((CONV_K, tc), lambda i: (0, i)), pl.BlockSpec((1, tc), lambda i: (0, i))],
        out_specs=[pl.BlockSpec((3, s, tc), lambda i: (0, 0, i)),
                   pl.BlockSpec((CONV_K, tc), lambda i: (0, i)), pl.BlockSpec((1, tc), lambda i: (0, i))],
        out_shape=[jax.ShapeDtypeStruct(dproj.shape, BF16),
                   jax.ShapeDtypeStruct((CONV_K, sw), F32), jax.ShapeDtypeStruct((1, sw), F32)],
        scratch_shapes=[pltpu.VMEM((s, tc), F32)],
        input_output_aliases={0: 0},
        compiler_params=_cp(1),
    )(dproj, dya, wa, proj, proj, proj, conv_w, conv_b)


def _pool_bwd(dproj, dyb, wpool):
    s, d = dyb.shape
    n_groups, gw, go = wpool.shape

    def body(dproj_hbm, dyb_ref, wp_ref, dp_ref):
        del dproj_hbm
        for gi, window in enumerate(POOL_WINDOWS):
            @pl.when(pl.program_id(0) == gi)
            def _():
                dpool = _dot(dyb_ref[...], wp_ref[...], NT)
                acc, k = dpool / _pool_counts(dpool.shape, window), 1
                while k < window:
                    acc = acc + _shift_up(acc, k)
                    k *= 2
                dp_ref[...] = (acc - dpool).astype(BF16)

    return pl.pallas_call(
        body, name="pool_bwd", grid=(n_groups,),
        in_specs=[pl.BlockSpec(memory_space=pl.ANY),
                  pl.BlockSpec((s, go), lambda g: (0, g)),
                  pl.BlockSpec((None, gw, go), lambda g: (g, 0, 0))],
        out_specs=pl.BlockSpec((None, s, gw), lambda g: (3, 0, g)),
        out_shape=jax.ShapeDtypeStruct(dproj.shape, BF16),
        input_output_aliases={0: 0},
        compiler_params=_cp(1),
    )(dproj, dyb, wpool)


def _rows128(v):
    return v.reshape(-1, LANES)


def kernel(x, norm1_g, w_in, b_gate, conv_w, conv_b, w_a_out, w_pool, pool_scale, w_o, norm2_g, w_ffn_gate, w_ffn_up, w_ffn_down, final_g, loss_target, m_norm1_g, m_w_in, m_b_gate, m_conv_w, m_conv_b, m_w_a_out, m_w_pool, m_pool_scale, m_w_o, m_norm2_g, m_w_ffn_gate, m_w_ffn_up, m_w_ffn_down, m_final_g, v_norm1_g, v_w_in, v_b_gate, v_conv_w, v_conv_b, v_w_a_out, v_w_pool, v_pool_scale, v_w_o, v_norm2_g, v_w_ffn_gate, v_w_ffn_up, v_w_ffn_down, v_final_g):
    s, d = x.shape[1], x.shape[2]
    sw = w_in.shape[2]
    n_groups = w_pool.shape[1]
    gw = w_pool.shape[2]
    go = w_pool.shape[3] * N_DEV
    f8 = w_ffn_gate.shape[2]
    cws = conv_w.shape[2]
    assert sw == conv_w.shape[2] * N_DEV == gw * n_groups and go * n_groups == d and n_groups == len(POOL_WINDOWS)

    xi, yi, ci = _coords()
    me = 4 * xi + 2 * yi + ci
    my_chip = 2 * xi + yi

    x2d = x.reshape(s, d)
    target = loss_target.reshape(s, d)
    final_g2 = final_g.reshape(1, d)
    b_gate2 = b_gate.reshape(2, d)

    big_names = ["w_in", "w_a_out", "w_pool", "w_o", "w_ffn_gate", "w_ffn_up", "w_ffn_down"]
    big_w = [w_in, w_a_out, w_pool, w_o, w_ffn_gate, w_ffn_up, w_ffn_down]
    big_m = [m_w_in, m_w_a_out, m_w_pool, m_w_o, m_w_ffn_gate, m_w_ffn_up, m_w_ffn_down]
    big_v = [v_w_in, v_w_a_out, v_w_pool, v_w_o, v_w_ffn_gate, v_w_ffn_up, v_w_ffn_down]
    shapes2d = [(w.size // w.shape[-1], w.shape[-1]) for w in big_w]
    big_w2 = [w.reshape(sh) for w, sh in zip(big_w, shapes2d)]
    transposed = (4, 5)

    def view2d(t, a):
        t2 = t.reshape(shapes2d[a])
        return t2.T if a in transposed else t2

    def unview(o, a):
        return (o.T if a in transposed else o).reshape(big_w[a].shape)

    sb = [_cast_bf16(w, "cast_" + nm) for w, nm in zip(big_w2, big_names)]
    win_g, wa_g, wpool_g, wo_g = _allgather_big(sb[0:4], "allgather_mixer", COLLECTIVE_GATHER)
    wg_g, wu_g = _allgather_big(sb[4:6], "allgather_ffn_up", COLLECTIVE_GATHER)
    (wd_g,) = _allgather_big(sb[6:7], "allgather_ffn_down", COLLECTIVE_GATHER)
    convw_g = _allgather_small(jnp.pad(conv_w.reshape(CONV_K, cws), ((0, 8 - CONV_K), (0, 0))), "allgather_conv_w")
    conv_w_full = convw_g[:, :CONV_K, :].transpose(1, 0, 2).reshape(CONV_K, sw)
    wpool = wpool_g.reshape(N_DEV, n_groups, gw, go // N_DEV).transpose(1, 2, 0, 3).reshape(n_groups, gw, go)
    wo = wo_g.reshape(d, d)

    h = _rms_fwd(x2d, norm1_g)
    proj = _proj_fwd(h, win_g)
    z = _conv_fwd(proj, conv_w_full, conv_b)
    p = _pool_fwd(proj)
    merged, *merge_factors = _merge_fwd(z, wa_g, p, wpool, proj, b_gate2, pool_scale)
    x1, h2 = _wo_fwd(merged, wo, x2d, norm2_g)
    dadu, dadg, act = _ffn_up_act_fwd(h2, wg_g, wu_g)
    ffn_out = _ffn_down_fwd(act, wd_g)
    dx2, dx2b, d_final_g, loss_blk = _loss_bwd(ffn_out, x1, target, final_g2)

    other_chips = jnp.stack([2 * (1 - xi) + yi, 2 * xi + (1 - yi), 2 * (1 - xi) + (1 - yi)])
    others = jnp.concatenate([other_chips, 2 * other_chips + ci]).astype(jnp.int32)

    def partials(grads, recvs, names):
        return [_chip_partial(others, g3, r, "chip_partial_" + nm) for g3, r, nm in zip(grads, recvs, names)]

    own = jnp.stack([me, my_chip]).astype(jnp.int32)

    def adam(a, g3, sib, chips):
        outs = _adam_big(own, view2d(big_w[a], a), view2d(big_m[a], a), view2d(big_v[a], a),
                         g3, sib, chips, "adam_" + big_names[a])
        return [unview(o, a) for o in outs]

    big_out = [None] * len(big_names)
    dg_act, du_act = _ffn_gate_bwd(dx2b, wd_g, dadg, dadu)
    gw_gate = _wgrad_rows(dg_act, h2, "wgrad_ffn_gate")
    gw_up = _wgrad_rows(du_act, h2, "wgrad_ffn_up")
    gw_down, sib_gu = _wgrad_rows(act, dx2b, "wgrad_ffn_down", carry=[gw_gate, gw_up])
    ps_gu = partials([gw_gate, gw_up], sib_gu, ["w_ffn_gate", "w_ffn_up"])
    chips_gu = _exchange_chips(ps_gu, "rs_chips_ffn_up", COLLECTIVE_CHIPS)
    dh2, sib_down = _input_grad([(dg_act, wg_g), (du_act, wu_g)], "ffn_in_bwd", after=ps_gu, carry=[gw_down])
    ps_down = partials([gw_down], sib_down, ["w_ffn_down"])
    chips_down = _exchange_chips(ps_down, "rs_chips_ffn_down", COLLECTIVE_CHIPS)
    dx1, dx1b, d_norm2_g = _rms_bwd(dh2, x1, norm2_g, dx2, "rms2_bwd")
    dya, dyb, dproj42, d_b_gate, d_pool_scale = _wo_bwd(dx1b, wo, merge_factors, sw, after=ps_down)
    dproj = dproj42.reshape(N_DEV, s, sw)
    dproj, d_conv_w, d_conv_b = _conv_bwd(dproj, dya, wa_g, proj, conv_w_full, conv_b)
    dproj = _pool_bwd(dproj, dyb, wpool)
    gw_in = _wgrad_cols(h, dproj, "wgrad_in")
    gw_o, sib_in = _wgrad_full(merged, dx1b, "wgrad_o", carry=[gw_in])
    ps_in = partials([gw_in], sib_in, ["w_in"])
    chips_in = _exchange_chips(ps_in, "rs_chips_w_in", COLLECTIVE_CHIPS)
    gw_a = _wgrad_cols(z, dya, "wgrad_a_out", after=ps_in)
    gw_pool = _wgrad_pool(p, dyb, n_groups)
    mix3 = [gw_a,
            gw_pool.reshape(n_groups, gw, N_DEV, go // N_DEV).transpose(2, 0, 1, 3).reshape(N_DEV, n_groups * gw, go // N_DEV),
            gw_o.reshape(N_DEV, d // N_DEV, d)]
    big_out[4] = adam(4, gw_gate, sib_gu[0], chips_gu[0])
    big_out[5] = adam(5, gw_up, sib_gu[1], chips_gu[1])
    big_out[6] = adam(6, gw_down, sib_down[0], chips_down[0])
    dh, sib_mix = _input_grad([(dproj, win_g)], "proj_in_bwd",
                              after=[big_out[4][0], big_out[5][0], big_out[6][0]], carry=mix3)
    ps_mix = partials(mix3, sib_mix, ["w_a_out", "w_pool", "w_o"])
    chips_mix = _exchange_chips(ps_mix, "rs_chips_mixer", COLLECTIVE_CHIPS)
    grad_x, _, d_norm1_g = _rms_bwd(dh, x2d, norm1_g, dx1, "rms1_bwd")
    big_out[0] = adam(0, gw_in, sib_in[0], chips_in[0])
    for k in range(3):
        big_out[1 + k] = adam(1 + k, mix3[k], sib_mix[k], chips_mix[k])

    small_parts = [d_norm1_g, d_b_gate, d_conv_w, d_conv_b, d_pool_scale, d_norm2_g, d_final_g, loss_blk]
    sizes = [v.size for v in small_parts]
    packed = jnp.concatenate([_rows128(v) for v in small_parts], axis=0)
    summed = _sum_small(_allgather_small(packed, "allgather_small_grads")).reshape(-1)
    offs = [0]
    for n in sizes:
        offs.append(offs[-1] + n)
    g_norm1, g_bgate, g_convw_full, g_convb, g_pscale, g_norm2, g_final, loss_sum = [
        summed[offs[k]:offs[k + 1]] for k in range(len(sizes))]
    loss = loss_sum[0]
    g_convw = lax.dynamic_slice(g_convw_full.reshape(CONV_K, sw), (0, me * cws), (CONV_K, cws))
    small_w = [norm1_g, b_gate, conv_w, conv_b, pool_scale, norm2_g, final_g]
    small_m = [m_norm1_g, m_b_gate, m_conv_w, m_conv_b, m_pool_scale, m_norm2_g, m_final_g]
    small_v = [v_norm1_g, v_b_gate, v_conv_w, v_conv_b, v_pool_scale, v_norm2_g, v_final_g]
    small_g = [g_norm1, g_bgate, g_convw, g_convb, g_pscale, g_norm2, g_final]

    def pack(parts):
        flat = jnp.concatenate([v.reshape(-1) for v in parts])
        pad = (-flat.size) % (8 * LANES)
        return jnp.pad(flat, (0, pad)).reshape(-1, LANES)

    s_delta, s_m, s_v = _adam_small(pack(small_w), pack(small_g), pack(small_m), pack(small_v))
    soffs = [0]
    for w in small_w:
        soffs.append(soffs[-1] + w.size)

    def unpack(buf):
        flat = buf.reshape(-1)
        return [flat[soffs[k]:soffs[k + 1]].reshape(small_w[k].shape) for k in range(len(small_w))]

    small_grads = [g.reshape(w.shape) for g, w in zip(small_g, small_w)]
    small_delta, small_new_m, small_new_v = unpack(s_delta), unpack(s_m), unpack(s_v)

    order = ["norm1_g", "w_in", "b_gate", "conv_w", "conv_b", "w_a_out", "w_pool", "pool_scale", "w_o", "norm2_g",
             "w_ffn_gate", "w_ffn_up", "w_ffn_down", "final_g"]
    small_names = ["norm1_g", "b_gate", "conv_w", "conv_b", "pool_scale", "norm2_g", "final_g"]
    per_kind = [{}, {}, {}, {}]
    for a, nm in enumerate(big_names):
        for kind in range(4):
            per_kind[kind][nm] = big_out[a][kind]
    for k, nm in enumerate(small_names):
        per_kind[0][nm] = small_grads[k]
        per_kind[1][nm] = small_delta[k]
        per_kind[2][nm] = small_new_m[k]
        per_kind[3][nm] = small_new_v[k]
    result = [loss, grad_x.reshape(x.shape)]
    for kind in range(4):
        result += [per_kind[kind][nm] for nm in order]
    return tuple(result)
```

```python
import jax
import jax.numpy as jnp
from jax import lax
from jax.experimental import pallas as pl
from jax.experimental.pallas import tpu as pltpu
from jax.experimental.pallas import tpu_sc as plsc

F32 = jnp.float32
BF16 = jnp.bfloat16
MESH = pl.DeviceIdType.MESH

N_DEV = 8
EPS = 1e-6
CONV_K = 3
POOL_WINDOWS = (2, 4, 8, 16)
ADAM_LR = 0.001
ADAM_B1 = 0.9
ADAM_B2 = 0.999
ADAM_EPS = 1e-08
ADAM_WD = 0.01
ADAM_STEP = 10

V7X_VMEM_LIMIT_BYTES = 56 * 1024 * 1024
LANES = 128

COLLECTIVE_GATHER = 1
COLLECTIVE_SIBLING = 2
COLLECTIVE_CHIPS = 3
SEQUENCER_COST_BYTES = 4 * 10**9

NN = ((1,), (0,))
NT = ((1,), (1,))
TN = ((0,), (0,))


def _dot(a, b, dims):
    return lax.dot_general(a, b, (dims, ((), ())), preferred_element_type=F32)


def _cp(n_axes):
    return pltpu.CompilerParams(dimension_semantics=("arbitrary",) * n_axes,
                                vmem_limit_bytes=V7X_VMEM_LIMIT_BYTES)


def _row_tile(rows, bytes_per_row, cap_bytes):
    best = None
    for t in range(16, rows + 1, 16):
        if rows % t == 0 and t * bytes_per_row <= cap_bytes:
            best = t
    return best if best is not None else rows


def _chunks(total, size):
    size = min(size, total)
    assert total % size == 0
    return [slice(r, r + size) for r in range(0, total, size)]


def _after_specs(after):
    return [pl.BlockSpec(memory_space=pl.ANY)] * len(after)


def _shift_down(v, k):
    row = lax.broadcasted_iota(jnp.int32, v.shape, 0)
    return jnp.where(row >= k, pltpu.roll(v, k, 0), 0.0)


def _shift_up(v, k):
    n = v.shape[0]
    row = lax.broadcasted_iota(jnp.int32, v.shape, 0)
    return jnp.where(row < n - k, pltpu.roll(v, n - k, 0), 0.0)


def _sigmoid(v):
    return jax.nn.sigmoid(v)


def _cast_bf16(w2d, name):
    rows, cols = w2d.shape
    tr = _row_tile(rows, cols * 4, 2 << 20)

    def body(i_ref, o_ref):
        o_ref[...] = i_ref[...].astype(BF16)

    return pl.pallas_call(
        body, name=name, grid=(rows // tr,),
        in_specs=[pl.BlockSpec((tr, cols), lambda i: (i, 0))],
        out_specs=pl.BlockSpec((tr, cols), lambda i: (i, 0)),
        out_shape=jax.ShapeDtypeStruct((rows, cols), BF16),
        compiler_params=_cp(1),
    )(w2d)


def _rms_fwd(x2d, g):
    s, d = x2d.shape
    tm = min(256, s)

    def body(x_ref, g_ref, h_ref):
        xv = x_ref[...]
        r = lax.rsqrt(jnp.mean(xv * xv, axis=-1, keepdims=True) + EPS)
        h_ref[...] = (xv * r * g_ref[...]).astype(BF16)

    return pl.pallas_call(
        body, name="rms1_fwd", grid=(s // tm,),
        in_specs=[pl.BlockSpec((tm, d), lambda i: (i, 0)), pl.BlockSpec((1, d), lambda i: (0, 0))],
        out_specs=pl.BlockSpec((tm, d), lambda i: (i, 0)),
        out_shape=jax.ShapeDtypeStruct((s, d), BF16),
        compiler_params=_cp(1),
    )(x2d, g)


def _coords():
    return lax.axis_index("x"), lax.axis_index("y"), lax.axis_index("c")


def _slot(p):
    return 4 * p[0] + 2 * p[1] + p[2]


def _handshake(peers):
    barrier = pltpu.get_barrier_semaphore()
    for peer in peers:
        pl.semaphore_signal(barrier, inc=1, device_id=peer, device_id_type=MESH)
    pl.semaphore_wait(barrier, len(peers))


def _sequencer_call(body, out_type, scratch_types, name, collective_id):
    return pl.kernel(
        body, out_type=out_type, name=name,
        mesh=plsc.ScalarSubcoreMesh(axis_name="seq", num_cores=1),
        scratch_types=scratch_types,
        cost_estimate=pl.CostEstimate(flops=0, transcendentals=0, bytes_accessed=SEQUENCER_COST_BYTES),
        compiler_params=pltpu.CompilerParams(collective_id=collective_id))


def _allgather_big(shards, name, collective_id, after=()):
    n = len(shards)

    def body(*refs):
        ins, outs = refs[:n], refs[n + len(after):2 * n + len(after)]
        send_sems, recv_sems, local_sems = refs[2 * n + len(after):]
        x, y, c = _coords()
        me, sibling = (x, y, c), (x, y, 1 - c)
        x_nbr, y_nbr, diag = (1 - x, y), (x, 1 - y), (1 - x, 1 - y)
        relay_from = (x + (1 - c) * (1 - 2 * x), y + c * (1 - 2 * y))
        relay_to = (x + c * (1 - 2 * x), y + (1 - c) * (1 - 2 * y))
        _handshake([sibling, (*x_nbr, c), (*y_nbr, c)])

        def copy(a, k, block, to, src=None):
            dst = outs[a].at[_slot(block)]
            return pltpu.make_async_remote_copy(
                src_ref=dst if src is None else src, dst_ref=dst,
                send_sem=send_sems.at[a, k], recv_sem=recv_sems.at[a, k],
                device_id=to, device_id_type=MESH)

        mine, sends = [], []
        for a in range(n):
            cp = pltpu.make_async_copy(ins[a], outs[a].at[_slot(me)], local_sems.at[a])
            cp.start()
            mine.append(cp)
            first = [copy(a, 0, me, sibling, src=ins[a]),
                     copy(a, 1, me, (*x_nbr, c), src=ins[a]),
                     copy(a, 2, me, (*y_nbr, c), src=ins[a])]
            for cp in first:
                cp.start()
            sends += first
        for a in range(n):
            copy(a, 1 + c, (*relay_from, c), me).wait_recv()
            passed = [copy(a, 3, (*relay_from, c), (*relay_to, c)), copy(a, 4 + c, (*relay_from, c), sibling)]
            for cp in passed:
                cp.start()
            copy(a, 2 - c, (*relay_to, c), me).wait_recv()
            cp = copy(a, 5 - c, (*relay_to, c), sibling)
            cp.start()
            passed.append(cp)
            copy(a, 3, (*diag, c), me).wait_recv()
            cp = copy(a, 6, (*diag, c), sibling)
            cp.start()
            sends += passed + [cp]
        for a in range(n):
            copy(a, 0, sibling, me).wait_recv()
            copy(a, 4, (*x_nbr, 1 - c), me).wait_recv()
            copy(a, 5, (*y_nbr, 1 - c), me).wait_recv()
            copy(a, 6, (*diag, 1 - c), me).wait_recv()
        for cp in sends:
            cp.wait_send()
        for cp in mine:
            cp.wait()

    return _sequencer_call(
        body, [jax.ShapeDtypeStruct((N_DEV,) + s.shape, s.dtype) for s in shards],
        [pltpu.SemaphoreType.DMA((n, 7)), pltpu.SemaphoreType.DMA((n, 7)), pltpu.SemaphoreType.DMA((n,))],
        name, collective_id)(*shards, *after)


def _sibling_copies(ins, recvs, send_sems, recv_sems):
    x, y, c = _coords()
    return [pltpu.make_async_remote_copy(
        src_ref=ins[a].at[2 * q + (1 - c)], dst_ref=recvs[a].at[q],
        send_sem=send_sems.at[a, q], recv_sem=recv_sems.at[a, q],
        device_id=(x, y, 1 - c), device_id_type=MESH) for a in range(len(ins)) for q in range(4)]


def _carry_specs(carry):
    any_spec = pl.BlockSpec(memory_space=pl.ANY)
    n = len(carry)
    sems = [pltpu.SemaphoreType.DMA((n, 4)), pltpu.SemaphoreType.DMA((n, 4))] if n else []
    return ([any_spec] * n, [any_spec] * n,
            [jax.ShapeDtypeStruct((4,) + g.shape[1:], g.dtype) for g in carry], sems)


def _carry_run(first, last, ins, recvs, sems):
    if not ins:
        return

    @pl.when(first)
    def _():
        x, y, c = _coords()
        _handshake([(x, y, 1 - c)])
        for cp in _sibling_copies(ins, recvs, *sems):
            cp.start()

    @pl.when(last)
    def _():
        copies = _sibling_copies(ins, recvs, *sems)
        for cp in copies:
            cp.wait_recv()
        for cp in copies:
            cp.wait_send()


def _cp_carry(n_axes, carry):
    if not carry:
        return _cp(n_axes)
    return pltpu.CompilerParams(dimension_semantics=("arbitrary",) * n_axes, vmem_limit_bytes=V7X_VMEM_LIMIT_BYTES,
                                collective_id=COLLECTIVE_SIBLING)


def _exchange_chips(psums, name, collective_id):
    n = len(psums)

    def body(*refs):
        ins, outs = refs[:n], refs[n:2 * n]
        send_sems, recv_sems = refs[2 * n:]
        x, y, c = _coords()
        chips = [(1 - x, y), (x, 1 - y), (1 - x, 1 - y)]
        _handshake([(*chip, c) for chip in chips])
        copies = []
        for a in range(n):
            for j, chip in enumerate(chips):
                cp = pltpu.make_async_remote_copy(
                    src_ref=ins[a].at[2 * chip[0] + chip[1]], dst_ref=outs[a].at[j],
                    send_sem=send_sems.at[a, j], recv_sem=recv_sems.at[a, j],
                    device_id=(*chip, c), device_id_type=MESH)
                cp.start()
                copies.append(cp)
        for cp in copies:
            cp.wait_recv()
        for cp in copies:
            cp.wait_send()

    return _sequencer_call(
        body, [jax.ShapeDtypeStruct((3,) + p.shape[1:], p.dtype) for p in psums],
        [pltpu.SemaphoreType.DMA((n, 3)), pltpu.SemaphoreType.DMA((n, 3))],
        name, collective_id)(*psums)


def _allgather_small(v2d, name):
    rows, cols = v2d.shape

    def body(v_ref, out_ref, send_sems, recv_sems):
        x, y, c = _coords()
        me = (x, y, c)
        out_ref[_slot(me)] = v_ref[...]
        peers = []
        for k in range(1, N_DEV):
            fx, fy, fc = (k >> 2) & 1, (k >> 1) & 1, k & 1
            peers.append(((1 - x) if fx else x, (1 - y) if fy else y, (1 - c) if fc else c))
        sends = []
        for k, peer in enumerate(peers):
            cp = pltpu.make_async_remote_copy(
                src_ref=v_ref, dst_ref=out_ref.at[_slot(me)],
                send_sem=send_sems.at[k], recv_sem=recv_sems.at[k],
                device_id=peer, device_id_type=MESH)
            cp.start()
            sends.append(cp)
        for k, peer in enumerate(peers):
            pltpu.make_async_remote_copy(
                src_ref=v_ref, dst_ref=out_ref.at[_slot(peer)],
                send_sem=send_sems.at[k], recv_sem=recv_sems.at[k],
                device_id=peer, device_id_type=MESH).wait_recv()
        for cp in sends:
            cp.wait_send()

    vmem = pl.BlockSpec(memory_space=pltpu.VMEM)
    return pl.pallas_call(
        body, name=name, in_specs=[vmem], out_specs=vmem,
        out_shape=jax.ShapeDtypeStruct((N_DEV, rows, cols), v2d.dtype),
        scratch_shapes=[pltpu.SemaphoreType.DMA((N_DEV - 1,)), pltpu.SemaphoreType.DMA((N_DEV - 1,))],
    )(v2d)


def _chip_partial(slots, g3, recv, name):
    _, rows, cols = g3.shape
    tr = _row_tile(rows, cols * 2, 2 << 20)

    def body(slots_ref, g_ref, r_ref, o_ref):
        o_ref[...] = (g_ref[...].astype(F32) + r_ref[...].astype(F32)).astype(BF16)

    return pl.pallas_call(
        body, name=name,
        grid_spec=pltpu.PrefetchScalarGridSpec(
            num_scalar_prefetch=1, grid=(4, rows // tr),
            in_specs=[pl.BlockSpec((None, tr, cols), lambda q, i, sl: (sl[q], i, 0)),
                      pl.BlockSpec((None, tr, cols), lambda q, i, sl: (q, i, 0))],
            out_specs=pl.BlockSpec((None, None, tr, cols), lambda q, i, sl: (q, 0, i, 0))),
        out_shape=jax.ShapeDtypeStruct((4, 1, rows, cols), BF16),
        compiler_params=_cp(2),
    )(slots, g3, recv)


def _adam_math(w, g, m, v):
    m = ADAM_B1 * m + (1.0 - ADAM_B1) * g
    v = ADAM_B2 * v + (1.0 - ADAM_B2) * (g * g)
    m_hat = m / (1.0 - ADAM_B1 ** ADAM_STEP)
    v_hat = v / (1.0 - ADAM_B2 ** ADAM_STEP)
    delta = -ADAM_LR * (m_hat / (jnp.sqrt(v_hat) + ADAM_EPS) + ADAM_WD * w)
    return delta, m, v


def _adam_big(chip, w, m, v, psum, recv_chips, name):
    rows, cols = w.shape
    nh = psum.shape[1]
    cb = cols // nh
    tr = _row_tile(rows, cb * 4, 2 << 20)

    def body(chip_ref, w_ref, m_ref, v_ref, g_ref, rc_ref, go_ref, do_ref, mo_ref, vo_ref):
        g = g_ref[...].astype(F32)
        g = g + rc_ref[0].astype(F32)
        g = g + rc_ref[1].astype(F32)
        g = g + rc_ref[2].astype(F32)
        delta, m_new, v_new = _adam_math(w_ref[...], g, m_ref[...], v_ref[...])
        go_ref[...] = g
        do_ref[...] = delta
        mo_ref[...] = m_new
        vo_ref[...] = v_new

    blk = pl.BlockSpec((tr, cb), lambda hf, i, o: (i, hf))
    out = jax.ShapeDtypeStruct((rows, cols), F32)
    return pl.pallas_call(
        body, name=name,
        grid_spec=pltpu.PrefetchScalarGridSpec(
            num_scalar_prefetch=1, grid=(nh, rows // tr),
            in_specs=[blk, blk, blk,
                      pl.BlockSpec((None, None, tr, cb), lambda hf, i, o: (o[0], hf, i, 0)),
                      pl.BlockSpec((3, None, tr, cb), lambda hf, i, o: (0, hf, i, 0))],
            out_specs=[blk, blk, blk, blk]),
        out_shape=[out, out, out, out],
        compiler_params=_cp(2),
    )(chip, w, m, v, psum, recv_chips)


def _sum_small(gathered):
    _, rows, cols = gathered.shape

    def body(g_ref, o_ref):
        acc = g_ref[0]
        for k in range(1, N_DEV):
            acc = acc + g_ref[k]
        o_ref[...] = acc

    vmem = pl.BlockSpec(memory_space=pltpu.VMEM)
    return pl.pallas_call(body, name="small_grad_sum", in_specs=[vmem], out_specs=vmem,
                          out_shape=jax.ShapeDtypeStruct((rows, cols), F32))(gathered)


def _adam_small(w, g, m, v):
    def body(w_ref, g_ref, m_ref, v_ref, do_ref, mo_ref, vo_ref):
        delta, m_new, v_new = _adam_math(w_ref[...], g_ref[...], m_ref[...], v_ref[...])
        do_ref[...] = delta
        mo_ref[...] = m_new
        vo_ref[...] = v_new

    vmem = pl.BlockSpec(memory_space=pltpu.VMEM)
    out = jax.ShapeDtypeStruct(w.shape, F32)
    return pl.pallas_call(body, name="adam_small", in_specs=[vmem] * 4, out_specs=[vmem] * 3,
                          out_shape=[out, out, out])(w, g, m, v)


def _proj_fwd(h, win_g):
    s, d = h.shape
    sw = win_g.shape[2]
    tn = min(512, sw)
    nh = sw // tn

    def body(h_ref, w_ref, o_ref):
        for rs in _chunks(s, 512):
            o_ref[rs, :] = _dot(h_ref[rs, :], w_ref[...], NN)

    return pl.pallas_call(
        body, name="proj_fwd", grid=(N_DEV * nh,),
        in_specs=[pl.BlockSpec((s, d), lambda j: (0, 0)),
                  pl.BlockSpec((None, d, tn), lambda j: (j // nh, 0, j % nh))],
        out_specs=pl.BlockSpec((None, s, tn), lambda j: (j // nh, 0, j % nh)),
        out_shape=jax.ShapeDtypeStruct((N_DEV, s, sw), F32),
        compiler_params=_cp(1),
    )(h, win_g)


def _conv_fwd(proj, conv_w, conv_b):
    _, s, sw = proj.shape
    tc = min(LANES, sw)

    def body(ba_ref, ca_ref, va_ref, cw_ref, cb_ref, z_ref):
        cv = ca_ref[...] * va_ref[...]
        u = (cb_ref[...] + cw_ref[0:1, :] * _shift_down(cv, 2) + cw_ref[1:2, :] * _shift_down(cv, 1)
             + cw_ref[2:3, :] * cv)
        z_ref[...] = (ba_ref[...] * u).astype(BF16)

    def part(k):
        return pl.BlockSpec((None, s, tc), lambda i: (k, 0, i))

    return pl.pallas_call(
        body, name="conv_fwd", grid=(sw // tc,),
        in_specs=[part(0), part(1), part(2),
                  pl.BlockSpec((CONV_K, tc), lambda i: (0, i)), pl.BlockSpec((1, tc), lambda i: (0, i))],
        out_specs=pl.BlockSpec((s, tc), lambda i: (0, i)),
        out_shape=jax.ShapeDtypeStruct((s, sw), BF16),
        compiler_params=_cp(1),
    )(proj, proj, proj, conv_w, conv_b)


def _pool_counts(shape, window):
    t = lax.broadcasted_iota(jnp.int32, shape, 0)
    return jnp.minimum(t + 1, window).astype(F32)


def _pool_fwd(proj):
    _, s, sw = proj.shape
    gw = sw // len(POOL_WINDOWS)

    def body(v_ref, p_ref):
        for gi, window in enumerate(POOL_WINDOWS):
            @pl.when(pl.program_id(0) == gi)
            def _():
                v = v_ref[...]
                acc, k = v, 1
                while k < window:
                    acc = acc + _shift_down(acc, k)
                    k *= 2
                p_ref[...] = (acc / _pool_counts(v.shape, window) - v).astype(BF16)

    return pl.pallas_call(
        body, name="pool_fwd", grid=(len(POOL_WINDOWS),),
        in_specs=[pl.BlockSpec((None, s, gw), lambda g: (3, 0, g))],
        out_specs=pl.BlockSpec((s, gw), lambda g: (0, g)),
        out_shape=jax.ShapeDtypeStruct((s, sw), BF16),
        compiler_params=_cp(1),
    )(proj)


def _merge_fwd(z, wa, p, wpool, proj, b_gate2, pool_scale):
    s, sw = z.shape
    tn = wa.shape[2]
    d = tn * N_DEV
    gw = sw // len(POOL_WINDOWS)
    nq = sw // tn

    def body(z_ref, wa_ref, p_ref, wp_ref, ga_ref, gb_ref, bg_ref, sc_ref,
             m_ref, dya_ref, dyb_ref, dga_ref, dgb_ref, dsc_ref):
        for rs in _chunks(s, 512):
            ya = _dot(z_ref[rs, :], wa_ref[...], NN)
            yb = _dot(p_ref[rs, :], wp_ref[...], NN)
            sa = _sigmoid(ga_ref[rs, :] + bg_ref[0:1, :])
            sb = _sigmoid(gb_ref[rs, :] + bg_ref[1:2, :])
            sc = sc_ref[...]
            sb_yb = sb * yb
            m_ref[rs, :] = (sa * ya + sb_yb * sc).astype(BF16)
            dya_ref[rs, :] = sa.astype(BF16)
            dyb_ref[rs, :] = (sb * sc).astype(BF16)
            dga_ref[rs, :] = (ya * (sa * (1.0 - sa))).astype(BF16)
            dgb_ref[rs, :] = ((yb * sc) * (sb * (1.0 - sb))).astype(BF16)
            dsc_ref[rs, :] = sb_yb.astype(BF16)

    col = pl.BlockSpec((s, tn), lambda j: (0, j))
    out = jax.ShapeDtypeStruct((s, d), BF16)
    return pl.pallas_call(
        body, name="merge_fwd", grid=(N_DEV,),
        in_specs=[pl.BlockSpec((s, sw), lambda j: (0, 0)),
                  pl.BlockSpec((None, sw, tn), lambda j: (j, 0, 0)),
                  pl.BlockSpec((s, gw), lambda j: (0, j // 2)),
                  pl.BlockSpec((None, gw, tn), lambda j: (j // 2, 0, j % 2)),
                  pl.BlockSpec((None, s, tn), lambda j: (4 + j // nq, 0, j % nq)),
                  pl.BlockSpec((None, s, tn), lambda j: (6 + j // nq, 0, j % nq)),
                  pl.BlockSpec((2, tn), lambda j: (0, j)),
                  pl.BlockSpec((1, tn), lambda j: (0, j))],
        out_specs=[col] * 6,
        out_shape=[out] * 6,
        compiler_params=_cp(1),
    )(z, wa, p, wpool, proj, proj, b_gate2, pool_scale)


def _wo_fwd(merged, wo, x2d, g2):
    s, d = x2d.shape
    tm = min(256, s)

    def body(m_ref, wo_ref, x_ref, g_ref, x1_ref, h2_ref):
        x1 = x_ref[...] + _dot(m_ref[...], wo_ref[...], NN)
        x1_ref[...] = x1
        r = lax.rsqrt(jnp.mean(x1 * x1, axis=-1, keepdims=True) + EPS)
        h2_ref[...] = (x1 * r * g_ref[...]).astype(BF16)

    row = pl.BlockSpec((tm, d), lambda i: (i, 0))
    return pl.pallas_call(
        body, name="wo_fwd", grid=(s // tm,),
        in_specs=[row, pl.BlockSpec((d, d), lambda i: (0, 0)), row, pl.BlockSpec((1, d), lambda i: (0, 0))],
        out_specs=[row, row],
        out_shape=[jax.ShapeDtypeStruct((s, d), F32), jax.ShapeDtypeStruct((s, d), BF16)],
        compiler_params=_cp(1),
    )(merged, wo, x2d, g2)


def _ffn_up_act_fwd(h2, wg_g, wu_g):
    s, d = h2.shape
    f8 = wg_g.shape[2]
    th = min(1024, s)

    def body(h_ref, wg_ref, wu_ref, dadu_ref, dadg_ref, a_ref):
        i = pl.program_id(1)
        for rs in _chunks(th, 512):
            rows = pl.ds(pl.multiple_of(i * th + rs.start, rs.stop - rs.start), rs.stop - rs.start)
            a = h_ref[rows, :]
            g = _dot(a, wg_ref[...], NN)
            u = _dot(a, wu_ref[...], NN)
            sg = _sigmoid(g)
            silu = g * sg
            dadu_ref[rs, :] = silu.astype(BF16)
            dadg_ref[rs, :] = (u * (sg * (1.0 + g * (1.0 - sg)))).astype(BF16)
            a_ref[rs, :] = (silu * u).astype(BF16)

    wspec = pl.BlockSpec((None, d, f8), lambda j, i: (j, 0, 0))
    ospec = pl.BlockSpec((None, th, f8), lambda j, i: (j, i, 0))
    out = jax.ShapeDtypeStruct((N_DEV, s, f8), BF16)
    return pl.pallas_call(
        body, name="ffn_up_fwd", grid=(N_DEV, s // th),
        in_specs=[pl.BlockSpec((s, d), lambda j, i: (0, 0)), wspec, wspec],
        out_specs=[ospec, ospec, ospec], out_shape=[out, out, out],
        compiler_params=_cp(2),
    )(h2, wg_g, wu_g)


def _ffn_down_fwd(act, wd_g):
    _, s, f8 = act.shape
    d = wd_g.shape[2]
    tn = min(1024, d)

    def body(a_ref, wd_ref, o_ref):
        j = pl.program_id(1)

        @pl.when(j == 0)
        def _():
            o_ref[...] = jnp.zeros_like(o_ref)

        for rs in _chunks(s, 1024):
            o_ref[rs, :] += _dot(a_ref[rs, :], wd_ref[...], NN)

    return pl.pallas_call(
        body, name="ffn_down_fwd", grid=(d // tn, N_DEV),
        in_specs=[pl.BlockSpec((None, s, f8), lambda n, j: (j, 0, 0)),
                  pl.BlockSpec((None, f8, tn), lambda n, j: (j, 0, n))],
        out_specs=pl.BlockSpec((s, tn), lambda n, j: (0, n)),
        out_shape=jax.ShapeDtypeStruct((s, d), F32),
        compiler_params=_cp(2),
    )(act, wd_g)


def _loss_bwd(ffn_out, x1, target, final_g):
    s, d = x1.shape
    tm = min(256, s)

    def body(f_ref, x1_ref, t_ref, gf_ref, dx_ref, dxb_ref, dgf_ref, loss_ref):
        @pl.when(pl.program_id(0) == 0)
        def _():
            dgf_ref[...] = jnp.zeros_like(dgf_ref)
            loss_ref[...] = jnp.zeros_like(loss_ref)

        x2 = x1_ref[...] + f_ref[...]
        r = lax.rsqrt(jnp.mean(x2 * x2, axis=-1, keepdims=True) + EPS)
        nrm = x2 * r
        gf = gf_ref[...]
        err = nrm * gf - t_ref[...]
        loss_ref[...] += jnp.sum(err * err) * (0.5 / d)
        dy = err * (1.0 / d)
        dgf_ref[...] += jnp.sum(dy * nrm, axis=0, keepdims=True)
        dn = dy * gf
        dx = r * (dn - nrm * jnp.mean(dn * nrm, axis=-1, keepdims=True))
        dx_ref[...] = dx
        dxb_ref[...] = dx.astype(BF16)

    row = pl.BlockSpec((tm, d), lambda i: (i, 0))
    vec = pl.BlockSpec((1, d), lambda i: (0, 0))
    return pl.pallas_call(
        body, name="loss_bwd", grid=(s // tm,),
        in_specs=[row, row, row, vec],
        out_specs=[row, row, vec, pl.BlockSpec((8, LANES), lambda i: (0, 0))],
        out_shape=[jax.ShapeDtypeStruct((s, d), F32), jax.ShapeDtypeStruct((s, d), BF16),
                   jax.ShapeDtypeStruct((1, d), F32), jax.ShapeDtypeStruct((8, LANES), F32)],
        compiler_params=_cp(1),
    )(ffn_out, x1, target, final_g)


def _ffn_gate_bwd(dx2b, wd_g, dadg, dadu):
    s, d = dx2b.shape
    f8 = dadg.shape[2]
    th = min(1024, s)

    def body(dx_ref, wd_ref, g_ref, u_ref, dg_ref, du_ref, da_ref):
        i = pl.program_id(1)
        chunks = _chunks(th, 256)

        def matmul(rs):
            rows = pl.ds(pl.multiple_of(i * th + rs.start, rs.stop - rs.start), rs.stop - rs.start)
            da_ref[rs, :] = _dot(dx_ref[rows, :], wd_ref[...], NT)

        matmul(chunks[0])
        for k, rs in enumerate(chunks):
            if k + 1 < len(chunks):
                matmul(chunks[k + 1])
            da = da_ref[rs, :]
            dg_ref[rs, :] = (da * g_ref[rs, :].astype(F32)).astype(BF16)
            du_ref[rs, :] = (da * u_ref[rs, :].astype(F32)).astype(BF16)

    aspec = pl.BlockSpec((None, th, f8), lambda j, i: (j, i, 0))
    out = jax.ShapeDtypeStruct((N_DEV, s, f8), BF16)
    return pl.pallas_call(
        body, name="ffn_act_bwd", grid=(N_DEV, s // th),
        in_specs=[pl.BlockSpec((s, d), lambda j, i: (0, 0)),
                  pl.BlockSpec((None, f8, d), lambda j, i: (j, 0, 0)), aspec, aspec],
        out_specs=[aspec, aspec], out_shape=[out, out],
        scratch_shapes=[pltpu.VMEM((th, f8), F32)],
        compiler_params=_cp(2),
    )(dx2b, wd_g, dadg, dadu)


def _wgrad_rs(a, b, order, name, a_sharded, nh=1, after=()):
    if a_sharded:
        _, s, k = a.shape
        nb = b.shape[1]
        a_spec = pl.BlockSpec((None, s, k), lambda t, o: (o[t], 0, 0))
        b_spec = pl.BlockSpec((s, nb), lambda t, o: (0, 0))
    else:
        s, k = a.shape
        nb = b.shape[2] // nh
        a_spec = pl.BlockSpec((s, k), lambda t, o: (0, 0))
        b_spec = pl.BlockSpec((None, s, nb), lambda t, o: (o[t], 0, t % nh))
    units = 4 * nh

    def body(order_ref, a_ref, b_ref, *rest):
        p_ref, recv_hbm, send_buf, recv_buf, send_sems, recv_sems, load_sem = rest[len(after):]
        t = pl.program_id(0)
        u = t % units
        slot = u % 2
        x, y, c = _coords()
        sibling = (x, y, 1 - c)

        def send(unit, buf_slot):
            return pltpu.make_async_remote_copy(
                src_ref=send_buf.at[buf_slot], dst_ref=recv_hbm.at[unit],
                send_sem=send_sems.at[unit], recv_sem=recv_sems.at[unit],
                device_id=sibling, device_id_type=MESH)

        @pl.when(t == 0)
        def _():
            _handshake([sibling])

        @pl.when(t < units)
        def _():
            res = _dot(a_ref[...], b_ref[...], TN).astype(BF16)

            @pl.when(u >= 2)
            def _():
                send(u - 2, slot).wait_send()

            send_buf[slot] = res
            send(u, slot).start()

        @pl.when(t >= units)
        def _():
            send(u, slot).wait_recv()
            load = pltpu.make_async_copy(recv_hbm.at[u], recv_buf, load_sem)
            load.start()
            res = _dot(a_ref[...], b_ref[...], TN)
            load.wait()
            p_ref[...] = (res + recv_buf[...].astype(F32)).astype(BF16)

        @pl.when(t == 2 * units - 1)
        def _():
            for unit in (units - 2, units - 1):
                send(unit, unit % 2).wait_send()

    outs = pl.pallas_call(
        body, name=name,
        grid_spec=pltpu.PrefetchScalarGridSpec(
            num_scalar_prefetch=1, grid=(2 * units,),
            in_specs=[a_spec, b_spec] + _after_specs(after),
            out_specs=[pl.BlockSpec((None, None, k, nb),
                                    lambda t, o: (jnp.maximum(t - units, 0) // nh, jnp.where(t >= units, t % nh, 0), 0, 0)),
                       pl.BlockSpec(memory_space=pl.ANY)],
            scratch_shapes=[pltpu.VMEM((2, k, nb), BF16), pltpu.VMEM((k, nb), BF16),
                            pltpu.SemaphoreType.DMA((units,)), pltpu.SemaphoreType.DMA((units,)),
                            pltpu.SemaphoreType.DMA]),
        out_shape=[jax.ShapeDtypeStruct((4, nh, k, nb), BF16), jax.ShapeDtypeStruct((units, k, nb), BF16)],
        compiler_params=_cp_carry(1, [None]),
    )(order, a, b, *after)
    return outs[0]


def _wgrad_rows(a3, b, name, after=(), carry=()):
    _, s, k = a3.shape
    n = b.shape[1]
    nc = len(carry)
    c_in, c_out, c_shape, c_sems = _carry_specs(carry)

    def body(a_ref, b_ref, *rest):
        rest = rest[len(after):]
        o_ref = rest[nc]
        j = pl.program_id(0)
        _carry_run(j == 0, j == N_DEV - 1, rest[:nc], rest[nc + 1:2 * nc + 1], rest[2 * nc + 1:])
        o_ref[...] = _dot(a_ref[...], b_ref[...], TN).astype(BF16)

    outs = pl.pallas_call(
        body, name=name, grid=(N_DEV,),
        in_specs=[pl.BlockSpec((None, s, k), lambda j: (j, 0, 0)),
                  pl.BlockSpec((s, n), lambda j: (0, 0))] + _after_specs(after) + c_in,
        out_specs=[pl.BlockSpec((None, k, n), lambda j: (j, 0, 0))] + c_out,
        out_shape=[jax.ShapeDtypeStruct((N_DEV, k, n), BF16)] + c_shape,
        scratch_shapes=c_sems,
        compiler_params=_cp_carry(1, carry),
    )(a3, b, *after, *carry)
    return (outs[0], list(outs[1:])) if nc else outs[0]


def _wgrad_cols(a, b3, name, after=()):
    s, k = a.shape
    if b3.ndim == 2:
        n = b3.shape[1] // N_DEV
        b_spec = pl.BlockSpec((s, n), lambda j: (0, j))
    else:
        n = b3.shape[2]
        b_spec = pl.BlockSpec((None, s, n), lambda j: (j, 0, 0))

    def body(a_ref, b_ref, *rest):
        o_ref = rest[len(after)]
        o_ref[...] = _dot(a_ref[...], b_ref[...], TN).astype(BF16)

    return pl.pallas_call(
        body, name=name, grid=(N_DEV,),
        in_specs=[pl.BlockSpec((s, k), lambda j: (0, 0)), b_spec] + _after_specs(after),
        out_specs=pl.BlockSpec((None, k, n), lambda j: (j, 0, 0)),
        out_shape=jax.ShapeDtypeStruct((N_DEV, k, n), BF16),
        compiler_params=_cp(1),
    )(a, b3, *after)


def _input_grad(pairs, name, after=(), carry=()):
    s = pairs[0][0].shape[1]
    d = pairs[0][1].shape[1]
    tn = min(1024, d)
    npair = len(pairs)
    nc = len(carry)
    c_in, c_out, c_shape, c_sems = _carry_specs(carry)

    def body(*refs):
        ops = refs[:2 * npair]
        rest = refs[2 * npair + len(after):]
        o_ref = rest[nc]
        nh, j = pl.program_id(0), pl.program_id(1)
        _carry_run((nh == 0) & (j == 0), (nh == d // tn - 1) & (j == N_DEV - 1),
                   rest[:nc], rest[nc + 1:2 * nc + 1], rest[2 * nc + 1:])

        @pl.when(j == 0)
        def _():
            o_ref[...] = jnp.zeros_like(o_ref)

        for rs in _chunks(s, 1024):
            part = _dot(ops[0][rs, :], ops[1][...], NT)
            for q in range(1, npair):
                part = part + _dot(ops[2 * q][rs, :], ops[2 * q + 1][...], NT)
            o_ref[rs, :] += part

    in_specs, args = [], []
    for a3, w3 in pairs:
        k = a3.shape[2]
        in_specs += [pl.BlockSpec((None, s, k), lambda n, j: (j, 0, 0)),
                     pl.BlockSpec((None, tn, k), lambda n, j: (j, n, 0))]
        args += [a3, w3]
    outs = pl.pallas_call(
        body, name=name, grid=(d // tn, N_DEV),
        in_specs=in_specs + _after_specs(after) + c_in,
        out_specs=[pl.BlockSpec((s, tn), lambda n, j: (0, n))] + c_out,
        out_shape=[jax.ShapeDtypeStruct((s, d), F32)] + c_shape,
        scratch_shapes=c_sems,
        compiler_params=_cp_carry(2, carry),
    )(*args, *after, *carry)
    return (outs[0], list(outs[1:])) if nc else outs[0]


def _rms_bwd(dh, xres, g, dres, name):
    s, d = xres.shape
    tm = min(256, s)

    def body(dh_ref, x_ref, g_ref, dres_ref, dx_ref, dxb_ref, dg_ref):
        @pl.when(pl.program_id(0) == 0)
        def _():
            dg_ref[...] = jnp.zeros_like(dg_ref)

        xv = x_ref[...]
        dh_v = dh_ref[...]
        r = lax.rsqrt(jnp.mean(xv * xv, axis=-1, keepdims=True) + EPS)
        nrm = xv * r
        dg_ref[...] += jnp.sum(dh_v * nrm, axis=0, keepdims=True)
        dn = dh_v * g_ref[...]
        dx = dres_ref[...] + r * (dn - nrm * jnp.mean(dn * nrm, axis=-1, keepdims=True))
        dx_ref[...] = dx
        dxb_ref[...] = dx.astype(BF16)

    row = pl.BlockSpec((tm, d), lambda i: (i, 0))
    vec = pl.BlockSpec((1, d), lambda i: (0, 0))
    return pl.pallas_call(
        body, name=name, grid=(s // tm,),
        in_specs=[row, row, vec, row],
        out_specs=[row, row, vec],
        out_shape=[jax.ShapeDtypeStruct((s, d), F32), jax.ShapeDtypeStruct((s, d), BF16),
                   jax.ShapeDtypeStruct((1, d), F32)],
        compiler_params=_cp(1),
    )(dh, xres, g, dres)


def _wgrad_full(a, b, name, after=(), carry=()):
    s, k = a.shape
    n = b.shape[1]
    tk = min(512, k)
    nc = len(carry)
    c_in, c_out, c_shape, c_sems = _carry_specs(carry)

    def body(a_ref, b_ref, *rest):
        rest = rest[len(after):]
        o_ref = rest[nc]
        j = pl.program_id(0)
        _carry_run(j == 0, j == k // tk - 1, rest[:nc], rest[nc + 1:2 * nc + 1], rest[2 * nc + 1:])
        o_ref[...] = _dot(a_ref[...], b_ref[...], TN).astype(BF16)

    outs = pl.pallas_call(
        body, name=name, grid=(k // tk,),
        in_specs=[pl.BlockSpec((s, tk), lambda j: (0, j)),
                  pl.BlockSpec((s, n), lambda j: (0, 0))] + _after_specs(after) + c_in,
        out_specs=[pl.BlockSpec((tk, n), lambda j: (j, 0))] + c_out,
        out_shape=[jax.ShapeDtypeStruct((k, n), BF16)] + c_shape,
        scratch_shapes=c_sems,
        compiler_params=_cp_carry(1, carry),
    )(a, b, *after, *carry)
    return (outs[0], list(outs[1:])) if nc else outs[0]


def _wgrad_pool(p, dyb, n_groups):
    s, sw = p.shape
    d = dyb.shape[1]
    gw, go = sw // n_groups, d // n_groups
    ts = min(512, s)
    ns = s // ts

    def body(a_ref, b_ref, o_ref, acc_ref):
        i = pl.program_id(1)

        @pl.when(i == 0)
        def _():
            acc_ref[...] = jnp.zeros_like(acc_ref)

        acc_ref[...] += _dot(a_ref[...], b_ref[...], TN)

        @pl.when(i == ns - 1)
        def _():
            o_ref[...] = acc_ref[...].astype(BF16)

    return pl.pallas_call(
        body, name="wgrad_pool", grid=(n_groups, ns),
        in_specs=[pl.BlockSpec((ts, gw), lambda g, i: (i, g)),
                  pl.BlockSpec((ts, go), lambda g, i: (i, g))],
        out_specs=pl.BlockSpec((None, gw, go), lambda g, i: (g, 0, 0)),
        out_shape=jax.ShapeDtypeStruct((n_groups, gw, go), BF16),
        scratch_shapes=[pltpu.VMEM((gw, go), F32)],
        compiler_params=_cp(2),
    )(p, dyb)


def _wo_bwd(dx1b, wo, factors, sw, after=()):
    s, d = dx1b.shape
    tn = d // N_DEV
    nq = sw // tn

    def body(dx_ref, wo_ref, fya_ref, fyb_ref, fga_ref, fgb_ref, fsc_ref, *rest):
        dya_ref, dyb_ref, dp_ref, dbg_ref, dsc_ref, dm_ref = rest[len(after):]
        dbg_ref[...] = jnp.zeros_like(dbg_ref)
        dsc_ref[...] = jnp.zeros_like(dsc_ref)
        for rs in _chunks(s, 1024):
            dm_ref[rs, :] = _dot(dx_ref[rs, :], wo_ref[...], NT)
        for rs in _chunks(s, 256):
            dm = dm_ref[rs, :]
            dya_ref[rs, :] = (dm * fya_ref[rs, :].astype(F32)).astype(BF16)
            dyb_ref[rs, :] = (dm * fyb_ref[rs, :].astype(F32)).astype(BF16)
            dsc_ref[...] += jnp.sum(dm * fsc_ref[rs, :].astype(F32), axis=0, keepdims=True)
            dga = dm * fga_ref[rs, :].astype(F32)
            dgb = dm * fgb_ref[rs, :].astype(F32)
            dp_ref[0, rs, :] = dga.astype(BF16)
            dp_ref[1, rs, :] = dgb.astype(BF16)
            dbg_ref[0:1, :] += jnp.sum(dga, axis=0, keepdims=True)
            dbg_ref[1:2, :] += jnp.sum(dgb, axis=0, keepdims=True)

    col = pl.BlockSpec((s, tn), lambda j: (0, j))
    out = jax.ShapeDtypeStruct((s, d), BF16)
    return pl.pallas_call(
        body, name="wo_bwd", grid=(N_DEV,),
        in_specs=[pl.BlockSpec((s, d), lambda j: (0, 0)),
                  pl.BlockSpec((tn, d), lambda j: (j, 0))] + [col] * 5 + _after_specs(after),
        out_specs=[col, col,
                   pl.BlockSpec((2, None, s, tn), lambda j: (1, j // nq, 0, j % nq)),
                   pl.BlockSpec((2, tn), lambda j: (0, j)),
                   pl.BlockSpec((1, tn), lambda j: (0, j))],
        out_shape=[out, out, jax.ShapeDtypeStruct((4, 2, s, sw), BF16),
                   jax.ShapeDtypeStruct((2, d), F32), jax.ShapeDtypeStruct((1, d), F32)],
        scratch_shapes=[pltpu.VMEM((s, tn), F32)],
        compiler_params=_cp(1),
    )(dx1b, wo, *factors, *after)


def _conv_bwd(dproj, dya, wa, proj, conv_w, conv_b):
    s, d = dya.shape
    sw, tn = wa.shape[1], wa.shape[2]
    tc = min(LANES, sw)

    def body(dproj_hbm, dya_ref, wa_ref, ba_ref, ca_ref, va_ref, cw_ref, cb_ref,
             dp_ref, dcw_ref, dcb_ref, dz_ref):
        del dproj_hbm
        for rs in _chunks(s, 512):
            part = _dot(dya_ref[rs, 0:tn], wa_ref[0], NT)
            for j in range(1, N_DEV):
                part = part + _dot(dya_ref[rs, j * tn:(j + 1) * tn], wa_ref[j], NT)
            dz_ref[rs, :] = part
        dz = dz_ref[...]
        ba, ca, va = ba_ref[...], ca_ref[...], va_ref[...]
        cv = ca * va
        cv1, cv2 = _shift_down(cv, 1), _shift_down(cv, 2)
        w0, w1, w2 = cw_ref[0:1, :], cw_ref[1:2, :], cw_ref[2:3, :]
        u = cb_ref[...] + w0 * cv2 + w1 * cv1 + w2 * cv
        du = dz * ba
        dp_ref[0] = (dz * u).astype(BF16)
        dcv = w2 * du + w1 * _shift_up(du, 1) + w0 * _shift_up(du, 2)
        dp_ref[1] = (dcv * va).astype(BF16)
        dp_ref[2] = (dcv * ca).astype(BF16)
        dcw_ref[0:1, :] = jnp.sum(du * cv2, axis=0, keepdims=True)
        dcw_ref[1:2, :] = jnp.sum(du * cv1, axis=0, keepdims=True)
        dcw_ref[2:3, :] = jnp.sum(du * cv, axis=0, keepdims=True)
        dcb_ref[...] = jnp.sum(du, axis=0, keepdims=True)

    def part(k):
        return pl.BlockSpec((None, s, tc), lambda i: (k, 0, i))

    return pl.pallas_call(
        body, name="conv_bwd", grid=(sw // tc,),
        in_specs=[pl.BlockSpec(memory_space=pl.ANY),
                  pl.BlockSpec((s, d), lambda i: (0, 0)),
                  pl.BlockSpec((N_DEV, tc, tn), lambda i: (0, i, 0)),
                  part(0), part(1), part(2),
                  pl.BlockSpec((CONV_K, tc), lambda i: (0, i)), pl.BlockSpec((1, tc), lambda i: (0, i))],
        out_specs=[pl.BlockSpec((3, s, tc), lambda i: (0, 0, i)),
                   pl.BlockSpec((CONV_K, tc), lambda i: (0, i)), pl.BlockSpec((1, tc), lambda i: (0, i))],
        out_shape=[jax.ShapeDtypeStruct(dproj.shape, BF16),
                   jax.ShapeDtypeStruct((CONV_K, sw), F32), jax.ShapeDtypeStruct((1, sw), F32)],
        scratch_shapes=[pltpu.VMEM((s, tc), F32)],
        input_output_aliases={0: 0},
        compiler_params=_cp(1),
    )(dproj, dya, wa, proj, proj, proj, conv_w, conv_b)


def _pool_bwd(dproj, dyb, wpool):
    s, d = dyb.shape
    n_groups, gw, go = wpool.shape

    def body(dproj_hbm, dyb_ref, wp_ref, dp_ref):
        del dproj_hbm
        for gi, window in enumerate(POOL_WINDOWS):
            @pl.when(pl.program_id(0) == gi)
            def _():
                dpool = _dot(dyb_ref[...], wp_ref[...], NT)
                acc, k = dpool / _pool_counts(dpool.shape, window), 1
                while k < window:
                    acc = acc + _shift_up(acc, k)
                    k *= 2
                dp_ref[...] = (acc - dpool).astype(BF16)

    return pl.pallas_call(
        body, name="pool_bwd", grid=(n_groups,),
        in_specs=[pl.BlockSpec(memory_space=pl.ANY),
                  pl.BlockSpec((s, go), lambda g: (0, g)),
                  pl.BlockSpec((None, gw, go), lambda g: (g, 0, 0))],
        out_specs=pl.BlockSpec((None, s, gw), lambda g: (3, 0, g)),
        out_shape=jax.ShapeDtypeStruct(dproj.shape, BF16),
        input_output_aliases={0: 0},
        compiler_params=_cp(1),
    )(dproj, dyb, wpool)


def _rows128(v):
    return v.reshape(-1, LANES)


def kernel(x, norm1_g, w_in, b_gate, conv_w, conv_b, w_a_out, w_pool, pool_scale, w_o, norm2_g, w_ffn_gate, w_ffn_up, w_ffn_down, final_g, loss_target, m_norm1_g, m_w_in, m_b_gate, m_conv_w, m_conv_b, m_w_a_out, m_w_pool, m_pool_scale, m_w_o, m_norm2_g, m_w_ffn_gate, m_w_ffn_up, m_w_ffn_down, m_final_g, v_norm1_g, v_w_in, v_b_gate, v_conv_w, v_conv_b, v_w_a_out, v_w_pool, v_pool_scale, v_w_o, v_norm2_g, v_w_ffn_gate, v_w_ffn_up, v_w_ffn_down, v_final_g):
    s, d = x.shape[1], x.shape[2]
    sw = w_in.shape[2]
    n_groups = w_pool.shape[1]
    gw = w_pool.shape[2]
    go = w_pool.shape[3] * N_DEV
    f8 = w_ffn_gate.shape[2]
    cws = conv_w.shape[2]
    assert sw == conv_w.shape[2] * N_DEV == gw * n_groups and go * n_groups == d and n_groups == len(POOL_WINDOWS)

    xi, yi, ci = _coords()
    me = 4 * xi + 2 * yi + ci
    my_chip = 2 * xi + yi

    x2d = x.reshape(s, d)
    target = loss_target.reshape(s, d)
    final_g2 = final_g.reshape(1, d)
    b_gate2 = b_gate.reshape(2, d)

    big_names = ["w_in", "w_a_out", "w_pool", "w_o", "w_ffn_gate", "w_ffn_up", "w_ffn_down"]
    big_w = [w_in, w_a_out, w_pool, w_o, w_ffn_gate, w_ffn_up, w_ffn_down]
    big_m = [m_w_in, m_w_a_out, m_w_pool, m_w_o, m_w_ffn_gate, m_w_ffn_up, m_w_ffn_down]
    big_v = [v_w_in, v_w_a_out, v_w_pool, v_w_o, v_w_ffn_gate, v_w_ffn_up, v_w_ffn_down]
    shapes2d = [(w.size // w.shape[-1], w.shape[-1]) for w in big_w]
    big_w2 = [w.reshape(sh) for w, sh in zip(big_w, shapes2d)]
    transposed = (4, 5)

    def view2d(t, a):
        t2 = t.reshape(shapes2d[a])
        return t2.T if a in transposed else t2

    def unview(o, a):
        return (o.T if a in transposed else o).reshape(big_w[a].shape)

    sb = [_cast_bf16(w, "cast_" + nm) for w, nm in zip(big_w2, big_names)]
    win_g, wa_g, wpool_g, wo_g = _allgather_big(sb[0:4], "allgather_mixer", COLLECTIVE_GATHER)
    wg_g, wu_g = _allgather_big(sb[4:6], "allgather_ffn_up", COLLECTIVE_GATHER)
    (wd_g,) = _allgather_big(sb[6:7], "allgather_ffn_down", COLLECTIVE_GATHER)
    convw_g = _allgather_small(jnp.pad(conv_w.reshape(CONV_K, cws), ((0, 8 - CONV_K), (0, 0))), "allgather_conv_w")
    conv_w_full = convw_g[:, :CONV_K, :].transpose(1, 0, 2).reshape(CONV_K, sw)
    wpool = wpool_g.reshape(N_DEV, n_groups, gw, go // N_DEV).transpose(1, 2, 0, 3).reshape(n_groups, gw, go)
    wo = wo_g.reshape(d, d)

    h = _rms_fwd(x2d, norm1_g)
    proj = _proj_fwd(h, win_g)
    z = _conv_fwd(proj, conv_w_full, conv_b)
    p = _pool_fwd(proj)
    merged, *merge_factors = _merge_fwd(z, wa_g, p, wpool, proj, b_gate2, pool_scale)
    x1, h2 = _wo_fwd(merged, wo, x2d, norm2_g)
    dadu, dadg, act = _ffn_up_act_fwd(h2, wg_g, wu_g)
    ffn_out = _ffn_down_fwd(act, wd_g)
    dx2, dx2b, d_final_g, loss_blk = _loss_bwd(ffn_out, x1, target, final_g2)

    quads = 2 * jnp.arange(4, dtype=jnp.int32)
    slots_own, slots_sib = quads + ci, quads + (1 - ci)
    order8 = jnp.concatenate([slots_sib, slots_own]).astype(jnp.int32)
    order16 = jnp.concatenate([jnp.repeat(slots_sib, 2), jnp.repeat(slots_own, 2)]).astype(jnp.int32)
    chip = jnp.stack([my_chip]).astype(jnp.int32)

    def adam(a, psum, chips):
        outs = _adam_big(chip, view2d(big_w[a], a), view2d(big_m[a], a), view2d(big_v[a], a),
                         psum, chips, "adam_" + big_names[a])
        return [unview(o, a) for o in outs]

    big_out = [None] * len(big_names)
    dg_act, du_act = _ffn_gate_bwd(dx2b, wd_g, dadg, dadu)
    ps_gate = _wgrad_rs(dg_act, h2, order8, "wgrad_ffn_gate", True)
    ps_up = _wgrad_rs(du_act, h2, order8, "wgrad_ffn_up", True)
    chips_gu = _exchange_chips([ps_gate, ps_up], "rs_chips_ffn_up", COLLECTIVE_CHIPS)
    ps_down = _wgrad_rs(act, dx2b, order8, "wgrad_ffn_down", True, after=[ps_up])
    chips_down = _exchange_chips([ps_down], "rs_chips_ffn_down", COLLECTIVE_CHIPS)
    dh2 = _input_grad([(dg_act, wg_g), (du_act, wu_g)], "ffn_in_bwd", after=[ps_down])
    dx1, dx1b, d_norm2_g = _rms_bwd(dh2, x1, norm2_g, dx2, "rms2_bwd")
    dya, dyb, dproj42, d_b_gate, d_pool_scale = _wo_bwd(dx1b, wo, merge_factors, sw)
    dproj = dproj42.reshape(N_DEV, s, sw)
    dproj, d_conv_w, d_conv_b = _conv_bwd(dproj, dya, wa_g, proj, conv_w_full, conv_b)
    dproj = _pool_bwd(dproj, dyb, wpool)
    ps_in = _wgrad_rs(h, dproj, order16, "wgrad_in", False, nh=2)
    chips_in = _exchange_chips([ps_in], "rs_chips_w_in", COLLECTIVE_CHIPS)
    gw_o = _wgrad_full(merged, dx1b, "wgrad_o", after=[ps_in])
    gw_a = _wgrad_cols(z, dya, "wgrad_a_out", after=[ps_in])
    gw_pool = _wgrad_pool(p, dyb, n_groups)
    mix3 = [gw_a,
            gw_pool.reshape(n_groups, gw, N_DEV, go // N_DEV).transpose(2, 0, 1, 3).reshape(N_DEV, n_groups * gw, go // N_DEV),
            gw_o.reshape(N_DEV, d // N_DEV, d)]
    big_out[4] = adam(4, ps_gate, chips_gu[0])
    big_out[5] = adam(5, ps_up, chips_gu[1])
    big_out[6] = adam(6, ps_down, chips_down[0])
    dh, sib_mix = _input_grad([(dproj, win_g)], "proj_in_bwd",
                              after=[big_out[4][0], big_out[5][0], big_out[6][0]], carry=mix3)
    ps_mix = [_chip_partial(slots_own.astype(jnp.int32), g3, r, "chip_partial_" + nm)
              for g3, r, nm in zip(mix3, sib_mix, ["w_a_out", "w_pool", "w_o"])]
    chips_mix = _exchange_chips(ps_mix, "rs_chips_mixer", COLLECTIVE_CHIPS)
    grad_x, _, d_norm1_g = _rms_bwd(dh, x2d, norm1_g, dx1, "rms1_bwd")
    big_out[0] = adam(0, ps_in, chips_in[0])
    for k in range(3):
        big_out[1 + k] = adam(1 + k, ps_mix[k], chips_mix[k])

    small_parts = [d_norm1_g, d_b_gate, d_conv_w, d_conv_b, d_pool_scale, d_norm2_g, d_final_g, loss_blk]
    sizes = [v.size for v in small_parts]
    packed = jnp.concatenate([_rows128(v) for v in small_parts], axis=0)
    summed = _sum_small(_allgather_small(packed, "allgather_small_grads")).reshape(-1)
    offs = [0]
    for n in sizes:
        offs.append(offs[-1] + n)
    g_norm1, g_bgate, g_convw_full, g_convb, g_pscale, g_norm2, g_final, loss_sum = [
        summed[offs[k]:offs[k + 1]] for k in range(len(sizes))]
    loss = loss_sum[0]
    g_convw = lax.dynamic_slice(g_convw_full.reshape(CONV_K, sw), (0, me * cws), (CONV_K, cws))
    small_w = [norm1_g, b_gate, conv_w, conv_b, pool_scale, norm2_g, final_g]
    small_m = [m_norm1_g, m_b_gate, m_conv_w, m_conv_b, m_pool_scale, m_norm2_g, m_final_g]
    small_v = [v_norm1_g, v_b_gate, v_conv_w, v_conv_b, v_pool_scale, v_norm2_g, v_final_g]
    small_g = [g_norm1, g_bgate, g_convw, g_convb, g_pscale, g_norm2, g_final]

    def pack(parts):
        flat = jnp.concatenate([v.reshape(-1) for v in parts])
        pad = (-flat.size) % (8 * LANES)
        return jnp.pad(flat, (0, pad)).reshape(-1, LANES)

    s_delta, s_m, s_v = _adam_small(pack(small_w), pack(small_g), pack(small_m), pack(small_v))
    soffs = [0]
    for w in small_w:
        soffs.append(soffs[-1] + w.size)

    def unpack(buf):
        flat = buf.reshape(-1)
        return [flat[soffs[k]:soffs[k + 1]].reshape(small_w[k].shape) for k in range(len(small_w))]

    small_grads = [g.reshape(w.shape) for g, w in zip(small_g, small_w)]
    small_delta, small_new_m, small_new_v = unpack(s_delta), unpack(s_m), unpack(s_v)

    order = ["norm1_g", "w_in", "b_gate", "conv_w", "conv_b", "w_a_out", "w_pool", "pool_scale", "w_o", "norm2_g",
             "w_ffn_gate", "w_ffn_up", "w_ffn_down", "final_g"]
    small_names = ["norm1_g", "b_gate", "conv_w", "conv_b", "pool_scale", "norm2_g", "final_g"]
    per_kind = [{}, {}, {}, {}]
    for a, nm in enumerate(big_names):
        for kind in range(4):
            per_kind[kind][nm] = big_out[a][kind]
    for k, nm in enumerate(small_names):
        per_kind[0][nm] = small_grads[k]
        per_kind[1][nm] = small_delta[k]
        per_kind[2][nm] = small_new_m[k]
        per_kind[3][nm] = small_new_v[k]
    result = [loss, grad_x.reshape(x.shape)]
    for kind in range(4):
        result += [per_kind[kind][nm] for nm in order]
    return tuple(result)
```

```python
import jax
import jax.numpy as jnp
from jax import lax
from jax.experimental import pallas as pl
from jax.experimental.pallas import tpu as pltpu
from jax.experimental.pallas import tpu_sc as plsc

F32 = jnp.float32
BF16 = jnp.bfloat16
MESH = pl.DeviceIdType.MESH

N_DEV = 8
EPS = 1e-6
CONV_K = 3
POOL_WINDOWS = (2, 4, 8, 16)
ADAM_LR = 0.001
ADAM_B1 = 0.9
ADAM_B2 = 0.999
ADAM_EPS = 1e-08
ADAM_WD = 0.01
ADAM_STEP = 10

V7X_VMEM_LIMIT_BYTES = 56 * 1024 * 1024
LANES = 128

COLLECTIVE_GATHER = 1
COLLECTIVE_SIBLING = 2
COLLECTIVE_CHIPS = 3
SEQUENCER_COST_BYTES = 4 * 10**9

NN = ((1,), (0,))
NT = ((1,), (1,))
TN = ((0,), (0,))


def _dot(a, b, dims):
    return lax.dot_general(a, b, (dims, ((), ())), preferred_element_type=F32)


def _cp(n_axes):
    return pltpu.CompilerParams(dimension_semantics=("arbitrary",) * n_axes,
                                vmem_limit_bytes=V7X_VMEM_LIMIT_BYTES)


def _row_tile(rows, bytes_per_row, cap_bytes):
    best = None
    for t in range(16, rows + 1, 16):
        if rows % t == 0 and t * bytes_per_row <= cap_bytes:
            best = t
    return best if best is not None else rows


def _chunks(total, size):
    size = min(size, total)
    assert total % size == 0
    return [slice(r, r + size) for r in range(0, total, size)]


def _after_specs(after):
    return [pl.BlockSpec(memory_space=pl.ANY)] * len(after)


def _shift_down(v, k):
    row = lax.broadcasted_iota(jnp.int32, v.shape, 0)
    return jnp.where(row >= k, pltpu.roll(v, k, 0), 0.0)


def _shift_up(v, k):
    n = v.shape[0]
    row = lax.broadcasted_iota(jnp.int32, v.shape, 0)
    return jnp.where(row < n - k, pltpu.roll(v, n - k, 0), 0.0)


def _sigmoid(v):
    return jax.nn.sigmoid(v)


def _cast_bf16(w2d, name):
    rows, cols = w2d.shape
    tr = _row_tile(rows, cols * 4, 2 << 20)

    def body(i_ref, o_ref):
        o_ref[...] = i_ref[...].astype(BF16)

    return pl.pallas_call(
        body, name=name, grid=(rows // tr,),
        in_specs=[pl.BlockSpec((tr, cols), lambda i: (i, 0))],
        out_specs=pl.BlockSpec((tr, cols), lambda i: (i, 0)),
        out_shape=jax.ShapeDtypeStruct((rows, cols), BF16),
        compiler_params=_cp(1),
    )(w2d)


def _rms_fwd(x2d, g):
    s, d = x2d.shape
    tm = min(256, s)

    def body(x_ref, g_ref, h_ref):
        xv = x_ref[...]
        r = lax.rsqrt(jnp.mean(xv * xv, axis=-1, keepdims=True) + EPS)
        h_ref[...] = (xv * r * g_ref[...]).astype(BF16)

    return pl.pallas_call(
        body, name="rms1_fwd", grid=(s // tm,),
        in_specs=[pl.BlockSpec((tm, d), lambda i: (i, 0)), pl.BlockSpec((1, d), lambda i: (0, 0))],
        out_specs=pl.BlockSpec((tm, d), lambda i: (i, 0)),
        out_shape=jax.ShapeDtypeStruct((s, d), BF16),
        compiler_params=_cp(1),
    )(x2d, g)


def _coords():
    return lax.axis_index("x"), lax.axis_index("y"), lax.axis_index("c")


def _slot(p):
    return 4 * p[0] + 2 * p[1] + p[2]


def _handshake(peers):
    barrier = pltpu.get_barrier_semaphore()
    for peer in peers:
        pl.semaphore_signal(barrier, inc=1, device_id=peer, device_id_type=MESH)
    pl.semaphore_wait(barrier, len(peers))


def _sequencer_call(body, out_type, scratch_types, name, collective_id):
    return pl.kernel(
        body, out_type=out_type, name=name,
        mesh=plsc.ScalarSubcoreMesh(axis_name="seq", num_cores=1),
        scratch_types=scratch_types,
        cost_estimate=pl.CostEstimate(flops=0, transcendentals=0, bytes_accessed=SEQUENCER_COST_BYTES),
        compiler_params=pltpu.CompilerParams(collective_id=collective_id))


def _allgather_big(shards, name, collective_id, after=()):
    n = len(shards)

    def body(*refs):
        ins, outs = refs[:n], refs[n + len(after):2 * n + len(after)]
        send_sems, recv_sems, local_sems = refs[2 * n + len(after):]
        x, y, c = _coords()
        me, sibling = (x, y, c), (x, y, 1 - c)
        x_nbr, y_nbr, diag = (1 - x, y), (x, 1 - y), (1 - x, 1 - y)
        relay_from = (x + (1 - c) * (1 - 2 * x), y + c * (1 - 2 * y))
        relay_to = (x + c * (1 - 2 * x), y + (1 - c) * (1 - 2 * y))
        _handshake([sibling, (*x_nbr, c), (*y_nbr, c)])

        def copy(a, k, block, to, src=None):
            dst = outs[a].at[_slot(block)]
            return pltpu.make_async_remote_copy(
                src_ref=dst if src is None else src, dst_ref=dst,
                send_sem=send_sems.at[a, k], recv_sem=recv_sems.at[a, k],
                device_id=to, device_id_type=MESH)

        mine, sends = [], []
        for a in range(n):
            cp = pltpu.make_async_copy(ins[a], outs[a].at[_slot(me)], local_sems.at[a])
            cp.start()
            mine.append(cp)
            first = [copy(a, 0, me, sibling, src=ins[a]),
                     copy(a, 1, me, (*x_nbr, c), src=ins[a]),
                     copy(a, 2, me, (*y_nbr, c), src=ins[a])]
            for cp in first:
                cp.start()
            sends += first
        for a in range(n):
            copy(a, 1 + c, (*relay_from, c), me).wait_recv()
            passed = [copy(a, 3, (*relay_from, c), (*relay_to, c)), copy(a, 4 + c, (*relay_from, c), sibling)]
            for cp in passed:
                cp.start()
            copy(a, 2 - c, (*relay_to, c), me).wait_recv()
            cp = copy(a, 5 - c, (*relay_to, c), sibling)
            cp.start()
            passed.append(cp)
            copy(a, 3, (*diag, c), me).wait_recv()
            cp = copy(a, 6, (*diag, c), sibling)
            cp.start()
            sends += passed + [cp]
        for a in range(n):
            copy(a, 0, sibling, me).wait_recv()
            copy(a, 4, (*x_nbr, 1 - c), me).wait_recv()
            copy(a, 5, (*y_nbr, 1 - c), me).wait_recv()
            copy(a, 6, (*diag, 1 - c), me).wait_recv()
        for cp in sends:
            cp.wait_send()
        for cp in mine:
            cp.wait()

    return _sequencer_call(
        body, [jax.ShapeDtypeStruct((N_DEV,) + s.shape, s.dtype) for s in shards],
        [pltpu.SemaphoreType.DMA((n, 7)), pltpu.SemaphoreType.DMA((n, 7)), pltpu.SemaphoreType.DMA((n,))],
        name, collective_id)(*shards, *after)


def _sibling_copies(ins, recvs, send_sems, recv_sems):
    x, y, c = _coords()
    return [pltpu.make_async_remote_copy(
        src_ref=ins[a].at[2 * q + (1 - c)], dst_ref=recvs[a].at[q],
        send_sem=send_sems.at[a, q], recv_sem=recv_sems.at[a, q],
        device_id=(x, y, 1 - c), device_id_type=MESH) for a in range(len(ins)) for q in range(4)]


def _carry_specs(carry):
    any_spec = pl.BlockSpec(memory_space=pl.ANY)
    n = len(carry)
    sems = [pltpu.SemaphoreType.DMA((n, 4)), pltpu.SemaphoreType.DMA((n, 4))] if n else []
    return ([any_spec] * n, [any_spec] * n,
            [jax.ShapeDtypeStruct((4,) + g.shape[1:], g.dtype) for g in carry], sems)


def _carry_run(first, last, ins, recvs, sems):
    if not ins:
        return

    @pl.when(first)
    def _():
        x, y, c = _coords()
        _handshake([(x, y, 1 - c)])
        for cp in _sibling_copies(ins, recvs, *sems):
            cp.start()

    @pl.when(last)
    def _():
        copies = _sibling_copies(ins, recvs, *sems)
        for cp in copies:
            cp.wait_recv()
        for cp in copies:
            cp.wait_send()


def _cp_carry(n_axes, carry):
    if not carry:
        return _cp(n_axes)
    return pltpu.CompilerParams(dimension_semantics=("arbitrary",) * n_axes, vmem_limit_bytes=V7X_VMEM_LIMIT_BYTES,
                                collective_id=COLLECTIVE_SIBLING)


def _exchange_chips(psums, name, collective_id):
    n = len(psums)

    def body(*refs):
        ins, outs = refs[:n], refs[n:2 * n]
        send_sems, recv_sems = refs[2 * n:]
        x, y, c = _coords()
        chips = [(1 - x, y), (x, 1 - y), (1 - x, 1 - y)]
        _handshake([(*chip, c) for chip in chips])
        copies = []
        for a in range(n):
            for j, chip in enumerate(chips):
                cp = pltpu.make_async_remote_copy(
                    src_ref=ins[a].at[2 * chip[0] + chip[1]], dst_ref=outs[a].at[j],
                    send_sem=send_sems.at[a, j], recv_sem=recv_sems.at[a, j],
                    device_id=(*chip, c), device_id_type=MESH)
                cp.start()
                copies.append(cp)
        for cp in copies:
            cp.wait_recv()
        for cp in copies:
            cp.wait_send()

    return _sequencer_call(
        body, [jax.ShapeDtypeStruct((3,) + p.shape[1:], p.dtype) for p in psums],
        [pltpu.SemaphoreType.DMA((n, 3)), pltpu.SemaphoreType.DMA((n, 3))],
        name, collective_id)(*psums)


def _allgather_small(v2d, name):
    rows, cols = v2d.shape

    def body(v_ref, out_ref, send_sems, recv_sems):
        x, y, c = _coords()
        me = (x, y, c)
        out_ref[_slot(me)] = v_ref[...]
        peers = []
        for k in range(1, N_DEV):
            fx, fy, fc = (k >> 2) & 1, (k >> 1) & 1, k & 1
            peers.append(((1 - x) if fx else x, (1 - y) if fy else y, (1 - c) if fc else c))
        sends = []
        for k, peer in enumerate(peers):
            cp = pltpu.make_async_remote_copy(
                src_ref=v_ref, dst_ref=out_ref.at[_slot(me)],
                send_sem=send_sems.at[k], recv_sem=recv_sems.at[k],
                device_id=peer, device_id_type=MESH)
            cp.start()
            sends.append(cp)
        for k, peer in enumerate(peers):
            pltpu.make_async_remote_copy(
                src_ref=v_ref, dst_ref=out_ref.at[_slot(peer)],
                send_sem=send_sems.at[k], recv_sem=recv_sems.at[k],
                device_id=peer, device_id_type=MESH).wait_recv()
        for cp in sends:
            cp.wait_send()

    vmem = pl.BlockSpec(memory_space=pltpu.VMEM)
    return pl.pallas_call(
        body, name=name, in_specs=[vmem], out_specs=vmem,
        out_shape=jax.ShapeDtypeStruct((N_DEV, rows, cols), v2d.dtype),
        scratch_shapes=[pltpu.SemaphoreType.DMA((N_DEV - 1,)), pltpu.SemaphoreType.DMA((N_DEV - 1,))],
    )(v2d)


def _chip_partial(others, g3, recv, name):
    _, rows, cols = g3.shape
    tr = _row_tile(rows, cols * 2, 2 << 20)

    def body(others_ref, g_ref, r_ref, o_ref):
        o_ref[...] = (g_ref[...].astype(F32) + r_ref[...].astype(F32)).astype(BF16)

    return pl.pallas_call(
        body, name=name,
        grid_spec=pltpu.PrefetchScalarGridSpec(
            num_scalar_prefetch=1, grid=(3, rows // tr),
            in_specs=[pl.BlockSpec((None, tr, cols), lambda k, i, o: (o[3 + k], i, 0)),
                      pl.BlockSpec((None, tr, cols), lambda k, i, o: (o[k], i, 0))],
            out_specs=pl.BlockSpec((None, tr, cols), lambda k, i, o: (o[k], i, 0))),
        out_shape=jax.ShapeDtypeStruct((4, rows, cols), BF16),
        compiler_params=_cp(2),
    )(others, g3, recv)


def _adam_math(w, g, m, v):
    m = ADAM_B1 * m + (1.0 - ADAM_B1) * g
    v = ADAM_B2 * v + (1.0 - ADAM_B2) * (g * g)
    m_hat = m / (1.0 - ADAM_B1 ** ADAM_STEP)
    v_hat = v / (1.0 - ADAM_B2 ** ADAM_STEP)
    delta = -ADAM_LR * (m_hat / (jnp.sqrt(v_hat) + ADAM_EPS) + ADAM_WD * w)
    return delta, m, v


def _adam_big(own, w, m, v, g3, recv_sib, recv_chips, name):
    rows, cols = w.shape
    tr = _row_tile(rows, cols * 4, 2 << 20)

    def body(own_ref, w_ref, m_ref, v_ref, g_ref, rs_ref, rc_ref, go_ref, do_ref, mo_ref, vo_ref):
        g = g_ref[...].astype(F32) + rs_ref[...].astype(F32)
        g = g + rc_ref[0].astype(F32)
        g = g + rc_ref[1].astype(F32)
        g = g + rc_ref[2].astype(F32)
        delta, m_new, v_new = _adam_math(w_ref[...], g, m_ref[...], v_ref[...])
        go_ref[...] = g
        do_ref[...] = delta
        mo_ref[...] = m_new
        vo_ref[...] = v_new

    blk = pl.BlockSpec((tr, cols), lambda i, o: (i, 0))
    out = jax.ShapeDtypeStruct((rows, cols), F32)
    return pl.pallas_call(
        body, name=name,
        grid_spec=pltpu.PrefetchScalarGridSpec(
            num_scalar_prefetch=1, grid=(rows // tr,),
            in_specs=[blk, blk, blk,
                      pl.BlockSpec((None, tr, cols), lambda i, o: (o[0], i, 0)),
                      pl.BlockSpec((None, tr, cols), lambda i, o: (o[1], i, 0)),
                      pl.BlockSpec((3, tr, cols), lambda i, o: (0, i, 0))],
            out_specs=[blk, blk, blk, blk]),
        out_shape=[out, out, out, out],
        compiler_params=_cp(1),
    )(own, w, m, v, g3, recv_sib, recv_chips)


def _sum_small(gathered):
    _, rows, cols = gathered.shape

    def body(g_ref, o_ref):
        acc = g_ref[0]
        for k in range(1, N_DEV):
            acc = acc + g_ref[k]
        o_ref[...] = acc

    vmem = pl.BlockSpec(memory_space=pltpu.VMEM)
    return pl.pallas_call(body, name="small_grad_sum", in_specs=[vmem], out_specs=vmem,
                          out_shape=jax.ShapeDtypeStruct((rows, cols), F32))(gathered)


def _adam_small(w, g, m, v):
    def body(w_ref, g_ref, m_ref, v_ref, do_ref, mo_ref, vo_ref):
        delta, m_new, v_new = _adam_math(w_ref[...], g_ref[...], m_ref[...], v_ref[...])
        do_ref[...] = delta
        mo_ref[...] = m_new
        vo_ref[...] = v_new

    vmem = pl.BlockSpec(memory_space=pltpu.VMEM)
    out = jax.ShapeDtypeStruct(w.shape, F32)
    return pl.pallas_call(body, name="adam_small", in_specs=[vmem] * 4, out_specs=[vmem] * 3,
                          out_shape=[out, out, out])(w, g, m, v)


def _proj_fwd(h, win_g):
    s, d = h.shape
    sw = win_g.shape[2]
    tn = min(512, sw)
    nh = sw // tn

    def body(h_ref, w_ref, o_ref):
        for rs in _chunks(s, 512):
            o_ref[rs, :] = _dot(h_ref[rs, :], w_ref[...], NN)

    return pl.pallas_call(
        body, name="proj_fwd", grid=(N_DEV * nh,),
        in_specs=[pl.BlockSpec((s, d), lambda j: (0, 0)),
                  pl.BlockSpec((None, d, tn), lambda j: (j // nh, 0, j % nh))],
        out_specs=pl.BlockSpec((None, s, tn), lambda j: (j // nh, 0, j % nh)),
        out_shape=jax.ShapeDtypeStruct((N_DEV, s, sw), F32),
        compiler_params=_cp(1),
    )(h, win_g)


def _conv_fwd(proj, conv_w, conv_b):
    _, s, sw = proj.shape
    tc = min(LANES, sw)

    def body(ba_ref, ca_ref, va_ref, cw_ref, cb_ref, z_ref):
        cv = ca_ref[...] * va_ref[...]
        u = (cb_ref[...] + cw_ref[0:1, :] * _shift_down(cv, 2) + cw_ref[1:2, :] * _shift_down(cv, 1)
             + cw_ref[2:3, :] * cv)
        z_ref[...] = (ba_ref[...] * u).astype(BF16)

    def part(k):
        return pl.BlockSpec((None, s, tc), lambda i: (k, 0, i))

    return pl.pallas_call(
        body, name="conv_fwd", grid=(sw // tc,),
        in_specs=[part(0), part(1), part(2),
                  pl.BlockSpec((CONV_K, tc), lambda i: (0, i)), pl.BlockSpec((1, tc), lambda i: (0, i))],
        out_specs=pl.BlockSpec((s, tc), lambda i: (0, i)),
        out_shape=jax.ShapeDtypeStruct((s, sw), BF16),
        compiler_params=_cp(1),
    )(proj, proj, proj, conv_w, conv_b)


def _pool_counts(shape, window):
    t = lax.broadcasted_iota(jnp.int32, shape, 0)
    return jnp.minimum(t + 1, window).astype(F32)


def _pool_fwd(proj):
    _, s, sw = proj.shape
    gw = sw // len(POOL_WINDOWS)

    def body(v_ref, p_ref):
        for gi, window in enumerate(POOL_WINDOWS):
            @pl.when(pl.program_id(0) == gi)
            def _():
                v = v_ref[...]
                acc, k = v, 1
                while k < window:
                    acc = acc + _shift_down(acc, k)
                    k *= 2
                p_ref[...] = (acc / _pool_counts(v.shape, window) - v).astype(BF16)

    return pl.pallas_call(
        body, name="pool_fwd", grid=(len(POOL_WINDOWS),),
        in_specs=[pl.BlockSpec((None, s, gw), lambda g: (3, 0, g))],
        out_specs=pl.BlockSpec((s, gw), lambda g: (0, g)),
        out_shape=jax.ShapeDtypeStruct((s, sw), BF16),
        compiler_params=_cp(1),
    )(proj)


def _merge_fwd(z, wa, p, wpool, proj, b_gate2, pool_scale):
    s, sw = z.shape
    tn = wa.shape[2]
    d = tn * N_DEV
    gw = sw // len(POOL_WINDOWS)
    nq = sw // tn

    def body(z_ref, wa_ref, p_ref, wp_ref, ga_ref, gb_ref, bg_ref, sc_ref,
             m_ref, dya_ref, dyb_ref, dga_ref, dgb_ref, dsc_ref):
        for rs in _chunks(s, 512):
            ya = _dot(z_ref[rs, :], wa_ref[...], NN)
            yb = _dot(p_ref[rs, :], wp_ref[...], NN)
            sa = _sigmoid(ga_ref[rs, :] + bg_ref[0:1, :])
            sb = _sigmoid(gb_ref[rs, :] + bg_ref[1:2, :])
            sc = sc_ref[...]
            sb_yb = sb * yb
            m_ref[rs, :] = (sa * ya + sb_yb * sc).astype(BF16)
            dya_ref[rs, :] = sa.astype(BF16)
            dyb_ref[rs, :] = (sb * sc).astype(BF16)
            dga_ref[rs, :] = (ya * (sa * (1.0 - sa))).astype(BF16)
            dgb_ref[rs, :] = ((yb * sc) * (sb * (1.0 - sb))).astype(BF16)
            dsc_ref[rs, :] = sb_yb.astype(BF16)

    col = pl.BlockSpec((s, tn), lambda j: (0, j))
    out = jax.ShapeDtypeStruct((s, d), BF16)
    return pl.pallas_call(
        body, name="merge_fwd", grid=(N_DEV,),
        in_specs=[pl.BlockSpec((s, sw), lambda j: (0, 0)),
                  pl.BlockSpec((None, sw, tn), lambda j: (j, 0, 0)),
                  pl.BlockSpec((s, gw), lambda j: (0, j // 2)),
                  pl.BlockSpec((None, gw, tn), lambda j: (j // 2, 0, j % 2)),
                  pl.BlockSpec((None, s, tn), lambda j: (4 + j // nq, 0, j % nq)),
                  pl.BlockSpec((None, s, tn), lambda j: (6 + j // nq, 0, j % nq)),
                  pl.BlockSpec((2, tn), lambda j: (0, j)),
                  pl.BlockSpec((1, tn), lambda j: (0, j))],
        out_specs=[col] * 6,
        out_shape=[out] * 6,
        compiler_params=_cp(1),
    )(z, wa, p, wpool, proj, proj, b_gate2, pool_scale)


def _wo_fwd(merged, wo, x2d, g2):
    s, d = x2d.shape
    tm = min(256, s)

    def body(m_ref, wo_ref, x_ref, g_ref, x1_ref, h2_ref):
        x1 = x_ref[...] + _dot(m_ref[...], wo_ref[...], NN)
        x1_ref[...] = x1
        r = lax.rsqrt(jnp.mean(x1 * x1, axis=-1, keepdims=True) + EPS)
        h2_ref[...] = (x1 * r * g_ref[...]).astype(BF16)

    row = pl.BlockSpec((tm, d), lambda i: (i, 0))
    return pl.pallas_call(
        body, name="wo_fwd", grid=(s // tm,),
        in_specs=[row, pl.BlockSpec((d, d), lambda i: (0, 0)), row, pl.BlockSpec((1, d), lambda i: (0, 0))],
        out_specs=[row, row],
        out_shape=[jax.ShapeDtypeStruct((s, d), F32), jax.ShapeDtypeStruct((s, d), BF16)],
        compiler_params=_cp(1),
    )(merged, wo, x2d, g2)


def _ffn_up_act_fwd(h2, wg_g, wu_g):
    s, d = h2.shape
    f8 = wg_g.shape[2]
    th = min(1024, s)

    def body(h_ref, wg_ref, wu_ref, dadu_ref, dadg_ref, a_ref):
        i = pl.program_id(1)
        for rs in _chunks(th, 512):
            rows = pl.ds(pl.multiple_of(i * th + rs.start, rs.stop - rs.start), rs.stop - rs.start)
            a = h_ref[rows, :]
            g = _dot(a, wg_ref[...], NN)
            u = _dot(a, wu_ref[...], NN)
            sg = _sigmoid(g)
            silu = g * sg
            dadu_ref[rs, :] = silu.astype(BF16)
            dadg_ref[rs, :] = (u * (sg * (1.0 + g * (1.0 - sg)))).astype(BF16)
            a_ref[rs, :] = (silu * u).astype(BF16)

    wspec = pl.BlockSpec((None, d, f8), lambda j, i: (j, 0, 0))
    ospec = pl.BlockSpec((None, th, f8), lambda j, i: (j, i, 0))
    out = jax.ShapeDtypeStruct((N_DEV, s, f8), BF16)
    return pl.pallas_call(
        body, name="ffn_up_fwd", grid=(N_DEV, s // th),
        in_specs=[pl.BlockSpec((s, d), lambda j, i: (0, 0)), wspec, wspec],
        out_specs=[ospec, ospec, ospec], out_shape=[out, out, out],
        compiler_params=_cp(2),
    )(h2, wg_g, wu_g)


def _ffn_down_fwd(act, wd_g):
    _, s, f8 = act.shape
    d = wd_g.shape[2]
    tn = min(1024, d)

    def body(a_ref, wd_ref, o_ref):
        j = pl.program_id(1)

        @pl.when(j == 0)
        def _():
            o_ref[...] = jnp.zeros_like(o_ref)

        for rs in _chunks(s, 1024):
            o_ref[rs, :] += _dot(a_ref[rs, :], wd_ref[...], NN)

    return pl.pallas_call(
        body, name="ffn_down_fwd", grid=(d // tn, N_DEV),
        in_specs=[pl.BlockSpec((None, s, f8), lambda n, j: (j, 0, 0)),
                  pl.BlockSpec((None, f8, tn), lambda n, j: (j, 0, n))],
        out_specs=pl.BlockSpec((s, tn), lambda n, j: (0, n)),
        out_shape=jax.ShapeDtypeStruct((s, d), F32),
        compiler_params=_cp(2),
    )(act, wd_g)


def _loss_bwd(ffn_out, x1, target, final_g):
    s, d = x1.shape
    tm = min(256, s)

    def body(f_ref, x1_ref, t_ref, gf_ref, dxb_ref, dgf_ref, loss_ref):
        @pl.when(pl.program_id(0) == 0)
        def _():
            dgf_ref[...] = jnp.zeros_like(dgf_ref)
            loss_ref[...] = jnp.zeros_like(loss_ref)

        x2 = x1_ref[...] + f_ref[...]
        r = lax.rsqrt(jnp.mean(x2 * x2, axis=-1, keepdims=True) + EPS)
        nrm = x2 * r
        gf = gf_ref[...]
        err = nrm * gf - t_ref[...]
        loss_ref[...] += jnp.sum(err * err) * (0.5 / d)
        dy = err * (1.0 / d)
        dgf_ref[...] += jnp.sum(dy * nrm, axis=0, keepdims=True)
        dn = dy * gf
        dx = r * (dn - nrm * jnp.mean(dn * nrm, axis=-1, keepdims=True))
        dxb_ref[...] = dx.astype(BF16)

    row = pl.BlockSpec((tm, d), lambda i: (i, 0))
    vec = pl.BlockSpec((1, d), lambda i: (0, 0))
    return pl.pallas_call(
        body, name="loss_bwd", grid=(s // tm,),
        in_specs=[row, row, row, vec],
        out_specs=[row, vec, pl.BlockSpec((8, LANES), lambda i: (0, 0))],
        out_shape=[jax.ShapeDtypeStruct((s, d), BF16),
                   jax.ShapeDtypeStruct((1, d), F32), jax.ShapeDtypeStruct((8, LANES), F32)],
        compiler_params=_cp(1),
    )(ffn_out, x1, target, final_g)


def _ffn_gate_bwd(dx2b, wd_g, dadg, dadu):
    s, d = dx2b.shape
    f8 = dadg.shape[2]
    th = min(1024, s)

    def body(dx_ref, wd_ref, g_ref, u_ref, dg_ref, du_ref, da_ref):
        i = pl.program_id(1)
        chunks = _chunks(th, 256)

        def matmul(rs):
            rows = pl.ds(pl.multiple_of(i * th + rs.start, rs.stop - rs.start), rs.stop - rs.start)
            da_ref[rs, :] = _dot(dx_ref[rows, :], wd_ref[...], NT)

        matmul(chunks[0])
        for k, rs in enumerate(chunks):
            if k + 1 < len(chunks):
                matmul(chunks[k + 1])
            da = da_ref[rs, :]
            dg_ref[rs, :] = (da * g_ref[rs, :].astype(F32)).astype(BF16)
            du_ref[rs, :] = (da * u_ref[rs, :].astype(F32)).astype(BF16)

    aspec = pl.BlockSpec((None, th, f8), lambda j, i: (j, i, 0))
    out = jax.ShapeDtypeStruct((N_DEV, s, f8), BF16)
    return pl.pallas_call(
        body, name="ffn_act_bwd", grid=(N_DEV, s // th),
        in_specs=[pl.BlockSpec((s, d), lambda j, i: (0, 0)),
                  pl.BlockSpec((None, f8, d), lambda j, i: (j, 0, 0)), aspec, aspec],
        out_specs=[aspec, aspec], out_shape=[out, out],
        scratch_shapes=[pltpu.VMEM((th, f8), F32)],
        compiler_params=_cp(2),
    )(dx2b, wd_g, dadg, dadu)


def _wgrad_rows(a3, b, name, after=(), carry=()):
    _, s, k = a3.shape
    n = b.shape[1]
    nc = len(carry)
    c_in, c_out, c_shape, c_sems = _carry_specs(carry)

    def body(a_ref, b_ref, *rest):
        rest = rest[len(after):]
        o_ref = rest[nc]
        j = pl.program_id(0)
        _carry_run(j == 0, j == N_DEV - 1, rest[:nc], rest[nc + 1:2 * nc + 1], rest[2 * nc + 1:])
        o_ref[...] = _dot(a_ref[...], b_ref[...], TN).astype(BF16)

    outs = pl.pallas_call(
        body, name=name, grid=(N_DEV,),
        in_specs=[pl.BlockSpec((None, s, k), lambda j: (j, 0, 0)),
                  pl.BlockSpec((s, n), lambda j: (0, 0))] + _after_specs(after) + c_in,
        out_specs=[pl.BlockSpec((None, k, n), lambda j: (j, 0, 0))] + c_out,
        out_shape=[jax.ShapeDtypeStruct((N_DEV, k, n), BF16)] + c_shape,
        scratch_shapes=c_sems,
        compiler_params=_cp_carry(1, carry),
    )(a3, b, *after, *carry)
    return (outs[0], list(outs[1:])) if nc else outs[0]


def _wgrad_cols(a, b3, name, after=()):
    s, k = a.shape
    if b3.ndim == 2:
        n = b3.shape[1] // N_DEV
        b_spec = pl.BlockSpec((s, n), lambda j: (0, j))
    else:
        n = b3.shape[2]
        b_spec = pl.BlockSpec((None, s, n), lambda j: (j, 0, 0))

    def body(a_ref, b_ref, *rest):
        o_ref = rest[len(after)]
        o_ref[...] = _dot(a_ref[...], b_ref[...], TN).astype(BF16)

    return pl.pallas_call(
        body, name=name, grid=(N_DEV,),
        in_specs=[pl.BlockSpec((s, k), lambda j: (0, 0)), b_spec] + _after_specs(after),
        out_specs=pl.BlockSpec((None, k, n), lambda j: (j, 0, 0)),
        out_shape=jax.ShapeDtypeStruct((N_DEV, k, n), BF16),
        compiler_params=_cp(1),
    )(a, b3, *after)


def _input_grad(pairs, name, after=(), carry=()):
    s = pairs[0][0].shape[1]
    d = pairs[0][1].shape[1]
    tn = min(1024, d)
    npair = len(pairs)
    nc = len(carry)
    c_in, c_out, c_shape, c_sems = _carry_specs(carry)

    def body(*refs):
        ops = refs[:2 * npair]
        rest = refs[2 * npair + len(after):]
        o_ref = rest[nc]
        nh, j = pl.program_id(0), pl.program_id(1)
        _carry_run((nh == 0) & (j == 0), (nh == d // tn - 1) & (j == N_DEV - 1),
                   rest[:nc], rest[nc + 1:2 * nc + 1], rest[2 * nc + 1:])

        @pl.when(j == 0)
        def _():
            o_ref[...] = jnp.zeros_like(o_ref)

        for rs in _chunks(s, 1024):
            part = _dot(ops[0][rs, :], ops[1][...], NT)
            for q in range(1, npair):
                part = part + _dot(ops[2 * q][rs, :], ops[2 * q + 1][...], NT)
            o_ref[rs, :] += part

    in_specs, args = [], []
    for a3, w3 in pairs:
        k = a3.shape[2]
        in_specs += [pl.BlockSpec((None, s, k), lambda n, j: (j, 0, 0)),
                     pl.BlockSpec((None, tn, k), lambda n, j: (j, n, 0))]
        args += [a3, w3]
    outs = pl.pallas_call(
        body, name=name, grid=(d // tn, N_DEV),
        in_specs=in_specs + _after_specs(after) + c_in,
        out_specs=[pl.BlockSpec((s, tn), lambda n, j: (0, n))] + c_out,
        out_shape=[jax.ShapeDtypeStruct((s, d), F32)] + c_shape,
        scratch_shapes=c_sems,
        compiler_params=_cp_carry(2, carry),
    )(*args, *after, *carry)
    return (outs[0], list(outs[1:])) if nc else outs[0]


def _rms_bwd(dh, xres, g, dres, name, with_bf16=True):
    s, d = xres.shape
    tm = min(256, s)

    def body(dh_ref, x_ref, g_ref, dres_ref, dx_ref, *rest):
        dg_ref = rest[-1]
        @pl.when(pl.program_id(0) == 0)
        def _():
            dg_ref[...] = jnp.zeros_like(dg_ref)

        xv = x_ref[...]
        dh_v = dh_ref[...]
        r = lax.rsqrt(jnp.mean(xv * xv, axis=-1, keepdims=True) + EPS)
        nrm = xv * r
        dg_ref[...] += jnp.sum(dh_v * nrm, axis=0, keepdims=True)
        dn = dh_v * g_ref[...]
        dx = dres_ref[...].astype(F32) + r * (dn - nrm * jnp.mean(dn * nrm, axis=-1, keepdims=True))
        dx_ref[...] = dx
        if with_bf16:
            rest[0][...] = dx.astype(BF16)

    row = pl.BlockSpec((tm, d), lambda i: (i, 0))
    vec = pl.BlockSpec((1, d), lambda i: (0, 0))
    copies = [jax.ShapeDtypeStruct((s, d), BF16)] if with_bf16 else []
    outs = pl.pallas_call(
        body, name=name, grid=(s // tm,),
        in_specs=[row, row, vec, row],
        out_specs=[row] + [row] * len(copies) + [vec],
        out_shape=[jax.ShapeDtypeStruct((s, d), F32)] + copies + [jax.ShapeDtypeStruct((1, d), F32)],
        compiler_params=_cp(1),
    )(dh, xres, g, dres)
    return (outs[0], outs[1], outs[2]) if with_bf16 else (outs[0], None, outs[1])


def _wgrad_full(a, b, name, after=(), carry=()):
    s, k = a.shape
    n = b.shape[1]
    tk = min(512, k)
    nc = len(carry)
    c_in, c_out, c_shape, c_sems = _carry_specs(carry)

    def body(a_ref, b_ref, *rest):
        rest = rest[len(after):]
        o_ref = rest[nc]
        j = pl.program_id(0)
        _carry_run(j == 0, j == k // tk - 1, rest[:nc], rest[nc + 1:2 * nc + 1], rest[2 * nc + 1:])
        o_ref[...] = _dot(a_ref[...], b_ref[...], TN).astype(BF16)

    outs = pl.pallas_call(
        body, name=name, grid=(k // tk,),
        in_specs=[pl.BlockSpec((s, tk), lambda j: (0, j)),
                  pl.BlockSpec((s, n), lambda j: (0, 0))] + _after_specs(after) + c_in,
        out_specs=[pl.BlockSpec((tk, n), lambda j: (j, 0))] + c_out,
        out_shape=[jax.ShapeDtypeStruct((k, n), BF16)] + c_shape,
        scratch_shapes=c_sems,
        compiler_params=_cp_carry(1, carry),
    )(a, b, *after, *carry)
    return (outs[0], list(outs[1:])) if nc else outs[0]


def _wgrad_pool(p, dyb, n_groups):
    s, sw = p.shape
    d = dyb.shape[1]
    gw, go = sw // n_groups, d // n_groups
    ts = min(512, s)
    ns = s // ts

    def body(a_ref, b_ref, o_ref, acc_ref):
        i = pl.program_id(1)

        @pl.when(i == 0)
        def _():
            acc_ref[...] = jnp.zeros_like(acc_ref)

        acc_ref[...] += _dot(a_ref[...], b_ref[...], TN)

        @pl.when(i == ns - 1)
        def _():
            o_ref[...] = acc_ref[...].astype(BF16)

    return pl.pallas_call(
        body, name="wgrad_pool", grid=(n_groups, ns),
        in_specs=[pl.BlockSpec((ts, gw), lambda g, i: (i, g)),
                  pl.BlockSpec((ts, go), lambda g, i: (i, g))],
        out_specs=pl.BlockSpec((None, gw, go), lambda g, i: (g, 0, 0)),
        out_shape=jax.ShapeDtypeStruct((n_groups, gw, go), BF16),
        scratch_shapes=[pltpu.VMEM((gw, go), F32)],
        compiler_params=_cp(2),
    )(p, dyb)


def _wo_bwd(dx1b, wo, factors, sw, after=()):
    s, d = dx1b.shape
    tn = d // N_DEV
    nq = sw // tn

    def body(dx_ref, wo_ref, fya_ref, fyb_ref, fga_ref, fgb_ref, fsc_ref, *rest):
        dya_ref, dyb_ref, dp_ref, dbg_ref, dsc_ref, dm_ref = rest[len(after):]
        dbg_ref[...] = jnp.zeros_like(dbg_ref)
        dsc_ref[...] = jnp.zeros_like(dsc_ref)
        for rs in _chunks(s, 1024):
            dm_ref[rs, :] = _dot(dx_ref[rs, :], wo_ref[...], NT)
        for rs in _chunks(s, 256):
            dm = dm_ref[rs, :]
            dya_ref[rs, :] = (dm * fya_ref[rs, :].astype(F32)).astype(BF16)
            dyb_ref[rs, :] = (dm * fyb_ref[rs, :].astype(F32)).astype(BF16)
            dsc_ref[...] += jnp.sum(dm * fsc_ref[rs, :].astype(F32), axis=0, keepdims=True)
            dga = dm * fga_ref[rs, :].astype(F32)
            dgb = dm * fgb_ref[rs, :].astype(F32)
            dp_ref[0, rs, :] = dga.astype(BF16)
            dp_ref[1, rs, :] = dgb.astype(BF16)
            dbg_ref[0:1, :] += jnp.sum(dga, axis=0, keepdims=True)
            dbg_ref[1:2, :] += jnp.sum(dgb, axis=0, keepdims=True)

    col = pl.BlockSpec((s, tn), lambda j: (0, j))
    out = jax.ShapeDtypeStruct((s, d), BF16)
    return pl.pallas_call(
        body, name="wo_bwd", grid=(N_DEV,),
        in_specs=[pl.BlockSpec((s, d), lambda j: (0, 0)),
                  pl.BlockSpec((tn, d), lambda j: (j, 0))] + [col] * 5 + _after_specs(after),
        out_specs=[col, col,
                   pl.BlockSpec((2, None, s, tn), lambda j: (1, j // nq, 0, j % nq)),
                   pl.BlockSpec((2, tn), lambda j: (0, j)),
                   pl.BlockSpec((1, tn), lambda j: (0, j))],
        out_shape=[out, out, jax.ShapeDtypeStruct((4, 2, s, sw), BF16),
                   jax.ShapeDtypeStruct((2, d), F32), jax.ShapeDtypeStruct((1, d), F32)],
        scratch_shapes=[pltpu.VMEM((s, tn), F32)],
        compiler_params=_cp(1),
    )(dx1b, wo, *factors, *after)


def _conv_bwd(dproj, dya, wa, proj, conv_w, conv_b):
    s, d = dya.shape
    sw, tn = wa.shape[1], wa.shape[2]
    tc = min(LANES, sw)

    def body(dproj_hbm, dya_ref, wa_ref, ba_ref, ca_ref, va_ref, cw_ref, cb_ref,
             dp_ref, dcw_ref, dcb_ref, dz_ref):
        del dproj_hbm
        for rs in _chunks(s, 512):
            part = _dot(dya_ref[rs, 0:tn], wa_ref[0], NT)
            for j in range(1, N_DEV):
                part = part + _dot(dya_ref[rs, j * tn:(j + 1) * tn], wa_ref[j], NT)
            dz_ref[rs, :] = part
        dz = dz_ref[...]
        ba, ca, va = ba_ref[...], ca_ref[...], va_ref[...]
        cv = ca * va
        cv1, cv2 = _shift_down(cv, 1), _shift_down(cv, 2)
        w0, w1, w2 = cw_ref[0:1, :], cw_ref[1:2, :], cw_ref[2:3, :]
        u = cb_ref[...] + w0 * cv2 + w1 * cv1 + w2 * cv
        du = dz * ba
        dp_ref[0] = (dz * u).astype(BF16)
        dcv = w2 * du + w1 * _shift_up(du, 1) + w0 * _shift_up(du, 2)
        dp_ref[1] = (dcv * va).astype(BF16)
        dp_ref[2] = (dcv * ca).astype(BF16)
        dcw_ref[0:1, :] = jnp.sum(du * cv2, axis=0, keepdims=True)
        dcw_ref[1:2, :] = jnp.sum(du * cv1, axis=0, keepdims=True)
        dcw_ref[2:3, :] = jnp.sum(du * cv, axis=0, keepdims=True)
        dcb_ref[...] = jnp.sum(du, axis=0, keepdims=True)

    def part(k):
        return pl.BlockSpec((None, s, tc), lambda i: (k, 0, i))

    return pl.pallas_call(
        body, name="conv_bwd", grid=(sw // tc,),
        in_specs=[pl.BlockSpec(memory_space=pl.ANY),
                  pl.BlockSpec((s, d), lambda i: (0, 0)),
                  pl.BlockSpec((N_DEV, tc, tn), lambda i: (0, i, 0)),
                  part(0), part(1), part(2),
                  pl.BlockSpec((CONV_K, tc), lambda i: (0, i)), pl.BlockSpec((1, tc), lambda i: (0, i))],
        out_specs=[pl.BlockSpec((3, s, tc), lambda i: (0, 0, i)),
                   pl.BlockSpec((CONV_K, tc), lambda i: (0, i)), pl.BlockSpec((1, tc), lambda i: (0, i))],
        out_shape=[jax.ShapeDtypeStruct(dproj.shape, BF16),
                   jax.ShapeDtypeStruct((CONV_K, sw), F32), jax.ShapeDtypeStruct((1, sw), F32)],
        scratch_shapes=[pltpu.VMEM((s, tc), F32)],
        input_output_aliases={0: 0},
        compiler_params=_cp(1),
    )(dproj, dya, wa, proj, proj, proj, conv_w, conv_b)


def _pool_bwd(dproj, dyb, wpool):
    s, d = dyb.shape
    n_groups, gw, go = wpool.shape

    def body(dproj_hbm, dyb_ref, wp_ref, dp_ref):
        del dproj_hbm
        for gi, window in enumerate(POOL_WINDOWS):
            @pl.when(pl.program_id(0) == gi)
            def _():
                dpool = _dot(dyb_ref[...], wp_ref[...], NT)
                acc, k = dpool / _pool_counts(dpool.shape, window), 1
                while k < window:
                    acc = acc + _shift_up(acc, k)
                    k *= 2
                dp_ref[...] = (acc - dpool).astype(BF16)

    return pl.pallas_call(
        body, name="pool_bwd", grid=(n_groups,),
        in_specs=[pl.BlockSpec(memory_space=pl.ANY),
                  pl.BlockSpec((s, go), lambda g: (0, g)),
                  pl.BlockSpec((None, gw, go), lambda g: (g, 0, 0))],
        out_specs=pl.BlockSpec((None, s, gw), lambda g: (3, 0, g)),
        out_shape=jax.ShapeDtypeStruct(dproj.shape, BF16),
        input_output_aliases={0: 0},
        compiler_params=_cp(1),
    )(dproj, dyb, wpool)


def _rows128(v):
    return v.reshape(-1, LANES)


def kernel(x, norm1_g, w_in, b_gate, conv_w, conv_b, w_a_out, w_pool, pool_scale, w_o, norm2_g, w_ffn_gate, w_ffn_up, w_ffn_down, final_g, loss_target, m_norm1_g, m_w_in, m_b_gate, m_conv_w, m_conv_b, m_w_a_out, m_w_pool, m_pool_scale, m_w_o, m_norm2_g, m_w_ffn_gate, m_w_ffn_up, m_w_ffn_down, m_final_g, v_norm1_g, v_w_in, v_b_gate, v_conv_w, v_conv_b, v_w_a_out, v_w_pool, v_pool_scale, v_w_o, v_norm2_g, v_w_ffn_gate, v_w_ffn_up, v_w_ffn_down, v_final_g):
    s, d = x.shape[1], x.shape[2]
    sw = w_in.shape[2]
    n_groups = w_pool.shape[1]
    gw = w_pool.shape[2]
    go = w_pool.shape[3] * N_DEV
    f8 = w_ffn_gate.shape[2]
    cws = conv_w.shape[2]
    assert sw == conv_w.shape[2] * N_DEV == gw * n_groups and go * n_groups == d and n_groups == len(POOL_WINDOWS)

    xi, yi, ci = _coords()
    me = 4 * xi + 2 * yi + ci
    my_chip = 2 * xi + yi

    x2d = x.reshape(s, d)
    target = loss_target.reshape(s, d)
    final_g2 = final_g.reshape(1, d)
    b_gate2 = b_gate.reshape(2, d)

    big_names = ["w_in", "w_a_out", "w_pool", "w_o", "w_ffn_gate", "w_ffn_up", "w_ffn_down"]
    big_w = [w_in, w_a_out, w_pool, w_o, w_ffn_gate, w_ffn_up, w_ffn_down]
    big_m = [m_w_in, m_w_a_out, m_w_pool, m_w_o, m_w_ffn_gate, m_w_ffn_up, m_w_ffn_down]
    big_v = [v_w_in, v_w_a_out, v_w_pool, v_w_o, v_w_ffn_gate, v_w_ffn_up, v_w_ffn_down]
    shapes2d = [(w.size // w.shape[-1], w.shape[-1]) for w in big_w]
    big_w2 = [w.reshape(sh) for w, sh in zip(big_w, shapes2d)]
    transposed = (4, 5)

    def view2d(t, a):
        t2 = t.reshape(shapes2d[a])
        return t2.T if a in transposed else t2

    def unview(o, a):
        return (o.T if a in transposed else o).reshape(big_w[a].shape)

    sb = [_cast_bf16(w, "cast_" + nm) for w, nm in zip(big_w2, big_names)]
    win_g, wa_g, wpool_g, wo_g = _allgather_big(sb[0:4], "allgather_mixer", COLLECTIVE_GATHER)
    wg_g, wu_g = _allgather_big(sb[4:6], "allgather_ffn_up", COLLECTIVE_GATHER)
    (wd_g,) = _allgather_big(sb[6:7], "allgather_ffn_down", COLLECTIVE_GATHER)
    convw_g = _allgather_small(jnp.pad(conv_w.reshape(CONV_K, cws), ((0, 8 - CONV_K), (0, 0))), "allgather_conv_w")
    conv_w_full = convw_g[:, :CONV_K, :].transpose(1, 0, 2).reshape(CONV_K, sw)
    wpool = wpool_g.reshape(N_DEV, n_groups, gw, go // N_DEV).transpose(1, 2, 0, 3).reshape(n_groups, gw, go)
    wo = wo_g.reshape(d, d)

    h = _rms_fwd(x2d, norm1_g)
    proj = _proj_fwd(h, win_g)
    z = _conv_fwd(proj, conv_w_full, conv_b)
    p = _pool_fwd(proj)
    merged, *merge_factors = _merge_fwd(z, wa_g, p, wpool, proj, b_gate2, pool_scale)
    x1, h2 = _wo_fwd(merged, wo, x2d, norm2_g)
    dadu, dadg, act = _ffn_up_act_fwd(h2, wg_g, wu_g)
    ffn_out = _ffn_down_fwd(act, wd_g)
    dx2b, d_final_g, loss_blk = _loss_bwd(ffn_out, x1, target, final_g2)

    other_chips = jnp.stack([2 * (1 - xi) + yi, 2 * xi + (1 - yi), 2 * (1 - xi) + (1 - yi)])
    others = jnp.concatenate([other_chips, 2 * other_chips + ci]).astype(jnp.int32)

    def partials(grads, recvs, names):
        return [_chip_partial(others, g3, r, "chip_partial_" + nm) for g3, r, nm in zip(grads, recvs, names)]

    own = jnp.stack([me, my_chip]).astype(jnp.int32)

    def adam(a, g3, sib, chips):
        outs = _adam_big(own, view2d(big_w[a], a), view2d(big_m[a], a), view2d(big_v[a], a),
                         g3, sib, chips, "adam_" + big_names[a])
        return [unview(o, a) for o in outs]

    big_out = [None] * len(big_names)
    dg_act, du_act = _ffn_gate_bwd(dx2b, wd_g, dadg, dadu)
    gw_gate = _wgrad_rows(dg_act, h2, "wgrad_ffn_gate")
    gw_up = _wgrad_rows(du_act, h2, "wgrad_ffn_up")
    gw_down, sib_gu = _wgrad_rows(act, dx2b, "wgrad_ffn_down", carry=[gw_gate, gw_up])
    ps_gu = partials([gw_gate, gw_up], sib_gu, ["w_ffn_gate", "w_ffn_up"])
    chips_gu = _exchange_chips(ps_gu, "rs_chips_ffn_up", COLLECTIVE_CHIPS)
    dh2, sib_down = _input_grad([(dg_act, wg_g), (du_act, wu_g)], "ffn_in_bwd", after=ps_gu, carry=[gw_down])
    ps_down = partials([gw_down], sib_down, ["w_ffn_down"])
    chips_down = _exchange_chips(ps_down, "rs_chips_ffn_down", COLLECTIVE_CHIPS)
    dx1, dx1b, d_norm2_g = _rms_bwd(dh2, x1, norm2_g, dx2b, "rms2_bwd")
    dya, dyb, dproj42, d_b_gate, d_pool_scale = _wo_bwd(dx1b, wo, merge_factors, sw, after=ps_down)
    dproj = dproj42.reshape(N_DEV, s, sw)
    dproj, d_conv_w, d_conv_b = _conv_bwd(dproj, dya, wa_g, proj, conv_w_full, conv_b)
    dproj = _pool_bwd(dproj, dyb, wpool)
    gw_in = _wgrad_cols(h, dproj, "wgrad_in")
    gw_o, sib_in = _wgrad_full(merged, dx1b, "wgrad_o", carry=[gw_in])
    ps_in = partials([gw_in], sib_in, ["w_in"])
    chips_in = _exchange_chips(ps_in, "rs_chips_w_in", COLLECTIVE_CHIPS)
    gw_a = _wgrad_cols(z, dya, "wgrad_a_out", after=ps_in)
    gw_pool = _wgrad_pool(p, dyb, n_groups)
    mix3 = [gw_a,
            gw_pool.reshape(n_groups, gw, N_DEV, go // N_DEV).transpose(2, 0, 1, 3).reshape(N_DEV, n_groups * gw, go // N_DEV),
            gw_o.reshape(N_DEV, d // N_DEV, d)]
    big_out[4] = adam(4, gw_gate, sib_gu[0], chips_gu[0])
    big_out[5] = adam(5, gw_up, sib_gu[1], chips_gu[1])
    big_out[6] = adam(6, gw_down, sib_down[0], chips_down[0])
    dh, sib_mix = _input_grad([(dproj, win_g)], "proj_in_bwd",
                              after=[big_out[4][0], big_out[5][0], big_out[6][0]], carry=mix3)
    ps_mix = partials(mix3, sib_mix, ["w_a_out", "w_pool", "w_o"])
    chips_mix = _exchange_chips(ps_mix, "rs_chips_mixer", COLLECTIVE_CHIPS)
    grad_x, _, d_norm1_g = _rms_bwd(dh, x2d, norm1_g, dx1, "rms1_bwd", with_bf16=False)
    big_out[0] = adam(0, gw_in, sib_in[0], chips_in[0])
    for k in range(3):
        big_out[1 + k] = adam(1 + k, mix3[k], sib_mix[k], chips_mix[k])

    small_parts = [d_norm1_g, d_b_gate, d_conv_w, d_conv_b, d_pool_scale, d_norm2_g, d_final_g, loss_blk]
    sizes = [v.size for v in small_parts]
    packed = jnp.concatenate([_rows128(v) for v in small_parts], axis=0)
    summed = _sum_small(_allgather_small(packed, "allgather_small_grads")).reshape(-1)
    offs = [0]
    for n in sizes:
        offs.append(offs[-1] + n)
    g_norm1, g_bgate, g_convw_full, g_convb, g_pscale, g_norm2, g_final, loss_sum = [
        summed[offs[k]:offs[k + 1]] for k in range(len(sizes))]
    loss = loss_sum[0]
    g_convw = lax.dynamic_slice(g_convw_full.reshape(CONV_K, sw), (0, me * cws), (CONV_K, cws))
    small_w = [norm1_g, b_gate, conv_w, conv_b, pool_scale, norm2_g, final_g]
    small_m = [m_norm1_g, m_b_gate, m_conv_w, m_conv_b, m_pool_scale, m_norm2_g, m_final_g]
    small_v = [v_norm1_g, v_b_gate, v_conv_w, v_conv_b, v_pool_scale, v_norm2_g, v_final_g]
    small_g = [g_norm1, g_bgate, g_convw, g_convb, g_pscale, g_norm2, g_final]

    def pack(parts):
        flat = jnp.concatenate([v.reshape(-1) for v in parts])
        pad = (-flat.size) % (8 * LANES)
        return jnp.pad(flat, (0, pad)).reshape(-1, LANES)

    s_delta, s_m, s_v = _adam_small(pack(small_w), pack(small_g), pack(small_m), pack(small_v))
    soffs = [0]
    for w in small_w:
        soffs.append(soffs[-1] + w.size)

    def unpack(buf):
        flat = buf.reshape(-1)
        return [flat[soffs[k]:soffs[k + 1]].reshape(small_w[k].shape) for k in range(len(small_w))]

    small_grads = [g.reshape(w.shape) for g, w in zip(small_g, small_w)]
    small_delta, small_new_m, small_new_v = unpack(s_delta), unpack(s_m), unpack(s_v)

    order = ["norm1_g", "w_in", "b_gate", "conv_w", "conv_b", "w_a_out", "w_pool", "pool_scale", "w_o", "norm2_g",
             "w_ffn_gate", "w_ffn_up", "w_ffn_down", "final_g"]
    small_names = ["norm1_g", "b_gate", "conv_w", "conv_b", "pool_scale", "norm2_g", "final_g"]
    per_kind = [{}, {}, {}, {}]
    for a, nm in enumerate(big_names):
        for kind in range(4):
            per_kind[kind][nm] = big_out[a][kind]
    for k, nm in enumerate(small_names):
        per_kind[0][nm] = small_grads[k]
        per_kind[1][nm] = small_delta[k]
        per_kind[2][nm] = small_new_m[k]
        per_kind[3][nm] = small_new_v[k]
    result = [loss, grad_x.reshape(x.shape)]
    for kind in range(4):
        result += [per_kind[kind][nm] for nm in order]
    return tuple(result)
```

```python
import jax
import jax.numpy as jnp
from jax import lax
from jax.experimental import pallas as pl
from jax.experimental.pallas import tpu as pltpu
from jax.experimental.pallas import tpu_sc as plsc

F32 = jnp.float32
BF16 = jnp.bfloat16
MESH = pl.DeviceIdType.MESH

N_DEV = 8
EPS = 1e-6
CONV_K = 3
POOL_WINDOWS = (2, 4, 8, 16)
ADAM_LR = 0.001
ADAM_B1 = 0.9
ADAM_B2 = 0.999
ADAM_EPS = 1e-08
ADAM_WD = 0.01
ADAM_STEP = 10

V7X_VMEM_LIMIT_BYTES = 56 * 1024 * 1024
LANES = 128

COLLECTIVE_GATHER = 1
COLLECTIVE_SIBLING = 2
COLLECTIVE_CHIPS = 3
SEQUENCER_COST_BYTES = 4 * 10**9

NN = ((1,), (0,))
NT = ((1,), (1,))
TN = ((0,), (0,))


def _dot(a, b, dims):
    return lax.dot_general(a, b, (dims, ((), ())), preferred_element_type=F32)


def _cp(n_axes):
    return pltpu.CompilerParams(dimension_semantics=("arbitrary",) * n_axes,
                                vmem_limit_bytes=V7X_VMEM_LIMIT_BYTES)


def _row_tile(rows, bytes_per_row, cap_bytes):
    best = None
    for t in range(16, rows + 1, 16):
        if rows % t == 0 and t * bytes_per_row <= cap_bytes:
            best = t
    return best if best is not None else rows


def _chunks(total, size):
    size = min(size, total)
    assert total % size == 0
    return [slice(r, r + size) for r in range(0, total, size)]


def _after_specs(after):
    return [pl.BlockSpec(memory_space=pl.ANY)] * len(after)


def _shift_down(v, k):
    row = lax.broadcasted_iota(jnp.int32, v.shape, 0)
    return jnp.where(row >= k, pltpu.roll(v, k, 0), 0.0)


def _shift_up(v, k):
    n = v.shape[0]
    row = lax.broadcasted_iota(jnp.int32, v.shape, 0)
    return jnp.where(row < n - k, pltpu.roll(v, n - k, 0), 0.0)


def _sigmoid(v):
    return jax.nn.sigmoid(v)


def _cast_bf16(w2d, name):
    rows, cols = w2d.shape
    tr = _row_tile(rows, cols * 4, 2 << 20)

    def body(i_ref, o_ref):
        o_ref[...] = i_ref[...].astype(BF16)

    return pl.pallas_call(
        body, name=name, grid=(rows // tr,),
        in_specs=[pl.BlockSpec((tr, cols), lambda i: (i, 0))],
        out_specs=pl.BlockSpec((tr, cols), lambda i: (i, 0)),
        out_shape=jax.ShapeDtypeStruct((rows, cols), BF16),
        compiler_params=_cp(1),
    )(w2d)


def _rms_fwd(x2d, g):
    s, d = x2d.shape
    tm = min(256, s)

    def body(x_ref, g_ref, h_ref):
        xv = x_ref[...]
        r = lax.rsqrt(jnp.mean(xv * xv, axis=-1, keepdims=True) + EPS)
        h_ref[...] = (xv * r * g_ref[...]).astype(BF16)

    return pl.pallas_call(
        body, name="rms1_fwd", grid=(s // tm,),
        in_specs=[pl.BlockSpec((tm, d), lambda i: (i, 0)), pl.BlockSpec((1, d), lambda i: (0, 0))],
        out_specs=pl.BlockSpec((tm, d), lambda i: (i, 0)),
        out_shape=jax.ShapeDtypeStruct((s, d), BF16),
        compiler_params=_cp(1),
    )(x2d, g)


def _coords():
    return lax.axis_index("x"), lax.axis_index("y"), lax.axis_index("c")


def _slot(p):
    return 4 * p[0] + 2 * p[1] + p[2]


def _handshake(peers):
    barrier = pltpu.get_barrier_semaphore()
    for peer in peers:
        pl.semaphore_signal(barrier, inc=1, device_id=peer, device_id_type=MESH)
    pl.semaphore_wait(barrier, len(peers))


def _sequencer_call(body, out_type, scratch_types, name, collective_id):
    return pl.kernel(
        body, out_type=out_type, name=name,
        mesh=plsc.ScalarSubcoreMesh(axis_name="seq", num_cores=1),
        scratch_types=scratch_types,
        cost_estimate=pl.CostEstimate(flops=0, transcendentals=0, bytes_accessed=SEQUENCER_COST_BYTES),
        compiler_params=pltpu.CompilerParams(collective_id=collective_id))


def _allgather_big(shards, name, collective_id, after=()):
    n = len(shards)

    def body(*refs):
        ins, outs = refs[:n], refs[n + len(after):2 * n + len(after)]
        send_sems, recv_sems, local_sems = refs[2 * n + len(after):]
        x, y, c = _coords()
        me, sibling = (x, y, c), (x, y, 1 - c)
        x_nbr, y_nbr, diag = (1 - x, y), (x, 1 - y), (1 - x, 1 - y)
        relay_from = (x + (1 - c) * (1 - 2 * x), y + c * (1 - 2 * y))
        relay_to = (x + c * (1 - 2 * x), y + (1 - c) * (1 - 2 * y))
        _handshake([sibling, (*x_nbr, c), (*y_nbr, c)])

        def copy(a, k, block, to, src=None):
            dst = outs[a].at[_slot(block)]
            return pltpu.make_async_remote_copy(
                src_ref=dst if src is None else src, dst_ref=dst,
                send_sem=send_sems.at[a, k], recv_sem=recv_sems.at[a, k],
                device_id=to, device_id_type=MESH)

        mine, sends = [], []
        for a in range(n):
            cp = pltpu.make_async_copy(ins[a], outs[a].at[_slot(me)], local_sems.at[a])
            cp.start()
            mine.append(cp)
            first = [copy(a, 0, me, sibling, src=ins[a]),
                     copy(a, 1, me, (*x_nbr, c), src=ins[a]),
                     copy(a, 2, me, (*y_nbr, c), src=ins[a])]
            for cp in first:
                cp.start()
            sends += first
        for a in range(n):
            copy(a, 1 + c, (*relay_from, c), me).wait_recv()
            passed = [copy(a, 3, (*relay_from, c), (*relay_to, c)), copy(a, 4 + c, (*relay_from, c), sibling)]
            for cp in passed:
                cp.start()
            copy(a, 2 - c, (*relay_to, c), me).wait_recv()
            cp = copy(a, 5 - c, (*relay_to, c), sibling)
            cp.start()
            passed.append(cp)
            copy(a, 3, (*diag, c), me).wait_recv()
            cp = copy(a, 6, (*diag, c), sibling)
            cp.start()
            sends += passed + [cp]
        for a in range(n):
            copy(a, 0, sibling, me).wait_recv()
            copy(a, 4, (*x_nbr, 1 - c), me).wait_recv()
            copy(a, 5, (*y_nbr, 1 - c), me).wait_recv()
            copy(a, 6, (*diag, 1 - c), me).wait_recv()
        for cp in sends:
            cp.wait_send()
        for cp in mine:
            cp.wait()

    return _sequencer_call(
        body, [jax.ShapeDtypeStruct((N_DEV,) + s.shape, s.dtype) for s in shards],
        [pltpu.SemaphoreType.DMA((n, 7)), pltpu.SemaphoreType.DMA((n, 7)), pltpu.SemaphoreType.DMA((n,))],
        name, collective_id)(*shards, *after)


def _sibling_copies(ins, recvs, send_sems, recv_sems):
    x, y, c = _coords()
    return [pltpu.make_async_remote_copy(
        src_ref=ins[a].at[2 * q + (1 - c)], dst_ref=recvs[a].at[q],
        send_sem=send_sems.at[a, q], recv_sem=recv_sems.at[a, q],
        device_id=(x, y, 1 - c), device_id_type=MESH) for a in range(len(ins)) for q in range(4)]


def _carry_specs(carry):
    any_spec = pl.BlockSpec(memory_space=pl.ANY)
    n = len(carry)
    sems = [pltpu.SemaphoreType.DMA((n, 4)), pltpu.SemaphoreType.DMA((n, 4))] if n else []
    return ([any_spec] * n, [any_spec] * n,
            [jax.ShapeDtypeStruct((4,) + g.shape[1:], g.dtype) for g in carry], sems)


def _carry_run(first, last, ins, recvs, sems):
    if not ins:
        return

    @pl.when(first)
    def _():
        x, y, c = _coords()
        _handshake([(x, y, 1 - c)])
        for cp in _sibling_copies(ins, recvs, *sems):
            cp.start()

    @pl.when(last)
    def _():
        copies = _sibling_copies(ins, recvs, *sems)
        for cp in copies:
            cp.wait_recv()
        for cp in copies:
            cp.wait_send()


def _cp_carry(n_axes, carry):
    if not carry:
        return _cp(n_axes)
    return pltpu.CompilerParams(dimension_semantics=("arbitrary",) * n_axes, vmem_limit_bytes=V7X_VMEM_LIMIT_BYTES,
                                collective_id=COLLECTIVE_SIBLING)


def _exchange_chips(psums, name, collective_id):
    n = len(psums)

    def body(*refs):
        ins, outs = refs[:n], refs[n:2 * n]
        send_sems, recv_sems = refs[2 * n:]
        x, y, c = _coords()
        chips = [(1 - x, y), (x, 1 - y), (1 - x, 1 - y)]
        _handshake([(*chip, c) for chip in chips])
        copies = []
        for a in range(n):
            for j, chip in enumerate(chips):
                cp = pltpu.make_async_remote_copy(
                    src_ref=ins[a].at[2 * chip[0] + chip[1]], dst_ref=outs[a].at[j],
                    send_sem=send_sems.at[a, j], recv_sem=recv_sems.at[a, j],
                    device_id=(*chip, c), device_id_type=MESH)
                cp.start()
                copies.append(cp)
        for cp in copies:
            cp.wait_recv()
        for cp in copies:
            cp.wait_send()

    return _sequencer_call(
        body, [jax.ShapeDtypeStruct((3,) + p.shape[1:], p.dtype) for p in psums],
        [pltpu.SemaphoreType.DMA((n, 3)), pltpu.SemaphoreType.DMA((n, 3))],
        name, collective_id)(*psums)


def _allgather_small(v2d, name):
    rows, cols = v2d.shape

    def body(v_ref, out_ref, send_sems, recv_sems):
        x, y, c = _coords()
        me = (x, y, c)
        out_ref[_slot(me)] = v_ref[...]
        peers = []
        for k in range(1, N_DEV):
            fx, fy, fc = (k >> 2) & 1, (k >> 1) & 1, k & 1
            peers.append(((1 - x) if fx else x, (1 - y) if fy else y, (1 - c) if fc else c))
        sends = []
        for k, peer in enumerate(peers):
            cp = pltpu.make_async_remote_copy(
                src_ref=v_ref, dst_ref=out_ref.at[_slot(me)],
                send_sem=send_sems.at[k], recv_sem=recv_sems.at[k],
                device_id=peer, device_id_type=MESH)
            cp.start()
            sends.append(cp)
        for k, peer in enumerate(peers):
            pltpu.make_async_remote_copy(
                src_ref=v_ref, dst_ref=out_ref.at[_slot(peer)],
                send_sem=send_sems.at[k], recv_sem=recv_sems.at[k],
                device_id=peer, device_id_type=MESH).wait_recv()
        for cp in sends:
            cp.wait_send()

    vmem = pl.BlockSpec(memory_space=pltpu.VMEM)
    return pl.pallas_call(
        body, name=name, in_specs=[vmem], out_specs=vmem,
        out_shape=jax.ShapeDtypeStruct((N_DEV, rows, cols), v2d.dtype),
        scratch_shapes=[pltpu.SemaphoreType.DMA((N_DEV - 1,)), pltpu.SemaphoreType.DMA((N_DEV - 1,))],
    )(v2d)


def _chip_partial(others, g3, recv, name):
    _, rows, cols = g3.shape
    tr = _row_tile(rows, cols * 2, 2 << 20)

    def body(others_ref, g_ref, r_ref, o_ref):
        o_ref[...] = (g_ref[...].astype(F32) + r_ref[...].astype(F32)).astype(BF16)

    return pl.pallas_call(
        body, name=name,
        grid_spec=pltpu.PrefetchScalarGridSpec(
            num_scalar_prefetch=1, grid=(3, rows // tr),
            in_specs=[pl.BlockSpec((None, tr, cols), lambda k, i, o: (o[3 + k], i, 0)),
                      pl.BlockSpec((None, tr, cols), lambda k, i, o: (o[k], i, 0))],
            out_specs=pl.BlockSpec((None, tr, cols), lambda k, i, o: (o[k], i, 0))),
        out_shape=jax.ShapeDtypeStruct((4, rows, cols), BF16),
        compiler_params=_cp(2),
    )(others, g3, recv)


def _adam_math(w, g, m, v):
    m = ADAM_B1 * m + (1.0 - ADAM_B1) * g
    v = ADAM_B2 * v + (1.0 - ADAM_B2) * (g * g)
    m_hat = m / (1.0 - ADAM_B1 ** ADAM_STEP)
    v_hat = v / (1.0 - ADAM_B2 ** ADAM_STEP)
    delta = -ADAM_LR * (m_hat / (jnp.sqrt(v_hat) + ADAM_EPS) + ADAM_WD * w)
    return delta, m, v


def _adam_big(own, w, m, v, g3, recv_sib, recv_chips, name):
    rows, cols = w.shape
    tr = _row_tile(rows, cols * 4, 2 << 20)

    def body(own_ref, w_ref, m_ref, v_ref, g_ref, rs_ref, rc_ref, go_ref, do_ref, mo_ref, vo_ref):
        g = g_ref[...].astype(F32) + rs_ref[...].astype(F32)
        g = g + rc_ref[0].astype(F32)
        g = g + rc_ref[1].astype(F32)
        g = g + rc_ref[2].astype(F32)
        delta, m_new, v_new = _adam_math(w_ref[...], g, m_ref[...], v_ref[...])
        go_ref[...] = g
        do_ref[...] = delta
        mo_ref[...] = m_new
        vo_ref[...] = v_new

    blk = pl.BlockSpec((tr, cols), lambda i, o: (i, 0))
    out = jax.ShapeDtypeStruct((rows, cols), F32)
    return pl.pallas_call(
        body, name=name,
        grid_spec=pltpu.PrefetchScalarGridSpec(
            num_scalar_prefetch=1, grid=(rows // tr,),
            in_specs=[blk, blk, blk,
                      pl.BlockSpec((None, tr, cols), lambda i, o: (o[0], i, 0)),
                      pl.BlockSpec((None, tr, cols), lambda i, o: (o[1], i, 0)),
                      pl.BlockSpec((3, tr, cols), lambda i, o: (0, i, 0))],
            out_specs=[blk, blk, blk, blk]),
        out_shape=[out, out, out, out],
        compiler_params=_cp(1),
    )(own, w, m, v, g3, recv_sib, recv_chips)


def _small_finish(gathered, params, row_offs, extra_rows):
    n = len(params)

    def body(g_ref, *refs):
        ins, outs = refs[:3 * n], refs[3 * n:]
        total = g_ref[0]
        for k in range(1, N_DEV):
            total = total + g_ref[k]
        for e, (r0, nr) in enumerate(extra_rows):
            outs[e][...] = total[r0:r0 + nr, :]
        for p in range(n):
            w_ref, m_ref, v_ref = ins[3 * p:3 * p + 3]
            g_out, d_out, m_out, v_out = outs[len(extra_rows) + 4 * p:len(extra_rows) + 4 * p + 4]
            g = total[row_offs[p]:row_offs[p] + w_ref.shape[0], :]
            delta, m_new, v_new = _adam_math(w_ref[...], g, m_ref[...], v_ref[...])
            g_out[...] = g
            d_out[...] = delta
            m_out[...] = m_new
            v_out[...] = v_new

    vmem = pl.BlockSpec(memory_space=pltpu.VMEM)
    out_shape = [jax.ShapeDtypeStruct((nr, LANES), F32) for _, nr in extra_rows]
    for w, _, _ in params:
        out_shape += [jax.ShapeDtypeStruct(w.shape, F32)] * 4
    flat = [t for wmv in params for t in wmv]
    return pl.pallas_call(body, name="small_finish", in_specs=[vmem] * (1 + len(flat)),
                          out_specs=[vmem] * len(out_shape), out_shape=out_shape)(gathered, *flat)


def _adam_small(w, g, m, v):
    def body(w_ref, g_ref, m_ref, v_ref, do_ref, mo_ref, vo_ref):
        delta, m_new, v_new = _adam_math(w_ref[...], g_ref[...], m_ref[...], v_ref[...])
        do_ref[...] = delta
        mo_ref[...] = m_new
        vo_ref[...] = v_new

    vmem = pl.BlockSpec(memory_space=pltpu.VMEM)
    out = jax.ShapeDtypeStruct(w.shape, F32)
    return pl.pallas_call(body, name="adam_small", in_specs=[vmem] * 4, out_specs=[vmem] * 3,
                          out_shape=[out, out, out])(w, g, m, v)


def _proj_fwd(h, win_g):
    s, d = h.shape
    sw = win_g.shape[2]
    tn = min(512, sw)
    nh = sw // tn

    def body(h_ref, w_ref, o_ref):
        for rs in _chunks(s, 512):
            o_ref[rs, :] = _dot(h_ref[rs, :], w_ref[...], NN)

    return pl.pallas_call(
        body, name="proj_fwd", grid=(N_DEV * nh,),
        in_specs=[pl.BlockSpec((s, d), lambda j: (0, 0)),
                  pl.BlockSpec((None, d, tn), lambda j: (j // nh, 0, j % nh))],
        out_specs=pl.BlockSpec((None, s, tn), lambda j: (j // nh, 0, j % nh)),
        out_shape=jax.ShapeDtypeStruct((N_DEV, s, sw), F32),
        compiler_params=_cp(1),
    )(h, win_g)


def _conv_fwd(proj, conv_w, conv_b):
    _, s, sw = proj.shape
    tc = min(LANES, sw)

    def body(ba_ref, ca_ref, va_ref, cw_ref, cb_ref, z_ref):
        cv = ca_ref[...] * va_ref[...]
        u = (cb_ref[...] + cw_ref[0:1, :] * _shift_down(cv, 2) + cw_ref[1:2, :] * _shift_down(cv, 1)
             + cw_ref[2:3, :] * cv)
        z_ref[...] = (ba_ref[...] * u).astype(BF16)

    def part(k):
        return pl.BlockSpec((None, s, tc), lambda i: (k, 0, i))

    return pl.pallas_call(
        body, name="conv_fwd", grid=(sw // tc,),
        in_specs=[part(0), part(1), part(2),
                  pl.BlockSpec((CONV_K, tc), lambda i: (0, i)), pl.BlockSpec((1, tc), lambda i: (0, i))],
        out_specs=pl.BlockSpec((s, tc), lambda i: (0, i)),
        out_shape=jax.ShapeDtypeStruct((s, sw), BF16),
        compiler_params=_cp(1),
    )(proj, proj, proj, conv_w, conv_b)


def _pool_counts(shape, window):
    t = lax.broadcasted_iota(jnp.int32, shape, 0)
    return jnp.minimum(t + 1, window).astype(F32)


def _pool_fwd(proj):
    _, s, sw = proj.shape
    gw = sw // len(POOL_WINDOWS)

    def body(v_ref, p_ref):
        for gi, window in enumerate(POOL_WINDOWS):
            @pl.when(pl.program_id(0) == gi)
            def _():
                v = v_ref[...]
                acc, k = v, 1
                while k < window:
                    acc = acc + _shift_down(acc, k)
                    k *= 2
                p_ref[...] = (acc / _pool_counts(v.shape, window) - v).astype(BF16)

    return pl.pallas_call(
        body, name="pool_fwd", grid=(len(POOL_WINDOWS),),
        in_specs=[pl.BlockSpec((None, s, gw), lambda g: (3, 0, g))],
        out_specs=pl.BlockSpec((s, gw), lambda g: (0, g)),
        out_shape=jax.ShapeDtypeStruct((s, sw), BF16),
        compiler_params=_cp(1),
    )(proj)


def _merge_fwd(z, wa, p, wpool, proj, b_gate2, pool_scale):
    s, sw = z.shape
    tn = wa.shape[2]
    d = tn * N_DEV
    gw = sw // len(POOL_WINDOWS)
    nq = sw // tn

    def body(z_ref, wa_ref, p_ref, wp_ref, ga_ref, gb_ref, bg_ref, sc_ref,
             m_ref, dya_ref, dyb_ref, dga_ref, dgb_ref, dsc_ref):
        for rs in _chunks(s, 512):
            ya = _dot(z_ref[rs, :], wa_ref[...], NN)
            yb = _dot(p_ref[rs, :], wp_ref[...], NN)
            sa = _sigmoid(ga_ref[rs, :] + bg_ref[0:1, :])
            sb = _sigmoid(gb_ref[rs, :] + bg_ref[1:2, :])
            sc = sc_ref[...]
            sb_yb = sb * yb
            m_ref[rs, :] = (sa * ya + sb_yb * sc).astype(BF16)
            dya_ref[rs, :] = sa.astype(BF16)
            dyb_ref[rs, :] = (sb * sc).astype(BF16)
            dga_ref[rs, :] = (ya * (sa * (1.0 - sa))).astype(BF16)
            dgb_ref[rs, :] = ((yb * sc) * (sb * (1.0 - sb))).astype(BF16)
            dsc_ref[rs, :] = sb_yb.astype(BF16)

    col = pl.BlockSpec((s, tn), lambda j: (0, j))
    out = jax.ShapeDtypeStruct((s, d), BF16)
    return pl.pallas_call(
        body, name="merge_fwd", grid=(N_DEV,),
        in_specs=[pl.BlockSpec((s, sw), lambda j: (0, 0)),
                  pl.BlockSpec((None, sw, tn), lambda j: (j, 0, 0)),
                  pl.BlockSpec((s, gw), lambda j: (0, j // 2)),
                  pl.BlockSpec((None, gw, tn), lambda j: (j // 2, 0, j % 2)),
                  pl.BlockSpec((None, s, tn), lambda j: (4 + j // nq, 0, j % nq)),
                  pl.BlockSpec((None, s, tn), lambda j: (6 + j // nq, 0, j % nq)),
                  pl.BlockSpec((2, tn), lambda j: (0, j)),
                  pl.BlockSpec((1, tn), lambda j: (0, j))],
        out_specs=[col] * 6,
        out_shape=[out] * 6,
        compiler_params=_cp(1),
    )(z, wa, p, wpool, proj, proj, b_gate2, pool_scale)


def _wo_fwd(merged, wo, x2d, g2):
    s, d = x2d.shape
    tm = min(256, s)

    def body(m_ref, wo_ref, x_ref, g_ref, x1_ref, h2_ref):
        x1 = x_ref[...] + _dot(m_ref[...], wo_ref[...], NN)
        x1_ref[...] = x1
        r = lax.rsqrt(jnp.mean(x1 * x1, axis=-1, keepdims=True) + EPS)
        h2_ref[...] = (x1 * r * g_ref[...]).astype(BF16)

    row = pl.BlockSpec((tm, d), lambda i: (i, 0))
    return pl.pallas_call(
        body, name="wo_fwd", grid=(s // tm,),
        in_specs=[row, pl.BlockSpec((d, d), lambda i: (0, 0)), row, pl.BlockSpec((1, d), lambda i: (0, 0))],
        out_specs=[row, row],
        out_shape=[jax.ShapeDtypeStruct((s, d), F32), jax.ShapeDtypeStruct((s, d), BF16)],
        compiler_params=_cp(1),
    )(merged, wo, x2d, g2)


def _ffn_up_act_fwd(h2, wg_g, wu_g):
    s, d = h2.shape
    f8 = wg_g.shape[2]
    th = min(1024, s)

    def body(h_ref, wg_ref, wu_ref, dadu_ref, dadg_ref, a_ref):
        i = pl.program_id(1)
        for rs in _chunks(th, 512):
            rows = pl.ds(pl.multiple_of(i * th + rs.start, rs.stop - rs.start), rs.stop - rs.start)
            a = h_ref[rows, :]
            g = _dot(a, wg_ref[...], NN)
            u = _dot(a, wu_ref[...], NN)
            sg = _sigmoid(g)
            silu = g * sg
            dadu_ref[rs, :] = silu.astype(BF16)
            dadg_ref[rs, :] = (u * (sg * (1.0 + g * (1.0 - sg)))).astype(BF16)
            a_ref[rs, :] = (silu * u).astype(BF16)

    wspec = pl.BlockSpec((None, d, f8), lambda j, i: (j, 0, 0))
    ospec = pl.BlockSpec((None, th, f8), lambda j, i: (j, i, 0))
    out = jax.ShapeDtypeStruct((N_DEV, s, f8), BF16)
    return pl.pallas_call(
        body, name="ffn_up_fwd", grid=(N_DEV, s // th),
        in_specs=[pl.BlockSpec((s, d), lambda j, i: (0, 0)), wspec, wspec],
        out_specs=[ospec, ospec, ospec], out_shape=[out, out, out],
        compiler_params=_cp(2),
    )(h2, wg_g, wu_g)


def _ffn_down_fwd(act, wd_g):
    _, s, f8 = act.shape
    d = wd_g.shape[2]
    tn = min(1024, d)

    def body(a_ref, wd_ref, o_ref):
        j = pl.program_id(1)

        @pl.when(j == 0)
        def _():
            o_ref[...] = jnp.zeros_like(o_ref)

        for rs in _chunks(s, 1024):
            o_ref[rs, :] += _dot(a_ref[rs, :], wd_ref[...], NN)

    return pl.pallas_call(
        body, name="ffn_down_fwd", grid=(d // tn, N_DEV),
        in_specs=[pl.BlockSpec((None, s, f8), lambda n, j: (j, 0, 0)),
                  pl.BlockSpec((None, f8, tn), lambda n, j: (j, 0, n))],
        out_specs=pl.BlockSpec((s, tn), lambda n, j: (0, n)),
        out_shape=jax.ShapeDtypeStruct((s, d), F32),
        compiler_params=_cp(2),
    )(act, wd_g)


def _loss_bwd(ffn_out, x1, target, final_g):
    s, d = x1.shape
    tm = min(256, s)

    def body(f_ref, x1_ref, t_ref, gf_ref, dxb_ref, dgf_ref, loss_ref):
        @pl.when(pl.program_id(0) == 0)
        def _():
            dgf_ref[...] = jnp.zeros_like(dgf_ref)
            loss_ref[...] = jnp.zeros_like(loss_ref)

        x2 = x1_ref[...] + f_ref[...]
        r = lax.rsqrt(jnp.mean(x2 * x2, axis=-1, keepdims=True) + EPS)
        nrm = x2 * r
        gf = gf_ref[...]
        err = nrm * gf - t_ref[...]
        loss_ref[...] += jnp.sum(err * err) * (0.5 / d)
        dy = err * (1.0 / d)
        dgf_ref[...] += jnp.sum(dy * nrm, axis=0, keepdims=True)
        dn = dy * gf
        dx = r * (dn - nrm * jnp.mean(dn * nrm, axis=-1, keepdims=True))
        dxb_ref[...] = dx.astype(BF16)

    row = pl.BlockSpec((tm, d), lambda i: (i, 0))
    vec = pl.BlockSpec((1, d), lambda i: (0, 0))
    return pl.pallas_call(
        body, name="loss_bwd", grid=(s // tm,),
        in_specs=[row, row, row, vec],
        out_specs=[row, vec, pl.BlockSpec((8, LANES), lambda i: (0, 0))],
        out_shape=[jax.ShapeDtypeStruct((s, d), BF16),
                   jax.ShapeDtypeStruct((1, d), F32), jax.ShapeDtypeStruct((8, LANES), F32)],
        compiler_params=_cp(1),
    )(ffn_out, x1, target, final_g)


def _ffn_gate_bwd(dx2b, wd_g, dadg, dadu):
    s, d = dx2b.shape
    f8 = dadg.shape[2]
    th = min(1024, s)

    def body(dx_ref, wd_ref, g_ref, u_ref, dg_ref, du_ref, da_ref):
        i = pl.program_id(1)
        chunks = _chunks(th, 256)

        def matmul(rs):
            rows = pl.ds(pl.multiple_of(i * th + rs.start, rs.stop - rs.start), rs.stop - rs.start)
            da_ref[rs, :] = _dot(dx_ref[rows, :], wd_ref[...], NT)

        matmul(chunks[0])
        for k, rs in enumerate(chunks):
            if k + 1 < len(chunks):
                matmul(chunks[k + 1])
            da = da_ref[rs, :]
            dg_ref[rs, :] = (da * g_ref[rs, :].astype(F32)).astype(BF16)
            du_ref[rs, :] = (da * u_ref[rs, :].astype(F32)).astype(BF16)

    aspec = pl.BlockSpec((None, th, f8), lambda j, i: (j, i, 0))
    out = jax.ShapeDtypeStruct((N_DEV, s, f8), BF16)
    return pl.pallas_call(
        body, name="ffn_act_bwd", grid=(N_DEV, s // th),
        in_specs=[pl.BlockSpec((s, d), lambda j, i: (0, 0)),
                  pl.BlockSpec((None, f8, d), lambda j, i: (j, 0, 0)), aspec, aspec],
        out_specs=[aspec, aspec], out_shape=[out, out],
        scratch_shapes=[pltpu.VMEM((th, f8), F32)],
        compiler_params=_cp(2),
    )(dx2b, wd_g, dadg, dadu)


def _wgrad_rows(a3, b, name, after=(), carry=()):
    _, s, k = a3.shape
    n = b.shape[1]
    nc = len(carry)
    c_in, c_out, c_shape, c_sems = _carry_specs(carry)

    def body(a_ref, b_ref, *rest):
        rest = rest[len(after):]
        o_ref = rest[nc]
        j = pl.program_id(0)
        _carry_run(j == 0, j == N_DEV - 1, rest[:nc], rest[nc + 1:2 * nc + 1], rest[2 * nc + 1:])
        o_ref[...] = _dot(a_ref[...], b_ref[...], TN).astype(BF16)

    outs = pl.pallas_call(
        body, name=name, grid=(N_DEV,),
        in_specs=[pl.BlockSpec((None, s, k), lambda j: (j, 0, 0)),
                  pl.BlockSpec((s, n), lambda j: (0, 0))] + _after_specs(after) + c_in,
        out_specs=[pl.BlockSpec((None, k, n), lambda j: (j, 0, 0))] + c_out,
        out_shape=[jax.ShapeDtypeStruct((N_DEV, k, n), BF16)] + c_shape,
        scratch_shapes=c_sems,
        compiler_params=_cp_carry(1, carry),
    )(a3, b, *after, *carry)
    return (outs[0], list(outs[1:])) if nc else outs[0]


def _wgrad_cols(a, b3, name, after=()):
    s, k = a.shape
    if b3.ndim == 2:
        n = b3.shape[1] // N_DEV
        b_spec = pl.BlockSpec((s, n), lambda j: (0, j))
    else:
        n = b3.shape[2]
        b_spec = pl.BlockSpec((None, s, n), lambda j: (j, 0, 0))

    def body(a_ref, b_ref, *rest):
        o_ref = rest[len(after)]
        o_ref[...] = _dot(a_ref[...], b_ref[...], TN).astype(BF16)

    return pl.pallas_call(
        body, name=name, grid=(N_DEV,),
        in_specs=[pl.BlockSpec((s, k), lambda j: (0, 0)), b_spec] + _after_specs(after),
        out_specs=pl.BlockSpec((None, k, n), lambda j: (j, 0, 0)),
        out_shape=jax.ShapeDtypeStruct((N_DEV, k, n), BF16),
        compiler_params=_cp(1),
    )(a, b3, *after)


def _input_grad(pairs, name, after=(), carry=()):
    s = pairs[0][0].shape[1]
    d = pairs[0][1].shape[1]
    tn = min(1024, d)
    npair = len(pairs)
    nc = len(carry)
    c_in, c_out, c_shape, c_sems = _carry_specs(carry)

    def body(*refs):
        ops = refs[:2 * npair]
        rest = refs[2 * npair + len(after):]
        o_ref = rest[nc]
        nh, j = pl.program_id(0), pl.program_id(1)
        _carry_run((nh == 0) & (j == 0), (nh == d // tn - 1) & (j == N_DEV - 1),
                   rest[:nc], rest[nc + 1:2 * nc + 1], rest[2 * nc + 1:])

        @pl.when(j == 0)
        def _():
            o_ref[...] = jnp.zeros_like(o_ref)

        for rs in _chunks(s, 1024):
            part = _dot(ops[0][rs, :], ops[1][...], NT)
            for q in range(1, npair):
                part = part + _dot(ops[2 * q][rs, :], ops[2 * q + 1][...], NT)
            o_ref[rs, :] += part

    in_specs, args = [], []
    for a3, w3 in pairs:
        k = a3.shape[2]
        in_specs += [pl.BlockSpec((None, s, k), lambda n, j: (j, 0, 0)),
                     pl.BlockSpec((None, tn, k), lambda n, j: (j, n, 0))]
        args += [a3, w3]
    outs = pl.pallas_call(
        body, name=name, grid=(d // tn, N_DEV),
        in_specs=in_specs + _after_specs(after) + c_in,
        out_specs=[pl.BlockSpec((s, tn), lambda n, j: (0, n))] + c_out,
        out_shape=[jax.ShapeDtypeStruct((s, d), F32)] + c_shape,
        scratch_shapes=c_sems,
        compiler_params=_cp_carry(2, carry),
    )(*args, *after, *carry)
    return (outs[0], list(outs[1:])) if nc else outs[0]


def _rms_bwd(dh, xres, g, dres, name, with_bf16=True):
    s, d = xres.shape
    tm = min(256, s)

    def body(dh_ref, x_ref, g_ref, dres_ref, dx_ref, *rest):
        dg_ref = rest[-1]
        @pl.when(pl.program_id(0) == 0)
        def _():
            dg_ref[...] = jnp.zeros_like(dg_ref)

        xv = x_ref[...]
        dh_v = dh_ref[...]
        r = lax.rsqrt(jnp.mean(xv * xv, axis=-1, keepdims=True) + EPS)
        nrm = xv * r
        dg_ref[...] += jnp.sum(dh_v * nrm, axis=0, keepdims=True)
        dn = dh_v * g_ref[...]
        dx = dres_ref[...].astype(F32) + r * (dn - nrm * jnp.mean(dn * nrm, axis=-1, keepdims=True))
        dx_ref[...] = dx
        if with_bf16:
            rest[0][...] = dx.astype(BF16)

    row = pl.BlockSpec((tm, d), lambda i: (i, 0))
    vec = pl.BlockSpec((1, d), lambda i: (0, 0))
    copies = [jax.ShapeDtypeStruct((s, d), BF16)] if with_bf16 else []
    outs = pl.pallas_call(
        body, name=name, grid=(s // tm,),
        in_specs=[row, row, vec, row],
        out_specs=[row] + [row] * len(copies) + [vec],
        out_shape=[jax.ShapeDtypeStruct((s, d), F32)] + copies + [jax.ShapeDtypeStruct((1, d), F32)],
        compiler_params=_cp(1),
    )(dh, xres, g, dres)
    return (outs[0], outs[1], outs[2]) if with_bf16 else (outs[0], None, outs[1])


def _wgrad_full(a, b, name, after=(), carry=()):
    s, k = a.shape
    n = b.shape[1]
    tk = min(512, k)
    nc = len(carry)
    c_in, c_out, c_shape, c_sems = _carry_specs(carry)

    def body(a_ref, b_ref, *rest):
        rest = rest[len(after):]
        o_ref = rest[nc]
        j = pl.program_id(0)
        _carry_run(j == 0, j == k // tk - 1, rest[:nc], rest[nc + 1:2 * nc + 1], rest[2 * nc + 1:])
        o_ref[...] = _dot(a_ref[...], b_ref[...], TN).astype(BF16)

    outs = pl.pallas_call(
        body, name=name, grid=(k // tk,),
        in_specs=[pl.BlockSpec((s, tk), lambda j: (0, j)),
                  pl.BlockSpec((s, n), lambda j: (0, 0))] + _after_specs(after) + c_in,
        out_specs=[pl.BlockSpec((tk, n), lambda j: (j, 0))] + c_out,
        out_shape=[jax.ShapeDtypeStruct((k, n), BF16)] + c_shape,
        scratch_shapes=c_sems,
        compiler_params=_cp_carry(1, carry),
    )(a, b, *after, *carry)
    return (outs[0], list(outs[1:])) if nc else outs[0]


def _wgrad_pool(p, dyb, n_groups):
    s, sw = p.shape
    d = dyb.shape[1]
    gw, go = sw // n_groups, d // n_groups
    ts = min(512, s)
    ns = s // ts

    def body(a_ref, b_ref, o_ref, acc_ref):
        i = pl.program_id(1)

        @pl.when(i == 0)
        def _():
            acc_ref[...] = jnp.zeros_like(acc_ref)

        acc_ref[...] += _dot(a_ref[...], b_ref[...], TN)

        @pl.when(i == ns - 1)
        def _():
            o_ref[...] = acc_ref[...].astype(BF16)

    return pl.pallas_call(
        body, name="wgrad_pool", grid=(n_groups, ns),
        in_specs=[pl.BlockSpec((ts, gw), lambda g, i: (i, g)),
                  pl.BlockSpec((ts, go), lambda g, i: (i, g))],
        out_specs=pl.BlockSpec((None, gw, go), lambda g, i: (g, 0, 0)),
        out_shape=jax.ShapeDtypeStruct((n_groups, gw, go), BF16),
        scratch_shapes=[pltpu.VMEM((gw, go), F32)],
        compiler_params=_cp(2),
    )(p, dyb)


def _wo_bwd(dx1b, wo, factors, sw, after=()):
    s, d = dx1b.shape
    tn = d // N_DEV
    nq = sw // tn

    def body(dx_ref, wo_ref, fya_ref, fyb_ref, fga_ref, fgb_ref, fsc_ref, *rest):
        dya_ref, dyb_ref, dp_ref, dbg_ref, dsc_ref, dm_ref = rest[len(after):]
        dbg_ref[...] = jnp.zeros_like(dbg_ref)
        dsc_ref[...] = jnp.zeros_like(dsc_ref)
        for rs in _chunks(s, 1024):
            dm_ref[rs, :] = _dot(dx_ref[rs, :], wo_ref[...], NT)
        for rs in _chunks(s, 256):
            dm = dm_ref[rs, :]
            dya_ref[rs, :] = (dm * fya_ref[rs, :].astype(F32)).astype(BF16)
            dyb_ref[rs, :] = (dm * fyb_ref[rs, :].astype(F32)).astype(BF16)
            dsc_ref[...] += jnp.sum(dm * fsc_ref[rs, :].astype(F32), axis=0, keepdims=True)
            dga = dm * fga_ref[rs, :].astype(F32)
            dgb = dm * fgb_ref[rs, :].astype(F32)
            dp_ref[0, rs, :] = dga.astype(BF16)
            dp_ref[1, rs, :] = dgb.astype(BF16)
            dbg_ref[0:1, :] += jnp.sum(dga, axis=0, keepdims=True)
            dbg_ref[1:2, :] += jnp.sum(dgb, axis=0, keepdims=True)

    col = pl.BlockSpec((s, tn), lambda j: (0, j))
    out = jax.ShapeDtypeStruct((s, d), BF16)
    return pl.pallas_call(
        body, name="wo_bwd", grid=(N_DEV,),
        in_specs=[pl.BlockSpec((s, d), lambda j: (0, 0)),
                  pl.BlockSpec((tn, d), lambda j: (j, 0))] + [col] * 5 + _after_specs(after),
        out_specs=[col, col,
                   pl.BlockSpec((2, None, s, tn), lambda j: (1, j // nq, 0, j % nq)),
                   pl.BlockSpec((2, tn), lambda j: (0, j)),
                   pl.BlockSpec((1, tn), lambda j: (0, j))],
        out_shape=[out, out, jax.ShapeDtypeStruct((4, 2, s, sw), BF16),
                   jax.ShapeDtypeStruct((2, d), F32), jax.ShapeDtypeStruct((1, d), F32)],
        scratch_shapes=[pltpu.VMEM((s, tn), F32)],
        compiler_params=_cp(1),
    )(dx1b, wo, *factors, *after)


def _conv_bwd(dproj, dya, wa, proj, conv_w, conv_b):
    s, d = dya.shape
    sw, tn = wa.shape[1], wa.shape[2]
    tc = min(LANES, sw)

    def body(dproj_hbm, dya_ref, wa_ref, ba_ref, ca_ref, va_ref, cw_ref, cb_ref,
             dp_ref, dcw_ref, dcb_ref, dz_ref):
        del dproj_hbm
        for rs in _chunks(s, 512):
            part = _dot(dya_ref[rs, 0:tn], wa_ref[0], NT)
            for j in range(1, N_DEV):
                part = part + _dot(dya_ref[rs, j * tn:(j + 1) * tn], wa_ref[j], NT)
            dz_ref[rs, :] = part
        dz = dz_ref[...]
        ba, ca, va = ba_ref[...], ca_ref[...], va_ref[...]
        cv = ca * va
        cv1, cv2 = _shift_down(cv, 1), _shift_down(cv, 2)
        w0, w1, w2 = cw_ref[0:1, :], cw_ref[1:2, :], cw_ref[2:3, :]
        u = cb_ref[...] + w0 * cv2 + w1 * cv1 + w2 * cv
        du = dz * ba
        dp_ref[0] = (dz * u).astype(BF16)
        dcv = w2 * du + w1 * _shift_up(du, 1) + w0 * _shift_up(du, 2)
        dp_ref[1] = (dcv * va).astype(BF16)
        dp_ref[2] = (dcv * ca).astype(BF16)
        dcw_ref[0:1, :] = jnp.sum(du * cv2, axis=0, keepdims=True)
        dcw_ref[1:2, :] = jnp.sum(du * cv1, axis=0, keepdims=True)
        dcw_ref[2:3, :] = jnp.sum(du * cv, axis=0, keepdims=True)
        dcb_ref[...] = jnp.sum(du, axis=0, keepdims=True)

    def part(k):
        return pl.BlockSpec((None, s, tc), lambda i: (k, 0, i))

    return pl.pallas_call(
        body, name="conv_bwd", grid=(sw // tc,),
        in_specs=[pl.BlockSpec(memory_space=pl.ANY),
                  pl.BlockSpec((s, d), lambda i: (0, 0)),
                  pl.BlockSpec((N_DEV, tc, tn), lambda i: (0, i, 0)),
                  part(0), part(1), part(2),
                  pl.BlockSpec((CONV_K, tc), lambda i: (0, i)), pl.BlockSpec((1, tc), lambda i: (0, i))],
        out_specs=[pl.BlockSpec((3, s, tc), lambda i: (0, 0, i)),
                   pl.BlockSpec((CONV_K, tc), lambda i: (0, i)), pl.BlockSpec((1, tc), lambda i: (0, i))],
        out_shape=[jax.ShapeDtypeStruct(dproj.shape, BF16),
                   jax.ShapeDtypeStruct((CONV_K, sw), F32), jax.ShapeDtypeStruct((1, sw), F32)],
        scratch_shapes=[pltpu.VMEM((s, tc), F32)],
        input_output_aliases={0: 0},
        compiler_params=_cp(1),
    )(dproj, dya, wa, proj, proj, proj, conv_w, conv_b)


def _pool_bwd(dproj, dyb, wpool):
    s, d = dyb.shape
    n_groups, gw, go = wpool.shape

    def body(dproj_hbm, dyb_ref, wp_ref, dp_ref):
        del dproj_hbm
        for gi, window in enumerate(POOL_WINDOWS):
            @pl.when(pl.program_id(0) == gi)
            def _():
                dpool = _dot(dyb_ref[...], wp_ref[...], NT)
                acc, k = dpool / _pool_counts(dpool.shape, window), 1
                while k < window:
                    acc = acc + _shift_up(acc, k)
                    k *= 2
                dp_ref[...] = (acc - dpool).astype(BF16)

    return pl.pallas_call(
        body, name="pool_bwd", grid=(n_groups,),
        in_specs=[pl.BlockSpec(memory_space=pl.ANY),
                  pl.BlockSpec((s, go), lambda g: (0, g)),
                  pl.BlockSpec((None, gw, go), lambda g: (g, 0, 0))],
        out_specs=pl.BlockSpec((None, s, gw), lambda g: (3, 0, g)),
        out_shape=jax.ShapeDtypeStruct(dproj.shape, BF16),
        input_output_aliases={0: 0},
        compiler_params=_cp(1),
    )(dproj, dyb, wpool)


def _rows128(v):
    return v.reshape(-1, LANES)


def kernel(x, norm1_g, w_in, b_gate, conv_w, conv_b, w_a_out, w_pool, pool_scale, w_o, norm2_g, w_ffn_gate, w_ffn_up, w_ffn_down, final_g, loss_target, m_norm1_g, m_w_in, m_b_gate, m_conv_w, m_conv_b, m_w_a_out, m_w_pool, m_pool_scale, m_w_o, m_norm2_g, m_w_ffn_gate, m_w_ffn_up, m_w_ffn_down, m_final_g, v_norm1_g, v_w_in, v_b_gate, v_conv_w, v_conv_b, v_w_a_out, v_w_pool, v_pool_scale, v_w_o, v_norm2_g, v_w_ffn_gate, v_w_ffn_up, v_w_ffn_down, v_final_g):
    s, d = x.shape[1], x.shape[2]
    sw = w_in.shape[2]
    n_groups = w_pool.shape[1]
    gw = w_pool.shape[2]
    go = w_pool.shape[3] * N_DEV
    f8 = w_ffn_gate.shape[2]
    cws = conv_w.shape[2]
    assert sw == conv_w.shape[2] * N_DEV == gw * n_groups and go * n_groups == d and n_groups == len(POOL_WINDOWS)

    xi, yi, ci = _coords()
    me = 4 * xi + 2 * yi + ci
    my_chip = 2 * xi + yi

    x2d = x.reshape(s, d)
    target = loss_target.reshape(s, d)
    final_g2 = final_g.reshape(1, d)
    b_gate2 = b_gate.reshape(2, d)

    big_names = ["w_in", "w_a_out", "w_pool", "w_o", "w_ffn_gate", "w_ffn_up", "w_ffn_down"]
    big_w = [w_in, w_a_out, w_pool, w_o, w_ffn_gate, w_ffn_up, w_ffn_down]
    big_m = [m_w_in, m_w_a_out, m_w_pool, m_w_o, m_w_ffn_gate, m_w_ffn_up, m_w_ffn_down]
    big_v = [v_w_in, v_w_a_out, v_w_pool, v_w_o, v_w_ffn_gate, v_w_ffn_up, v_w_ffn_down]
    shapes2d = [(w.size // w.shape[-1], w.shape[-1]) for w in big_w]
    big_w2 = [w.reshape(sh) for w, sh in zip(big_w, shapes2d)]
    transposed = (4, 5)

    def view2d(t, a):
        t2 = t.reshape(shapes2d[a])
        return t2.T if a in transposed else t2

    def unview(o, a):
        return (o.T if a in transposed else o).reshape(big_w[a].shape)

    sb = [_cast_bf16(w, "cast_" + nm) for w, nm in zip(big_w2, big_names)]
    win_g, wa_g, wpool_g, wo_g = _allgather_big(sb[0:4], "allgather_mixer", COLLECTIVE_GATHER)
    wg_g, wu_g = _allgather_big(sb[4:6], "allgather_ffn_up", COLLECTIVE_GATHER)
    (wd_g,) = _allgather_big(sb[6:7], "allgather_ffn_down", COLLECTIVE_GATHER)
    convw_g = _allgather_small(jnp.pad(conv_w.reshape(CONV_K, cws), ((0, 8 - CONV_K), (0, 0))), "allgather_conv_w")
    conv_w_full = convw_g[:, :CONV_K, :].transpose(1, 0, 2).reshape(CONV_K, sw)
    wpool = wpool_g.reshape(N_DEV, n_groups, gw, go // N_DEV).transpose(1, 2, 0, 3).reshape(n_groups, gw, go)
    wo = wo_g.reshape(d, d)

    h = _rms_fwd(x2d, norm1_g)
    proj = _proj_fwd(h, win_g)
    z = _conv_fwd(proj, conv_w_full, conv_b)
    p = _pool_fwd(proj)
    merged, *merge_factors = _merge_fwd(z, wa_g, p, wpool, proj, b_gate2, pool_scale)
    x1, h2 = _wo_fwd(merged, wo, x2d, norm2_g)
    dadu, dadg, act = _ffn_up_act_fwd(h2, wg_g, wu_g)
    ffn_out = _ffn_down_fwd(act, wd_g)
    dx2b, d_final_g, loss_blk = _loss_bwd(ffn_out, x1, target, final_g2)

    other_chips = jnp.stack([2 * (1 - xi) + yi, 2 * xi + (1 - yi), 2 * (1 - xi) + (1 - yi)])
    others = jnp.concatenate([other_chips, 2 * other_chips + ci]).astype(jnp.int32)

    def partials(grads, recvs, names):
        return [_chip_partial(others, g3, r, "chip_partial_" + nm) for g3, r, nm in zip(grads, recvs, names)]

    own = jnp.stack([me, my_chip]).astype(jnp.int32)

    def adam(a, g3, sib, chips):
        outs = _adam_big(own, view2d(big_w[a], a), view2d(big_m[a], a), view2d(big_v[a], a),
                         g3, sib, chips, "adam_" + big_names[a])
        return [unview(o, a) for o in outs]

    big_out = [None] * len(big_names)
    dg_act, du_act = _ffn_gate_bwd(dx2b, wd_g, dadg, dadu)
    gw_gate = _wgrad_rows(dg_act, h2, "wgrad_ffn_gate")
    gw_up = _wgrad_rows(du_act, h2, "wgrad_ffn_up")
    gw_down, sib_gu = _wgrad_rows(act, dx2b, "wgrad_ffn_down", carry=[gw_gate, gw_up])
    ps_gu = partials([gw_gate, gw_up], sib_gu, ["w_ffn_gate", "w_ffn_up"])
    chips_gu = _exchange_chips(ps_gu, "rs_chips_ffn_up", COLLECTIVE_CHIPS)
    dh2, sib_down = _input_grad([(dg_act, wg_g), (du_act, wu_g)], "ffn_in_bwd", after=ps_gu, carry=[gw_down])
    ps_down = partials([gw_down], sib_down, ["w_ffn_down"])
    chips_down = _exchange_chips(ps_down, "rs_chips_ffn_down", COLLECTIVE_CHIPS)
    dx1, dx1b, d_norm2_g = _rms_bwd(dh2, x1, norm2_g, dx2b, "rms2_bwd")
    dya, dyb, dproj42, d_b_gate, d_pool_scale = _wo_bwd(dx1b, wo, merge_factors, sw, after=ps_down)
    dproj = dproj42.reshape(N_DEV, s, sw)
    dproj, d_conv_w, d_conv_b = _conv_bwd(dproj, dya, wa_g, proj, conv_w_full, conv_b)
    dproj = _pool_bwd(dproj, dyb, wpool)
    gw_in = _wgrad_cols(h, dproj, "wgrad_in")
    gw_o, sib_in = _wgrad_full(merged, dx1b, "wgrad_o", carry=[gw_in])
    ps_in = partials([gw_in], sib_in, ["w_in"])
    chips_in = _exchange_chips(ps_in, "rs_chips_w_in", COLLECTIVE_CHIPS)
    gw_a = _wgrad_cols(z, dya, "wgrad_a_out", after=ps_in)
    gw_pool = _wgrad_pool(p, dyb, n_groups)
    mix3 = [gw_a,
            gw_pool.reshape(n_groups, gw, N_DEV, go // N_DEV).transpose(2, 0, 1, 3).reshape(N_DEV, n_groups * gw, go // N_DEV),
            gw_o.reshape(N_DEV, d // N_DEV, d)]
    big_out[4] = adam(4, gw_gate, sib_gu[0], chips_gu[0])
    big_out[5] = adam(5, gw_up, sib_gu[1], chips_gu[1])
    big_out[6] = adam(6, gw_down, sib_down[0], chips_down[0])
    dh, sib_mix = _input_grad([(dproj, win_g)], "proj_in_bwd",
                              after=[big_out[4][0], big_out[5][0], big_out[6][0]], carry=mix3)
    ps_mix = partials(mix3, sib_mix, ["w_a_out", "w_pool", "w_o"])
    chips_mix = _exchange_chips(ps_mix, "rs_chips_mixer", COLLECTIVE_CHIPS)
    grad_x, _, d_norm1_g = _rms_bwd(dh, x2d, norm1_g, dx1, "rms1_bwd", with_bf16=False)
    big_out[0] = adam(0, gw_in, sib_in[0], chips_in[0])
    for k in range(3):
        big_out[1 + k] = adam(1 + k, mix3[k], sib_mix[k], chips_mix[k])

    small_parts = [d_norm1_g, d_b_gate, d_conv_w, d_conv_b, d_pool_scale, d_norm2_g, d_final_g, loss_blk]
    rows = [v.size // LANES for v in small_parts]
    row0 = [sum(rows[:k]) for k in range(len(rows))]
    packed = jnp.concatenate([_rows128(v) for v in small_parts], axis=0)
    gathered = _allgather_small(packed, "allgather_small_grads")
    small_names = ["norm1_g", "b_gate", "conv_b", "pool_scale", "norm2_g", "final_g", "conv_w"]
    small_w = [norm1_g, b_gate, conv_b, pool_scale, norm2_g, final_g]
    small_m = [m_norm1_g, m_b_gate, m_conv_b, m_pool_scale, m_norm2_g, m_final_g]
    small_v = [v_norm1_g, v_b_gate, v_conv_b, v_pool_scale, v_norm2_g, v_final_g]
    finished = _small_finish(gathered, [tuple(_rows128(t) for t in wmv) for wmv in zip(small_w, small_m, small_v)],
                             [row0[k] for k in (0, 1, 3, 4, 5, 6)], [(row0[2], rows[2]), (row0[7], rows[7])])
    g_convw_full, loss_rows = finished[0], finished[1]
    loss = loss_rows[0, 0]
    small_out = [[t.reshape(w.shape) for t in finished[2 + 4 * k:6 + 4 * k]] for k, w in enumerate(small_w)]
    g_convw = lax.dynamic_slice(g_convw_full.reshape(CONV_K, sw), (0, me * cws), (CONV_K, cws))
    cw_delta, cw_m, cw_v = _adam_small(conv_w.reshape(CONV_K, cws), g_convw,
                                       m_conv_w.reshape(CONV_K, cws), v_conv_w.reshape(CONV_K, cws))
    small_out.append([t.reshape(conv_w.shape) for t in (g_convw, cw_delta, cw_m, cw_v)])

    order = ["norm1_g", "w_in", "b_gate", "conv_w", "conv_b", "w_a_out", "w_pool", "pool_scale", "w_o", "norm2_g",
             "w_ffn_gate", "w_ffn_up", "w_ffn_down", "final_g"]
    per_kind = [{}, {}, {}, {}]
    for a, nm in enumerate(big_names):
        for kind in range(4):
            per_kind[kind][nm] = big_out[a][kind]
    for k, nm in enumerate(small_names):
        for kind in range(4):
            per_kind[kind][nm] = small_out[k][kind]
    result = [loss, grad_x.reshape(x.shape)]
    for kind in range(4):
        result += [per_kind[kind][nm] for nm in order]
    return tuple(result)
```

```python
import jax
import jax.numpy as jnp
from jax import lax
from jax.experimental import pallas as pl
from jax.experimental.pallas import tpu as pltpu
from jax.experimental.pallas import tpu_sc as plsc

F32 = jnp.float32
BF16 = jnp.bfloat16
MESH = pl.DeviceIdType.MESH

N_DEV = 8
EPS = 1e-6
CONV_K = 3
POOL_WINDOWS = (2, 4, 8, 16)
ADAM_LR = 0.001
ADAM_B1 = 0.9
ADAM_B2 = 0.999
ADAM_EPS = 1e-08
ADAM_WD = 0.01
ADAM_STEP = 10

V7X_VMEM_LIMIT_BYTES = 56 * 1024 * 1024
LANES = 128

COLLECTIVE_GATHER = 1
COLLECTIVE_SIBLING = 2
COLLECTIVE_CHIPS = 3
SEQUENCER_COST_BYTES = 4 * 10**9

NN = ((1,), (0,))
NT = ((1,), (1,))
TN = ((0,), (0,))


def _dot(a, b, dims):
    return lax.dot_general(a, b, (dims, ((), ())), preferred_element_type=F32)


def _cp(n_axes):
    return pltpu.CompilerParams(dimension_semantics=("arbitrary",) * n_axes,
                                vmem_limit_bytes=V7X_VMEM_LIMIT_BYTES)


def _row_tile(rows, bytes_per_row, cap_bytes):
    best = None
    for t in range(16, rows + 1, 16):
        if rows % t == 0 and t * bytes_per_row <= cap_bytes:
            best = t
    return best if best is not None else rows


def _chunks(total, size):
    size = min(size, total)
    assert total % size == 0
    return [slice(r, r + size) for r in range(0, total, size)]


def _after_specs(after):
    return [pl.BlockSpec(memory_space=pl.ANY)] * len(after)


def _shift_down(v, k):
    row = lax.broadcasted_iota(jnp.int32, v.shape, 0)
    return jnp.where(row >= k, pltpu.roll(v, k, 0), 0.0)


def _shift_up(v, k):
    n = v.shape[0]
    row = lax.broadcasted_iota(jnp.int32, v.shape, 0)
    return jnp.where(row < n - k, pltpu.roll(v, n - k, 0), 0.0)


def _sigmoid(v):
    return jax.nn.sigmoid(v)


def _cast_bf16(w2d, name):
    rows, cols = w2d.shape
    tr = _row_tile(rows, cols * 4, 2 << 20)

    def body(i_ref, o_ref):
        o_ref[...] = i_ref[...].astype(BF16)

    return pl.pallas_call(
        body, name=name, grid=(rows // tr,),
        in_specs=[pl.BlockSpec((tr, cols), lambda i: (i, 0))],
        out_specs=pl.BlockSpec((tr, cols), lambda i: (i, 0)),
        out_shape=jax.ShapeDtypeStruct((rows, cols), BF16),
        compiler_params=_cp(1),
    )(w2d)


def _rms_fwd(x2d, g):
    s, d = x2d.shape
    tm = min(256, s)

    def body(x_ref, g_ref, h_ref):
        xv = x_ref[...]
        r = lax.rsqrt(jnp.mean(xv * xv, axis=-1, keepdims=True) + EPS)
        h_ref[...] = (xv * r * g_ref[...]).astype(BF16)

    return pl.pallas_call(
        body, name="rms1_fwd", grid=(s // tm,),
        in_specs=[pl.BlockSpec((tm, d), lambda i: (i, 0)), pl.BlockSpec((1, d), lambda i: (0, 0))],
        out_specs=pl.BlockSpec((tm, d), lambda i: (i, 0)),
        out_shape=jax.ShapeDtypeStruct((s, d), BF16),
        compiler_params=_cp(1),
    )(x2d, g)


def _coords():
    return lax.axis_index("x"), lax.axis_index("y"), lax.axis_index("c")


def _slot(p):
    return 4 * p[0] + 2 * p[1] + p[2]


def _handshake(peers):
    barrier = pltpu.get_barrier_semaphore()
    for peer in peers:
        pl.semaphore_signal(barrier, inc=1, device_id=peer, device_id_type=MESH)
    pl.semaphore_wait(barrier, len(peers))


def _sequencer_call(body, out_type, scratch_types, name, collective_id):
    return pl.kernel(
        body, out_type=out_type, name=name,
        mesh=plsc.ScalarSubcoreMesh(axis_name="seq", num_cores=1),
        scratch_types=scratch_types,
        cost_estimate=pl.CostEstimate(flops=0, transcendentals=0, bytes_accessed=SEQUENCER_COST_BYTES),
        compiler_params=pltpu.CompilerParams(collective_id=collective_id))


def _allgather_big(shards, name, collective_id, after=()):
    n = len(shards)

    def body(*refs):
        ins, outs = refs[:n], refs[n + len(after):2 * n + len(after)]
        send_sems, recv_sems, local_sems = refs[2 * n + len(after):]
        x, y, c = _coords()
        me, sibling = (x, y, c), (x, y, 1 - c)
        x_nbr, y_nbr, diag = (1 - x, y), (x, 1 - y), (1 - x, 1 - y)
        relay_from = (x + (1 - c) * (1 - 2 * x), y + c * (1 - 2 * y))
        relay_to = (x + c * (1 - 2 * x), y + (1 - c) * (1 - 2 * y))
        _handshake([sibling, (*x_nbr, c), (*y_nbr, c)])

        def copy(a, k, block, to, src=None):
            dst = outs[a].at[_slot(block)]
            return pltpu.make_async_remote_copy(
                src_ref=dst if src is None else src, dst_ref=dst,
                send_sem=send_sems.at[a, k], recv_sem=recv_sems.at[a, k],
                device_id=to, device_id_type=MESH)

        mine, sends = [], []
        for a in range(n):
            cp = pltpu.make_async_copy(ins[a], outs[a].at[_slot(me)], local_sems.at[a])
            cp.start()
            mine.append(cp)
            first = [copy(a, 0, me, sibling, src=ins[a]),
                     copy(a, 1, me, (*x_nbr, c), src=ins[a]),
                     copy(a, 2, me, (*y_nbr, c), src=ins[a])]
            for cp in first:
                cp.start()
            sends += first
        for a in range(n):
            copy(a, 1 + c, (*relay_from, c), me).wait_recv()
            passed = [copy(a, 3, (*relay_from, c), (*relay_to, c)), copy(a, 4 + c, (*relay_from, c), sibling)]
            for cp in passed:
                cp.start()
            copy(a, 2 - c, (*relay_to, c), me).wait_recv()
            cp = copy(a, 5 - c, (*relay_to, c), sibling)
            cp.start()
            passed.append(cp)
            copy(a, 3, (*diag, c), me).wait_recv()
            cp = copy(a, 6, (*diag, c), sibling)
            cp.start()
            sends += passed + [cp]
        for a in range(n):
            copy(a, 0, sibling, me).wait_recv()
            copy(a, 4, (*x_nbr, 1 - c), me).wait_recv()
            copy(a, 5, (*y_nbr, 1 - c), me).wait_recv()
            copy(a, 6, (*diag, 1 - c), me).wait_recv()
        for cp in sends:
            cp.wait_send()
        for cp in mine:
            cp.wait()

    return _sequencer_call(
        body, [jax.ShapeDtypeStruct((N_DEV,) + s.shape, s.dtype) for s in shards],
        [pltpu.SemaphoreType.DMA((n, 7)), pltpu.SemaphoreType.DMA((n, 7)), pltpu.SemaphoreType.DMA((n,))],
        name, collective_id)(*shards, *after)


def _sibling_copies(ins, recvs, send_sems, recv_sems):
    x, y, c = _coords()
    return [pltpu.make_async_remote_copy(
        src_ref=ins[a].at[2 * q + (1 - c)], dst_ref=recvs[a].at[q],
        send_sem=send_sems.at[a, q], recv_sem=recv_sems.at[a, q],
        device_id=(x, y, 1 - c), device_id_type=MESH) for a in range(len(ins)) for q in range(4)]


def _carry_specs(carry):
    any_spec = pl.BlockSpec(memory_space=pl.ANY)
    n = len(carry)
    sems = [pltpu.SemaphoreType.DMA((n, 4)), pltpu.SemaphoreType.DMA((n, 4))] if n else []
    return ([any_spec] * n, [any_spec] * n,
            [jax.ShapeDtypeStruct((4,) + g.shape[1:], g.dtype) for g in carry], sems)


def _carry_run(first, last, ins, recvs, sems):
    if not ins:
        return

    @pl.when(first)
    def _():
        x, y, c = _coords()
        _handshake([(x, y, 1 - c)])
        for cp in _sibling_copies(ins, recvs, *sems):
            cp.start()

    @pl.when(last)
    def _():
        copies = _sibling_copies(ins, recvs, *sems)
        for cp in copies:
            cp.wait_recv()
        for cp in copies:
            cp.wait_send()


def _cp_carry(n_axes, carry):
    if not carry:
        return _cp(n_axes)
    return pltpu.CompilerParams(dimension_semantics=("arbitrary",) * n_axes, vmem_limit_bytes=V7X_VMEM_LIMIT_BYTES,
                                collective_id=COLLECTIVE_SIBLING)


def _exchange_chips(psums, name, collective_id):
    n = len(psums)

    def body(*refs):
        ins, outs = refs[:n], refs[n:2 * n]
        send_sems, recv_sems = refs[2 * n:]
        x, y, c = _coords()
        chips = [(1 - x, y), (x, 1 - y), (1 - x, 1 - y)]
        _handshake([(*chip, c) for chip in chips])
        copies = []
        for a in range(n):
            for j, chip in enumerate(chips):
                cp = pltpu.make_async_remote_copy(
                    src_ref=ins[a].at[2 * chip[0] + chip[1]], dst_ref=outs[a].at[j],
                    send_sem=send_sems.at[a, j], recv_sem=recv_sems.at[a, j],
                    device_id=(*chip, c), device_id_type=MESH)
                cp.start()
                copies.append(cp)
        for cp in copies:
            cp.wait_recv()
        for cp in copies:
            cp.wait_send()

    return _sequencer_call(
        body, [jax.ShapeDtypeStruct((3,) + p.shape[1:], p.dtype) for p in psums],
        [pltpu.SemaphoreType.DMA((n, 3)), pltpu.SemaphoreType.DMA((n, 3))],
        name, collective_id)(*psums)


def _allgather_small(v2d, name):
    rows, cols = v2d.shape

    def body(v_ref, out_ref, send_sems, recv_sems):
        x, y, c = _coords()
        me = (x, y, c)
        out_ref[_slot(me)] = v_ref[...]
        peers = []
        for k in range(1, N_DEV):
            fx, fy, fc = (k >> 2) & 1, (k >> 1) & 1, k & 1
            peers.append(((1 - x) if fx else x, (1 - y) if fy else y, (1 - c) if fc else c))
        sends = []
        for k, peer in enumerate(peers):
            cp = pltpu.make_async_remote_copy(
                src_ref=v_ref, dst_ref=out_ref.at[_slot(me)],
                send_sem=send_sems.at[k], recv_sem=recv_sems.at[k],
                device_id=peer, device_id_type=MESH)
            cp.start()
            sends.append(cp)
        for k, peer in enumerate(peers):
            pltpu.make_async_remote_copy(
                src_ref=v_ref, dst_ref=out_ref.at[_slot(peer)],
                send_sem=send_sems.at[k], recv_sem=recv_sems.at[k],
                device_id=peer, device_id_type=MESH).wait_recv()
        for cp in sends:
            cp.wait_send()

    vmem = pl.BlockSpec(memory_space=pltpu.VMEM)
    return pl.pallas_call(
        body, name=name, in_specs=[vmem], out_specs=vmem,
        out_shape=jax.ShapeDtypeStruct((N_DEV, rows, cols), v2d.dtype),
        scratch_shapes=[pltpu.SemaphoreType.DMA((N_DEV - 1,)), pltpu.SemaphoreType.DMA((N_DEV - 1,))],
    )(v2d)


def _chip_partial(others, g3, recv, name):
    _, rows, cols = g3.shape
    tr = _row_tile(rows, cols * 2, 2 << 20)

    def body(others_ref, g_ref, r_ref, o_ref):
        o_ref[...] = (g_ref[...].astype(F32) + r_ref[...].astype(F32)).astype(BF16)

    return pl.pallas_call(
        body, name=name,
        grid_spec=pltpu.PrefetchScalarGridSpec(
            num_scalar_prefetch=1, grid=(3, rows // tr),
            in_specs=[pl.BlockSpec((None, tr, cols), lambda k, i, o: (o[3 + k], i, 0)),
                      pl.BlockSpec((None, tr, cols), lambda k, i, o: (o[k], i, 0))],
            out_specs=pl.BlockSpec((None, tr, cols), lambda k, i, o: (o[k], i, 0))),
        out_shape=jax.ShapeDtypeStruct((4, rows, cols), BF16),
        compiler_params=_cp(2),
    )(others, g3, recv)


def _adam_math(w, g, m, v):
    m = ADAM_B1 * m + (1.0 - ADAM_B1) * g
    v = ADAM_B2 * v + (1.0 - ADAM_B2) * (g * g)
    m_hat = m / (1.0 - ADAM_B1 ** ADAM_STEP)
    v_hat = v / (1.0 - ADAM_B2 ** ADAM_STEP)
    delta = -ADAM_LR * (m_hat / (jnp.sqrt(v_hat) + ADAM_EPS) + ADAM_WD * w)
    return delta, m, v


def _adam_big(own, w, m, v, g3, recv_sib, recv_chips, name):
    rows, cols = w.shape
    tr = _row_tile(rows, cols * 4, 2 << 20)

    def body(own_ref, w_ref, m_ref, v_ref, g_ref, rs_ref, rc_ref, go_ref, do_ref, mo_ref, vo_ref):
        g = g_ref[...].astype(F32) + rs_ref[...].astype(F32)
        g = g + rc_ref[0].astype(F32)
        g = g + rc_ref[1].astype(F32)
        g = g + rc_ref[2].astype(F32)
        delta, m_new, v_new = _adam_math(w_ref[...], g, m_ref[...], v_ref[...])
        go_ref[...] = g
        do_ref[...] = delta
        mo_ref[...] = m_new
        vo_ref[...] = v_new

    blk = pl.BlockSpec((tr, cols), lambda i, o: (i, 0))
    out = jax.ShapeDtypeStruct((rows, cols), F32)
    return pl.pallas_call(
        body, name=name,
        grid_spec=pltpu.PrefetchScalarGridSpec(
            num_scalar_prefetch=1, grid=(rows // tr,),
            in_specs=[blk, blk, blk,
                      pl.BlockSpec((None, tr, cols), lambda i, o: (o[0], i, 0)),
                      pl.BlockSpec((None, tr, cols), lambda i, o: (o[1], i, 0)),
                      pl.BlockSpec((3, tr, cols), lambda i, o: (0, i, 0))],
            out_specs=[blk, blk, blk, blk]),
        out_shape=[out, out, out, out],
        compiler_params=_cp(1),
    )(own, w, m, v, g3, recv_sib, recv_chips)


def _small_finish(gathered, params, row_offs, extra_rows):
    n = len(params)

    def body(g_ref, *refs):
        ins, outs = refs[:3 * n], refs[3 * n:]
        total = g_ref[0]
        for k in range(1, N_DEV):
            total = total + g_ref[k]
        for e, (r0, nr) in enumerate(extra_rows):
            outs[e][...] = total[r0:r0 + nr, :]
        for p in range(n):
            w_ref, m_ref, v_ref = ins[3 * p:3 * p + 3]
            g_out, d_out, m_out, v_out = outs[len(extra_rows) + 4 * p:len(extra_rows) + 4 * p + 4]
            g = total[row_offs[p]:row_offs[p] + w_ref.shape[0], :]
            delta, m_new, v_new = _adam_math(w_ref[...], g, m_ref[...], v_ref[...])
            g_out[...] = g
            d_out[...] = delta
            m_out[...] = m_new
            v_out[...] = v_new

    vmem = pl.BlockSpec(memory_space=pltpu.VMEM)
    out_shape = [jax.ShapeDtypeStruct((nr, LANES), F32) for _, nr in extra_rows]
    for w, _, _ in params:
        out_shape += [jax.ShapeDtypeStruct(w.shape, F32)] * 4
    flat = [t for wmv in params for t in wmv]
    return pl.pallas_call(body, name="small_finish", in_specs=[vmem] * (1 + len(flat)),
                          out_specs=[vmem] * len(out_shape), out_shape=out_shape)(gathered, *flat)


def _adam_small(w, g, m, v):
    def body(w_ref, g_ref, m_ref, v_ref, do_ref, mo_ref, vo_ref):
        delta, m_new, v_new = _adam_math(w_ref[...], g_ref[...], m_ref[...], v_ref[...])
        do_ref[...] = delta
        mo_ref[...] = m_new
        vo_ref[...] = v_new

    vmem = pl.BlockSpec(memory_space=pltpu.VMEM)
    out = jax.ShapeDtypeStruct(w.shape, F32)
    return pl.pallas_call(body, name="adam_small", in_specs=[vmem] * 4, out_specs=[vmem] * 3,
                          out_shape=[out, out, out])(w, g, m, v)


def _proj_fwd(h, win_g):
    s, d = h.shape
    sw = win_g.shape[2]
    tn = min(512, sw)
    nh = sw // tn

    def body(h_ref, w_ref, o_ref):
        for rs in _chunks(s, 512):
            o_ref[rs, :] = _dot(h_ref[rs, :], w_ref[...], NN)

    return pl.pallas_call(
        body, name="proj_fwd", grid=(N_DEV * nh,),
        in_specs=[pl.BlockSpec((s, d), lambda j: (0, 0)),
                  pl.BlockSpec((None, d, tn), lambda j: (j // nh, 0, j % nh))],
        out_specs=pl.BlockSpec((None, s, tn), lambda j: (j // nh, 0, j % nh)),
        out_shape=jax.ShapeDtypeStruct((N_DEV, s, sw), F32),
        compiler_params=_cp(1),
    )(h, win_g)


def _conv_fwd(proj, conv_w, conv_b):
    _, s, sw = proj.shape
    tc = min(LANES, sw)

    def body(ba_ref, ca_ref, va_ref, cw_ref, cb_ref, z_ref):
        cv = ca_ref[...] * va_ref[...]
        u = (cb_ref[...] + cw_ref[0:1, :] * _shift_down(cv, 2) + cw_ref[1:2, :] * _shift_down(cv, 1)
             + cw_ref[2:3, :] * cv)
        z_ref[...] = (ba_ref[...] * u).astype(BF16)

    def part(k):
        return pl.BlockSpec((None, s, tc), lambda i: (k, 0, i))

    return pl.pallas_call(
        body, name="conv_fwd", grid=(sw // tc,),
        in_specs=[part(0), part(1), part(2),
                  pl.BlockSpec((CONV_K, tc), lambda i: (0, i)), pl.BlockSpec((1, tc), lambda i: (0, i))],
        out_specs=pl.BlockSpec((s, tc), lambda i: (0, i)),
        out_shape=jax.ShapeDtypeStruct((s, sw), BF16),
        compiler_params=_cp(1),
    )(proj, proj, proj, conv_w, conv_b)


def _pool_counts(shape, window):
    t = lax.broadcasted_iota(jnp.int32, shape, 0)
    return jnp.minimum(t + 1, window).astype(F32)


def _pool_fwd(proj):
    _, s, sw = proj.shape
    gw = sw // len(POOL_WINDOWS)

    def body(v_ref, p_ref):
        for gi, window in enumerate(POOL_WINDOWS):
            @pl.when(pl.program_id(0) == gi)
            def _():
                v = v_ref[...]
                acc, k = v, 1
                while k < window:
                    acc = acc + _shift_down(acc, k)
                    k *= 2
                p_ref[...] = (acc / _pool_counts(v.shape, window) - v).astype(BF16)

    return pl.pallas_call(
        body, name="pool_fwd", grid=(len(POOL_WINDOWS),),
        in_specs=[pl.BlockSpec((None, s, gw), lambda g: (3, 0, g))],
        out_specs=pl.BlockSpec((s, gw), lambda g: (0, g)),
        out_shape=jax.ShapeDtypeStruct((s, sw), BF16),
        compiler_params=_cp(1),
    )(proj)


def _merge_fwd(z, wa, p, wpool, proj, b_gate2, pool_scale):
    s, sw = z.shape
    tn = wa.shape[2]
    d = tn * N_DEV
    gw = sw // len(POOL_WINDOWS)
    nq = sw // tn

    def body(z_ref, wa_ref, p_ref, wp_ref, ga_ref, gb_ref, bg_ref, sc_ref,
             m_ref, dya_ref, dyb_ref, dga_ref, dgb_ref, dsc_ref):
        for rs in _chunks(s, 512):
            ya = _dot(z_ref[rs, :], wa_ref[...], NN)
            yb = _dot(p_ref[rs, :], wp_ref[...], NN)
            sa = _sigmoid(ga_ref[rs, :] + bg_ref[0:1, :])
            sb = _sigmoid(gb_ref[rs, :] + bg_ref[1:2, :])
            sc = sc_ref[...]
            sb_yb = sb * yb
            m_ref[rs, :] = (sa * ya + sb_yb * sc).astype(BF16)
            dya_ref[rs, :] = sa.astype(BF16)
            dyb_ref[rs, :] = (sb * sc).astype(BF16)
            dga_ref[rs, :] = (ya * (sa * (1.0 - sa))).astype(BF16)
            dgb_ref[rs, :] = ((yb * sc) * (sb * (1.0 - sb))).astype(BF16)
            dsc_ref[rs, :] = sb_yb.astype(BF16)

    col = pl.BlockSpec((s, tn), lambda j: (0, j))
    out = jax.ShapeDtypeStruct((s, d), BF16)
    return pl.pallas_call(
        body, name="merge_fwd", grid=(N_DEV,),
        in_specs=[pl.BlockSpec((s, sw), lambda j: (0, 0)),
                  pl.BlockSpec((None, sw, tn), lambda j: (j, 0, 0)),
                  pl.BlockSpec((s, gw), lambda j: (0, j // 2)),
                  pl.BlockSpec((None, gw, tn), lambda j: (j // 2, 0, j % 2)),
                  pl.BlockSpec((None, s, tn), lambda j: (4 + j // nq, 0, j % nq)),
                  pl.BlockSpec((None, s, tn), lambda j: (6 + j // nq, 0, j % nq)),
                  pl.BlockSpec((2, tn), lambda j: (0, j)),
                  pl.BlockSpec((1, tn), lambda j: (0, j))],
        out_specs=[col] * 6,
        out_shape=[out] * 6,
        compiler_params=_cp(1),
    )(z, wa, p, wpool, proj, proj, b_gate2, pool_scale)


def _wo_fwd(merged, wo, x2d, g2):
    s, d = x2d.shape
    tm = min(256, s)

    def body(m_ref, wo_ref, x_ref, g_ref, x1_ref, h2_ref):
        x1 = x_ref[...] + _dot(m_ref[...], wo_ref[...], NN)
        x1_ref[...] = x1
        r = lax.rsqrt(jnp.mean(x1 * x1, axis=-1, keepdims=True) + EPS)
        h2_ref[...] = (x1 * r * g_ref[...]).astype(BF16)

    row = pl.BlockSpec((tm, d), lambda i: (i, 0))
    return pl.pallas_call(
        body, name="wo_fwd", grid=(s // tm,),
        in_specs=[row, pl.BlockSpec((d, d), lambda i: (0, 0)), row, pl.BlockSpec((1, d), lambda i: (0, 0))],
        out_specs=[row, row],
        out_shape=[jax.ShapeDtypeStruct((s, d), F32), jax.ShapeDtypeStruct((s, d), BF16)],
        compiler_params=_cp(1),
    )(merged, wo, x2d, g2)


def _ffn_up_act_fwd(h2, wg_g, wu_g):
    s, d = h2.shape
    f8 = wg_g.shape[2]
    th = min(1024, s)

    def body(h_ref, wg_ref, wu_ref, dadu_ref, dadg_ref, a_ref):
        i = pl.program_id(1)
        for rs in _chunks(th, 512):
            rows = pl.ds(pl.multiple_of(i * th + rs.start, rs.stop - rs.start), rs.stop - rs.start)
            a = h_ref[rows, :]
            g = _dot(a, wg_ref[...], NN)
            u = _dot(a, wu_ref[...], NN)
            sg = _sigmoid(g)
            silu = g * sg
            dadu_ref[rs, :] = silu.astype(BF16)
            dadg_ref[rs, :] = (u * (sg * (1.0 + g * (1.0 - sg)))).astype(BF16)
            a_ref[rs, :] = (silu * u).astype(BF16)

    wspec = pl.BlockSpec((None, d, f8), lambda j, i: (j, 0, 0))
    ospec = pl.BlockSpec((None, th, f8), lambda j, i: (j, i, 0))
    out = jax.ShapeDtypeStruct((N_DEV, s, f8), BF16)
    return pl.pallas_call(
        body, name="ffn_up_fwd", grid=(N_DEV, s // th),
        in_specs=[pl.BlockSpec((s, d), lambda j, i: (0, 0)), wspec, wspec],
        out_specs=[ospec, ospec, ospec], out_shape=[out, out, out],
        compiler_params=_cp(2),
    )(h2, wg_g, wu_g)


def _ffn_down_loss(act, wd_g, x1, target, final_g):
    _, s, f8 = act.shape
    d = wd_g.shape[2]
    tr = min(256, s)
    nchunk = s // tr

    def body(a_ref, wd_ref, gf_ref, x1_hbm, t_hbm, dxb_hbm, dgf_ref, loss_ref,
             acc_ref, x1_buf, t_buf, o_buf, in_sems, out_sems):
        j = pl.program_id(0)

        @pl.when(j == 0)
        def _():
            acc_ref[...] = jnp.zeros_like(acc_ref)

        for rs in _chunks(s, 1024):
            acc_ref[rs, :] += _dot(a_ref[rs, :], wd_ref[...], NN)

        @pl.when(j == N_DEV - 1)
        def _():
            def fetch(c, slot):
                rows = pl.ds(c * tr, tr)
                return [pltpu.make_async_copy(x1_hbm.at[rows], x1_buf.at[slot], in_sems.at[0, slot]),
                        pltpu.make_async_copy(t_hbm.at[rows], t_buf.at[slot], in_sems.at[1, slot])]

            def store(c, slot):
                return pltpu.make_async_copy(o_buf.at[slot], dxb_hbm.at[pl.ds(c * tr, tr)], out_sems.at[slot])

            gf = gf_ref[...]
            dgf = jnp.zeros((1, d), F32)
            loss = jnp.zeros((), F32)
            for cp in fetch(0, 0):
                cp.start()
            for c in range(nchunk):
                slot = c % 2
                if c + 1 < nchunk:
                    for cp in fetch(c + 1, 1 - slot):
                        cp.start()
                for cp in fetch(c, slot):
                    cp.wait()
                x2 = x1_buf[slot] + acc_ref[c * tr:(c + 1) * tr, :]
                r = lax.rsqrt(jnp.mean(x2 * x2, axis=-1, keepdims=True) + EPS)
                nrm = x2 * r
                err = nrm * gf - t_buf[slot]
                loss = loss + jnp.sum(err * err) * (0.5 / d)
                dy = err * (1.0 / d)
                dgf = dgf + jnp.sum(dy * nrm, axis=0, keepdims=True)
                dn = dy * gf
                dx = r * (dn - nrm * jnp.mean(dn * nrm, axis=-1, keepdims=True))
                if c >= 2:
                    store(c - 2, slot).wait()
                o_buf[slot] = dx.astype(BF16)
                store(c, slot).start()
            for c in range(max(nchunk - 2, 0), nchunk):
                store(c, c % 2).wait()
            dgf_ref[...] = dgf
            loss_ref[...] = jnp.zeros_like(loss_ref) + loss

    any_spec = pl.BlockSpec(memory_space=pl.ANY)
    return pl.pallas_call(
        body, name="ffn_down_loss", grid=(N_DEV,),
        in_specs=[pl.BlockSpec((None, s, f8), lambda j: (j, 0, 0)),
                  pl.BlockSpec((None, f8, d), lambda j: (j, 0, 0)),
                  pl.BlockSpec((1, d), lambda j: (0, 0)), any_spec, any_spec],
        out_specs=[any_spec, pl.BlockSpec((1, d), lambda j: (0, 0)), pl.BlockSpec((8, LANES), lambda j: (0, 0))],
        out_shape=[jax.ShapeDtypeStruct((s, d), BF16), jax.ShapeDtypeStruct((1, d), F32),
                   jax.ShapeDtypeStruct((8, LANES), F32)],
        scratch_shapes=[pltpu.VMEM((s, d), F32), pltpu.VMEM((2, tr, d), F32), pltpu.VMEM((2, tr, d), F32),
                        pltpu.VMEM((2, tr, d), BF16), pltpu.SemaphoreType.DMA((2, 2)), pltpu.SemaphoreType.DMA((2,))],
        compiler_params=_cp(1),
    )(act, wd_g, final_g, x1, target)


def _ffn_down_fwd(act, wd_g):
    _, s, f8 = act.shape
    d = wd_g.shape[2]
    tn = min(1024, d)

    def body(a_ref, wd_ref, o_ref):
        j = pl.program_id(1)

        @pl.when(j == 0)
        def _():
            o_ref[...] = jnp.zeros_like(o_ref)

        for rs in _chunks(s, 1024):
            o_ref[rs, :] += _dot(a_ref[rs, :], wd_ref[...], NN)

    return pl.pallas_call(
        body, name="ffn_down_fwd", grid=(d // tn, N_DEV),
        in_specs=[pl.BlockSpec((None, s, f8), lambda n, j: (j, 0, 0)),
                  pl.BlockSpec((None, f8, tn), lambda n, j: (j, 0, n))],
        out_specs=pl.BlockSpec((s, tn), lambda n, j: (0, n)),
        out_shape=jax.ShapeDtypeStruct((s, d), F32),
        compiler_params=_cp(2),
    )(act, wd_g)


def _loss_bwd(ffn_out, x1, target, final_g):
    s, d = x1.shape
    tm = min(256, s)

    def body(f_ref, x1_ref, t_ref, gf_ref, dxb_ref, dgf_ref, loss_ref):
        @pl.when(pl.program_id(0) == 0)
        def _():
            dgf_ref[...] = jnp.zeros_like(dgf_ref)
            loss_ref[...] = jnp.zeros_like(loss_ref)

        x2 = x1_ref[...] + f_ref[...]
        r = lax.rsqrt(jnp.mean(x2 * x2, axis=-1, keepdims=True) + EPS)
        nrm = x2 * r
        gf = gf_ref[...]
        err = nrm * gf - t_ref[...]
        loss_ref[...] += jnp.sum(err * err) * (0.5 / d)
        dy = err * (1.0 / d)
        dgf_ref[...] += jnp.sum(dy * nrm, axis=0, keepdims=True)
        dn = dy * gf
        dx = r * (dn - nrm * jnp.mean(dn * nrm, axis=-1, keepdims=True))
        dxb_ref[...] = dx.astype(BF16)

    row = pl.BlockSpec((tm, d), lambda i: (i, 0))
    vec = pl.BlockSpec((1, d), lambda i: (0, 0))
    return pl.pallas_call(
        body, name="loss_bwd", grid=(s // tm,),
        in_specs=[row, row, row, vec],
        out_specs=[row, vec, pl.BlockSpec((8, LANES), lambda i: (0, 0))],
        out_shape=[jax.ShapeDtypeStruct((s, d), BF16),
                   jax.ShapeDtypeStruct((1, d), F32), jax.ShapeDtypeStruct((8, LANES), F32)],
        compiler_params=_cp(1),
    )(ffn_out, x1, target, final_g)


def _ffn_gate_bwd(dx2b, wd_g, dadg, dadu):
    s, d = dx2b.shape
    f8 = dadg.shape[2]
    th = min(1024, s)

    def body(dx_ref, wd_ref, g_ref, u_ref, dg_ref, du_ref, da_ref):
        i = pl.program_id(1)
        chunks = _chunks(th, 256)

        def matmul(rs):
            rows = pl.ds(pl.multiple_of(i * th + rs.start, rs.stop - rs.start), rs.stop - rs.start)
            da_ref[rs, :] = _dot(dx_ref[rows, :], wd_ref[...], NT)

        matmul(chunks[0])
        for k, rs in enumerate(chunks):
            if k + 1 < len(chunks):
                matmul(chunks[k + 1])
            da = da_ref[rs, :]
            dg_ref[rs, :] = (da * g_ref[rs, :].astype(F32)).astype(BF16)
            du_ref[rs, :] = (da * u_ref[rs, :].astype(F32)).astype(BF16)

    aspec = pl.BlockSpec((None, th, f8), lambda j, i: (j, i, 0))
    out = jax.ShapeDtypeStruct((N_DEV, s, f8), BF16)
    return pl.pallas_call(
        body, name="ffn_act_bwd", grid=(N_DEV, s // th),
        in_specs=[pl.BlockSpec((s, d), lambda j, i: (0, 0)),
                  pl.BlockSpec((None, f8, d), lambda j, i: (j, 0, 0)), aspec, aspec],
        out_specs=[aspec, aspec], out_shape=[out, out],
        scratch_shapes=[pltpu.VMEM((th, f8), F32)],
        compiler_params=_cp(2),
    )(dx2b, wd_g, dadg, dadu)


def _wgrad_rows(a3, b, name, after=(), carry=()):
    _, s, k = a3.shape
    n = b.shape[1]
    nc = len(carry)
    c_in, c_out, c_shape, c_sems = _carry_specs(carry)

    def body(a_ref, b_ref, *rest):
        rest = rest[len(after):]
        o_ref = rest[nc]
        j = pl.program_id(0)
        _carry_run(j == 0, j == N_DEV - 1, rest[:nc], rest[nc + 1:2 * nc + 1], rest[2 * nc + 1:])
        o_ref[...] = _dot(a_ref[...], b_ref[...], TN).astype(BF16)

    outs = pl.pallas_call(
        body, name=name, grid=(N_DEV,),
        in_specs=[pl.BlockSpec((None, s, k), lambda j: (j, 0, 0)),
                  pl.BlockSpec((s, n), lambda j: (0, 0))] + _after_specs(after) + c_in,
        out_specs=[pl.BlockSpec((None, k, n), lambda j: (j, 0, 0))] + c_out,
        out_shape=[jax.ShapeDtypeStruct((N_DEV, k, n), BF16)] + c_shape,
        scratch_shapes=c_sems,
        compiler_params=_cp_carry(1, carry),
    )(a3, b, *after, *carry)
    return (outs[0], list(outs[1:])) if nc else outs[0]


def _wgrad_cols(a, b3, name, after=()):
    s, k = a.shape
    if b3.ndim == 2:
        n = b3.shape[1] // N_DEV
        b_spec = pl.BlockSpec((s, n), lambda j: (0, j))
    else:
        n = b3.shape[2]
        b_spec = pl.BlockSpec((None, s, n), lambda j: (j, 0, 0))

    def body(a_ref, b_ref, *rest):
        o_ref = rest[len(after)]
        o_ref[...] = _dot(a_ref[...], b_ref[...], TN).astype(BF16)

    return pl.pallas_call(
        body, name=name, grid=(N_DEV,),
        in_specs=[pl.BlockSpec((s, k), lambda j: (0, 0)), b_spec] + _after_specs(after),
        out_specs=pl.BlockSpec((None, k, n), lambda j: (j, 0, 0)),
        out_shape=jax.ShapeDtypeStruct((N_DEV, k, n), BF16),
        compiler_params=_cp(1),
    )(a, b3, *after)


def _input_grad(pairs, name, after=(), carry=()):
    s = pairs[0][0].shape[1]
    d = pairs[0][1].shape[1]
    tn = min(1024, d)
    npair = len(pairs)
    nc = len(carry)
    c_in, c_out, c_shape, c_sems = _carry_specs(carry)

    def body(*refs):
        ops = refs[:2 * npair]
        rest = refs[2 * npair + len(after):]
        o_ref = rest[nc]
        nh, j = pl.program_id(0), pl.program_id(1)
        _carry_run((nh == 0) & (j == 0), (nh == d // tn - 1) & (j == N_DEV - 1),
                   rest[:nc], rest[nc + 1:2 * nc + 1], rest[2 * nc + 1:])

        @pl.when(j == 0)
        def _():
            o_ref[...] = jnp.zeros_like(o_ref)

        for rs in _chunks(s, 1024):
            part = _dot(ops[0][rs, :], ops[1][...], NT)
            for q in range(1, npair):
                part = part + _dot(ops[2 * q][rs, :], ops[2 * q + 1][...], NT)
            o_ref[rs, :] += part

    in_specs, args = [], []
    for a3, w3 in pairs:
        k = a3.shape[2]
        in_specs += [pl.BlockSpec((None, s, k), lambda n, j: (j, 0, 0)),
                     pl.BlockSpec((None, tn, k), lambda n, j: (j, n, 0))]
        args += [a3, w3]
    outs = pl.pallas_call(
        body, name=name, grid=(d // tn, N_DEV),
        in_specs=in_specs + _after_specs(after) + c_in,
        out_specs=[pl.BlockSpec((s, tn), lambda n, j: (0, n))] + c_out,
        out_shape=[jax.ShapeDtypeStruct((s, d), F32)] + c_shape,
        scratch_shapes=c_sems,
        compiler_params=_cp_carry(2, carry),
    )(*args, *after, *carry)
    return (outs[0], list(outs[1:])) if nc else outs[0]


def _rms_bwd(dh, xres, g, dres, name, with_bf16=True):
    s, d = xres.shape
    tm = min(256, s)

    def body(dh_ref, x_ref, g_ref, dres_ref, dx_ref, *rest):
        dg_ref = rest[-1]
        @pl.when(pl.program_id(0) == 0)
        def _():
            dg_ref[...] = jnp.zeros_like(dg_ref)

        xv = x_ref[...]
        dh_v = dh_ref[...]
        r = lax.rsqrt(jnp.mean(xv * xv, axis=-1, keepdims=True) + EPS)
        nrm = xv * r
        dg_ref[...] += jnp.sum(dh_v * nrm, axis=0, keepdims=True)
        dn = dh_v * g_ref[...]
        dx = dres_ref[...].astype(F32) + r * (dn - nrm * jnp.mean(dn * nrm, axis=-1, keepdims=True))
        dx_ref[...] = dx
        if with_bf16:
            rest[0][...] = dx.astype(BF16)

    row = pl.BlockSpec((tm, d), lambda i: (i, 0))
    vec = pl.BlockSpec((1, d), lambda i: (0, 0))
    copies = [jax.ShapeDtypeStruct((s, d), BF16)] if with_bf16 else []
    outs = pl.pallas_call(
        body, name=name, grid=(s // tm,),
        in_specs=[row, row, vec, row],
        out_specs=[row] + [row] * len(copies) + [vec],
        out_shape=[jax.ShapeDtypeStruct((s, d), F32)] + copies + [jax.ShapeDtypeStruct((1, d), F32)],
        compiler_params=_cp(1),
    )(dh, xres, g, dres)
    return (outs[0], outs[1], outs[2]) if with_bf16 else (outs[0], None, outs[1])


def _wgrad_full(a, b, name, after=(), carry=()):
    s, k = a.shape
    n = b.shape[1]
    tk = min(512, k)
    nc = len(carry)
    c_in, c_out, c_shape, c_sems = _carry_specs(carry)

    def body(a_ref, b_ref, *rest):
        rest = rest[len(after):]
        o_ref = rest[nc]
        j = pl.program_id(0)
        _carry_run(j == 0, j == k // tk - 1, rest[:nc], rest[nc + 1:2 * nc + 1], rest[2 * nc + 1:])
        o_ref[...] = _dot(a_ref[...], b_ref[...], TN).astype(BF16)

    outs = pl.pallas_call(
        body, name=name, grid=(k // tk,),
        in_specs=[pl.BlockSpec((s, tk), lambda j: (0, j)),
                  pl.BlockSpec((s, n), lambda j: (0, 0))] + _after_specs(after) + c_in,
        out_specs=[pl.BlockSpec((tk, n), lambda j: (j, 0))] + c_out,
        out_shape=[jax.ShapeDtypeStruct((k, n), BF16)] + c_shape,
        scratch_shapes=c_sems,
        compiler_params=_cp_carry(1, carry),
    )(a, b, *after, *carry)
    return (outs[0], list(outs[1:])) if nc else outs[0]


def _wgrad_pool(p, dyb, n_groups):
    s, sw = p.shape
    d = dyb.shape[1]
    gw, go = sw // n_groups, d // n_groups
    ts = min(512, s)
    ns = s // ts

    def body(a_ref, b_ref, o_ref, acc_ref):
        i = pl.program_id(1)

        @pl.when(i == 0)
        def _():
            acc_ref[...] = jnp.zeros_like(acc_ref)

        acc_ref[...] += _dot(a_ref[...], b_ref[...], TN)

        @pl.when(i == ns - 1)
        def _():
            o_ref[...] = acc_ref[...].astype(BF16)

    return pl.pallas_call(
        body, name="wgrad_pool", grid=(n_groups, ns),
        in_specs=[pl.BlockSpec((ts, gw), lambda g, i: (i, g)),
                  pl.BlockSpec((ts, go), lambda g, i: (i, g))],
        out_specs=pl.BlockSpec((None, gw, go), lambda g, i: (g, 0, 0)),
        out_shape=jax.ShapeDtypeStruct((n_groups, gw, go), BF16),
        scratch_shapes=[pltpu.VMEM((gw, go), F32)],
        compiler_params=_cp(2),
    )(p, dyb)


def _wo_bwd(dx1b, wo, factors, sw, after=()):
    s, d = dx1b.shape
    tn = d // N_DEV
    nq = sw // tn

    def body(dx_ref, wo_ref, fya_ref, fyb_ref, fga_ref, fgb_ref, fsc_ref, *rest):
        dya_ref, dyb_ref, dp_ref, dbg_ref, dsc_ref, dm_ref = rest[len(after):]
        dbg_ref[...] = jnp.zeros_like(dbg_ref)
        dsc_ref[...] = jnp.zeros_like(dsc_ref)
        for rs in _chunks(s, 1024):
            dm_ref[rs, :] = _dot(dx_ref[rs, :], wo_ref[...], NT)
        for rs in _chunks(s, 256):
            dm = dm_ref[rs, :]
            dya_ref[rs, :] = (dm * fya_ref[rs, :].astype(F32)).astype(BF16)
            dyb_ref[rs, :] = (dm * fyb_ref[rs, :].astype(F32)).astype(BF16)
            dsc_ref[...] += jnp.sum(dm * fsc_ref[rs, :].astype(F32), axis=0, keepdims=True)
            dga = dm * fga_ref[rs, :].astype(F32)
            dgb = dm * fgb_ref[rs, :].astype(F32)
            dp_ref[0, rs, :] = dga.astype(BF16)
            dp_ref[1, rs, :] = dgb.astype(BF16)
            dbg_ref[0:1, :] += jnp.sum(dga, axis=0, keepdims=True)
            dbg_ref[1:2, :] += jnp.sum(dgb, axis=0, keepdims=True)

    col = pl.BlockSpec((s, tn), lambda j: (0, j))
    out = jax.ShapeDtypeStruct((s, d), BF16)
    return pl.pallas_call(
        body, name="wo_bwd", grid=(N_DEV,),
        in_specs=[pl.BlockSpec((s, d), lambda j: (0, 0)),
                  pl.BlockSpec((tn, d), lambda j: (j, 0))] + [col] * 5 + _after_specs(after),
        out_specs=[col, col,
                   pl.BlockSpec((2, None, s, tn), lambda j: (1, j // nq, 0, j % nq)),
                   pl.BlockSpec((2, tn), lambda j: (0, j)),
                   pl.BlockSpec((1, tn), lambda j: (0, j))],
        out_shape=[out, out, jax.ShapeDtypeStruct((4, 2, s, sw), BF16),
                   jax.ShapeDtypeStruct((2, d), F32), jax.ShapeDtypeStruct((1, d), F32)],
        scratch_shapes=[pltpu.VMEM((s, tn), F32)],
        compiler_params=_cp(1),
    )(dx1b, wo, *factors, *after)


def _conv_bwd(dproj, dya, wa, proj, conv_w, conv_b):
    s, d = dya.shape
    sw, tn = wa.shape[1], wa.shape[2]
    tc = min(LANES, sw)

    def body(dproj_hbm, dya_ref, wa_ref, ba_ref, ca_ref, va_ref, cw_ref, cb_ref,
             dp_ref, dcw_ref, dcb_ref, dz_ref):
        del dproj_hbm
        for rs in _chunks(s, 512):
            part = _dot(dya_ref[rs, 0:tn], wa_ref[0], NT)
            for j in range(1, N_DEV):
                part = part + _dot(dya_ref[rs, j * tn:(j + 1) * tn], wa_ref[j], NT)
            dz_ref[rs, :] = part
        dz = dz_ref[...]
        ba, ca, va = ba_ref[...], ca_ref[...], va_ref[...]
        cv = ca * va
        cv1, cv2 = _shift_down(cv, 1), _shift_down(cv, 2)
        w0, w1, w2 = cw_ref[0:1, :], cw_ref[1:2, :], cw_ref[2:3, :]
        u = cb_ref[...] + w0 * cv2 + w1 * cv1 + w2 * cv
        du = dz * ba
        dp_ref[0] = (dz * u).astype(BF16)
        dcv = w2 * du + w1 * _shift_up(du, 1) + w0 * _shift_up(du, 2)
        dp_ref[1] = (dcv * va).astype(BF16)
        dp_ref[2] = (dcv * ca).astype(BF16)
        dcw_ref[0:1, :] = jnp.sum(du * cv2, axis=0, keepdims=True)
        dcw_ref[1:2, :] = jnp.sum(du * cv1, axis=0, keepdims=True)
        dcw_ref[2:3, :] = jnp.sum(du * cv, axis=0, keepdims=True)
        dcb_ref[...] = jnp.sum(du, axis=0, keepdims=True)

    def part(k):
        return pl.BlockSpec((None, s, tc), lambda i: (k, 0, i))

    return pl.pallas_call(
        body, name="conv_bwd", grid=(sw // tc,),
        in_specs=[pl.BlockSpec(memory_space=pl.ANY),
                  pl.BlockSpec((s, d), lambda i: (0, 0)),
                  pl.BlockSpec((N_DEV, tc, tn), lambda i: (0, i, 0)),
                  part(0), part(1), part(2),
                  pl.BlockSpec((CONV_K, tc), lambda i: (0, i)), pl.BlockSpec((1, tc), lambda i: (0, i))],
        out_specs=[pl.BlockSpec((3, s, tc), lambda i: (0, 0, i)),
                   pl.BlockSpec((CONV_K, tc), lambda i: (0, i)), pl.BlockSpec((1, tc), lambda i: (0, i))],
        out_shape=[jax.ShapeDtypeStruct(dproj.shape, BF16),
                   jax.ShapeDtypeStruct((CONV_K, sw), F32), jax.ShapeDtypeStruct((1, sw), F32)],
        scratch_shapes=[pltpu.VMEM((s, tc), F32)],
        input_output_aliases={0: 0},
        compiler_params=_cp(1),
    )(dproj, dya, wa, proj, proj, proj, conv_w, conv_b)


def _pool_bwd(dproj, dyb, wpool):
    s, d = dyb.shape
    n_groups, gw, go = wpool.shape

    def body(dproj_hbm, dyb_ref, wp_ref, dp_ref):
        del dproj_hbm
        for gi, window in enumerate(POOL_WINDOWS):
            @pl.when(pl.program_id(0) == gi)
            def _():
                dpool = _dot(dyb_ref[...], wp_ref[...], NT)
                acc, k = dpool / _pool_counts(dpool.shape, window), 1
                while k < window:
                    acc = acc + _shift_up(acc, k)
                    k *= 2
                dp_ref[...] = (acc - dpool).astype(BF16)

    return pl.pallas_call(
        body, name="pool_bwd", grid=(n_groups,),
        in_specs=[pl.BlockSpec(memory_space=pl.ANY),
                  pl.BlockSpec((s, go), lambda g: (0, g)),
                  pl.BlockSpec((None, gw, go), lambda g: (g, 0, 0))],
        out_specs=pl.BlockSpec((None, s, gw), lambda g: (3, 0, g)),
        out_shape=jax.ShapeDtypeStruct(dproj.shape, BF16),
        input_output_aliases={0: 0},
        compiler_params=_cp(1),
    )(dproj, dyb, wpool)


def _rows128(v):
    return v.reshape(-1, LANES)


def kernel(x, norm1_g, w_in, b_gate, conv_w, conv_b, w_a_out, w_pool, pool_scale, w_o, norm2_g, w_ffn_gate, w_ffn_up, w_ffn_down, final_g, loss_target, m_norm1_g, m_w_in, m_b_gate, m_conv_w, m_conv_b, m_w_a_out, m_w_pool, m_pool_scale, m_w_o, m_norm2_g, m_w_ffn_gate, m_w_ffn_up, m_w_ffn_down, m_final_g, v_norm1_g, v_w_in, v_b_gate, v_conv_w, v_conv_b, v_w_a_out, v_w_pool, v_pool_scale, v_w_o, v_norm2_g, v_w_ffn_gate, v_w_ffn_up, v_w_ffn_down, v_final_g):
    s, d = x.shape[1], x.shape[2]
    sw = w_in.shape[2]
    n_groups = w_pool.shape[1]
    gw = w_pool.shape[2]
    go = w_pool.shape[3] * N_DEV
    f8 = w_ffn_gate.shape[2]
    cws = conv_w.shape[2]
    assert sw == conv_w.shape[2] * N_DEV == gw * n_groups and go * n_groups == d and n_groups == len(POOL_WINDOWS)

    xi, yi, ci = _coords()
    me = 4 * xi + 2 * yi + ci
    my_chip = 2 * xi + yi

    x2d = x.reshape(s, d)
    target = loss_target.reshape(s, d)
    final_g2 = final_g.reshape(1, d)
    b_gate2 = b_gate.reshape(2, d)

    big_names = ["w_in", "w_a_out", "w_pool", "w_o", "w_ffn_gate", "w_ffn_up", "w_ffn_down"]
    big_w = [w_in, w_a_out, w_pool, w_o, w_ffn_gate, w_ffn_up, w_ffn_down]
    big_m = [m_w_in, m_w_a_out, m_w_pool, m_w_o, m_w_ffn_gate, m_w_ffn_up, m_w_ffn_down]
    big_v = [v_w_in, v_w_a_out, v_w_pool, v_w_o, v_w_ffn_gate, v_w_ffn_up, v_w_ffn_down]
    shapes2d = [(w.size // w.shape[-1], w.shape[-1]) for w in big_w]
    big_w2 = [w.reshape(sh) for w, sh in zip(big_w, shapes2d)]
    transposed = (4, 5)

    def view2d(t, a):
        t2 = t.reshape(shapes2d[a])
        return t2.T if a in transposed else t2

    def unview(o, a):
        return (o.T if a in transposed else o).reshape(big_w[a].shape)

    sb = [_cast_bf16(w, "cast_" + nm) for w, nm in zip(big_w2, big_names)]
    win_g, wa_g, wpool_g, wo_g = _allgather_big(sb[0:4], "allgather_mixer", COLLECTIVE_GATHER)
    wg_g, wu_g = _allgather_big(sb[4:6], "allgather_ffn_up", COLLECTIVE_GATHER)
    (wd_g,) = _allgather_big(sb[6:7], "allgather_ffn_down", COLLECTIVE_GATHER)
    convw_g = _allgather_small(jnp.pad(conv_w.reshape(CONV_K, cws), ((0, 8 - CONV_K), (0, 0))), "allgather_conv_w")
    conv_w_full = convw_g[:, :CONV_K, :].transpose(1, 0, 2).reshape(CONV_K, sw)
    wpool = wpool_g.reshape(N_DEV, n_groups, gw, go // N_DEV).transpose(1, 2, 0, 3).reshape(n_groups, gw, go)
    wo = wo_g.reshape(d, d)

    h = _rms_fwd(x2d, norm1_g)
    proj = _proj_fwd(h, win_g)
    z = _conv_fwd(proj, conv_w_full, conv_b)
    p = _pool_fwd(proj)
    merged, *merge_factors = _merge_fwd(z, wa_g, p, wpool, proj, b_gate2, pool_scale)
    x1, h2 = _wo_fwd(merged, wo, x2d, norm2_g)
    dadu, dadg, act = _ffn_up_act_fwd(h2, wg_g, wu_g)
    dx2b, d_final_g, loss_blk = _ffn_down_loss(act, wd_g, x1, target, final_g2)

    other_chips = jnp.stack([2 * (1 - xi) + yi, 2 * xi + (1 - yi), 2 * (1 - xi) + (1 - yi)])
    others = jnp.concatenate([other_chips, 2 * other_chips + ci]).astype(jnp.int32)

    def partials(grads, recvs, names):
        return [_chip_partial(others, g3, r, "chip_partial_" + nm) for g3, r, nm in zip(grads, recvs, names)]

    own = jnp.stack([me, my_chip]).astype(jnp.int32)

    def adam(a, g3, sib, chips):
        outs = _adam_big(own, view2d(big_w[a], a), view2d(big_m[a], a), view2d(big_v[a], a),
                         g3, sib, chips, "adam_" + big_names[a])
        return [unview(o, a) for o in outs]

    big_out = [None] * len(big_names)
    dg_act, du_act = _ffn_gate_bwd(dx2b, wd_g, dadg, dadu)
    gw_gate = _wgrad_rows(dg_act, h2, "wgrad_ffn_gate")
    gw_up = _wgrad_rows(du_act, h2, "wgrad_ffn_up")
    gw_down, sib_gu = _wgrad_rows(act, dx2b, "wgrad_ffn_down", carry=[gw_gate, gw_up])
    ps_gu = partials([gw_gate, gw_up], sib_gu, ["w_ffn_gate", "w_ffn_up"])
    chips_gu = _exchange_chips(ps_gu, "rs_chips_ffn_up", COLLECTIVE_CHIPS)
    dh2, sib_down = _input_grad([(dg_act, wg_g), (du_act, wu_g)], "ffn_in_bwd", after=ps_gu, carry=[gw_down])
    ps_down = partials([gw_down], sib_down, ["w_ffn_down"])
    chips_down = _exchange_chips(ps_down, "rs_chips_ffn_down", COLLECTIVE_CHIPS)
    dx1, dx1b, d_norm2_g = _rms_bwd(dh2, x1, norm2_g, dx2b, "rms2_bwd")
    dya, dyb, dproj42, d_b_gate, d_pool_scale = _wo_bwd(dx1b, wo, merge_factors, sw, after=ps_down)
    dproj = dproj42.reshape(N_DEV, s, sw)
    dproj, d_conv_w, d_conv_b = _conv_bwd(dproj, dya, wa_g, proj, conv_w_full, conv_b)
    dproj = _pool_bwd(dproj, dyb, wpool)
    gw_in = _wgrad_cols(h, dproj, "wgrad_in")
    gw_o, sib_in = _wgrad_full(merged, dx1b, "wgrad_o", carry=[gw_in])
    ps_in = partials([gw_in], sib_in, ["w_in"])
    chips_in = _exchange_chips(ps_in, "rs_chips_w_in", COLLECTIVE_CHIPS)
    gw_a = _wgrad_cols(z, dya, "wgrad_a_out", after=ps_in)
    gw_pool = _wgrad_pool(p, dyb, n_groups)
    mix3 = [gw_a,
            gw_pool.reshape(n_groups, gw, N_DEV, go // N_DEV).transpose(2, 0, 1, 3).reshape(N_DEV, n_groups * gw, go // N_DEV),
            gw_o.reshape(N_DEV, d // N_DEV, d)]
    big_out[4] = adam(4, gw_gate, sib_gu[0], chips_gu[0])
    big_out[5] = adam(5, gw_up, sib_gu[1], chips_gu[1])
    big_out[6] = adam(6, gw_down, sib_down[0], chips_down[0])
    dh, sib_mix = _input_grad([(dproj, win_g)], "proj_in_bwd",
                              after=[big_out[4][0], big_out[5][0], big_out[6][0]], carry=mix3)
    ps_mix = partials(mix3, sib_mix, ["w_a_out", "w_pool", "w_o"])
    chips_mix = _exchange_chips(ps_mix, "rs_chips_mixer", COLLECTIVE_CHIPS)
    grad_x, _, d_norm1_g = _rms_bwd(dh, x2d, norm1_g, dx1, "rms1_bwd", with_bf16=False)
    big_out[0] = adam(0, gw_in, sib_in[0], chips_in[0])
    for k in range(3):
        big_out[1 + k] = adam(1 + k, mix3[k], sib_mix[k], chips_mix[k])

    small_parts = [d_norm1_g, d_b_gate, d_conv_w, d_conv_b, d_pool_scale, d_norm2_g, d_final_g, loss_blk]
    rows = [v.size // LANES for v in small_parts]
    row0 = [sum(rows[:k]) for k in range(len(rows))]
    packed = jnp.concatenate([_rows128(v) for v in small_parts], axis=0)
    gathered = _allgather_small(packed, "allgather_small_grads")
    small_names = ["norm1_g", "b_gate", "conv_b", "pool_scale", "norm2_g", "final_g", "conv_w"]
    small_w = [norm1_g, b_gate, conv_b, pool_scale, norm2_g, final_g]
    small_m = [m_norm1_g, m_b_gate, m_conv_b, m_pool_scale, m_norm2_g, m_final_g]
    small_v = [v_norm1_g, v_b_gate, v_conv_b, v_pool_scale, v_norm2_g, v_final_g]
    finished = _small_finish(gathered, [tuple(_rows128(t) for t in wmv) for wmv in zip(small_w, small_m, small_v)],
                             [row0[k] for k in (0, 1, 3, 4, 5, 6)], [(row0[2], rows[2]), (row0[7], rows[7])])
    g_convw_full, loss_rows = finished[0], finished[1]
    loss = loss_rows[0, 0]
    small_out = [[t.reshape(w.shape) for t in finished[2 + 4 * k:6 + 4 * k]] for k, w in enumerate(small_w)]
    g_convw = lax.dynamic_slice(g_convw_full.reshape(CONV_K, sw), (0, me * cws), (CONV_K, cws))
    cw_delta, cw_m, cw_v = _adam_small(conv_w.reshape(CONV_K, cws), g_convw,
                                       m_conv_w.reshape(CONV_K, cws), v_conv_w.reshape(CONV_K, cws))
    small_out.append([t.reshape(conv_w.shape) for t in (g_convw, cw_delta, cw_m, cw_v)])

    order = ["norm1_g", "w_in", "b_gate", "conv_w", "conv_b", "w_a_out", "w_pool", "pool_scale", "w_o", "norm2_g",
             "w_ffn_gate", "w_ffn_up", "w_ffn_down", "final_g"]
    per_kind = [{}, {}, {}, {}]
    for a, nm in enumerate(big_names):
        for kind in range(4):
            per_kind[kind][nm] = big_out[a][kind]
    for k, nm in enumerate(small_names):
        for kind in range(4):
            per_kind[kind][nm] = small_out[k][kind]
    result = [loss, grad_x.reshape(x.shape)]
    for kind in range(4):
        result += [per_kind[kind][nm] for nm in order]
    return tuple(result)
```

```python
import jax
import jax.numpy as jnp
from jax import lax
from jax.experimental import pallas as pl
from jax.experimental.pallas import tpu as pltpu
from jax.experimental.pallas import tpu_sc as plsc

F32 = jnp.float32
BF16 = jnp.bfloat16
MESH = pl.DeviceIdType.MESH

N_DEV = 8
EPS = 1e-6
CONV_K = 3
POOL_WINDOWS = (2, 4, 8, 16)
ADAM_LR = 0.001
ADAM_B1 = 0.9
ADAM_B2 = 0.999
ADAM_EPS = 1e-08
ADAM_WD = 0.01
ADAM_STEP = 10

V7X_VMEM_LIMIT_BYTES = 56 * 1024 * 1024
LANES = 128

COLLECTIVE_GATHER = 1
COLLECTIVE_SIBLING = 2
COLLECTIVE_CHIPS = 3
SEQUENCER_COST_BYTES = 4 * 10**9

NN = ((1,), (0,))
NT = ((1,), (1,))
TN = ((0,), (0,))


def _dot(a, b, dims):
    return lax.dot_general(a, b, (dims, ((), ())), preferred_element_type=F32)


def _cp(n_axes):
    return pltpu.CompilerParams(dimension_semantics=("arbitrary",) * n_axes,
                                vmem_limit_bytes=V7X_VMEM_LIMIT_BYTES)


def _row_tile(rows, bytes_per_row, cap_bytes):
    best = None
    for t in range(16, rows + 1, 16):
        if rows % t == 0 and t * bytes_per_row <= cap_bytes:
            best = t
    return best if best is not None else rows


def _chunks(total, size):
    size = min(size, total)
    assert total % size == 0
    return [slice(r, r + size) for r in range(0, total, size)]


def _after_specs(after):
    return [pl.BlockSpec(memory_space=pl.ANY)] * len(after)


def _shift_down(v, k):
    row = lax.broadcasted_iota(jnp.int32, v.shape, 0)
    return jnp.where(row >= k, pltpu.roll(v, k, 0), 0.0)


def _shift_up(v, k):
    n = v.shape[0]
    row = lax.broadcasted_iota(jnp.int32, v.shape, 0)
    return jnp.where(row < n - k, pltpu.roll(v, n - k, 0), 0.0)


def _sigmoid(v):
    return jax.nn.sigmoid(v)


def _cast_bf16(w2d, name):
    rows, cols = w2d.shape
    tr = _row_tile(rows, cols * 4, 2 << 20)

    def body(i_ref, o_ref):
        o_ref[...] = i_ref[...].astype(BF16)

    return pl.pallas_call(
        body, name=name, grid=(rows // tr,),
        in_specs=[pl.BlockSpec((tr, cols), lambda i: (i, 0))],
        out_specs=pl.BlockSpec((tr, cols), lambda i: (i, 0)),
        out_shape=jax.ShapeDtypeStruct((rows, cols), BF16),
        compiler_params=_cp(1),
    )(w2d)


def _rms_fwd(x2d, g):
    s, d = x2d.shape
    tm = min(256, s)

    def body(x_ref, g_ref, h_ref):
        xv = x_ref[...]
        r = lax.rsqrt(jnp.mean(xv * xv, axis=-1, keepdims=True) + EPS)
        h_ref[...] = (xv * r * g_ref[...]).astype(BF16)

    return pl.pallas_call(
        body, name="rms1_fwd", grid=(s // tm,),
        in_specs=[pl.BlockSpec((tm, d), lambda i: (i, 0)), pl.BlockSpec((1, d), lambda i: (0, 0))],
        out_specs=pl.BlockSpec((tm, d), lambda i: (i, 0)),
        out_shape=jax.ShapeDtypeStruct((s, d), BF16),
        compiler_params=_cp(1),
    )(x2d, g)


def _coords():
    return lax.axis_index("x"), lax.axis_index("y"), lax.axis_index("c")


def _slot(p):
    return 4 * p[0] + 2 * p[1] + p[2]


def _handshake(peers):
    barrier = pltpu.get_barrier_semaphore()
    for peer in peers:
        pl.semaphore_signal(barrier, inc=1, device_id=peer, device_id_type=MESH)
    pl.semaphore_wait(barrier, len(peers))


def _sequencer_call(body, out_type, scratch_types, name, collective_id):
    return pl.kernel(
        body, out_type=out_type, name=name,
        mesh=plsc.ScalarSubcoreMesh(axis_name="seq", num_cores=1),
        scratch_types=scratch_types,
        cost_estimate=pl.CostEstimate(flops=0, transcendentals=0, bytes_accessed=SEQUENCER_COST_BYTES),
        compiler_params=pltpu.CompilerParams(collective_id=collective_id))


def _allgather_big(shards, name, collective_id, after=()):
    n = len(shards)

    def body(*refs):
        ins, outs = refs[:n], refs[n + len(after):2 * n + len(after)]
        send_sems, recv_sems, local_sems = refs[2 * n + len(after):]
        x, y, c = _coords()
        me, sibling = (x, y, c), (x, y, 1 - c)
        x_nbr, y_nbr, diag = (1 - x, y), (x, 1 - y), (1 - x, 1 - y)
        relay_from = (x + (1 - c) * (1 - 2 * x), y + c * (1 - 2 * y))
        relay_to = (x + c * (1 - 2 * x), y + (1 - c) * (1 - 2 * y))
        _handshake([sibling, (*x_nbr, c), (*y_nbr, c)])

        def copy(a, k, block, to, src=None):
            dst = outs[a].at[_slot(block)]
            return pltpu.make_async_remote_copy(
                src_ref=dst if src is None else src, dst_ref=dst,
                send_sem=send_sems.at[a, k], recv_sem=recv_sems.at[a, k],
                device_id=to, device_id_type=MESH)

        mine, sends = [], []
        for a in range(n):
            cp = pltpu.make_async_copy(ins[a], outs[a].at[_slot(me)], local_sems.at[a])
            cp.start()
            mine.append(cp)
            first = [copy(a, 0, me, sibling, src=ins[a]),
                     copy(a, 1, me, (*x_nbr, c), src=ins[a]),
                     copy(a, 2, me, (*y_nbr, c), src=ins[a])]
            for cp in first:
                cp.start()
            sends += first
        for a in range(n):
            copy(a, 1 + c, (*relay_from, c), me).wait_recv()
            passed = [copy(a, 3, (*relay_from, c), (*relay_to, c)), copy(a, 4 + c, (*relay_from, c), sibling)]
            for cp in passed:
                cp.start()
            copy(a, 2 - c, (*relay_to, c), me).wait_recv()
            cp = copy(a, 5 - c, (*relay_to, c), sibling)
            cp.start()
            passed.append(cp)
            copy(a, 3, (*diag, c), me).wait_recv()
            cp = copy(a, 6, (*diag, c), sibling)
            cp.start()
            sends += passed + [cp]
        for a in range(n):
            copy(a, 0, sibling, me).wait_recv()
            copy(a, 4, (*x_nbr, 1 - c), me).wait_recv()
            copy(a, 5, (*y_nbr, 1 - c), me).wait_recv()
            copy(a, 6, (*diag, 1 - c), me).wait_recv()
        for cp in sends:
            cp.wait_send()
        for cp in mine:
            cp.wait()

    return _sequencer_call(
        body, [jax.ShapeDtypeStruct((N_DEV,) + s.shape, s.dtype) for s in shards],
        [pltpu.SemaphoreType.DMA((n, 7)), pltpu.SemaphoreType.DMA((n, 7)), pltpu.SemaphoreType.DMA((n,))],
        name, collective_id)(*shards, *after)


def _sibling_copies(ins, recvs, send_sems, recv_sems):
    x, y, c = _coords()
    return [pltpu.make_async_remote_copy(
        src_ref=ins[a].at[2 * q + (1 - c)], dst_ref=recvs[a].at[q],
        send_sem=send_sems.at[a, q], recv_sem=recv_sems.at[a, q],
        device_id=(x, y, 1 - c), device_id_type=MESH) for a in range(len(ins)) for q in range(4)]


def _carry_specs(carry):
    any_spec = pl.BlockSpec(memory_space=pl.ANY)
    n = len(carry)
    sems = [pltpu.SemaphoreType.DMA((n, 4)), pltpu.SemaphoreType.DMA((n, 4))] if n else []
    return ([any_spec] * n, [any_spec] * n,
            [jax.ShapeDtypeStruct((4,) + g.shape[1:], g.dtype) for g in carry], sems)


def _carry_run(first, last, ins, recvs, sems):
    if not ins:
        return

    @pl.when(first)
    def _():
        x, y, c = _coords()
        _handshake([(x, y, 1 - c)])
        for cp in _sibling_copies(ins, recvs, *sems):
            cp.start()

    @pl.when(last)
    def _():
        copies = _sibling_copies(ins, recvs, *sems)
        for cp in copies:
            cp.wait_recv()
        for cp in copies:
            cp.wait_send()


def _cp_carry(n_axes, carry):
    if not carry:
        return _cp(n_axes)
    return pltpu.CompilerParams(dimension_semantics=("arbitrary",) * n_axes, vmem_limit_bytes=V7X_VMEM_LIMIT_BYTES,
                                collective_id=COLLECTIVE_SIBLING)


def _exchange_chips(psums, name, collective_id):
    n = len(psums)

    def body(*refs):
        ins, outs = refs[:n], refs[n:2 * n]
        send_sems, recv_sems = refs[2 * n:]
        x, y, c = _coords()
        chips = [(1 - x, y), (x, 1 - y), (1 - x, 1 - y)]
        _handshake([(*chip, c) for chip in chips])
        copies = []
        for a in range(n):
            for j, chip in enumerate(chips):
                cp = pltpu.make_async_remote_copy(
                    src_ref=ins[a].at[2 * chip[0] + chip[1]], dst_ref=outs[a].at[j],
                    send_sem=send_sems.at[a, j], recv_sem=recv_sems.at[a, j],
                    device_id=(*chip, c), device_id_type=MESH)
                cp.start()
                copies.append(cp)
        for cp in copies:
            cp.wait_recv()
        for cp in copies:
            cp.wait_send()

    return _sequencer_call(
        body, [jax.ShapeDtypeStruct((3,) + p.shape[1:], p.dtype) for p in psums],
        [pltpu.SemaphoreType.DMA((n, 3)), pltpu.SemaphoreType.DMA((n, 3))],
        name, collective_id)(*psums)


def _allgather_small(v2d, name):
    rows, cols = v2d.shape

    def body(v_ref, out_ref, send_sems, recv_sems):
        x, y, c = _coords()
        me = (x, y, c)
        out_ref[_slot(me)] = v_ref[...]
        peers = []
        for k in range(1, N_DEV):
            fx, fy, fc = (k >> 2) & 1, (k >> 1) & 1, k & 1
            peers.append(((1 - x) if fx else x, (1 - y) if fy else y, (1 - c) if fc else c))
        sends = []
        for k, peer in enumerate(peers):
            cp = pltpu.make_async_remote_copy(
                src_ref=v_ref, dst_ref=out_ref.at[_slot(me)],
                send_sem=send_sems.at[k], recv_sem=recv_sems.at[k],
                device_id=peer, device_id_type=MESH)
            cp.start()
            sends.append(cp)
        for k, peer in enumerate(peers):
            pltpu.make_async_remote_copy(
                src_ref=v_ref, dst_ref=out_ref.at[_slot(peer)],
                send_sem=send_sems.at[k], recv_sem=recv_sems.at[k],
                device_id=peer, device_id_type=MESH).wait_recv()
        for cp in sends:
            cp.wait_send()

    vmem = pl.BlockSpec(memory_space=pltpu.VMEM)
    return pl.pallas_call(
        body, name=name, in_specs=[vmem], out_specs=vmem,
        out_shape=jax.ShapeDtypeStruct((N_DEV, rows, cols), v2d.dtype),
        scratch_shapes=[pltpu.SemaphoreType.DMA((N_DEV - 1,)), pltpu.SemaphoreType.DMA((N_DEV - 1,))],
    )(v2d)


def _chip_partial(others, grads, recvs, name):
    n = len(grads)
    _, rows, cols = grads[0].shape
    tr = _row_tile(rows, cols * 2, (2 << 20) // n)

    def body(others_ref, *refs):
        for a in range(n):
            refs[2 * n + a][...] = (refs[a][...].astype(F32) + refs[n + a][...].astype(F32)).astype(BF16)

    return pl.pallas_call(
        body, name=name,
        grid_spec=pltpu.PrefetchScalarGridSpec(
            num_scalar_prefetch=1, grid=(3, rows // tr),
            in_specs=[pl.BlockSpec((None, tr, cols), lambda k, i, o: (o[3 + k], i, 0))] * n
            + [pl.BlockSpec((None, tr, cols), lambda k, i, o: (o[k], i, 0))] * n,
            out_specs=[pl.BlockSpec((None, tr, cols), lambda k, i, o: (o[k], i, 0))] * n),
        out_shape=[jax.ShapeDtypeStruct((4, rows, cols), BF16)] * n,
        compiler_params=_cp(2),
    )(others, *grads, *recvs)


def _adam_math(w, g, m, v):
    m = ADAM_B1 * m + (1.0 - ADAM_B1) * g
    v = ADAM_B2 * v + (1.0 - ADAM_B2) * (g * g)
    m_hat = m / (1.0 - ADAM_B1 ** ADAM_STEP)
    v_hat = v / (1.0 - ADAM_B2 ** ADAM_STEP)
    delta = -ADAM_LR * (m_hat / (jnp.sqrt(v_hat) + ADAM_EPS) + ADAM_WD * w)
    return delta, m, v


def _adam_big(own, wmvs, g3s, recv_sibs, recv_chipss, name):
    n = len(wmvs)
    rows, cols = wmvs[0][0].shape
    tr = _row_tile(rows, cols * 4, (2 << 20) // n)

    def body(own_ref, *refs):
        ins, outs = refs[:6 * n], refs[6 * n:]
        for a in range(n):
            w_ref, m_ref, v_ref, g_ref, rs_ref, rc_ref = ins[6 * a:6 * a + 6]
            g = g_ref[...].astype(F32) + rs_ref[...].astype(F32)
            g = g + rc_ref[0].astype(F32)
            g = g + rc_ref[1].astype(F32)
            g = g + rc_ref[2].astype(F32)
            delta, m_new, v_new = _adam_math(w_ref[...], g, m_ref[...], v_ref[...])
            outs[4 * a][...] = g
            outs[4 * a + 1][...] = delta
            outs[4 * a + 2][...] = m_new
            outs[4 * a + 3][...] = v_new

    blk = pl.BlockSpec((tr, cols), lambda i, o: (i, 0))
    per_shard = [blk, blk, blk,
                 pl.BlockSpec((None, tr, cols), lambda i, o: (o[0], i, 0)),
                 pl.BlockSpec((None, tr, cols), lambda i, o: (o[1], i, 0)),
                 pl.BlockSpec((3, tr, cols), lambda i, o: (0, i, 0))]
    out = jax.ShapeDtypeStruct((rows, cols), F32)
    args = [t for a in range(n) for t in (*wmvs[a], g3s[a], recv_sibs[a], recv_chipss[a])]
    outs = pl.pallas_call(
        body, name=name,
        grid_spec=pltpu.PrefetchScalarGridSpec(
            num_scalar_prefetch=1, grid=(rows // tr,),
            in_specs=per_shard * n, out_specs=[blk] * (4 * n)),
        out_shape=[out] * (4 * n),
        compiler_params=_cp(1),
    )(own, *args)
    return [outs[4 * a:4 * a + 4] for a in range(n)]


def _small_finish(gathered, params, row_offs, extra_rows):
    n = len(params)

    def body(g_ref, *refs):
        ins, outs = refs[:3 * n], refs[3 * n:]
        total = g_ref[0]
        for k in range(1, N_DEV):
            total = total + g_ref[k]
        for e, (r0, nr) in enumerate(extra_rows):
            outs[e][...] = total[r0:r0 + nr, :]
        for p in range(n):
            w_ref, m_ref, v_ref = ins[3 * p:3 * p + 3]
            g_out, d_out, m_out, v_out = outs[len(extra_rows) + 4 * p:len(extra_rows) + 4 * p + 4]
            g = total[row_offs[p]:row_offs[p] + w_ref.shape[0], :]
            delta, m_new, v_new = _adam_math(w_ref[...], g, m_ref[...], v_ref[...])
            g_out[...] = g
            d_out[...] = delta
            m_out[...] = m_new
            v_out[...] = v_new

    vmem = pl.BlockSpec(memory_space=pltpu.VMEM)
    out_shape = [jax.ShapeDtypeStruct((nr, LANES), F32) for _, nr in extra_rows]
    for w, _, _ in params:
        out_shape += [jax.ShapeDtypeStruct(w.shape, F32)] * 4
    flat = [t for wmv in params for t in wmv]
    return pl.pallas_call(body, name="small_finish", in_specs=[vmem] * (1 + len(flat)),
                          out_specs=[vmem] * len(out_shape), out_shape=out_shape)(gathered, *flat)


def _adam_small(w, g, m, v):
    def body(w_ref, g_ref, m_ref, v_ref, do_ref, mo_ref, vo_ref):
        delta, m_new, v_new = _adam_math(w_ref[...], g_ref[...], m_ref[...], v_ref[...])
        do_ref[...] = delta
        mo_ref[...] = m_new
        vo_ref[...] = v_new

    vmem = pl.BlockSpec(memory_space=pltpu.VMEM)
    out = jax.ShapeDtypeStruct(w.shape, F32)
    return pl.pallas_call(body, name="adam_small", in_specs=[vmem] * 4, out_specs=[vmem] * 3,
                          out_shape=[out, out, out])(w, g, m, v)


def _proj_fwd(h, win_g):
    s, d = h.shape
    sw = win_g.shape[2]
    tn = min(512, sw)
    nh = sw // tn

    def body(h_ref, w_ref, o_ref):
        for rs in _chunks(s, 512):
            o_ref[rs, :] = _dot(h_ref[rs, :], w_ref[...], NN)

    return pl.pallas_call(
        body, name="proj_fwd", grid=(N_DEV * nh,),
        in_specs=[pl.BlockSpec((s, d), lambda j: (0, 0)),
                  pl.BlockSpec((None, d, tn), lambda j: (j // nh, 0, j % nh))],
        out_specs=pl.BlockSpec((None, s, tn), lambda j: (j // nh, 0, j % nh)),
        out_shape=jax.ShapeDtypeStruct((N_DEV, s, sw), F32),
        compiler_params=_cp(1),
    )(h, win_g)


def _conv_fwd(proj, conv_w, conv_b):
    _, s, sw = proj.shape
    tc = min(LANES, sw)

    def body(ba_ref, ca_ref, va_ref, cw_ref, cb_ref, z_ref):
        cv = ca_ref[...] * va_ref[...]
        u = (cb_ref[...] + cw_ref[0:1, :] * _shift_down(cv, 2) + cw_ref[1:2, :] * _shift_down(cv, 1)
             + cw_ref[2:3, :] * cv)
        z_ref[...] = (ba_ref[...] * u).astype(BF16)

    def part(k):
        return pl.BlockSpec((None, s, tc), lambda i: (k, 0, i))

    return pl.pallas_call(
        body, name="conv_fwd", grid=(sw // tc,),
        in_specs=[part(0), part(1), part(2),
                  pl.BlockSpec((CONV_K, tc), lambda i: (0, i)), pl.BlockSpec((1, tc), lambda i: (0, i))],
        out_specs=pl.BlockSpec((s, tc), lambda i: (0, i)),
        out_shape=jax.ShapeDtypeStruct((s, sw), BF16),
        compiler_params=_cp(1),
    )(proj, proj, proj, conv_w, conv_b)


def _pool_counts(shape, window):
    t = lax.broadcasted_iota(jnp.int32, shape, 0)
    return jnp.minimum(t + 1, window).astype(F32)


def _pool_fwd(proj):
    _, s, sw = proj.shape
    gw = sw // len(POOL_WINDOWS)

    def body(v_ref, p_ref):
        for gi, window in enumerate(POOL_WINDOWS):
            @pl.when(pl.program_id(0) == gi)
            def _():
                v = v_ref[...]
                acc, k = v, 1
                while k < window:
                    acc = acc + _shift_down(acc, k)
                    k *= 2
                p_ref[...] = (acc / _pool_counts(v.shape, window) - v).astype(BF16)

    return pl.pallas_call(
        body, name="pool_fwd", grid=(len(POOL_WINDOWS),),
        in_specs=[pl.BlockSpec((None, s, gw), lambda g: (3, 0, g))],
        out_specs=pl.BlockSpec((s, gw), lambda g: (0, g)),
        out_shape=jax.ShapeDtypeStruct((s, sw), BF16),
        compiler_params=_cp(1),
    )(proj)


def _merge_fwd(z, wa, p, wpool, proj, b_gate2, pool_scale):
    s, sw = z.shape
    tn = wa.shape[2]
    d = tn * N_DEV
    gw = sw // len(POOL_WINDOWS)
    nq = sw // tn

    def body(z_ref, wa_ref, p_ref, wp_ref, ga_ref, gb_ref, bg_ref, sc_ref,
             m_ref, dya_ref, dyb_ref, dga_ref, dgb_ref, dsc_ref):
        for rs in _chunks(s, 512):
            ya = _dot(z_ref[rs, :], wa_ref[...], NN)
            yb = _dot(p_ref[rs, :], wp_ref[...], NN)
            sa = _sigmoid(ga_ref[rs, :] + bg_ref[0:1, :])
            sb = _sigmoid(gb_ref[rs, :] + bg_ref[1:2, :])
            sc = sc_ref[...]
            sb_yb = sb * yb
            m_ref[rs, :] = (sa * ya + sb_yb * sc).astype(BF16)
            dya_ref[rs, :] = sa.astype(BF16)
            dyb_ref[rs, :] = (sb * sc).astype(BF16)
            dga_ref[rs, :] = (ya * (sa * (1.0 - sa))).astype(BF16)
            dgb_ref[rs, :] = ((yb * sc) * (sb * (1.0 - sb))).astype(BF16)
            dsc_ref[rs, :] = sb_yb.astype(BF16)

    col = pl.BlockSpec((s, tn), lambda j: (0, j))
    out = jax.ShapeDtypeStruct((s, d), BF16)
    return pl.pallas_call(
        body, name="merge_fwd", grid=(N_DEV,),
        in_specs=[pl.BlockSpec((s, sw), lambda j: (0, 0)),
                  pl.BlockSpec((None, sw, tn), lambda j: (j, 0, 0)),
                  pl.BlockSpec((s, gw), lambda j: (0, j // 2)),
                  pl.BlockSpec((None, gw, tn), lambda j: (j // 2, 0, j % 2)),
                  pl.BlockSpec((None, s, tn), lambda j: (4 + j // nq, 0, j % nq)),
                  pl.BlockSpec((None, s, tn), lambda j: (6 + j // nq, 0, j % nq)),
                  pl.BlockSpec((2, tn), lambda j: (0, j)),
                  pl.BlockSpec((1, tn), lambda j: (0, j))],
        out_specs=[col] * 6,
        out_shape=[out] * 6,
        compiler_params=_cp(1),
    )(z, wa, p, wpool, proj, proj, b_gate2, pool_scale)


def _wo_fwd(merged, wo, x2d, g2):
    s, d = x2d.shape
    tm = min(256, s)

    def body(m_ref, wo_ref, x_ref, g_ref, x1_ref, h2_ref):
        x1 = x_ref[...] + _dot(m_ref[...], wo_ref[...], NN)
        x1_ref[...] = x1
        r = lax.rsqrt(jnp.mean(x1 * x1, axis=-1, keepdims=True) + EPS)
        h2_ref[...] = (x1 * r * g_ref[...]).astype(BF16)

    row = pl.BlockSpec((tm, d), lambda i: (i, 0))
    return pl.pallas_call(
        body, name="wo_fwd", grid=(s // tm,),
        in_specs=[row, pl.BlockSpec((d, d), lambda i: (0, 0)), row, pl.BlockSpec((1, d), lambda i: (0, 0))],
        out_specs=[row, row],
        out_shape=[jax.ShapeDtypeStruct((s, d), F32), jax.ShapeDtypeStruct((s, d), BF16)],
        compiler_params=_cp(1),
    )(merged, wo, x2d, g2)


def _ffn_up_act_fwd(h2, wg_g, wu_g):
    s, d = h2.shape
    f8 = wg_g.shape[2]
    th = min(1024, s)

    def body(h_ref, wg_ref, wu_ref, dadu_ref, dadg_ref, a_ref):
        i = pl.program_id(1)
        for rs in _chunks(th, 512):
            rows = pl.ds(pl.multiple_of(i * th + rs.start, rs.stop - rs.start), rs.stop - rs.start)
            a = h_ref[rows, :]
            g = _dot(a, wg_ref[...], NN)
            u = _dot(a, wu_ref[...], NN)
            sg = _sigmoid(g)
            silu = g * sg
            dadu_ref[rs, :] = silu.astype(BF16)
            dadg_ref[rs, :] = (u * (sg * (1.0 + g * (1.0 - sg)))).astype(BF16)
            a_ref[rs, :] = (silu * u).astype(BF16)

    wspec = pl.BlockSpec((None, d, f8), lambda j, i: (j, 0, 0))
    ospec = pl.BlockSpec((None, th, f8), lambda j, i: (j, i, 0))
    out = jax.ShapeDtypeStruct((N_DEV, s, f8), BF16)
    return pl.pallas_call(
        body, name="ffn_up_fwd", grid=(N_DEV, s // th),
        in_specs=[pl.BlockSpec((s, d), lambda j, i: (0, 0)), wspec, wspec],
        out_specs=[ospec, ospec, ospec], out_shape=[out, out, out],
        compiler_params=_cp(2),
    )(h2, wg_g, wu_g)


def _ffn_down_fwd(act, wd_g):
    _, s, f8 = act.shape
    d = wd_g.shape[2]
    tn = min(1024, d)

    def body(a_ref, wd_ref, o_ref):
        j = pl.program_id(1)

        @pl.when(j == 0)
        def _():
            o_ref[...] = jnp.zeros_like(o_ref)

        for rs in _chunks(s, 1024):
            o_ref[rs, :] += _dot(a_ref[rs, :], wd_ref[...], NN)

    return pl.pallas_call(
        body, name="ffn_down_fwd", grid=(d // tn, N_DEV),
        in_specs=[pl.BlockSpec((None, s, f8), lambda n, j: (j, 0, 0)),
                  pl.BlockSpec((None, f8, tn), lambda n, j: (j, 0, n))],
        out_specs=pl.BlockSpec((s, tn), lambda n, j: (0, n)),
        out_shape=jax.ShapeDtypeStruct((s, d), F32),
        compiler_params=_cp(2),
    )(act, wd_g)


def _loss_bwd(ffn_out, x1, target, final_g):
    s, d = x1.shape
    tm = min(256, s)

    def body(f_ref, x1_ref, t_ref, gf_ref, dxb_ref, dgf_ref, loss_ref):
        @pl.when(pl.program_id(0) == 0)
        def _():
            dgf_ref[...] = jnp.zeros_like(dgf_ref)
            loss_ref[...] = jnp.zeros_like(loss_ref)

        x2 = x1_ref[...] + f_ref[...]
        r = lax.rsqrt(jnp.mean(x2 * x2, axis=-1, keepdims=True) + EPS)
        nrm = x2 * r
        gf = gf_ref[...]
        err = nrm * gf - t_ref[...]
        loss_ref[...] += jnp.sum(err * err) * (0.5 / d)
        dy = err * (1.0 / d)
        dgf_ref[...] += jnp.sum(dy * nrm, axis=0, keepdims=True)
        dn = dy * gf
        dx = r * (dn - nrm * jnp.mean(dn * nrm, axis=-1, keepdims=True))
        dxb_ref[...] = dx.astype(BF16)

    row = pl.BlockSpec((tm, d), lambda i: (i, 0))
    vec = pl.BlockSpec((1, d), lambda i: (0, 0))
    return pl.pallas_call(
        body, name="loss_bwd", grid=(s // tm,),
        in_specs=[row, row, row, vec],
        out_specs=[row, vec, pl.BlockSpec((8, LANES), lambda i: (0, 0))],
        out_shape=[jax.ShapeDtypeStruct((s, d), BF16),
                   jax.ShapeDtypeStruct((1, d), F32), jax.ShapeDtypeStruct((8, LANES), F32)],
        compiler_params=_cp(1),
    )(ffn_out, x1, target, final_g)


def _ffn_gate_bwd(dx2b, wd_g, dadg, dadu):
    s, d = dx2b.shape
    f8 = dadg.shape[2]
    th = min(1024, s)

    def body(dx_ref, wd_ref, g_ref, u_ref, dg_ref, du_ref, da_ref):
        i = pl.program_id(1)
        chunks = _chunks(th, 256)

        def matmul(rs):
            rows = pl.ds(pl.multiple_of(i * th + rs.start, rs.stop - rs.start), rs.stop - rs.start)
            da_ref[rs, :] = _dot(dx_ref[rows, :], wd_ref[...], NT)

        matmul(chunks[0])
        for k, rs in enumerate(chunks):
            if k + 1 < len(chunks):
                matmul(chunks[k + 1])
            da = da_ref[rs, :]
            dg_ref[rs, :] = (da * g_ref[rs, :].astype(F32)).astype(BF16)
            du_ref[rs, :] = (da * u_ref[rs, :].astype(F32)).astype(BF16)

    aspec = pl.BlockSpec((None, th, f8), lambda j, i: (j, i, 0))
    out = jax.ShapeDtypeStruct((N_DEV, s, f8), BF16)
    return pl.pallas_call(
        body, name="ffn_act_bwd", grid=(N_DEV, s // th),
        in_specs=[pl.BlockSpec((s, d), lambda j, i: (0, 0)),
                  pl.BlockSpec((None, f8, d), lambda j, i: (j, 0, 0)), aspec, aspec],
        out_specs=[aspec, aspec], out_shape=[out, out],
        scratch_shapes=[pltpu.VMEM((th, f8), F32)],
        compiler_params=_cp(2),
    )(dx2b, wd_g, dadg, dadu)


def _wgrad_rows(a3, b, name, after=(), carry=()):
    _, s, k = a3.shape
    n = b.shape[1]
    nc = len(carry)
    c_in, c_out, c_shape, c_sems = _carry_specs(carry)

    def body(a_ref, b_ref, *rest):
        rest = rest[len(after):]
        o_ref = rest[nc]
        j = pl.program_id(0)
        _carry_run(j == 0, j == N_DEV - 1, rest[:nc], rest[nc + 1:2 * nc + 1], rest[2 * nc + 1:])
        o_ref[...] = _dot(a_ref[...], b_ref[...], TN).astype(BF16)

    outs = pl.pallas_call(
        body, name=name, grid=(N_DEV,),
        in_specs=[pl.BlockSpec((None, s, k), lambda j: (j, 0, 0)),
                  pl.BlockSpec((s, n), lambda j: (0, 0))] + _after_specs(after) + c_in,
        out_specs=[pl.BlockSpec((None, k, n), lambda j: (j, 0, 0))] + c_out,
        out_shape=[jax.ShapeDtypeStruct((N_DEV, k, n), BF16)] + c_shape,
        scratch_shapes=c_sems,
        compiler_params=_cp_carry(1, carry),
    )(a3, b, *after, *carry)
    return (outs[0], list(outs[1:])) if nc else outs[0]


def _wgrad_cols(a, b3, name, after=()):
    s, k = a.shape
    if b3.ndim == 2:
        n = b3.shape[1] // N_DEV
        b_spec = pl.BlockSpec((s, n), lambda j: (0, j))
    else:
        n = b3.shape[2]
        b_spec = pl.BlockSpec((None, s, n), lambda j: (j, 0, 0))

    def body(a_ref, b_ref, *rest):
        o_ref = rest[len(after)]
        o_ref[...] = _dot(a_ref[...], b_ref[...], TN).astype(BF16)

    return pl.pallas_call(
        body, name=name, grid=(N_DEV,),
        in_specs=[pl.BlockSpec((s, k), lambda j: (0, 0)), b_spec] + _after_specs(after),
        out_specs=pl.BlockSpec((None, k, n), lambda j: (j, 0, 0)),
        out_shape=jax.ShapeDtypeStruct((N_DEV, k, n), BF16),
        compiler_params=_cp(1),
    )(a, b3, *after)


def _input_grad(pairs, name, after=(), carry=()):
    s = pairs[0][0].shape[1]
    d = pairs[0][1].shape[1]
    tn = min(1024, d)
    npair = len(pairs)
    nc = len(carry)
    c_in, c_out, c_shape, c_sems = _carry_specs(carry)

    def body(*refs):
        ops = refs[:2 * npair]
        rest = refs[2 * npair + len(after):]
        o_ref = rest[nc]
        nh, j = pl.program_id(0), pl.program_id(1)
        _carry_run((nh == 0) & (j == 0), (nh == d // tn - 1) & (j == N_DEV - 1),
                   rest[:nc], rest[nc + 1:2 * nc + 1], rest[2 * nc + 1:])

        @pl.when(j == 0)
        def _():
            o_ref[...] = jnp.zeros_like(o_ref)

        for rs in _chunks(s, 1024):
            part = _dot(ops[0][rs, :], ops[1][...], NT)
            for q in range(1, npair):
                part = part + _dot(ops[2 * q][rs, :], ops[2 * q + 1][...], NT)
            o_ref[rs, :] += part

    in_specs, args = [], []
    for a3, w3 in pairs:
        k = a3.shape[2]
        in_specs += [pl.BlockSpec((None, s, k), lambda n, j: (j, 0, 0)),
                     pl.BlockSpec((None, tn, k), lambda n, j: (j, n, 0))]
        args += [a3, w3]
    outs = pl.pallas_call(
        body, name=name, grid=(d // tn, N_DEV),
        in_specs=in_specs + _after_specs(after) + c_in,
        out_specs=[pl.BlockSpec((s, tn), lambda n, j: (0, n))] + c_out,
        out_shape=[jax.ShapeDtypeStruct((s, d), F32)] + c_shape,
        scratch_shapes=c_sems,
        compiler_params=_cp_carry(2, carry),
    )(*args, *after, *carry)
    return (outs[0], list(outs[1:])) if nc else outs[0]


def _rms_bwd(dh, xres, g, dres, name, with_bf16=True):
    s, d = xres.shape
    tm = min(256, s)

    def body(dh_ref, x_ref, g_ref, dres_ref, dx_ref, *rest):
        dg_ref = rest[-1]
        @pl.when(pl.program_id(0) == 0)
        def _():
            dg_ref[...] = jnp.zeros_like(dg_ref)

        xv = x_ref[...]
        dh_v = dh_ref[...]
        r = lax.rsqrt(jnp.mean(xv * xv, axis=-1, keepdims=True) + EPS)
        nrm = xv * r
        dg_ref[...] += jnp.sum(dh_v * nrm, axis=0, keepdims=True)
        dn = dh_v * g_ref[...]
        dx = dres_ref[...].astype(F32) + r * (dn - nrm * jnp.mean(dn * nrm, axis=-1, keepdims=True))
        dx_ref[...] = dx
        if with_bf16:
            rest[0][...] = dx.astype(BF16)

    row = pl.BlockSpec((tm, d), lambda i: (i, 0))
    vec = pl.BlockSpec((1, d), lambda i: (0, 0))
    copies = [jax.ShapeDtypeStruct((s, d), BF16)] if with_bf16 else []
    outs = pl.pallas_call(
        body, name=name, grid=(s // tm,),
        in_specs=[row, row, vec, row],
        out_specs=[row] + [row] * len(copies) + [vec],
        out_shape=[jax.ShapeDtypeStruct((s, d), F32)] + copies + [jax.ShapeDtypeStruct((1, d), F32)],
        compiler_params=_cp(1),
    )(dh, xres, g, dres)
    return (outs[0], outs[1], outs[2]) if with_bf16 else (outs[0], None, outs[1])


def _wgrad_full(a, b, name, after=(), carry=()):
    s, k = a.shape
    n = b.shape[1]
    tk = min(512, k)
    nc = len(carry)
    c_in, c_out, c_shape, c_sems = _carry_specs(carry)

    def body(a_ref, b_ref, *rest):
        rest = rest[len(after):]
        o_ref = rest[nc]
        j = pl.program_id(0)
        _carry_run(j == 0, j == k // tk - 1, rest[:nc], rest[nc + 1:2 * nc + 1], rest[2 * nc + 1:])
        o_ref[...] = _dot(a_ref[...], b_ref[...], TN).astype(BF16)

    outs = pl.pallas_call(
        body, name=name, grid=(k // tk,),
        in_specs=[pl.BlockSpec((s, tk), lambda j: (0, j)),
                  pl.BlockSpec((s, n), lambda j: (0, 0))] + _after_specs(after) + c_in,
        out_specs=[pl.BlockSpec((tk, n), lambda j: (j, 0))] + c_out,
        out_shape=[jax.ShapeDtypeStruct((k, n), BF16)] + c_shape,
        scratch_shapes=c_sems,
        compiler_params=_cp_carry(1, carry),
    )(a, b, *after, *carry)
    return (outs[0], list(outs[1:])) if nc else outs[0]


def _wgrad_pool(p, dyb, n_groups):
    s, sw = p.shape
    d = dyb.shape[1]
    gw, go = sw // n_groups, d // n_groups
    ts = min(512, s)
    ns = s // ts

    def body(a_ref, b_ref, o_ref, acc_ref):
        i = pl.program_id(1)

        @pl.when(i == 0)
        def _():
            acc_ref[...] = jnp.zeros_like(acc_ref)

        acc_ref[...] += _dot(a_ref[...], b_ref[...], TN)

        @pl.when(i == ns - 1)
        def _():
            o_ref[...] = acc_ref[...].astype(BF16)

    return pl.pallas_call(
        body, name="wgrad_pool", grid=(n_groups, ns),
        in_specs=[pl.BlockSpec((ts, gw), lambda g, i: (i, g)),
                  pl.BlockSpec((ts, go), lambda g, i: (i, g))],
        out_specs=pl.BlockSpec((None, gw, go), lambda g, i: (g, 0, 0)),
        out_shape=jax.ShapeDtypeStruct((n_groups, gw, go), BF16),
        scratch_shapes=[pltpu.VMEM((gw, go), F32)],
        compiler_params=_cp(2),
    )(p, dyb)


def _wo_bwd(dx1b, wo, factors, sw, after=()):
    s, d = dx1b.shape
    tn = d // N_DEV
    nq = sw // tn

    def body(dx_ref, wo_ref, fya_ref, fyb_ref, fga_ref, fgb_ref, fsc_ref, *rest):
        dya_ref, dyb_ref, dp_ref, dbg_ref, dsc_ref, dm_ref = rest[len(after):]
        dbg_ref[...] = jnp.zeros_like(dbg_ref)
        dsc_ref[...] = jnp.zeros_like(dsc_ref)
        for rs in _chunks(s, 1024):
            dm_ref[rs, :] = _dot(dx_ref[rs, :], wo_ref[...], NT)
        for rs in _chunks(s, 256):
            dm = dm_ref[rs, :]
            dya_ref[rs, :] = (dm * fya_ref[rs, :].astype(F32)).astype(BF16)
            dyb_ref[rs, :] = (dm * fyb_ref[rs, :].astype(F32)).astype(BF16)
            dsc_ref[...] += jnp.sum(dm * fsc_ref[rs, :].astype(F32), axis=0, keepdims=True)
            dga = dm * fga_ref[rs, :].astype(F32)
            dgb = dm * fgb_ref[rs, :].astype(F32)
            dp_ref[0, rs, :] = dga.astype(BF16)
            dp_ref[1, rs, :] = dgb.astype(BF16)
            dbg_ref[0:1, :] += jnp.sum(dga, axis=0, keepdims=True)
            dbg_ref[1:2, :] += jnp.sum(dgb, axis=0, keepdims=True)

    col = pl.BlockSpec((s, tn), lambda j: (0, j))
    out = jax.ShapeDtypeStruct((s, d), BF16)
    return pl.pallas_call(
        body, name="wo_bwd", grid=(N_DEV,),
        in_specs=[pl.BlockSpec((s, d), lambda j: (0, 0)),
                  pl.BlockSpec((tn, d), lambda j: (j, 0))] + [col] * 5 + _after_specs(after),
        out_specs=[col, col,
                   pl.BlockSpec((2, None, s, tn), lambda j: (1, j // nq, 0, j % nq)),
                   pl.BlockSpec((2, tn), lambda j: (0, j)),
                   pl.BlockSpec((1, tn), lambda j: (0, j))],
        out_shape=[out, out, jax.ShapeDtypeStruct((4, 2, s, sw), BF16),
                   jax.ShapeDtypeStruct((2, d), F32), jax.ShapeDtypeStruct((1, d), F32)],
        scratch_shapes=[pltpu.VMEM((s, tn), F32)],
        compiler_params=_cp(1),
    )(dx1b, wo, *factors, *after)


def _conv_bwd(dproj, dya, wa, proj, conv_w, conv_b):
    s, d = dya.shape
    sw, tn = wa.shape[1], wa.shape[2]
    tc = min(LANES, sw)

    def body(dproj_hbm, dya_ref, wa_ref, ba_ref, ca_ref, va_ref, cw_ref, cb_ref,
             dp_ref, dcw_ref, dcb_ref, dz_ref):
        del dproj_hbm
        for rs in _chunks(s, 512):
            part = _dot(dya_ref[rs, 0:tn], wa_ref[0], NT)
            for j in range(1, N_DEV):
                part = part + _dot(dya_ref[rs, j * tn:(j + 1) * tn], wa_ref[j], NT)
            dz_ref[rs, :] = part
        dz = dz_ref[...]
        ba, ca, va = ba_ref[...], ca_ref[...], va_ref[...]
        cv = ca * va
        cv1, cv2 = _shift_down(cv, 1), _shift_down(cv, 2)
        w0, w1, w2 = cw_ref[0:1, :], cw_ref[1:2, :], cw_ref[2:3, :]
        u = cb_ref[...] + w0 * cv2 + w1 * cv1 + w2 * cv
        du = dz * ba
        dp_ref[0] = (dz * u).astype(BF16)
        dcv = w2 * du + w1 * _shift_up(du, 1) + w0 * _shift_up(du, 2)
        dp_ref[1] = (dcv * va).astype(BF16)
        dp_ref[2] = (dcv * ca).astype(BF16)
        dcw_ref[0:1, :] = jnp.sum(du * cv2, axis=0, keepdims=True)
        dcw_ref[1:2, :] = jnp.sum(du * cv1, axis=0, keepdims=True)
        dcw_ref[2:3, :] = jnp.sum(du * cv, axis=0, keepdims=True)
        dcb_ref[...] = jnp.sum(du, axis=0, keepdims=True)

    def part(k):
        return pl.BlockSpec((None, s, tc), lambda i: (k, 0, i))

    return pl.pallas_call(
        body, name="conv_bwd", grid=(sw // tc,),
        in_specs=[pl.BlockSpec(memory_space=pl.ANY),
                  pl.BlockSpec((s, d), lambda i: (0, 0)),
                  pl.BlockSpec((N_DEV, tc, tn), lambda i: (0, i, 0)),
                  part(0), part(1), part(2),
                  pl.BlockSpec((CONV_K, tc), lambda i: (0, i)), pl.BlockSpec((1, tc), lambda i: (0, i))],
        out_specs=[pl.BlockSpec((3, s, tc), lambda i: (0, 0, i)),
                   pl.BlockSpec((CONV_K, tc), lambda i: (0, i)), pl.BlockSpec((1, tc), lambda i: (0, i))],
        out_shape=[jax.ShapeDtypeStruct(dproj.shape, BF16),
                   jax.ShapeDtypeStruct((CONV_K, sw), F32), jax.ShapeDtypeStruct((1, sw), F32)],
        scratch_shapes=[pltpu.VMEM((s, tc), F32)],
        input_output_aliases={0: 0},
        compiler_params=_cp(1),
    )(dproj, dya, wa, proj, proj, proj, conv_w, conv_b)


def _pool_bwd(dproj, dyb, wpool):
    s, d = dyb.shape
    n_groups, gw, go = wpool.shape

    def body(dproj_hbm, dyb_ref, wp_ref, dp_ref):
        del dproj_hbm
        for gi, window in enumerate(POOL_WINDOWS):
            @pl.when(pl.program_id(0) == gi)
            def _():
                dpool = _dot(dyb_ref[...], wp_ref[...], NT)
                acc, k = dpool / _pool_counts(dpool.shape, window), 1
                while k < window:
                    acc = acc + _shift_up(acc, k)
                    k *= 2
                dp_ref[...] = (acc - dpool).astype(BF16)

    return pl.pallas_call(
        body, name="pool_bwd", grid=(n_groups,),
        in_specs=[pl.BlockSpec(memory_space=pl.ANY),
                  pl.BlockSpec((s, go), lambda g: (0, g)),
                  pl.BlockSpec((None, gw, go), lambda g: (g, 0, 0))],
        out_specs=pl.BlockSpec((None, s, gw), lambda g: (3, 0, g)),
        out_shape=jax.ShapeDtypeStruct(dproj.shape, BF16),
        input_output_aliases={0: 0},
        compiler_params=_cp(1),
    )(dproj, dyb, wpool)


def _rows128(v):
    return v.reshape(-1, LANES)


def kernel(x, norm1_g, w_in, b_gate, conv_w, conv_b, w_a_out, w_pool, pool_scale, w_o, norm2_g, w_ffn_gate, w_ffn_up, w_ffn_down, final_g, loss_target, m_norm1_g, m_w_in, m_b_gate, m_conv_w, m_conv_b, m_w_a_out, m_w_pool, m_pool_scale, m_w_o, m_norm2_g, m_w_ffn_gate, m_w_ffn_up, m_w_ffn_down, m_final_g, v_norm1_g, v_w_in, v_b_gate, v_conv_w, v_conv_b, v_w_a_out, v_w_pool, v_pool_scale, v_w_o, v_norm2_g, v_w_ffn_gate, v_w_ffn_up, v_w_ffn_down, v_final_g):
    s, d = x.shape[1], x.shape[2]
    sw = w_in.shape[2]
    n_groups = w_pool.shape[1]
    gw = w_pool.shape[2]
    go = w_pool.shape[3] * N_DEV
    f8 = w_ffn_gate.shape[2]
    cws = conv_w.shape[2]
    assert sw == conv_w.shape[2] * N_DEV == gw * n_groups and go * n_groups == d and n_groups == len(POOL_WINDOWS)

    xi, yi, ci = _coords()
    me = 4 * xi + 2 * yi + ci
    my_chip = 2 * xi + yi

    x2d = x.reshape(s, d)
    target = loss_target.reshape(s, d)
    final_g2 = final_g.reshape(1, d)
    b_gate2 = b_gate.reshape(2, d)

    big_names = ["w_in", "w_a_out", "w_pool", "w_o", "w_ffn_gate", "w_ffn_up", "w_ffn_down"]
    big_w = [w_in, w_a_out, w_pool, w_o, w_ffn_gate, w_ffn_up, w_ffn_down]
    big_m = [m_w_in, m_w_a_out, m_w_pool, m_w_o, m_w_ffn_gate, m_w_ffn_up, m_w_ffn_down]
    big_v = [v_w_in, v_w_a_out, v_w_pool, v_w_o, v_w_ffn_gate, v_w_ffn_up, v_w_ffn_down]
    shapes2d = [(w.size // w.shape[-1], w.shape[-1]) for w in big_w]
    big_w2 = [w.reshape(sh) for w, sh in zip(big_w, shapes2d)]
    transposed = (4, 5)

    def view2d(t, a):
        t2 = t.reshape(shapes2d[a])
        return t2.T if a in transposed else t2

    def unview(o, a):
        return (o.T if a in transposed else o).reshape(big_w[a].shape)

    sb = [_cast_bf16(w, "cast_" + nm) for w, nm in zip(big_w2, big_names)]
    win_g, wa_g, wpool_g, wo_g = _allgather_big(sb[0:4], "allgather_mixer", COLLECTIVE_GATHER)
    wg_g, wu_g = _allgather_big(sb[4:6], "allgather_ffn_up", COLLECTIVE_GATHER)
    (wd_g,) = _allgather_big(sb[6:7], "allgather_ffn_down", COLLECTIVE_GATHER)
    convw_g = _allgather_small(jnp.pad(conv_w.reshape(CONV_K, cws), ((0, 8 - CONV_K), (0, 0))), "allgather_conv_w")
    conv_w_full = convw_g[:, :CONV_K, :].transpose(1, 0, 2).reshape(CONV_K, sw)
    wpool = wpool_g.reshape(N_DEV, n_groups, gw, go // N_DEV).transpose(1, 2, 0, 3).reshape(n_groups, gw, go)
    wo = wo_g.reshape(d, d)

    h = _rms_fwd(x2d, norm1_g)
    proj = _proj_fwd(h, win_g)
    z = _conv_fwd(proj, conv_w_full, conv_b)
    p = _pool_fwd(proj)
    merged, *merge_factors = _merge_fwd(z, wa_g, p, wpool, proj, b_gate2, pool_scale)
    x1, h2 = _wo_fwd(merged, wo, x2d, norm2_g)
    dadu, dadg, act = _ffn_up_act_fwd(h2, wg_g, wu_g)
    ffn_out = _ffn_down_fwd(act, wd_g)
    dx2b, d_final_g, loss_blk = _loss_bwd(ffn_out, x1, target, final_g2)

    other_chips = jnp.stack([2 * (1 - xi) + yi, 2 * xi + (1 - yi), 2 * (1 - xi) + (1 - yi)])
    others = jnp.concatenate([other_chips, 2 * other_chips + ci]).astype(jnp.int32)

    def partials(grads, recvs, names):
        if all(g.shape == grads[0].shape for g in grads):
            return list(_chip_partial(others, grads, recvs, "chip_partial_" + names[0]))
        return [_chip_partial(others, [g3], [r], "chip_partial_" + nm)[0] for g3, r, nm in zip(grads, recvs, names)]

    own = jnp.stack([me, my_chip]).astype(jnp.int32)

    def adam(idx, g3s, sibs, chipss):
        wmvs = [(view2d(big_w[a], a), view2d(big_m[a], a), view2d(big_v[a], a)) for a in idx]
        outs = _adam_big(own, wmvs, g3s, sibs, chipss, "adam_" + big_names[idx[0]])
        for a, o4 in zip(idx, outs):
            big_out[a] = [unview(o, a) for o in o4]

    big_out = [None] * len(big_names)
    dg_act, du_act = _ffn_gate_bwd(dx2b, wd_g, dadg, dadu)
    gw_gate = _wgrad_rows(dg_act, h2, "wgrad_ffn_gate")
    gw_up = _wgrad_rows(du_act, h2, "wgrad_ffn_up")
    gw_down, sib_gu = _wgrad_rows(act, dx2b, "wgrad_ffn_down", carry=[gw_gate, gw_up])
    ps_gu = partials([gw_gate, gw_up], sib_gu, ["w_ffn_gate", "w_ffn_up"])
    chips_gu = _exchange_chips(ps_gu, "rs_chips_ffn_up", COLLECTIVE_CHIPS)
    dh2, sib_down = _input_grad([(dg_act, wg_g), (du_act, wu_g)], "ffn_in_bwd", after=ps_gu, carry=[gw_down])
    ps_down = partials([gw_down], sib_down, ["w_ffn_down"])
    chips_down = _exchange_chips(ps_down, "rs_chips_ffn_down", COLLECTIVE_CHIPS)
    dx1, dx1b, d_norm2_g = _rms_bwd(dh2, x1, norm2_g, dx2b, "rms2_bwd")
    dya, dyb, dproj42, d_b_gate, d_pool_scale = _wo_bwd(dx1b, wo, merge_factors, sw, after=ps_down)
    dproj = dproj42.reshape(N_DEV, s, sw)
    dproj, d_conv_w, d_conv_b = _conv_bwd(dproj, dya, wa_g, proj, conv_w_full, conv_b)
    dproj = _pool_bwd(dproj, dyb, wpool)
    gw_in = _wgrad_cols(h, dproj, "wgrad_in")
    gw_o, sib_in = _wgrad_full(merged, dx1b, "wgrad_o", carry=[gw_in])
    ps_in = partials([gw_in], sib_in, ["w_in"])
    chips_in = _exchange_chips(ps_in, "rs_chips_w_in", COLLECTIVE_CHIPS)
    gw_a = _wgrad_cols(z, dya, "wgrad_a_out", after=ps_in)
    gw_pool = _wgrad_pool(p, dyb, n_groups)
    mix3 = [gw_a,
            gw_pool.reshape(n_groups, gw, N_DEV, go // N_DEV).transpose(2, 0, 1, 3).reshape(N_DEV, n_groups * gw, go // N_DEV),
            gw_o.reshape(N_DEV, d // N_DEV, d)]
    adam([4, 5, 6], [gw_gate, gw_up, gw_down], sib_gu + sib_down, chips_gu + chips_down)
    dh, sib_mix = _input_grad([(dproj, win_g)], "proj_in_bwd",
                              after=[big_out[4][0], big_out[5][0], big_out[6][0]], carry=mix3)
    ps_mix = partials(mix3, sib_mix, ["w_a_out", "w_pool", "w_o"])
    chips_mix = _exchange_chips(ps_mix, "rs_chips_mixer", COLLECTIVE_CHIPS)
    grad_x, _, d_norm1_g = _rms_bwd(dh, x2d, norm1_g, dx1, "rms1_bwd", with_bf16=False)
    adam([0], [gw_in], sib_in, chips_in)
    for k in range(3):
        adam([1 + k], [mix3[k]], [sib_mix[k]], [chips_mix[k]])

    small_parts = [d_norm1_g, d_b_gate, d_conv_w, d_conv_b, d_pool_scale, d_norm2_g, d_final_g, loss_blk]
    rows = [v.size // LANES for v in small_parts]
    row0 = [sum(rows[:k]) for k in range(len(rows))]
    packed = jnp.concatenate([_rows128(v) for v in small_parts], axis=0)
    gathered = _allgather_small(packed, "allgather_small_grads")
    small_names = ["norm1_g", "b_gate", "conv_b", "pool_scale", "norm2_g", "final_g", "conv_w"]
    small_w = [norm1_g, b_gate, conv_b, pool_scale, norm2_g, final_g]
    small_m = [m_norm1_g, m_b_gate, m_conv_b, m_pool_scale, m_norm2_g, m_final_g]
    small_v = [v_norm1_g, v_b_gate, v_conv_b, v_pool_scale, v_norm2_g, v_final_g]
    finished = _small_finish(gathered, [tuple(_rows128(t) for t in wmv) for wmv in zip(small_w, small_m, small_v)],
                             [row0[k] for k in (0, 1, 3, 4, 5, 6)], [(row0[2], rows[2]), (row0[7], rows[7])])
    g_convw_full, loss_rows = finished[0], finished[1]
    loss = loss_rows[0, 0]
    small_out = [[t.reshape(w.shape) for t in finished[2 + 4 * k:6 + 4 * k]] for k, w in enumerate(small_w)]
    g_convw = lax.dynamic_slice(g_convw_full.reshape(CONV_K, sw), (0, me * cws), (CONV_K, cws))
    cw_delta, cw_m, cw_v = _adam_small(conv_w.reshape(CONV_K, cws), g_convw,
                                       m_conv_w.reshape(CONV_K, cws), v_conv_w.reshape(CONV_K, cws))
    small_out.append([t.reshape(conv_w.shape) for t in (g_convw, cw_delta, cw_m, cw_v)])

    order = ["norm1_g", "w_in", "b_gate", "conv_w", "conv_b", "w_a_out", "w_pool", "pool_scale", "w_o", "norm2_g",
             "w_ffn_gate", "w_ffn_up", "w_ffn_down", "final_g"]
    per_kind = [{}, {}, {}, {}]
    for a, nm in enumerate(big_names):
        for kind in range(4):
            per_kind[kind][nm] = big_out[a][kind]
    for k, nm in enumerate(small_names):
        for kind in range(4):
            per_kind[kind][nm] = small_out[k][kind]
    result = [loss, grad_x.reshape(x.shape)]
    for kind in range(4):
        result += [per_kind[kind][nm] for nm in order]
    return tuple(result)
```

```python
import jax
import jax.numpy as jnp
from jax import lax
from jax.experimental import pallas as pl
from jax.experimental.pallas import tpu as pltpu
from jax.experimental.pallas import tpu_sc as plsc

F32 = jnp.float32
BF16 = jnp.bfloat16
MESH = pl.DeviceIdType.MESH

N_DEV = 8
EPS = 1e-6
CONV_K = 3
POOL_WINDOWS = (2, 4, 8, 16)
ADAM_LR = 0.001
ADAM_B1 = 0.9
ADAM_B2 = 0.999
ADAM_EPS = 1e-08
ADAM_WD = 0.01
ADAM_STEP = 10

V7X_VMEM_LIMIT_BYTES = 56 * 1024 * 1024
LANES = 128

COLLECTIVE_GATHER = 1
COLLECTIVE_SIBLING = 2
COLLECTIVE_CHIPS = 3
SEQUENCER_COST_BYTES = 4 * 10**9

NN = ((1,), (0,))
NT = ((1,), (1,))
TN = ((0,), (0,))


def _dot(a, b, dims):
    return lax.dot_general(a, b, (dims, ((), ())), preferred_element_type=F32)


def _cp(n_axes):
    return pltpu.CompilerParams(dimension_semantics=("arbitrary",) * n_axes,
                                vmem_limit_bytes=V7X_VMEM_LIMIT_BYTES)


def _row_tile(rows, bytes_per_row, cap_bytes):
    best = None
    for t in range(16, rows + 1, 16):
        if rows % t == 0 and t * bytes_per_row <= cap_bytes:
            best = t
    return best if best is not None else rows


def _chunks(total, size):
    size = min(size, total)
    assert total % size == 0
    return [slice(r, r + size) for r in range(0, total, size)]


def _after_specs(after):
    return [pl.BlockSpec(memory_space=pl.ANY)] * len(after)


def _shift_down(v, k):
    row = lax.broadcasted_iota(jnp.int32, v.shape, 0)
    return jnp.where(row >= k, pltpu.roll(v, k, 0), 0.0)


def _shift_up(v, k):
    n = v.shape[0]
    row = lax.broadcasted_iota(jnp.int32, v.shape, 0)
    return jnp.where(row < n - k, pltpu.roll(v, n - k, 0), 0.0)


def _sigmoid(v):
    return jax.nn.sigmoid(v)


def _cast_bf16(w2d, name):
    rows, cols = w2d.shape
    tr = _row_tile(rows, cols * 4, 2 << 20)

    def body(i_ref, o_ref):
        o_ref[...] = i_ref[...].astype(BF16)

    return pl.pallas_call(
        body, name=name, grid=(rows // tr,),
        in_specs=[pl.BlockSpec((tr, cols), lambda i: (i, 0))],
        out_specs=pl.BlockSpec((tr, cols), lambda i: (i, 0)),
        out_shape=jax.ShapeDtypeStruct((rows, cols), BF16),
        compiler_params=_cp(1),
    )(w2d)


def _rms_fwd(x2d, g):
    s, d = x2d.shape
    tm = min(256, s)

    def body(x_ref, g_ref, h_ref):
        xv = x_ref[...]
        r = lax.rsqrt(jnp.mean(xv * xv, axis=-1, keepdims=True) + EPS)
        h_ref[...] = (xv * r * g_ref[...]).astype(BF16)

    return pl.pallas_call(
        body, name="rms1_fwd", grid=(s // tm,),
        in_specs=[pl.BlockSpec((tm, d), lambda i: (i, 0)), pl.BlockSpec((1, d), lambda i: (0, 0))],
        out_specs=pl.BlockSpec((tm, d), lambda i: (i, 0)),
        out_shape=jax.ShapeDtypeStruct((s, d), BF16),
        compiler_params=_cp(1),
    )(x2d, g)


def _coords():
    return lax.axis_index("x"), lax.axis_index("y"), lax.axis_index("c")


def _slot(p):
    return 4 * p[0] + 2 * p[1] + p[2]


def _handshake(peers):
    barrier = pltpu.get_barrier_semaphore()
    for peer in peers:
        pl.semaphore_signal(barrier, inc=1, device_id=peer, device_id_type=MESH)
    pl.semaphore_wait(barrier, len(peers))


def _sequencer_call(body, out_type, scratch_types, name, collective_id):
    return pl.kernel(
        body, out_type=out_type, name=name,
        mesh=plsc.ScalarSubcoreMesh(axis_name="seq", num_cores=1),
        scratch_types=scratch_types,
        cost_estimate=pl.CostEstimate(flops=0, transcendentals=0, bytes_accessed=SEQUENCER_COST_BYTES),
        compiler_params=pltpu.CompilerParams(collective_id=collective_id))


def _allgather_big(shards, name, collective_id, after=()):
    n = len(shards)

    def body(*refs):
        ins, outs = refs[:n], refs[n + len(after):2 * n + len(after)]
        send_sems, recv_sems, local_sems = refs[2 * n + len(after):]
        x, y, c = _coords()
        me, sibling = (x, y, c), (x, y, 1 - c)
        x_nbr, y_nbr, diag = (1 - x, y), (x, 1 - y), (1 - x, 1 - y)
        relay_from = (x + (1 - c) * (1 - 2 * x), y + c * (1 - 2 * y))
        relay_to = (x + c * (1 - 2 * x), y + (1 - c) * (1 - 2 * y))
        _handshake([sibling, (*x_nbr, c), (*y_nbr, c)])

        def copy(a, k, block, to, src=None):
            dst = outs[a].at[_slot(block)]
            return pltpu.make_async_remote_copy(
                src_ref=dst if src is None else src, dst_ref=dst,
                send_sem=send_sems.at[a, k], recv_sem=recv_sems.at[a, k],
                device_id=to, device_id_type=MESH)

        mine, sends = [], []
        for a in range(n):
            cp = pltpu.make_async_copy(ins[a], outs[a].at[_slot(me)], local_sems.at[a])
            cp.start()
            mine.append(cp)
            first = [copy(a, 0, me, sibling, src=ins[a]),
                     copy(a, 1, me, (*x_nbr, c), src=ins[a]),
                     copy(a, 2, me, (*y_nbr, c), src=ins[a])]
            for cp in first:
                cp.start()
            sends += first
        for a in range(n):
            copy(a, 1 + c, (*relay_from, c), me).wait_recv()
            passed = [copy(a, 3, (*relay_from, c), (*relay_to, c)), copy(a, 4 + c, (*relay_from, c), sibling)]
            for cp in passed:
                cp.start()
            copy(a, 2 - c, (*relay_to, c), me).wait_recv()
            cp = copy(a, 5 - c, (*relay_to, c), sibling)
            cp.start()
            passed.append(cp)
            copy(a, 3, (*diag, c), me).wait_recv()
            cp = copy(a, 6, (*diag, c), sibling)
            cp.start()
            sends += passed + [cp]
        for a in range(n):
            copy(a, 0, sibling, me).wait_recv()
            copy(a, 4, (*x_nbr, 1 - c), me).wait_recv()
            copy(a, 5, (*y_nbr, 1 - c), me).wait_recv()
            copy(a, 6, (*diag, 1 - c), me).wait_recv()
        for cp in sends:
            cp.wait_send()
        for cp in mine:
            cp.wait()

    return _sequencer_call(
        body, [jax.ShapeDtypeStruct((N_DEV,) + s.shape, s.dtype) for s in shards],
        [pltpu.SemaphoreType.DMA((n, 7)), pltpu.SemaphoreType.DMA((n, 7)), pltpu.SemaphoreType.DMA((n,))],
        name, collective_id)(*shards, *after)


def _sibling_copies(ins, recvs, send_sems, recv_sems):
    x, y, c = _coords()
    return [pltpu.make_async_remote_copy(
        src_ref=ins[a].at[2 * q + (1 - c)], dst_ref=recvs[a].at[q],
        send_sem=send_sems.at[a, q], recv_sem=recv_sems.at[a, q],
        device_id=(x, y, 1 - c), device_id_type=MESH) for a in range(len(ins)) for q in range(4)]


def _carry_specs(carry):
    any_spec = pl.BlockSpec(memory_space=pl.ANY)
    n = len(carry)
    sems = [pltpu.SemaphoreType.DMA((n, 4)), pltpu.SemaphoreType.DMA((n, 4))] if n else []
    return ([any_spec] * n, [any_spec] * n,
            [jax.ShapeDtypeStruct((4,) + g.shape[1:], g.dtype) for g in carry], sems)


def _carry_run(first, last, ins, recvs, sems):
    if not ins:
        return

    @pl.when(first)
    def _():
        x, y, c = _coords()
        _handshake([(x, y, 1 - c)])
        for cp in _sibling_copies(ins, recvs, *sems):
            cp.start()

    @pl.when(last)
    def _():
        copies = _sibling_copies(ins, recvs, *sems)
        for cp in copies:
            cp.wait_recv()
        for cp in copies:
            cp.wait_send()


def _cp_carry(n_axes, carry):
    if not carry:
        return _cp(n_axes)
    return pltpu.CompilerParams(dimension_semantics=("arbitrary",) * n_axes, vmem_limit_bytes=V7X_VMEM_LIMIT_BYTES,
                                collective_id=COLLECTIVE_SIBLING)


def _exchange_chips(psums, name, collective_id):
    n = len(psums)

    def body(*refs):
        ins, outs = refs[:n], refs[n:2 * n]
        send_sems, recv_sems = refs[2 * n:]
        x, y, c = _coords()
        chips = [(1 - x, y), (x, 1 - y), (1 - x, 1 - y)]
        _handshake([(*chip, c) for chip in chips])
        copies = []
        for a in range(n):
            for j, chip in enumerate(chips):
                cp = pltpu.make_async_remote_copy(
                    src_ref=ins[a].at[2 * chip[0] + chip[1]], dst_ref=outs[a].at[j],
                    send_sem=send_sems.at[a, j], recv_sem=recv_sems.at[a, j],
                    device_id=(*chip, c), device_id_type=MESH)
                cp.start()
                copies.append(cp)
        for cp in copies:
            cp.wait_recv()
        for cp in copies:
            cp.wait_send()

    return _sequencer_call(
        body, [jax.ShapeDtypeStruct((3,) + p.shape[1:], p.dtype) for p in psums],
        [pltpu.SemaphoreType.DMA((n, 3)), pltpu.SemaphoreType.DMA((n, 3))],
        name, collective_id)(*psums)


def _allgather_small(v2d, name):
    rows, cols = v2d.shape

    def body(v_ref, out_ref, send_sems, recv_sems):
        x, y, c = _coords()
        me = (x, y, c)
        out_ref[_slot(me)] = v_ref[...]
        peers = []
        for k in range(1, N_DEV):
            fx, fy, fc = (k >> 2) & 1, (k >> 1) & 1, k & 1
            peers.append(((1 - x) if fx else x, (1 - y) if fy else y, (1 - c) if fc else c))
        sends = []
        for k, peer in enumerate(peers):
            cp = pltpu.make_async_remote_copy(
                src_ref=v_ref, dst_ref=out_ref.at[_slot(me)],
                send_sem=send_sems.at[k], recv_sem=recv_sems.at[k],
                device_id=peer, device_id_type=MESH)
            cp.start()
            sends.append(cp)
        for k, peer in enumerate(peers):
            pltpu.make_async_remote_copy(
                src_ref=v_ref, dst_ref=out_ref.at[_slot(peer)],
                send_sem=send_sems.at[k], recv_sem=recv_sems.at[k],
                device_id=peer, device_id_type=MESH).wait_recv()
        for cp in sends:
            cp.wait_send()

    vmem = pl.BlockSpec(memory_space=pltpu.VMEM)
    return pl.pallas_call(
        body, name=name, in_specs=[vmem], out_specs=vmem,
        out_shape=jax.ShapeDtypeStruct((N_DEV, rows, cols), v2d.dtype),
        scratch_shapes=[pltpu.SemaphoreType.DMA((N_DEV - 1,)), pltpu.SemaphoreType.DMA((N_DEV - 1,))],
    )(v2d)


def _chip_partial(others, grads, recvs, name):
    n = len(grads)
    _, rows, cols = grads[0].shape
    tr = _row_tile(rows, cols * 2, (2 << 20) // n)

    def body(others_ref, *refs):
        for a in range(n):
            refs[2 * n + a][...] = (refs[a][...].astype(F32) + refs[n + a][...].astype(F32)).astype(BF16)

    return pl.pallas_call(
        body, name=name,
        grid_spec=pltpu.PrefetchScalarGridSpec(
            num_scalar_prefetch=1, grid=(3, rows // tr),
            in_specs=[pl.BlockSpec((None, tr, cols), lambda k, i, o: (o[3 + k], i, 0))] * n
            + [pl.BlockSpec((None, tr, cols), lambda k, i, o: (o[k], i, 0))] * n,
            out_specs=[pl.BlockSpec((None, tr, cols), lambda k, i, o: (o[k], i, 0))] * n),
        out_shape=[jax.ShapeDtypeStruct((4, rows, cols), BF16)] * n,
        compiler_params=_cp(2),
    )(others, *grads, *recvs)


def _adam_math(w, g, m, v):
    m = ADAM_B1 * m + (1.0 - ADAM_B1) * g
    v = ADAM_B2 * v + (1.0 - ADAM_B2) * (g * g)
    m_hat = m / (1.0 - ADAM_B1 ** ADAM_STEP)
    v_hat = v / (1.0 - ADAM_B2 ** ADAM_STEP)
    delta = -ADAM_LR * (m_hat / (jnp.sqrt(v_hat) + ADAM_EPS) + ADAM_WD * w)
    return delta, m, v


def _adam_big(own, wmvs, g3s, recv_sibs, recv_chipss, name):
    n = len(wmvs)
    rows, cols = wmvs[0][0].shape
    tr = _row_tile(rows, cols * 4, (2 << 20) // n)

    def body(own_ref, *refs):
        ins, outs = refs[:6 * n], refs[6 * n:]
        for a in range(n):
            w_ref, m_ref, v_ref, g_ref, rs_ref, rc_ref = ins[6 * a:6 * a + 6]
            g = g_ref[...].astype(F32) + rs_ref[...].astype(F32)
            g = g + rc_ref[0].astype(F32)
            g = g + rc_ref[1].astype(F32)
            g = g + rc_ref[2].astype(F32)
            delta, m_new, v_new = _adam_math(w_ref[...], g, m_ref[...], v_ref[...])
            outs[4 * a][...] = g
            outs[4 * a + 1][...] = delta
            outs[4 * a + 2][...] = m_new
            outs[4 * a + 3][...] = v_new

    blk = pl.BlockSpec((tr, cols), lambda i, o: (i, 0))
    per_shard = [blk, blk, blk,
                 pl.BlockSpec((None, tr, cols), lambda i, o: (o[0], i, 0)),
                 pl.BlockSpec((None, tr, cols), lambda i, o: (o[1], i, 0)),
                 pl.BlockSpec((3, tr, cols), lambda i, o: (0, i, 0))]
    out = jax.ShapeDtypeStruct((rows, cols), F32)
    args = [t for a in range(n) for t in (*wmvs[a], g3s[a], recv_sibs[a], recv_chipss[a])]
    outs = pl.pallas_call(
        body, name=name,
        grid_spec=pltpu.PrefetchScalarGridSpec(
            num_scalar_prefetch=1, grid=(rows // tr,),
            in_specs=per_shard * n, out_specs=[blk] * (4 * n)),
        out_shape=[out] * (4 * n),
        compiler_params=_cp(1),
    )(own, *args)
    return [outs[4 * a:4 * a + 4] for a in range(n)]


def _small_finish(gathered, params, row_offs, extra_rows):
    n = len(params)

    def body(g_ref, *refs):
        ins, outs = refs[:3 * n], refs[3 * n:]
        total = g_ref[0]
        for k in range(1, N_DEV):
            total = total + g_ref[k]
        for e, (r0, nr) in enumerate(extra_rows):
            outs[e][...] = total[r0:r0 + nr, :]
        for p in range(n):
            w_ref, m_ref, v_ref = ins[3 * p:3 * p + 3]
            g_out, d_out, m_out, v_out = outs[len(extra_rows) + 4 * p:len(extra_rows) + 4 * p + 4]
            g = total[row_offs[p]:row_offs[p] + w_ref.shape[0], :]
            delta, m_new, v_new = _adam_math(w_ref[...], g, m_ref[...], v_ref[...])
            g_out[...] = g
            d_out[...] = delta
            m_out[...] = m_new
            v_out[...] = v_new

    vmem = pl.BlockSpec(memory_space=pltpu.VMEM)
    out_shape = [jax.ShapeDtypeStruct((nr, LANES), F32) for _, nr in extra_rows]
    for w, _, _ in params:
        out_shape += [jax.ShapeDtypeStruct(w.shape, F32)] * 4
    flat = [t for wmv in params for t in wmv]
    return pl.pallas_call(body, name="small_finish", in_specs=[vmem] * (1 + len(flat)),
                          out_specs=[vmem] * len(out_shape), out_shape=out_shape)(gathered, *flat)


def _adam_small(w, g, m, v):
    def body(w_ref, g_ref, m_ref, v_ref, do_ref, mo_ref, vo_ref):
        delta, m_new, v_new = _adam_math(w_ref[...], g_ref[...], m_ref[...], v_ref[...])
        do_ref[...] = delta
        mo_ref[...] = m_new
        vo_ref[...] = v_new

    vmem = pl.BlockSpec(memory_space=pltpu.VMEM)
    out = jax.ShapeDtypeStruct(w.shape, F32)
    return pl.pallas_call(body, name="adam_small", in_specs=[vmem] * 4, out_specs=[vmem] * 3,
                          out_shape=[out, out, out])(w, g, m, v)


def _proj_fwd(h, win_g):
    s, d = h.shape
    sw = win_g.shape[2]
    tn = min(512, sw)
    nh = sw // tn

    def body(h_ref, w_ref, o_ref):
        for rs in _chunks(s, 512):
            o_ref[rs, :] = _dot(h_ref[rs, :], w_ref[...], NN)

    return pl.pallas_call(
        body, name="proj_fwd", grid=(N_DEV * nh,),
        in_specs=[pl.BlockSpec((s, d), lambda j: (0, 0)),
                  pl.BlockSpec((None, d, tn), lambda j: (j // nh, 0, j % nh))],
        out_specs=pl.BlockSpec((None, s, tn), lambda j: (j // nh, 0, j % nh)),
        out_shape=jax.ShapeDtypeStruct((N_DEV, s, sw), F32),
        compiler_params=_cp(1),
    )(h, win_g)


def _conv_fwd(proj, conv_w, conv_b):
    _, s, sw = proj.shape
    tc = min(LANES, sw)

    def body(ba_ref, ca_ref, va_ref, cw_ref, cb_ref, z_ref):
        cv = ca_ref[...] * va_ref[...]
        u = (cb_ref[...] + cw_ref[0:1, :] * _shift_down(cv, 2) + cw_ref[1:2, :] * _shift_down(cv, 1)
             + cw_ref[2:3, :] * cv)
        z_ref[...] = (ba_ref[...] * u).astype(BF16)

    def part(k):
        return pl.BlockSpec((None, s, tc), lambda i: (k, 0, i))

    return pl.pallas_call(
        body, name="conv_fwd", grid=(sw // tc,),
        in_specs=[part(0), part(1), part(2),
                  pl.BlockSpec((CONV_K, tc), lambda i: (0, i)), pl.BlockSpec((1, tc), lambda i: (0, i))],
        out_specs=pl.BlockSpec((s, tc), lambda i: (0, i)),
        out_shape=jax.ShapeDtypeStruct((s, sw), BF16),
        compiler_params=_cp(1),
    )(proj, proj, proj, conv_w, conv_b)


def _pool_counts(shape, window):
    t = lax.broadcasted_iota(jnp.int32, shape, 0)
    return jnp.minimum(t + 1, window).astype(F32)


def _pool_fwd(proj):
    _, s, sw = proj.shape
    gw = sw // len(POOL_WINDOWS)

    def body(v_ref, p_ref):
        for gi, window in enumerate(POOL_WINDOWS):
            @pl.when(pl.program_id(0) == gi)
            def _():
                v = v_ref[...]
                acc, k = v, 1
                while k < window:
                    acc = acc + _shift_down(acc, k)
                    k *= 2
                p_ref[...] = (acc / _pool_counts(v.shape, window) - v).astype(BF16)

    return pl.pallas_call(
        body, name="pool_fwd", grid=(len(POOL_WINDOWS),),
        in_specs=[pl.BlockSpec((None, s, gw), lambda g: (3, 0, g))],
        out_specs=pl.BlockSpec((s, gw), lambda g: (0, g)),
        out_shape=jax.ShapeDtypeStruct((s, sw), BF16),
        compiler_params=_cp(1),
    )(proj)


def _merge_fwd(z, wa, p, wpool, proj, b_gate2, pool_scale):
    s, sw = z.shape
    tn = wa.shape[2]
    d = tn * N_DEV
    gw = sw // len(POOL_WINDOWS)
    nq = sw // tn

    def body(z_ref, wa_ref, p_ref, wp_ref, ga_ref, gb_ref, bg_ref, sc_ref,
             m_ref, dya_ref, dyb_ref, dga_ref, dgb_ref, dsc_ref):
        for rs in _chunks(s, 512):
            ya = _dot(z_ref[rs, :], wa_ref[...], NN)
            yb = _dot(p_ref[rs, :], wp_ref[...], NN)
            sa = _sigmoid(ga_ref[rs, :] + bg_ref[0:1, :])
            sb = _sigmoid(gb_ref[rs, :] + bg_ref[1:2, :])
            sc = sc_ref[...]
            sb_yb = sb * yb
            m_ref[rs, :] = (sa * ya + sb_yb * sc).astype(BF16)
            dya_ref[rs, :] = sa.astype(BF16)
            dyb_ref[rs, :] = (sb * sc).astype(BF16)
            dga_ref[rs, :] = (ya * (sa * (1.0 - sa))).astype(BF16)
            dgb_ref[rs, :] = ((yb * sc) * (sb * (1.0 - sb))).astype(BF16)
            dsc_ref[rs, :] = sb_yb.astype(BF16)

    col = pl.BlockSpec((s, tn), lambda j: (0, j))
    out = jax.ShapeDtypeStruct((s, d), BF16)
    return pl.pallas_call(
        body, name="merge_fwd", grid=(N_DEV,),
        in_specs=[pl.BlockSpec((s, sw), lambda j: (0, 0)),
                  pl.BlockSpec((None, sw, tn), lambda j: (j, 0, 0)),
                  pl.BlockSpec((s, gw), lambda j: (0, j // 2)),
                  pl.BlockSpec((None, gw, tn), lambda j: (j // 2, 0, j % 2)),
                  pl.BlockSpec((None, s, tn), lambda j: (4 + j // nq, 0, j % nq)),
                  pl.BlockSpec((None, s, tn), lambda j: (6 + j // nq, 0, j % nq)),
                  pl.BlockSpec((2, tn), lambda j: (0, j)),
                  pl.BlockSpec((1, tn), lambda j: (0, j))],
        out_specs=[col] * 6,
        out_shape=[out] * 6,
        compiler_params=_cp(1),
    )(z, wa, p, wpool, proj, proj, b_gate2, pool_scale)


def _wo_fwd(merged, wo, x2d, g2):
    s, d = x2d.shape
    tm = min(256, s)

    def body(m_ref, wo_ref, x_ref, g_ref, x1_ref, h2_ref):
        x1 = x_ref[...] + _dot(m_ref[...], wo_ref[...], NN)
        x1_ref[...] = x1
        r = lax.rsqrt(jnp.mean(x1 * x1, axis=-1, keepdims=True) + EPS)
        h2_ref[...] = (x1 * r * g_ref[...]).astype(BF16)

    row = pl.BlockSpec((tm, d), lambda i: (i, 0))
    return pl.pallas_call(
        body, name="wo_fwd", grid=(s // tm,),
        in_specs=[row, pl.BlockSpec((d, d), lambda i: (0, 0)), row, pl.BlockSpec((1, d), lambda i: (0, 0))],
        out_specs=[row, row],
        out_shape=[jax.ShapeDtypeStruct((s, d), F32), jax.ShapeDtypeStruct((s, d), BF16)],
        compiler_params=_cp(1),
    )(merged, wo, x2d, g2)


def _ffn_up_act_fwd(h2, wg_g, wu_g):
    s, d = h2.shape
    f8 = wg_g.shape[2]
    th = min(1024, s)

    def body(h_ref, wg_ref, wu_ref, dadu_ref, dadg_ref, a_ref):
        i = pl.program_id(1)
        for rs in _chunks(th, 512):
            rows = pl.ds(pl.multiple_of(i * th + rs.start, rs.stop - rs.start), rs.stop - rs.start)
            a = h_ref[rows, :]
            g = _dot(a, wg_ref[...], NN)
            u = _dot(a, wu_ref[...], NN)
            sg = _sigmoid(g)
            silu = g * sg
            dadu_ref[rs, :] = silu.astype(BF16)
            dadg_ref[rs, :] = (u * (sg * (1.0 + g * (1.0 - sg)))).astype(BF16)
            a_ref[rs, :] = (silu * u).astype(BF16)

    wspec = pl.BlockSpec((None, d, f8), lambda j, i: (j, 0, 0))
    ospec = pl.BlockSpec((None, th, f8), lambda j, i: (j, i, 0))
    out = jax.ShapeDtypeStruct((N_DEV, s, f8), BF16)
    return pl.pallas_call(
        body, name="ffn_up_fwd", grid=(N_DEV, s // th),
        in_specs=[pl.BlockSpec((s, d), lambda j, i: (0, 0)), wspec, wspec],
        out_specs=[ospec, ospec, ospec], out_shape=[out, out, out],
        compiler_params=_cp(2),
    )(h2, wg_g, wu_g)


def _ffn_down_fwd(act, wd_g):
    _, s, f8 = act.shape
    d = wd_g.shape[2]
    tn = min(1024, d)

    def body(a_ref, wd_ref, o_ref):
        j = pl.program_id(1)

        @pl.when(j == 0)
        def _():
            o_ref[...] = jnp.zeros_like(o_ref)

        for rs in _chunks(s, 1024):
            o_ref[rs, :] += _dot(a_ref[rs, :], wd_ref[...], NN)

    return pl.pallas_call(
        body, name="ffn_down_fwd", grid=(d // tn, N_DEV),
        in_specs=[pl.BlockSpec((None, s, f8), lambda n, j: (j, 0, 0)),
                  pl.BlockSpec((None, f8, tn), lambda n, j: (j, 0, n))],
        out_specs=pl.BlockSpec((s, tn), lambda n, j: (0, n)),
        out_shape=jax.ShapeDtypeStruct((s, d), F32),
        compiler_params=_cp(2),
    )(act, wd_g)


def _loss_bwd(ffn_out, x1, target, final_g):
    s, d = x1.shape
    tm = min(256, s)

    def body(f_ref, x1_ref, t_ref, gf_ref, dxb_ref, dgf_ref, loss_ref):
        @pl.when(pl.program_id(0) == 0)
        def _():
            dgf_ref[...] = jnp.zeros_like(dgf_ref)
            loss_ref[...] = jnp.zeros_like(loss_ref)

        x2 = x1_ref[...] + f_ref[...]
        r = lax.rsqrt(jnp.mean(x2 * x2, axis=-1, keepdims=True) + EPS)
        nrm = x2 * r
        gf = gf_ref[...]
        err = nrm * gf - t_ref[...]
        loss_ref[...] += jnp.sum(err * err) * (0.5 / d)
        dy = err * (1.0 / d)
        dgf_ref[...] += jnp.sum(dy * nrm, axis=0, keepdims=True)
        dn = dy * gf
        dx = r * (dn - nrm * jnp.mean(dn * nrm, axis=-1, keepdims=True))
        dxb_ref[...] = dx.astype(BF16)

    row = pl.BlockSpec((tm, d), lambda i: (i, 0))
    vec = pl.BlockSpec((1, d), lambda i: (0, 0))
    return pl.pallas_call(
        body, name="loss_bwd", grid=(s // tm,),
        in_specs=[row, row, row, vec],
        out_specs=[row, vec, pl.BlockSpec((8, LANES), lambda i: (0, 0))],
        out_shape=[jax.ShapeDtypeStruct((s, d), BF16),
                   jax.ShapeDtypeStruct((1, d), F32), jax.ShapeDtypeStruct((8, LANES), F32)],
        compiler_params=_cp(1),
    )(ffn_out, x1, target, final_g)


def _ffn_gate_bwd(dx2b, wd_g, dadg, dadu):
    s, d = dx2b.shape
    f8 = dadg.shape[2]
    th = min(1024, s)

    def body(dx_ref, wd_ref, g_ref, u_ref, dg_ref, du_ref, da_ref):
        i = pl.program_id(1)
        chunks = _chunks(th, 256)

        def matmul(rs):
            rows = pl.ds(pl.multiple_of(i * th + rs.start, rs.stop - rs.start), rs.stop - rs.start)
            da_ref[rs, :] = _dot(dx_ref[rows, :], wd_ref[...], NT)

        matmul(chunks[0])
        for k, rs in enumerate(chunks):
            if k + 1 < len(chunks):
                matmul(chunks[k + 1])
            da = da_ref[rs, :]
            dg_ref[rs, :] = (da * g_ref[rs, :].astype(F32)).astype(BF16)
            du_ref[rs, :] = (da * u_ref[rs, :].astype(F32)).astype(BF16)

    aspec = pl.BlockSpec((None, th, f8), lambda j, i: (j, i, 0))
    out = jax.ShapeDtypeStruct((N_DEV, s, f8), BF16)
    return pl.pallas_call(
        body, name="ffn_act_bwd", grid=(N_DEV, s // th),
        in_specs=[pl.BlockSpec((s, d), lambda j, i: (0, 0)),
                  pl.BlockSpec((None, f8, d), lambda j, i: (j, 0, 0)), aspec, aspec],
        out_specs=[aspec, aspec], out_shape=[out, out],
        scratch_shapes=[pltpu.VMEM((th, f8), F32)],
        compiler_params=_cp(2),
    )(dx2b, wd_g, dadg, dadu)


def _wgrad_rows(a3, b, name, after=(), carry=()):
    _, s, k = a3.shape
    n = b.shape[1]
    nc = len(carry)
    c_in, c_out, c_shape, c_sems = _carry_specs(carry)

    def body(a_ref, b_ref, *rest):
        rest = rest[len(after):]
        o_ref = rest[nc]
        j = pl.program_id(0)
        _carry_run(j == 0, j == N_DEV - 1, rest[:nc], rest[nc + 1:2 * nc + 1], rest[2 * nc + 1:])
        o_ref[...] = _dot(a_ref[...], b_ref[...], TN).astype(BF16)

    outs = pl.pallas_call(
        body, name=name, grid=(N_DEV,),
        in_specs=[pl.BlockSpec((None, s, k), lambda j: (j, 0, 0)),
                  pl.BlockSpec((s, n), lambda j: (0, 0))] + _after_specs(after) + c_in,
        out_specs=[pl.BlockSpec((None, k, n), lambda j: (j, 0, 0))] + c_out,
        out_shape=[jax.ShapeDtypeStruct((N_DEV, k, n), BF16)] + c_shape,
        scratch_shapes=c_sems,
        compiler_params=_cp_carry(1, carry),
    )(a3, b, *after, *carry)
    return (outs[0], list(outs[1:])) if nc else outs[0]


def _wgrad_cols(a, b3, name, after=()):
    s, k = a.shape
    if b3.ndim == 2:
        n = b3.shape[1] // N_DEV
        b_spec = pl.BlockSpec((s, n), lambda j: (0, j))
    else:
        n = b3.shape[2]
        b_spec = pl.BlockSpec((None, s, n), lambda j: (j, 0, 0))

    def body(a_ref, b_ref, *rest):
        o_ref = rest[len(after)]
        o_ref[...] = _dot(a_ref[...], b_ref[...], TN).astype(BF16)

    return pl.pallas_call(
        body, name=name, grid=(N_DEV,),
        in_specs=[pl.BlockSpec((s, k), lambda j: (0, 0)), b_spec] + _after_specs(after),
        out_specs=pl.BlockSpec((None, k, n), lambda j: (j, 0, 0)),
        out_shape=jax.ShapeDtypeStruct((N_DEV, k, n), BF16),
        compiler_params=_cp(1),
    )(a, b3, *after)


def _input_grad(pairs, name, after=(), carry=()):
    s = pairs[0][0].shape[1]
    d = pairs[0][1].shape[1]
    tn = min(1024, d)
    npair = len(pairs)
    nc = len(carry)
    c_in, c_out, c_shape, c_sems = _carry_specs(carry)

    def body(*refs):
        ops = refs[:2 * npair]
        rest = refs[2 * npair + len(after):]
        o_ref = rest[nc]
        nh, j = pl.program_id(0), pl.program_id(1)
        _carry_run((nh == 0) & (j == 0), (nh == d // tn - 1) & (j == N_DEV - 1),
                   rest[:nc], rest[nc + 1:2 * nc + 1], rest[2 * nc + 1:])

        @pl.when(j == 0)
        def _():
            o_ref[...] = jnp.zeros_like(o_ref)

        for rs in _chunks(s, 1024):
            part = _dot(ops[0][rs, :], ops[1][...], NT)
            for q in range(1, npair):
                part = part + _dot(ops[2 * q][rs, :], ops[2 * q + 1][...], NT)
            o_ref[rs, :] += part

    in_specs, args = [], []
    for a3, w3 in pairs:
        k = a3.shape[2]
        in_specs += [pl.BlockSpec((None, s, k), lambda n, j: (j, 0, 0)),
                     pl.BlockSpec((None, tn, k), lambda n, j: (j, n, 0))]
        args += [a3, w3]
    outs = pl.pallas_call(
        body, name=name, grid=(d // tn, N_DEV),
        in_specs=in_specs + _after_specs(after) + c_in,
        out_specs=[pl.BlockSpec((s, tn), lambda n, j: (0, n))] + c_out,
        out_shape=[jax.ShapeDtypeStruct((s, d), F32)] + c_shape,
        scratch_shapes=c_sems,
        compiler_params=_cp_carry(2, carry),
    )(*args, *after, *carry)
    return (outs[0], list(outs[1:])) if nc else outs[0]


def _rms_bwd(dh, xres, g, dres, name, with_bf16=True):
    s, d = xres.shape
    tm = min(256, s)

    def body(dh_ref, x_ref, g_ref, dres_ref, dx_ref, *rest):
        dg_ref = rest[-1]
        @pl.when(pl.program_id(0) == 0)
        def _():
            dg_ref[...] = jnp.zeros_like(dg_ref)

        xv = x_ref[...]
        dh_v = dh_ref[...]
        r = lax.rsqrt(jnp.mean(xv * xv, axis=-1, keepdims=True) + EPS)
        nrm = xv * r
        dg_ref[...] += jnp.sum(dh_v * nrm, axis=0, keepdims=True)
        dn = dh_v * g_ref[...]
        dx = dres_ref[...].astype(F32) + r * (dn - nrm * jnp.mean(dn * nrm, axis=-1, keepdims=True))
        dx_ref[...] = dx
        if with_bf16:
            rest[0][...] = dx.astype(BF16)

    row = pl.BlockSpec((tm, d), lambda i: (i, 0))
    vec = pl.BlockSpec((1, d), lambda i: (0, 0))
    copies = [jax.ShapeDtypeStruct((s, d), BF16)] if with_bf16 else []
    outs = pl.pallas_call(
        body, name=name, grid=(s // tm,),
        in_specs=[row, row, vec, row],
        out_specs=[row] + [row] * len(copies) + [vec],
        out_shape=[jax.ShapeDtypeStruct((s, d), F32)] + copies + [jax.ShapeDtypeStruct((1, d), F32)],
        compiler_params=_cp(1),
    )(dh, xres, g, dres)
    return (outs[0], outs[1], outs[2]) if with_bf16 else (outs[0], None, outs[1])


def _wgrad_full(a, b, name, after=(), carry=()):
    s, k = a.shape
    n = b.shape[1]
    tk = min(512, k)
    nc = len(carry)
    c_in, c_out, c_shape, c_sems = _carry_specs(carry)

    def body(a_ref, b_ref, *rest):
        rest = rest[len(after):]
        o_ref = rest[nc]
        j = pl.program_id(0)
        _carry_run(j == 0, j == k // tk - 1, rest[:nc], rest[nc + 1:2 * nc + 1], rest[2 * nc + 1:])
        o_ref[...] = _dot(a_ref[...], b_ref[...], TN).astype(BF16)

    outs = pl.pallas_call(
        body, name=name, grid=(k // tk,),
        in_specs=[pl.BlockSpec((s, tk), lambda j: (0, j)),
                  pl.BlockSpec((s, n), lambda j: (0, 0))] + _after_specs(after) + c_in,
        out_specs=[pl.BlockSpec((tk, n), lambda j: (j, 0))] + c_out,
        out_shape=[jax.ShapeDtypeStruct((k, n), BF16)] + c_shape,
        scratch_shapes=c_sems,
        compiler_params=_cp_carry(1, carry),
    )(a, b, *after, *carry)
    return (outs[0], list(outs[1:])) if nc else outs[0]


def _wgrad_pool(p, dyb, n_groups):
    s, sw = p.shape
    d = dyb.shape[1]
    gw, go = sw // n_groups, d // n_groups
    ts = min(512, s)
    ns = s // ts

    def body(a_ref, b_ref, o_ref, acc_ref):
        i = pl.program_id(1)

        @pl.when(i == 0)
        def _():
            acc_ref[...] = jnp.zeros_like(acc_ref)

        acc_ref[...] += _dot(a_ref[...], b_ref[...], TN)

        @pl.when(i == ns - 1)
        def _():
            o_ref[...] = acc_ref[...].astype(BF16)

    return pl.pallas_call(
        body, name="wgrad_pool", grid=(n_groups, ns),
        in_specs=[pl.BlockSpec((ts, gw), lambda g, i: (i, g)),
                  pl.BlockSpec((ts, go), lambda g, i: (i, g))],
        out_specs=pl.BlockSpec((None, gw, go), lambda g, i: (g, 0, 0)),
        out_shape=jax.ShapeDtypeStruct((n_groups, gw, go), BF16),
        scratch_shapes=[pltpu.VMEM((gw, go), F32)],
        compiler_params=_cp(2),
    )(p, dyb)


def _wo_bwd(dx1b, wo, factors, sw, after=()):
    s, d = dx1b.shape
    tn = d // N_DEV
    nq = sw // tn

    def body(dx_ref, wo_ref, fya_ref, fyb_ref, fga_ref, fgb_ref, fsc_ref, *rest):
        dya_ref, dyb_ref, dp_ref, dbg_ref, dsc_ref, dm_ref = rest[len(after):]
        dbg_ref[...] = jnp.zeros_like(dbg_ref)
        dsc_ref[...] = jnp.zeros_like(dsc_ref)
        for rs in _chunks(s, 1024):
            dm_ref[rs, :] = _dot(dx_ref[rs, :], wo_ref[...], NT)
        for rs in _chunks(s, 256):
            dm = dm_ref[rs, :]
            dya_ref[rs, :] = (dm * fya_ref[rs, :].astype(F32)).astype(BF16)
            dyb_ref[rs, :] = (dm * fyb_ref[rs, :].astype(F32)).astype(BF16)
            dsc_ref[...] += jnp.sum(dm * fsc_ref[rs, :].astype(F32), axis=0, keepdims=True)
            dga = dm * fga_ref[rs, :].astype(F32)
            dgb = dm * fgb_ref[rs, :].astype(F32)
            dp_ref[0, rs, :] = dga.astype(BF16)
            dp_ref[1, rs, :] = dgb.astype(BF16)
            dbg_ref[0:1, :] += jnp.sum(dga, axis=0, keepdims=True)
            dbg_ref[1:2, :] += jnp.sum(dgb, axis=0, keepdims=True)

    col = pl.BlockSpec((s, tn), lambda j: (0, j))
    out = jax.ShapeDtypeStruct((s, d), BF16)
    return pl.pallas_call(
        body, name="wo_bwd", grid=(N_DEV,),
        in_specs=[pl.BlockSpec((s, d), lambda j: (0, 0)),
                  pl.BlockSpec((tn, d), lambda j: (j, 0))] + [col] * 5 + _after_specs(after),
        out_specs=[col, col,
                   pl.BlockSpec((2, None, s, tn), lambda j: (1, j // nq, 0, j % nq)),
                   pl.BlockSpec((2, tn), lambda j: (0, j)),
                   pl.BlockSpec((1, tn), lambda j: (0, j))],
        out_shape=[out, out, jax.ShapeDtypeStruct((4, 2, s, sw), BF16),
                   jax.ShapeDtypeStruct((2, d), F32), jax.ShapeDtypeStruct((1, d), F32)],
        scratch_shapes=[pltpu.VMEM((s, tn), F32)],
        compiler_params=_cp(1),
    )(dx1b, wo, *factors, *after)


def _conv_bwd(dproj, dya, wa, proj, conv_w, conv_b):
    s, d = dya.shape
    sw, tn = wa.shape[1], wa.shape[2]
    tc = min(LANES, sw)

    def body(dproj_hbm, dya_ref, wa_ref, ba_ref, ca_ref, va_ref, cw_ref, cb_ref,
             dp_ref, dcw_ref, dcb_ref, dz_ref):
        del dproj_hbm
        for rs in _chunks(s, 512):
            part = _dot(dya_ref[rs, 0:tn], wa_ref[0], NT)
            for j in range(1, N_DEV):
                part = part + _dot(dya_ref[rs, j * tn:(j + 1) * tn], wa_ref[j], NT)
            dz_ref[rs, :] = part
        dz = dz_ref[...]
        ba, ca, va = ba_ref[...], ca_ref[...], va_ref[...]
        cv = ca * va
        cv1, cv2 = _shift_down(cv, 1), _shift_down(cv, 2)
        w0, w1, w2 = cw_ref[0:1, :], cw_ref[1:2, :], cw_ref[2:3, :]
        u = cb_ref[...] + w0 * cv2 + w1 * cv1 + w2 * cv
        du = dz * ba
        dp_ref[0] = (dz * u).astype(BF16)
        dcv = w2 * du + w1 * _shift_up(du, 1) + w0 * _shift_up(du, 2)
        dp_ref[1] = (dcv * va).astype(BF16)
        dp_ref[2] = (dcv * ca).astype(BF16)
        dcw_ref[0:1, :] = jnp.sum(du * cv2, axis=0, keepdims=True)
        dcw_ref[1:2, :] = jnp.sum(du * cv1, axis=0, keepdims=True)
        dcw_ref[2:3, :] = jnp.sum(du * cv, axis=0, keepdims=True)
        dcb_ref[...] = jnp.sum(du, axis=0, keepdims=True)

    def part(k):
        return pl.BlockSpec((None, s, tc), lambda i: (k, 0, i))

    return pl.pallas_call(
        body, name="conv_bwd", grid=(sw // tc,),
        in_specs=[pl.BlockSpec(memory_space=pl.ANY),
                  pl.BlockSpec((s, d), lambda i: (0, 0)),
                  pl.BlockSpec((N_DEV, tc, tn), lambda i: (0, i, 0)),
                  part(0), part(1), part(2),
                  pl.BlockSpec((CONV_K, tc), lambda i: (0, i)), pl.BlockSpec((1, tc), lambda i: (0, i))],
        out_specs=[pl.BlockSpec((3, s, tc), lambda i: (0, 0, i)),
                   pl.BlockSpec((CONV_K, tc), lambda i: (0, i)), pl.BlockSpec((1, tc), lambda i: (0, i))],
        out_shape=[jax.ShapeDtypeStruct(dproj.shape, BF16),
                   jax.ShapeDtypeStruct((CONV_K, sw), F32), jax.ShapeDtypeStruct((1, sw), F32)],
        scratch_shapes=[pltpu.VMEM((s, tc), F32)],
        input_output_aliases={0: 0},
        compiler_params=_cp(1),
    )(dproj, dya, wa, proj, proj, proj, conv_w, conv_b)


def _pool_bwd(dproj, dyb, wpool):
    s, d = dyb.shape
    n_groups, gw, go = wpool.shape

    def body(dproj_hbm, dyb_ref, wp_ref, dp_ref):
        del dproj_hbm
        for gi, window in enumerate(POOL_WINDOWS):
            @pl.when(pl.program_id(0) == gi)
            def _():
                dpool = _dot(dyb_ref[...], wp_ref[...], NT)
                acc, k = dpool / _pool_counts(dpool.shape, window), 1
                while k < window:
                    acc = acc + _shift_up(acc, k)
                    k *= 2
                dp_ref[...] = (acc - dpool).astype(BF16)

    return pl.pallas_call(
        body, name="pool_bwd", grid=(n_groups,),
        in_specs=[pl.BlockSpec(memory_space=pl.ANY),
                  pl.BlockSpec((s, go), lambda g: (0, g)),
                  pl.BlockSpec((None, gw, go), lambda g: (g, 0, 0))],
        out_specs=pl.BlockSpec((None, s, gw), lambda g: (3, 0, g)),
        out_shape=jax.ShapeDtypeStruct(dproj.shape, BF16),
        input_output_aliases={0: 0},
        compiler_params=_cp(1),
    )(dproj, dyb, wpool)


def _rows128(v):
    return v.reshape(-1, LANES)


def kernel(x, norm1_g, w_in, b_gate, conv_w, conv_b, w_a_out, w_pool, pool_scale, w_o, norm2_g, w_ffn_gate, w_ffn_up, w_ffn_down, final_g, loss_target, m_norm1_g, m_w_in, m_b_gate, m_conv_w, m_conv_b, m_w_a_out, m_w_pool, m_pool_scale, m_w_o, m_norm2_g, m_w_ffn_gate, m_w_ffn_up, m_w_ffn_down, m_final_g, v_norm1_g, v_w_in, v_b_gate, v_conv_w, v_conv_b, v_w_a_out, v_w_pool, v_pool_scale, v_w_o, v_norm2_g, v_w_ffn_gate, v_w_ffn_up, v_w_ffn_down, v_final_g):
    s, d = x.shape[1], x.shape[2]
    sw = w_in.shape[2]
    n_groups = w_pool.shape[1]
    gw = w_pool.shape[2]
    go = w_pool.shape[3] * N_DEV
    f8 = w_ffn_gate.shape[2]
    cws = conv_w.shape[2]
    assert sw == conv_w.shape[2] * N_DEV == gw * n_groups and go * n_groups == d and n_groups == len(POOL_WINDOWS)

    xi, yi, ci = _coords()
    me = 4 * xi + 2 * yi + ci
    my_chip = 2 * xi + yi

    x2d = x.reshape(s, d)
    target = loss_target.reshape(s, d)
    final_g2 = final_g.reshape(1, d)
    b_gate2 = b_gate.reshape(2, d)

    big_names = ["w_in", "w_a_out", "w_pool", "w_o", "w_ffn_gate", "w_ffn_up", "w_ffn_down"]
    big_w = [w_in, w_a_out, w_pool, w_o, w_ffn_gate, w_ffn_up, w_ffn_down]
    big_m = [m_w_in, m_w_a_out, m_w_pool, m_w_o, m_w_ffn_gate, m_w_ffn_up, m_w_ffn_down]
    big_v = [v_w_in, v_w_a_out, v_w_pool, v_w_o, v_w_ffn_gate, v_w_ffn_up, v_w_ffn_down]
    shapes2d = [(w.size // w.shape[-1], w.shape[-1]) for w in big_w]
    big_w2 = [w.reshape(sh) for w, sh in zip(big_w, shapes2d)]
    transposed = (4, 5)

    def view2d(t, a):
        t2 = t.reshape(shapes2d[a])
        return t2.T if a in transposed else t2

    def unview(o, a):
        return (o.T if a in transposed else o).reshape(big_w[a].shape)

    sb = [_cast_bf16(w, "cast_" + nm) for w, nm in zip(big_w2, big_names)]
    win_g, wa_g, wpool_g, wo_g = _allgather_big(sb[0:4], "allgather_mixer", COLLECTIVE_GATHER)
    wg_g, wu_g = _allgather_big(sb[4:6], "allgather_ffn_up", COLLECTIVE_GATHER)
    (wd_g,) = _allgather_big(sb[6:7], "allgather_ffn_down", COLLECTIVE_GATHER)
    convw_g = _allgather_small(jnp.pad(conv_w.reshape(CONV_K, cws), ((0, 8 - CONV_K), (0, 0))), "allgather_conv_w")
    conv_w_full = convw_g[:, :CONV_K, :].transpose(1, 0, 2).reshape(CONV_K, sw)
    wpool = wpool_g.reshape(N_DEV, n_groups, gw, go // N_DEV).transpose(1, 2, 0, 3).reshape(n_groups, gw, go)
    wo = wo_g.reshape(d, d)

    h = _rms_fwd(x2d, norm1_g)
    proj = _proj_fwd(h, win_g)
    z = _conv_fwd(proj, conv_w_full, conv_b)
    p = _pool_fwd(proj)
    merged, *merge_factors = _merge_fwd(z, wa_g, p, wpool, proj, b_gate2, pool_scale)
    x1, h2 = _wo_fwd(merged, wo, x2d, norm2_g)
    dadu, dadg, act = _ffn_up_act_fwd(h2, wg_g, wu_g)
    ffn_out = _ffn_down_fwd(act, wd_g)
    dx2b, d_final_g, loss_blk = _loss_bwd(ffn_out, x1, target, final_g2)

    other_chips = jnp.stack([2 * (1 - xi) + yi, 2 * xi + (1 - yi), 2 * (1 - xi) + (1 - yi)])
    others = jnp.concatenate([other_chips, 2 * other_chips + ci]).astype(jnp.int32)

    def partials(grads, recvs, names):
        if all(g.shape == grads[0].shape for g in grads):
            return list(_chip_partial(others, grads, recvs, "chip_partial_" + names[0]))
        return [_chip_partial(others, [g3], [r], "chip_partial_" + nm)[0] for g3, r, nm in zip(grads, recvs, names)]

    own = jnp.stack([me, my_chip]).astype(jnp.int32)

    def adam(idx, g3s, sibs, chipss):
        wmvs = [(view2d(big_w[a], a), view2d(big_m[a], a), view2d(big_v[a], a)) for a in idx]
        outs = _adam_big(own, wmvs, g3s, sibs, chipss, "adam_" + big_names[idx[0]])
        for a, o4 in zip(idx, outs):
            big_out[a] = [unview(o, a) for o in o4]

    big_out = [None] * len(big_names)
    dg_act, du_act = _ffn_gate_bwd(dx2b, wd_g, dadg, dadu)
    gw_gate = _wgrad_rows(dg_act, h2, "wgrad_ffn_gate")
    gw_up = _wgrad_rows(du_act, h2, "wgrad_ffn_up")
    gw_down, sib_gu = _wgrad_rows(act, dx2b, "wgrad_ffn_down", carry=[gw_gate, gw_up])
    ps_gu = partials([gw_gate, gw_up], sib_gu, ["w_ffn_gate", "w_ffn_up"])
    chips_gu = _exchange_chips(ps_gu, "rs_chips_ffn_up", COLLECTIVE_CHIPS)
    dh2, sib_down = _input_grad([(dg_act, wg_g), (du_act, wu_g)], "ffn_in_bwd", after=ps_gu, carry=[gw_down])
    ps_down = partials([gw_down], sib_down, ["w_ffn_down"])
    chips_down = _exchange_chips(ps_down, "rs_chips_ffn_down", COLLECTIVE_CHIPS)
    dx1, dx1b, d_norm2_g = _rms_bwd(dh2, x1, norm2_g, dx2b, "rms2_bwd")
    dya, dyb, dproj42, d_b_gate, d_pool_scale = _wo_bwd(dx1b, wo, merge_factors, sw, after=ps_down)
    dproj = dproj42.reshape(N_DEV, s, sw)
    dproj, d_conv_w, d_conv_b = _conv_bwd(dproj, dya, wa_g, proj, conv_w_full, conv_b)
    dproj = _pool_bwd(dproj, dyb, wpool)
    gw_in = _wgrad_cols(h, dproj, "wgrad_in")
    gw_o, sib_in = _wgrad_full(merged, dx1b, "wgrad_o", carry=[gw_in])
    ps_in = partials([gw_in], sib_in, ["w_in"])
    chips_in = _exchange_chips(ps_in, "rs_chips_w_in", COLLECTIVE_CHIPS)
    gw_a = _wgrad_cols(z, dya, "wgrad_a_out", after=ps_in)
    gw_pool = _wgrad_pool(p, dyb, n_groups)
    mix3 = [gw_a,
            gw_pool.reshape(n_groups, gw, N_DEV, go // N_DEV).transpose(2, 0, 1, 3).reshape(N_DEV, n_groups * gw, go // N_DEV),
            gw_o.reshape(N_DEV, d // N_DEV, d)]
    adam([4, 5, 6], [gw_gate, gw_up, gw_down], sib_gu + sib_down, chips_gu + chips_down)
    dh, sib_mix = _input_grad([(dproj, win_g)], "proj_in_bwd", after=[big_out[6][0]], carry=mix3)
    ps_mix = partials(mix3, sib_mix, ["w_a_out", "w_pool", "w_o"])
    chips_mix = _exchange_chips(ps_mix, "rs_chips_mixer", COLLECTIVE_CHIPS)
    grad_x, _, d_norm1_g = _rms_bwd(dh, x2d, norm1_g, dx1, "rms1_bwd", with_bf16=False)
    adam([0], [gw_in], sib_in, chips_in)
    for k in range(3):
        adam([1 + k], [mix3[k]], [sib_mix[k]], [chips_mix[k]])

    small_parts = [d_norm1_g, d_b_gate, d_conv_w, d_conv_b, d_pool_scale, d_norm2_g, d_final_g, loss_blk]
    rows = [v.size // LANES for v in small_parts]
    row0 = [sum(rows[:k]) for k in range(len(rows))]
    packed = jnp.concatenate([_rows128(v) for v in small_parts], axis=0)
    gathered = _allgather_small(packed, "allgather_small_grads")
    small_names = ["norm1_g", "b_gate", "conv_b", "pool_scale", "norm2_g", "final_g", "conv_w"]
    small_w = [norm1_g, b_gate, conv_b, pool_scale, norm2_g, final_g]
    small_m = [m_norm1_g, m_b_gate, m_conv_b, m_pool_scale, m_norm2_g, m_final_g]
    small_v = [v_norm1_g, v_b_gate, v_conv_b, v_pool_scale, v_norm2_g, v_final_g]
    finished = _small_finish(gathered, [tuple(_rows128(t) for t in wmv) for wmv in zip(small_w, small_m, small_v)],
                             [row0[k] for k in (0, 1, 3, 4, 5, 6)], [(row0[2], rows[2]), (row0[7], rows[7])])
    g_convw_full, loss_rows = finished[0], finished[1]
    loss = loss_rows[0, 0]
    small_out = [[t.reshape(w.shape) for t in finished[2 + 4 * k:6 + 4 * k]] for k, w in enumerate(small_w)]
    g_convw = lax.dynamic_slice(g_convw_full.reshape(CONV_K, sw), (0, me * cws), (CONV_K, cws))
    cw_delta, cw_m, cw_v = _adam_small(conv_w.reshape(CONV_K, cws), g_convw,
                                       m_conv_w.reshape(CONV_K, cws), v_conv_w.reshape(CONV_K, cws))
    small_out.append([t.reshape(conv_w.shape) for t in (g_convw, cw_delta, cw_m, cw_v)])

    order = ["norm1_g", "w_in", "b_gate", "conv_w", "conv_b", "w_a_out", "w_pool", "pool_scale", "w_o", "norm2_g",
             "w_ffn_gate", "w_ffn_up", "w_ffn_down", "final_g"]
    per_kind = [{}, {}, {}, {}]
    for a, nm in enumerate(big_names):
        for kind in range(4):
            per_kind[kind][nm] = big_out[a][kind]
    for k, nm in enumerate(small_names):
        for kind in range(4):
            per_kind[kind][nm] = small_out[k][kind]
    result = [loss, grad_x.reshape(x.shape)]
    for kind in range(4):
        result += [per_kind[kind][nm] for nm in order]
    return tuple(result)
```

```python
import jax
import jax.numpy as jnp
from jax import lax
from jax.experimental import pallas as pl
from jax.experimental.pallas import tpu as pltpu
from jax.experimental.pallas import tpu_sc as plsc

F32 = jnp.float32
BF16 = jnp.bfloat16
MESH = pl.DeviceIdType.MESH

N_DEV = 8
EPS = 1e-6
CONV_K = 3
POOL_WINDOWS = (2, 4, 8, 16)
ADAM_LR = 0.001
ADAM_B1 = 0.9
ADAM_B2 = 0.999
ADAM_EPS = 1e-08
ADAM_WD = 0.01
ADAM_STEP = 10

V7X_VMEM_LIMIT_BYTES = 56 * 1024 * 1024
LANES = 128

COLLECTIVE_GATHER = 1
COLLECTIVE_SIBLING = 2
COLLECTIVE_CHIPS = 3
SEQUENCER_COST_BYTES = 4 * 10**9

NN = ((1,), (0,))
NT = ((1,), (1,))
TN = ((0,), (0,))


def _dot(a, b, dims):
    return lax.dot_general(a, b, (dims, ((), ())), preferred_element_type=F32)


def _cp(n_axes):
    return pltpu.CompilerParams(dimension_semantics=("arbitrary",) * n_axes,
                                vmem_limit_bytes=V7X_VMEM_LIMIT_BYTES)


def _row_tile(rows, bytes_per_row, cap_bytes):
    best = None
    for t in range(16, rows + 1, 16):
        if rows % t == 0 and t * bytes_per_row <= cap_bytes:
            best = t
    return best if best is not None else rows


def _chunks(total, size):
    size = min(size, total)
    assert total % size == 0
    return [slice(r, r + size) for r in range(0, total, size)]


def _after_specs(after):
    return [pl.BlockSpec(memory_space=pl.ANY)] * len(after)


def _shift_down(v, k):
    row = lax.broadcasted_iota(jnp.int32, v.shape, 0)
    return jnp.where(row >= k, pltpu.roll(v, k, 0), 0.0)


def _shift_up(v, k):
    n = v.shape[0]
    row = lax.broadcasted_iota(jnp.int32, v.shape, 0)
    return jnp.where(row < n - k, pltpu.roll(v, n - k, 0), 0.0)


def _sigmoid(v):
    return jax.nn.sigmoid(v)


def _cast_bf16(w2d, name):
    rows, cols = w2d.shape
    tr = _row_tile(rows, cols * 4, 2 << 20)

    def body(i_ref, o_ref):
        o_ref[...] = i_ref[...].astype(BF16)

    return pl.pallas_call(
        body, name=name, grid=(rows // tr,),
        in_specs=[pl.BlockSpec((tr, cols), lambda i: (i, 0))],
        out_specs=pl.BlockSpec((tr, cols), lambda i: (i, 0)),
        out_shape=jax.ShapeDtypeStruct((rows, cols), BF16),
        compiler_params=_cp(1),
    )(w2d)


def _rms_fwd(x2d, g):
    s, d = x2d.shape
    tm = min(256, s)

    def body(x_ref, g_ref, h_ref):
        xv = x_ref[...]
        r = lax.rsqrt(jnp.mean(xv * xv, axis=-1, keepdims=True) + EPS)
        h_ref[...] = (xv * r * g_ref[...]).astype(BF16)

    return pl.pallas_call(
        body, name="rms1_fwd", grid=(s // tm,),
        in_specs=[pl.BlockSpec((tm, d), lambda i: (i, 0)), pl.BlockSpec((1, d), lambda i: (0, 0))],
        out_specs=pl.BlockSpec((tm, d), lambda i: (i, 0)),
        out_shape=jax.ShapeDtypeStruct((s, d), BF16),
        compiler_params=_cp(1),
    )(x2d, g)


def _coords():
    return lax.axis_index("x"), lax.axis_index("y"), lax.axis_index("c")


def _slot(p):
    return 4 * p[0] + 2 * p[1] + p[2]


def _handshake(peers):
    barrier = pltpu.get_barrier_semaphore()
    for peer in peers:
        pl.semaphore_signal(barrier, inc=1, device_id=peer, device_id_type=MESH)
    pl.semaphore_wait(barrier, len(peers))


def _sequencer_call(body, out_type, scratch_types, name, collective_id):
    return pl.kernel(
        body, out_type=out_type, name=name,
        mesh=plsc.ScalarSubcoreMesh(axis_name="seq", num_cores=1),
        scratch_types=scratch_types,
        cost_estimate=pl.CostEstimate(flops=0, transcendentals=0, bytes_accessed=SEQUENCER_COST_BYTES),
        compiler_params=pltpu.CompilerParams(collective_id=collective_id))


def _allgather_big(shards, name, collective_id, after=()):
    n = len(shards)

    def body(*refs):
        ins, outs = refs[:n], refs[n + len(after):2 * n + len(after)]
        send_sems, recv_sems, local_sems = refs[2 * n + len(after):]
        x, y, c = _coords()
        me, sibling = (x, y, c), (x, y, 1 - c)
        x_nbr, y_nbr, diag = (1 - x, y), (x, 1 - y), (1 - x, 1 - y)
        relay_from = (x + (1 - c) * (1 - 2 * x), y + c * (1 - 2 * y))
        relay_to = (x + c * (1 - 2 * x), y + (1 - c) * (1 - 2 * y))
        _handshake([sibling, (*x_nbr, c), (*y_nbr, c)])

        def copy(a, k, block, to, src=None):
            dst = outs[a].at[_slot(block)]
            return pltpu.make_async_remote_copy(
                src_ref=dst if src is None else src, dst_ref=dst,
                send_sem=send_sems.at[a, k], recv_sem=recv_sems.at[a, k],
                device_id=to, device_id_type=MESH)

        mine, sends = [], []
        for a in range(n):
            cp = pltpu.make_async_copy(ins[a], outs[a].at[_slot(me)], local_sems.at[a])
            cp.start()
            mine.append(cp)
            first = [copy(a, 0, me, sibling, src=ins[a]),
                     copy(a, 1, me, (*x_nbr, c), src=ins[a]),
                     copy(a, 2, me, (*y_nbr, c), src=ins[a])]
            for cp in first:
                cp.start()
            sends += first
        for a in range(n):
            copy(a, 1 + c, (*relay_from, c), me).wait_recv()
            passed = [copy(a, 3, (*relay_from, c), (*relay_to, c)), copy(a, 4 + c, (*relay_from, c), sibling)]
            for cp in passed:
                cp.start()
            copy(a, 2 - c, (*relay_to, c), me).wait_recv()
            cp = copy(a, 5 - c, (*relay_to, c), sibling)
            cp.start()
            passed.append(cp)
            copy(a, 3, (*diag, c), me).wait_recv()
            cp = copy(a, 6, (*diag, c), sibling)
            cp.start()
            sends += passed + [cp]
        for a in range(n):
            copy(a, 0, sibling, me).wait_recv()
            copy(a, 4, (*x_nbr, 1 - c), me).wait_recv()
            copy(a, 5, (*y_nbr, 1 - c), me).wait_recv()
            copy(a, 6, (*diag, 1 - c), me).wait_recv()
        for cp in sends:
            cp.wait_send()
        for cp in mine:
            cp.wait()

    return _sequencer_call(
        body, [jax.ShapeDtypeStruct((N_DEV,) + s.shape, s.dtype) for s in shards],
        [pltpu.SemaphoreType.DMA((n, 7)), pltpu.SemaphoreType.DMA((n, 7)), pltpu.SemaphoreType.DMA((n,))],
        name, collective_id)(*shards, *after)


def _sibling_copies(ins, recvs, send_sems, recv_sems):
    x, y, c = _coords()
    return [pltpu.make_async_remote_copy(
        src_ref=ins[a].at[2 * q + (1 - c)], dst_ref=recvs[a].at[q],
        send_sem=send_sems.at[a, q], recv_sem=recv_sems.at[a, q],
        device_id=(x, y, 1 - c), device_id_type=MESH) for a in range(len(ins)) for q in range(4)]


def _carry_specs(carry):
    any_spec = pl.BlockSpec(memory_space=pl.ANY)
    n = len(carry)
    sems = [pltpu.SemaphoreType.DMA((n, 4)), pltpu.SemaphoreType.DMA((n, 4))] if n else []
    return ([any_spec] * n, [any_spec] * n,
            [jax.ShapeDtypeStruct((4,) + g.shape[1:], g.dtype) for g in carry], sems)


def _carry_run(first, last, ins, recvs, sems):
    if not ins:
        return

    @pl.when(first)
    def _():
        x, y, c = _coords()
        _handshake([(x, y, 1 - c)])
        for cp in _sibling_copies(ins, recvs, *sems):
            cp.start()

    @pl.when(last)
    def _():
        copies = _sibling_copies(ins, recvs, *sems)
        for cp in copies:
            cp.wait_recv()
        for cp in copies:
            cp.wait_send()


def _cp_carry(n_axes, carry):
    if not carry:
        return _cp(n_axes)
    return pltpu.CompilerParams(dimension_semantics=("arbitrary",) * n_axes, vmem_limit_bytes=V7X_VMEM_LIMIT_BYTES,
                                collective_id=COLLECTIVE_SIBLING)


def _exchange_chips(psums, name, collective_id):
    n = len(psums)

    def body(*refs):
        ins, outs = refs[:n], refs[n:2 * n]
        send_sems, recv_sems = refs[2 * n:]
        x, y, c = _coords()
        chips = [(1 - x, y), (x, 1 - y), (1 - x, 1 - y)]
        _handshake([(*chip, c) for chip in chips])
        copies = []
        for a in range(n):
            for j, chip in enumerate(chips):
                cp = pltpu.make_async_remote_copy(
                    src_ref=ins[a].at[2 * chip[0] + chip[1]], dst_ref=outs[a].at[j],
                    send_sem=send_sems.at[a, j], recv_sem=recv_sems.at[a, j],
                    device_id=(*chip, c), device_id_type=MESH)
                cp.start()
                copies.append(cp)
        for cp in copies:
            cp.wait_recv()
        for cp in copies:
            cp.wait_send()

    return _sequencer_call(
        body, [jax.ShapeDtypeStruct((3,) + p.shape[1:], p.dtype) for p in psums],
        [pltpu.SemaphoreType.DMA((n, 3)), pltpu.SemaphoreType.DMA((n, 3))],
        name, collective_id)(*psums)


def _allgather_small(v2d, name):
    rows, cols = v2d.shape

    def body(v_ref, out_ref, send_sems, recv_sems):
        x, y, c = _coords()
        me = (x, y, c)
        out_ref[_slot(me)] = v_ref[...]
        peers = []
        for k in range(1, N_DEV):
            fx, fy, fc = (k >> 2) & 1, (k >> 1) & 1, k & 1
            peers.append(((1 - x) if fx else x, (1 - y) if fy else y, (1 - c) if fc else c))
        sends = []
        for k, peer in enumerate(peers):
            cp = pltpu.make_async_remote_copy(
                src_ref=v_ref, dst_ref=out_ref.at[_slot(me)],
                send_sem=send_sems.at[k], recv_sem=recv_sems.at[k],
                device_id=peer, device_id_type=MESH)
            cp.start()
            sends.append(cp)
        for k, peer in enumerate(peers):
            pltpu.make_async_remote_copy(
                src_ref=v_ref, dst_ref=out_ref.at[_slot(peer)],
                send_sem=send_sems.at[k], recv_sem=recv_sems.at[k],
                device_id=peer, device_id_type=MESH).wait_recv()
        for cp in sends:
            cp.wait_send()

    vmem = pl.BlockSpec(memory_space=pltpu.VMEM)
    return pl.pallas_call(
        body, name=name, in_specs=[vmem], out_specs=vmem,
        out_shape=jax.ShapeDtypeStruct((N_DEV, rows, cols), v2d.dtype),
        scratch_shapes=[pltpu.SemaphoreType.DMA((N_DEV - 1,)), pltpu.SemaphoreType.DMA((N_DEV - 1,))],
    )(v2d)


def _chip_partial(others, grads, recvs, name):
    n = len(grads)
    _, rows, cols = grads[0].shape
    tr = _row_tile(rows, cols * 2, (2 << 20) // n)

    def body(others_ref, *refs):
        for a in range(n):
            refs[2 * n + a][...] = (refs[a][...].astype(F32) + refs[n + a][...].astype(F32)).astype(BF16)

    return pl.pallas_call(
        body, name=name,
        grid_spec=pltpu.PrefetchScalarGridSpec(
            num_scalar_prefetch=1, grid=(3, rows // tr),
            in_specs=[pl.BlockSpec((None, tr, cols), lambda k, i, o: (o[3 + k], i, 0))] * n
            + [pl.BlockSpec((None, tr, cols), lambda k, i, o: (o[k], i, 0))] * n,
            out_specs=[pl.BlockSpec((None, tr, cols), lambda k, i, o: (o[k], i, 0))] * n),
        out_shape=[jax.ShapeDtypeStruct((4, rows, cols), BF16)] * n,
        compiler_params=_cp(2),
    )(others, *grads, *recvs)


def _adam_math(w, g, m, v):
    m = ADAM_B1 * m + (1.0 - ADAM_B1) * g
    v = ADAM_B2 * v + (1.0 - ADAM_B2) * (g * g)
    m_hat = m / (1.0 - ADAM_B1 ** ADAM_STEP)
    v_hat = v / (1.0 - ADAM_B2 ** ADAM_STEP)
    delta = -ADAM_LR * (m_hat / (jnp.sqrt(v_hat) + ADAM_EPS) + ADAM_WD * w)
    return delta, m, v


def _adam_big(own, wmvs, g3s, recv_sibs, recv_chipss, name):
    n = len(wmvs)
    rows, cols = wmvs[0][0].shape
    tr = _row_tile(rows, cols * 4, (2 << 20) // n)

    def body(own_ref, *refs):
        ins, outs = refs[:6 * n], refs[6 * n:]
        for a in range(n):
            w_ref, m_ref, v_ref, g_ref, rs_ref, rc_ref = ins[6 * a:6 * a + 6]
            g = g_ref[...].astype(F32) + rs_ref[...].astype(F32)
            g = g + rc_ref[0].astype(F32)
            g = g + rc_ref[1].astype(F32)
            g = g + rc_ref[2].astype(F32)
            delta, m_new, v_new = _adam_math(w_ref[...], g, m_ref[...], v_ref[...])
            outs[4 * a][...] = g
            outs[4 * a + 1][...] = delta
            outs[4 * a + 2][...] = m_new
            outs[4 * a + 3][...] = v_new

    blk = pl.BlockSpec((tr, cols), lambda i, o: (i, 0))
    per_shard = [blk, blk, blk,
                 pl.BlockSpec((None, tr, cols), lambda i, o: (o[0], i, 0)),
                 pl.BlockSpec((None, tr, cols), lambda i, o: (o[1], i, 0)),
                 pl.BlockSpec((3, tr, cols), lambda i, o: (0, i, 0))]
    out = jax.ShapeDtypeStruct((rows, cols), F32)
    args = [t for a in range(n) for t in (*wmvs[a], g3s[a], recv_sibs[a], recv_chipss[a])]
    outs = pl.pallas_call(
        body, name=name,
        grid_spec=pltpu.PrefetchScalarGridSpec(
            num_scalar_prefetch=1, grid=(rows // tr,),
            in_specs=per_shard * n, out_specs=[blk] * (4 * n)),
        out_shape=[out] * (4 * n),
        compiler_params=_cp(1),
    )(own, *args)
    return [outs[4 * a:4 * a + 4] for a in range(n)]


def _small_finish(gathered, params, row_offs, extra_rows):
    n = len(params)

    def body(g_ref, *refs):
        ins, outs = refs[:3 * n], refs[3 * n:]
        total = g_ref[0]
        for k in range(1, N_DEV):
            total = total + g_ref[k]
        for e, (r0, nr) in enumerate(extra_rows):
            outs[e][...] = total[r0:r0 + nr, :]
        for p in range(n):
            w_ref, m_ref, v_ref = ins[3 * p:3 * p + 3]
            g_out, d_out, m_out, v_out = outs[len(extra_rows) + 4 * p:len(extra_rows) + 4 * p + 4]
            g = total[row_offs[p]:row_offs[p] + w_ref.shape[0], :]
            delta, m_new, v_new = _adam_math(w_ref[...], g, m_ref[...], v_ref[...])
            g_out[...] = g
            d_out[...] = delta
            m_out[...] = m_new
            v_out[...] = v_new

    vmem = pl.BlockSpec(memory_space=pltpu.VMEM)
    out_shape = [jax.ShapeDtypeStruct((nr, LANES), F32) for _, nr in extra_rows]
    for w, _, _ in params:
        out_shape += [jax.ShapeDtypeStruct(w.shape, F32)] * 4
    flat = [t for wmv in params for t in wmv]
    return pl.pallas_call(body, name="small_finish", in_specs=[vmem] * (1 + len(flat)),
                          out_specs=[vmem] * len(out_shape), out_shape=out_shape)(gathered, *flat)


def _adam_small(w, g, m, v):
    def body(w_ref, g_ref, m_ref, v_ref, do_ref, mo_ref, vo_ref):
        delta, m_new, v_new = _adam_math(w_ref[...], g_ref[...], m_ref[...], v_ref[...])
        do_ref[...] = delta
        mo_ref[...] = m_new
        vo_ref[...] = v_new

    vmem = pl.BlockSpec(memory_space=pltpu.VMEM)
    out = jax.ShapeDtypeStruct(w.shape, F32)
    return pl.pallas_call(body, name="adam_small", in_specs=[vmem] * 4, out_specs=[vmem] * 3,
                          out_shape=[out, out, out])(w, g, m, v)


def _proj_fwd(h, win_g):
    s, d = h.shape
    sw = win_g.shape[2]
    tn = min(512, sw)
    nh = sw // tn

    def body(h_ref, w_ref, o_ref):
        for rs in _chunks(s, 512):
            o_ref[rs, :] = _dot(h_ref[rs, :], w_ref[...], NN)

    return pl.pallas_call(
        body, name="proj_fwd", grid=(N_DEV * nh,),
        in_specs=[pl.BlockSpec((s, d), lambda j: (0, 0)),
                  pl.BlockSpec((None, d, tn), lambda j: (j // nh, 0, j % nh))],
        out_specs=pl.BlockSpec((None, s, tn), lambda j: (j // nh, 0, j % nh)),
        out_shape=jax.ShapeDtypeStruct((N_DEV, s, sw), F32),
        compiler_params=_cp(1),
    )(h, win_g)


def _conv_fwd(proj, conv_w, conv_b):
    _, s, sw = proj.shape
    tc = min(LANES, sw)

    def body(ba_ref, ca_ref, va_ref, cw_ref, cb_ref, z_ref):
        cv = ca_ref[...] * va_ref[...]
        u = (cb_ref[...] + cw_ref[0:1, :] * _shift_down(cv, 2) + cw_ref[1:2, :] * _shift_down(cv, 1)
             + cw_ref[2:3, :] * cv)
        z_ref[...] = (ba_ref[...] * u).astype(BF16)

    def part(k):
        return pl.BlockSpec((None, s, tc), lambda i: (k, 0, i))

    return pl.pallas_call(
        body, name="conv_fwd", grid=(sw // tc,),
        in_specs=[part(0), part(1), part(2),
                  pl.BlockSpec((CONV_K, tc), lambda i: (0, i)), pl.BlockSpec((1, tc), lambda i: (0, i))],
        out_specs=pl.BlockSpec((s, tc), lambda i: (0, i)),
        out_shape=jax.ShapeDtypeStruct((s, sw), BF16),
        compiler_params=_cp(1),
    )(proj, proj, proj, conv_w, conv_b)


def _pool_counts(shape, window):
    t = lax.broadcasted_iota(jnp.int32, shape, 0)
    return jnp.minimum(t + 1, window).astype(F32)


def _pool_fwd(proj):
    _, s, sw = proj.shape
    gw = sw // len(POOL_WINDOWS)

    def body(v_ref, p_ref):
        for gi, window in enumerate(POOL_WINDOWS):
            @pl.when(pl.program_id(0) == gi)
            def _():
                v = v_ref[...]
                acc, k = v, 1
                while k < window:
                    acc = acc + _shift_down(acc, k)
                    k *= 2
                p_ref[...] = (acc / _pool_counts(v.shape, window) - v).astype(BF16)

    return pl.pallas_call(
        body, name="pool_fwd", grid=(len(POOL_WINDOWS),),
        in_specs=[pl.BlockSpec((None, s, gw), lambda g: (3, 0, g))],
        out_specs=pl.BlockSpec((s, gw), lambda g: (0, g)),
        out_shape=jax.ShapeDtypeStruct((s, sw), BF16),
        compiler_params=_cp(1),
    )(proj)


def _merge_fwd(z, wa, p, wpool, proj, b_gate2, pool_scale):
    s, sw = z.shape
    tn = wa.shape[2]
    d = tn * N_DEV
    gw = sw // len(POOL_WINDOWS)
    nq = sw // tn

    def body(z_ref, wa_ref, p_ref, wp_ref, ga_ref, gb_ref, bg_ref, sc_ref,
             m_ref, dya_ref, dyb_ref, dga_ref, dgb_ref, dsc_ref):
        for rs in _chunks(s, 512):
            ya = _dot(z_ref[rs, :], wa_ref[...], NN)
            yb = _dot(p_ref[rs, :], wp_ref[...], NN)
            sa = _sigmoid(ga_ref[rs, :] + bg_ref[0:1, :])
            sb = _sigmoid(gb_ref[rs, :] + bg_ref[1:2, :])
            sc = sc_ref[...]
            sb_yb = sb * yb
            m_ref[rs, :] = (sa * ya + sb_yb * sc).astype(BF16)
            dya_ref[rs, :] = sa.astype(BF16)
            dyb_ref[rs, :] = (sb * sc).astype(BF16)
            dga_ref[rs, :] = (ya * (sa * (1.0 - sa))).astype(BF16)
            dgb_ref[rs, :] = ((yb * sc) * (sb * (1.0 - sb))).astype(BF16)
            dsc_ref[rs, :] = sb_yb.astype(BF16)

    col = pl.BlockSpec((s, tn), lambda j: (0, j))
    out = jax.ShapeDtypeStruct((s, d), BF16)
    return pl.pallas_call(
        body, name="merge_fwd", grid=(N_DEV,),
        in_specs=[pl.BlockSpec((s, sw), lambda j: (0, 0)),
                  pl.BlockSpec((None, sw, tn), lambda j: (j, 0, 0)),
                  pl.BlockSpec((s, gw), lambda j: (0, j // 2)),
                  pl.BlockSpec((None, gw, tn), lambda j: (j // 2, 0, j % 2)),
                  pl.BlockSpec((None, s, tn), lambda j: (4 + j // nq, 0, j % nq)),
                  pl.BlockSpec((None, s, tn), lambda j: (6 + j // nq, 0, j % nq)),
                  pl.BlockSpec((2, tn), lambda j: (0, j)),
                  pl.BlockSpec((1, tn), lambda j: (0, j))],
        out_specs=[col] * 6,
        out_shape=[out] * 6,
        compiler_params=_cp(1),
    )(z, wa, p, wpool, proj, proj, b_gate2, pool_scale)


def _wo_fwd(merged, wo, x2d, g2):
    s, d = x2d.shape
    tm = min(256, s)

    def body(m_ref, wo_ref, x_ref, g_ref, x1_ref, h2_ref):
        x1 = x_ref[...] + _dot(m_ref[...], wo_ref[...], NN)
        x1_ref[...] = x1
        r = lax.rsqrt(jnp.mean(x1 * x1, axis=-1, keepdims=True) + EPS)
        h2_ref[...] = (x1 * r * g_ref[...]).astype(BF16)

    row = pl.BlockSpec((tm, d), lambda i: (i, 0))
    return pl.pallas_call(
        body, name="wo_fwd", grid=(s // tm,),
        in_specs=[row, pl.BlockSpec((d, d), lambda i: (0, 0)), row, pl.BlockSpec((1, d), lambda i: (0, 0))],
        out_specs=[row, row],
        out_shape=[jax.ShapeDtypeStruct((s, d), F32), jax.ShapeDtypeStruct((s, d), BF16)],
        compiler_params=_cp(1),
    )(merged, wo, x2d, g2)


def _ffn_up_act_fwd(h2, wg_g, wu_g):
    s, d = h2.shape
    f8 = wg_g.shape[2]
    th = min(1024, s)

    def body(h_ref, wg_ref, wu_ref, dadu_ref, dadg_ref, a_ref):
        i = pl.program_id(1)
        for rs in _chunks(th, 512):
            rows = pl.ds(pl.multiple_of(i * th + rs.start, rs.stop - rs.start), rs.stop - rs.start)
            a = h_ref[rows, :]
            g = _dot(a, wg_ref[...], NN)
            u = _dot(a, wu_ref[...], NN)
            sg = _sigmoid(g)
            silu = g * sg
            dadu_ref[rs, :] = silu.astype(BF16)
            dadg_ref[rs, :] = (u * (sg * (1.0 + g * (1.0 - sg)))).astype(BF16)
            a_ref[rs, :] = (silu * u).astype(BF16)

    wspec = pl.BlockSpec((None, d, f8), lambda j, i: (j, 0, 0))
    ospec = pl.BlockSpec((None, th, f8), lambda j, i: (j, i, 0))
    out = jax.ShapeDtypeStruct((N_DEV, s, f8), BF16)
    return pl.pallas_call(
        body, name="ffn_up_fwd", grid=(N_DEV, s // th),
        in_specs=[pl.BlockSpec((s, d), lambda j, i: (0, 0)), wspec, wspec],
        out_specs=[ospec, ospec, ospec], out_shape=[out, out, out],
        compiler_params=_cp(2),
    )(h2, wg_g, wu_g)


def _ffn_down_fwd(act, wd_g):
    _, s, f8 = act.shape
    d = wd_g.shape[2]
    tn = min(1024, d)

    def body(a_ref, wd_ref, o_ref):
        j = pl.program_id(1)

        @pl.when(j == 0)
        def _():
            o_ref[...] = jnp.zeros_like(o_ref)

        for rs in _chunks(s, 1024):
            o_ref[rs, :] += _dot(a_ref[rs, :], wd_ref[...], NN)

    return pl.pallas_call(
        body, name="ffn_down_fwd", grid=(d // tn, N_DEV),
        in_specs=[pl.BlockSpec((None, s, f8), lambda n, j: (j, 0, 0)),
                  pl.BlockSpec((None, f8, tn), lambda n, j: (j, 0, n))],
        out_specs=pl.BlockSpec((s, tn), lambda n, j: (0, n)),
        out_shape=jax.ShapeDtypeStruct((s, d), F32),
        compiler_params=_cp(2),
    )(act, wd_g)


def _loss_bwd(ffn_out, x1, target, final_g):
    s, d = x1.shape
    tm = min(256, s)

    def body(f_ref, x1_ref, t_ref, gf_ref, dxb_ref, dgf_ref, loss_ref):
        @pl.when(pl.program_id(0) == 0)
        def _():
            dgf_ref[...] = jnp.zeros_like(dgf_ref)
            loss_ref[...] = jnp.zeros_like(loss_ref)

        x2 = x1_ref[...] + f_ref[...]
        r = lax.rsqrt(jnp.mean(x2 * x2, axis=-1, keepdims=True) + EPS)
        nrm = x2 * r
        gf = gf_ref[...]
        err = nrm * gf - t_ref[...]
        loss_ref[...] += jnp.sum(err * err) * (0.5 / d)
        dy = err * (1.0 / d)
        dgf_ref[...] += jnp.sum(dy * nrm, axis=0, keepdims=True)
        dn = dy * gf
        dx = r * (dn - nrm * jnp.mean(dn * nrm, axis=-1, keepdims=True))
        dxb_ref[...] = dx.astype(BF16)

    row = pl.BlockSpec((tm, d), lambda i: (i, 0))
    vec = pl.BlockSpec((1, d), lambda i: (0, 0))
    return pl.pallas_call(
        body, name="loss_bwd", grid=(s // tm,),
        in_specs=[row, row, row, vec],
        out_specs=[row, vec, pl.BlockSpec((8, LANES), lambda i: (0, 0))],
        out_shape=[jax.ShapeDtypeStruct((s, d), BF16),
                   jax.ShapeDtypeStruct((1, d), F32), jax.ShapeDtypeStruct((8, LANES), F32)],
        compiler_params=_cp(1),
    )(ffn_out, x1, target, final_g)


def _ffn_gate_bwd(dx2b, wd_g, dadg, dadu):
    s, d = dx2b.shape
    f8 = dadg.shape[2]
    th = min(1024, s)

    def body(dx_ref, wd_ref, g_ref, u_ref, dg_ref, du_ref, da_ref):
        i = pl.program_id(1)
        chunks = _chunks(th, 256)

        def matmul(rs):
            rows = pl.ds(pl.multiple_of(i * th + rs.start, rs.stop - rs.start), rs.stop - rs.start)
            da_ref[rs, :] = _dot(dx_ref[rows, :], wd_ref[...], NT)

        matmul(chunks[0])
        for k, rs in enumerate(chunks):
            if k + 1 < len(chunks):
                matmul(chunks[k + 1])
            da = da_ref[rs, :]
            dg_ref[rs, :] = (da * g_ref[rs, :].astype(F32)).astype(BF16)
            du_ref[rs, :] = (da * u_ref[rs, :].astype(F32)).astype(BF16)

    aspec = pl.BlockSpec((None, th, f8), lambda j, i: (j, i, 0))
    out = jax.ShapeDtypeStruct((N_DEV, s, f8), BF16)
    return pl.pallas_call(
        body, name="ffn_act_bwd", grid=(N_DEV, s // th),
        in_specs=[pl.BlockSpec((s, d), lambda j, i: (0, 0)),
                  pl.BlockSpec((None, f8, d), lambda j, i: (j, 0, 0)), aspec, aspec],
        out_specs=[aspec, aspec], out_shape=[out, out],
        scratch_shapes=[pltpu.VMEM((th, f8), F32)],
        compiler_params=_cp(2),
    )(dx2b, wd_g, dadg, dadu)


def _wgrad_rows(a3, b, name, after=(), carry=()):
    _, s, k = a3.shape
    n = b.shape[1]
    nc = len(carry)
    c_in, c_out, c_shape, c_sems = _carry_specs(carry)

    def body(a_ref, b_ref, *rest):
        rest = rest[len(after):]
        o_ref = rest[nc]
        j = pl.program_id(0)
        _carry_run(j == 0, j == N_DEV - 1, rest[:nc], rest[nc + 1:2 * nc + 1], rest[2 * nc + 1:])
        o_ref[...] = _dot(a_ref[...], b_ref[...], TN).astype(BF16)

    outs = pl.pallas_call(
        body, name=name, grid=(N_DEV,),
        in_specs=[pl.BlockSpec((None, s, k), lambda j: (j, 0, 0)),
                  pl.BlockSpec((s, n), lambda j: (0, 0))] + _after_specs(after) + c_in,
        out_specs=[pl.BlockSpec((None, k, n), lambda j: (j, 0, 0))] + c_out,
        out_shape=[jax.ShapeDtypeStruct((N_DEV, k, n), BF16)] + c_shape,
        scratch_shapes=c_sems,
        compiler_params=_cp_carry(1, carry),
    )(a3, b, *after, *carry)
    return (outs[0], list(outs[1:])) if nc else outs[0]


def _wgrad_cols(a, b3, name, after=()):
    s, k = a.shape
    if b3.ndim == 2:
        n = b3.shape[1] // N_DEV
        b_spec = pl.BlockSpec((s, n), lambda j: (0, j))
    else:
        n = b3.shape[2]
        b_spec = pl.BlockSpec((None, s, n), lambda j: (j, 0, 0))

    def body(a_ref, b_ref, *rest):
        o_ref = rest[len(after)]
        o_ref[...] = _dot(a_ref[...], b_ref[...], TN).astype(BF16)

    return pl.pallas_call(
        body, name=name, grid=(N_DEV,),
        in_specs=[pl.BlockSpec((s, k), lambda j: (0, 0)), b_spec] + _after_specs(after),
        out_specs=pl.BlockSpec((None, k, n), lambda j: (j, 0, 0)),
        out_shape=jax.ShapeDtypeStruct((N_DEV, k, n), BF16),
        compiler_params=_cp(1),
    )(a, b3, *after)


def _input_grad(pairs, name, after=(), carry=()):
    s = pairs[0][0].shape[1]
    d = pairs[0][1].shape[1]
    tn = min(1024, d)
    npair = len(pairs)
    nc = len(carry)
    c_in, c_out, c_shape, c_sems = _carry_specs(carry)

    def body(*refs):
        ops = refs[:2 * npair]
        rest = refs[2 * npair + len(after):]
        o_ref, acc_ref = rest[nc], rest[-1]
        nh, j = pl.program_id(0), pl.program_id(1)
        _carry_run((nh == 0) & (j == 0), (nh == d // tn - 1) & (j == N_DEV - 1),
                   rest[:nc], rest[nc + 1:2 * nc + 1], rest[2 * nc + 1:-1])

        @pl.when(j == 0)
        def _():
            acc_ref[...] = jnp.zeros_like(acc_ref)

        for rs in _chunks(s, 1024):
            part = _dot(ops[0][rs, :], ops[1][...], NT)
            for q in range(1, npair):
                part = part + _dot(ops[2 * q][rs, :], ops[2 * q + 1][...], NT)
            acc_ref[rs, :] += part

        @pl.when(j == N_DEV - 1)
        def _():
            o_ref[...] = acc_ref[...].astype(BF16)

    in_specs, args = [], []
    for a3, w3 in pairs:
        k = a3.shape[2]
        in_specs += [pl.BlockSpec((None, s, k), lambda n, j: (j, 0, 0)),
                     pl.BlockSpec((None, tn, k), lambda n, j: (j, n, 0))]
        args += [a3, w3]
    outs = pl.pallas_call(
        body, name=name, grid=(d // tn, N_DEV),
        in_specs=in_specs + _after_specs(after) + c_in,
        out_specs=[pl.BlockSpec((s, tn), lambda n, j: (0, n))] + c_out,
        out_shape=[jax.ShapeDtypeStruct((s, d), BF16)] + c_shape,
        scratch_shapes=c_sems + [pltpu.VMEM((s, tn), F32)],
        compiler_params=_cp_carry(2, carry),
    )(*args, *after, *carry)
    return (outs[0], list(outs[1:])) if nc else outs[0]


def _rms_bwd(dh, xres, g, dres, name, with_bf16=True):
    s, d = xres.shape
    tm = min(256, s)

    def body(dh_ref, x_ref, g_ref, dres_ref, dx_ref, *rest):
        dg_ref = rest[-1]
        @pl.when(pl.program_id(0) == 0)
        def _():
            dg_ref[...] = jnp.zeros_like(dg_ref)

        xv = x_ref[...]
        dh_v = dh_ref[...].astype(F32)
        r = lax.rsqrt(jnp.mean(xv * xv, axis=-1, keepdims=True) + EPS)
        nrm = xv * r
        dg_ref[...] += jnp.sum(dh_v * nrm, axis=0, keepdims=True)
        dn = dh_v * g_ref[...]
        dx = dres_ref[...].astype(F32) + r * (dn - nrm * jnp.mean(dn * nrm, axis=-1, keepdims=True))
        dx_ref[...] = dx
        if with_bf16:
            rest[0][...] = dx.astype(BF16)

    row = pl.BlockSpec((tm, d), lambda i: (i, 0))
    vec = pl.BlockSpec((1, d), lambda i: (0, 0))
    copies = [jax.ShapeDtypeStruct((s, d), BF16)] if with_bf16 else []
    outs = pl.pallas_call(
        body, name=name, grid=(s // tm,),
        in_specs=[row, row, vec, row],
        out_specs=[row] + [row] * len(copies) + [vec],
        out_shape=[jax.ShapeDtypeStruct((s, d), F32)] + copies + [jax.ShapeDtypeStruct((1, d), F32)],
        compiler_params=_cp(1),
    )(dh, xres, g, dres)
    return (outs[0], outs[1], outs[2]) if with_bf16 else (outs[0], None, outs[1])


def _wgrad_full(a, b, name, after=(), carry=()):
    s, k = a.shape
    n = b.shape[1]
    tk = min(512, k)
    nc = len(carry)
    c_in, c_out, c_shape, c_sems = _carry_specs(carry)

    def body(a_ref, b_ref, *rest):
        rest = rest[len(after):]
        o_ref = rest[nc]
        j = pl.program_id(0)
        _carry_run(j == 0, j == k // tk - 1, rest[:nc], rest[nc + 1:2 * nc + 1], rest[2 * nc + 1:])
        o_ref[...] = _dot(a_ref[...], b_ref[...], TN).astype(BF16)

    outs = pl.pallas_call(
        body, name=name, grid=(k // tk,),
        in_specs=[pl.BlockSpec((s, tk), lambda j: (0, j)),
                  pl.BlockSpec((s, n), lambda j: (0, 0))] + _after_specs(after) + c_in,
        out_specs=[pl.BlockSpec((tk, n), lambda j: (j, 0))] + c_out,
        out_shape=[jax.ShapeDtypeStruct((k, n), BF16)] + c_shape,
        scratch_shapes=c_sems,
        compiler_params=_cp_carry(1, carry),
    )(a, b, *after, *carry)
    return (outs[0], list(outs[1:])) if nc else outs[0]


def _wgrad_pool(p, dyb, n_groups):
    s, sw = p.shape
    d = dyb.shape[1]
    gw, go = sw // n_groups, d // n_groups
    ts = min(512, s)
    ns = s // ts

    def body(a_ref, b_ref, o_ref, acc_ref):
        i = pl.program_id(1)

        @pl.when(i == 0)
        def _():
            acc_ref[...] = jnp.zeros_like(acc_ref)

        acc_ref[...] += _dot(a_ref[...], b_ref[...], TN)

        @pl.when(i == ns - 1)
        def _():
            o_ref[...] = acc_ref[...].astype(BF16)

    return pl.pallas_call(
        body, name="wgrad_pool", grid=(n_groups, ns),
        in_specs=[pl.BlockSpec((ts, gw), lambda g, i: (i, g)),
                  pl.BlockSpec((ts, go), lambda g, i: (i, g))],
        out_specs=pl.BlockSpec((None, gw, go), lambda g, i: (g, 0, 0)),
        out_shape=jax.ShapeDtypeStruct((n_groups, gw, go), BF16),
        scratch_shapes=[pltpu.VMEM((gw, go), F32)],
        compiler_params=_cp(2),
    )(p, dyb)


def _wo_bwd(dx1b, wo, factors, sw, after=()):
    s, d = dx1b.shape
    tn = d // N_DEV
    nq = sw // tn

    def body(dx_ref, wo_ref, fya_ref, fyb_ref, fga_ref, fgb_ref, fsc_ref, *rest):
        dya_ref, dyb_ref, dp_ref, dbg_ref, dsc_ref, dm_ref = rest[len(after):]
        dbg_ref[...] = jnp.zeros_like(dbg_ref)
        dsc_ref[...] = jnp.zeros_like(dsc_ref)
        for rs in _chunks(s, 1024):
            dm_ref[rs, :] = _dot(dx_ref[rs, :], wo_ref[...], NT)
        for rs in _chunks(s, 256):
            dm = dm_ref[rs, :]
            dya_ref[rs, :] = (dm * fya_ref[rs, :].astype(F32)).astype(BF16)
            dyb_ref[rs, :] = (dm * fyb_ref[rs, :].astype(F32)).astype(BF16)
            dsc_ref[...] += jnp.sum(dm * fsc_ref[rs, :].astype(F32), axis=0, keepdims=True)
            dga = dm * fga_ref[rs, :].astype(F32)
            dgb = dm * fgb_ref[rs, :].astype(F32)
            dp_ref[0, rs, :] = dga.astype(BF16)
            dp_ref[1, rs, :] = dgb.astype(BF16)
            dbg_ref[0:1, :] += jnp.sum(dga, axis=0, keepdims=True)
            dbg_ref[1:2, :] += jnp.sum(dgb, axis=0, keepdims=True)

    col = pl.BlockSpec((s, tn), lambda j: (0, j))
    out = jax.ShapeDtypeStruct((s, d), BF16)
    return pl.pallas_call(
        body, name="wo_bwd", grid=(N_DEV,),
        in_specs=[pl.BlockSpec((s, d), lambda j: (0, 0)),
                  pl.BlockSpec((tn, d), lambda j: (j, 0))] + [col] * 5 + _after_specs(after),
        out_specs=[col, col,
                   pl.BlockSpec((2, None, s, tn), lambda j: (1, j // nq, 0, j % nq)),
                   pl.BlockSpec((2, tn), lambda j: (0, j)),
                   pl.BlockSpec((1, tn), lambda j: (0, j))],
        out_shape=[out, out, jax.ShapeDtypeStruct((4, 2, s, sw), BF16),
                   jax.ShapeDtypeStruct((2, d), F32), jax.ShapeDtypeStruct((1, d), F32)],
        scratch_shapes=[pltpu.VMEM((s, tn), F32)],
        compiler_params=_cp(1),
    )(dx1b, wo, *factors, *after)


def _conv_bwd(dproj, dya, wa, proj, conv_w, conv_b):
    s, d = dya.shape
    sw, tn = wa.shape[1], wa.shape[2]
    tc = min(LANES, sw)

    def body(dproj_hbm, dya_ref, wa_ref, ba_ref, ca_ref, va_ref, cw_ref, cb_ref,
             dp_ref, dcw_ref, dcb_ref, dz_ref):
        del dproj_hbm
        for rs in _chunks(s, 512):
            part = _dot(dya_ref[rs, 0:tn], wa_ref[0], NT)
            for j in range(1, N_DEV):
                part = part + _dot(dya_ref[rs, j * tn:(j + 1) * tn], wa_ref[j], NT)
            dz_ref[rs, :] = part
        dz = dz_ref[...]
        ba, ca, va = ba_ref[...], ca_ref[...], va_ref[...]
        cv = ca * va
        cv1, cv2 = _shift_down(cv, 1), _shift_down(cv, 2)
        w0, w1, w2 = cw_ref[0:1, :], cw_ref[1:2, :], cw_ref[2:3, :]
        u = cb_ref[...] + w0 * cv2 + w1 * cv1 + w2 * cv
        du = dz * ba
        dp_ref[0] = (dz * u).astype(BF16)
        dcv = w2 * du + w1 * _shift_up(du, 1) + w0 * _shift_up(du, 2)
        dp_ref[1] = (dcv * va).astype(BF16)
        dp_ref[2] = (dcv * ca).astype(BF16)
        dcw_ref[0:1, :] = jnp.sum(du * cv2, axis=0, keepdims=True)
        dcw_ref[1:2, :] = jnp.sum(du * cv1, axis=0, keepdims=True)
        dcw_ref[2:3, :] = jnp.sum(du * cv, axis=0, keepdims=True)
        dcb_ref[...] = jnp.sum(du, axis=0, keepdims=True)

    def part(k):
        return pl.BlockSpec((None, s, tc), lambda i: (k, 0, i))

    return pl.pallas_call(
        body, name="conv_bwd", grid=(sw // tc,),
        in_specs=[pl.BlockSpec(memory_space=pl.ANY),
                  pl.BlockSpec((s, d), lambda i: (0, 0)),
                  pl.BlockSpec((N_DEV, tc, tn), lambda i: (0, i, 0)),
                  part(0), part(1), part(2),
                  pl.BlockSpec((CONV_K, tc), lambda i: (0, i)), pl.BlockSpec((1, tc), lambda i: (0, i))],
        out_specs=[pl.BlockSpec((3, s, tc), lambda i: (0, 0, i)),
                   pl.BlockSpec((CONV_K, tc), lambda i: (0, i)), pl.BlockSpec((1, tc), lambda i: (0, i))],
        out_shape=[jax.ShapeDtypeStruct(dproj.shape, BF16),
                   jax.ShapeDtypeStruct((CONV_K, sw), F32), jax.ShapeDtypeStruct((1, sw), F32)],
        scratch_shapes=[pltpu.VMEM((s, tc), F32)],
        input_output_aliases={0: 0},
        compiler_params=_cp(1),
    )(dproj, dya, wa, proj, proj, proj, conv_w, conv_b)


def _pool_bwd(dproj, dyb, wpool):
    s, d = dyb.shape
    n_groups, gw, go = wpool.shape

    def body(dproj_hbm, dyb_ref, wp_ref, dp_ref):
        del dproj_hbm
        for gi, window in enumerate(POOL_WINDOWS):
            @pl.when(pl.program_id(0) == gi)
            def _():
                dpool = _dot(dyb_ref[...], wp_ref[...], NT)
                acc, k = dpool / _pool_counts(dpool.shape, window), 1
                while k < window:
                    acc = acc + _shift_up(acc, k)
                    k *= 2
                dp_ref[...] = (acc - dpool).astype(BF16)

    return pl.pallas_call(
        body, name="pool_bwd", grid=(n_groups,),
        in_specs=[pl.BlockSpec(memory_space=pl.ANY),
                  pl.BlockSpec((s, go), lambda g: (0, g)),
                  pl.BlockSpec((None, gw, go), lambda g: (g, 0, 0))],
        out_specs=pl.BlockSpec((None, s, gw), lambda g: (3, 0, g)),
        out_shape=jax.ShapeDtypeStruct(dproj.shape, BF16),
        input_output_aliases={0: 0},
        compiler_params=_cp(1),
    )(dproj, dyb, wpool)


def _rows128(v):
    return v.reshape(-1, LANES)


def kernel(x, norm1_g, w_in, b_gate, conv_w, conv_b, w_a_out, w_pool, pool_scale, w_o, norm2_g, w_ffn_gate, w_ffn_up, w_ffn_down, final_g, loss_target, m_norm1_g, m_w_in, m_b_gate, m_conv_w, m_conv_b, m_w_a_out, m_w_pool, m_pool_scale, m_w_o, m_norm2_g, m_w_ffn_gate, m_w_ffn_up, m_w_ffn_down, m_final_g, v_norm1_g, v_w_in, v_b_gate, v_conv_w, v_conv_b, v_w_a_out, v_w_pool, v_pool_scale, v_w_o, v_norm2_g, v_w_ffn_gate, v_w_ffn_up, v_w_ffn_down, v_final_g):
    s, d = x.shape[1], x.shape[2]
    sw = w_in.shape[2]
    n_groups = w_pool.shape[1]
    gw = w_pool.shape[2]
    go = w_pool.shape[3] * N_DEV
    f8 = w_ffn_gate.shape[2]
    cws = conv_w.shape[2]
    assert sw == conv_w.shape[2] * N_DEV == gw * n_groups and go * n_groups == d and n_groups == len(POOL_WINDOWS)

    xi, yi, ci = _coords()
    me = 4 * xi + 2 * yi + ci
    my_chip = 2 * xi + yi

    x2d = x.reshape(s, d)
    target = loss_target.reshape(s, d)
    final_g2 = final_g.reshape(1, d)
    b_gate2 = b_gate.reshape(2, d)

    big_names = ["w_in", "w_a_out", "w_pool", "w_o", "w_ffn_gate", "w_ffn_up", "w_ffn_down"]
    big_w = [w_in, w_a_out, w_pool, w_o, w_ffn_gate, w_ffn_up, w_ffn_down]
    big_m = [m_w_in, m_w_a_out, m_w_pool, m_w_o, m_w_ffn_gate, m_w_ffn_up, m_w_ffn_down]
    big_v = [v_w_in, v_w_a_out, v_w_pool, v_w_o, v_w_ffn_gate, v_w_ffn_up, v_w_ffn_down]
    shapes2d = [(w.size // w.shape[-1], w.shape[-1]) for w in big_w]
    big_w2 = [w.reshape(sh) for w, sh in zip(big_w, shapes2d)]
    transposed = (4, 5)

    def view2d(t, a):
        t2 = t.reshape(shapes2d[a])
        return t2.T if a in transposed else t2

    def unview(o, a):
        return (o.T if a in transposed else o).reshape(big_w[a].shape)

    sb = [_cast_bf16(w, "cast_" + nm) for w, nm in zip(big_w2, big_names)]
    win_g, wa_g, wpool_g, wo_g = _allgather_big(sb[0:4], "allgather_mixer", COLLECTIVE_GATHER)
    wg_g, wu_g = _allgather_big(sb[4:6], "allgather_ffn_up", COLLECTIVE_GATHER)
    (wd_g,) = _allgather_big(sb[6:7], "allgather_ffn_down", COLLECTIVE_GATHER)
    convw_g = _allgather_small(jnp.pad(conv_w.reshape(CONV_K, cws), ((0, 8 - CONV_K), (0, 0))), "allgather_conv_w")
    conv_w_full = convw_g[:, :CONV_K, :].transpose(1, 0, 2).reshape(CONV_K, sw)
    wpool = wpool_g.reshape(N_DEV, n_groups, gw, go // N_DEV).transpose(1, 2, 0, 3).reshape(n_groups, gw, go)
    wo = wo_g.reshape(d, d)

    h = _rms_fwd(x2d, norm1_g)
    proj = _proj_fwd(h, win_g)
    z = _conv_fwd(proj, conv_w_full, conv_b)
    p = _pool_fwd(proj)
    merged, *merge_factors = _merge_fwd(z, wa_g, p, wpool, proj, b_gate2, pool_scale)
    x1, h2 = _wo_fwd(merged, wo, x2d, norm2_g)
    dadu, dadg, act = _ffn_up_act_fwd(h2, wg_g, wu_g)
    ffn_out = _ffn_down_fwd(act, wd_g)
    dx2b, d_final_g, loss_blk = _loss_bwd(ffn_out, x1, target, final_g2)

    other_chips = jnp.stack([2 * (1 - xi) + yi, 2 * xi + (1 - yi), 2 * (1 - xi) + (1 - yi)])
    others = jnp.concatenate([other_chips, 2 * other_chips + ci]).astype(jnp.int32)

    def partials(grads, recvs, names):
        if all(g.shape == grads[0].shape for g in grads):
            return list(_chip_partial(others, grads, recvs, "chip_partial_" + names[0]))
        return [_chip_partial(others, [g3], [r], "chip_partial_" + nm)[0] for g3, r, nm in zip(grads, recvs, names)]

    own = jnp.stack([me, my_chip]).astype(jnp.int32)

    def adam(idx, g3s, sibs, chipss):
        wmvs = [(view2d(big_w[a], a), view2d(big_m[a], a), view2d(big_v[a], a)) for a in idx]
        outs = _adam_big(own, wmvs, g3s, sibs, chipss, "adam_" + big_names[idx[0]])
        for a, o4 in zip(idx, outs):
            big_out[a] = [unview(o, a) for o in o4]

    big_out = [None] * len(big_names)
    dg_act, du_act = _ffn_gate_bwd(dx2b, wd_g, dadg, dadu)
    gw_gate = _wgrad_rows(dg_act, h2, "wgrad_ffn_gate")
    gw_up = _wgrad_rows(du_act, h2, "wgrad_ffn_up")
    gw_down, sib_gu = _wgrad_rows(act, dx2b, "wgrad_ffn_down", carry=[gw_gate, gw_up])
    ps_gu = partials([gw_gate, gw_up], sib_gu, ["w_ffn_gate", "w_ffn_up"])
    chips_gu = _exchange_chips(ps_gu, "rs_chips_ffn_up", COLLECTIVE_CHIPS)
    dh2, sib_down = _input_grad([(dg_act, wg_g), (du_act, wu_g)], "ffn_in_bwd", after=ps_gu, carry=[gw_down])
    ps_down = partials([gw_down], sib_down, ["w_ffn_down"])
    chips_down = _exchange_chips(ps_down, "rs_chips_ffn_down", COLLECTIVE_CHIPS)
    dx1, dx1b, d_norm2_g = _rms_bwd(dh2, x1, norm2_g, dx2b, "rms2_bwd")
    dya, dyb, dproj42, d_b_gate, d_pool_scale = _wo_bwd(dx1b, wo, merge_factors, sw, after=ps_down)
    dproj = dproj42.reshape(N_DEV, s, sw)
    dproj, d_conv_w, d_conv_b = _conv_bwd(dproj, dya, wa_g, proj, conv_w_full, conv_b)
    dproj = _pool_bwd(dproj, dyb, wpool)
    gw_in = _wgrad_cols(h, dproj, "wgrad_in")
    gw_o, sib_in = _wgrad_full(merged, dx1b, "wgrad_o", carry=[gw_in])
    ps_in = partials([gw_in], sib_in, ["w_in"])
    chips_in = _exchange_chips(ps_in, "rs_chips_w_in", COLLECTIVE_CHIPS)
    gw_a = _wgrad_cols(z, dya, "wgrad_a_out", after=ps_in)
    gw_pool = _wgrad_pool(p, dyb, n_groups)
    mix3 = [gw_a,
            gw_pool.reshape(n_groups, gw, N_DEV, go // N_DEV).transpose(2, 0, 1, 3).reshape(N_DEV, n_groups * gw, go // N_DEV),
            gw_o.reshape(N_DEV, d // N_DEV, d)]
    adam([4, 5, 6], [gw_gate, gw_up, gw_down], sib_gu + sib_down, chips_gu + chips_down)
    dh, sib_mix = _input_grad([(dproj, win_g)], "proj_in_bwd", after=[big_out[6][0]], carry=mix3)
    ps_mix = partials(mix3, sib_mix, ["w_a_out", "w_pool", "w_o"])
    chips_mix = _exchange_chips(ps_mix, "rs_chips_mixer", COLLECTIVE_CHIPS)
    grad_x, _, d_norm1_g = _rms_bwd(dh, x2d, norm1_g, dx1, "rms1_bwd", with_bf16=False)
    adam([0], [gw_in], sib_in, chips_in)
    for k in range(3):
        adam([1 + k], [mix3[k]], [sib_mix[k]], [chips_mix[k]])

    small_parts = [d_norm1_g, d_b_gate, d_conv_w, d_conv_b, d_pool_scale, d_norm2_g, d_final_g, loss_blk]
    rows = [v.size // LANES for v in small_parts]
    row0 = [sum(rows[:k]) for k in range(len(rows))]
    packed = jnp.concatenate([_rows128(v) for v in small_parts], axis=0)
    gathered = _allgather_small(packed, "allgather_small_grads")
    small_names = ["norm1_g", "b_gate", "conv_b", "pool_scale", "norm2_g", "final_g", "conv_w"]
    small_w = [norm1_g, b_gate, conv_b, pool_scale, norm2_g, final_g]
    small_m = [m_norm1_g, m_b_gate, m_conv_b, m_pool_scale, m_norm2_g, m_final_g]
    small_v = [v_norm1_g, v_b_gate, v_conv_b, v_pool_scale, v_norm2_g, v_final_g]
    finished = _small_finish(gathered, [tuple(_rows128(t) for t in wmv) for wmv in zip(small_w, small_m, small_v)],
                             [row0[k] for k in (0, 1, 3, 4, 5, 6)], [(row0[2], rows[2]), (row0[7], rows[7])])
    g_convw_full, loss_rows = finished[0], finished[1]
    loss = loss_rows[0, 0]
    small_out = [[t.reshape(w.shape) for t in finished[2 + 4 * k:6 + 4 * k]] for k, w in enumerate(small_w)]
    g_convw = lax.dynamic_slice(g_convw_full.reshape(CONV_K, sw), (0, me * cws), (CONV_K, cws))
    cw_delta, cw_m, cw_v = _adam_small(conv_w.reshape(CONV_K, cws), g_convw,
                                       m_conv_w.reshape(CONV_K, cws), v_conv_w.reshape(CONV_K, cws))
    small_out.append([t.reshape(conv_w.shape) for t in (g_convw, cw_delta, cw_m, cw_v)])

    order = ["norm1_g", "w_in", "b_gate", "conv_w", "conv_b", "w_a_out", "w_pool", "pool_scale", "w_o", "norm2_g",
             "w_ffn_gate", "w_ffn_up", "w_ffn_down", "final_g"]
    per_kind = [{}, {}, {}, {}]
    for a, nm in enumerate(big_names):
        for kind in range(4):
            per_kind[kind][nm] = big_out[a][kind]
    for k, nm in enumerate(small_names):
        for kind in range(4):
            per_kind[kind][nm] = small_out[k][kind]
    result = [loss, grad_x.reshape(x.shape)]
    for kind in range(4):
        result += [per_kind[kind][nm] for nm in order]
    return tuple(result)
```

```python
import jax
import jax.numpy as jnp
from jax import lax
from jax.experimental import pallas as pl
from jax.experimental.pallas import tpu as pltpu
from jax.experimental.pallas import tpu_sc as plsc

F32 = jnp.float32
BF16 = jnp.bfloat16
MESH = pl.DeviceIdType.MESH

N_DEV = 8
EPS = 1e-6
CONV_K = 3
POOL_WINDOWS = (2, 4, 8, 16)
ADAM_LR = 0.001
ADAM_B1 = 0.9
ADAM_B2 = 0.999
ADAM_EPS = 1e-08
ADAM_WD = 0.01
ADAM_STEP = 10

V7X_VMEM_LIMIT_BYTES = 56 * 1024 * 1024
LANES = 128

COLLECTIVE_GATHER = 1
COLLECTIVE_SIBLING = 2
COLLECTIVE_CHIPS = 3
SEQUENCER_COST_BYTES = 4 * 10**9

NN = ((1,), (0,))
NT = ((1,), (1,))
TN = ((0,), (0,))


def _dot(a, b, dims):
    return lax.dot_general(a, b, (dims, ((), ())), preferred_element_type=F32)


def _cp(n_axes):
    return pltpu.CompilerParams(dimension_semantics=("arbitrary",) * n_axes,
                                vmem_limit_bytes=V7X_VMEM_LIMIT_BYTES)


def _row_tile(rows, bytes_per_row, cap_bytes):
    best = None
    for t in range(16, rows + 1, 16):
        if rows % t == 0 and t * bytes_per_row <= cap_bytes:
            best = t
    return best if best is not None else rows


def _chunks(total, size):
    size = min(size, total)
    assert total % size == 0
    return [slice(r, r + size) for r in range(0, total, size)]


def _after_specs(after):
    return [pl.BlockSpec(memory_space=pl.ANY)] * len(after)


def _shift_down(v, k):
    row = lax.broadcasted_iota(jnp.int32, v.shape, 0)
    return jnp.where(row >= k, pltpu.roll(v, k, 0), 0.0)


def _shift_up(v, k):
    n = v.shape[0]
    row = lax.broadcasted_iota(jnp.int32, v.shape, 0)
    return jnp.where(row < n - k, pltpu.roll(v, n - k, 0), 0.0)


def _sigmoid(v):
    return jax.nn.sigmoid(v)


def _cast_bf16(w2d, name):
    rows, cols = w2d.shape
    tr = _row_tile(rows, cols * 4, 2 << 20)

    def body(i_ref, o_ref):
        o_ref[...] = i_ref[...].astype(BF16)

    return pl.pallas_call(
        body, name=name, grid=(rows // tr,),
        in_specs=[pl.BlockSpec((tr, cols), lambda i: (i, 0))],
        out_specs=pl.BlockSpec((tr, cols), lambda i: (i, 0)),
        out_shape=jax.ShapeDtypeStruct((rows, cols), BF16),
        compiler_params=_cp(1),
    )(w2d)


def _rms_fwd(x2d, g):
    s, d = x2d.shape
    tm = min(256, s)

    def body(x_ref, g_ref, h_ref):
        xv = x_ref[...]
        r = lax.rsqrt(jnp.mean(xv * xv, axis=-1, keepdims=True) + EPS)
        h_ref[...] = (xv * r * g_ref[...]).astype(BF16)

    return pl.pallas_call(
        body, name="rms1_fwd", grid=(s // tm,),
        in_specs=[pl.BlockSpec((tm, d), lambda i: (i, 0)), pl.BlockSpec((1, d), lambda i: (0, 0))],
        out_specs=pl.BlockSpec((tm, d), lambda i: (i, 0)),
        out_shape=jax.ShapeDtypeStruct((s, d), BF16),
        compiler_params=_cp(1),
    )(x2d, g)


def _coords():
    return lax.axis_index("x"), lax.axis_index("y"), lax.axis_index("c")


def _slot(p):
    return 4 * p[0] + 2 * p[1] + p[2]


def _handshake(peers):
    barrier = pltpu.get_barrier_semaphore()
    for peer in peers:
        pl.semaphore_signal(barrier, inc=1, device_id=peer, device_id_type=MESH)
    pl.semaphore_wait(barrier, len(peers))


def _sequencer_call(body, out_type, scratch_types, name, collective_id):
    return pl.kernel(
        body, out_type=out_type, name=name,
        mesh=plsc.ScalarSubcoreMesh(axis_name="seq", num_cores=1),
        scratch_types=scratch_types,
        cost_estimate=pl.CostEstimate(flops=0, transcendentals=0, bytes_accessed=SEQUENCER_COST_BYTES),
        compiler_params=pltpu.CompilerParams(collective_id=collective_id))


def _allgather_big(shards, name, collective_id, after=()):
    n = len(shards)

    def body(*refs):
        ins, outs = refs[:n], refs[n + len(after):2 * n + len(after)]
        send_sems, recv_sems, local_sems = refs[2 * n + len(after):]
        x, y, c = _coords()
        me, sibling = (x, y, c), (x, y, 1 - c)
        x_nbr, y_nbr, diag = (1 - x, y), (x, 1 - y), (1 - x, 1 - y)
        relay_from = (x + (1 - c) * (1 - 2 * x), y + c * (1 - 2 * y))
        relay_to = (x + c * (1 - 2 * x), y + (1 - c) * (1 - 2 * y))
        _handshake([sibling, (*x_nbr, c), (*y_nbr, c)])

        def copy(a, k, block, to, src=None):
            dst = outs[a].at[_slot(block)]
            return pltpu.make_async_remote_copy(
                src_ref=dst if src is None else src, dst_ref=dst,
                send_sem=send_sems.at[a, k], recv_sem=recv_sems.at[a, k],
                device_id=to, device_id_type=MESH)

        mine, sends = [], []
        for a in range(n):
            cp = pltpu.make_async_copy(ins[a], outs[a].at[_slot(me)], local_sems.at[a])
            cp.start()
            mine.append(cp)
            first = [copy(a, 0, me, sibling, src=ins[a]),
                     copy(a, 1, me, (*x_nbr, c), src=ins[a]),
                     copy(a, 2, me, (*y_nbr, c), src=ins[a])]
            for cp in first:
                cp.start()
            sends += first
        for a in range(n):
            copy(a, 1 + c, (*relay_from, c), me).wait_recv()
            passed = [copy(a, 3, (*relay_from, c), (*relay_to, c)), copy(a, 4 + c, (*relay_from, c), sibling)]
            for cp in passed:
                cp.start()
            copy(a, 2 - c, (*relay_to, c), me).wait_recv()
            cp = copy(a, 5 - c, (*relay_to, c), sibling)
            cp.start()
            passed.append(cp)
            copy(a, 3, (*diag, c), me).wait_recv()
            cp = copy(a, 6, (*diag, c), sibling)
            cp.start()
            sends += passed + [cp]
        for a in range(n):
            copy(a, 0, sibling, me).wait_recv()
            copy(a, 4, (*x_nbr, 1 - c), me).wait_recv()
            copy(a, 5, (*y_nbr, 1 - c), me).wait_recv()
            copy(a, 6, (*diag, 1 - c), me).wait_recv()
        for cp in sends:
            cp.wait_send()
        for cp in mine:
            cp.wait()

    return _sequencer_call(
        body, [jax.ShapeDtypeStruct((N_DEV,) + s.shape, s.dtype) for s in shards],
        [pltpu.SemaphoreType.DMA((n, 7)), pltpu.SemaphoreType.DMA((n, 7)), pltpu.SemaphoreType.DMA((n,))],
        name, collective_id)(*shards, *after)


def _sibling_copies(ins, recvs, send_sems, recv_sems):
    x, y, c = _coords()
    return [pltpu.make_async_remote_copy(
        src_ref=ins[a].at[2 * q + (1 - c)], dst_ref=recvs[a].at[q],
        send_sem=send_sems.at[a, q], recv_sem=recv_sems.at[a, q],
        device_id=(x, y, 1 - c), device_id_type=MESH) for a in range(len(ins)) for q in range(4)]


def _carry_specs(carry):
    any_spec = pl.BlockSpec(memory_space=pl.ANY)
    n = len(carry)
    sems = [pltpu.SemaphoreType.DMA((n, 4)), pltpu.SemaphoreType.DMA((n, 4))] if n else []
    return ([any_spec] * n, [any_spec] * n,
            [jax.ShapeDtypeStruct((4,) + g.shape[1:], g.dtype) for g in carry], sems)


def _carry_run(first, last, ins, recvs, sems):
    if not ins:
        return

    @pl.when(first)
    def _():
        x, y, c = _coords()
        _handshake([(x, y, 1 - c)])
        for cp in _sibling_copies(ins, recvs, *sems):
            cp.start()

    @pl.when(last)
    def _():
        copies = _sibling_copies(ins, recvs, *sems)
        for cp in copies:
            cp.wait_recv()
        for cp in copies:
            cp.wait_send()


def _cp_carry(n_axes, carry):
    if not carry:
        return _cp(n_axes)
    return pltpu.CompilerParams(dimension_semantics=("arbitrary",) * n_axes, vmem_limit_bytes=V7X_VMEM_LIMIT_BYTES,
                                collective_id=COLLECTIVE_SIBLING)


def _exchange_chips(psums, name, collective_id):
    n = len(psums)

    def body(*refs):
        ins, outs = refs[:n], refs[n:2 * n]
        send_sems, recv_sems = refs[2 * n:]
        x, y, c = _coords()
        chips = [(1 - x, y), (x, 1 - y), (1 - x, 1 - y)]
        _handshake([(*chip, c) for chip in chips])
        copies = []
        for a in range(n):
            for j, chip in enumerate(chips):
                cp = pltpu.make_async_remote_copy(
                    src_ref=ins[a].at[2 * chip[0] + chip[1]], dst_ref=outs[a].at[j],
                    send_sem=send_sems.at[a, j], recv_sem=recv_sems.at[a, j],
                    device_id=(*chip, c), device_id_type=MESH)
                cp.start()
                copies.append(cp)
        for cp in copies:
            cp.wait_recv()
        for cp in copies:
            cp.wait_send()

    return _sequencer_call(
        body, [jax.ShapeDtypeStruct((3,) + p.shape[1:], p.dtype) for p in psums],
        [pltpu.SemaphoreType.DMA((n, 3)), pltpu.SemaphoreType.DMA((n, 3))],
        name, collective_id)(*psums)


def _allgather_small(v2d, name):
    rows, cols = v2d.shape

    def body(v_ref, out_ref, send_sems, recv_sems):
        x, y, c = _coords()
        me = (x, y, c)
        out_ref[_slot(me)] = v_ref[...]
        peers = []
        for k in range(1, N_DEV):
            fx, fy, fc = (k >> 2) & 1, (k >> 1) & 1, k & 1
            peers.append(((1 - x) if fx else x, (1 - y) if fy else y, (1 - c) if fc else c))
        sends = []
        for k, peer in enumerate(peers):
            cp = pltpu.make_async_remote_copy(
                src_ref=v_ref, dst_ref=out_ref.at[_slot(me)],
                send_sem=send_sems.at[k], recv_sem=recv_sems.at[k],
                device_id=peer, device_id_type=MESH)
            cp.start()
            sends.append(cp)
        for k, peer in enumerate(peers):
            pltpu.make_async_remote_copy(
                src_ref=v_ref, dst_ref=out_ref.at[_slot(peer)],
                send_sem=send_sems.at[k], recv_sem=recv_sems.at[k],
                device_id=peer, device_id_type=MESH).wait_recv()
        for cp in sends:
            cp.wait_send()

    vmem = pl.BlockSpec(memory_space=pltpu.VMEM)
    return pl.pallas_call(
        body, name=name, in_specs=[vmem], out_specs=vmem,
        out_shape=jax.ShapeDtypeStruct((N_DEV, rows, cols), v2d.dtype),
        scratch_shapes=[pltpu.SemaphoreType.DMA((N_DEV - 1,)), pltpu.SemaphoreType.DMA((N_DEV - 1,))],
    )(v2d)


def _chip_partial(others, grads, recvs, name):
    n = len(grads)
    _, rows, cols = grads[0].shape
    tr = _row_tile(rows, cols * 2, (2 << 20) // n)

    def body(others_ref, *refs):
        for a in range(n):
            refs[2 * n + a][...] = (refs[a][...].astype(F32) + refs[n + a][...].astype(F32)).astype(BF16)

    return pl.pallas_call(
        body, name=name,
        grid_spec=pltpu.PrefetchScalarGridSpec(
            num_scalar_prefetch=1, grid=(3, rows // tr),
            in_specs=[pl.BlockSpec((None, tr, cols), lambda k, i, o: (o[3 + k], i, 0))] * n
            + [pl.BlockSpec((None, tr, cols), lambda k, i, o: (o[k], i, 0))] * n,
            out_specs=[pl.BlockSpec((None, tr, cols), lambda k, i, o: (o[k], i, 0))] * n),
        out_shape=[jax.ShapeDtypeStruct((4, rows, cols), BF16)] * n,
        compiler_params=_cp(2),
    )(others, *grads, *recvs)


def _adam_math(w, g, m, v):
    m = ADAM_B1 * m + (1.0 - ADAM_B1) * g
    v = ADAM_B2 * v + (1.0 - ADAM_B2) * (g * g)
    m_hat = m / (1.0 - ADAM_B1 ** ADAM_STEP)
    v_hat = v / (1.0 - ADAM_B2 ** ADAM_STEP)
    delta = -ADAM_LR * (m_hat / (jnp.sqrt(v_hat) + ADAM_EPS) + ADAM_WD * w)
    return delta, m, v


def _adam_big(own, wmvs, g3s, recv_sibs, recv_chipss, name):
    n = len(wmvs)
    rows, cols = wmvs[0][0].shape
    tr = _row_tile(rows, cols * 4, (2 << 20) // n)

    def body(own_ref, *refs):
        ins, outs = refs[:6 * n], refs[6 * n:]
        for a in range(n):
            w_ref, m_ref, v_ref, g_ref, rs_ref, rc_ref = ins[6 * a:6 * a + 6]
            g = g_ref[...].astype(F32) + rs_ref[...].astype(F32)
            g = g + rc_ref[0].astype(F32)
            g = g + rc_ref[1].astype(F32)
            g = g + rc_ref[2].astype(F32)
            delta, m_new, v_new = _adam_math(w_ref[...], g, m_ref[...], v_ref[...])
            outs[4 * a][...] = g
            outs[4 * a + 1][...] = delta
            outs[4 * a + 2][...] = m_new
            outs[4 * a + 3][...] = v_new

    blk = pl.BlockSpec((tr, cols), lambda i, o: (i, 0))
    per_shard = [blk, blk, blk,
                 pl.BlockSpec((None, tr, cols), lambda i, o: (o[0], i, 0)),
                 pl.BlockSpec((None, tr, cols), lambda i, o: (o[1], i, 0)),
                 pl.BlockSpec((3, tr, cols), lambda i, o: (0, i, 0))]
    out = jax.ShapeDtypeStruct((rows, cols), F32)
    args = [t for a in range(n) for t in (*wmvs[a], g3s[a], recv_sibs[a], recv_chipss[a])]
    outs = pl.pallas_call(
        body, name=name,
        grid_spec=pltpu.PrefetchScalarGridSpec(
            num_scalar_prefetch=1, grid=(rows // tr,),
            in_specs=per_shard * n, out_specs=[blk] * (4 * n)),
        out_shape=[out] * (4 * n),
        compiler_params=_cp(1),
    )(own, *args)
    return [outs[4 * a:4 * a + 4] for a in range(n)]


def _small_finish(gathered, params, row_offs, extra_rows):
    n = len(params)

    def body(g_ref, *refs):
        ins, outs = refs[:3 * n], refs[3 * n:]
        total = g_ref[0]
        for k in range(1, N_DEV):
            total = total + g_ref[k]
        for e, (r0, nr) in enumerate(extra_rows):
            outs[e][...] = total[r0:r0 + nr, :]
        for p in range(n):
            w_ref, m_ref, v_ref = ins[3 * p:3 * p + 3]
            g_out, d_out, m_out, v_out = outs[len(extra_rows) + 4 * p:len(extra_rows) + 4 * p + 4]
            g = total[row_offs[p]:row_offs[p] + w_ref.shape[0], :]
            delta, m_new, v_new = _adam_math(w_ref[...], g, m_ref[...], v_ref[...])
            g_out[...] = g
            d_out[...] = delta
            m_out[...] = m_new
            v_out[...] = v_new

    vmem = pl.BlockSpec(memory_space=pltpu.VMEM)
    out_shape = [jax.ShapeDtypeStruct((nr, LANES), F32) for _, nr in extra_rows]
    for w, _, _ in params:
        out_shape += [jax.ShapeDtypeStruct(w.shape, F32)] * 4
    flat = [t for wmv in params for t in wmv]
    return pl.pallas_call(body, name="small_finish", in_specs=[vmem] * (1 + len(flat)),
                          out_specs=[vmem] * len(out_shape), out_shape=out_shape)(gathered, *flat)


def _adam_small(w, g, m, v):
    def body(w_ref, g_ref, m_ref, v_ref, do_ref, mo_ref, vo_ref):
        delta, m_new, v_new = _adam_math(w_ref[...], g_ref[...], m_ref[...], v_ref[...])
        do_ref[...] = delta
        mo_ref[...] = m_new
        vo_ref[...] = v_new

    vmem = pl.BlockSpec(memory_space=pltpu.VMEM)
    out = jax.ShapeDtypeStruct(w.shape, F32)
    return pl.pallas_call(body, name="adam_small", in_specs=[vmem] * 4, out_specs=[vmem] * 3,
                          out_shape=[out, out, out])(w, g, m, v)


def _proj_fwd(h, win_g):
    s, d = h.shape
    sw = win_g.shape[2]
    tn = min(512, sw)
    nh = sw // tn

    def body(h_ref, w_ref, o_ref):
        for rs in _chunks(s, 512):
            o_ref[rs, :] = _dot(h_ref[rs, :], w_ref[...], NN)

    return pl.pallas_call(
        body, name="proj_fwd", grid=(N_DEV * nh,),
        in_specs=[pl.BlockSpec((s, d), lambda j: (0, 0)),
                  pl.BlockSpec((None, d, tn), lambda j: (j // nh, 0, j % nh))],
        out_specs=pl.BlockSpec((None, s, tn), lambda j: (j // nh, 0, j % nh)),
        out_shape=jax.ShapeDtypeStruct((N_DEV, s, sw), F32),
        compiler_params=_cp(1),
    )(h, win_g)


def _conv_fwd(proj, conv_w, conv_b):
    _, s, sw = proj.shape
    tc = min(LANES, sw)

    def body(ba_ref, ca_ref, va_ref, cw_ref, cb_ref, z_ref):
        cv = ca_ref[...] * va_ref[...]
        u = (cb_ref[...] + cw_ref[0:1, :] * _shift_down(cv, 2) + cw_ref[1:2, :] * _shift_down(cv, 1)
             + cw_ref[2:3, :] * cv)
        z_ref[...] = (ba_ref[...] * u).astype(BF16)

    def part(k):
        return pl.BlockSpec((None, s, tc), lambda i: (k, 0, i))

    return pl.pallas_call(
        body, name="conv_fwd", grid=(sw // tc,),
        in_specs=[part(0), part(1), part(2),
                  pl.BlockSpec((CONV_K, tc), lambda i: (0, i)), pl.BlockSpec((1, tc), lambda i: (0, i))],
        out_specs=pl.BlockSpec((s, tc), lambda i: (0, i)),
        out_shape=jax.ShapeDtypeStruct((s, sw), BF16),
        compiler_params=_cp(1),
    )(proj, proj, proj, conv_w, conv_b)


def _pool_counts(shape, window):
    t = lax.broadcasted_iota(jnp.int32, shape, 0)
    return jnp.minimum(t + 1, window).astype(F32)


def _pool_fwd(proj):
    _, s, sw = proj.shape
    gw = sw // len(POOL_WINDOWS)

    def body(v_ref, p_ref):
        for gi, window in enumerate(POOL_WINDOWS):
            @pl.when(pl.program_id(0) == gi)
            def _():
                v = v_ref[...]
                acc, k = v, 1
                while k < window:
                    acc = acc + _shift_down(acc, k)
                    k *= 2
                p_ref[...] = (acc / _pool_counts(v.shape, window) - v).astype(BF16)

    return pl.pallas_call(
        body, name="pool_fwd", grid=(len(POOL_WINDOWS),),
        in_specs=[pl.BlockSpec((None, s, gw), lambda g: (3, 0, g))],
        out_specs=pl.BlockSpec((s, gw), lambda g: (0, g)),
        out_shape=jax.ShapeDtypeStruct((s, sw), BF16),
        compiler_params=_cp(1),
    )(proj)


def _merge_fwd(z, wa, p, wpool, proj, b_gate2, pool_scale):
    s, sw = z.shape
    tn = wa.shape[2]
    d = tn * N_DEV
    gw = sw // len(POOL_WINDOWS)
    nq = sw // tn

    def body(z_ref, wa_ref, p_ref, wp_ref, ga_ref, gb_ref, bg_ref, sc_ref,
             m_ref, dya_ref, dyb_ref, dga_ref, dgb_ref, dsc_ref):
        for rs in _chunks(s, 512):
            ya = _dot(z_ref[rs, :], wa_ref[...], NN)
            yb = _dot(p_ref[rs, :], wp_ref[...], NN)
            sa = _sigmoid(ga_ref[rs, :] + bg_ref[0:1, :])
            sb = _sigmoid(gb_ref[rs, :] + bg_ref[1:2, :])
            sc = sc_ref[...]
            sb_yb = sb * yb
            m_ref[rs, :] = (sa * ya + sb_yb * sc).astype(BF16)
            dya_ref[rs, :] = sa.astype(BF16)
            dyb_ref[rs, :] = (sb * sc).astype(BF16)
            dga_ref[rs, :] = (ya * (sa * (1.0 - sa))).astype(BF16)
            dgb_ref[rs, :] = ((yb * sc) * (sb * (1.0 - sb))).astype(BF16)
            dsc_ref[rs, :] = sb_yb.astype(BF16)

    col = pl.BlockSpec((s, tn), lambda j: (0, j))
    out = jax.ShapeDtypeStruct((s, d), BF16)
    return pl.pallas_call(
        body, name="merge_fwd", grid=(N_DEV,),
        in_specs=[pl.BlockSpec((s, sw), lambda j: (0, 0)),
                  pl.BlockSpec((None, sw, tn), lambda j: (j, 0, 0)),
                  pl.BlockSpec((s, gw), lambda j: (0, j // 2)),
                  pl.BlockSpec((None, gw, tn), lambda j: (j // 2, 0, j % 2)),
                  pl.BlockSpec((None, s, tn), lambda j: (4 + j // nq, 0, j % nq)),
                  pl.BlockSpec((None, s, tn), lambda j: (6 + j // nq, 0, j % nq)),
                  pl.BlockSpec((2, tn), lambda j: (0, j)),
                  pl.BlockSpec((1, tn), lambda j: (0, j))],
        out_specs=[col] * 6,
        out_shape=[out] * 6,
        compiler_params=_cp(1),
    )(z, wa, p, wpool, proj, proj, b_gate2, pool_scale)


def _wo_fwd(merged, wo, x2d, g2):
    s, d = x2d.shape
    tm = min(256, s)

    def body(m_ref, wo_ref, x_ref, g_ref, x1_ref, h2_ref):
        x1 = x_ref[...] + _dot(m_ref[...], wo_ref[...], NN)
        x1_ref[...] = x1
        r = lax.rsqrt(jnp.mean(x1 * x1, axis=-1, keepdims=True) + EPS)
        h2_ref[...] = (x1 * r * g_ref[...]).astype(BF16)

    row = pl.BlockSpec((tm, d), lambda i: (i, 0))
    return pl.pallas_call(
        body, name="wo_fwd", grid=(s // tm,),
        in_specs=[row, pl.BlockSpec((d, d), lambda i: (0, 0)), row, pl.BlockSpec((1, d), lambda i: (0, 0))],
        out_specs=[row, row],
        out_shape=[jax.ShapeDtypeStruct((s, d), F32), jax.ShapeDtypeStruct((s, d), BF16)],
        compiler_params=_cp(1),
    )(merged, wo, x2d, g2)


def _ffn_up_act_fwd(h2, wg_g, wu_g):
    s, d = h2.shape
    f8 = wg_g.shape[2]
    th = min(1024, s)

    def body(h_ref, wg_ref, wu_ref, dadu_ref, dadg_ref, a_ref):
        i = pl.program_id(1)
        for rs in _chunks(th, 512):
            rows = pl.ds(pl.multiple_of(i * th + rs.start, rs.stop - rs.start), rs.stop - rs.start)
            a = h_ref[rows, :]
            g = _dot(a, wg_ref[...], NN)
            u = _dot(a, wu_ref[...], NN)
            sg = _sigmoid(g)
            silu = g * sg
            dadu_ref[rs, :] = silu.astype(BF16)
            dadg_ref[rs, :] = (u * (sg * (1.0 + g * (1.0 - sg)))).astype(BF16)
            a_ref[rs, :] = (silu * u).astype(BF16)

    wspec = pl.BlockSpec((None, d, f8), lambda j, i: (j, 0, 0))
    ospec = pl.BlockSpec((None, th, f8), lambda j, i: (j, i, 0))
    out = jax.ShapeDtypeStruct((N_DEV, s, f8), BF16)
    return pl.pallas_call(
        body, name="ffn_up_fwd", grid=(N_DEV, s // th),
        in_specs=[pl.BlockSpec((s, d), lambda j, i: (0, 0)), wspec, wspec],
        out_specs=[ospec, ospec, ospec], out_shape=[out, out, out],
        compiler_params=_cp(2),
    )(h2, wg_g, wu_g)


def _ffn_down_fwd(act, wd_g):
    _, s, f8 = act.shape
    d = wd_g.shape[2]
    tn = min(1024, d)

    def body(a_ref, wd_ref, o_ref):
        j = pl.program_id(1)

        @pl.when(j == 0)
        def _():
            o_ref[...] = jnp.zeros_like(o_ref)

        for rs in _chunks(s, 1024):
            o_ref[rs, :] += _dot(a_ref[rs, :], wd_ref[...], NN)

    return pl.pallas_call(
        body, name="ffn_down_fwd", grid=(d // tn, N_DEV),
        in_specs=[pl.BlockSpec((None, s, f8), lambda n, j: (j, 0, 0)),
                  pl.BlockSpec((None, f8, tn), lambda n, j: (j, 0, n))],
        out_specs=pl.BlockSpec((s, tn), lambda n, j: (0, n)),
        out_shape=jax.ShapeDtypeStruct((s, d), F32),
        compiler_params=_cp(2),
    )(act, wd_g)


def _loss_bwd(ffn_out, x1, target, final_g):
    s, d = x1.shape
    tm = min(256, s)

    def body(f_ref, x1_ref, t_ref, gf_ref, dxb_ref, dgf_ref, loss_ref):
        @pl.when(pl.program_id(0) == 0)
        def _():
            dgf_ref[...] = jnp.zeros_like(dgf_ref)
            loss_ref[...] = jnp.zeros_like(loss_ref)

        x2 = x1_ref[...] + f_ref[...]
        r = lax.rsqrt(jnp.mean(x2 * x2, axis=-1, keepdims=True) + EPS)
        nrm = x2 * r
        gf = gf_ref[...]
        err = nrm * gf - t_ref[...]
        loss_ref[...] += jnp.sum(err * err) * (0.5 / d)
        dy = err * (1.0 / d)
        dgf_ref[...] += jnp.sum(dy * nrm, axis=0, keepdims=True)
        dn = dy * gf
        dx = r * (dn - nrm * jnp.mean(dn * nrm, axis=-1, keepdims=True))
        dxb_ref[...] = dx.astype(BF16)

    row = pl.BlockSpec((tm, d), lambda i: (i, 0))
    vec = pl.BlockSpec((1, d), lambda i: (0, 0))
    return pl.pallas_call(
        body, name="loss_bwd", grid=(s // tm,),
        in_specs=[row, row, row, vec],
        out_specs=[row, vec, pl.BlockSpec((8, LANES), lambda i: (0, 0))],
        out_shape=[jax.ShapeDtypeStruct((s, d), BF16),
                   jax.ShapeDtypeStruct((1, d), F32), jax.ShapeDtypeStruct((8, LANES), F32)],
        compiler_params=_cp(1),
    )(ffn_out, x1, target, final_g)


def _ffn_gate_bwd(dx2b, wd_g, dadg, dadu):
    s, d = dx2b.shape
    f8 = dadg.shape[2]
    th = min(1024, s)

    def body(dx_ref, wd_ref, g_ref, u_ref, dg_ref, du_ref, da_ref):
        i = pl.program_id(1)
        chunks = _chunks(th, 256)

        def matmul(rs):
            rows = pl.ds(pl.multiple_of(i * th + rs.start, rs.stop - rs.start), rs.stop - rs.start)
            da_ref[rs, :] = _dot(dx_ref[rows, :], wd_ref[...], NT)

        matmul(chunks[0])
        for k, rs in enumerate(chunks):
            if k + 1 < len(chunks):
                matmul(chunks[k + 1])
            da = da_ref[rs, :].astype(BF16)
            dg_ref[rs, :] = da * g_ref[rs, :]
            du_ref[rs, :] = da * u_ref[rs, :]

    aspec = pl.BlockSpec((None, th, f8), lambda j, i: (j, i, 0))
    out = jax.ShapeDtypeStruct((N_DEV, s, f8), BF16)
    return pl.pallas_call(
        body, name="ffn_act_bwd", grid=(N_DEV, s // th),
        in_specs=[pl.BlockSpec((s, d), lambda j, i: (0, 0)),
                  pl.BlockSpec((None, f8, d), lambda j, i: (j, 0, 0)), aspec, aspec],
        out_specs=[aspec, aspec], out_shape=[out, out],
        scratch_shapes=[pltpu.VMEM((th, f8), F32)],
        compiler_params=_cp(2),
    )(dx2b, wd_g, dadg, dadu)


def _wgrad_rows(a3, b, name, after=(), carry=()):
    _, s, k = a3.shape
    n = b.shape[1]
    nc = len(carry)
    c_in, c_out, c_shape, c_sems = _carry_specs(carry)

    def body(a_ref, b_ref, *rest):
        rest = rest[len(after):]
        o_ref = rest[nc]
        j = pl.program_id(0)
        _carry_run(j == 0, j == N_DEV - 1, rest[:nc], rest[nc + 1:2 * nc + 1], rest[2 * nc + 1:])
        o_ref[...] = _dot(a_ref[...], b_ref[...], TN).astype(BF16)

    outs = pl.pallas_call(
        body, name=name, grid=(N_DEV,),
        in_specs=[pl.BlockSpec((None, s, k), lambda j: (j, 0, 0)),
                  pl.BlockSpec((s, n), lambda j: (0, 0))] + _after_specs(after) + c_in,
        out_specs=[pl.BlockSpec((None, k, n), lambda j: (j, 0, 0))] + c_out,
        out_shape=[jax.ShapeDtypeStruct((N_DEV, k, n), BF16)] + c_shape,
        scratch_shapes=c_sems,
        compiler_params=_cp_carry(1, carry),
    )(a3, b, *after, *carry)
    return (outs[0], list(outs[1:])) if nc else outs[0]


def _wgrad_cols(a, b3, name, after=()):
    s, k = a.shape
    if b3.ndim == 2:
        n = b3.shape[1] // N_DEV
        b_spec = pl.BlockSpec((s, n), lambda j: (0, j))
    else:
        n = b3.shape[2]
        b_spec = pl.BlockSpec((None, s, n), lambda j: (j, 0, 0))

    def body(a_ref, b_ref, *rest):
        o_ref = rest[len(after)]
        o_ref[...] = _dot(a_ref[...], b_ref[...], TN).astype(BF16)

    return pl.pallas_call(
        body, name=name, grid=(N_DEV,),
        in_specs=[pl.BlockSpec((s, k), lambda j: (0, 0)), b_spec] + _after_specs(after),
        out_specs=pl.BlockSpec((None, k, n), lambda j: (j, 0, 0)),
        out_shape=jax.ShapeDtypeStruct((N_DEV, k, n), BF16),
        compiler_params=_cp(1),
    )(a, b3, *after)


def _input_grad(pairs, name, after=(), carry=()):
    s = pairs[0][0].shape[1]
    d = pairs[0][1].shape[1]
    tn = min(1024, d)
    npair = len(pairs)
    nc = len(carry)
    c_in, c_out, c_shape, c_sems = _carry_specs(carry)

    def body(*refs):
        ops = refs[:2 * npair]
        rest = refs[2 * npair + len(after):]
        o_ref, acc_ref = rest[nc], rest[-1]
        nh, j = pl.program_id(0), pl.program_id(1)
        _carry_run((nh == 0) & (j == 0), (nh == d // tn - 1) & (j == N_DEV - 1),
                   rest[:nc], rest[nc + 1:2 * nc + 1], rest[2 * nc + 1:-1])

        @pl.when(j == 0)
        def _():
            acc_ref[...] = jnp.zeros_like(acc_ref)

        for rs in _chunks(s, 1024):
            part = _dot(ops[0][rs, :], ops[1][...], NT)
            for q in range(1, npair):
                part = part + _dot(ops[2 * q][rs, :], ops[2 * q + 1][...], NT)
            acc_ref[rs, :] += part

        @pl.when(j == N_DEV - 1)
        def _():
            o_ref[...] = acc_ref[...].astype(BF16)

    in_specs, args = [], []
    for a3, w3 in pairs:
        k = a3.shape[2]
        in_specs += [pl.BlockSpec((None, s, k), lambda n, j: (j, 0, 0)),
                     pl.BlockSpec((None, tn, k), lambda n, j: (j, n, 0))]
        args += [a3, w3]
    outs = pl.pallas_call(
        body, name=name, grid=(d // tn, N_DEV),
        in_specs=in_specs + _after_specs(after) + c_in,
        out_specs=[pl.BlockSpec((s, tn), lambda n, j: (0, n))] + c_out,
        out_shape=[jax.ShapeDtypeStruct((s, d), BF16)] + c_shape,
        scratch_shapes=c_sems + [pltpu.VMEM((s, tn), F32)],
        compiler_params=_cp_carry(2, carry),
    )(*args, *after, *carry)
    return (outs[0], list(outs[1:])) if nc else outs[0]


def _rms_bwd(dh, xres, g, dres, name, with_bf16=True):
    s, d = xres.shape
    tm = min(256, s)

    def body(dh_ref, x_ref, g_ref, dres_ref, dx_ref, *rest):
        dg_ref = rest[-1]
        @pl.when(pl.program_id(0) == 0)
        def _():
            dg_ref[...] = jnp.zeros_like(dg_ref)

        xv = x_ref[...]
        dh_v = dh_ref[...].astype(F32)
        r = lax.rsqrt(jnp.mean(xv * xv, axis=-1, keepdims=True) + EPS)
        nrm = xv * r
        dg_ref[...] += jnp.sum(dh_v * nrm, axis=0, keepdims=True)
        dn = dh_v * g_ref[...]
        dx = dres_ref[...].astype(F32) + r * (dn - nrm * jnp.mean(dn * nrm, axis=-1, keepdims=True))
        dx_ref[...] = dx
        if with_bf16:
            rest[0][...] = dx.astype(BF16)

    row = pl.BlockSpec((tm, d), lambda i: (i, 0))
    vec = pl.BlockSpec((1, d), lambda i: (0, 0))
    copies = [jax.ShapeDtypeStruct((s, d), BF16)] if with_bf16 else []
    outs = pl.pallas_call(
        body, name=name, grid=(s // tm,),
        in_specs=[row, row, vec, row],
        out_specs=[row] + [row] * len(copies) + [vec],
        out_shape=[jax.ShapeDtypeStruct((s, d), F32)] + copies + [jax.ShapeDtypeStruct((1, d), F32)],
        compiler_params=_cp(1),
    )(dh, xres, g, dres)
    return (outs[0], outs[1], outs[2]) if with_bf16 else (outs[0], None, outs[1])


def _wgrad_full(a, b, name, after=(), carry=()):
    s, k = a.shape
    n = b.shape[1]
    tk = min(512, k)
    nc = len(carry)
    c_in, c_out, c_shape, c_sems = _carry_specs(carry)

    def body(a_ref, b_ref, *rest):
        rest = rest[len(after):]
        o_ref = rest[nc]
        j = pl.program_id(0)
        _carry_run(j == 0, j == k // tk - 1, rest[:nc], rest[nc + 1:2 * nc + 1], rest[2 * nc + 1:])
        o_ref[...] = _dot(a_ref[...], b_ref[...], TN).astype(BF16)

    outs = pl.pallas_call(
        body, name=name, grid=(k // tk,),
        in_specs=[pl.BlockSpec((s, tk), lambda j: (0, j)),
                  pl.BlockSpec((s, n), lambda j: (0, 0))] + _after_specs(after) + c_in,
        out_specs=[pl.BlockSpec((tk, n), lambda j: (j, 0))] + c_out,
        out_shape=[jax.ShapeDtypeStruct((k, n), BF16)] + c_shape,
        scratch_shapes=c_sems,
        compiler_params=_cp_carry(1, carry),
    )(a, b, *after, *carry)
    return (outs[0], list(outs[1:])) if nc else outs[0]


def _wgrad_pool(p, dyb, n_groups):
    s, sw = p.shape
    d = dyb.shape[1]
    gw, go = sw // n_groups, d // n_groups
    ts = min(512, s)
    ns = s // ts

    def body(a_ref, b_ref, o_ref, acc_ref):
        i = pl.program_id(1)

        @pl.when(i == 0)
        def _():
            acc_ref[...] = jnp.zeros_like(acc_ref)

        acc_ref[...] += _dot(a_ref[...], b_ref[...], TN)

        @pl.when(i == ns - 1)
        def _():
            o_ref[...] = acc_ref[...].astype(BF16)

    return pl.pallas_call(
        body, name="wgrad_pool", grid=(n_groups, ns),
        in_specs=[pl.BlockSpec((ts, gw), lambda g, i: (i, g)),
                  pl.BlockSpec((ts, go), lambda g, i: (i, g))],
        out_specs=pl.BlockSpec((None, gw, go), lambda g, i: (g, 0, 0)),
        out_shape=jax.ShapeDtypeStruct((n_groups, gw, go), BF16),
        scratch_shapes=[pltpu.VMEM((gw, go), F32)],
        compiler_params=_cp(2),
    )(p, dyb)


def _wo_bwd(dx1b, wo, factors, sw, after=()):
    s, d = dx1b.shape
    tn = d // N_DEV
    nq = sw // tn

    def body(dx_ref, wo_ref, fya_ref, fyb_ref, fga_ref, fgb_ref, fsc_ref, *rest):
        dya_ref, dyb_ref, dp_ref, dbg_ref, dsc_ref, dm_ref = rest[len(after):]
        dbg_ref[...] = jnp.zeros_like(dbg_ref)
        dsc_ref[...] = jnp.zeros_like(dsc_ref)
        for rs in _chunks(s, 1024):
            dm_ref[rs, :] = _dot(dx_ref[rs, :], wo_ref[...], NT)
        for rs in _chunks(s, 256):
            dm = dm_ref[rs, :]
            dya_ref[rs, :] = (dm * fya_ref[rs, :].astype(F32)).astype(BF16)
            dyb_ref[rs, :] = (dm * fyb_ref[rs, :].astype(F32)).astype(BF16)
            dsc_ref[...] += jnp.sum(dm * fsc_ref[rs, :].astype(F32), axis=0, keepdims=True)
            dga = dm * fga_ref[rs, :].astype(F32)
            dgb = dm * fgb_ref[rs, :].astype(F32)
            dp_ref[0, rs, :] = dga.astype(BF16)
            dp_ref[1, rs, :] = dgb.astype(BF16)
            dbg_ref[0:1, :] += jnp.sum(dga, axis=0, keepdims=True)
            dbg_ref[1:2, :] += jnp.sum(dgb, axis=0, keepdims=True)

    col = pl.BlockSpec((s, tn), lambda j: (0, j))
    out = jax.ShapeDtypeStruct((s, d), BF16)
    return pl.pallas_call(
        body, name="wo_bwd", grid=(N_DEV,),
        in_specs=[pl.BlockSpec((s, d), lambda j: (0, 0)),
                  pl.BlockSpec((tn, d), lambda j: (j, 0))] + [col] * 5 + _after_specs(after),
        out_specs=[col, col,
                   pl.BlockSpec((2, None, s, tn), lambda j: (1, j // nq, 0, j % nq)),
                   pl.BlockSpec((2, tn), lambda j: (0, j)),
                   pl.BlockSpec((1, tn), lambda j: (0, j))],
        out_shape=[out, out, jax.ShapeDtypeStruct((4, 2, s, sw), BF16),
                   jax.ShapeDtypeStruct((2, d), F32), jax.ShapeDtypeStruct((1, d), F32)],
        scratch_shapes=[pltpu.VMEM((s, tn), F32)],
        compiler_params=_cp(1),
    )(dx1b, wo, *factors, *after)


def _conv_bwd(dproj, dya, wa, proj, conv_w, conv_b):
    s, d = dya.shape
    sw, tn = wa.shape[1], wa.shape[2]
    tc = min(LANES, sw)

    def body(dproj_hbm, dya_ref, wa_ref, ba_ref, ca_ref, va_ref, cw_ref, cb_ref,
             dp_ref, dcw_ref, dcb_ref, dz_ref):
        del dproj_hbm
        for rs in _chunks(s, 512):
            part = _dot(dya_ref[rs, 0:tn], wa_ref[0], NT)
            for j in range(1, N_DEV):
                part = part + _dot(dya_ref[rs, j * tn:(j + 1) * tn], wa_ref[j], NT)
            dz_ref[rs, :] = part
        dz = dz_ref[...]
        ba, ca, va = ba_ref[...], ca_ref[...], va_ref[...]
        cv = ca * va
        cv1, cv2 = _shift_down(cv, 1), _shift_down(cv, 2)
        w0, w1, w2 = cw_ref[0:1, :], cw_ref[1:2, :], cw_ref[2:3, :]
        u = cb_ref[...] + w0 * cv2 + w1 * cv1 + w2 * cv
        du = dz * ba
        dp_ref[0] = (dz * u).astype(BF16)
        dcv = w2 * du + w1 * _shift_up(du, 1) + w0 * _shift_up(du, 2)
        dp_ref[1] = (dcv * va).astype(BF16)
        dp_ref[2] = (dcv * ca).astype(BF16)
        dcw_ref[0:1, :] = jnp.sum(du * cv2, axis=0, keepdims=True)
        dcw_ref[1:2, :] = jnp.sum(du * cv1, axis=0, keepdims=True)
        dcw_ref[2:3, :] = jnp.sum(du * cv, axis=0, keepdims=True)
        dcb_ref[...] = jnp.sum(du, axis=0, keepdims=True)

    def part(k):
        return pl.BlockSpec((None, s, tc), lambda i: (k, 0, i))

    return pl.pallas_call(
        body, name="conv_bwd", grid=(sw // tc,),
        in_specs=[pl.BlockSpec(memory_space=pl.ANY),
                  pl.BlockSpec((s, d), lambda i: (0, 0)),
                  pl.BlockSpec((N_DEV, tc, tn), lambda i: (0, i, 0)),
                  part(0), part(1), part(2),
                  pl.BlockSpec((CONV_K, tc), lambda i: (0, i)), pl.BlockSpec((1, tc), lambda i: (0, i))],
        out_specs=[pl.BlockSpec((3, s, tc), lambda i: (0, 0, i)),
                   pl.BlockSpec((CONV_K, tc), lambda i: (0, i)), pl.BlockSpec((1, tc), lambda i: (0, i))],
        out_shape=[jax.ShapeDtypeStruct(dproj.shape, BF16),
                   jax.ShapeDtypeStruct((CONV_K, sw), F32), jax.ShapeDtypeStruct((1, sw), F32)],
        scratch_shapes=[pltpu.VMEM((s, tc), F32)],
        input_output_aliases={0: 0},
        compiler_params=_cp(1),
    )(dproj, dya, wa, proj, proj, proj, conv_w, conv_b)


def _pool_bwd(dproj, dyb, wpool):
    s, d = dyb.shape
    n_groups, gw, go = wpool.shape

    def body(dproj_hbm, dyb_ref, wp_ref, dp_ref):
        del dproj_hbm
        for gi, window in enumerate(POOL_WINDOWS):
            @pl.when(pl.program_id(0) == gi)
            def _():
                dpool = _dot(dyb_ref[...], wp_ref[...], NT)
                acc, k = dpool / _pool_counts(dpool.shape, window), 1
                while k < window:
                    acc = acc + _shift_up(acc, k)
                    k *= 2
                dp_ref[...] = (acc - dpool).astype(BF16)

    return pl.pallas_call(
        body, name="pool_bwd", grid=(n_groups,),
        in_specs=[pl.BlockSpec(memory_space=pl.ANY),
                  pl.BlockSpec((s, go), lambda g: (0, g)),
                  pl.BlockSpec((None, gw, go), lambda g: (g, 0, 0))],
        out_specs=pl.BlockSpec((None, s, gw), lambda g: (3, 0, g)),
        out_shape=jax.ShapeDtypeStruct(dproj.shape, BF16),
        input_output_aliases={0: 0},
        compiler_params=_cp(1),
    )(dproj, dyb, wpool)


def _rows128(v):
    return v.reshape(-1, LANES)


def kernel(x, norm1_g, w_in, b_gate, conv_w, conv_b, w_a_out, w_pool, pool_scale, w_o, norm2_g, w_ffn_gate, w_ffn_up, w_ffn_down, final_g, loss_target, m_norm1_g, m_w_in, m_b_gate, m_conv_w, m_conv_b, m_w_a_out, m_w_pool, m_pool_scale, m_w_o, m_norm2_g, m_w_ffn_gate, m_w_ffn_up, m_w_ffn_down, m_final_g, v_norm1_g, v_w_in, v_b_gate, v_conv_w, v_conv_b, v_w_a_out, v_w_pool, v_pool_scale, v_w_o, v_norm2_g, v_w_ffn_gate, v_w_ffn_up, v_w_ffn_down, v_final_g):
    s, d = x.shape[1], x.shape[2]
    sw = w_in.shape[2]
    n_groups = w_pool.shape[1]
    gw = w_pool.shape[2]
    go = w_pool.shape[3] * N_DEV
    f8 = w_ffn_gate.shape[2]
    cws = conv_w.shape[2]
    assert sw == conv_w.shape[2] * N_DEV == gw * n_groups and go * n_groups == d and n_groups == len(POOL_WINDOWS)

    xi, yi, ci = _coords()
    me = 4 * xi + 2 * yi + ci
    my_chip = 2 * xi + yi

    x2d = x.reshape(s, d)
    target = loss_target.reshape(s, d)
    final_g2 = final_g.reshape(1, d)
    b_gate2 = b_gate.reshape(2, d)

    big_names = ["w_in", "w_a_out", "w_pool", "w_o", "w_ffn_gate", "w_ffn_up", "w_ffn_down"]
    big_w = [w_in, w_a_out, w_pool, w_o, w_ffn_gate, w_ffn_up, w_ffn_down]
    big_m = [m_w_in, m_w_a_out, m_w_pool, m_w_o, m_w_ffn_gate, m_w_ffn_up, m_w_ffn_down]
    big_v = [v_w_in, v_w_a_out, v_w_pool, v_w_o, v_w_ffn_gate, v_w_ffn_up, v_w_ffn_down]
    shapes2d = [(w.size // w.shape[-1], w.shape[-1]) for w in big_w]
    big_w2 = [w.reshape(sh) for w, sh in zip(big_w, shapes2d)]
    transposed = (4, 5)

    def view2d(t, a):
        t2 = t.reshape(shapes2d[a])
        return t2.T if a in transposed else t2

    def unview(o, a):
        return (o.T if a in transposed else o).reshape(big_w[a].shape)

    sb = [_cast_bf16(w, "cast_" + nm) for w, nm in zip(big_w2, big_names)]
    win_g, wa_g, wpool_g, wo_g = _allgather_big(sb[0:4], "allgather_mixer", COLLECTIVE_GATHER)
    wg_g, wu_g = _allgather_big(sb[4:6], "allgather_ffn_up", COLLECTIVE_GATHER)
    (wd_g,) = _allgather_big(sb[6:7], "allgather_ffn_down", COLLECTIVE_GATHER)
    convw_g = _allgather_small(jnp.pad(conv_w.reshape(CONV_K, cws), ((0, 8 - CONV_K), (0, 0))), "allgather_conv_w")
    conv_w_full = convw_g[:, :CONV_K, :].transpose(1, 0, 2).reshape(CONV_K, sw)
    wpool = wpool_g.reshape(N_DEV, n_groups, gw, go // N_DEV).transpose(1, 2, 0, 3).reshape(n_groups, gw, go)
    wo = wo_g.reshape(d, d)

    h = _rms_fwd(x2d, norm1_g)
    proj = _proj_fwd(h, win_g)
    z = _conv_fwd(proj, conv_w_full, conv_b)
    p = _pool_fwd(proj)
    merged, *merge_factors = _merge_fwd(z, wa_g, p, wpool, proj, b_gate2, pool_scale)
    x1, h2 = _wo_fwd(merged, wo, x2d, norm2_g)
    dadu, dadg, act = _ffn_up_act_fwd(h2, wg_g, wu_g)
    ffn_out = _ffn_down_fwd(act, wd_g)
    dx2b, d_final_g, loss_blk = _loss_bwd(ffn_out, x1, target, final_g2)

    other_chips = jnp.stack([2 * (1 - xi) + yi, 2 * xi + (1 - yi), 2 * (1 - xi) + (1 - yi)])
    others = jnp.concatenate([other_chips, 2 * other_chips + ci]).astype(jnp.int32)

    def partials(grads, recvs, names):
        if all(g.shape == grads[0].shape for g in grads):
            return list(_chip_partial(others, grads, recvs, "chip_partial_" + names[0]))
        return [_chip_partial(others, [g3], [r], "chip_partial_" + nm)[0] for g3, r, nm in zip(grads, recvs, names)]

    own = jnp.stack([me, my_chip]).astype(jnp.int32)

    def adam(idx, g3s, sibs, chipss):
        wmvs = [(view2d(big_w[a], a), view2d(big_m[a], a), view2d(big_v[a], a)) for a in idx]
        outs = _adam_big(own, wmvs, g3s, sibs, chipss, "adam_" + big_names[idx[0]])
        for a, o4 in zip(idx, outs):
            big_out[a] = [unview(o, a) for o in o4]

    big_out = [None] * len(big_names)
    dg_act, du_act = _ffn_gate_bwd(dx2b, wd_g, dadg, dadu)
    gw_gate = _wgrad_rows(dg_act, h2, "wgrad_ffn_gate")
    gw_up = _wgrad_rows(du_act, h2, "wgrad_ffn_up")
    gw_down, sib_gu = _wgrad_rows(act, dx2b, "wgrad_ffn_down", carry=[gw_gate, gw_up])
    ps_gu = partials([gw_gate, gw_up], sib_gu, ["w_ffn_gate", "w_ffn_up"])
    chips_gu = _exchange_chips(ps_gu, "rs_chips_ffn_up", COLLECTIVE_CHIPS)
    dh2, sib_down = _input_grad([(dg_act, wg_g), (du_act, wu_g)], "ffn_in_bwd", after=ps_gu, carry=[gw_down])
    ps_down = partials([gw_down], sib_down, ["w_ffn_down"])
    chips_down = _exchange_chips(ps_down, "rs_chips_ffn_down", COLLECTIVE_CHIPS)
    dx1, dx1b, d_norm2_g = _rms_bwd(dh2, x1, norm2_g, dx2b, "rms2_bwd")
    dya, dyb, dproj42, d_b_gate, d_pool_scale = _wo_bwd(dx1b, wo, merge_factors, sw, after=ps_down)
    dproj = dproj42.reshape(N_DEV, s, sw)
    dproj, d_conv_w, d_conv_b = _conv_bwd(dproj, dya, wa_g, proj, conv_w_full, conv_b)
    dproj = _pool_bwd(dproj, dyb, wpool)
    gw_in = _wgrad_cols(h, dproj, "wgrad_in")
    gw_o, sib_in = _wgrad_full(merged, dx1b, "wgrad_o", carry=[gw_in])
    ps_in = partials([gw_in], sib_in, ["w_in"])
    chips_in = _exchange_chips(ps_in, "rs_chips_w_in", COLLECTIVE_CHIPS)
    gw_a = _wgrad_cols(z, dya, "wgrad_a_out", after=ps_in)
    gw_pool = _wgrad_pool(p, dyb, n_groups)
    mix3 = [gw_a,
            gw_pool.reshape(n_groups, gw, N_DEV, go // N_DEV).transpose(2, 0, 1, 3).reshape(N_DEV, n_groups * gw, go // N_DEV),
            gw_o.reshape(N_DEV, d // N_DEV, d)]
    adam([4, 5, 6], [gw_gate, gw_up, gw_down], sib_gu + sib_down, chips_gu + chips_down)
    dh, sib_mix = _input_grad([(dproj, win_g)], "proj_in_bwd", after=[big_out[6][0]], carry=mix3)
    ps_mix = partials(mix3, sib_mix, ["w_a_out", "w_pool", "w_o"])
    chips_mix = _exchange_chips(ps_mix, "rs_chips_mixer", COLLECTIVE_CHIPS)
    grad_x, _, d_norm1_g = _rms_bwd(dh, x2d, norm1_g, dx1, "rms1_bwd", with_bf16=False)
    adam([0], [gw_in], sib_in, chips_in)
    for k in range(3):
        adam([1 + k], [mix3[k]], [sib_mix[k]], [chips_mix[k]])

    small_parts = [d_norm1_g, d_b_gate, d_conv_w, d_conv_b, d_pool_scale, d_norm2_g, d_final_g, loss_blk]
    rows = [v.size // LANES for v in small_parts]
    row0 = [sum(rows[:k]) for k in range(len(rows))]
    packed = jnp.concatenate([_rows128(v) for v in small_parts], axis=0)
    gathered = _allgather_small(packed, "allgather_small_grads")
    small_names = ["norm1_g", "b_gate", "conv_b", "pool_scale", "norm2_g", "final_g", "conv_w"]
    small_w = [norm1_g, b_gate, conv_b, pool_scale, norm2_g, final_g]
    small_m = [m_norm1_g, m_b_gate, m_conv_b, m_pool_scale, m_norm2_g, m_final_g]
    small_v = [v_norm1_g, v_b_gate, v_conv_b, v_pool_scale, v_norm2_g, v_final_g]
    finished = _small_finish(gathered, [tuple(_rows128(t) for t in wmv) for wmv in zip(small_w, small_m, small_v)],
                             [row0[k] for k in (0, 1, 3, 4, 5, 6)], [(row0[2], rows[2]), (row0[7], rows[7])])
    g_convw_full, loss_rows = finished[0], finished[1]
    loss = loss_rows[0, 0]
    small_out = [[t.reshape(w.shape) for t in finished[2 + 4 * k:6 + 4 * k]] for k, w in enumerate(small_w)]
    g_convw = lax.dynamic_slice(g_convw_full.reshape(CONV_K, sw), (0, me * cws), (CONV_K, cws))
    cw_delta, cw_m, cw_v = _adam_small(conv_w.reshape(CONV_K, cws), g_convw,
                                       m_conv_w.reshape(CONV_K, cws), v_conv_w.reshape(CONV_K, cws))
    small_out.append([t.reshape(conv_w.shape) for t in (g_convw, cw_delta, cw_m, cw_v)])

    order = ["norm1_g", "w_in", "b_gate", "conv_w", "conv_b", "w_a_out", "w_pool", "pool_scale", "w_o", "norm2_g",
             "w_ffn_gate", "w_ffn_up", "w_ffn_down", "final_g"]
    per_kind = [{}, {}, {}, {}]
    for a, nm in enumerate(big_names):
        for kind in range(4):
            per_kind[kind][nm] = big_out[a][kind]
    for k, nm in enumerate(small_names):
        for kind in range(4):
            per_kind[kind][nm] = small_out[k][kind]
    result = [loss, grad_x.reshape(x.shape)]
    for kind in range(4):
        result += [per_kind[kind][nm] for nm in order]
    return tuple(result)
```

```python
import jax
import jax.numpy as jnp
from jax import lax
from jax.experimental import pallas as pl
from jax.experimental.pallas import tpu as pltpu
from jax.experimental.pallas import tpu_sc as plsc

F32 = jnp.float32
BF16 = jnp.bfloat16
MESH = pl.DeviceIdType.MESH

N_DEV = 8
EPS = 1e-6
CONV_K = 3
POOL_WINDOWS = (2, 4, 8, 16)
ADAM_LR = 0.001
ADAM_B1 = 0.9
ADAM_B2 = 0.999
ADAM_EPS = 1e-08
ADAM_WD = 0.01
ADAM_STEP = 10

V7X_VMEM_LIMIT_BYTES = 56 * 1024 * 1024
LANES = 128

COLLECTIVE_GATHER = 1
COLLECTIVE_SIBLING = 2
COLLECTIVE_CHIPS = 3
SEQUENCER_COST_BYTES = 4 * 10**9

NN = ((1,), (0,))
NT = ((1,), (1,))
TN = ((0,), (0,))


def _dot(a, b, dims):
    return lax.dot_general(a, b, (dims, ((), ())), preferred_element_type=F32)


def _cp(n_axes):
    return pltpu.CompilerParams(dimension_semantics=("arbitrary",) * n_axes,
                                vmem_limit_bytes=V7X_VMEM_LIMIT_BYTES)


def _row_tile(rows, bytes_per_row, cap_bytes):
    best = None
    for t in range(16, rows + 1, 16):
        if rows % t == 0 and t * bytes_per_row <= cap_bytes:
            best = t
    return best if best is not None else rows


def _chunks(total, size):
    size = min(size, total)
    assert total % size == 0
    return [slice(r, r + size) for r in range(0, total, size)]


def _after_specs(after):
    return [pl.BlockSpec(memory_space=pl.ANY)] * len(after)


def _shift_down(v, k):
    row = lax.broadcasted_iota(jnp.int32, v.shape, 0)
    return jnp.where(row >= k, pltpu.roll(v, k, 0), 0.0)


def _shift_up(v, k):
    n = v.shape[0]
    row = lax.broadcasted_iota(jnp.int32, v.shape, 0)
    return jnp.where(row < n - k, pltpu.roll(v, n - k, 0), 0.0)


def _sigmoid(v):
    return jax.nn.sigmoid(v)


def _cast_bf16(w2d, name):
    rows, cols = w2d.shape
    tr = _row_tile(rows, cols * 4, 2 << 20)

    def body(i_ref, o_ref):
        o_ref[...] = i_ref[...].astype(BF16)

    return pl.pallas_call(
        body, name=name, grid=(rows // tr,),
        in_specs=[pl.BlockSpec((tr, cols), lambda i: (i, 0))],
        out_specs=pl.BlockSpec((tr, cols), lambda i: (i, 0)),
        out_shape=jax.ShapeDtypeStruct((rows, cols), BF16),
        compiler_params=_cp(1),
    )(w2d)


def _rms_fwd(x2d, g):
    s, d = x2d.shape
    tm = min(256, s)

    def body(x_ref, g_ref, h_ref):
        xv = x_ref[...]
        r = lax.rsqrt(jnp.mean(xv * xv, axis=-1, keepdims=True) + EPS)
        h_ref[...] = (xv * r * g_ref[...]).astype(BF16)

    return pl.pallas_call(
        body, name="rms1_fwd", grid=(s // tm,),
        in_specs=[pl.BlockSpec((tm, d), lambda i: (i, 0)), pl.BlockSpec((1, d), lambda i: (0, 0))],
        out_specs=pl.BlockSpec((tm, d), lambda i: (i, 0)),
        out_shape=jax.ShapeDtypeStruct((s, d), BF16),
        compiler_params=_cp(1),
    )(x2d, g)


def _coords():
    return lax.axis_index("x"), lax.axis_index("y"), lax.axis_index("c")


def _slot(p):
    return 4 * p[0] + 2 * p[1] + p[2]


def _handshake(peers):
    barrier = pltpu.get_barrier_semaphore()
    for peer in peers:
        pl.semaphore_signal(barrier, inc=1, device_id=peer, device_id_type=MESH)
    pl.semaphore_wait(barrier, len(peers))


def _sequencer_call(body, out_type, scratch_types, name, collective_id):
    return pl.kernel(
        body, out_type=out_type, name=name,
        mesh=plsc.ScalarSubcoreMesh(axis_name="seq", num_cores=1),
        scratch_types=scratch_types,
        cost_estimate=pl.CostEstimate(flops=0, transcendentals=0, bytes_accessed=SEQUENCER_COST_BYTES),
        compiler_params=pltpu.CompilerParams(collective_id=collective_id))


def _allgather_big(shards, name, collective_id, after=()):
    n = len(shards)

    def body(*refs):
        ins, outs = refs[:n], refs[n + len(after):2 * n + len(after)]
        send_sems, recv_sems, local_sems = refs[2 * n + len(after):]
        x, y, c = _coords()
        me, sibling = (x, y, c), (x, y, 1 - c)
        x_nbr, y_nbr, diag = (1 - x, y), (x, 1 - y), (1 - x, 1 - y)
        relay_from = (x + (1 - c) * (1 - 2 * x), y + c * (1 - 2 * y))
        relay_to = (x + c * (1 - 2 * x), y + (1 - c) * (1 - 2 * y))
        _handshake([sibling, (*x_nbr, c), (*y_nbr, c)])

        def copy(a, k, block, to, src=None):
            dst = outs[a].at[_slot(block)]
            return pltpu.make_async_remote_copy(
                src_ref=dst if src is None else src, dst_ref=dst,
                send_sem=send_sems.at[a, k], recv_sem=recv_sems.at[a, k],
                device_id=to, device_id_type=MESH)

        mine, sends = [], []
        for a in range(n):
            cp = pltpu.make_async_copy(ins[a], outs[a].at[_slot(me)], local_sems.at[a])
            cp.start()
            mine.append(cp)
            first = [copy(a, 0, me, sibling, src=ins[a]),
                     copy(a, 1, me, (*x_nbr, c), src=ins[a]),
                     copy(a, 2, me, (*y_nbr, c), src=ins[a])]
            for cp in first:
                cp.start()
            sends += first
        for a in range(n):
            copy(a, 1 + c, (*relay_from, c), me).wait_recv()
            passed = [copy(a, 3, (*relay_from, c), (*relay_to, c)), copy(a, 4 + c, (*relay_from, c), sibling)]
            for cp in passed:
                cp.start()
            copy(a, 2 - c, (*relay_to, c), me).wait_recv()
            cp = copy(a, 5 - c, (*relay_to, c), sibling)
            cp.start()
            passed.append(cp)
            copy(a, 3, (*diag, c), me).wait_recv()
            cp = copy(a, 6, (*diag, c), sibling)
            cp.start()
            sends += passed + [cp]
        for a in range(n):
            copy(a, 0, sibling, me).wait_recv()
            copy(a, 4, (*x_nbr, 1 - c), me).wait_recv()
            copy(a, 5, (*y_nbr, 1 - c), me).wait_recv()
            copy(a, 6, (*diag, 1 - c), me).wait_recv()
        for cp in sends:
            cp.wait_send()
        for cp in mine:
            cp.wait()

    return _sequencer_call(
        body, [jax.ShapeDtypeStruct((N_DEV,) + s.shape, s.dtype) for s in shards],
        [pltpu.SemaphoreType.DMA((n, 7)), pltpu.SemaphoreType.DMA((n, 7)), pltpu.SemaphoreType.DMA((n,))],
        name, collective_id)(*shards, *after)


def _sibling_copies(ins, recvs, send_sems, recv_sems):
    x, y, c = _coords()
    return [pltpu.make_async_remote_copy(
        src_ref=ins[a].at[2 * q + (1 - c)], dst_ref=recvs[a].at[q],
        send_sem=send_sems.at[a, q], recv_sem=recv_sems.at[a, q],
        device_id=(x, y, 1 - c), device_id_type=MESH) for a in range(len(ins)) for q in range(4)]


def _carry_specs(carry):
    any_spec = pl.BlockSpec(memory_space=pl.ANY)
    n = len(carry)
    sems = [pltpu.SemaphoreType.DMA((n, 4)), pltpu.SemaphoreType.DMA((n, 4))] if n else []
    return ([any_spec] * n, [any_spec] * n,
            [jax.ShapeDtypeStruct((4,) + g.shape[1:], g.dtype) for g in carry], sems)


def _carry_run(first, last, ins, recvs, sems):
    if not ins:
        return

    @pl.when(first)
    def _():
        x, y, c = _coords()
        _handshake([(x, y, 1 - c)])
        for cp in _sibling_copies(ins, recvs, *sems):
            cp.start()

    @pl.when(last)
    def _():
        copies = _sibling_copies(ins, recvs, *sems)
        for cp in copies:
            cp.wait_recv()
        for cp in copies:
            cp.wait_send()


def _cp_carry(n_axes, carry):
    if not carry:
        return _cp(n_axes)
    return pltpu.CompilerParams(dimension_semantics=("arbitrary",) * n_axes, vmem_limit_bytes=V7X_VMEM_LIMIT_BYTES,
                                collective_id=COLLECTIVE_SIBLING)


def _exchange_chips(psums, name, collective_id):
    n = len(psums)

    def body(*refs):
        ins, outs = refs[:n], refs[n:2 * n]
        send_sems, recv_sems = refs[2 * n:]
        x, y, c = _coords()
        chips = [(1 - x, y), (x, 1 - y), (1 - x, 1 - y)]
        _handshake([(*chip, c) for chip in chips])
        copies = []
        for a in range(n):
            for j, chip in enumerate(chips):
                cp = pltpu.make_async_remote_copy(
                    src_ref=ins[a].at[2 * chip[0] + chip[1]], dst_ref=outs[a].at[j],
                    send_sem=send_sems.at[a, j], recv_sem=recv_sems.at[a, j],
                    device_id=(*chip, c), device_id_type=MESH)
                cp.start()
                copies.append(cp)
        for cp in copies:
            cp.wait_recv()
        for cp in copies:
            cp.wait_send()

    return _sequencer_call(
        body, [jax.ShapeDtypeStruct((3,) + p.shape[1:], p.dtype) for p in psums],
        [pltpu.SemaphoreType.DMA((n, 3)), pltpu.SemaphoreType.DMA((n, 3))],
        name, collective_id)(*psums)


def _allgather_small(v2d, name):
    rows, cols = v2d.shape

    def body(v_ref, out_ref, send_sems, recv_sems):
        x, y, c = _coords()
        me = (x, y, c)
        out_ref[_slot(me)] = v_ref[...]
        peers = []
        for k in range(1, N_DEV):
            fx, fy, fc = (k >> 2) & 1, (k >> 1) & 1, k & 1
            peers.append(((1 - x) if fx else x, (1 - y) if fy else y, (1 - c) if fc else c))
        sends = []
        for k, peer in enumerate(peers):
            cp = pltpu.make_async_remote_copy(
                src_ref=v_ref, dst_ref=out_ref.at[_slot(me)],
                send_sem=send_sems.at[k], recv_sem=recv_sems.at[k],
                device_id=peer, device_id_type=MESH)
            cp.start()
            sends.append(cp)
        for k, peer in enumerate(peers):
            pltpu.make_async_remote_copy(
                src_ref=v_ref, dst_ref=out_ref.at[_slot(peer)],
                send_sem=send_sems.at[k], recv_sem=recv_sems.at[k],
                device_id=peer, device_id_type=MESH).wait_recv()
        for cp in sends:
            cp.wait_send()

    vmem = pl.BlockSpec(memory_space=pltpu.VMEM)
    return pl.pallas_call(
        body, name=name, in_specs=[vmem], out_specs=vmem,
        out_shape=jax.ShapeDtypeStruct((N_DEV, rows, cols), v2d.dtype),
        scratch_shapes=[pltpu.SemaphoreType.DMA((N_DEV - 1,)), pltpu.SemaphoreType.DMA((N_DEV - 1,))],
    )(v2d)


def _chip_partial(others, grads, recvs, name):
    n = len(grads)
    _, rows, cols = grads[0].shape
    tr = _row_tile(rows, cols * 2, (2 << 20) // n)

    def body(others_ref, *refs):
        for a in range(n):
            refs[2 * n + a][...] = (refs[a][...].astype(F32) + refs[n + a][...].astype(F32)).astype(BF16)

    return pl.pallas_call(
        body, name=name,
        grid_spec=pltpu.PrefetchScalarGridSpec(
            num_scalar_prefetch=1, grid=(3, rows // tr),
            in_specs=[pl.BlockSpec((None, tr, cols), lambda k, i, o: (o[3 + k], i, 0))] * n
            + [pl.BlockSpec((None, tr, cols), lambda k, i, o: (o[k], i, 0))] * n,
            out_specs=[pl.BlockSpec((None, tr, cols), lambda k, i, o: (o[k], i, 0))] * n),
        out_shape=[jax.ShapeDtypeStruct((4, rows, cols), BF16)] * n,
        compiler_params=_cp(2),
    )(others, *grads, *recvs)


def _adam_math(w, g, m, v):
    m = ADAM_B1 * m + (1.0 - ADAM_B1) * g
    v = ADAM_B2 * v + (1.0 - ADAM_B2) * (g * g)
    m_hat = m / (1.0 - ADAM_B1 ** ADAM_STEP)
    v_hat = v / (1.0 - ADAM_B2 ** ADAM_STEP)
    delta = -ADAM_LR * (m_hat / (jnp.sqrt(v_hat) + ADAM_EPS) + ADAM_WD * w)
    return delta, m, v


def _adam_big(own, wmvs, g3s, recv_sibs, recv_chipss, name):
    n = len(wmvs)
    rows, cols = wmvs[0][0].shape
    tr = _row_tile(rows, cols * 4, (2 << 20) // n)

    def body(own_ref, *refs):
        ins, outs = refs[:6 * n], refs[6 * n:]
        for a in range(n):
            w_ref, m_ref, v_ref, g_ref, rs_ref, rc_ref = ins[6 * a:6 * a + 6]
            g = g_ref[...].astype(F32) + rs_ref[...].astype(F32)
            g = g + rc_ref[0].astype(F32)
            g = g + rc_ref[1].astype(F32)
            g = g + rc_ref[2].astype(F32)
            delta, m_new, v_new = _adam_math(w_ref[...], g, m_ref[...], v_ref[...])
            outs[4 * a][...] = g
            outs[4 * a + 1][...] = delta
            outs[4 * a + 2][...] = m_new
            outs[4 * a + 3][...] = v_new

    blk = pl.BlockSpec((tr, cols), lambda i, o: (i, 0))
    per_shard = [blk, blk, blk,
                 pl.BlockSpec((None, tr, cols), lambda i, o: (o[0], i, 0)),
                 pl.BlockSpec((None, tr, cols), lambda i, o: (o[1], i, 0)),
                 pl.BlockSpec((3, tr, cols), lambda i, o: (0, i, 0))]
    out = jax.ShapeDtypeStruct((rows, cols), F32)
    args = [t for a in range(n) for t in (*wmvs[a], g3s[a], recv_sibs[a], recv_chipss[a])]
    outs = pl.pallas_call(
        body, name=name,
        grid_spec=pltpu.PrefetchScalarGridSpec(
            num_scalar_prefetch=1, grid=(rows // tr,),
            in_specs=per_shard * n, out_specs=[blk] * (4 * n)),
        out_shape=[out] * (4 * n),
        compiler_params=_cp(1),
    )(own, *args)
    return [outs[4 * a:4 * a + 4] for a in range(n)]


def _small_finish(gathered, params, row_offs, extra_rows):
    n = len(params)

    def body(g_ref, *refs):
        ins, outs = refs[:3 * n], refs[3 * n:]
        total = g_ref[0]
        for k in range(1, N_DEV):
            total = total + g_ref[k]
        for e, (r0, nr) in enumerate(extra_rows):
            outs[e][...] = total[r0:r0 + nr, :]
        for p in range(n):
            w_ref, m_ref, v_ref = ins[3 * p:3 * p + 3]
            g_out, d_out, m_out, v_out = outs[len(extra_rows) + 4 * p:len(extra_rows) + 4 * p + 4]
            g = total[row_offs[p]:row_offs[p] + w_ref.shape[0], :]
            delta, m_new, v_new = _adam_math(w_ref[...], g, m_ref[...], v_ref[...])
            g_out[...] = g
            d_out[...] = delta
            m_out[...] = m_new
            v_out[...] = v_new

    vmem = pl.BlockSpec(memory_space=pltpu.VMEM)
    out_shape = [jax.ShapeDtypeStruct((nr, LANES), F32) for _, nr in extra_rows]
    for w, _, _ in params:
        out_shape += [jax.ShapeDtypeStruct(w.shape, F32)] * 4
    flat = [t for wmv in params for t in wmv]
    return pl.pallas_call(body, name="small_finish", in_specs=[vmem] * (1 + len(flat)),
                          out_specs=[vmem] * len(out_shape), out_shape=out_shape)(gathered, *flat)


def _adam_small(w, g, m, v):
    def body(w_ref, g_ref, m_ref, v_ref, do_ref, mo_ref, vo_ref):
        delta, m_new, v_new = _adam_math(w_ref[...], g_ref[...], m_ref[...], v_ref[...])
        do_ref[...] = delta
        mo_ref[...] = m_new
        vo_ref[...] = v_new

    vmem = pl.BlockSpec(memory_space=pltpu.VMEM)
    out = jax.ShapeDtypeStruct(w.shape, F32)
    return pl.pallas_call(body, name="adam_small", in_specs=[vmem] * 4, out_specs=[vmem] * 3,
                          out_shape=[out, out, out])(w, g, m, v)


def _proj_fwd(h, win_g):
    s, d = h.shape
    sw = win_g.shape[2]
    tn = min(512, sw)
    nh = sw // tn

    def body(h_ref, w_ref, o_ref):
        for rs in _chunks(s, 512):
            o_ref[rs, :] = _dot(h_ref[rs, :], w_ref[...], NN)

    return pl.pallas_call(
        body, name="proj_fwd", grid=(N_DEV * nh,),
        in_specs=[pl.BlockSpec((s, d), lambda j: (0, 0)),
                  pl.BlockSpec((None, d, tn), lambda j: (j // nh, 0, j % nh))],
        out_specs=pl.BlockSpec((None, s, tn), lambda j: (j // nh, 0, j % nh)),
        out_shape=jax.ShapeDtypeStruct((N_DEV, s, sw), F32),
        compiler_params=_cp(1),
    )(h, win_g)


def _conv_fwd(proj, conv_w, conv_b):
    _, s, sw = proj.shape
    tc = min(LANES, sw)

    def body(ba_ref, ca_ref, va_ref, cw_ref, cb_ref, z_ref):
        cv = ca_ref[...] * va_ref[...]
        u = (cb_ref[...] + cw_ref[0:1, :] * _shift_down(cv, 2) + cw_ref[1:2, :] * _shift_down(cv, 1)
             + cw_ref[2:3, :] * cv)
        z_ref[...] = (ba_ref[...] * u).astype(BF16)

    def part(k):
        return pl.BlockSpec((None, s, tc), lambda i: (k, 0, i))

    return pl.pallas_call(
        body, name="conv_fwd", grid=(sw // tc,),
        in_specs=[part(0), part(1), part(2),
                  pl.BlockSpec((CONV_K, tc), lambda i: (0, i)), pl.BlockSpec((1, tc), lambda i: (0, i))],
        out_specs=pl.BlockSpec((s, tc), lambda i: (0, i)),
        out_shape=jax.ShapeDtypeStruct((s, sw), BF16),
        compiler_params=_cp(1),
    )(proj, proj, proj, conv_w, conv_b)


def _pool_counts(shape, window):
    t = lax.broadcasted_iota(jnp.int32, shape, 0)
    return jnp.minimum(t + 1, window).astype(F32)


def _pool_fwd(proj):
    _, s, sw = proj.shape
    gw = sw // len(POOL_WINDOWS)

    def body(v_ref, p_ref):
        for gi, window in enumerate(POOL_WINDOWS):
            @pl.when(pl.program_id(0) == gi)
            def _():
                v = v_ref[...]
                acc, k = v, 1
                while k < window:
                    acc = acc + _shift_down(acc, k)
                    k *= 2
                p_ref[...] = (acc / _pool_counts(v.shape, window) - v).astype(BF16)

    return pl.pallas_call(
        body, name="pool_fwd", grid=(len(POOL_WINDOWS),),
        in_specs=[pl.BlockSpec((None, s, gw), lambda g: (3, 0, g))],
        out_specs=pl.BlockSpec((s, gw), lambda g: (0, g)),
        out_shape=jax.ShapeDtypeStruct((s, sw), BF16),
        compiler_params=_cp(1),
    )(proj)


def _merge_fwd(z, wa, p, wpool, proj, b_gate2, pool_scale):
    s, sw = z.shape
    tn = wa.shape[2]
    d = tn * N_DEV
    gw = sw // len(POOL_WINDOWS)
    nq = sw // tn

    def body(z_ref, wa_ref, p_ref, wp_ref, ga_ref, gb_ref, bg_ref, sc_ref,
             m_ref, dya_ref, dyb_ref, dga_ref, dgb_ref, dsc_ref):
        for rs in _chunks(s, 512):
            ya = _dot(z_ref[rs, :], wa_ref[...], NN)
            yb = _dot(p_ref[rs, :], wp_ref[...], NN)
            sa = _sigmoid(ga_ref[rs, :] + bg_ref[0:1, :])
            sb = _sigmoid(gb_ref[rs, :] + bg_ref[1:2, :])
            sc = sc_ref[...]
            sb_yb = sb * yb
            m_ref[rs, :] = (sa * ya + sb_yb * sc).astype(BF16)
            dya_ref[rs, :] = sa.astype(BF16)
            dyb_ref[rs, :] = (sb * sc).astype(BF16)
            dga_ref[rs, :] = (ya * (sa * (1.0 - sa))).astype(BF16)
            dgb_ref[rs, :] = ((yb * sc) * (sb * (1.0 - sb))).astype(BF16)
            dsc_ref[rs, :] = sb_yb.astype(BF16)

    col = pl.BlockSpec((s, tn), lambda j: (0, j))
    out = jax.ShapeDtypeStruct((s, d), BF16)
    return pl.pallas_call(
        body, name="merge_fwd", grid=(N_DEV,),
        in_specs=[pl.BlockSpec((s, sw), lambda j: (0, 0)),
                  pl.BlockSpec((None, sw, tn), lambda j: (j, 0, 0)),
                  pl.BlockSpec((s, gw), lambda j: (0, j // 2)),
                  pl.BlockSpec((None, gw, tn), lambda j: (j // 2, 0, j % 2)),
                  pl.BlockSpec((None, s, tn), lambda j: (4 + j // nq, 0, j % nq)),
                  pl.BlockSpec((None, s, tn), lambda j: (6 + j // nq, 0, j % nq)),
                  pl.BlockSpec((2, tn), lambda j: (0, j)),
                  pl.BlockSpec((1, tn), lambda j: (0, j))],
        out_specs=[col] * 6,
        out_shape=[out] * 6,
        compiler_params=_cp(1),
    )(z, wa, p, wpool, proj, proj, b_gate2, pool_scale)


def _wo_fwd(merged, wo, x2d, g2):
    s, d = x2d.shape
    tm = min(256, s)

    def body(m_ref, wo_ref, x_ref, g_ref, x1_ref, h2_ref):
        x1 = x_ref[...] + _dot(m_ref[...], wo_ref[...], NN)
        x1_ref[...] = x1
        r = lax.rsqrt(jnp.mean(x1 * x1, axis=-1, keepdims=True) + EPS)
        h2_ref[...] = (x1 * r * g_ref[...]).astype(BF16)

    row = pl.BlockSpec((tm, d), lambda i: (i, 0))
    return pl.pallas_call(
        body, name="wo_fwd", grid=(s // tm,),
        in_specs=[row, pl.BlockSpec((d, d), lambda i: (0, 0)), row, pl.BlockSpec((1, d), lambda i: (0, 0))],
        out_specs=[row, row],
        out_shape=[jax.ShapeDtypeStruct((s, d), F32), jax.ShapeDtypeStruct((s, d), BF16)],
        compiler_params=_cp(1),
    )(merged, wo, x2d, g2)


def _ffn_up_act_fwd(h2, wg_g, wu_g):
    s, d = h2.shape
    f8 = wg_g.shape[2]
    th = min(1024, s)

    def body(h_ref, wg_ref, wu_ref, dadu_ref, dadg_ref, a_ref):
        i = pl.program_id(1)
        for rs in _chunks(th, 512):
            rows = pl.ds(pl.multiple_of(i * th + rs.start, rs.stop - rs.start), rs.stop - rs.start)
            a = h_ref[rows, :]
            g = _dot(a, wg_ref[...], NN)
            u = _dot(a, wu_ref[...], NN)
            sg = _sigmoid(g)
            silu = g * sg
            dadu_ref[rs, :] = silu.astype(BF16)
            dadg_ref[rs, :] = (u * (sg * (1.0 + g * (1.0 - sg)))).astype(BF16)
            a_ref[rs, :] = (silu * u).astype(BF16)

    wspec = pl.BlockSpec((None, d, f8), lambda j, i: (j, 0, 0))
    ospec = pl.BlockSpec((None, th, f8), lambda j, i: (j, i, 0))
    out = jax.ShapeDtypeStruct((N_DEV, s, f8), BF16)
    return pl.pallas_call(
        body, name="ffn_up_fwd", grid=(N_DEV, s // th),
        in_specs=[pl.BlockSpec((s, d), lambda j, i: (0, 0)), wspec, wspec],
        out_specs=[ospec, ospec, ospec], out_shape=[out, out, out],
        compiler_params=_cp(2),
    )(h2, wg_g, wu_g)


def _ffn_down_fwd(act, wd_g):
    _, s, f8 = act.shape
    d = wd_g.shape[2]
    tn = min(1024, d)

    def body(a_ref, wd_ref, o_ref):
        j = pl.program_id(1)

        @pl.when(j == 0)
        def _():
            o_ref[...] = jnp.zeros_like(o_ref)

        for rs in _chunks(s, 1024):
            o_ref[rs, :] += _dot(a_ref[rs, :], wd_ref[...], NN)

    return pl.pallas_call(
        body, name="ffn_down_fwd", grid=(d // tn, N_DEV),
        in_specs=[pl.BlockSpec((None, s, f8), lambda n, j: (j, 0, 0)),
                  pl.BlockSpec((None, f8, tn), lambda n, j: (j, 0, n))],
        out_specs=pl.BlockSpec((s, tn), lambda n, j: (0, n)),
        out_shape=jax.ShapeDtypeStruct((s, d), F32),
        compiler_params=_cp(2),
    )(act, wd_g)


def _loss_bwd(ffn_out, x1, target, final_g):
    s, d = x1.shape
    tm = min(256, s)

    def body(f_ref, x1_ref, t_ref, gf_ref, dxb_ref, dgf_ref, loss_ref):
        @pl.when(pl.program_id(0) == 0)
        def _():
            dgf_ref[...] = jnp.zeros_like(dgf_ref)
            loss_ref[...] = jnp.zeros_like(loss_ref)

        x2 = x1_ref[...] + f_ref[...]
        r = lax.rsqrt(jnp.mean(x2 * x2, axis=-1, keepdims=True) + EPS)
        nrm = x2 * r
        gf = gf_ref[...]
        err = nrm * gf - t_ref[...]
        loss_ref[...] += jnp.sum(err * err) * (0.5 / d)
        dy = err * (1.0 / d)
        dgf_ref[...] += jnp.sum(dy * nrm, axis=0, keepdims=True)
        dn = dy * gf
        dx = r * (dn - nrm * jnp.mean(dn * nrm, axis=-1, keepdims=True))
        dxb_ref[...] = dx.astype(BF16)

    row = pl.BlockSpec((tm, d), lambda i: (i, 0))
    vec = pl.BlockSpec((1, d), lambda i: (0, 0))
    return pl.pallas_call(
        body, name="loss_bwd", grid=(s // tm,),
        in_specs=[row, row, row, vec],
        out_specs=[row, vec, pl.BlockSpec((8, LANES), lambda i: (0, 0))],
        out_shape=[jax.ShapeDtypeStruct((s, d), BF16),
                   jax.ShapeDtypeStruct((1, d), F32), jax.ShapeDtypeStruct((8, LANES), F32)],
        compiler_params=_cp(1),
    )(ffn_out, x1, target, final_g)


def _ffn_gate_bwd(dx2b, wd_g, dadg, dadu):
    s, d = dx2b.shape
    f8 = dadg.shape[2]
    th = min(1024, s)

    def body(dx_ref, wd_ref, g_ref, u_ref, dg_ref, du_ref, da_ref):
        i = pl.program_id(1)
        chunks = _chunks(th, 256)

        def matmul(rs):
            rows = pl.ds(pl.multiple_of(i * th + rs.start, rs.stop - rs.start), rs.stop - rs.start)
            da_ref[rs, :] = _dot(dx_ref[rows, :], wd_ref[...], NT)

        matmul(chunks[0])
        for k, rs in enumerate(chunks):
            if k + 1 < len(chunks):
                matmul(chunks[k + 1])
            da = da_ref[rs, :].astype(BF16)
            dg_ref[rs, :] = da * g_ref[rs, :]
            du_ref[rs, :] = da * u_ref[rs, :]

    aspec = pl.BlockSpec((None, th, f8), lambda j, i: (j, i, 0))
    out = jax.ShapeDtypeStruct((N_DEV, s, f8), BF16)
    return pl.pallas_call(
        body, name="ffn_act_bwd", grid=(N_DEV, s // th),
        in_specs=[pl.BlockSpec((s, d), lambda j, i: (0, 0)),
                  pl.BlockSpec((None, f8, d), lambda j, i: (j, 0, 0)), aspec, aspec],
        out_specs=[aspec, aspec], out_shape=[out, out],
        scratch_shapes=[pltpu.VMEM((th, f8), F32)],
        compiler_params=_cp(2),
    )(dx2b, wd_g, dadg, dadu)


def _wgrad_rows(a3, b, name, after=(), carry=()):
    _, s, k = a3.shape
    n = b.shape[1]
    nc = len(carry)
    c_in, c_out, c_shape, c_sems = _carry_specs(carry)

    def body(a_ref, b_ref, *rest):
        rest = rest[len(after):]
        o_ref = rest[nc]
        j = pl.program_id(0)
        _carry_run(j == 0, j == N_DEV - 1, rest[:nc], rest[nc + 1:2 * nc + 1], rest[2 * nc + 1:])
        o_ref[...] = _dot(a_ref[...], b_ref[...], TN).astype(BF16)

    outs = pl.pallas_call(
        body, name=name, grid=(N_DEV,),
        in_specs=[pl.BlockSpec((None, s, k), lambda j: (j, 0, 0)),
                  pl.BlockSpec((s, n), lambda j: (0, 0))] + _after_specs(after) + c_in,
        out_specs=[pl.BlockSpec((None, k, n), lambda j: (j, 0, 0))] + c_out,
        out_shape=[jax.ShapeDtypeStruct((N_DEV, k, n), BF16)] + c_shape,
        scratch_shapes=c_sems,
        compiler_params=_cp_carry(1, carry),
    )(a3, b, *after, *carry)
    return (outs[0], list(outs[1:])) if nc else outs[0]


def _wgrad_cols(a, b3, name, after=()):
    s, k = a.shape
    if b3.ndim == 2:
        n = b3.shape[1] // N_DEV
        b_spec = pl.BlockSpec((s, n), lambda j: (0, j))
    else:
        n = b3.shape[2]
        b_spec = pl.BlockSpec((None, s, n), lambda j: (j, 0, 0))

    def body(a_ref, b_ref, *rest):
        o_ref = rest[len(after)]
        o_ref[...] = _dot(a_ref[...], b_ref[...], TN).astype(BF16)

    return pl.pallas_call(
        body, name=name, grid=(N_DEV,),
        in_specs=[pl.BlockSpec((s, k), lambda j: (0, 0)), b_spec] + _after_specs(after),
        out_specs=pl.BlockSpec((None, k, n), lambda j: (j, 0, 0)),
        out_shape=jax.ShapeDtypeStruct((N_DEV, k, n), BF16),
        compiler_params=_cp(1),
    )(a, b3, *after)


def _input_grad(pairs, name, after=(), carry=()):
    s = pairs[0][0].shape[1]
    d = pairs[0][1].shape[1]
    tn = min(1024, d)
    npair = len(pairs)
    nc = len(carry)
    c_in, c_out, c_shape, c_sems = _carry_specs(carry)

    def body(*refs):
        ops = refs[:2 * npair]
        rest = refs[2 * npair + len(after):]
        o_ref, acc_ref = rest[nc], rest[-1]
        nh, j = pl.program_id(0), pl.program_id(1)
        _carry_run((nh == 0) & (j == 0), (nh == d // tn - 1) & (j == N_DEV - 1),
                   rest[:nc], rest[nc + 1:2 * nc + 1], rest[2 * nc + 1:-1])

        @pl.when(j == 0)
        def _():
            acc_ref[...] = jnp.zeros_like(acc_ref)

        for rs in _chunks(s, 1024):
            part = _dot(ops[0][rs, :], ops[1][...], NT)
            for q in range(1, npair):
                part = part + _dot(ops[2 * q][rs, :], ops[2 * q + 1][...], NT)
            acc_ref[rs, :] += part

        @pl.when(j == N_DEV - 1)
        def _():
            o_ref[...] = acc_ref[...].astype(BF16)

    in_specs, args = [], []
    for a3, w3 in pairs:
        k = a3.shape[2]
        in_specs += [pl.BlockSpec((None, s, k), lambda n, j: (j, 0, 0)),
                     pl.BlockSpec((None, tn, k), lambda n, j: (j, n, 0))]
        args += [a3, w3]
    outs = pl.pallas_call(
        body, name=name, grid=(d // tn, N_DEV),
        in_specs=in_specs + _after_specs(after) + c_in,
        out_specs=[pl.BlockSpec((s, tn), lambda n, j: (0, n))] + c_out,
        out_shape=[jax.ShapeDtypeStruct((s, d), BF16)] + c_shape,
        scratch_shapes=c_sems + [pltpu.VMEM((s, tn), F32)],
        compiler_params=_cp_carry(2, carry),
    )(*args, *after, *carry)
    return (outs[0], list(outs[1:])) if nc else outs[0]


def _input_grad_adam(own, a3, w3, name, after, carry, wmvs, g3s, recv_sibs, recv_chipss):
    s, k = a3.shape[1], a3.shape[2]
    d = w3.shape[1]
    tn = min(1024, d)
    nsteps = (d // tn) * N_DEV
    nc, na = len(carry), len(wmvs)
    rows, cols = wmvs[0][0].shape
    cb = cols // nsteps
    c_in, c_out, c_shape, c_sems = _carry_specs(carry)

    def body(own_ref, a_ref, w_ref, *rest):
        rest = rest[len(after):]
        side_in = rest[nc:nc + 6 * na]
        o_ref = rest[nc + 6 * na]
        recvs = rest[nc + 6 * na + 1:2 * nc + 6 * na + 1]
        side_out = rest[2 * nc + 6 * na + 1:2 * nc + 10 * na + 1]
        sems, acc_ref = rest[2 * nc + 10 * na + 1:-1], rest[-1]
        nh, j = pl.program_id(0), pl.program_id(1)
        _carry_run((nh == 0) & (j == 0), (nh == d // tn - 1) & (j == N_DEV - 1), rest[:nc], recvs, sems)

        @pl.when(j == 0)
        def _():
            acc_ref[...] = jnp.zeros_like(acc_ref)

        for rs in _chunks(s, 1024):
            acc_ref[rs, :] += _dot(a_ref[rs, :], w_ref[...], NT)

        @pl.when(j == N_DEV - 1)
        def _():
            o_ref[...] = acc_ref[...].astype(BF16)

        for a in range(na):
            w_ref2, m_ref, v_ref, g_ref, rs_ref, rc_ref = side_in[6 * a:6 * a + 6]
            g = g_ref[...].astype(F32) + rs_ref[...].astype(F32)
            g = g + rc_ref[0].astype(F32)
            g = g + rc_ref[1].astype(F32)
            g = g + rc_ref[2].astype(F32)
            delta, m_new, v_new = _adam_math(w_ref2[...], g, m_ref[...], v_ref[...])
            side_out[4 * a][...] = g
            side_out[4 * a + 1][...] = delta
            side_out[4 * a + 2][...] = m_new
            side_out[4 * a + 3][...] = v_new

    def step(n, j):
        return n * N_DEV + j

    blk = pl.BlockSpec((rows, cb), lambda n, j, o: (0, step(n, j)))
    per_shard = [blk, blk, blk,
                 pl.BlockSpec((None, rows, cb), lambda n, j, o: (o[0], 0, step(n, j))),
                 pl.BlockSpec((None, rows, cb), lambda n, j, o: (o[1], 0, step(n, j))),
                 pl.BlockSpec((3, rows, cb), lambda n, j, o: (0, 0, step(n, j)))]
    side_args = [t for a in range(na) for t in (*wmvs[a], g3s[a], recv_sibs[a], recv_chipss[a])]
    out = jax.ShapeDtypeStruct((rows, cols), F32)
    outs = pl.pallas_call(
        body, name=name,
        grid_spec=pltpu.PrefetchScalarGridSpec(
            num_scalar_prefetch=1, grid=(d // tn, N_DEV),
            in_specs=[pl.BlockSpec((None, s, k), lambda n, j, o: (j, 0, 0)),
                      pl.BlockSpec((None, tn, k), lambda n, j, o: (j, n, 0))]
            + _after_specs(after) + c_in + per_shard * na,
            out_specs=[pl.BlockSpec((s, tn), lambda n, j, o: (0, n))] + c_out + [blk] * (4 * na),
            scratch_shapes=c_sems + [pltpu.VMEM((s, tn), F32)]),
        out_shape=[jax.ShapeDtypeStruct((s, d), BF16)] + c_shape + [out] * (4 * na),
        compiler_params=_cp_carry(2, carry),
    )(own, a3, w3, *after, *carry, *side_args)
    return outs[0], list(outs[1:1 + nc]), [outs[1 + nc + 4 * a:1 + nc + 4 * a + 4] for a in range(na)]


def _rms_bwd(dh, xres, g, dres, name, with_bf16=True):
    s, d = xres.shape
    tm = min(256, s)

    def body(dh_ref, x_ref, g_ref, dres_ref, dx_ref, *rest):
        dg_ref = rest[-1]
        @pl.when(pl.program_id(0) == 0)
        def _():
            dg_ref[...] = jnp.zeros_like(dg_ref)

        xv = x_ref[...]
        dh_v = dh_ref[...].astype(F32)
        r = lax.rsqrt(jnp.mean(xv * xv, axis=-1, keepdims=True) + EPS)
        nrm = xv * r
        dg_ref[...] += jnp.sum(dh_v * nrm, axis=0, keepdims=True)
        dn = dh_v * g_ref[...]
        dx = dres_ref[...].astype(F32) + r * (dn - nrm * jnp.mean(dn * nrm, axis=-1, keepdims=True))
        dx_ref[...] = dx
        if with_bf16:
            rest[0][...] = dx.astype(BF16)

    row = pl.BlockSpec((tm, d), lambda i: (i, 0))
    vec = pl.BlockSpec((1, d), lambda i: (0, 0))
    copies = [jax.ShapeDtypeStruct((s, d), BF16)] if with_bf16 else []
    outs = pl.pallas_call(
        body, name=name, grid=(s // tm,),
        in_specs=[row, row, vec, row],
        out_specs=[row] + [row] * len(copies) + [vec],
        out_shape=[jax.ShapeDtypeStruct((s, d), F32)] + copies + [jax.ShapeDtypeStruct((1, d), F32)],
        compiler_params=_cp(1),
    )(dh, xres, g, dres)
    return (outs[0], outs[1], outs[2]) if with_bf16 else (outs[0], None, outs[1])


def _wgrad_full(a, b, name, after=(), carry=()):
    s, k = a.shape
    n = b.shape[1]
    tk = min(512, k)
    nc = len(carry)
    c_in, c_out, c_shape, c_sems = _carry_specs(carry)

    def body(a_ref, b_ref, *rest):
        rest = rest[len(after):]
        o_ref = rest[nc]
        j = pl.program_id(0)
        _carry_run(j == 0, j == k // tk - 1, rest[:nc], rest[nc + 1:2 * nc + 1], rest[2 * nc + 1:])
        o_ref[...] = _dot(a_ref[...], b_ref[...], TN).astype(BF16)

    outs = pl.pallas_call(
        body, name=name, grid=(k // tk,),
        in_specs=[pl.BlockSpec((s, tk), lambda j: (0, j)),
                  pl.BlockSpec((s, n), lambda j: (0, 0))] + _after_specs(after) + c_in,
        out_specs=[pl.BlockSpec((tk, n), lambda j: (j, 0))] + c_out,
        out_shape=[jax.ShapeDtypeStruct((k, n), BF16)] + c_shape,
        scratch_shapes=c_sems,
        compiler_params=_cp_carry(1, carry),
    )(a, b, *after, *carry)
    return (outs[0], list(outs[1:])) if nc else outs[0]


def _wgrad_pool(p, dyb, n_groups):
    s, sw = p.shape
    d = dyb.shape[1]
    gw, go = sw // n_groups, d // n_groups
    ts = min(512, s)
    ns = s // ts

    def body(a_ref, b_ref, o_ref, acc_ref):
        i = pl.program_id(1)

        @pl.when(i == 0)
        def _():
            acc_ref[...] = jnp.zeros_like(acc_ref)

        acc_ref[...] += _dot(a_ref[...], b_ref[...], TN)

        @pl.when(i == ns - 1)
        def _():
            o_ref[...] = acc_ref[...].astype(BF16)

    return pl.pallas_call(
        body, name="wgrad_pool", grid=(n_groups, ns),
        in_specs=[pl.BlockSpec((ts, gw), lambda g, i: (i, g)),
                  pl.BlockSpec((ts, go), lambda g, i: (i, g))],
        out_specs=pl.BlockSpec((None, gw, go), lambda g, i: (g, 0, 0)),
        out_shape=jax.ShapeDtypeStruct((n_groups, gw, go), BF16),
        scratch_shapes=[pltpu.VMEM((gw, go), F32)],
        compiler_params=_cp(2),
    )(p, dyb)


def _wo_bwd(dx1b, wo, factors, sw, after=()):
    s, d = dx1b.shape
    tn = d // N_DEV
    nq = sw // tn

    def body(dx_ref, wo_ref, fya_ref, fyb_ref, fga_ref, fgb_ref, fsc_ref, *rest):
        dya_ref, dyb_ref, dp_ref, dbg_ref, dsc_ref, dm_ref = rest[len(after):]
        dbg_ref[...] = jnp.zeros_like(dbg_ref)
        dsc_ref[...] = jnp.zeros_like(dsc_ref)
        for rs in _chunks(s, 1024):
            dm_ref[rs, :] = _dot(dx_ref[rs, :], wo_ref[...], NT)
        for rs in _chunks(s, 256):
            dm = dm_ref[rs, :]
            dya_ref[rs, :] = (dm * fya_ref[rs, :].astype(F32)).astype(BF16)
            dyb_ref[rs, :] = (dm * fyb_ref[rs, :].astype(F32)).astype(BF16)
            dsc_ref[...] += jnp.sum(dm * fsc_ref[rs, :].astype(F32), axis=0, keepdims=True)
            dga = dm * fga_ref[rs, :].astype(F32)
            dgb = dm * fgb_ref[rs, :].astype(F32)
            dp_ref[0, rs, :] = dga.astype(BF16)
            dp_ref[1, rs, :] = dgb.astype(BF16)
            dbg_ref[0:1, :] += jnp.sum(dga, axis=0, keepdims=True)
            dbg_ref[1:2, :] += jnp.sum(dgb, axis=0, keepdims=True)

    col = pl.BlockSpec((s, tn), lambda j: (0, j))
    out = jax.ShapeDtypeStruct((s, d), BF16)
    return pl.pallas_call(
        body, name="wo_bwd", grid=(N_DEV,),
        in_specs=[pl.BlockSpec((s, d), lambda j: (0, 0)),
                  pl.BlockSpec((tn, d), lambda j: (j, 0))] + [col] * 5 + _after_specs(after),
        out_specs=[col, col,
                   pl.BlockSpec((2, None, s, tn), lambda j: (1, j // nq, 0, j % nq)),
                   pl.BlockSpec((2, tn), lambda j: (0, j)),
                   pl.BlockSpec((1, tn), lambda j: (0, j))],
        out_shape=[out, out, jax.ShapeDtypeStruct((4, 2, s, sw), BF16),
                   jax.ShapeDtypeStruct((2, d), F32), jax.ShapeDtypeStruct((1, d), F32)],
        scratch_shapes=[pltpu.VMEM((s, tn), F32)],
        compiler_params=_cp(1),
    )(dx1b, wo, *factors, *after)


def _conv_bwd(dproj, dya, wa, proj, conv_w, conv_b):
    s, d = dya.shape
    sw, tn = wa.shape[1], wa.shape[2]
    tc = min(LANES, sw)

    def body(dproj_hbm, dya_ref, wa_ref, ba_ref, ca_ref, va_ref, cw_ref, cb_ref,
             dp_ref, dcw_ref, dcb_ref, dz_ref):
        del dproj_hbm
        for rs in _chunks(s, 512):
            part = _dot(dya_ref[rs, 0:tn], wa_ref[0], NT)
            for j in range(1, N_DEV):
                part = part + _dot(dya_ref[rs, j * tn:(j + 1) * tn], wa_ref[j], NT)
            dz_ref[rs, :] = part
        dz = dz_ref[...]
        ba, ca, va = ba_ref[...], ca_ref[...], va_ref[...]
        cv = ca * va
        cv1, cv2 = _shift_down(cv, 1), _shift_down(cv, 2)
        w0, w1, w2 = cw_ref[0:1, :], cw_ref[1:2, :], cw_ref[2:3, :]
        u = cb_ref[...] + w0 * cv2 + w1 * cv1 + w2 * cv
        du = dz * ba
        dp_ref[0] = (dz * u).astype(BF16)
        dcv = w2 * du + w1 * _shift_up(du, 1) + w0 * _shift_up(du, 2)
        dp_ref[1] = (dcv * va).astype(BF16)
        dp_ref[2] = (dcv * ca).astype(BF16)
        dcw_ref[0:1, :] = jnp.sum(du * cv2, axis=0, keepdims=True)
        dcw_ref[1:2, :] = jnp.sum(du * cv1, axis=0, keepdims=True)
        dcw_ref[2:3, :] = jnp.sum(du * cv, axis=0, keepdims=True)
        dcb_ref[...] = jnp.sum(du, axis=0, keepdims=True)

    def part(k):
        return pl.BlockSpec((None, s, tc), lambda i: (k, 0, i))

    return pl.pallas_call(
        body, name="conv_bwd", grid=(sw // tc,),
        in_specs=[pl.BlockSpec(memory_space=pl.ANY),
                  pl.BlockSpec((s, d), lambda i: (0, 0)),
                  pl.BlockSpec((N_DEV, tc, tn), lambda i: (0, i, 0)),
                  part(0), part(1), part(2),
                  pl.BlockSpec((CONV_K, tc), lambda i: (0, i)), pl.BlockSpec((1, tc), lambda i: (0, i))],
        out_specs=[pl.BlockSpec((3, s, tc), lambda i: (0, 0, i)),
                   pl.BlockSpec((CONV_K, tc), lambda i: (0, i)), pl.BlockSpec((1, tc), lambda i: (0, i))],
        out_shape=[jax.ShapeDtypeStruct(dproj.shape, BF16),
                   jax.ShapeDtypeStruct((CONV_K, sw), F32), jax.ShapeDtypeStruct((1, sw), F32)],
        scratch_shapes=[pltpu.VMEM((s, tc), F32)],
        input_output_aliases={0: 0},
        compiler_params=_cp(1),
    )(dproj, dya, wa, proj, proj, proj, conv_w, conv_b)


def _pool_bwd(dproj, dyb, wpool):
    s, d = dyb.shape
    n_groups, gw, go = wpool.shape

    def body(dproj_hbm, dyb_ref, wp_ref, dp_ref):
        del dproj_hbm
        for gi, window in enumerate(POOL_WINDOWS):
            @pl.when(pl.program_id(0) == gi)
            def _():
                dpool = _dot(dyb_ref[...], wp_ref[...], NT)
                acc, k = dpool / _pool_counts(dpool.shape, window), 1
                while k < window:
                    acc = acc + _shift_up(acc, k)
                    k *= 2
                dp_ref[...] = (acc - dpool).astype(BF16)

    return pl.pallas_call(
        body, name="pool_bwd", grid=(n_groups,),
        in_specs=[pl.BlockSpec(memory_space=pl.ANY),
                  pl.BlockSpec((s, go), lambda g: (0, g)),
                  pl.BlockSpec((None, gw, go), lambda g: (g, 0, 0))],
        out_specs=pl.BlockSpec((None, s, gw), lambda g: (3, 0, g)),
        out_shape=jax.ShapeDtypeStruct(dproj.shape, BF16),
        input_output_aliases={0: 0},
        compiler_params=_cp(1),
    )(dproj, dyb, wpool)


def _rows128(v):
    return v.reshape(-1, LANES)


def kernel(x, norm1_g, w_in, b_gate, conv_w, conv_b, w_a_out, w_pool, pool_scale, w_o, norm2_g, w_ffn_gate, w_ffn_up, w_ffn_down, final_g, loss_target, m_norm1_g, m_w_in, m_b_gate, m_conv_w, m_conv_b, m_w_a_out, m_w_pool, m_pool_scale, m_w_o, m_norm2_g, m_w_ffn_gate, m_w_ffn_up, m_w_ffn_down, m_final_g, v_norm1_g, v_w_in, v_b_gate, v_conv_w, v_conv_b, v_w_a_out, v_w_pool, v_pool_scale, v_w_o, v_norm2_g, v_w_ffn_gate, v_w_ffn_up, v_w_ffn_down, v_final_g):
    s, d = x.shape[1], x.shape[2]
    sw = w_in.shape[2]
    n_groups = w_pool.shape[1]
    gw = w_pool.shape[2]
    go = w_pool.shape[3] * N_DEV
    f8 = w_ffn_gate.shape[2]
    cws = conv_w.shape[2]
    assert sw == conv_w.shape[2] * N_DEV == gw * n_groups and go * n_groups == d and n_groups == len(POOL_WINDOWS)

    xi, yi, ci = _coords()
    me = 4 * xi + 2 * yi + ci
    my_chip = 2 * xi + yi

    x2d = x.reshape(s, d)
    target = loss_target.reshape(s, d)
    final_g2 = final_g.reshape(1, d)
    b_gate2 = b_gate.reshape(2, d)

    big_names = ["w_in", "w_a_out", "w_pool", "w_o", "w_ffn_gate", "w_ffn_up", "w_ffn_down"]
    big_w = [w_in, w_a_out, w_pool, w_o, w_ffn_gate, w_ffn_up, w_ffn_down]
    big_m = [m_w_in, m_w_a_out, m_w_pool, m_w_o, m_w_ffn_gate, m_w_ffn_up, m_w_ffn_down]
    big_v = [v_w_in, v_w_a_out, v_w_pool, v_w_o, v_w_ffn_gate, v_w_ffn_up, v_w_ffn_down]
    shapes2d = [(w.size // w.shape[-1], w.shape[-1]) for w in big_w]
    big_w2 = [w.reshape(sh) for w, sh in zip(big_w, shapes2d)]
    transposed = (4, 5)

    def view2d(t, a):
        t2 = t.reshape(shapes2d[a])
        return t2.T if a in transposed else t2

    def unview(o, a):
        return (o.T if a in transposed else o).reshape(big_w[a].shape)

    sb = [_cast_bf16(w, "cast_" + nm) for w, nm in zip(big_w2, big_names)]
    win_g, wa_g, wpool_g, wo_g = _allgather_big(sb[0:4], "allgather_mixer", COLLECTIVE_GATHER)
    wg_g, wu_g = _allgather_big(sb[4:6], "allgather_ffn_up", COLLECTIVE_GATHER)
    (wd_g,) = _allgather_big(sb[6:7], "allgather_ffn_down", COLLECTIVE_GATHER)
    convw_g = _allgather_small(jnp.pad(conv_w.reshape(CONV_K, cws), ((0, 8 - CONV_K), (0, 0))), "allgather_conv_w")
    conv_w_full = convw_g[:, :CONV_K, :].transpose(1, 0, 2).reshape(CONV_K, sw)
    wpool = wpool_g.reshape(N_DEV, n_groups, gw, go // N_DEV).transpose(1, 2, 0, 3).reshape(n_groups, gw, go)
    wo = wo_g.reshape(d, d)

    h = _rms_fwd(x2d, norm1_g)
    proj = _proj_fwd(h, win_g)
    z = _conv_fwd(proj, conv_w_full, conv_b)
    p = _pool_fwd(proj)
    merged, *merge_factors = _merge_fwd(z, wa_g, p, wpool, proj, b_gate2, pool_scale)
    x1, h2 = _wo_fwd(merged, wo, x2d, norm2_g)
    dadu, dadg, act = _ffn_up_act_fwd(h2, wg_g, wu_g)
    ffn_out = _ffn_down_fwd(act, wd_g)
    dx2b, d_final_g, loss_blk = _loss_bwd(ffn_out, x1, target, final_g2)

    other_chips = jnp.stack([2 * (1 - xi) + yi, 2 * xi + (1 - yi), 2 * (1 - xi) + (1 - yi)])
    others = jnp.concatenate([other_chips, 2 * other_chips + ci]).astype(jnp.int32)

    def partials(grads, recvs, names):
        if all(g.shape == grads[0].shape for g in grads):
            return list(_chip_partial(others, grads, recvs, "chip_partial_" + names[0]))
        return [_chip_partial(others, [g3], [r], "chip_partial_" + nm)[0] for g3, r, nm in zip(grads, recvs, names)]

    own = jnp.stack([me, my_chip]).astype(jnp.int32)

    def adam(idx, g3s, sibs, chipss):
        wmvs = [(view2d(big_w[a], a), view2d(big_m[a], a), view2d(big_v[a], a)) for a in idx]
        outs = _adam_big(own, wmvs, g3s, sibs, chipss, "adam_" + big_names[idx[0]])
        for a, o4 in zip(idx, outs):
            big_out[a] = [unview(o, a) for o in o4]

    big_out = [None] * len(big_names)
    dg_act, du_act = _ffn_gate_bwd(dx2b, wd_g, dadg, dadu)
    gw_gate = _wgrad_rows(dg_act, h2, "wgrad_ffn_gate")
    gw_up = _wgrad_rows(du_act, h2, "wgrad_ffn_up")
    gw_down, sib_gu = _wgrad_rows(act, dx2b, "wgrad_ffn_down", carry=[gw_gate, gw_up])
    ps_gu = partials([gw_gate, gw_up], sib_gu, ["w_ffn_gate", "w_ffn_up"])
    chips_gu = _exchange_chips(ps_gu, "rs_chips_ffn_up", COLLECTIVE_CHIPS)
    dh2, sib_down = _input_grad([(dg_act, wg_g), (du_act, wu_g)], "ffn_in_bwd", after=ps_gu, carry=[gw_down])
    ps_down = partials([gw_down], sib_down, ["w_ffn_down"])
    chips_down = _exchange_chips(ps_down, "rs_chips_ffn_down", COLLECTIVE_CHIPS)
    dx1, dx1b, d_norm2_g = _rms_bwd(dh2, x1, norm2_g, dx2b, "rms2_bwd")
    dya, dyb, dproj42, d_b_gate, d_pool_scale = _wo_bwd(dx1b, wo, merge_factors, sw, after=ps_down)
    dproj = dproj42.reshape(N_DEV, s, sw)
    dproj, d_conv_w, d_conv_b = _conv_bwd(dproj, dya, wa_g, proj, conv_w_full, conv_b)
    dproj = _pool_bwd(dproj, dyb, wpool)
    gw_in = _wgrad_cols(h, dproj, "wgrad_in")
    gw_o, sib_in = _wgrad_full(merged, dx1b, "wgrad_o", carry=[gw_in])
    ps_in = partials([gw_in], sib_in, ["w_in"])
    chips_in = _exchange_chips(ps_in, "rs_chips_w_in", COLLECTIVE_CHIPS)
    gw_a = _wgrad_cols(z, dya, "wgrad_a_out", after=ps_in)
    gw_pool = _wgrad_pool(p, dyb, n_groups)
    mix3 = [gw_a,
            gw_pool.reshape(n_groups, gw, N_DEV, go // N_DEV).transpose(2, 0, 1, 3).reshape(N_DEV, n_groups * gw, go // N_DEV),
            gw_o.reshape(N_DEV, d // N_DEV, d)]
    adam([6], [gw_down], sib_down, chips_down)
    gu_wmvs = [(view2d(big_w[a], a), view2d(big_m[a], a), view2d(big_v[a], a)) for a in (4, 5)]
    dh, sib_mix, gu_out = _input_grad_adam(own, dproj, win_g, "proj_in_bwd", [big_out[6][0]], mix3,
                                           gu_wmvs, [gw_gate, gw_up], sib_gu, chips_gu)
    for a, o4 in zip((4, 5), gu_out):
        big_out[a] = [unview(o, a) for o in o4]
    ps_mix = partials(mix3, sib_mix, ["w_a_out", "w_pool", "w_o"])
    chips_mix = _exchange_chips(ps_mix, "rs_chips_mixer", COLLECTIVE_CHIPS)
    grad_x, _, d_norm1_g = _rms_bwd(dh, x2d, norm1_g, dx1, "rms1_bwd", with_bf16=False)
    adam([0], [gw_in], sib_in, chips_in)
    for k in range(3):
        adam([1 + k], [mix3[k]], [sib_mix[k]], [chips_mix[k]])

    small_parts = [d_norm1_g, d_b_gate, d_conv_w, d_conv_b, d_pool_scale, d_norm2_g, d_final_g, loss_blk]
    rows = [v.size // LANES for v in small_parts]
    row0 = [sum(rows[:k]) for k in range(len(rows))]
    packed = jnp.concatenate([_rows128(v) for v in small_parts], axis=0)
    gathered = _allgather_small(packed, "allgather_small_grads")
    small_names = ["norm1_g", "b_gate", "conv_b", "pool_scale", "norm2_g", "final_g", "conv_w"]
    small_w = [norm1_g, b_gate, conv_b, pool_scale, norm2_g, final_g]
    small_m = [m_norm1_g, m_b_gate, m_conv_b, m_pool_scale, m_norm2_g, m_final_g]
    small_v = [v_norm1_g, v_b_gate, v_conv_b, v_pool_scale, v_norm2_g, v_final_g]
    finished = _small_finish(gathered, [tuple(_rows128(t) for t in wmv) for wmv in zip(small_w, small_m, small_v)],
                             [row0[k] for k in (0, 1, 3, 4, 5, 6)], [(row0[2], rows[2]), (row0[7], rows[7])])
    g_convw_full, loss_rows = finished[0], finished[1]
    loss = loss_rows[0, 0]
    small_out = [[t.reshape(w.shape) for t in finished[2 + 4 * k:6 + 4 * k]] for k, w in enumerate(small_w)]
    g_convw = lax.dynamic_slice(g_convw_full.reshape(CONV_K, sw), (0, me * cws), (CONV_K, cws))
    cw_delta, cw_m, cw_v = _adam_small(conv_w.reshape(CONV_K, cws), g_convw,
                                       m_conv_w.reshape(CONV_K, cws), v_conv_w.reshape(CONV_K, cws))
    small_out.append([t.reshape(conv_w.shape) for t in (g_convw, cw_delta, cw_m, cw_v)])

    order = ["norm1_g", "w_in", "b_gate", "conv_w", "conv_b", "w_a_out", "w_pool", "pool_scale", "w_o", "norm2_g",
             "w_ffn_gate", "w_ffn_up", "w_ffn_down", "final_g"]
    per_kind = [{}, {}, {}, {}]
    for a, nm in enumerate(big_names):
        for kind in range(4):
            per_kind[kind][nm] = big_out[a][kind]
    for k, nm in enumerate(small_names):
        for kind in range(4):
            per_kind[kind][nm] = small_out[k][kind]
    result = [loss, grad_x.reshape(x.shape)]
    for kind in range(4):
        result += [per_kind[kind][nm] for nm in order]
    return tuple(result)
```

```python
import jax
import jax.numpy as jnp
from jax import lax
from jax.experimental import pallas as pl
from jax.experimental.pallas import tpu as pltpu
from jax.experimental.pallas import tpu_sc as plsc

F32 = jnp.float32
BF16 = jnp.bfloat16
MESH = pl.DeviceIdType.MESH

N_DEV = 8
EPS = 1e-6
CONV_K = 3
POOL_WINDOWS = (2, 4, 8, 16)
ADAM_LR = 0.001
ADAM_B1 = 0.9
ADAM_B2 = 0.999
ADAM_EPS = 1e-08
ADAM_WD = 0.01
ADAM_STEP = 10

V7X_VMEM_LIMIT_BYTES = 56 * 1024 * 1024
LANES = 128

COLLECTIVE_GATHER = 1
COLLECTIVE_SIBLING = 2
COLLECTIVE_CHIPS = 3
SEQUENCER_COST_BYTES = 4 * 10**9

NN = ((1,), (0,))
NT = ((1,), (1,))
TN = ((0,), (0,))


def _dot(a, b, dims):
    return lax.dot_general(a, b, (dims, ((), ())), preferred_element_type=F32)


def _cp(n_axes):
    return pltpu.CompilerParams(dimension_semantics=("arbitrary",) * n_axes,
                                vmem_limit_bytes=V7X_VMEM_LIMIT_BYTES)


def _row_tile(rows, bytes_per_row, cap_bytes):
    best = None
    for t in range(16, rows + 1, 16):
        if rows % t == 0 and t * bytes_per_row <= cap_bytes:
            best = t
    return best if best is not None else rows


def _chunks(total, size):
    size = min(size, total)
    assert total % size == 0
    return [slice(r, r + size) for r in range(0, total, size)]


def _after_specs(after):
    return [pl.BlockSpec(memory_space=pl.ANY)] * len(after)


def _shift_down(v, k):
    row = lax.broadcasted_iota(jnp.int32, v.shape, 0)
    return jnp.where(row >= k, pltpu.roll(v, k, 0), 0.0)


def _shift_up(v, k):
    n = v.shape[0]
    row = lax.broadcasted_iota(jnp.int32, v.shape, 0)
    return jnp.where(row < n - k, pltpu.roll(v, n - k, 0), 0.0)


def _sigmoid(v):
    return jax.nn.sigmoid(v)


def _cast_bf16(w2d, name):
    rows, cols = w2d.shape
    tr = _row_tile(rows, cols * 4, 2 << 20)

    def body(i_ref, o_ref):
        o_ref[...] = i_ref[...].astype(BF16)

    return pl.pallas_call(
        body, name=name, grid=(rows // tr,),
        in_specs=[pl.BlockSpec((tr, cols), lambda i: (i, 0))],
        out_specs=pl.BlockSpec((tr, cols), lambda i: (i, 0)),
        out_shape=jax.ShapeDtypeStruct((rows, cols), BF16),
        compiler_params=_cp(1),
    )(w2d)


def _rms_fwd(x2d, g):
    s, d = x2d.shape
    tm = min(256, s)

    def body(x_ref, g_ref, h_ref):
        xv = x_ref[...]
        r = lax.rsqrt(jnp.mean(xv * xv, axis=-1, keepdims=True) + EPS)
        h_ref[...] = (xv * r * g_ref[...]).astype(BF16)

    return pl.pallas_call(
        body, name="rms1_fwd", grid=(s // tm,),
        in_specs=[pl.BlockSpec((tm, d), lambda i: (i, 0)), pl.BlockSpec((1, d), lambda i: (0, 0))],
        out_specs=pl.BlockSpec((tm, d), lambda i: (i, 0)),
        out_shape=jax.ShapeDtypeStruct((s, d), BF16),
        compiler_params=_cp(1),
    )(x2d, g)


def _coords():
    return lax.axis_index("x"), lax.axis_index("y"), lax.axis_index("c")


def _slot(p):
    return 4 * p[0] + 2 * p[1] + p[2]


def _handshake(peers):
    barrier = pltpu.get_barrier_semaphore()
    for peer in peers:
        pl.semaphore_signal(barrier, inc=1, device_id=peer, device_id_type=MESH)
    pl.semaphore_wait(barrier, len(peers))


def _sequencer_call(body, out_type, scratch_types, name, collective_id):
    return pl.kernel(
        body, out_type=out_type, name=name,
        mesh=plsc.ScalarSubcoreMesh(axis_name="seq", num_cores=1),
        scratch_types=scratch_types,
        cost_estimate=pl.CostEstimate(flops=0, transcendentals=0, bytes_accessed=SEQUENCER_COST_BYTES),
        compiler_params=pltpu.CompilerParams(collective_id=collective_id))


def _allgather_big(shards, name, collective_id, after=()):
    n = len(shards)

    def body(*refs):
        ins, outs = refs[:n], refs[n + len(after):2 * n + len(after)]
        send_sems, recv_sems, local_sems = refs[2 * n + len(after):]
        x, y, c = _coords()
        me, sibling = (x, y, c), (x, y, 1 - c)
        x_nbr, y_nbr, diag = (1 - x, y), (x, 1 - y), (1 - x, 1 - y)
        relay_from = (x + (1 - c) * (1 - 2 * x), y + c * (1 - 2 * y))
        relay_to = (x + c * (1 - 2 * x), y + (1 - c) * (1 - 2 * y))
        _handshake([sibling, (*x_nbr, c), (*y_nbr, c)])

        def copy(a, k, block, to, src=None):
            dst = outs[a].at[_slot(block)]
            return pltpu.make_async_remote_copy(
                src_ref=dst if src is None else src, dst_ref=dst,
                send_sem=send_sems.at[a, k], recv_sem=recv_sems.at[a, k],
                device_id=to, device_id_type=MESH)

        mine, sends = [], []
        for a in range(n):
            cp = pltpu.make_async_copy(ins[a], outs[a].at[_slot(me)], local_sems.at[a])
            cp.start()
            mine.append(cp)
            first = [copy(a, 0, me, sibling, src=ins[a]),
                     copy(a, 1, me, (*x_nbr, c), src=ins[a]),
                     copy(a, 2, me, (*y_nbr, c), src=ins[a])]
            for cp in first:
                cp.start()
            sends += first
        for a in range(n):
            copy(a, 1 + c, (*relay_from, c), me).wait_recv()
            passed = [copy(a, 3, (*relay_from, c), (*relay_to, c)), copy(a, 4 + c, (*relay_from, c), sibling)]
            for cp in passed:
                cp.start()
            copy(a, 2 - c, (*relay_to, c), me).wait_recv()
            cp = copy(a, 5 - c, (*relay_to, c), sibling)
            cp.start()
            passed.append(cp)
            copy(a, 3, (*diag, c), me).wait_recv()
            cp = copy(a, 6, (*diag, c), sibling)
            cp.start()
            sends += passed + [cp]
        for a in range(n):
            copy(a, 0, sibling, me).wait_recv()
            copy(a, 4, (*x_nbr, 1 - c), me).wait_recv()
            copy(a, 5, (*y_nbr, 1 - c), me).wait_recv()
            copy(a, 6, (*diag, 1 - c), me).wait_recv()
        for cp in sends:
            cp.wait_send()
        for cp in mine:
            cp.wait()

    return _sequencer_call(
        body, [jax.ShapeDtypeStruct((N_DEV,) + s.shape, s.dtype) for s in shards],
        [pltpu.SemaphoreType.DMA((n, 7)), pltpu.SemaphoreType.DMA((n, 7)), pltpu.SemaphoreType.DMA((n,))],
        name, collective_id)(*shards, *after)


def _sibling_copies(ins, recvs, send_sems, recv_sems):
    x, y, c = _coords()
    return [pltpu.make_async_remote_copy(
        src_ref=ins[a].at[2 * q + (1 - c)], dst_ref=recvs[a].at[q],
        send_sem=send_sems.at[a, q], recv_sem=recv_sems.at[a, q],
        device_id=(x, y, 1 - c), device_id_type=MESH) for a in range(len(ins)) for q in range(4)]


def _carry_specs(carry):
    any_spec = pl.BlockSpec(memory_space=pl.ANY)
    n = len(carry)
    sems = [pltpu.SemaphoreType.DMA((n, 4)), pltpu.SemaphoreType.DMA((n, 4))] if n else []
    return ([any_spec] * n, [any_spec] * n,
            [jax.ShapeDtypeStruct((4,) + g.shape[1:], g.dtype) for g in carry], sems)


def _carry_run(first, last, ins, recvs, sems):
    if not ins:
        return

    @pl.when(first)
    def _():
        x, y, c = _coords()
        _handshake([(x, y, 1 - c)])
        for cp in _sibling_copies(ins, recvs, *sems):
            cp.start()

    @pl.when(last)
    def _():
        copies = _sibling_copies(ins, recvs, *sems)
        for cp in copies:
            cp.wait_recv()
        for cp in copies:
            cp.wait_send()


def _cp_carry(n_axes, carry):
    if not carry:
        return _cp(n_axes)
    return pltpu.CompilerParams(dimension_semantics=("arbitrary",) * n_axes, vmem_limit_bytes=V7X_VMEM_LIMIT_BYTES,
                                collective_id=COLLECTIVE_SIBLING)


def _exchange_chips(psums, name, collective_id):
    n = len(psums)

    def body(*refs):
        ins, outs = refs[:n], refs[n:2 * n]
        send_sems, recv_sems = refs[2 * n:]
        x, y, c = _coords()
        chips = [(1 - x, y), (x, 1 - y), (1 - x, 1 - y)]
        _handshake([(*chip, c) for chip in chips])
        copies = []
        for a in range(n):
            for j, chip in enumerate(chips):
                cp = pltpu.make_async_remote_copy(
                    src_ref=ins[a].at[2 * chip[0] + chip[1]], dst_ref=outs[a].at[j],
                    send_sem=send_sems.at[a, j], recv_sem=recv_sems.at[a, j],
                    device_id=(*chip, c), device_id_type=MESH)
                cp.start()
                copies.append(cp)
        for cp in copies:
            cp.wait_recv()
        for cp in copies:
            cp.wait_send()

    return _sequencer_call(
        body, [jax.ShapeDtypeStruct((3,) + p.shape[1:], p.dtype) for p in psums],
        [pltpu.SemaphoreType.DMA((n, 3)), pltpu.SemaphoreType.DMA((n, 3))],
        name, collective_id)(*psums)


def _allgather_small(v2d, name):
    rows, cols = v2d.shape

    def body(v_ref, out_ref, send_sems, recv_sems):
        x, y, c = _coords()
        me = (x, y, c)
        out_ref[_slot(me)] = v_ref[...]
        peers = []
        for k in range(1, N_DEV):
            fx, fy, fc = (k >> 2) & 1, (k >> 1) & 1, k & 1
            peers.append(((1 - x) if fx else x, (1 - y) if fy else y, (1 - c) if fc else c))
        sends = []
        for k, peer in enumerate(peers):
            cp = pltpu.make_async_remote_copy(
                src_ref=v_ref, dst_ref=out_ref.at[_slot(me)],
                send_sem=send_sems.at[k], recv_sem=recv_sems.at[k],
                device_id=peer, device_id_type=MESH)
            cp.start()
            sends.append(cp)
        for k, peer in enumerate(peers):
            pltpu.make_async_remote_copy(
                src_ref=v_ref, dst_ref=out_ref.at[_slot(peer)],
                send_sem=send_sems.at[k], recv_sem=recv_sems.at[k],
                device_id=peer, device_id_type=MESH).wait_recv()
        for cp in sends:
            cp.wait_send()

    vmem = pl.BlockSpec(memory_space=pltpu.VMEM)
    return pl.pallas_call(
        body, name=name, in_specs=[vmem], out_specs=vmem,
        out_shape=jax.ShapeDtypeStruct((N_DEV, rows, cols), v2d.dtype),
        scratch_shapes=[pltpu.SemaphoreType.DMA((N_DEV - 1,)), pltpu.SemaphoreType.DMA((N_DEV - 1,))],
    )(v2d)


def _chip_partial(others, grads, recvs, name):
    n = len(grads)
    _, rows, cols = grads[0].shape
    tr = _row_tile(rows, cols * 2, (2 << 20) // n)

    def body(others_ref, *refs):
        for a in range(n):
            refs[2 * n + a][...] = (refs[a][...].astype(F32) + refs[n + a][...].astype(F32)).astype(BF16)

    return pl.pallas_call(
        body, name=name,
        grid_spec=pltpu.PrefetchScalarGridSpec(
            num_scalar_prefetch=1, grid=(3, rows // tr),
            in_specs=[pl.BlockSpec((None, tr, cols), lambda k, i, o: (o[3 + k], i, 0))] * n
            + [pl.BlockSpec((None, tr, cols), lambda k, i, o: (o[k], i, 0))] * n,
            out_specs=[pl.BlockSpec((None, tr, cols), lambda k, i, o: (o[k], i, 0))] * n),
        out_shape=[jax.ShapeDtypeStruct((4, rows, cols), BF16)] * n,
        compiler_params=_cp(2),
    )(others, *grads, *recvs)


def _adam_math(w, g, m, v):
    m = ADAM_B1 * m + (1.0 - ADAM_B1) * g
    v = ADAM_B2 * v + (1.0 - ADAM_B2) * (g * g)
    m_hat = m / (1.0 - ADAM_B1 ** ADAM_STEP)
    v_hat = v / (1.0 - ADAM_B2 ** ADAM_STEP)
    delta = -ADAM_LR * (m_hat / (jnp.sqrt(v_hat) + ADAM_EPS) + ADAM_WD * w)
    return delta, m, v


def _adam_big(own, wmvs, g3s, recv_sibs, recv_chipss, name):
    n = len(wmvs)
    rows, cols = wmvs[0][0].shape
    tr = _row_tile(rows, cols * 4, (2 << 20) // n)

    def body(own_ref, *refs):
        ins, outs = refs[:6 * n], refs[6 * n:]
        for a in range(n):
            w_ref, m_ref, v_ref, g_ref, rs_ref, rc_ref = ins[6 * a:6 * a + 6]
            g = g_ref[...].astype(F32) + rs_ref[...].astype(F32)
            g = g + rc_ref[0].astype(F32)
            g = g + rc_ref[1].astype(F32)
            g = g + rc_ref[2].astype(F32)
            delta, m_new, v_new = _adam_math(w_ref[...], g, m_ref[...], v_ref[...])
            outs[4 * a][...] = g
            outs[4 * a + 1][...] = delta
            outs[4 * a + 2][...] = m_new
            outs[4 * a + 3][...] = v_new

    blk = pl.BlockSpec((tr, cols), lambda i, o: (i, 0))
    per_shard = [blk, blk, blk,
                 pl.BlockSpec((None, tr, cols), lambda i, o: (o[0], i, 0)),
                 pl.BlockSpec((None, tr, cols), lambda i, o: (o[1], i, 0)),
                 pl.BlockSpec((3, tr, cols), lambda i, o: (0, i, 0))]
    out = jax.ShapeDtypeStruct((rows, cols), F32)
    args = [t for a in range(n) for t in (*wmvs[a], g3s[a], recv_sibs[a], recv_chipss[a])]
    outs = pl.pallas_call(
        body, name=name,
        grid_spec=pltpu.PrefetchScalarGridSpec(
            num_scalar_prefetch=1, grid=(rows // tr,),
            in_specs=per_shard * n, out_specs=[blk] * (4 * n)),
        out_shape=[out] * (4 * n),
        compiler_params=_cp(1),
    )(own, *args)
    return [outs[4 * a:4 * a + 4] for a in range(n)]


def _small_finish(gathered, params, row_offs, extra_rows):
    n = len(params)

    def body(g_ref, *refs):
        ins, outs = refs[:3 * n], refs[3 * n:]
        total = g_ref[0]
        for k in range(1, N_DEV):
            total = total + g_ref[k]
        for e, (r0, nr) in enumerate(extra_rows):
            outs[e][...] = total[r0:r0 + nr, :]
        for p in range(n):
            w_ref, m_ref, v_ref = ins[3 * p:3 * p + 3]
            g_out, d_out, m_out, v_out = outs[len(extra_rows) + 4 * p:len(extra_rows) + 4 * p + 4]
            g = total[row_offs[p]:row_offs[p] + w_ref.shape[0], :]
            delta, m_new, v_new = _adam_math(w_ref[...], g, m_ref[...], v_ref[...])
            g_out[...] = g
            d_out[...] = delta
            m_out[...] = m_new
            v_out[...] = v_new

    vmem = pl.BlockSpec(memory_space=pltpu.VMEM)
    out_shape = [jax.ShapeDtypeStruct((nr, LANES), F32) for _, nr in extra_rows]
    for w, _, _ in params:
        out_shape += [jax.ShapeDtypeStruct(w.shape, F32)] * 4
    flat = [t for wmv in params for t in wmv]
    return pl.pallas_call(body, name="small_finish", in_specs=[vmem] * (1 + len(flat)),
                          out_specs=[vmem] * len(out_shape), out_shape=out_shape)(gathered, *flat)


def _adam_small(w, g, m, v):
    def body(w_ref, g_ref, m_ref, v_ref, do_ref, mo_ref, vo_ref):
        delta, m_new, v_new = _adam_math(w_ref[...], g_ref[...], m_ref[...], v_ref[...])
        do_ref[...] = delta
        mo_ref[...] = m_new
        vo_ref[...] = v_new

    vmem = pl.BlockSpec(memory_space=pltpu.VMEM)
    out = jax.ShapeDtypeStruct(w.shape, F32)
    return pl.pallas_call(body, name="adam_small", in_specs=[vmem] * 4, out_specs=[vmem] * 3,
                          out_shape=[out, out, out])(w, g, m, v)


def _proj_fwd(h, win_g):
    s, d = h.shape
    sw = win_g.shape[2]
    tn = min(512, sw)
    nh = sw // tn

    def body(h_ref, w_ref, o_ref):
        for rs in _chunks(s, 512):
            o_ref[rs, :] = _dot(h_ref[rs, :], w_ref[...], NN)

    return pl.pallas_call(
        body, name="proj_fwd", grid=(N_DEV * nh,),
        in_specs=[pl.BlockSpec((s, d), lambda j: (0, 0)),
                  pl.BlockSpec((None, d, tn), lambda j: (j // nh, 0, j % nh))],
        out_specs=pl.BlockSpec((None, s, tn), lambda j: (j // nh, 0, j % nh)),
        out_shape=jax.ShapeDtypeStruct((N_DEV, s, sw), F32),
        compiler_params=_cp(1),
    )(h, win_g)


def _conv_fwd(proj, conv_w, conv_b):
    _, s, sw = proj.shape
    tc = min(LANES, sw)

    def body(ba_ref, ca_ref, va_ref, cw_ref, cb_ref, z_ref):
        cv = ca_ref[...] * va_ref[...]
        u = (cb_ref[...] + cw_ref[0:1, :] * _shift_down(cv, 2) + cw_ref[1:2, :] * _shift_down(cv, 1)
             + cw_ref[2:3, :] * cv)
        z_ref[...] = (ba_ref[...] * u).astype(BF16)

    def part(k):
        return pl.BlockSpec((None, s, tc), lambda i: (k, 0, i))

    return pl.pallas_call(
        body, name="conv_fwd", grid=(sw // tc,),
        in_specs=[part(0), part(1), part(2),
                  pl.BlockSpec((CONV_K, tc), lambda i: (0, i)), pl.BlockSpec((1, tc), lambda i: (0, i))],
        out_specs=pl.BlockSpec((s, tc), lambda i: (0, i)),
        out_shape=jax.ShapeDtypeStruct((s, sw), BF16),
        compiler_params=_cp(1),
    )(proj, proj, proj, conv_w, conv_b)


def _pool_counts(shape, window):
    t = lax.broadcasted_iota(jnp.int32, shape, 0)
    return jnp.minimum(t + 1, window).astype(F32)


def _pool_fwd(proj):
    _, s, sw = proj.shape
    gw = sw // len(POOL_WINDOWS)

    def body(v_ref, p_ref):
        for gi, window in enumerate(POOL_WINDOWS):
            @pl.when(pl.program_id(0) == gi)
            def _():
                v = v_ref[...]
                acc, k = v, 1
                while k < window:
                    acc = acc + _shift_down(acc, k)
                    k *= 2
                p_ref[...] = (acc / _pool_counts(v.shape, window) - v).astype(BF16)

    return pl.pallas_call(
        body, name="pool_fwd", grid=(len(POOL_WINDOWS),),
        in_specs=[pl.BlockSpec((None, s, gw), lambda g: (3, 0, g))],
        out_specs=pl.BlockSpec((s, gw), lambda g: (0, g)),
        out_shape=jax.ShapeDtypeStruct((s, sw), BF16),
        compiler_params=_cp(1),
    )(proj)


def _merge_fwd(z, wa, p, wpool, proj, b_gate2, pool_scale):
    s, sw = z.shape
    tn = wa.shape[2]
    d = tn * N_DEV
    gw = sw // len(POOL_WINDOWS)
    nq = sw // tn

    def body(z_ref, wa_ref, p_ref, wp_ref, ga_ref, gb_ref, bg_ref, sc_ref,
             m_ref, dya_ref, dyb_ref, dga_ref, dgb_ref, dsc_ref):
        for rs in _chunks(s, 512):
            ya = _dot(z_ref[rs, :], wa_ref[...], NN)
            yb = _dot(p_ref[rs, :], wp_ref[...], NN)
            sa = _sigmoid(ga_ref[rs, :] + bg_ref[0:1, :])
            sb = _sigmoid(gb_ref[rs, :] + bg_ref[1:2, :])
            sc = sc_ref[...]
            sb_yb = sb * yb
            m_ref[rs, :] = (sa * ya + sb_yb * sc).astype(BF16)
            dya_ref[rs, :] = sa.astype(BF16)
            dyb_ref[rs, :] = (sb * sc).astype(BF16)
            dga_ref[rs, :] = (ya * (sa * (1.0 - sa))).astype(BF16)
            dgb_ref[rs, :] = ((yb * sc) * (sb * (1.0 - sb))).astype(BF16)
            dsc_ref[rs, :] = sb_yb.astype(BF16)

    col = pl.BlockSpec((s, tn), lambda j: (0, j))
    out = jax.ShapeDtypeStruct((s, d), BF16)
    return pl.pallas_call(
        body, name="merge_fwd", grid=(N_DEV,),
        in_specs=[pl.BlockSpec((s, sw), lambda j: (0, 0)),
                  pl.BlockSpec((None, sw, tn), lambda j: (j, 0, 0)),
                  pl.BlockSpec((s, gw), lambda j: (0, j // 2)),
                  pl.BlockSpec((None, gw, tn), lambda j: (j // 2, 0, j % 2)),
                  pl.BlockSpec((None, s, tn), lambda j: (4 + j // nq, 0, j % nq)),
                  pl.BlockSpec((None, s, tn), lambda j: (6 + j // nq, 0, j % nq)),
                  pl.BlockSpec((2, tn), lambda j: (0, j)),
                  pl.BlockSpec((1, tn), lambda j: (0, j))],
        out_specs=[col] * 6,
        out_shape=[out] * 6,
        compiler_params=_cp(1),
    )(z, wa, p, wpool, proj, proj, b_gate2, pool_scale)


def _wo_fwd(merged, wo, x2d, g2):
    s, d = x2d.shape
    tm = min(256, s)

    def body(m_ref, wo_ref, x_ref, g_ref, x1_ref, h2_ref):
        x1 = x_ref[...] + _dot(m_ref[...], wo_ref[...], NN)
        x1_ref[...] = x1
        r = lax.rsqrt(jnp.mean(x1 * x1, axis=-1, keepdims=True) + EPS)
        h2_ref[...] = (x1 * r * g_ref[...]).astype(BF16)

    row = pl.BlockSpec((tm, d), lambda i: (i, 0))
    return pl.pallas_call(
        body, name="wo_fwd", grid=(s // tm,),
        in_specs=[row, pl.BlockSpec((d, d), lambda i: (0, 0)), row, pl.BlockSpec((1, d), lambda i: (0, 0))],
        out_specs=[row, row],
        out_shape=[jax.ShapeDtypeStruct((s, d), F32), jax.ShapeDtypeStruct((s, d), BF16)],
        compiler_params=_cp(1),
    )(merged, wo, x2d, g2)


def _ffn_up_act_fwd(h2, wg_g, wu_g):
    s, d = h2.shape
    f8 = wg_g.shape[2]
    th = min(1024, s)

    def body(h_ref, wg_ref, wu_ref, dadu_ref, dadg_ref, a_ref):
        i = pl.program_id(1)
        for rs in _chunks(th, 512):
            rows = pl.ds(pl.multiple_of(i * th + rs.start, rs.stop - rs.start), rs.stop - rs.start)
            a = h_ref[rows, :]
            g = _dot(a, wg_ref[...], NN)
            u = _dot(a, wu_ref[...], NN)
            sg = _sigmoid(g)
            silu = g * sg
            dadu_ref[rs, :] = silu.astype(BF16)
            dadg_ref[rs, :] = (u * (sg * (1.0 + g * (1.0 - sg)))).astype(BF16)
            a_ref[rs, :] = (silu * u).astype(BF16)

    wspec = pl.BlockSpec((None, d, f8), lambda j, i: (j, 0, 0))
    ospec = pl.BlockSpec((None, th, f8), lambda j, i: (j, i, 0))
    out = jax.ShapeDtypeStruct((N_DEV, s, f8), BF16)
    return pl.pallas_call(
        body, name="ffn_up_fwd", grid=(N_DEV, s // th),
        in_specs=[pl.BlockSpec((s, d), lambda j, i: (0, 0)), wspec, wspec],
        out_specs=[ospec, ospec, ospec], out_shape=[out, out, out],
        compiler_params=_cp(2),
    )(h2, wg_g, wu_g)


def _ffn_down_fwd(act, wd_g):
    _, s, f8 = act.shape
    d = wd_g.shape[2]
    tn = min(1024, d)
    per = 2

    def body(a_ref, wd_ref, o_ref):
        j = pl.program_id(1)

        @pl.when(j == 0)
        def _():
            o_ref[...] = jnp.zeros_like(o_ref)

        for rs in _chunks(s, 1024):
            part = _dot(a_ref[0, rs, :], wd_ref[0], NN)
            for q in range(1, per):
                part = part + _dot(a_ref[q, rs, :], wd_ref[q], NN)
            o_ref[rs, :] += part

    return pl.pallas_call(
        body, name="ffn_down_fwd", grid=(d // tn, N_DEV // per),
        in_specs=[pl.BlockSpec((per, s, f8), lambda n, j: (j, 0, 0)),
                  pl.BlockSpec((per, f8, tn), lambda n, j: (j, 0, n))],
        out_specs=pl.BlockSpec((s, tn), lambda n, j: (0, n)),
        out_shape=jax.ShapeDtypeStruct((s, d), F32),
        compiler_params=_cp(2),
    )(act, wd_g)


def _loss_bwd(ffn_out, x1, target, final_g):
    s, d = x1.shape
    tm = min(256, s)

    def body(f_ref, x1_ref, t_ref, gf_ref, dxb_ref, dgf_ref, loss_ref):
        @pl.when(pl.program_id(0) == 0)
        def _():
            dgf_ref[...] = jnp.zeros_like(dgf_ref)
            loss_ref[...] = jnp.zeros_like(loss_ref)

        x2 = x1_ref[...] + f_ref[...]
        r = lax.rsqrt(jnp.mean(x2 * x2, axis=-1, keepdims=True) + EPS)
        nrm = x2 * r
        gf = gf_ref[...]
        err = nrm * gf - t_ref[...]
        loss_ref[...] += jnp.sum(err * err) * (0.5 / d)
        dy = err * (1.0 / d)
        dgf_ref[...] += jnp.sum(dy * nrm, axis=0, keepdims=True)
        dn = dy * gf
        dx = r * (dn - nrm * jnp.mean(dn * nrm, axis=-1, keepdims=True))
        dxb_ref[...] = dx.astype(BF16)

    row = pl.BlockSpec((tm, d), lambda i: (i, 0))
    vec = pl.BlockSpec((1, d), lambda i: (0, 0))
    return pl.pallas_call(
        body, name="loss_bwd", grid=(s // tm,),
        in_specs=[row, row, row, vec],
        out_specs=[row, vec, pl.BlockSpec((8, LANES), lambda i: (0, 0))],
        out_shape=[jax.ShapeDtypeStruct((s, d), BF16),
                   jax.ShapeDtypeStruct((1, d), F32), jax.ShapeDtypeStruct((8, LANES), F32)],
        compiler_params=_cp(1),
    )(ffn_out, x1, target, final_g)


def _ffn_gate_bwd(dx2b, wd_g, dadg, dadu):
    s, d = dx2b.shape
    f8 = dadg.shape[2]
    th = min(1024, s)

    def body(dx_ref, wd_ref, g_ref, u_ref, dg_ref, du_ref, da_ref):
        i = pl.program_id(1)
        chunks = _chunks(th, 256)

        def matmul(rs):
            rows = pl.ds(pl.multiple_of(i * th + rs.start, rs.stop - rs.start), rs.stop - rs.start)
            da_ref[rs, :] = _dot(dx_ref[rows, :], wd_ref[...], NT)

        matmul(chunks[0])
        for k, rs in enumerate(chunks):
            if k + 1 < len(chunks):
                matmul(chunks[k + 1])
            da = da_ref[rs, :].astype(BF16)
            dg_ref[rs, :] = da * g_ref[rs, :]
            du_ref[rs, :] = da * u_ref[rs, :]

    aspec = pl.BlockSpec((None, th, f8), lambda j, i: (j, i, 0))
    out = jax.ShapeDtypeStruct((N_DEV, s, f8), BF16)
    return pl.pallas_call(
        body, name="ffn_act_bwd", grid=(N_DEV, s // th),
        in_specs=[pl.BlockSpec((s, d), lambda j, i: (0, 0)),
                  pl.BlockSpec((None, f8, d), lambda j, i: (j, 0, 0)), aspec, aspec],
        out_specs=[aspec, aspec], out_shape=[out, out],
        scratch_shapes=[pltpu.VMEM((th, f8), F32)],
        compiler_params=_cp(2),
    )(dx2b, wd_g, dadg, dadu)


def _wgrad_rows(a3, b, name, after=(), carry=()):
    _, s, k = a3.shape
    n = b.shape[1]
    nc = len(carry)
    c_in, c_out, c_shape, c_sems = _carry_specs(carry)

    def body(a_ref, b_ref, *rest):
        rest = rest[len(after):]
        o_ref = rest[nc]
        j = pl.program_id(0)
        _carry_run(j == 0, j == N_DEV - 1, rest[:nc], rest[nc + 1:2 * nc + 1], rest[2 * nc + 1:])
        o_ref[...] = _dot(a_ref[...], b_ref[...], TN).astype(BF16)

    outs = pl.pallas_call(
        body, name=name, grid=(N_DEV,),
        in_specs=[pl.BlockSpec((None, s, k), lambda j: (j, 0, 0)),
                  pl.BlockSpec((s, n), lambda j: (0, 0))] + _after_specs(after) + c_in,
        out_specs=[pl.BlockSpec((None, k, n), lambda j: (j, 0, 0))] + c_out,
        out_shape=[jax.ShapeDtypeStruct((N_DEV, k, n), BF16)] + c_shape,
        scratch_shapes=c_sems,
        compiler_params=_cp_carry(1, carry),
    )(a3, b, *after, *carry)
    return (outs[0], list(outs[1:])) if nc else outs[0]


def _wgrad_cols(a, b3, name, after=()):
    s, k = a.shape
    if b3.ndim == 2:
        n = b3.shape[1] // N_DEV
        b_spec = pl.BlockSpec((s, n), lambda j: (0, j))
    else:
        n = b3.shape[2]
        b_spec = pl.BlockSpec((None, s, n), lambda j: (j, 0, 0))

    def body(a_ref, b_ref, *rest):
        o_ref = rest[len(after)]
        o_ref[...] = _dot(a_ref[...], b_ref[...], TN).astype(BF16)

    return pl.pallas_call(
        body, name=name, grid=(N_DEV,),
        in_specs=[pl.BlockSpec((s, k), lambda j: (0, 0)), b_spec] + _after_specs(after),
        out_specs=pl.BlockSpec((None, k, n), lambda j: (j, 0, 0)),
        out_shape=jax.ShapeDtypeStruct((N_DEV, k, n), BF16),
        compiler_params=_cp(1),
    )(a, b3, *after)


def _input_grad(pairs, name, after=(), carry=(), per=1):
    s = pairs[0][0].shape[1]
    d = pairs[0][1].shape[1]
    tn = min(1024, d)
    npair = len(pairs)
    nc = len(carry)
    c_in, c_out, c_shape, c_sems = _carry_specs(carry)

    def body(*refs):
        ops = refs[:2 * npair]
        rest = refs[2 * npair + len(after):]
        o_ref, acc_ref = rest[nc], rest[-1]
        nh, j = pl.program_id(0), pl.program_id(1)
        last_j = N_DEV // per - 1
        _carry_run((nh == 0) & (j == 0), (nh == d // tn - 1) & (j == last_j),
                   rest[:nc], rest[nc + 1:2 * nc + 1], rest[2 * nc + 1:-1])

        @pl.when(j == 0)
        def _():
            acc_ref[...] = jnp.zeros_like(acc_ref)

        for rs in _chunks(s, 1024):
            part = None
            for q in range(npair):
                for e in range(per):
                    term = _dot(ops[2 * q][e, rs, :], ops[2 * q + 1][e], NT)
                    part = term if part is None else part + term
            acc_ref[rs, :] += part

        @pl.when(j == last_j)
        def _():
            o_ref[...] = acc_ref[...].astype(BF16)

    in_specs, args = [], []
    for a3, w3 in pairs:
        k = a3.shape[2]
        in_specs += [pl.BlockSpec((per, s, k), lambda n, j: (j, 0, 0)),
                     pl.BlockSpec((per, tn, k), lambda n, j: (j, n, 0))]
        args += [a3, w3]
    outs = pl.pallas_call(
        body, name=name, grid=(d // tn, N_DEV // per),
        in_specs=in_specs + _after_specs(after) + c_in,
        out_specs=[pl.BlockSpec((s, tn), lambda n, j: (0, n))] + c_out,
        out_shape=[jax.ShapeDtypeStruct((s, d), BF16)] + c_shape,
        scratch_shapes=c_sems + [pltpu.VMEM((s, tn), F32)],
        compiler_params=_cp_carry(2, carry),
    )(*args, *after, *carry)
    return (outs[0], list(outs[1:])) if nc else outs[0]


def _rms_bwd(dh, xres, g, dres, name, with_bf16=True):
    s, d = xres.shape
    tm = min(256, s)

    def body(dh_ref, x_ref, g_ref, dres_ref, dx_ref, *rest):
        dg_ref = rest[-1]
        @pl.when(pl.program_id(0) == 0)
        def _():
            dg_ref[...] = jnp.zeros_like(dg_ref)

        xv = x_ref[...]
        dh_v = dh_ref[...].astype(F32)
        r = lax.rsqrt(jnp.mean(xv * xv, axis=-1, keepdims=True) + EPS)
        nrm = xv * r
        dg_ref[...] += jnp.sum(dh_v * nrm, axis=0, keepdims=True)
        dn = dh_v * g_ref[...]
        dx = dres_ref[...].astype(F32) + r * (dn - nrm * jnp.mean(dn * nrm, axis=-1, keepdims=True))
        dx_ref[...] = dx
        if with_bf16:
            rest[0][...] = dx.astype(BF16)

    row = pl.BlockSpec((tm, d), lambda i: (i, 0))
    vec = pl.BlockSpec((1, d), lambda i: (0, 0))
    copies = [jax.ShapeDtypeStruct((s, d), BF16)] if with_bf16 else []
    outs = pl.pallas_call(
        body, name=name, grid=(s // tm,),
        in_specs=[row, row, vec, row],
        out_specs=[row] + [row] * len(copies) + [vec],
        out_shape=[jax.ShapeDtypeStruct((s, d), F32)] + copies + [jax.ShapeDtypeStruct((1, d), F32)],
        compiler_params=_cp(1),
    )(dh, xres, g, dres)
    return (outs[0], outs[1], outs[2]) if with_bf16 else (outs[0], None, outs[1])


def _wgrad_full(a, b, name, after=(), carry=()):
    s, k = a.shape
    n = b.shape[1]
    tk = min(512, k)
    nc = len(carry)
    c_in, c_out, c_shape, c_sems = _carry_specs(carry)

    def body(a_ref, b_ref, *rest):
        rest = rest[len(after):]
        o_ref = rest[nc]
        j = pl.program_id(0)
        _carry_run(j == 0, j == k // tk - 1, rest[:nc], rest[nc + 1:2 * nc + 1], rest[2 * nc + 1:])
        o_ref[...] = _dot(a_ref[...], b_ref[...], TN).astype(BF16)

    outs = pl.pallas_call(
        body, name=name, grid=(k // tk,),
        in_specs=[pl.BlockSpec((s, tk), lambda j: (0, j)),
                  pl.BlockSpec((s, n), lambda j: (0, 0))] + _after_specs(after) + c_in,
        out_specs=[pl.BlockSpec((tk, n), lambda j: (j, 0))] + c_out,
        out_shape=[jax.ShapeDtypeStruct((k, n), BF16)] + c_shape,
        scratch_shapes=c_sems,
        compiler_params=_cp_carry(1, carry),
    )(a, b, *after, *carry)
    return (outs[0], list(outs[1:])) if nc else outs[0]


def _wgrad_pool(p, dyb, n_groups):
    s, sw = p.shape
    d = dyb.shape[1]
    gw, go = sw // n_groups, d // n_groups
    ts = min(512, s)
    ns = s // ts

    def body(a_ref, b_ref, o_ref, acc_ref):
        i = pl.program_id(1)

        @pl.when(i == 0)
        def _():
            acc_ref[...] = jnp.zeros_like(acc_ref)

        acc_ref[...] += _dot(a_ref[...], b_ref[...], TN)

        @pl.when(i == ns - 1)
        def _():
            o_ref[...] = acc_ref[...].astype(BF16)

    return pl.pallas_call(
        body, name="wgrad_pool", grid=(n_groups, ns),
        in_specs=[pl.BlockSpec((ts, gw), lambda g, i: (i, g)),
                  pl.BlockSpec((ts, go), lambda g, i: (i, g))],
        out_specs=pl.BlockSpec((None, gw, go), lambda g, i: (g, 0, 0)),
        out_shape=jax.ShapeDtypeStruct((n_groups, gw, go), BF16),
        scratch_shapes=[pltpu.VMEM((gw, go), F32)],
        compiler_params=_cp(2),
    )(p, dyb)


def _wo_bwd(dx1b, wo, factors, sw, after=()):
    s, d = dx1b.shape
    tn = d // N_DEV
    nq = sw // tn

    def body(dx_ref, wo_ref, fya_ref, fyb_ref, fga_ref, fgb_ref, fsc_ref, *rest):
        dya_ref, dyb_ref, dp_ref, dbg_ref, dsc_ref, dm_ref = rest[len(after):]
        dbg_ref[...] = jnp.zeros_like(dbg_ref)
        dsc_ref[...] = jnp.zeros_like(dsc_ref)
        for rs in _chunks(s, 1024):
            dm_ref[rs, :] = _dot(dx_ref[rs, :], wo_ref[...], NT)
        for rs in _chunks(s, 256):
            dm = dm_ref[rs, :]
            dya_ref[rs, :] = (dm * fya_ref[rs, :].astype(F32)).astype(BF16)
            dyb_ref[rs, :] = (dm * fyb_ref[rs, :].astype(F32)).astype(BF16)
            dsc_ref[...] += jnp.sum(dm * fsc_ref[rs, :].astype(F32), axis=0, keepdims=True)
            dga = dm * fga_ref[rs, :].astype(F32)
            dgb = dm * fgb_ref[rs, :].astype(F32)
            dp_ref[0, rs, :] = dga.astype(BF16)
            dp_ref[1, rs, :] = dgb.astype(BF16)
            dbg_ref[0:1, :] += jnp.sum(dga, axis=0, keepdims=True)
            dbg_ref[1:2, :] += jnp.sum(dgb, axis=0, keepdims=True)

    col = pl.BlockSpec((s, tn), lambda j: (0, j))
    out = jax.ShapeDtypeStruct((s, d), BF16)
    return pl.pallas_call(
        body, name="wo_bwd", grid=(N_DEV,),
        in_specs=[pl.BlockSpec((s, d), lambda j: (0, 0)),
                  pl.BlockSpec((tn, d), lambda j: (j, 0))] + [col] * 5 + _after_specs(after),
        out_specs=[col, col,
                   pl.BlockSpec((2, None, s, tn), lambda j: (1, j // nq, 0, j % nq)),
                   pl.BlockSpec((2, tn), lambda j: (0, j)),
                   pl.BlockSpec((1, tn), lambda j: (0, j))],
        out_shape=[out, out, jax.ShapeDtypeStruct((4, 2, s, sw), BF16),
                   jax.ShapeDtypeStruct((2, d), F32), jax.ShapeDtypeStruct((1, d), F32)],
        scratch_shapes=[pltpu.VMEM((s, tn), F32)],
        compiler_params=_cp(1),
    )(dx1b, wo, *factors, *after)


def _conv_bwd(dproj, dya, wa, proj, conv_w, conv_b):
    s, d = dya.shape
    sw, tn = wa.shape[1], wa.shape[2]
    tc = min(LANES, sw)

    def body(dproj_hbm, dya_ref, wa_ref, ba_ref, ca_ref, va_ref, cw_ref, cb_ref,
             dp_ref, dcw_ref, dcb_ref, dz_ref):
        del dproj_hbm
        for rs in _chunks(s, 512):
            part = _dot(dya_ref[rs, 0:tn], wa_ref[0], NT)
            for j in range(1, N_DEV):
                part = part + _dot(dya_ref[rs, j * tn:(j + 1) * tn], wa_ref[j], NT)
            dz_ref[rs, :] = part
        dz = dz_ref[...]
        ba, ca, va = ba_ref[...], ca_ref[...], va_ref[...]
        cv = ca * va
        cv1, cv2 = _shift_down(cv, 1), _shift_down(cv, 2)
        w0, w1, w2 = cw_ref[0:1, :], cw_ref[1:2, :], cw_ref[2:3, :]
        u = cb_ref[...] + w0 * cv2 + w1 * cv1 + w2 * cv
        du = dz * ba
        dp_ref[0] = (dz * u).astype(BF16)
        dcv = w2 * du + w1 * _shift_up(du, 1) + w0 * _shift_up(du, 2)
        dp_ref[1] = (dcv * va).astype(BF16)
        dp_ref[2] = (dcv * ca).astype(BF16)
        dcw_ref[0:1, :] = jnp.sum(du * cv2, axis=0, keepdims=True)
        dcw_ref[1:2, :] = jnp.sum(du * cv1, axis=0, keepdims=True)
        dcw_ref[2:3, :] = jnp.sum(du * cv, axis=0, keepdims=True)
        dcb_ref[...] = jnp.sum(du, axis=0, keepdims=True)

    def part(k):
        return pl.BlockSpec((None, s, tc), lambda i: (k, 0, i))

    return pl.pallas_call(
        body, name="conv_bwd", grid=(sw // tc,),
        in_specs=[pl.BlockSpec(memory_space=pl.ANY),
                  pl.BlockSpec((s, d), lambda i: (0, 0)),
                  pl.BlockSpec((N_DEV, tc, tn), lambda i: (0, i, 0)),
                  part(0), part(1), part(2),
                  pl.BlockSpec((CONV_K, tc), lambda i: (0, i)), pl.BlockSpec((1, tc), lambda i: (0, i))],
        out_specs=[pl.BlockSpec((3, s, tc), lambda i: (0, 0, i)),
                   pl.BlockSpec((CONV_K, tc), lambda i: (0, i)), pl.BlockSpec((1, tc), lambda i: (0, i))],
        out_shape=[jax.ShapeDtypeStruct(dproj.shape, BF16),
                   jax.ShapeDtypeStruct((CONV_K, sw), F32), jax.ShapeDtypeStruct((1, sw), F32)],
        scratch_shapes=[pltpu.VMEM((s, tc), F32)],
        input_output_aliases={0: 0},
        compiler_params=_cp(1),
    )(dproj, dya, wa, proj, proj, proj, conv_w, conv_b)


def _pool_bwd(dproj, dyb, wpool):
    s, d = dyb.shape
    n_groups, gw, go = wpool.shape

    def body(dproj_hbm, dyb_ref, wp_ref, dp_ref):
        del dproj_hbm
        for gi, window in enumerate(POOL_WINDOWS):
            @pl.when(pl.program_id(0) == gi)
            def _():
                dpool = _dot(dyb_ref[...], wp_ref[...], NT)
                acc, k = dpool / _pool_counts(dpool.shape, window), 1
                while k < window:
                    acc = acc + _shift_up(acc, k)
                    k *= 2
                dp_ref[...] = (acc - dpool).astype(BF16)

    return pl.pallas_call(
        body, name="pool_bwd", grid=(n_groups,),
        in_specs=[pl.BlockSpec(memory_space=pl.ANY),
                  pl.BlockSpec((s, go), lambda g: (0, g)),
                  pl.BlockSpec((None, gw, go), lambda g: (g, 0, 0))],
        out_specs=pl.BlockSpec((None, s, gw), lambda g: (3, 0, g)),
        out_shape=jax.ShapeDtypeStruct(dproj.shape, BF16),
        input_output_aliases={0: 0},
        compiler_params=_cp(1),
    )(dproj, dyb, wpool)


def _rows128(v):
    return v.reshape(-1, LANES)


def kernel(x, norm1_g, w_in, b_gate, conv_w, conv_b, w_a_out, w_pool, pool_scale, w_o, norm2_g, w_ffn_gate, w_ffn_up, w_ffn_down, final_g, loss_target, m_norm1_g, m_w_in, m_b_gate, m_conv_w, m_conv_b, m_w_a_out, m_w_pool, m_pool_scale, m_w_o, m_norm2_g, m_w_ffn_gate, m_w_ffn_up, m_w_ffn_down, m_final_g, v_norm1_g, v_w_in, v_b_gate, v_conv_w, v_conv_b, v_w_a_out, v_w_pool, v_pool_scale, v_w_o, v_norm2_g, v_w_ffn_gate, v_w_ffn_up, v_w_ffn_down, v_final_g):
    s, d = x.shape[1], x.shape[2]
    sw = w_in.shape[2]
    n_groups = w_pool.shape[1]
    gw = w_pool.shape[2]
    go = w_pool.shape[3] * N_DEV
    f8 = w_ffn_gate.shape[2]
    cws = conv_w.shape[2]
    assert sw == conv_w.shape[2] * N_DEV == gw * n_groups and go * n_groups == d and n_groups == len(POOL_WINDOWS)

    xi, yi, ci = _coords()
    me = 4 * xi + 2 * yi + ci
    my_chip = 2 * xi + yi

    x2d = x.reshape(s, d)
    target = loss_target.reshape(s, d)
    final_g2 = final_g.reshape(1, d)
    b_gate2 = b_gate.reshape(2, d)

    big_names = ["w_in", "w_a_out", "w_pool", "w_o", "w_ffn_gate", "w_ffn_up", "w_ffn_down"]
    big_w = [w_in, w_a_out, w_pool, w_o, w_ffn_gate, w_ffn_up, w_ffn_down]
    big_m = [m_w_in, m_w_a_out, m_w_pool, m_w_o, m_w_ffn_gate, m_w_ffn_up, m_w_ffn_down]
    big_v = [v_w_in, v_w_a_out, v_w_pool, v_w_o, v_w_ffn_gate, v_w_ffn_up, v_w_ffn_down]
    shapes2d = [(w.size // w.shape[-1], w.shape[-1]) for w in big_w]
    big_w2 = [w.reshape(sh) for w, sh in zip(big_w, shapes2d)]
    transposed = (4, 5)

    def view2d(t, a):
        t2 = t.reshape(shapes2d[a])
        return t2.T if a in transposed else t2

    def unview(o, a):
        return (o.T if a in transposed else o).reshape(big_w[a].shape)

    sb = [_cast_bf16(w, "cast_" + nm) for w, nm in zip(big_w2, big_names)]
    win_g, wa_g, wpool_g, wo_g = _allgather_big(sb[0:4], "allgather_mixer", COLLECTIVE_GATHER)
    wg_g, wu_g = _allgather_big(sb[4:6], "allgather_ffn_up", COLLECTIVE_GATHER)
    (wd_g,) = _allgather_big(sb[6:7], "allgather_ffn_down", COLLECTIVE_GATHER)
    convw_g = _allgather_small(jnp.pad(conv_w.reshape(CONV_K, cws), ((0, 8 - CONV_K), (0, 0))), "allgather_conv_w")
    conv_w_full = convw_g[:, :CONV_K, :].transpose(1, 0, 2).reshape(CONV_K, sw)
    wpool = wpool_g.reshape(N_DEV, n_groups, gw, go // N_DEV).transpose(1, 2, 0, 3).reshape(n_groups, gw, go)
    wo = wo_g.reshape(d, d)

    h = _rms_fwd(x2d, norm1_g)
    proj = _proj_fwd(h, win_g)
    z = _conv_fwd(proj, conv_w_full, conv_b)
    p = _pool_fwd(proj)
    merged, *merge_factors = _merge_fwd(z, wa_g, p, wpool, proj, b_gate2, pool_scale)
    x1, h2 = _wo_fwd(merged, wo, x2d, norm2_g)
    dadu, dadg, act = _ffn_up_act_fwd(h2, wg_g, wu_g)
    ffn_out = _ffn_down_fwd(act, wd_g)
    dx2b, d_final_g, loss_blk = _loss_bwd(ffn_out, x1, target, final_g2)

    other_chips = jnp.stack([2 * (1 - xi) + yi, 2 * xi + (1 - yi), 2 * (1 - xi) + (1 - yi)])
    others = jnp.concatenate([other_chips, 2 * other_chips + ci]).astype(jnp.int32)

    def partials(grads, recvs, names):
        if all(g.shape == grads[0].shape for g in grads):
            return list(_chip_partial(others, grads, recvs, "chip_partial_" + names[0]))
        return [_chip_partial(others, [g3], [r], "chip_partial_" + nm)[0] for g3, r, nm in zip(grads, recvs, names)]

    own = jnp.stack([me, my_chip]).astype(jnp.int32)

    def adam(idx, g3s, sibs, chipss):
        wmvs = [(view2d(big_w[a], a), view2d(big_m[a], a), view2d(big_v[a], a)) for a in idx]
        outs = _adam_big(own, wmvs, g3s, sibs, chipss, "adam_" + big_names[idx[0]])
        for a, o4 in zip(idx, outs):
            big_out[a] = [unview(o, a) for o in o4]

    big_out = [None] * len(big_names)
    dg_act, du_act = _ffn_gate_bwd(dx2b, wd_g, dadg, dadu)
    gw_gate = _wgrad_rows(dg_act, h2, "wgrad_ffn_gate")
    gw_up = _wgrad_rows(du_act, h2, "wgrad_ffn_up")
    gw_down, sib_gu = _wgrad_rows(act, dx2b, "wgrad_ffn_down", carry=[gw_gate, gw_up])
    ps_gu = partials([gw_gate, gw_up], sib_gu, ["w_ffn_gate", "w_ffn_up"])
    chips_gu = _exchange_chips(ps_gu, "rs_chips_ffn_up", COLLECTIVE_CHIPS)
    dh2, sib_down = _input_grad([(dg_act, wg_g), (du_act, wu_g)], "ffn_in_bwd", after=ps_gu, carry=[gw_down])
    ps_down = partials([gw_down], sib_down, ["w_ffn_down"])
    chips_down = _exchange_chips(ps_down, "rs_chips_ffn_down", COLLECTIVE_CHIPS)
    dx1, dx1b, d_norm2_g = _rms_bwd(dh2, x1, norm2_g, dx2b, "rms2_bwd")
    dya, dyb, dproj42, d_b_gate, d_pool_scale = _wo_bwd(dx1b, wo, merge_factors, sw, after=ps_down)
    dproj = dproj42.reshape(N_DEV, s, sw)
    dproj, d_conv_w, d_conv_b = _conv_bwd(dproj, dya, wa_g, proj, conv_w_full, conv_b)
    dproj = _pool_bwd(dproj, dyb, wpool)
    gw_in = _wgrad_cols(h, dproj, "wgrad_in")
    gw_o, sib_in = _wgrad_full(merged, dx1b, "wgrad_o", carry=[gw_in])
    ps_in = partials([gw_in], sib_in, ["w_in"])
    chips_in = _exchange_chips(ps_in, "rs_chips_w_in", COLLECTIVE_CHIPS)
    gw_a = _wgrad_cols(z, dya, "wgrad_a_out", after=ps_in)
    gw_pool = _wgrad_pool(p, dyb, n_groups)
    mix3 = [gw_a,
            gw_pool.reshape(n_groups, gw, N_DEV, go // N_DEV).transpose(2, 0, 1, 3).reshape(N_DEV, n_groups * gw, go // N_DEV),
            gw_o.reshape(N_DEV, d // N_DEV, d)]
    adam([4, 5, 6], [gw_gate, gw_up, gw_down], sib_gu + sib_down, chips_gu + chips_down)
    dh, sib_mix = _input_grad([(dproj, win_g)], "proj_in_bwd", after=[big_out[6][0]], carry=mix3, per=2)
    ps_mix = partials(mix3, sib_mix, ["w_a_out", "w_pool", "w_o"])
    chips_mix = _exchange_chips(ps_mix, "rs_chips_mixer", COLLECTIVE_CHIPS)
    grad_x, _, d_norm1_g = _rms_bwd(dh, x2d, norm1_g, dx1, "rms1_bwd", with_bf16=False)
    adam([0], [gw_in], sib_in, chips_in)
    for k in range(3):
        adam([1 + k], [mix3[k]], [sib_mix[k]], [chips_mix[k]])

    small_parts = [d_norm1_g, d_b_gate, d_conv_w, d_conv_b, d_pool_scale, d_norm2_g, d_final_g, loss_blk]
    rows = [v.size // LANES for v in small_parts]
    row0 = [sum(rows[:k]) for k in range(len(rows))]
    packed = jnp.concatenate([_rows128(v) for v in small_parts], axis=0)
    gathered = _allgather_small(packed, "allgather_small_grads")
    small_names = ["norm1_g", "b_gate", "conv_b", "pool_scale", "norm2_g", "final_g", "conv_w"]
    small_w = [norm1_g, b_gate, conv_b, pool_scale, norm2_g, final_g]
    small_m = [m_norm1_g, m_b_gate, m_conv_b, m_pool_scale, m_norm2_g, m_final_g]
    small_v = [v_norm1_g, v_b_gate, v_conv_b, v_pool_scale, v_norm2_g, v_final_g]
    finished = _small_finish(gathered, [tuple(_rows128(t) for t in wmv) for wmv in zip(small_w, small_m, small_v)],
                             [row0[k] for k in (0, 1, 3, 4, 5, 6)], [(row0[2], rows[2]), (row0[7], rows[7])])
    g_convw_full, loss_rows = finished[0], finished[1]
    loss = loss_rows[0, 0]
    small_out = [[t.reshape(w.shape) for t in finished[2 + 4 * k:6 + 4 * k]] for k, w in enumerate(small_w)]
    g_convw = lax.dynamic_slice(g_convw_full.reshape(CONV_K, sw), (0, me * cws), (CONV_K, cws))
    cw_delta, cw_m, cw_v = _adam_small(conv_w.reshape(CONV_K, cws), g_convw,
                                       m_conv_w.reshape(CONV_K, cws), v_conv_w.reshape(CONV_K, cws))
    small_out.append([t.reshape(conv_w.shape) for t in (g_convw, cw_delta, cw_m, cw_v)])

    order = ["norm1_g", "w_in", "b_gate", "conv_w", "conv_b", "w_a_out", "w_pool", "pool_scale", "w_o", "norm2_g",
             "w_ffn_gate", "w_ffn_up", "w_ffn_down", "final_g"]
    per_kind = [{}, {}, {}, {}]
    for a, nm in enumerate(big_names):
        for kind in range(4):
            per_kind[kind][nm] = big_out[a][kind]
    for k, nm in enumerate(small_names):
        for kind in range(4):
            per_kind[kind][nm] = small_out[k][kind]
    result = [loss, grad_x.reshape(x.shape)]
    for kind in range(4):
        result += [per_kind[kind][nm] for nm in order]
    return tuple(result)
```

```python
import jax
import jax.numpy as jnp
from jax import lax
from jax.experimental import pallas as pl
from jax.experimental.pallas import tpu as pltpu
from jax.experimental.pallas import tpu_sc as plsc

F32 = jnp.float32
BF16 = jnp.bfloat16
MESH = pl.DeviceIdType.MESH

N_DEV = 8
EPS = 1e-6
CONV_K = 3
POOL_WINDOWS = (2, 4, 8, 16)
ADAM_LR = 0.001
ADAM_B1 = 0.9
ADAM_B2 = 0.999
ADAM_EPS = 1e-08
ADAM_WD = 0.01
ADAM_STEP = 10

V7X_VMEM_LIMIT_BYTES = 56 * 1024 * 1024
LANES = 128

COLLECTIVE_GATHER = 1
COLLECTIVE_SIBLING = 2
COLLECTIVE_CHIPS = 3
SEQUENCER_COST_BYTES = 4 * 10**9

NN = ((1,), (0,))
NT = ((1,), (1,))
TN = ((0,), (0,))


def _dot(a, b, dims):
    return lax.dot_general(a, b, (dims, ((), ())), preferred_element_type=F32)


def _cp(n_axes):
    return pltpu.CompilerParams(dimension_semantics=("arbitrary",) * n_axes,
                                vmem_limit_bytes=V7X_VMEM_LIMIT_BYTES)


def _row_tile(rows, bytes_per_row, cap_bytes):
    best = None
    for t in range(16, rows + 1, 16):
        if rows % t == 0 and t * bytes_per_row <= cap_bytes:
            best = t
    return best if best is not None else rows


def _chunks(total, size):
    size = min(size, total)
    assert total % size == 0
    return [slice(r, r + size) for r in range(0, total, size)]


def _after_specs(after):
    return [pl.BlockSpec(memory_space=pl.ANY)] * len(after)


def _shift_down(v, k):
    row = lax.broadcasted_iota(jnp.int32, v.shape, 0)
    return jnp.where(row >= k, pltpu.roll(v, k, 0), 0.0)


def _shift_up(v, k):
    n = v.shape[0]
    row = lax.broadcasted_iota(jnp.int32, v.shape, 0)
    return jnp.where(row < n - k, pltpu.roll(v, n - k, 0), 0.0)


def _sigmoid(v):
    return jax.nn.sigmoid(v)


def _cast_bf16(w2d, name, parts=1):
    rows, cols = w2d.shape
    tr = _row_tile(rows, cols * 4, 2 << 20)
    pc = cols // parts

    def body(i_ref, *o_refs):
        for q, o_ref in enumerate(o_refs):
            o_ref[...] = i_ref[:, q * pc:(q + 1) * pc].astype(BF16)

    return pl.pallas_call(
        body, name=name, grid=(rows // tr,),
        in_specs=[pl.BlockSpec((tr, cols), lambda i: (i, 0))],
        out_specs=[pl.BlockSpec((tr, pc), lambda i: (i, 0))] * parts,
        out_shape=[jax.ShapeDtypeStruct((rows, pc), BF16)] * parts,
        compiler_params=_cp(1),
    )(w2d)


def _rms_fwd(x2d, g):
    s, d = x2d.shape
    tm = min(256, s)

    def body(x_ref, g_ref, h_ref):
        xv = x_ref[...]
        r = lax.rsqrt(jnp.mean(xv * xv, axis=-1, keepdims=True) + EPS)
        h_ref[...] = (xv * r * g_ref[...]).astype(BF16)

    return pl.pallas_call(
        body, name="rms1_fwd", grid=(s // tm,),
        in_specs=[pl.BlockSpec((tm, d), lambda i: (i, 0)), pl.BlockSpec((1, d), lambda i: (0, 0))],
        out_specs=pl.BlockSpec((tm, d), lambda i: (i, 0)),
        out_shape=jax.ShapeDtypeStruct((s, d), BF16),
        compiler_params=_cp(1),
    )(x2d, g)


def _coords():
    return lax.axis_index("x"), lax.axis_index("y"), lax.axis_index("c")


def _slot(p):
    return 4 * p[0] + 2 * p[1] + p[2]


def _handshake(peers):
    barrier = pltpu.get_barrier_semaphore()
    for peer in peers:
        pl.semaphore_signal(barrier, inc=1, device_id=peer, device_id_type=MESH)
    pl.semaphore_wait(barrier, len(peers))


def _sequencer_call(body, out_type, scratch_types, name, collective_id):
    return pl.kernel(
        body, out_type=out_type, name=name,
        mesh=plsc.ScalarSubcoreMesh(axis_name="seq", num_cores=1),
        scratch_types=scratch_types,
        cost_estimate=pl.CostEstimate(flops=0, transcendentals=0, bytes_accessed=SEQUENCER_COST_BYTES),
        compiler_params=pltpu.CompilerParams(collective_id=collective_id))


def _allgather_big(shards, name, collective_id, after=()):
    n = len(shards)

    def body(*refs):
        ins, outs = refs[:n], refs[n + len(after):2 * n + len(after)]
        send_sems, recv_sems, local_sems = refs[2 * n + len(after):]
        x, y, c = _coords()
        me, sibling = (x, y, c), (x, y, 1 - c)
        x_nbr, y_nbr, diag = (1 - x, y), (x, 1 - y), (1 - x, 1 - y)
        relay_from = (x + (1 - c) * (1 - 2 * x), y + c * (1 - 2 * y))
        relay_to = (x + c * (1 - 2 * x), y + (1 - c) * (1 - 2 * y))
        _handshake([sibling, (*x_nbr, c), (*y_nbr, c)])

        def copy(a, k, block, to, src=None):
            dst = outs[a].at[_slot(block)]
            return pltpu.make_async_remote_copy(
                src_ref=dst if src is None else src, dst_ref=dst,
                send_sem=send_sems.at[a, k], recv_sem=recv_sems.at[a, k],
                device_id=to, device_id_type=MESH)

        mine, sends = [], []
        for a in range(n):
            cp = pltpu.make_async_copy(ins[a], outs[a].at[_slot(me)], local_sems.at[a])
            cp.start()
            mine.append(cp)
            first = [copy(a, 0, me, sibling, src=ins[a]),
                     copy(a, 1, me, (*x_nbr, c), src=ins[a]),
                     copy(a, 2, me, (*y_nbr, c), src=ins[a])]
            for cp in first:
                cp.start()
            sends += first
        for a in range(n):
            copy(a, 1 + c, (*relay_from, c), me).wait_recv()
            passed = [copy(a, 3, (*relay_from, c), (*relay_to, c)), copy(a, 4 + c, (*relay_from, c), sibling)]
            for cp in passed:
                cp.start()
            copy(a, 2 - c, (*relay_to, c), me).wait_recv()
            cp = copy(a, 5 - c, (*relay_to, c), sibling)
            cp.start()
            passed.append(cp)
            copy(a, 3, (*diag, c), me).wait_recv()
            cp = copy(a, 6, (*diag, c), sibling)
            cp.start()
            sends += passed + [cp]
        for a in range(n):
            copy(a, 0, sibling, me).wait_recv()
            copy(a, 4, (*x_nbr, 1 - c), me).wait_recv()
            copy(a, 5, (*y_nbr, 1 - c), me).wait_recv()
            copy(a, 6, (*diag, 1 - c), me).wait_recv()
        for cp in sends:
            cp.wait_send()
        for cp in mine:
            cp.wait()

    return _sequencer_call(
        body, [jax.ShapeDtypeStruct((N_DEV,) + s.shape, s.dtype) for s in shards],
        [pltpu.SemaphoreType.DMA((n, 7)), pltpu.SemaphoreType.DMA((n, 7)), pltpu.SemaphoreType.DMA((n,))],
        name, collective_id)(*shards, *after)


def _sibling_copies(ins, recvs, send_sems, recv_sems):
    x, y, c = _coords()
    return [pltpu.make_async_remote_copy(
        src_ref=ins[a].at[2 * q + (1 - c)], dst_ref=recvs[a].at[q],
        send_sem=send_sems.at[a, q], recv_sem=recv_sems.at[a, q],
        device_id=(x, y, 1 - c), device_id_type=MESH) for a in range(len(ins)) for q in range(4)]


def _carry_specs(carry):
    any_spec = pl.BlockSpec(memory_space=pl.ANY)
    n = len(carry)
    sems = [pltpu.SemaphoreType.DMA((n, 4)), pltpu.SemaphoreType.DMA((n, 4))] if n else []
    return ([any_spec] * n, [any_spec] * n,
            [jax.ShapeDtypeStruct((4,) + g.shape[1:], g.dtype) for g in carry], sems)


def _carry_run(first, last, ins, recvs, sems):
    if not ins:
        return

    @pl.when(first)
    def _():
        x, y, c = _coords()
        _handshake([(x, y, 1 - c)])
        for cp in _sibling_copies(ins, recvs, *sems):
            cp.start()

    @pl.when(last)
    def _():
        copies = _sibling_copies(ins, recvs, *sems)
        for cp in copies:
            cp.wait_recv()
        for cp in copies:
            cp.wait_send()


def _cp_carry(n_axes, carry):
    if not carry:
        return _cp(n_axes)
    return pltpu.CompilerParams(dimension_semantics=("arbitrary",) * n_axes, vmem_limit_bytes=V7X_VMEM_LIMIT_BYTES,
                                collective_id=COLLECTIVE_SIBLING)


def _exchange_chips(psums, name, collective_id):
    n = len(psums)

    def body(*refs):
        ins, outs = refs[:n], refs[n:2 * n]
        send_sems, recv_sems = refs[2 * n:]
        x, y, c = _coords()
        chips = [(1 - x, y), (x, 1 - y), (1 - x, 1 - y)]
        _handshake([(*chip, c) for chip in chips])
        copies = []
        for a in range(n):
            for j, chip in enumerate(chips):
                cp = pltpu.make_async_remote_copy(
                    src_ref=ins[a].at[2 * chip[0] + chip[1]], dst_ref=outs[a].at[j],
                    send_sem=send_sems.at[a, j], recv_sem=recv_sems.at[a, j],
                    device_id=(*chip, c), device_id_type=MESH)
                cp.start()
                copies.append(cp)
        for cp in copies:
            cp.wait_recv()
        for cp in copies:
            cp.wait_send()

    return _sequencer_call(
        body, [jax.ShapeDtypeStruct((3,) + p.shape[1:], p.dtype) for p in psums],
        [pltpu.SemaphoreType.DMA((n, 3)), pltpu.SemaphoreType.DMA((n, 3))],
        name, collective_id)(*psums)


def _allgather_small(v2d, name):
    rows, cols = v2d.shape

    def body(v_ref, out_ref, send_sems, recv_sems):
        x, y, c = _coords()
        me = (x, y, c)
        out_ref[_slot(me)] = v_ref[...]
        peers = []
        for k in range(1, N_DEV):
            fx, fy, fc = (k >> 2) & 1, (k >> 1) & 1, k & 1
            peers.append(((1 - x) if fx else x, (1 - y) if fy else y, (1 - c) if fc else c))
        sends = []
        for k, peer in enumerate(peers):
            cp = pltpu.make_async_remote_copy(
                src_ref=v_ref, dst_ref=out_ref.at[_slot(me)],
                send_sem=send_sems.at[k], recv_sem=recv_sems.at[k],
                device_id=peer, device_id_type=MESH)
            cp.start()
            sends.append(cp)
        for k, peer in enumerate(peers):
            pltpu.make_async_remote_copy(
                src_ref=v_ref, dst_ref=out_ref.at[_slot(peer)],
                send_sem=send_sems.at[k], recv_sem=recv_sems.at[k],
                device_id=peer, device_id_type=MESH).wait_recv()
        for cp in sends:
            cp.wait_send()

    vmem = pl.BlockSpec(memory_space=pltpu.VMEM)
    return pl.pallas_call(
        body, name=name, in_specs=[vmem], out_specs=vmem,
        out_shape=jax.ShapeDtypeStruct((N_DEV, rows, cols), v2d.dtype),
        scratch_shapes=[pltpu.SemaphoreType.DMA((N_DEV - 1,)), pltpu.SemaphoreType.DMA((N_DEV - 1,))],
    )(v2d)


def _chip_partial(others, grads, recvs, name):
    n = len(grads)
    _, rows, cols = grads[0].shape
    tr = _row_tile(rows, cols * 2, (2 << 20) // n)

    def body(others_ref, *refs):
        for a in range(n):
            refs[2 * n + a][...] = (refs[a][...].astype(F32) + refs[n + a][...].astype(F32)).astype(BF16)

    return pl.pallas_call(
        body, name=name,
        grid_spec=pltpu.PrefetchScalarGridSpec(
            num_scalar_prefetch=1, grid=(3, rows // tr),
            in_specs=[pl.BlockSpec((None, tr, cols), lambda k, i, o: (o[3 + k], i, 0))] * n
            + [pl.BlockSpec((None, tr, cols), lambda k, i, o: (o[k], i, 0))] * n,
            out_specs=[pl.BlockSpec((None, tr, cols), lambda k, i, o: (o[k], i, 0))] * n),
        out_shape=[jax.ShapeDtypeStruct((4, rows, cols), BF16)] * n,
        compiler_params=_cp(2),
    )(others, *grads, *recvs)


def _adam_math(w, g, m, v):
    m = ADAM_B1 * m + (1.0 - ADAM_B1) * g
    v = ADAM_B2 * v + (1.0 - ADAM_B2) * (g * g)
    m_hat = m / (1.0 - ADAM_B1 ** ADAM_STEP)
    v_hat = v / (1.0 - ADAM_B2 ** ADAM_STEP)
    delta = -ADAM_LR * (m_hat / (jnp.sqrt(v_hat) + ADAM_EPS) + ADAM_WD * w)
    return delta, m, v


def _adam_big(own, wmvs, g3s, recv_sibs, recv_chipss, name):
    n = len(wmvs)
    rows, cols = wmvs[0][0].shape
    tr = _row_tile(rows, cols * 4, (2 << 20) // n)

    def body(own_ref, *refs):
        ins, outs = refs[:6 * n], refs[6 * n:]
        for a in range(n):
            w_ref, m_ref, v_ref, g_ref, rs_ref, rc_ref = ins[6 * a:6 * a + 6]
            g = g_ref[...].astype(F32) + rs_ref[...].astype(F32)
            g = g + rc_ref[0].astype(F32)
            g = g + rc_ref[1].astype(F32)
            g = g + rc_ref[2].astype(F32)
            delta, m_new, v_new = _adam_math(w_ref[...], g, m_ref[...], v_ref[...])
            outs[4 * a][...] = g
            outs[4 * a + 1][...] = delta
            outs[4 * a + 2][...] = m_new
            outs[4 * a + 3][...] = v_new

    blk = pl.BlockSpec((tr, cols), lambda i, o: (i, 0))
    per_shard = [blk, blk, blk,
                 pl.BlockSpec((None, tr, cols), lambda i, o: (o[0], i, 0)),
                 pl.BlockSpec((None, tr, cols), lambda i, o: (o[1], i, 0)),
                 pl.BlockSpec((3, tr, cols), lambda i, o: (0, i, 0))]
    out = jax.ShapeDtypeStruct((rows, cols), F32)
    args = [t for a in range(n) for t in (*wmvs[a], g3s[a], recv_sibs[a], recv_chipss[a])]
    outs = pl.pallas_call(
        body, name=name,
        grid_spec=pltpu.PrefetchScalarGridSpec(
            num_scalar_prefetch=1, grid=(rows // tr,),
            in_specs=per_shard * n, out_specs=[blk] * (4 * n)),
        out_shape=[out] * (4 * n),
        compiler_params=_cp(1),
    )(own, *args)
    return [outs[4 * a:4 * a + 4] for a in range(n)]


def _small_finish(gathered, params, row_offs, extra_rows):
    n = len(params)

    def body(g_ref, *refs):
        ins, outs = refs[:3 * n], refs[3 * n:]
        total = g_ref[0]
        for k in range(1, N_DEV):
            total = total + g_ref[k]
        for e, (r0, nr) in enumerate(extra_rows):
            outs[e][...] = total[r0:r0 + nr, :]
        for p in range(n):
            w_ref, m_ref, v_ref = ins[3 * p:3 * p + 3]
            g_out, d_out, m_out, v_out = outs[len(extra_rows) + 4 * p:len(extra_rows) + 4 * p + 4]
            g = total[row_offs[p]:row_offs[p] + w_ref.shape[0], :]
            delta, m_new, v_new = _adam_math(w_ref[...], g, m_ref[...], v_ref[...])
            g_out[...] = g
            d_out[...] = delta
            m_out[...] = m_new
            v_out[...] = v_new

    vmem = pl.BlockSpec(memory_space=pltpu.VMEM)
    out_shape = [jax.ShapeDtypeStruct((nr, LANES), F32) for _, nr in extra_rows]
    for w, _, _ in params:
        out_shape += [jax.ShapeDtypeStruct(w.shape, F32)] * 4
    flat = [t for wmv in params for t in wmv]
    return pl.pallas_call(body, name="small_finish", in_specs=[vmem] * (1 + len(flat)),
                          out_specs=[vmem] * len(out_shape), out_shape=out_shape)(gathered, *flat)


def _adam_small(w, g, m, v):
    def body(w_ref, g_ref, m_ref, v_ref, do_ref, mo_ref, vo_ref):
        delta, m_new, v_new = _adam_math(w_ref[...], g_ref[...], m_ref[...], v_ref[...])
        do_ref[...] = delta
        mo_ref[...] = m_new
        vo_ref[...] = v_new

    vmem = pl.BlockSpec(memory_space=pltpu.VMEM)
    out = jax.ShapeDtypeStruct(w.shape, F32)
    return pl.pallas_call(body, name="adam_small", in_specs=[vmem] * 4, out_specs=[vmem] * 3,
                          out_shape=[out, out, out])(w, g, m, v)


def _proj_fwd(h, win_g):
    s, d = h.shape
    sw = win_g.shape[2]
    tn = min(512, sw)
    nh = sw // tn

    def body(h_ref, w_ref, o_ref):
        for rs in _chunks(s, 512):
            o_ref[rs, :] = _dot(h_ref[rs, :], w_ref[...], NN)

    return pl.pallas_call(
        body, name="proj_fwd", grid=(N_DEV * nh,),
        in_specs=[pl.BlockSpec((s, d), lambda j: (0, 0)),
                  pl.BlockSpec((None, d, tn), lambda j: (j // nh, 0, j % nh))],
        out_specs=pl.BlockSpec((None, s, tn), lambda j: (j // nh, 0, j % nh)),
        out_shape=jax.ShapeDtypeStruct((N_DEV, s, sw), F32),
        compiler_params=_cp(1),
    )(h, win_g)


def _conv_fwd(proj, conv_w, conv_b):
    _, s, sw = proj.shape
    tc = min(LANES, sw)

    def body(ba_ref, ca_ref, va_ref, cw_ref, cb_ref, z_ref):
        cv = ca_ref[...] * va_ref[...]
        u = (cb_ref[...] + cw_ref[0:1, :] * _shift_down(cv, 2) + cw_ref[1:2, :] * _shift_down(cv, 1)
             + cw_ref[2:3, :] * cv)
        z_ref[...] = (ba_ref[...] * u).astype(BF16)

    def part(k):
        return pl.BlockSpec((None, s, tc), lambda i: (k, 0, i))

    return pl.pallas_call(
        body, name="conv_fwd", grid=(sw // tc,),
        in_specs=[part(0), part(1), part(2),
                  pl.BlockSpec((CONV_K, tc), lambda i: (0, i)), pl.BlockSpec((1, tc), lambda i: (0, i))],
        out_specs=pl.BlockSpec((s, tc), lambda i: (0, i)),
        out_shape=jax.ShapeDtypeStruct((s, sw), BF16),
        compiler_params=_cp(1),
    )(proj, proj, proj, conv_w, conv_b)


def _pool_counts(shape, window):
    t = lax.broadcasted_iota(jnp.int32, shape, 0)
    return jnp.minimum(t + 1, window).astype(F32)


def _pool_fwd(proj):
    _, s, sw = proj.shape
    gw = sw // len(POOL_WINDOWS)

    def body(v_ref, p_ref):
        for gi, window in enumerate(POOL_WINDOWS):
            @pl.when(pl.program_id(0) == gi)
            def _():
                v = v_ref[...]
                acc, k = v, 1
                while k < window:
                    acc = acc + _shift_down(acc, k)
                    k *= 2
                p_ref[...] = (acc / _pool_counts(v.shape, window) - v).astype(BF16)

    return pl.pallas_call(
        body, name="pool_fwd", grid=(len(POOL_WINDOWS),),
        in_specs=[pl.BlockSpec((None, s, gw), lambda g: (3, 0, g))],
        out_specs=pl.BlockSpec((s, gw), lambda g: (0, g)),
        out_shape=jax.ShapeDtypeStruct((s, sw), BF16),
        compiler_params=_cp(1),
    )(proj)


def _merge_fwd(z, wa, p, wpool, proj, b_gate2, pool_scale):
    s, sw = z.shape
    tn = wa.shape[2]
    d = tn * N_DEV
    gw = sw // len(POOL_WINDOWS)
    nq = sw // tn

    def body(z_ref, wa_ref, p_ref, wp_ref, ga_ref, gb_ref, bg_ref, sc_ref,
             m_ref, dya_ref, dyb_ref, dga_ref, dgb_ref, dsc_ref):
        for rs in _chunks(s, 512):
            ya = _dot(z_ref[rs, :], wa_ref[...], NN)
            yb = _dot(p_ref[rs, :], wp_ref[...], NN)
            sa = _sigmoid(ga_ref[rs, :] + bg_ref[0:1, :])
            sb = _sigmoid(gb_ref[rs, :] + bg_ref[1:2, :])
            sc = sc_ref[...]
            sb_yb = sb * yb
            m_ref[rs, :] = (sa * ya + sb_yb * sc).astype(BF16)
            dya_ref[rs, :] = sa.astype(BF16)
            dyb_ref[rs, :] = (sb * sc).astype(BF16)
            dga_ref[rs, :] = (ya * (sa * (1.0 - sa))).astype(BF16)
            dgb_ref[rs, :] = ((yb * sc) * (sb * (1.0 - sb))).astype(BF16)
            dsc_ref[rs, :] = sb_yb.astype(BF16)

    col = pl.BlockSpec((s, tn), lambda j: (0, j))
    out = jax.ShapeDtypeStruct((s, d), BF16)
    return pl.pallas_call(
        body, name="merge_fwd", grid=(N_DEV,),
        in_specs=[pl.BlockSpec((s, sw), lambda j: (0, 0)),
                  pl.BlockSpec((None, sw, tn), lambda j: (j, 0, 0)),
                  pl.BlockSpec((s, gw), lambda j: (0, j // 2)),
                  pl.BlockSpec((None, gw, tn), lambda j: (j // 2, 0, j % 2)),
                  pl.BlockSpec((None, s, tn), lambda j: (4 + j // nq, 0, j % nq)),
                  pl.BlockSpec((None, s, tn), lambda j: (6 + j // nq, 0, j % nq)),
                  pl.BlockSpec((2, tn), lambda j: (0, j)),
                  pl.BlockSpec((1, tn), lambda j: (0, j))],
        out_specs=[col] * 6,
        out_shape=[out] * 6,
        compiler_params=_cp(1),
    )(z, wa, p, wpool, proj, proj, b_gate2, pool_scale)


def _wo_fwd(merged, wo, x2d, g2):
    s, d = x2d.shape
    tm = min(256, s)

    def body(m_ref, wo_ref, x_ref, g_ref, x1_ref, h2_ref):
        x1 = x_ref[...] + _dot(m_ref[...], wo_ref[...], NN)
        x1_ref[...] = x1
        r = lax.rsqrt(jnp.mean(x1 * x1, axis=-1, keepdims=True) + EPS)
        h2_ref[...] = (x1 * r * g_ref[...]).astype(BF16)

    row = pl.BlockSpec((tm, d), lambda i: (i, 0))
    return pl.pallas_call(
        body, name="wo_fwd", grid=(s // tm,),
        in_specs=[row, pl.BlockSpec((d, d), lambda i: (0, 0)), row, pl.BlockSpec((1, d), lambda i: (0, 0))],
        out_specs=[row, row],
        out_shape=[jax.ShapeDtypeStruct((s, d), F32), jax.ShapeDtypeStruct((s, d), BF16)],
        compiler_params=_cp(1),
    )(merged, wo, x2d, g2)


def _ffn_gate_fwd(h2, wg_g):
    s, d = h2.shape
    f8 = wg_g.shape[2]
    th = min(1024, s)

    def body(h_ref, wg_ref, g_ref):
        i = pl.program_id(1)
        for rs in _chunks(th, 512):
            rows = pl.ds(pl.multiple_of(i * th + rs.start, rs.stop - rs.start), rs.stop - rs.start)
            g_ref[rs, :] = _dot(h_ref[rows, :], wg_ref[...], NN)

    return pl.pallas_call(
        body, name="ffn_gate_fwd", grid=(N_DEV, s // th),
        in_specs=[pl.BlockSpec((s, d), lambda j, i: (0, 0)), pl.BlockSpec((None, d, f8), lambda j, i: (j, 0, 0))],
        out_specs=pl.BlockSpec((None, th, f8), lambda j, i: (j, i, 0)),
        out_shape=jax.ShapeDtypeStruct((N_DEV, s, f8), F32),
        compiler_params=_cp(2),
    )(h2, wg_g)


def _ffn_up_act_fwd(h2, wu_g, gate):
    s, d = h2.shape
    f8 = wu_g.shape[2]
    th = min(1024, s)

    def body(h_ref, wu_ref, g_ref, dadu_ref, dadg_ref, a_ref):
        i = pl.program_id(1)
        for rs in _chunks(th, 512):
            rows = pl.ds(pl.multiple_of(i * th + rs.start, rs.stop - rs.start), rs.stop - rs.start)
            g = g_ref[rs, :]
            u = _dot(h_ref[rows, :], wu_ref[...], NN)
            sg = _sigmoid(g)
            silu = g * sg
            dadu_ref[rs, :] = silu.astype(BF16)
            dadg_ref[rs, :] = (u * (sg * (1.0 + g * (1.0 - sg)))).astype(BF16)
            a_ref[rs, :] = (silu * u).astype(BF16)

    wspec = pl.BlockSpec((None, d, f8), lambda j, i: (j, 0, 0))
    ospec = pl.BlockSpec((None, th, f8), lambda j, i: (j, i, 0))
    out = jax.ShapeDtypeStruct((N_DEV, s, f8), BF16)
    return pl.pallas_call(
        body, name="ffn_up_fwd", grid=(N_DEV, s // th),
        in_specs=[pl.BlockSpec((s, d), lambda j, i: (0, 0)), wspec, ospec],
        out_specs=[ospec, ospec, ospec], out_shape=[out, out, out],
        compiler_params=_cp(2),
    )(h2, wu_g, gate)


def _ffn_down_fwd(act, wd_part, name):
    _, s, f8 = act.shape
    tn = wd_part.shape[2]
    per = 2

    def body(a_ref, wd_ref, o_ref):
        @pl.when(pl.program_id(0) == 0)
        def _():
            o_ref[...] = jnp.zeros_like(o_ref)

        for rs in _chunks(s, 1024):
            part = _dot(a_ref[0, rs, :], wd_ref[0], NN)
            for q in range(1, per):
                part = part + _dot(a_ref[q, rs, :], wd_ref[q], NN)
            o_ref[rs, :] += part

    return pl.pallas_call(
        body, name=name, grid=(N_DEV // per,),
        in_specs=[pl.BlockSpec((per, s, f8), lambda j: (j, 0, 0)),
                  pl.BlockSpec((per, f8, tn), lambda j: (j, 0, 0))],
        out_specs=pl.BlockSpec((s, tn), lambda j: (0, 0)),
        out_shape=jax.ShapeDtypeStruct((s, tn), F32),
        compiler_params=_cp(1),
    )(act, wd_part)


def _loss_bwd(ffn_parts, x1, target, final_g):
    s, d = x1.shape
    tm = min(256, s)
    nparts = len(ffn_parts)

    def body(*refs):
        f_refs = refs[:nparts]
        x1_ref, t_ref, gf_ref, dxb_ref, dgf_ref, loss_ref = refs[nparts:]

        @pl.when(pl.program_id(0) == 0)
        def _():
            dgf_ref[...] = jnp.zeros_like(dgf_ref)
            loss_ref[...] = jnp.zeros_like(loss_ref)

        x2 = x1_ref[...] + jnp.concatenate([f_ref[...] for f_ref in f_refs], axis=-1)
        r = lax.rsqrt(jnp.mean(x2 * x2, axis=-1, keepdims=True) + EPS)
        nrm = x2 * r
        gf = gf_ref[...]
        err = nrm * gf - t_ref[...]
        loss_ref[...] += jnp.sum(err * err) * (0.5 / d)
        dy = err * (1.0 / d)
        dgf_ref[...] += jnp.sum(dy * nrm, axis=0, keepdims=True)
        dn = dy * gf
        dx = r * (dn - nrm * jnp.mean(dn * nrm, axis=-1, keepdims=True))
        dxb_ref[...] = dx.astype(BF16)

    row = pl.BlockSpec((tm, d), lambda i: (i, 0))
    vec = pl.BlockSpec((1, d), lambda i: (0, 0))
    return pl.pallas_call(
        body, name="loss_bwd", grid=(s // tm,),
        in_specs=[pl.BlockSpec((tm, f.shape[1]), lambda i: (i, 0)) for f in ffn_parts] + [row, row, vec],
        out_specs=[row, vec, pl.BlockSpec((8, LANES), lambda i: (0, 0))],
        out_shape=[jax.ShapeDtypeStruct((s, d), BF16),
                   jax.ShapeDtypeStruct((1, d), F32), jax.ShapeDtypeStruct((8, LANES), F32)],
        compiler_params=_cp(1),
    )(*ffn_parts, x1, target, final_g)


def _ffn_gate_bwd(dx2b, wd_parts, dadg, dadu):
    s, d = dx2b.shape
    f8 = dadg.shape[2]
    th = min(1024, s)
    nparts = len(wd_parts)
    pc = d // nparts

    def body(dx_ref, *refs):
        wd_refs = refs[:nparts]
        g_ref, u_ref, dg_ref, du_ref, da_ref = refs[nparts:]
        i = pl.program_id(1)
        chunks = _chunks(th, 256)

        def matmul(rs):
            rows = pl.ds(pl.multiple_of(i * th + rs.start, rs.stop - rs.start), rs.stop - rs.start)
            part = None
            for q, wd_ref in enumerate(wd_refs):
                term = _dot(dx_ref[rows, q * pc:(q + 1) * pc], wd_ref[...], NT)
                part = term if part is None else part + term
            da_ref[rs, :] = part

        matmul(chunks[0])
        for k, rs in enumerate(chunks):
            if k + 1 < len(chunks):
                matmul(chunks[k + 1])
            da = da_ref[rs, :].astype(BF16)
            dg_ref[rs, :] = da * g_ref[rs, :]
            du_ref[rs, :] = da * u_ref[rs, :]

    aspec = pl.BlockSpec((None, th, f8), lambda j, i: (j, i, 0))
    out = jax.ShapeDtypeStruct((N_DEV, s, f8), BF16)
    return pl.pallas_call(
        body, name="ffn_act_bwd", grid=(N_DEV, s // th),
        in_specs=[pl.BlockSpec((s, d), lambda j, i: (0, 0))]
        + [pl.BlockSpec((None, f8, pc), lambda j, i: (j, 0, 0))] * nparts + [aspec, aspec],
        out_specs=[aspec, aspec], out_shape=[out, out],
        scratch_shapes=[pltpu.VMEM((th, f8), F32)],
        compiler_params=_cp(2),
    )(dx2b, *wd_parts, dadg, dadu)


def _wgrad_rows(a3, b, name, after=(), carry=()):
    _, s, k = a3.shape
    n = b.shape[1]
    nc = len(carry)
    c_in, c_out, c_shape, c_sems = _carry_specs(carry)

    def body(a_ref, b_ref, *rest):
        rest = rest[len(after):]
        o_ref = rest[nc]
        j = pl.program_id(0)
        _carry_run(j == 0, j == N_DEV - 1, rest[:nc], rest[nc + 1:2 * nc + 1], rest[2 * nc + 1:])
        o_ref[...] = _dot(a_ref[...], b_ref[...], TN).astype(BF16)

    outs = pl.pallas_call(
        body, name=name, grid=(N_DEV,),
        in_specs=[pl.BlockSpec((None, s, k), lambda j: (j, 0, 0)),
                  pl.BlockSpec((s, n), lambda j: (0, 0))] + _after_specs(after) + c_in,
        out_specs=[pl.BlockSpec((None, k, n), lambda j: (j, 0, 0))] + c_out,
        out_shape=[jax.ShapeDtypeStruct((N_DEV, k, n), BF16)] + c_shape,
        scratch_shapes=c_sems,
        compiler_params=_cp_carry(1, carry),
    )(a3, b, *after, *carry)
    return (outs[0], list(outs[1:])) if nc else outs[0]


def _wgrad_cols(a, b3, name, after=()):
    s, k = a.shape
    if b3.ndim == 2:
        n = b3.shape[1] // N_DEV
        b_spec = pl.BlockSpec((s, n), lambda j: (0, j))
    else:
        n = b3.shape[2]
        b_spec = pl.BlockSpec((None, s, n), lambda j: (j, 0, 0))

    def body(a_ref, b_ref, *rest):
        o_ref = rest[len(after)]
        o_ref[...] = _dot(a_ref[...], b_ref[...], TN).astype(BF16)

    return pl.pallas_call(
        body, name=name, grid=(N_DEV,),
        in_specs=[pl.BlockSpec((s, k), lambda j: (0, 0)), b_spec] + _after_specs(after),
        out_specs=pl.BlockSpec((None, k, n), lambda j: (j, 0, 0)),
        out_shape=jax.ShapeDtypeStruct((N_DEV, k, n), BF16),
        compiler_params=_cp(1),
    )(a, b3, *after)


def _input_grad(pairs, name, after=(), carry=(), per=1):
    s = pairs[0][0].shape[1]
    d = pairs[0][1].shape[1]
    tn = min(1024, d)
    npair = len(pairs)
    nc = len(carry)
    c_in, c_out, c_shape, c_sems = _carry_specs(carry)

    def body(*refs):
        ops = refs[:2 * npair]
        rest = refs[2 * npair + len(after):]
        o_ref, acc_ref = rest[nc], rest[-1]
        nh, j = pl.program_id(0), pl.program_id(1)
        last_j = N_DEV // per - 1
        _carry_run((nh == 0) & (j == 0), (nh == d // tn - 1) & (j == last_j),
                   rest[:nc], rest[nc + 1:2 * nc + 1], rest[2 * nc + 1:-1])

        @pl.when(j == 0)
        def _():
            acc_ref[...] = jnp.zeros_like(acc_ref)

        for rs in _chunks(s, 1024):
            part = None
            for q in range(npair):
                for e in range(per):
                    term = _dot(ops[2 * q][e, rs, :], ops[2 * q + 1][e], NT)
                    part = term if part is None else part + term
            acc_ref[rs, :] += part

        @pl.when(j == last_j)
        def _():
            o_ref[...] = acc_ref[...].astype(BF16)

    in_specs, args = [], []
    for a3, w3 in pairs:
        k = a3.shape[2]
        in_specs += [pl.BlockSpec((per, s, k), lambda n, j: (j, 0, 0)),
                     pl.BlockSpec((per, tn, k), lambda n, j: (j, n, 0))]
        args += [a3, w3]
    outs = pl.pallas_call(
        body, name=name, grid=(d // tn, N_DEV // per),
        in_specs=in_specs + _after_specs(after) + c_in,
        out_specs=[pl.BlockSpec((s, tn), lambda n, j: (0, n))] + c_out,
        out_shape=[jax.ShapeDtypeStruct((s, d), BF16)] + c_shape,
        scratch_shapes=c_sems + [pltpu.VMEM((s, tn), F32)],
        compiler_params=_cp_carry(2, carry),
    )(*args, *after, *carry)
    return (outs[0], list(outs[1:])) if nc else outs[0]


def _rms_bwd(dh, xres, g, dres, name, with_bf16=True):
    s, d = xres.shape
    tm = min(256, s)

    def body(dh_ref, x_ref, g_ref, dres_ref, dx_ref, *rest):
        dg_ref = rest[-1]
        @pl.when(pl.program_id(0) == 0)
        def _():
            dg_ref[...] = jnp.zeros_like(dg_ref)

        xv = x_ref[...]
        dh_v = dh_ref[...].astype(F32)
        r = lax.rsqrt(jnp.mean(xv * xv, axis=-1, keepdims=True) + EPS)
        nrm = xv * r
        dg_ref[...] += jnp.sum(dh_v * nrm, axis=0, keepdims=True)
        dn = dh_v * g_ref[...]
        dx = dres_ref[...].astype(F32) + r * (dn - nrm * jnp.mean(dn * nrm, axis=-1, keepdims=True))
        dx_ref[...] = dx
        if with_bf16:
            rest[0][...] = dx.astype(BF16)

    row = pl.BlockSpec((tm, d), lambda i: (i, 0))
    vec = pl.BlockSpec((1, d), lambda i: (0, 0))
    copies = [jax.ShapeDtypeStruct((s, d), BF16)] if with_bf16 else []
    outs = pl.pallas_call(
        body, name=name, grid=(s // tm,),
        in_specs=[row, row, vec, row],
        out_specs=[row] + [row] * len(copies) + [vec],
        out_shape=[jax.ShapeDtypeStruct((s, d), F32)] + copies + [jax.ShapeDtypeStruct((1, d), F32)],
        compiler_params=_cp(1),
    )(dh, xres, g, dres)
    return (outs[0], outs[1], outs[2]) if with_bf16 else (outs[0], None, outs[1])


def _wgrad_full(a, b, name, after=(), carry=()):
    s, k = a.shape
    n = b.shape[1]
    tk = min(512, k)
    nc = len(carry)
    c_in, c_out, c_shape, c_sems = _carry_specs(carry)

    def body(a_ref, b_ref, *rest):
        rest = rest[len(after):]
        o_ref = rest[nc]
        j = pl.program_id(0)
        _carry_run(j == 0, j == k // tk - 1, rest[:nc], rest[nc + 1:2 * nc + 1], rest[2 * nc + 1:])
        o_ref[...] = _dot(a_ref[...], b_ref[...], TN).astype(BF16)

    outs = pl.pallas_call(
        body, name=name, grid=(k // tk,),
        in_specs=[pl.BlockSpec((s, tk), lambda j: (0, j)),
                  pl.BlockSpec((s, n), lambda j: (0, 0))] + _after_specs(after) + c_in,
        out_specs=[pl.BlockSpec((tk, n), lambda j: (j, 0))] + c_out,
        out_shape=[jax.ShapeDtypeStruct((k, n), BF16)] + c_shape,
        scratch_shapes=c_sems,
        compiler_params=_cp_carry(1, carry),
    )(a, b, *after, *carry)
    return (outs[0], list(outs[1:])) if nc else outs[0]


def _wgrad_pool(p, dyb, n_groups):
    s, sw = p.shape
    d = dyb.shape[1]
    gw, go = sw // n_groups, d // n_groups
    ts = min(512, s)
    ns = s // ts

    def body(a_ref, b_ref, o_ref, acc_ref):
        i = pl.program_id(1)

        @pl.when(i == 0)
        def _():
            acc_ref[...] = jnp.zeros_like(acc_ref)

        acc_ref[...] += _dot(a_ref[...], b_ref[...], TN)

        @pl.when(i == ns - 1)
        def _():
            o_ref[...] = acc_ref[...].astype(BF16)

    return pl.pallas_call(
        body, name="wgrad_pool", grid=(n_groups, ns),
        in_specs=[pl.BlockSpec((ts, gw), lambda g, i: (i, g)),
                  pl.BlockSpec((ts, go), lambda g, i: (i, g))],
        out_specs=pl.BlockSpec((None, gw, go), lambda g, i: (g, 0, 0)),
        out_shape=jax.ShapeDtypeStruct((n_groups, gw, go), BF16),
        scratch_shapes=[pltpu.VMEM((gw, go), F32)],
        compiler_params=_cp(2),
    )(p, dyb)


def _wo_bwd(dx1b, wo, factors, sw, after=()):
    s, d = dx1b.shape
    tn = d // N_DEV
    nq = sw // tn

    def body(dx_ref, wo_ref, fya_ref, fyb_ref, fga_ref, fgb_ref, fsc_ref, *rest):
        dya_ref, dyb_ref, dp_ref, dbg_ref, dsc_ref, dm_ref = rest[len(after):]
        dbg_ref[...] = jnp.zeros_like(dbg_ref)
        dsc_ref[...] = jnp.zeros_like(dsc_ref)
        for rs in _chunks(s, 1024):
            dm_ref[rs, :] = _dot(dx_ref[rs, :], wo_ref[...], NT)
        for rs in _chunks(s, 256):
            dm = dm_ref[rs, :]
            dya_ref[rs, :] = (dm * fya_ref[rs, :].astype(F32)).astype(BF16)
            dyb_ref[rs, :] = (dm * fyb_ref[rs, :].astype(F32)).astype(BF16)
            dsc_ref[...] += jnp.sum(dm * fsc_ref[rs, :].astype(F32), axis=0, keepdims=True)
            dga = dm * fga_ref[rs, :].astype(F32)
            dgb = dm * fgb_ref[rs, :].astype(F32)
            dp_ref[0, rs, :] = dga.astype(BF16)
            dp_ref[1, rs, :] = dgb.astype(BF16)
            dbg_ref[0:1, :] += jnp.sum(dga, axis=0, keepdims=True)
            dbg_ref[1:2, :] += jnp.sum(dgb, axis=0, keepdims=True)

    col = pl.BlockSpec((s, tn), lambda j: (0, j))
    out = jax.ShapeDtypeStruct((s, d), BF16)
    return pl.pallas_call(
        body, name="wo_bwd", grid=(N_DEV,),
        in_specs=[pl.BlockSpec((s, d), lambda j: (0, 0)),
                  pl.BlockSpec((tn, d), lambda j: (j, 0))] + [col] * 5 + _after_specs(after),
        out_specs=[col, col,
                   pl.BlockSpec((2, None, s, tn), lambda j: (1, j // nq, 0, j % nq)),
                   pl.BlockSpec((2, tn), lambda j: (0, j)),
                   pl.BlockSpec((1, tn), lambda j: (0, j))],
        out_shape=[out, out, jax.ShapeDtypeStruct((4, 2, s, sw), BF16),
                   jax.ShapeDtypeStruct((2, d), F32), jax.ShapeDtypeStruct((1, d), F32)],
        scratch_shapes=[pltpu.VMEM((s, tn), F32)],
        compiler_params=_cp(1),
    )(dx1b, wo, *factors, *after)


def _conv_bwd(dproj, dya, wa, proj, conv_w, conv_b):
    s, d = dya.shape
    sw, tn = wa.shape[1], wa.shape[2]
    tc = min(LANES, sw)

    def body(dproj_hbm, dya_ref, wa_ref, ba_ref, ca_ref, va_ref, cw_ref, cb_ref,
             dp_ref, dcw_ref, dcb_ref, dz_ref):
        del dproj_hbm
        for rs in _chunks(s, 512):
            part = _dot(dya_ref[rs, 0:tn], wa_ref[0], NT)
            for j in range(1, N_DEV):
                part = part + _dot(dya_ref[rs, j * tn:(j + 1) * tn], wa_ref[j], NT)
            dz_ref[rs, :] = part
        dz = dz_ref[...]
        ba, ca, va = ba_ref[...], ca_ref[...], va_ref[...]
        cv = ca * va
        cv1, cv2 = _shift_down(cv, 1), _shift_down(cv, 2)
        w0, w1, w2 = cw_ref[0:1, :], cw_ref[1:2, :], cw_ref[2:3, :]
        u = cb_ref[...] + w0 * cv2 + w1 * cv1 + w2 * cv
        du = dz * ba
        dp_ref[0] = (dz * u).astype(BF16)
        dcv = w2 * du + w1 * _shift_up(du, 1) + w0 * _shift_up(du, 2)
        dp_ref[1] = (dcv * va).astype(BF16)
        dp_ref[2] = (dcv * ca).astype(BF16)
        dcw_ref[0:1, :] = jnp.sum(du * cv2, axis=0, keepdims=True)
        dcw_ref[1:2, :] = jnp.sum(du * cv1, axis=0, keepdims=True)
        dcw_ref[2:3, :] = jnp.sum(du * cv, axis=0, keepdims=True)
        dcb_ref[...] = jnp.sum(du, axis=0, keepdims=True)

    def part(k):
        return pl.BlockSpec((None, s, tc), lambda i: (k, 0, i))

    return pl.pallas_call(
        body, name="conv_bwd", grid=(sw // tc,),
        in_specs=[pl.BlockSpec(memory_space=pl.ANY),
                  pl.BlockSpec((s, d), lambda i: (0, 0)),
                  pl.BlockSpec((N_DEV, tc, tn), lambda i: (0, i, 0)),
                  part(0), part(1), part(2),
                  pl.BlockSpec((CONV_K, tc), lambda i: (0, i)), pl.BlockSpec((1, tc), lambda i: (0, i))],
        out_specs=[pl.BlockSpec((3, s, tc), lambda i: (0, 0, i)),
                   pl.BlockSpec((CONV_K, tc), lambda i: (0, i)), pl.BlockSpec((1, tc), lambda i: (0, i))],
        out_shape=[jax.ShapeDtypeStruct(dproj.shape, BF16),
                   jax.ShapeDtypeStruct((CONV_K, sw), F32), jax.ShapeDtypeStruct((1, sw), F32)],
        scratch_shapes=[pltpu.VMEM((s, tc), F32)],
        input_output_aliases={0: 0},
        compiler_params=_cp(1),
    )(dproj, dya, wa, proj, proj, proj, conv_w, conv_b)


def _pool_bwd(dproj, dyb, wpool):
    s, d = dyb.shape
    n_groups, gw, go = wpool.shape

    def body(dproj_hbm, dyb_ref, wp_ref, dp_ref):
        del dproj_hbm
        for gi, window in enumerate(POOL_WINDOWS):
            @pl.when(pl.program_id(0) == gi)
            def _():
                dpool = _dot(dyb_ref[...], wp_ref[...], NT)
                acc, k = dpool / _pool_counts(dpool.shape, window), 1
                while k < window:
                    acc = acc + _shift_up(acc, k)
                    k *= 2
                dp_ref[...] = (acc - dpool).astype(BF16)

    return pl.pallas_call(
        body, name="pool_bwd", grid=(n_groups,),
        in_specs=[pl.BlockSpec(memory_space=pl.ANY),
                  pl.BlockSpec((s, go), lambda g: (0, g)),
                  pl.BlockSpec((None, gw, go), lambda g: (g, 0, 0))],
        out_specs=pl.BlockSpec((None, s, gw), lambda g: (3, 0, g)),
        out_shape=jax.ShapeDtypeStruct(dproj.shape, BF16),
        input_output_aliases={0: 0},
        compiler_params=_cp(1),
    )(dproj, dyb, wpool)


def _rows128(v):
    return v.reshape(-1, LANES)


def kernel(x, norm1_g, w_in, b_gate, conv_w, conv_b, w_a_out, w_pool, pool_scale, w_o, norm2_g, w_ffn_gate, w_ffn_up, w_ffn_down, final_g, loss_target, m_norm1_g, m_w_in, m_b_gate, m_conv_w, m_conv_b, m_w_a_out, m_w_pool, m_pool_scale, m_w_o, m_norm2_g, m_w_ffn_gate, m_w_ffn_up, m_w_ffn_down, m_final_g, v_norm1_g, v_w_in, v_b_gate, v_conv_w, v_conv_b, v_w_a_out, v_w_pool, v_pool_scale, v_w_o, v_norm2_g, v_w_ffn_gate, v_w_ffn_up, v_w_ffn_down, v_final_g):
    s, d = x.shape[1], x.shape[2]
    sw = w_in.shape[2]
    n_groups = w_pool.shape[1]
    gw = w_pool.shape[2]
    go = w_pool.shape[3] * N_DEV
    f8 = w_ffn_gate.shape[2]
    cws = conv_w.shape[2]
    assert sw == conv_w.shape[2] * N_DEV == gw * n_groups and go * n_groups == d and n_groups == len(POOL_WINDOWS)

    xi, yi, ci = _coords()
    me = 4 * xi + 2 * yi + ci
    my_chip = 2 * xi + yi

    x2d = x.reshape(s, d)
    target = loss_target.reshape(s, d)
    final_g2 = final_g.reshape(1, d)
    b_gate2 = b_gate.reshape(2, d)

    big_names = ["w_in", "w_a_out", "w_pool", "w_o", "w_ffn_gate", "w_ffn_up", "w_ffn_down"]
    big_w = [w_in, w_a_out, w_pool, w_o, w_ffn_gate, w_ffn_up, w_ffn_down]
    big_m = [m_w_in, m_w_a_out, m_w_pool, m_w_o, m_w_ffn_gate, m_w_ffn_up, m_w_ffn_down]
    big_v = [v_w_in, v_w_a_out, v_w_pool, v_w_o, v_w_ffn_gate, v_w_ffn_up, v_w_ffn_down]
    shapes2d = [(w.size // w.shape[-1], w.shape[-1]) for w in big_w]
    big_w2 = [w.reshape(sh) for w, sh in zip(big_w, shapes2d)]
    transposed = (4, 5)

    def view2d(t, a):
        t2 = t.reshape(shapes2d[a])
        return t2.T if a in transposed else t2

    def unview(o, a):
        return (o.T if a in transposed else o).reshape(big_w[a].shape)

    sb = [_cast_bf16(w, "cast_" + nm, parts=2 if nm == "w_ffn_down" else 1) for w, nm in zip(big_w2, big_names)]
    win_g, wa_g, wpool_g, wo_g = _allgather_big([b[0] for b in sb[0:4]], "allgather_mixer", COLLECTIVE_GATHER)
    (wg_g,) = _allgather_big(sb[4], "allgather_ffn_gate", COLLECTIVE_GATHER)
    (wu_g,) = _allgather_big(sb[5], "allgather_ffn_up", COLLECTIVE_GATHER)
    wd_parts = [_allgather_big([part], "allgather_ffn_down_%d" % q, COLLECTIVE_GATHER)[0]
                for q, part in enumerate(sb[6])]
    convw_g = _allgather_small(jnp.pad(conv_w.reshape(CONV_K, cws), ((0, 8 - CONV_K), (0, 0))), "allgather_conv_w")
    conv_w_full = convw_g[:, :CONV_K, :].transpose(1, 0, 2).reshape(CONV_K, sw)
    wpool = wpool_g.reshape(N_DEV, n_groups, gw, go // N_DEV).transpose(1, 2, 0, 3).reshape(n_groups, gw, go)
    wo = wo_g.reshape(d, d)

    h = _rms_fwd(x2d, norm1_g)
    proj = _proj_fwd(h, win_g)
    z = _conv_fwd(proj, conv_w_full, conv_b)
    p = _pool_fwd(proj)
    merged, *merge_factors = _merge_fwd(z, wa_g, p, wpool, proj, b_gate2, pool_scale)
    x1, h2 = _wo_fwd(merged, wo, x2d, norm2_g)
    gate = _ffn_gate_fwd(h2, wg_g)
    dadu, dadg, act = _ffn_up_act_fwd(h2, wu_g, gate)
    ffn_parts = [_ffn_down_fwd(act, wd, "ffn_down_fwd_%d" % q) for q, wd in enumerate(wd_parts)]
    dx2b, d_final_g, loss_blk = _loss_bwd(ffn_parts, x1, target, final_g2)

    other_chips = jnp.stack([2 * (1 - xi) + yi, 2 * xi + (1 - yi), 2 * (1 - xi) + (1 - yi)])
    others = jnp.concatenate([other_chips, 2 * other_chips + ci]).astype(jnp.int32)

    def partials(grads, recvs, names):
        if all(g.shape == grads[0].shape for g in grads):
            return list(_chip_partial(others, grads, recvs, "chip_partial_" + names[0]))
        return [_chip_partial(others, [g3], [r], "chip_partial_" + nm)[0] for g3, r, nm in zip(grads, recvs, names)]

    own = jnp.stack([me, my_chip]).astype(jnp.int32)

    def adam(idx, g3s, sibs, chipss):
        wmvs = [(view2d(big_w[a], a), view2d(big_m[a], a), view2d(big_v[a], a)) for a in idx]
        outs = _adam_big(own, wmvs, g3s, sibs, chipss, "adam_" + big_names[idx[0]])
        for a, o4 in zip(idx, outs):
            big_out[a] = [unview(o, a) for o in o4]

    big_out = [None] * len(big_names)
    dg_act, du_act = _ffn_gate_bwd(dx2b, wd_parts, dadg, dadu)
    gw_gate = _wgrad_rows(dg_act, h2, "wgrad_ffn_gate")
    gw_up = _wgrad_rows(du_act, h2, "wgrad_ffn_up")
    gw_down, sib_gu = _wgrad_rows(act, dx2b, "wgrad_ffn_down", carry=[gw_gate, gw_up])
    ps_gu = partials([gw_gate, gw_up], sib_gu, ["w_ffn_gate", "w_ffn_up"])
    chips_gu = _exchange_chips(ps_gu, "rs_chips_ffn_up", COLLECTIVE_CHIPS)
    dh2, sib_down = _input_grad([(dg_act, wg_g), (du_act, wu_g)], "ffn_in_bwd", after=ps_gu, carry=[gw_down])
    ps_down = partials([gw_down], sib_down, ["w_ffn_down"])
    chips_down = _exchange_chips(ps_down, "rs_chips_ffn_down", COLLECTIVE_CHIPS)
    dx1, dx1b, d_norm2_g = _rms_bwd(dh2, x1, norm2_g, dx2b, "rms2_bwd")
    dya, dyb, dproj42, d_b_gate, d_pool_scale = _wo_bwd(dx1b, wo, merge_factors, sw, after=ps_down)
    dproj = dproj42.reshape(N_DEV, s, sw)
    dproj, d_conv_w, d_conv_b = _conv_bwd(dproj, dya, wa_g, proj, conv_w_full, conv_b)
    dproj = _pool_bwd(dproj, dyb, wpool)
    gw_in = _wgrad_cols(h, dproj, "wgrad_in")
    gw_o, sib_in = _wgrad_full(merged, dx1b, "wgrad_o", carry=[gw_in])
    ps_in = partials([gw_in], sib_in, ["w_in"])
    chips_in = _exchange_chips(ps_in, "rs_chips_w_in", COLLECTIVE_CHIPS)
    gw_a = _wgrad_cols(z, dya, "wgrad_a_out", after=ps_in)
    gw_pool = _wgrad_pool(p, dyb, n_groups)
    mix3 = [gw_a,
            gw_pool.reshape(n_groups, gw, N_DEV, go // N_DEV).transpose(2, 0, 1, 3).reshape(N_DEV, n_groups * gw, go // N_DEV),
            gw_o.reshape(N_DEV, d // N_DEV, d)]
    adam([4, 5, 6], [gw_gate, gw_up, gw_down], sib_gu + sib_down, chips_gu + chips_down)
    dh, sib_mix = _input_grad([(dproj, win_g)], "proj_in_bwd", after=[big_out[6][0]], carry=mix3, per=2)
    ps_mix = partials(mix3, sib_mix, ["w_a_out", "w_pool", "w_o"])
    chips_mix = _exchange_chips(ps_mix, "rs_chips_mixer", COLLECTIVE_CHIPS)
    grad_x, _, d_norm1_g = _rms_bwd(dh, x2d, norm1_g, dx1, "rms1_bwd", with_bf16=False)
    adam([0], [gw_in], sib_in, chips_in)
    for k in range(3):
        adam([1 + k], [mix3[k]], [sib_mix[k]], [chips_mix[k]])

    small_parts = [d_norm1_g, d_b_gate, d_conv_w, d_conv_b, d_pool_scale, d_norm2_g, d_final_g, loss_blk]
    rows = [v.size // LANES for v in small_parts]
    row0 = [sum(rows[:k]) for k in range(len(rows))]
    packed = jnp.concatenate([_rows128(v) for v in small_parts], axis=0)
    gathered = _allgather_small(packed, "allgather_small_grads")
    small_names = ["norm1_g", "b_gate", "conv_b", "pool_scale", "norm2_g", "final_g", "conv_w"]
    small_w = [norm1_g, b_gate, conv_b, pool_scale, norm2_g, final_g]
    small_m = [m_norm1_g, m_b_gate, m_conv_b, m_pool_scale, m_norm2_g, m_final_g]
    small_v = [v_norm1_g, v_b_gate, v_conv_b, v_pool_scale, v_norm2_g, v_final_g]
    finished = _small_finish(gathered, [tuple(_rows128(t) for t in wmv) for wmv in zip(small_w, small_m, small_v)],
                             [row0[k] for k in (0, 1, 3, 4, 5, 6)], [(row0[2], rows[2]), (row0[7], rows[7])])
    g_convw_full, loss_rows = finished[0], finished[1]
    loss = loss_rows[0, 0]
    small_out = [[t.reshape(w.shape) for t in finished[2 + 4 * k:6 + 4 * k]] for k, w in enumerate(small_w)]
    g_convw = lax.dynamic_slice(g_convw_full.reshape(CONV_K, sw), (0, me * cws), (CONV_K, cws))
    cw_delta, cw_m, cw_v = _adam_small(conv_w.reshape(CONV_K, cws), g_convw,
                                       m_conv_w.reshape(CONV_K, cws), v_conv_w.reshape(CONV_K, cws))
    small_out.append([t.reshape(conv_w.shape) for t in (g_convw, cw_delta, cw_m, cw_v)])

    order = ["norm1_g", "w_in", "b_gate", "conv_w", "conv_b", "w_a_out", "w_pool", "pool_scale", "w_o", "norm2_g",
             "w_ffn_gate", "w_ffn_up", "w_ffn_down", "final_g"]
    per_kind = [{}, {}, {}, {}]
    for a, nm in enumerate(big_names):
        for kind in range(4):
            per_kind[kind][nm] = big_out[a][kind]
    for k, nm in enumerate(small_names):
        for kind in range(4):
            per_kind[kind][nm] = small_out[k][kind]
    result = [loss, grad_x.reshape(x.shape)]
    for kind in range(4):
        result += [per_kind[kind][nm] for nm in order]
    return tuple(result)
```

```python
import jax
import jax.numpy as jnp
from jax import lax
from jax.experimental import pallas as pl
from jax.experimental.pallas import tpu as pltpu
from jax.experimental.pallas import tpu_sc as plsc

F32 = jnp.float32
BF16 = jnp.bfloat16
MESH = pl.DeviceIdType.MESH

N_DEV = 8
EPS = 1e-6
CONV_K = 3
POOL_WINDOWS = (2, 4, 8, 16)
ADAM_LR = 0.001
ADAM_B1 = 0.9
ADAM_B2 = 0.999
ADAM_EPS = 1e-08
ADAM_WD = 0.01
ADAM_STEP = 10

V7X_VMEM_LIMIT_BYTES = 56 * 1024 * 1024
LANES = 128

COLLECTIVE_GATHER = 1
COLLECTIVE_SIBLING = 2
COLLECTIVE_CHIPS = 3
SEQUENCER_COST_BYTES = 4 * 10**9

NN = ((1,), (0,))
NT = ((1,), (1,))
TN = ((0,), (0,))


def _dot(a, b, dims):
    return lax.dot_general(a, b, (dims, ((), ())), preferred_element_type=F32)


def _cp(n_axes):
    return pltpu.CompilerParams(dimension_semantics=("arbitrary",) * n_axes,
                                vmem_limit_bytes=V7X_VMEM_LIMIT_BYTES)


def _row_tile(rows, bytes_per_row, cap_bytes):
    best = None
    for t in range(16, rows + 1, 16):
        if rows % t == 0 and t * bytes_per_row <= cap_bytes:
            best = t
    return best if best is not None else rows


def _chunks(total, size):
    size = min(size, total)
    assert total % size == 0
    return [slice(r, r + size) for r in range(0, total, size)]


def _after_specs(after):
    return [pl.BlockSpec(memory_space=pl.ANY)] * len(after)


def _shift_down(v, k):
    row = lax.broadcasted_iota(jnp.int32, v.shape, 0)
    return jnp.where(row >= k, pltpu.roll(v, k, 0), 0.0)


def _shift_up(v, k):
    n = v.shape[0]
    row = lax.broadcasted_iota(jnp.int32, v.shape, 0)
    return jnp.where(row < n - k, pltpu.roll(v, n - k, 0), 0.0)


def _sigmoid(v):
    return jax.nn.sigmoid(v)


def _cast_bf16(w2d, name, parts=1):
    rows, cols = w2d.shape
    tr = _row_tile(rows, cols * 4, 2 << 20)
    pc = cols // parts

    def body(i_ref, *o_refs):
        for q, o_ref in enumerate(o_refs):
            o_ref[...] = i_ref[:, q * pc:(q + 1) * pc].astype(BF16)

    return pl.pallas_call(
        body, name=name, grid=(rows // tr,),
        in_specs=[pl.BlockSpec((tr, cols), lambda i: (i, 0))],
        out_specs=[pl.BlockSpec((tr, pc), lambda i: (i, 0))] * parts,
        out_shape=[jax.ShapeDtypeStruct((rows, pc), BF16)] * parts,
        compiler_params=_cp(1),
    )(w2d)


def _rms_fwd(x2d, g):
    s, d = x2d.shape
    tm = min(256, s)

    def body(x_ref, g_ref, h_ref):
        xv = x_ref[...]
        r = lax.rsqrt(jnp.mean(xv * xv, axis=-1, keepdims=True) + EPS)
        h_ref[...] = (xv * r * g_ref[...]).astype(BF16)

    return pl.pallas_call(
        body, name="rms1_fwd", grid=(s // tm,),
        in_specs=[pl.BlockSpec((tm, d), lambda i: (i, 0)), pl.BlockSpec((1, d), lambda i: (0, 0))],
        out_specs=pl.BlockSpec((tm, d), lambda i: (i, 0)),
        out_shape=jax.ShapeDtypeStruct((s, d), BF16),
        compiler_params=_cp(1),
    )(x2d, g)


def _coords():
    return lax.axis_index("x"), lax.axis_index("y"), lax.axis_index("c")


def _slot(p):
    return 4 * p[0] + 2 * p[1] + p[2]


def _handshake(peers):
    barrier = pltpu.get_barrier_semaphore()
    for peer in peers:
        pl.semaphore_signal(barrier, inc=1, device_id=peer, device_id_type=MESH)
    pl.semaphore_wait(barrier, len(peers))


def _sequencer_call(body, out_type, scratch_types, name, collective_id):
    return pl.kernel(
        body, out_type=out_type, name=name,
        mesh=plsc.ScalarSubcoreMesh(axis_name="seq", num_cores=1),
        scratch_types=scratch_types,
        cost_estimate=pl.CostEstimate(flops=0, transcendentals=0, bytes_accessed=SEQUENCER_COST_BYTES),
        compiler_params=pltpu.CompilerParams(collective_id=collective_id))


def _allgather_big(shards, name, collective_id, after=()):
    n = len(shards)

    def body(*refs):
        ins, outs = refs[:n], refs[n + len(after):2 * n + len(after)]
        send_sems, recv_sems, local_sems = refs[2 * n + len(after):]
        x, y, c = _coords()
        me, sibling = (x, y, c), (x, y, 1 - c)
        x_nbr, y_nbr, diag = (1 - x, y), (x, 1 - y), (1 - x, 1 - y)
        relay_from = (x + (1 - c) * (1 - 2 * x), y + c * (1 - 2 * y))
        relay_to = (x + c * (1 - 2 * x), y + (1 - c) * (1 - 2 * y))
        _handshake([sibling, (*x_nbr, c), (*y_nbr, c)])

        def copy(a, k, block, to, src=None):
            dst = outs[a].at[_slot(block)]
            return pltpu.make_async_remote_copy(
                src_ref=dst if src is None else src, dst_ref=dst,
                send_sem=send_sems.at[a, k], recv_sem=recv_sems.at[a, k],
                device_id=to, device_id_type=MESH)

        mine, sends = [], []
        for a in range(n):
            cp = pltpu.make_async_copy(ins[a], outs[a].at[_slot(me)], local_sems.at[a])
            cp.start()
            mine.append(cp)
            first = [copy(a, 0, me, sibling, src=ins[a]),
                     copy(a, 1, me, (*x_nbr, c), src=ins[a]),
                     copy(a, 2, me, (*y_nbr, c), src=ins[a])]
            for cp in first:
                cp.start()
            sends += first
        for a in range(n):
            copy(a, 1 + c, (*relay_from, c), me).wait_recv()
            passed = [copy(a, 3, (*relay_from, c), (*relay_to, c)), copy(a, 4 + c, (*relay_from, c), sibling)]
            for cp in passed:
                cp.start()
            copy(a, 2 - c, (*relay_to, c), me).wait_recv()
            cp = copy(a, 5 - c, (*relay_to, c), sibling)
            cp.start()
            passed.append(cp)
            copy(a, 3, (*diag, c), me).wait_recv()
            cp = copy(a, 6, (*diag, c), sibling)
            cp.start()
            sends += passed + [cp]
        for a in range(n):
            copy(a, 0, sibling, me).wait_recv()
            copy(a, 4, (*x_nbr, 1 - c), me).wait_recv()
            copy(a, 5, (*y_nbr, 1 - c), me).wait_recv()
            copy(a, 6, (*diag, 1 - c), me).wait_recv()
        for cp in sends:
            cp.wait_send()
        for cp in mine:
            cp.wait()

    return _sequencer_call(
        body, [jax.ShapeDtypeStruct((N_DEV,) + s.shape, s.dtype) for s in shards],
        [pltpu.SemaphoreType.DMA((n, 7)), pltpu.SemaphoreType.DMA((n, 7)), pltpu.SemaphoreType.DMA((n,))],
        name, collective_id)(*shards, *after)


def _sibling_copies(ins, recvs, send_sems, recv_sems):
    x, y, c = _coords()
    return [pltpu.make_async_remote_copy(
        src_ref=ins[a].at[2 * q + (1 - c)], dst_ref=recvs[a].at[q],
        send_sem=send_sems.at[a, q], recv_sem=recv_sems.at[a, q],
        device_id=(x, y, 1 - c), device_id_type=MESH) for a in range(len(ins)) for q in range(4)]


def _carry_specs(carry):
    any_spec = pl.BlockSpec(memory_space=pl.ANY)
    n = len(carry)
    sems = [pltpu.SemaphoreType.DMA((n, 4)), pltpu.SemaphoreType.DMA((n, 4))] if n else []
    return ([any_spec] * n, [any_spec] * n,
            [jax.ShapeDtypeStruct((4,) + g.shape[1:], g.dtype) for g in carry], sems)


def _carry_run(first, last, ins, recvs, sems):
    if not ins:
        return

    @pl.when(first)
    def _():
        x, y, c = _coords()
        _handshake([(x, y, 1 - c)])
        for cp in _sibling_copies(ins, recvs, *sems):
            cp.start()

    @pl.when(last)
    def _():
        copies = _sibling_copies(ins, recvs, *sems)
        for cp in copies:
            cp.wait_recv()
        for cp in copies:
            cp.wait_send()


def _cp_carry(n_axes, carry):
    if not carry:
        return _cp(n_axes)
    return pltpu.CompilerParams(dimension_semantics=("arbitrary",) * n_axes, vmem_limit_bytes=V7X_VMEM_LIMIT_BYTES,
                                collective_id=COLLECTIVE_SIBLING)


def _exchange_chips(psums, name, collective_id):
    n = len(psums)

    def body(*refs):
        ins, outs = refs[:n], refs[n:2 * n]
        send_sems, recv_sems = refs[2 * n:]
        x, y, c = _coords()
        chips = [(1 - x, y), (x, 1 - y), (1 - x, 1 - y)]
        _handshake([(*chip, c) for chip in chips])
        copies = []
        for a in range(n):
            for j, chip in enumerate(chips):
                cp = pltpu.make_async_remote_copy(
                    src_ref=ins[a].at[2 * chip[0] + chip[1]], dst_ref=outs[a].at[j],
                    send_sem=send_sems.at[a, j], recv_sem=recv_sems.at[a, j],
                    device_id=(*chip, c), device_id_type=MESH)
                cp.start()
                copies.append(cp)
        for cp in copies:
            cp.wait_recv()
        for cp in copies:
            cp.wait_send()

    return _sequencer_call(
        body, [jax.ShapeDtypeStruct((3,) + p.shape[1:], p.dtype) for p in psums],
        [pltpu.SemaphoreType.DMA((n, 3)), pltpu.SemaphoreType.DMA((n, 3))],
        name, collective_id)(*psums)


def _allgather_small(v2d, name):
    rows, cols = v2d.shape

    def body(v_ref, out_ref, send_sems, recv_sems):
        x, y, c = _coords()
        me = (x, y, c)
        out_ref[_slot(me)] = v_ref[...]
        peers = []
        for k in range(1, N_DEV):
            fx, fy, fc = (k >> 2) & 1, (k >> 1) & 1, k & 1
            peers.append(((1 - x) if fx else x, (1 - y) if fy else y, (1 - c) if fc else c))
        sends = []
        for k, peer in enumerate(peers):
            cp = pltpu.make_async_remote_copy(
                src_ref=v_ref, dst_ref=out_ref.at[_slot(me)],
                send_sem=send_sems.at[k], recv_sem=recv_sems.at[k],
                device_id=peer, device_id_type=MESH)
            cp.start()
            sends.append(cp)
        for k, peer in enumerate(peers):
            pltpu.make_async_remote_copy(
                src_ref=v_ref, dst_ref=out_ref.at[_slot(peer)],
                send_sem=send_sems.at[k], recv_sem=recv_sems.at[k],
                device_id=peer, device_id_type=MESH).wait_recv()
        for cp in sends:
            cp.wait_send()

    vmem = pl.BlockSpec(memory_space=pltpu.VMEM)
    return pl.pallas_call(
        body, name=name, in_specs=[vmem], out_specs=vmem,
        out_shape=jax.ShapeDtypeStruct((N_DEV, rows, cols), v2d.dtype),
        scratch_shapes=[pltpu.SemaphoreType.DMA((N_DEV - 1,)), pltpu.SemaphoreType.DMA((N_DEV - 1,))],
    )(v2d)


def _chip_partial(others, grads, recvs, name):
    n = len(grads)
    _, rows, cols = grads[0].shape
    tr = _row_tile(rows, cols * 2, (2 << 20) // n)

    def body(others_ref, *refs):
        for a in range(n):
            refs[2 * n + a][...] = (refs[a][...].astype(F32) + refs[n + a][...].astype(F32)).astype(BF16)

    return pl.pallas_call(
        body, name=name,
        grid_spec=pltpu.PrefetchScalarGridSpec(
            num_scalar_prefetch=1, grid=(3, rows // tr),
            in_specs=[pl.BlockSpec((None, tr, cols), lambda k, i, o: (o[3 + k], i, 0))] * n
            + [pl.BlockSpec((None, tr, cols), lambda k, i, o: (o[k], i, 0))] * n,
            out_specs=[pl.BlockSpec((None, tr, cols), lambda k, i, o: (o[k], i, 0))] * n),
        out_shape=[jax.ShapeDtypeStruct((4, rows, cols), BF16)] * n,
        compiler_params=_cp(2),
    )(others, *grads, *recvs)


def _adam_math(w, g, m, v):
    m = ADAM_B1 * m + (1.0 - ADAM_B1) * g
    v = ADAM_B2 * v + (1.0 - ADAM_B2) * (g * g)
    m_hat = m / (1.0 - ADAM_B1 ** ADAM_STEP)
    v_hat = v / (1.0 - ADAM_B2 ** ADAM_STEP)
    delta = -ADAM_LR * (m_hat / (jnp.sqrt(v_hat) + ADAM_EPS) + ADAM_WD * w)
    return delta, m, v


def _adam_big(own, wmvs, g3s, recv_sibs, recv_chipss, name):
    n = len(wmvs)
    rows, cols = wmvs[0][0].shape
    tr = _row_tile(rows, cols * 4, (2 << 20) // n)

    def body(own_ref, *refs):
        ins, outs = refs[:6 * n], refs[6 * n:]
        for a in range(n):
            w_ref, m_ref, v_ref, g_ref, rs_ref, rc_ref = ins[6 * a:6 * a + 6]
            g = g_ref[...].astype(F32) + rs_ref[...].astype(F32)
            g = g + rc_ref[0].astype(F32)
            g = g + rc_ref[1].astype(F32)
            g = g + rc_ref[2].astype(F32)
            delta, m_new, v_new = _adam_math(w_ref[...], g, m_ref[...], v_ref[...])
            outs[4 * a][...] = g
            outs[4 * a + 1][...] = delta
            outs[4 * a + 2][...] = m_new
            outs[4 * a + 3][...] = v_new

    blk = pl.BlockSpec((tr, cols), lambda i, o: (i, 0))
    per_shard = [blk, blk, blk,
                 pl.BlockSpec((None, tr, cols), lambda i, o: (o[0], i, 0)),
                 pl.BlockSpec((None, tr, cols), lambda i, o: (o[1], i, 0)),
                 pl.BlockSpec((3, tr, cols), lambda i, o: (0, i, 0))]
    out = jax.ShapeDtypeStruct((rows, cols), F32)
    args = [t for a in range(n) for t in (*wmvs[a], g3s[a], recv_sibs[a], recv_chipss[a])]
    outs = pl.pallas_call(
        body, name=name,
        grid_spec=pltpu.PrefetchScalarGridSpec(
            num_scalar_prefetch=1, grid=(rows // tr,),
            in_specs=per_shard * n, out_specs=[blk] * (4 * n)),
        out_shape=[out] * (4 * n),
        compiler_params=_cp(1),
    )(own, *args)
    return [outs[4 * a:4 * a + 4] for a in range(n)]


def _small_finish(gathered, params, row_offs, extra_rows):
    n = len(params)

    def body(g_ref, *refs):
        ins, outs = refs[:3 * n], refs[3 * n:]
        total = g_ref[0]
        for k in range(1, N_DEV):
            total = total + g_ref[k]
        for e, (r0, nr) in enumerate(extra_rows):
            outs[e][...] = total[r0:r0 + nr, :]
        for p in range(n):
            w_ref, m_ref, v_ref = ins[3 * p:3 * p + 3]
            g_out, d_out, m_out, v_out = outs[len(extra_rows) + 4 * p:len(extra_rows) + 4 * p + 4]
            g = total[row_offs[p]:row_offs[p] + w_ref.shape[0], :]
            delta, m_new, v_new = _adam_math(w_ref[...], g, m_ref[...], v_ref[...])
            g_out[...] = g
            d_out[...] = delta
            m_out[...] = m_new
            v_out[...] = v_new

    vmem = pl.BlockSpec(memory_space=pltpu.VMEM)
    out_shape = [jax.ShapeDtypeStruct((nr, LANES), F32) for _, nr in extra_rows]
    for w, _, _ in params:
        out_shape += [jax.ShapeDtypeStruct(w.shape, F32)] * 4
    flat = [t for wmv in params for t in wmv]
    return pl.pallas_call(body, name="small_finish", in_specs=[vmem] * (1 + len(flat)),
                          out_specs=[vmem] * len(out_shape), out_shape=out_shape)(gathered, *flat)


def _adam_small(w, g, m, v):
    def body(w_ref, g_ref, m_ref, v_ref, do_ref, mo_ref, vo_ref):
        delta, m_new, v_new = _adam_math(w_ref[...], g_ref[...], m_ref[...], v_ref[...])
        do_ref[...] = delta
        mo_ref[...] = m_new
        vo_ref[...] = v_new

    vmem = pl.BlockSpec(memory_space=pltpu.VMEM)
    out = jax.ShapeDtypeStruct(w.shape, F32)
    return pl.pallas_call(body, name="adam_small", in_specs=[vmem] * 4, out_specs=[vmem] * 3,
                          out_shape=[out, out, out])(w, g, m, v)


def _proj_fwd(h, win_g):
    s, d = h.shape
    sw = win_g.shape[2]
    tn = min(512, sw)
    nh = sw // tn

    def body(h_ref, w_ref, o_ref):
        for rs in _chunks(s, 512):
            o_ref[rs, :] = _dot(h_ref[rs, :], w_ref[...], NN)

    return pl.pallas_call(
        body, name="proj_fwd", grid=(N_DEV * nh,),
        in_specs=[pl.BlockSpec((s, d), lambda j: (0, 0)),
                  pl.BlockSpec((None, d, tn), lambda j: (j // nh, 0, j % nh))],
        out_specs=pl.BlockSpec((None, s, tn), lambda j: (j // nh, 0, j % nh)),
        out_shape=jax.ShapeDtypeStruct((N_DEV, s, sw), F32),
        compiler_params=_cp(1),
    )(h, win_g)


def _conv_fwd(proj, conv_w, conv_b):
    _, s, sw = proj.shape
    tc = min(LANES, sw)

    def body(ba_ref, ca_ref, va_ref, cw_ref, cb_ref, z_ref):
        cv = ca_ref[...] * va_ref[...]
        u = (cb_ref[...] + cw_ref[0:1, :] * _shift_down(cv, 2) + cw_ref[1:2, :] * _shift_down(cv, 1)
             + cw_ref[2:3, :] * cv)
        z_ref[...] = (ba_ref[...] * u).astype(BF16)

    def part(k):
        return pl.BlockSpec((None, s, tc), lambda i: (k, 0, i))

    return pl.pallas_call(
        body, name="conv_fwd", grid=(sw // tc,),
        in_specs=[part(0), part(1), part(2),
                  pl.BlockSpec((CONV_K, tc), lambda i: (0, i)), pl.BlockSpec((1, tc), lambda i: (0, i))],
        out_specs=pl.BlockSpec((s, tc), lambda i: (0, i)),
        out_shape=jax.ShapeDtypeStruct((s, sw), BF16),
        compiler_params=_cp(1),
    )(proj, proj, proj, conv_w, conv_b)


def _pool_counts(shape, window):
    t = lax.broadcasted_iota(jnp.int32, shape, 0)
    return jnp.minimum(t + 1, window).astype(F32)


def _pool_fwd(proj):
    _, s, sw = proj.shape
    gw = sw // len(POOL_WINDOWS)

    def body(v_ref, p_ref):
        for gi, window in enumerate(POOL_WINDOWS):
            @pl.when(pl.program_id(0) == gi)
            def _():
                v = v_ref[...]
                acc, k = v, 1
                while k < window:
                    acc = acc + _shift_down(acc, k)
                    k *= 2
                p_ref[...] = (acc / _pool_counts(v.shape, window) - v).astype(BF16)

    return pl.pallas_call(
        body, name="pool_fwd", grid=(len(POOL_WINDOWS),),
        in_specs=[pl.BlockSpec((None, s, gw), lambda g: (3, 0, g))],
        out_specs=pl.BlockSpec((s, gw), lambda g: (0, g)),
        out_shape=jax.ShapeDtypeStruct((s, sw), BF16),
        compiler_params=_cp(1),
    )(proj)


def _merge_fwd(z, wa, p, wpool, proj, b_gate2, pool_scale):
    s, sw = z.shape
    tn = wa.shape[2]
    d = tn * N_DEV
    gw = sw // len(POOL_WINDOWS)
    nq = sw // tn

    def body(z_ref, wa_ref, p_ref, wp_ref, ga_ref, gb_ref, bg_ref, sc_ref,
             m_ref, dya_ref, dyb_ref, dga_ref, dgb_ref, dsc_ref):
        for rs in _chunks(s, 512):
            ya = _dot(z_ref[rs, :], wa_ref[...], NN)
            yb = _dot(p_ref[rs, :], wp_ref[...], NN)
            sa = _sigmoid(ga_ref[rs, :] + bg_ref[0:1, :])
            sb = _sigmoid(gb_ref[rs, :] + bg_ref[1:2, :])
            sc = sc_ref[...]
            sb_yb = sb * yb
            m_ref[rs, :] = (sa * ya + sb_yb * sc).astype(BF16)
            dya_ref[rs, :] = sa.astype(BF16)
            dyb_ref[rs, :] = (sb * sc).astype(BF16)
            dga_ref[rs, :] = (ya * (sa * (1.0 - sa))).astype(BF16)
            dgb_ref[rs, :] = ((yb * sc) * (sb * (1.0 - sb))).astype(BF16)
            dsc_ref[rs, :] = sb_yb.astype(BF16)

    col = pl.BlockSpec((s, tn), lambda j: (0, j))
    out = jax.ShapeDtypeStruct((s, d), BF16)
    return pl.pallas_call(
        body, name="merge_fwd", grid=(N_DEV,),
        in_specs=[pl.BlockSpec((s, sw), lambda j: (0, 0)),
                  pl.BlockSpec((None, sw, tn), lambda j: (j, 0, 0)),
                  pl.BlockSpec((s, gw), lambda j: (0, j // 2)),
                  pl.BlockSpec((None, gw, tn), lambda j: (j // 2, 0, j % 2)),
                  pl.BlockSpec((None, s, tn), lambda j: (4 + j // nq, 0, j % nq)),
                  pl.BlockSpec((None, s, tn), lambda j: (6 + j // nq, 0, j % nq)),
                  pl.BlockSpec((2, tn), lambda j: (0, j)),
                  pl.BlockSpec((1, tn), lambda j: (0, j))],
        out_specs=[col] * 6,
        out_shape=[out] * 6,
        compiler_params=_cp(1),
    )(z, wa, p, wpool, proj, proj, b_gate2, pool_scale)


def _wo_fwd(merged, wo, x2d, g2):
    s, d = x2d.shape
    tm = min(256, s)

    def body(m_ref, wo_ref, x_ref, g_ref, x1_ref, h2_ref):
        x1 = x_ref[...] + _dot(m_ref[...], wo_ref[...], NN)
        x1_ref[...] = x1
        r = lax.rsqrt(jnp.mean(x1 * x1, axis=-1, keepdims=True) + EPS)
        h2_ref[...] = (x1 * r * g_ref[...]).astype(BF16)

    row = pl.BlockSpec((tm, d), lambda i: (i, 0))
    return pl.pallas_call(
        body, name="wo_fwd", grid=(s // tm,),
        in_specs=[row, pl.BlockSpec((d, d), lambda i: (0, 0)), row, pl.BlockSpec((1, d), lambda i: (0, 0))],
        out_specs=[row, row],
        out_shape=[jax.ShapeDtypeStruct((s, d), F32), jax.ShapeDtypeStruct((s, d), BF16)],
        compiler_params=_cp(1),
    )(merged, wo, x2d, g2)


def _ffn_gate_fwd(h2, wg_g):
    s, d = h2.shape
    f8 = wg_g.shape[2]
    th = min(1024, s)

    def body(h_ref, wg_ref, g_ref):
        i = pl.program_id(1)
        for rs in _chunks(th, 512):
            rows = pl.ds(pl.multiple_of(i * th + rs.start, rs.stop - rs.start), rs.stop - rs.start)
            g_ref[rs, :] = _dot(h_ref[rows, :], wg_ref[...], NN)

    return pl.pallas_call(
        body, name="ffn_gate_fwd", grid=(N_DEV, s // th),
        in_specs=[pl.BlockSpec((s, d), lambda j, i: (0, 0)), pl.BlockSpec((None, d, f8), lambda j, i: (j, 0, 0))],
        out_specs=pl.BlockSpec((None, th, f8), lambda j, i: (j, i, 0)),
        out_shape=jax.ShapeDtypeStruct((N_DEV, s, f8), F32),
        compiler_params=_cp(2),
    )(h2, wg_g)


def _ffn_up_act_fwd(h2, wu_g, gate):
    s, d = h2.shape
    f8 = wu_g.shape[2]
    th = min(1024, s)

    def body(h_ref, wu_ref, g_ref, dadu_ref, dadg_ref, a_ref, u_ref):
        i = pl.program_id(1)
        chunks = _chunks(th, 256)

        def matmul(rs):
            rows = pl.ds(pl.multiple_of(i * th + rs.start, rs.stop - rs.start), rs.stop - rs.start)
            u_ref[rs, :] = _dot(h_ref[rows, :], wu_ref[...], NN)

        matmul(chunks[0])
        for k, rs in enumerate(chunks):
            if k + 1 < len(chunks):
                matmul(chunks[k + 1])
            g = g_ref[rs, :]
            u = u_ref[rs, :]
            sg = _sigmoid(g)
            silu = g * sg
            dadu_ref[rs, :] = silu.astype(BF16)
            dadg_ref[rs, :] = (u * (sg * (1.0 + g * (1.0 - sg)))).astype(BF16)
            a_ref[rs, :] = (silu * u).astype(BF16)

    wspec = pl.BlockSpec((None, d, f8), lambda j, i: (j, 0, 0))
    ospec = pl.BlockSpec((None, th, f8), lambda j, i: (j, i, 0))
    out = jax.ShapeDtypeStruct((N_DEV, s, f8), BF16)
    return pl.pallas_call(
        body, name="ffn_up_fwd", grid=(N_DEV, s // th),
        in_specs=[pl.BlockSpec((s, d), lambda j, i: (0, 0)), wspec, ospec],
        out_specs=[ospec, ospec, ospec], out_shape=[out, out, out],
        scratch_shapes=[pltpu.VMEM((th, f8), F32)],
        compiler_params=_cp(2),
    )(h2, wu_g, gate)


def _ffn_down_fwd(act, wd_part, name):
    _, s, f8 = act.shape
    tn = wd_part.shape[2]
    per = 2

    def body(a_ref, wd_ref, o_ref):
        @pl.when(pl.program_id(0) == 0)
        def _():
            o_ref[...] = jnp.zeros_like(o_ref)

        for rs in _chunks(s, 1024):
            part = _dot(a_ref[0, rs, :], wd_ref[0], NN)
            for q in range(1, per):
                part = part + _dot(a_ref[q, rs, :], wd_ref[q], NN)
            o_ref[rs, :] += part

    return pl.pallas_call(
        body, name=name, grid=(N_DEV // per,),
        in_specs=[pl.BlockSpec((per, s, f8), lambda j: (j, 0, 0)),
                  pl.BlockSpec((per, f8, tn), lambda j: (j, 0, 0))],
        out_specs=pl.BlockSpec((s, tn), lambda j: (0, 0)),
        out_shape=jax.ShapeDtypeStruct((s, tn), F32),
        compiler_params=_cp(1),
    )(act, wd_part)


def _loss_bwd(ffn_parts, x1, target, final_g):
    s, d = x1.shape
    tm = min(256, s)
    nparts = len(ffn_parts)

    def body(*refs):
        f_refs = refs[:nparts]
        x1_ref, t_ref, gf_ref, dxb_ref, dgf_ref, loss_ref = refs[nparts:]

        @pl.when(pl.program_id(0) == 0)
        def _():
            dgf_ref[...] = jnp.zeros_like(dgf_ref)
            loss_ref[...] = jnp.zeros_like(loss_ref)

        x2 = x1_ref[...] + jnp.concatenate([f_ref[...] for f_ref in f_refs], axis=-1)
        r = lax.rsqrt(jnp.mean(x2 * x2, axis=-1, keepdims=True) + EPS)
        nrm = x2 * r
        gf = gf_ref[...]
        err = nrm * gf - t_ref[...]
        loss_ref[...] += jnp.sum(err * err) * (0.5 / d)
        dy = err * (1.0 / d)
        dgf_ref[...] += jnp.sum(dy * nrm, axis=0, keepdims=True)
        dn = dy * gf
        dx = r * (dn - nrm * jnp.mean(dn * nrm, axis=-1, keepdims=True))
        dxb_ref[...] = dx.astype(BF16)

    row = pl.BlockSpec((tm, d), lambda i: (i, 0))
    vec = pl.BlockSpec((1, d), lambda i: (0, 0))
    return pl.pallas_call(
        body, name="loss_bwd", grid=(s // tm,),
        in_specs=[pl.BlockSpec((tm, f.shape[1]), lambda i: (i, 0)) for f in ffn_parts] + [row, row, vec],
        out_specs=[row, vec, pl.BlockSpec((8, LANES), lambda i: (0, 0))],
        out_shape=[jax.ShapeDtypeStruct((s, d), BF16),
                   jax.ShapeDtypeStruct((1, d), F32), jax.ShapeDtypeStruct((8, LANES), F32)],
        compiler_params=_cp(1),
    )(*ffn_parts, x1, target, final_g)


def _ffn_gate_bwd(dx2b, wd_parts, dadg, dadu):
    s, d = dx2b.shape
    f8 = dadg.shape[2]
    th = min(1024, s)
    nparts = len(wd_parts)
    pc = d // nparts

    def body(dx_ref, *refs):
        wd_refs = refs[:nparts]
        g_ref, u_ref, dg_ref, du_ref, da_ref = refs[nparts:]
        i = pl.program_id(1)
        chunks = _chunks(th, 256)

        def matmul(rs):
            rows = pl.ds(pl.multiple_of(i * th + rs.start, rs.stop - rs.start), rs.stop - rs.start)
            part = None
            for q, wd_ref in enumerate(wd_refs):
                term = _dot(dx_ref[rows, q * pc:(q + 1) * pc], wd_ref[...], NT)
                part = term if part is None else part + term
            da_ref[rs, :] = part

        matmul(chunks[0])
        for k, rs in enumerate(chunks):
            if k + 1 < len(chunks):
                matmul(chunks[k + 1])
            da = da_ref[rs, :].astype(BF16)
            dg_ref[rs, :] = da * g_ref[rs, :]
            du_ref[rs, :] = da * u_ref[rs, :]

    aspec = pl.BlockSpec((None, th, f8), lambda j, i: (j, i, 0))
    out = jax.ShapeDtypeStruct((N_DEV, s, f8), BF16)
    return pl.pallas_call(
        body, name="ffn_act_bwd", grid=(N_DEV, s // th),
        in_specs=[pl.BlockSpec((s, d), lambda j, i: (0, 0))]
        + [pl.BlockSpec((None, f8, pc), lambda j, i: (j, 0, 0))] * nparts + [aspec, aspec],
        out_specs=[aspec, aspec], out_shape=[out, out],
        scratch_shapes=[pltpu.VMEM((th, f8), F32)],
        compiler_params=_cp(2),
    )(dx2b, *wd_parts, dadg, dadu)


def _wgrad_rows(a3, b, name, after=(), carry=()):
    _, s, k = a3.shape
    n = b.shape[1]
    nc = len(carry)
    c_in, c_out, c_shape, c_sems = _carry_specs(carry)

    def body(a_ref, b_ref, *rest):
        rest = rest[len(after):]
        o_ref = rest[nc]
        j = pl.program_id(0)
        _carry_run(j == 0, j == N_DEV - 1, rest[:nc], rest[nc + 1:2 * nc + 1], rest[2 * nc + 1:])
        o_ref[...] = _dot(a_ref[...], b_ref[...], TN).astype(BF16)

    outs = pl.pallas_call(
        body, name=name, grid=(N_DEV,),
        in_specs=[pl.BlockSpec((None, s, k), lambda j: (j, 0, 0)),
                  pl.BlockSpec((s, n), lambda j: (0, 0))] + _after_specs(after) + c_in,
        out_specs=[pl.BlockSpec((None, k, n), lambda j: (j, 0, 0))] + c_out,
        out_shape=[jax.ShapeDtypeStruct((N_DEV, k, n), BF16)] + c_shape,
        scratch_shapes=c_sems,
        compiler_params=_cp_carry(1, carry),
    )(a3, b, *after, *carry)
    return (outs[0], list(outs[1:])) if nc else outs[0]


def _wgrad_cols(a, b3, name, after=()):
    s, k = a.shape
    if b3.ndim == 2:
        n = b3.shape[1] // N_DEV
        b_spec = pl.BlockSpec((s, n), lambda j: (0, j))
    else:
        n = b3.shape[2]
        b_spec = pl.BlockSpec((None, s, n), lambda j: (j, 0, 0))

    def body(a_ref, b_ref, *rest):
        o_ref = rest[len(after)]
        o_ref[...] = _dot(a_ref[...], b_ref[...], TN).astype(BF16)

    return pl.pallas_call(
        body, name=name, grid=(N_DEV,),
        in_specs=[pl.BlockSpec((s, k), lambda j: (0, 0)), b_spec] + _after_specs(after),
        out_specs=pl.BlockSpec((None, k, n), lambda j: (j, 0, 0)),
        out_shape=jax.ShapeDtypeStruct((N_DEV, k, n), BF16),
        compiler_params=_cp(1),
    )(a, b3, *after)


def _input_grad(pairs, name, after=(), carry=(), per=1):
    s = pairs[0][0].shape[1]
    d = pairs[0][1].shape[1]
    tn = min(1024, d)
    npair = len(pairs)
    nc = len(carry)
    c_in, c_out, c_shape, c_sems = _carry_specs(carry)

    def body(*refs):
        ops = refs[:2 * npair]
        rest = refs[2 * npair + len(after):]
        o_ref, acc_ref = rest[nc], rest[-1]
        nh, j = pl.program_id(0), pl.program_id(1)
        last_j = N_DEV // per - 1
        _carry_run((nh == 0) & (j == 0), (nh == d // tn - 1) & (j == last_j),
                   rest[:nc], rest[nc + 1:2 * nc + 1], rest[2 * nc + 1:-1])

        @pl.when(j == 0)
        def _():
            acc_ref[...] = jnp.zeros_like(acc_ref)

        for rs in _chunks(s, 1024):
            part = None
            for q in range(npair):
                for e in range(per):
                    term = _dot(ops[2 * q][e, rs, :], ops[2 * q + 1][e], NT)
                    part = term if part is None else part + term
            acc_ref[rs, :] += part

        @pl.when(j == last_j)
        def _():
            o_ref[...] = acc_ref[...].astype(BF16)

    in_specs, args = [], []
    for a3, w3 in pairs:
        k = a3.shape[2]
        in_specs += [pl.BlockSpec((per, s, k), lambda n, j: (j, 0, 0)),
                     pl.BlockSpec((per, tn, k), lambda n, j: (j, n, 0))]
        args += [a3, w3]
    outs = pl.pallas_call(
        body, name=name, grid=(d // tn, N_DEV // per),
        in_specs=in_specs + _after_specs(after) + c_in,
        out_specs=[pl.BlockSpec((s, tn), lambda n, j: (0, n))] + c_out,
        out_shape=[jax.ShapeDtypeStruct((s, d), BF16)] + c_shape,
        scratch_shapes=c_sems + [pltpu.VMEM((s, tn), F32)],
        compiler_params=_cp_carry(2, carry),
    )(*args, *after, *carry)
    return (outs[0], list(outs[1:])) if nc else outs[0]


def _rms_bwd(dh, xres, g, dres, name, with_bf16=True):
    s, d = xres.shape
    tm = min(256, s)

    def body(dh_ref, x_ref, g_ref, dres_ref, dx_ref, *rest):
        dg_ref = rest[-1]
        @pl.when(pl.program_id(0) == 0)
        def _():
            dg_ref[...] = jnp.zeros_like(dg_ref)

        xv = x_ref[...]
        dh_v = dh_ref[...].astype(F32)
        r = lax.rsqrt(jnp.mean(xv * xv, axis=-1, keepdims=True) + EPS)
        nrm = xv * r
        dg_ref[...] += jnp.sum(dh_v * nrm, axis=0, keepdims=True)
        dn = dh_v * g_ref[...]
        dx = dres_ref[...].astype(F32) + r * (dn - nrm * jnp.mean(dn * nrm, axis=-1, keepdims=True))
        dx_ref[...] = dx
        if with_bf16:
            rest[0][...] = dx.astype(BF16)

    row = pl.BlockSpec((tm, d), lambda i: (i, 0))
    vec = pl.BlockSpec((1, d), lambda i: (0, 0))
    copies = [jax.ShapeDtypeStruct((s, d), BF16)] if with_bf16 else []
    outs = pl.pallas_call(
        body, name=name, grid=(s // tm,),
        in_specs=[row, row, vec, row],
        out_specs=[row] + [row] * len(copies) + [vec],
        out_shape=[jax.ShapeDtypeStruct((s, d), F32)] + copies + [jax.ShapeDtypeStruct((1, d), F32)],
        compiler_params=_cp(1),
    )(dh, xres, g, dres)
    return (outs[0], outs[1], outs[2]) if with_bf16 else (outs[0], None, outs[1])


def _wgrad_full(a, b, name, after=(), carry=()):
    s, k = a.shape
    n = b.shape[1]
    tk = min(512, k)
    nc = len(carry)
    c_in, c_out, c_shape, c_sems = _carry_specs(carry)

    def body(a_ref, b_ref, *rest):
        rest = rest[len(after):]
        o_ref = rest[nc]
        j = pl.program_id(0)
        _carry_run(j == 0, j == k // tk - 1, rest[:nc], rest[nc + 1:2 * nc + 1], rest[2 * nc + 1:])
        o_ref[...] = _dot(a_ref[...], b_ref[...], TN).astype(BF16)

    outs = pl.pallas_call(
        body, name=name, grid=(k // tk,),
        in_specs=[pl.BlockSpec((s, tk), lambda j: (0, j)),
                  pl.BlockSpec((s, n), lambda j: (0, 0))] + _after_specs(after) + c_in,
        out_specs=[pl.BlockSpec((tk, n), lambda j: (j, 0))] + c_out,
        out_shape=[jax.ShapeDtypeStruct((k, n), BF16)] + c_shape,
        scratch_shapes=c_sems,
        compiler_params=_cp_carry(1, carry),
    )(a, b, *after, *carry)
    return (outs[0], list(outs[1:])) if nc else outs[0]


def _wgrad_pool(p, dyb, n_groups):
    s, sw = p.shape
    d = dyb.shape[1]
    gw, go = sw // n_groups, d // n_groups
    ts = min(512, s)
    ns = s // ts

    def body(a_ref, b_ref, o_ref, acc_ref):
        i = pl.program_id(1)

        @pl.when(i == 0)
        def _():
            acc_ref[...] = jnp.zeros_like(acc_ref)

        acc_ref[...] += _dot(a_ref[...], b_ref[...], TN)

        @pl.when(i == ns - 1)
        def _():
            o_ref[...] = acc_ref[...].astype(BF16)

    return pl.pallas_call(
        body, name="wgrad_pool", grid=(n_groups, ns),
        in_specs=[pl.BlockSpec((ts, gw), lambda g, i: (i, g)),
                  pl.BlockSpec((ts, go), lambda g, i: (i, g))],
        out_specs=pl.BlockSpec((None, gw, go), lambda g, i: (g, 0, 0)),
        out_shape=jax.ShapeDtypeStruct((n_groups, gw, go), BF16),
        scratch_shapes=[pltpu.VMEM((gw, go), F32)],
        compiler_params=_cp(2),
    )(p, dyb)


def _wo_bwd(dx1b, wo, factors, sw, after=()):
    s, d = dx1b.shape
    tn = d // N_DEV
    nq = sw // tn

    def body(dx_ref, wo_ref, fya_ref, fyb_ref, fga_ref, fgb_ref, fsc_ref, *rest):
        dya_ref, dyb_ref, dp_ref, dbg_ref, dsc_ref, dm_ref = rest[len(after):]
        dbg_ref[...] = jnp.zeros_like(dbg_ref)
        dsc_ref[...] = jnp.zeros_like(dsc_ref)
        for rs in _chunks(s, 1024):
            dm_ref[rs, :] = _dot(dx_ref[rs, :], wo_ref[...], NT)
        for rs in _chunks(s, 256):
            dm = dm_ref[rs, :]
            dya_ref[rs, :] = (dm * fya_ref[rs, :].astype(F32)).astype(BF16)
            dyb_ref[rs, :] = (dm * fyb_ref[rs, :].astype(F32)).astype(BF16)
            dsc_ref[...] += jnp.sum(dm * fsc_ref[rs, :].astype(F32), axis=0, keepdims=True)
            dga = dm * fga_ref[rs, :].astype(F32)
            dgb = dm * fgb_ref[rs, :].astype(F32)
            dp_ref[0, rs, :] = dga.astype(BF16)
            dp_ref[1, rs, :] = dgb.astype(BF16)
            dbg_ref[0:1, :] += jnp.sum(dga, axis=0, keepdims=True)
            dbg_ref[1:2, :] += jnp.sum(dgb, axis=0, keepdims=True)

    col = pl.BlockSpec((s, tn), lambda j: (0, j))
    out = jax.ShapeDtypeStruct((s, d), BF16)
    return pl.pallas_call(
        body, name="wo_bwd", grid=(N_DEV,),
        in_specs=[pl.BlockSpec((s, d), lambda j: (0, 0)),
                  pl.BlockSpec((tn, d), lambda j: (j, 0))] + [col] * 5 + _after_specs(after),
        out_specs=[col, col,
                   pl.BlockSpec((2, None, s, tn), lambda j: (1, j // nq, 0, j % nq)),
                   pl.BlockSpec((2, tn), lambda j: (0, j)),
                   pl.BlockSpec((1, tn), lambda j: (0, j))],
        out_shape=[out, out, jax.ShapeDtypeStruct((4, 2, s, sw), BF16),
                   jax.ShapeDtypeStruct((2, d), F32), jax.ShapeDtypeStruct((1, d), F32)],
        scratch_shapes=[pltpu.VMEM((s, tn), F32)],
        compiler_params=_cp(1),
    )(dx1b, wo, *factors, *after)


def _conv_bwd(dproj, dya, wa, proj, conv_w, conv_b):
    s, d = dya.shape
    sw, tn = wa.shape[1], wa.shape[2]
    tc = min(LANES, sw)

    def body(dproj_hbm, dya_ref, wa_ref, ba_ref, ca_ref, va_ref, cw_ref, cb_ref,
             dp_ref, dcw_ref, dcb_ref, dz_ref):
        del dproj_hbm
        for rs in _chunks(s, 512):
            part = _dot(dya_ref[rs, 0:tn], wa_ref[0], NT)
            for j in range(1, N_DEV):
                part = part + _dot(dya_ref[rs, j * tn:(j + 1) * tn], wa_ref[j], NT)
            dz_ref[rs, :] = part
        dz = dz_ref[...]
        ba, ca, va = ba_ref[...], ca_ref[...], va_ref[...]
        cv = ca * va
        cv1, cv2 = _shift_down(cv, 1), _shift_down(cv, 2)
        w0, w1, w2 = cw_ref[0:1, :], cw_ref[1:2, :], cw_ref[2:3, :]
        u = cb_ref[...] + w0 * cv2 + w1 * cv1 + w2 * cv
        du = dz * ba
        dp_ref[0] = (dz * u).astype(BF16)
        dcv = w2 * du + w1 * _shift_up(du, 1) + w0 * _shift_up(du, 2)
        dp_ref[1] = (dcv * va).astype(BF16)
        dp_ref[2] = (dcv * ca).astype(BF16)
        dcw_ref[0:1, :] = jnp.sum(du * cv2, axis=0, keepdims=True)
        dcw_ref[1:2, :] = jnp.sum(du * cv1, axis=0, keepdims=True)
        dcw_ref[2:3, :] = jnp.sum(du * cv, axis=0, keepdims=True)
        dcb_ref[...] = jnp.sum(du, axis=0, keepdims=True)

    def part(k):
        return pl.BlockSpec((None, s, tc), lambda i: (k, 0, i))

    return pl.pallas_call(
        body, name="conv_bwd", grid=(sw // tc,),
        in_specs=[pl.BlockSpec(memory_space=pl.ANY),
                  pl.BlockSpec((s, d), lambda i: (0, 0)),
                  pl.BlockSpec((N_DEV, tc, tn), lambda i: (0, i, 0)),
                  part(0), part(1), part(2),
                  pl.BlockSpec((CONV_K, tc), lambda i: (0, i)), pl.BlockSpec((1, tc), lambda i: (0, i))],
        out_specs=[pl.BlockSpec((3, s, tc), lambda i: (0, 0, i)),
                   pl.BlockSpec((CONV_K, tc), lambda i: (0, i)), pl.BlockSpec((1, tc), lambda i: (0, i))],
        out_shape=[jax.ShapeDtypeStruct(dproj.shape, BF16),
                   jax.ShapeDtypeStruct((CONV_K, sw), F32), jax.ShapeDtypeStruct((1, sw), F32)],
        scratch_shapes=[pltpu.VMEM((s, tc), F32)],
        input_output_aliases={0: 0},
        compiler_params=_cp(1),
    )(dproj, dya, wa, proj, proj, proj, conv_w, conv_b)


def _pool_bwd(dproj, dyb, wpool):
    s, d = dyb.shape
    n_groups, gw, go = wpool.shape

    def body(dproj_hbm, dyb_ref, wp_ref, dp_ref):
        del dproj_hbm
        for gi, window in enumerate(POOL_WINDOWS):
            @pl.when(pl.program_id(0) == gi)
            def _():
                dpool = _dot(dyb_ref[...], wp_ref[...], NT)
                acc, k = dpool / _pool_counts(dpool.shape, window), 1
                while k < window:
                    acc = acc + _shift_up(acc, k)
                    k *= 2
                dp_ref[...] = (acc - dpool).astype(BF16)

    return pl.pallas_call(
        body, name="pool_bwd", grid=(n_groups,),
        in_specs=[pl.BlockSpec(memory_space=pl.ANY),
                  pl.BlockSpec((s, go), lambda g: (0, g)),
                  pl.BlockSpec((None, gw, go), lambda g: (g, 0, 0))],
        out_specs=pl.BlockSpec((None, s, gw), lambda g: (3, 0, g)),
        out_shape=jax.ShapeDtypeStruct(dproj.shape, BF16),
        input_output_aliases={0: 0},
        compiler_params=_cp(1),
    )(dproj, dyb, wpool)


def _rows128(v):
    return v.reshape(-1, LANES)


def kernel(x, norm1_g, w_in, b_gate, conv_w, conv_b, w_a_out, w_pool, pool_scale, w_o, norm2_g, w_ffn_gate, w_ffn_up, w_ffn_down, final_g, loss_target, m_norm1_g, m_w_in, m_b_gate, m_conv_w, m_conv_b, m_w_a_out, m_w_pool, m_pool_scale, m_w_o, m_norm2_g, m_w_ffn_gate, m_w_ffn_up, m_w_ffn_down, m_final_g, v_norm1_g, v_w_in, v_b_gate, v_conv_w, v_conv_b, v_w_a_out, v_w_pool, v_pool_scale, v_w_o, v_norm2_g, v_w_ffn_gate, v_w_ffn_up, v_w_ffn_down, v_final_g):
    s, d = x.shape[1], x.shape[2]
    sw = w_in.shape[2]
    n_groups = w_pool.shape[1]
    gw = w_pool.shape[2]
    go = w_pool.shape[3] * N_DEV
    f8 = w_ffn_gate.shape[2]
    cws = conv_w.shape[2]
    assert sw == conv_w.shape[2] * N_DEV == gw * n_groups and go * n_groups == d and n_groups == len(POOL_WINDOWS)

    xi, yi, ci = _coords()
    me = 4 * xi + 2 * yi + ci
    my_chip = 2 * xi + yi

    x2d = x.reshape(s, d)
    target = loss_target.reshape(s, d)
    final_g2 = final_g.reshape(1, d)
    b_gate2 = b_gate.reshape(2, d)

    big_names = ["w_in", "w_a_out", "w_pool", "w_o", "w_ffn_gate", "w_ffn_up", "w_ffn_down"]
    big_w = [w_in, w_a_out, w_pool, w_o, w_ffn_gate, w_ffn_up, w_ffn_down]
    big_m = [m_w_in, m_w_a_out, m_w_pool, m_w_o, m_w_ffn_gate, m_w_ffn_up, m_w_ffn_down]
    big_v = [v_w_in, v_w_a_out, v_w_pool, v_w_o, v_w_ffn_gate, v_w_ffn_up, v_w_ffn_down]
    shapes2d = [(w.size // w.shape[-1], w.shape[-1]) for w in big_w]
    big_w2 = [w.reshape(sh) for w, sh in zip(big_w, shapes2d)]
    transposed = (4, 5)

    def view2d(t, a):
        t2 = t.reshape(shapes2d[a])
        return t2.T if a in transposed else t2

    def unview(o, a):
        return (o.T if a in transposed else o).reshape(big_w[a].shape)

    sb = [_cast_bf16(w, "cast_" + nm, parts=2 if nm == "w_ffn_down" else 1) for w, nm in zip(big_w2, big_names)]
    win_g, wa_g, wpool_g, wo_g = _allgather_big([b[0] for b in sb[0:4]], "allgather_mixer", COLLECTIVE_GATHER)
    (wg_g,) = _allgather_big(sb[4], "allgather_ffn_gate", COLLECTIVE_GATHER)
    (wu_g,) = _allgather_big(sb[5], "allgather_ffn_up", COLLECTIVE_GATHER)
    wd_parts = [_allgather_big([part], "allgather_ffn_down_%d" % q, COLLECTIVE_GATHER)[0]
                for q, part in enumerate(sb[6])]
    convw_g = _allgather_small(jnp.pad(conv_w.reshape(CONV_K, cws), ((0, 8 - CONV_K), (0, 0))), "allgather_conv_w")
    conv_w_full = convw_g[:, :CONV_K, :].transpose(1, 0, 2).reshape(CONV_K, sw)
    wpool = wpool_g.reshape(N_DEV, n_groups, gw, go // N_DEV).transpose(1, 2, 0, 3).reshape(n_groups, gw, go)
    wo = wo_g.reshape(d, d)

    h = _rms_fwd(x2d, norm1_g)
    proj = _proj_fwd(h, win_g)
    z = _conv_fwd(proj, conv_w_full, conv_b)
    p = _pool_fwd(proj)
    merged, *merge_factors = _merge_fwd(z, wa_g, p, wpool, proj, b_gate2, pool_scale)
    x1, h2 = _wo_fwd(merged, wo, x2d, norm2_g)
    gate = _ffn_gate_fwd(h2, wg_g)
    dadu, dadg, act = _ffn_up_act_fwd(h2, wu_g, gate)
    ffn_parts = [_ffn_down_fwd(act, wd, "ffn_down_fwd_%d" % q) for q, wd in enumerate(wd_parts)]
    dx2b, d_final_g, loss_blk = _loss_bwd(ffn_parts, x1, target, final_g2)

    other_chips = jnp.stack([2 * (1 - xi) + yi, 2 * xi + (1 - yi), 2 * (1 - xi) + (1 - yi)])
    others = jnp.concatenate([other_chips, 2 * other_chips + ci]).astype(jnp.int32)

    def partials(grads, recvs, names):
        if all(g.shape == grads[0].shape for g in grads):
            return list(_chip_partial(others, grads, recvs, "chip_partial_" + names[0]))
        return [_chip_partial(others, [g3], [r], "chip_partial_" + nm)[0] for g3, r, nm in zip(grads, recvs, names)]

    own = jnp.stack([me, my_chip]).astype(jnp.int32)

    def adam(idx, g3s, sibs, chipss):
        wmvs = [(view2d(big_w[a], a), view2d(big_m[a], a), view2d(big_v[a], a)) for a in idx]
        outs = _adam_big(own, wmvs, g3s, sibs, chipss, "adam_" + big_names[idx[0]])
        for a, o4 in zip(idx, outs):
            big_out[a] = [unview(o, a) for o in o4]

    big_out = [None] * len(big_names)
    dg_act, du_act = _ffn_gate_bwd(dx2b, wd_parts, dadg, dadu)
    gw_gate = _wgrad_rows(dg_act, h2, "wgrad_ffn_gate")
    gw_up = _wgrad_rows(du_act, h2, "wgrad_ffn_up")
    gw_down, sib_gu = _wgrad_rows(act, dx2b, "wgrad_ffn_down", carry=[gw_gate, gw_up])
    ps_gu = partials([gw_gate, gw_up], sib_gu, ["w_ffn_gate", "w_ffn_up"])
    chips_gu = _exchange_chips(ps_gu, "rs_chips_ffn_up", COLLECTIVE_CHIPS)
    dh2, sib_down = _input_grad([(dg_act, wg_g), (du_act, wu_g)], "ffn_in_bwd", after=ps_gu, carry=[gw_down])
    ps_down = partials([gw_down], sib_down, ["w_ffn_down"])
    chips_down = _exchange_chips(ps_down, "rs_chips_ffn_down", COLLECTIVE_CHIPS)
    dx1, dx1b, d_norm2_g = _rms_bwd(dh2, x1, norm2_g, dx2b, "rms2_bwd")
    dya, dyb, dproj42, d_b_gate, d_pool_scale = _wo_bwd(dx1b, wo, merge_factors, sw, after=ps_down)
    dproj = dproj42.reshape(N_DEV, s, sw)
    dproj, d_conv_w, d_conv_b = _conv_bwd(dproj, dya, wa_g, proj, conv_w_full, conv_b)
    dproj = _pool_bwd(dproj, dyb, wpool)
    gw_in = _wgrad_cols(h, dproj, "wgrad_in")
    gw_o, sib_in = _wgrad_full(merged, dx1b, "wgrad_o", carry=[gw_in])
    ps_in = partials([gw_in], sib_in, ["w_in"])
    chips_in = _exchange_chips(ps_in, "rs_chips_w_in", COLLECTIVE_CHIPS)
    gw_a = _wgrad_cols(z, dya, "wgrad_a_out", after=ps_in)
    gw_pool = _wgrad_pool(p, dyb, n_groups)
    mix3 = [gw_a,
            gw_pool.reshape(n_groups, gw, N_DEV, go // N_DEV).transpose(2, 0, 1, 3).reshape(N_DEV, n_groups * gw, go // N_DEV),
            gw_o.reshape(N_DEV, d // N_DEV, d)]
    adam([4, 5, 6], [gw_gate, gw_up, gw_down], sib_gu + sib_down, chips_gu + chips_down)
    dh, sib_mix = _input_grad([(dproj, win_g)], "proj_in_bwd", after=[big_out[6][0]], carry=mix3, per=2)
    ps_mix = partials(mix3, sib_mix, ["w_a_out", "w_pool", "w_o"])
    chips_mix = _exchange_chips(ps_mix, "rs_chips_mixer", COLLECTIVE_CHIPS)
    grad_x, _, d_norm1_g = _rms_bwd(dh, x2d, norm1_g, dx1, "rms1_bwd", with_bf16=False)
    adam([0], [gw_in], sib_in, chips_in)
    for k in range(3):
        adam([1 + k], [mix3[k]], [sib_mix[k]], [chips_mix[k]])

    small_parts = [d_norm1_g, d_b_gate, d_conv_w, d_conv_b, d_pool_scale, d_norm2_g, d_final_g, loss_blk]
    rows = [v.size // LANES for v in small_parts]
    row0 = [sum(rows[:k]) for k in range(len(rows))]
    packed = jnp.concatenate([_rows128(v) for v in small_parts], axis=0)
    gathered = _allgather_small(packed, "allgather_small_grads")
    small_names = ["norm1_g", "b_gate", "conv_b", "pool_scale", "norm2_g", "final_g", "conv_w"]
    small_w = [norm1_g, b_gate, conv_b, pool_scale, norm2_g, final_g]
    small_m = [m_norm1_g, m_b_gate, m_conv_b, m_pool_scale, m_norm2_g, m_final_g]
    small_v = [v_norm1_g, v_b_gate, v_conv_b, v_pool_scale, v_norm2_g, v_final_g]
    finished = _small_finish(gathered, [tuple(_rows128(t) for t in wmv) for wmv in zip(small_w, small_m, small_v)],
                             [row0[k] for k in (0, 1, 3, 4, 5, 6)], [(row0[2], rows[2]), (row0[7], rows[7])])
    g_convw_full, loss_rows = finished[0], finished[1]
    loss = loss_rows[0, 0]
    small_out = [[t.reshape(w.shape) for t in finished[2 + 4 * k:6 + 4 * k]] for k, w in enumerate(small_w)]
    g_convw = lax.dynamic_slice(g_convw_full.reshape(CONV_K, sw), (0, me * cws), (CONV_K, cws))
    cw_delta, cw_m, cw_v = _adam_small(conv_w.reshape(CONV_K, cws), g_convw,
                                       m_conv_w.reshape(CONV_K, cws), v_conv_w.reshape(CONV_K, cws))
    small_out.append([t.reshape(conv_w.shape) for t in (g_convw, cw_delta, cw_m, cw_v)])

    order = ["norm1_g", "w_in", "b_gate", "conv_w", "conv_b", "w_a_out", "w_pool", "pool_scale", "w_o", "norm2_g",
             "w_ffn_gate", "w_ffn_up", "w_ffn_down", "final_g"]
    per_kind = [{}, {}, {}, {}]
    for a, nm in enumerate(big_names):
        for kind in range(4):
            per_kind[kind][nm] = big_out[a][kind]
    for k, nm in enumerate(small_names):
        for kind in range(4):
            per_kind[kind][nm] = small_out[k][kind]
    result = [loss, grad_x.reshape(x.shape)]
    for kind in range(4):
        result += [per_kind[kind][nm] for nm in order]
    return tuple(result)
```

```python
import jax
import jax.numpy as jnp
from jax import lax
from jax.experimental import pallas as pl
from jax.experimental.pallas import tpu as pltpu
from jax.experimental.pallas import tpu_sc as plsc

F32 = jnp.float32
BF16 = jnp.bfloat16
MESH = pl.DeviceIdType.MESH

N_DEV = 8
EPS = 1e-6
CONV_K = 3
POOL_WINDOWS = (2, 4, 8, 16)
ADAM_LR = 0.001
ADAM_B1 = 0.9
ADAM_B2 = 0.999
ADAM_EPS = 1e-08
ADAM_WD = 0.01
ADAM_STEP = 10

V7X_VMEM_LIMIT_BYTES = 56 * 1024 * 1024
LANES = 128

COLLECTIVE_GATHER = 1
COLLECTIVE_SIBLING = 2
COLLECTIVE_CHIPS = 3
SEQUENCER_COST_BYTES = 4 * 10**9

NN = ((1,), (0,))
NT = ((1,), (1,))
TN = ((0,), (0,))


def _dot(a, b, dims):
    return lax.dot_general(a, b, (dims, ((), ())), preferred_element_type=F32)


def _cp(n_axes):
    return pltpu.CompilerParams(dimension_semantics=("arbitrary",) * n_axes,
                                vmem_limit_bytes=V7X_VMEM_LIMIT_BYTES)


def _row_tile(rows, bytes_per_row, cap_bytes):
    best = None
    for t in range(16, rows + 1, 16):
        if rows % t == 0 and t * bytes_per_row <= cap_bytes:
            best = t
    return best if best is not None else rows


def _chunks(total, size):
    size = min(size, total)
    assert total % size == 0
    return [slice(r, r + size) for r in range(0, total, size)]


def _after_specs(after):
    return [pl.BlockSpec(memory_space=pl.ANY)] * len(after)


def _shift_down(v, k):
    row = lax.broadcasted_iota(jnp.int32, v.shape, 0)
    return jnp.where(row >= k, pltpu.roll(v, k, 0), 0.0)


def _shift_up(v, k):
    n = v.shape[0]
    row = lax.broadcasted_iota(jnp.int32, v.shape, 0)
    return jnp.where(row < n - k, pltpu.roll(v, n - k, 0), 0.0)


def _sigmoid(v):
    return jax.nn.sigmoid(v)


def _cast_bf16(w2d, name, parts=1):
    rows, cols = w2d.shape
    tr = _row_tile(rows, cols * 4, 2 << 20)
    pc = cols // parts

    def body(i_ref, *o_refs):
        for q, o_ref in enumerate(o_refs):
            o_ref[...] = i_ref[:, q * pc:(q + 1) * pc].astype(BF16)

    return pl.pallas_call(
        body, name=name, grid=(rows // tr,),
        in_specs=[pl.BlockSpec((tr, cols), lambda i: (i, 0))],
        out_specs=[pl.BlockSpec((tr, pc), lambda i: (i, 0))] * parts,
        out_shape=[jax.ShapeDtypeStruct((rows, pc), BF16)] * parts,
        compiler_params=_cp(1),
    )(w2d)


def _rms_fwd(x2d, g):
    s, d = x2d.shape
    tm = min(256, s)

    def body(x_ref, g_ref, h_ref):
        xv = x_ref[...]
        r = lax.rsqrt(jnp.mean(xv * xv, axis=-1, keepdims=True) + EPS)
        h_ref[...] = (xv * r * g_ref[...]).astype(BF16)

    return pl.pallas_call(
        body, name="rms1_fwd", grid=(s // tm,),
        in_specs=[pl.BlockSpec((tm, d), lambda i: (i, 0)), pl.BlockSpec((1, d), lambda i: (0, 0))],
        out_specs=pl.BlockSpec((tm, d), lambda i: (i, 0)),
        out_shape=jax.ShapeDtypeStruct((s, d), BF16),
        compiler_params=_cp(1),
    )(x2d, g)


def _coords():
    return lax.axis_index("x"), lax.axis_index("y"), lax.axis_index("c")


def _slot(p):
    return 4 * p[0] + 2 * p[1] + p[2]


def _handshake(peers):
    barrier = pltpu.get_barrier_semaphore()
    for peer in peers:
        pl.semaphore_signal(barrier, inc=1, device_id=peer, device_id_type=MESH)
    pl.semaphore_wait(barrier, len(peers))


def _sequencer_call(body, out_type, scratch_types, name, collective_id):
    return pl.kernel(
        body, out_type=out_type, name=name,
        mesh=plsc.ScalarSubcoreMesh(axis_name="seq", num_cores=1),
        scratch_types=scratch_types,
        cost_estimate=pl.CostEstimate(flops=0, transcendentals=0, bytes_accessed=SEQUENCER_COST_BYTES),
        compiler_params=pltpu.CompilerParams(collective_id=collective_id))


def _allgather_big(shards, name, collective_id, after=()):
    n = len(shards)

    def body(*refs):
        ins, outs = refs[:n], refs[n + len(after):2 * n + len(after)]
        send_sems, recv_sems, local_sems = refs[2 * n + len(after):]
        x, y, c = _coords()
        me, sibling = (x, y, c), (x, y, 1 - c)
        x_nbr, y_nbr, diag = (1 - x, y), (x, 1 - y), (1 - x, 1 - y)
        relay_from = (x + (1 - c) * (1 - 2 * x), y + c * (1 - 2 * y))
        relay_to = (x + c * (1 - 2 * x), y + (1 - c) * (1 - 2 * y))
        _handshake([sibling, (*x_nbr, c), (*y_nbr, c)])

        def copy(a, k, block, to, src=None):
            dst = outs[a].at[_slot(block)]
            return pltpu.make_async_remote_copy(
                src_ref=dst if src is None else src, dst_ref=dst,
                send_sem=send_sems.at[a, k], recv_sem=recv_sems.at[a, k],
                device_id=to, device_id_type=MESH)

        mine, sends = [], []
        for a in range(n):
            cp = pltpu.make_async_copy(ins[a], outs[a].at[_slot(me)], local_sems.at[a])
            cp.start()
            mine.append(cp)
            first = [copy(a, 0, me, sibling, src=ins[a]),
                     copy(a, 1, me, (*x_nbr, c), src=ins[a]),
                     copy(a, 2, me, (*y_nbr, c), src=ins[a])]
            for cp in first:
                cp.start()
            sends += first
        for a in range(n):
            copy(a, 1 + c, (*relay_from, c), me).wait_recv()
            passed = [copy(a, 3, (*relay_from, c), (*relay_to, c)), copy(a, 4 + c, (*relay_from, c), sibling)]
            for cp in passed:
                cp.start()
            copy(a, 2 - c, (*relay_to, c), me).wait_recv()
            cp = copy(a, 5 - c, (*relay_to, c), sibling)
            cp.start()
            passed.append(cp)
            copy(a, 3, (*diag, c), me).wait_recv()
            cp = copy(a, 6, (*diag, c), sibling)
            cp.start()
            sends += passed + [cp]
        for a in range(n):
            copy(a, 0, sibling, me).wait_recv()
            copy(a, 4, (*x_nbr, 1 - c), me).wait_recv()
            copy(a, 5, (*y_nbr, 1 - c), me).wait_recv()
            copy(a, 6, (*diag, 1 - c), me).wait_recv()
        for cp in sends:
            cp.wait_send()
        for cp in mine:
            cp.wait()

    return _sequencer_call(
        body, [jax.ShapeDtypeStruct((N_DEV,) + s.shape, s.dtype) for s in shards],
        [pltpu.SemaphoreType.DMA((n, 7)), pltpu.SemaphoreType.DMA((n, 7)), pltpu.SemaphoreType.DMA((n,))],
        name, collective_id)(*shards, *after)


def _sibling_copies(ins, recvs, send_sems, recv_sems):
    x, y, c = _coords()
    return [pltpu.make_async_remote_copy(
        src_ref=ins[a].at[2 * q + (1 - c)], dst_ref=recvs[a].at[q],
        send_sem=send_sems.at[a, q], recv_sem=recv_sems.at[a, q],
        device_id=(x, y, 1 - c), device_id_type=MESH) for a in range(len(ins)) for q in range(4)]


def _carry_specs(carry):
    any_spec = pl.BlockSpec(memory_space=pl.ANY)
    n = len(carry)
    sems = [pltpu.SemaphoreType.DMA((n, 4)), pltpu.SemaphoreType.DMA((n, 4))] if n else []
    return ([any_spec] * n, [any_spec] * n,
            [jax.ShapeDtypeStruct((4,) + g.shape[1:], g.dtype) for g in carry], sems)


def _carry_run(first, last, ins, recvs, sems):
    if not ins:
        return

    @pl.when(first)
    def _():
        x, y, c = _coords()
        _handshake([(x, y, 1 - c)])
        for cp in _sibling_copies(ins, recvs, *sems):
            cp.start()

    @pl.when(last)
    def _():
        copies = _sibling_copies(ins, recvs, *sems)
        for cp in copies:
            cp.wait_recv()
        for cp in copies:
            cp.wait_send()


def _cp_carry(n_axes, carry):
    if not carry:
        return _cp(n_axes)
    return pltpu.CompilerParams(dimension_semantics=("arbitrary",) * n_axes, vmem_limit_bytes=V7X_VMEM_LIMIT_BYTES,
                                collective_id=COLLECTIVE_SIBLING)


def _exchange_chips(psums, name, collective_id):
    n = len(psums)

    def body(*refs):
        ins, outs = refs[:n], refs[n:2 * n]
        send_sems, recv_sems = refs[2 * n:]
        x, y, c = _coords()
        chips = [(1 - x, y), (x, 1 - y), (1 - x, 1 - y)]
        _handshake([(*chip, c) for chip in chips])
        copies = []
        for a in range(n):
            for j, chip in enumerate(chips):
                cp = pltpu.make_async_remote_copy(
                    src_ref=ins[a].at[2 * chip[0] + chip[1]], dst_ref=outs[a].at[j],
                    send_sem=send_sems.at[a, j], recv_sem=recv_sems.at[a, j],
                    device_id=(*chip, c), device_id_type=MESH)
                cp.start()
                copies.append(cp)
        for cp in copies:
            cp.wait_recv()
        for cp in copies:
            cp.wait_send()

    return _sequencer_call(
        body, [jax.ShapeDtypeStruct((3,) + p.shape[1:], p.dtype) for p in psums],
        [pltpu.SemaphoreType.DMA((n, 3)), pltpu.SemaphoreType.DMA((n, 3))],
        name, collective_id)(*psums)


def _allgather_small(v2d, name):
    rows, cols = v2d.shape

    def body(v_ref, out_ref, send_sems, recv_sems):
        x, y, c = _coords()
        me = (x, y, c)
        out_ref[_slot(me)] = v_ref[...]
        peers = []
        for k in range(1, N_DEV):
            fx, fy, fc = (k >> 2) & 1, (k >> 1) & 1, k & 1
            peers.append(((1 - x) if fx else x, (1 - y) if fy else y, (1 - c) if fc else c))
        sends = []
        for k, peer in enumerate(peers):
            cp = pltpu.make_async_remote_copy(
                src_ref=v_ref, dst_ref=out_ref.at[_slot(me)],
                send_sem=send_sems.at[k], recv_sem=recv_sems.at[k],
                device_id=peer, device_id_type=MESH)
            cp.start()
            sends.append(cp)
        for k, peer in enumerate(peers):
            pltpu.make_async_remote_copy(
                src_ref=v_ref, dst_ref=out_ref.at[_slot(peer)],
                send_sem=send_sems.at[k], recv_sem=recv_sems.at[k],
                device_id=peer, device_id_type=MESH).wait_recv()
        for cp in sends:
            cp.wait_send()

    vmem = pl.BlockSpec(memory_space=pltpu.VMEM)
    return pl.pallas_call(
        body, name=name, in_specs=[vmem], out_specs=vmem,
        out_shape=jax.ShapeDtypeStruct((N_DEV, rows, cols), v2d.dtype),
        scratch_shapes=[pltpu.SemaphoreType.DMA((N_DEV - 1,)), pltpu.SemaphoreType.DMA((N_DEV - 1,))],
    )(v2d)


def _chip_partial(others, grads, recvs, name):
    n = len(grads)
    _, rows, cols = grads[0].shape
    tr = _row_tile(rows, cols * 2, (2 << 20) // n)

    def body(others_ref, *refs):
        for a in range(n):
            refs[2 * n + a][...] = (refs[a][...].astype(F32) + refs[n + a][...].astype(F32)).astype(BF16)

    return pl.pallas_call(
        body, name=name,
        grid_spec=pltpu.PrefetchScalarGridSpec(
            num_scalar_prefetch=1, grid=(3, rows // tr),
            in_specs=[pl.BlockSpec((None, tr, cols), lambda k, i, o: (o[3 + k], i, 0))] * n
            + [pl.BlockSpec((None, tr, cols), lambda k, i, o: (o[k], i, 0))] * n,
            out_specs=[pl.BlockSpec((None, tr, cols), lambda k, i, o: (o[k], i, 0))] * n),
        out_shape=[jax.ShapeDtypeStruct((4, rows, cols), BF16)] * n,
        compiler_params=_cp(2),
    )(others, *grads, *recvs)


def _adam_math(w, g, m, v):
    m = ADAM_B1 * m + (1.0 - ADAM_B1) * g
    v = ADAM_B2 * v + (1.0 - ADAM_B2) * (g * g)
    m_hat = m / (1.0 - ADAM_B1 ** ADAM_STEP)
    v_hat = v / (1.0 - ADAM_B2 ** ADAM_STEP)
    delta = -ADAM_LR * (m_hat / (jnp.sqrt(v_hat) + ADAM_EPS) + ADAM_WD * w)
    return delta, m, v


def _adam_big(own, wmvs, g3s, recv_sibs, recv_chipss, name):
    n = len(wmvs)
    rows, cols = wmvs[0][0].shape
    tr = _row_tile(rows, cols * 4, (2 << 20) // n)

    def body(own_ref, *refs):
        ins, outs = refs[:6 * n], refs[6 * n:]
        for a in range(n):
            w_ref, m_ref, v_ref, g_ref, rs_ref, rc_ref = ins[6 * a:6 * a + 6]
            g = g_ref[...].astype(F32) + rs_ref[...].astype(F32)
            g = g + rc_ref[0].astype(F32)
            g = g + rc_ref[1].astype(F32)
            g = g + rc_ref[2].astype(F32)
            delta, m_new, v_new = _adam_math(w_ref[...], g, m_ref[...], v_ref[...])
            outs[4 * a][...] = g
            outs[4 * a + 1][...] = delta
            outs[4 * a + 2][...] = m_new
            outs[4 * a + 3][...] = v_new

    blk = pl.BlockSpec((tr, cols), lambda i, o: (i, 0))
    per_shard = [blk, blk, blk,
                 pl.BlockSpec((None, tr, cols), lambda i, o: (o[0], i, 0)),
                 pl.BlockSpec((None, tr, cols), lambda i, o: (o[1], i, 0)),
                 pl.BlockSpec((3, tr, cols), lambda i, o: (0, i, 0))]
    out = jax.ShapeDtypeStruct((rows, cols), F32)
    args = [t for a in range(n) for t in (*wmvs[a], g3s[a], recv_sibs[a], recv_chipss[a])]
    outs = pl.pallas_call(
        body, name=name,
        grid_spec=pltpu.PrefetchScalarGridSpec(
            num_scalar_prefetch=1, grid=(rows // tr,),
            in_specs=per_shard * n, out_specs=[blk] * (4 * n)),
        out_shape=[out] * (4 * n),
        compiler_params=_cp(1),
    )(own, *args)
    return [outs[4 * a:4 * a + 4] for a in range(n)]


def _small_finish(gathered, params, row_offs, extra_rows):
    n = len(params)

    def body(g_ref, *refs):
        ins, outs = refs[:3 * n], refs[3 * n:]
        total = g_ref[0]
        for k in range(1, N_DEV):
            total = total + g_ref[k]
        for e, (r0, nr) in enumerate(extra_rows):
            outs[e][...] = total[r0:r0 + nr, :]
        for p in range(n):
            w_ref, m_ref, v_ref = ins[3 * p:3 * p + 3]
            g_out, d_out, m_out, v_out = outs[len(extra_rows) + 4 * p:len(extra_rows) + 4 * p + 4]
            g = total[row_offs[p]:row_offs[p] + w_ref.shape[0], :]
            delta, m_new, v_new = _adam_math(w_ref[...], g, m_ref[...], v_ref[...])
            g_out[...] = g
            d_out[...] = delta
            m_out[...] = m_new
            v_out[...] = v_new

    vmem = pl.BlockSpec(memory_space=pltpu.VMEM)
    out_shape = [jax.ShapeDtypeStruct((nr, LANES), F32) for _, nr in extra_rows]
    for w, _, _ in params:
        out_shape += [jax.ShapeDtypeStruct(w.shape, F32)] * 4
    flat = [t for wmv in params for t in wmv]
    return pl.pallas_call(body, name="small_finish", in_specs=[vmem] * (1 + len(flat)),
                          out_specs=[vmem] * len(out_shape), out_shape=out_shape)(gathered, *flat)


def _adam_small(w, g, m, v):
    def body(w_ref, g_ref, m_ref, v_ref, do_ref, mo_ref, vo_ref):
        delta, m_new, v_new = _adam_math(w_ref[...], g_ref[...], m_ref[...], v_ref[...])
        do_ref[...] = delta
        mo_ref[...] = m_new
        vo_ref[...] = v_new

    vmem = pl.BlockSpec(memory_space=pltpu.VMEM)
    out = jax.ShapeDtypeStruct(w.shape, F32)
    return pl.pallas_call(body, name="adam_small", in_specs=[vmem] * 4, out_specs=[vmem] * 3,
                          out_shape=[out, out, out])(w, g, m, v)


def _proj_fwd(h, win_g):
    s, d = h.shape
    sw = win_g.shape[2]
    tn = min(512, sw)
    nh = sw // tn

    def body(h_ref, w_ref, o_ref):
        for rs in _chunks(s, 512):
            o_ref[rs, :] = _dot(h_ref[rs, :], w_ref[...], NN)

    return pl.pallas_call(
        body, name="proj_fwd", grid=(N_DEV * nh,),
        in_specs=[pl.BlockSpec((s, d), lambda j: (0, 0)),
                  pl.BlockSpec((None, d, tn), lambda j: (j // nh, 0, j % nh))],
        out_specs=pl.BlockSpec((None, s, tn), lambda j: (j // nh, 0, j % nh)),
        out_shape=jax.ShapeDtypeStruct((N_DEV, s, sw), F32),
        compiler_params=_cp(1),
    )(h, win_g)


def _conv_fwd(proj, conv_w, conv_b):
    _, s, sw = proj.shape
    tc = min(LANES, sw)

    def body(ba_ref, ca_ref, va_ref, cw_ref, cb_ref, z_ref):
        cv = ca_ref[...] * va_ref[...]
        u = (cb_ref[...] + cw_ref[0:1, :] * _shift_down(cv, 2) + cw_ref[1:2, :] * _shift_down(cv, 1)
             + cw_ref[2:3, :] * cv)
        z_ref[...] = (ba_ref[...] * u).astype(BF16)

    def part(k):
        return pl.BlockSpec((None, s, tc), lambda i: (k, 0, i))

    return pl.pallas_call(
        body, name="conv_fwd", grid=(sw // tc,),
        in_specs=[part(0), part(1), part(2),
                  pl.BlockSpec((CONV_K, tc), lambda i: (0, i)), pl.BlockSpec((1, tc), lambda i: (0, i))],
        out_specs=pl.BlockSpec((s, tc), lambda i: (0, i)),
        out_shape=jax.ShapeDtypeStruct((s, sw), BF16),
        compiler_params=_cp(1),
    )(proj, proj, proj, conv_w, conv_b)


def _pool_counts(shape, window):
    t = lax.broadcasted_iota(jnp.int32, shape, 0)
    return jnp.minimum(t + 1, window).astype(F32)


def _pool_fwd(proj):
    _, s, sw = proj.shape
    gw = sw // len(POOL_WINDOWS)

    def body(v_ref, p_ref):
        for gi, window in enumerate(POOL_WINDOWS):
            @pl.when(pl.program_id(0) == gi)
            def _():
                v = v_ref[...]
                acc, k = v, 1
                while k < window:
                    acc = acc + _shift_down(acc, k)
                    k *= 2
                p_ref[...] = (acc / _pool_counts(v.shape, window) - v).astype(BF16)

    return pl.pallas_call(
        body, name="pool_fwd", grid=(len(POOL_WINDOWS),),
        in_specs=[pl.BlockSpec((None, s, gw), lambda g: (3, 0, g))],
        out_specs=pl.BlockSpec((s, gw), lambda g: (0, g)),
        out_shape=jax.ShapeDtypeStruct((s, sw), BF16),
        compiler_params=_cp(1),
    )(proj)


def _merge_fwd(z, wa, p, wpool, proj, b_gate2, pool_scale):
    s, sw = z.shape
    tn = wa.shape[2]
    d = tn * N_DEV
    gw = sw // len(POOL_WINDOWS)
    nq = sw // tn

    def body(z_ref, wa_ref, p_ref, wp_ref, ga_ref, gb_ref, bg_ref, sc_ref,
             m_ref, dya_ref, dyb_ref, dga_ref, dgb_ref, dsc_ref):
        for rs in _chunks(s, 512):
            ya = _dot(z_ref[rs, :], wa_ref[...], NN)
            yb = _dot(p_ref[rs, :], wp_ref[...], NN)
            sa = _sigmoid(ga_ref[rs, :] + bg_ref[0:1, :])
            sb = _sigmoid(gb_ref[rs, :] + bg_ref[1:2, :])
            sc = sc_ref[...]
            sb_yb = sb * yb
            m_ref[rs, :] = (sa * ya + sb_yb * sc).astype(BF16)
            dya_ref[rs, :] = sa.astype(BF16)
            dyb_ref[rs, :] = (sb * sc).astype(BF16)
            dga_ref[rs, :] = (ya * (sa * (1.0 - sa))).astype(BF16)
            dgb_ref[rs, :] = ((yb * sc) * (sb * (1.0 - sb))).astype(BF16)
            dsc_ref[rs, :] = sb_yb.astype(BF16)

    col = pl.BlockSpec((s, tn), lambda j: (0, j))
    out = jax.ShapeDtypeStruct((s, d), BF16)
    return pl.pallas_call(
        body, name="merge_fwd", grid=(N_DEV,),
        in_specs=[pl.BlockSpec((s, sw), lambda j: (0, 0)),
                  pl.BlockSpec((None, sw, tn), lambda j: (j, 0, 0)),
                  pl.BlockSpec((s, gw), lambda j: (0, j // 2)),
                  pl.BlockSpec((None, gw, tn), lambda j: (j // 2, 0, j % 2)),
                  pl.BlockSpec((None, s, tn), lambda j: (4 + j // nq, 0, j % nq)),
                  pl.BlockSpec((None, s, tn), lambda j: (6 + j // nq, 0, j % nq)),
                  pl.BlockSpec((2, tn), lambda j: (0, j)),
                  pl.BlockSpec((1, tn), lambda j: (0, j))],
        out_specs=[col] * 6,
        out_shape=[out] * 6,
        compiler_params=_cp(1),
    )(z, wa, p, wpool, proj, proj, b_gate2, pool_scale)


def _wo_fwd(merged, wo, x2d, g2):
    s, d = x2d.shape
    tm = min(256, s)

    def body(m_ref, wo_ref, x_ref, g_ref, x1_ref, h2_ref):
        x1 = x_ref[...] + _dot(m_ref[...], wo_ref[...], NN)
        x1_ref[...] = x1
        r = lax.rsqrt(jnp.mean(x1 * x1, axis=-1, keepdims=True) + EPS)
        h2_ref[...] = (x1 * r * g_ref[...]).astype(BF16)

    row = pl.BlockSpec((tm, d), lambda i: (i, 0))
    return pl.pallas_call(
        body, name="wo_fwd", grid=(s // tm,),
        in_specs=[row, pl.BlockSpec((d, d), lambda i: (0, 0)), row, pl.BlockSpec((1, d), lambda i: (0, 0))],
        out_specs=[row, row],
        out_shape=[jax.ShapeDtypeStruct((s, d), F32), jax.ShapeDtypeStruct((s, d), BF16)],
        compiler_params=_cp(1),
    )(merged, wo, x2d, g2)


def _ffn_gate_fwd(h2, wg_g):
    s, d = h2.shape
    f8 = wg_g.shape[2]
    th = min(1024, s)

    def body(h_ref, wg_ref, g_ref):
        i = pl.program_id(1)
        for rs in _chunks(th, 512):
            rows = pl.ds(pl.multiple_of(i * th + rs.start, rs.stop - rs.start), rs.stop - rs.start)
            g_ref[rs, :] = _dot(h_ref[rows, :], wg_ref[...], NN).astype(BF16)

    return pl.pallas_call(
        body, name="ffn_gate_fwd", grid=(N_DEV, s // th),
        in_specs=[pl.BlockSpec((s, d), lambda j, i: (0, 0)), pl.BlockSpec((None, d, f8), lambda j, i: (j, 0, 0))],
        out_specs=pl.BlockSpec((None, th, f8), lambda j, i: (j, i, 0)),
        out_shape=jax.ShapeDtypeStruct((N_DEV, s, f8), BF16),
        compiler_params=_cp(2),
    )(h2, wg_g)


def _ffn_up_act_fwd(h2, wu_g, gate):
    s, d = h2.shape
    f8 = wu_g.shape[2]
    th = min(1024, s)

    def body(h_ref, wu_ref, g_ref, dadu_ref, dadg_ref, a_ref, u_ref):
        i = pl.program_id(1)
        chunks = _chunks(th, 256)

        def matmul(rs):
            rows = pl.ds(pl.multiple_of(i * th + rs.start, rs.stop - rs.start), rs.stop - rs.start)
            u_ref[rs, :] = _dot(h_ref[rows, :], wu_ref[...], NN)

        matmul(chunks[0])
        for k, rs in enumerate(chunks):
            if k + 1 < len(chunks):
                matmul(chunks[k + 1])
            g = g_ref[rs, :].astype(F32)
            u = u_ref[rs, :]
            sg = _sigmoid(g)
            silu = g * sg
            dadu_ref[rs, :] = silu.astype(BF16)
            dadg_ref[rs, :] = (u * (sg * (1.0 + g * (1.0 - sg)))).astype(BF16)
            a_ref[rs, :] = (silu * u).astype(BF16)

    wspec = pl.BlockSpec((None, d, f8), lambda j, i: (j, 0, 0))
    ospec = pl.BlockSpec((None, th, f8), lambda j, i: (j, i, 0))
    out = jax.ShapeDtypeStruct((N_DEV, s, f8), BF16)
    return pl.pallas_call(
        body, name="ffn_up_fwd", grid=(N_DEV, s // th),
        in_specs=[pl.BlockSpec((s, d), lambda j, i: (0, 0)), wspec, ospec],
        out_specs=[ospec, ospec, ospec], out_shape=[out, out, out],
        scratch_shapes=[pltpu.VMEM((th, f8), F32)],
        compiler_params=_cp(2),
    )(h2, wu_g, gate)


def _ffn_down_fwd(act, wd_part, name):
    _, s, f8 = act.shape
    tn = wd_part.shape[2]
    per = 2

    def body(a_ref, wd_ref, o_ref):
        @pl.when(pl.program_id(0) == 0)
        def _():
            o_ref[...] = jnp.zeros_like(o_ref)

        for rs in _chunks(s, 1024):
            part = _dot(a_ref[0, rs, :], wd_ref[0], NN)
            for q in range(1, per):
                part = part + _dot(a_ref[q, rs, :], wd_ref[q], NN)
            o_ref[rs, :] += part

    return pl.pallas_call(
        body, name=name, grid=(N_DEV // per,),
        in_specs=[pl.BlockSpec((per, s, f8), lambda j: (j, 0, 0)),
                  pl.BlockSpec((per, f8, tn), lambda j: (j, 0, 0))],
        out_specs=pl.BlockSpec((s, tn), lambda j: (0, 0)),
        out_shape=jax.ShapeDtypeStruct((s, tn), F32),
        compiler_params=_cp(1),
    )(act, wd_part)


def _loss_bwd(ffn_parts, x1, target, final_g):
    s, d = x1.shape
    tm = min(256, s)
    nparts = len(ffn_parts)

    def body(*refs):
        f_refs = refs[:nparts]
        x1_ref, t_ref, gf_ref, dxb_ref, dgf_ref, loss_ref = refs[nparts:]

        @pl.when(pl.program_id(0) == 0)
        def _():
            dgf_ref[...] = jnp.zeros_like(dgf_ref)
            loss_ref[...] = jnp.zeros_like(loss_ref)

        x2 = x1_ref[...] + jnp.concatenate([f_ref[...] for f_ref in f_refs], axis=-1)
        r = lax.rsqrt(jnp.mean(x2 * x2, axis=-1, keepdims=True) + EPS)
        nrm = x2 * r
        gf = gf_ref[...]
        err = nrm * gf - t_ref[...]
        loss_ref[...] += jnp.sum(err * err) * (0.5 / d)
        dy = err * (1.0 / d)
        dgf_ref[...] += jnp.sum(dy * nrm, axis=0, keepdims=True)
        dn = dy * gf
        dx = r * (dn - nrm * jnp.mean(dn * nrm, axis=-1, keepdims=True))
        dxb_ref[...] = dx.astype(BF16)

    row = pl.BlockSpec((tm, d), lambda i: (i, 0))
    vec = pl.BlockSpec((1, d), lambda i: (0, 0))
    return pl.pallas_call(
        body, name="loss_bwd", grid=(s // tm,),
        in_specs=[pl.BlockSpec((tm, f.shape[1]), lambda i: (i, 0)) for f in ffn_parts] + [row, row, vec],
        out_specs=[row, vec, pl.BlockSpec((8, LANES), lambda i: (0, 0))],
        out_shape=[jax.ShapeDtypeStruct((s, d), BF16),
                   jax.ShapeDtypeStruct((1, d), F32), jax.ShapeDtypeStruct((8, LANES), F32)],
        compiler_params=_cp(1),
    )(*ffn_parts, x1, target, final_g)


def _ffn_gate_bwd(dx2b, wd_parts, dadg, dadu):
    s, d = dx2b.shape
    f8 = dadg.shape[2]
    th = min(1024, s)
    nparts = len(wd_parts)
    pc = d // nparts

    def body(dx_ref, *refs):
        wd_refs = refs[:nparts]
        g_ref, u_ref, dg_ref, du_ref, da_ref = refs[nparts:]
        i = pl.program_id(1)
        chunks = _chunks(th, 256)

        def matmul(rs):
            rows = pl.ds(pl.multiple_of(i * th + rs.start, rs.stop - rs.start), rs.stop - rs.start)
            part = None
            for q, wd_ref in enumerate(wd_refs):
                term = _dot(dx_ref[rows, q * pc:(q + 1) * pc], wd_ref[...], NT)
                part = term if part is None else part + term
            da_ref[rs, :] = part

        matmul(chunks[0])
        for k, rs in enumerate(chunks):
            if k + 1 < len(chunks):
                matmul(chunks[k + 1])
            da = da_ref[rs, :].astype(BF16)
            dg_ref[rs, :] = da * g_ref[rs, :]
            du_ref[rs, :] = da * u_ref[rs, :]

    aspec = pl.BlockSpec((None, th, f8), lambda j, i: (j, i, 0))
    out = jax.ShapeDtypeStruct((N_DEV, s, f8), BF16)
    return pl.pallas_call(
        body, name="ffn_act_bwd", grid=(N_DEV, s // th),
        in_specs=[pl.BlockSpec((s, d), lambda j, i: (0, 0))]
        + [pl.BlockSpec((None, f8, pc), lambda j, i: (j, 0, 0))] * nparts + [aspec, aspec],
        out_specs=[aspec, aspec], out_shape=[out, out],
        scratch_shapes=[pltpu.VMEM((th, f8), F32)],
        compiler_params=_cp(2),
    )(dx2b, *wd_parts, dadg, dadu)


def _wgrad_rows(a3, b, name, after=(), carry=()):
    _, s, k = a3.shape
    n = b.shape[1]
    nc = len(carry)
    c_in, c_out, c_shape, c_sems = _carry_specs(carry)

    def body(a_ref, b_ref, *rest):
        rest = rest[len(after):]
        o_ref = rest[nc]
        j = pl.program_id(0)
        _carry_run(j == 0, j == N_DEV - 1, rest[:nc], rest[nc + 1:2 * nc + 1], rest[2 * nc + 1:])
        o_ref[...] = _dot(a_ref[...], b_ref[...], TN).astype(BF16)

    outs = pl.pallas_call(
        body, name=name, grid=(N_DEV,),
        in_specs=[pl.BlockSpec((None, s, k), lambda j: (j, 0, 0)),
                  pl.BlockSpec((s, n), lambda j: (0, 0))] + _after_specs(after) + c_in,
        out_specs=[pl.BlockSpec((None, k, n), lambda j: (j, 0, 0))] + c_out,
        out_shape=[jax.ShapeDtypeStruct((N_DEV, k, n), BF16)] + c_shape,
        scratch_shapes=c_sems,
        compiler_params=_cp_carry(1, carry),
    )(a3, b, *after, *carry)
    return (outs[0], list(outs[1:])) if nc else outs[0]


def _wgrad_cols(a, b3, name, after=()):
    s, k = a.shape
    if b3.ndim == 2:
        n = b3.shape[1] // N_DEV
        b_spec = pl.BlockSpec((s, n), lambda j: (0, j))
    else:
        n = b3.shape[2]
        b_spec = pl.BlockSpec((None, s, n), lambda j: (j, 0, 0))

    def body(a_ref, b_ref, *rest):
        o_ref = rest[len(after)]
        o_ref[...] = _dot(a_ref[...], b_ref[...], TN).astype(BF16)

    return pl.pallas_call(
        body, name=name, grid=(N_DEV,),
        in_specs=[pl.BlockSpec((s, k), lambda j: (0, 0)), b_spec] + _after_specs(after),
        out_specs=pl.BlockSpec((None, k, n), lambda j: (j, 0, 0)),
        out_shape=jax.ShapeDtypeStruct((N_DEV, k, n), BF16),
        compiler_params=_cp(1),
    )(a, b3, *after)


def _input_grad(pairs, name, after=(), carry=(), per=1):
    s = pairs[0][0].shape[1]
    d = pairs[0][1].shape[1]
    tn = min(1024, d)
    npair = len(pairs)
    nc = len(carry)
    c_in, c_out, c_shape, c_sems = _carry_specs(carry)

    def body(*refs):
        ops = refs[:2 * npair]
        rest = refs[2 * npair + len(after):]
        o_ref, acc_ref = rest[nc], rest[-1]
        nh, j = pl.program_id(0), pl.program_id(1)
        last_j = N_DEV // per - 1
        _carry_run((nh == 0) & (j == 0), (nh == d // tn - 1) & (j == last_j),
                   rest[:nc], rest[nc + 1:2 * nc + 1], rest[2 * nc + 1:-1])

        @pl.when(j == 0)
        def _():
            acc_ref[...] = jnp.zeros_like(acc_ref)

        for rs in _chunks(s, 1024):
            part = None
            for q in range(npair):
                for e in range(per):
                    term = _dot(ops[2 * q][e, rs, :], ops[2 * q + 1][e], NT)
                    part = term if part is None else part + term
            acc_ref[rs, :] += part

        @pl.when(j == last_j)
        def _():
            o_ref[...] = acc_ref[...].astype(BF16)

    in_specs, args = [], []
    for a3, w3 in pairs:
        k = a3.shape[2]
        in_specs += [pl.BlockSpec((per, s, k), lambda n, j: (j, 0, 0)),
                     pl.BlockSpec((per, tn, k), lambda n, j: (j, n, 0))]
        args += [a3, w3]
    outs = pl.pallas_call(
        body, name=name, grid=(d // tn, N_DEV // per),
        in_specs=in_specs + _after_specs(after) + c_in,
        out_specs=[pl.BlockSpec((s, tn), lambda n, j: (0, n))] + c_out,
        out_shape=[jax.ShapeDtypeStruct((s, d), BF16)] + c_shape,
        scratch_shapes=c_sems + [pltpu.VMEM((s, tn), F32)],
        compiler_params=_cp_carry(2, carry),
    )(*args, *after, *carry)
    return (outs[0], list(outs[1:])) if nc else outs[0]


def _rms_bwd(dh, xres, g, dres, name, with_bf16=True):
    s, d = xres.shape
    tm = min(256, s)

    def body(dh_ref, x_ref, g_ref, dres_ref, dx_ref, *rest):
        dg_ref = rest[-1]
        @pl.when(pl.program_id(0) == 0)
        def _():
            dg_ref[...] = jnp.zeros_like(dg_ref)

        xv = x_ref[...]
        dh_v = dh_ref[...].astype(F32)
        r = lax.rsqrt(jnp.mean(xv * xv, axis=-1, keepdims=True) + EPS)
        nrm = xv * r
        dg_ref[...] += jnp.sum(dh_v * nrm, axis=0, keepdims=True)
        dn = dh_v * g_ref[...]
        dx = dres_ref[...].astype(F32) + r * (dn - nrm * jnp.mean(dn * nrm, axis=-1, keepdims=True))
        dx_ref[...] = dx
        if with_bf16:
            rest[0][...] = dx.astype(BF16)

    row = pl.BlockSpec((tm, d), lambda i: (i, 0))
    vec = pl.BlockSpec((1, d), lambda i: (0, 0))
    copies = [jax.ShapeDtypeStruct((s, d), BF16)] if with_bf16 else []
    outs = pl.pallas_call(
        body, name=name, grid=(s // tm,),
        in_specs=[row, row, vec, row],
        out_specs=[row] + [row] * len(copies) + [vec],
        out_shape=[jax.ShapeDtypeStruct((s, d), F32)] + copies + [jax.ShapeDtypeStruct((1, d), F32)],
        compiler_params=_cp(1),
    )(dh, xres, g, dres)
    return (outs[0], outs[1], outs[2]) if with_bf16 else (outs[0], None, outs[1])


def _wgrad_full(a, b, name, after=(), carry=()):
    s, k = a.shape
    n = b.shape[1]
    tk = min(512, k)
    nc = len(carry)
    c_in, c_out, c_shape, c_sems = _carry_specs(carry)

    def body(a_ref, b_ref, *rest):
        rest = rest[len(after):]
        o_ref = rest[nc]
        j = pl.program_id(0)
        _carry_run(j == 0, j == k // tk - 1, rest[:nc], rest[nc + 1:2 * nc + 1], rest[2 * nc + 1:])
        o_ref[...] = _dot(a_ref[...], b_ref[...], TN).astype(BF16)

    outs = pl.pallas_call(
        body, name=name, grid=(k // tk,),
        in_specs=[pl.BlockSpec((s, tk), lambda j: (0, j)),
                  pl.BlockSpec((s, n), lambda j: (0, 0))] + _after_specs(after) + c_in,
        out_specs=[pl.BlockSpec((tk, n), lambda j: (j, 0))] + c_out,
        out_shape=[jax.ShapeDtypeStruct((k, n), BF16)] + c_shape,
        scratch_shapes=c_sems,
        compiler_params=_cp_carry(1, carry),
    )(a, b, *after, *carry)
    return (outs[0], list(outs[1:])) if nc else outs[0]


def _wgrad_pool(p, dyb, n_groups):
    s, sw = p.shape
    d = dyb.shape[1]
    gw, go = sw // n_groups, d // n_groups
    ts = min(512, s)
    ns = s // ts

    def body(a_ref, b_ref, o_ref, acc_ref):
        i = pl.program_id(1)

        @pl.when(i == 0)
        def _():
            acc_ref[...] = jnp.zeros_like(acc_ref)

        acc_ref[...] += _dot(a_ref[...], b_ref[...], TN)

        @pl.when(i == ns - 1)
        def _():
            o_ref[...] = acc_ref[...].astype(BF16)

    return pl.pallas_call(
        body, name="wgrad_pool", grid=(n_groups, ns),
        in_specs=[pl.BlockSpec((ts, gw), lambda g, i: (i, g)),
                  pl.BlockSpec((ts, go), lambda g, i: (i, g))],
        out_specs=pl.BlockSpec((None, gw, go), lambda g, i: (g, 0, 0)),
        out_shape=jax.ShapeDtypeStruct((n_groups, gw, go), BF16),
        scratch_shapes=[pltpu.VMEM((gw, go), F32)],
        compiler_params=_cp(2),
    )(p, dyb)


def _wo_bwd(dx1b, wo, factors, sw, after=()):
    s, d = dx1b.shape
    tn = d // N_DEV
    nq = sw // tn

    def body(dx_ref, wo_ref, fya_ref, fyb_ref, fga_ref, fgb_ref, fsc_ref, *rest):
        dya_ref, dyb_ref, dp_ref, dbg_ref, dsc_ref, dm_ref = rest[len(after):]
        dbg_ref[...] = jnp.zeros_like(dbg_ref)
        dsc_ref[...] = jnp.zeros_like(dsc_ref)
        for rs in _chunks(s, 1024):
            dm_ref[rs, :] = _dot(dx_ref[rs, :], wo_ref[...], NT)
        for rs in _chunks(s, 256):
            dm = dm_ref[rs, :]
            dya_ref[rs, :] = (dm * fya_ref[rs, :].astype(F32)).astype(BF16)
            dyb_ref[rs, :] = (dm * fyb_ref[rs, :].astype(F32)).astype(BF16)
            dsc_ref[...] += jnp.sum(dm * fsc_ref[rs, :].astype(F32), axis=0, keepdims=True)
            dga = dm * fga_ref[rs, :].astype(F32)
            dgb = dm * fgb_ref[rs, :].astype(F32)
            dp_ref[0, rs, :] = dga.astype(BF16)
            dp_ref[1, rs, :] = dgb.astype(BF16)
            dbg_ref[0:1, :] += jnp.sum(dga, axis=0, keepdims=True)
            dbg_ref[1:2, :] += jnp.sum(dgb, axis=0, keepdims=True)

    col = pl.BlockSpec((s, tn), lambda j: (0, j))
    out = jax.ShapeDtypeStruct((s, d), BF16)
    return pl.pallas_call(
        body, name="wo_bwd", grid=(N_DEV,),
        in_specs=[pl.BlockSpec((s, d), lambda j: (0, 0)),
                  pl.BlockSpec((tn, d), lambda j: (j, 0))] + [col] * 5 + _after_specs(after),
        out_specs=[col, col,
                   pl.BlockSpec((2, None, s, tn), lambda j: (1, j // nq, 0, j % nq)),
                   pl.BlockSpec((2, tn), lambda j: (0, j)),
                   pl.BlockSpec((1, tn), lambda j: (0, j))],
        out_shape=[out, out, jax.ShapeDtypeStruct((4, 2, s, sw), BF16),
                   jax.ShapeDtypeStruct((2, d), F32), jax.ShapeDtypeStruct((1, d), F32)],
        scratch_shapes=[pltpu.VMEM((s, tn), F32)],
        compiler_params=_cp(1),
    )(dx1b, wo, *factors, *after)


def _conv_bwd(dproj, dya, wa, proj, conv_w, conv_b):
    s, d = dya.shape
    sw, tn = wa.shape[1], wa.shape[2]
    tc = min(LANES, sw)

    def body(dproj_hbm, dya_ref, wa_ref, ba_ref, ca_ref, va_ref, cw_ref, cb_ref,
             dp_ref, dcw_ref, dcb_ref, dz_ref):
        del dproj_hbm
        for rs in _chunks(s, 512):
            part = _dot(dya_ref[rs, 0:tn], wa_ref[0], NT)
            for j in range(1, N_DEV):
                part = part + _dot(dya_ref[rs, j * tn:(j + 1) * tn], wa_ref[j], NT)
            dz_ref[rs, :] = part
        dz = dz_ref[...]
        ba, ca, va = ba_ref[...], ca_ref[...], va_ref[...]
        cv = ca * va
        cv1, cv2 = _shift_down(cv, 1), _shift_down(cv, 2)
        w0, w1, w2 = cw_ref[0:1, :], cw_ref[1:2, :], cw_ref[2:3, :]
        u = cb_ref[...] + w0 * cv2 + w1 * cv1 + w2 * cv
        du = dz * ba
        dp_ref[0] = (dz * u).astype(BF16)
        dcv = w2 * du + w1 * _shift_up(du, 1) + w0 * _shift_up(du, 2)
        dp_ref[1] = (dcv * va).astype(BF16)
        dp_ref[2] = (dcv * ca).astype(BF16)
        dcw_ref[0:1, :] = jnp.sum(du * cv2, axis=0, keepdims=True)
        dcw_ref[1:2, :] = jnp.sum(du * cv1, axis=0, keepdims=True)
        dcw_ref[2:3, :] = jnp.sum(du * cv, axis=0, keepdims=True)
        dcb_ref[...] = jnp.sum(du, axis=0, keepdims=True)

    def part(k):
        return pl.BlockSpec((None, s, tc), lambda i: (k, 0, i))

    return pl.pallas_call(
        body, name="conv_bwd", grid=(sw // tc,),
        in_specs=[pl.BlockSpec(memory_space=pl.ANY),
                  pl.BlockSpec((s, d), lambda i: (0, 0)),
                  pl.BlockSpec((N_DEV, tc, tn), lambda i: (0, i, 0)),
                  part(0), part(1), part(2),
                  pl.BlockSpec((CONV_K, tc), lambda i: (0, i)), pl.BlockSpec((1, tc), lambda i: (0, i))],
        out_specs=[pl.BlockSpec((3, s, tc), lambda i: (0, 0, i)),
                   pl.BlockSpec((CONV_K, tc), lambda i: (0, i)), pl.BlockSpec((1, tc), lambda i: (0, i))],
        out_shape=[jax.ShapeDtypeStruct(dproj.shape, BF16),
                   jax.ShapeDtypeStruct((CONV_K, sw), F32), jax.ShapeDtypeStruct((1, sw), F32)],
        scratch_shapes=[pltpu.VMEM((s, tc), F32)],
        input_output_aliases={0: 0},
        compiler_params=_cp(1),
    )(dproj, dya, wa, proj, proj, proj, conv_w, conv_b)


def _pool_bwd(dproj, dyb, wpool):
    s, d = dyb.shape
    n_groups, gw, go = wpool.shape

    def body(dproj_hbm, dyb_ref, wp_ref, dp_ref):
        del dproj_hbm
        for gi, window in enumerate(POOL_WINDOWS):
            @pl.when(pl.program_id(0) == gi)
            def _():
                dpool = _dot(dyb_ref[...], wp_ref[...], NT)
                acc, k = dpool / _pool_counts(dpool.shape, window), 1
                while k < window:
                    acc = acc + _shift_up(acc, k)
                    k *= 2
                dp_ref[...] = (acc - dpool).astype(BF16)

    return pl.pallas_call(
        body, name="pool_bwd", grid=(n_groups,),
        in_specs=[pl.BlockSpec(memory_space=pl.ANY),
                  pl.BlockSpec((s, go), lambda g: (0, g)),
                  pl.BlockSpec((None, gw, go), lambda g: (g, 0, 0))],
        out_specs=pl.BlockSpec((None, s, gw), lambda g: (3, 0, g)),
        out_shape=jax.ShapeDtypeStruct(dproj.shape, BF16),
        input_output_aliases={0: 0},
        compiler_params=_cp(1),
    )(dproj, dyb, wpool)


def _rows128(v):
    return v.reshape(-1, LANES)


def kernel(x, norm1_g, w_in, b_gate, conv_w, conv_b, w_a_out, w_pool, pool_scale, w_o, norm2_g, w_ffn_gate, w_ffn_up, w_ffn_down, final_g, loss_target, m_norm1_g, m_w_in, m_b_gate, m_conv_w, m_conv_b, m_w_a_out, m_w_pool, m_pool_scale, m_w_o, m_norm2_g, m_w_ffn_gate, m_w_ffn_up, m_w_ffn_down, m_final_g, v_norm1_g, v_w_in, v_b_gate, v_conv_w, v_conv_b, v_w_a_out, v_w_pool, v_pool_scale, v_w_o, v_norm2_g, v_w_ffn_gate, v_w_ffn_up, v_w_ffn_down, v_final_g):
    s, d = x.shape[1], x.shape[2]
    sw = w_in.shape[2]
    n_groups = w_pool.shape[1]
    gw = w_pool.shape[2]
    go = w_pool.shape[3] * N_DEV
    f8 = w_ffn_gate.shape[2]
    cws = conv_w.shape[2]
    assert sw == conv_w.shape[2] * N_DEV == gw * n_groups and go * n_groups == d and n_groups == len(POOL_WINDOWS)

    xi, yi, ci = _coords()
    me = 4 * xi + 2 * yi + ci
    my_chip = 2 * xi + yi

    x2d = x.reshape(s, d)
    target = loss_target.reshape(s, d)
    final_g2 = final_g.reshape(1, d)
    b_gate2 = b_gate.reshape(2, d)

    big_names = ["w_in", "w_a_out", "w_pool", "w_o", "w_ffn_gate", "w_ffn_up", "w_ffn_down"]
    big_w = [w_in, w_a_out, w_pool, w_o, w_ffn_gate, w_ffn_up, w_ffn_down]
    big_m = [m_w_in, m_w_a_out, m_w_pool, m_w_o, m_w_ffn_gate, m_w_ffn_up, m_w_ffn_down]
    big_v = [v_w_in, v_w_a_out, v_w_pool, v_w_o, v_w_ffn_gate, v_w_ffn_up, v_w_ffn_down]
    shapes2d = [(w.size // w.shape[-1], w.shape[-1]) for w in big_w]
    big_w2 = [w.reshape(sh) for w, sh in zip(big_w, shapes2d)]
    transposed = (4, 5)

    def view2d(t, a):
        t2 = t.reshape(shapes2d[a])
        return t2.T if a in transposed else t2

    def unview(o, a):
        return (o.T if a in transposed else o).reshape(big_w[a].shape)

    sb = [_cast_bf16(w, "cast_" + nm, parts=2 if nm == "w_ffn_down" else 1) for w, nm in zip(big_w2, big_names)]
    win_g, wa_g, wpool_g, wo_g = _allgather_big([b[0] for b in sb[0:4]], "allgather_mixer", COLLECTIVE_GATHER)
    (wg_g,) = _allgather_big(sb[4], "allgather_ffn_gate", COLLECTIVE_GATHER)
    (wu_g,) = _allgather_big(sb[5], "allgather_ffn_up", COLLECTIVE_GATHER)
    wd_parts = [_allgather_big([part], "allgather_ffn_down_%d" % q, COLLECTIVE_GATHER)[0]
                for q, part in enumerate(sb[6])]
    convw_g = _allgather_small(jnp.pad(conv_w.reshape(CONV_K, cws), ((0, 8 - CONV_K), (0, 0))), "allgather_conv_w")
    conv_w_full = convw_g[:, :CONV_K, :].transpose(1, 0, 2).reshape(CONV_K, sw)
    wpool = wpool_g.reshape(N_DEV, n_groups, gw, go // N_DEV).transpose(1, 2, 0, 3).reshape(n_groups, gw, go)
    wo = wo_g.reshape(d, d)

    h = _rms_fwd(x2d, norm1_g)
    proj = _proj_fwd(h, win_g)
    z = _conv_fwd(proj, conv_w_full, conv_b)
    p = _pool_fwd(proj)
    merged, *merge_factors = _merge_fwd(z, wa_g, p, wpool, proj, b_gate2, pool_scale)
    x1, h2 = _wo_fwd(merged, wo, x2d, norm2_g)
    gate = _ffn_gate_fwd(h2, wg_g)
    dadu, dadg, act = _ffn_up_act_fwd(h2, wu_g, gate)
    ffn_parts = [_ffn_down_fwd(act, wd, "ffn_down_fwd_%d" % q) for q, wd in enumerate(wd_parts)]
    dx2b, d_final_g, loss_blk = _loss_bwd(ffn_parts, x1, target, final_g2)

    other_chips = jnp.stack([2 * (1 - xi) + yi, 2 * xi + (1 - yi), 2 * (1 - xi) + (1 - yi)])
    others = jnp.concatenate([other_chips, 2 * other_chips + ci]).astype(jnp.int32)

    def partials(grads, recvs, names):
        if all(g.shape == grads[0].shape for g in grads):
            return list(_chip_partial(others, grads, recvs, "chip_partial_" + names[0]))
        return [_chip_partial(others, [g3], [r], "chip_partial_" + nm)[0] for g3, r, nm in zip(grads, recvs, names)]

    own = jnp.stack([me, my_chip]).astype(jnp.int32)

    def adam(idx, g3s, sibs, chipss):
        wmvs = [(view2d(big_w[a], a), view2d(big_m[a], a), view2d(big_v[a], a)) for a in idx]
        outs = _adam_big(own, wmvs, g3s, sibs, chipss, "adam_" + big_names[idx[0]])
        for a, o4 in zip(idx, outs):
            big_out[a] = [unview(o, a) for o in o4]

    big_out = [None] * len(big_names)
    dg_act, du_act = _ffn_gate_bwd(dx2b, wd_parts, dadg, dadu)
    gw_gate = _wgrad_rows(dg_act, h2, "wgrad_ffn_gate")
    gw_up = _wgrad_rows(du_act, h2, "wgrad_ffn_up")
    gw_down, sib_gu = _wgrad_rows(act, dx2b, "wgrad_ffn_down", carry=[gw_gate, gw_up])
    ps_gu = partials([gw_gate, gw_up], sib_gu, ["w_ffn_gate", "w_ffn_up"])
    chips_gu = _exchange_chips(ps_gu, "rs_chips_ffn_up", COLLECTIVE_CHIPS)
    dh2, sib_down = _input_grad([(dg_act, wg_g), (du_act, wu_g)], "ffn_in_bwd", after=ps_gu, carry=[gw_down])
    ps_down = partials([gw_down], sib_down, ["w_ffn_down"])
    chips_down = _exchange_chips(ps_down, "rs_chips_ffn_down", COLLECTIVE_CHIPS)
    dx1, dx1b, d_norm2_g = _rms_bwd(dh2, x1, norm2_g, dx2b, "rms2_bwd")
    dya, dyb, dproj42, d_b_gate, d_pool_scale = _wo_bwd(dx1b, wo, merge_factors, sw, after=ps_down)
    dproj = dproj42.reshape(N_DEV, s, sw)
    dproj, d_conv_w, d_conv_b = _conv_bwd(dproj, dya, wa_g, proj, conv_w_full, conv_b)
    dproj = _pool_bwd(dproj, dyb, wpool)
    gw_in = _wgrad_cols(h, dproj, "wgrad_in")
    gw_o, sib_in = _wgrad_full(merged, dx1b, "wgrad_o", carry=[gw_in])
    ps_in = partials([gw_in], sib_in, ["w_in"])
    chips_in = _exchange_chips(ps_in, "rs_chips_w_in", COLLECTIVE_CHIPS)
    gw_a = _wgrad_cols(z, dya, "wgrad_a_out", after=ps_in)
    gw_pool = _wgrad_pool(p, dyb, n_groups)
    mix3 = [gw_a,
            gw_pool.reshape(n_groups, gw, N_DEV, go // N_DEV).transpose(2, 0, 1, 3).reshape(N_DEV, n_groups * gw, go // N_DEV),
            gw_o.reshape(N_DEV, d // N_DEV, d)]
    adam([4, 5, 6], [gw_gate, gw_up, gw_down], sib_gu + sib_down, chips_gu + chips_down)
    dh, sib_mix = _input_grad([(dproj, win_g)], "proj_in_bwd", after=[big_out[6][0]], carry=mix3, per=2)
    ps_mix = partials(mix3, sib_mix, ["w_a_out", "w_pool", "w_o"])
    chips_mix = _exchange_chips(ps_mix, "rs_chips_mixer", COLLECTIVE_CHIPS)
    grad_x, _, d_norm1_g = _rms_bwd(dh, x2d, norm1_g, dx1, "rms1_bwd", with_bf16=False)
    adam([0], [gw_in], sib_in, chips_in)
    for k in range(3):
        adam([1 + k], [mix3[k]], [sib_mix[k]], [chips_mix[k]])

    small_parts = [d_norm1_g, d_b_gate, d_conv_w, d_conv_b, d_pool_scale, d_norm2_g, d_final_g, loss_blk]
    rows = [v.size // LANES for v in small_parts]
    row0 = [sum(rows[:k]) for k in range(len(rows))]
    packed = jnp.concatenate([_rows128(v) for v in small_parts], axis=0)
    gathered = _allgather_small(packed, "allgather_small_grads")
    small_names = ["norm1_g", "b_gate", "conv_b", "pool_scale", "norm2_g", "final_g", "conv_w"]
    small_w = [norm1_g, b_gate, conv_b, pool_scale, norm2_g, final_g]
    small_m = [m_norm1_g, m_b_gate, m_conv_b, m_pool_scale, m_norm2_g, m_final_g]
    small_v = [v_norm1_g, v_b_gate, v_conv_b, v_pool_scale, v_norm2_g, v_final_g]
    finished = _small_finish(gathered, [tuple(_rows128(t) for t in wmv) for wmv in zip(small_w, small_m, small_v)],
                             [row0[k] for k in (0, 1, 3, 4, 5, 6)], [(row0[2], rows[2]), (row0[7], rows[7])])
    g_convw_full, loss_rows = finished[0], finished[1]
    loss = loss_rows[0, 0]
    small_out = [[t.reshape(w.shape) for t in finished[2 + 4 * k:6 + 4 * k]] for k, w in enumerate(small_w)]
    g_convw = lax.dynamic_slice(g_convw_full.reshape(CONV_K, sw), (0, me * cws), (CONV_K, cws))
    cw_delta, cw_m, cw_v = _adam_small(conv_w.reshape(CONV_K, cws), g_convw,
                                       m_conv_w.reshape(CONV_K, cws), v_conv_w.reshape(CONV_K, cws))
    small_out.append([t.reshape(conv_w.shape) for t in (g_convw, cw_delta, cw_m, cw_v)])

    order = ["norm1_g", "w_in", "b_gate", "conv_w", "conv_b", "w_a_out", "w_pool", "pool_scale", "w_o", "norm2_g",
             "w_ffn_gate", "w_ffn_up", "w_ffn_down", "final_g"]
    per_kind = [{}, {}, {}, {}]
    for a, nm in enumerate(big_names):
        for kind in range(4):
            per_kind[kind][nm] = big_out[a][kind]
    for k, nm in enumerate(small_names):
        for kind in range(4):
            per_kind[kind][nm] = small_out[k][kind]
    result = [loss, grad_x.reshape(x.shape)]
    for kind in range(4):
        result += [per_kind[kind][nm] for nm in order]
    return tuple(result)
```

```python
import jax
import jax.numpy as jnp
from jax import lax
from jax.experimental import pallas as pl
from jax.experimental.pallas import tpu as pltpu
from jax.experimental.pallas import tpu_sc as plsc

F32 = jnp.float32
BF16 = jnp.bfloat16
MESH = pl.DeviceIdType.MESH

N_DEV = 8
EPS = 1e-6
CONV_K = 3
POOL_WINDOWS = (2, 4, 8, 16)
ADAM_LR = 0.001
ADAM_B1 = 0.9
ADAM_B2 = 0.999
ADAM_EPS = 1e-08
ADAM_WD = 0.01
ADAM_STEP = 10

V7X_VMEM_LIMIT_BYTES = 56 * 1024 * 1024
LANES = 128

COLLECTIVE_GATHER = 1
COLLECTIVE_SIBLING = 2
COLLECTIVE_CHIPS = 3
SEQUENCER_COST_BYTES = 4 * 10**9

NN = ((1,), (0,))
NT = ((1,), (1,))
TN = ((0,), (0,))


def _dot(a, b, dims):
    return lax.dot_general(a, b, (dims, ((), ())), preferred_element_type=F32)


def _cp(n_axes):
    return pltpu.CompilerParams(dimension_semantics=("arbitrary",) * n_axes,
                                vmem_limit_bytes=V7X_VMEM_LIMIT_BYTES)


def _row_tile(rows, bytes_per_row, cap_bytes):
    best = None
    for t in range(16, rows + 1, 16):
        if rows % t == 0 and t * bytes_per_row <= cap_bytes:
            best = t
    return best if best is not None else rows


def _chunks(total, size):
    size = min(size, total)
    assert total % size == 0
    return [slice(r, r + size) for r in range(0, total, size)]


def _after_specs(after):
    return [pl.BlockSpec(memory_space=pl.ANY)] * len(after)


def _shift_down(v, k):
    row = lax.broadcasted_iota(jnp.int32, v.shape, 0)
    return jnp.where(row >= k, pltpu.roll(v, k, 0), 0.0)


def _shift_up(v, k):
    n = v.shape[0]
    row = lax.broadcasted_iota(jnp.int32, v.shape, 0)
    return jnp.where(row < n - k, pltpu.roll(v, n - k, 0), 0.0)


def _sigmoid(v):
    return jax.nn.sigmoid(v)


def _cast_bf16(w2d, name, parts=1):
    rows, cols = w2d.shape
    tr = _row_tile(rows, cols * 4, 2 << 20)
    pc = cols // parts

    def body(i_ref, *o_refs):
        for q, o_ref in enumerate(o_refs):
            o_ref[...] = i_ref[:, q * pc:(q + 1) * pc].astype(BF16)

    return pl.pallas_call(
        body, name=name, grid=(rows // tr,),
        in_specs=[pl.BlockSpec((tr, cols), lambda i: (i, 0))],
        out_specs=[pl.BlockSpec((tr, pc), lambda i: (i, 0))] * parts,
        out_shape=[jax.ShapeDtypeStruct((rows, pc), BF16)] * parts,
        compiler_params=_cp(1),
    )(w2d)


def _rms_fwd(x2d, g):
    s, d = x2d.shape
    tm = min(256, s)

    def body(x_ref, g_ref, h_ref):
        xv = x_ref[...]
        r = lax.rsqrt(jnp.mean(xv * xv, axis=-1, keepdims=True) + EPS)
        h_ref[...] = (xv * r * g_ref[...]).astype(BF16)

    return pl.pallas_call(
        body, name="rms1_fwd", grid=(s // tm,),
        in_specs=[pl.BlockSpec((tm, d), lambda i: (i, 0)), pl.BlockSpec((1, d), lambda i: (0, 0))],
        out_specs=pl.BlockSpec((tm, d), lambda i: (i, 0)),
        out_shape=jax.ShapeDtypeStruct((s, d), BF16),
        compiler_params=_cp(1),
    )(x2d, g)


def _coords():
    return lax.axis_index("x"), lax.axis_index("y"), lax.axis_index("c")


def _slot(p):
    return 4 * p[0] + 2 * p[1] + p[2]


def _handshake(peers):
    barrier = pltpu.get_barrier_semaphore()
    for peer in peers:
        pl.semaphore_signal(barrier, inc=1, device_id=peer, device_id_type=MESH)
    pl.semaphore_wait(barrier, len(peers))


def _sequencer_call(body, out_type, scratch_types, name, collective_id):
    return pl.kernel(
        body, out_type=out_type, name=name,
        mesh=plsc.ScalarSubcoreMesh(axis_name="seq", num_cores=1),
        scratch_types=scratch_types,
        cost_estimate=pl.CostEstimate(flops=0, transcendentals=0, bytes_accessed=SEQUENCER_COST_BYTES),
        compiler_params=pltpu.CompilerParams(collective_id=collective_id))


def _allgather_big(shards, name, collective_id, after=()):
    n = len(shards)

    def body(*refs):
        ins, outs = refs[:n], refs[n + len(after):2 * n + len(after)]
        send_sems, recv_sems, local_sems = refs[2 * n + len(after):]
        x, y, c = _coords()
        me, sibling = (x, y, c), (x, y, 1 - c)
        x_nbr, y_nbr, diag = (1 - x, y), (x, 1 - y), (1 - x, 1 - y)
        relay_from = (x + (1 - c) * (1 - 2 * x), y + c * (1 - 2 * y))
        relay_to = (x + c * (1 - 2 * x), y + (1 - c) * (1 - 2 * y))
        _handshake([sibling, (*x_nbr, c), (*y_nbr, c)])

        def copy(a, k, block, to, src=None):
            dst = outs[a].at[_slot(block)]
            return pltpu.make_async_remote_copy(
                src_ref=dst if src is None else src, dst_ref=dst,
                send_sem=send_sems.at[a, k], recv_sem=recv_sems.at[a, k],
                device_id=to, device_id_type=MESH)

        mine, sends = [], []
        for a in range(n):
            cp = pltpu.make_async_copy(ins[a], outs[a].at[_slot(me)], local_sems.at[a])
            cp.start()
            mine.append(cp)
            first = [copy(a, 0, me, sibling, src=ins[a]),
                     copy(a, 1, me, (*x_nbr, c), src=ins[a]),
                     copy(a, 2, me, (*y_nbr, c), src=ins[a])]
            for cp in first:
                cp.start()
            sends += first
        for a in range(n):
            copy(a, 1 + c, (*relay_from, c), me).wait_recv()
            passed = [copy(a, 3, (*relay_from, c), (*relay_to, c)), copy(a, 4 + c, (*relay_from, c), sibling)]
            for cp in passed:
                cp.start()
            copy(a, 2 - c, (*relay_to, c), me).wait_recv()
            cp = copy(a, 5 - c, (*relay_to, c), sibling)
            cp.start()
            passed.append(cp)
            copy(a, 3, (*diag, c), me).wait_recv()
            cp = copy(a, 6, (*diag, c), sibling)
            cp.start()
            sends += passed + [cp]
        for a in range(n):
            copy(a, 0, sibling, me).wait_recv()
            copy(a, 4, (*x_nbr, 1 - c), me).wait_recv()
            copy(a, 5, (*y_nbr, 1 - c), me).wait_recv()
            copy(a, 6, (*diag, 1 - c), me).wait_recv()
        for cp in sends:
            cp.wait_send()
        for cp in mine:
            cp.wait()

    return _sequencer_call(
        body, [jax.ShapeDtypeStruct((N_DEV,) + s.shape, s.dtype) for s in shards],
        [pltpu.SemaphoreType.DMA((n, 7)), pltpu.SemaphoreType.DMA((n, 7)), pltpu.SemaphoreType.DMA((n,))],
        name, collective_id)(*shards, *after)


def _sibling_copies(ins, recvs, send_sems, recv_sems):
    x, y, c = _coords()
    return [pltpu.make_async_remote_copy(
        src_ref=ins[a].at[2 * q + (1 - c)], dst_ref=recvs[a].at[q],
        send_sem=send_sems.at[a, q], recv_sem=recv_sems.at[a, q],
        device_id=(x, y, 1 - c), device_id_type=MESH) for a in range(len(ins)) for q in range(4)]


def _carry_specs(carry):
    any_spec = pl.BlockSpec(memory_space=pl.ANY)
    n = len(carry)
    sems = [pltpu.SemaphoreType.DMA((n, 4)), pltpu.SemaphoreType.DMA((n, 4))] if n else []
    return ([any_spec] * n, [any_spec] * n,
            [jax.ShapeDtypeStruct((4,) + g.shape[1:], g.dtype) for g in carry], sems)


def _carry_run(first, last, ins, recvs, sems):
    if not ins:
        return

    @pl.when(first)
    def _():
        x, y, c = _coords()
        _handshake([(x, y, 1 - c)])
        for cp in _sibling_copies(ins, recvs, *sems):
            cp.start()

    @pl.when(last)
    def _():
        copies = _sibling_copies(ins, recvs, *sems)
        for cp in copies:
            cp.wait_recv()
        for cp in copies:
            cp.wait_send()


def _cp_carry(n_axes, carry):
    if not carry:
        return _cp(n_axes)
    return pltpu.CompilerParams(dimension_semantics=("arbitrary",) * n_axes, vmem_limit_bytes=V7X_VMEM_LIMIT_BYTES,
                                collective_id=COLLECTIVE_SIBLING)


def _exchange_chips(psums, name, collective_id):
    n = len(psums)

    def body(*refs):
        ins, outs = refs[:n], refs[n:2 * n]
        send_sems, recv_sems = refs[2 * n:]
        x, y, c = _coords()
        chips = [(1 - x, y), (x, 1 - y), (1 - x, 1 - y)]
        _handshake([(*chip, c) for chip in chips])
        copies = []
        for a in range(n):
            for j, chip in enumerate(chips):
                cp = pltpu.make_async_remote_copy(
                    src_ref=ins[a].at[2 * chip[0] + chip[1]], dst_ref=outs[a].at[j],
                    send_sem=send_sems.at[a, j], recv_sem=recv_sems.at[a, j],
                    device_id=(*chip, c), device_id_type=MESH)
                cp.start()
                copies.append(cp)
        for cp in copies:
            cp.wait_recv()
        for cp in copies:
            cp.wait_send()

    return _sequencer_call(
        body, [jax.ShapeDtypeStruct((3,) + p.shape[1:], p.dtype) for p in psums],
        [pltpu.SemaphoreType.DMA((n, 3)), pltpu.SemaphoreType.DMA((n, 3))],
        name, collective_id)(*psums)


def _allgather_small(v2d, name):
    rows, cols = v2d.shape

    def body(v_ref, out_ref, send_sems, recv_sems):
        x, y, c = _coords()
        me = (x, y, c)
        out_ref[_slot(me)] = v_ref[...]
        peers = []
        for k in range(1, N_DEV):
            fx, fy, fc = (k >> 2) & 1, (k >> 1) & 1, k & 1
            peers.append(((1 - x) if fx else x, (1 - y) if fy else y, (1 - c) if fc else c))
        sends = []
        for k, peer in enumerate(peers):
            cp = pltpu.make_async_remote_copy(
                src_ref=v_ref, dst_ref=out_ref.at[_slot(me)],
                send_sem=send_sems.at[k], recv_sem=recv_sems.at[k],
                device_id=peer, device_id_type=MESH)
            cp.start()
            sends.append(cp)
        for k, peer in enumerate(peers):
            pltpu.make_async_remote_copy(
                src_ref=v_ref, dst_ref=out_ref.at[_slot(peer)],
                send_sem=send_sems.at[k], recv_sem=recv_sems.at[k],
                device_id=peer, device_id_type=MESH).wait_recv()
        for cp in sends:
            cp.wait_send()

    vmem = pl.BlockSpec(memory_space=pltpu.VMEM)
    return pl.pallas_call(
        body, name=name, in_specs=[vmem], out_specs=vmem,
        out_shape=jax.ShapeDtypeStruct((N_DEV, rows, cols), v2d.dtype),
        scratch_shapes=[pltpu.SemaphoreType.DMA((N_DEV - 1,)), pltpu.SemaphoreType.DMA((N_DEV - 1,))],
    )(v2d)


def _chip_partial(others, grads, recvs, name):
    n = len(grads)
    _, rows, cols = grads[0].shape
    tr = _row_tile(rows, cols * 2, (2 << 20) // n)

    def body(others_ref, *refs):
        for a in range(n):
            refs[2 * n + a][...] = (refs[a][...].astype(F32) + refs[n + a][...].astype(F32)).astype(BF16)

    return pl.pallas_call(
        body, name=name,
        grid_spec=pltpu.PrefetchScalarGridSpec(
            num_scalar_prefetch=1, grid=(3, rows // tr),
            in_specs=[pl.BlockSpec((None, tr, cols), lambda k, i, o: (o[3 + k], i, 0))] * n
            + [pl.BlockSpec((None, tr, cols), lambda k, i, o: (o[k], i, 0))] * n,
            out_specs=[pl.BlockSpec((None, tr, cols), lambda k, i, o: (o[k], i, 0))] * n),
        out_shape=[jax.ShapeDtypeStruct((4, rows, cols), BF16)] * n,
        compiler_params=_cp(2),
    )(others, *grads, *recvs)


def _adam_math(w, g, m, v):
    m = ADAM_B1 * m + (1.0 - ADAM_B1) * g
    v = ADAM_B2 * v + (1.0 - ADAM_B2) * (g * g)
    m_hat = m / (1.0 - ADAM_B1 ** ADAM_STEP)
    v_hat = v / (1.0 - ADAM_B2 ** ADAM_STEP)
    delta = -ADAM_LR * (m_hat / (jnp.sqrt(v_hat) + ADAM_EPS) + ADAM_WD * w)
    return delta, m, v


def _adam_big(own, wmvs, g3s, recv_sibs, recv_chipss, name):
    n = len(wmvs)
    rows, cols = wmvs[0][0].shape
    tr = _row_tile(rows, cols * 4, (2 << 20) // n)

    def body(own_ref, *refs):
        ins, outs = refs[:6 * n], refs[6 * n:]
        for a in range(n):
            w_ref, m_ref, v_ref, g_ref, rs_ref, rc_ref = ins[6 * a:6 * a + 6]
            g = g_ref[...].astype(F32) + rs_ref[...].astype(F32)
            g = g + rc_ref[0].astype(F32)
            g = g + rc_ref[1].astype(F32)
            g = g + rc_ref[2].astype(F32)
            delta, m_new, v_new = _adam_math(w_ref[...], g, m_ref[...], v_ref[...])
            outs[4 * a][...] = g
            outs[4 * a + 1][...] = delta
            outs[4 * a + 2][...] = m_new
            outs[4 * a + 3][...] = v_new

    blk = pl.BlockSpec((tr, cols), lambda i, o: (i, 0))
    per_shard = [blk, blk, blk,
                 pl.BlockSpec((None, tr, cols), lambda i, o: (o[0], i, 0)),
                 pl.BlockSpec((None, tr, cols), lambda i, o: (o[1], i, 0)),
                 pl.BlockSpec((3, tr, cols), lambda i, o: (0, i, 0))]
    out = jax.ShapeDtypeStruct((rows, cols), F32)
    args = [t for a in range(n) for t in (*wmvs[a], g3s[a], recv_sibs[a], recv_chipss[a])]
    outs = pl.pallas_call(
        body, name=name,
        grid_spec=pltpu.PrefetchScalarGridSpec(
            num_scalar_prefetch=1, grid=(rows // tr,),
            in_specs=per_shard * n, out_specs=[blk] * (4 * n)),
        out_shape=[out] * (4 * n),
        compiler_params=_cp(1),
    )(own, *args)
    return [outs[4 * a:4 * a + 4] for a in range(n)]


def _small_finish(gathered, params, row_offs, extra_rows):
    n = len(params)

    def body(g_ref, *refs):
        ins, outs = refs[:3 * n], refs[3 * n:]
        total = g_ref[0]
        for k in range(1, N_DEV):
            total = total + g_ref[k]
        for e, (r0, nr) in enumerate(extra_rows):
            outs[e][...] = total[r0:r0 + nr, :]
        for p in range(n):
            w_ref, m_ref, v_ref = ins[3 * p:3 * p + 3]
            g_out, d_out, m_out, v_out = outs[len(extra_rows) + 4 * p:len(extra_rows) + 4 * p + 4]
            g = total[row_offs[p]:row_offs[p] + w_ref.shape[0], :]
            delta, m_new, v_new = _adam_math(w_ref[...], g, m_ref[...], v_ref[...])
            g_out[...] = g
            d_out[...] = delta
            m_out[...] = m_new
            v_out[...] = v_new

    vmem = pl.BlockSpec(memory_space=pltpu.VMEM)
    out_shape = [jax.ShapeDtypeStruct((nr, LANES), F32) for _, nr in extra_rows]
    for w, _, _ in params:
        out_shape += [jax.ShapeDtypeStruct(w.shape, F32)] * 4
    flat = [t for wmv in params for t in wmv]
    return pl.pallas_call(body, name="small_finish", in_specs=[vmem] * (1 + len(flat)),
                          out_specs=[vmem] * len(out_shape), out_shape=out_shape)(gathered, *flat)


def _adam_small(w, g, m, v):
    def body(w_ref, g_ref, m_ref, v_ref, do_ref, mo_ref, vo_ref):
        delta, m_new, v_new = _adam_math(w_ref[...], g_ref[...], m_ref[...], v_ref[...])
        do_ref[...] = delta
        mo_ref[...] = m_new
        vo_ref[...] = v_new

    vmem = pl.BlockSpec(memory_space=pltpu.VMEM)
    out = jax.ShapeDtypeStruct(w.shape, F32)
    return pl.pallas_call(body, name="adam_small", in_specs=[vmem] * 4, out_specs=[vmem] * 3,
                          out_shape=[out, out, out])(w, g, m, v)


def _proj_fwd(h, win_g):
    s, d = h.shape
    sw = win_g.shape[2]
    tn = min(512, sw)
    nh = sw // tn

    def body(h_ref, w_ref, o_ref):
        for rs in _chunks(s, 512):
            o_ref[rs, :] = _dot(h_ref[rs, :], w_ref[...], NN)

    return pl.pallas_call(
        body, name="proj_fwd", grid=(N_DEV * nh,),
        in_specs=[pl.BlockSpec((s, d), lambda j: (0, 0)),
                  pl.BlockSpec((None, d, tn), lambda j: (j // nh, 0, j % nh))],
        out_specs=pl.BlockSpec((None, s, tn), lambda j: (j // nh, 0, j % nh)),
        out_shape=jax.ShapeDtypeStruct((N_DEV, s, sw), F32),
        compiler_params=_cp(1),
    )(h, win_g)


def _conv_fwd(proj, conv_w, conv_b):
    _, s, sw = proj.shape
    tc = min(LANES, sw)

    def body(ba_ref, ca_ref, va_ref, cw_ref, cb_ref, z_ref):
        cv = ca_ref[...] * va_ref[...]
        u = (cb_ref[...] + cw_ref[0:1, :] * _shift_down(cv, 2) + cw_ref[1:2, :] * _shift_down(cv, 1)
             + cw_ref[2:3, :] * cv)
        z_ref[...] = (ba_ref[...] * u).astype(BF16)

    def part(k):
        return pl.BlockSpec((None, s, tc), lambda i: (k, 0, i))

    return pl.pallas_call(
        body, name="conv_fwd", grid=(sw // tc,),
        in_specs=[part(0), part(1), part(2),
                  pl.BlockSpec((CONV_K, tc), lambda i: (0, i)), pl.BlockSpec((1, tc), lambda i: (0, i))],
        out_specs=pl.BlockSpec((s, tc), lambda i: (0, i)),
        out_shape=jax.ShapeDtypeStruct((s, sw), BF16),
        compiler_params=_cp(1),
    )(proj, proj, proj, conv_w, conv_b)


def _pool_counts(shape, window):
    t = lax.broadcasted_iota(jnp.int32, shape, 0)
    return jnp.minimum(t + 1, window).astype(F32)


def _pool_fwd(proj):
    _, s, sw = proj.shape
    gw = sw // len(POOL_WINDOWS)

    def body(v_ref, p_ref):
        for gi, window in enumerate(POOL_WINDOWS):
            @pl.when(pl.program_id(0) == gi)
            def _():
                v = v_ref[...]
                acc, k = v, 1
                while k < window:
                    acc = acc + _shift_down(acc, k)
                    k *= 2
                p_ref[...] = (acc / _pool_counts(v.shape, window) - v).astype(BF16)

    return pl.pallas_call(
        body, name="pool_fwd", grid=(len(POOL_WINDOWS),),
        in_specs=[pl.BlockSpec((None, s, gw), lambda g: (3, 0, g))],
        out_specs=pl.BlockSpec((s, gw), lambda g: (0, g)),
        out_shape=jax.ShapeDtypeStruct((s, sw), BF16),
        compiler_params=_cp(1),
    )(proj)


def _merge_fwd(z, wa, p, wpool, proj, b_gate2, pool_scale):
    s, sw = z.shape
    tn = wa.shape[2]
    d = tn * N_DEV
    gw = sw // len(POOL_WINDOWS)
    nq = sw // tn

    def body(z_ref, wa_ref, p_ref, wp_ref, ga_ref, gb_ref, bg_ref, sc_ref,
             m_ref, dya_ref, dyb_ref, dga_ref, dgb_ref, dsc_ref):
        for rs in _chunks(s, 512):
            ya = _dot(z_ref[rs, :], wa_ref[...], NN)
            yb = _dot(p_ref[rs, :], wp_ref[...], NN)
            sa = _sigmoid(ga_ref[rs, :] + bg_ref[0:1, :])
            sb = _sigmoid(gb_ref[rs, :] + bg_ref[1:2, :])
            sc = sc_ref[...]
            sb_yb = sb * yb
            m_ref[rs, :] = (sa * ya + sb_yb * sc).astype(BF16)
            dya_ref[rs, :] = sa.astype(BF16)
            dyb_ref[rs, :] = (sb * sc).astype(BF16)
            dga_ref[rs, :] = (ya * (sa * (1.0 - sa))).astype(BF16)
            dgb_ref[rs, :] = ((yb * sc) * (sb * (1.0 - sb))).astype(BF16)
            dsc_ref[rs, :] = sb_yb.astype(BF16)

    col = pl.BlockSpec((s, tn), lambda j: (0, j))
    out = jax.ShapeDtypeStruct((s, d), BF16)
    return pl.pallas_call(
        body, name="merge_fwd", grid=(N_DEV,),
        in_specs=[pl.BlockSpec((s, sw), lambda j: (0, 0)),
                  pl.BlockSpec((None, sw, tn), lambda j: (j, 0, 0)),
                  pl.BlockSpec((s, gw), lambda j: (0, j // 2)),
                  pl.BlockSpec((None, gw, tn), lambda j: (j // 2, 0, j % 2)),
                  pl.BlockSpec((None, s, tn), lambda j: (4 + j // nq, 0, j % nq)),
                  pl.BlockSpec((None, s, tn), lambda j: (6 + j // nq, 0, j % nq)),
                  pl.BlockSpec((2, tn), lambda j: (0, j)),
                  pl.BlockSpec((1, tn), lambda j: (0, j))],
        out_specs=[col] * 6,
        out_shape=[out] * 6,
        compiler_params=_cp(1),
    )(z, wa, p, wpool, proj, proj, b_gate2, pool_scale)


def _wo_fwd(merged, wo, x2d, g2):
    s, d = x2d.shape
    tm = min(256, s)

    def body(m_ref, wo_ref, x_ref, g_ref, x1_ref, h2_ref):
        x1 = x_ref[...] + _dot(m_ref[...], wo_ref[...], NN)
        x1_ref[...] = x1
        r = lax.rsqrt(jnp.mean(x1 * x1, axis=-1, keepdims=True) + EPS)
        h2_ref[...] = (x1 * r * g_ref[...]).astype(BF16)

    row = pl.BlockSpec((tm, d), lambda i: (i, 0))
    return pl.pallas_call(
        body, name="wo_fwd", grid=(s // tm,),
        in_specs=[row, pl.BlockSpec((d, d), lambda i: (0, 0)), row, pl.BlockSpec((1, d), lambda i: (0, 0))],
        out_specs=[row, row],
        out_shape=[jax.ShapeDtypeStruct((s, d), F32), jax.ShapeDtypeStruct((s, d), BF16)],
        compiler_params=_cp(1),
    )(merged, wo, x2d, g2)


def _ffn_gate_fwd(h2, wg_g):
    s, d = h2.shape
    f8 = wg_g.shape[2]
    th = min(1024, s)

    def body(h_ref, wg_ref, g_ref):
        i = pl.program_id(1)
        for rs in _chunks(th, 512):
            rows = pl.ds(pl.multiple_of(i * th + rs.start, rs.stop - rs.start), rs.stop - rs.start)
            g_ref[rs, :] = _dot(h_ref[rows, :], wg_ref[...], NN).astype(BF16)

    return pl.pallas_call(
        body, name="ffn_gate_fwd", grid=(N_DEV, s // th),
        in_specs=[pl.BlockSpec((s, d), lambda j, i: (0, 0)), pl.BlockSpec((None, d, f8), lambda j, i: (j, 0, 0))],
        out_specs=pl.BlockSpec((None, th, f8), lambda j, i: (j, i, 0)),
        out_shape=jax.ShapeDtypeStruct((N_DEV, s, f8), BF16),
        compiler_params=_cp(2),
    )(h2, wg_g)


def _ffn_up_act_fwd(h2, wu_g, gate):
    s, d = h2.shape
    f8 = wu_g.shape[2]
    th = min(1024, s)

    def body(h_ref, wu_ref, g_ref, dadu_ref, dadg_ref, a_ref, u_ref):
        i = pl.program_id(1)
        chunks = _chunks(th, 512)

        def matmul(rs):
            rows = pl.ds(pl.multiple_of(i * th + rs.start, rs.stop - rs.start), rs.stop - rs.start)
            u_ref[rs, :] = _dot(h_ref[rows, :], wu_ref[...], NN)

        matmul(chunks[0])
        for k, rs in enumerate(chunks):
            if k + 1 < len(chunks):
                matmul(chunks[k + 1])
            g = g_ref[rs, :].astype(F32)
            u = u_ref[rs, :]
            sg = _sigmoid(g)
            silu = g * sg
            dadu_ref[rs, :] = silu.astype(BF16)
            dadg_ref[rs, :] = (u * (sg * (1.0 + g * (1.0 - sg)))).astype(BF16)
            a_ref[rs, :] = (silu * u).astype(BF16)

    wspec = pl.BlockSpec((None, d, f8), lambda j, i: (j, 0, 0))
    ospec = pl.BlockSpec((None, th, f8), lambda j, i: (j, i, 0))
    out = jax.ShapeDtypeStruct((N_DEV, s, f8), BF16)
    return pl.pallas_call(
        body, name="ffn_up_fwd", grid=(N_DEV, s // th),
        in_specs=[pl.BlockSpec((s, d), lambda j, i: (0, 0)), wspec, ospec],
        out_specs=[ospec, ospec, ospec], out_shape=[out, out, out],
        scratch_shapes=[pltpu.VMEM((th, f8), F32)],
        compiler_params=_cp(2),
    )(h2, wu_g, gate)


def _ffn_down_fwd(act, wd_part, x1, part, name):
    _, s, f8 = act.shape
    tn = wd_part.shape[2]
    per = 2
    steps = N_DEV // per
    tr = s // steps

    def body(a_ref, wd_ref, x1_ref, o_ref):
        j = pl.program_id(0)

        @pl.when(j == 0)
        def _():
            o_ref[...] = jnp.zeros_like(o_ref)

        for rs in _chunks(s, 1024):
            term = _dot(a_ref[0, rs, :], wd_ref[0], NN)
            for q in range(1, per):
                term = term + _dot(a_ref[q, rs, :], wd_ref[q], NN)
            o_ref[rs, :] += term
        rows = pl.ds(pl.multiple_of(j * tr, tr), tr)
        o_ref[rows, :] += x1_ref[...]

    return pl.pallas_call(
        body, name=name, grid=(steps,),
        in_specs=[pl.BlockSpec((per, s, f8), lambda j: (j, 0, 0)),
                  pl.BlockSpec((per, f8, tn), lambda j: (j, 0, 0)),
                  pl.BlockSpec((tr, tn), lambda j: (j, part))],
        out_specs=pl.BlockSpec((s, tn), lambda j: (0, 0)),
        out_shape=jax.ShapeDtypeStruct((s, tn), F32),
        compiler_params=_cp(1),
    )(act, wd_part, x1)


def _loss_bwd(x2_parts, target, final_g):
    s, d = target.shape
    tm = min(256, s)
    nparts = len(x2_parts)

    def body(*refs):
        x2_refs = refs[:nparts]
        t_ref, gf_ref, dxb_ref, dgf_ref, loss_ref = refs[nparts:]

        @pl.when(pl.program_id(0) == 0)
        def _():
            dgf_ref[...] = jnp.zeros_like(dgf_ref)
            loss_ref[...] = jnp.zeros_like(loss_ref)

        x2 = jnp.concatenate([x2_ref[...] for x2_ref in x2_refs], axis=-1)
        r = lax.rsqrt(jnp.mean(x2 * x2, axis=-1, keepdims=True) + EPS)
        nrm = x2 * r
        gf = gf_ref[...]
        err = nrm * gf - t_ref[...]
        loss_ref[...] += jnp.sum(err * err) * (0.5 / d)
        dy = err * (1.0 / d)
        dgf_ref[...] += jnp.sum(dy * nrm, axis=0, keepdims=True)
        dn = dy * gf
        dx = r * (dn - nrm * jnp.mean(dn * nrm, axis=-1, keepdims=True))
        dxb_ref[...] = dx.astype(BF16)

    row = pl.BlockSpec((tm, d), lambda i: (i, 0))
    vec = pl.BlockSpec((1, d), lambda i: (0, 0))
    return pl.pallas_call(
        body, name="loss_bwd", grid=(s // tm,),
        in_specs=[pl.BlockSpec((tm, p.shape[1]), lambda i: (i, 0)) for p in x2_parts] + [row, vec],
        out_specs=[row, vec, pl.BlockSpec((8, LANES), lambda i: (0, 0))],
        out_shape=[jax.ShapeDtypeStruct((s, d), BF16),
                   jax.ShapeDtypeStruct((1, d), F32), jax.ShapeDtypeStruct((8, LANES), F32)],
        compiler_params=_cp(1),
    )(*x2_parts, target, final_g)


def _ffn_gate_bwd(dx2b, wd_parts, dadg, dadu):
    s, d = dx2b.shape
    f8 = dadg.shape[2]
    th = min(1024, s)
    nparts = len(wd_parts)
    pc = d // nparts

    def body(dx_ref, *refs):
        wd_refs = refs[:nparts]
        g_ref, u_ref, dg_ref, du_ref, da_ref = refs[nparts:]
        i = pl.program_id(1)
        chunks = _chunks(th, 256)

        def matmul(rs):
            rows = pl.ds(pl.multiple_of(i * th + rs.start, rs.stop - rs.start), rs.stop - rs.start)
            part = None
            for q, wd_ref in enumerate(wd_refs):
                term = _dot(dx_ref[rows, q * pc:(q + 1) * pc], wd_ref[...], NT)
                part = term if part is None else part + term
            da_ref[rs, :] = part

        matmul(chunks[0])
        for k, rs in enumerate(chunks):
            if k + 1 < len(chunks):
                matmul(chunks[k + 1])
            da = da_ref[rs, :].astype(BF16)
            dg_ref[rs, :] = da * g_ref[rs, :]
            du_ref[rs, :] = da * u_ref[rs, :]

    aspec = pl.BlockSpec((None, th, f8), lambda j, i: (j, i, 0))
    out = jax.ShapeDtypeStruct((N_DEV, s, f8), BF16)
    return pl.pallas_call(
        body, name="ffn_act_bwd", grid=(N_DEV, s // th),
        in_specs=[pl.BlockSpec((s, d), lambda j, i: (0, 0))]
        + [pl.BlockSpec((None, f8, pc), lambda j, i: (j, 0, 0))] * nparts + [aspec, aspec],
        out_specs=[aspec, aspec], out_shape=[out, out],
        scratch_shapes=[pltpu.VMEM((th, f8), F32)],
        compiler_params=_cp(2),
    )(dx2b, *wd_parts, dadg, dadu)


def _wgrad_rows(a3, b, name, after=(), carry=()):
    _, s, k = a3.shape
    n = b.shape[1]
    nc = len(carry)
    c_in, c_out, c_shape, c_sems = _carry_specs(carry)

    def body(a_ref, b_ref, *rest):
        rest = rest[len(after):]
        o_ref = rest[nc]
        j = pl.program_id(0)
        _carry_run(j == 0, j == N_DEV - 1, rest[:nc], rest[nc + 1:2 * nc + 1], rest[2 * nc + 1:])
        o_ref[...] = _dot(a_ref[...], b_ref[...], TN).astype(BF16)

    outs = pl.pallas_call(
        body, name=name, grid=(N_DEV,),
        in_specs=[pl.BlockSpec((None, s, k), lambda j: (j, 0, 0)),
                  pl.BlockSpec((s, n), lambda j: (0, 0))] + _after_specs(after) + c_in,
        out_specs=[pl.BlockSpec((None, k, n), lambda j: (j, 0, 0))] + c_out,
        out_shape=[jax.ShapeDtypeStruct((N_DEV, k, n), BF16)] + c_shape,
        scratch_shapes=c_sems,
        compiler_params=_cp_carry(1, carry),
    )(a3, b, *after, *carry)
    return (outs[0], list(outs[1:])) if nc else outs[0]


def _wgrad_cols(a, b3, name, after=()):
    s, k = a.shape
    if b3.ndim == 2:
        n = b3.shape[1] // N_DEV
        b_spec = pl.BlockSpec((s, n), lambda j: (0, j))
    else:
        n = b3.shape[2]
        b_spec = pl.BlockSpec((None, s, n), lambda j: (j, 0, 0))

    def body(a_ref, b_ref, *rest):
        o_ref = rest[len(after)]
        o_ref[...] = _dot(a_ref[...], b_ref[...], TN).astype(BF16)

    return pl.pallas_call(
        body, name=name, grid=(N_DEV,),
        in_specs=[pl.BlockSpec((s, k), lambda j: (0, 0)), b_spec] + _after_specs(after),
        out_specs=pl.BlockSpec((None, k, n), lambda j: (j, 0, 0)),
        out_shape=jax.ShapeDtypeStruct((N_DEV, k, n), BF16),
        compiler_params=_cp(1),
    )(a, b3, *after)


def _input_grad(pairs, name, after=(), carry=(), per=1):
    s = pairs[0][0].shape[1]
    d = pairs[0][1].shape[1]
    tn = min(1024, d)
    npair = len(pairs)
    nc = len(carry)
    c_in, c_out, c_shape, c_sems = _carry_specs(carry)

    def body(*refs):
        ops = refs[:2 * npair]
        rest = refs[2 * npair + len(after):]
        o_ref, acc_ref = rest[nc], rest[-1]
        nh, j = pl.program_id(0), pl.program_id(1)
        last_j = N_DEV // per - 1
        _carry_run((nh == 0) & (j == 0), (nh == d // tn - 1) & (j == last_j),
                   rest[:nc], rest[nc + 1:2 * nc + 1], rest[2 * nc + 1:-1])

        @pl.when(j == 0)
        def _():
            acc_ref[...] = jnp.zeros_like(acc_ref)

        for rs in _chunks(s, 1024):
            part = None
            for q in range(npair):
                for e in range(per):
                    term = _dot(ops[2 * q][e, rs, :], ops[2 * q + 1][e], NT)
                    part = term if part is None else part + term
            acc_ref[rs, :] += part

        @pl.when(j == last_j)
        def _():
            o_ref[...] = acc_ref[...].astype(BF16)

    in_specs, args = [], []
    for a3, w3 in pairs:
        k = a3.shape[2]
        in_specs += [pl.BlockSpec((per, s, k), lambda n, j: (j, 0, 0)),
                     pl.BlockSpec((per, tn, k), lambda n, j: (j, n, 0))]
        args += [a3, w3]
    outs = pl.pallas_call(
        body, name=name, grid=(d // tn, N_DEV // per),
        in_specs=in_specs + _after_specs(after) + c_in,
        out_specs=[pl.BlockSpec((s, tn), lambda n, j: (0, n))] + c_out,
        out_shape=[jax.ShapeDtypeStruct((s, d), BF16)] + c_shape,
        scratch_shapes=c_sems + [pltpu.VMEM((s, tn), F32)],
        compiler_params=_cp_carry(2, carry),
    )(*args, *after, *carry)
    return (outs[0], list(outs[1:])) if nc else outs[0]


def _rms_bwd(dh, xres, g, dres, name, with_bf16=True):
    s, d = xres.shape
    tm = min(256, s)

    def body(dh_ref, x_ref, g_ref, dres_ref, dx_ref, *rest):
        dg_ref = rest[-1]
        @pl.when(pl.program_id(0) == 0)
        def _():
            dg_ref[...] = jnp.zeros_like(dg_ref)

        xv = x_ref[...]
        dh_v = dh_ref[...].astype(F32)
        r = lax.rsqrt(jnp.mean(xv * xv, axis=-1, keepdims=True) + EPS)
        nrm = xv * r
        dg_ref[...] += jnp.sum(dh_v * nrm, axis=0, keepdims=True)
        dn = dh_v * g_ref[...]
        dx = dres_ref[...].astype(F32) + r * (dn - nrm * jnp.mean(dn * nrm, axis=-1, keepdims=True))
        dx_ref[...] = dx
        if with_bf16:
            rest[0][...] = dx.astype(BF16)

    row = pl.BlockSpec((tm, d), lambda i: (i, 0))
    vec = pl.BlockSpec((1, d), lambda i: (0, 0))
    copies = [jax.ShapeDtypeStruct((s, d), BF16)] if with_bf16 else []
    outs = pl.pallas_call(
        body, name=name, grid=(s // tm,),
        in_specs=[row, row, vec, row],
        out_specs=[row] + [row] * len(copies) + [vec],
        out_shape=[jax.ShapeDtypeStruct((s, d), F32)] + copies + [jax.ShapeDtypeStruct((1, d), F32)],
        compiler_params=_cp(1),
    )(dh, xres, g, dres)
    return (outs[0], outs[1], outs[2]) if with_bf16 else (outs[0], None, outs[1])


def _wgrad_full(a, b, name, after=(), carry=()):
    s, k = a.shape
    n = b.shape[1]
    tk = min(512, k)
    nc = len(carry)
    c_in, c_out, c_shape, c_sems = _carry_specs(carry)

    def body(a_ref, b_ref, *rest):
        rest = rest[len(after):]
        o_ref = rest[nc]
        j = pl.program_id(0)
        _carry_run(j == 0, j == k // tk - 1, rest[:nc], rest[nc + 1:2 * nc + 1], rest[2 * nc + 1:])
        o_ref[...] = _dot(a_ref[...], b_ref[...], TN).astype(BF16)

    outs = pl.pallas_call(
        body, name=name, grid=(k // tk,),
        in_specs=[pl.BlockSpec((s, tk), lambda j: (0, j)),
                  pl.BlockSpec((s, n), lambda j: (0, 0))] + _after_specs(after) + c_in,
        out_specs=[pl.BlockSpec((tk, n), lambda j: (j, 0))] + c_out,
        out_shape=[jax.ShapeDtypeStruct((k, n), BF16)] + c_shape,
        scratch_shapes=c_sems,
        compiler_params=_cp_carry(1, carry),
    )(a, b, *after, *carry)
    return (outs[0], list(outs[1:])) if nc else outs[0]


def _wgrad_pool(p, dyb, n_groups):
    s, sw = p.shape
    d = dyb.shape[1]
    gw, go = sw // n_groups, d // n_groups
    ts = min(512, s)
    ns = s // ts

    def body(a_ref, b_ref, o_ref, acc_ref):
        i = pl.program_id(1)

        @pl.when(i == 0)
        def _():
            acc_ref[...] = jnp.zeros_like(acc_ref)

        acc_ref[...] += _dot(a_ref[...], b_ref[...], TN)

        @pl.when(i == ns - 1)
        def _():
            o_ref[...] = acc_ref[...].astype(BF16)

    return pl.pallas_call(
        body, name="wgrad_pool", grid=(n_groups, ns),
        in_specs=[pl.BlockSpec((ts, gw), lambda g, i: (i, g)),
                  pl.BlockSpec((ts, go), lambda g, i: (i, g))],
        out_specs=pl.BlockSpec((None, gw, go), lambda g, i: (g, 0, 0)),
        out_shape=jax.ShapeDtypeStruct((n_groups, gw, go), BF16),
        scratch_shapes=[pltpu.VMEM((gw, go), F32)],
        compiler_params=_cp(2),
    )(p, dyb)


def _wo_bwd(dx1b, wo, factors, sw, after=()):
    s, d = dx1b.shape
    tn = d // N_DEV
    nq = sw // tn

    def body(dx_ref, wo_ref, fya_ref, fyb_ref, fga_ref, fgb_ref, fsc_ref, *rest):
        dya_ref, dyb_ref, dp_ref, dbg_ref, dsc_ref, dm_ref = rest[len(after):]
        dbg_ref[...] = jnp.zeros_like(dbg_ref)
        dsc_ref[...] = jnp.zeros_like(dsc_ref)
        for rs in _chunks(s, 1024):
            dm_ref[rs, :] = _dot(dx_ref[rs, :], wo_ref[...], NT)
        for rs in _chunks(s, 256):
            dm = dm_ref[rs, :]
            dya_ref[rs, :] = (dm * fya_ref[rs, :].astype(F32)).astype(BF16)
            dyb_ref[rs, :] = (dm * fyb_ref[rs, :].astype(F32)).astype(BF16)
            dsc_ref[...] += jnp.sum(dm * fsc_ref[rs, :].astype(F32), axis=0, keepdims=True)
            dga = dm * fga_ref[rs, :].astype(F32)
            dgb = dm * fgb_ref[rs, :].astype(F32)
            dp_ref[0, rs, :] = dga.astype(BF16)
            dp_ref[1, rs, :] = dgb.astype(BF16)
            dbg_ref[0:1, :] += jnp.sum(dga, axis=0, keepdims=True)
            dbg_ref[1:2, :] += jnp.sum(dgb, axis=0, keepdims=True)

    col = pl.BlockSpec((s, tn), lambda j: (0, j))
    out = jax.ShapeDtypeStruct((s, d), BF16)
    return pl.pallas_call(
        body, name="wo_bwd", grid=(N_DEV,),
        in_specs=[pl.BlockSpec((s, d), lambda j: (0, 0)),
                  pl.BlockSpec((tn, d), lambda j: (j, 0))] + [col] * 5 + _after_specs(after),
        out_specs=[col, col,
                   pl.BlockSpec((2, None, s, tn), lambda j: (1, j // nq, 0, j % nq)),
                   pl.BlockSpec((2, tn), lambda j: (0, j)),
                   pl.BlockSpec((1, tn), lambda j: (0, j))],
        out_shape=[out, out, jax.ShapeDtypeStruct((4, 2, s, sw), BF16),
                   jax.ShapeDtypeStruct((2, d), F32), jax.ShapeDtypeStruct((1, d), F32)],
        scratch_shapes=[pltpu.VMEM((s, tn), F32)],
        compiler_params=_cp(1),
    )(dx1b, wo, *factors, *after)


def _conv_bwd(dproj, dya, wa, proj, conv_w, conv_b):
    s, d = dya.shape
    sw, tn = wa.shape[1], wa.shape[2]
    tc = min(LANES, sw)

    def body(dproj_hbm, dya_ref, wa_ref, ba_ref, ca_ref, va_ref, cw_ref, cb_ref,
             dp_ref, dcw_ref, dcb_ref, dz_ref):
        del dproj_hbm
        for rs in _chunks(s, 512):
            part = _dot(dya_ref[rs, 0:tn], wa_ref[0], NT)
            for j in range(1, N_DEV):
                part = part + _dot(dya_ref[rs, j * tn:(j + 1) * tn], wa_ref[j], NT)
            dz_ref[rs, :] = part
        dz = dz_ref[...]
        ba, ca, va = ba_ref[...], ca_ref[...], va_ref[...]
        cv = ca * va
        cv1, cv2 = _shift_down(cv, 1), _shift_down(cv, 2)
        w0, w1, w2 = cw_ref[0:1, :], cw_ref[1:2, :], cw_ref[2:3, :]
        u = cb_ref[...] + w0 * cv2 + w1 * cv1 + w2 * cv
        du = dz * ba
        dp_ref[0] = (dz * u).astype(BF16)
        dcv = w2 * du + w1 * _shift_up(du, 1) + w0 * _shift_up(du, 2)
        dp_ref[1] = (dcv * va).astype(BF16)
        dp_ref[2] = (dcv * ca).astype(BF16)
        dcw_ref[0:1, :] = jnp.sum(du * cv2, axis=0, keepdims=True)
        dcw_ref[1:2, :] = jnp.sum(du * cv1, axis=0, keepdims=True)
        dcw_ref[2:3, :] = jnp.sum(du * cv, axis=0, keepdims=True)
        dcb_ref[...] = jnp.sum(du, axis=0, keepdims=True)

    def part(k):
        return pl.BlockSpec((None, s, tc), lambda i: (k, 0, i))

    return pl.pallas_call(
        body, name="conv_bwd", grid=(sw // tc,),
        in_specs=[pl.BlockSpec(memory_space=pl.ANY),
                  pl.BlockSpec((s, d), lambda i: (0, 0)),
                  pl.BlockSpec((N_DEV, tc, tn), lambda i: (0, i, 0)),
                  part(0), part(1), part(2),
                  pl.BlockSpec((CONV_K, tc), lambda i: (0, i)), pl.BlockSpec((1, tc), lambda i: (0, i))],
        out_specs=[pl.BlockSpec((3, s, tc), lambda i: (0, 0, i)),
                   pl.BlockSpec((CONV_K, tc), lambda i: (0, i)), pl.BlockSpec((1, tc), lambda i: (0, i))],
        out_shape=[jax.ShapeDtypeStruct(dproj.shape, BF16),
                   jax.ShapeDtypeStruct((CONV_K, sw), F32), jax.ShapeDtypeStruct((1, sw), F32)],
        scratch_shapes=[pltpu.VMEM((s, tc), F32)],
        input_output_aliases={0: 0},
        compiler_params=_cp(1),
    )(dproj, dya, wa, proj, proj, proj, conv_w, conv_b)


def _pool_bwd(dproj, dyb, wpool):
    s, d = dyb.shape
    n_groups, gw, go = wpool.shape

    def body(dproj_hbm, dyb_ref, wp_ref, dp_ref):
        del dproj_hbm
        for gi, window in enumerate(POOL_WINDOWS):
            @pl.when(pl.program_id(0) == gi)
            def _():
                dpool = _dot(dyb_ref[...], wp_ref[...], NT)
                acc, k = dpool / _pool_counts(dpool.shape, window), 1
                while k < window:
                    acc = acc + _shift_up(acc, k)
                    k *= 2
                dp_ref[...] = (acc - dpool).astype(BF16)

    return pl.pallas_call(
        body, name="pool_bwd", grid=(n_groups,),
        in_specs=[pl.BlockSpec(memory_space=pl.ANY),
                  pl.BlockSpec((s, go), lambda g: (0, g)),
                  pl.BlockSpec((None, gw, go), lambda g: (g, 0, 0))],
        out_specs=pl.BlockSpec((None, s, gw), lambda g: (3, 0, g)),
        out_shape=jax.ShapeDtypeStruct(dproj.shape, BF16),
        input_output_aliases={0: 0},
        compiler_params=_cp(1),
    )(dproj, dyb, wpool)


def _rows128(v):
    return v.reshape(-1, LANES)


def kernel(x, norm1_g, w_in, b_gate, conv_w, conv_b, w_a_out, w_pool, pool_scale, w_o, norm2_g, w_ffn_gate, w_ffn_up, w_ffn_down, final_g, loss_target, m_norm1_g, m_w_in, m_b_gate, m_conv_w, m_conv_b, m_w_a_out, m_w_pool, m_pool_scale, m_w_o, m_norm2_g, m_w_ffn_gate, m_w_ffn_up, m_w_ffn_down, m_final_g, v_norm1_g, v_w_in, v_b_gate, v_conv_w, v_conv_b, v_w_a_out, v_w_pool, v_pool_scale, v_w_o, v_norm2_g, v_w_ffn_gate, v_w_ffn_up, v_w_ffn_down, v_final_g):
    s, d = x.shape[1], x.shape[2]
    sw = w_in.shape[2]
    n_groups = w_pool.shape[1]
    gw = w_pool.shape[2]
    go = w_pool.shape[3] * N_DEV
    f8 = w_ffn_gate.shape[2]
    cws = conv_w.shape[2]
    assert sw == conv_w.shape[2] * N_DEV == gw * n_groups and go * n_groups == d and n_groups == len(POOL_WINDOWS)

    xi, yi, ci = _coords()
    me = 4 * xi + 2 * yi + ci
    my_chip = 2 * xi + yi

    x2d = x.reshape(s, d)
    target = loss_target.reshape(s, d)
    final_g2 = final_g.reshape(1, d)
    b_gate2 = b_gate.reshape(2, d)

    big_names = ["w_in", "w_a_out", "w_pool", "w_o", "w_ffn_gate", "w_ffn_up", "w_ffn_down"]
    big_w = [w_in, w_a_out, w_pool, w_o, w_ffn_gate, w_ffn_up, w_ffn_down]
    big_m = [m_w_in, m_w_a_out, m_w_pool, m_w_o, m_w_ffn_gate, m_w_ffn_up, m_w_ffn_down]
    big_v = [v_w_in, v_w_a_out, v_w_pool, v_w_o, v_w_ffn_gate, v_w_ffn_up, v_w_ffn_down]
    shapes2d = [(w.size // w.shape[-1], w.shape[-1]) for w in big_w]
    big_w2 = [w.reshape(sh) for w, sh in zip(big_w, shapes2d)]
    transposed = (4, 5)

    def view2d(t, a):
        t2 = t.reshape(shapes2d[a])
        return t2.T if a in transposed else t2

    def unview(o, a):
        return (o.T if a in transposed else o).reshape(big_w[a].shape)

    sb = [_cast_bf16(w, "cast_" + nm, parts=2 if nm == "w_ffn_down" else 1) for w, nm in zip(big_w2, big_names)]
    win_g, wa_g, wpool_g, wo_g = _allgather_big([b[0] for b in sb[0:4]], "allgather_mixer", COLLECTIVE_GATHER)
    (wg_g,) = _allgather_big(sb[4], "allgather_ffn_gate", COLLECTIVE_GATHER)
    (wu_g,) = _allgather_big(sb[5], "allgather_ffn_up", COLLECTIVE_GATHER)
    wd_parts = [_allgather_big([part], "allgather_ffn_down_%d" % q, COLLECTIVE_GATHER)[0]
                for q, part in enumerate(sb[6])]
    convw_g = _allgather_small(jnp.pad(conv_w.reshape(CONV_K, cws), ((0, 8 - CONV_K), (0, 0))), "allgather_conv_w")
    conv_w_full = convw_g[:, :CONV_K, :].transpose(1, 0, 2).reshape(CONV_K, sw)
    wpool = wpool_g.reshape(N_DEV, n_groups, gw, go // N_DEV).transpose(1, 2, 0, 3).reshape(n_groups, gw, go)
    wo = wo_g.reshape(d, d)

    h = _rms_fwd(x2d, norm1_g)
    proj = _proj_fwd(h, win_g)
    z = _conv_fwd(proj, conv_w_full, conv_b)
    p = _pool_fwd(proj)
    merged, *merge_factors = _merge_fwd(z, wa_g, p, wpool, proj, b_gate2, pool_scale)
    x1, h2 = _wo_fwd(merged, wo, x2d, norm2_g)
    gate = _ffn_gate_fwd(h2, wg_g)
    dadu, dadg, act = _ffn_up_act_fwd(h2, wu_g, gate)
    x2_parts = [_ffn_down_fwd(act, wd, x1, q, "ffn_down_fwd_%d" % q) for q, wd in enumerate(wd_parts)]
    dx2b, d_final_g, loss_blk = _loss_bwd(x2_parts, target, final_g2)

    other_chips = jnp.stack([2 * (1 - xi) + yi, 2 * xi + (1 - yi), 2 * (1 - xi) + (1 - yi)])
    others = jnp.concatenate([other_chips, 2 * other_chips + ci]).astype(jnp.int32)

    def partials(grads, recvs, names):
        if all(g.shape == grads[0].shape for g in grads):
            return list(_chip_partial(others, grads, recvs, "chip_partial_" + names[0]))
        return [_chip_partial(others, [g3], [r], "chip_partial_" + nm)[0] for g3, r, nm in zip(grads, recvs, names)]

    own = jnp.stack([me, my_chip]).astype(jnp.int32)

    def adam(idx, g3s, sibs, chipss):
        wmvs = [(view2d(big_w[a], a), view2d(big_m[a], a), view2d(big_v[a], a)) for a in idx]
        outs = _adam_big(own, wmvs, g3s, sibs, chipss, "adam_" + big_names[idx[0]])
        for a, o4 in zip(idx, outs):
            big_out[a] = [unview(o, a) for o in o4]

    big_out = [None] * len(big_names)
    dg_act, du_act = _ffn_gate_bwd(dx2b, wd_parts, dadg, dadu)
    gw_gate = _wgrad_rows(dg_act, h2, "wgrad_ffn_gate")
    gw_up = _wgrad_rows(du_act, h2, "wgrad_ffn_up")
    gw_down, sib_gu = _wgrad_rows(act, dx2b, "wgrad_ffn_down", carry=[gw_gate, gw_up])
    ps_gu = partials([gw_gate, gw_up], sib_gu, ["w_ffn_gate", "w_ffn_up"])
    chips_gu = _exchange_chips(ps_gu, "rs_chips_ffn_up", COLLECTIVE_CHIPS)
    dh2, sib_down = _input_grad([(dg_act, wg_g), (du_act, wu_g)], "ffn_in_bwd", after=ps_gu, carry=[gw_down])
    ps_down = partials([gw_down], sib_down, ["w_ffn_down"])
    chips_down = _exchange_chips(ps_down, "rs_chips_ffn_down", COLLECTIVE_CHIPS)
    dx1, dx1b, d_norm2_g = _rms_bwd(dh2, x1, norm2_g, dx2b, "rms2_bwd")
    dya, dyb, dproj42, d_b_gate, d_pool_scale = _wo_bwd(dx1b, wo, merge_factors, sw, after=ps_down)
    dproj = dproj42.reshape(N_DEV, s, sw)
    dproj, d_conv_w, d_conv_b = _conv_bwd(dproj, dya, wa_g, proj, conv_w_full, conv_b)
    dproj = _pool_bwd(dproj, dyb, wpool)
    gw_in = _wgrad_cols(h, dproj, "wgrad_in")
    gw_o, sib_in = _wgrad_full(merged, dx1b, "wgrad_o", carry=[gw_in])
    ps_in = partials([gw_in], sib_in, ["w_in"])
    chips_in = _exchange_chips(ps_in, "rs_chips_w_in", COLLECTIVE_CHIPS)
    gw_a = _wgrad_cols(z, dya, "wgrad_a_out", after=ps_in)
    gw_pool = _wgrad_pool(p, dyb, n_groups)
    mix3 = [gw_a,
            gw_pool.reshape(n_groups, gw, N_DEV, go // N_DEV).transpose(2, 0, 1, 3).reshape(N_DEV, n_groups * gw, go // N_DEV),
            gw_o.reshape(N_DEV, d // N_DEV, d)]
    adam([4, 5, 6], [gw_gate, gw_up, gw_down], sib_gu + sib_down, chips_gu + chips_down)
    dh, sib_mix = _input_grad([(dproj, win_g)], "proj_in_bwd", after=[big_out[6][0]], carry=mix3, per=2)
    ps_mix = partials(mix3, sib_mix, ["w_a_out", "w_pool", "w_o"])
    chips_mix = _exchange_chips(ps_mix, "rs_chips_mixer", COLLECTIVE_CHIPS)
    grad_x, _, d_norm1_g = _rms_bwd(dh, x2d, norm1_g, dx1, "rms1_bwd", with_bf16=False)
    adam([0], [gw_in], sib_in, chips_in)
    for k in range(3):
        adam([1 + k], [mix3[k]], [sib_mix[k]], [chips_mix[k]])

    small_parts = [d_norm1_g, d_b_gate, d_conv_w, d_conv_b, d_pool_scale, d_norm2_g, d_final_g, loss_blk]
    rows = [v.size // LANES for v in small_parts]
    row0 = [sum(rows[:k]) for k in range(len(rows))]
    packed = jnp.concatenate([_rows128(v) for v in small_parts], axis=0)
    gathered = _allgather_small(packed, "allgather_small_grads")
    small_names = ["norm1_g", "b_gate", "conv_b", "pool_scale", "norm2_g", "final_g", "conv_w"]
    small_w = [norm1_g, b_gate, conv_b, pool_scale, norm2_g, final_g]
    small_m = [m_norm1_g, m_b_gate, m_conv_b, m_pool_scale, m_norm2_g, m_final_g]
    small_v = [v_norm1_g, v_b_gate, v_conv_b, v_pool_scale, v_norm2_g, v_final_g]
    finished = _small_finish(gathered, [tuple(_rows128(t) for t in wmv) for wmv in zip(small_w, small_m, small_v)],
                             [row0[k] for k in (0, 1, 3, 4, 5, 6)], [(row0[2], rows[2]), (row0[7], rows[7])])
    g_convw_full, loss_rows = finished[0], finished[1]
    loss = loss_rows[0, 0]
    small_out = [[t.reshape(w.shape) for t in finished[2 + 4 * k:6 + 4 * k]] for k, w in enumerate(small_w)]
    g_convw = lax.dynamic_slice(g_convw_full.reshape(CONV_K, sw), (0, me * cws), (CONV_K, cws))
    cw_delta, cw_m, cw_v = _adam_small(conv_w.reshape(CONV_K, cws), g_convw,
                                       m_conv_w.reshape(CONV_K, cws), v_conv_w.reshape(CONV_K, cws))
    small_out.append([t.reshape(conv_w.shape) for t in (g_convw, cw_delta, cw_m, cw_v)])

    order = ["norm1_g", "w_in", "b_gate", "conv_w", "conv_b", "w_a_out", "w_pool", "pool_scale", "w_o", "norm2_g",
             "w_ffn_gate", "w_ffn_up", "w_ffn_down", "final_g"]
    per_kind = [{}, {}, {}, {}]
    for a, nm in enumerate(big_names):
        for kind in range(4):
            per_kind[kind][nm] = big_out[a][kind]
    for k, nm in enumerate(small_names):
        for kind in range(4):
            per_kind[kind][nm] = small_out[k][kind]
    result = [loss, grad_x.reshape(x.shape)]
    for kind in range(4):
        result += [per_kind[kind][nm] for nm in order]
    return tuple(result)
```

```python
import jax
import jax.numpy as jnp
from jax import lax
from jax.experimental import pallas as pl
from jax.experimental.pallas import tpu as pltpu
from jax.experimental.pallas import tpu_sc as plsc

F32 = jnp.float32
BF16 = jnp.bfloat16
MESH = pl.DeviceIdType.MESH

N_DEV = 8
EPS = 1e-6
CONV_K = 3
POOL_WINDOWS = (2, 4, 8, 16)
ADAM_LR = 0.001
ADAM_B1 = 0.9
ADAM_B2 = 0.999
ADAM_EPS = 1e-08
ADAM_WD = 0.01
ADAM_STEP = 10

V7X_VMEM_LIMIT_BYTES = 56 * 1024 * 1024
LANES = 128

COLLECTIVE_GATHER = 1
COLLECTIVE_SIBLING = 2
COLLECTIVE_CHIPS = 3
COLLECTIVE_ALL = 4
SEQUENCER_COST_BYTES = 4 * 10**9

NN = ((1,), (0,))
NT = ((1,), (1,))
TN = ((0,), (0,))


def _dot(a, b, dims):
    return lax.dot_general(a, b, (dims, ((), ())), preferred_element_type=F32)


def _cp(n_axes):
    return pltpu.CompilerParams(dimension_semantics=("arbitrary",) * n_axes,
                                vmem_limit_bytes=V7X_VMEM_LIMIT_BYTES)


def _row_tile(rows, bytes_per_row, cap_bytes):
    best = None
    for t in range(16, rows + 1, 16):
        if rows % t == 0 and t * bytes_per_row <= cap_bytes:
            best = t
    return best if best is not None else rows


def _chunks(total, size):
    size = min(size, total)
    assert total % size == 0
    return [slice(r, r + size) for r in range(0, total, size)]


def _after_specs(after):
    return [pl.BlockSpec(memory_space=pl.ANY)] * len(after)


def _shift_down(v, k):
    row = lax.broadcasted_iota(jnp.int32, v.shape, 0)
    return jnp.where(row >= k, pltpu.roll(v, k, 0), 0.0)


def _shift_up(v, k):
    n = v.shape[0]
    row = lax.broadcasted_iota(jnp.int32, v.shape, 0)
    return jnp.where(row < n - k, pltpu.roll(v, n - k, 0), 0.0)


def _sigmoid(v):
    return jax.nn.sigmoid(v)


def _cast_bf16(w2d, name, parts=1):
    rows, cols = w2d.shape
    tr = _row_tile(rows, cols * 4, 2 << 20)
    pc = cols // parts

    def body(i_ref, *o_refs):
        for q, o_ref in enumerate(o_refs):
            o_ref[...] = i_ref[:, q * pc:(q + 1) * pc].astype(BF16)

    return pl.pallas_call(
        body, name=name, grid=(rows // tr,),
        in_specs=[pl.BlockSpec((tr, cols), lambda i: (i, 0))],
        out_specs=[pl.BlockSpec((tr, pc), lambda i: (i, 0))] * parts,
        out_shape=[jax.ShapeDtypeStruct((rows, pc), BF16)] * parts,
        compiler_params=_cp(1),
    )(w2d)


def _rms_fwd(x2d, g):
    s, d = x2d.shape
    tm = min(256, s)

    def body(x_ref, g_ref, h_ref):
        xv = x_ref[...]
        r = lax.rsqrt(jnp.mean(xv * xv, axis=-1, keepdims=True) + EPS)
        h_ref[...] = (xv * r * g_ref[...]).astype(BF16)

    return pl.pallas_call(
        body, name="rms1_fwd", grid=(s // tm,),
        in_specs=[pl.BlockSpec((tm, d), lambda i: (i, 0)), pl.BlockSpec((1, d), lambda i: (0, 0))],
        out_specs=pl.BlockSpec((tm, d), lambda i: (i, 0)),
        out_shape=jax.ShapeDtypeStruct((s, d), BF16),
        compiler_params=_cp(1),
    )(x2d, g)


def _coords():
    return lax.axis_index("x"), lax.axis_index("y"), lax.axis_index("c")


def _slot(p):
    return 4 * p[0] + 2 * p[1] + p[2]


def _handshake(peers):
    barrier = pltpu.get_barrier_semaphore()
    for peer in peers:
        pl.semaphore_signal(barrier, inc=1, device_id=peer, device_id_type=MESH)
    pl.semaphore_wait(barrier, len(peers))


def _sequencer_call(body, out_type, scratch_types, name, collective_id):
    return pl.kernel(
        body, out_type=out_type, name=name,
        mesh=plsc.ScalarSubcoreMesh(axis_name="seq", num_cores=1),
        scratch_types=scratch_types,
        cost_estimate=pl.CostEstimate(flops=0, transcendentals=0, bytes_accessed=SEQUENCER_COST_BYTES),
        compiler_params=pltpu.CompilerParams(collective_id=collective_id))


def _allgather_big(shards, name, collective_id, after=()):
    n = len(shards)

    def body(*refs):
        ins, outs = refs[:n], refs[n + len(after):2 * n + len(after)]
        send_sems, recv_sems, local_sems = refs[2 * n + len(after):]
        x, y, c = _coords()
        me, sibling = (x, y, c), (x, y, 1 - c)
        x_nbr, y_nbr, diag = (1 - x, y), (x, 1 - y), (1 - x, 1 - y)
        relay_from = (x + (1 - c) * (1 - 2 * x), y + c * (1 - 2 * y))
        relay_to = (x + c * (1 - 2 * x), y + (1 - c) * (1 - 2 * y))
        _handshake([sibling, (*x_nbr, c), (*y_nbr, c)])

        def copy(a, k, block, to, src=None):
            dst = outs[a].at[_slot(block)]
            return pltpu.make_async_remote_copy(
                src_ref=dst if src is None else src, dst_ref=dst,
                send_sem=send_sems.at[a, k], recv_sem=recv_sems.at[a, k],
                device_id=to, device_id_type=MESH)

        mine, sends = [], []
        for a in range(n):
            cp = pltpu.make_async_copy(ins[a], outs[a].at[_slot(me)], local_sems.at[a])
            cp.start()
            mine.append(cp)
            first = [copy(a, 0, me, sibling, src=ins[a]),
                     copy(a, 1, me, (*x_nbr, c), src=ins[a]),
                     copy(a, 2, me, (*y_nbr, c), src=ins[a])]
            for cp in first:
                cp.start()
            sends += first
        for a in range(n):
            copy(a, 1 + c, (*relay_from, c), me).wait_recv()
            passed = [copy(a, 3, (*relay_from, c), (*relay_to, c)), copy(a, 4 + c, (*relay_from, c), sibling)]
            for cp in passed:
                cp.start()
            copy(a, 2 - c, (*relay_to, c), me).wait_recv()
            cp = copy(a, 5 - c, (*relay_to, c), sibling)
            cp.start()
            passed.append(cp)
            copy(a, 3, (*diag, c), me).wait_recv()
            cp = copy(a, 6, (*diag, c), sibling)
            cp.start()
            sends += passed + [cp]
        for a in range(n):
            copy(a, 0, sibling, me).wait_recv()
            copy(a, 4, (*x_nbr, 1 - c), me).wait_recv()
            copy(a, 5, (*y_nbr, 1 - c), me).wait_recv()
            copy(a, 6, (*diag, 1 - c), me).wait_recv()
        for cp in sends:
            cp.wait_send()
        for cp in mine:
            cp.wait()

    return _sequencer_call(
        body, [jax.ShapeDtypeStruct((N_DEV,) + s.shape, s.dtype) for s in shards],
        [pltpu.SemaphoreType.DMA((n, 7)), pltpu.SemaphoreType.DMA((n, 7)), pltpu.SemaphoreType.DMA((n,))],
        name, collective_id)(*shards, *after)


def _sibling_copies(ins, recvs, send_sems, recv_sems):
    x, y, c = _coords()
    return [pltpu.make_async_remote_copy(
        src_ref=ins[a].at[2 * q + (1 - c)], dst_ref=recvs[a].at[q],
        send_sem=send_sems.at[a, q], recv_sem=recv_sems.at[a, q],
        device_id=(x, y, 1 - c), device_id_type=MESH) for a in range(len(ins)) for q in range(4)]


def _carry_specs(carry):
    any_spec = pl.BlockSpec(memory_space=pl.ANY)
    n = len(carry)
    sems = [pltpu.SemaphoreType.DMA((n, 4)), pltpu.SemaphoreType.DMA((n, 4))] if n else []
    return ([any_spec] * n, [any_spec] * n,
            [jax.ShapeDtypeStruct((4,) + g.shape[1:], g.dtype) for g in carry], sems)


def _carry_run(first, last, ins, recvs, sems):
    if not ins:
        return

    @pl.when(first)
    def _():
        x, y, c = _coords()
        _handshake([(x, y, 1 - c)])
        for cp in _sibling_copies(ins, recvs, *sems):
            cp.start()

    @pl.when(last)
    def _():
        copies = _sibling_copies(ins, recvs, *sems)
        for cp in copies:
            cp.wait_recv()
        for cp in copies:
            cp.wait_send()


def _cp_carry(n_axes, carry):
    if not carry:
        return _cp(n_axes)
    return pltpu.CompilerParams(dimension_semantics=("arbitrary",) * n_axes, vmem_limit_bytes=V7X_VMEM_LIMIT_BYTES,
                                collective_id=COLLECTIVE_SIBLING)


def _exchange_chips(psums, name, collective_id):
    n = len(psums)

    def body(*refs):
        ins, outs = refs[:n], refs[n:2 * n]
        send_sems, recv_sems = refs[2 * n:]
        x, y, c = _coords()
        chips = [(1 - x, y), (x, 1 - y), (1 - x, 1 - y)]
        _handshake([(*chip, c) for chip in chips])
        copies = []
        for a in range(n):
            for j, chip in enumerate(chips):
                cp = pltpu.make_async_remote_copy(
                    src_ref=ins[a].at[2 * chip[0] + chip[1]], dst_ref=outs[a].at[j],
                    send_sem=send_sems.at[a, j], recv_sem=recv_sems.at[a, j],
                    device_id=(*chip, c), device_id_type=MESH)
                cp.start()
                copies.append(cp)
        for cp in copies:
            cp.wait_recv()
        for cp in copies:
            cp.wait_send()

    return _sequencer_call(
        body, [jax.ShapeDtypeStruct((3,) + p.shape[1:], p.dtype) for p in psums],
        [pltpu.SemaphoreType.DMA((n, 3)), pltpu.SemaphoreType.DMA((n, 3))],
        name, collective_id)(*psums)


def _allgather_small(v2d, name):
    rows, cols = v2d.shape

    def body(v_ref, out_ref, send_sems, recv_sems):
        x, y, c = _coords()
        me = (x, y, c)
        out_ref[_slot(me)] = v_ref[...]
        peers = []
        for k in range(1, N_DEV):
            fx, fy, fc = (k >> 2) & 1, (k >> 1) & 1, k & 1
            peers.append(((1 - x) if fx else x, (1 - y) if fy else y, (1 - c) if fc else c))
        sends = []
        for k, peer in enumerate(peers):
            cp = pltpu.make_async_remote_copy(
                src_ref=v_ref, dst_ref=out_ref.at[_slot(me)],
                send_sem=send_sems.at[k], recv_sem=recv_sems.at[k],
                device_id=peer, device_id_type=MESH)
            cp.start()
            sends.append(cp)
        for k, peer in enumerate(peers):
            pltpu.make_async_remote_copy(
                src_ref=v_ref, dst_ref=out_ref.at[_slot(peer)],
                send_sem=send_sems.at[k], recv_sem=recv_sems.at[k],
                device_id=peer, device_id_type=MESH).wait_recv()
        for cp in sends:
            cp.wait_send()

    vmem = pl.BlockSpec(memory_space=pltpu.VMEM)
    return pl.pallas_call(
        body, name=name, in_specs=[vmem], out_specs=vmem,
        out_shape=jax.ShapeDtypeStruct((N_DEV, rows, cols), v2d.dtype),
        scratch_shapes=[pltpu.SemaphoreType.DMA((N_DEV - 1,)), pltpu.SemaphoreType.DMA((N_DEV - 1,))],
    )(v2d)


def _allgather_small_async(v2d, name):
    rows, cols = v2d.shape

    def body(v_ref, out_ref, send_sems, recv_sems, local_sem):
        x, y, c = _coords()
        me = (x, y, c)
        peers = []
        for k in range(1, N_DEV):
            fx, fy, fc = (k >> 2) & 1, (k >> 1) & 1, k & 1
            peers.append(((1 - x) if fx else x, (1 - y) if fy else y, (1 - c) if fc else c))
        _handshake(peers)

        def copy(k, block, to):
            return pltpu.make_async_remote_copy(
                src_ref=v_ref, dst_ref=out_ref.at[_slot(block)],
                send_sem=send_sems.at[k], recv_sem=recv_sems.at[k],
                device_id=to, device_id_type=MESH)

        mine = pltpu.make_async_copy(v_ref, out_ref.at[_slot(me)], local_sem.at[0])
        mine.start()
        sends = [copy(k, me, peer) for k, peer in enumerate(peers)]
        for cp in sends:
            cp.start()
        for k, peer in enumerate(peers):
            copy(k, peer, peer).wait_recv()
        for cp in sends:
            cp.wait_send()
        mine.wait()

    return _sequencer_call(
        body, [jax.ShapeDtypeStruct((N_DEV, rows, cols), v2d.dtype)],
        [pltpu.SemaphoreType.DMA((N_DEV - 1,)), pltpu.SemaphoreType.DMA((N_DEV - 1,)), pltpu.SemaphoreType.DMA((1,))],
        name, COLLECTIVE_ALL)(v2d)[0]


def _chip_partial(others, grads, recvs, name):
    n = len(grads)
    _, rows, cols = grads[0].shape
    tr = _row_tile(rows, cols * 2, (2 << 20) // n)

    def body(others_ref, *refs):
        for a in range(n):
            refs[2 * n + a][...] = (refs[a][...].astype(F32) + refs[n + a][...].astype(F32)).astype(BF16)

    return pl.pallas_call(
        body, name=name,
        grid_spec=pltpu.PrefetchScalarGridSpec(
            num_scalar_prefetch=1, grid=(3, rows // tr),
            in_specs=[pl.BlockSpec((None, tr, cols), lambda k, i, o: (o[3 + k], i, 0))] * n
            + [pl.BlockSpec((None, tr, cols), lambda k, i, o: (o[k], i, 0))] * n,
            out_specs=[pl.BlockSpec((None, tr, cols), lambda k, i, o: (o[k], i, 0))] * n),
        out_shape=[jax.ShapeDtypeStruct((4, rows, cols), BF16)] * n,
        compiler_params=_cp(2),
    )(others, *grads, *recvs)


def _adam_math(w, g, m, v):
    m = ADAM_B1 * m + (1.0 - ADAM_B1) * g
    v = ADAM_B2 * v + (1.0 - ADAM_B2) * (g * g)
    m_hat = m / (1.0 - ADAM_B1 ** ADAM_STEP)
    v_hat = v / (1.0 - ADAM_B2 ** ADAM_STEP)
    delta = -ADAM_LR * (m_hat / (jnp.sqrt(v_hat) + ADAM_EPS) + ADAM_WD * w)
    return delta, m, v


def _adam_big(own, wmvs, g3s, recv_sibs, recv_chipss, name):
    n = len(wmvs)
    rows, cols = wmvs[0][0].shape
    tr = _row_tile(rows, cols * 4, (2 << 20) // n)

    def body(own_ref, *refs):
        ins, outs = refs[:6 * n], refs[6 * n:]
        for a in range(n):
            w_ref, m_ref, v_ref, g_ref, rs_ref, rc_ref = ins[6 * a:6 * a + 6]
            g = g_ref[...].astype(F32) + rs_ref[...].astype(F32)
            g = g + rc_ref[0].astype(F32)
            g = g + rc_ref[1].astype(F32)
            g = g + rc_ref[2].astype(F32)
            delta, m_new, v_new = _adam_math(w_ref[...], g, m_ref[...], v_ref[...])
            outs[4 * a][...] = g
            outs[4 * a + 1][...] = delta
            outs[4 * a + 2][...] = m_new
            outs[4 * a + 3][...] = v_new

    blk = pl.BlockSpec((tr, cols), lambda i, o: (i, 0))
    per_shard = [blk, blk, blk,
                 pl.BlockSpec((None, tr, cols), lambda i, o: (o[0], i, 0)),
                 pl.BlockSpec((None, tr, cols), lambda i, o: (o[1], i, 0)),
                 pl.BlockSpec((3, tr, cols), lambda i, o: (0, i, 0))]
    out = jax.ShapeDtypeStruct((rows, cols), F32)
    args = [t for a in range(n) for t in (*wmvs[a], g3s[a], recv_sibs[a], recv_chipss[a])]
    outs = pl.pallas_call(
        body, name=name,
        grid_spec=pltpu.PrefetchScalarGridSpec(
            num_scalar_prefetch=1, grid=(rows // tr,),
            in_specs=per_shard * n, out_specs=[blk] * (4 * n)),
        out_shape=[out] * (4 * n),
        compiler_params=_cp(1),
    )(own, *args)
    return [outs[4 * a:4 * a + 4] for a in range(n)]


def _small_finish(gathered, params, row_offs, extra_rows, after=()):
    n = len(params)

    def body(g_ref, *refs):
        ins, outs = refs[:3 * n], refs[3 * n + len(after):]
        total = g_ref[0]
        for k in range(1, N_DEV):
            total = total + g_ref[k]
        for e, (r0, nr) in enumerate(extra_rows):
            outs[e][...] = total[r0:r0 + nr, :]
        for p in range(n):
            w_ref, m_ref, v_ref = ins[3 * p:3 * p + 3]
            g_out, d_out, m_out, v_out = outs[len(extra_rows) + 4 * p:len(extra_rows) + 4 * p + 4]
            g = total[row_offs[p]:row_offs[p] + w_ref.shape[0], :]
            delta, m_new, v_new = _adam_math(w_ref[...], g, m_ref[...], v_ref[...])
            g_out[...] = g
            d_out[...] = delta
            m_out[...] = m_new
            v_out[...] = v_new

    vmem = pl.BlockSpec(memory_space=pltpu.VMEM)
    out_shape = [jax.ShapeDtypeStruct((nr, LANES), F32) for _, nr in extra_rows]
    for w, _, _ in params:
        out_shape += [jax.ShapeDtypeStruct(w.shape, F32)] * 4
    flat = [t for wmv in params for t in wmv]
    return pl.pallas_call(body, name="small_finish", in_specs=[vmem] * (1 + len(flat)) + _after_specs(after),
                          out_specs=[vmem] * len(out_shape), out_shape=out_shape)(gathered, *flat, *after)


def _adam_small(w, g, m, v):
    def body(w_ref, g_ref, m_ref, v_ref, do_ref, mo_ref, vo_ref):
        delta, m_new, v_new = _adam_math(w_ref[...], g_ref[...], m_ref[...], v_ref[...])
        do_ref[...] = delta
        mo_ref[...] = m_new
        vo_ref[...] = v_new

    vmem = pl.BlockSpec(memory_space=pltpu.VMEM)
    out = jax.ShapeDtypeStruct(w.shape, F32)
    return pl.pallas_call(body, name="adam_small", in_specs=[vmem] * 4, out_specs=[vmem] * 3,
                          out_shape=[out, out, out])(w, g, m, v)


def _proj_fwd(h, win_g):
    s, d = h.shape
    sw = win_g.shape[2]
    tn = min(512, sw)
    nh = sw // tn

    def body(h_ref, w_ref, o_ref):
        for rs in _chunks(s, 512):
            o_ref[rs, :] = _dot(h_ref[rs, :], w_ref[...], NN)

    return pl.pallas_call(
        body, name="proj_fwd", grid=(N_DEV * nh,),
        in_specs=[pl.BlockSpec((s, d), lambda j: (0, 0)),
                  pl.BlockSpec((None, d, tn), lambda j: (j // nh, 0, j % nh))],
        out_specs=pl.BlockSpec((None, s, tn), lambda j: (j // nh, 0, j % nh)),
        out_shape=jax.ShapeDtypeStruct((N_DEV, s, sw), F32),
        compiler_params=_cp(1),
    )(h, win_g)


def _conv_fwd(proj, conv_w, conv_b):
    _, s, sw = proj.shape
    tc = min(LANES, sw)

    def body(ba_ref, ca_ref, va_ref, cw_ref, cb_ref, z_ref):
        cv = ca_ref[...] * va_ref[...]
        u = (cb_ref[...] + cw_ref[0:1, :] * _shift_down(cv, 2) + cw_ref[1:2, :] * _shift_down(cv, 1)
             + cw_ref[2:3, :] * cv)
        z_ref[...] = (ba_ref[...] * u).astype(BF16)

    def part(k):
        return pl.BlockSpec((None, s, tc), lambda i: (k, 0, i))

    return pl.pallas_call(
        body, name="conv_fwd", grid=(sw // tc,),
        in_specs=[part(0), part(1), part(2),
                  pl.BlockSpec((CONV_K, tc), lambda i: (0, i)), pl.BlockSpec((1, tc), lambda i: (0, i))],
        out_specs=pl.BlockSpec((s, tc), lambda i: (0, i)),
        out_shape=jax.ShapeDtypeStruct((s, sw), BF16),
        compiler_params=_cp(1),
    )(proj, proj, proj, conv_w, conv_b)


def _pool_counts(shape, window):
    t = lax.broadcasted_iota(jnp.int32, shape, 0)
    return jnp.minimum(t + 1, window).astype(F32)


def _pool_fwd(proj):
    _, s, sw = proj.shape
    gw = sw // len(POOL_WINDOWS)

    def body(v_ref, p_ref):
        for gi, window in enumerate(POOL_WINDOWS):
            @pl.when(pl.program_id(0) == gi)
            def _():
                v = v_ref[...]
                acc, k = v, 1
                while k < window:
                    acc = acc + _shift_down(acc, k)
                    k *= 2
                p_ref[...] = (acc / _pool_counts(v.shape, window) - v).astype(BF16)

    return pl.pallas_call(
        body, name="pool_fwd", grid=(len(POOL_WINDOWS),),
        in_specs=[pl.BlockSpec((None, s, gw), lambda g: (3, 0, g))],
        out_specs=pl.BlockSpec((s, gw), lambda g: (0, g)),
        out_shape=jax.ShapeDtypeStruct((s, sw), BF16),
        compiler_params=_cp(1),
    )(proj)


def _merge_fwd(z, wa, p, wpool, proj, b_gate2, pool_scale):
    s, sw = z.shape
    tn = wa.shape[2]
    d = tn * N_DEV
    gw = sw // len(POOL_WINDOWS)
    nq = sw // tn

    def body(z_ref, wa_ref, p_ref, wp_ref, ga_ref, gb_ref, bg_ref, sc_ref,
             m_ref, dya_ref, dyb_ref, dga_ref, dgb_ref, dsc_ref):
        for rs in _chunks(s, 512):
            ya = _dot(z_ref[rs, :], wa_ref[...], NN)
            yb = _dot(p_ref[rs, :], wp_ref[...], NN)
            sa = _sigmoid(ga_ref[rs, :] + bg_ref[0:1, :])
            sb = _sigmoid(gb_ref[rs, :] + bg_ref[1:2, :])
            sc = sc_ref[...]
            sb_yb = sb * yb
            m_ref[rs, :] = (sa * ya + sb_yb * sc).astype(BF16)
            dya_ref[rs, :] = sa.astype(BF16)
            dyb_ref[rs, :] = (sb * sc).astype(BF16)
            dga_ref[rs, :] = (ya * (sa * (1.0 - sa))).astype(BF16)
            dgb_ref[rs, :] = ((yb * sc) * (sb * (1.0 - sb))).astype(BF16)
            dsc_ref[rs, :] = sb_yb.astype(BF16)

    col = pl.BlockSpec((s, tn), lambda j: (0, j))
    out = jax.ShapeDtypeStruct((s, d), BF16)
    return pl.pallas_call(
        body, name="merge_fwd", grid=(N_DEV,),
        in_specs=[pl.BlockSpec((s, sw), lambda j: (0, 0)),
                  pl.BlockSpec((None, sw, tn), lambda j: (j, 0, 0)),
                  pl.BlockSpec((s, gw), lambda j: (0, j // 2)),
                  pl.BlockSpec((None, gw, tn), lambda j: (j // 2, 0, j % 2)),
                  pl.BlockSpec((None, s, tn), lambda j: (4 + j // nq, 0, j % nq)),
                  pl.BlockSpec((None, s, tn), lambda j: (6 + j // nq, 0, j % nq)),
                  pl.BlockSpec((2, tn), lambda j: (0, j)),
                  pl.BlockSpec((1, tn), lambda j: (0, j))],
        out_specs=[col] * 6,
        out_shape=[out] * 6,
        compiler_params=_cp(1),
    )(z, wa, p, wpool, proj, proj, b_gate2, pool_scale)


def _wo_fwd(merged, wo, x2d, g2):
    s, d = x2d.shape
    tm = min(256, s)

    def body(m_ref, wo_ref, x_ref, g_ref, x1_ref, h2_ref):
        x1 = x_ref[...] + _dot(m_ref[...], wo_ref[...], NN)
        x1_ref[...] = x1
        r = lax.rsqrt(jnp.mean(x1 * x1, axis=-1, keepdims=True) + EPS)
        h2_ref[...] = (x1 * r * g_ref[...]).astype(BF16)

    row = pl.BlockSpec((tm, d), lambda i: (i, 0))
    return pl.pallas_call(
        body, name="wo_fwd", grid=(s // tm,),
        in_specs=[row, pl.BlockSpec((d, d), lambda i: (0, 0)), row, pl.BlockSpec((1, d), lambda i: (0, 0))],
        out_specs=[row, row],
        out_shape=[jax.ShapeDtypeStruct((s, d), F32), jax.ShapeDtypeStruct((s, d), BF16)],
        compiler_params=_cp(1),
    )(merged, wo, x2d, g2)


def _ffn_gate_fwd(h2, wg_g):
    s, d = h2.shape
    f8 = wg_g.shape[2]
    th = min(1024, s)

    def body(h_ref, wg_ref, g_ref):
        i = pl.program_id(1)
        for rs in _chunks(th, 512):
            rows = pl.ds(pl.multiple_of(i * th + rs.start, rs.stop - rs.start), rs.stop - rs.start)
            g_ref[rs, :] = _dot(h_ref[rows, :], wg_ref[...], NN).astype(BF16)

    return pl.pallas_call(
        body, name="ffn_gate_fwd", grid=(N_DEV, s // th),
        in_specs=[pl.BlockSpec((s, d), lambda j, i: (0, 0)), pl.BlockSpec((None, d, f8), lambda j, i: (j, 0, 0))],
        out_specs=pl.BlockSpec((None, th, f8), lambda j, i: (j, i, 0)),
        out_shape=jax.ShapeDtypeStruct((N_DEV, s, f8), BF16),
        compiler_params=_cp(2),
    )(h2, wg_g)


def _ffn_up_act_fwd(h2, wu_g, gate):
    s, d = h2.shape
    f8 = wu_g.shape[2]
    th = min(1024, s)

    def body(h_ref, wu_ref, g_ref, dadu_ref, dadg_ref, a_ref, u_ref):
        i = pl.program_id(1)
        chunks = _chunks(th, 256)

        def matmul(rs):
            rows = pl.ds(pl.multiple_of(i * th + rs.start, rs.stop - rs.start), rs.stop - rs.start)
            u_ref[rs, :] = _dot(h_ref[rows, :], wu_ref[...], NN)

        matmul(chunks[0])
        for k, rs in enumerate(chunks):
            if k + 1 < len(chunks):
                matmul(chunks[k + 1])
            g = g_ref[rs, :].astype(F32)
            u = u_ref[rs, :]
            sg = _sigmoid(g)
            silu = g * sg
            dadu_ref[rs, :] = silu.astype(BF16)
            dadg_ref[rs, :] = (u * (sg * (1.0 + g * (1.0 - sg)))).astype(BF16)
            a_ref[rs, :] = (silu * u).astype(BF16)

    wspec = pl.BlockSpec((None, d, f8), lambda j, i: (j, 0, 0))
    ospec = pl.BlockSpec((None, th, f8), lambda j, i: (j, i, 0))
    out = jax.ShapeDtypeStruct((N_DEV, s, f8), BF16)
    return pl.pallas_call(
        body, name="ffn_up_fwd", grid=(N_DEV, s // th),
        in_specs=[pl.BlockSpec((s, d), lambda j, i: (0, 0)), wspec, ospec],
        out_specs=[ospec, ospec, ospec], out_shape=[out, out, out],
        scratch_shapes=[pltpu.VMEM((th, f8), F32)],
        compiler_params=_cp(2),
    )(h2, wu_g, gate)


def _ffn_down_fwd(act, wd_part, name):
    _, s, f8 = act.shape
    tn = wd_part.shape[2]
    per = 2

    def body(a_ref, wd_ref, o_ref):
        @pl.when(pl.program_id(0) == 0)
        def _():
            o_ref[...] = jnp.zeros_like(o_ref)

        for rs in _chunks(s, 1024):
            part = _dot(a_ref[0, rs, :], wd_ref[0], NN)
            for q in range(1, per):
                part = part + _dot(a_ref[q, rs, :], wd_ref[q], NN)
            o_ref[rs, :] += part

    return pl.pallas_call(
        body, name=name, grid=(N_DEV // per,),
        in_specs=[pl.BlockSpec((per, s, f8), lambda j: (j, 0, 0)),
                  pl.BlockSpec((per, f8, tn), lambda j: (j, 0, 0))],
        out_specs=pl.BlockSpec((s, tn), lambda j: (0, 0)),
        out_shape=jax.ShapeDtypeStruct((s, tn), F32),
        compiler_params=_cp(1),
    )(act, wd_part)


def _loss_bwd(ffn_parts, x1, target, final_g):
    s, d = x1.shape
    tm = min(256, s)
    nparts = len(ffn_parts)

    def body(*refs):
        f_refs = refs[:nparts]
        x1_ref, t_ref, gf_ref, dxb_ref, dgf_ref, loss_ref = refs[nparts:]

        @pl.when(pl.program_id(0) == 0)
        def _():
            dgf_ref[...] = jnp.zeros_like(dgf_ref)
            loss_ref[...] = jnp.zeros_like(loss_ref)

        x2 = x1_ref[...] + jnp.concatenate([f_ref[...] for f_ref in f_refs], axis=-1)
        r = lax.rsqrt(jnp.mean(x2 * x2, axis=-1, keepdims=True) + EPS)
        nrm = x2 * r
        gf = gf_ref[...]
        err = nrm * gf - t_ref[...]
        loss_ref[...] += jnp.sum(err * err) * (0.5 / d)
        dy = err * (1.0 / d)
        dgf_ref[...] += jnp.sum(dy * nrm, axis=0, keepdims=True)
        dn = dy * gf
        dx = r * (dn - nrm * jnp.mean(dn * nrm, axis=-1, keepdims=True))
        dxb_ref[...] = dx.astype(BF16)

    row = pl.BlockSpec((tm, d), lambda i: (i, 0))
    vec = pl.BlockSpec((1, d), lambda i: (0, 0))
    return pl.pallas_call(
        body, name="loss_bwd", grid=(s // tm,),
        in_specs=[pl.BlockSpec((tm, f.shape[1]), lambda i: (i, 0)) for f in ffn_parts] + [row, row, vec],
        out_specs=[row, vec, pl.BlockSpec((8, LANES), lambda i: (0, 0))],
        out_shape=[jax.ShapeDtypeStruct((s, d), BF16),
                   jax.ShapeDtypeStruct((1, d), F32), jax.ShapeDtypeStruct((8, LANES), F32)],
        compiler_params=_cp(1),
    )(*ffn_parts, x1, target, final_g)


def _ffn_gate_bwd(dx2b, wd_parts, dadg, dadu):
    s, d = dx2b.shape
    f8 = dadg.shape[2]
    th = min(1024, s)
    nparts = len(wd_parts)
    pc = d // nparts

    def body(dx_ref, *refs):
        wd_refs = refs[:nparts]
        g_ref, u_ref, dg_ref, du_ref, da_ref = refs[nparts:]
        i = pl.program_id(1)
        chunks = _chunks(th, 256)

        def matmul(rs):
            rows = pl.ds(pl.multiple_of(i * th + rs.start, rs.stop - rs.start), rs.stop - rs.start)
            part = None
            for q, wd_ref in enumerate(wd_refs):
                term = _dot(dx_ref[rows, q * pc:(q + 1) * pc], wd_ref[...], NT)
                part = term if part is None else part + term
            da_ref[rs, :] = part

        matmul(chunks[0])
        for k, rs in enumerate(chunks):
            if k + 1 < len(chunks):
                matmul(chunks[k + 1])
            da = da_ref[rs, :].astype(BF16)
            dg_ref[rs, :] = da * g_ref[rs, :]
            du_ref[rs, :] = da * u_ref[rs, :]

    aspec = pl.BlockSpec((None, th, f8), lambda j, i: (j, i, 0))
    out = jax.ShapeDtypeStruct((N_DEV, s, f8), BF16)
    return pl.pallas_call(
        body, name="ffn_act_bwd", grid=(N_DEV, s // th),
        in_specs=[pl.BlockSpec((s, d), lambda j, i: (0, 0))]
        + [pl.BlockSpec((None, f8, pc), lambda j, i: (j, 0, 0))] * nparts + [aspec, aspec],
        out_specs=[aspec, aspec], out_shape=[out, out],
        scratch_shapes=[pltpu.VMEM((th, f8), F32)],
        compiler_params=_cp(2),
    )(dx2b, *wd_parts, dadg, dadu)


def _wgrad_rows(a3, b, name, after=(), carry=()):
    _, s, k = a3.shape
    n = b.shape[1]
    nc = len(carry)
    c_in, c_out, c_shape, c_sems = _carry_specs(carry)

    def body(a_ref, b_ref, *rest):
        rest = rest[len(after):]
        o_ref = rest[nc]
        j = pl.program_id(0)
        _carry_run(j == 0, j == N_DEV - 1, rest[:nc], rest[nc + 1:2 * nc + 1], rest[2 * nc + 1:])
        o_ref[...] = _dot(a_ref[...], b_ref[...], TN).astype(BF16)

    outs = pl.pallas_call(
        body, name=name, grid=(N_DEV,),
        in_specs=[pl.BlockSpec((None, s, k), lambda j: (j, 0, 0)),
                  pl.BlockSpec((s, n), lambda j: (0, 0))] + _after_specs(after) + c_in,
        out_specs=[pl.BlockSpec((None, k, n), lambda j: (j, 0, 0))] + c_out,
        out_shape=[jax.ShapeDtypeStruct((N_DEV, k, n), BF16)] + c_shape,
        scratch_shapes=c_sems,
        compiler_params=_cp_carry(1, carry),
    )(a3, b, *after, *carry)
    return (outs[0], list(outs[1:])) if nc else outs[0]


def _wgrad_cols(a, b3, name, after=()):
    s, k = a.shape
    if b3.ndim == 2:
        n = b3.shape[1] // N_DEV
        b_spec = pl.BlockSpec((s, n), lambda j: (0, j))
    else:
        n = b3.shape[2]
        b_spec = pl.BlockSpec((None, s, n), lambda j: (j, 0, 0))

    def body(a_ref, b_ref, *rest):
        o_ref = rest[len(after)]
        o_ref[...] = _dot(a_ref[...], b_ref[...], TN).astype(BF16)

    return pl.pallas_call(
        body, name=name, grid=(N_DEV,),
        in_specs=[pl.BlockSpec((s, k), lambda j: (0, 0)), b_spec] + _after_specs(after),
        out_specs=pl.BlockSpec((None, k, n), lambda j: (j, 0, 0)),
        out_shape=jax.ShapeDtypeStruct((N_DEV, k, n), BF16),
        compiler_params=_cp(1),
    )(a, b3, *after)


def _input_grad(pairs, name, after=(), carry=(), per=1):
    s = pairs[0][0].shape[1]
    d = pairs[0][1].shape[1]
    tn = min(1024, d)
    npair = len(pairs)
    nc = len(carry)
    c_in, c_out, c_shape, c_sems = _carry_specs(carry)

    def body(*refs):
        ops = refs[:2 * npair]
        rest = refs[2 * npair + len(after):]
        o_ref, acc_ref = rest[nc], rest[-1]
        nh, j = pl.program_id(0), pl.program_id(1)
        last_j = N_DEV // per - 1
        _carry_run((nh == 0) & (j == 0), (nh == d // tn - 1) & (j == last_j),
                   rest[:nc], rest[nc + 1:2 * nc + 1], rest[2 * nc + 1:-1])

        @pl.when(j == 0)
        def _():
            acc_ref[...] = jnp.zeros_like(acc_ref)

        for rs in _chunks(s, 1024):
            part = None
            for q in range(npair):
                for e in range(per):
                    term = _dot(ops[2 * q][e, rs, :], ops[2 * q + 1][e], NT)
                    part = term if part is None else part + term
            acc_ref[rs, :] += part

        @pl.when(j == last_j)
        def _():
            o_ref[...] = acc_ref[...].astype(BF16)

    in_specs, args = [], []
    for a3, w3 in pairs:
        k = a3.shape[2]
        in_specs += [pl.BlockSpec((per, s, k), lambda n, j: (j, 0, 0)),
                     pl.BlockSpec((per, tn, k), lambda n, j: (j, n, 0))]
        args += [a3, w3]
    outs = pl.pallas_call(
        body, name=name, grid=(d // tn, N_DEV // per),
        in_specs=in_specs + _after_specs(after) + c_in,
        out_specs=[pl.BlockSpec((s, tn), lambda n, j: (0, n))] + c_out,
        out_shape=[jax.ShapeDtypeStruct((s, d), BF16)] + c_shape,
        scratch_shapes=c_sems + [pltpu.VMEM((s, tn), F32)],
        compiler_params=_cp_carry(2, carry),
    )(*args, *after, *carry)
    return (outs[0], list(outs[1:])) if nc else outs[0]


def _rms_bwd(dh, xres, g, dres, name, with_bf16=True):
    s, d = xres.shape
    tm = min(256, s)

    def body(dh_ref, x_ref, g_ref, dres_ref, dx_ref, *rest):
        dg_ref = rest[-1]
        @pl.when(pl.program_id(0) == 0)
        def _():
            dg_ref[...] = jnp.zeros_like(dg_ref)

        xv = x_ref[...]
        dh_v = dh_ref[...].astype(F32)
        r = lax.rsqrt(jnp.mean(xv * xv, axis=-1, keepdims=True) + EPS)
        nrm = xv * r
        dg_ref[...] += jnp.sum(dh_v * nrm, axis=0, keepdims=True)
        dn = dh_v * g_ref[...]
        dx = dres_ref[...].astype(F32) + r * (dn - nrm * jnp.mean(dn * nrm, axis=-1, keepdims=True))
        dx_ref[...] = dx
        if with_bf16:
            rest[0][...] = dx.astype(BF16)

    row = pl.BlockSpec((tm, d), lambda i: (i, 0))
    vec = pl.BlockSpec((1, d), lambda i: (0, 0))
    copies = [jax.ShapeDtypeStruct((s, d), BF16)] if with_bf16 else []
    outs = pl.pallas_call(
        body, name=name, grid=(s // tm,),
        in_specs=[row, row, vec, row],
        out_specs=[row] + [row] * len(copies) + [vec],
        out_shape=[jax.ShapeDtypeStruct((s, d), F32)] + copies + [jax.ShapeDtypeStruct((1, d), F32)],
        compiler_params=_cp(1),
    )(dh, xres, g, dres)
    return (outs[0], outs[1], outs[2]) if with_bf16 else (outs[0], None, outs[1])


def _wgrad_full(a, b, name, after=(), carry=()):
    s, k = a.shape
    n = b.shape[1]
    tk = min(512, k)
    nc = len(carry)
    c_in, c_out, c_shape, c_sems = _carry_specs(carry)

    def body(a_ref, b_ref, *rest):
        rest = rest[len(after):]
        o_ref = rest[nc]
        j = pl.program_id(0)
        _carry_run(j == 0, j == k // tk - 1, rest[:nc], rest[nc + 1:2 * nc + 1], rest[2 * nc + 1:])
        o_ref[...] = _dot(a_ref[...], b_ref[...], TN).astype(BF16)

    outs = pl.pallas_call(
        body, name=name, grid=(k // tk,),
        in_specs=[pl.BlockSpec((s, tk), lambda j: (0, j)),
                  pl.BlockSpec((s, n), lambda j: (0, 0))] + _after_specs(after) + c_in,
        out_specs=[pl.BlockSpec((tk, n), lambda j: (j, 0))] + c_out,
        out_shape=[jax.ShapeDtypeStruct((k, n), BF16)] + c_shape,
        scratch_shapes=c_sems,
        compiler_params=_cp_carry(1, carry),
    )(a, b, *after, *carry)
    return (outs[0], list(outs[1:])) if nc else outs[0]


def _wgrad_pool(p, dyb, n_groups):
    s, sw = p.shape
    d = dyb.shape[1]
    gw, go = sw // n_groups, d // n_groups
    ts = min(512, s)
    ns = s // ts

    def body(a_ref, b_ref, o_ref, acc_ref):
        i = pl.program_id(1)

        @pl.when(i == 0)
        def _():
            acc_ref[...] = jnp.zeros_like(acc_ref)

        acc_ref[...] += _dot(a_ref[...], b_ref[...], TN)

        @pl.when(i == ns - 1)
        def _():
            o_ref[...] = acc_ref[...].astype(BF16)

    return pl.pallas_call(
        body, name="wgrad_pool", grid=(n_groups, ns),
        in_specs=[pl.BlockSpec((ts, gw), lambda g, i: (i, g)),
                  pl.BlockSpec((ts, go), lambda g, i: (i, g))],
        out_specs=pl.BlockSpec((None, gw, go), lambda g, i: (g, 0, 0)),
        out_shape=jax.ShapeDtypeStruct((n_groups, gw, go), BF16),
        scratch_shapes=[pltpu.VMEM((gw, go), F32)],
        compiler_params=_cp(2),
    )(p, dyb)


def _wo_bwd(dx1b, wo, factors, sw, after=()):
    s, d = dx1b.shape
    tn = d // N_DEV
    nq = sw // tn

    def body(dx_ref, wo_ref, fya_ref, fyb_ref, fga_ref, fgb_ref, fsc_ref, *rest):
        dya_ref, dyb_ref, dp_ref, dbg_ref, dsc_ref, dm_ref = rest[len(after):]
        dbg_ref[...] = jnp.zeros_like(dbg_ref)
        dsc_ref[...] = jnp.zeros_like(dsc_ref)
        for rs in _chunks(s, 1024):
            dm_ref[rs, :] = _dot(dx_ref[rs, :], wo_ref[...], NT)
        for rs in _chunks(s, 256):
            dm = dm_ref[rs, :]
            dya_ref[rs, :] = (dm * fya_ref[rs, :].astype(F32)).astype(BF16)
            dyb_ref[rs, :] = (dm * fyb_ref[rs, :].astype(F32)).astype(BF16)
            dsc_ref[...] += jnp.sum(dm * fsc_ref[rs, :].astype(F32), axis=0, keepdims=True)
            dga = dm * fga_ref[rs, :].astype(F32)
            dgb = dm * fgb_ref[rs, :].astype(F32)
            dp_ref[0, rs, :] = dga.astype(BF16)
            dp_ref[1, rs, :] = dgb.astype(BF16)
            dbg_ref[0:1, :] += jnp.sum(dga, axis=0, keepdims=True)
            dbg_ref[1:2, :] += jnp.sum(dgb, axis=0, keepdims=True)

    col = pl.BlockSpec((s, tn), lambda j: (0, j))
    out = jax.ShapeDtypeStruct((s, d), BF16)
    return pl.pallas_call(
        body, name="wo_bwd", grid=(N_DEV,),
        in_specs=[pl.BlockSpec((s, d), lambda j: (0, 0)),
                  pl.BlockSpec((tn, d), lambda j: (j, 0))] + [col] * 5 + _after_specs(after),
        out_specs=[col, col,
                   pl.BlockSpec((2, None, s, tn), lambda j: (1, j // nq, 0, j % nq)),
                   pl.BlockSpec((2, tn), lambda j: (0, j)),
                   pl.BlockSpec((1, tn), lambda j: (0, j))],
        out_shape=[out, out, jax.ShapeDtypeStruct((4, 2, s, sw), BF16),
                   jax.ShapeDtypeStruct((2, d), F32), jax.ShapeDtypeStruct((1, d), F32)],
        scratch_shapes=[pltpu.VMEM((s, tn), F32)],
        compiler_params=_cp(1),
    )(dx1b, wo, *factors, *after)


def _conv_bwd(dproj, dya, wa, proj, conv_w, conv_b):
    s, d = dya.shape
    sw, tn = wa.shape[1], wa.shape[2]
    tc = min(LANES, sw)

    def body(dproj_hbm, dya_ref, wa_ref, ba_ref, ca_ref, va_ref, cw_ref, cb_ref,
             dp_ref, dcw_ref, dcb_ref, dz_ref):
        del dproj_hbm
        for rs in _chunks(s, 512):
            part = _dot(dya_ref[rs, 0:tn], wa_ref[0], NT)
            for j in range(1, N_DEV):
                part = part + _dot(dya_ref[rs, j * tn:(j + 1) * tn], wa_ref[j], NT)
            dz_ref[rs, :] = part
        dz = dz_ref[...]
        ba, ca, va = ba_ref[...], ca_ref[...], va_ref[...]
        cv = ca * va
        cv1, cv2 = _shift_down(cv, 1), _shift_down(cv, 2)
        w0, w1, w2 = cw_ref[0:1, :], cw_ref[1:2, :], cw_ref[2:3, :]
        u = cb_ref[...] + w0 * cv2 + w1 * cv1 + w2 * cv
        du = dz * ba
        dp_ref[0] = (dz * u).astype(BF16)
        dcv = w2 * du + w1 * _shift_up(du, 1) + w0 * _shift_up(du, 2)
        dp_ref[1] = (dcv * va).astype(BF16)
        dp_ref[2] = (dcv * ca).astype(BF16)
        dcw_ref[0:1, :] = jnp.sum(du * cv2, axis=0, keepdims=True)
        dcw_ref[1:2, :] = jnp.sum(du * cv1, axis=0, keepdims=True)
        dcw_ref[2:3, :] = jnp.sum(du * cv, axis=0, keepdims=True)
        dcb_ref[...] = jnp.sum(du, axis=0, keepdims=True)

    def part(k):
        return pl.BlockSpec((None, s, tc), lambda i: (k, 0, i))

    return pl.pallas_call(
        body, name="conv_bwd", grid=(sw // tc,),
        in_specs=[pl.BlockSpec(memory_space=pl.ANY),
                  pl.BlockSpec((s, d), lambda i: (0, 0)),
                  pl.BlockSpec((N_DEV, tc, tn), lambda i: (0, i, 0)),
                  part(0), part(1), part(2),
                  pl.BlockSpec((CONV_K, tc), lambda i: (0, i)), pl.BlockSpec((1, tc), lambda i: (0, i))],
        out_specs=[pl.BlockSpec((3, s, tc), lambda i: (0, 0, i)),
                   pl.BlockSpec((CONV_K, tc), lambda i: (0, i)), pl.BlockSpec((1, tc), lambda i: (0, i))],
        out_shape=[jax.ShapeDtypeStruct(dproj.shape, BF16),
                   jax.ShapeDtypeStruct((CONV_K, sw), F32), jax.ShapeDtypeStruct((1, sw), F32)],
        scratch_shapes=[pltpu.VMEM((s, tc), F32)],
        input_output_aliases={0: 0},
        compiler_params=_cp(1),
    )(dproj, dya, wa, proj, proj, proj, conv_w, conv_b)


def _pool_bwd(dproj, dyb, wpool):
    s, d = dyb.shape
    n_groups, gw, go = wpool.shape

    def body(dproj_hbm, dyb_ref, wp_ref, dp_ref):
        del dproj_hbm
        for gi, window in enumerate(POOL_WINDOWS):
            @pl.when(pl.program_id(0) == gi)
            def _():
                dpool = _dot(dyb_ref[...], wp_ref[...], NT)
                acc, k = dpool / _pool_counts(dpool.shape, window), 1
                while k < window:
                    acc = acc + _shift_up(acc, k)
                    k *= 2
                dp_ref[...] = (acc - dpool).astype(BF16)

    return pl.pallas_call(
        body, name="pool_bwd", grid=(n_groups,),
        in_specs=[pl.BlockSpec(memory_space=pl.ANY),
                  pl.BlockSpec((s, go), lambda g: (0, g)),
                  pl.BlockSpec((None, gw, go), lambda g: (g, 0, 0))],
        out_specs=pl.BlockSpec((None, s, gw), lambda g: (3, 0, g)),
        out_shape=jax.ShapeDtypeStruct(dproj.shape, BF16),
        input_output_aliases={0: 0},
        compiler_params=_cp(1),
    )(dproj, dyb, wpool)


def _rows128(v):
    return v.reshape(-1, LANES)


def kernel(x, norm1_g, w_in, b_gate, conv_w, conv_b, w_a_out, w_pool, pool_scale, w_o, norm2_g, w_ffn_gate, w_ffn_up, w_ffn_down, final_g, loss_target, m_norm1_g, m_w_in, m_b_gate, m_conv_w, m_conv_b, m_w_a_out, m_w_pool, m_pool_scale, m_w_o, m_norm2_g, m_w_ffn_gate, m_w_ffn_up, m_w_ffn_down, m_final_g, v_norm1_g, v_w_in, v_b_gate, v_conv_w, v_conv_b, v_w_a_out, v_w_pool, v_pool_scale, v_w_o, v_norm2_g, v_w_ffn_gate, v_w_ffn_up, v_w_ffn_down, v_final_g):
    s, d = x.shape[1], x.shape[2]
    sw = w_in.shape[2]
    n_groups = w_pool.shape[1]
    gw = w_pool.shape[2]
    go = w_pool.shape[3] * N_DEV
    f8 = w_ffn_gate.shape[2]
    cws = conv_w.shape[2]
    assert sw == conv_w.shape[2] * N_DEV == gw * n_groups and go * n_groups == d and n_groups == len(POOL_WINDOWS)

    xi, yi, ci = _coords()
    me = 4 * xi + 2 * yi + ci
    my_chip = 2 * xi + yi

    x2d = x.reshape(s, d)
    target = loss_target.reshape(s, d)
    final_g2 = final_g.reshape(1, d)
    b_gate2 = b_gate.reshape(2, d)

    big_names = ["w_in", "w_a_out", "w_pool", "w_o", "w_ffn_gate", "w_ffn_up", "w_ffn_down"]
    big_w = [w_in, w_a_out, w_pool, w_o, w_ffn_gate, w_ffn_up, w_ffn_down]
    big_m = [m_w_in, m_w_a_out, m_w_pool, m_w_o, m_w_ffn_gate, m_w_ffn_up, m_w_ffn_down]
    big_v = [v_w_in, v_w_a_out, v_w_pool, v_w_o, v_w_ffn_gate, v_w_ffn_up, v_w_ffn_down]
    shapes2d = [(w.size // w.shape[-1], w.shape[-1]) for w in big_w]
    big_w2 = [w.reshape(sh) for w, sh in zip(big_w, shapes2d)]
    transposed = (4, 5)

    def view2d(t, a):
        t2 = t.reshape(shapes2d[a])
        return t2.T if a in transposed else t2

    def unview(o, a):
        return (o.T if a in transposed else o).reshape(big_w[a].shape)

    sb = [_cast_bf16(w, "cast_" + nm, parts=2 if nm == "w_ffn_down" else 1) for w, nm in zip(big_w2, big_names)]
    win_g, wa_g, wpool_g, wo_g = _allgather_big([b[0] for b in sb[0:4]], "allgather_mixer", COLLECTIVE_GATHER)
    (wg_g,) = _allgather_big(sb[4], "allgather_ffn_gate", COLLECTIVE_GATHER)
    (wu_g,) = _allgather_big(sb[5], "allgather_ffn_up", COLLECTIVE_GATHER)
    wd_parts = [_allgather_big([part], "allgather_ffn_down_%d" % q, COLLECTIVE_GATHER)[0]
                for q, part in enumerate(sb[6])]
    convw_g = _allgather_small(jnp.pad(conv_w.reshape(CONV_K, cws), ((0, 8 - CONV_K), (0, 0))), "allgather_conv_w")
    conv_w_full = convw_g[:, :CONV_K, :].transpose(1, 0, 2).reshape(CONV_K, sw)
    wpool = wpool_g.reshape(N_DEV, n_groups, gw, go // N_DEV).transpose(1, 2, 0, 3).reshape(n_groups, gw, go)
    wo = wo_g.reshape(d, d)

    h = _rms_fwd(x2d, norm1_g)
    proj = _proj_fwd(h, win_g)
    z = _conv_fwd(proj, conv_w_full, conv_b)
    p = _pool_fwd(proj)
    merged, *merge_factors = _merge_fwd(z, wa_g, p, wpool, proj, b_gate2, pool_scale)
    x1, h2 = _wo_fwd(merged, wo, x2d, norm2_g)
    gate = _ffn_gate_fwd(h2, wg_g)
    dadu, dadg, act = _ffn_up_act_fwd(h2, wu_g, gate)
    ffn_parts = [_ffn_down_fwd(act, wd, "ffn_down_fwd_%d" % q) for q, wd in enumerate(wd_parts)]
    dx2b, d_final_g, loss_blk = _loss_bwd(ffn_parts, x1, target, final_g2)

    other_chips = jnp.stack([2 * (1 - xi) + yi, 2 * xi + (1 - yi), 2 * (1 - xi) + (1 - yi)])
    others = jnp.concatenate([other_chips, 2 * other_chips + ci]).astype(jnp.int32)

    def partials(grads, recvs, names):
        if all(g.shape == grads[0].shape for g in grads):
            return list(_chip_partial(others, grads, recvs, "chip_partial_" + names[0]))
        return [_chip_partial(others, [g3], [r], "chip_partial_" + nm)[0] for g3, r, nm in zip(grads, recvs, names)]

    own = jnp.stack([me, my_chip]).astype(jnp.int32)

    def adam(idx, g3s, sibs, chipss):
        wmvs = [(view2d(big_w[a], a), view2d(big_m[a], a), view2d(big_v[a], a)) for a in idx]
        outs = _adam_big(own, wmvs, g3s, sibs, chipss, "adam_" + big_names[idx[0]])
        for a, o4 in zip(idx, outs):
            big_out[a] = [unview(o, a) for o in o4]

    big_out = [None] * len(big_names)
    dg_act, du_act = _ffn_gate_bwd(dx2b, wd_parts, dadg, dadu)
    gw_gate = _wgrad_rows(dg_act, h2, "wgrad_ffn_gate")
    gw_up = _wgrad_rows(du_act, h2, "wgrad_ffn_up")
    gw_down, sib_gu = _wgrad_rows(act, dx2b, "wgrad_ffn_down", carry=[gw_gate, gw_up])
    ps_gu = partials([gw_gate, gw_up], sib_gu, ["w_ffn_gate", "w_ffn_up"])
    chips_gu = _exchange_chips(ps_gu, "rs_chips_ffn_up", COLLECTIVE_CHIPS)
    dh2, sib_down = _input_grad([(dg_act, wg_g), (du_act, wu_g)], "ffn_in_bwd", after=ps_gu, carry=[gw_down])
    ps_down = partials([gw_down], sib_down, ["w_ffn_down"])
    chips_down = _exchange_chips(ps_down, "rs_chips_ffn_down", COLLECTIVE_CHIPS)
    dx1, dx1b, d_norm2_g = _rms_bwd(dh2, x1, norm2_g, dx2b, "rms2_bwd")
    dya, dyb, dproj42, d_b_gate, d_pool_scale = _wo_bwd(dx1b, wo, merge_factors, sw, after=ps_down)
    dproj = dproj42.reshape(N_DEV, s, sw)
    dproj, d_conv_w, d_conv_b = _conv_bwd(dproj, dya, wa_g, proj, conv_w_full, conv_b)
    dproj = _pool_bwd(dproj, dyb, wpool)
    gw_in = _wgrad_cols(h, dproj, "wgrad_in")
    gw_o, sib_in = _wgrad_full(merged, dx1b, "wgrad_o", carry=[gw_in])
    ps_in = partials([gw_in], sib_in, ["w_in"])
    chips_in = _exchange_chips(ps_in, "rs_chips_w_in", COLLECTIVE_CHIPS)
    gw_a = _wgrad_cols(z, dya, "wgrad_a_out", after=ps_in)
    gw_pool = _wgrad_pool(p, dyb, n_groups)
    mix3 = [gw_a,
            gw_pool.reshape(n_groups, gw, N_DEV, go // N_DEV).transpose(2, 0, 1, 3).reshape(N_DEV, n_groups * gw, go // N_DEV),
            gw_o.reshape(N_DEV, d // N_DEV, d)]
    adam([4, 5, 6], [gw_gate, gw_up, gw_down], sib_gu + sib_down, chips_gu + chips_down)
    dh, sib_mix = _input_grad([(dproj, win_g)], "proj_in_bwd", after=[big_out[6][0]], carry=mix3, per=2)
    ps_mix = partials(mix3, sib_mix, ["w_a_out", "w_pool", "w_o"])
    chips_mix = _exchange_chips(ps_mix, "rs_chips_mixer", COLLECTIVE_CHIPS)
    grad_x, _, d_norm1_g = _rms_bwd(dh, x2d, norm1_g, dx1, "rms1_bwd", with_bf16=False)

    small_parts = [d_norm1_g, d_b_gate, d_conv_w, d_conv_b, d_pool_scale, d_norm2_g, d_final_g, loss_blk]
    rows = [v.size // LANES for v in small_parts]
    row0 = [sum(rows[:k]) for k in range(len(rows))]
    packed = jnp.concatenate([_rows128(v) for v in small_parts], axis=0)
    gathered = _allgather_small_async(packed, "allgather_small_grads")

    adam([0], [gw_in], sib_in, chips_in)
    for k in range(3):
        adam([1 + k], [mix3[k]], [sib_mix[k]], [chips_mix[k]])

    small_names = ["norm1_g", "b_gate", "conv_b", "pool_scale", "norm2_g", "final_g", "conv_w"]
    small_w = [norm1_g, b_gate, conv_b, pool_scale, norm2_g, final_g]
    small_m = [m_norm1_g, m_b_gate, m_conv_b, m_pool_scale, m_norm2_g, m_final_g]
    small_v = [v_norm1_g, v_b_gate, v_conv_b, v_pool_scale, v_norm2_g, v_final_g]
    finished = _small_finish(gathered, [tuple(_rows128(t) for t in wmv) for wmv in zip(small_w, small_m, small_v)],
                             [row0[k] for k in (0, 1, 3, 4, 5, 6)], [(row0[2], rows[2]), (row0[7], rows[7])],
                             after=[big_out[3][0]])
    g_convw_full, loss_rows = finished[0], finished[1]
    loss = loss_rows[0, 0]
    small_out = [[t.reshape(w.shape) for t in finished[2 + 4 * k:6 + 4 * k]] for k, w in enumerate(small_w)]
    g_convw = lax.dynamic_slice(g_convw_full.reshape(CONV_K, sw), (0, me * cws), (CONV_K, cws))
    cw_delta, cw_m, cw_v = _adam_small(conv_w.reshape(CONV_K, cws), g_convw,
                                       m_conv_w.reshape(CONV_K, cws), v_conv_w.reshape(CONV_K, cws))
    small_out.append([t.reshape(conv_w.shape) for t in (g_convw, cw_delta, cw_m, cw_v)])

    order = ["norm1_g", "w_in", "b_gate", "conv_w", "conv_b", "w_a_out", "w_pool", "pool_scale", "w_o", "norm2_g",
             "w_ffn_gate", "w_ffn_up", "w_ffn_down", "final_g"]
    per_kind = [{}, {}, {}, {}]
    for a, nm in enumerate(big_names):
        for kind in range(4):
            per_kind[kind][nm] = big_out[a][kind]
    for k, nm in enumerate(small_names):
        for kind in range(4):
            per_kind[kind][nm] = small_out[k][kind]
    result = [loss, grad_x.reshape(x.shape)]
    for kind in range(4):
        result += [per_kind[kind][nm] for nm in order]
    return tuple(result)
```

```python
import jax
import jax.numpy as jnp
from jax import lax
from jax.experimental import pallas as pl
from jax.experimental.pallas import tpu as pltpu
from jax.experimental.pallas import tpu_sc as plsc

F32 = jnp.float32
BF16 = jnp.bfloat16
MESH = pl.DeviceIdType.MESH

N_DEV = 8
EPS = 1e-6
CONV_K = 3
POOL_WINDOWS = (2, 4, 8, 16)
ADAM_LR = 0.001
ADAM_B1 = 0.9
ADAM_B2 = 0.999
ADAM_EPS = 1e-08
ADAM_WD = 0.01
ADAM_STEP = 10

V7X_VMEM_LIMIT_BYTES = 56 * 1024 * 1024
LANES = 128

COLLECTIVE_GATHER = 1
COLLECTIVE_SIBLING = 2
COLLECTIVE_CHIPS = 3
COLLECTIVE_ALL = 4
SEQUENCER_COST_BYTES = 4 * 10**9

NN = ((1,), (0,))
NT = ((1,), (1,))
TN = ((0,), (0,))


def _dot(a, b, dims):
    return lax.dot_general(a, b, (dims, ((), ())), preferred_element_type=F32)


def _cp(n_axes):
    return pltpu.CompilerParams(dimension_semantics=("arbitrary",) * n_axes,
                                vmem_limit_bytes=V7X_VMEM_LIMIT_BYTES)


def _row_tile(rows, bytes_per_row, cap_bytes):
    best = None
    for t in range(16, rows + 1, 16):
        if rows % t == 0 and t * bytes_per_row <= cap_bytes:
            best = t
    return best if best is not None else rows


def _chunks(total, size):
    size = min(size, total)
    assert total % size == 0
    return [slice(r, r + size) for r in range(0, total, size)]


def _after_specs(after):
    return [pl.BlockSpec(memory_space=pl.ANY)] * len(after)


def _shift_down(v, k):
    row = lax.broadcasted_iota(jnp.int32, v.shape, 0)
    return jnp.where(row >= k, pltpu.roll(v, k, 0), 0.0)


def _shift_up(v, k):
    n = v.shape[0]
    row = lax.broadcasted_iota(jnp.int32, v.shape, 0)
    return jnp.where(row < n - k, pltpu.roll(v, n - k, 0), 0.0)


def _sigmoid(v):
    return jax.nn.sigmoid(v)


def _cast_bf16(w2d, name, parts=1):
    rows, cols = w2d.shape
    tr = _row_tile(rows, cols * 4, 2 << 20)
    pc = cols // parts

    def body(i_ref, *o_refs):
        for q, o_ref in enumerate(o_refs):
            o_ref[...] = i_ref[:, q * pc:(q + 1) * pc].astype(BF16)

    return pl.pallas_call(
        body, name=name, grid=(rows // tr,),
        in_specs=[pl.BlockSpec((tr, cols), lambda i: (i, 0))],
        out_specs=[pl.BlockSpec((tr, pc), lambda i: (i, 0))] * parts,
        out_shape=[jax.ShapeDtypeStruct((rows, pc), BF16)] * parts,
        compiler_params=_cp(1),
    )(w2d)


def _rms_fwd(x2d, g):
    s, d = x2d.shape
    tm = min(256, s)

    def body(x_ref, g_ref, h_ref):
        xv = x_ref[...]
        r = lax.rsqrt(jnp.mean(xv * xv, axis=-1, keepdims=True) + EPS)
        h_ref[...] = (xv * r * g_ref[...]).astype(BF16)

    return pl.pallas_call(
        body, name="rms1_fwd", grid=(s // tm,),
        in_specs=[pl.BlockSpec((tm, d), lambda i: (i, 0)), pl.BlockSpec((1, d), lambda i: (0, 0))],
        out_specs=pl.BlockSpec((tm, d), lambda i: (i, 0)),
        out_shape=jax.ShapeDtypeStruct((s, d), BF16),
        compiler_params=_cp(1),
    )(x2d, g)


def _coords():
    return lax.axis_index("x"), lax.axis_index("y"), lax.axis_index("c")


def _slot(p):
    return 4 * p[0] + 2 * p[1] + p[2]


def _handshake(peers):
    barrier = pltpu.get_barrier_semaphore()
    for peer in peers:
        pl.semaphore_signal(barrier, inc=1, device_id=peer, device_id_type=MESH)
    pl.semaphore_wait(barrier, len(peers))


def _sequencer_call(body, out_type, scratch_types, name, collective_id):
    return pl.kernel(
        body, out_type=out_type, name=name,
        mesh=plsc.ScalarSubcoreMesh(axis_name="seq", num_cores=1),
        scratch_types=scratch_types,
        cost_estimate=pl.CostEstimate(flops=0, transcendentals=0, bytes_accessed=SEQUENCER_COST_BYTES),
        compiler_params=pltpu.CompilerParams(collective_id=collective_id))


def _allgather_big(shards, name, collective_id, after=()):
    n = len(shards)

    def body(*refs):
        ins, outs = refs[:n], refs[n + len(after):2 * n + len(after)]
        send_sems, recv_sems, local_sems = refs[2 * n + len(after):]
        x, y, c = _coords()
        me, sibling = (x, y, c), (x, y, 1 - c)
        x_nbr, y_nbr, diag = (1 - x, y), (x, 1 - y), (1 - x, 1 - y)
        relay_from = (x + (1 - c) * (1 - 2 * x), y + c * (1 - 2 * y))
        relay_to = (x + c * (1 - 2 * x), y + (1 - c) * (1 - 2 * y))
        _handshake([sibling, (*x_nbr, c), (*y_nbr, c)])

        def copy(a, k, block, to, src=None):
            dst = outs[a].at[_slot(block)]
            return pltpu.make_async_remote_copy(
                src_ref=dst if src is None else src, dst_ref=dst,
                send_sem=send_sems.at[a, k], recv_sem=recv_sems.at[a, k],
                device_id=to, device_id_type=MESH)

        mine, sends = [], []
        for a in range(n):
            cp = pltpu.make_async_copy(ins[a], outs[a].at[_slot(me)], local_sems.at[a])
            cp.start()
            mine.append(cp)
            first = [copy(a, 0, me, sibling, src=ins[a]),
                     copy(a, 1, me, (*x_nbr, c), src=ins[a]),
                     copy(a, 2, me, (*y_nbr, c), src=ins[a])]
            for cp in first:
                cp.start()
            sends += first
        for a in range(n):
            copy(a, 1 + c, (*relay_from, c), me).wait_recv()
            passed = [copy(a, 3, (*relay_from, c), (*relay_to, c)), copy(a, 4 + c, (*relay_from, c), sibling)]
            for cp in passed:
                cp.start()
            copy(a, 2 - c, (*relay_to, c), me).wait_recv()
            cp = copy(a, 5 - c, (*relay_to, c), sibling)
            cp.start()
            passed.append(cp)
            copy(a, 3, (*diag, c), me).wait_recv()
            cp = copy(a, 6, (*diag, c), sibling)
            cp.start()
            sends += passed + [cp]
        for a in range(n):
            copy(a, 0, sibling, me).wait_recv()
            copy(a, 4, (*x_nbr, 1 - c), me).wait_recv()
            copy(a, 5, (*y_nbr, 1 - c), me).wait_recv()
            copy(a, 6, (*diag, 1 - c), me).wait_recv()
        for cp in sends:
            cp.wait_send()
        for cp in mine:
            cp.wait()

    return _sequencer_call(
        body, [jax.ShapeDtypeStruct((N_DEV,) + s.shape, s.dtype) for s in shards],
        [pltpu.SemaphoreType.DMA((n, 7)), pltpu.SemaphoreType.DMA((n, 7)), pltpu.SemaphoreType.DMA((n,))],
        name, collective_id)(*shards, *after)


def _sibling_copies(ins, recvs, send_sems, recv_sems):
    x, y, c = _coords()
    return [pltpu.make_async_remote_copy(
        src_ref=ins[a].at[2 * q + (1 - c)], dst_ref=recvs[a].at[q],
        send_sem=send_sems.at[a, q], recv_sem=recv_sems.at[a, q],
        device_id=(x, y, 1 - c), device_id_type=MESH) for a in range(len(ins)) for q in range(4)]


def _carry_specs(carry):
    any_spec = pl.BlockSpec(memory_space=pl.ANY)
    n = len(carry)
    sems = [pltpu.SemaphoreType.DMA((n, 4)), pltpu.SemaphoreType.DMA((n, 4))] if n else []
    return ([any_spec] * n, [any_spec] * n,
            [jax.ShapeDtypeStruct((4,) + g.shape[1:], g.dtype) for g in carry], sems)


def _carry_run(first, last, ins, recvs, sems):
    if not ins:
        return

    @pl.when(first)
    def _():
        x, y, c = _coords()
        _handshake([(x, y, 1 - c)])
        for cp in _sibling_copies(ins, recvs, *sems):
            cp.start()

    @pl.when(last)
    def _():
        copies = _sibling_copies(ins, recvs, *sems)
        for cp in copies:
            cp.wait_recv()
        for cp in copies:
            cp.wait_send()


def _cp_carry(n_axes, carry):
    if not carry:
        return _cp(n_axes)
    return pltpu.CompilerParams(dimension_semantics=("arbitrary",) * n_axes, vmem_limit_bytes=V7X_VMEM_LIMIT_BYTES,
                                collective_id=COLLECTIVE_SIBLING)


def _exchange_chips(psums, name, collective_id):
    n = len(psums)

    def body(*refs):
        ins, outs = refs[:n], refs[n:2 * n]
        send_sems, recv_sems = refs[2 * n:]
        x, y, c = _coords()
        chips = [(1 - x, y), (x, 1 - y), (1 - x, 1 - y)]
        _handshake([(*chip, c) for chip in chips])
        copies = []
        for a in range(n):
            for j, chip in enumerate(chips):
                cp = pltpu.make_async_remote_copy(
                    src_ref=ins[a].at[2 * chip[0] + chip[1]], dst_ref=outs[a].at[j],
                    send_sem=send_sems.at[a, j], recv_sem=recv_sems.at[a, j],
                    device_id=(*chip, c), device_id_type=MESH)
                cp.start()
                copies.append(cp)
        for cp in copies:
            cp.wait_recv()
        for cp in copies:
            cp.wait_send()

    return _sequencer_call(
        body, [jax.ShapeDtypeStruct((3,) + p.shape[1:], p.dtype) for p in psums],
        [pltpu.SemaphoreType.DMA((n, 3)), pltpu.SemaphoreType.DMA((n, 3))],
        name, collective_id)(*psums)


def _allgather_small(v2d, name):
    rows, cols = v2d.shape

    def body(v_ref, out_ref, send_sems, recv_sems):
        x, y, c = _coords()
        me = (x, y, c)
        out_ref[_slot(me)] = v_ref[...]
        peers = []
        for k in range(1, N_DEV):
            fx, fy, fc = (k >> 2) & 1, (k >> 1) & 1, k & 1
            peers.append(((1 - x) if fx else x, (1 - y) if fy else y, (1 - c) if fc else c))
        sends = []
        for k, peer in enumerate(peers):
            cp = pltpu.make_async_remote_copy(
                src_ref=v_ref, dst_ref=out_ref.at[_slot(me)],
                send_sem=send_sems.at[k], recv_sem=recv_sems.at[k],
                device_id=peer, device_id_type=MESH)
            cp.start()
            sends.append(cp)
        for k, peer in enumerate(peers):
            pltpu.make_async_remote_copy(
                src_ref=v_ref, dst_ref=out_ref.at[_slot(peer)],
                send_sem=send_sems.at[k], recv_sem=recv_sems.at[k],
                device_id=peer, device_id_type=MESH).wait_recv()
        for cp in sends:
            cp.wait_send()

    vmem = pl.BlockSpec(memory_space=pltpu.VMEM)
    return pl.pallas_call(
        body, name=name, in_specs=[vmem], out_specs=vmem,
        out_shape=jax.ShapeDtypeStruct((N_DEV, rows, cols), v2d.dtype),
        scratch_shapes=[pltpu.SemaphoreType.DMA((N_DEV - 1,)), pltpu.SemaphoreType.DMA((N_DEV - 1,))],
    )(v2d)


def _allgather_small_async(v2d, name, after=()):
    rows, cols = v2d.shape

    def body(v_ref, *refs):
        out_ref, send_sems, recv_sems, local_sem = refs[len(after):]
        x, y, c = _coords()
        me = (x, y, c)
        peers = []
        for k in range(1, N_DEV):
            fx, fy, fc = (k >> 2) & 1, (k >> 1) & 1, k & 1
            peers.append(((1 - x) if fx else x, (1 - y) if fy else y, (1 - c) if fc else c))
        _handshake(peers)

        def copy(k, block, to):
            return pltpu.make_async_remote_copy(
                src_ref=v_ref, dst_ref=out_ref.at[_slot(block)],
                send_sem=send_sems.at[k], recv_sem=recv_sems.at[k],
                device_id=to, device_id_type=MESH)

        mine = pltpu.make_async_copy(v_ref, out_ref.at[_slot(me)], local_sem.at[0])
        mine.start()
        sends = [copy(k, me, peer) for k, peer in enumerate(peers)]
        for cp in sends:
            cp.start()
        for k, peer in enumerate(peers):
            copy(k, peer, peer).wait_recv()
        for cp in sends:
            cp.wait_send()
        mine.wait()

    return _sequencer_call(
        body, [jax.ShapeDtypeStruct((N_DEV, rows, cols), v2d.dtype)],
        [pltpu.SemaphoreType.DMA((N_DEV - 1,)), pltpu.SemaphoreType.DMA((N_DEV - 1,)), pltpu.SemaphoreType.DMA((1,))],
        name, COLLECTIVE_ALL)(v2d, *after)[0]


def _chip_partial(others, grads, recvs, name):
    n = len(grads)
    _, rows, cols = grads[0].shape
    tr = _row_tile(rows, cols * 2, (2 << 20) // n)

    def body(others_ref, *refs):
        for a in range(n):
            refs[2 * n + a][...] = (refs[a][...].astype(F32) + refs[n + a][...].astype(F32)).astype(BF16)

    return pl.pallas_call(
        body, name=name,
        grid_spec=pltpu.PrefetchScalarGridSpec(
            num_scalar_prefetch=1, grid=(3, rows // tr),
            in_specs=[pl.BlockSpec((None, tr, cols), lambda k, i, o: (o[3 + k], i, 0))] * n
            + [pl.BlockSpec((None, tr, cols), lambda k, i, o: (o[k], i, 0))] * n,
            out_specs=[pl.BlockSpec((None, tr, cols), lambda k, i, o: (o[k], i, 0))] * n),
        out_shape=[jax.ShapeDtypeStruct((4, rows, cols), BF16)] * n,
        compiler_params=_cp(2),
    )(others, *grads, *recvs)


def _adam_math(w, g, m, v):
    m = ADAM_B1 * m + (1.0 - ADAM_B1) * g
    v = ADAM_B2 * v + (1.0 - ADAM_B2) * (g * g)
    m_hat = m / (1.0 - ADAM_B1 ** ADAM_STEP)
    v_hat = v / (1.0 - ADAM_B2 ** ADAM_STEP)
    delta = -ADAM_LR * (m_hat / (jnp.sqrt(v_hat) + ADAM_EPS) + ADAM_WD * w)
    return delta, m, v


def _adam_big(own, wmvs, g3s, recv_sibs, recv_chipss, name):
    n = len(wmvs)
    rows, cols = wmvs[0][0].shape
    tr = _row_tile(rows, cols * 4, (2 << 20) // n)

    def body(own_ref, *refs):
        ins, outs = refs[:6 * n], refs[6 * n:]
        for a in range(n):
            w_ref, m_ref, v_ref, g_ref, rs_ref, rc_ref = ins[6 * a:6 * a + 6]
            g = g_ref[...].astype(F32) + rs_ref[...].astype(F32)
            g = g + rc_ref[0].astype(F32)
            g = g + rc_ref[1].astype(F32)
            g = g + rc_ref[2].astype(F32)
            delta, m_new, v_new = _adam_math(w_ref[...], g, m_ref[...], v_ref[...])
            outs[4 * a][...] = g
            outs[4 * a + 1][...] = delta
            outs[4 * a + 2][...] = m_new
            outs[4 * a + 3][...] = v_new

    blk = pl.BlockSpec((tr, cols), lambda i, o: (i, 0))
    per_shard = [blk, blk, blk,
                 pl.BlockSpec((None, tr, cols), lambda i, o: (o[0], i, 0)),
                 pl.BlockSpec((None, tr, cols), lambda i, o: (o[1], i, 0)),
                 pl.BlockSpec((3, tr, cols), lambda i, o: (0, i, 0))]
    out = jax.ShapeDtypeStruct((rows, cols), F32)
    args = [t for a in range(n) for t in (*wmvs[a], g3s[a], recv_sibs[a], recv_chipss[a])]
    outs = pl.pallas_call(
        body, name=name,
        grid_spec=pltpu.PrefetchScalarGridSpec(
            num_scalar_prefetch=1, grid=(rows // tr,),
            in_specs=per_shard * n, out_specs=[blk] * (4 * n)),
        out_shape=[out] * (4 * n),
        compiler_params=_cp(1),
    )(own, *args)
    return [outs[4 * a:4 * a + 4] for a in range(n)]


def _small_finish(gathered, params, row_offs, extra_rows, after=()):
    n = len(params)

    def body(g_ref, *refs):
        ins, outs = refs[:3 * n], refs[3 * n + len(after):]
        total = g_ref[0]
        for k in range(1, N_DEV):
            total = total + g_ref[k]
        for e, (r0, nr) in enumerate(extra_rows):
            outs[e][...] = total[r0:r0 + nr, :]
        for p in range(n):
            w_ref, m_ref, v_ref = ins[3 * p:3 * p + 3]
            g_out, d_out, m_out, v_out = outs[len(extra_rows) + 4 * p:len(extra_rows) + 4 * p + 4]
            g = total[row_offs[p]:row_offs[p] + w_ref.shape[0], :]
            delta, m_new, v_new = _adam_math(w_ref[...], g, m_ref[...], v_ref[...])
            g_out[...] = g
            d_out[...] = delta
            m_out[...] = m_new
            v_out[...] = v_new

    vmem = pl.BlockSpec(memory_space=pltpu.VMEM)
    out_shape = [jax.ShapeDtypeStruct((nr, LANES), F32) for _, nr in extra_rows]
    for w, _, _ in params:
        out_shape += [jax.ShapeDtypeStruct(w.shape, F32)] * 4
    flat = [t for wmv in params for t in wmv]
    return pl.pallas_call(body, name="small_finish", in_specs=[vmem] * (1 + len(flat)) + _after_specs(after),
                          out_specs=[vmem] * len(out_shape), out_shape=out_shape)(gathered, *flat, *after)


def _adam_small(w, g, m, v):
    def body(w_ref, g_ref, m_ref, v_ref, do_ref, mo_ref, vo_ref):
        delta, m_new, v_new = _adam_math(w_ref[...], g_ref[...], m_ref[...], v_ref[...])
        do_ref[...] = delta
        mo_ref[...] = m_new
        vo_ref[...] = v_new

    vmem = pl.BlockSpec(memory_space=pltpu.VMEM)
    out = jax.ShapeDtypeStruct(w.shape, F32)
    return pl.pallas_call(body, name="adam_small", in_specs=[vmem] * 4, out_specs=[vmem] * 3,
                          out_shape=[out, out, out])(w, g, m, v)


def _proj_fwd(h, win_g):
    s, d = h.shape
    sw = win_g.shape[2]
    tn = min(512, sw)
    nh = sw // tn

    def body(h_ref, w_ref, o_ref):
        for rs in _chunks(s, 512):
            o_ref[rs, :] = _dot(h_ref[rs, :], w_ref[...], NN)

    return pl.pallas_call(
        body, name="proj_fwd", grid=(N_DEV * nh,),
        in_specs=[pl.BlockSpec((s, d), lambda j: (0, 0)),
                  pl.BlockSpec((None, d, tn), lambda j: (j // nh, 0, j % nh))],
        out_specs=pl.BlockSpec((None, s, tn), lambda j: (j // nh, 0, j % nh)),
        out_shape=jax.ShapeDtypeStruct((N_DEV, s, sw), F32),
        compiler_params=_cp(1),
    )(h, win_g)


def _conv_fwd(proj, conv_w, conv_b):
    _, s, sw = proj.shape
    tc = min(LANES, sw)

    def body(ba_ref, ca_ref, va_ref, cw_ref, cb_ref, z_ref):
        cv = ca_ref[...] * va_ref[...]
        u = (cb_ref[...] + cw_ref[0:1, :] * _shift_down(cv, 2) + cw_ref[1:2, :] * _shift_down(cv, 1)
             + cw_ref[2:3, :] * cv)
        z_ref[...] = (ba_ref[...] * u).astype(BF16)

    def part(k):
        return pl.BlockSpec((None, s, tc), lambda i: (k, 0, i))

    return pl.pallas_call(
        body, name="conv_fwd", grid=(sw // tc,),
        in_specs=[part(0), part(1), part(2),
                  pl.BlockSpec((CONV_K, tc), lambda i: (0, i)), pl.BlockSpec((1, tc), lambda i: (0, i))],
        out_specs=pl.BlockSpec((s, tc), lambda i: (0, i)),
        out_shape=jax.ShapeDtypeStruct((s, sw), BF16),
        compiler_params=_cp(1),
    )(proj, proj, proj, conv_w, conv_b)


def _pool_counts(shape, window):
    t = lax.broadcasted_iota(jnp.int32, shape, 0)
    return jnp.minimum(t + 1, window).astype(F32)


def _pool_fwd(proj):
    _, s, sw = proj.shape
    gw = sw // len(POOL_WINDOWS)

    def body(v_ref, p_ref):
        for gi, window in enumerate(POOL_WINDOWS):
            @pl.when(pl.program_id(0) == gi)
            def _():
                v = v_ref[...]
                acc, k = v, 1
                while k < window:
                    acc = acc + _shift_down(acc, k)
                    k *= 2
                p_ref[...] = (acc / _pool_counts(v.shape, window) - v).astype(BF16)

    return pl.pallas_call(
        body, name="pool_fwd", grid=(len(POOL_WINDOWS),),
        in_specs=[pl.BlockSpec((None, s, gw), lambda g: (3, 0, g))],
        out_specs=pl.BlockSpec((s, gw), lambda g: (0, g)),
        out_shape=jax.ShapeDtypeStruct((s, sw), BF16),
        compiler_params=_cp(1),
    )(proj)


def _merge_fwd(z, wa, p, wpool, proj, b_gate2, pool_scale):
    s, sw = z.shape
    tn = wa.shape[2]
    d = tn * N_DEV
    gw = sw // len(POOL_WINDOWS)
    nq = sw // tn

    def body(z_ref, wa_ref, p_ref, wp_ref, ga_ref, gb_ref, bg_ref, sc_ref,
             m_ref, dya_ref, dyb_ref, dga_ref, dgb_ref, dsc_ref):
        for rs in _chunks(s, 512):
            ya = _dot(z_ref[rs, :], wa_ref[...], NN)
            yb = _dot(p_ref[rs, :], wp_ref[...], NN)
            sa = _sigmoid(ga_ref[rs, :] + bg_ref[0:1, :])
            sb = _sigmoid(gb_ref[rs, :] + bg_ref[1:2, :])
            sc = sc_ref[...]
            sb_yb = sb * yb
            m_ref[rs, :] = (sa * ya + sb_yb * sc).astype(BF16)
            dya_ref[rs, :] = sa.astype(BF16)
            dyb_ref[rs, :] = (sb * sc).astype(BF16)
            dga_ref[rs, :] = (ya * (sa * (1.0 - sa))).astype(BF16)
            dgb_ref[rs, :] = ((yb * sc) * (sb * (1.0 - sb))).astype(BF16)
            dsc_ref[rs, :] = sb_yb.astype(BF16)

    col = pl.BlockSpec((s, tn), lambda j: (0, j))
    out = jax.ShapeDtypeStruct((s, d), BF16)
    return pl.pallas_call(
        body, name="merge_fwd", grid=(N_DEV,),
        in_specs=[pl.BlockSpec((s, sw), lambda j: (0, 0)),
                  pl.BlockSpec((None, sw, tn), lambda j: (j, 0, 0)),
                  pl.BlockSpec((s, gw), lambda j: (0, j // 2)),
                  pl.BlockSpec((None, gw, tn), lambda j: (j // 2, 0, j % 2)),
                  pl.BlockSpec((None, s, tn), lambda j: (4 + j // nq, 0, j % nq)),
                  pl.BlockSpec((None, s, tn), lambda j: (6 + j // nq, 0, j % nq)),
                  pl.BlockSpec((2, tn), lambda j: (0, j)),
                  pl.BlockSpec((1, tn), lambda j: (0, j))],
        out_specs=[col] * 6,
        out_shape=[out] * 6,
        compiler_params=_cp(1),
    )(z, wa, p, wpool, proj, proj, b_gate2, pool_scale)


def _wo_fwd(merged, wo, x2d, g2):
    s, d = x2d.shape
    tm = min(256, s)

    def body(m_ref, wo_ref, x_ref, g_ref, x1_ref, h2_ref):
        x1 = x_ref[...] + _dot(m_ref[...], wo_ref[...], NN)
        x1_ref[...] = x1
        r = lax.rsqrt(jnp.mean(x1 * x1, axis=-1, keepdims=True) + EPS)
        h2_ref[...] = (x1 * r * g_ref[...]).astype(BF16)

    row = pl.BlockSpec((tm, d), lambda i: (i, 0))
    return pl.pallas_call(
        body, name="wo_fwd", grid=(s // tm,),
        in_specs=[row, pl.BlockSpec((d, d), lambda i: (0, 0)), row, pl.BlockSpec((1, d), lambda i: (0, 0))],
        out_specs=[row, row],
        out_shape=[jax.ShapeDtypeStruct((s, d), F32), jax.ShapeDtypeStruct((s, d), BF16)],
        compiler_params=_cp(1),
    )(merged, wo, x2d, g2)


def _ffn_gate_fwd(h2, wg_g):
    s, d = h2.shape
    f8 = wg_g.shape[2]
    th = min(1024, s)

    def body(h_ref, wg_ref, g_ref):
        i = pl.program_id(1)
        for rs in _chunks(th, 512):
            rows = pl.ds(pl.multiple_of(i * th + rs.start, rs.stop - rs.start), rs.stop - rs.start)
            g_ref[rs, :] = _dot(h_ref[rows, :], wg_ref[...], NN).astype(BF16)

    return pl.pallas_call(
        body, name="ffn_gate_fwd", grid=(N_DEV, s // th),
        in_specs=[pl.BlockSpec((s, d), lambda j, i: (0, 0)), pl.BlockSpec((None, d, f8), lambda j, i: (j, 0, 0))],
        out_specs=pl.BlockSpec((None, th, f8), lambda j, i: (j, i, 0)),
        out_shape=jax.ShapeDtypeStruct((N_DEV, s, f8), BF16),
        compiler_params=_cp(2),
    )(h2, wg_g)


def _ffn_up_act_fwd(h2, wu_g, gate):
    s, d = h2.shape
    f8 = wu_g.shape[2]
    th = min(1024, s)

    def body(h_ref, wu_ref, g_ref, dadu_ref, dadg_ref, a_ref, u_ref):
        i = pl.program_id(1)
        chunks = _chunks(th, 256)

        def matmul(rs):
            rows = pl.ds(pl.multiple_of(i * th + rs.start, rs.stop - rs.start), rs.stop - rs.start)
            u_ref[rs, :] = _dot(h_ref[rows, :], wu_ref[...], NN)

        matmul(chunks[0])
        for k, rs in enumerate(chunks):
            if k + 1 < len(chunks):
                matmul(chunks[k + 1])
            g = g_ref[rs, :].astype(F32)
            u = u_ref[rs, :]
            sg = _sigmoid(g)
            silu = g * sg
            dadu_ref[rs, :] = silu.astype(BF16)
            dadg_ref[rs, :] = (u * (sg * (1.0 + g * (1.0 - sg)))).astype(BF16)
            a_ref[rs, :] = (silu * u).astype(BF16)

    wspec = pl.BlockSpec((None, d, f8), lambda j, i: (j, 0, 0))
    ospec = pl.BlockSpec((None, th, f8), lambda j, i: (j, i, 0))
    out = jax.ShapeDtypeStruct((N_DEV, s, f8), BF16)
    return pl.pallas_call(
        body, name="ffn_up_fwd", grid=(N_DEV, s // th),
        in_specs=[pl.BlockSpec((s, d), lambda j, i: (0, 0)), wspec, ospec],
        out_specs=[ospec, ospec, ospec], out_shape=[out, out, out],
        scratch_shapes=[pltpu.VMEM((th, f8), F32)],
        compiler_params=_cp(2),
    )(h2, wu_g, gate)


def _ffn_down_fwd(act, wd_part, name):
    _, s, f8 = act.shape
    tn = wd_part.shape[2]
    per = 2

    def body(a_ref, wd_ref, o_ref):
        @pl.when(pl.program_id(0) == 0)
        def _():
            o_ref[...] = jnp.zeros_like(o_ref)

        for rs in _chunks(s, 1024):
            part = _dot(a_ref[0, rs, :], wd_ref[0], NN)
            for q in range(1, per):
                part = part + _dot(a_ref[q, rs, :], wd_ref[q], NN)
            o_ref[rs, :] += part

    return pl.pallas_call(
        body, name=name, grid=(N_DEV // per,),
        in_specs=[pl.BlockSpec((per, s, f8), lambda j: (j, 0, 0)),
                  pl.BlockSpec((per, f8, tn), lambda j: (j, 0, 0))],
        out_specs=pl.BlockSpec((s, tn), lambda j: (0, 0)),
        out_shape=jax.ShapeDtypeStruct((s, tn), F32),
        compiler_params=_cp(1),
    )(act, wd_part)


def _loss_bwd(ffn_parts, x1, target, final_g):
    s, d = x1.shape
    tm = min(256, s)
    nparts = len(ffn_parts)

    def body(*refs):
        f_refs = refs[:nparts]
        x1_ref, t_ref, gf_ref, dxb_ref, dgf_ref, loss_ref = refs[nparts:]

        @pl.when(pl.program_id(0) == 0)
        def _():
            dgf_ref[...] = jnp.zeros_like(dgf_ref)
            loss_ref[...] = jnp.zeros_like(loss_ref)

        x2 = x1_ref[...] + jnp.concatenate([f_ref[...] for f_ref in f_refs], axis=-1)
        r = lax.rsqrt(jnp.mean(x2 * x2, axis=-1, keepdims=True) + EPS)
        nrm = x2 * r
        gf = gf_ref[...]
        err = nrm * gf - t_ref[...]
        loss_ref[...] += jnp.sum(err * err) * (0.5 / d)
        dy = err * (1.0 / d)
        dgf_ref[...] += jnp.sum(dy * nrm, axis=0, keepdims=True)
        dn = dy * gf
        dx = r * (dn - nrm * jnp.mean(dn * nrm, axis=-1, keepdims=True))
        dxb_ref[...] = dx.astype(BF16)

    row = pl.BlockSpec((tm, d), lambda i: (i, 0))
    vec = pl.BlockSpec((1, d), lambda i: (0, 0))
    return pl.pallas_call(
        body, name="loss_bwd", grid=(s // tm,),
        in_specs=[pl.BlockSpec((tm, f.shape[1]), lambda i: (i, 0)) for f in ffn_parts] + [row, row, vec],
        out_specs=[row, vec, pl.BlockSpec((8, LANES), lambda i: (0, 0))],
        out_shape=[jax.ShapeDtypeStruct((s, d), BF16),
                   jax.ShapeDtypeStruct((1, d), F32), jax.ShapeDtypeStruct((8, LANES), F32)],
        compiler_params=_cp(1),
    )(*ffn_parts, x1, target, final_g)


def _ffn_gate_bwd(dx2b, wd_parts, dadg, dadu):
    s, d = dx2b.shape
    f8 = dadg.shape[2]
    th = min(1024, s)
    nparts = len(wd_parts)
    pc = d // nparts

    def body(dx_ref, *refs):
        wd_refs = refs[:nparts]
        g_ref, u_ref, dg_ref, du_ref, da_ref = refs[nparts:]
        i = pl.program_id(1)
        chunks = _chunks(th, 256)

        def matmul(rs):
            rows = pl.ds(pl.multiple_of(i * th + rs.start, rs.stop - rs.start), rs.stop - rs.start)
            part = None
            for q, wd_ref in enumerate(wd_refs):
                term = _dot(dx_ref[rows, q * pc:(q + 1) * pc], wd_ref[...], NT)
                part = term if part is None else part + term
            da_ref[rs, :] = part

        matmul(chunks[0])
        for k, rs in enumerate(chunks):
            if k + 1 < len(chunks):
                matmul(chunks[k + 1])
            da = da_ref[rs, :].astype(BF16)
            dg_ref[rs, :] = da * g_ref[rs, :]
            du_ref[rs, :] = da * u_ref[rs, :]

    aspec = pl.BlockSpec((None, th, f8), lambda j, i: (j, i, 0))
    out = jax.ShapeDtypeStruct((N_DEV, s, f8), BF16)
    return pl.pallas_call(
        body, name="ffn_act_bwd", grid=(N_DEV, s // th),
        in_specs=[pl.BlockSpec((s, d), lambda j, i: (0, 0))]
        + [pl.BlockSpec((None, f8, pc), lambda j, i: (j, 0, 0))] * nparts + [aspec, aspec],
        out_specs=[aspec, aspec], out_shape=[out, out],
        scratch_shapes=[pltpu.VMEM((th, f8), F32)],
        compiler_params=_cp(2),
    )(dx2b, *wd_parts, dadg, dadu)


def _wgrad_rows(a3, b, name, after=(), carry=()):
    _, s, k = a3.shape
    n = b.shape[1]
    nc = len(carry)
    c_in, c_out, c_shape, c_sems = _carry_specs(carry)

    def body(a_ref, b_ref, *rest):
        rest = rest[len(after):]
        o_ref = rest[nc]
        j = pl.program_id(0)
        _carry_run(j == 0, j == N_DEV - 1, rest[:nc], rest[nc + 1:2 * nc + 1], rest[2 * nc + 1:])
        o_ref[...] = _dot(a_ref[...], b_ref[...], TN).astype(BF16)

    outs = pl.pallas_call(
        body, name=name, grid=(N_DEV,),
        in_specs=[pl.BlockSpec((None, s, k), lambda j: (j, 0, 0)),
                  pl.BlockSpec((s, n), lambda j: (0, 0))] + _after_specs(after) + c_in,
        out_specs=[pl.BlockSpec((None, k, n), lambda j: (j, 0, 0))] + c_out,
        out_shape=[jax.ShapeDtypeStruct((N_DEV, k, n), BF16)] + c_shape,
        scratch_shapes=c_sems,
        compiler_params=_cp_carry(1, carry),
    )(a3, b, *after, *carry)
    return (outs[0], list(outs[1:])) if nc else outs[0]


def _wgrad_cols(a, b3, name, after=()):
    s, k = a.shape
    if b3.ndim == 2:
        n = b3.shape[1] // N_DEV
        b_spec = pl.BlockSpec((s, n), lambda j: (0, j))
    else:
        n = b3.shape[2]
        b_spec = pl.BlockSpec((None, s, n), lambda j: (j, 0, 0))

    def body(a_ref, b_ref, *rest):
        o_ref = rest[len(after)]
        o_ref[...] = _dot(a_ref[...], b_ref[...], TN).astype(BF16)

    return pl.pallas_call(
        body, name=name, grid=(N_DEV,),
        in_specs=[pl.BlockSpec((s, k), lambda j: (0, 0)), b_spec] + _after_specs(after),
        out_specs=pl.BlockSpec((None, k, n), lambda j: (j, 0, 0)),
        out_shape=jax.ShapeDtypeStruct((N_DEV, k, n), BF16),
        compiler_params=_cp(1),
    )(a, b3, *after)


def _input_grad(pairs, name, after=(), carry=(), per=1):
    s = pairs[0][0].shape[1]
    d = pairs[0][1].shape[1]
    tn = min(1024, d)
    npair = len(pairs)
    nc = len(carry)
    c_in, c_out, c_shape, c_sems = _carry_specs(carry)

    def body(*refs):
        ops = refs[:2 * npair]
        rest = refs[2 * npair + len(after):]
        o_ref, acc_ref = rest[nc], rest[-1]
        nh, j = pl.program_id(0), pl.program_id(1)
        last_j = N_DEV // per - 1
        _carry_run((nh == 0) & (j == 0), (nh == d // tn - 1) & (j == last_j),
                   rest[:nc], rest[nc + 1:2 * nc + 1], rest[2 * nc + 1:-1])

        @pl.when(j == 0)
        def _():
            acc_ref[...] = jnp.zeros_like(acc_ref)

        for rs in _chunks(s, 1024):
            part = None
            for q in range(npair):
                for e in range(per):
                    term = _dot(ops[2 * q][e, rs, :], ops[2 * q + 1][e], NT)
                    part = term if part is None else part + term
            acc_ref[rs, :] += part

        @pl.when(j == last_j)
        def _():
            o_ref[...] = acc_ref[...].astype(BF16)

    in_specs, args = [], []
    for a3, w3 in pairs:
        k = a3.shape[2]
        in_specs += [pl.BlockSpec((per, s, k), lambda n, j: (j, 0, 0)),
                     pl.BlockSpec((per, tn, k), lambda n, j: (j, n, 0))]
        args += [a3, w3]
    outs = pl.pallas_call(
        body, name=name, grid=(d // tn, N_DEV // per),
        in_specs=in_specs + _after_specs(after) + c_in,
        out_specs=[pl.BlockSpec((s, tn), lambda n, j: (0, n))] + c_out,
        out_shape=[jax.ShapeDtypeStruct((s, d), BF16)] + c_shape,
        scratch_shapes=c_sems + [pltpu.VMEM((s, tn), F32)],
        compiler_params=_cp_carry(2, carry),
    )(*args, *after, *carry)
    return (outs[0], list(outs[1:])) if nc else outs[0]


def _rms_bwd(dh, xres, g, dres, name, with_bf16=True):
    s, d = xres.shape
    tm = min(256, s)

    def body(dh_ref, x_ref, g_ref, dres_ref, dx_ref, *rest):
        dg_ref = rest[-1]
        @pl.when(pl.program_id(0) == 0)
        def _():
            dg_ref[...] = jnp.zeros_like(dg_ref)

        xv = x_ref[...]
        dh_v = dh_ref[...].astype(F32)
        r = lax.rsqrt(jnp.mean(xv * xv, axis=-1, keepdims=True) + EPS)
        nrm = xv * r
        dg_ref[...] += jnp.sum(dh_v * nrm, axis=0, keepdims=True)
        dn = dh_v * g_ref[...]
        dx = dres_ref[...].astype(F32) + r * (dn - nrm * jnp.mean(dn * nrm, axis=-1, keepdims=True))
        dx_ref[...] = dx
        if with_bf16:
            rest[0][...] = dx.astype(BF16)

    row = pl.BlockSpec((tm, d), lambda i: (i, 0))
    vec = pl.BlockSpec((1, d), lambda i: (0, 0))
    copies = [jax.ShapeDtypeStruct((s, d), BF16)] if with_bf16 else []
    outs = pl.pallas_call(
        body, name=name, grid=(s // tm,),
        in_specs=[row, row, vec, row],
        out_specs=[row] + [row] * len(copies) + [vec],
        out_shape=[jax.ShapeDtypeStruct((s, d), F32)] + copies + [jax.ShapeDtypeStruct((1, d), F32)],
        compiler_params=_cp(1),
    )(dh, xres, g, dres)
    return (outs[0], outs[1], outs[2]) if with_bf16 else (outs[0], None, outs[1])


def _wgrad_full(a, b, name, after=(), carry=()):
    s, k = a.shape
    n = b.shape[1]
    tk = min(512, k)
    nc = len(carry)
    c_in, c_out, c_shape, c_sems = _carry_specs(carry)

    def body(a_ref, b_ref, *rest):
        rest = rest[len(after):]
        o_ref = rest[nc]
        j = pl.program_id(0)
        _carry_run(j == 0, j == k // tk - 1, rest[:nc], rest[nc + 1:2 * nc + 1], rest[2 * nc + 1:])
        o_ref[...] = _dot(a_ref[...], b_ref[...], TN).astype(BF16)

    outs = pl.pallas_call(
        body, name=name, grid=(k // tk,),
        in_specs=[pl.BlockSpec((s, tk), lambda j: (0, j)),
                  pl.BlockSpec((s, n), lambda j: (0, 0))] + _after_specs(after) + c_in,
        out_specs=[pl.BlockSpec((tk, n), lambda j: (j, 0))] + c_out,
        out_shape=[jax.ShapeDtypeStruct((k, n), BF16)] + c_shape,
        scratch_shapes=c_sems,
        compiler_params=_cp_carry(1, carry),
    )(a, b, *after, *carry)
    return (outs[0], list(outs[1:])) if nc else outs[0]


def _wgrad_pool(p, dyb, n_groups):
    s, sw = p.shape
    d = dyb.shape[1]
    gw, go = sw // n_groups, d // n_groups
    ts = min(512, s)
    ns = s // ts

    def body(a_ref, b_ref, o_ref, acc_ref):
        i = pl.program_id(1)

        @pl.when(i == 0)
        def _():
            acc_ref[...] = jnp.zeros_like(acc_ref)

        acc_ref[...] += _dot(a_ref[...], b_ref[...], TN)

        @pl.when(i == ns - 1)
        def _():
            o_ref[...] = acc_ref[...].astype(BF16)

    return pl.pallas_call(
        body, name="wgrad_pool", grid=(n_groups, ns),
        in_specs=[pl.BlockSpec((ts, gw), lambda g, i: (i, g)),
                  pl.BlockSpec((ts, go), lambda g, i: (i, g))],
        out_specs=pl.BlockSpec((None, gw, go), lambda g, i: (g, 0, 0)),
        out_shape=jax.ShapeDtypeStruct((n_groups, gw, go), BF16),
        scratch_shapes=[pltpu.VMEM((gw, go), F32)],
        compiler_params=_cp(2),
    )(p, dyb)


def _wo_bwd(dx1b, wo, factors, sw, after=()):
    s, d = dx1b.shape
    tn = d // N_DEV
    nq = sw // tn

    def body(dx_ref, wo_ref, fya_ref, fyb_ref, fga_ref, fgb_ref, fsc_ref, *rest):
        dya_ref, dyb_ref, dp_ref, dbg_ref, dsc_ref, dm_ref = rest[len(after):]
        dbg_ref[...] = jnp.zeros_like(dbg_ref)
        dsc_ref[...] = jnp.zeros_like(dsc_ref)
        for rs in _chunks(s, 1024):
            dm_ref[rs, :] = _dot(dx_ref[rs, :], wo_ref[...], NT)
        for rs in _chunks(s, 256):
            dm = dm_ref[rs, :]
            dya_ref[rs, :] = (dm * fya_ref[rs, :].astype(F32)).astype(BF16)
            dyb_ref[rs, :] = (dm * fyb_ref[rs, :].astype(F32)).astype(BF16)
            dsc_ref[...] += jnp.sum(dm * fsc_ref[rs, :].astype(F32), axis=0, keepdims=True)
            dga = dm * fga_ref[rs, :].astype(F32)
            dgb = dm * fgb_ref[rs, :].astype(F32)
            dp_ref[0, rs, :] = dga.astype(BF16)
            dp_ref[1, rs, :] = dgb.astype(BF16)
            dbg_ref[0:1, :] += jnp.sum(dga, axis=0, keepdims=True)
            dbg_ref[1:2, :] += jnp.sum(dgb, axis=0, keepdims=True)

    col = pl.BlockSpec((s, tn), lambda j: (0, j))
    out = jax.ShapeDtypeStruct((s, d), BF16)
    return pl.pallas_call(
        body, name="wo_bwd", grid=(N_DEV,),
        in_specs=[pl.BlockSpec((s, d), lambda j: (0, 0)),
                  pl.BlockSpec((tn, d), lambda j: (j, 0))] + [col] * 5 + _after_specs(after),
        out_specs=[col, col,
                   pl.BlockSpec((2, None, s, tn), lambda j: (1, j // nq, 0, j % nq)),
                   pl.BlockSpec((2, tn), lambda j: (0, j)),
                   pl.BlockSpec((1, tn), lambda j: (0, j))],
        out_shape=[out, out, jax.ShapeDtypeStruct((4, 2, s, sw), BF16),
                   jax.ShapeDtypeStruct((2, d), F32), jax.ShapeDtypeStruct((1, d), F32)],
        scratch_shapes=[pltpu.VMEM((s, tn), F32)],
        compiler_params=_cp(1),
    )(dx1b, wo, *factors, *after)


def _conv_bwd(dproj, dya, wa, proj, conv_w, conv_b):
    s, d = dya.shape
    sw, tn = wa.shape[1], wa.shape[2]
    tc = min(LANES, sw)

    def body(dproj_hbm, dya_ref, wa_ref, ba_ref, ca_ref, va_ref, cw_ref, cb_ref,
             dp_ref, dcw_ref, dcb_ref, dz_ref):
        del dproj_hbm
        for rs in _chunks(s, 512):
            part = _dot(dya_ref[rs, 0:tn], wa_ref[0], NT)
            for j in range(1, N_DEV):
                part = part + _dot(dya_ref[rs, j * tn:(j + 1) * tn], wa_ref[j], NT)
            dz_ref[rs, :] = part
        dz = dz_ref[...]
        ba, ca, va = ba_ref[...], ca_ref[...], va_ref[...]
        cv = ca * va
        cv1, cv2 = _shift_down(cv, 1), _shift_down(cv, 2)
        w0, w1, w2 = cw_ref[0:1, :], cw_ref[1:2, :], cw_ref[2:3, :]
        u = cb_ref[...] + w0 * cv2 + w1 * cv1 + w2 * cv
        du = dz * ba
        dp_ref[0] = (dz * u).astype(BF16)
        dcv = w2 * du + w1 * _shift_up(du, 1) + w0 * _shift_up(du, 2)
        dp_ref[1] = (dcv * va).astype(BF16)
        dp_ref[2] = (dcv * ca).astype(BF16)
        dcw_ref[0:1, :] = jnp.sum(du * cv2, axis=0, keepdims=True)
        dcw_ref[1:2, :] = jnp.sum(du * cv1, axis=0, keepdims=True)
        dcw_ref[2:3, :] = jnp.sum(du * cv, axis=0, keepdims=True)
        dcb_ref[...] = jnp.sum(du, axis=0, keepdims=True)

    def part(k):
        return pl.BlockSpec((None, s, tc), lambda i: (k, 0, i))

    return pl.pallas_call(
        body, name="conv_bwd", grid=(sw // tc,),
        in_specs=[pl.BlockSpec(memory_space=pl.ANY),
                  pl.BlockSpec((s, d), lambda i: (0, 0)),
                  pl.BlockSpec((N_DEV, tc, tn), lambda i: (0, i, 0)),
                  part(0), part(1), part(2),
                  pl.BlockSpec((CONV_K, tc), lambda i: (0, i)), pl.BlockSpec((1, tc), lambda i: (0, i))],
        out_specs=[pl.BlockSpec((3, s, tc), lambda i: (0, 0, i)),
                   pl.BlockSpec((CONV_K, tc), lambda i: (0, i)), pl.BlockSpec((1, tc), lambda i: (0, i))],
        out_shape=[jax.ShapeDtypeStruct(dproj.shape, BF16),
                   jax.ShapeDtypeStruct((CONV_K, sw), F32), jax.ShapeDtypeStruct((1, sw), F32)],
        scratch_shapes=[pltpu.VMEM((s, tc), F32)],
        input_output_aliases={0: 0},
        compiler_params=_cp(1),
    )(dproj, dya, wa, proj, proj, proj, conv_w, conv_b)


def _pool_bwd(dproj, dyb, wpool):
    s, d = dyb.shape
    n_groups, gw, go = wpool.shape

    def body(dproj_hbm, dyb_ref, wp_ref, dp_ref):
        del dproj_hbm
        for gi, window in enumerate(POOL_WINDOWS):
            @pl.when(pl.program_id(0) == gi)
            def _():
                dpool = _dot(dyb_ref[...], wp_ref[...], NT)
                acc, k = dpool / _pool_counts(dpool.shape, window), 1
                while k < window:
                    acc = acc + _shift_up(acc, k)
                    k *= 2
                dp_ref[...] = (acc - dpool).astype(BF16)

    return pl.pallas_call(
        body, name="pool_bwd", grid=(n_groups,),
        in_specs=[pl.BlockSpec(memory_space=pl.ANY),
                  pl.BlockSpec((s, go), lambda g: (0, g)),
                  pl.BlockSpec((None, gw, go), lambda g: (g, 0, 0))],
        out_specs=pl.BlockSpec((None, s, gw), lambda g: (3, 0, g)),
        out_shape=jax.ShapeDtypeStruct(dproj.shape, BF16),
        input_output_aliases={0: 0},
        compiler_params=_cp(1),
    )(dproj, dyb, wpool)


def _rows128(v):
    return v.reshape(-1, LANES)


def kernel(x, norm1_g, w_in, b_gate, conv_w, conv_b, w_a_out, w_pool, pool_scale, w_o, norm2_g, w_ffn_gate, w_ffn_up, w_ffn_down, final_g, loss_target, m_norm1_g, m_w_in, m_b_gate, m_conv_w, m_conv_b, m_w_a_out, m_w_pool, m_pool_scale, m_w_o, m_norm2_g, m_w_ffn_gate, m_w_ffn_up, m_w_ffn_down, m_final_g, v_norm1_g, v_w_in, v_b_gate, v_conv_w, v_conv_b, v_w_a_out, v_w_pool, v_pool_scale, v_w_o, v_norm2_g, v_w_ffn_gate, v_w_ffn_up, v_w_ffn_down, v_final_g):
    s, d = x.shape[1], x.shape[2]
    sw = w_in.shape[2]
    n_groups = w_pool.shape[1]
    gw = w_pool.shape[2]
    go = w_pool.shape[3] * N_DEV
    f8 = w_ffn_gate.shape[2]
    cws = conv_w.shape[2]
    assert sw == conv_w.shape[2] * N_DEV == gw * n_groups and go * n_groups == d and n_groups == len(POOL_WINDOWS)

    xi, yi, ci = _coords()
    me = 4 * xi + 2 * yi + ci
    my_chip = 2 * xi + yi

    x2d = x.reshape(s, d)
    target = loss_target.reshape(s, d)
    final_g2 = final_g.reshape(1, d)
    b_gate2 = b_gate.reshape(2, d)

    big_names = ["w_in", "w_a_out", "w_pool", "w_o", "w_ffn_gate", "w_ffn_up", "w_ffn_down"]
    big_w = [w_in, w_a_out, w_pool, w_o, w_ffn_gate, w_ffn_up, w_ffn_down]
    big_m = [m_w_in, m_w_a_out, m_w_pool, m_w_o, m_w_ffn_gate, m_w_ffn_up, m_w_ffn_down]
    big_v = [v_w_in, v_w_a_out, v_w_pool, v_w_o, v_w_ffn_gate, v_w_ffn_up, v_w_ffn_down]
    shapes2d = [(w.size // w.shape[-1], w.shape[-1]) for w in big_w]
    big_w2 = [w.reshape(sh) for w, sh in zip(big_w, shapes2d)]
    transposed = (4, 5)

    def view2d(t, a):
        t2 = t.reshape(shapes2d[a])
        return t2.T if a in transposed else t2

    def unview(o, a):
        return (o.T if a in transposed else o).reshape(big_w[a].shape)

    sb = [_cast_bf16(w, "cast_" + nm, parts=2 if nm == "w_ffn_down" else 1) for w, nm in zip(big_w2, big_names)]
    win_g, wa_g, wpool_g, wo_g = _allgather_big([b[0] for b in sb[0:4]], "allgather_mixer", COLLECTIVE_GATHER)
    (wg_g,) = _allgather_big(sb[4], "allgather_ffn_gate", COLLECTIVE_GATHER)
    (wu_g,) = _allgather_big(sb[5], "allgather_ffn_up", COLLECTIVE_GATHER)
    wd_parts = [_allgather_big([part], "allgather_ffn_down_%d" % q, COLLECTIVE_GATHER)[0]
                for q, part in enumerate(sb[6])]
    convw_g = _allgather_small(jnp.pad(conv_w.reshape(CONV_K, cws), ((0, 8 - CONV_K), (0, 0))), "allgather_conv_w")
    conv_w_full = convw_g[:, :CONV_K, :].transpose(1, 0, 2).reshape(CONV_K, sw)
    wpool = wpool_g.reshape(N_DEV, n_groups, gw, go // N_DEV).transpose(1, 2, 0, 3).reshape(n_groups, gw, go)
    wo = wo_g.reshape(d, d)

    h = _rms_fwd(x2d, norm1_g)
    proj = _proj_fwd(h, win_g)
    z = _conv_fwd(proj, conv_w_full, conv_b)
    p = _pool_fwd(proj)
    merged, *merge_factors = _merge_fwd(z, wa_g, p, wpool, proj, b_gate2, pool_scale)
    x1, h2 = _wo_fwd(merged, wo, x2d, norm2_g)
    gate = _ffn_gate_fwd(h2, wg_g)
    dadu, dadg, act = _ffn_up_act_fwd(h2, wu_g, gate)
    ffn_parts = [_ffn_down_fwd(act, wd, "ffn_down_fwd_%d" % q) for q, wd in enumerate(wd_parts)]
    dx2b, d_final_g, loss_blk = _loss_bwd(ffn_parts, x1, target, final_g2)

    other_chips = jnp.stack([2 * (1 - xi) + yi, 2 * xi + (1 - yi), 2 * (1 - xi) + (1 - yi)])
    others = jnp.concatenate([other_chips, 2 * other_chips + ci]).astype(jnp.int32)

    def partials(grads, recvs, names):
        if all(g.shape == grads[0].shape for g in grads):
            return list(_chip_partial(others, grads, recvs, "chip_partial_" + names[0]))
        return [_chip_partial(others, [g3], [r], "chip_partial_" + nm)[0] for g3, r, nm in zip(grads, recvs, names)]

    own = jnp.stack([me, my_chip]).astype(jnp.int32)

    def adam(idx, g3s, sibs, chipss):
        wmvs = [(view2d(big_w[a], a), view2d(big_m[a], a), view2d(big_v[a], a)) for a in idx]
        outs = _adam_big(own, wmvs, g3s, sibs, chipss, "adam_" + big_names[idx[0]])
        for a, o4 in zip(idx, outs):
            big_out[a] = [unview(o, a) for o in o4]

    big_out = [None] * len(big_names)
    dg_act, du_act = _ffn_gate_bwd(dx2b, wd_parts, dadg, dadu)
    gw_gate = _wgrad_rows(dg_act, h2, "wgrad_ffn_gate")
    gw_up = _wgrad_rows(du_act, h2, "wgrad_ffn_up")
    gw_down, sib_gu = _wgrad_rows(act, dx2b, "wgrad_ffn_down", carry=[gw_gate, gw_up])
    ps_gu = partials([gw_gate, gw_up], sib_gu, ["w_ffn_gate", "w_ffn_up"])
    chips_gu = _exchange_chips(ps_gu, "rs_chips_ffn_up", COLLECTIVE_CHIPS)
    dh2, sib_down = _input_grad([(dg_act, wg_g), (du_act, wu_g)], "ffn_in_bwd", after=ps_gu, carry=[gw_down])
    ps_down = partials([gw_down], sib_down, ["w_ffn_down"])
    chips_down = _exchange_chips(ps_down, "rs_chips_ffn_down", COLLECTIVE_CHIPS)
    dx1, dx1b, d_norm2_g = _rms_bwd(dh2, x1, norm2_g, dx2b, "rms2_bwd")
    dya, dyb, dproj42, d_b_gate, d_pool_scale = _wo_bwd(dx1b, wo, merge_factors, sw, after=ps_down)
    dproj = dproj42.reshape(N_DEV, s, sw)
    dproj, d_conv_w, d_conv_b = _conv_bwd(dproj, dya, wa_g, proj, conv_w_full, conv_b)
    dproj = _pool_bwd(dproj, dyb, wpool)
    gw_in = _wgrad_cols(h, dproj, "wgrad_in")
    gw_o, sib_in = _wgrad_full(merged, dx1b, "wgrad_o", carry=[gw_in])
    ps_in = partials([gw_in], sib_in, ["w_in"])
    chips_in = _exchange_chips(ps_in, "rs_chips_w_in", COLLECTIVE_CHIPS)
    gw_a = _wgrad_cols(z, dya, "wgrad_a_out", after=ps_in)
    gw_pool = _wgrad_pool(p, dyb, n_groups)
    mix3 = [gw_a,
            gw_pool.reshape(n_groups, gw, N_DEV, go // N_DEV).transpose(2, 0, 1, 3).reshape(N_DEV, n_groups * gw, go // N_DEV),
            gw_o.reshape(N_DEV, d // N_DEV, d)]
    adam([4, 5, 6], [gw_gate, gw_up, gw_down], sib_gu + sib_down, chips_gu + chips_down)
    dh, sib_mix = _input_grad([(dproj, win_g)], "proj_in_bwd", after=[big_out[6][0]], carry=mix3, per=2)
    ps_mix = partials(mix3, sib_mix, ["w_a_out", "w_pool", "w_o"])
    chips_mix = _exchange_chips(ps_mix, "rs_chips_mixer", COLLECTIVE_CHIPS)
    grad_x, _, d_norm1_g = _rms_bwd(dh, x2d, norm1_g, dx1, "rms1_bwd", with_bf16=False)

    small_parts = [d_norm1_g, d_b_gate, d_conv_w, d_conv_b, d_pool_scale, d_norm2_g, d_final_g, loss_blk]
    rows = [v.size // LANES for v in small_parts]
    row0 = [sum(rows[:k]) for k in range(len(rows))]
    packed = jnp.concatenate([_rows128(v) for v in small_parts], axis=0)
    adam([0], [gw_in], sib_in, chips_in)
    gathered = _allgather_small_async(packed, "allgather_small_grads", after=[big_out[0][0]])
    for k in range(3):
        adam([1 + k], [mix3[k]], [sib_mix[k]], [chips_mix[k]])

    small_names = ["norm1_g", "b_gate", "conv_b", "pool_scale", "norm2_g", "final_g", "conv_w"]
    small_w = [norm1_g, b_gate, conv_b, pool_scale, norm2_g, final_g]
    small_m = [m_norm1_g, m_b_gate, m_conv_b, m_pool_scale, m_norm2_g, m_final_g]
    small_v = [v_norm1_g, v_b_gate, v_conv_b, v_pool_scale, v_norm2_g, v_final_g]
    finished = _small_finish(gathered, [tuple(_rows128(t) for t in wmv) for wmv in zip(small_w, small_m, small_v)],
                             [row0[k] for k in (0, 1, 3, 4, 5, 6)], [(row0[2], rows[2]), (row0[7], rows[7])],
                             after=[big_out[3][0]])
    g_convw_full, loss_rows = finished[0], finished[1]
    loss = loss_rows[0, 0]
    small_out = [[t.reshape(w.shape) for t in finished[2 + 4 * k:6 + 4 * k]] for k, w in enumerate(small_w)]
    g_convw = lax.dynamic_slice(g_convw_full.reshape(CONV_K, sw), (0, me * cws), (CONV_K, cws))
    cw_delta, cw_m, cw_v = _adam_small(conv_w.reshape(CONV_K, cws), g_convw,
                                       m_conv_w.reshape(CONV_K, cws), v_conv_w.reshape(CONV_K, cws))
    small_out.append([t.reshape(conv_w.shape) for t in (g_convw, cw_delta, cw_m, cw_v)])

    order = ["norm1_g", "w_in", "b_gate", "conv_w", "conv_b", "w_a_out", "w_pool", "pool_scale", "w_o", "norm2_g",
             "w_ffn_gate", "w_ffn_up", "w_ffn_down", "final_g"]
    per_kind = [{}, {}, {}, {}]
    for a, nm in enumerate(big_names):
        for kind in range(4):
            per_kind[kind][nm] = big_out[a][kind]
    for k, nm in enumerate(small_names):
        for kind in range(4):
            per_kind[kind][nm] = small_out[k][kind]
    result = [loss, grad_x.reshape(x.shape)]
    for kind in range(4):
        result += [per_kind[kind][nm] for nm in order]
    return tuple(result)
```

```python
import jax
import jax.numpy as jnp
from jax import lax
from jax.experimental import pallas as pl
from jax.experimental.pallas import tpu as pltpu
from jax.experimental.pallas import tpu_sc as plsc

F32 = jnp.float32
BF16 = jnp.bfloat16
MESH = pl.DeviceIdType.MESH

N_DEV = 8
EPS = 1e-6
CONV_K = 3
POOL_WINDOWS = (2, 4, 8, 16)
ADAM_LR = 0.001
ADAM_B1 = 0.9
ADAM_B2 = 0.999
ADAM_EPS = 1e-08
ADAM_WD = 0.01
ADAM_STEP = 10

V7X_VMEM_LIMIT_BYTES = 56 * 1024 * 1024
LANES = 128

COLLECTIVE_GATHER = 1
COLLECTIVE_SIBLING = 2
COLLECTIVE_CHIPS = 3
SEQUENCER_COST_BYTES = 4 * 10**9

NN = ((1,), (0,))
NT = ((1,), (1,))
TN = ((0,), (0,))


def _dot(a, b, dims):
    return lax.dot_general(a, b, (dims, ((), ())), preferred_element_type=F32)


def _cp(n_axes):
    return pltpu.CompilerParams(dimension_semantics=("arbitrary",) * n_axes,
                                vmem_limit_bytes=V7X_VMEM_LIMIT_BYTES)


def _row_tile(rows, bytes_per_row, cap_bytes):
    best = None
    for t in range(16, rows + 1, 16):
        if rows % t == 0 and t * bytes_per_row <= cap_bytes:
            best = t
    return best if best is not None else rows


def _chunks(total, size):
    size = min(size, total)
    assert total % size == 0
    return [slice(r, r + size) for r in range(0, total, size)]


def _after_specs(after):
    return [pl.BlockSpec(memory_space=pl.ANY)] * len(after)


def _shift_down(v, k):
    row = lax.broadcasted_iota(jnp.int32, v.shape, 0)
    return jnp.where(row >= k, pltpu.roll(v, k, 0), 0.0)


def _shift_up(v, k):
    n = v.shape[0]
    row = lax.broadcasted_iota(jnp.int32, v.shape, 0)
    return jnp.where(row < n - k, pltpu.roll(v, n - k, 0), 0.0)


def _sigmoid(v):
    return jax.nn.sigmoid(v)


def _cast_bf16(w2d, name, parts=1):
    rows, cols = w2d.shape
    tr = _row_tile(rows, cols * 4, 2 << 20)
    pc = cols // parts

    def body(i_ref, *o_refs):
        for q, o_ref in enumerate(o_refs):
            o_ref[...] = i_ref[:, q * pc:(q + 1) * pc].astype(BF16)

    return pl.pallas_call(
        body, name=name, grid=(rows // tr,),
        in_specs=[pl.BlockSpec((tr, cols), lambda i: (i, 0))],
        out_specs=[pl.BlockSpec((tr, pc), lambda i: (i, 0))] * parts,
        out_shape=[jax.ShapeDtypeStruct((rows, pc), BF16)] * parts,
        compiler_params=_cp(1),
    )(w2d)


def _rms_fwd(x2d, g):
    s, d = x2d.shape
    tm = min(256, s)

    def body(x_ref, g_ref, h_ref):
        xv = x_ref[...]
        r = lax.rsqrt(jnp.mean(xv * xv, axis=-1, keepdims=True) + EPS)
        h_ref[...] = (xv * r * g_ref[...]).astype(BF16)

    return pl.pallas_call(
        body, name="rms1_fwd", grid=(s // tm,),
        in_specs=[pl.BlockSpec((tm, d), lambda i: (i, 0)), pl.BlockSpec((1, d), lambda i: (0, 0))],
        out_specs=pl.BlockSpec((tm, d), lambda i: (i, 0)),
        out_shape=jax.ShapeDtypeStruct((s, d), BF16),
        compiler_params=_cp(1),
    )(x2d, g)


def _coords():
    return lax.axis_index("x"), lax.axis_index("y"), lax.axis_index("c")


def _slot(p):
    return 4 * p[0] + 2 * p[1] + p[2]


def _handshake(peers):
    barrier = pltpu.get_barrier_semaphore()
    for peer in peers:
        pl.semaphore_signal(barrier, inc=1, device_id=peer, device_id_type=MESH)
    pl.semaphore_wait(barrier, len(peers))


def _sequencer_call(body, out_type, scratch_types, name, collective_id):
    return pl.kernel(
        body, out_type=out_type, name=name,
        mesh=plsc.ScalarSubcoreMesh(axis_name="seq", num_cores=1),
        scratch_types=scratch_types,
        cost_estimate=pl.CostEstimate(flops=0, transcendentals=0, bytes_accessed=SEQUENCER_COST_BYTES),
        compiler_params=pltpu.CompilerParams(collective_id=collective_id))


def _allgather_big(shards, name, collective_id, after=()):
    n = len(shards)

    def body(*refs):
        ins, outs = refs[:n], refs[n + len(after):2 * n + len(after)]
        send_sems, recv_sems, local_sems = refs[2 * n + len(after):]
        x, y, c = _coords()
        me, sibling = (x, y, c), (x, y, 1 - c)
        x_nbr, y_nbr, diag = (1 - x, y), (x, 1 - y), (1 - x, 1 - y)
        relay_from = (x + (1 - c) * (1 - 2 * x), y + c * (1 - 2 * y))
        relay_to = (x + c * (1 - 2 * x), y + (1 - c) * (1 - 2 * y))
        _handshake([sibling, (*x_nbr, c), (*y_nbr, c)])

        def copy(a, k, block, to, src=None):
            dst = outs[a].at[_slot(block)]
            return pltpu.make_async_remote_copy(
                src_ref=dst if src is None else src, dst_ref=dst,
                send_sem=send_sems.at[a, k], recv_sem=recv_sems.at[a, k],
                device_id=to, device_id_type=MESH)

        mine, sends = [], []
        for a in range(n):
            cp = pltpu.make_async_copy(ins[a], outs[a].at[_slot(me)], local_sems.at[a])
            cp.start()
            mine.append(cp)
            first = [copy(a, 0, me, sibling, src=ins[a]),
                     copy(a, 1, me, (*x_nbr, c), src=ins[a]),
                     copy(a, 2, me, (*y_nbr, c), src=ins[a])]
            for cp in first:
                cp.start()
            sends += first
        for a in range(n):
            copy(a, 1 + c, (*relay_from, c), me).wait_recv()
            passed = [copy(a, 3, (*relay_from, c), (*relay_to, c)), copy(a, 4 + c, (*relay_from, c), sibling)]
            for cp in passed:
                cp.start()
            copy(a, 2 - c, (*relay_to, c), me).wait_recv()
            cp = copy(a, 5 - c, (*relay_to, c), sibling)
            cp.start()
            passed.append(cp)
            copy(a, 3, (*diag, c), me).wait_recv()
            cp = copy(a, 6, (*diag, c), sibling)
            cp.start()
            sends += passed + [cp]
        for a in range(n):
            copy(a, 0, sibling, me).wait_recv()
            copy(a, 4, (*x_nbr, 1 - c), me).wait_recv()
            copy(a, 5, (*y_nbr, 1 - c), me).wait_recv()
            copy(a, 6, (*diag, 1 - c), me).wait_recv()
        for cp in sends:
            cp.wait_send()
        for cp in mine:
            cp.wait()

    return _sequencer_call(
        body, [jax.ShapeDtypeStruct((N_DEV,) + s.shape, s.dtype) for s in shards],
        [pltpu.SemaphoreType.DMA((n, 7)), pltpu.SemaphoreType.DMA((n, 7)), pltpu.SemaphoreType.DMA((n,))],
        name, collective_id)(*shards, *after)


def _sibling_copies(ins, recvs, send_sems, recv_sems):
    x, y, c = _coords()
    return [pltpu.make_async_remote_copy(
        src_ref=ins[a].at[2 * q + (1 - c)], dst_ref=recvs[a].at[q],
        send_sem=send_sems.at[a, q], recv_sem=recv_sems.at[a, q],
        device_id=(x, y, 1 - c), device_id_type=MESH) for a in range(len(ins)) for q in range(4)]


def _carry_specs(carry):
    any_spec = pl.BlockSpec(memory_space=pl.ANY)
    n = len(carry)
    sems = [pltpu.SemaphoreType.DMA((n, 4)), pltpu.SemaphoreType.DMA((n, 4))] if n else []
    return ([any_spec] * n, [any_spec] * n,
            [jax.ShapeDtypeStruct((4,) + g.shape[1:], g.dtype) for g in carry], sems)


def _carry_run(first, last, ins, recvs, sems):
    if not ins:
        return

    @pl.when(first)
    def _():
        x, y, c = _coords()
        _handshake([(x, y, 1 - c)])
        for cp in _sibling_copies(ins, recvs, *sems):
            cp.start()

    @pl.when(last)
    def _():
        copies = _sibling_copies(ins, recvs, *sems)
        for cp in copies:
            cp.wait_recv()
        for cp in copies:
            cp.wait_send()


def _cp_carry(n_axes, carry):
    if not carry:
        return _cp(n_axes)
    return pltpu.CompilerParams(dimension_semantics=("arbitrary",) * n_axes, vmem_limit_bytes=V7X_VMEM_LIMIT_BYTES,
                                collective_id=COLLECTIVE_SIBLING)


def _exchange_chips(psums, name, collective_id):
    n = len(psums)

    def body(*refs):
        ins, outs = refs[:n], refs[n:2 * n]
        send_sems, recv_sems = refs[2 * n:]
        x, y, c = _coords()
        chips = [(1 - x, y), (x, 1 - y), (1 - x, 1 - y)]
        _handshake([(*chip, c) for chip in chips])
        copies = []
        for a in range(n):
            for j, chip in enumerate(chips):
                cp = pltpu.make_async_remote_copy(
                    src_ref=ins[a].at[2 * chip[0] + chip[1]], dst_ref=outs[a].at[j],
                    send_sem=send_sems.at[a, j], recv_sem=recv_sems.at[a, j],
                    device_id=(*chip, c), device_id_type=MESH)
                cp.start()
                copies.append(cp)
        for cp in copies:
            cp.wait_recv()
        for cp in copies:
            cp.wait_send()

    return _sequencer_call(
        body, [jax.ShapeDtypeStruct((3,) + p.shape[1:], p.dtype) for p in psums],
        [pltpu.SemaphoreType.DMA((n, 3)), pltpu.SemaphoreType.DMA((n, 3))],
        name, collective_id)(*psums)


def _allgather_small(v2d, name):
    rows, cols = v2d.shape

    def body(v_ref, out_ref, send_sems, recv_sems):
        x, y, c = _coords()
        me = (x, y, c)
        out_ref[_slot(me)] = v_ref[...]
        peers = []
        for k in range(1, N_DEV):
            fx, fy, fc = (k >> 2) & 1, (k >> 1) & 1, k & 1
            peers.append(((1 - x) if fx else x, (1 - y) if fy else y, (1 - c) if fc else c))
        sends = []
        for k, peer in enumerate(peers):
            cp = pltpu.make_async_remote_copy(
                src_ref=v_ref, dst_ref=out_ref.at[_slot(me)],
                send_sem=send_sems.at[k], recv_sem=recv_sems.at[k],
                device_id=peer, device_id_type=MESH)
            cp.start()
            sends.append(cp)
        for k, peer in enumerate(peers):
            pltpu.make_async_remote_copy(
                src_ref=v_ref, dst_ref=out_ref.at[_slot(peer)],
                send_sem=send_sems.at[k], recv_sem=recv_sems.at[k],
                device_id=peer, device_id_type=MESH).wait_recv()
        for cp in sends:
            cp.wait_send()

    vmem = pl.BlockSpec(memory_space=pltpu.VMEM)
    return pl.pallas_call(
        body, name=name, in_specs=[vmem], out_specs=vmem,
        out_shape=jax.ShapeDtypeStruct((N_DEV, rows, cols), v2d.dtype),
        scratch_shapes=[pltpu.SemaphoreType.DMA((N_DEV - 1,)), pltpu.SemaphoreType.DMA((N_DEV - 1,))],
    )(v2d)


def _chip_partial(others, grads, recvs, name):
    n = len(grads)
    _, rows, cols = grads[0].shape
    tr = _row_tile(rows, cols * 2, (2 << 20) // n)

    def body(others_ref, *refs):
        for a in range(n):
            refs[2 * n + a][...] = (refs[a][...].astype(F32) + refs[n + a][...].astype(F32)).astype(BF16)

    return pl.pallas_call(
        body, name=name,
        grid_spec=pltpu.PrefetchScalarGridSpec(
            num_scalar_prefetch=1, grid=(3, rows // tr),
            in_specs=[pl.BlockSpec((None, tr, cols), lambda k, i, o: (o[3 + k], i, 0))] * n
            + [pl.BlockSpec((None, tr, cols), lambda k, i, o: (o[k], i, 0))] * n,
            out_specs=[pl.BlockSpec((None, tr, cols), lambda k, i, o: (o[k], i, 0))] * n),
        out_shape=[jax.ShapeDtypeStruct((4, rows, cols), BF16)] * n,
        compiler_params=_cp(2),
    )(others, *grads, *recvs)


def _adam_math(w, g, m, v):
    m = ADAM_B1 * m + (1.0 - ADAM_B1) * g
    v = ADAM_B2 * v + (1.0 - ADAM_B2) * (g * g)
    m_hat = m / (1.0 - ADAM_B1 ** ADAM_STEP)
    v_hat = v / (1.0 - ADAM_B2 ** ADAM_STEP)
    delta = -ADAM_LR * (m_hat / (jnp.sqrt(v_hat) + ADAM_EPS) + ADAM_WD * w)
    return delta, m, v


def _adam_big(own, wmvs, g3s, recv_sibs, recv_chipss, name):
    n = len(wmvs)
    rows, cols = wmvs[0][0].shape
    tr = _row_tile(rows, cols * 4, (2 << 20) // n)

    def body(own_ref, *refs):
        ins, outs = refs[:6 * n], refs[6 * n:]
        for a in range(n):
            w_ref, m_ref, v_ref, g_ref, rs_ref, rc_ref = ins[6 * a:6 * a + 6]
            g = g_ref[...].astype(F32) + rs_ref[...].astype(F32)
            g = g + rc_ref[0].astype(F32)
            g = g + rc_ref[1].astype(F32)
            g = g + rc_ref[2].astype(F32)
            delta, m_new, v_new = _adam_math(w_ref[...], g, m_ref[...], v_ref[...])
            outs[4 * a][...] = g
            outs[4 * a + 1][...] = delta
            outs[4 * a + 2][...] = m_new
            outs[4 * a + 3][...] = v_new

    blk = pl.BlockSpec((tr, cols), lambda i, o: (i, 0))
    per_shard = [blk, blk, blk,
                 pl.BlockSpec((None, tr, cols), lambda i, o: (o[0], i, 0)),
                 pl.BlockSpec((None, tr, cols), lambda i, o: (o[1], i, 0)),
                 pl.BlockSpec((3, tr, cols), lambda i, o: (0, i, 0))]
    out = jax.ShapeDtypeStruct((rows, cols), F32)
    args = [t for a in range(n) for t in (*wmvs[a], g3s[a], recv_sibs[a], recv_chipss[a])]
    outs = pl.pallas_call(
        body, name=name,
        grid_spec=pltpu.PrefetchScalarGridSpec(
            num_scalar_prefetch=1, grid=(rows // tr,),
            in_specs=per_shard * n, out_specs=[blk] * (4 * n)),
        out_shape=[out] * (4 * n),
        compiler_params=_cp(1),
    )(own, *args)
    return [outs[4 * a:4 * a + 4] for a in range(n)]


def _small_finish(gathered, params, row_offs, extra_rows):
    n = len(params)

    def body(g_ref, *refs):
        ins, outs = refs[:3 * n], refs[3 * n:]
        total = g_ref[0]
        for k in range(1, N_DEV):
            total = total + g_ref[k]
        for e, (r0, nr) in enumerate(extra_rows):
            outs[e][...] = total[r0:r0 + nr, :]
        for p in range(n):
            w_ref, m_ref, v_ref = ins[3 * p:3 * p + 3]
            g_out, d_out, m_out, v_out = outs[len(extra_rows) + 4 * p:len(extra_rows) + 4 * p + 4]
            g = total[row_offs[p]:row_offs[p] + w_ref.shape[0], :]
            delta, m_new, v_new = _adam_math(w_ref[...], g, m_ref[...], v_ref[...])
            g_out[...] = g
            d_out[...] = delta
            m_out[...] = m_new
            v_out[...] = v_new

    vmem = pl.BlockSpec(memory_space=pltpu.VMEM)
    out_shape = [jax.ShapeDtypeStruct((nr, LANES), F32) for _, nr in extra_rows]
    for w, _, _ in params:
        out_shape += [jax.ShapeDtypeStruct(w.shape, F32)] * 4
    flat = [t for wmv in params for t in wmv]
    return pl.pallas_call(body, name="small_finish", in_specs=[vmem] * (1 + len(flat)),
                          out_specs=[vmem] * len(out_shape), out_shape=out_shape)(gathered, *flat)


def _adam_small(w, g, m, v):
    def body(w_ref, g_ref, m_ref, v_ref, do_ref, mo_ref, vo_ref):
        delta, m_new, v_new = _adam_math(w_ref[...], g_ref[...], m_ref[...], v_ref[...])
        do_ref[...] = delta
        mo_ref[...] = m_new
        vo_ref[...] = v_new

    vmem = pl.BlockSpec(memory_space=pltpu.VMEM)
    out = jax.ShapeDtypeStruct(w.shape, F32)
    return pl.pallas_call(body, name="adam_small", in_specs=[vmem] * 4, out_specs=[vmem] * 3,
                          out_shape=[out, out, out])(w, g, m, v)


def _proj_fwd(h, win_g):
    s, d = h.shape
    sw = win_g.shape[2]
    tn = min(512, sw)
    nh = sw // tn

    def body(h_ref, w_ref, o_ref):
        for rs in _chunks(s, 512):
            o_ref[rs, :] = _dot(h_ref[rs, :], w_ref[...], NN)

    return pl.pallas_call(
        body, name="proj_fwd", grid=(N_DEV * nh,),
        in_specs=[pl.BlockSpec((s, d), lambda j: (0, 0)),
                  pl.BlockSpec((None, d, tn), lambda j: (j // nh, 0, j % nh))],
        out_specs=pl.BlockSpec((None, s, tn), lambda j: (j // nh, 0, j % nh)),
        out_shape=jax.ShapeDtypeStruct((N_DEV, s, sw), F32),
        compiler_params=_cp(1),
    )(h, win_g)


def _conv_fwd(proj, conv_w, conv_b):
    _, s, sw = proj.shape
    tc = min(LANES, sw)

    def body(ba_ref, ca_ref, va_ref, cw_ref, cb_ref, z_ref):
        cv = ca_ref[...] * va_ref[...]
        u = (cb_ref[...] + cw_ref[0:1, :] * _shift_down(cv, 2) + cw_ref[1:2, :] * _shift_down(cv, 1)
             + cw_ref[2:3, :] * cv)
        z_ref[...] = (ba_ref[...] * u).astype(BF16)

    def part(k):
        return pl.BlockSpec((None, s, tc), lambda i: (k, 0, i))

    return pl.pallas_call(
        body, name="conv_fwd", grid=(sw // tc,),
        in_specs=[part(0), part(1), part(2),
                  pl.BlockSpec((CONV_K, tc), lambda i: (0, i)), pl.BlockSpec((1, tc), lambda i: (0, i))],
        out_specs=pl.BlockSpec((s, tc), lambda i: (0, i)),
        out_shape=jax.ShapeDtypeStruct((s, sw), BF16),
        compiler_params=_cp(1),
    )(proj, proj, proj, conv_w, conv_b)


def _pool_counts(shape, window):
    t = lax.broadcasted_iota(jnp.int32, shape, 0)
    return jnp.minimum(t + 1, window).astype(F32)


def _pool_fwd(proj):
    _, s, sw = proj.shape
    gw = sw // len(POOL_WINDOWS)

    def body(v_ref, p_ref):
        for gi, window in enumerate(POOL_WINDOWS):
            @pl.when(pl.program_id(0) == gi)
            def _():
                v = v_ref[...]
                acc, k = v, 1
                while k < window:
                    acc = acc + _shift_down(acc, k)
                    k *= 2
                p_ref[...] = (acc / _pool_counts(v.shape, window) - v).astype(BF16)

    return pl.pallas_call(
        body, name="pool_fwd", grid=(len(POOL_WINDOWS),),
        in_specs=[pl.BlockSpec((None, s, gw), lambda g: (3, 0, g))],
        out_specs=pl.BlockSpec((s, gw), lambda g: (0, g)),
        out_shape=jax.ShapeDtypeStruct((s, sw), BF16),
        compiler_params=_cp(1),
    )(proj)


def _merge_fwd(z, wa, p, wpool, proj, b_gate2, pool_scale):
    s, sw = z.shape
    tn = wa.shape[2]
    d = tn * N_DEV
    gw = sw // len(POOL_WINDOWS)
    nq = sw // tn

    def body(z_ref, wa_ref, p_ref, wp_ref, ga_ref, gb_ref, bg_ref, sc_ref,
             m_ref, dya_ref, dyb_ref, dga_ref, dgb_ref, dsc_ref):
        for rs in _chunks(s, 512):
            ya = _dot(z_ref[rs, :], wa_ref[...], NN)
            yb = _dot(p_ref[rs, :], wp_ref[...], NN)
            sa = _sigmoid(ga_ref[rs, :] + bg_ref[0:1, :])
            sb = _sigmoid(gb_ref[rs, :] + bg_ref[1:2, :])
            sc = sc_ref[...]
            sb_yb = sb * yb
            m_ref[rs, :] = (sa * ya + sb_yb * sc).astype(BF16)
            dya_ref[rs, :] = sa.astype(BF16)
            dyb_ref[rs, :] = (sb * sc).astype(BF16)
            dga_ref[rs, :] = (ya * (sa * (1.0 - sa))).astype(BF16)
            dgb_ref[rs, :] = ((yb * sc) * (sb * (1.0 - sb))).astype(BF16)
            dsc_ref[rs, :] = sb_yb.astype(BF16)

    col = pl.BlockSpec((s, tn), lambda j: (0, j))
    out = jax.ShapeDtypeStruct((s, d), BF16)
    return pl.pallas_call(
        body, name="merge_fwd", grid=(N_DEV,),
        in_specs=[pl.BlockSpec((s, sw), lambda j: (0, 0)),
                  pl.BlockSpec((None, sw, tn), lambda j: (j, 0, 0)),
                  pl.BlockSpec((s, gw), lambda j: (0, j // 2)),
                  pl.BlockSpec((None, gw, tn), lambda j: (j // 2, 0, j % 2)),
                  pl.BlockSpec((None, s, tn), lambda j: (4 + j // nq, 0, j % nq)),
                  pl.BlockSpec((None, s, tn), lambda j: (6 + j // nq, 0, j % nq)),
                  pl.BlockSpec((2, tn), lambda j: (0, j)),
                  pl.BlockSpec((1, tn), lambda j: (0, j))],
        out_specs=[col] * 6,
        out_shape=[out] * 6,
        compiler_params=_cp(1),
    )(z, wa, p, wpool, proj, proj, b_gate2, pool_scale)


def _wo_fwd(merged, wo, x2d, g2):
    s, d = x2d.shape
    tm = min(256, s)

    def body(m_ref, wo_ref, x_ref, g_ref, x1_ref, h2_ref):
        x1 = x_ref[...] + _dot(m_ref[...], wo_ref[...], NN)
        x1_ref[...] = x1
        r = lax.rsqrt(jnp.mean(x1 * x1, axis=-1, keepdims=True) + EPS)
        h2_ref[...] = (x1 * r * g_ref[...]).astype(BF16)

    row = pl.BlockSpec((tm, d), lambda i: (i, 0))
    return pl.pallas_call(
        body, name="wo_fwd", grid=(s // tm,),
        in_specs=[row, pl.BlockSpec((d, d), lambda i: (0, 0)), row, pl.BlockSpec((1, d), lambda i: (0, 0))],
        out_specs=[row, row],
        out_shape=[jax.ShapeDtypeStruct((s, d), F32), jax.ShapeDtypeStruct((s, d), BF16)],
        compiler_params=_cp(1),
    )(merged, wo, x2d, g2)


def _ffn_gate_fwd(h2, wg_g):
    s, d = h2.shape
    f8 = wg_g.shape[2]
    th = min(1024, s)

    def body(h_ref, wg_ref, g_ref):
        i = pl.program_id(1)
        for rs in _chunks(th, 512):
            rows = pl.ds(pl.multiple_of(i * th + rs.start, rs.stop - rs.start), rs.stop - rs.start)
            g_ref[rs, :] = _dot(h_ref[rows, :], wg_ref[...], NN).astype(BF16)

    return pl.pallas_call(
        body, name="ffn_gate_fwd", grid=(N_DEV, s // th),
        in_specs=[pl.BlockSpec((s, d), lambda j, i: (0, 0)), pl.BlockSpec((None, d, f8), lambda j, i: (j, 0, 0))],
        out_specs=pl.BlockSpec((None, th, f8), lambda j, i: (j, i, 0)),
        out_shape=jax.ShapeDtypeStruct((N_DEV, s, f8), BF16),
        compiler_params=_cp(2),
    )(h2, wg_g)


def _ffn_up_act_fwd(h2, wu_g, gate):
    s, d = h2.shape
    f8 = wu_g.shape[2]
    th = min(1024, s)

    def body(h_ref, wu_ref, g_ref, dadu_ref, dadg_ref, a_ref, u_ref):
        i = pl.program_id(1)
        chunks = _chunks(th, 256)

        def matmul(rs):
            rows = pl.ds(pl.multiple_of(i * th + rs.start, rs.stop - rs.start), rs.stop - rs.start)
            u_ref[rs, :] = _dot(h_ref[rows, :], wu_ref[...], NN)

        matmul(chunks[0])
        for k, rs in enumerate(chunks):
            if k + 1 < len(chunks):
                matmul(chunks[k + 1])
            g = g_ref[rs, :].astype(F32)
            u = u_ref[rs, :]
            sg = _sigmoid(g)
            silu = g * sg
            dadu_ref[rs, :] = silu.astype(BF16)
            dadg_ref[rs, :] = (u * (sg * (1.0 + g * (1.0 - sg)))).astype(BF16)
            a_ref[rs, :] = (silu * u).astype(BF16)

    wspec = pl.BlockSpec((None, d, f8), lambda j, i: (j, 0, 0))
    ospec = pl.BlockSpec((None, th, f8), lambda j, i: (j, i, 0))
    out = jax.ShapeDtypeStruct((N_DEV, s, f8), BF16)
    return pl.pallas_call(
        body, name="ffn_up_fwd", grid=(N_DEV, s // th),
        in_specs=[pl.BlockSpec((s, d), lambda j, i: (0, 0)), wspec, ospec],
        out_specs=[ospec, ospec, ospec], out_shape=[out, out, out],
        scratch_shapes=[pltpu.VMEM((th, f8), F32)],
        compiler_params=_cp(2),
    )(h2, wu_g, gate)


def _ffn_down_fwd(act, wd_part, name):
    _, s, f8 = act.shape
    tn = wd_part.shape[2]
    per = 2

    def body(a_ref, wd_ref, o_ref):
        @pl.when(pl.program_id(0) == 0)
        def _():
            o_ref[...] = jnp.zeros_like(o_ref)

        for rs in _chunks(s, 1024):
            part = _dot(a_ref[0, rs, :], wd_ref[0], NN)
            for q in range(1, per):
                part = part + _dot(a_ref[q, rs, :], wd_ref[q], NN)
            o_ref[rs, :] += part

    return pl.pallas_call(
        body, name=name, grid=(N_DEV // per,),
        in_specs=[pl.BlockSpec((per, s, f8), lambda j: (j, 0, 0)),
                  pl.BlockSpec((per, f8, tn), lambda j: (j, 0, 0))],
        out_specs=pl.BlockSpec((s, tn), lambda j: (0, 0)),
        out_shape=jax.ShapeDtypeStruct((s, tn), F32),
        compiler_params=_cp(1),
    )(act, wd_part)


def _loss_bwd(ffn_parts, x1, target, final_g):
    s, d = x1.shape
    tm = min(256, s)
    nparts = len(ffn_parts)

    def body(*refs):
        f_refs = refs[:nparts]
        x1_ref, t_ref, gf_ref, dxb_ref, dgf_ref, loss_ref = refs[nparts:]

        @pl.when(pl.program_id(0) == 0)
        def _():
            dgf_ref[...] = jnp.zeros_like(dgf_ref)
            loss_ref[...] = jnp.zeros_like(loss_ref)

        x2 = x1_ref[...] + jnp.concatenate([f_ref[...] for f_ref in f_refs], axis=-1)
        r = lax.rsqrt(jnp.mean(x2 * x2, axis=-1, keepdims=True) + EPS)
        nrm = x2 * r
        gf = gf_ref[...]
        err = nrm * gf - t_ref[...]
        loss_ref[...] += jnp.sum(err * err) * (0.5 / d)
        dy = err * (1.0 / d)
        dgf_ref[...] += jnp.sum(dy * nrm, axis=0, keepdims=True)
        dn = dy * gf
        dx = r * (dn - nrm * jnp.mean(dn * nrm, axis=-1, keepdims=True))
        dxb_ref[...] = dx.astype(BF16)

    row = pl.BlockSpec((tm, d), lambda i: (i, 0))
    vec = pl.BlockSpec((1, d), lambda i: (0, 0))
    return pl.pallas_call(
        body, name="loss_bwd", grid=(s // tm,),
        in_specs=[pl.BlockSpec((tm, f.shape[1]), lambda i: (i, 0)) for f in ffn_parts] + [row, row, vec],
        out_specs=[row, vec, pl.BlockSpec((8, LANES), lambda i: (0, 0))],
        out_shape=[jax.ShapeDtypeStruct((s, d), BF16),
                   jax.ShapeDtypeStruct((1, d), F32), jax.ShapeDtypeStruct((8, LANES), F32)],
        compiler_params=_cp(1),
    )(*ffn_parts, x1, target, final_g)


def _ffn_gate_bwd(dx2b, wd_parts, dadg, dadu):
    s, d = dx2b.shape
    f8 = dadg.shape[2]
    th = min(1024, s)
    nparts = len(wd_parts)
    pc = d // nparts

    ni = s // th
    steps = N_DEV * ni
    slots = 3

    def body(dx_ref, *refs):
        wd_refs = refs[:nparts]
        g_hbm, u_hbm, dg_ref, du_ref, da_ref, g_buf, u_buf, sems = refs[nparts:]
        i = pl.program_id(1)
        t = pl.program_id(0) * ni + i

        def fetch(step):
            slot = step % slots
            rows = pl.ds(pl.multiple_of((step % ni) * th, th), th)
            return [pltpu.make_async_copy(src.at[step // ni, rows, :], buf.at[slot], sems.at[a, slot])
                    for a, (src, buf) in enumerate(((g_hbm, g_buf), (u_hbm, u_buf)))]

        @pl.when(t == 0)
        def _():
            for step in range(slots - 1):
                for cp in fetch(step):
                    cp.start()

        @pl.when(t + slots - 1 < steps)
        def _():
            for cp in fetch(t + slots - 1):
                cp.start()

        for cp in fetch(t):
            cp.wait()
        slot = t % slots
        g_ref, u_ref = g_buf.at[slot], u_buf.at[slot]
        chunks = _chunks(th, 256)

        def matmul(rs):
            rows = pl.ds(pl.multiple_of(i * th + rs.start, rs.stop - rs.start), rs.stop - rs.start)
            part = None
            for q, wd_ref in enumerate(wd_refs):
                term = _dot(dx_ref[rows, q * pc:(q + 1) * pc], wd_ref[...], NT)
                part = term if part is None else part + term
            da_ref[rs, :] = part

        matmul(chunks[0])
        for k, rs in enumerate(chunks):
            if k + 1 < len(chunks):
                matmul(chunks[k + 1])
            da = da_ref[rs, :].astype(BF16)
            dg_ref[rs, :] = da * g_ref[rs, :]
            du_ref[rs, :] = da * u_ref[rs, :]

    aspec = pl.BlockSpec((None, th, f8), lambda j, i: (j, i, 0))
    out = jax.ShapeDtypeStruct((N_DEV, s, f8), BF16)
    return pl.pallas_call(
        body, name="ffn_act_bwd", grid=(N_DEV, s // th),
        in_specs=[pl.BlockSpec((s, d), lambda j, i: (0, 0))]
        + [pl.BlockSpec((None, f8, pc), lambda j, i: (j, 0, 0))] * nparts + [pl.BlockSpec(memory_space=pl.ANY)] * 2,
        out_specs=[aspec, aspec], out_shape=[out, out],
        scratch_shapes=[pltpu.VMEM((th, f8), F32), pltpu.VMEM((slots, th, f8), BF16),
                        pltpu.VMEM((slots, th, f8), BF16), pltpu.SemaphoreType.DMA((2, slots))],
        compiler_params=_cp(2),
    )(dx2b, *wd_parts, dadg, dadu)


def _wgrad_rows(a3, b, name, after=(), carry=()):
    _, s, k = a3.shape
    n = b.shape[1]
    nc = len(carry)
    c_in, c_out, c_shape, c_sems = _carry_specs(carry)

    def body(a_ref, b_ref, *rest):
        rest = rest[len(after):]
        o_ref = rest[nc]
        j = pl.program_id(0)
        _carry_run(j == 0, j == N_DEV - 1, rest[:nc], rest[nc + 1:2 * nc + 1], rest[2 * nc + 1:])
        o_ref[...] = _dot(a_ref[...], b_ref[...], TN).astype(BF16)

    outs = pl.pallas_call(
        body, name=name, grid=(N_DEV,),
        in_specs=[pl.BlockSpec((None, s, k), lambda j: (j, 0, 0)),
                  pl.BlockSpec((s, n), lambda j: (0, 0))] + _after_specs(after) + c_in,
        out_specs=[pl.BlockSpec((None, k, n), lambda j: (j, 0, 0))] + c_out,
        out_shape=[jax.ShapeDtypeStruct((N_DEV, k, n), BF16)] + c_shape,
        scratch_shapes=c_sems,
        compiler_params=_cp_carry(1, carry),
    )(a3, b, *after, *carry)
    return (outs[0], list(outs[1:])) if nc else outs[0]


def _wgrad_cols(a, b3, name, after=()):
    s, k = a.shape
    if b3.ndim == 2:
        n = b3.shape[1] // N_DEV
        b_spec = pl.BlockSpec((s, n), lambda j: (0, j))
    else:
        n = b3.shape[2]
        b_spec = pl.BlockSpec((None, s, n), lambda j: (j, 0, 0))

    def body(a_ref, b_ref, *rest):
        o_ref = rest[len(after)]
        o_ref[...] = _dot(a_ref[...], b_ref[...], TN).astype(BF16)

    return pl.pallas_call(
        body, name=name, grid=(N_DEV,),
        in_specs=[pl.BlockSpec((s, k), lambda j: (0, 0)), b_spec] + _after_specs(after),
        out_specs=pl.BlockSpec((None, k, n), lambda j: (j, 0, 0)),
        out_shape=jax.ShapeDtypeStruct((N_DEV, k, n), BF16),
        compiler_params=_cp(1),
    )(a, b3, *after)


def _input_grad(pairs, name, after=(), carry=(), per=1):
    s = pairs[0][0].shape[1]
    d = pairs[0][1].shape[1]
    tn = min(1024, d)
    npair = len(pairs)
    nc = len(carry)
    c_in, c_out, c_shape, c_sems = _carry_specs(carry)

    def body(*refs):
        ops = refs[:2 * npair]
        rest = refs[2 * npair + len(after):]
        o_ref, acc_ref = rest[nc], rest[-1]
        nh, j = pl.program_id(0), pl.program_id(1)
        last_j = N_DEV // per - 1
        _carry_run((nh == 0) & (j == 0), (nh == d // tn - 1) & (j == last_j),
                   rest[:nc], rest[nc + 1:2 * nc + 1], rest[2 * nc + 1:-1])

        @pl.when(j == 0)
        def _():
            acc_ref[...] = jnp.zeros_like(acc_ref)

        for rs in _chunks(s, 1024):
            part = None
            for q in range(npair):
                for e in range(per):
                    term = _dot(ops[2 * q][e, rs, :], ops[2 * q + 1][e], NT)
                    part = term if part is None else part + term
            acc_ref[rs, :] += part

        @pl.when(j == last_j)
        def _():
            o_ref[...] = acc_ref[...].astype(BF16)

    in_specs, args = [], []
    for a3, w3 in pairs:
        k = a3.shape[2]
        in_specs += [pl.BlockSpec((per, s, k), lambda n, j: (j, 0, 0)),
                     pl.BlockSpec((per, tn, k), lambda n, j: (j, n, 0))]
        args += [a3, w3]
    outs = pl.pallas_call(
        body, name=name, grid=(d // tn, N_DEV // per),
        in_specs=in_specs + _after_specs(after) + c_in,
        out_specs=[pl.BlockSpec((s, tn), lambda n, j: (0, n))] + c_out,
        out_shape=[jax.ShapeDtypeStruct((s, d), BF16)] + c_shape,
        scratch_shapes=c_sems + [pltpu.VMEM((s, tn), F32)],
        compiler_params=_cp_carry(2, carry),
    )(*args, *after, *carry)
    return (outs[0], list(outs[1:])) if nc else outs[0]


def _rms_bwd(dh, xres, g, dres, name, with_bf16=True):
    s, d = xres.shape
    tm = min(256, s)

    def body(dh_ref, x_ref, g_ref, dres_ref, dx_ref, *rest):
        dg_ref = rest[-1]
        @pl.when(pl.program_id(0) == 0)
        def _():
            dg_ref[...] = jnp.zeros_like(dg_ref)

        xv = x_ref[...]
        dh_v = dh_ref[...].astype(F32)
        r = lax.rsqrt(jnp.mean(xv * xv, axis=-1, keepdims=True) + EPS)
        nrm = xv * r
        dg_ref[...] += jnp.sum(dh_v * nrm, axis=0, keepdims=True)
        dn = dh_v * g_ref[...]
        dx = dres_ref[...].astype(F32) + r * (dn - nrm * jnp.mean(dn * nrm, axis=-1, keepdims=True))
        dx_ref[...] = dx
        if with_bf16:
            rest[0][...] = dx.astype(BF16)

    row = pl.BlockSpec((tm, d), lambda i: (i, 0))
    vec = pl.BlockSpec((1, d), lambda i: (0, 0))
    copies = [jax.ShapeDtypeStruct((s, d), BF16)] if with_bf16 else []
    outs = pl.pallas_call(
        body, name=name, grid=(s // tm,),
        in_specs=[row, row, vec, row],
        out_specs=[row] + [row] * len(copies) + [vec],
        out_shape=[jax.ShapeDtypeStruct((s, d), F32)] + copies + [jax.ShapeDtypeStruct((1, d), F32)],
        compiler_params=_cp(1),
    )(dh, xres, g, dres)
    return (outs[0], outs[1], outs[2]) if with_bf16 else (outs[0], None, outs[1])


def _wgrad_full(a, b, name, after=(), carry=()):
    s, k = a.shape
    n = b.shape[1]
    tk = min(512, k)
    nc = len(carry)
    c_in, c_out, c_shape, c_sems = _carry_specs(carry)

    def body(a_ref, b_ref, *rest):
        rest = rest[len(after):]
        o_ref = rest[nc]
        j = pl.program_id(0)
        _carry_run(j == 0, j == k // tk - 1, rest[:nc], rest[nc + 1:2 * nc + 1], rest[2 * nc + 1:])
        o_ref[...] = _dot(a_ref[...], b_ref[...], TN).astype(BF16)

    outs = pl.pallas_call(
        body, name=name, grid=(k // tk,),
        in_specs=[pl.BlockSpec((s, tk), lambda j: (0, j)),
                  pl.BlockSpec((s, n), lambda j: (0, 0))] + _after_specs(after) + c_in,
        out_specs=[pl.BlockSpec((tk, n), lambda j: (j, 0))] + c_out,
        out_shape=[jax.ShapeDtypeStruct((k, n), BF16)] + c_shape,
        scratch_shapes=c_sems,
        compiler_params=_cp_carry(1, carry),
    )(a, b, *after, *carry)
    return (outs[0], list(outs[1:])) if nc else outs[0]


def _wgrad_pool(p, dyb, n_groups):
    s, sw = p.shape
    d = dyb.shape[1]
    gw, go = sw // n_groups, d // n_groups
    ts = min(512, s)
    ns = s // ts

    def body(a_ref, b_ref, o_ref, acc_ref):
        i = pl.program_id(1)

        @pl.when(i == 0)
        def _():
            acc_ref[...] = jnp.zeros_like(acc_ref)

        acc_ref[...] += _dot(a_ref[...], b_ref[...], TN)

        @pl.when(i == ns - 1)
        def _():
            o_ref[...] = acc_ref[...].astype(BF16)

    return pl.pallas_call(
        body, name="wgrad_pool", grid=(n_groups, ns),
        in_specs=[pl.BlockSpec((ts, gw), lambda g, i: (i, g)),
                  pl.BlockSpec((ts, go), lambda g, i: (i, g))],
        out_specs=pl.BlockSpec((None, gw, go), lambda g, i: (g, 0, 0)),
        out_shape=jax.ShapeDtypeStruct((n_groups, gw, go), BF16),
        scratch_shapes=[pltpu.VMEM((gw, go), F32)],
        compiler_params=_cp(2),
    )(p, dyb)


def _wo_bwd(dx1b, wo, factors, sw, after=()):
    s, d = dx1b.shape
    tn = d // N_DEV
    nq = sw // tn

    def body(dx_ref, wo_ref, fya_ref, fyb_ref, fga_ref, fgb_ref, fsc_ref, *rest):
        dya_ref, dyb_ref, dp_ref, dbg_ref, dsc_ref, dm_ref = rest[len(after):]
        dbg_ref[...] = jnp.zeros_like(dbg_ref)
        dsc_ref[...] = jnp.zeros_like(dsc_ref)
        for rs in _chunks(s, 1024):
            dm_ref[rs, :] = _dot(dx_ref[rs, :], wo_ref[...], NT)
        for rs in _chunks(s, 256):
            dm = dm_ref[rs, :]
            dya_ref[rs, :] = (dm * fya_ref[rs, :].astype(F32)).astype(BF16)
            dyb_ref[rs, :] = (dm * fyb_ref[rs, :].astype(F32)).astype(BF16)
            dsc_ref[...] += jnp.sum(dm * fsc_ref[rs, :].astype(F32), axis=0, keepdims=True)
            dga = dm * fga_ref[rs, :].astype(F32)
            dgb = dm * fgb_ref[rs, :].astype(F32)
            dp_ref[0, rs, :] = dga.astype(BF16)
            dp_ref[1, rs, :] = dgb.astype(BF16)
            dbg_ref[0:1, :] += jnp.sum(dga, axis=0, keepdims=True)
            dbg_ref[1:2, :] += jnp.sum(dgb, axis=0, keepdims=True)

    col = pl.BlockSpec((s, tn), lambda j: (0, j))
    out = jax.ShapeDtypeStruct((s, d), BF16)
    return pl.pallas_call(
        body, name="wo_bwd", grid=(N_DEV,),
        in_specs=[pl.BlockSpec((s, d), lambda j: (0, 0)),
                  pl.BlockSpec((tn, d), lambda j: (j, 0))] + [col] * 5 + _after_specs(after),
        out_specs=[col, col,
                   pl.BlockSpec((2, None, s, tn), lambda j: (1, j // nq, 0, j % nq)),
                   pl.BlockSpec((2, tn), lambda j: (0, j)),
                   pl.BlockSpec((1, tn), lambda j: (0, j))],
        out_shape=[out, out, jax.ShapeDtypeStruct((4, 2, s, sw), BF16),
                   jax.ShapeDtypeStruct((2, d), F32), jax.ShapeDtypeStruct((1, d), F32)],
        scratch_shapes=[pltpu.VMEM((s, tn), F32)],
        compiler_params=_cp(1),
    )(dx1b, wo, *factors, *after)


def _conv_bwd(dproj, dya, wa, proj, conv_w, conv_b):
    s, d = dya.shape
    sw, tn = wa.shape[1], wa.shape[2]
    tc = min(LANES, sw)

    def body(dproj_hbm, dya_ref, wa_ref, ba_ref, ca_ref, va_ref, cw_ref, cb_ref,
             dp_ref, dcw_ref, dcb_ref, dz_ref):
        del dproj_hbm
        for rs in _chunks(s, 512):
            part = _dot(dya_ref[rs, 0:tn], wa_ref[0], NT)
            for j in range(1, N_DEV):
                part = part + _dot(dya_ref[rs, j * tn:(j + 1) * tn], wa_ref[j], NT)
            dz_ref[rs, :] = part
        dz = dz_ref[...]
        ba, ca, va = ba_ref[...], ca_ref[...], va_ref[...]
        cv = ca * va
        cv1, cv2 = _shift_down(cv, 1), _shift_down(cv, 2)
        w0, w1, w2 = cw_ref[0:1, :], cw_ref[1:2, :], cw_ref[2:3, :]
        u = cb_ref[...] + w0 * cv2 + w1 * cv1 + w2 * cv
        du = dz * ba
        dp_ref[0] = (dz * u).astype(BF16)
        dcv = w2 * du + w1 * _shift_up(du, 1) + w0 * _shift_up(du, 2)
        dp_ref[1] = (dcv * va).astype(BF16)
        dp_ref[2] = (dcv * ca).astype(BF16)
        dcw_ref[0:1, :] = jnp.sum(du * cv2, axis=0, keepdims=True)
        dcw_ref[1:2, :] = jnp.sum(du * cv1, axis=0, keepdims=True)
        dcw_ref[2:3, :] = jnp.sum(du * cv, axis=0, keepdims=True)
        dcb_ref[...] = jnp.sum(du, axis=0, keepdims=True)

    def part(k):
        return pl.BlockSpec((None, s, tc), lambda i: (k, 0, i))

    return pl.pallas_call(
        body, name="conv_bwd", grid=(sw // tc,),
        in_specs=[pl.BlockSpec(memory_space=pl.ANY),
                  pl.BlockSpec((s, d), lambda i: (0, 0)),
                  pl.BlockSpec((N_DEV, tc, tn), lambda i: (0, i, 0)),
                  part(0), part(1), part(2),
                  pl.BlockSpec((CONV_K, tc), lambda i: (0, i)), pl.BlockSpec((1, tc), lambda i: (0, i))],
        out_specs=[pl.BlockSpec((3, s, tc), lambda i: (0, 0, i)),
                   pl.BlockSpec((CONV_K, tc), lambda i: (0, i)), pl.BlockSpec((1, tc), lambda i: (0, i))],
        out_shape=[jax.ShapeDtypeStruct(dproj.shape, BF16),
                   jax.ShapeDtypeStruct((CONV_K, sw), F32), jax.ShapeDtypeStruct((1, sw), F32)],
        scratch_shapes=[pltpu.VMEM((s, tc), F32)],
        input_output_aliases={0: 0},
        compiler_params=_cp(1),
    )(dproj, dya, wa, proj, proj, proj, conv_w, conv_b)


def _pool_bwd(dproj, dyb, wpool):
    s, d = dyb.shape
    n_groups, gw, go = wpool.shape

    def body(dproj_hbm, dyb_ref, wp_ref, dp_ref):
        del dproj_hbm
        for gi, window in enumerate(POOL_WINDOWS):
            @pl.when(pl.program_id(0) == gi)
            def _():
                dpool = _dot(dyb_ref[...], wp_ref[...], NT)
                acc, k = dpool / _pool_counts(dpool.shape, window), 1
                while k < window:
                    acc = acc + _shift_up(acc, k)
                    k *= 2
                dp_ref[...] = (acc - dpool).astype(BF16)

    return pl.pallas_call(
        body, name="pool_bwd", grid=(n_groups,),
        in_specs=[pl.BlockSpec(memory_space=pl.ANY),
                  pl.BlockSpec((s, go), lambda g: (0, g)),
                  pl.BlockSpec((None, gw, go), lambda g: (g, 0, 0))],
        out_specs=pl.BlockSpec((None, s, gw), lambda g: (3, 0, g)),
        out_shape=jax.ShapeDtypeStruct(dproj.shape, BF16),
        input_output_aliases={0: 0},
        compiler_params=_cp(1),
    )(dproj, dyb, wpool)


def _rows128(v):
    return v.reshape(-1, LANES)


def kernel(x, norm1_g, w_in, b_gate, conv_w, conv_b, w_a_out, w_pool, pool_scale, w_o, norm2_g, w_ffn_gate, w_ffn_up, w_ffn_down, final_g, loss_target, m_norm1_g, m_w_in, m_b_gate, m_conv_w, m_conv_b, m_w_a_out, m_w_pool, m_pool_scale, m_w_o, m_norm2_g, m_w_ffn_gate, m_w_ffn_up, m_w_ffn_down, m_final_g, v_norm1_g, v_w_in, v_b_gate, v_conv_w, v_conv_b, v_w_a_out, v_w_pool, v_pool_scale, v_w_o, v_norm2_g, v_w_ffn_gate, v_w_ffn_up, v_w_ffn_down, v_final_g):
    s, d = x.shape[1], x.shape[2]
    sw = w_in.shape[2]
    n_groups = w_pool.shape[1]
    gw = w_pool.shape[2]
    go = w_pool.shape[3] * N_DEV
    f8 = w_ffn_gate.shape[2]
    cws = conv_w.shape[2]
    assert sw == conv_w.shape[2] * N_DEV == gw * n_groups and go * n_groups == d and n_groups == len(POOL_WINDOWS)

    xi, yi, ci = _coords()
    me = 4 * xi + 2 * yi + ci
    my_chip = 2 * xi + yi

    x2d = x.reshape(s, d)
    target = loss_target.reshape(s, d)
    final_g2 = final_g.reshape(1, d)
    b_gate2 = b_gate.reshape(2, d)

    big_names = ["w_in", "w_a_out", "w_pool", "w_o", "w_ffn_gate", "w_ffn_up", "w_ffn_down"]
    big_w = [w_in, w_a_out, w_pool, w_o, w_ffn_gate, w_ffn_up, w_ffn_down]
    big_m = [m_w_in, m_w_a_out, m_w_pool, m_w_o, m_w_ffn_gate, m_w_ffn_up, m_w_ffn_down]
    big_v = [v_w_in, v_w_a_out, v_w_pool, v_w_o, v_w_ffn_gate, v_w_ffn_up, v_w_ffn_down]
    shapes2d = [(w.size // w.shape[-1], w.shape[-1]) for w in big_w]
    big_w2 = [w.reshape(sh) for w, sh in zip(big_w, shapes2d)]
    transposed = (4, 5)

    def view2d(t, a):
        t2 = t.reshape(shapes2d[a])
        return t2.T if a in transposed else t2

    def unview(o, a):
        return (o.T if a in transposed else o).reshape(big_w[a].shape)

    sb = [_cast_bf16(w, "cast_" + nm, parts=2 if nm == "w_ffn_down" else 1) for w, nm in zip(big_w2, big_names)]
    win_g, wa_g, wpool_g, wo_g = _allgather_big([b[0] for b in sb[0:4]], "allgather_mixer", COLLECTIVE_GATHER)
    (wg_g,) = _allgather_big(sb[4], "allgather_ffn_gate", COLLECTIVE_GATHER)
    (wu_g,) = _allgather_big(sb[5], "allgather_ffn_up", COLLECTIVE_GATHER)
    wd_parts = [_allgather_big([part], "allgather_ffn_down_%d" % q, COLLECTIVE_GATHER)[0]
                for q, part in enumerate(sb[6])]
    convw_g = _allgather_small(jnp.pad(conv_w.reshape(CONV_K, cws), ((0, 8 - CONV_K), (0, 0))), "allgather_conv_w")
    conv_w_full = convw_g[:, :CONV_K, :].transpose(1, 0, 2).reshape(CONV_K, sw)
    wpool = wpool_g.reshape(N_DEV, n_groups, gw, go // N_DEV).transpose(1, 2, 0, 3).reshape(n_groups, gw, go)
    wo = wo_g.reshape(d, d)

    h = _rms_fwd(x2d, norm1_g)
    proj = _proj_fwd(h, win_g)
    z = _conv_fwd(proj, conv_w_full, conv_b)
    p = _pool_fwd(proj)
    merged, *merge_factors = _merge_fwd(z, wa_g, p, wpool, proj, b_gate2, pool_scale)
    x1, h2 = _wo_fwd(merged, wo, x2d, norm2_g)
    gate = _ffn_gate_fwd(h2, wg_g)
    dadu, dadg, act = _ffn_up_act_fwd(h2, wu_g, gate)
    ffn_parts = [_ffn_down_fwd(act, wd, "ffn_down_fwd_%d" % q) for q, wd in enumerate(wd_parts)]
    dx2b, d_final_g, loss_blk = _loss_bwd(ffn_parts, x1, target, final_g2)

    other_chips = jnp.stack([2 * (1 - xi) + yi, 2 * xi + (1 - yi), 2 * (1 - xi) + (1 - yi)])
    others = jnp.concatenate([other_chips, 2 * other_chips + ci]).astype(jnp.int32)

    def partials(grads, recvs, names):
        if all(g.shape == grads[0].shape for g in grads):
            return list(_chip_partial(others, grads, recvs, "chip_partial_" + names[0]))
        return [_chip_partial(others, [g3], [r], "chip_partial_" + nm)[0] for g3, r, nm in zip(grads, recvs, names)]

    own = jnp.stack([me, my_chip]).astype(jnp.int32)

    def adam(idx, g3s, sibs, chipss):
        wmvs = [(view2d(big_w[a], a), view2d(big_m[a], a), view2d(big_v[a], a)) for a in idx]
        outs = _adam_big(own, wmvs, g3s, sibs, chipss, "adam_" + big_names[idx[0]])
        for a, o4 in zip(idx, outs):
            big_out[a] = [unview(o, a) for o in o4]

    big_out = [None] * len(big_names)
    dg_act, du_act = _ffn_gate_bwd(dx2b, wd_parts, dadg, dadu)
    gw_gate = _wgrad_rows(dg_act, h2, "wgrad_ffn_gate")
    gw_up = _wgrad_rows(du_act, h2, "wgrad_ffn_up")
    gw_down, sib_gu = _wgrad_rows(act, dx2b, "wgrad_ffn_down", carry=[gw_gate, gw_up])
    ps_gu = partials([gw_gate, gw_up], sib_gu, ["w_ffn_gate", "w_ffn_up"])
    chips_gu = _exchange_chips(ps_gu, "rs_chips_ffn_up", COLLECTIVE_CHIPS)
    dh2, sib_down = _input_grad([(dg_act, wg_g), (du_act, wu_g)], "ffn_in_bwd", after=ps_gu, carry=[gw_down])
    ps_down = partials([gw_down], sib_down, ["w_ffn_down"])
    chips_down = _exchange_chips(ps_down, "rs_chips_ffn_down", COLLECTIVE_CHIPS)
    dx1, dx1b, d_norm2_g = _rms_bwd(dh2, x1, norm2_g, dx2b, "rms2_bwd")
    dya, dyb, dproj42, d_b_gate, d_pool_scale = _wo_bwd(dx1b, wo, merge_factors, sw, after=ps_down)
    dproj = dproj42.reshape(N_DEV, s, sw)
    dproj, d_conv_w, d_conv_b = _conv_bwd(dproj, dya, wa_g, proj, conv_w_full, conv_b)
    dproj = _pool_bwd(dproj, dyb, wpool)
    gw_in = _wgrad_cols(h, dproj, "wgrad_in")
    gw_o, sib_in = _wgrad_full(merged, dx1b, "wgrad_o", carry=[gw_in])
    ps_in = partials([gw_in], sib_in, ["w_in"])
    chips_in = _exchange_chips(ps_in, "rs_chips_w_in", COLLECTIVE_CHIPS)
    gw_a = _wgrad_cols(z, dya, "wgrad_a_out", after=ps_in)
    gw_pool = _wgrad_pool(p, dyb, n_groups)
    mix3 = [gw_a,
            gw_pool.reshape(n_groups, gw, N_DEV, go // N_DEV).transpose(2, 0, 1, 3).reshape(N_DEV, n_groups * gw, go // N_DEV),
            gw_o.reshape(N_DEV, d // N_DEV, d)]
    adam([4, 5, 6], [gw_gate, gw_up, gw_down], sib_gu + sib_down, chips_gu + chips_down)
    dh, sib_mix = _input_grad([(dproj, win_g)], "proj_in_bwd", after=[big_out[6][0]], carry=mix3, per=2)
    ps_mix = partials(mix3, sib_mix, ["w_a_out", "w_pool", "w_o"])
    chips_mix = _exchange_chips(ps_mix, "rs_chips_mixer", COLLECTIVE_CHIPS)
    grad_x, _, d_norm1_g = _rms_bwd(dh, x2d, norm1_g, dx1, "rms1_bwd", with_bf16=False)
    adam([0], [gw_in], sib_in, chips_in)
    for k in range(3):
        adam([1 + k], [mix3[k]], [sib_mix[k]], [chips_mix[k]])

    small_parts = [d_norm1_g, d_b_gate, d_conv_w, d_conv_b, d_pool_scale, d_norm2_g, d_final_g, loss_blk]
    rows = [v.size // LANES for v in small_parts]
    row0 = [sum(rows[:k]) for k in range(len(rows))]
    packed = jnp.concatenate([_rows128(v) for v in small_parts], axis=0)
    gathered = _allgather_small(packed, "allgather_small_grads")
    small_names = ["norm1_g", "b_gate", "conv_b", "pool_scale", "norm2_g", "final_g", "conv_w"]
    small_w = [norm1_g, b_gate, conv_b, pool_scale, norm2_g, final_g]
    small_m = [m_norm1_g, m_b_gate, m_conv_b, m_pool_scale, m_norm2_g, m_final_g]
    small_v = [v_norm1_g, v_b_gate, v_conv_b, v_pool_scale, v_norm2_g, v_final_g]
    finished = _small_finish(gathered, [tuple(_rows128(t) for t in wmv) for wmv in zip(small_w, small_m, small_v)],
                             [row0[k] for k in (0, 1, 3, 4, 5, 6)], [(row0[2], rows[2]), (row0[7], rows[7])])
    g_convw_full, loss_rows = finished[0], finished[1]
    loss = loss_rows[0, 0]
    small_out = [[t.reshape(w.shape) for t in finished[2 + 4 * k:6 + 4 * k]] for k, w in enumerate(small_w)]
    g_convw = lax.dynamic_slice(g_convw_full.reshape(CONV_K, sw), (0, me * cws), (CONV_K, cws))
    cw_delta, cw_m, cw_v = _adam_small(conv_w.reshape(CONV_K, cws), g_convw,
                                       m_conv_w.reshape(CONV_K, cws), v_conv_w.reshape(CONV_K, cws))
    small_out.append([t.reshape(conv_w.shape) for t in (g_convw, cw_delta, cw_m, cw_v)])

    order = ["norm1_g", "w_in", "b_gate", "conv_w", "conv_b", "w_a_out", "w_pool", "pool_scale", "w_o", "norm2_g",
             "w_ffn_gate", "w_ffn_up", "w_ffn_down", "final_g"]
    per_kind = [{}, {}, {}, {}]
    for a, nm in enumerate(big_names):
        for kind in range(4):
            per_kind[kind][nm] = big_out[a][kind]
    for k, nm in enumerate(small_names):
        for kind in range(4):
            per_kind[kind][nm] = small_out[k][kind]
    result = [loss, grad_x.reshape(x.shape)]
    for kind in range(4):
        result += [per_kind[kind][nm] for nm in order]
    return tuple(result)
```

```python
import jax
import jax.numpy as jnp
from jax import lax
from jax.experimental import pallas as pl
from jax.experimental.pallas import tpu as pltpu
from jax.experimental.pallas import tpu_sc as plsc

F32 = jnp.float32
BF16 = jnp.bfloat16
MESH = pl.DeviceIdType.MESH

N_DEV = 8
EPS = 1e-6
CONV_K = 3
POOL_WINDOWS = (2, 4, 8, 16)
ADAM_LR = 0.001
ADAM_B1 = 0.9
ADAM_B2 = 0.999
ADAM_EPS = 1e-08
ADAM_WD = 0.01
ADAM_STEP = 10

V7X_VMEM_LIMIT_BYTES = 56 * 1024 * 1024
LANES = 128

COLLECTIVE_GATHER = 1
COLLECTIVE_SIBLING = 2
COLLECTIVE_CHIPS = 3
SEQUENCER_COST_BYTES = 4 * 10**9

NN = ((1,), (0,))
NT = ((1,), (1,))
TN = ((0,), (0,))


def _dot(a, b, dims):
    return lax.dot_general(a, b, (dims, ((), ())), preferred_element_type=F32)


def _cp(n_axes):
    return pltpu.CompilerParams(dimension_semantics=("arbitrary",) * n_axes,
                                vmem_limit_bytes=V7X_VMEM_LIMIT_BYTES)


def _row_tile(rows, bytes_per_row, cap_bytes):
    best = None
    for t in range(16, rows + 1, 16):
        if rows % t == 0 and t * bytes_per_row <= cap_bytes:
            best = t
    return best if best is not None else rows


def _chunks(total, size):
    size = min(size, total)
    assert total % size == 0
    return [slice(r, r + size) for r in range(0, total, size)]


def _after_specs(after):
    return [pl.BlockSpec(memory_space=pl.ANY)] * len(after)


def _shift_down(v, k):
    row = lax.broadcasted_iota(jnp.int32, v.shape, 0)
    return jnp.where(row >= k, pltpu.roll(v, k, 0), 0.0)


def _shift_up(v, k):
    n = v.shape[0]
    row = lax.broadcasted_iota(jnp.int32, v.shape, 0)
    return jnp.where(row < n - k, pltpu.roll(v, n - k, 0), 0.0)


def _sigmoid(v):
    return jax.nn.sigmoid(v)


def _cast_bf16(w2d, name, parts=1):
    rows, cols = w2d.shape
    tr = _row_tile(rows, cols * 4, 2 << 20)
    pc = cols // parts

    def body(i_ref, *o_refs):
        for q, o_ref in enumerate(o_refs):
            o_ref[...] = i_ref[:, q * pc:(q + 1) * pc].astype(BF16)

    return pl.pallas_call(
        body, name=name, grid=(rows // tr,),
        in_specs=[pl.BlockSpec((tr, cols), lambda i: (i, 0))],
        out_specs=[pl.BlockSpec((tr, pc), lambda i: (i, 0))] * parts,
        out_shape=[jax.ShapeDtypeStruct((rows, pc), BF16)] * parts,
        compiler_params=_cp(1),
    )(w2d)


def _rms_fwd(x2d, g):
    s, d = x2d.shape
    tm = min(256, s)

    def body(x_ref, g_ref, h_ref):
        xv = x_ref[...]
        r = lax.rsqrt(jnp.mean(xv * xv, axis=-1, keepdims=True) + EPS)
        h_ref[...] = (xv * r * g_ref[...]).astype(BF16)

    return pl.pallas_call(
        body, name="rms1_fwd", grid=(s // tm,),
        in_specs=[pl.BlockSpec((tm, d), lambda i: (i, 0)), pl.BlockSpec((1, d), lambda i: (0, 0))],
        out_specs=pl.BlockSpec((tm, d), lambda i: (i, 0)),
        out_shape=jax.ShapeDtypeStruct((s, d), BF16),
        compiler_params=_cp(1),
    )(x2d, g)


def _coords():
    return lax.axis_index("x"), lax.axis_index("y"), lax.axis_index("c")


def _slot(p):
    return 4 * p[0] + 2 * p[1] + p[2]


def _handshake(peers):
    barrier = pltpu.get_barrier_semaphore()
    for peer in peers:
        pl.semaphore_signal(barrier, inc=1, device_id=peer, device_id_type=MESH)
    pl.semaphore_wait(barrier, len(peers))


def _sequencer_call(body, out_type, scratch_types, name, collective_id):
    return pl.kernel(
        body, out_type=out_type, name=name,
        mesh=plsc.ScalarSubcoreMesh(axis_name="seq", num_cores=1),
        scratch_types=scratch_types,
        cost_estimate=pl.CostEstimate(flops=0, transcendentals=0, bytes_accessed=SEQUENCER_COST_BYTES),
        compiler_params=pltpu.CompilerParams(collective_id=collective_id))


def _allgather_big(shards, name, collective_id, after=()):
    n = len(shards)

    def body(*refs):
        ins, outs = refs[:n], refs[n + len(after):2 * n + len(after)]
        send_sems, recv_sems, local_sems = refs[2 * n + len(after):]
        x, y, c = _coords()
        me, sibling = (x, y, c), (x, y, 1 - c)
        x_nbr, y_nbr, diag = (1 - x, y), (x, 1 - y), (1 - x, 1 - y)
        relay_from = (x + (1 - c) * (1 - 2 * x), y + c * (1 - 2 * y))
        relay_to = (x + c * (1 - 2 * x), y + (1 - c) * (1 - 2 * y))
        _handshake([sibling, (*x_nbr, c), (*y_nbr, c)])

        def copy(a, k, block, to, src=None):
            dst = outs[a].at[_slot(block)]
            return pltpu.make_async_remote_copy(
                src_ref=dst if src is None else src, dst_ref=dst,
                send_sem=send_sems.at[a, k], recv_sem=recv_sems.at[a, k],
                device_id=to, device_id_type=MESH)

        mine, sends = [], []
        for a in range(n):
            cp = pltpu.make_async_copy(ins[a], outs[a].at[_slot(me)], local_sems.at[a])
            cp.start()
            mine.append(cp)
            first = [copy(a, 0, me, sibling, src=ins[a]),
                     copy(a, 1, me, (*x_nbr, c), src=ins[a]),
                     copy(a, 2, me, (*y_nbr, c), src=ins[a])]
            for cp in first:
                cp.start()
            sends += first
        for a in range(n):
            copy(a, 1 + c, (*relay_from, c), me).wait_recv()
            passed = [copy(a, 3, (*relay_from, c), (*relay_to, c)), copy(a, 4 + c, (*relay_from, c), sibling)]
            for cp in passed:
                cp.start()
            copy(a, 2 - c, (*relay_to, c), me).wait_recv()
            cp = copy(a, 5 - c, (*relay_to, c), sibling)
            cp.start()
            passed.append(cp)
            copy(a, 3, (*diag, c), me).wait_recv()
            cp = copy(a, 6, (*diag, c), sibling)
            cp.start()
            sends += passed + [cp]
        for a in range(n):
            copy(a, 0, sibling, me).wait_recv()
            copy(a, 4, (*x_nbr, 1 - c), me).wait_recv()
            copy(a, 5, (*y_nbr, 1 - c), me).wait_recv()
            copy(a, 6, (*diag, 1 - c), me).wait_recv()
        for cp in sends:
            cp.wait_send()
        for cp in mine:
            cp.wait()

    return _sequencer_call(
        body, [jax.ShapeDtypeStruct((N_DEV,) + s.shape, s.dtype) for s in shards],
        [pltpu.SemaphoreType.DMA((n, 7)), pltpu.SemaphoreType.DMA((n, 7)), pltpu.SemaphoreType.DMA((n,))],
        name, collective_id)(*shards, *after)


def _sibling_copies(ins, recvs, send_sems, recv_sems):
    x, y, c = _coords()
    return [pltpu.make_async_remote_copy(
        src_ref=ins[a].at[2 * q + (1 - c)], dst_ref=recvs[a].at[q],
        send_sem=send_sems.at[a, q], recv_sem=recv_sems.at[a, q],
        device_id=(x, y, 1 - c), device_id_type=MESH) for a in range(len(ins)) for q in range(4)]


def _carry_specs(carry):
    any_spec = pl.BlockSpec(memory_space=pl.ANY)
    n = len(carry)
    sems = [pltpu.SemaphoreType.DMA((n, 4)), pltpu.SemaphoreType.DMA((n, 4))] if n else []
    return ([any_spec] * n, [any_spec] * n,
            [jax.ShapeDtypeStruct((4,) + g.shape[1:], g.dtype) for g in carry], sems)


def _carry_run(first, last, ins, recvs, sems):
    if not ins:
        return

    @pl.when(first)
    def _():
        x, y, c = _coords()
        _handshake([(x, y, 1 - c)])
        for cp in _sibling_copies(ins, recvs, *sems):
            cp.start()

    @pl.when(last)
    def _():
        copies = _sibling_copies(ins, recvs, *sems)
        for cp in copies:
            cp.wait_recv()
        for cp in copies:
            cp.wait_send()


def _cp_carry(n_axes, carry):
    if not carry:
        return _cp(n_axes)
    return pltpu.CompilerParams(dimension_semantics=("arbitrary",) * n_axes, vmem_limit_bytes=V7X_VMEM_LIMIT_BYTES,
                                collective_id=COLLECTIVE_SIBLING)


def _exchange_chips(psums, name, collective_id):
    n = len(psums)

    def body(*refs):
        ins, outs = refs[:n], refs[n:2 * n]
        send_sems, recv_sems = refs[2 * n:]
        x, y, c = _coords()
        chips = [(1 - x, y), (x, 1 - y), (1 - x, 1 - y)]
        _handshake([(*chip, c) for chip in chips])
        copies = []
        for a in range(n):
            for j, chip in enumerate(chips):
                cp = pltpu.make_async_remote_copy(
                    src_ref=ins[a].at[2 * chip[0] + chip[1]], dst_ref=outs[a].at[j],
                    send_sem=send_sems.at[a, j], recv_sem=recv_sems.at[a, j],
                    device_id=(*chip, c), device_id_type=MESH)
                cp.start()
                copies.append(cp)
        for cp in copies:
            cp.wait_recv()
        for cp in copies:
            cp.wait_send()

    return _sequencer_call(
        body, [jax.ShapeDtypeStruct((3,) + p.shape[1:], p.dtype) for p in psums],
        [pltpu.SemaphoreType.DMA((n, 3)), pltpu.SemaphoreType.DMA((n, 3))],
        name, collective_id)(*psums)


def _allgather_small(v2d, name):
    rows, cols = v2d.shape

    def body(v_ref, out_ref, send_sems, recv_sems):
        x, y, c = _coords()
        me = (x, y, c)
        out_ref[_slot(me)] = v_ref[...]
        peers = []
        for k in range(1, N_DEV):
            fx, fy, fc = (k >> 2) & 1, (k >> 1) & 1, k & 1
            peers.append(((1 - x) if fx else x, (1 - y) if fy else y, (1 - c) if fc else c))
        sends = []
        for k, peer in enumerate(peers):
            cp = pltpu.make_async_remote_copy(
                src_ref=v_ref, dst_ref=out_ref.at[_slot(me)],
                send_sem=send_sems.at[k], recv_sem=recv_sems.at[k],
                device_id=peer, device_id_type=MESH)
            cp.start()
            sends.append(cp)
        for k, peer in enumerate(peers):
            pltpu.make_async_remote_copy(
                src_ref=v_ref, dst_ref=out_ref.at[_slot(peer)],
                send_sem=send_sems.at[k], recv_sem=recv_sems.at[k],
                device_id=peer, device_id_type=MESH).wait_recv()
        for cp in sends:
            cp.wait_send()

    vmem = pl.BlockSpec(memory_space=pltpu.VMEM)
    return pl.pallas_call(
        body, name=name, in_specs=[vmem], out_specs=vmem,
        out_shape=jax.ShapeDtypeStruct((N_DEV, rows, cols), v2d.dtype),
        scratch_shapes=[pltpu.SemaphoreType.DMA((N_DEV - 1,)), pltpu.SemaphoreType.DMA((N_DEV - 1,))],
    )(v2d)


def _chip_partial(others, grads, recvs, name):
    n = len(grads)
    _, rows, cols = grads[0].shape
    tr = _row_tile(rows, cols * 2, (2 << 20) // n)

    def body(others_ref, *refs):
        for a in range(n):
            refs[2 * n + a][...] = (refs[a][...].astype(F32) + refs[n + a][...].astype(F32)).astype(BF16)

    return pl.pallas_call(
        body, name=name,
        grid_spec=pltpu.PrefetchScalarGridSpec(
            num_scalar_prefetch=1, grid=(3, rows // tr),
            in_specs=[pl.BlockSpec((None, tr, cols), lambda k, i, o: (o[3 + k], i, 0))] * n
            + [pl.BlockSpec((None, tr, cols), lambda k, i, o: (o[k], i, 0))] * n,
            out_specs=[pl.BlockSpec((None, tr, cols), lambda k, i, o: (o[k], i, 0))] * n),
        out_shape=[jax.ShapeDtypeStruct((4, rows, cols), BF16)] * n,
        compiler_params=_cp(2),
    )(others, *grads, *recvs)


def _adam_math(w, g, m, v):
    m = ADAM_B1 * m + (1.0 - ADAM_B1) * g
    v = ADAM_B2 * v + (1.0 - ADAM_B2) * (g * g)
    m_hat = m / (1.0 - ADAM_B1 ** ADAM_STEP)
    v_hat = v / (1.0 - ADAM_B2 ** ADAM_STEP)
    delta = -ADAM_LR * (m_hat / (jnp.sqrt(v_hat) + ADAM_EPS) + ADAM_WD * w)
    return delta, m, v


def _adam_big(own, wmvs, g3s, recv_sibs, recv_chipss, name):
    n = len(wmvs)
    rows, cols = wmvs[0][0].shape
    tr = _row_tile(rows, cols * 4, (2 << 20) // n)

    def body(own_ref, *refs):
        ins, outs = refs[:6 * n], refs[6 * n:]
        for a in range(n):
            w_ref, m_ref, v_ref, g_ref, rs_ref, rc_ref = ins[6 * a:6 * a + 6]
            g = g_ref[...].astype(F32) + rs_ref[...].astype(F32)
            g = g + rc_ref[0].astype(F32)
            g = g + rc_ref[1].astype(F32)
            g = g + rc_ref[2].astype(F32)
            delta, m_new, v_new = _adam_math(w_ref[...], g, m_ref[...], v_ref[...])
            outs[4 * a][...] = g
            outs[4 * a + 1][...] = delta
            outs[4 * a + 2][...] = m_new
            outs[4 * a + 3][...] = v_new

    blk = pl.BlockSpec((tr, cols), lambda i, o: (i, 0))
    per_shard = [blk, blk, blk,
                 pl.BlockSpec((None, tr, cols), lambda i, o: (o[0], i, 0)),
                 pl.BlockSpec((None, tr, cols), lambda i, o: (o[1], i, 0)),
                 pl.BlockSpec((3, tr, cols), lambda i, o: (0, i, 0))]
    out = jax.ShapeDtypeStruct((rows, cols), F32)
    args = [t for a in range(n) for t in (*wmvs[a], g3s[a], recv_sibs[a], recv_chipss[a])]
    outs = pl.pallas_call(
        body, name=name,
        grid_spec=pltpu.PrefetchScalarGridSpec(
            num_scalar_prefetch=1, grid=(rows // tr,),
            in_specs=per_shard * n, out_specs=[blk] * (4 * n)),
        out_shape=[out] * (4 * n),
        compiler_params=_cp(1),
    )(own, *args)
    return [outs[4 * a:4 * a + 4] for a in range(n)]


def _small_finish(gathered, params, row_offs, extra_rows):
    n = len(params)

    def body(g_ref, *refs):
        ins, outs = refs[:3 * n], refs[3 * n:]
        total = g_ref[0]
        for k in range(1, N_DEV):
            total = total + g_ref[k]
        for e, (r0, nr) in enumerate(extra_rows):
            outs[e][...] = total[r0:r0 + nr, :]
        for p in range(n):
            w_ref, m_ref, v_ref = ins[3 * p:3 * p + 3]
            g_out, d_out, m_out, v_out = outs[len(extra_rows) + 4 * p:len(extra_rows) + 4 * p + 4]
            g = total[row_offs[p]:row_offs[p] + w_ref.shape[0], :]
            delta, m_new, v_new = _adam_math(w_ref[...], g, m_ref[...], v_ref[...])
            g_out[...] = g
            d_out[...] = delta
            m_out[...] = m_new
            v_out[...] = v_new

    vmem = pl.BlockSpec(memory_space=pltpu.VMEM)
    out_shape = [jax.ShapeDtypeStruct((nr, LANES), F32) for _, nr in extra_rows]
    for w, _, _ in params:
        out_shape += [jax.ShapeDtypeStruct(w.shape, F32)] * 4
    flat = [t for wmv in params for t in wmv]
    return pl.pallas_call(body, name="small_finish", in_specs=[vmem] * (1 + len(flat)),
                          out_specs=[vmem] * len(out_shape), out_shape=out_shape)(gathered, *flat)


def _adam_small(w, g, m, v):
    def body(w_ref, g_ref, m_ref, v_ref, do_ref, mo_ref, vo_ref):
        delta, m_new, v_new = _adam_math(w_ref[...], g_ref[...], m_ref[...], v_ref[...])
        do_ref[...] = delta
        mo_ref[...] = m_new
        vo_ref[...] = v_new

    vmem = pl.BlockSpec(memory_space=pltpu.VMEM)
    out = jax.ShapeDtypeStruct(w.shape, F32)
    return pl.pallas_call(body, name="adam_small", in_specs=[vmem] * 4, out_specs=[vmem] * 3,
                          out_shape=[out, out, out])(w, g, m, v)


def _proj_fwd(h, win_g):
    s, d = h.shape
    sw = win_g.shape[2]
    tn = min(512, sw)
    nh = sw // tn

    def body(h_ref, w_ref, o_ref):
        for rs in _chunks(s, 512):
            o_ref[rs, :] = _dot(h_ref[rs, :], w_ref[...], NN)

    return pl.pallas_call(
        body, name="proj_fwd", grid=(N_DEV * nh,),
        in_specs=[pl.BlockSpec((s, d), lambda j: (0, 0)),
                  pl.BlockSpec((None, d, tn), lambda j: (j // nh, 0, j % nh))],
        out_specs=pl.BlockSpec((None, s, tn), lambda j: (j // nh, 0, j % nh)),
        out_shape=jax.ShapeDtypeStruct((N_DEV, s, sw), F32),
        compiler_params=_cp(1),
    )(h, win_g)


def _conv_fwd(proj, conv_w, conv_b):
    _, s, sw = proj.shape
    tc = min(LANES, sw)

    def body(ba_ref, ca_ref, va_ref, cw_ref, cb_ref, z_ref):
        cv = ca_ref[...] * va_ref[...]
        u = (cb_ref[...] + cw_ref[0:1, :] * _shift_down(cv, 2) + cw_ref[1:2, :] * _shift_down(cv, 1)
             + cw_ref[2:3, :] * cv)
        z_ref[...] = (ba_ref[...] * u).astype(BF16)

    def part(k):
        return pl.BlockSpec((None, s, tc), lambda i: (k, 0, i))

    return pl.pallas_call(
        body, name="conv_fwd", grid=(sw // tc,),
        in_specs=[part(0), part(1), part(2),
                  pl.BlockSpec((CONV_K, tc), lambda i: (0, i)), pl.BlockSpec((1, tc), lambda i: (0, i))],
        out_specs=pl.BlockSpec((s, tc), lambda i: (0, i)),
        out_shape=jax.ShapeDtypeStruct((s, sw), BF16),
        compiler_params=_cp(1),
    )(proj, proj, proj, conv_w, conv_b)


def _pool_counts(shape, window):
    t = lax.broadcasted_iota(jnp.int32, shape, 0)
    return jnp.minimum(t + 1, window).astype(F32)


def _pool_fwd(proj):
    _, s, sw = proj.shape
    gw = sw // len(POOL_WINDOWS)

    def body(v_ref, p_ref):
        for gi, window in enumerate(POOL_WINDOWS):
            @pl.when(pl.program_id(0) == gi)
            def _():
                v = v_ref[...]
                acc, k = v, 1
                while k < window:
                    acc = acc + _shift_down(acc, k)
                    k *= 2
                p_ref[...] = (acc / _pool_counts(v.shape, window) - v).astype(BF16)

    return pl.pallas_call(
        body, name="pool_fwd", grid=(len(POOL_WINDOWS),),
        in_specs=[pl.BlockSpec((None, s, gw), lambda g: (3, 0, g))],
        out_specs=pl.BlockSpec((s, gw), lambda g: (0, g)),
        out_shape=jax.ShapeDtypeStruct((s, sw), BF16),
        compiler_params=_cp(1),
    )(proj)


def _merge_fwd(z, wa, p, wpool, proj, b_gate2, pool_scale):
    s, sw = z.shape
    tn = wa.shape[2]
    d = tn * N_DEV
    gw = sw // len(POOL_WINDOWS)
    nq = sw // tn

    def body(z_ref, wa_ref, p_ref, wp_ref, ga_ref, gb_ref, bg_ref, sc_ref,
             m_ref, dya_ref, dyb_ref, dga_ref, dgb_ref, dsc_ref):
        for rs in _chunks(s, 512):
            ya = _dot(z_ref[rs, :], wa_ref[...], NN)
            yb = _dot(p_ref[rs, :], wp_ref[...], NN)
            sa = _sigmoid(ga_ref[rs, :] + bg_ref[0:1, :])
            sb = _sigmoid(gb_ref[rs, :] + bg_ref[1:2, :])
            sc = sc_ref[...]
            sb_yb = sb * yb
            m_ref[rs, :] = (sa * ya + sb_yb * sc).astype(BF16)
            dya_ref[rs, :] = sa.astype(BF16)
            dyb_ref[rs, :] = (sb * sc).astype(BF16)
            dga_ref[rs, :] = (ya * (sa * (1.0 - sa))).astype(BF16)
            dgb_ref[rs, :] = ((yb * sc) * (sb * (1.0 - sb))).astype(BF16)
            dsc_ref[rs, :] = sb_yb.astype(BF16)

    col = pl.BlockSpec((s, tn), lambda j: (0, j))
    out = jax.ShapeDtypeStruct((s, d), BF16)
    return pl.pallas_call(
        body, name="merge_fwd", grid=(N_DEV,),
        in_specs=[pl.BlockSpec((s, sw), lambda j: (0, 0)),
                  pl.BlockSpec((None, sw, tn), lambda j: (j, 0, 0)),
                  pl.BlockSpec((s, gw), lambda j: (0, j // 2)),
                  pl.BlockSpec((None, gw, tn), lambda j: (j // 2, 0, j % 2)),
                  pl.BlockSpec((None, s, tn), lambda j: (4 + j // nq, 0, j % nq)),
                  pl.BlockSpec((None, s, tn), lambda j: (6 + j // nq, 0, j % nq)),
                  pl.BlockSpec((2, tn), lambda j: (0, j)),
                  pl.BlockSpec((1, tn), lambda j: (0, j))],
        out_specs=[col] * 6,
        out_shape=[out] * 6,
        compiler_params=_cp(1),
    )(z, wa, p, wpool, proj, proj, b_gate2, pool_scale)


def _wo_fwd(merged, wo, x2d, g2):
    s, d = x2d.shape
    tm = min(256, s)

    def body(m_ref, wo_ref, x_ref, g_ref, x1_ref, h2_ref):
        x1 = x_ref[...] + _dot(m_ref[...], wo_ref[...], NN)
        x1_ref[...] = x1
        r = lax.rsqrt(jnp.mean(x1 * x1, axis=-1, keepdims=True) + EPS)
        h2_ref[...] = (x1 * r * g_ref[...]).astype(BF16)

    row = pl.BlockSpec((tm, d), lambda i: (i, 0))
    return pl.pallas_call(
        body, name="wo_fwd", grid=(s // tm,),
        in_specs=[row, pl.BlockSpec((d, d), lambda i: (0, 0)), row, pl.BlockSpec((1, d), lambda i: (0, 0))],
        out_specs=[row, row],
        out_shape=[jax.ShapeDtypeStruct((s, d), F32), jax.ShapeDtypeStruct((s, d), BF16)],
        compiler_params=_cp(1),
    )(merged, wo, x2d, g2)


def _ffn_gate_fwd(h2, wg_g):
    s, d = h2.shape
    f8 = wg_g.shape[2]
    th = min(1024, s)

    def body(h_ref, wg_ref, g_ref):
        i = pl.program_id(1)
        for rs in _chunks(th, 512):
            rows = pl.ds(pl.multiple_of(i * th + rs.start, rs.stop - rs.start), rs.stop - rs.start)
            g_ref[rs, :] = _dot(h_ref[rows, :], wg_ref[...], NN).astype(BF16)

    return pl.pallas_call(
        body, name="ffn_gate_fwd", grid=(N_DEV, s // th),
        in_specs=[pl.BlockSpec((s, d), lambda j, i: (0, 0)), pl.BlockSpec((None, d, f8), lambda j, i: (j, 0, 0))],
        out_specs=pl.BlockSpec((None, th, f8), lambda j, i: (j, i, 0)),
        out_shape=jax.ShapeDtypeStruct((N_DEV, s, f8), BF16),
        compiler_params=_cp(2),
    )(h2, wg_g)


def _ffn_up_act_fwd(h2, wu_g, gate):
    s, d = h2.shape
    f8 = wu_g.shape[2]
    th = min(1024, s)

    def body(h_ref, wu_ref, g_ref, dadu_ref, dadg_ref, a_ref, u_ref):
        i = pl.program_id(1)
        chunks = _chunks(th, 256)

        def matmul(rs):
            rows = pl.ds(pl.multiple_of(i * th + rs.start, rs.stop - rs.start), rs.stop - rs.start)
            u_ref[rs, :] = _dot(h_ref[rows, :], wu_ref[...], NN)

        matmul(chunks[0])
        for k, rs in enumerate(chunks):
            if k + 1 < len(chunks):
                matmul(chunks[k + 1])
            g = g_ref[rs, :].astype(F32)
            u = u_ref[rs, :]
            sg = _sigmoid(g)
            silu = g * sg
            dadu_ref[rs, :] = silu.astype(BF16)
            dadg_ref[rs, :] = (u * (sg * (1.0 + g * (1.0 - sg)))).astype(BF16)
            a_ref[rs, :] = (silu * u).astype(BF16)

    wspec = pl.BlockSpec((None, d, f8), lambda j, i: (j, 0, 0))
    ospec = pl.BlockSpec((None, th, f8), lambda j, i: (j, i, 0))
    out = jax.ShapeDtypeStruct((N_DEV, s, f8), BF16)
    return pl.pallas_call(
        body, name="ffn_up_fwd", grid=(N_DEV, s // th),
        in_specs=[pl.BlockSpec((s, d), lambda j, i: (0, 0)), wspec, ospec],
        out_specs=[ospec, ospec, ospec], out_shape=[out, out, out],
        scratch_shapes=[pltpu.VMEM((th, f8), F32)],
        compiler_params=_cp(2),
    )(h2, wu_g, gate)


def _ffn_down_fwd(act, wd_part, name):
    _, s, f8 = act.shape
    tn = wd_part.shape[2]
    per = 2

    def body(a_ref, wd_ref, o_ref):
        @pl.when(pl.program_id(0) == 0)
        def _():
            o_ref[...] = jnp.zeros_like(o_ref)

        for rs in _chunks(s, 1024):
            part = _dot(a_ref[0, rs, :], wd_ref[0], NN)
            for q in range(1, per):
                part = part + _dot(a_ref[q, rs, :], wd_ref[q], NN)
            o_ref[rs, :] += part

    return pl.pallas_call(
        body, name=name, grid=(N_DEV // per,),
        in_specs=[pl.BlockSpec((per, s, f8), lambda j: (j, 0, 0)),
                  pl.BlockSpec((per, f8, tn), lambda j: (j, 0, 0))],
        out_specs=pl.BlockSpec((s, tn), lambda j: (0, 0)),
        out_shape=jax.ShapeDtypeStruct((s, tn), F32),
        compiler_params=_cp(1),
    )(act, wd_part)


def _loss_bwd(ffn_parts, x1, target, final_g):
    s, d = x1.shape
    tm = min(256, s)
    nparts = len(ffn_parts)

    def body(*refs):
        f_refs = refs[:nparts]
        x1_ref, t_ref, gf_ref, dxb_ref, dgf_ref, loss_ref = refs[nparts:]

        @pl.when(pl.program_id(0) == 0)
        def _():
            dgf_ref[...] = jnp.zeros_like(dgf_ref)
            loss_ref[...] = jnp.zeros_like(loss_ref)

        x2 = x1_ref[...] + jnp.concatenate([f_ref[...] for f_ref in f_refs], axis=-1)
        r = lax.rsqrt(jnp.mean(x2 * x2, axis=-1, keepdims=True) + EPS)
        nrm = x2 * r
        gf = gf_ref[...]
        err = nrm * gf - t_ref[...]
        loss_ref[...] += jnp.sum(err * err) * (0.5 / d)
        dy = err * (1.0 / d)
        dgf_ref[...] += jnp.sum(dy * nrm, axis=0, keepdims=True)
        dn = dy * gf
        dx = r * (dn - nrm * jnp.mean(dn * nrm, axis=-1, keepdims=True))
        dxb_ref[...] = dx.astype(BF16)

    row = pl.BlockSpec((tm, d), lambda i: (i, 0))
    vec = pl.BlockSpec((1, d), lambda i: (0, 0))
    return pl.pallas_call(
        body, name="loss_bwd", grid=(s // tm,),
        in_specs=[pl.BlockSpec((tm, f.shape[1]), lambda i: (i, 0)) for f in ffn_parts] + [row, row, vec],
        out_specs=[row, vec, pl.BlockSpec((8, LANES), lambda i: (0, 0))],
        out_shape=[jax.ShapeDtypeStruct((s, d), BF16),
                   jax.ShapeDtypeStruct((1, d), F32), jax.ShapeDtypeStruct((8, LANES), F32)],
        compiler_params=_cp(1),
    )(*ffn_parts, x1, target, final_g)


def _ffn_gate_bwd(dx2b, wd_parts, dadg, dadu):
    s, d = dx2b.shape
    f8 = dadg.shape[2]
    th = min(1024, s)
    nparts = len(wd_parts)
    pc = d // nparts

    def body(dx_ref, *refs):
        wd_refs = refs[:nparts]
        g_ref, u_ref, dg_ref, du_ref, da_ref = refs[nparts:]
        i = pl.program_id(1)
        chunks = _chunks(th, 256)

        def matmul(rs):
            rows = pl.ds(pl.multiple_of(i * th + rs.start, rs.stop - rs.start), rs.stop - rs.start)
            part = None
            for q, wd_ref in enumerate(wd_refs):
                term = _dot(dx_ref[rows, q * pc:(q + 1) * pc], wd_ref[...], NT)
                part = term if part is None else part + term
            da_ref[rs, :] = part.astype(BF16)

        matmul(chunks[0])
        for k, rs in enumerate(chunks):
            if k + 1 < len(chunks):
                matmul(chunks[k + 1])
            da = da_ref[rs, :]
            dg_ref[rs, :] = da * g_ref[rs, :]
            du_ref[rs, :] = da * u_ref[rs, :]

    aspec = pl.BlockSpec((None, th, f8), lambda j, i: (j, i, 0))
    out = jax.ShapeDtypeStruct((N_DEV, s, f8), BF16)
    return pl.pallas_call(
        body, name="ffn_act_bwd", grid=(N_DEV, s // th),
        in_specs=[pl.BlockSpec((s, d), lambda j, i: (0, 0))]
        + [pl.BlockSpec((None, f8, pc), lambda j, i: (j, 0, 0))] * nparts + [aspec, aspec],
        out_specs=[aspec, aspec], out_shape=[out, out],
        scratch_shapes=[pltpu.VMEM((th, f8), BF16)],
        compiler_params=_cp(2),
    )(dx2b, *wd_parts, dadg, dadu)


def _wgrad_rows(a3, b, name, after=(), carry=()):
    _, s, k = a3.shape
    n = b.shape[1]
    nc = len(carry)
    c_in, c_out, c_shape, c_sems = _carry_specs(carry)

    def body(a_ref, b_ref, *rest):
        rest = rest[len(after):]
        o_ref = rest[nc]
        j = pl.program_id(0)
        _carry_run(j == 0, j == N_DEV - 1, rest[:nc], rest[nc + 1:2 * nc + 1], rest[2 * nc + 1:])
        o_ref[...] = _dot(a_ref[...], b_ref[...], TN).astype(BF16)

    outs = pl.pallas_call(
        body, name=name, grid=(N_DEV,),
        in_specs=[pl.BlockSpec((None, s, k), lambda j: (j, 0, 0)),
                  pl.BlockSpec((s, n), lambda j: (0, 0))] + _after_specs(after) + c_in,
        out_specs=[pl.BlockSpec((None, k, n), lambda j: (j, 0, 0))] + c_out,
        out_shape=[jax.ShapeDtypeStruct((N_DEV, k, n), BF16)] + c_shape,
        scratch_shapes=c_sems,
        compiler_params=_cp_carry(1, carry),
    )(a3, b, *after, *carry)
    return (outs[0], list(outs[1:])) if nc else outs[0]


def _wgrad_cols(a, b3, name, after=()):
    s, k = a.shape
    if b3.ndim == 2:
        n = b3.shape[1] // N_DEV
        b_spec = pl.BlockSpec((s, n), lambda j: (0, j))
    else:
        n = b3.shape[2]
        b_spec = pl.BlockSpec((None, s, n), lambda j: (j, 0, 0))

    def body(a_ref, b_ref, *rest):
        o_ref = rest[len(after)]
        o_ref[...] = _dot(a_ref[...], b_ref[...], TN).astype(BF16)

    return pl.pallas_call(
        body, name=name, grid=(N_DEV,),
        in_specs=[pl.BlockSpec((s, k), lambda j: (0, 0)), b_spec] + _after_specs(after),
        out_specs=pl.BlockSpec((None, k, n), lambda j: (j, 0, 0)),
        out_shape=jax.ShapeDtypeStruct((N_DEV, k, n), BF16),
        compiler_params=_cp(1),
    )(a, b3, *after)


def _input_grad(pairs, name, after=(), carry=(), per=1):
    s = pairs[0][0].shape[1]
    d = pairs[0][1].shape[1]
    tn = min(1024, d)
    npair = len(pairs)
    nc = len(carry)
    c_in, c_out, c_shape, c_sems = _carry_specs(carry)

    def body(*refs):
        ops = refs[:2 * npair]
        rest = refs[2 * npair + len(after):]
        o_ref, acc_ref = rest[nc], rest[-1]
        nh, j = pl.program_id(0), pl.program_id(1)
        last_j = N_DEV // per - 1
        _carry_run((nh == 0) & (j == 0), (nh == d // tn - 1) & (j == last_j),
                   rest[:nc], rest[nc + 1:2 * nc + 1], rest[2 * nc + 1:-1])

        @pl.when(j == 0)
        def _():
            acc_ref[...] = jnp.zeros_like(acc_ref)

        for rs in _chunks(s, 1024):
            part = None
            for q in range(npair):
                for e in range(per):
                    term = _dot(ops[2 * q][e, rs, :], ops[2 * q + 1][e], NT)
                    part = term if part is None else part + term
            acc_ref[rs, :] += part

        @pl.when(j == last_j)
        def _():
            o_ref[...] = acc_ref[...].astype(BF16)

    in_specs, args = [], []
    for a3, w3 in pairs:
        k = a3.shape[2]
        in_specs += [pl.BlockSpec((per, s, k), lambda n, j: (j, 0, 0)),
                     pl.BlockSpec((per, tn, k), lambda n, j: (j, n, 0))]
        args += [a3, w3]
    outs = pl.pallas_call(
        body, name=name, grid=(d // tn, N_DEV // per),
        in_specs=in_specs + _after_specs(after) + c_in,
        out_specs=[pl.BlockSpec((s, tn), lambda n, j: (0, n))] + c_out,
        out_shape=[jax.ShapeDtypeStruct((s, d), BF16)] + c_shape,
        scratch_shapes=c_sems + [pltpu.VMEM((s, tn), F32)],
        compiler_params=_cp_carry(2, carry),
    )(*args, *after, *carry)
    return (outs[0], list(outs[1:])) if nc else outs[0]


def _rms_bwd(dh, xres, g, dres, name, with_bf16=True):
    s, d = xres.shape
    tm = min(256, s)

    def body(dh_ref, x_ref, g_ref, dres_ref, dx_ref, *rest):
        dg_ref = rest[-1]
        @pl.when(pl.program_id(0) == 0)
        def _():
            dg_ref[...] = jnp.zeros_like(dg_ref)

        xv = x_ref[...]
        dh_v = dh_ref[...].astype(F32)
        r = lax.rsqrt(jnp.mean(xv * xv, axis=-1, keepdims=True) + EPS)
        nrm = xv * r
        dg_ref[...] += jnp.sum(dh_v * nrm, axis=0, keepdims=True)
        dn = dh_v * g_ref[...]
        dx = dres_ref[...].astype(F32) + r * (dn - nrm * jnp.mean(dn * nrm, axis=-1, keepdims=True))
        dx_ref[...] = dx
        if with_bf16:
            rest[0][...] = dx.astype(BF16)

    row = pl.BlockSpec((tm, d), lambda i: (i, 0))
    vec = pl.BlockSpec((1, d), lambda i: (0, 0))
    copies = [jax.ShapeDtypeStruct((s, d), BF16)] if with_bf16 else []
    outs = pl.pallas_call(
        body, name=name, grid=(s // tm,),
        in_specs=[row, row, vec, row],
        out_specs=[row] + [row] * len(copies) + [vec],
        out_shape=[jax.ShapeDtypeStruct((s, d), F32)] + copies + [jax.ShapeDtypeStruct((1, d), F32)],
        compiler_params=_cp(1),
    )(dh, xres, g, dres)
    return (outs[0], outs[1], outs[2]) if with_bf16 else (outs[0], None, outs[1])


def _wgrad_full(a, b, name, after=(), carry=()):
    s, k = a.shape
    n = b.shape[1]
    tk = min(512, k)
    nc = len(carry)
    c_in, c_out, c_shape, c_sems = _carry_specs(carry)

    def body(a_ref, b_ref, *rest):
        rest = rest[len(after):]
        o_ref = rest[nc]
        j = pl.program_id(0)
        _carry_run(j == 0, j == k // tk - 1, rest[:nc], rest[nc + 1:2 * nc + 1], rest[2 * nc + 1:])
        o_ref[...] = _dot(a_ref[...], b_ref[...], TN).astype(BF16)

    outs = pl.pallas_call(
        body, name=name, grid=(k // tk,),
        in_specs=[pl.BlockSpec((s, tk), lambda j: (0, j)),
                  pl.BlockSpec((s, n), lambda j: (0, 0))] + _after_specs(after) + c_in,
        out_specs=[pl.BlockSpec((tk, n), lambda j: (j, 0))] + c_out,
        out_shape=[jax.ShapeDtypeStruct((k, n), BF16)] + c_shape,
        scratch_shapes=c_sems,
        compiler_params=_cp_carry(1, carry),
    )(a, b, *after, *carry)
    return (outs[0], list(outs[1:])) if nc else outs[0]


def _wgrad_pool(p, dyb, n_groups):
    s, sw = p.shape
    d = dyb.shape[1]
    gw, go = sw // n_groups, d // n_groups
    ts = min(512, s)
    ns = s // ts

    def body(a_ref, b_ref, o_ref, acc_ref):
        i = pl.program_id(1)

        @pl.when(i == 0)
        def _():
            acc_ref[...] = jnp.zeros_like(acc_ref)

        acc_ref[...] += _dot(a_ref[...], b_ref[...], TN)

        @pl.when(i == ns - 1)
        def _():
            o_ref[...] = acc_ref[...].astype(BF16)

    return pl.pallas_call(
        body, name="wgrad_pool", grid=(n_groups, ns),
        in_specs=[pl.BlockSpec((ts, gw), lambda g, i: (i, g)),
                  pl.BlockSpec((ts, go), lambda g, i: (i, g))],
        out_specs=pl.BlockSpec((None, gw, go), lambda g, i: (g, 0, 0)),
        out_shape=jax.ShapeDtypeStruct((n_groups, gw, go), BF16),
        scratch_shapes=[pltpu.VMEM((gw, go), F32)],
        compiler_params=_cp(2),
    )(p, dyb)


def _wo_bwd(dx1b, wo, factors, sw, after=()):
    s, d = dx1b.shape
    tn = d // N_DEV
    nq = sw // tn

    def body(dx_ref, wo_ref, fya_ref, fyb_ref, fga_ref, fgb_ref, fsc_ref, *rest):
        dya_ref, dyb_ref, dp_ref, dbg_ref, dsc_ref, dm_ref = rest[len(after):]
        dbg_ref[...] = jnp.zeros_like(dbg_ref)
        dsc_ref[...] = jnp.zeros_like(dsc_ref)
        for rs in _chunks(s, 1024):
            dm_ref[rs, :] = _dot(dx_ref[rs, :], wo_ref[...], NT)
        for rs in _chunks(s, 256):
            dm = dm_ref[rs, :]
            dya_ref[rs, :] = (dm * fya_ref[rs, :].astype(F32)).astype(BF16)
            dyb_ref[rs, :] = (dm * fyb_ref[rs, :].astype(F32)).astype(BF16)
            dsc_ref[...] += jnp.sum(dm * fsc_ref[rs, :].astype(F32), axis=0, keepdims=True)
            dga = dm * fga_ref[rs, :].astype(F32)
            dgb = dm * fgb_ref[rs, :].astype(F32)
            dp_ref[0, rs, :] = dga.astype(BF16)
            dp_ref[1, rs, :] = dgb.astype(BF16)
            dbg_ref[0:1, :] += jnp.sum(dga, axis=0, keepdims=True)
            dbg_ref[1:2, :] += jnp.sum(dgb, axis=0, keepdims=True)

    col = pl.BlockSpec((s, tn), lambda j: (0, j))
    out = jax.ShapeDtypeStruct((s, d), BF16)
    return pl.pallas_call(
        body, name="wo_bwd", grid=(N_DEV,),
        in_specs=[pl.BlockSpec((s, d), lambda j: (0, 0)),
                  pl.BlockSpec((tn, d), lambda j: (j, 0))] + [col] * 5 + _after_specs(after),
        out_specs=[col, col,
                   pl.BlockSpec((2, None, s, tn), lambda j: (1, j // nq, 0, j % nq)),
                   pl.BlockSpec((2, tn), lambda j: (0, j)),
                   pl.BlockSpec((1, tn), lambda j: (0, j))],
        out_shape=[out, out, jax.ShapeDtypeStruct((4, 2, s, sw), BF16),
                   jax.ShapeDtypeStruct((2, d), F32), jax.ShapeDtypeStruct((1, d), F32)],
        scratch_shapes=[pltpu.VMEM((s, tn), F32)],
        compiler_params=_cp(1),
    )(dx1b, wo, *factors, *after)


def _conv_bwd(dproj, dya, wa, proj, conv_w, conv_b):
    s, d = dya.shape
    sw, tn = wa.shape[1], wa.shape[2]
    tc = min(LANES, sw)

    def body(dproj_hbm, dya_ref, wa_ref, ba_ref, ca_ref, va_ref, cw_ref, cb_ref,
             dp_ref, dcw_ref, dcb_ref, dz_ref):
        del dproj_hbm
        for rs in _chunks(s, 512):
            part = _dot(dya_ref[rs, 0:tn], wa_ref[0], NT)
            for j in range(1, N_DEV):
                part = part + _dot(dya_ref[rs, j * tn:(j + 1) * tn], wa_ref[j], NT)
            dz_ref[rs, :] = part
        dz = dz_ref[...]
        ba, ca, va = ba_ref[...], ca_ref[...], va_ref[...]
        cv = ca * va
        cv1, cv2 = _shift_down(cv, 1), _shift_down(cv, 2)
        w0, w1, w2 = cw_ref[0:1, :], cw_ref[1:2, :], cw_ref[2:3, :]
        u = cb_ref[...] + w0 * cv2 + w1 * cv1 + w2 * cv
        du = dz * ba
        dp_ref[0] = (dz * u).astype(BF16)
        dcv = w2 * du + w1 * _shift_up(du, 1) + w0 * _shift_up(du, 2)
        dp_ref[1] = (dcv * va).astype(BF16)
        dp_ref[2] = (dcv * ca).astype(BF16)
        dcw_ref[0:1, :] = jnp.sum(du * cv2, axis=0, keepdims=True)
        dcw_ref[1:2, :] = jnp.sum(du * cv1, axis=0, keepdims=True)
        dcw_ref[2:3, :] = jnp.sum(du * cv, axis=0, keepdims=True)
        dcb_ref[...] = jnp.sum(du, axis=0, keepdims=True)

    def part(k):
        return pl.BlockSpec((None, s, tc), lambda i: (k, 0, i))

    return pl.pallas_call(
        body, name="conv_bwd", grid=(sw // tc,),
        in_specs=[pl.BlockSpec(memory_space=pl.ANY),
                  pl.BlockSpec((s, d), lambda i: (0, 0)),
                  pl.BlockSpec((N_DEV, tc, tn), lambda i: (0, i, 0)),
                  part(0), part(1), part(2),
                  pl.BlockSpec((CONV_K, tc), lambda i: (0, i)), pl.BlockSpec((1, tc), lambda i: (0, i))],
        out_specs=[pl.BlockSpec((3, s, tc), lambda i: (0, 0, i)),
                   pl.BlockSpec((CONV_K, tc), lambda i: (0, i)), pl.BlockSpec((1, tc), lambda i: (0, i))],
        out_shape=[jax.ShapeDtypeStruct(dproj.shape, BF16),
                   jax.ShapeDtypeStruct((CONV_K, sw), F32), jax.ShapeDtypeStruct((1, sw), F32)],
        scratch_shapes=[pltpu.VMEM((s, tc), F32)],
        input_output_aliases={0: 0},
        compiler_params=_cp(1),
    )(dproj, dya, wa, proj, proj, proj, conv_w, conv_b)


def _pool_bwd(dproj, dyb, wpool):
    s, d = dyb.shape
    n_groups, gw, go = wpool.shape

    def body(dproj_hbm, dyb_ref, wp_ref, dp_ref):
        del dproj_hbm
        for gi, window in enumerate(POOL_WINDOWS):
            @pl.when(pl.program_id(0) == gi)
            def _():
                dpool = _dot(dyb_ref[...], wp_ref[...], NT)
                acc, k = dpool / _pool_counts(dpool.shape, window), 1
                while k < window:
                    acc = acc + _shift_up(acc, k)
                    k *= 2
                dp_ref[...] = (acc - dpool).astype(BF16)

    return pl.pallas_call(
        body, name="pool_bwd", grid=(n_groups,),
        in_specs=[pl.BlockSpec(memory_space=pl.ANY),
                  pl.BlockSpec((s, go), lambda g: (0, g)),
                  pl.BlockSpec((None, gw, go), lambda g: (g, 0, 0))],
        out_specs=pl.BlockSpec((None, s, gw), lambda g: (3, 0, g)),
        out_shape=jax.ShapeDtypeStruct(dproj.shape, BF16),
        input_output_aliases={0: 0},
        compiler_params=_cp(1),
    )(dproj, dyb, wpool)


def _rows128(v):
    return v.reshape(-1, LANES)


def kernel(x, norm1_g, w_in, b_gate, conv_w, conv_b, w_a_out, w_pool, pool_scale, w_o, norm2_g, w_ffn_gate, w_ffn_up, w_ffn_down, final_g, loss_target, m_norm1_g, m_w_in, m_b_gate, m_conv_w, m_conv_b, m_w_a_out, m_w_pool, m_pool_scale, m_w_o, m_norm2_g, m_w_ffn_gate, m_w_ffn_up, m_w_ffn_down, m_final_g, v_norm1_g, v_w_in, v_b_gate, v_conv_w, v_conv_b, v_w_a_out, v_w_pool, v_pool_scale, v_w_o, v_norm2_g, v_w_ffn_gate, v_w_ffn_up, v_w_ffn_down, v_final_g):
    s, d = x.shape[1], x.shape[2]
    sw = w_in.shape[2]
    n_groups = w_pool.shape[1]
    gw = w_pool.shape[2]
    go = w_pool.shape[3] * N_DEV
    f8 = w_ffn_gate.shape[2]
    cws = conv_w.shape[2]
    assert sw == conv_w.shape[2] * N_DEV == gw * n_groups and go * n_groups == d and n_groups == len(POOL_WINDOWS)

    xi, yi, ci = _coords()
    me = 4 * xi + 2 * yi + ci
    my_chip = 2 * xi + yi

    x2d = x.reshape(s, d)
    target = loss_target.reshape(s, d)
    final_g2 = final_g.reshape(1, d)
    b_gate2 = b_gate.reshape(2, d)

    big_names = ["w_in", "w_a_out", "w_pool", "w_o", "w_ffn_gate", "w_ffn_up", "w_ffn_down"]
    big_w = [w_in, w_a_out, w_pool, w_o, w_ffn_gate, w_ffn_up, w_ffn_down]
    big_m = [m_w_in, m_w_a_out, m_w_pool, m_w_o, m_w_ffn_gate, m_w_ffn_up, m_w_ffn_down]
    big_v = [v_w_in, v_w_a_out, v_w_pool, v_w_o, v_w_ffn_gate, v_w_ffn_up, v_w_ffn_down]
    shapes2d = [(w.size // w.shape[-1], w.shape[-1]) for w in big_w]
    big_w2 = [w.reshape(sh) for w, sh in zip(big_w, shapes2d)]
    transposed = (4, 5)

    def view2d(t, a):
        t2 = t.reshape(shapes2d[a])
        return t2.T if a in transposed else t2

    def unview(o, a):
        return (o.T if a in transposed else o).reshape(big_w[a].shape)

    sb = [_cast_bf16(w, "cast_" + nm, parts=2 if nm == "w_ffn_down" else 1) for w, nm in zip(big_w2, big_names)]
    win_g, wa_g, wpool_g, wo_g = _allgather_big([b[0] for b in sb[0:4]], "allgather_mixer", COLLECTIVE_GATHER)
    (wg_g,) = _allgather_big(sb[4], "allgather_ffn_gate", COLLECTIVE_GATHER)
    (wu_g,) = _allgather_big(sb[5], "allgather_ffn_up", COLLECTIVE_GATHER)
    wd_parts = [_allgather_big([part], "allgather_ffn_down_%d" % q, COLLECTIVE_GATHER)[0]
                for q, part in enumerate(sb[6])]
    convw_g = _allgather_small(jnp.pad(conv_w.reshape(CONV_K, cws), ((0, 8 - CONV_K), (0, 0))), "allgather_conv_w")
    conv_w_full = convw_g[:, :CONV_K, :].transpose(1, 0, 2).reshape(CONV_K, sw)
    wpool = wpool_g.reshape(N_DEV, n_groups, gw, go // N_DEV).transpose(1, 2, 0, 3).reshape(n_groups, gw, go)
    wo = wo_g.reshape(d, d)

    h = _rms_fwd(x2d, norm1_g)
    proj = _proj_fwd(h, win_g)
    z = _conv_fwd(proj, conv_w_full, conv_b)
    p = _pool_fwd(proj)
    merged, *merge_factors = _merge_fwd(z, wa_g, p, wpool, proj, b_gate2, pool_scale)
    x1, h2 = _wo_fwd(merged, wo, x2d, norm2_g)
    gate = _ffn_gate_fwd(h2, wg_g)
    dadu, dadg, act = _ffn_up_act_fwd(h2, wu_g, gate)
    ffn_parts = [_ffn_down_fwd(act, wd, "ffn_down_fwd_%d" % q) for q, wd in enumerate(wd_parts)]
    dx2b, d_final_g, loss_blk = _loss_bwd(ffn_parts, x1, target, final_g2)

    other_chips = jnp.stack([2 * (1 - xi) + yi, 2 * xi + (1 - yi), 2 * (1 - xi) + (1 - yi)])
    others = jnp.concatenate([other_chips, 2 * other_chips + ci]).astype(jnp.int32)

    def partials(grads, recvs, names):
        if all(g.shape == grads[0].shape for g in grads):
            return list(_chip_partial(others, grads, recvs, "chip_partial_" + names[0]))
        return [_chip_partial(others, [g3], [r], "chip_partial_" + nm)[0] for g3, r, nm in zip(grads, recvs, names)]

    own = jnp.stack([me, my_chip]).astype(jnp.int32)

    def adam(idx, g3s, sibs, chipss):
        wmvs = [(view2d(big_w[a], a), view2d(big_m[a], a), view2d(big_v[a], a)) for a in idx]
        outs = _adam_big(own, wmvs, g3s, sibs, chipss, "adam_" + big_names[idx[0]])
        for a, o4 in zip(idx, outs):
            big_out[a] = [unview(o, a) for o in o4]

    big_out = [None] * len(big_names)
    dg_act, du_act = _ffn_gate_bwd(dx2b, wd_parts, dadg, dadu)
    gw_gate = _wgrad_rows(dg_act, h2, "wgrad_ffn_gate")
    gw_up = _wgrad_rows(du_act, h2, "wgrad_ffn_up")
    gw_down, sib_gu = _wgrad_rows(act, dx2b, "wgrad_ffn_down", carry=[gw_gate, gw_up])
    ps_gu = partials([gw_gate, gw_up], sib_gu, ["w_ffn_gate", "w_ffn_up"])
    chips_gu = _exchange_chips(ps_gu, "rs_chips_ffn_up", COLLECTIVE_CHIPS)
    dh2, sib_down = _input_grad([(dg_act, wg_g), (du_act, wu_g)], "ffn_in_bwd", after=ps_gu, carry=[gw_down])
    ps_down = partials([gw_down], sib_down, ["w_ffn_down"])
    chips_down = _exchange_chips(ps_down, "rs_chips_ffn_down", COLLECTIVE_CHIPS)
    dx1, dx1b, d_norm2_g = _rms_bwd(dh2, x1, norm2_g, dx2b, "rms2_bwd")
    dya, dyb, dproj42, d_b_gate, d_pool_scale = _wo_bwd(dx1b, wo, merge_factors, sw, after=ps_down)
    dproj = dproj42.reshape(N_DEV, s, sw)
    dproj, d_conv_w, d_conv_b = _conv_bwd(dproj, dya, wa_g, proj, conv_w_full, conv_b)
    dproj = _pool_bwd(dproj, dyb, wpool)
    gw_in = _wgrad_cols(h, dproj, "wgrad_in")
    gw_o, sib_in = _wgrad_full(merged, dx1b, "wgrad_o", carry=[gw_in])
    ps_in = partials([gw_in], sib_in, ["w_in"])
    chips_in = _exchange_chips(ps_in, "rs_chips_w_in", COLLECTIVE_CHIPS)
    gw_a = _wgrad_cols(z, dya, "wgrad_a_out", after=ps_in)
    gw_pool = _wgrad_pool(p, dyb, n_groups)
    mix3 = [gw_a,
            gw_pool.reshape(n_groups, gw, N_DEV, go // N_DEV).transpose(2, 0, 1, 3).reshape(N_DEV, n_groups * gw, go // N_DEV),
            gw_o.reshape(N_DEV, d // N_DEV, d)]
    adam([4, 5, 6], [gw_gate, gw_up, gw_down], sib_gu + sib_down, chips_gu + chips_down)
    dh, sib_mix = _input_grad([(dproj, win_g)], "proj_in_bwd", after=[big_out[6][0]], carry=mix3, per=2)
    ps_mix = partials(mix3, sib_mix, ["w_a_out", "w_pool", "w_o"])
    chips_mix = _exchange_chips(ps_mix, "rs_chips_mixer", COLLECTIVE_CHIPS)
    grad_x, _, d_norm1_g = _rms_bwd(dh, x2d, norm1_g, dx1, "rms1_bwd", with_bf16=False)
    adam([0], [gw_in], sib_in, chips_in)
    for k in range(3):
        adam([1 + k], [mix3[k]], [sib_mix[k]], [chips_mix[k]])

    small_parts = [d_norm1_g, d_b_gate, d_conv_w, d_conv_b, d_pool_scale, d_norm2_g, d_final_g, loss_blk]
    rows = [v.size // LANES for v in small_parts]
    row0 = [sum(rows[:k]) for k in range(len(rows))]
    packed = jnp.concatenate([_rows128(v) for v in small_parts], axis=0)
    gathered = _allgather_small(packed, "allgather_small_grads")
    small_names = ["norm1_g", "b_gate", "conv_b", "pool_scale", "norm2_g", "final_g", "conv_w"]
    small_w = [norm1_g, b_gate, conv_b, pool_scale, norm2_g, final_g]
    small_m = [m_norm1_g, m_b_gate, m_conv_b, m_pool_scale, m_norm2_g, m_final_g]
    small_v = [v_norm1_g, v_b_gate, v_conv_b, v_pool_scale, v_norm2_g, v_final_g]
    finished = _small_finish(gathered, [tuple(_rows128(t) for t in wmv) for wmv in zip(small_w, small_m, small_v)],
                             [row0[k] for k in (0, 1, 3, 4, 5, 6)], [(row0[2], rows[2]), (row0[7], rows[7])])
    g_convw_full, loss_rows = finished[0], finished[1]
    loss = loss_rows[0, 0]
    small_out = [[t.reshape(w.shape) for t in finished[2 + 4 * k:6 + 4 * k]] for k, w in enumerate(small_w)]
    g_convw = lax.dynamic_slice(g_convw_full.reshape(CONV_K, sw), (0, me * cws), (CONV_K, cws))
    cw_delta, cw_m, cw_v = _adam_small(conv_w.reshape(CONV_K, cws), g_convw,
                                       m_conv_w.reshape(CONV_K, cws), v_conv_w.reshape(CONV_K, cws))
    small_out.append([t.reshape(conv_w.shape) for t in (g_convw, cw_delta, cw_m, cw_v)])

    order = ["norm1_g", "w_in", "b_gate", "conv_w", "conv_b", "w_a_out", "w_pool", "pool_scale", "w_o", "norm2_g",
             "w_ffn_gate", "w_ffn_up", "w_ffn_down", "final_g"]
    per_kind = [{}, {}, {}, {}]
    for a, nm in enumerate(big_names):
        for kind in range(4):
            per_kind[kind][nm] = big_out[a][kind]
    for k, nm in enumerate(small_names):
        for kind in range(4):
            per_kind[kind][nm] = small_out[k][kind]
    result = [loss, grad_x.reshape(x.shape)]
    for kind in range(4):
        result += [per_kind[kind][nm] for nm in order]
    return tuple(result)
```
